```python
import math
import jax, jax.numpy as jnp
from jax import lax
import numpy as np

D_MODEL = 1024
BATCH = 32
SEQ = 2048
DEPTH = 1

CTX_LEN = 256
GRID_W = 64
D_INNER = 2 * D_MODEL
D_SSM = D_INNER // 2
D_CONV = D_INNER - D_SSM
SSM_HEAD_DIM = 64
SSM_HEADS = D_SSM // SSM_HEAD_DIM
SSM_GROUPS = 2
SSM_STATE = 128
SSM_CONV_W = 5
SSM_CHUNK = 128
CONF_KERNEL = 31
CONF_CH_GROUP = 64
D_FF = ((int(8 * D_MODEL / 3) + 255) // 256) * 256
GN = SSM_GROUPS * SSM_STATE
OFF_Z = 0
OFF_X = OFF_Z + D_SSM
OFF_B = OFF_X + D_SSM
OFF_C = OFF_B + GN
OFF_DT = OFF_C + GN
OFF_GLU = OFF_DT + 2 * SSM_HEADS
D_IN_PROJ = OFF_GLU + 2 * D_CONV
N_MOD = 9
EPS = 1e-6

kernel_name = "hybrid_ssd_conformer_macaron_dit_block"


def rmsnorm(x, w):
    xf = x.astype(jnp.float32)
    y = xf * lax.rsqrt(jnp.mean(xf * xf, axis=-1, keepdims=True) + EPS)
    return (y * w.astype(jnp.float32)).astype(x.dtype)


def group_rmsnorm(x, w, groups):
    xf = x.astype(jnp.float32).reshape(*x.shape[:-1], groups, x.shape[-1] // groups)
    y = xf * lax.rsqrt(jnp.mean(xf * xf, axis=-1, keepdims=True) + EPS)
    return (y.reshape(x.shape) * w.astype(jnp.float32)).astype(x.dtype)


def layernorm(x, w, b):
    xf = x.astype(jnp.float32)
    mu = jnp.mean(xf, axis=-1, keepdims=True)
    var = jnp.mean(jnp.square(xf - mu), axis=-1, keepdims=True)
    y = (xf - mu) * lax.rsqrt(var + EPS)
    return (y * w.astype(jnp.float32) + b.astype(jnp.float32)).astype(x.dtype)


def modulate(h, shift, scale):
    return h * (1.0 + scale) + shift


def swiglu(h, w_gate, w_up, w_down):
    return (jax.nn.silu(h @ w_gate) * (h @ w_up)) @ w_down


def _flip(t):
    return jnp.flip(t, axis=1)


def dwconv1d(x, w, b):
    k = w.shape[0]
    pad = k // 2
    y = lax.conv_general_dilated(x, w[:, None, :], (1,), [(pad, pad)],
                                 dimension_numbers=("NWC", "WIO", "NWC"),
                                 feature_group_count=x.shape[-1])
    return y + b


def axial_dwconv(u, w, b, rows):
    bsz, seqlen, ch = u.shape
    k = w.shape[0]
    pad = k // 2
    half = ch // 2
    g = u.reshape(bsz, rows, GRID_W, ch)
    kh = w[:, :half][None, :, None, :]
    kv = w[:, half:][:, None, None, :]
    yh = lax.conv_general_dilated(g[..., :half], kh, (1, 1), [(0, 0), (pad, pad)],
                                  dimension_numbers=("NHWC", "HWIO", "NHWC"),
                                  feature_group_count=half)
    yv = lax.conv_general_dilated(g[..., half:], kv, (1, 1), [(pad, pad), (0, 0)],
                                  dimension_numbers=("NHWC", "HWIO", "NHWC"),
                                  feature_group_count=ch - half)
    return jnp.concatenate([yh, yv], axis=-1).reshape(bsz, seqlen, ch) + b


def ssd_chunked(xh, dt, a, bm, cm, h0):
    bsz, seqlen, nh, hd = xh.shape
    ng, ns = bm.shape[2], bm.shape[3]
    ne = nh // ng
    nc = seqlen // SSM_CHUNK
    dtype = xh.dtype
    xs = (xh * dt[..., None]).reshape(bsz, nc, SSM_CHUNK, ng, ne, hd)
    da = (dt.astype(jnp.float32) * a.astype(jnp.float32)).reshape(bsz, nc, SSM_CHUNK, ng, ne)
    cs = jnp.cumsum(da, axis=2)
    bc = bm.reshape(bsz, nc, SSM_CHUNK, ng, ns)
    cc = cm.reshape(bsz, nc, SSM_CHUNK, ng, ns)
    seg = cs[:, :, :, None] - cs[:, :, None, :]
    scan_order = jnp.tril(jnp.ones((SSM_CHUNK, SSM_CHUNK), dtype=bool))[None, None, :, :, None, None]
    decay = jnp.exp(jnp.where(scan_order, seg, -jnp.inf)).astype(dtype)
    scores = jnp.einsum("bclgn,bcsgn->bclsg", cc, bc)
    y_diag = jnp.einsum("bclsge,bcsgep->bclgep", scores[..., None] * decay, xs)
    w_state = jnp.exp(cs[:, :, -1:] - cs).astype(dtype)
    chunk_states = jnp.einsum("bclgn,bclge,bclgep->bcgepn", bc, w_state, xs)
    chunk_decay = jnp.exp(cs[:, :, -1]).astype(dtype)

    def step(h, inp):
        s, d = inp
        return h * d[..., None, None] + s, h

    _, h_prev = lax.scan(step, h0.reshape(bsz, ng, ne, hd, ns).astype(dtype),
                         (jnp.moveaxis(chunk_states, 1, 0), jnp.moveaxis(chunk_decay, 1, 0)))
    y_off = jnp.einsum("bclgn,cbgepn,bclge->bclgep", cc, h_prev, jnp.exp(cs).astype(dtype))
    return (y_diag + y_off).reshape(bsz, seqlen, nh, hd)


def ssd_final_state(xh, dt, a, bm):
    bsz, seqlen, nh, hd = xh.shape
    ng, ns = bm.shape[2], bm.shape[3]
    ne = nh // ng
    cs = jnp.cumsum(dt.astype(jnp.float32) * a.astype(jnp.float32), axis=1)
    w = (jnp.exp(cs[:, -1:] - cs) * dt.astype(jnp.float32)).astype(xh.dtype).reshape(bsz, seqlen, ng, ne)
    st = jnp.einsum("blgn,blge,blgep->bgepn", bm, w, xh.reshape(bsz, seqlen, ng, ne, hd))
    return st.reshape(bsz, nh, hd, ns)


def ctx_ssd_states(hc, w_in, conv_w, conv_b, dtb_f, dtb_b, alog_f, alog_b):
    bsz, clen, _ = hc.shape
    xb = jax.nn.silu(dwconv1d(hc @ w_in[:, OFF_X:OFF_C], conv_w[:, :D_SSM + GN], conv_b[:D_SSM + GN]))
    dt_raw = hc @ w_in[:, OFF_DT:OFF_GLU]
    xh = xb[..., :D_SSM].reshape(bsz, clen, SSM_HEADS, SSM_HEAD_DIM)
    bm = xb[..., D_SSM:].reshape(bsz, clen, SSM_GROUPS, SSM_STATE)
    dt_f = jax.nn.softplus(dt_raw[..., :SSM_HEADS] + dtb_f)
    dt_b = jax.nn.softplus(dt_raw[..., SSM_HEADS:] + dtb_b)
    s_f = ssd_final_state(xh, dt_f, -jnp.exp(alog_f), bm)
    s_b = ssd_final_state(_flip(xh), _flip(dt_b), -jnp.exp(alog_b), _flip(bm))
    return s_f, s_b


def mixer(h, w_in, w_out, conv_w, conv_b, dtb_f, dtb_b, alog_f, alog_b, d_skip, norm_w,
          cw, cb, ln_w, ln_b, h0_f, h0_b, rows):
    bsz, seqlen, _ = h.shape
    proj = h @ w_in
    z = proj[..., OFF_Z:OFF_X]
    xbc = jax.nn.silu(dwconv1d(proj[..., OFF_X:OFF_DT], conv_w, conv_b))
    dt_raw = proj[..., OFF_DT:OFF_GLU]
    glu = proj[..., OFF_GLU:]
    xh = xbc[..., :D_SSM].reshape(bsz, seqlen, SSM_HEADS, SSM_HEAD_DIM)
    bm = xbc[..., D_SSM:D_SSM + GN].reshape(bsz, seqlen, SSM_GROUPS, SSM_STATE)
    cm = xbc[..., D_SSM + GN:].reshape(bsz, seqlen, SSM_GROUPS, SSM_STATE)
    dt_f = jax.nn.softplus(dt_raw[..., :SSM_HEADS] + dtb_f)
    dt_b = jax.nn.softplus(dt_raw[..., SSM_HEADS:] + dtb_b)
    y_f = ssd_chunked(xh, dt_f, -jnp.exp(alog_f), bm, cm, h0_f)
    y_b = _flip(ssd_chunked(_flip(xh), _flip(dt_b), -jnp.exp(alog_b), _flip(bm), _flip(cm), h0_b))
    y = (y_f + y_b + d_skip[:, None] * xh).reshape(bsz, seqlen, D_SSM)
    y = group_rmsnorm(y * jax.nn.silu(z), norm_w, SSM_GROUPS)
    u = glu[..., :D_CONV] * jax.nn.sigmoid(glu[..., D_CONV:])
    if rows is None:
        u = dwconv1d(u, cw, cb)
    else:
        u = axial_dwconv(u, cw, cb, rows)
    u = jax.nn.silu(layernorm(u, ln_w, ln_b))
    return jnp.concatenate([y, u], axis=-1) @ w_out


def _fwd_setup_inputs(seed: int = 0) -> dict:
    key = jax.random.key(seed)
    ks = iter(jax.random.split(key, 40))

    def nrm(shape, scale):
        return jax.random.normal(next(ks), shape, jnp.float32) * scale

    L = DEPTH
    d_in_scale = D_MODEL ** -0.5
    u_dt = jax.random.uniform(next(ks), (2, L, SSM_HEADS), jnp.float32)
    dt0 = jnp.exp(u_dt * (math.log(0.1) - math.log(1e-3)) + math.log(1e-3))
    dt_bias = dt0 + jnp.log(-jnp.expm1(-dt0))
    a_log = jnp.log(jax.random.uniform(next(ks), (2, L, SSM_HEADS), jnp.float32, 1.0, 16.0))
    return {
        "x": nrm((BATCH, SEQ, D_MODEL), 1.0),
        "c": nrm((BATCH, D_MODEL), 1.0),
        "ctx": nrm((BATCH, CTX_LEN, D_MODEL), 1.0),
        "c_ctx": nrm((D_MODEL,), 1.0),
        "w_mod": nrm((L, D_MODEL, N_MOD * D_MODEL), 0.5 * d_in_scale),
        "b_mod": nrm((L, N_MOD * D_MODEL), 0.02),
        "norm_ffn1": 1.0 + nrm((L, D_MODEL), 0.02),
        "ffn1_gate": nrm((L, D_MODEL, D_FF), d_in_scale),
        "ffn1_up": nrm((L, D_MODEL, D_FF), d_in_scale),
        "ffn1_down": nrm((L, D_FF, D_MODEL), D_FF ** -0.5),
        "norm_mix": 1.0 + nrm((L, D_MODEL), 0.02),
        "w_in": nrm((L, D_MODEL, D_IN_PROJ), d_in_scale),
        "ssm_conv_w": nrm((L, SSM_CONV_W, D_SSM + 2 * GN), SSM_CONV_W ** -0.5),
        "ssm_conv_b": nrm((L, D_SSM + 2 * GN), 0.02),
        "dt_bias_fwd": dt_bias[0],
        "dt_bias_bwd": dt_bias[1],
        "a_log_fwd": a_log[0],
        "a_log_bwd": a_log[1],
        "ssm_d": 1.0 + nrm((L, SSM_HEADS), 0.02),
        "ssm_norm_w": 1.0 + nrm((L, D_SSM), 0.02),
        "cconv_w": nrm((L, CONF_KERNEL, D_CONV), CONF_KERNEL ** -0.5),
        "cconv_b": nrm((L, D_CONV), 0.02),
        "cconv_ln_w": 1.0 + nrm((L, D_CONV), 0.02),
        "cconv_ln_b": nrm((L, D_CONV), 0.02),
        "w_out": nrm((L, D_INNER, D_MODEL), D_INNER ** -0.5),
        "norm_ffn2": 1.0 + nrm((L, D_MODEL), 0.02),
        "ffn2_gate": nrm((L, D_MODEL, D_FF), d_in_scale),
        "ffn2_up": nrm((L, D_MODEL, D_FF), d_in_scale),
        "ffn2_down": nrm((L, D_FF, D_MODEL), D_FF ** -0.5),
        "final_norm": 1.0 + nrm((D_MODEL,), 0.02),
    }


def _fwd_reference(x, c, ctx, c_ctx, w_mod, b_mod, norm_ffn1, ffn1_gate, ffn1_up, ffn1_down, norm_mix,
              w_in, ssm_conv_w, ssm_conv_b, dt_bias_fwd, dt_bias_bwd, a_log_fwd, a_log_bwd, ssm_d,
              ssm_norm_w, cconv_w, cconv_b, cconv_ln_w, cconv_ln_b, w_out, norm_ffn2, ffn2_gate,
              ffn2_up, ffn2_down, final_norm):
    bsz = x.shape[0]
    rows = x.shape[1] // GRID_W
    xc = ctx
    for i in range(DEPTH):
        last = i == DEPTH - 1
        mod = (jax.nn.silu(c) @ w_mod[i] + b_mod[i]).reshape(bsz, N_MOD, 1, D_MODEL)
        n_c = 5 if last else N_MOD
        mod_c = (jax.nn.silu(c_ctx) @ w_mod[i][:, :n_c * D_MODEL] + b_mod[i][:n_c * D_MODEL]).reshape(n_c, 1, D_MODEL)
        x = x + 0.5 * mod[:, 2] * swiglu(modulate(rmsnorm(x, norm_ffn1[i]), mod[:, 0], mod[:, 1]),
                                         ffn1_gate[i], ffn1_up[i], ffn1_down[i])
        xc = xc + 0.5 * mod_c[2] * swiglu(modulate(rmsnorm(xc, norm_ffn1[i]), mod_c[0], mod_c[1]),
                                          ffn1_gate[i], ffn1_up[i], ffn1_down[i])
        hx = modulate(rmsnorm(x, norm_mix[i]), mod[:, 3], mod[:, 4])
        hc = modulate(rmsnorm(xc, norm_mix[i]), mod_c[3], mod_c[4])
        s_f, s_b = ctx_ssd_states(hc, w_in[i], ssm_conv_w[i], ssm_conv_b[i], dt_bias_fwd[i],
                                  dt_bias_bwd[i], a_log_fwd[i], a_log_bwd[i])
        x = x + mod[:, 5] * mixer(hx, w_in[i], w_out[i], ssm_conv_w[i], ssm_conv_b[i], dt_bias_fwd[i],
                                  dt_bias_bwd[i], a_log_fwd[i], a_log_bwd[i], ssm_d[i], ssm_norm_w[i],
                                  cconv_w[i], cconv_b[i], cconv_ln_w[i], cconv_ln_b[i], s_f, s_b, rows)
        if not last:
            zero_state = jnp.zeros_like(s_f)
            xc = xc + mod_c[5] * mixer(hc, w_in[i], w_out[i], ssm_conv_w[i], ssm_conv_b[i], dt_bias_fwd[i],
                                       dt_bias_bwd[i], a_log_fwd[i], a_log_bwd[i], ssm_d[i], ssm_norm_w[i],
                                       cconv_w[i], cconv_b[i], cconv_ln_w[i], cconv_ln_b[i],
                                       zero_state, zero_state, None)
        x = x + 0.5 * mod[:, 8] * swiglu(modulate(rmsnorm(x, norm_ffn2[i]), mod[:, 6], mod[:, 7]),
                                         ffn2_gate[i], ffn2_up[i], ffn2_down[i])
        if not last:
            xc = xc + 0.5 * mod_c[8] * swiglu(modulate(rmsnorm(xc, norm_ffn2[i]), mod_c[6], mod_c[7]),
                                              ffn2_gate[i], ffn2_up[i], ffn2_down[i])
    return rmsnorm(x, final_norm)


import jax as _jax
import jax.numpy as _jnp

TWIN_FORMAT = 'train_step'
FWD_PARAMS = ['x', 'c', 'ctx', 'c_ctx', 'w_mod', 'b_mod', 'norm_ffn1', 'ffn1_gate', 'ffn1_up', 'ffn1_down', 'norm_mix', 'w_in', 'ssm_conv_w', 'ssm_conv_b', 'dt_bias_fwd', 'dt_bias_bwd', 'a_log_fwd', 'a_log_bwd', 'ssm_d', 'ssm_norm_w', 'cconv_w', 'cconv_b', 'cconv_ln_w', 'cconv_ln_b', 'w_out', 'norm_ffn2', 'ffn2_gate', 'ffn2_up', 'ffn2_down', 'final_norm']
TWIN_WEIGHTS = ['c_ctx', 'w_mod', 'b_mod', 'norm_ffn1', 'ffn1_gate', 'ffn1_up', 'ffn1_down', 'norm_mix', 'w_in', 'ssm_conv_w', 'ssm_conv_b', 'dt_bias_fwd', 'dt_bias_bwd', 'a_log_fwd', 'a_log_bwd', 'ssm_d', 'ssm_norm_w', 'cconv_w', 'cconv_b', 'cconv_ln_w', 'cconv_ln_b', 'w_out', 'norm_ffn2', 'ffn2_gate', 'ffn2_up', 'ffn2_down', 'final_norm']
TWIN_DIFF_INPUT = 'x'
TWIN_INPUTS = ['x', 'c', 'ctx', 'c_ctx', 'w_mod', 'b_mod', 'norm_ffn1', 'ffn1_gate', 'ffn1_up', 'ffn1_down', 'norm_mix', 'w_in', 'ssm_conv_w', 'ssm_conv_b', 'dt_bias_fwd', 'dt_bias_bwd', 'a_log_fwd', 'a_log_bwd', 'ssm_d', 'ssm_norm_w', 'cconv_w', 'cconv_b', 'cconv_ln_w', 'cconv_ln_b', 'w_out', 'norm_ffn2', 'ffn2_gate', 'ffn2_up', 'ffn2_down', 'final_norm', 'loss_target', 'm_c_ctx', 'm_w_mod', 'm_b_mod', 'm_norm_ffn1', 'm_ffn1_gate', 'm_ffn1_up', 'm_ffn1_down', 'm_norm_mix', 'm_w_in', 'm_ssm_conv_w', 'm_ssm_conv_b', 'm_dt_bias_fwd', 'm_dt_bias_bwd', 'm_a_log_fwd', 'm_a_log_bwd', 'm_ssm_d', 'm_ssm_norm_w', 'm_cconv_w', 'm_cconv_b', 'm_cconv_ln_w', 'm_cconv_ln_b', 'm_w_out', 'm_norm_ffn2', 'm_ffn2_gate', 'm_ffn2_up', 'm_ffn2_down', 'm_final_norm', 'v_c_ctx', 'v_w_mod', 'v_b_mod', 'v_norm_ffn1', 'v_ffn1_gate', 'v_ffn1_up', 'v_ffn1_down', 'v_norm_mix', 'v_w_in', 'v_ssm_conv_w', 'v_ssm_conv_b', 'v_dt_bias_fwd', 'v_dt_bias_bwd', 'v_a_log_fwd', 'v_a_log_bwd', 'v_ssm_d', 'v_ssm_norm_w', 'v_cconv_w', 'v_cconv_b', 'v_cconv_ln_w', 'v_cconv_ln_b', 'v_w_out', 'v_norm_ffn2', 'v_ffn2_gate', 'v_ffn2_up', 'v_ffn2_down', 'v_final_norm']
TWIN_OUTPUTS = ['loss', 'grad_x', 'grad_c_ctx', 'grad_w_mod', 'grad_b_mod', 'grad_norm_ffn1', 'grad_ffn1_gate', 'grad_ffn1_up', 'grad_ffn1_down', 'grad_norm_mix', 'grad_w_in', 'grad_ssm_conv_w', 'grad_ssm_conv_b', 'grad_dt_bias_fwd', 'grad_dt_bias_bwd', 'grad_a_log_fwd', 'grad_a_log_bwd', 'grad_ssm_d', 'grad_ssm_norm_w', 'grad_cconv_w', 'grad_cconv_b', 'grad_cconv_ln_w', 'grad_cconv_ln_b', 'grad_w_out', 'grad_norm_ffn2', 'grad_ffn2_gate', 'grad_ffn2_up', 'grad_ffn2_down', 'grad_final_norm', 'delta_c_ctx', 'delta_w_mod', 'delta_b_mod', 'delta_norm_ffn1', 'delta_ffn1_gate', 'delta_ffn1_up', 'delta_ffn1_down', 'delta_norm_mix', 'delta_w_in', 'delta_ssm_conv_w', 'delta_ssm_conv_b', 'delta_dt_bias_fwd', 'delta_dt_bias_bwd', 'delta_a_log_fwd', 'delta_a_log_bwd', 'delta_ssm_d', 'delta_ssm_norm_w', 'delta_cconv_w', 'delta_cconv_b', 'delta_cconv_ln_w', 'delta_cconv_ln_b', 'delta_w_out', 'delta_norm_ffn2', 'delta_ffn2_gate', 'delta_ffn2_up', 'delta_ffn2_down', 'delta_final_norm', 'new_m_c_ctx', 'new_m_w_mod', 'new_m_b_mod', 'new_m_norm_ffn1', 'new_m_ffn1_gate', 'new_m_ffn1_up', 'new_m_ffn1_down', 'new_m_norm_mix', 'new_m_w_in', 'new_m_ssm_conv_w', 'new_m_ssm_conv_b', 'new_m_dt_bias_fwd', 'new_m_dt_bias_bwd', 'new_m_a_log_fwd', 'new_m_a_log_bwd', 'new_m_ssm_d', 'new_m_ssm_norm_w', 'new_m_cconv_w', 'new_m_cconv_b', 'new_m_cconv_ln_w', 'new_m_cconv_ln_b', 'new_m_w_out', 'new_m_norm_ffn2', 'new_m_ffn2_gate', 'new_m_ffn2_up', 'new_m_ffn2_down', 'new_m_final_norm', 'new_v_c_ctx', 'new_v_w_mod', 'new_v_b_mod', 'new_v_norm_ffn1', 'new_v_ffn1_gate', 'new_v_ffn1_up', 'new_v_ffn1_down', 'new_v_norm_mix', 'new_v_w_in', 'new_v_ssm_conv_w', 'new_v_ssm_conv_b', 'new_v_dt_bias_fwd', 'new_v_dt_bias_bwd', 'new_v_a_log_fwd', 'new_v_a_log_bwd', 'new_v_ssm_d', 'new_v_ssm_norm_w', 'new_v_cconv_w', 'new_v_cconv_b', 'new_v_cconv_ln_w', 'new_v_cconv_ln_b', 'new_v_w_out', 'new_v_norm_ffn2', 'new_v_ffn2_gate', 'new_v_ffn2_up', 'new_v_ffn2_down', 'new_v_final_norm']
TWIN_LEAF_KINDS = {'loss': 'loss', 'grad_x': 'grad_x', 'grad_c_ctx': 'grad_w', 'grad_w_mod': 'grad_w', 'grad_b_mod': 'grad_w', 'grad_norm_ffn1': 'grad_w', 'grad_ffn1_gate': 'grad_w', 'grad_ffn1_up': 'grad_w', 'grad_ffn1_down': 'grad_w', 'grad_norm_mix': 'grad_w', 'grad_w_in': 'grad_w', 'grad_ssm_conv_w': 'grad_w', 'grad_ssm_conv_b': 'grad_w', 'grad_dt_bias_fwd': 'grad_w', 'grad_dt_bias_bwd': 'grad_w', 'grad_a_log_fwd': 'grad_w', 'grad_a_log_bwd': 'grad_w', 'grad_ssm_d': 'grad_w', 'grad_ssm_norm_w': 'grad_w', 'grad_cconv_w': 'grad_w', 'grad_cconv_b': 'grad_w', 'grad_cconv_ln_w': 'grad_w', 'grad_cconv_ln_b': 'grad_w', 'grad_w_out': 'grad_w', 'grad_norm_ffn2': 'grad_w', 'grad_ffn2_gate': 'grad_w', 'grad_ffn2_up': 'grad_w', 'grad_ffn2_down': 'grad_w', 'grad_final_norm': 'grad_w', 'delta_c_ctx': 'delta_w', 'delta_w_mod': 'delta_w', 'delta_b_mod': 'delta_w', 'delta_norm_ffn1': 'delta_w', 'delta_ffn1_gate': 'delta_w', 'delta_ffn1_up': 'delta_w', 'delta_ffn1_down': 'delta_w', 'delta_norm_mix': 'delta_w', 'delta_w_in': 'delta_w', 'delta_ssm_conv_w': 'delta_w', 'delta_ssm_conv_b': 'delta_w', 'delta_dt_bias_fwd': 'delta_w', 'delta_dt_bias_bwd': 'delta_w', 'delta_a_log_fwd': 'delta_w', 'delta_a_log_bwd': 'delta_w', 'delta_ssm_d': 'delta_w', 'delta_ssm_norm_w': 'delta_w', 'delta_cconv_w': 'delta_w', 'delta_cconv_b': 'delta_w', 'delta_cconv_ln_w': 'delta_w', 'delta_cconv_ln_b': 'delta_w', 'delta_w_out': 'delta_w', 'delta_norm_ffn2': 'delta_w', 'delta_ffn2_gate': 'delta_w', 'delta_ffn2_up': 'delta_w', 'delta_ffn2_down': 'delta_w', 'delta_final_norm': 'delta_w', 'new_m_c_ctx': 'new_m', 'new_m_w_mod': 'new_m', 'new_m_b_mod': 'new_m', 'new_m_norm_ffn1': 'new_m', 'new_m_ffn1_gate': 'new_m', 'new_m_ffn1_up': 'new_m', 'new_m_ffn1_down': 'new_m', 'new_m_norm_mix': 'new_m', 'new_m_w_in': 'new_m', 'new_m_ssm_conv_w': 'new_m', 'new_m_ssm_conv_b': 'new_m', 'new_m_dt_bias_fwd': 'new_m', 'new_m_dt_bias_bwd': 'new_m', 'new_m_a_log_fwd': 'new_m', 'new_m_a_log_bwd': 'new_m', 'new_m_ssm_d': 'new_m', 'new_m_ssm_norm_w': 'new_m', 'new_m_cconv_w': 'new_m', 'new_m_cconv_b': 'new_m', 'new_m_cconv_ln_w': 'new_m', 'new_m_cconv_ln_b': 'new_m', 'new_m_w_out': 'new_m', 'new_m_norm_ffn2': 'new_m', 'new_m_ffn2_gate': 'new_m', 'new_m_ffn2_up': 'new_m', 'new_m_ffn2_down': 'new_m', 'new_m_final_norm': 'new_m', 'new_v_c_ctx': 'new_v', 'new_v_w_mod': 'new_v', 'new_v_b_mod': 'new_v', 'new_v_norm_ffn1': 'new_v', 'new_v_ffn1_gate': 'new_v', 'new_v_ffn1_up': 'new_v', 'new_v_ffn1_down': 'new_v', 'new_v_norm_mix': 'new_v', 'new_v_w_in': 'new_v', 'new_v_ssm_conv_w': 'new_v', 'new_v_ssm_conv_b': 'new_v', 'new_v_dt_bias_fwd': 'new_v', 'new_v_dt_bias_bwd': 'new_v', 'new_v_a_log_fwd': 'new_v', 'new_v_a_log_bwd': 'new_v', 'new_v_ssm_d': 'new_v', 'new_v_ssm_norm_w': 'new_v', 'new_v_cconv_w': 'new_v', 'new_v_cconv_b': 'new_v', 'new_v_cconv_ln_w': 'new_v', 'new_v_cconv_ln_b': 'new_v', 'new_v_w_out': 'new_v', 'new_v_norm_ffn2': 'new_v', 'new_v_ffn2_gate': 'new_v', 'new_v_ffn2_up': 'new_v', 'new_v_ffn2_down': 'new_v', 'new_v_final_norm': 'new_v'}


def _forward(args):
    return _fwd_reference(*[args[k] for k in FWD_PARAMS])


def _output_shape():
    out = _jax.eval_shape(lambda: _forward(_fwd_setup_inputs(0)))
    return out.shape, out.dtype

N_MICROBATCH = 1
ADAM_LR = 0.001
ADAM_B1 = 0.9
ADAM_B2 = 0.999
ADAM_EPS = 1e-08
ADAM_WD = 0.01
ADAM_STEP = 10
PER_EXAMPLE_BATCH_AXIS = {'x': 0, 'c': 0, 'ctx': 0, 'loss_target': 0}
SHARED_INPUTS = []
_WEIGHT_DTYPES = {'c_ctx': _jnp.float32, 'w_mod': _jnp.float32, 'b_mod': _jnp.float32, 'norm_ffn1': _jnp.float32, 'ffn1_gate': _jnp.float32, 'ffn1_up': _jnp.float32, 'ffn1_down': _jnp.float32, 'norm_mix': _jnp.float32, 'w_in': _jnp.float32, 'ssm_conv_w': _jnp.float32, 'ssm_conv_b': _jnp.float32, 'dt_bias_fwd': _jnp.float32, 'dt_bias_bwd': _jnp.float32, 'a_log_fwd': _jnp.float32, 'a_log_bwd': _jnp.float32, 'ssm_d': _jnp.float32, 'ssm_norm_w': _jnp.float32, 'cconv_w': _jnp.float32, 'cconv_b': _jnp.float32, 'cconv_ln_w': _jnp.float32, 'cconv_ln_b': _jnp.float32, 'w_out': _jnp.float32, 'norm_ffn2': _jnp.float32, 'ffn2_gate': _jnp.float32, 'ffn2_up': _jnp.float32, 'ffn2_down': _jnp.float32, 'final_norm': _jnp.float32}
MOMENT_SCALE = {'c_ctx': 4.072552e-03, 'w_mod': 5.801144e-02, 'b_mod': 9.480400e-02, 'norm_ffn1': 4.171027e-02, 'ffn1_gate': 1.684971e-02, 'ffn1_up': 1.633136e-02, 'ffn1_down': 2.729049e-02, 'norm_mix': 7.844858e-02, 'w_in': 3.899261e-02, 'ssm_conv_w': 4.273586e-02, 'ssm_conv_b': 6.579684e-02, 'dt_bias_fwd': 9.824790e-02, 'dt_bias_bwd': 2.595755e-01, 'a_log_fwd': 1.246388e-01, 'a_log_bwd': 2.475921e-01, 'ssm_d': 2.186613e-01, 'ssm_norm_w': 5.545694e-02, 'cconv_w': 3.271440e-02, 'cconv_b': 7.448317e-02, 'cconv_ln_w': 3.987853e-02, 'cconv_ln_b': 3.091305e-02, 'w_out': 5.953707e-02, 'norm_ffn2': 3.561426e-02, 'ffn2_gate': 1.625191e-02, 'ffn2_up': 1.571546e-02, 'ffn2_down': 2.604968e-02, 'final_norm': 6.389156e+01}


def _to_microbatches(a, axis):
    t = _jnp.moveaxis(a, axis, 0)
    t = t.reshape((N_MICROBATCH, t.shape[0] // N_MICROBATCH) + t.shape[1:])
    return _jnp.moveaxis(t, 1, axis + 1)


def setup_inputs(seed: int = 0) -> dict:
    inp = _fwd_setup_inputs(seed)
    key = _jax.random.fold_in(_jax.random.key(seed), 7919)
    shape, _ = _output_shape()
    out = dict(inp)
    out["loss_target"] = _jax.random.normal(_jax.random.fold_in(key, 0), shape, _jnp.float32)
    for i, name in enumerate(TWIN_WEIGHTS):
        w = inp[name].astype(_jnp.float32)
        if MOMENT_SCALE is None:
            s = _jnp.sqrt(_jnp.mean(_jnp.square(w)) + 1e-30)
        else:
            s = MOMENT_SCALE[name]
        km, kv = _jax.random.split(_jax.random.fold_in(key, i + 1))
        out[name] = w
        out["m_" + name] = s * _jax.random.normal(km, w.shape, _jnp.float32)
        out["v_" + name] = (s * s) * _jax.random.uniform(kv, w.shape, _jnp.float32, 0.5, 1.5)
    if N_MICROBATCH > 1:
        for name, axis in PER_EXAMPLE_BATCH_AXIS.items():
            out[name] = _to_microbatches(out[name], axis)
    return {'x': out['x'], 'c': out['c'], 'ctx': out['ctx'], 'c_ctx': out['c_ctx'], 'w_mod': out['w_mod'], 'b_mod': out['b_mod'], 'norm_ffn1': out['norm_ffn1'], 'ffn1_gate': out['ffn1_gate'], 'ffn1_up': out['ffn1_up'], 'ffn1_down': out['ffn1_down'], 'norm_mix': out['norm_mix'], 'w_in': out['w_in'], 'ssm_conv_w': out['ssm_conv_w'], 'ssm_conv_b': out['ssm_conv_b'], 'dt_bias_fwd': out['dt_bias_fwd'], 'dt_bias_bwd': out['dt_bias_bwd'], 'a_log_fwd': out['a_log_fwd'], 'a_log_bwd': out['a_log_bwd'], 'ssm_d': out['ssm_d'], 'ssm_norm_w': out['ssm_norm_w'], 'cconv_w': out['cconv_w'], 'cconv_b': out['cconv_b'], 'cconv_ln_w': out['cconv_ln_w'], 'cconv_ln_b': out['cconv_ln_b'], 'w_out': out['w_out'], 'norm_ffn2': out['norm_ffn2'], 'ffn2_gate': out['ffn2_gate'], 'ffn2_up': out['ffn2_up'], 'ffn2_down': out['ffn2_down'], 'final_norm': out['final_norm'], 'loss_target': out['loss_target'], 'm_c_ctx': out['m_c_ctx'], 'm_w_mod': out['m_w_mod'], 'm_b_mod': out['m_b_mod'], 'm_norm_ffn1': out['m_norm_ffn1'], 'm_ffn1_gate': out['m_ffn1_gate'], 'm_ffn1_up': out['m_ffn1_up'], 'm_ffn1_down': out['m_ffn1_down'], 'm_norm_mix': out['m_norm_mix'], 'm_w_in': out['m_w_in'], 'm_ssm_conv_w': out['m_ssm_conv_w'], 'm_ssm_conv_b': out['m_ssm_conv_b'], 'm_dt_bias_fwd': out['m_dt_bias_fwd'], 'm_dt_bias_bwd': out['m_dt_bias_bwd'], 'm_a_log_fwd': out['m_a_log_fwd'], 'm_a_log_bwd': out['m_a_log_bwd'], 'm_ssm_d': out['m_ssm_d'], 'm_ssm_norm_w': out['m_ssm_norm_w'], 'm_cconv_w': out['m_cconv_w'], 'm_cconv_b': out['m_cconv_b'], 'm_cconv_ln_w': out['m_cconv_ln_w'], 'm_cconv_ln_b': out['m_cconv_ln_b'], 'm_w_out': out['m_w_out'], 'm_norm_ffn2': out['m_norm_ffn2'], 'm_ffn2_gate': out['m_ffn2_gate'], 'm_ffn2_up': out['m_ffn2_up'], 'm_ffn2_down': out['m_ffn2_down'], 'm_final_norm': out['m_final_norm'], 'v_c_ctx': out['v_c_ctx'], 'v_w_mod': out['v_w_mod'], 'v_b_mod': out['v_b_mod'], 'v_norm_ffn1': out['v_norm_ffn1'], 'v_ffn1_gate': out['v_ffn1_gate'], 'v_ffn1_up': out['v_ffn1_up'], 'v_ffn1_down': out['v_ffn1_down'], 'v_norm_mix': out['v_norm_mix'], 'v_w_in': out['v_w_in'], 'v_ssm_conv_w': out['v_ssm_conv_w'], 'v_ssm_conv_b': out['v_ssm_conv_b'], 'v_dt_bias_fwd': out['v_dt_bias_fwd'], 'v_dt_bias_bwd': out['v_dt_bias_bwd'], 'v_a_log_fwd': out['v_a_log_fwd'], 'v_a_log_bwd': out['v_a_log_bwd'], 'v_ssm_d': out['v_ssm_d'], 'v_ssm_norm_w': out['v_ssm_norm_w'], 'v_cconv_w': out['v_cconv_w'], 'v_cconv_b': out['v_cconv_b'], 'v_cconv_ln_w': out['v_cconv_ln_w'], 'v_cconv_ln_b': out['v_cconv_ln_b'], 'v_w_out': out['v_w_out'], 'v_norm_ffn2': out['v_norm_ffn2'], 'v_ffn2_gate': out['v_ffn2_gate'], 'v_ffn2_up': out['v_ffn2_up'], 'v_ffn2_down': out['v_ffn2_down'], 'v_final_norm': out['v_final_norm']}


def _loss(weights, diff, rest, loss_target):
    with _jax.named_scope("forward"):
        args = {**rest, TWIN_DIFF_INPUT: diff, **{k: w.astype(_WEIGHT_DTYPES[k]) for k, w in weights.items()}}
        y = _forward(args)
    with _jax.named_scope("loss_head"):
        err = _jnp.square(y.astype(_jnp.float32) - loss_target)
        return 0.5 * _jnp.sum(_jnp.mean(err, axis=-1)) if err.ndim else 0.5 * err


def _adamw(w, g, m, v):
    m = ADAM_B1 * m + (1.0 - ADAM_B1) * g
    v = ADAM_B2 * v + (1.0 - ADAM_B2) * _jnp.square(g)
    m_hat = m / (1.0 - ADAM_B1 ** ADAM_STEP)
    v_hat = v / (1.0 - ADAM_B2 ** ADAM_STEP)
    delta = -ADAM_LR * (m_hat / (_jnp.sqrt(v_hat) + ADAM_EPS) + ADAM_WD * w)
    return delta, m, v


def reference(x, c, ctx, c_ctx, w_mod, b_mod, norm_ffn1, ffn1_gate, ffn1_up, ffn1_down, norm_mix, w_in, ssm_conv_w, ssm_conv_b, dt_bias_fwd, dt_bias_bwd, a_log_fwd, a_log_bwd, ssm_d, ssm_norm_w, cconv_w, cconv_b, cconv_ln_w, cconv_ln_b, w_out, norm_ffn2, ffn2_gate, ffn2_up, ffn2_down, final_norm, loss_target, m_c_ctx, m_w_mod, m_b_mod, m_norm_ffn1, m_ffn1_gate, m_ffn1_up, m_ffn1_down, m_norm_mix, m_w_in, m_ssm_conv_w, m_ssm_conv_b, m_dt_bias_fwd, m_dt_bias_bwd, m_a_log_fwd, m_a_log_bwd, m_ssm_d, m_ssm_norm_w, m_cconv_w, m_cconv_b, m_cconv_ln_w, m_cconv_ln_b, m_w_out, m_norm_ffn2, m_ffn2_gate, m_ffn2_up, m_ffn2_down, m_final_norm, v_c_ctx, v_w_mod, v_b_mod, v_norm_ffn1, v_ffn1_gate, v_ffn1_up, v_ffn1_down, v_norm_mix, v_w_in, v_ssm_conv_w, v_ssm_conv_b, v_dt_bias_fwd, v_dt_bias_bwd, v_a_log_fwd, v_a_log_bwd, v_ssm_d, v_ssm_norm_w, v_cconv_w, v_cconv_b, v_cconv_ln_w, v_cconv_ln_b, v_w_out, v_norm_ffn2, v_ffn2_gate, v_ffn2_up, v_ffn2_down, v_final_norm):
    given = dict(x=x, c=c, ctx=ctx, c_ctx=c_ctx, w_mod=w_mod, b_mod=b_mod, norm_ffn1=norm_ffn1, ffn1_gate=ffn1_gate, ffn1_up=ffn1_up, ffn1_down=ffn1_down, norm_mix=norm_mix, w_in=w_in, ssm_conv_w=ssm_conv_w, ssm_conv_b=ssm_conv_b, dt_bias_fwd=dt_bias_fwd, dt_bias_bwd=dt_bias_bwd, a_log_fwd=a_log_fwd, a_log_bwd=a_log_bwd, ssm_d=ssm_d, ssm_norm_w=ssm_norm_w, cconv_w=cconv_w, cconv_b=cconv_b, cconv_ln_w=cconv_ln_w, cconv_ln_b=cconv_ln_b, w_out=w_out, norm_ffn2=norm_ffn2, ffn2_gate=ffn2_gate, ffn2_up=ffn2_up, ffn2_down=ffn2_down, final_norm=final_norm, loss_target=loss_target, m_c_ctx=m_c_ctx, m_w_mod=m_w_mod, m_b_mod=m_b_mod, m_norm_ffn1=m_norm_ffn1, m_ffn1_gate=m_ffn1_gate, m_ffn1_up=m_ffn1_up, m_ffn1_down=m_ffn1_down, m_norm_mix=m_norm_mix, m_w_in=m_w_in, m_ssm_conv_w=m_ssm_conv_w, m_ssm_conv_b=m_ssm_conv_b, m_dt_bias_fwd=m_dt_bias_fwd, m_dt_bias_bwd=m_dt_bias_bwd, m_a_log_fwd=m_a_log_fwd, m_a_log_bwd=m_a_log_bwd, m_ssm_d=m_ssm_d, m_ssm_norm_w=m_ssm_norm_w, m_cconv_w=m_cconv_w, m_cconv_b=m_cconv_b, m_cconv_ln_w=m_cconv_ln_w, m_cconv_ln_b=m_cconv_ln_b, m_w_out=m_w_out, m_norm_ffn2=m_norm_ffn2, m_ffn2_gate=m_ffn2_gate, m_ffn2_up=m_ffn2_up, m_ffn2_down=m_ffn2_down, m_final_norm=m_final_norm, v_c_ctx=v_c_ctx, v_w_mod=v_w_mod, v_b_mod=v_b_mod, v_norm_ffn1=v_norm_ffn1, v_ffn1_gate=v_ffn1_gate, v_ffn1_up=v_ffn1_up, v_ffn1_down=v_ffn1_down, v_norm_mix=v_norm_mix, v_w_in=v_w_in, v_ssm_conv_w=v_ssm_conv_w, v_ssm_conv_b=v_ssm_conv_b, v_dt_bias_fwd=v_dt_bias_fwd, v_dt_bias_bwd=v_dt_bias_bwd, v_a_log_fwd=v_a_log_fwd, v_a_log_bwd=v_a_log_bwd, v_ssm_d=v_ssm_d, v_ssm_norm_w=v_ssm_norm_w, v_cconv_w=v_cconv_w, v_cconv_b=v_cconv_b, v_cconv_ln_w=v_cconv_ln_w, v_cconv_ln_b=v_cconv_ln_b, v_w_out=v_w_out, v_norm_ffn2=v_norm_ffn2, v_ffn2_gate=v_ffn2_gate, v_ffn2_up=v_ffn2_up, v_ffn2_down=v_ffn2_down, v_final_norm=v_final_norm)
    weights = {n: given[n] for n in TWIN_WEIGHTS}
    shared = {n: given[n] for n in SHARED_INPUTS}
    per_example = {n: given[n] for n in ['x', 'c', 'ctx']}
    grad_fn = _jax.value_and_grad(_loss, argnums=(0, 1))

    def one_microbatch(ex, loss_target):
        ex = dict(ex)
        diff = ex.pop(TWIN_DIFF_INPUT)
        return grad_fn(weights, diff, {**shared, **ex}, loss_target)

    if N_MICROBATCH == 1:
        loss, (grad_w, grad_x) = one_microbatch(per_example, given["loss_target"])
    else:
        def body(carry, xs):
            loss_sum, grad_sum = carry
            l_k, (gw_k, gx_k) = one_microbatch(xs[0], xs[1])
            with _jax.named_scope("update"):
                return (loss_sum + l_k, _jax.tree.map(_jnp.add, grad_sum, gw_k)), gx_k

        init = (_jnp.zeros((), _jnp.float32), _jax.tree.map(_jnp.zeros_like, weights))
        (loss, grad_w), grad_x = _jax.lax.scan(body, init, (per_example, given["loss_target"]))
    with _jax.named_scope("update"):
        delta_w, new_m, new_v = {}, {}, {}
        for n in TWIN_WEIGHTS:
            delta_w[n], new_m[n], new_v[n] = _adamw(weights[n], grad_w[n], given["m_" + n], given["v_" + n])
    return (loss, grad_x, *[grad_w[n] for n in TWIN_WEIGHTS], *[delta_w[n] for n in TWIN_WEIGHTS],
            *[new_m[n] for n in TWIN_WEIGHTS], *[new_v[n] for n in TWIN_WEIGHTS])
```

```python
import functools
import math

import jax
import jax.numpy as jnp
from jax import lax
from jax.experimental import pallas as pl
from jax.experimental.pallas import tpu as pltpu

F32 = jnp.float32
BF16 = jnp.bfloat16
HI = lax.Precision.HIGHEST
MESH = pl.DeviceIdType.MESH

EPS = 1e-6
GRID_W = 64
HEAD_DIM = 64
N_STATE = 128
CHUNK = 128
LANES = 128
N_CHIPS = 4
ADAM_LR, ADAM_B1, ADAM_B2, ADAM_EPS, ADAM_WD, ADAM_STEP = 0.001, 0.9, 0.999, 1e-08, 0.01, 10
VMEM_CAP = 56 * 1024 * 1024


def _params(vmem_bytes=None, n_axes=1):
    kw = dict(dimension_semantics=("arbitrary",) * n_axes)
    if vmem_bytes is not None:
        kw["vmem_limit_bytes"] = int(min(VMEM_CAP, max(32 * 1024 * 1024, vmem_bytes)))
    return pltpu.CompilerParams(**kw)


def _nbytes(shape, dtype):
    return math.prod(shape) * jnp.dtype(dtype).itemsize


def _row_tile(rows, width, cap_bytes=1 << 20):
    best = None
    for t in range(8, rows + 1, 8):
        if rows % t == 0 and t * width * 4 <= cap_bytes:
            best = t
    return best if best is not None else rows


_MODES = {"all8": (8, (1, 2, 3, 4, 5, 6, 7), 0), "chips": (4, (2, 4, 6), 1), "sibling": (2, (1,), 0)}


def exchange(name, arrs, mode, scatter=False):
    nslot, deltas, shift = _MODES[mode]
    n_arr, n_peer = len(arrs), len(deltas)
    out_shape = tuple(jax.ShapeDtypeStruct((nslot,) + (a.shape[1:] if scatter else a.shape), a.dtype) for a in arrs)

    def body(*refs):
        ins, outs = refs[:n_arr], refs[n_arr:2 * n_arr]
        send_sems, recv_sems, local_sems = refs[2 * n_arr:]
        x, y, c = lax.axis_index("x"), lax.axis_index("y"), lax.axis_index("c")
        me = 4 * x + 2 * y + c

        def slot_of(dev):
            return (dev >> shift) & (nslot - 1)

        def src(a, slot):
            return ins[a].at[slot] if scatter else ins[a]

        def peer_of(d):
            flip = lambda v, bit: 1 - v if bit else v
            return (flip(x, (d >> 2) & 1), flip(y, (d >> 1) & 1), flip(c, d & 1))

        def remote(a, k, d, from_slot, to_slot):
            return pltpu.make_async_remote_copy(
                src_ref=src(a, from_slot), dst_ref=outs[a].at[to_slot], send_sem=send_sems.at[a, k],
                recv_sem=recv_sems.at[a, k], device_id=peer_of(d), device_id_type=MESH)

        mine = slot_of(me)
        local = [pltpu.make_async_copy(src(a, mine), outs[a].at[mine], local_sems.at[a]) for a in range(n_arr)]
        for cp in local:
            cp.start()
        sends = [remote(a, k, d, slot_of(me ^ d), mine) for k, d in enumerate(deltas) for a in range(n_arr)]
        for cp in sends:
            cp.start()
        for k, d in enumerate(deltas):
            for a in range(n_arr):
                remote(a, k, d, mine, slot_of(me ^ d)).wait_recv()
        for cp in sends:
            cp.wait_send()
        for cp in local:
            cp.wait()

    any_spec = pl.BlockSpec(memory_space=pl.ANY)
    return pl.pallas_call(
        body, name=name, out_shape=out_shape,
        in_specs=[any_spec] * n_arr, out_specs=tuple([any_spec] * n_arr),
        scratch_shapes=[pltpu.SemaphoreType.DMA((n_arr, n_peer)), pltpu.SemaphoreType.DMA((n_arr, n_peer)),
                        pltpu.SemaphoreType.DMA((n_arr,))],
    )(*arrs)


_DIMS = {"nn": (((1,), (0,)), ((), ())), "nt": (((1,), (1,)), ((), ())), "tn": (((0,), (0,)), ((), ()))}


def matmul(name, pairs, kind, *, a_ch=False, b_ch=False, out_ch=False, out_dtype=F32, rows=None, row_off=0, tm=512):
    a0, b0 = pairs[0]
    n_chunk = a0.shape[0] if a_ch else (b0.shape[0] if b_ch else 1)
    total_rows = a0.shape[-2]
    rows = total_rows - row_off if rows is None else rows
    tm = min(tm, rows)
    assert rows % tm == 0 and row_off % tm == 0, (name, rows, tm, row_off)
    n_rt, off = rows // tm, row_off // tm
    dims = _DIMS[kind]
    n_pair = len(pairs)

    if kind == "tn":
        grid, red_axis, n_red = (n_chunk, n_rt), 1, n_rt
        a_idx = (lambda k, i: (k, i + off, 0)) if a_ch else (lambda k, i: (i + off, 0))
        b_idx = (lambda k, i: (k, i + off, 0)) if b_ch else (lambda k, i: (i + off, 0))
        a_blk = lambda a: ((None, tm, a.shape[-1]) if a_ch else (tm, a.shape[-1]))
        b_blk = lambda b: ((None, tm, b.shape[-1]) if b_ch else (tm, b.shape[-1]))
        o2 = (a0.shape[-1], b0.shape[-1])
        out_shape = ((n_chunk,) + o2) if out_ch else o2
        out_spec = pl.BlockSpec((None,) + o2, lambda k, i: (k, 0, 0)) if out_ch else pl.BlockSpec(o2, lambda k, i: (0, 0))
        acc_shape = o2
    else:
        n_out = b0.shape[-1] if kind == "nn" else b0.shape[-2]
        b2 = b0.shape[-2:]
        if a_ch and b_ch and not out_ch:
            grid, red_axis, n_red = (n_rt, n_chunk), 1, n_chunk
            a_idx, b_idx = (lambda i, k: (k, i + off, 0)), (lambda i, k: (k, 0, 0))
            a_blk = lambda a: (None, tm, a.shape[-1])
            b_blk = lambda b: (None,) + tuple(b.shape[-2:])
            out_shape, out_spec = (rows, n_out), pl.BlockSpec((tm, n_out), lambda i, k: (i, 0))
        elif out_ch:
            assert b_ch and not a_ch
            grid, red_axis, n_red = (n_chunk, n_rt), None, 1
            a_idx, b_idx = (lambda k, i: (i + off, 0)), (lambda k, i: (k, 0, 0))
            a_blk = lambda a: (tm, a.shape[-1])
            b_blk = lambda b: (None,) + tuple(b.shape[-2:])
            out_shape, out_spec = (n_chunk, rows, n_out), pl.BlockSpec((None, tm, n_out), lambda k, i: (k, i, 0))
        else:
            assert not (a_ch or b_ch)
            grid, red_axis, n_red = (n_rt,), None, 1
            a_idx, b_idx = (lambda i: (i + off, 0)), (lambda i: (0, 0))
            a_blk = lambda a: (tm, a.shape[-1])
            b_blk = lambda b: tuple(b.shape)
            out_shape, out_spec = (rows, n_out), pl.BlockSpec((tm, n_out), lambda i: (i, 0))
        acc_shape = (tm, n_out)

    def body(*refs):
        out = refs[2 * n_pair]

        def compute():
            acc = None
            for p in range(n_pair):
                d = lax.dot_general(refs[2 * p][...].astype(BF16), refs[2 * p + 1][...].astype(BF16), dims,
                                    preferred_element_type=F32)
                acc = d if acc is None else acc + d
            return acc

        if n_red == 1:
            out[...] = compute().astype(out.dtype)
        else:
            acc_ref = refs[2 * n_pair + 1]
            r = pl.program_id(red_axis)

            @pl.when(r == 0)
            def _():
                acc_ref[...] = jnp.zeros_like(acc_ref)

            acc_ref[...] += compute()

            @pl.when(r == n_red - 1)
            def _():
                out[...] = acc_ref[...].astype(out.dtype)

    in_specs, args, vmem = [], [], 0
    for a, b in pairs:
        in_specs += [pl.BlockSpec(a_blk(a), a_idx), pl.BlockSpec(b_blk(b), b_idx)]
        args += [a, b]
        vmem += 2 * (_nbytes([s for s in a_blk(a) if s], a.dtype) + _nbytes([s for s in b_blk(b) if s], b.dtype))
    vmem += 3 * _nbytes(acc_shape, F32) + 2 * n_pair * _nbytes(acc_shape, F32)
    scratch = [pltpu.VMEM(acc_shape, F32)] if n_red > 1 else []
    return pl.pallas_call(
        body, name=name, out_shape=jax.ShapeDtypeStruct(out_shape, out_dtype), grid=grid, in_specs=in_specs,
        out_specs=out_spec, scratch_shapes=scratch, compiler_params=_params(vmem + (8 << 20), len(grid)),
    )(*args)


def row(arr, width=None, cb=0, roff=0):
    return (arr, arr.shape[-1] if width is None else width, cb, roff)


def _row_spec(desc, tm, limit=None):
    _, width, cb, roff = desc
    if limit is None:
        return pl.BlockSpec((tm, width), lambda i: (i + roff, cb))
    return pl.BlockSpec((tm, width), lambda i: (jnp.minimum(i, limit - 1) + roff, cb))


def _segmenter(tm, seq_len, n_lat):
    seg = lambda i: jnp.where(i * tm < n_lat, (i * tm) // seq_len, n_lat // seq_len)
    first = lambda i: jnp.where(i * tm < n_lat, (i * tm) % seq_len == 0, i * tm == n_lat)
    return seg, first


def rowwise(name, fn, rows, segs, params, outs, *, tm, n_tiles, seg_fn=None):
    n_r, n_s, n_p = len(rows), len(segs), len(params)

    def body(*refs):
        vals = [r[...].astype(F32) for r in refs[:n_r]] + [r[...] for r in refs[n_r:n_r + n_s + n_p]]
        res = fn(*vals)
        for o_ref, v in zip(refs[n_r + n_s + n_p:], res):
            o_ref[...] = v.astype(o_ref.dtype)

    in_specs = [_row_spec(d, tm) for d in rows]
    in_specs += [pl.BlockSpec((None, 1, s.shape[-1]), lambda i: (seg_fn(i), 0, 0)) for s in segs]
    in_specs += [pl.BlockSpec(p.shape, lambda i: (0, 0)) for p in params]
    vmem = sum(2 * tm * d[1] * 4 for d in rows) + sum(3 * tm * w * 4 for _, w, _ in outs) + sum(2 * p.size * 4 for p in params)
    res = pl.pallas_call(
        body, name=name, grid=(n_tiles,), in_specs=in_specs,
        out_shape=tuple(jax.ShapeDtypeStruct((r, w), dt) for r, w, dt in outs),
        out_specs=tuple(pl.BlockSpec((tm, w), lambda i: (i, 0)) for _, w, _ in outs),
        compiler_params=_params(2 * vmem + (8 << 20)),
    )(*[d[0] for d in rows], *segs, *params)
    return res


def rowwise_bwd(name, fn, rows, segs, params, cts, row_grads, *, tm, n_tiles, seg_fn=None, first_fn=None, adds=None):
    adds = adds or {}
    need = [k for k, v in enumerate(row_grads) if v is not None]
    n_r, n_s, n_p = len(rows), len(segs), len(params)
    n_ct = sum(len(lst) for lst in cts)
    add_keys = sorted(adds)

    def body(*refs):
        it = iter(refs)
        row_refs = [next(it) for _ in range(n_r)]
        seg_refs = [next(it) for _ in range(n_s)]
        par_refs = [next(it) for _ in range(n_p)]
        ct_refs = [[next(it) for _ in lst] for lst in cts]
        add_refs = {k: next(it) for k in add_keys}
        rg_refs = {k: next(it) for k in need}
        sg_refs = [next(it) for _ in range(n_s)]
        pg_refs = [next(it) for _ in range(n_p)]
        i = pl.program_id(0)
        rv = [r[...].astype(F32) for r in row_refs]
        sv = [r[...] for r in seg_refs]
        pv = [r[...] for r in par_refs]

        def f(*args):
            rr = list(rv)
            for j, k in enumerate(need):
                rr[k] = args[j]
            return fn(*rr, *args[len(need):])

        _, vjp = jax.vjp(f, *[rv[k] for k in need], *sv, *pv)
        ctv = []
        for lst in ct_refs:
            acc = lst[0][...].astype(F32)
            for r in lst[1:]:
                acc = acc + r[...].astype(F32)
            ctv.append(acc)
        g = vjp(tuple(ctv))
        for j, k in enumerate(need):
            gv = g[j]
            if k in adds:
                lim = adds[k][1]
                av = add_refs[k][...].astype(F32)
                gv = gv + (av if lim is None else jnp.where(i < lim, av, 0.0))
            lim = row_grads[k][2]
            if lim is None:
                rg_refs[k][...] = gv.astype(rg_refs[k].dtype)
            else:
                @pl.when(i < lim)
                def _(gv=gv, k=k):
                    rg_refs[k][...] = gv.astype(rg_refs[k].dtype)
        if n_s:
            opens = first_fn(i)
            for ref, gv in zip(sg_refs, g[len(need):len(need) + n_s]):
                @pl.when(opens)
                def _(ref=ref, gv=gv):
                    ref[...] = gv

                @pl.when(jnp.logical_not(opens))
                def _(ref=ref, gv=gv):
                    ref[...] += gv
        for ref, gv in zip(pg_refs, g[len(need) + n_s:]):
            @pl.when(i == 0)
            def _(ref=ref, gv=gv):
                ref[...] = gv

            @pl.when(i > 0)
            def _(ref=ref, gv=gv):
                ref[...] += gv

    seg_spec = lambda s: pl.BlockSpec((None, 1, s.shape[-1]), lambda i: (seg_fn(i), 0, 0))
    par_spec = lambda p: pl.BlockSpec(p.shape, lambda i: (0, 0))
    in_specs = [_row_spec(d, tm) for d in rows] + [seg_spec(s) for s in segs] + [par_spec(p) for p in params]
    args = [d[0] for d in rows] + list(segs) + list(params)
    for lst in cts:
        in_specs += [_row_spec(d, tm) for d in lst]
        args += [d[0] for d in lst]
    for k in add_keys:
        in_specs.append(_row_spec(adds[k][0], tm, adds[k][1]))
        args.append(adds[k][0][0])
    out_shape, out_specs = [], []
    for k in need:
        n_rows, dt, lim = row_grads[k]
        out_shape.append(jax.ShapeDtypeStruct((n_rows, rows[k][1]), dt))
        out_specs.append(_row_spec((None, rows[k][1], 0, 0), tm, lim))
    for s in segs:
        out_shape.append(jax.ShapeDtypeStruct(s.shape, F32))
        out_specs.append(seg_spec(s))
    for p in params:
        out_shape.append(jax.ShapeDtypeStruct(p.shape, F32))
        out_specs.append(par_spec(p))
    vmem = sum(tm * d[1] * 4 for d in rows) * 6 + n_ct * tm * max(d[1] for d in rows) * 8
    return pl.pallas_call(
        body, name=name, grid=(n_tiles,), in_specs=in_specs, out_shape=tuple(out_shape), out_specs=tuple(out_specs),
        compiler_params=_params(vmem + (8 << 20)),
    )(*args)


def _silu(v):
    return v * jax.nn.sigmoid(v)


def _rms(v, w):
    return v * lax.rsqrt(jnp.mean(v * v, axis=-1, keepdims=True) + EPS) * w


def fn_norm_mod(x, shift, scale, w):
    return (_rms(x, w) * (1.0 + scale) + shift,)


def fn_act(g, u):
    return (_silu(g) * u,)


def make_fn_resid(coef):
    def fn(x, f, gate):
        return (x + coef * gate * f,)
    return fn


def fn_silu_bias(v, b):
    return (_silu(v + b),)


def make_fn_gate_groupnorm(width):
    half = width // 2

    def fn(yf, yb, z, w):
        y = (yf + yb) * _silu(z)
        lane = lax.broadcasted_iota(jnp.int32, y.shape, 1)
        lo = lane < half
        sq = y * y
        s_lo = jnp.sum(jnp.where(lo, sq, 0.0), axis=-1, keepdims=True)
        s_hi = jnp.sum(jnp.where(lo, 0.0, sq), axis=-1, keepdims=True)
        r = jnp.where(lo, lax.rsqrt(s_lo / half + EPS), lax.rsqrt(s_hi / half + EPS))
        return (y * r * w,)
    return fn


def fn_glu(a, b):
    return (a * jax.nn.sigmoid(b),)


def fn_ln_silu(vw, vh, cb, lw, lb):
    v = jnp.concatenate([vw, vh], axis=-1) + cb
    mu = jnp.mean(v, axis=-1, keepdims=True)
    var = jnp.mean(jnp.square(v - mu), axis=-1, keepdims=True)
    return (_silu((v - mu) * lax.rsqrt(var + EPS) * lw + lb),)


def _col_tile(width):
    return width // 3 if width % (3 * LANES) == 0 else width


def mod_fwd(a_rows, w_shard, b_shard):
    n, d = a_rows.shape
    ws = w_shard.shape[1]
    tn = _col_tile(ws)

    def body(a_ref, w_ref, b_ref, o_ref):
        a = _silu(a_ref[...]).astype(BF16)
        o_ref[...] = jnp.dot(a, w_ref[...].astype(BF16), preferred_element_type=F32) + b_ref[...]

    return pl.pallas_call(
        body, name="mod_fwd", grid=(ws // tn,), out_shape=jax.ShapeDtypeStruct((n, ws), F32),
        in_specs=[pl.BlockSpec((n, d), lambda j: (0, 0)), pl.BlockSpec((d, tn), lambda j: (0, j)),
                  pl.BlockSpec((1, tn), lambda j: (0, j))],
        out_specs=pl.BlockSpec((n, tn), lambda j: (0, j)), compiler_params=_params(),
    )(a_rows, w_shard, b_shard)


def mod_bwd(a_rows, d_shard, d_full, w_shard, ctx_rows):
    n, d = a_rows.shape
    ws = w_shard.shape[1]
    tn = _col_tile(ws)
    n_ct = ws // tn

    def body(a_ref, ds_ref, df_ref, w_ref, gw_ref, gb_ref, q_ref):
        j = pl.program_id(0)
        a = _silu(a_ref[...])
        ds = ds_ref[...]
        gw_ref[...] = lax.dot_general(a, ds, _DIMS["tn"], precision=HI, preferred_element_type=F32)
        dctx = ds[ctx_rows[0]:ctx_rows[0] + 1, :]
        for r in ctx_rows[1:]:
            dctx = dctx + ds[r:r + 1, :]
        q = lax.dot_general(jnp.broadcast_to(dctx, (8, tn)), w_ref[...], _DIMS["nt"], precision=HI,
                            preferred_element_type=F32)

        @pl.when(j == 0)
        def _():
            q_ref[...] = q
            df = df_ref[...]
            acc = df[0:1, :]
            for r in range(1, n):
                acc = acc + df[r:r + 1, :]
            gb_ref[...] = acc

        @pl.when(j > 0)
        def _():
            q_ref[...] += q

    return pl.pallas_call(
        body, name="mod_bwd", grid=(n_ct,),
        out_shape=(jax.ShapeDtypeStruct((d, ws), F32), jax.ShapeDtypeStruct((1, d_full.shape[1]), F32),
                   jax.ShapeDtypeStruct((8, d), F32)),
        in_specs=[pl.BlockSpec((n, d), lambda j: (0, 0)), pl.BlockSpec((n, tn), lambda j: (0, j)),
                  pl.BlockSpec(d_full.shape, lambda j: (0, 0)), pl.BlockSpec((d, tn), lambda j: (0, j))],
        out_specs=(pl.BlockSpec((d, tn), lambda j: (0, j)), pl.BlockSpec((1, d_full.shape[1]), lambda j: (0, 0)),
                   pl.BlockSpec((8, d), lambda j: (0, 0))),
        compiler_params=_params(40 << 20),
    )(a_rows, d_shard, d_full, w_shard)


def _shifted(xs, d, tok, width):
    if d == 0:
        return xs
    n = xs.shape[0]
    sh = pltpu.roll(xs, (-d) % n, axis=0)
    return jnp.where((tok + d >= 0) & (tok + d < width), sh, 0.0)


def tapsum_roll(name, x, xcb, w, wcb, *, seq_len, n_seq, row_blk_off, width, piece, cb, ncb, pad, flip):
    n_tap = w.shape[0]
    n_piece = seq_len // piece

    def body(x_ref, w_ref, o_ref):
        wv = w_ref[...]
        tok = lax.broadcasted_iota(jnp.int32, (piece, 1), 0) % width

        def do_piece(p, carry):
            start = pl.multiple_of(p * piece, piece)
            xs = x_ref[pl.ds(start, piece), :]
            acc = jnp.zeros_like(xs)
            for k in range(n_tap):
                d = pad - k if flip else k - pad
                acc = acc + wv[k:k + 1, :] * _shifted(xs, d, tok, width)
            o_ref[pl.ds(start, piece), :] = acc
            return carry

        lax.fori_loop(0, n_piece, do_piece, 0)

    return pl.pallas_call(
        body, name=name, grid=(ncb, n_seq), out_shape=jax.ShapeDtypeStruct((n_seq * seq_len, ncb * cb), F32),
        in_specs=[pl.BlockSpec((seq_len, cb), lambda j, s: (row_blk_off + s, xcb + j)),
                  pl.BlockSpec((n_tap, cb), lambda j, s: (0, wcb + j))],
        out_specs=pl.BlockSpec((seq_len, cb), lambda j, s: (s, j)),
        compiler_params=_params(8 * seq_len * cb * 4 + (8 << 20), 2),
    )(x, w)


def tapgrad_roll(name, dy, dycb, dy_blk_off, x, xcb, x_blk_off, *, n_tap, seq_len, n_seq, width, piece, cb, ncb, pad):
    n_piece = seq_len // piece

    def body(dy_ref, x_ref, o_ref):
        @pl.when(pl.program_id(1) == 0)
        def _():
            o_ref[...] = jnp.zeros_like(o_ref)

        tok = lax.broadcasted_iota(jnp.int32, (piece, 1), 0) % width

        def do_piece(p, carry):
            start = pl.multiple_of(p * piece, piece)
            xs = x_ref[pl.ds(start, piece), :]
            dv = dy_ref[pl.ds(start, piece), :]
            for k in range(n_tap):
                o_ref[k:k + 1, :] += jnp.sum(dv * _shifted(xs, k - pad, tok, width), axis=0, keepdims=True)
            return carry

        lax.fori_loop(0, n_piece, do_piece, 0)

    return pl.pallas_call(
        body, name=name, grid=(ncb, n_seq), out_shape=jax.ShapeDtypeStruct((n_tap, ncb * cb), F32),
        in_specs=[pl.BlockSpec((seq_len, cb), lambda j, s: (dy_blk_off + s, dycb + j)),
                  pl.BlockSpec((seq_len, cb), lambda j, s: (x_blk_off + s, xcb + j))],
        out_specs=pl.BlockSpec((n_tap, cb), lambda j, s: (0, j)),
        compiler_params=_params(8 * seq_len * cb * 4 + (8 << 20), 2),
    )(dy, x)


def tapsum_rows(name, x, xcb, w, wcb, *, seq_len, n_seq, cb, ncb, pad, flip):
    n_tap = w.shape[0]
    n_row = seq_len // GRID_W
    halo = pad * GRID_W

    def body(x_ref, w_ref, o_ref, xp):
        xp[pl.ds(0, halo), :] = jnp.zeros((halo, cb), F32)
        xp[pl.ds(halo + seq_len, halo), :] = jnp.zeros((halo, cb), F32)
        xp[pl.ds(halo, seq_len), :] = x_ref[...]
        wv = w_ref[...]

        def do_row(r, carry):
            acc = jnp.zeros((GRID_W, cb), F32)
            for k in range(n_tap):
                d = pad - k if flip else k - pad
                acc = acc + wv[k:k + 1, :] * xp[pl.ds(pl.multiple_of((r + pad + d) * GRID_W, GRID_W), GRID_W), :]
            o_ref[pl.ds(pl.multiple_of(r * GRID_W, GRID_W), GRID_W), :] = acc
            return carry

        lax.fori_loop(0, n_row, do_row, 0)

    return pl.pallas_call(
        body, name=name, grid=(ncb, n_seq), out_shape=jax.ShapeDtypeStruct((n_seq * seq_len, ncb * cb), F32),
        in_specs=[pl.BlockSpec((seq_len, cb), lambda j, s: (s, xcb + j)),
                  pl.BlockSpec((n_tap, cb), lambda j, s: (0, wcb + j))],
        out_specs=pl.BlockSpec((seq_len, cb), lambda j, s: (s, j)),
        scratch_shapes=[pltpu.VMEM((seq_len + 2 * halo, cb), F32)],
        compiler_params=_params(10 * seq_len * cb * 4 + (8 << 20), 2),
    )(x, w)


def tapgrad_rows(name, dy, dycb, x, xcb, *, n_tap, seq_len, n_seq, cb, ncb, pad):
    n_row = seq_len // GRID_W
    halo = pad * GRID_W

    def body(dy_ref, x_ref, o_ref, xp):
        @pl.when(pl.program_id(1) == 0)
        def _():
            o_ref[...] = jnp.zeros_like(o_ref)

        xp[pl.ds(0, halo), :] = jnp.zeros((halo, cb), F32)
        xp[pl.ds(halo + seq_len, halo), :] = jnp.zeros((halo, cb), F32)
        xp[pl.ds(halo, seq_len), :] = x_ref[...]

        def do_row(r, carry):
            dv = dy_ref[pl.ds(pl.multiple_of(r * GRID_W, GRID_W), GRID_W), :]
            for k in range(n_tap):
                xs = xp[pl.ds(pl.multiple_of((r + k) * GRID_W, GRID_W), GRID_W), :]
                o_ref[k:k + 1, :] += jnp.sum(dv * xs, axis=0, keepdims=True)
            return carry

        lax.fori_loop(0, n_row, do_row, 0)

    return pl.pallas_call(
        body, name=name, grid=(ncb, n_seq), out_shape=jax.ShapeDtypeStruct((n_tap, ncb * cb), F32),
        in_specs=[pl.BlockSpec((seq_len, cb), lambda j, s: (s, dycb + j)),
                  pl.BlockSpec((seq_len, cb), lambda j, s: (s, xcb + j))],
        out_specs=pl.BlockSpec((n_tap, cb), lambda j, s: (0, j)),
        scratch_shapes=[pltpu.VMEM((seq_len + 2 * halo, cb), F32)],
        compiler_params=_params(10 * seq_len * cb * 4 + (8 << 20), 2),
    )(dy, x)


def _ssd_blocks(b, s, *, rev, n_ctx, n_lat, lat_blocks):
    if rev:
        return jnp.where(s < n_ctx, lat_blocks + b * n_ctx + (n_ctx - 1 - s), b * n_lat + (n_lat - 1 - (s - n_ctx)))
    return jnp.where(s < n_ctx, lat_blocks + b * n_ctx + s, b * n_lat + (s - n_ctx))


def _ssd_common(xbc, raw, dtb, alog, dsk, *, rev, ds, n_head):
    if rev:
        raw = pltpu.roll(raw, LANES - n_head, axis=1)
    pre = raw + dtb
    dt = jnp.maximum(pre, 0.0) + jnp.log1p(jnp.exp(-jnp.abs(pre)))
    sig = jax.nn.sigmoid(pre)
    a = -jnp.exp(alog)
    da = dt * a
    ri = lax.broadcasted_iota(jnp.int32, (CHUNK, CHUNK), 0)
    ci = lax.broadcasted_iota(jnp.int32, (CHUNK, CHUNK), 1)
    mask = (ci >= ri) if rev else (ci <= ri)
    tri = mask.astype(F32)
    tri_t = ((ci <= ri) if rev else (ci >= ri)).astype(F32)
    cs = jnp.dot(tri, da, precision=HI, preferred_element_type=F32)
    tot = jnp.sum(da, axis=0, keepdims=True)
    widen = (lax.broadcasted_iota(jnp.int32, (LANES, ds), 1) // HEAD_DIM
             == lax.broadcasted_iota(jnp.int32, (LANES, ds), 0)).astype(F32)
    wide = lambda v: jnp.dot(v, widen, precision=HI, preferred_element_type=F32)
    wide1 = lambda v: wide(jnp.broadcast_to(v, (8, LANES)))[0:1, :]
    cs_w, tot_w = wide(cs), wide1(tot)
    xh = xbc[:, :ds]
    dt_w = wide(dt)
    return dict(
        dt=dt, sig=sig, a=a, cs=cs, cs_t=cs.T, tot=tot, mask=mask, tri_t=tri_t,
        e_w=jnp.exp(cs_w), wt_w=jnp.exp(tot_w - cs_w), dec_w=jnp.exp(tot_w), dt_w=dt_w, dsk_w=wide1(dsk),
        xh=xh, xs_w=xh * dt_w, bm=xbc[:, ds:ds + 2 * N_STATE], cm=xbc[:, ds + 2 * N_STATE:ds + 4 * N_STATE])


def _decay(q, col):
    seg = q["cs"][:, col:col + 1] - q["cs_t"][col:col + 1, :]
    return jnp.exp(jnp.where(q["mask"], seg, -jnp.inf))


def _split_heads(v):
    lane = lax.broadcasted_iota(jnp.int32, v.shape, 1)
    return jnp.concatenate([jnp.where(lane < HEAD_DIM, v, 0.0), jnp.where(lane >= HEAD_DIM, v, 0.0)], axis=0)


def ssd_fwd(name, xbc, proj, dt_cb, dtb, alog, dsk, *, rev, n_ex, seq_len, ctx_len, ds):
    n_head, half = ds // HEAD_DIM, ds // 2
    n_ctx, n_lat = ctx_len // CHUNK, seq_len // CHUNK
    n_step = n_ctx + n_lat
    blk = functools.partial(_ssd_blocks, rev=rev, n_ctx=n_ctx, n_lat=n_lat, lat_blocks=n_ex * n_lat)
    xw = xbc.shape[1]

    def y_blk(b, s):
        sl = jnp.maximum(s, n_ctx) - n_ctx
        return b * n_lat + ((n_lat - 1 - sl) if rev else sl)

    def body(xbc_ref, dt_ref, dtb_ref, alog_ref, dsk_ref, y_ref, hs_ref, h_scr):
        @pl.when(pl.program_id(1) == 0)
        def _():
            h_scr[...] = jnp.zeros_like(h_scr)

        q = _ssd_common(xbc_ref[...], dt_ref[...], dtb_ref[...], alog_ref[...], dsk_ref[...], rev=rev, ds=ds, n_head=n_head)
        h = h_scr[...]
        hs_ref[...] = h
        for g in range(2):
            lo = g * half
            bg = q["bm"][:, g * N_STATE:(g + 1) * N_STATE].astype(BF16)
            cg = q["cm"][:, g * N_STATE:(g + 1) * N_STATE].astype(BF16)
            scores = lax.dot_general(cg, bg, _DIMS["nt"], preferred_element_type=F32)
            hg = h[:, lo:lo + half]
            off = jnp.dot(cg, hg.astype(BF16), preferred_element_type=F32)
            for j in range(half // LANES):
                c0 = (lo + j * LANES) // HEAD_DIM
                ln = slice(lo + j * LANES, lo + (j + 1) * LANES)
                p_cat = jnp.concatenate([scores * _decay(q, c0), scores * _decay(q, c0 + 1)], axis=1).astype(BF16)
                diag = jnp.dot(p_cat, _split_heads(q["xs_w"][:, ln]).astype(BF16), preferred_element_type=F32)
                y_ref[:, ln] = (diag + q["e_w"][:, ln] * off[:, j * LANES:(j + 1) * LANES]
                                + q["dsk_w"][:, ln] * q["xh"][:, ln])
            v = (q["wt_w"][:, lo:lo + half] * q["xs_w"][:, lo:lo + half]).astype(BF16)
            h_scr[:, lo:lo + half] = (q["dec_w"][:, lo:lo + half] * hg
                                      + lax.dot_general(bg, v, _DIMS["tn"], preferred_element_type=F32))

    vec = pl.BlockSpec((1, LANES), lambda b, s: (0, 0))
    return pl.pallas_call(
        body, name=name, grid=(n_ex, n_step),
        out_shape=(jax.ShapeDtypeStruct((n_ex * seq_len, ds), F32), jax.ShapeDtypeStruct((n_ex, n_step, N_STATE, ds), F32)),
        in_specs=[pl.BlockSpec((CHUNK, xw), lambda b, s: (blk(b, s), 0)),
                  pl.BlockSpec((CHUNK, LANES), lambda b, s: (blk(b, s), dt_cb)), vec, vec, vec],
        out_specs=(pl.BlockSpec((CHUNK, ds), lambda b, s: (y_blk(b, s), 0)),
                   pl.BlockSpec((None, None, N_STATE, ds), lambda b, s: (b, s, 0, 0))),
        scratch_shapes=[pltpu.VMEM((N_STATE, ds), F32)], compiler_params=_params(40 << 20, 2),
    )(xbc, proj, dtb, alog, dsk)


def ssd_bwd(name, xbc, proj, dt_cb, hs, dy, dtb, alog, dsk, *, rev, n_ex, seq_len, ctx_len, ds):
    n_head, half = ds // HEAD_DIM, ds // 2
    n_ctx, n_lat = ctx_len // CHUNK, seq_len // CHUNK
    n_step = n_ctx + n_lat
    n_tok = n_ex * (seq_len + ctx_len)
    blk0 = functools.partial(_ssd_blocks, rev=rev, n_ctx=n_ctx, n_lat=n_lat, lat_blocks=n_ex * n_lat)
    step = lambda sp: n_step - 1 - sp
    blk = lambda b, sp: blk0(b, step(sp))
    xw = xbc.shape[1]

    def dy_blk(b, sp):
        sl = jnp.maximum(step(sp), n_ctx) - n_ctx
        return b * n_lat + ((n_lat - 1 - sl) if rev else sl)

    def body(xbc_ref, dt_ref, hs_ref, dy_ref, dtb_ref, alog_ref, dsk_ref,
             dxbc_ref, ddt_ref, dalog_ref, ddtb_ref, ddsk_ref, dh_scr):
        b, sp = pl.program_id(0), pl.program_id(1)

        @pl.when(sp == 0)
        def _():
            dh_scr[...] = jnp.zeros_like(dh_scr)

        @pl.when((sp == 0) & (b == 0))
        def _():
            dalog_ref[...] = jnp.zeros_like(dalog_ref)
            ddtb_ref[...] = jnp.zeros_like(ddtb_ref)
            ddsk_ref[...] = jnp.zeros_like(ddsk_ref)

        q = _ssd_common(xbc_ref[...], dt_ref[...], dtb_ref[...], alog_ref[...], dsk_ref[...], rev=rev, ds=ds, n_head=n_head)
        h = hs_ref[...]
        d_y = jnp.where(step(sp) >= n_ctx, dy_ref[...], 0.0)
        dh_next = dh_scr[...]
        lane_row = lax.broadcasted_iota(jnp.int32, (1, LANES), 1)
        d_cs = jnp.zeros((CHUNK, LANES), F32)
        dxs_parts, de_parts, dwt_parts, ddec_parts = [], [], [], []
        for g in range(2):
            lo = g * half
            gs = slice(lo, lo + half)
            bg = q["bm"][:, g * N_STATE:(g + 1) * N_STATE].astype(BF16)
            cg = q["cm"][:, g * N_STATE:(g + 1) * N_STATE].astype(BF16)
            scores = lax.dot_general(cg, bg, _DIMS["nt"], preferred_element_type=F32)
            hg, dyg, dhn = h[:, gs], d_y[:, gs], dh_next[:, gs]
            off = jnp.dot(cg, hg.astype(BF16), preferred_element_type=F32)
            d_off = (q["e_w"][:, gs] * dyg).astype(BF16)
            de_parts.append(dyg * off)
            d_c = lax.dot_general(d_off, hg.astype(BF16), _DIMS["nt"], preferred_element_type=F32)
            dh_scr[:, gs] = (lax.dot_general(cg, d_off, _DIMS["tn"], preferred_element_type=F32)
                             + q["dec_w"][:, gs] * dhn)
            b_dh = jnp.dot(bg, dhn.astype(BF16), preferred_element_type=F32)
            v = q["wt_w"][:, gs] * q["xs_w"][:, gs]
            d_b = lax.dot_general(v.astype(BF16), dhn.astype(BF16), _DIMS["nt"], preferred_element_type=F32)
            dwt_parts.append(q["xs_w"][:, gs] * b_dh)
            ddec_parts.append(jnp.sum(hg * dhn, axis=0, keepdims=True))
            d_scores = jnp.zeros((CHUNK, CHUNK), F32)
            for j in range(half // LANES):
                c0 = (lo + j * LANES) // HEAD_DIM
                ln = slice(lo + j * LANES, lo + (j + 1) * LANES)
                l0, l1 = _decay(q, c0), _decay(q, c0 + 1)
                p0, p1 = scores * l0, scores * l1
                dy_st = _split_heads(d_y[:, ln]).astype(BF16)
                d_p = lax.dot_general(dy_st, q["xs_w"][:, ln].astype(BF16), _DIMS["nt"], preferred_element_type=F32)
                d_p0, d_p1 = d_p[:CHUNK], d_p[CHUNK:]
                d_scores = d_scores + d_p0 * l0 + d_p1 * l1
                for col, t in ((c0, d_p0 * p0), (c0 + 1, d_p1 * p1)):
                    d_cs = d_cs + jnp.sum(t - t.T, axis=1, keepdims=True) * (lane_row == col).astype(F32)
                p_st = jnp.concatenate([p0, p1], axis=0).astype(BF16)
                dxs_parts.append(lax.dot_general(p_st, dy_st, _DIMS["tn"], preferred_element_type=F32)
                                 + q["wt_w"][:, ln] * b_dh[:, j * LANES:(j + 1) * LANES])
            d_sc = d_scores.astype(BF16)
            d_c = d_c + jnp.dot(d_sc, bg, preferred_element_type=F32)
            d_b = d_b + lax.dot_general(d_sc, cg, _DIMS["tn"], preferred_element_type=F32)
            dxbc_ref[:, ds + g * N_STATE:ds + (g + 1) * N_STATE] = d_b
            dxbc_ref[:, ds + (2 + g) * N_STATE:ds + (3 + g) * N_STATE] = d_c
        d_xs = jnp.concatenate(dxs_parts, axis=1)
        narrow_m = (lax.broadcasted_iota(jnp.int32, (ds, LANES), 0) // HEAD_DIM
                    == lax.broadcasted_iota(jnp.int32, (ds, LANES), 1)).astype(F32)
        narrow = lambda v: jnp.dot(v, narrow_m, precision=HI, preferred_element_type=F32)
        narrow1 = lambda v: narrow(jnp.broadcast_to(v, (8, ds)))[0:1, :]
        e, wt, dec = jnp.exp(q["cs"]), jnp.exp(q["tot"] - q["cs"]), jnp.exp(q["tot"])
        d_wt = narrow(jnp.concatenate(dwt_parts, axis=1)) * wt
        d_cs = d_cs + narrow(jnp.concatenate(de_parts, axis=1)) * e - d_wt
        d_tot = jnp.sum(d_wt, axis=0, keepdims=True) + narrow1(jnp.concatenate(ddec_parts, axis=1)) * dec
        d_da = jnp.dot(q["tri_t"], d_cs, precision=HI, preferred_element_type=F32) + d_tot
        d_dt = d_da * q["a"] + narrow(d_xs * q["xh"])
        dxbc_ref[:, :ds] = d_xs * q["dt_w"] + q["dsk_w"] * d_y
        dalog_ref[...] += jnp.sum(d_da * q["dt"], axis=0, keepdims=True) * q["a"]
        d_raw = d_dt * q["sig"]
        ddtb_ref[...] += jnp.sum(d_raw, axis=0, keepdims=True)
        ddsk_ref[...] += narrow1(jnp.sum(d_y * q["xh"], axis=0, keepdims=True))
        ddt_ref[...] = pltpu.roll(d_raw, n_head, axis=1) if rev else d_raw

    vec = pl.BlockSpec((1, LANES), lambda b, s: (0, 0))
    vec_shape = jax.ShapeDtypeStruct((1, LANES), F32)
    return pl.pallas_call(
        body, name=name, grid=(n_ex, n_step),
        out_shape=(jax.ShapeDtypeStruct((n_tok, xw), F32), jax.ShapeDtypeStruct((n_tok, LANES), F32),
                   vec_shape, vec_shape, vec_shape),
        in_specs=[pl.BlockSpec((CHUNK, xw), lambda b, s: (blk(b, s), 0)),
                  pl.BlockSpec((CHUNK, LANES), lambda b, s: (blk(b, s), dt_cb)),
                  pl.BlockSpec((None, None, N_STATE, ds), lambda b, s: (b, step(s), 0, 0)),
                  pl.BlockSpec((CHUNK, ds), lambda b, s: (dy_blk(b, s), 0)), vec, vec, vec],
        out_specs=(pl.BlockSpec((CHUNK, xw), lambda b, s: (blk(b, s), 0)),
                   pl.BlockSpec((CHUNK, LANES), lambda b, s: (blk(b, s), 0)), vec, vec, vec),
        scratch_shapes=[pltpu.VMEM((N_STATE, ds), F32)], compiler_params=_params(48 << 20, 2),
    )(xbc, proj, hs, dy, dtb, alog, dsk)


def final_loss(x3, target, w, *, tm):
    n, d = x3.shape

    def body(x_ref, t_ref, w_ref, dx_ref, dw_ref, loss_ref):
        i = pl.program_id(0)
        t = t_ref[...]

        def per_feature(xv, wv):
            err = _rms(xv, wv) - t
            return 0.5 * jnp.sum(err * err, axis=0, keepdims=True) / d

        lv, vjp = jax.vjp(per_feature, x_ref[...], w_ref[...])
        dx, dw = vjp(jnp.ones_like(lv))
        dx_ref[...] = dx

        @pl.when(i == 0)
        def _():
            dw_ref[...] = dw
            loss_ref[...] = lv

        @pl.when(i > 0)
        def _():
            dw_ref[...] += dw
            loss_ref[...] += lv

    tile = pl.BlockSpec((tm, d), lambda i: (i, 0))
    vec = pl.BlockSpec((1, d), lambda i: (0, 0))
    return pl.pallas_call(
        body, name="final_loss", grid=(n // tm,), in_specs=[tile, tile, vec],
        out_shape=(jax.ShapeDtypeStruct((n, d), F32), jax.ShapeDtypeStruct((1, d), F32), jax.ShapeDtypeStruct((1, d), F32)),
        out_specs=(tile, vec, vec), compiler_params=_params(tm * d * 4 * 16 + (8 << 20)),
    )(x3, target, w)


def sum_slots(name, arr):
    n_slot, n_row, width = arr.shape
    tm = _row_tile(n_row, width * n_slot)

    def body(a_ref, o_ref):
        acc = a_ref[0]
        for j in range(1, n_slot):
            acc = acc + a_ref[j]
        o_ref[...] = acc

    return pl.pallas_call(
        body, name=name, grid=(n_row // tm,), out_shape=jax.ShapeDtypeStruct((n_row, width), arr.dtype),
        in_specs=[pl.BlockSpec((n_slot, tm, width), lambda i: (0, i, 0))],
        out_specs=pl.BlockSpec((tm, width), lambda i: (i, 0)), compiler_params=_params(),
    )(arr)


def adamw(name, w, g_slots, m, v):
    n_slot, n_row, width = g_slots.shape
    tm = _row_tile(n_row, width * 2)

    def body(w_ref, g_ref, m_ref, v_ref, go_ref, d_ref, mo_ref, vo_ref):
        g = g_ref[0]
        for j in range(1, n_slot):
            g = g + g_ref[j]
        m2 = ADAM_B1 * m_ref[...] + (1.0 - ADAM_B1) * g
        v2 = ADAM_B2 * v_ref[...] + (1.0 - ADAM_B2) * jnp.square(g)
        m_hat = m2 / (1.0 - ADAM_B1 ** ADAM_STEP)
        v_hat = v2 / (1.0 - ADAM_B2 ** ADAM_STEP)
        go_ref[...] = g
        d_ref[...] = -ADAM_LR * (m_hat / (jnp.sqrt(v_hat) + ADAM_EPS) + ADAM_WD * w_ref[...])
        mo_ref[...] = m2
        vo_ref[...] = v2

    tile = pl.BlockSpec((tm, width), lambda i: (i, 0))
    shape = jax.ShapeDtypeStruct((n_row, width), F32)
    return pl.pallas_call(
        body, name=name, grid=(n_row // tm,), out_shape=(shape,) * 4,
        in_specs=[tile, pl.BlockSpec((n_slot, tm, width), lambda i: (0, i, 0)), tile, tile],
        out_specs=(tile,) * 4, compiler_params=_params(),
    )(w, g_slots, m, v)


def cctx_grad(q_all, c_ctx_row):
    d = c_ctx_row.shape[1]

    def body(q_ref, c_ref, o_ref):
        acc = q_ref[0, 0:1, :]
        for j in (2, 4, 6):
            acc = acc + q_ref[j, 0:1, :]
        _, vjp = jax.vjp(_silu, c_ref[...])
        o_ref[...] = vjp(acc)[0]

    return pl.pallas_call(
        body, name="cctx_grad", out_shape=jax.ShapeDtypeStruct((1, d), F32),
    )(q_all, c_ctx_row)


def loss_total(pack_sum, d):
    def body(p_ref, o_ref):
        o_ref[...] = jnp.sum(p_ref[:, 0:d], axis=1, keepdims=True)

    return pl.pallas_call(
        body, name="loss_total", out_shape=jax.ShapeDtypeStruct((1, 1), F32),
    )(pack_sum)


def _ffn_fwd(tag, xin, n_rows, tm, seg_fn, shift, scale, gate, norm_w, wg, wu, wd):
    d = xin.shape[1]
    n_tiles = n_rows // tm
    (h,) = rowwise(f"{tag}_norm", fn_norm_mod, [row(xin)], [shift, scale], [norm_w], [(n_rows, d, BF16)],
                   tm=tm, n_tiles=n_tiles, seg_fn=seg_fn)
    g = matmul(f"{tag}_gate", [(h, wg)], "nn", b_ch=True, out_ch=True, tm=tm)
    u = matmul(f"{tag}_up", [(h, wu)], "nn", b_ch=True, out_ch=True, tm=tm)
    n_ch, _, n_hid = g.shape
    (act,) = rowwise(f"{tag}_act", fn_act, [row(g.reshape(n_ch * n_rows, n_hid)), row(u.reshape(n_ch * n_rows, n_hid))],
                     [], [], [(n_ch * n_rows, n_hid, BF16)], tm=tm, n_tiles=n_ch * n_tiles)
    act = act.reshape(n_ch, n_rows, n_hid)
    f = matmul(f"{tag}_down", [(act, wd)], "nn", a_ch=True, b_ch=True, tm=tm)
    (xo,) = rowwise(f"{tag}_resid", make_fn_resid(0.5), [row(xin), row(f)], [gate], [], [(n_rows, d, F32)],
                    tm=tm, n_tiles=n_tiles, seg_fn=seg_fn)
    return xo, (h, g, u, act, f)


def _ffn_bwd(tag, d_xo, saved, xin, n_rows, tm, seg_fn, first_fn, shift, scale, gate, norm_w, wg, wu, wd, dx_rows, dx_limit):
    h, g, u, act, f = saved
    d = xin.shape[1]
    n_tiles = n_rows // tm
    n_ch, _, n_hid = g.shape
    d_f, d_gate = rowwise_bwd(f"{tag}_resid_bwd", make_fn_resid(0.5), [row(xin), row(f)], [gate], [], [[row(d_xo)]],
                              [None, (n_rows, BF16, None)], tm=tm, n_tiles=n_tiles, seg_fn=seg_fn, first_fn=first_fn)
    d_act = matmul(f"{tag}_down_dx", [(d_f, wd)], "nt", b_ch=True, out_ch=True, tm=tm)
    d_wd = matmul(f"{tag}_down_dw", [(act, d_f)], "tn", a_ch=True, out_ch=True, tm=tm)
    flat = lambda t: t.reshape(n_ch * n_rows, n_hid)
    d_g, d_u = rowwise_bwd(f"{tag}_act_bwd", fn_act, [row(flat(g)), row(flat(u))], [], [], [[row(flat(d_act))]],
                           [(n_ch * n_rows, BF16, None)] * 2, tm=tm, n_tiles=n_ch * n_tiles)
    d_g, d_u = d_g.reshape(g.shape), d_u.reshape(g.shape)
    d_h = matmul(f"{tag}_up_dx", [(d_g, wg), (d_u, wu)], "nt", a_ch=True, b_ch=True, tm=tm)
    d_wg = matmul(f"{tag}_gate_dw", [(h, d_g)], "tn", b_ch=True, out_ch=True, tm=tm)
    d_wu = matmul(f"{tag}_up_dw", [(h, d_u)], "tn", b_ch=True, out_ch=True, tm=tm)
    d_x, d_shift, d_scale, d_nw = rowwise_bwd(
        f"{tag}_norm_bwd", fn_norm_mod, [row(xin)], [shift, scale], [norm_w], [[row(d_h)]], [(dx_rows, F32, dx_limit)],
        tm=tm, n_tiles=n_tiles, seg_fn=seg_fn, first_fn=first_fn, adds={0: (row(d_xo), None)})
    return d_x, (d_shift, d_scale, d_gate), d_nw, (d_wg, d_wu, d_wd)


def kernel(x, c, ctx, c_ctx, w_mod, b_mod, norm_ffn1, ffn1_gate, ffn1_up, ffn1_down, norm_mix, w_in, ssm_conv_w, ssm_conv_b, dt_bias_fwd, dt_bias_bwd, a_log_fwd, a_log_bwd, ssm_d, ssm_norm_w, cconv_w, cconv_b, cconv_ln_w, cconv_ln_b, w_out, norm_ffn2, ffn2_gate, ffn2_up, ffn2_down, final_norm, loss_target, m_c_ctx, m_w_mod, m_b_mod, m_norm_ffn1, m_ffn1_gate, m_ffn1_up, m_ffn1_down, m_norm_mix, m_w_in, m_ssm_conv_w, m_ssm_conv_b, m_dt_bias_fwd, m_dt_bias_bwd, m_a_log_fwd, m_a_log_bwd, m_ssm_d, m_ssm_norm_w, m_cconv_w, m_cconv_b, m_cconv_ln_w, m_cconv_ln_b, m_w_out, m_norm_ffn2, m_ffn2_gate, m_ffn2_up, m_ffn2_down, m_final_norm, v_c_ctx, v_w_mod, v_b_mod, v_norm_ffn1, v_ffn1_gate, v_ffn1_up, v_ffn1_down, v_norm_mix, v_w_in, v_ssm_conv_w, v_ssm_conv_b, v_dt_bias_fwd, v_dt_bias_bwd, v_a_log_fwd, v_a_log_bwd, v_ssm_d, v_ssm_norm_w, v_cconv_w, v_cconv_b, v_cconv_ln_w, v_cconv_ln_b, v_w_out, v_norm_ffn2, v_ffn2_gate, v_ffn2_up, v_ffn2_down, v_final_norm):
    weights = dict(c_ctx=c_ctx, w_mod=w_mod, b_mod=b_mod, norm_ffn1=norm_ffn1, ffn1_gate=ffn1_gate, ffn1_up=ffn1_up, ffn1_down=ffn1_down, norm_mix=norm_mix, w_in=w_in, ssm_conv_w=ssm_conv_w, ssm_conv_b=ssm_conv_b, dt_bias_fwd=dt_bias_fwd, dt_bias_bwd=dt_bias_bwd, a_log_fwd=a_log_fwd, a_log_bwd=a_log_bwd, ssm_d=ssm_d, ssm_norm_w=ssm_norm_w, cconv_w=cconv_w, cconv_b=cconv_b, cconv_ln_w=cconv_ln_w, cconv_ln_b=cconv_ln_b, w_out=w_out, norm_ffn2=norm_ffn2, ffn2_gate=ffn2_gate, ffn2_up=ffn2_up, ffn2_down=ffn2_down, final_norm=final_norm)
    mom1 = dict(c_ctx=m_c_ctx, w_mod=m_w_mod, b_mod=m_b_mod, norm_ffn1=m_norm_ffn1, ffn1_gate=m_ffn1_gate, ffn1_up=m_ffn1_up, ffn1_down=m_ffn1_down, norm_mix=m_norm_mix, w_in=m_w_in, ssm_conv_w=m_ssm_conv_w, ssm_conv_b=m_ssm_conv_b, dt_bias_fwd=m_dt_bias_fwd, dt_bias_bwd=m_dt_bias_bwd, a_log_fwd=m_a_log_fwd, a_log_bwd=m_a_log_bwd, ssm_d=m_ssm_d, ssm_norm_w=m_ssm_norm_w, cconv_w=m_cconv_w, cconv_b=m_cconv_b, cconv_ln_w=m_cconv_ln_w, cconv_ln_b=m_cconv_ln_b, w_out=m_w_out, norm_ffn2=m_norm_ffn2, ffn2_gate=m_ffn2_gate, ffn2_up=m_ffn2_up, ffn2_down=m_ffn2_down, final_norm=m_final_norm)
    mom2 = dict(c_ctx=v_c_ctx, w_mod=v_w_mod, b_mod=v_b_mod, norm_ffn1=v_norm_ffn1, ffn1_gate=v_ffn1_gate, ffn1_up=v_ffn1_up, ffn1_down=v_ffn1_down, norm_mix=v_norm_mix, w_in=v_w_in, ssm_conv_w=v_ssm_conv_w, ssm_conv_b=v_ssm_conv_b, dt_bias_fwd=v_dt_bias_fwd, dt_bias_bwd=v_dt_bias_bwd, a_log_fwd=v_a_log_fwd, a_log_bwd=v_a_log_bwd, ssm_d=v_ssm_d, ssm_norm_w=v_ssm_norm_w, cconv_w=v_cconv_w, cconv_b=v_cconv_b, cconv_ln_w=v_cconv_ln_w, cconv_ln_b=v_cconv_ln_b, w_out=v_w_out, norm_ffn2=v_norm_ffn2, ffn2_gate=v_ffn2_gate, ffn2_up=v_ffn2_up, ffn2_down=v_ffn2_down, final_norm=v_final_norm)
    order = list(weights)

    n_ex, seq_len, d = x.shape
    ctx_len = ctx.shape[1]
    ds = d
    n_head = ds // HEAD_DIM
    xw = ds + 4 * N_STATE
    n_lat, n_ctx_rows = n_ex * seq_len, n_ex * ctx_len
    n_tok = n_lat + n_ctx_rows
    tm = math.gcd(math.gcd(512, seq_len), n_ctx_rows)
    seg_all, first_all = _segmenter(tm, seq_len, n_lat)
    lat_tiles = n_lat // tm

    xi, yi, ci = lax.axis_index("x"), lax.axis_index("y"), lax.axis_index("c")
    me, chip = 4 * xi + 2 * yi + ci, 2 * xi + yi

    (c_all,) = exchange("gather_c", [c], "all8")
    n_all = 8 * n_ex
    n_cond = -(-(n_all + 1) // 8) * 8
    cond = jnp.concatenate([c_all.reshape(n_all, d), c_ctx[None, :], jnp.zeros((n_cond - n_all - 1, d), F32)])
    mod_w = w_mod.shape[2]
    b_shard = lax.dynamic_slice(b_mod, (0, chip * mod_w), (1, mod_w))
    (mod_g,) = exchange("gather_mod", [mod_fwd(cond, w_mod[0], b_shard)], "chips")
    mod_full = mod_g.transpose(1, 0, 2).reshape(n_cond, N_CHIPS * mod_w)
    mod_mine = lax.dynamic_slice(mod_full, (me * n_ex, 0), (n_ex, 9 * d)).reshape(n_ex, 9, d)
    mod_ctx = mod_full[n_all].reshape(9, d)
    tabs = [jnp.concatenate([mod_mine[:, j], mod_ctx[j][None]])[:, None, :] for j in range(9)]
    lat = lambda t: t[:n_ex]

    bf = lambda w: w[0].astype(BF16)
    wg1, wu1, wd1, win_g, wout_g, wg2, wu2, wd2, w5_g, w31_g = exchange(
        "gather_weights", [bf(ffn1_gate), bf(ffn1_up), bf(ffn1_down), bf(w_in), bf(w_out), bf(ffn2_gate), bf(ffn2_up),
                           bf(ffn2_down), ssm_conv_w[0], cconv_w[0]], "chips")
    unshard_cols = lambda t: t.transpose(1, 0, 2).reshape(t.shape[1], N_CHIPS * t.shape[2])
    win = unshard_cols(win_g)
    o_x, o_dt, o_glu = ds, ds + xw, ds + xw + 2 * n_head
    w_z, w_xbc, w_dt = win[:, :ds], win[:, o_x:o_dt], win[:, o_dt:o_glu]
    w_ga, w_gb = win[:, o_glu:o_glu + d], win[:, o_glu + d:]
    w_dtp = jnp.concatenate([w_dt, jnp.zeros((d, LANES - 2 * n_head), BF16)], axis=1)
    w_cat = jnp.concatenate([w_z, w_ga, w_gb, w_xbc, w_dtp], axis=1)
    cbw = d // 2
    xbc_cb, dt_cb = 3 * d // cbw, (3 * d + xw) // LANES
    wout = wout_g.reshape(2 * d, d)
    wo_y, wo_u = wout[:ds], wout[ds:]
    w5, w31 = unshard_cols(w5_g), unshard_cols(w31_g)
    pad_vec = lambda v: jnp.concatenate([v.reshape(1, -1), jnp.zeros((1, LANES - v.size), F32)], axis=1)
    dtb_f, dtb_b, alog_f, alog_b = map(pad_vec, (dt_bias_fwd, dt_bias_bwd, a_log_fwd, a_log_bwd))
    dsk_f, dsk_b = pad_vec(ssm_d), jnp.zeros((1, LANES), F32)

    xt = jnp.concatenate([x.reshape(n_lat, d), ctx.reshape(n_ctx_rows, d)])
    x1, saved1 = _ffn_fwd("ffn1", xt, n_tok, tm, seg_all, tabs[0], tabs[1], tabs[2], norm_ffn1, wg1, wu1, wd1)
    (h2,) = rowwise("mix_norm", fn_norm_mod, [row(x1)], [tabs[3], tabs[4]], [norm_mix], [(n_tok, d, BF16)],
                    tm=tm, n_tiles=n_tok // tm, seg_fn=seg_all)
    proj = matmul("mix_proj", [(h2, w_cat)], "nn", tm=min(tm, 256))
    c5 = lambda name, src, cb0, flip, seq, off: tapsum_roll(
        name, src, cb0, w5, 0, seq_len=seq, n_seq=n_ex, row_blk_off=off, width=seq, piece=seq, cb=cbw, ncb=xw // cbw,
        pad=w5.shape[0] // 2, flip=flip)
    craw = jnp.concatenate([c5("xbc_conv_lat", proj, xbc_cb, False, seq_len, 0),
                            c5("xbc_conv_ctx", proj, xbc_cb, False, ctx_len, n_lat // ctx_len)])
    (xbc,) = rowwise("xbc_silu", fn_silu_bias, [row(craw)], [], [ssm_conv_b], [(n_tok, xw, F32)], tm=tm, n_tiles=n_tok // tm)
    ssd = dict(n_ex=n_ex, seq_len=seq_len, ctx_len=ctx_len, ds=ds)
    y_f, hs_f = ssd_fwd("ssd_fwd_f", xbc, proj, dt_cb, dtb_f, alog_f, dsk_f, rev=False, **ssd)
    y_b, hs_b = ssd_fwd("ssd_fwd_b", xbc, proj, dt_cb, dtb_b, alog_b, dsk_b, rev=True, **ssd)
    fn_gate = make_fn_gate_groupnorm(ds)
    (yn,) = rowwise("ssd_gate", fn_gate, [row(y_f), row(y_b), row(proj, d, 0)], [], [ssm_norm_w], [(n_lat, ds, BF16)],
                    tm=tm, n_tiles=lat_tiles)
    (u0,) = rowwise("glu", fn_glu, [row(proj, d, 1), row(proj, d, 2)], [], [], [(n_lat, d, F32)], tm=tm, n_tiles=lat_tiles)
    cb31 = max(LANES, d // 4)
    ncb31 = (d // 2) // cb31
    pad31 = w31.shape[0] // 2
    piece31 = min(seq_len, 4 * GRID_W)
    v_w = tapsum_roll("cconv_cols", u0, 0, w31, 0, seq_len=seq_len, n_seq=n_ex, row_blk_off=0, width=GRID_W,
                      piece=piece31, cb=cb31, ncb=ncb31, pad=pad31, flip=False)
    v_h = tapsum_rows("cconv_rows", u0, ncb31, w31, ncb31, seq_len=seq_len, n_seq=n_ex, cb=cb31, ncb=ncb31, pad=pad31, flip=False)
    (un,) = rowwise("cconv_ln", fn_ln_silu, [row(v_w), row(v_h)], [], [cconv_b, cconv_ln_w, cconv_ln_b], [(n_lat, d, BF16)],
                    tm=tm, n_tiles=lat_tiles)
    mix = matmul("mix_out", [(yn, wo_y), (un, wo_u)], "nn", tm=tm)
    seg_lat, first_lat = _segmenter(tm, seq_len, n_lat)
    (x2,) = rowwise("mix_resid", make_fn_resid(1.0), [row(x1), row(mix)], [lat(tabs[5])], [], [(n_lat, d, F32)],
                    tm=tm, n_tiles=lat_tiles, seg_fn=seg_lat)
    x3, saved2 = _ffn_fwd("ffn2", x2, n_lat, tm, seg_lat, lat(tabs[6]), lat(tabs[7]), lat(tabs[8]), norm_ffn2, wg2, wu2, wd2)
    d_x3, d_final, loss_vec = final_loss(x3, loss_target.reshape(n_lat, d), final_norm.reshape(1, d), tm=tm)

    d_x2, (d_s6, d_s7, d_g8), d_nffn2, (d_wg2, d_wu2, d_wd2) = _ffn_bwd(
        "ffn2", d_x3, saved2, x2, n_lat, tm, seg_lat, first_lat, lat(tabs[6]), lat(tabs[7]), lat(tabs[8]), norm_ffn2,
        wg2, wu2, wd2, n_lat, None)
    d_mix, d_g5 = rowwise_bwd("mix_resid_bwd", make_fn_resid(1.0), [row(x1), row(mix)], [lat(tabs[5])], [], [[row(d_x2)]],
                              [None, (n_lat, BF16, None)], tm=tm, n_tiles=lat_tiles, seg_fn=seg_lat, first_fn=first_lat)
    d_yn = matmul("mix_out_dy", [(d_mix, wo_y)], "nt", tm=tm)
    d_un = matmul("mix_out_du", [(d_mix, wo_u)], "nt", tm=tm)
    d_wout = jnp.concatenate([matmul("mix_out_dwy", [(yn, d_mix)], "tn", tm=tm), matmul("mix_out_dwu", [(un, d_mix)], "tn", tm=tm)])
    d_vw, d_vh, d_cb, d_lnw, d_lnb = rowwise_bwd(
        "cconv_ln_bwd", fn_ln_silu, [row(v_w), row(v_h)], [], [cconv_b, cconv_ln_w, cconv_ln_b], [[row(d_un)]],
        [(n_lat, F32, None)] * 2, tm=tm, n_tiles=lat_tiles)
    d_u0w = tapsum_roll("cconv_cols_dx", d_vw, 0, w31, 0, seq_len=seq_len, n_seq=n_ex, row_blk_off=0, width=GRID_W,
                        piece=piece31, cb=cb31, ncb=ncb31, pad=pad31, flip=True)
    d_u0h = tapsum_rows("cconv_rows_dx", d_vh, 0, w31, ncb31, seq_len=seq_len, n_seq=n_ex, cb=cb31, ncb=ncb31, pad=pad31, flip=True)
    d_w31 = jnp.concatenate([
        tapgrad_roll("cconv_cols_dw", d_vw, 0, 0, u0, 0, 0, n_tap=w31.shape[0], seq_len=seq_len, n_seq=n_ex, width=GRID_W,
                     piece=piece31, cb=cb31, ncb=ncb31, pad=pad31),
        tapgrad_rows("cconv_rows_dw", d_vh, 0, u0, ncb31, n_tap=w31.shape[0], seq_len=seq_len, n_seq=n_ex, cb=cb31,
                     ncb=ncb31, pad=pad31)], axis=1)
    d_u0 = jnp.concatenate([d_u0w, d_u0h], axis=1)
    d_ga, d_gb = rowwise_bwd("glu_bwd", fn_glu, [row(proj, d, 1), row(proj, d, 2)], [], [], [[row(d_u0)]],
                             [(n_lat, BF16, None)] * 2, tm=tm, n_tiles=lat_tiles)
    d_ysum, d_z, d_ssmnw = rowwise_bwd(
        "ssd_gate_bwd", fn_gate, [row(y_f), row(y_b), row(proj, d, 0)], [], [ssm_norm_w], [[row(d_yn)]],
        [(n_lat, F32, None), None, (n_lat, BF16, None)], tm=tm, n_tiles=lat_tiles)
    dxbc_f, ddt_f, dalog_f, ddtb_f, ddsk = ssd_bwd("ssd_bwd_f", xbc, proj, dt_cb, hs_f, d_ysum, dtb_f, alog_f, dsk_f, rev=False, **ssd)
    dxbc_b, ddt_b, dalog_b, ddtb_b, _ = ssd_bwd("ssd_bwd_b", xbc, proj, dt_cb, hs_b, d_ysum, dtb_b, alog_b, dsk_b, rev=True, **ssd)
    d_craw, d_conv_b = rowwise_bwd("xbc_silu_bwd", fn_silu_bias, [row(craw)], [], [ssm_conv_b], [[row(dxbc_f), row(dxbc_b)]],
                                   [(n_tok, F32, None)], tm=tm, n_tiles=n_tok // tm)
    d_pxbc = jnp.concatenate([c5("xbc_conv_lat_dx", d_craw, 0, True, seq_len, 0),
                              c5("xbc_conv_ctx_dx", d_craw, 0, True, ctx_len, n_lat // ctx_len)])
    g5 = lambda name, seq, off: tapgrad_roll(name, d_craw, 0, off, proj, xbc_cb, off, n_tap=w5.shape[0], seq_len=seq,
                                             n_seq=n_ex, width=seq, piece=seq, cb=cbw, ncb=xw // cbw, pad=w5.shape[0] // 2)
    d_w5 = g5("xbc_conv_lat_dw", seq_len, 0) + g5("xbc_conv_ctx_dw", ctx_len, n_lat // ctx_len)
    lat_pairs = [(d_z, w_z), (d_ga, w_ga), (d_gb, w_gb), (d_pxbc, w_xbc), (ddt_f, w_dtp), (ddt_b, w_dtp)]
    d_h2 = jnp.concatenate([matmul("mix_proj_dx_lat", lat_pairs, "nt", rows=n_lat, tm=min(tm, 256)),
                            matmul("mix_proj_dx_ctx", lat_pairs[3:], "nt", rows=n_ctx_rows, row_off=n_lat, tm=min(tm, 256))])
    d_wz = matmul("mix_proj_dwz", [(h2, d_z)], "tn", rows=n_lat, tm=tm)
    d_wga = matmul("mix_proj_dwa", [(h2, d_ga)], "tn", rows=n_lat, tm=tm)
    d_wgb = matmul("mix_proj_dwb", [(h2, d_gb)], "tn", rows=n_lat, tm=tm)
    d_wxbc = matmul("mix_proj_dwx", [(h2, d_pxbc)], "tn", tm=tm)
    d_wdt = matmul("mix_proj_dwt", [(h2, ddt_f), (h2, ddt_b)], "tn", tm=tm)
    d_win = jnp.concatenate([d_wz, d_wxbc, d_wdt[:, :2 * n_head], d_wga, d_wgb], axis=1)
    d_x1, d_s3, d_s4, d_nmix = rowwise_bwd(
        "mix_norm_bwd", fn_norm_mod, [row(x1)], [tabs[3], tabs[4]], [norm_mix], [[row(d_h2)]], [(n_tok, F32, None)],
        tm=tm, n_tiles=n_tok // tm, seg_fn=seg_all, first_fn=first_all, adds={0: (row(d_x2), lat_tiles)})
    d_xt, (d_s0, d_s1, d_g2), d_nffn1, (d_wg1, d_wu1, d_wd1) = _ffn_bwd(
        "ffn1", d_x1, saved1, xt, n_tok, tm, seg_all, first_all, tabs[0], tabs[1], tabs[2], norm_ffn1, wg1, wu1, wd1,
        n_lat, lat_tiles)
    grad_x = d_xt.reshape(n_ex, seq_len, d)

    with_ctx0 = lambda t: jnp.concatenate([t, jnp.zeros((1, 1, d), F32)])
    d_tabs = [d_s0, d_s1, d_g2, d_s3, d_s4, with_ctx0(d_g5), with_ctx0(d_s6), with_ctx0(d_s7), with_ctx0(d_g8)]
    d_mod_rows = jnp.concatenate([t[:, 0, :] for t in d_tabs], axis=1)
    n_pad_rows = -(-(n_ex + 1) // 8) * 8
    d_mod_rows = jnp.concatenate([d_mod_rows, jnp.zeros((n_pad_rows - n_ex - 1, 9 * d), F32)])
    small = [("loss", loss_vec), ("norm_ffn1", d_nffn1), ("norm_mix", d_nmix), ("ssm_conv_b", d_conv_b),
             ("dt_bias_fwd", ddtb_f[:, :n_head]), ("dt_bias_bwd", ddtb_b[:, :n_head]), ("a_log_fwd", dalog_f[:, :n_head]),
             ("a_log_bwd", dalog_b[:, :n_head]), ("ssm_d", ddsk[:, :n_head]), ("ssm_norm_w", d_ssmnw), ("cconv_b", d_cb),
             ("cconv_ln_w", d_lnw), ("cconv_ln_b", d_lnb), ("norm_ffn2", d_nffn2), ("final_norm", d_final)]
    n_small = sum(v.size for _, v in small)
    n_pack = -(-n_small // (8 * LANES)) * (8 * LANES)
    pack = jnp.concatenate([v.reshape(-1) for _, v in small] + [jnp.zeros((n_pack - n_small,), F32)]).reshape(-1, LANES)
    pack_all, d_mod_all = exchange("gather_small", [pack, d_mod_rows], "all8")
    pack_sum = sum_slots("small_sum", pack_all)
    loss = loss_total(pack_sum.reshape(1, n_pack), d).reshape(())
    flat_sum = pack_sum.reshape(-1)
    small_grads, pos = {}, 0
    for nm, v in small:
        small_grads[nm] = flat_sum[pos:pos + v.size]
        pos += v.size
    d_mod_all = d_mod_all.reshape(8 * n_pad_rows, 9 * d)
    cond_rows = [jnp.concatenate([cond[j * n_ex:(j + 1) * n_ex], c_ctx[None, :],
                                  jnp.zeros((n_pad_rows - n_ex - 1, d), F32)]) for j in range(8)]
    cond_bwd = jnp.concatenate(cond_rows)
    d_mod_shard = lax.dynamic_slice(d_mod_all, (0, chip * mod_w), (8 * n_pad_rows, mod_w))
    g_wmod, g_bmod, q_part = mod_bwd(cond_bwd, d_mod_shard, d_mod_all, w_mod[0],
                                     tuple(j * n_pad_rows + n_ex for j in range(8)))
    (q_all,) = exchange("gather_cctx", [q_part], "all8")
    g_cctx = cctx_grad(q_all, c_ctx.reshape(1, d))
    small_grads["c_ctx"], small_grads["b_mod"] = g_cctx.reshape(-1), g_bmod.reshape(-1)

    shard_cols = lambda t: t.reshape(t.shape[0], N_CHIPS, -1).transpose(1, 0, 2)
    big_names = ["ffn1_gate", "ffn1_up", "ffn1_down", "w_in", "w_out", "ffn2_gate", "ffn2_up", "ffn2_down", "ssm_conv_w", "cconv_w"]
    big_parts = [d_wg1, d_wu1, d_wd1, shard_cols(d_win), d_wout.reshape(N_CHIPS, -1, d), d_wg2, d_wu2, d_wd2,
                 shard_cols(d_w5), shard_cols(d_w31)]
    received = exchange("scatter_grads", big_parts, "chips", scatter=True)
    chip_sums = [sum_slots(f"chip_sum_{nm}", r) for nm, r in zip(big_names, received)]
    both = exchange("swap_sibling", chip_sums, "sibling")
    results = {}
    for nm, g_slots in zip(big_names, both):
        shape = weights[nm].shape
        two_d = lambda t: t.reshape(shape[-2], shape[-1])
        results[nm] = [r.reshape(shape) for r in adamw(f"adamw_{nm}", two_d(weights[nm]), g_slots, two_d(mom1[nm]), two_d(mom2[nm]))]
    results["w_mod"] = [r.reshape(w_mod.shape) for r in adamw("adamw_w_mod", w_mod[0], g_wmod[None], m_w_mod[0], v_w_mod[0])]
    small_names = [nm for nm in order if nm not in results]
    n_sm = sum(weights[nm].size for nm in small_names)
    n_smp = -(-n_sm // (8 * LANES)) * (8 * LANES)
    packed = lambda src: jnp.concatenate([src[nm].reshape(-1) for nm in small_names] + [jnp.zeros((n_smp - n_sm,), F32)]).reshape(-1, LANES)
    sm_out = adamw("adamw_small", packed(weights), packed(small_grads)[None], packed(mom1), packed(mom2))
    pos = 0
    for nm in small_names:
        size = weights[nm].size
        results[nm] = [r.reshape(-1)[pos:pos + size].reshape(weights[nm].shape) for r in sm_out]
        pos += size
    return (loss, grad_x, *[results[nm][0] for nm in order], *[results[nm][1] for nm in order],
            *[results[nm][2] for nm in order], *[results[nm][3] for nm in order])
```

```python
import functools
import math

import jax
import jax.numpy as jnp
from jax import lax
from jax.experimental import pallas as pl
from jax.experimental.pallas import tpu as pltpu

F32 = jnp.float32
BF16 = jnp.bfloat16
HI = lax.Precision.HIGHEST
MESH = pl.DeviceIdType.MESH

EPS = 1e-6
GRID_W = 64
HEAD_DIM = 64
N_STATE = 128
CHUNK = 128
LANES = 128
N_CHIPS = 4
ADAM_LR, ADAM_B1, ADAM_B2, ADAM_EPS, ADAM_WD, ADAM_STEP = 0.001, 0.9, 0.999, 1e-08, 0.01, 10
VMEM_CAP = 56 * 1024 * 1024


def _params(vmem_bytes=None, n_axes=1):
    kw = dict(dimension_semantics=("arbitrary",) * n_axes)
    if vmem_bytes is not None:
        kw["vmem_limit_bytes"] = int(min(VMEM_CAP, max(32 * 1024 * 1024, vmem_bytes)))
    return pltpu.CompilerParams(**kw)


def _nbytes(shape, dtype):
    return math.prod(shape) * jnp.dtype(dtype).itemsize


def _row_tile(rows, width, cap_bytes=1 << 20, mult=8):
    best = None
    for t in range(mult, rows + 1, mult):
        if rows % t == 0 and t * width * 4 <= cap_bytes:
            best = t
    return best if best is not None else rows


_MODES = {"all8": (8, (1, 2, 3, 4, 5, 6, 7), 0), "chips": (4, (2, 4, 6), 1), "sibling": (2, (1,), 0)}


def exchange(name, arrs, mode, scatter=False):
    nslot, deltas, shift = _MODES[mode]
    n_arr, n_peer = len(arrs), len(deltas)
    out_shape = tuple(jax.ShapeDtypeStruct((nslot,) + (a.shape[1:] if scatter else a.shape), a.dtype) for a in arrs)

    def body(*refs):
        ins, outs = refs[:n_arr], refs[n_arr:2 * n_arr]
        send_sems, recv_sems, local_sems = refs[2 * n_arr:]
        x, y, c = lax.axis_index("x"), lax.axis_index("y"), lax.axis_index("c")
        me = 4 * x + 2 * y + c

        def slot_of(dev):
            return (dev >> shift) & (nslot - 1)

        def src(a, slot):
            return ins[a].at[slot] if scatter else ins[a]

        def peer_of(d):
            flip = lambda v, bit: 1 - v if bit else v
            return (flip(x, (d >> 2) & 1), flip(y, (d >> 1) & 1), flip(c, d & 1))

        def remote(a, k, d, from_slot, to_slot):
            return pltpu.make_async_remote_copy(
                src_ref=src(a, from_slot), dst_ref=outs[a].at[to_slot], send_sem=send_sems.at[a, k],
                recv_sem=recv_sems.at[a, k], device_id=peer_of(d), device_id_type=MESH)

        mine = slot_of(me)
        local = [pltpu.make_async_copy(src(a, mine), outs[a].at[mine], local_sems.at[a]) for a in range(n_arr)]
        for cp in local:
            cp.start()
        sends = [remote(a, k, d, slot_of(me ^ d), mine) for k, d in enumerate(deltas) for a in range(n_arr)]
        for cp in sends:
            cp.start()
        for k, d in enumerate(deltas):
            for a in range(n_arr):
                remote(a, k, d, mine, slot_of(me ^ d)).wait_recv()
        for cp in sends:
            cp.wait_send()
        for cp in local:
            cp.wait()

    any_spec = pl.BlockSpec(memory_space=pl.ANY)
    return pl.pallas_call(
        body, name=name, out_shape=out_shape,
        in_specs=[any_spec] * n_arr, out_specs=tuple([any_spec] * n_arr),
        scratch_shapes=[pltpu.SemaphoreType.DMA((n_arr, n_peer)), pltpu.SemaphoreType.DMA((n_arr, n_peer)),
                        pltpu.SemaphoreType.DMA((n_arr,))],
    )(*arrs)


_DIMS = {"nn": (((1,), (0,)), ((), ())), "nt": (((1,), (1,)), ((), ())), "tn": (((0,), (0,)), ((), ()))}


def matmul(name, pairs, kind, *, a_ch=False, b_ch=False, out_ch=False, out_dtype=F32, rows=None, row_off=0, tm=512):
    a0, b0 = pairs[0]
    n_chunk = a0.shape[0] if a_ch else (b0.shape[0] if b_ch else 1)
    total_rows = a0.shape[-2]
    rows = total_rows - row_off if rows is None else rows
    tm = min(tm, rows)
    assert rows % tm == 0 and row_off % tm == 0, (name, rows, tm, row_off)
    n_rt, off = rows // tm, row_off // tm
    dims = _DIMS[kind]
    n_pair = len(pairs)

    if kind == "tn":
        grid, red_axis, n_red = (n_chunk, n_rt), 1, n_rt
        a_idx = (lambda k, i: (k, i + off, 0)) if a_ch else (lambda k, i: (i + off, 0))
        b_idx = (lambda k, i: (k, i + off, 0)) if b_ch else (lambda k, i: (i + off, 0))
        a_blk = lambda a: ((None, tm, a.shape[-1]) if a_ch else (tm, a.shape[-1]))
        b_blk = lambda b: ((None, tm, b.shape[-1]) if b_ch else (tm, b.shape[-1]))
        o2 = (a0.shape[-1], b0.shape[-1])
        out_shape = ((n_chunk,) + o2) if out_ch else o2
        out_spec = pl.BlockSpec((None,) + o2, lambda k, i: (k, 0, 0)) if out_ch else pl.BlockSpec(o2, lambda k, i: (0, 0))
        acc_shape = o2
    else:
        n_out = b0.shape[-1] if kind == "nn" else b0.shape[-2]
        b2 = b0.shape[-2:]
        if a_ch and b_ch and not out_ch:
            grid, red_axis, n_red = (n_rt, n_chunk), 1, n_chunk
            a_idx, b_idx = (lambda i, k: (k, i + off, 0)), (lambda i, k: (k, 0, 0))
            a_blk = lambda a: (None, tm, a.shape[-1])
            b_blk = lambda b: (None,) + tuple(b.shape[-2:])
            out_shape, out_spec = (rows, n_out), pl.BlockSpec((tm, n_out), lambda i, k: (i, 0))
        elif out_ch:
            assert b_ch and not a_ch
            grid, red_axis, n_red = (n_chunk, n_rt), None, 1
            a_idx, b_idx = (lambda k, i: (i + off, 0)), (lambda k, i: (k, 0, 0))
            a_blk = lambda a: (tm, a.shape[-1])
            b_blk = lambda b: (None,) + tuple(b.shape[-2:])
            out_shape, out_spec = (n_chunk, rows, n_out), pl.BlockSpec((None, tm, n_out), lambda k, i: (k, i, 0))
        else:
            assert not (a_ch or b_ch)
            grid, red_axis, n_red = (n_rt,), None, 1
            a_idx, b_idx = (lambda i: (i + off, 0)), (lambda i: (0, 0))
            a_blk = lambda a: (tm, a.shape[-1])
            b_blk = lambda b: tuple(b.shape)
            out_shape, out_spec = (rows, n_out), pl.BlockSpec((tm, n_out), lambda i: (i, 0))
        acc_shape = (tm, n_out)

    def body(*refs):
        out = refs[2 * n_pair]

        def compute():
            acc = None
            for p in range(n_pair):
                d = lax.dot_general(refs[2 * p][...].astype(BF16), refs[2 * p + 1][...].astype(BF16), dims,
                                    preferred_element_type=F32)
                acc = d if acc is None else acc + d
            return acc

        if n_red == 1:
            out[...] = compute().astype(out.dtype)
        else:
            acc_ref = refs[2 * n_pair + 1]
            r = pl.program_id(red_axis)

            @pl.when(r == 0)
            def _():
                acc_ref[...] = jnp.zeros_like(acc_ref)

            acc_ref[...] += compute()

            @pl.when(r == n_red - 1)
            def _():
                out[...] = acc_ref[...].astype(out.dtype)

    in_specs, args, vmem = [], [], 0
    for a, b in pairs:
        in_specs += [pl.BlockSpec(a_blk(a), a_idx), pl.BlockSpec(b_blk(b), b_idx)]
        args += [a, b]
        vmem += 2 * (_nbytes([s for s in a_blk(a) if s], a.dtype) + _nbytes([s for s in b_blk(b) if s], b.dtype))
    vmem += 3 * _nbytes(acc_shape, F32) + 2 * n_pair * _nbytes(acc_shape, F32)
    scratch = [pltpu.VMEM(acc_shape, F32)] if n_red > 1 else []
    return pl.pallas_call(
        body, name=name, out_shape=jax.ShapeDtypeStruct(out_shape, out_dtype), grid=grid, in_specs=in_specs,
        out_specs=out_spec, scratch_shapes=scratch, compiler_params=_params(vmem + (8 << 20), len(grid)),
    )(*args)


def row(arr, width=None, cb=0, roff=0):
    return (arr, arr.shape[-1] if width is None else width, cb, roff)


def _row_spec(desc, tm, limit=None):
    _, width, cb, roff = desc
    if limit is None:
        return pl.BlockSpec((tm, width), lambda i: (i + roff, cb))
    return pl.BlockSpec((tm, width), lambda i: (jnp.minimum(i, limit - 1) + roff, cb))


def _segmenter(tm, seq_len, n_lat):
    seg = lambda i: jnp.where(i * tm < n_lat, (i * tm) // seq_len, n_lat // seq_len)
    first = lambda i: jnp.where(i * tm < n_lat, (i * tm) % seq_len == 0, i * tm == n_lat)
    return seg, first


def rowwise(name, fn, rows, segs, params, outs, *, tm, n_tiles, seg_fn=None):
    n_r, n_s, n_p = len(rows), len(segs), len(params)

    def body(*refs):
        vals = [r[...].astype(F32) for r in refs[:n_r]] + [r[...] for r in refs[n_r:n_r + n_s + n_p]]
        res = fn(*vals)
        for o_ref, v in zip(refs[n_r + n_s + n_p:], res):
            o_ref[...] = v.astype(o_ref.dtype)

    in_specs = [_row_spec(d, tm) for d in rows]
    in_specs += [pl.BlockSpec((None, 1, s.shape[-1]), lambda i: (seg_fn(i), 0, 0)) for s in segs]
    in_specs += [pl.BlockSpec(p.shape, lambda i: (0, 0)) for p in params]
    vmem = sum(2 * tm * d[1] * 4 for d in rows) + sum(3 * tm * w * 4 for _, w, _ in outs) + sum(2 * p.size * 4 for p in params)
    res = pl.pallas_call(
        body, name=name, grid=(n_tiles,), in_specs=in_specs,
        out_shape=tuple(jax.ShapeDtypeStruct((r, w), dt) for r, w, dt in outs),
        out_specs=tuple(pl.BlockSpec((tm, w), lambda i: (i, 0)) for _, w, _ in outs),
        compiler_params=_params(2 * vmem + (8 << 20)),
    )(*[d[0] for d in rows], *segs, *params)
    return res


def rowwise_bwd(name, fn, rows, segs, params, cts, row_grads, *, tm, n_tiles, seg_fn=None, first_fn=None, adds=None):
    adds = adds or {}
    need = [k for k, v in enumerate(row_grads) if v is not None]
    n_r, n_s, n_p = len(rows), len(segs), len(params)
    n_ct = sum(len(lst) for lst in cts)
    add_keys = sorted(adds)

    def body(*refs):
        it = iter(refs)
        row_refs = [next(it) for _ in range(n_r)]
        seg_refs = [next(it) for _ in range(n_s)]
        par_refs = [next(it) for _ in range(n_p)]
        ct_refs = [[next(it) for _ in lst] for lst in cts]
        add_refs = {k: next(it) for k in add_keys}
        rg_refs = {k: next(it) for k in need}
        sg_refs = [next(it) for _ in range(n_s)]
        pg_refs = [next(it) for _ in range(n_p)]
        i = pl.program_id(0)
        rv = [r[...].astype(F32) for r in row_refs]
        sv = [r[...] for r in seg_refs]
        pv = [r[...] for r in par_refs]

        def f(*args):
            rr = list(rv)
            for j, k in enumerate(need):
                rr[k] = args[j]
            return fn(*rr, *args[len(need):])

        _, vjp = jax.vjp(f, *[rv[k] for k in need], *sv, *pv)
        ctv = []
        for lst in ct_refs:
            acc = lst[0][...].astype(F32)
            for r in lst[1:]:
                acc = acc + r[...].astype(F32)
            ctv.append(acc)
        g = vjp(tuple(ctv))
        for j, k in enumerate(need):
            gv = g[j]
            if k in adds:
                lim = adds[k][1]
                av = add_refs[k][...].astype(F32)
                gv = gv + (av if lim is None else jnp.where(i < lim, av, 0.0))
            lim = row_grads[k][2]
            if lim is None:
                rg_refs[k][...] = gv.astype(rg_refs[k].dtype)
            else:
                @pl.when(i < lim)
                def _(gv=gv, k=k):
                    rg_refs[k][...] = gv.astype(rg_refs[k].dtype)
        if n_s:
            opens = first_fn(i)
            for ref, gv in zip(sg_refs, g[len(need):len(need) + n_s]):
                @pl.when(opens)
                def _(ref=ref, gv=gv):
                    ref[...] = gv

                @pl.when(jnp.logical_not(opens))
                def _(ref=ref, gv=gv):
                    ref[...] += gv
        for ref, gv in zip(pg_refs, g[len(need) + n_s:]):
            @pl.when(i == 0)
            def _(ref=ref, gv=gv):
                ref[...] = gv

            @pl.when(i > 0)
            def _(ref=ref, gv=gv):
                ref[...] += gv

    seg_spec = lambda s: pl.BlockSpec((None, 1, s.shape[-1]), lambda i: (seg_fn(i), 0, 0))
    par_spec = lambda p: pl.BlockSpec(p.shape, lambda i: (0, 0))
    in_specs = [_row_spec(d, tm) for d in rows] + [seg_spec(s) for s in segs] + [par_spec(p) for p in params]
    args = [d[0] for d in rows] + list(segs) + list(params)
    for lst in cts:
        in_specs += [_row_spec(d, tm) for d in lst]
        args += [d[0] for d in lst]
    for k in add_keys:
        in_specs.append(_row_spec(adds[k][0], tm, adds[k][1]))
        args.append(adds[k][0][0])
    out_shape, out_specs = [], []
    for k in need:
        n_rows, dt, lim = row_grads[k]
        out_shape.append(jax.ShapeDtypeStruct((n_rows, rows[k][1]), dt))
        out_specs.append(_row_spec((None, rows[k][1], 0, 0), tm, lim))
    for s in segs:
        out_shape.append(jax.ShapeDtypeStruct(s.shape, F32))
        out_specs.append(seg_spec(s))
    for p in params:
        out_shape.append(jax.ShapeDtypeStruct(p.shape, F32))
        out_specs.append(par_spec(p))
    vmem = sum(tm * d[1] * 4 for d in rows) * 6 + n_ct * tm * max(d[1] for d in rows) * 8
    return pl.pallas_call(
        body, name=name, grid=(n_tiles,), in_specs=in_specs, out_shape=tuple(out_shape), out_specs=tuple(out_specs),
        compiler_params=_params(vmem + (8 << 20)),
    )(*args)


def _silu(v):
    return v * jax.nn.sigmoid(v)


def _rms(v, w):
    return v * lax.rsqrt(jnp.mean(v * v, axis=-1, keepdims=True) + EPS) * w


def fn_norm_mod(x, shift, scale, w):
    return (_rms(x, w) * (1.0 + scale) + shift,)


def fn_act(g, u):
    return (_silu(g) * u,)


def make_fn_resid(coef):
    def fn(x, f, gate):
        return (x + coef * gate * f,)
    return fn


def fn_silu_bias(v, b):
    return (_silu(v + b),)


def make_fn_gate_groupnorm(width):
    half = width // 2

    def fn(yf, yb, z, w):
        y = (yf + yb) * _silu(z)
        lane = lax.broadcasted_iota(jnp.int32, y.shape, 1)
        lo = lane < half
        sq = y * y
        s_lo = jnp.sum(jnp.where(lo, sq, 0.0), axis=-1, keepdims=True)
        s_hi = jnp.sum(jnp.where(lo, 0.0, sq), axis=-1, keepdims=True)
        r = jnp.where(lo, lax.rsqrt(s_lo / half + EPS), lax.rsqrt(s_hi / half + EPS))
        return (y * r * w,)
    return fn


def fn_glu(a, b):
    return (a * jax.nn.sigmoid(b),)


def fn_ln_silu(vw, vh, cb, lw, lb):
    v = jnp.concatenate([vw, vh], axis=-1) + cb
    mu = jnp.mean(v, axis=-1, keepdims=True)
    var = jnp.mean(jnp.square(v - mu), axis=-1, keepdims=True)
    return (_silu((v - mu) * lax.rsqrt(var + EPS) * lw + lb),)


def _col_tile(width):
    return width // 3 if width % (3 * LANES) == 0 else width


def mod_fwd(a_rows, w_shard, b_shard):
    n, d = a_rows.shape
    ws = w_shard.shape[1]
    tn = _col_tile(ws)

    def body(a_ref, w_ref, b_ref, o_ref):
        a = _silu(a_ref[...]).astype(BF16)
        o_ref[...] = jnp.dot(a, w_ref[...].astype(BF16), preferred_element_type=F32) + b_ref[...]

    return pl.pallas_call(
        body, name="mod_fwd", grid=(ws // tn,), out_shape=jax.ShapeDtypeStruct((n, ws), F32),
        in_specs=[pl.BlockSpec((n, d), lambda j: (0, 0)), pl.BlockSpec((d, tn), lambda j: (0, j)),
                  pl.BlockSpec((1, tn), lambda j: (0, j))],
        out_specs=pl.BlockSpec((n, tn), lambda j: (0, j)), compiler_params=_params(),
    )(a_rows, w_shard, b_shard)


def mod_bwd(a_rows, d_shard, d_full, w_shard, ctx_rows):
    n, d = a_rows.shape
    ws = w_shard.shape[1]
    tn = _col_tile(ws)
    n_ct = ws // tn

    def body(a_ref, ds_ref, df_ref, w_ref, gw_ref, gb_ref, q_ref):
        j = pl.program_id(0)
        a = _silu(a_ref[...])
        ds = ds_ref[...]
        gw_ref[...] = lax.dot_general(a, ds, _DIMS["tn"], precision=HI, preferred_element_type=F32)
        dctx = ds[ctx_rows[0]:ctx_rows[0] + 1, :]
        for r in ctx_rows[1:]:
            dctx = dctx + ds[r:r + 1, :]
        q = lax.dot_general(jnp.broadcast_to(dctx, (8, tn)), w_ref[...], _DIMS["nt"], precision=HI,
                            preferred_element_type=F32)

        @pl.when(j == 0)
        def _():
            q_ref[...] = q
            df = df_ref[...]
            acc = df[0:1, :]
            for r in range(1, n):
                acc = acc + df[r:r + 1, :]
            gb_ref[...] = acc

        @pl.when(j > 0)
        def _():
            q_ref[...] += q

    return pl.pallas_call(
        body, name="mod_bwd", grid=(n_ct,),
        out_shape=(jax.ShapeDtypeStruct((d, ws), F32), jax.ShapeDtypeStruct((1, d_full.shape[1]), F32),
                   jax.ShapeDtypeStruct((8, d), F32)),
        in_specs=[pl.BlockSpec((n, d), lambda j: (0, 0)), pl.BlockSpec((n, tn), lambda j: (0, j)),
                  pl.BlockSpec(d_full.shape, lambda j: (0, 0)), pl.BlockSpec((d, tn), lambda j: (0, j))],
        out_specs=(pl.BlockSpec((d, tn), lambda j: (0, j)), pl.BlockSpec((1, d_full.shape[1]), lambda j: (0, 0)),
                   pl.BlockSpec((8, d), lambda j: (0, 0))),
        compiler_params=_params(40 << 20),
    )(a_rows, d_shard, d_full, w_shard)


def _shifted(xs, d, tok, width):
    if d == 0:
        return xs
    n = xs.shape[0]
    sh = pltpu.roll(xs, (-d) % n, axis=0)
    return jnp.where((tok + d >= 0) & (tok + d < width), sh, 0.0)


def tapsum_roll(name, x, xcb, w, wcb, *, seq_len, n_seq, row_blk_off, width, piece, cb, ncb, pad, flip):
    n_tap = w.shape[0]
    n_piece = seq_len // piece

    def body(x_ref, w_ref, o_ref):
        wv = w_ref[...]
        tok = lax.broadcasted_iota(jnp.int32, (piece, 1), 0) % width

        def do_piece(p, carry):
            start = pl.multiple_of(p * piece, piece)
            xs = x_ref[pl.ds(start, piece), :]
            acc = jnp.zeros_like(xs)
            for k in range(n_tap):
                d = pad - k if flip else k - pad
                acc = acc + wv[k:k + 1, :] * _shifted(xs, d, tok, width)
            o_ref[pl.ds(start, piece), :] = acc
            return carry

        lax.fori_loop(0, n_piece, do_piece, 0)

    return pl.pallas_call(
        body, name=name, grid=(ncb, n_seq), out_shape=jax.ShapeDtypeStruct((n_seq * seq_len, ncb * cb), F32),
        in_specs=[pl.BlockSpec((seq_len, cb), lambda j, s: (row_blk_off + s, xcb + j)),
                  pl.BlockSpec((n_tap, cb), lambda j, s: (0, wcb + j))],
        out_specs=pl.BlockSpec((seq_len, cb), lambda j, s: (s, j)),
        compiler_params=_params(8 * seq_len * cb * 4 + (8 << 20), 2),
    )(x, w)


def tapgrad_roll(name, dy, dycb, dy_blk_off, x, xcb, x_blk_off, *, n_tap, seq_len, n_seq, width, piece, cb, ncb, pad):
    n_piece = seq_len // piece

    def body(dy_ref, x_ref, o_ref):
        @pl.when(pl.program_id(1) == 0)
        def _():
            o_ref[...] = jnp.zeros_like(o_ref)

        tok = lax.broadcasted_iota(jnp.int32, (piece, 1), 0) % width

        def do_piece(p, carry):
            start = pl.multiple_of(p * piece, piece)
            xs = x_ref[pl.ds(start, piece), :]
            dv = dy_ref[pl.ds(start, piece), :]
            for k in range(n_tap):
                o_ref[k:k + 1, :] += jnp.sum(dv * _shifted(xs, k - pad, tok, width), axis=0, keepdims=True)
            return carry

        lax.fori_loop(0, n_piece, do_piece, 0)

    return pl.pallas_call(
        body, name=name, grid=(ncb, n_seq), out_shape=jax.ShapeDtypeStruct((n_tap, ncb * cb), F32),
        in_specs=[pl.BlockSpec((seq_len, cb), lambda j, s: (dy_blk_off + s, dycb + j)),
                  pl.BlockSpec((seq_len, cb), lambda j, s: (x_blk_off + s, xcb + j))],
        out_specs=pl.BlockSpec((n_tap, cb), lambda j, s: (0, j)),
        compiler_params=_params(8 * seq_len * cb * 4 + (8 << 20), 2),
    )(dy, x)


def tapsum_rows(name, x, xcb, w, wcb, *, seq_len, n_seq, cb, ncb, pad, flip):
    n_tap = w.shape[0]
    n_row = seq_len // GRID_W
    halo = pad * GRID_W

    def body(x_ref, w_ref, o_ref, xp):
        xp[pl.ds(0, halo), :] = jnp.zeros((halo, cb), F32)
        xp[pl.ds(halo + seq_len, halo), :] = jnp.zeros((halo, cb), F32)
        xp[pl.ds(halo, seq_len), :] = x_ref[...]
        wv = w_ref[...]

        def do_row(r, carry):
            acc = jnp.zeros((GRID_W, cb), F32)
            for k in range(n_tap):
                d = pad - k if flip else k - pad
                acc = acc + wv[k:k + 1, :] * xp[pl.ds(pl.multiple_of((r + pad + d) * GRID_W, GRID_W), GRID_W), :]
            o_ref[pl.ds(pl.multiple_of(r * GRID_W, GRID_W), GRID_W), :] = acc
            return carry

        lax.fori_loop(0, n_row, do_row, 0)

    return pl.pallas_call(
        body, name=name, grid=(ncb, n_seq), out_shape=jax.ShapeDtypeStruct((n_seq * seq_len, ncb * cb), F32),
        in_specs=[pl.BlockSpec((seq_len, cb), lambda j, s: (s, xcb + j)),
                  pl.BlockSpec((n_tap, cb), lambda j, s: (0, wcb + j))],
        out_specs=pl.BlockSpec((seq_len, cb), lambda j, s: (s, j)),
        scratch_shapes=[pltpu.VMEM((seq_len + 2 * halo, cb), F32)],
        compiler_params=_params(10 * seq_len * cb * 4 + (8 << 20), 2),
    )(x, w)


def tapgrad_rows(name, dy, dycb, x, xcb, *, n_tap, seq_len, n_seq, cb, ncb, pad):
    n_row = seq_len // GRID_W
    halo = pad * GRID_W

    def body(dy_ref, x_ref, o_ref, xp):
        @pl.when(pl.program_id(1) == 0)
        def _():
            o_ref[...] = jnp.zeros_like(o_ref)

        xp[pl.ds(0, halo), :] = jnp.zeros((halo, cb), F32)
        xp[pl.ds(halo + seq_len, halo), :] = jnp.zeros((halo, cb), F32)
        xp[pl.ds(halo, seq_len), :] = x_ref[...]

        def do_row(r, carry):
            dv = dy_ref[pl.ds(pl.multiple_of(r * GRID_W, GRID_W), GRID_W), :]
            for k in range(n_tap):
                xs = xp[pl.ds(pl.multiple_of((r + k) * GRID_W, GRID_W), GRID_W), :]
                o_ref[k:k + 1, :] += jnp.sum(dv * xs, axis=0, keepdims=True)
            return carry

        lax.fori_loop(0, n_row, do_row, 0)

    return pl.pallas_call(
        body, name=name, grid=(ncb, n_seq), out_shape=jax.ShapeDtypeStruct((n_tap, ncb * cb), F32),
        in_specs=[pl.BlockSpec((seq_len, cb), lambda j, s: (s, dycb + j)),
                  pl.BlockSpec((seq_len, cb), lambda j, s: (s, xcb + j))],
        out_specs=pl.BlockSpec((n_tap, cb), lambda j, s: (0, j)),
        scratch_shapes=[pltpu.VMEM((seq_len + 2 * halo, cb), F32)],
        compiler_params=_params(10 * seq_len * cb * 4 + (8 << 20), 2),
    )(dy, x)


def _ssd_blocks(b, s, *, rev, n_ctx, n_lat, lat_blocks):
    if rev:
        return jnp.where(s < n_ctx, lat_blocks + b * n_ctx + (n_ctx - 1 - s), b * n_lat + (n_lat - 1 - (s - n_ctx)))
    return jnp.where(s < n_ctx, lat_blocks + b * n_ctx + s, b * n_lat + (s - n_ctx))


def _ssd_common(xbc, raw, dtb, alog, dsk, *, rev, ds, n_head):
    if rev:
        raw = pltpu.roll(raw, LANES - n_head, axis=1)
    pre = raw + dtb
    dt = jnp.maximum(pre, 0.0) + jnp.log1p(jnp.exp(-jnp.abs(pre)))
    sig = jax.nn.sigmoid(pre)
    a = -jnp.exp(alog)
    da = dt * a
    ri = lax.broadcasted_iota(jnp.int32, (CHUNK, CHUNK), 0)
    ci = lax.broadcasted_iota(jnp.int32, (CHUNK, CHUNK), 1)
    mask = (ci >= ri) if rev else (ci <= ri)
    tri = mask.astype(F32)
    tri_t = ((ci <= ri) if rev else (ci >= ri)).astype(F32)
    cs = jnp.dot(tri, da, precision=HI, preferred_element_type=F32)
    tot = jnp.sum(da, axis=0, keepdims=True)
    widen = (lax.broadcasted_iota(jnp.int32, (LANES, ds), 1) // HEAD_DIM
             == lax.broadcasted_iota(jnp.int32, (LANES, ds), 0)).astype(F32)
    wide = lambda v: jnp.dot(v, widen, precision=HI, preferred_element_type=F32)
    wide1 = lambda v: wide(jnp.broadcast_to(v, (8, LANES)))[0:1, :]
    cs_w, tot_w = wide(cs), wide1(tot)
    xh = xbc[:, :ds]
    dt_w = wide(dt)
    return dict(
        dt=dt, sig=sig, a=a, cs=cs, cs_t=cs.T, tot=tot, mask=mask, tri_t=tri_t,
        e_w=jnp.exp(cs_w), wt_w=jnp.exp(tot_w - cs_w), dec_w=jnp.exp(tot_w), dt_w=dt_w, dsk_w=wide1(dsk),
        xh=xh, xs_w=xh * dt_w, bm=xbc[:, ds:ds + 2 * N_STATE], cm=xbc[:, ds + 2 * N_STATE:ds + 4 * N_STATE])


def _decay(q, col):
    seg = q["cs"][:, col:col + 1] - q["cs_t"][col:col + 1, :]
    return jnp.exp(jnp.where(q["mask"], seg, -jnp.inf))


def _split_heads(v):
    lane = lax.broadcasted_iota(jnp.int32, v.shape, 1)
    return jnp.concatenate([jnp.where(lane < HEAD_DIM, v, 0.0), jnp.where(lane >= HEAD_DIM, v, 0.0)], axis=0)


def ssd_fwd(name, xbc, proj, dt_cb, dtb, alog, dsk, *, rev, n_ex, seq_len, ctx_len, ds):
    n_head, half = ds // HEAD_DIM, ds // 2
    n_ctx, n_lat = ctx_len // CHUNK, seq_len // CHUNK
    n_step = n_ctx + n_lat
    blk = functools.partial(_ssd_blocks, rev=rev, n_ctx=n_ctx, n_lat=n_lat, lat_blocks=n_ex * n_lat)
    xw = xbc.shape[1]

    def y_blk(b, s):
        sl = jnp.maximum(s, n_ctx) - n_ctx
        return b * n_lat + ((n_lat - 1 - sl) if rev else sl)

    def body(xbc_ref, dt_ref, dtb_ref, alog_ref, dsk_ref, y_ref, hs_ref, h_scr):
        @pl.when(pl.program_id(1) == 0)
        def _():
            h_scr[...] = jnp.zeros_like(h_scr)

        q = _ssd_common(xbc_ref[...], dt_ref[...], dtb_ref[...], alog_ref[...], dsk_ref[...], rev=rev, ds=ds, n_head=n_head)
        h = h_scr[...]
        hs_ref[...] = h
        for g in range(2):
            lo = g * half
            bg = q["bm"][:, g * N_STATE:(g + 1) * N_STATE].astype(BF16)
            cg = q["cm"][:, g * N_STATE:(g + 1) * N_STATE].astype(BF16)
            scores = lax.dot_general(cg, bg, _DIMS["nt"], preferred_element_type=F32)
            hg = h[:, lo:lo + half]
            off = jnp.dot(cg, hg.astype(BF16), preferred_element_type=F32)
            for j in range(half // LANES):
                c0 = (lo + j * LANES) // HEAD_DIM
                ln = slice(lo + j * LANES, lo + (j + 1) * LANES)
                p_cat = jnp.concatenate([scores * _decay(q, c0), scores * _decay(q, c0 + 1)], axis=1).astype(BF16)
                diag = jnp.dot(p_cat, _split_heads(q["xs_w"][:, ln]).astype(BF16), preferred_element_type=F32)
                y_ref[:, ln] = (diag + q["e_w"][:, ln] * off[:, j * LANES:(j + 1) * LANES]
                                + q["dsk_w"][:, ln] * q["xh"][:, ln])
            v = (q["wt_w"][:, lo:lo + half] * q["xs_w"][:, lo:lo + half]).astype(BF16)
            h_scr[:, lo:lo + half] = (q["dec_w"][:, lo:lo + half] * hg
                                      + lax.dot_general(bg, v, _DIMS["tn"], preferred_element_type=F32))

    vec = pl.BlockSpec((1, LANES), lambda b, s: (0, 0))
    return pl.pallas_call(
        body, name=name, grid=(n_ex, n_step),
        out_shape=(jax.ShapeDtypeStruct((n_ex * seq_len, ds), F32), jax.ShapeDtypeStruct((n_ex, n_step, N_STATE, ds), F32)),
        in_specs=[pl.BlockSpec((CHUNK, xw), lambda b, s: (blk(b, s), 0)),
                  pl.BlockSpec((CHUNK, LANES), lambda b, s: (blk(b, s), dt_cb)), vec, vec, vec],
        out_specs=(pl.BlockSpec((CHUNK, ds), lambda b, s: (y_blk(b, s), 0)),
                   pl.BlockSpec((None, None, N_STATE, ds), lambda b, s: (b, s, 0, 0))),
        scratch_shapes=[pltpu.VMEM((N_STATE, ds), F32)], compiler_params=_params(40 << 20, 2),
    )(xbc, proj, dtb, alog, dsk)


def ssd_bwd(name, xbc, proj, dt_cb, hs, dy, dtb, alog, dsk, *, rev, n_ex, seq_len, ctx_len, ds):
    n_head, half = ds // HEAD_DIM, ds // 2
    n_ctx, n_lat = ctx_len // CHUNK, seq_len // CHUNK
    n_step = n_ctx + n_lat
    n_tok = n_ex * (seq_len + ctx_len)
    blk0 = functools.partial(_ssd_blocks, rev=rev, n_ctx=n_ctx, n_lat=n_lat, lat_blocks=n_ex * n_lat)
    step = lambda sp: n_step - 1 - sp
    blk = lambda b, sp: blk0(b, step(sp))
    xw = xbc.shape[1]

    def dy_blk(b, sp):
        sl = jnp.maximum(step(sp), n_ctx) - n_ctx
        return b * n_lat + ((n_lat - 1 - sl) if rev else sl)

    def body(xbc_ref, dt_ref, hs_ref, dy_ref, dtb_ref, alog_ref, dsk_ref,
             dxbc_ref, ddt_ref, dalog_ref, ddtb_ref, ddsk_ref, dh_scr):
        b, sp = pl.program_id(0), pl.program_id(1)

        @pl.when(sp == 0)
        def _():
            dh_scr[...] = jnp.zeros_like(dh_scr)

        @pl.when((sp == 0) & (b == 0))
        def _():
            dalog_ref[...] = jnp.zeros_like(dalog_ref)
            ddtb_ref[...] = jnp.zeros_like(ddtb_ref)
            ddsk_ref[...] = jnp.zeros_like(ddsk_ref)

        q = _ssd_common(xbc_ref[...], dt_ref[...], dtb_ref[...], alog_ref[...], dsk_ref[...], rev=rev, ds=ds, n_head=n_head)
        h = hs_ref[...]
        d_y = jnp.where(step(sp) >= n_ctx, dy_ref[...], 0.0)
        dh_next = dh_scr[...]
        lane_row = lax.broadcasted_iota(jnp.int32, (1, LANES), 1)
        d_cs = jnp.zeros((CHUNK, LANES), F32)
        dxs_parts, de_parts, dwt_parts, ddec_parts = [], [], [], []
        for g in range(2):
            lo = g * half
            gs = slice(lo, lo + half)
            bg = q["bm"][:, g * N_STATE:(g + 1) * N_STATE].astype(BF16)
            cg = q["cm"][:, g * N_STATE:(g + 1) * N_STATE].astype(BF16)
            scores = lax.dot_general(cg, bg, _DIMS["nt"], preferred_element_type=F32)
            hg, dyg, dhn = h[:, gs], d_y[:, gs], dh_next[:, gs]
            off = jnp.dot(cg, hg.astype(BF16), preferred_element_type=F32)
            d_off = (q["e_w"][:, gs] * dyg).astype(BF16)
            de_parts.append(dyg * off)
            d_c = lax.dot_general(d_off, hg.astype(BF16), _DIMS["nt"], preferred_element_type=F32)
            dh_scr[:, gs] = (lax.dot_general(cg, d_off, _DIMS["tn"], preferred_element_type=F32)
                             + q["dec_w"][:, gs] * dhn)
            b_dh = jnp.dot(bg, dhn.astype(BF16), preferred_element_type=F32)
            v = q["wt_w"][:, gs] * q["xs_w"][:, gs]
            d_b = lax.dot_general(v.astype(BF16), dhn.astype(BF16), _DIMS["nt"], preferred_element_type=F32)
            dwt_parts.append(q["xs_w"][:, gs] * b_dh)
            ddec_parts.append(jnp.sum(hg * dhn, axis=0, keepdims=True))
            d_scores = jnp.zeros((CHUNK, CHUNK), F32)
            for j in range(half // LANES):
                c0 = (lo + j * LANES) // HEAD_DIM
                ln = slice(lo + j * LANES, lo + (j + 1) * LANES)
                l0, l1 = _decay(q, c0), _decay(q, c0 + 1)
                p0, p1 = scores * l0, scores * l1
                dy_st = _split_heads(d_y[:, ln]).astype(BF16)
                d_p = lax.dot_general(dy_st, q["xs_w"][:, ln].astype(BF16), _DIMS["nt"], preferred_element_type=F32)
                d_p0, d_p1 = d_p[:CHUNK], d_p[CHUNK:]
                d_scores = d_scores + d_p0 * l0 + d_p1 * l1
                for col, t in ((c0, d_p0 * p0), (c0 + 1, d_p1 * p1)):
                    d_cs = d_cs + jnp.sum(t - t.T, axis=1, keepdims=True) * (lane_row == col).astype(F32)
                p_st = jnp.concatenate([p0, p1], axis=0).astype(BF16)
                dxs_parts.append(lax.dot_general(p_st, dy_st, _DIMS["tn"], preferred_element_type=F32)
                                 + q["wt_w"][:, ln] * b_dh[:, j * LANES:(j + 1) * LANES])
            d_sc = d_scores.astype(BF16)
            d_c = d_c + jnp.dot(d_sc, bg, preferred_element_type=F32)
            d_b = d_b + lax.dot_general(d_sc, cg, _DIMS["tn"], preferred_element_type=F32)
            dxbc_ref[:, ds + g * N_STATE:ds + (g + 1) * N_STATE] = d_b
            dxbc_ref[:, ds + (2 + g) * N_STATE:ds + (3 + g) * N_STATE] = d_c
        d_xs = jnp.concatenate(dxs_parts, axis=1)
        narrow_m = (lax.broadcasted_iota(jnp.int32, (ds, LANES), 0) // HEAD_DIM
                    == lax.broadcasted_iota(jnp.int32, (ds, LANES), 1)).astype(F32)
        narrow = lambda v: jnp.dot(v, narrow_m, precision=HI, preferred_element_type=F32)
        narrow1 = lambda v: narrow(jnp.broadcast_to(v, (8, ds)))[0:1, :]
        e, wt, dec = jnp.exp(q["cs"]), jnp.exp(q["tot"] - q["cs"]), jnp.exp(q["tot"])
        d_wt = narrow(jnp.concatenate(dwt_parts, axis=1)) * wt
        d_cs = d_cs + narrow(jnp.concatenate(de_parts, axis=1)) * e - d_wt
        d_tot = jnp.sum(d_wt, axis=0, keepdims=True) + narrow1(jnp.concatenate(ddec_parts, axis=1)) * dec
        d_da = jnp.dot(q["tri_t"], d_cs, precision=HI, preferred_element_type=F32) + d_tot
        d_dt = d_da * q["a"] + narrow(d_xs * q["xh"])
        dxbc_ref[:, :ds] = d_xs * q["dt_w"] + q["dsk_w"] * d_y
        dalog_ref[...] += jnp.sum(d_da * q["dt"], axis=0, keepdims=True) * q["a"]
        d_raw = d_dt * q["sig"]
        ddtb_ref[...] += jnp.sum(d_raw, axis=0, keepdims=True)
        ddsk_ref[...] += narrow1(jnp.sum(d_y * q["xh"], axis=0, keepdims=True))
        ddt_ref[...] = pltpu.roll(d_raw, n_head, axis=1) if rev else d_raw

    vec = pl.BlockSpec((1, LANES), lambda b, s: (0, 0))
    vec_shape = jax.ShapeDtypeStruct((1, LANES), F32)
    return pl.pallas_call(
        body, name=name, grid=(n_ex, n_step),
        out_shape=(jax.ShapeDtypeStruct((n_tok, xw), F32), jax.ShapeDtypeStruct((n_tok, LANES), F32),
                   vec_shape, vec_shape, vec_shape),
        in_specs=[pl.BlockSpec((CHUNK, xw), lambda b, s: (blk(b, s), 0)),
                  pl.BlockSpec((CHUNK, LANES), lambda b, s: (blk(b, s), dt_cb)),
                  pl.BlockSpec((None, None, N_STATE, ds), lambda b, s: (b, step(s), 0, 0)),
                  pl.BlockSpec((CHUNK, ds), lambda b, s: (dy_blk(b, s), 0)), vec, vec, vec],
        out_specs=(pl.BlockSpec((CHUNK, xw), lambda b, s: (blk(b, s), 0)),
                   pl.BlockSpec((CHUNK, LANES), lambda b, s: (blk(b, s), 0)), vec, vec, vec),
        scratch_shapes=[pltpu.VMEM((N_STATE, ds), F32)], compiler_params=_params(48 << 20, 2),
    )(xbc, proj, hs, dy, dtb, alog, dsk)


def final_loss(x3, target, w, *, tm):
    n, d = x3.shape

    def body(x_ref, t_ref, w_ref, dx_ref, dw_ref, loss_ref):
        i = pl.program_id(0)
        t = t_ref[...]

        def per_feature(xv, wv):
            err = _rms(xv, wv) - t
            return 0.5 * jnp.sum(err * err, axis=0, keepdims=True) / d

        lv, vjp = jax.vjp(per_feature, x_ref[...], w_ref[...])
        dx, dw = vjp(jnp.ones_like(lv))
        dx_ref[...] = dx

        @pl.when(i == 0)
        def _():
            dw_ref[...] = dw
            loss_ref[...] = lv

        @pl.when(i > 0)
        def _():
            dw_ref[...] += dw
            loss_ref[...] += lv

    tile = pl.BlockSpec((tm, d), lambda i: (i, 0))
    vec = pl.BlockSpec((1, d), lambda i: (0, 0))
    return pl.pallas_call(
        body, name="final_loss", grid=(n // tm,), in_specs=[tile, tile, vec],
        out_shape=(jax.ShapeDtypeStruct((n, d), F32), jax.ShapeDtypeStruct((1, d), F32), jax.ShapeDtypeStruct((1, d), F32)),
        out_specs=(tile, vec, vec), compiler_params=_params(tm * d * 4 * 16 + (8 << 20)),
    )(x3, target, w)


def sum_slots(name, arr):
    n_slot, n_row, width = arr.shape
    tm = _row_tile(n_row, width * n_slot, mult=16)

    def body(a_ref, o_ref):
        acc = a_ref[0].astype(F32)
        for j in range(1, n_slot):
            acc = acc + a_ref[j].astype(F32)
        o_ref[...] = acc

    return pl.pallas_call(
        body, name=name, grid=(n_row // tm,), out_shape=jax.ShapeDtypeStruct((n_row, width), F32),
        in_specs=[pl.BlockSpec((n_slot, tm, width), lambda i: (0, i, 0))],
        out_specs=pl.BlockSpec((tm, width), lambda i: (i, 0)), compiler_params=_params(),
    )(arr)


def adamw(name, w, g_slots, m, v):
    n_slot, n_row, width = g_slots.shape
    tm = _row_tile(n_row, width * 2)

    def body(w_ref, g_ref, m_ref, v_ref, go_ref, d_ref, mo_ref, vo_ref):
        g = g_ref[0]
        for j in range(1, n_slot):
            g = g + g_ref[j]
        m2 = ADAM_B1 * m_ref[...] + (1.0 - ADAM_B1) * g
        v2 = ADAM_B2 * v_ref[...] + (1.0 - ADAM_B2) * jnp.square(g)
        m_hat = m2 / (1.0 - ADAM_B1 ** ADAM_STEP)
        v_hat = v2 / (1.0 - ADAM_B2 ** ADAM_STEP)
        go_ref[...] = g
        d_ref[...] = -ADAM_LR * (m_hat / (jnp.sqrt(v_hat) + ADAM_EPS) + ADAM_WD * w_ref[...])
        mo_ref[...] = m2
        vo_ref[...] = v2

    tile = pl.BlockSpec((tm, width), lambda i: (i, 0))
    shape = jax.ShapeDtypeStruct((n_row, width), F32)
    return pl.pallas_call(
        body, name=name, grid=(n_row // tm,), out_shape=(shape,) * 4,
        in_specs=[tile, pl.BlockSpec((n_slot, tm, width), lambda i: (0, i, 0)), tile, tile],
        out_specs=(tile,) * 4, compiler_params=_params(),
    )(w, g_slots, m, v)


def cctx_grad(q_all, c_ctx_row):
    d = c_ctx_row.shape[1]

    def body(q_ref, c_ref, o_ref):
        acc = q_ref[0, 0:1, :]
        for j in (2, 4, 6):
            acc = acc + q_ref[j, 0:1, :]
        _, vjp = jax.vjp(_silu, c_ref[...])
        o_ref[...] = vjp(acc)[0]

    return pl.pallas_call(
        body, name="cctx_grad", out_shape=jax.ShapeDtypeStruct((1, d), F32),
    )(q_all, c_ctx_row)


def loss_total(pack_sum, d):
    def body(p_ref, o_ref):
        o_ref[...] = jnp.sum(p_ref[:, 0:d], axis=1, keepdims=True)

    return pl.pallas_call(
        body, name="loss_total", out_shape=jax.ShapeDtypeStruct((1, 1), F32),
    )(pack_sum)


def _ffn_fwd(tag, xin, n_rows, tm, seg_fn, shift, scale, gate, norm_w, wg, wu, wd):
    d = xin.shape[1]
    n_tiles = n_rows // tm
    (h,) = rowwise(f"{tag}_norm", fn_norm_mod, [row(xin)], [shift, scale], [norm_w], [(n_rows, d, BF16)],
                   tm=tm, n_tiles=n_tiles, seg_fn=seg_fn)
    g = matmul(f"{tag}_gate", [(h, wg)], "nn", b_ch=True, out_ch=True, tm=tm)
    u = matmul(f"{tag}_up", [(h, wu)], "nn", b_ch=True, out_ch=True, tm=tm)
    n_ch, _, n_hid = g.shape
    (act,) = rowwise(f"{tag}_act", fn_act, [row(g.reshape(n_ch * n_rows, n_hid)), row(u.reshape(n_ch * n_rows, n_hid))],
                     [], [], [(n_ch * n_rows, n_hid, BF16)], tm=tm, n_tiles=n_ch * n_tiles)
    act = act.reshape(n_ch, n_rows, n_hid)
    f = matmul(f"{tag}_down", [(act, wd)], "nn", a_ch=True, b_ch=True, tm=tm)
    (xo,) = rowwise(f"{tag}_resid", make_fn_resid(0.5), [row(xin), row(f)], [gate], [], [(n_rows, d, F32)],
                    tm=tm, n_tiles=n_tiles, seg_fn=seg_fn)
    return xo, (h, g, u, act, f)


def _ffn_bwd(tag, d_xo, saved, xin, n_rows, tm, seg_fn, first_fn, shift, scale, gate, norm_w, wg, wu, wd, dx_rows, dx_limit):
    h, g, u, act, f = saved
    d = xin.shape[1]
    n_tiles = n_rows // tm
    n_ch, _, n_hid = g.shape
    d_f, d_gate = rowwise_bwd(f"{tag}_resid_bwd", make_fn_resid(0.5), [row(xin), row(f)], [gate], [], [[row(d_xo)]],
                              [None, (n_rows, BF16, None)], tm=tm, n_tiles=n_tiles, seg_fn=seg_fn, first_fn=first_fn)
    d_act = matmul(f"{tag}_down_dx", [(d_f, wd)], "nt", b_ch=True, out_ch=True, tm=tm)
    d_wd = matmul(f"{tag}_down_dw", [(act, d_f)], "tn", out_dtype=BF16, a_ch=True, out_ch=True, tm=tm)
    flat = lambda t: t.reshape(n_ch * n_rows, n_hid)
    d_g, d_u = rowwise_bwd(f"{tag}_act_bwd", fn_act, [row(flat(g)), row(flat(u))], [], [], [[row(flat(d_act))]],
                           [(n_ch * n_rows, BF16, None)] * 2, tm=tm, n_tiles=n_ch * n_tiles)
    d_g, d_u = d_g.reshape(g.shape), d_u.reshape(g.shape)
    d_h = matmul(f"{tag}_up_dx", [(d_g, wg), (d_u, wu)], "nt", a_ch=True, b_ch=True, tm=tm)
    d_wg = matmul(f"{tag}_gate_dw", [(h, d_g)], "tn", out_dtype=BF16, b_ch=True, out_ch=True, tm=tm)
    d_wu = matmul(f"{tag}_up_dw", [(h, d_u)], "tn", out_dtype=BF16, b_ch=True, out_ch=True, tm=tm)
    d_x, d_shift, d_scale, d_nw = rowwise_bwd(
        f"{tag}_norm_bwd", fn_norm_mod, [row(xin)], [shift, scale], [norm_w], [[row(d_h)]], [(dx_rows, F32, dx_limit)],
        tm=tm, n_tiles=n_tiles, seg_fn=seg_fn, first_fn=first_fn, adds={0: (row(d_xo), None)})
    return d_x, (d_shift, d_scale, d_gate), d_nw, (d_wg, d_wu, d_wd)


def kernel(x, c, ctx, c_ctx, w_mod, b_mod, norm_ffn1, ffn1_gate, ffn1_up, ffn1_down, norm_mix, w_in, ssm_conv_w, ssm_conv_b, dt_bias_fwd, dt_bias_bwd, a_log_fwd, a_log_bwd, ssm_d, ssm_norm_w, cconv_w, cconv_b, cconv_ln_w, cconv_ln_b, w_out, norm_ffn2, ffn2_gate, ffn2_up, ffn2_down, final_norm, loss_target, m_c_ctx, m_w_mod, m_b_mod, m_norm_ffn1, m_ffn1_gate, m_ffn1_up, m_ffn1_down, m_norm_mix, m_w_in, m_ssm_conv_w, m_ssm_conv_b, m_dt_bias_fwd, m_dt_bias_bwd, m_a_log_fwd, m_a_log_bwd, m_ssm_d, m_ssm_norm_w, m_cconv_w, m_cconv_b, m_cconv_ln_w, m_cconv_ln_b, m_w_out, m_norm_ffn2, m_ffn2_gate, m_ffn2_up, m_ffn2_down, m_final_norm, v_c_ctx, v_w_mod, v_b_mod, v_norm_ffn1, v_ffn1_gate, v_ffn1_up, v_ffn1_down, v_norm_mix, v_w_in, v_ssm_conv_w, v_ssm_conv_b, v_dt_bias_fwd, v_dt_bias_bwd, v_a_log_fwd, v_a_log_bwd, v_ssm_d, v_ssm_norm_w, v_cconv_w, v_cconv_b, v_cconv_ln_w, v_cconv_ln_b, v_w_out, v_norm_ffn2, v_ffn2_gate, v_ffn2_up, v_ffn2_down, v_final_norm):
    weights = dict(c_ctx=c_ctx, w_mod=w_mod, b_mod=b_mod, norm_ffn1=norm_ffn1, ffn1_gate=ffn1_gate, ffn1_up=ffn1_up, ffn1_down=ffn1_down, norm_mix=norm_mix, w_in=w_in, ssm_conv_w=ssm_conv_w, ssm_conv_b=ssm_conv_b, dt_bias_fwd=dt_bias_fwd, dt_bias_bwd=dt_bias_bwd, a_log_fwd=a_log_fwd, a_log_bwd=a_log_bwd, ssm_d=ssm_d, ssm_norm_w=ssm_norm_w, cconv_w=cconv_w, cconv_b=cconv_b, cconv_ln_w=cconv_ln_w, cconv_ln_b=cconv_ln_b, w_out=w_out, norm_ffn2=norm_ffn2, ffn2_gate=ffn2_gate, ffn2_up=ffn2_up, ffn2_down=ffn2_down, final_norm=final_norm)
    mom1 = dict(c_ctx=m_c_ctx, w_mod=m_w_mod, b_mod=m_b_mod, norm_ffn1=m_norm_ffn1, ffn1_gate=m_ffn1_gate, ffn1_up=m_ffn1_up, ffn1_down=m_ffn1_down, norm_mix=m_norm_mix, w_in=m_w_in, ssm_conv_w=m_ssm_conv_w, ssm_conv_b=m_ssm_conv_b, dt_bias_fwd=m_dt_bias_fwd, dt_bias_bwd=m_dt_bias_bwd, a_log_fwd=m_a_log_fwd, a_log_bwd=m_a_log_bwd, ssm_d=m_ssm_d, ssm_norm_w=m_ssm_norm_w, cconv_w=m_cconv_w, cconv_b=m_cconv_b, cconv_ln_w=m_cconv_ln_w, cconv_ln_b=m_cconv_ln_b, w_out=m_w_out, norm_ffn2=m_norm_ffn2, ffn2_gate=m_ffn2_gate, ffn2_up=m_ffn2_up, ffn2_down=m_ffn2_down, final_norm=m_final_norm)
    mom2 = dict(c_ctx=v_c_ctx, w_mod=v_w_mod, b_mod=v_b_mod, norm_ffn1=v_norm_ffn1, ffn1_gate=v_ffn1_gate, ffn1_up=v_ffn1_up, ffn1_down=v_ffn1_down, norm_mix=v_norm_mix, w_in=v_w_in, ssm_conv_w=v_ssm_conv_w, ssm_conv_b=v_ssm_conv_b, dt_bias_fwd=v_dt_bias_fwd, dt_bias_bwd=v_dt_bias_bwd, a_log_fwd=v_a_log_fwd, a_log_bwd=v_a_log_bwd, ssm_d=v_ssm_d, ssm_norm_w=v_ssm_norm_w, cconv_w=v_cconv_w, cconv_b=v_cconv_b, cconv_ln_w=v_cconv_ln_w, cconv_ln_b=v_cconv_ln_b, w_out=v_w_out, norm_ffn2=v_norm_ffn2, ffn2_gate=v_ffn2_gate, ffn2_up=v_ffn2_up, ffn2_down=v_ffn2_down, final_norm=v_final_norm)
    order = list(weights)

    n_ex, seq_len, d = x.shape
    ctx_len = ctx.shape[1]
    ds = d
    n_head = ds // HEAD_DIM
    xw = ds + 4 * N_STATE
    n_lat, n_ctx_rows = n_ex * seq_len, n_ex * ctx_len
    n_tok = n_lat + n_ctx_rows
    tm = math.gcd(math.gcd(512, seq_len), n_ctx_rows)
    seg_all, first_all = _segmenter(tm, seq_len, n_lat)
    lat_tiles = n_lat // tm

    xi, yi, ci = lax.axis_index("x"), lax.axis_index("y"), lax.axis_index("c")
    me, chip = 4 * xi + 2 * yi + ci, 2 * xi + yi

    (c_all,) = exchange("gather_c", [c], "all8")
    n_all = 8 * n_ex
    n_cond = -(-(n_all + 1) // 8) * 8
    cond = jnp.concatenate([c_all.reshape(n_all, d), c_ctx[None, :], jnp.zeros((n_cond - n_all - 1, d), F32)])
    mod_w = w_mod.shape[2]
    b_shard = lax.dynamic_slice(b_mod, (0, chip * mod_w), (1, mod_w))
    (mod_g,) = exchange("gather_mod", [mod_fwd(cond, w_mod[0], b_shard)], "chips")
    mod_full = mod_g.transpose(1, 0, 2).reshape(n_cond, N_CHIPS * mod_w)
    mod_mine = lax.dynamic_slice(mod_full, (me * n_ex, 0), (n_ex, 9 * d)).reshape(n_ex, 9, d)
    mod_ctx = mod_full[n_all].reshape(9, d)
    tabs = [jnp.concatenate([mod_mine[:, j], mod_ctx[j][None]])[:, None, :] for j in range(9)]
    lat = lambda t: t[:n_ex]

    bf = lambda w: w[0].astype(BF16)
    wg1, wu1, wd1, win_g, wout_g, wg2, wu2, wd2, w5_g, w31_g = exchange(
        "gather_weights", [bf(ffn1_gate), bf(ffn1_up), bf(ffn1_down), bf(w_in), bf(w_out), bf(ffn2_gate), bf(ffn2_up),
                           bf(ffn2_down), ssm_conv_w[0], cconv_w[0]], "chips")
    unshard_cols = lambda t: t.transpose(1, 0, 2).reshape(t.shape[1], N_CHIPS * t.shape[2])
    win = unshard_cols(win_g)
    o_x, o_dt, o_glu = ds, ds + xw, ds + xw + 2 * n_head
    w_z, w_xbc, w_dt = win[:, :ds], win[:, o_x:o_dt], win[:, o_dt:o_glu]
    w_ga, w_gb = win[:, o_glu:o_glu + d], win[:, o_glu + d:]
    w_dtp = jnp.concatenate([w_dt, jnp.zeros((d, LANES - 2 * n_head), BF16)], axis=1)
    w_cat = jnp.concatenate([w_z, w_ga, w_gb, w_xbc, w_dtp], axis=1)
    cbw = d // 2
    xbc_cb, dt_cb = 3 * d // cbw, (3 * d + xw) // LANES
    wout = wout_g.reshape(2 * d, d)
    wo_y, wo_u = wout[:ds], wout[ds:]
    w5, w31 = unshard_cols(w5_g), unshard_cols(w31_g)
    pad_vec = lambda v: jnp.concatenate([v.reshape(1, -1), jnp.zeros((1, LANES - v.size), F32)], axis=1)
    dtb_f, dtb_b, alog_f, alog_b = map(pad_vec, (dt_bias_fwd, dt_bias_bwd, a_log_fwd, a_log_bwd))
    dsk_f, dsk_b = pad_vec(ssm_d), jnp.zeros((1, LANES), F32)

    xt = jnp.concatenate([x.reshape(n_lat, d), ctx.reshape(n_ctx_rows, d)])
    x1, saved1 = _ffn_fwd("ffn1", xt, n_tok, tm, seg_all, tabs[0], tabs[1], tabs[2], norm_ffn1, wg1, wu1, wd1)
    (h2,) = rowwise("mix_norm", fn_norm_mod, [row(x1)], [tabs[3], tabs[4]], [norm_mix], [(n_tok, d, BF16)],
                    tm=tm, n_tiles=n_tok // tm, seg_fn=seg_all)
    proj = matmul("mix_proj", [(h2, w_cat)], "nn", tm=min(tm, 256))
    c5 = lambda name, src, cb0, flip, seq, off: tapsum_roll(
        name, src, cb0, w5, 0, seq_len=seq, n_seq=n_ex, row_blk_off=off, width=seq, piece=seq, cb=cbw, ncb=xw // cbw,
        pad=w5.shape[0] // 2, flip=flip)
    craw = jnp.concatenate([c5("xbc_conv_lat", proj, xbc_cb, False, seq_len, 0),
                            c5("xbc_conv_ctx", proj, xbc_cb, False, ctx_len, n_lat // ctx_len)])
    (xbc,) = rowwise("xbc_silu", fn_silu_bias, [row(craw)], [], [ssm_conv_b], [(n_tok, xw, F32)], tm=tm, n_tiles=n_tok // tm)
    ssd = dict(n_ex=n_ex, seq_len=seq_len, ctx_len=ctx_len, ds=ds)
    y_f, hs_f = ssd_fwd("ssd_fwd_f", xbc, proj, dt_cb, dtb_f, alog_f, dsk_f, rev=False, **ssd)
    y_b, hs_b = ssd_fwd("ssd_fwd_b", xbc, proj, dt_cb, dtb_b, alog_b, dsk_b, rev=True, **ssd)
    fn_gate = make_fn_gate_groupnorm(ds)
    (yn,) = rowwise("ssd_gate", fn_gate, [row(y_f), row(y_b), row(proj, d, 0)], [], [ssm_norm_w], [(n_lat, ds, BF16)],
                    tm=tm, n_tiles=lat_tiles)
    (u0,) = rowwise("glu", fn_glu, [row(proj, d, 1), row(proj, d, 2)], [], [], [(n_lat, d, F32)], tm=tm, n_tiles=lat_tiles)
    cb31 = max(LANES, d // 4)
    ncb31 = (d // 2) // cb31
    pad31 = w31.shape[0] // 2
    piece31 = min(seq_len, 4 * GRID_W)
    v_w = tapsum_roll("cconv_cols", u0, 0, w31, 0, seq_len=seq_len, n_seq=n_ex, row_blk_off=0, width=GRID_W,
                      piece=piece31, cb=cb31, ncb=ncb31, pad=pad31, flip=False)
    v_h = tapsum_rows("cconv_rows", u0, ncb31, w31, ncb31, seq_len=seq_len, n_seq=n_ex, cb=cb31, ncb=ncb31, pad=pad31, flip=False)
    (un,) = rowwise("cconv_ln", fn_ln_silu, [row(v_w), row(v_h)], [], [cconv_b, cconv_ln_w, cconv_ln_b], [(n_lat, d, BF16)],
                    tm=tm, n_tiles=lat_tiles)
    mix = matmul("mix_out", [(yn, wo_y), (un, wo_u)], "nn", tm=tm)
    seg_lat, first_lat = _segmenter(tm, seq_len, n_lat)
    (x2,) = rowwise("mix_resid", make_fn_resid(1.0), [row(x1), row(mix)], [lat(tabs[5])], [], [(n_lat, d, F32)],
                    tm=tm, n_tiles=lat_tiles, seg_fn=seg_lat)
    x3, saved2 = _ffn_fwd("ffn2", x2, n_lat, tm, seg_lat, lat(tabs[6]), lat(tabs[7]), lat(tabs[8]), norm_ffn2, wg2, wu2, wd2)
    d_x3, d_final, loss_vec = final_loss(x3, loss_target.reshape(n_lat, d), final_norm.reshape(1, d), tm=tm)

    d_x2, (d_s6, d_s7, d_g8), d_nffn2, (d_wg2, d_wu2, d_wd2) = _ffn_bwd(
        "ffn2", d_x3, saved2, x2, n_lat, tm, seg_lat, first_lat, lat(tabs[6]), lat(tabs[7]), lat(tabs[8]), norm_ffn2,
        wg2, wu2, wd2, n_lat, None)
    d_mix, d_g5 = rowwise_bwd("mix_resid_bwd", make_fn_resid(1.0), [row(x1), row(mix)], [lat(tabs[5])], [], [[row(d_x2)]],
                              [None, (n_lat, BF16, None)], tm=tm, n_tiles=lat_tiles, seg_fn=seg_lat, first_fn=first_lat)
    d_yn = matmul("mix_out_dy", [(d_mix, wo_y)], "nt", tm=tm)
    d_un = matmul("mix_out_du", [(d_mix, wo_u)], "nt", tm=tm)
    d_wout = jnp.concatenate([matmul("mix_out_dwy", [(yn, d_mix)], "tn", out_dtype=BF16, tm=tm), matmul("mix_out_dwu", [(un, d_mix)], "tn", out_dtype=BF16, tm=tm)])
    d_vw, d_vh, d_cb, d_lnw, d_lnb = rowwise_bwd(
        "cconv_ln_bwd", fn_ln_silu, [row(v_w), row(v_h)], [], [cconv_b, cconv_ln_w, cconv_ln_b], [[row(d_un)]],
        [(n_lat, F32, None)] * 2, tm=tm, n_tiles=lat_tiles)
    d_u0w = tapsum_roll("cconv_cols_dx", d_vw, 0, w31, 0, seq_len=seq_len, n_seq=n_ex, row_blk_off=0, width=GRID_W,
                        piece=piece31, cb=cb31, ncb=ncb31, pad=pad31, flip=True)
    d_u0h = tapsum_rows("cconv_rows_dx", d_vh, 0, w31, ncb31, seq_len=seq_len, n_seq=n_ex, cb=cb31, ncb=ncb31, pad=pad31, flip=True)
    d_w31 = jnp.concatenate([
        tapgrad_roll("cconv_cols_dw", d_vw, 0, 0, u0, 0, 0, n_tap=w31.shape[0], seq_len=seq_len, n_seq=n_ex, width=GRID_W,
                     piece=piece31, cb=cb31, ncb=ncb31, pad=pad31),
        tapgrad_rows("cconv_rows_dw", d_vh, 0, u0, ncb31, n_tap=w31.shape[0], seq_len=seq_len, n_seq=n_ex, cb=cb31,
                     ncb=ncb31, pad=pad31)], axis=1)
    d_u0 = jnp.concatenate([d_u0w, d_u0h], axis=1)
    d_ga, d_gb = rowwise_bwd("glu_bwd", fn_glu, [row(proj, d, 1), row(proj, d, 2)], [], [], [[row(d_u0)]],
                             [(n_lat, BF16, None)] * 2, tm=tm, n_tiles=lat_tiles)
    d_ysum, d_z, d_ssmnw = rowwise_bwd(
        "ssd_gate_bwd", fn_gate, [row(y_f), row(y_b), row(proj, d, 0)], [], [ssm_norm_w], [[row(d_yn)]],
        [(n_lat, F32, None), None, (n_lat, BF16, None)], tm=tm, n_tiles=lat_tiles)
    dxbc_f, ddt_f, dalog_f, ddtb_f, ddsk = ssd_bwd("ssd_bwd_f", xbc, proj, dt_cb, hs_f, d_ysum, dtb_f, alog_f, dsk_f, rev=False, **ssd)
    dxbc_b, ddt_b, dalog_b, ddtb_b, _ = ssd_bwd("ssd_bwd_b", xbc, proj, dt_cb, hs_b, d_ysum, dtb_b, alog_b, dsk_b, rev=True, **ssd)
    d_craw, d_conv_b = rowwise_bwd("xbc_silu_bwd", fn_silu_bias, [row(craw)], [], [ssm_conv_b], [[row(dxbc_f), row(dxbc_b)]],
                                   [(n_tok, F32, None)], tm=tm, n_tiles=n_tok // tm)
    d_pxbc = jnp.concatenate([c5("xbc_conv_lat_dx", d_craw, 0, True, seq_len, 0),
                              c5("xbc_conv_ctx_dx", d_craw, 0, True, ctx_len, n_lat // ctx_len)])
    g5 = lambda name, seq, off: tapgrad_roll(name, d_craw, 0, off, proj, xbc_cb, off, n_tap=w5.shape[0], seq_len=seq,
                                             n_seq=n_ex, width=seq, piece=seq, cb=cbw, ncb=xw // cbw, pad=w5.shape[0] // 2)
    d_w5 = g5("xbc_conv_lat_dw", seq_len, 0) + g5("xbc_conv_ctx_dw", ctx_len, n_lat // ctx_len)
    lat_pairs = [(d_z, w_z), (d_ga, w_ga), (d_gb, w_gb), (d_pxbc, w_xbc), (ddt_f, w_dtp), (ddt_b, w_dtp)]
    d_h2 = jnp.concatenate([matmul("mix_proj_dx_lat", lat_pairs, "nt", rows=n_lat, tm=min(tm, 256)),
                            matmul("mix_proj_dx_ctx", lat_pairs[3:], "nt", rows=n_ctx_rows, row_off=n_lat, tm=min(tm, 256))])
    d_wz = matmul("mix_proj_dwz", [(h2, d_z)], "tn", out_dtype=BF16, rows=n_lat, tm=tm)
    d_wga = matmul("mix_proj_dwa", [(h2, d_ga)], "tn", out_dtype=BF16, rows=n_lat, tm=tm)
    d_wgb = matmul("mix_proj_dwb", [(h2, d_gb)], "tn", out_dtype=BF16, rows=n_lat, tm=tm)
    d_wxbc = matmul("mix_proj_dwx", [(h2, d_pxbc)], "tn", out_dtype=BF16, tm=tm)
    d_wdt = matmul("mix_proj_dwt", [(h2, ddt_f), (h2, ddt_b)], "tn", out_dtype=BF16, tm=tm)
    d_win = jnp.concatenate([d_wz, d_wxbc, d_wdt[:, :2 * n_head], d_wga, d_wgb], axis=1)
    d_x1, d_s3, d_s4, d_nmix = rowwise_bwd(
        "mix_norm_bwd", fn_norm_mod, [row(x1)], [tabs[3], tabs[4]], [norm_mix], [[row(d_h2)]], [(n_tok, F32, None)],
        tm=tm, n_tiles=n_tok // tm, seg_fn=seg_all, first_fn=first_all, adds={0: (row(d_x2), lat_tiles)})
    d_xt, (d_s0, d_s1, d_g2), d_nffn1, (d_wg1, d_wu1, d_wd1) = _ffn_bwd(
        "ffn1", d_x1, saved1, xt, n_tok, tm, seg_all, first_all, tabs[0], tabs[1], tabs[2], norm_ffn1, wg1, wu1, wd1,
        n_lat, lat_tiles)
    grad_x = d_xt.reshape(n_ex, seq_len, d)

    with_ctx0 = lambda t: jnp.concatenate([t, jnp.zeros((1, 1, d), F32)])
    d_tabs = [d_s0, d_s1, d_g2, d_s3, d_s4, with_ctx0(d_g5), with_ctx0(d_s6), with_ctx0(d_s7), with_ctx0(d_g8)]
    d_mod_rows = jnp.concatenate([t[:, 0, :] for t in d_tabs], axis=1)
    n_pad_rows = -(-(n_ex + 1) // 8) * 8
    d_mod_rows = jnp.concatenate([d_mod_rows, jnp.zeros((n_pad_rows - n_ex - 1, 9 * d), F32)])
    small = [("loss", loss_vec), ("norm_ffn1", d_nffn1), ("norm_mix", d_nmix), ("ssm_conv_b", d_conv_b),
             ("dt_bias_fwd", ddtb_f[:, :n_head]), ("dt_bias_bwd", ddtb_b[:, :n_head]), ("a_log_fwd", dalog_f[:, :n_head]),
             ("a_log_bwd", dalog_b[:, :n_head]), ("ssm_d", ddsk[:, :n_head]), ("ssm_norm_w", d_ssmnw), ("cconv_b", d_cb),
             ("cconv_ln_w", d_lnw), ("cconv_ln_b", d_lnb), ("norm_ffn2", d_nffn2), ("final_norm", d_final)]
    n_small = sum(v.size for _, v in small)
    n_pack = -(-n_small // (8 * LANES)) * (8 * LANES)
    pack = jnp.concatenate([v.reshape(-1) for _, v in small] + [jnp.zeros((n_pack - n_small,), F32)]).reshape(-1, LANES)
    pack_all, d_mod_all = exchange("gather_small", [pack, d_mod_rows], "all8")
    pack_sum = sum_slots("small_sum", pack_all)
    loss = loss_total(pack_sum.reshape(1, n_pack), d).reshape(())
    flat_sum = pack_sum.reshape(-1)
    small_grads, pos = {}, 0
    for nm, v in small:
        small_grads[nm] = flat_sum[pos:pos + v.size]
        pos += v.size
    d_mod_all = d_mod_all.reshape(8 * n_pad_rows, 9 * d)
    cond_rows = [jnp.concatenate([cond[j * n_ex:(j + 1) * n_ex], c_ctx[None, :],
                                  jnp.zeros((n_pad_rows - n_ex - 1, d), F32)]) for j in range(8)]
    cond_bwd = jnp.concatenate(cond_rows)
    d_mod_shard = lax.dynamic_slice(d_mod_all, (0, chip * mod_w), (8 * n_pad_rows, mod_w))
    g_wmod, g_bmod, q_part = mod_bwd(cond_bwd, d_mod_shard, d_mod_all, w_mod[0],
                                     tuple(j * n_pad_rows + n_ex for j in range(8)))
    (q_all,) = exchange("gather_cctx", [q_part], "all8")
    g_cctx = cctx_grad(q_all, c_ctx.reshape(1, d))
    small_grads["c_ctx"], small_grads["b_mod"] = g_cctx.reshape(-1), g_bmod.reshape(-1)

    shard_cols = lambda t: t.reshape(t.shape[0], N_CHIPS, -1).transpose(1, 0, 2)

    def device_pieces(t):
        t = jnp.pad(t, ((0, 0), (0, t.shape[1] % 2), (0, 0)))
        return t.reshape(2 * N_CHIPS, t.shape[1] // 2, t.shape[2]).astype(BF16)

    big_names = ["ffn1_gate", "ffn1_up", "ffn1_down", "w_in", "w_out", "ffn2_gate", "ffn2_up", "ffn2_down", "ssm_conv_w", "cconv_w"]
    big_parts = [d_wg1, d_wu1, d_wd1, shard_cols(d_win), d_wout.reshape(N_CHIPS, -1, d), d_wg2, d_wu2, d_wd2,
                 shard_cols(d_w5), shard_cols(d_w31)]
    received = exchange("scatter_grads", [device_pieces(p) for p in big_parts], "all8", scatter=True)
    half_sums = [sum_slots(f"sum_{nm}", r) for nm, r in zip(big_names, received)]
    both = exchange("swap_sibling", half_sums, "sibling")
    results = {}
    for nm, halves in zip(big_names, both):
        shape = weights[nm].shape
        two_d = lambda t: t.reshape(shape[-2], shape[-1])
        g_full = halves.reshape(1, -1, shape[-1])[:, :shape[-2]]
        results[nm] = [r.reshape(shape) for r in adamw(f"adamw_{nm}", two_d(weights[nm]), g_full, two_d(mom1[nm]), two_d(mom2[nm]))]
    results["w_mod"] = [r.reshape(w_mod.shape) for r in adamw("adamw_w_mod", w_mod[0], g_wmod[None], m_w_mod[0], v_w_mod[0])]
    small_names = [nm for nm in order if nm not in results]
    n_sm = sum(weights[nm].size for nm in small_names)
    n_smp = -(-n_sm // (8 * LANES)) * (8 * LANES)
    packed = lambda src: jnp.concatenate([src[nm].reshape(-1) for nm in small_names] + [jnp.zeros((n_smp - n_sm,), F32)]).reshape(-1, LANES)
    sm_out = adamw("adamw_small", packed(weights), packed(small_grads)[None], packed(mom1), packed(mom2))
    pos = 0
    for nm in small_names:
        size = weights[nm].size
        results[nm] = [r.reshape(-1)[pos:pos + size].reshape(weights[nm].shape) for r in sm_out]
        pos += size
    return (loss, grad_x, *[results[nm][0] for nm in order], *[results[nm][1] for nm in order],
            *[results[nm][2] for nm in order], *[results[nm][3] for nm in order])
```

```python
import functools
import math

import jax
import jax.numpy as jnp
from jax import lax
from jax.experimental import pallas as pl
from jax.experimental.pallas import tpu as pltpu

F32 = jnp.float32
BF16 = jnp.bfloat16
HI = lax.Precision.HIGHEST
MESH = pl.DeviceIdType.MESH

EPS = 1e-6
GRID_W = 64
HEAD_DIM = 64
N_STATE = 128
CHUNK = 128
LANES = 128
N_CHIPS = 4
ADAM_LR, ADAM_B1, ADAM_B2, ADAM_EPS, ADAM_WD, ADAM_STEP = 0.001, 0.9, 0.999, 1e-08, 0.01, 10
VMEM_CAP = 56 * 1024 * 1024


def _params(vmem_bytes=None, n_axes=1):
    kw = dict(dimension_semantics=("arbitrary",) * n_axes)
    if vmem_bytes is not None:
        kw["vmem_limit_bytes"] = int(min(VMEM_CAP, max(32 * 1024 * 1024, vmem_bytes)))
    return pltpu.CompilerParams(**kw)


def _nbytes(shape, dtype):
    return math.prod(shape) * jnp.dtype(dtype).itemsize


def _row_tile(rows, width, cap_bytes=1 << 20, mult=8):
    best = None
    for t in range(mult, rows + 1, mult):
        if rows % t == 0 and t * width * 4 <= cap_bytes:
            best = t
    return best if best is not None else rows


_MODES = {"all8": (8, (1, 2, 3, 4, 5, 6, 7), 0), "chips": (4, (2, 4, 6), 1), "sibling": (2, (1,), 0)}


class Rider:
    def __init__(self, arrs, mode, scatter=False):
        self.arrs, self.scatter = list(arrs), scatter
        self.nslot, self.deltas, self.shift = _MODES[mode]
        self.n = len(self.arrs)
        self.out_shape = [jax.ShapeDtypeStruct((self.nslot,) + (a.shape[1:] if scatter else a.shape), a.dtype)
                          for a in self.arrs]
        any_spec = pl.BlockSpec(memory_space=pl.ANY)
        self.in_specs = [any_spec] * self.n
        self.out_specs = [any_spec] * self.n
        n_peer = len(self.deltas)
        self.scratch = [pltpu.SemaphoreType.DMA((self.n, n_peer)), pltpu.SemaphoreType.DMA((self.n, n_peer)),
                        pltpu.SemaphoreType.DMA((self.n,))]

    def _copies(self, ins, outs, sems):
        send_sems, recv_sems, local_sems = sems
        x, y, c = lax.axis_index("x"), lax.axis_index("y"), lax.axis_index("c")
        me = 4 * x + 2 * y + c
        slot_of = lambda dev: (dev >> self.shift) & (self.nslot - 1)
        src = lambda a, slot: ins[a].at[slot] if self.scatter else ins[a]
        flip = lambda v, bit: 1 - v if bit else v

        def remote(a, k, d, from_slot, to_slot):
            return pltpu.make_async_remote_copy(
                src_ref=src(a, from_slot), dst_ref=outs[a].at[to_slot], send_sem=send_sems.at[a, k],
                recv_sem=recv_sems.at[a, k], device_id=(flip(x, (d >> 2) & 1), flip(y, (d >> 1) & 1), flip(c, d & 1)),
                device_id_type=MESH)

        mine = slot_of(me)
        local = [pltpu.make_async_copy(src(a, mine), outs[a].at[mine], local_sems.at[a]) for a in range(self.n)]
        sends = [remote(a, k, d, slot_of(me ^ d), mine) for k, d in enumerate(self.deltas) for a in range(self.n)]
        recvs = [remote(a, k, d, mine, slot_of(me ^ d)) for k, d in enumerate(self.deltas) for a in range(self.n)]
        return local, sends, recvs

    def start(self, ins, outs, sems):
        local, sends, _ = self._copies(ins, outs, sems)
        for cp in local + sends:
            cp.start()

    def wait(self, ins, outs, sems):
        local, sends, recvs = self._copies(ins, outs, sems)
        for cp in recvs:
            cp.wait_recv()
        for cp in sends:
            cp.wait_send()
        for cp in local:
            cp.wait()


class _Hosted:
    def __init__(self, rider, n_in, n_out, n_scratch, grid):
        self.rider, self.n_in, self.n_out, self.n_scratch, self.grid = rider, n_in, n_out, n_scratch, grid
        self.n = rider.n if rider else 0

    def split(self, refs):
        a, b = self.n_in, self.n_in + self.n
        c, e = b + self.n_out, b + self.n_out + self.n
        self._r = (refs[a:b], refs[c:e], refs[e + self.n_scratch:])
        if self.rider:
            ids = [pl.program_id(ax) for ax in range(len(self.grid))]
            first = functools.reduce(jnp.logical_and, [i == 0 for i in ids]) if ids else True
            pl.when(first)(lambda: self.rider.start(*self._r))
        return refs[:a], refs[b:c], refs[e:e + self.n_scratch]

    def finish(self):
        if self.rider:
            ids = [pl.program_id(ax) for ax in range(len(self.grid))]
            last = functools.reduce(jnp.logical_and, [i == n - 1 for i, n in zip(ids, self.grid)]) if ids else True
            pl.when(last)(lambda: self.rider.wait(*self._r))

    def call_args(self, in_specs, out_shape, out_specs, scratch, args):
        r = self.rider
        if not r:
            return list(in_specs), tuple(out_shape), tuple(out_specs), list(scratch), list(args)
        return (list(in_specs) + r.in_specs, tuple(out_shape) + tuple(r.out_shape), tuple(out_specs) + tuple(r.out_specs),
                list(scratch) + r.scratch, list(args) + r.arrs)

    def results(self, res, unwrap=True):
        res = list(res) if isinstance(res, (tuple, list)) else [res]
        host = res[:self.n_out]
        host = host[0] if (self.n_out == 1 and unwrap) else tuple(host)
        return (host, res[self.n_out:]) if self.rider else host


def exchange(name, arrs, mode, scatter=False):
    rider = Rider(arrs, mode, scatter)

    def body(*refs):
        ins, outs, sems = refs[:rider.n], refs[rider.n:2 * rider.n], refs[2 * rider.n:]
        rider.start(ins, outs, sems)
        rider.wait(ins, outs, sems)

    return pl.pallas_call(
        body, name=name, out_shape=tuple(rider.out_shape), in_specs=rider.in_specs, out_specs=tuple(rider.out_specs),
        scratch_shapes=rider.scratch,
    )(*arrs)


_DIMS = {"nn": (((1,), (0,)), ((), ())), "nt": (((1,), (1,)), ((), ())), "tn": (((0,), (0,)), ((), ()))}


def matmul(name, pairs, kind, *, a_ch=False, b_ch=False, out_ch=False, out_dtype=F32, rows=None, row_off=0, tm=512,
           rider=None):
    a0, b0 = pairs[0]
    n_chunk = a0.shape[0] if a_ch else (b0.shape[0] if b_ch else 1)
    total_rows = a0.shape[-2]
    rows = total_rows - row_off if rows is None else rows
    tm = min(tm, rows)
    assert rows % tm == 0 and row_off % tm == 0, (name, rows, tm, row_off)
    n_rt, off = rows // tm, row_off // tm
    dims = _DIMS[kind]
    n_pair = len(pairs)

    if kind == "tn":
        grid, red_axis, n_red = (n_chunk, n_rt), 1, n_rt
        a_idx = (lambda k, i: (k, i + off, 0)) if a_ch else (lambda k, i: (i + off, 0))
        b_idx = (lambda k, i: (k, i + off, 0)) if b_ch else (lambda k, i: (i + off, 0))
        a_blk = lambda a: ((None, tm, a.shape[-1]) if a_ch else (tm, a.shape[-1]))
        b_blk = lambda b: ((None, tm, b.shape[-1]) if b_ch else (tm, b.shape[-1]))
        o2 = (a0.shape[-1], b0.shape[-1])
        out_shape = ((n_chunk,) + o2) if out_ch else o2
        out_spec = pl.BlockSpec((None,) + o2, lambda k, i: (k, 0, 0)) if out_ch else pl.BlockSpec(o2, lambda k, i: (0, 0))
        acc_shape = o2
    else:
        n_out = b0.shape[-1] if kind == "nn" else b0.shape[-2]
        b2 = b0.shape[-2:]
        if a_ch and b_ch and not out_ch:
            grid, red_axis, n_red = (n_rt, n_chunk), 1, n_chunk
            a_idx, b_idx = (lambda i, k: (k, i + off, 0)), (lambda i, k: (k, 0, 0))
            a_blk = lambda a: (None, tm, a.shape[-1])
            b_blk = lambda b: (None,) + tuple(b.shape[-2:])
            out_shape, out_spec = (rows, n_out), pl.BlockSpec((tm, n_out), lambda i, k: (i, 0))
        elif out_ch:
            assert b_ch and not a_ch
            grid, red_axis, n_red = (n_chunk, n_rt), None, 1
            a_idx, b_idx = (lambda k, i: (i + off, 0)), (lambda k, i: (k, 0, 0))
            a_blk = lambda a: (tm, a.shape[-1])
            b_blk = lambda b: (None,) + tuple(b.shape[-2:])
            out_shape, out_spec = (n_chunk, rows, n_out), pl.BlockSpec((None, tm, n_out), lambda k, i: (k, i, 0))
        else:
            assert not (a_ch or b_ch)
            grid, red_axis, n_red = (n_rt,), None, 1
            a_idx, b_idx = (lambda i: (i + off, 0)), (lambda i: (0, 0))
            a_blk = lambda a: (tm, a.shape[-1])
            b_blk = lambda b: tuple(b.shape)
            out_shape, out_spec = (rows, n_out), pl.BlockSpec((tm, n_out), lambda i: (i, 0))
        acc_shape = (tm, n_out)

    hosted = _Hosted(rider, 2 * n_pair, 1, int(n_red > 1), grid)

    def body(*refs):
        ins, (out,), scr = hosted.split(refs)

        def compute():
            acc = None
            for p in range(n_pair):
                d = lax.dot_general(ins[2 * p][...].astype(BF16), ins[2 * p + 1][...].astype(BF16), dims,
                                    preferred_element_type=F32)
                acc = d if acc is None else acc + d
            return acc

        if n_red == 1:
            out[...] = compute().astype(out.dtype)
        else:
            acc_ref = scr[0]
            r = pl.program_id(red_axis)

            @pl.when(r == 0)
            def _():
                acc_ref[...] = jnp.zeros_like(acc_ref)

            acc_ref[...] += compute()

            @pl.when(r == n_red - 1)
            def _():
                out[...] = acc_ref[...].astype(out.dtype)
        hosted.finish()

    in_specs, args, vmem = [], [], 0
    for a, b in pairs:
        in_specs += [pl.BlockSpec(a_blk(a), a_idx), pl.BlockSpec(b_blk(b), b_idx)]
        args += [a, b]
        vmem += 2 * (_nbytes([s for s in a_blk(a) if s], a.dtype) + _nbytes([s for s in b_blk(b) if s], b.dtype))
    vmem += 3 * _nbytes(acc_shape, F32) + 2 * n_pair * _nbytes(acc_shape, F32)
    scratch = [pltpu.VMEM(acc_shape, F32)] if n_red > 1 else []
    in_specs, out_shapes, out_specs, scratch, args = hosted.call_args(
        in_specs, [jax.ShapeDtypeStruct(out_shape, out_dtype)], [out_spec], scratch, args)
    return hosted.results(pl.pallas_call(
        body, name=name, out_shape=out_shapes, grid=grid, in_specs=in_specs, out_specs=out_specs,
        scratch_shapes=scratch, compiler_params=_params(vmem + (8 << 20), len(grid)),
    )(*args))


def row(arr, width=None, cb=0, roff=0):
    return (arr, arr.shape[-1] if width is None else width, cb, roff)


def _row_spec(desc, tm, limit=None):
    _, width, cb, roff = desc
    if limit is None:
        return pl.BlockSpec((tm, width), lambda i: (i + roff, cb))
    return pl.BlockSpec((tm, width), lambda i: (jnp.minimum(i, limit - 1) + roff, cb))


def _segmenter(tm, seq_len, n_lat):
    seg = lambda i: jnp.where(i * tm < n_lat, (i * tm) // seq_len, n_lat // seq_len)
    first = lambda i: jnp.where(i * tm < n_lat, (i * tm) % seq_len == 0, i * tm == n_lat)
    return seg, first


def rowwise(name, fn, rows, segs, params, outs, *, tm, n_tiles, seg_fn=None, rider=None):
    n_r, n_s, n_p = len(rows), len(segs), len(params)
    hosted = _Hosted(rider, n_r + n_s + n_p, len(outs), 0, (n_tiles,))

    def body(*refs):
        ins, out_refs, _ = hosted.split(refs)
        vals = [r[...].astype(F32) for r in ins[:n_r]] + [r[...] for r in ins[n_r:]]
        res = fn(*vals)
        for o_ref, v in zip(out_refs, res):
            o_ref[...] = v.astype(o_ref.dtype)
        hosted.finish()

    in_specs = [_row_spec(d, tm) for d in rows]
    in_specs += [pl.BlockSpec((None, 1, s.shape[-1]), lambda i: (seg_fn(i), 0, 0)) for s in segs]
    in_specs += [pl.BlockSpec(p.shape, lambda i: (0, 0)) for p in params]
    vmem = sum(2 * tm * d[1] * 4 for d in rows) + sum(3 * tm * w * 4 for _, w, _ in outs) + sum(2 * p.size * 4 for p in params)
    in_specs, out_shapes, out_specs, scratch, args = hosted.call_args(
        in_specs, [jax.ShapeDtypeStruct((r, w), dt) for r, w, dt in outs],
        [pl.BlockSpec((tm, w), lambda i: (i, 0)) for _, w, _ in outs], [], [d[0] for d in rows] + list(segs) + list(params))
    return hosted.results(pl.pallas_call(
        body, name=name, grid=(n_tiles,), in_specs=in_specs, out_shape=out_shapes, out_specs=out_specs,
        scratch_shapes=scratch, compiler_params=_params(2 * vmem + (8 << 20)),
    )(*args), unwrap=False)


def rowwise_bwd(name, fn, rows, segs, params, cts, row_grads, *, tm, n_tiles, seg_fn=None, first_fn=None, adds=None,
                rider=None):
    adds = adds or {}
    need = [k for k, v in enumerate(row_grads) if v is not None]
    n_r, n_s, n_p = len(rows), len(segs), len(params)
    n_ct = sum(len(lst) for lst in cts)
    add_keys = sorted(adds)
    hosted = _Hosted(rider, n_r + n_s + n_p + n_ct + len(add_keys), len(need) + n_s + n_p, 0, (n_tiles,))

    def body(*refs):
        host_in, host_out, _ = hosted.split(refs)
        it = iter(list(host_in) + list(host_out))
        row_refs = [next(it) for _ in range(n_r)]
        seg_refs = [next(it) for _ in range(n_s)]
        par_refs = [next(it) for _ in range(n_p)]
        ct_refs = [[next(it) for _ in lst] for lst in cts]
        add_refs = {k: next(it) for k in add_keys}
        rg_refs = {k: next(it) for k in need}
        sg_refs = [next(it) for _ in range(n_s)]
        pg_refs = [next(it) for _ in range(n_p)]
        i = pl.program_id(0)
        rv = [r[...].astype(F32) for r in row_refs]
        sv = [r[...] for r in seg_refs]
        pv = [r[...] for r in par_refs]

        def f(*args):
            rr = list(rv)
            for j, k in enumerate(need):
                rr[k] = args[j]
            return fn(*rr, *args[len(need):])

        _, vjp = jax.vjp(f, *[rv[k] for k in need], *sv, *pv)
        ctv = []
        for lst in ct_refs:
            acc = lst[0][...].astype(F32)
            for r in lst[1:]:
                acc = acc + r[...].astype(F32)
            ctv.append(acc)
        g = vjp(tuple(ctv))
        for j, k in enumerate(need):
            gv = g[j]
            if k in adds:
                lim = adds[k][1]
                av = add_refs[k][...].astype(F32)
                gv = gv + (av if lim is None else jnp.where(i < lim, av, 0.0))
            lim = row_grads[k][2]
            if lim is None:
                rg_refs[k][...] = gv.astype(rg_refs[k].dtype)
            else:
                @pl.when(i < lim)
                def _(gv=gv, k=k):
                    rg_refs[k][...] = gv.astype(rg_refs[k].dtype)
        if n_s:
            opens = first_fn(i)
            for ref, gv in zip(sg_refs, g[len(need):len(need) + n_s]):
                @pl.when(opens)
                def _(ref=ref, gv=gv):
                    ref[...] = gv

                @pl.when(jnp.logical_not(opens))
                def _(ref=ref, gv=gv):
                    ref[...] += gv
        for ref, gv in zip(pg_refs, g[len(need) + n_s:]):
            @pl.when(i == 0)
            def _(ref=ref, gv=gv):
                ref[...] = gv

            @pl.when(i > 0)
            def _(ref=ref, gv=gv):
                ref[...] += gv
        hosted.finish()

    seg_spec = lambda s: pl.BlockSpec((None, 1, s.shape[-1]), lambda i: (seg_fn(i), 0, 0))
    par_spec = lambda p: pl.BlockSpec(p.shape, lambda i: (0, 0))
    in_specs = [_row_spec(d, tm) for d in rows] + [seg_spec(s) for s in segs] + [par_spec(p) for p in params]
    args = [d[0] for d in rows] + list(segs) + list(params)
    for lst in cts:
        in_specs += [_row_spec(d, tm) for d in lst]
        args += [d[0] for d in lst]
    for k in add_keys:
        in_specs.append(_row_spec(adds[k][0], tm, adds[k][1]))
        args.append(adds[k][0][0])
    out_shape, out_specs = [], []
    for k in need:
        n_rows, dt, lim = row_grads[k]
        out_shape.append(jax.ShapeDtypeStruct((n_rows, rows[k][1]), dt))
        out_specs.append(_row_spec((None, rows[k][1], 0, 0), tm, lim))
    for s in segs:
        out_shape.append(jax.ShapeDtypeStruct(s.shape, F32))
        out_specs.append(seg_spec(s))
    for p in params:
        out_shape.append(jax.ShapeDtypeStruct(p.shape, F32))
        out_specs.append(par_spec(p))
    vmem = sum(tm * d[1] * 4 for d in rows) * 6 + n_ct * tm * max(d[1] for d in rows) * 8
    in_specs, out_shape, out_specs, scratch, args = hosted.call_args(in_specs, out_shape, out_specs, [], args)
    return hosted.results(pl.pallas_call(
        body, name=name, grid=(n_tiles,), in_specs=in_specs, out_shape=out_shape, out_specs=out_specs,
        scratch_shapes=scratch, compiler_params=_params(vmem + (8 << 20)),
    )(*args), unwrap=False)


def _silu(v):
    return v * jax.nn.sigmoid(v)


def _rms(v, w):
    return v * lax.rsqrt(jnp.mean(v * v, axis=-1, keepdims=True) + EPS) * w


def fn_norm_mod(x, shift, scale, w):
    return (_rms(x, w) * (1.0 + scale) + shift,)


def fn_act(g, u):
    return (_silu(g) * u,)


def make_fn_resid(coef):
    def fn(x, f, gate):
        return (x + coef * gate * f,)
    return fn


def fn_silu_bias(v, b):
    return (_silu(v + b),)


def make_fn_gate_groupnorm(width):
    half = width // 2

    def fn(yf, yb, z, w):
        y = (yf + yb) * _silu(z)
        lane = lax.broadcasted_iota(jnp.int32, y.shape, 1)
        lo = lane < half
        sq = y * y
        s_lo = jnp.sum(jnp.where(lo, sq, 0.0), axis=-1, keepdims=True)
        s_hi = jnp.sum(jnp.where(lo, 0.0, sq), axis=-1, keepdims=True)
        r = jnp.where(lo, lax.rsqrt(s_lo / half + EPS), lax.rsqrt(s_hi / half + EPS))
        return (y * r * w,)
    return fn


def fn_glu(a, b):
    return (a * jax.nn.sigmoid(b),)


def fn_ln_silu(vw, vh, cb, lw, lb):
    v = jnp.concatenate([vw, vh], axis=-1) + cb
    mu = jnp.mean(v, axis=-1, keepdims=True)
    var = jnp.mean(jnp.square(v - mu), axis=-1, keepdims=True)
    return (_silu((v - mu) * lax.rsqrt(var + EPS) * lw + lb),)


def _col_tile(width):
    return width // 3 if width % (3 * LANES) == 0 else width


def mod_fwd(a_rows, w_shard, b_shard):
    n, d = a_rows.shape
    ws = w_shard.shape[1]
    tn = _col_tile(ws)

    def body(a_ref, w_ref, b_ref, o_ref):
        a = _silu(a_ref[...]).astype(BF16)
        o_ref[...] = jnp.dot(a, w_ref[...].astype(BF16), preferred_element_type=F32) + b_ref[...]

    return pl.pallas_call(
        body, name="mod_fwd", grid=(ws // tn,), out_shape=jax.ShapeDtypeStruct((n, ws), F32),
        in_specs=[pl.BlockSpec((n, d), lambda j: (0, 0)), pl.BlockSpec((d, tn), lambda j: (0, j)),
                  pl.BlockSpec((1, tn), lambda j: (0, j))],
        out_specs=pl.BlockSpec((n, tn), lambda j: (0, j)), compiler_params=_params(),
    )(a_rows, w_shard, b_shard)


def mod_bwd(a_rows, d_shard, d_full, w_shard, ctx_rows):
    n, d = a_rows.shape
    ws = w_shard.shape[1]
    tn = _col_tile(ws)
    n_ct = ws // tn

    def body(a_ref, ds_ref, df_ref, w_ref, gw_ref, gb_ref, q_ref):
        j = pl.program_id(0)
        a = _silu(a_ref[...])
        ds = ds_ref[...]
        gw_ref[...] = lax.dot_general(a, ds, _DIMS["tn"], precision=HI, preferred_element_type=F32)
        dctx = ds[ctx_rows[0]:ctx_rows[0] + 1, :]
        for r in ctx_rows[1:]:
            dctx = dctx + ds[r:r + 1, :]
        q = lax.dot_general(jnp.broadcast_to(dctx, (8, tn)), w_ref[...], _DIMS["nt"], precision=HI,
                            preferred_element_type=F32)

        @pl.when(j == 0)
        def _():
            q_ref[...] = q
            df = df_ref[...]
            acc = df[0:1, :]
            for r in range(1, n):
                acc = acc + df[r:r + 1, :]
            gb_ref[...] = acc

        @pl.when(j > 0)
        def _():
            q_ref[...] += q

    return pl.pallas_call(
        body, name="mod_bwd", grid=(n_ct,),
        out_shape=(jax.ShapeDtypeStruct((d, ws), F32), jax.ShapeDtypeStruct((1, d_full.shape[1]), F32),
                   jax.ShapeDtypeStruct((8, d), F32)),
        in_specs=[pl.BlockSpec((n, d), lambda j: (0, 0)), pl.BlockSpec((n, tn), lambda j: (0, j)),
                  pl.BlockSpec(d_full.shape, lambda j: (0, 0)), pl.BlockSpec((d, tn), lambda j: (0, j))],
        out_specs=(pl.BlockSpec((d, tn), lambda j: (0, j)), pl.BlockSpec((1, d_full.shape[1]), lambda j: (0, 0)),
                   pl.BlockSpec((8, d), lambda j: (0, 0))),
        compiler_params=_params(40 << 20),
    )(a_rows, d_shard, d_full, w_shard)


def _shifted(xs, d, tok, width):
    if d == 0:
        return xs
    n = xs.shape[0]
    sh = pltpu.roll(xs, (-d) % n, axis=0)
    return jnp.where((tok + d >= 0) & (tok + d < width), sh, 0.0)


def tapsum_roll(name, x, xcb, w, wcb, *, seq_len, n_seq, row_blk_off, width, piece, cb, ncb, pad, flip):
    n_tap = w.shape[0]
    n_piece = seq_len // piece

    def body(x_ref, w_ref, o_ref):
        wv = w_ref[...]
        tok = lax.broadcasted_iota(jnp.int32, (piece, 1), 0) % width

        def do_piece(p, carry):
            start = pl.multiple_of(p * piece, piece)
            xs = x_ref[pl.ds(start, piece), :]
            acc = jnp.zeros_like(xs)
            for k in range(n_tap):
                d = pad - k if flip else k - pad
                acc = acc + wv[k:k + 1, :] * _shifted(xs, d, tok, width)
            o_ref[pl.ds(start, piece), :] = acc
            return carry

        lax.fori_loop(0, n_piece, do_piece, 0)

    return pl.pallas_call(
        body, name=name, grid=(ncb, n_seq), out_shape=jax.ShapeDtypeStruct((n_seq * seq_len, ncb * cb), F32),
        in_specs=[pl.BlockSpec((seq_len, cb), lambda j, s: (row_blk_off + s, xcb + j)),
                  pl.BlockSpec((n_tap, cb), lambda j, s: (0, wcb + j))],
        out_specs=pl.BlockSpec((seq_len, cb), lambda j, s: (s, j)),
        compiler_params=_params(8 * seq_len * cb * 4 + (8 << 20), 2),
    )(x, w)


def tapgrad_roll(name, dy, dycb, dy_blk_off, x, xcb, x_blk_off, *, n_tap, seq_len, n_seq, width, piece, cb, ncb, pad):
    n_piece = seq_len // piece

    def body(dy_ref, x_ref, o_ref):
        @pl.when(pl.program_id(1) == 0)
        def _():
            o_ref[...] = jnp.zeros_like(o_ref)

        tok = lax.broadcasted_iota(jnp.int32, (piece, 1), 0) % width

        def do_piece(p, carry):
            start = pl.multiple_of(p * piece, piece)
            xs = x_ref[pl.ds(start, piece), :]
            dv = dy_ref[pl.ds(start, piece), :]
            for k in range(n_tap):
                o_ref[k:k + 1, :] += jnp.sum(dv * _shifted(xs, k - pad, tok, width), axis=0, keepdims=True)
            return carry

        lax.fori_loop(0, n_piece, do_piece, 0)

    return pl.pallas_call(
        body, name=name, grid=(ncb, n_seq), out_shape=jax.ShapeDtypeStruct((n_tap, ncb * cb), F32),
        in_specs=[pl.BlockSpec((seq_len, cb), lambda j, s: (dy_blk_off + s, dycb + j)),
                  pl.BlockSpec((seq_len, cb), lambda j, s: (x_blk_off + s, xcb + j))],
        out_specs=pl.BlockSpec((n_tap, cb), lambda j, s: (0, j)),
        compiler_params=_params(8 * seq_len * cb * 4 + (8 << 20), 2),
    )(dy, x)


def tapsum_rows(name, x, xcb, w, wcb, *, seq_len, n_seq, cb, ncb, pad, flip):
    n_tap = w.shape[0]
    n_row = seq_len // GRID_W
    halo = pad * GRID_W

    def body(x_ref, w_ref, o_ref, xp):
        xp[pl.ds(0, halo), :] = jnp.zeros((halo, cb), F32)
        xp[pl.ds(halo + seq_len, halo), :] = jnp.zeros((halo, cb), F32)
        xp[pl.ds(halo, seq_len), :] = x_ref[...]
        wv = w_ref[...]

        def do_row(r, carry):
            acc = jnp.zeros((GRID_W, cb), F32)
            for k in range(n_tap):
                d = pad - k if flip else k - pad
                acc = acc + wv[k:k + 1, :] * xp[pl.ds(pl.multiple_of((r + pad + d) * GRID_W, GRID_W), GRID_W), :]
            o_ref[pl.ds(pl.multiple_of(r * GRID_W, GRID_W), GRID_W), :] = acc
            return carry

        lax.fori_loop(0, n_row, do_row, 0)

    return pl.pallas_call(
        body, name=name, grid=(ncb, n_seq), out_shape=jax.ShapeDtypeStruct((n_seq * seq_len, ncb * cb), F32),
        in_specs=[pl.BlockSpec((seq_len, cb), lambda j, s: (s, xcb + j)),
                  pl.BlockSpec((n_tap, cb), lambda j, s: (0, wcb + j))],
        out_specs=pl.BlockSpec((seq_len, cb), lambda j, s: (s, j)),
        scratch_shapes=[pltpu.VMEM((seq_len + 2 * halo, cb), F32)],
        compiler_params=_params(10 * seq_len * cb * 4 + (8 << 20), 2),
    )(x, w)


def tapgrad_rows(name, dy, dycb, x, xcb, *, n_tap, seq_len, n_seq, cb, ncb, pad):
    n_row = seq_len // GRID_W
    halo = pad * GRID_W

    def body(dy_ref, x_ref, o_ref, xp):
        @pl.when(pl.program_id(1) == 0)
        def _():
            o_ref[...] = jnp.zeros_like(o_ref)

        xp[pl.ds(0, halo), :] = jnp.zeros((halo, cb), F32)
        xp[pl.ds(halo + seq_len, halo), :] = jnp.zeros((halo, cb), F32)
        xp[pl.ds(halo, seq_len), :] = x_ref[...]

        def do_row(r, carry):
            dv = dy_ref[pl.ds(pl.multiple_of(r * GRID_W, GRID_W), GRID_W), :]
            for k in range(n_tap):
                xs = xp[pl.ds(pl.multiple_of((r + k) * GRID_W, GRID_W), GRID_W), :]
                o_ref[k:k + 1, :] += jnp.sum(dv * xs, axis=0, keepdims=True)
            return carry

        lax.fori_loop(0, n_row, do_row, 0)

    return pl.pallas_call(
        body, name=name, grid=(ncb, n_seq), out_shape=jax.ShapeDtypeStruct((n_tap, ncb * cb), F32),
        in_specs=[pl.BlockSpec((seq_len, cb), lambda j, s: (s, dycb + j)),
                  pl.BlockSpec((seq_len, cb), lambda j, s: (s, xcb + j))],
        out_specs=pl.BlockSpec((n_tap, cb), lambda j, s: (0, j)),
        scratch_shapes=[pltpu.VMEM((seq_len + 2 * halo, cb), F32)],
        compiler_params=_params(10 * seq_len * cb * 4 + (8 << 20), 2),
    )(dy, x)


def _ssd_blocks(b, s, *, rev, n_ctx, n_lat, lat_blocks):
    if rev:
        return jnp.where(s < n_ctx, lat_blocks + b * n_ctx + (n_ctx - 1 - s), b * n_lat + (n_lat - 1 - (s - n_ctx)))
    return jnp.where(s < n_ctx, lat_blocks + b * n_ctx + s, b * n_lat + (s - n_ctx))


def _ssd_common(xbc, raw, dtb, alog, dsk, *, rev, ds, n_head):
    if rev:
        raw = pltpu.roll(raw, LANES - n_head, axis=1)
    pre = raw + dtb
    dt = jnp.maximum(pre, 0.0) + jnp.log1p(jnp.exp(-jnp.abs(pre)))
    sig = jax.nn.sigmoid(pre)
    a = -jnp.exp(alog)
    da = dt * a
    ri = lax.broadcasted_iota(jnp.int32, (CHUNK, CHUNK), 0)
    ci = lax.broadcasted_iota(jnp.int32, (CHUNK, CHUNK), 1)
    mask = (ci >= ri) if rev else (ci <= ri)
    tri = mask.astype(F32)
    tri_t = ((ci <= ri) if rev else (ci >= ri)).astype(F32)
    cs = jnp.dot(tri, da, precision=HI, preferred_element_type=F32)
    tot = jnp.sum(da, axis=0, keepdims=True)
    widen = (lax.broadcasted_iota(jnp.int32, (LANES, ds), 1) // HEAD_DIM
             == lax.broadcasted_iota(jnp.int32, (LANES, ds), 0)).astype(F32)
    wide = lambda v: jnp.dot(v, widen, precision=HI, preferred_element_type=F32)
    wide1 = lambda v: wide(jnp.broadcast_to(v, (8, LANES)))[0:1, :]
    cs_w, tot_w = wide(cs), wide1(tot)
    xh = xbc[:, :ds]
    dt_w = wide(dt)
    return dict(
        dt=dt, sig=sig, a=a, cs=cs, cs_t=cs.T, tot=tot, mask=mask, tri_t=tri_t,
        e_w=jnp.exp(cs_w), wt_w=jnp.exp(tot_w - cs_w), dec_w=jnp.exp(tot_w), dt_w=dt_w, dsk_w=wide1(dsk),
        xh=xh, xs_w=xh * dt_w, bm=xbc[:, ds:ds + 2 * N_STATE], cm=xbc[:, ds + 2 * N_STATE:ds + 4 * N_STATE])


def _decay(q, col):
    seg = q["cs"][:, col:col + 1] - q["cs_t"][col:col + 1, :]
    return jnp.exp(jnp.where(q["mask"], seg, -jnp.inf))


def _split_heads(v):
    lane = lax.broadcasted_iota(jnp.int32, v.shape, 1)
    return jnp.concatenate([jnp.where(lane < HEAD_DIM, v, 0.0), jnp.where(lane >= HEAD_DIM, v, 0.0)], axis=0)


def ssd_fwd(name, xbc, proj, dt_cb, dtb, alog, dsk, *, rev, n_ex, seq_len, ctx_len, ds, rider=None):
    n_head, half = ds // HEAD_DIM, ds // 2
    n_ctx, n_lat = ctx_len // CHUNK, seq_len // CHUNK
    n_step = n_ctx + n_lat
    blk = functools.partial(_ssd_blocks, rev=rev, n_ctx=n_ctx, n_lat=n_lat, lat_blocks=n_ex * n_lat)
    xw = xbc.shape[1]

    def y_blk(b, s):
        sl = jnp.maximum(s, n_ctx) - n_ctx
        return b * n_lat + ((n_lat - 1 - sl) if rev else sl)

    hosted = _Hosted(rider, 5, 2, 1, (n_ex, n_step))

    def body(*refs):
        (xbc_ref, dt_ref, dtb_ref, alog_ref, dsk_ref), (y_ref, hs_ref), (h_scr,) = hosted.split(refs)

        @pl.when(pl.program_id(1) == 0)
        def _():
            h_scr[...] = jnp.zeros_like(h_scr)

        q = _ssd_common(xbc_ref[...], dt_ref[...], dtb_ref[...], alog_ref[...], dsk_ref[...], rev=rev, ds=ds, n_head=n_head)
        h = h_scr[...]
        hs_ref[...] = h
        for g in range(2):
            lo = g * half
            bg = q["bm"][:, g * N_STATE:(g + 1) * N_STATE].astype(BF16)
            cg = q["cm"][:, g * N_STATE:(g + 1) * N_STATE].astype(BF16)
            scores = lax.dot_general(cg, bg, _DIMS["nt"], preferred_element_type=F32)
            hg = h[:, lo:lo + half]
            off = jnp.dot(cg, hg.astype(BF16), preferred_element_type=F32)
            for j in range(half // LANES):
                c0 = (lo + j * LANES) // HEAD_DIM
                ln = slice(lo + j * LANES, lo + (j + 1) * LANES)
                p_cat = jnp.concatenate([scores * _decay(q, c0), scores * _decay(q, c0 + 1)], axis=1).astype(BF16)
                diag = jnp.dot(p_cat, _split_heads(q["xs_w"][:, ln]).astype(BF16), preferred_element_type=F32)
                y_ref[:, ln] = (diag + q["e_w"][:, ln] * off[:, j * LANES:(j + 1) * LANES]
                                + q["dsk_w"][:, ln] * q["xh"][:, ln])
            v = (q["wt_w"][:, lo:lo + half] * q["xs_w"][:, lo:lo + half]).astype(BF16)
            h_scr[:, lo:lo + half] = (q["dec_w"][:, lo:lo + half] * hg
                                      + lax.dot_general(bg, v, _DIMS["tn"], preferred_element_type=F32))
        hosted.finish()

    vec = pl.BlockSpec((1, LANES), lambda b, s: (0, 0))
    in_specs, out_shape, out_specs, scratch, args = hosted.call_args(
        [pl.BlockSpec((CHUNK, xw), lambda b, s: (blk(b, s), 0)),
         pl.BlockSpec((CHUNK, LANES), lambda b, s: (blk(b, s), dt_cb)), vec, vec, vec],
        (jax.ShapeDtypeStruct((n_ex * seq_len, ds), F32), jax.ShapeDtypeStruct((n_ex, n_step, N_STATE, ds), F32)),
        (pl.BlockSpec((CHUNK, ds), lambda b, s: (y_blk(b, s), 0)),
         pl.BlockSpec((None, None, N_STATE, ds), lambda b, s: (b, s, 0, 0))),
        [pltpu.VMEM((N_STATE, ds), F32)], [xbc, proj, dtb, alog, dsk])
    return hosted.results(pl.pallas_call(
        body, name=name, grid=(n_ex, n_step), out_shape=out_shape, in_specs=in_specs, out_specs=out_specs,
        scratch_shapes=scratch, compiler_params=_params(40 << 20, 2),
    )(*args))


def ssd_bwd(name, xbc, proj, dt_cb, hs, dy, dtb, alog, dsk, *, rev, n_ex, seq_len, ctx_len, ds, rider=None):
    n_head, half = ds // HEAD_DIM, ds // 2
    n_ctx, n_lat = ctx_len // CHUNK, seq_len // CHUNK
    n_step = n_ctx + n_lat
    n_tok = n_ex * (seq_len + ctx_len)
    blk0 = functools.partial(_ssd_blocks, rev=rev, n_ctx=n_ctx, n_lat=n_lat, lat_blocks=n_ex * n_lat)
    step = lambda sp: n_step - 1 - sp
    blk = lambda b, sp: blk0(b, step(sp))
    xw = xbc.shape[1]

    def dy_blk(b, sp):
        sl = jnp.maximum(step(sp), n_ctx) - n_ctx
        return b * n_lat + ((n_lat - 1 - sl) if rev else sl)

    hosted = _Hosted(rider, 7, 5, 1, (n_ex, n_step))

    def body(*refs):
        ((xbc_ref, dt_ref, hs_ref, dy_ref, dtb_ref, alog_ref, dsk_ref),
         (dxbc_ref, ddt_ref, dalog_ref, ddtb_ref, ddsk_ref), (dh_scr,)) = hosted.split(refs)
        b, sp = pl.program_id(0), pl.program_id(1)

        @pl.when(sp == 0)
        def _():
            dh_scr[...] = jnp.zeros_like(dh_scr)

        @pl.when((sp == 0) & (b == 0))
        def _():
            dalog_ref[...] = jnp.zeros_like(dalog_ref)
            ddtb_ref[...] = jnp.zeros_like(ddtb_ref)
            ddsk_ref[...] = jnp.zeros_like(ddsk_ref)

        q = _ssd_common(xbc_ref[...], dt_ref[...], dtb_ref[...], alog_ref[...], dsk_ref[...], rev=rev, ds=ds, n_head=n_head)
        h = hs_ref[...]
        d_y = jnp.where(step(sp) >= n_ctx, dy_ref[...], 0.0)
        dh_next = dh_scr[...]
        lane_row = lax.broadcasted_iota(jnp.int32, (1, LANES), 1)
        d_cs = jnp.zeros((CHUNK, LANES), F32)
        dxs_parts, de_parts, dwt_parts, ddec_parts = [], [], [], []
        for g in range(2):
            lo = g * half
            gs = slice(lo, lo + half)
            bg = q["bm"][:, g * N_STATE:(g + 1) * N_STATE].astype(BF16)
            cg = q["cm"][:, g * N_STATE:(g + 1) * N_STATE].astype(BF16)
            scores = lax.dot_general(cg, bg, _DIMS["nt"], preferred_element_type=F32)
            hg, dyg, dhn = h[:, gs], d_y[:, gs], dh_next[:, gs]
            off = jnp.dot(cg, hg.astype(BF16), preferred_element_type=F32)
            d_off = (q["e_w"][:, gs] * dyg).astype(BF16)
            de_parts.append(dyg * off)
            d_c = lax.dot_general(d_off, hg.astype(BF16), _DIMS["nt"], preferred_element_type=F32)
            dh_scr[:, gs] = (lax.dot_general(cg, d_off, _DIMS["tn"], preferred_element_type=F32)
                             + q["dec_w"][:, gs] * dhn)
            b_dh = jnp.dot(bg, dhn.astype(BF16), preferred_element_type=F32)
            v = q["wt_w"][:, gs] * q["xs_w"][:, gs]
            d_b = lax.dot_general(v.astype(BF16), dhn.astype(BF16), _DIMS["nt"], preferred_element_type=F32)
            dwt_parts.append(q["xs_w"][:, gs] * b_dh)
            ddec_parts.append(jnp.sum(hg * dhn, axis=0, keepdims=True))
            d_scores = jnp.zeros((CHUNK, CHUNK), F32)
            for j in range(half // LANES):
                c0 = (lo + j * LANES) // HEAD_DIM
                ln = slice(lo + j * LANES, lo + (j + 1) * LANES)
                l0, l1 = _decay(q, c0), _decay(q, c0 + 1)
                p0, p1 = scores * l0, scores * l1
                dy_st = _split_heads(d_y[:, ln]).astype(BF16)
                d_p = lax.dot_general(dy_st, q["xs_w"][:, ln].astype(BF16), _DIMS["nt"], preferred_element_type=F32)
                d_p0, d_p1 = d_p[:CHUNK], d_p[CHUNK:]
                d_scores = d_scores + d_p0 * l0 + d_p1 * l1
                for col, t in ((c0, d_p0 * p0), (c0 + 1, d_p1 * p1)):
                    d_cs = d_cs + jnp.sum(t - t.T, axis=1, keepdims=True) * (lane_row == col).astype(F32)
                p_st = jnp.concatenate([p0, p1], axis=0).astype(BF16)
                dxs_parts.append(lax.dot_general(p_st, dy_st, _DIMS["tn"], preferred_element_type=F32)
                                 + q["wt_w"][:, ln] * b_dh[:, j * LANES:(j + 1) * LANES])
            d_sc = d_scores.astype(BF16)
            d_c = d_c + jnp.dot(d_sc, bg, preferred_element_type=F32)
            d_b = d_b + lax.dot_general(d_sc, cg, _DIMS["tn"], preferred_element_type=F32)
            dxbc_ref[:, ds + g * N_STATE:ds + (g + 1) * N_STATE] = d_b
            dxbc_ref[:, ds + (2 + g) * N_STATE:ds + (3 + g) * N_STATE] = d_c
        d_xs = jnp.concatenate(dxs_parts, axis=1)
        narrow_m = (lax.broadcasted_iota(jnp.int32, (ds, LANES), 0) // HEAD_DIM
                    == lax.broadcasted_iota(jnp.int32, (ds, LANES), 1)).astype(F32)
        narrow = lambda v: jnp.dot(v, narrow_m, precision=HI, preferred_element_type=F32)
        narrow1 = lambda v: narrow(jnp.broadcast_to(v, (8, ds)))[0:1, :]
        e, wt, dec = jnp.exp(q["cs"]), jnp.exp(q["tot"] - q["cs"]), jnp.exp(q["tot"])
        d_wt = narrow(jnp.concatenate(dwt_parts, axis=1)) * wt
        d_cs = d_cs + narrow(jnp.concatenate(de_parts, axis=1)) * e - d_wt
        d_tot = jnp.sum(d_wt, axis=0, keepdims=True) + narrow1(jnp.concatenate(ddec_parts, axis=1)) * dec
        d_da = jnp.dot(q["tri_t"], d_cs, precision=HI, preferred_element_type=F32) + d_tot
        d_dt = d_da * q["a"] + narrow(d_xs * q["xh"])
        dxbc_ref[:, :ds] = d_xs * q["dt_w"] + q["dsk_w"] * d_y
        dalog_ref[...] += jnp.sum(d_da * q["dt"], axis=0, keepdims=True) * q["a"]
        d_raw = d_dt * q["sig"]
        ddtb_ref[...] += jnp.sum(d_raw, axis=0, keepdims=True)
        ddsk_ref[...] += narrow1(jnp.sum(d_y * q["xh"], axis=0, keepdims=True))
        ddt_ref[...] = pltpu.roll(d_raw, n_head, axis=1) if rev else d_raw
        hosted.finish()

    vec = pl.BlockSpec((1, LANES), lambda b, s: (0, 0))
    vec_shape = jax.ShapeDtypeStruct((1, LANES), F32)
    in_specs, out_shape, out_specs, scratch, args = hosted.call_args(
        [pl.BlockSpec((CHUNK, xw), lambda b, s: (blk(b, s), 0)),
         pl.BlockSpec((CHUNK, LANES), lambda b, s: (blk(b, s), dt_cb)),
         pl.BlockSpec((None, None, N_STATE, ds), lambda b, s: (b, step(s), 0, 0)),
         pl.BlockSpec((CHUNK, ds), lambda b, s: (dy_blk(b, s), 0)), vec, vec, vec],
        (jax.ShapeDtypeStruct((n_tok, xw), F32), jax.ShapeDtypeStruct((n_tok, LANES), F32), vec_shape, vec_shape, vec_shape),
        (pl.BlockSpec((CHUNK, xw), lambda b, s: (blk(b, s), 0)),
         pl.BlockSpec((CHUNK, LANES), lambda b, s: (blk(b, s), 0)), vec, vec, vec),
        [pltpu.VMEM((N_STATE, ds), F32)], [xbc, proj, hs, dy, dtb, alog, dsk])
    return hosted.results(pl.pallas_call(
        body, name=name, grid=(n_ex, n_step), out_shape=out_shape, in_specs=in_specs, out_specs=out_specs,
        scratch_shapes=scratch, compiler_params=_params(48 << 20, 2),
    )(*args))


def final_loss(x3, target, w, *, tm):
    n, d = x3.shape

    def body(x_ref, t_ref, w_ref, dx_ref, dw_ref, loss_ref):
        i = pl.program_id(0)
        t = t_ref[...]

        def per_feature(xv, wv):
            err = _rms(xv, wv) - t
            return 0.5 * jnp.sum(err * err, axis=0, keepdims=True) / d

        lv, vjp = jax.vjp(per_feature, x_ref[...], w_ref[...])
        dx, dw = vjp(jnp.ones_like(lv))
        dx_ref[...] = dx

        @pl.when(i == 0)
        def _():
            dw_ref[...] = dw
            loss_ref[...] = lv

        @pl.when(i > 0)
        def _():
            dw_ref[...] += dw
            loss_ref[...] += lv

    tile = pl.BlockSpec((tm, d), lambda i: (i, 0))
    vec = pl.BlockSpec((1, d), lambda i: (0, 0))
    return pl.pallas_call(
        body, name="final_loss", grid=(n // tm,), in_specs=[tile, tile, vec],
        out_shape=(jax.ShapeDtypeStruct((n, d), F32), jax.ShapeDtypeStruct((1, d), F32), jax.ShapeDtypeStruct((1, d), F32)),
        out_specs=(tile, vec, vec), compiler_params=_params(tm * d * 4 * 16 + (8 << 20)),
    )(x3, target, w)


def sum_slots(name, arr):
    n_slot, n_row, width = arr.shape
    tm = _row_tile(n_row, width * n_slot, mult=16)

    def body(a_ref, o_ref):
        acc = a_ref[0].astype(F32)
        for j in range(1, n_slot):
            acc = acc + a_ref[j].astype(F32)
        o_ref[...] = acc

    return pl.pallas_call(
        body, name=name, grid=(n_row // tm,), out_shape=jax.ShapeDtypeStruct((n_row, width), F32),
        in_specs=[pl.BlockSpec((n_slot, tm, width), lambda i: (0, i, 0))],
        out_specs=pl.BlockSpec((tm, width), lambda i: (i, 0)), compiler_params=_params(),
    )(arr)


def adamw(name, w, g_slots, m, v):
    n_slot, n_row, width = g_slots.shape
    tm = _row_tile(n_row, width * 2)

    def body(w_ref, g_ref, m_ref, v_ref, go_ref, d_ref, mo_ref, vo_ref):
        g = g_ref[0]
        for j in range(1, n_slot):
            g = g + g_ref[j]
        m2 = ADAM_B1 * m_ref[...] + (1.0 - ADAM_B1) * g
        v2 = ADAM_B2 * v_ref[...] + (1.0 - ADAM_B2) * jnp.square(g)
        m_hat = m2 / (1.0 - ADAM_B1 ** ADAM_STEP)
        v_hat = v2 / (1.0 - ADAM_B2 ** ADAM_STEP)
        go_ref[...] = g
        d_ref[...] = -ADAM_LR * (m_hat / (jnp.sqrt(v_hat) + ADAM_EPS) + ADAM_WD * w_ref[...])
        mo_ref[...] = m2
        vo_ref[...] = v2

    tile = pl.BlockSpec((tm, width), lambda i: (i, 0))
    shape = jax.ShapeDtypeStruct((n_row, width), F32)
    return pl.pallas_call(
        body, name=name, grid=(n_row // tm,), out_shape=(shape,) * 4,
        in_specs=[tile, pl.BlockSpec((n_slot, tm, width), lambda i: (0, i, 0)), tile, tile],
        out_specs=(tile,) * 4, compiler_params=_params(),
    )(w, g_slots, m, v)


def cctx_grad(q_all, c_ctx_row):
    d = c_ctx_row.shape[1]

    def body(q_ref, c_ref, o_ref):
        acc = q_ref[0, 0:1, :]
        for j in (2, 4, 6):
            acc = acc + q_ref[j, 0:1, :]
        _, vjp = jax.vjp(_silu, c_ref[...])
        o_ref[...] = vjp(acc)[0]

    return pl.pallas_call(
        body, name="cctx_grad", out_shape=jax.ShapeDtypeStruct((1, d), F32),
    )(q_all, c_ctx_row)


def loss_total(pack_sum, d):
    def body(p_ref, o_ref):
        o_ref[...] = jnp.sum(p_ref[:, 0:d], axis=1, keepdims=True)

    return pl.pallas_call(
        body, name="loss_total", out_shape=jax.ShapeDtypeStruct((1, 1), F32),
    )(pack_sum)


class _Plan:
    def __init__(self):
        self.builders, self.got = {}, {}

    def on(self, host, key, builder):
        self.builders[host] = (key, builder)

    def run(self, host, fn, *args, **kw):
        if host not in self.builders:
            return fn(host, *args, **kw)
        key, builder = self.builders[host]
        res, landed = fn(host, *args, rider=builder(self), **kw)
        self.got[key] = landed
        return res


def _val(w):
    return w() if callable(w) else w


def _ffn_fwd(plan, tag, xin, n_rows, tm, seg_fn, shift, scale, gate, norm_w, wg, wu, wd):
    d = xin.shape[1]
    n_tiles = n_rows // tm
    (h,) = plan.run(f"{tag}_norm", rowwise, fn_norm_mod, [row(xin)], [shift, scale], [norm_w], [(n_rows, d, BF16)],
                    tm=tm, n_tiles=n_tiles, seg_fn=seg_fn)
    g = plan.run(f"{tag}_gate", matmul, [(h, _val(wg))], "nn", b_ch=True, out_ch=True, tm=tm)
    u = plan.run(f"{tag}_up", matmul, [(h, _val(wu))], "nn", b_ch=True, out_ch=True, tm=tm)
    n_ch, _, n_hid = g.shape
    (act,) = plan.run(f"{tag}_act", rowwise, fn_act, [row(g.reshape(n_ch * n_rows, n_hid)), row(u.reshape(n_ch * n_rows, n_hid))],
                      [], [], [(n_ch * n_rows, n_hid, BF16)], tm=tm, n_tiles=n_ch * n_tiles)
    act = act.reshape(n_ch, n_rows, n_hid)
    f = plan.run(f"{tag}_down", matmul, [(act, _val(wd))], "nn", a_ch=True, b_ch=True, tm=tm)
    (xo,) = plan.run(f"{tag}_resid", rowwise, make_fn_resid(0.5), [row(xin), row(f)], [gate], [], [(n_rows, d, F32)],
                     tm=tm, n_tiles=n_tiles, seg_fn=seg_fn)
    return xo, (h, g, u, act, f)


def _ffn_bwd(plan, tag, d_xo, saved, xin, n_rows, tm, seg_fn, first_fn, shift, scale, gate, norm_w, wg, wu, wd, dx_rows, dx_limit):
    h, g, u, act, f = saved
    d = xin.shape[1]
    n_tiles = n_rows // tm
    n_ch, _, n_hid = g.shape
    d_f, d_gate = plan.run(f"{tag}_resid_bwd", rowwise_bwd, make_fn_resid(0.5), [row(xin), row(f)], [gate], [], [[row(d_xo)]],
                           [None, (n_rows, BF16, None)], tm=tm, n_tiles=n_tiles, seg_fn=seg_fn, first_fn=first_fn)
    d_act = plan.run(f"{tag}_down_dx", matmul, [(d_f, wd)], "nt", b_ch=True, out_ch=True, tm=tm)
    plan.got[f"{tag}_d_wd"] = plan.run(f"{tag}_down_dw", matmul, [(act, d_f)], "tn", out_dtype=BF16, a_ch=True, out_ch=True, tm=tm)
    flat = lambda t: t.reshape(n_ch * n_rows, n_hid)
    d_g, d_u = plan.run(f"{tag}_act_bwd", rowwise_bwd, fn_act, [row(flat(g)), row(flat(u))], [], [], [[row(flat(d_act))]],
                        [(n_ch * n_rows, BF16, None)] * 2, tm=tm, n_tiles=n_ch * n_tiles)
    d_g, d_u = d_g.reshape(g.shape), d_u.reshape(g.shape)
    d_h = plan.run(f"{tag}_up_dx", matmul, [(d_g, wg), (d_u, wu)], "nt", a_ch=True, b_ch=True, tm=tm)
    plan.got[f"{tag}_d_wg"] = plan.run(f"{tag}_gate_dw", matmul, [(h, d_g)], "tn", out_dtype=BF16, b_ch=True, out_ch=True, tm=tm)
    plan.got[f"{tag}_d_wu"] = plan.run(f"{tag}_up_dw", matmul, [(h, d_u)], "tn", out_dtype=BF16, b_ch=True, out_ch=True, tm=tm)
    d_x, d_shift, d_scale, d_nw = plan.run(
        f"{tag}_norm_bwd", rowwise_bwd, fn_norm_mod, [row(xin)], [shift, scale], [norm_w], [[row(d_h)]], [(dx_rows, F32, dx_limit)],
        tm=tm, n_tiles=n_tiles, seg_fn=seg_fn, first_fn=first_fn, adds={0: (row(d_xo), None)})
    return d_x, (d_shift, d_scale, d_gate), d_nw


def kernel(x, c, ctx, c_ctx, w_mod, b_mod, norm_ffn1, ffn1_gate, ffn1_up, ffn1_down, norm_mix, w_in, ssm_conv_w, ssm_conv_b, dt_bias_fwd, dt_bias_bwd, a_log_fwd, a_log_bwd, ssm_d, ssm_norm_w, cconv_w, cconv_b, cconv_ln_w, cconv_ln_b, w_out, norm_ffn2, ffn2_gate, ffn2_up, ffn2_down, final_norm, loss_target, m_c_ctx, m_w_mod, m_b_mod, m_norm_ffn1, m_ffn1_gate, m_ffn1_up, m_ffn1_down, m_norm_mix, m_w_in, m_ssm_conv_w, m_ssm_conv_b, m_dt_bias_fwd, m_dt_bias_bwd, m_a_log_fwd, m_a_log_bwd, m_ssm_d, m_ssm_norm_w, m_cconv_w, m_cconv_b, m_cconv_ln_w, m_cconv_ln_b, m_w_out, m_norm_ffn2, m_ffn2_gate, m_ffn2_up, m_ffn2_down, m_final_norm, v_c_ctx, v_w_mod, v_b_mod, v_norm_ffn1, v_ffn1_gate, v_ffn1_up, v_ffn1_down, v_norm_mix, v_w_in, v_ssm_conv_w, v_ssm_conv_b, v_dt_bias_fwd, v_dt_bias_bwd, v_a_log_fwd, v_a_log_bwd, v_ssm_d, v_ssm_norm_w, v_cconv_w, v_cconv_b, v_cconv_ln_w, v_cconv_ln_b, v_w_out, v_norm_ffn2, v_ffn2_gate, v_ffn2_up, v_ffn2_down, v_final_norm):
    weights = dict(c_ctx=c_ctx, w_mod=w_mod, b_mod=b_mod, norm_ffn1=norm_ffn1, ffn1_gate=ffn1_gate, ffn1_up=ffn1_up, ffn1_down=ffn1_down, norm_mix=norm_mix, w_in=w_in, ssm_conv_w=ssm_conv_w, ssm_conv_b=ssm_conv_b, dt_bias_fwd=dt_bias_fwd, dt_bias_bwd=dt_bias_bwd, a_log_fwd=a_log_fwd, a_log_bwd=a_log_bwd, ssm_d=ssm_d, ssm_norm_w=ssm_norm_w, cconv_w=cconv_w, cconv_b=cconv_b, cconv_ln_w=cconv_ln_w, cconv_ln_b=cconv_ln_b, w_out=w_out, norm_ffn2=norm_ffn2, ffn2_gate=ffn2_gate, ffn2_up=ffn2_up, ffn2_down=ffn2_down, final_norm=final_norm)
    mom1 = dict(c_ctx=m_c_ctx, w_mod=m_w_mod, b_mod=m_b_mod, norm_ffn1=m_norm_ffn1, ffn1_gate=m_ffn1_gate, ffn1_up=m_ffn1_up, ffn1_down=m_ffn1_down, norm_mix=m_norm_mix, w_in=m_w_in, ssm_conv_w=m_ssm_conv_w, ssm_conv_b=m_ssm_conv_b, dt_bias_fwd=m_dt_bias_fwd, dt_bias_bwd=m_dt_bias_bwd, a_log_fwd=m_a_log_fwd, a_log_bwd=m_a_log_bwd, ssm_d=m_ssm_d, ssm_norm_w=m_ssm_norm_w, cconv_w=m_cconv_w, cconv_b=m_cconv_b, cconv_ln_w=m_cconv_ln_w, cconv_ln_b=m_cconv_ln_b, w_out=m_w_out, norm_ffn2=m_norm_ffn2, ffn2_gate=m_ffn2_gate, ffn2_up=m_ffn2_up, ffn2_down=m_ffn2_down, final_norm=m_final_norm)
    mom2 = dict(c_ctx=v_c_ctx, w_mod=v_w_mod, b_mod=v_b_mod, norm_ffn1=v_norm_ffn1, ffn1_gate=v_ffn1_gate, ffn1_up=v_ffn1_up, ffn1_down=v_ffn1_down, norm_mix=v_norm_mix, w_in=v_w_in, ssm_conv_w=v_ssm_conv_w, ssm_conv_b=v_ssm_conv_b, dt_bias_fwd=v_dt_bias_fwd, dt_bias_bwd=v_dt_bias_bwd, a_log_fwd=v_a_log_fwd, a_log_bwd=v_a_log_bwd, ssm_d=v_ssm_d, ssm_norm_w=v_ssm_norm_w, cconv_w=v_cconv_w, cconv_b=v_cconv_b, cconv_ln_w=v_cconv_ln_w, cconv_ln_b=v_cconv_ln_b, w_out=v_w_out, norm_ffn2=v_norm_ffn2, ffn2_gate=v_ffn2_gate, ffn2_up=v_ffn2_up, ffn2_down=v_ffn2_down, final_norm=v_final_norm)
    order = list(weights)

    n_ex, seq_len, d = x.shape
    ctx_len = ctx.shape[1]
    ds = d
    n_head = ds // HEAD_DIM
    xw = ds + 4 * N_STATE
    n_lat, n_ctx_rows = n_ex * seq_len, n_ex * ctx_len
    n_tok = n_lat + n_ctx_rows
    tm = math.gcd(math.gcd(512, seq_len), n_ctx_rows)
    seg_all, first_all = _segmenter(tm, seq_len, n_lat)
    lat_tiles = n_lat // tm

    xi, yi, ci = lax.axis_index("x"), lax.axis_index("y"), lax.axis_index("c")
    me, chip = 4 * xi + 2 * yi + ci, 2 * xi + yi

    (c_all,) = exchange("gather_c", [c], "all8")
    n_all = 8 * n_ex
    n_cond = -(-(n_all + 1) // 8) * 8
    cond = jnp.concatenate([c_all.reshape(n_all, d), c_ctx[None, :], jnp.zeros((n_cond - n_all - 1, d), F32)])
    mod_w = w_mod.shape[2]
    b_shard = lax.dynamic_slice(b_mod, (0, chip * mod_w), (1, mod_w))
    (mod_g,) = exchange("gather_mod", [mod_fwd(cond, w_mod[0], b_shard)], "chips")
    mod_full = mod_g.transpose(1, 0, 2).reshape(n_cond, N_CHIPS * mod_w)
    mod_mine = lax.dynamic_slice(mod_full, (me * n_ex, 0), (n_ex, 9 * d)).reshape(n_ex, 9, d)
    mod_ctx = mod_full[n_all].reshape(9, d)
    tabs = [jnp.concatenate([mod_mine[:, j], mod_ctx[j][None]])[:, None, :] for j in range(9)]
    lat = lambda t: t[:n_ex]

    bf = lambda w: w[0].astype(BF16)
    plan = _Plan()
    gather = lambda *ws: (lambda p: Rider(list(ws), "chips"))
    wg1, w5_g, w31_g = exchange("gather_first", [bf(ffn1_gate), ssm_conv_w[0], cconv_w[0]], "chips")
    plan.on("ffn1_gate", "wu1", gather(bf(ffn1_up)))
    plan.on("ffn1_up", "wd1", gather(bf(ffn1_down)))
    plan.on("ffn1_act", "win", gather(bf(w_in)))
    plan.on("ffn1_down", "wout_wg2", gather(bf(w_out), bf(ffn2_gate)))
    xt = jnp.concatenate([x.reshape(n_lat, d), ctx.reshape(n_ctx_rows, d)])
    x1, saved1 = _ffn_fwd(plan, "ffn1", xt, n_tok, tm, seg_all, tabs[0], tabs[1], tabs[2], norm_ffn1, wg1,
                          lambda: plan.got["wu1"][0], lambda: plan.got["wd1"][0])
    (wu1,), (wd1,), (win_g,), (wout_g, wg2) = (plan.got[k] for k in ("wu1", "wd1", "win", "wout_wg2"))
    unshard_cols = lambda t: t.transpose(1, 0, 2).reshape(t.shape[1], N_CHIPS * t.shape[2])
    win = unshard_cols(win_g)
    o_x, o_dt, o_glu = ds, ds + xw, ds + xw + 2 * n_head
    w_z, w_xbc, w_dt = win[:, :ds], win[:, o_x:o_dt], win[:, o_dt:o_glu]
    w_ga, w_gb = win[:, o_glu:o_glu + d], win[:, o_glu + d:]
    w_dtp = jnp.concatenate([w_dt, jnp.zeros((d, LANES - 2 * n_head), BF16)], axis=1)
    w_cat = jnp.concatenate([w_z, w_ga, w_gb, w_xbc, w_dtp], axis=1)
    cbw = d // 2
    xbc_cb, dt_cb = 3 * d // cbw, (3 * d + xw) // LANES
    wout = wout_g.reshape(2 * d, d)
    wo_y, wo_u = wout[:ds], wout[ds:]
    w5, w31 = unshard_cols(w5_g), unshard_cols(w31_g)
    pad_vec = lambda v: jnp.concatenate([v.reshape(1, -1), jnp.zeros((1, LANES - v.size), F32)], axis=1)
    dtb_f, dtb_b, alog_f, alog_b = map(pad_vec, (dt_bias_fwd, dt_bias_bwd, a_log_fwd, a_log_bwd))
    dsk_f, dsk_b = pad_vec(ssm_d), jnp.zeros((1, LANES), F32)

    (h2,) = rowwise("mix_norm", fn_norm_mod, [row(x1)], [tabs[3], tabs[4]], [norm_mix], [(n_tok, d, BF16)],
                    tm=tm, n_tiles=n_tok // tm, seg_fn=seg_all)
    proj, (wu2,) = matmul("mix_proj", [(h2, w_cat)], "nn", tm=min(tm, 256), rider=Rider([bf(ffn2_up)], "chips"))
    c5 = lambda name, src, cb0, flip, seq, off: tapsum_roll(
        name, src, cb0, w5, 0, seq_len=seq, n_seq=n_ex, row_blk_off=off, width=seq, piece=seq, cb=cbw, ncb=xw // cbw,
        pad=w5.shape[0] // 2, flip=flip)
    craw = jnp.concatenate([c5("xbc_conv_lat", proj, xbc_cb, False, seq_len, 0),
                            c5("xbc_conv_ctx", proj, xbc_cb, False, ctx_len, n_lat // ctx_len)])
    (xbc,) = rowwise("xbc_silu", fn_silu_bias, [row(craw)], [], [ssm_conv_b], [(n_tok, xw, F32)], tm=tm, n_tiles=n_tok // tm)
    ssd = dict(n_ex=n_ex, seq_len=seq_len, ctx_len=ctx_len, ds=ds)
    (y_f, hs_f), (wd2,) = ssd_fwd("ssd_fwd_f", xbc, proj, dt_cb, dtb_f, alog_f, dsk_f, rev=False,
                                  rider=Rider([bf(ffn2_down)], "chips"), **ssd)
    y_b, hs_b = ssd_fwd("ssd_fwd_b", xbc, proj, dt_cb, dtb_b, alog_b, dsk_b, rev=True, **ssd)
    fn_gate = make_fn_gate_groupnorm(ds)
    (yn,) = rowwise("ssd_gate", fn_gate, [row(y_f), row(y_b), row(proj, d, 0)], [], [ssm_norm_w], [(n_lat, ds, BF16)],
                    tm=tm, n_tiles=lat_tiles)
    (u0,) = rowwise("glu", fn_glu, [row(proj, d, 1), row(proj, d, 2)], [], [], [(n_lat, d, F32)], tm=tm, n_tiles=lat_tiles)
    cb31 = max(LANES, d // 4)
    ncb31 = (d // 2) // cb31
    pad31 = w31.shape[0] // 2
    piece31 = min(seq_len, 4 * GRID_W)
    v_w = tapsum_roll("cconv_cols", u0, 0, w31, 0, seq_len=seq_len, n_seq=n_ex, row_blk_off=0, width=GRID_W,
                      piece=piece31, cb=cb31, ncb=ncb31, pad=pad31, flip=False)
    v_h = tapsum_rows("cconv_rows", u0, ncb31, w31, ncb31, seq_len=seq_len, n_seq=n_ex, cb=cb31, ncb=ncb31, pad=pad31, flip=False)
    (un,) = rowwise("cconv_ln", fn_ln_silu, [row(v_w), row(v_h)], [], [cconv_b, cconv_ln_w, cconv_ln_b], [(n_lat, d, BF16)],
                    tm=tm, n_tiles=lat_tiles)
    mix = matmul("mix_out", [(yn, wo_y), (un, wo_u)], "nn", tm=tm)
    seg_lat, first_lat = _segmenter(tm, seq_len, n_lat)
    (x2,) = rowwise("mix_resid", make_fn_resid(1.0), [row(x1), row(mix)], [lat(tabs[5])], [], [(n_lat, d, F32)],
                    tm=tm, n_tiles=lat_tiles, seg_fn=seg_lat)
    x3, saved2 = _ffn_fwd(plan, "ffn2", x2, n_lat, tm, seg_lat, lat(tabs[6]), lat(tabs[7]), lat(tabs[8]), norm_ffn2, wg2, wu2, wd2)
    d_x3, d_final, loss_vec = final_loss(x3, loss_target.reshape(n_lat, d), final_norm.reshape(1, d), tm=tm)

    shard_cols = lambda t: t.reshape(t.shape[0], N_CHIPS, -1).transpose(1, 0, 2)

    def pieces(t):
        t = jnp.pad(t, ((0, 0), (0, t.shape[1] % 2), (0, 0)))
        return t.reshape(2 * N_CHIPS, t.shape[1] // 2, t.shape[2]).astype(BF16)

    scatter = lambda *ts: Rider([pieces(t) for t in ts], "all8", scatter=True)
    halves = lambda names, landed: Rider([sum_slots(f"sum_{nm}", r) for nm, r in zip(names, landed)], "sibling")
    swapped = {}
    plan.on("ffn2_act_bwd", "sc_ffn2_down", lambda p: scatter(p.got["ffn2_d_wd"]))
    plan.on("ffn2_up_dw", "sc_ffn2_gate", lambda p: scatter(p.got["ffn2_d_wg"]))
    d_x2, (d_s6, d_s7, d_g8), d_nffn2 = _ffn_bwd(
        plan, "ffn2", d_x3, saved2, x2, n_lat, tm, seg_lat, first_lat, lat(tabs[6]), lat(tabs[7]), lat(tabs[8]), norm_ffn2,
        wg2, wu2, wd2, n_lat, None)
    d_mix, d_g5 = rowwise_bwd("mix_resid_bwd", make_fn_resid(1.0), [row(x1), row(mix)], [lat(tabs[5])], [], [[row(d_x2)]],
                              [None, (n_lat, BF16, None)], tm=tm, n_tiles=lat_tiles, seg_fn=seg_lat, first_fn=first_lat)
    d_yn = matmul("mix_out_dy", [(d_mix, wo_y)], "nt", tm=tm)
    d_un = matmul("mix_out_du", [(d_mix, wo_u)], "nt", tm=tm)
    d_wout = jnp.concatenate([matmul("mix_out_dwy", [(yn, d_mix)], "tn", out_dtype=BF16, tm=tm), matmul("mix_out_dwu", [(un, d_mix)], "tn", out_dtype=BF16, tm=tm)])
    d_vw, d_vh, d_cb, d_lnw, d_lnb = rowwise_bwd(
        "cconv_ln_bwd", fn_ln_silu, [row(v_w), row(v_h)], [], [cconv_b, cconv_ln_w, cconv_ln_b], [[row(d_un)]],
        [(n_lat, F32, None)] * 2, tm=tm, n_tiles=lat_tiles)
    d_u0w = tapsum_roll("cconv_cols_dx", d_vw, 0, w31, 0, seq_len=seq_len, n_seq=n_ex, row_blk_off=0, width=GRID_W,
                        piece=piece31, cb=cb31, ncb=ncb31, pad=pad31, flip=True)
    d_u0h = tapsum_rows("cconv_rows_dx", d_vh, 0, w31, ncb31, seq_len=seq_len, n_seq=n_ex, cb=cb31, ncb=ncb31, pad=pad31, flip=True)
    d_w31 = jnp.concatenate([
        tapgrad_roll("cconv_cols_dw", d_vw, 0, 0, u0, 0, 0, n_tap=w31.shape[0], seq_len=seq_len, n_seq=n_ex, width=GRID_W,
                     piece=piece31, cb=cb31, ncb=ncb31, pad=pad31),
        tapgrad_rows("cconv_rows_dw", d_vh, 0, u0, ncb31, n_tap=w31.shape[0], seq_len=seq_len, n_seq=n_ex, cb=cb31,
                     ncb=ncb31, pad=pad31)], axis=1)
    d_u0 = jnp.concatenate([d_u0w, d_u0h], axis=1)
    d_ga, d_gb = rowwise_bwd("glu_bwd", fn_glu, [row(proj, d, 1), row(proj, d, 2)], [], [], [[row(d_u0)]],
                             [(n_lat, BF16, None)] * 2, tm=tm, n_tiles=lat_tiles)
    d_ysum, d_z, d_ssmnw = rowwise_bwd(
        "ssd_gate_bwd", fn_gate, [row(y_f), row(y_b), row(proj, d, 0)], [], [ssm_norm_w], [[row(d_yn)]],
        [(n_lat, F32, None), None, (n_lat, BF16, None)], tm=tm, n_tiles=lat_tiles)
    (dxbc_f, ddt_f, dalog_f, ddtb_f, ddsk), landed = ssd_bwd(
        "ssd_bwd_f", xbc, proj, dt_cb, hs_f, d_ysum, dtb_f, alog_f, dsk_f, rev=False,
        rider=scatter(plan.got["ffn2_d_wu"], d_wout.reshape(N_CHIPS, -1, d)), **ssd)
    (dxbc_b, ddt_b, dalog_b, ddtb_b, _), both = ssd_bwd(
        "ssd_bwd_b", xbc, proj, dt_cb, hs_b, d_ysum, dtb_b, alog_b, dsk_b, rev=True,
        rider=halves(["ffn2_down", "ffn2_gate"], plan.got["sc_ffn2_down"] + plan.got["sc_ffn2_gate"]), **ssd)
    swapped.update(zip(["ffn2_down", "ffn2_gate"], both))
    (d_craw, d_conv_b), both = rowwise_bwd(
        "xbc_silu_bwd", fn_silu_bias, [row(craw)], [], [ssm_conv_b], [[row(dxbc_f), row(dxbc_b)]],
        [(n_tok, F32, None)], tm=tm, n_tiles=n_tok // tm, rider=halves(["ffn2_up", "w_out"], landed))
    swapped.update(zip(["ffn2_up", "w_out"], both))
    d_pxbc = jnp.concatenate([c5("xbc_conv_lat_dx", d_craw, 0, True, seq_len, 0),
                              c5("xbc_conv_ctx_dx", d_craw, 0, True, ctx_len, n_lat // ctx_len)])
    g5 = lambda name, seq, off: tapgrad_roll(name, d_craw, 0, off, proj, xbc_cb, off, n_tap=w5.shape[0], seq_len=seq,
                                             n_seq=n_ex, width=seq, piece=seq, cb=cbw, ncb=xw // cbw, pad=w5.shape[0] // 2)
    d_w5 = g5("xbc_conv_lat_dw", seq_len, 0) + g5("xbc_conv_ctx_dw", ctx_len, n_lat // ctx_len)
    lat_pairs = [(d_z, w_z), (d_ga, w_ga), (d_gb, w_gb), (d_pxbc, w_xbc), (ddt_f, w_dtp), (ddt_b, w_dtp)]
    d_h2 = jnp.concatenate([matmul("mix_proj_dx_lat", lat_pairs, "nt", rows=n_lat, tm=min(tm, 256)),
                            matmul("mix_proj_dx_ctx", lat_pairs[3:], "nt", rows=n_ctx_rows, row_off=n_lat, tm=min(tm, 256))])
    d_wz = matmul("mix_proj_dwz", [(h2, d_z)], "tn", out_dtype=BF16, rows=n_lat, tm=tm)
    d_wga = matmul("mix_proj_dwa", [(h2, d_ga)], "tn", out_dtype=BF16, rows=n_lat, tm=tm)
    d_wgb = matmul("mix_proj_dwb", [(h2, d_gb)], "tn", out_dtype=BF16, rows=n_lat, tm=tm)
    d_wxbc = matmul("mix_proj_dwx", [(h2, d_pxbc)], "tn", out_dtype=BF16, tm=tm)
    d_wdt = matmul("mix_proj_dwt", [(h2, ddt_f), (h2, ddt_b)], "tn", out_dtype=BF16, tm=tm)
    d_win = jnp.concatenate([d_wz, d_wxbc, d_wdt[:, :2 * n_head], d_wga, d_wgb], axis=1)
    d_x1, d_s3, d_s4, d_nmix = rowwise_bwd(
        "mix_norm_bwd", fn_norm_mod, [row(x1)], [tabs[3], tabs[4]], [norm_mix], [[row(d_h2)]], [(n_tok, F32, None)],
        tm=tm, n_tiles=n_tok // tm, seg_fn=seg_all, first_fn=first_all, adds={0: (row(d_x2), lat_tiles)})
    mix_names = ["w_in", "ssm_conv_w", "cconv_w"]
    plan.on("ffn1_down_dx", "sc_mix", lambda p: scatter(shard_cols(d_win), shard_cols(d_w5), shard_cols(d_w31)))
    plan.on("ffn1_act_bwd", "sc_ffn1_down", lambda p: scatter(p.got["ffn1_d_wd"]))
    plan.on("ffn1_up_dx", "sw_mix", lambda p: halves(mix_names, p.got["sc_mix"]))
    plan.on("ffn1_gate_dw", "sw_ffn1_down", lambda p: halves(["ffn1_down"], p.got["sc_ffn1_down"]))
    plan.on("ffn1_up_dw", "sc_ffn1_gate", lambda p: scatter(p.got["ffn1_d_wg"]))
    plan.on("ffn1_norm_bwd", "sc_ffn1_up", lambda p: scatter(p.got["ffn1_d_wu"]))
    d_xt, (d_s0, d_s1, d_g2), d_nffn1 = _ffn_bwd(
        plan, "ffn1", d_x1, saved1, xt, n_tok, tm, seg_all, first_all, tabs[0], tabs[1], tabs[2], norm_ffn1, wg1, wu1, wd1,
        n_lat, lat_tiles)
    swapped.update(zip(mix_names + ["ffn1_down"], plan.got["sw_mix"] + plan.got["sw_ffn1_down"]))
    last_names = ["ffn1_gate", "ffn1_up"]
    last = halves(last_names, plan.got["sc_ffn1_gate"] + plan.got["sc_ffn1_up"])
    swapped.update(zip(last_names, exchange("swap_sibling", last.arrs, "sibling")))
    grad_x = d_xt.reshape(n_ex, seq_len, d)

    with_ctx0 = lambda t: jnp.concatenate([t, jnp.zeros((1, 1, d), F32)])
    d_tabs = [d_s0, d_s1, d_g2, d_s3, d_s4, with_ctx0(d_g5), with_ctx0(d_s6), with_ctx0(d_s7), with_ctx0(d_g8)]
    d_mod_rows = jnp.concatenate([t[:, 0, :] for t in d_tabs], axis=1)
    n_pad_rows = -(-(n_ex + 1) // 8) * 8
    d_mod_rows = jnp.concatenate([d_mod_rows, jnp.zeros((n_pad_rows - n_ex - 1, 9 * d), F32)])
    small = [("loss", loss_vec), ("norm_ffn1", d_nffn1), ("norm_mix", d_nmix), ("ssm_conv_b", d_conv_b),
             ("dt_bias_fwd", ddtb_f[:, :n_head]), ("dt_bias_bwd", ddtb_b[:, :n_head]), ("a_log_fwd", dalog_f[:, :n_head]),
             ("a_log_bwd", dalog_b[:, :n_head]), ("ssm_d", ddsk[:, :n_head]), ("ssm_norm_w", d_ssmnw), ("cconv_b", d_cb),
             ("cconv_ln_w", d_lnw), ("cconv_ln_b", d_lnb), ("norm_ffn2", d_nffn2), ("final_norm", d_final)]
    n_small = sum(v.size for _, v in small)
    n_pack = -(-n_small // (8 * LANES)) * (8 * LANES)
    pack = jnp.concatenate([v.reshape(-1) for _, v in small] + [jnp.zeros((n_pack - n_small,), F32)]).reshape(-1, LANES)
    pack_all, d_mod_all = exchange("gather_small", [pack, d_mod_rows], "all8")
    pack_sum = sum_slots("small_sum", pack_all)
    loss = loss_total(pack_sum.reshape(1, n_pack), d).reshape(())
    flat_sum = pack_sum.reshape(-1)
    small_grads, pos = {}, 0
    for nm, v in small:
        small_grads[nm] = flat_sum[pos:pos + v.size]
        pos += v.size
    d_mod_all = d_mod_all.reshape(8 * n_pad_rows, 9 * d)
    cond_rows = [jnp.concatenate([cond[j * n_ex:(j + 1) * n_ex], c_ctx[None, :],
                                  jnp.zeros((n_pad_rows - n_ex - 1, d), F32)]) for j in range(8)]
    cond_bwd = jnp.concatenate(cond_rows)
    d_mod_shard = lax.dynamic_slice(d_mod_all, (0, chip * mod_w), (8 * n_pad_rows, mod_w))
    g_wmod, g_bmod, q_part = mod_bwd(cond_bwd, d_mod_shard, d_mod_all, w_mod[0],
                                     tuple(j * n_pad_rows + n_ex for j in range(8)))
    (q_all,) = exchange("gather_cctx", [q_part], "all8")
    g_cctx = cctx_grad(q_all, c_ctx.reshape(1, d))
    small_grads["c_ctx"], small_grads["b_mod"] = g_cctx.reshape(-1), g_bmod.reshape(-1)

    results = {}
    for nm, both in swapped.items():
        shape = weights[nm].shape
        two_d = lambda t: t.reshape(shape[-2], shape[-1])
        g_full = both.reshape(1, -1, shape[-1])[:, :shape[-2]]
        results[nm] = [r.reshape(shape) for r in adamw(f"adamw_{nm}", two_d(weights[nm]), g_full, two_d(mom1[nm]), two_d(mom2[nm]))]
    results["w_mod"] = [r.reshape(w_mod.shape) for r in adamw("adamw_w_mod", w_mod[0], g_wmod[None], m_w_mod[0], v_w_mod[0])]
    small_names = [nm for nm in order if nm not in results]
    n_sm = sum(weights[nm].size for nm in small_names)
    n_smp = -(-n_sm // (8 * LANES)) * (8 * LANES)
    packed = lambda src: jnp.concatenate([src[nm].reshape(-1) for nm in small_names] + [jnp.zeros((n_smp - n_sm,), F32)]).reshape(-1, LANES)
    sm_out = adamw("adamw_small", packed(weights), packed(small_grads)[None], packed(mom1), packed(mom2))
    pos = 0
    for nm in small_names:
        size = weights[nm].size
        results[nm] = [r.reshape(-1)[pos:pos + size].reshape(weights[nm].shape) for r in sm_out]
        pos += size
    return (loss, grad_x, *[results[nm][0] for nm in order], *[results[nm][1] for nm in order],
            *[results[nm][2] for nm in order], *[results[nm][3] for nm in order])
```

```python
import functools
import math

import jax
import jax.numpy as jnp
from jax import lax
from jax.experimental import pallas as pl
from jax.experimental.pallas import tpu as pltpu

F32 = jnp.float32
BF16 = jnp.bfloat16
HI = lax.Precision.HIGHEST
MESH = pl.DeviceIdType.MESH

EPS = 1e-6
GRID_W = 64
HEAD_DIM = 64
N_STATE = 128
CHUNK = 128
LANES = 128
N_CHIPS = 4
ADAM_LR, ADAM_B1, ADAM_B2, ADAM_EPS, ADAM_WD, ADAM_STEP = 0.001, 0.9, 0.999, 1e-08, 0.01, 10
VMEM_CAP = 56 * 1024 * 1024


def _params(vmem_bytes=None, n_axes=1):
    kw = dict(dimension_semantics=("arbitrary",) * n_axes)
    if vmem_bytes is not None:
        kw["vmem_limit_bytes"] = int(min(VMEM_CAP, max(32 * 1024 * 1024, vmem_bytes)))
    return pltpu.CompilerParams(**kw)


def _nbytes(shape, dtype):
    return math.prod(shape) * jnp.dtype(dtype).itemsize


def _row_tile(rows, width, cap_bytes=1 << 20, mult=8):
    best = None
    for t in range(mult, rows + 1, mult):
        if rows % t == 0 and t * width * 4 <= cap_bytes:
            best = t
    return best if best is not None else rows


_MODES = {"all8": (8, (1, 2, 3, 4, 5, 6, 7), 0), "chips": (4, (2, 4, 6), 1), "sibling": (2, (1,), 0)}


class Rider:
    def __init__(self, arrs, mode, scatter=False):
        self.arrs, self.scatter = list(arrs), scatter
        self.nslot, self.deltas, self.shift = _MODES[mode]
        self.n = len(self.arrs)
        self.out_shape = [jax.ShapeDtypeStruct((self.nslot,) + (a.shape[1:] if scatter else a.shape), a.dtype)
                          for a in self.arrs]
        any_spec = pl.BlockSpec(memory_space=pl.ANY)
        self.in_specs = [any_spec] * self.n
        self.out_specs = [any_spec] * self.n
        n_peer = len(self.deltas)
        self.scratch = [pltpu.SemaphoreType.DMA((self.n, n_peer)), pltpu.SemaphoreType.DMA((self.n, n_peer)),
                        pltpu.SemaphoreType.DMA((self.n,))]

    def _copies(self, ins, outs, sems, arrivals):
        send_sems, recv_sems, local_sems = sems
        x, y, c = lax.axis_index("x"), lax.axis_index("y"), lax.axis_index("c")
        me = 4 * x + 2 * y + c
        slot_of = lambda dev: (dev >> self.shift) & (self.nslot - 1)
        src = lambda a, slot: ins[a].at[slot] if self.scatter else ins[a]
        flip = lambda v, bit: 1 - v if bit else v

        def remote(a, k, d, from_slot, to_slot):
            return pltpu.make_async_remote_copy(
                src_ref=src(a, from_slot), dst_ref=outs[a].at[to_slot], send_sem=send_sems.at[a, k],
                recv_sem=recv_sems.at[a, k], device_id=(flip(x, (d >> 2) & 1), flip(y, (d >> 1) & 1), flip(c, d & 1)),
                device_id_type=MESH)

        mine = slot_of(me)
        local = [pltpu.make_async_copy(src(a, mine), outs[a].at[mine], local_sems.at[a]) for a in range(self.n)]
        sends = [remote(a, k, d, slot_of(me ^ d), mine) for k, d in enumerate(self.deltas) for a in range(self.n)]
        if not arrivals:
            return local, sends
        return local, sends, [remote(a, k, d, mine, slot_of(me ^ d)) for k, d in enumerate(self.deltas) for a in range(self.n)]

    def start(self, ins, outs, sems):
        local, sends = self._copies(ins, outs, sems, arrivals=False)
        for cp in local + sends:
            cp.start()

    def wait(self, ins, outs, sems):
        local, sends, recvs = self._copies(ins, outs, sems, arrivals=True)
        for cp in recvs:
            cp.wait_recv()
        for cp in sends:
            cp.wait_send()
        for cp in local:
            cp.wait()


class Riders:
    def __init__(self, riders):
        self.riders = list(riders)
        self.n = sum(r.n for r in self.riders)
        cat = lambda attr: [v for r in self.riders for v in getattr(r, attr)]
        self.arrs, self.out_shape, self.in_specs = cat("arrs"), cat("out_shape"), cat("in_specs")
        self.out_specs, self.scratch = cat("out_specs"), cat("scratch")

    def _each(self, method, ins, outs, sems):
        i = s = 0
        for r in self.riders:
            getattr(r, method)(ins[i:i + r.n], outs[i:i + r.n], sems[s:s + len(r.scratch)])
            i, s = i + r.n, s + len(r.scratch)

    def start(self, ins, outs, sems):
        self._each("start", ins, outs, sems)

    def wait(self, ins, outs, sems):
        self._each("wait", ins, outs, sems)


class _Hosted:
    def __init__(self, rider, n_in, n_out, n_scratch, grid):
        self.rider, self.n_in, self.n_out, self.n_scratch, self.grid = rider, n_in, n_out, n_scratch, grid
        self.n = rider.n if rider else 0

    def split(self, refs):
        a, b = self.n_in, self.n_in + self.n
        c, e = b + self.n_out, b + self.n_out + self.n
        self._r = (refs[a:b], refs[c:e], refs[e + self.n_scratch:])
        if self.rider:
            ids = [pl.program_id(ax) for ax in range(len(self.grid))]
            first = functools.reduce(jnp.logical_and, [i == 0 for i in ids]) if ids else True
            pl.when(first)(lambda: self.rider.start(*self._r))
        return refs[:a], refs[b:c], refs[e:e + self.n_scratch]

    def finish(self):
        if self.rider:
            ids = [pl.program_id(ax) for ax in range(len(self.grid))]
            last = functools.reduce(jnp.logical_and, [i == n - 1 for i, n in zip(ids, self.grid)]) if ids else True
            pl.when(last)(lambda: self.rider.wait(*self._r))

    def call_args(self, in_specs, out_shape, out_specs, scratch, args):
        r = self.rider
        if not r:
            return list(in_specs), tuple(out_shape), tuple(out_specs), list(scratch), list(args)
        return (list(in_specs) + r.in_specs, tuple(out_shape) + tuple(r.out_shape), tuple(out_specs) + tuple(r.out_specs),
                list(scratch) + r.scratch, list(args) + r.arrs)

    def results(self, res, unwrap=True):
        res = list(res) if isinstance(res, (tuple, list)) else [res]
        host = res[:self.n_out]
        host = host[0] if (self.n_out == 1 and unwrap) else tuple(host)
        return (host, res[self.n_out:]) if self.rider else host


def exchange(name, arrs, mode, scatter=False):
    rider = Rider(arrs, mode, scatter)

    def body(*refs):
        ins, outs, sems = refs[:rider.n], refs[rider.n:2 * rider.n], refs[2 * rider.n:]
        rider.start(ins, outs, sems)
        rider.wait(ins, outs, sems)

    return pl.pallas_call(
        body, name=name, out_shape=tuple(rider.out_shape), in_specs=rider.in_specs, out_specs=tuple(rider.out_specs),
        scratch_shapes=rider.scratch,
    )(*arrs)


_DIMS = {"nn": (((1,), (0,)), ((), ())), "nt": (((1,), (1,)), ((), ())), "tn": (((0,), (0,)), ((), ()))}


def matmul(name, pairs, kind, *, a_ch=False, b_ch=False, out_ch=False, out_dtype=F32, rows=None, row_off=0, tm=512,
           rider=None, post=None):
    a0, b0 = pairs[0]
    n_chunk = a0.shape[0] if a_ch else (b0.shape[0] if b_ch else 1)
    total_rows = a0.shape[-2]
    rows = total_rows - row_off if rows is None else rows
    tm = min(tm, rows)
    assert rows % tm == 0 and row_off % tm == 0, (name, rows, tm, row_off)
    n_rt, off = rows // tm, row_off // tm
    dims = _DIMS[kind]
    n_pair = len(pairs)

    if kind == "tn":
        grid, red_axis, n_red = (n_chunk, n_rt), 1, n_rt
        a_idx = (lambda k, i: (k, i + off, 0)) if a_ch else (lambda k, i: (i + off, 0))
        b_idx = (lambda k, i: (k, i + off, 0)) if b_ch else (lambda k, i: (i + off, 0))
        a_blk = lambda a: ((None, tm, a.shape[-1]) if a_ch else (tm, a.shape[-1]))
        b_blk = lambda b: ((None, tm, b.shape[-1]) if b_ch else (tm, b.shape[-1]))
        o2 = (a0.shape[-1], b0.shape[-1])
        out_shape = ((n_chunk,) + o2) if out_ch else o2
        out_spec = pl.BlockSpec((None,) + o2, lambda k, i: (k, 0, 0)) if out_ch else pl.BlockSpec(o2, lambda k, i: (0, 0))
        acc_shape = o2
    else:
        n_out = b0.shape[-1] if kind == "nn" else b0.shape[-2]
        b2 = b0.shape[-2:]
        if a_ch and b_ch and not out_ch:
            grid, red_axis, n_red = (n_rt, n_chunk), 1, n_chunk
            a_idx, b_idx = (lambda i, k: (k, i + off, 0)), (lambda i, k: (k, 0, 0))
            a_blk = lambda a: (None, tm, a.shape[-1])
            b_blk = lambda b: (None,) + tuple(b.shape[-2:])
            out_shape, out_spec = (rows, n_out), pl.BlockSpec((tm, n_out), lambda i, k: (i, 0))
        elif out_ch:
            assert b_ch and not a_ch
            grid, red_axis, n_red = (n_chunk, n_rt), None, 1
            a_idx, b_idx = (lambda k, i: (i + off, 0)), (lambda k, i: (k, 0, 0))
            a_blk = lambda a: (tm, a.shape[-1])
            b_blk = lambda b: (None,) + tuple(b.shape[-2:])
            out_shape, out_spec = (n_chunk, rows, n_out), pl.BlockSpec((None, tm, n_out), lambda k, i: (k, i, 0))
        else:
            assert not (a_ch or b_ch)
            grid, red_axis, n_red = (n_rt,), None, 1
            a_idx, b_idx = (lambda i: (i + off, 0)), (lambda i: (0, 0))
            a_blk = lambda a: (tm, a.shape[-1])
            b_blk = lambda b: tuple(b.shape)
            out_shape, out_spec = (rows, n_out), pl.BlockSpec((tm, n_out), lambda i: (i, 0))
        acc_shape = (tm, n_out)

    post_ins, post_fn, out_dtypes = ([], None, [out_dtype]) if post is None else post
    hosted = _Hosted(rider, 2 * n_pair + len(post_ins), len(out_dtypes), int(n_red > 1), grid)

    def body(*refs):
        ins, outs, scr = hosted.split(refs)

        def compute():
            acc = None
            for p in range(n_pair):
                d = lax.dot_general(ins[2 * p][...].astype(BF16), ins[2 * p + 1][...].astype(BF16), dims,
                                    preferred_element_type=F32)
                acc = d if acc is None else acc + d
            return acc

        def emit(acc):
            vals = (acc,) if post_fn is None else post_fn(acc, *[r[...].astype(F32) for r in ins[2 * n_pair:]])
            for o_ref, v in zip(outs, vals):
                o_ref[...] = v.astype(o_ref.dtype)

        if n_red == 1:
            emit(compute())
        else:
            acc_ref = scr[0]
            r = pl.program_id(red_axis)

            @pl.when(r == 0)
            def _():
                acc_ref[...] = jnp.zeros_like(acc_ref)

            acc_ref[...] += compute()

            @pl.when(r == n_red - 1)
            def _():
                emit(acc_ref[...])
        hosted.finish()

    in_specs, args, vmem = [], [], 0
    for a, b in pairs:
        in_specs += [pl.BlockSpec(a_blk(a), a_idx), pl.BlockSpec(b_blk(b), b_idx)]
        args += [a, b]
        vmem += 2 * (_nbytes([s for s in a_blk(a) if s], a.dtype) + _nbytes([s for s in b_blk(b) if s], b.dtype))
    in_specs += [out_spec] * len(post_ins)
    args += list(post_ins)
    vmem += (3 + 2 * n_pair + 2 * len(post_ins) + 2 * len(out_dtypes)) * _nbytes(acc_shape, F32)
    scratch = [pltpu.VMEM(acc_shape, F32)] if n_red > 1 else []
    in_specs, out_shapes, out_specs, scratch, args = hosted.call_args(
        in_specs, [jax.ShapeDtypeStruct(out_shape, dt) for dt in out_dtypes], [out_spec] * len(out_dtypes), scratch, args)
    return hosted.results(pl.pallas_call(
        body, name=name, out_shape=out_shapes, grid=grid, in_specs=in_specs, out_specs=out_specs,
        scratch_shapes=scratch, compiler_params=_params(vmem + (8 << 20), len(grid)),
    )(*args))


def row(arr, width=None, cb=0, roff=0):
    return (arr, arr.shape[-1] if width is None else width, cb, roff)


def _row_spec(desc, tm, limit=None):
    _, width, cb, roff = desc
    if limit is None:
        return pl.BlockSpec((tm, width), lambda i: (i + roff, cb))
    return pl.BlockSpec((tm, width), lambda i: (jnp.minimum(i, limit - 1) + roff, cb))


def _segmenter(tm, seq_len, n_lat):
    seg = lambda i: jnp.where(i * tm < n_lat, (i * tm) // seq_len, n_lat // seq_len)
    first = lambda i: jnp.where(i * tm < n_lat, (i * tm) % seq_len == 0, i * tm == n_lat)
    return seg, first


def rowwise(name, fn, rows, segs, params, outs, *, tm, n_tiles, seg_fn=None, rider=None):
    n_r, n_s, n_p = len(rows), len(segs), len(params)
    hosted = _Hosted(rider, n_r + n_s + n_p, len(outs), 0, (n_tiles,))

    def body(*refs):
        ins, out_refs, _ = hosted.split(refs)
        vals = [r[...].astype(F32) for r in ins[:n_r]] + [r[...] for r in ins[n_r:]]
        res = fn(*vals)
        for o_ref, v in zip(out_refs, res):
            o_ref[...] = v.astype(o_ref.dtype)
        hosted.finish()

    in_specs = [_row_spec(d, tm) for d in rows]
    in_specs += [pl.BlockSpec((None, 1, s.shape[-1]), lambda i: (seg_fn(i), 0, 0)) for s in segs]
    in_specs += [pl.BlockSpec(p.shape, lambda i: (0, 0)) for p in params]
    vmem = sum(2 * tm * d[1] * 4 for d in rows) + sum(3 * tm * w * 4 for _, w, _ in outs) + sum(2 * p.size * 4 for p in params)
    in_specs, out_shapes, out_specs, scratch, args = hosted.call_args(
        in_specs, [jax.ShapeDtypeStruct((r, w), dt) for r, w, dt in outs],
        [pl.BlockSpec((tm, w), lambda i: (i, 0)) for _, w, _ in outs], [], [d[0] for d in rows] + list(segs) + list(params))
    return hosted.results(pl.pallas_call(
        body, name=name, grid=(n_tiles,), in_specs=in_specs, out_shape=out_shapes, out_specs=out_specs,
        scratch_shapes=scratch, compiler_params=_params(2 * vmem + (8 << 20)),
    )(*args), unwrap=False)


def rowwise_bwd(name, fn, rows, segs, params, cts, row_grads, *, tm, n_tiles, seg_fn=None, first_fn=None, adds=None,
                rider=None):
    adds = adds or {}
    need = [k for k, v in enumerate(row_grads) if v is not None]
    n_r, n_s, n_p = len(rows), len(segs), len(params)
    n_ct = sum(len(lst) for lst in cts)
    add_keys = sorted(adds)
    hosted = _Hosted(rider, n_r + n_s + n_p + n_ct + len(add_keys), len(need) + n_s + n_p, 0, (n_tiles,))

    def body(*refs):
        host_in, host_out, _ = hosted.split(refs)
        it = iter(list(host_in) + list(host_out))
        row_refs = [next(it) for _ in range(n_r)]
        seg_refs = [next(it) for _ in range(n_s)]
        par_refs = [next(it) for _ in range(n_p)]
        ct_refs = [[next(it) for _ in lst] for lst in cts]
        add_refs = {k: next(it) for k in add_keys}
        rg_refs = {k: next(it) for k in need}
        sg_refs = [next(it) for _ in range(n_s)]
        pg_refs = [next(it) for _ in range(n_p)]
        i = pl.program_id(0)
        rv = [r[...].astype(F32) for r in row_refs]
        sv = [r[...] for r in seg_refs]
        pv = [r[...] for r in par_refs]

        def f(*args):
            rr = list(rv)
            for j, k in enumerate(need):
                rr[k] = args[j]
            return fn(*rr, *args[len(need):])

        _, vjp = jax.vjp(f, *[rv[k] for k in need], *sv, *pv)
        ctv = []
        for lst in ct_refs:
            acc = lst[0][...].astype(F32)
            for r in lst[1:]:
                acc = acc + r[...].astype(F32)
            ctv.append(acc)
        g = vjp(tuple(ctv))
        for j, k in enumerate(need):
            gv = g[j]
            if k in adds:
                lim = adds[k][1]
                av = add_refs[k][...].astype(F32)
                gv = gv + (av if lim is None else jnp.where(i < lim, av, 0.0))
            lim = row_grads[k][2]
            if lim is None:
                rg_refs[k][...] = gv.astype(rg_refs[k].dtype)
            else:
                @pl.when(i < lim)
                def _(gv=gv, k=k):
                    rg_refs[k][...] = gv.astype(rg_refs[k].dtype)
        if n_s:
            opens = first_fn(i)
            for ref, gv in zip(sg_refs, g[len(need):len(need) + n_s]):
                @pl.when(opens)
                def _(ref=ref, gv=gv):
                    ref[...] = gv

                @pl.when(jnp.logical_not(opens))
                def _(ref=ref, gv=gv):
                    ref[...] += gv
        for ref, gv in zip(pg_refs, g[len(need) + n_s:]):
            @pl.when(i == 0)
            def _(ref=ref, gv=gv):
                ref[...] = gv

            @pl.when(i > 0)
            def _(ref=ref, gv=gv):
                ref[...] += gv
        hosted.finish()

    seg_spec = lambda s: pl.BlockSpec((None, 1, s.shape[-1]), lambda i: (seg_fn(i), 0, 0))
    par_spec = lambda p: pl.BlockSpec(p.shape, lambda i: (0, 0))
    in_specs = [_row_spec(d, tm) for d in rows] + [seg_spec(s) for s in segs] + [par_spec(p) for p in params]
    args = [d[0] for d in rows] + list(segs) + list(params)
    for lst in cts:
        in_specs += [_row_spec(d, tm) for d in lst]
        args += [d[0] for d in lst]
    for k in add_keys:
        in_specs.append(_row_spec(adds[k][0], tm, adds[k][1]))
        args.append(adds[k][0][0])
    out_shape, out_specs = [], []
    for k in need:
        n_rows, dt, lim = row_grads[k]
        out_shape.append(jax.ShapeDtypeStruct((n_rows, rows[k][1]), dt))
        out_specs.append(_row_spec((None, rows[k][1], 0, 0), tm, lim))
    for s in segs:
        out_shape.append(jax.ShapeDtypeStruct(s.shape, F32))
        out_specs.append(seg_spec(s))
    for p in params:
        out_shape.append(jax.ShapeDtypeStruct(p.shape, F32))
        out_specs.append(par_spec(p))
    vmem = sum(tm * d[1] * 4 for d in rows) * 6 + n_ct * tm * max(d[1] for d in rows) * 8
    in_specs, out_shape, out_specs, scratch, args = hosted.call_args(in_specs, out_shape, out_specs, [], args)
    return hosted.results(pl.pallas_call(
        body, name=name, grid=(n_tiles,), in_specs=in_specs, out_shape=out_shape, out_specs=out_specs,
        scratch_shapes=scratch, compiler_params=_params(vmem + (8 << 20)),
    )(*args), unwrap=False)


def _silu(v):
    return v * jax.nn.sigmoid(v)


def _rms(v, w):
    return v * lax.rsqrt(jnp.mean(v * v, axis=-1, keepdims=True) + EPS) * w


def fn_norm_mod(x, shift, scale, w):
    return (_rms(x, w) * (1.0 + scale) + shift,)


def fn_act(g, u):
    return (_silu(g) * u,)


def make_fn_resid(coef):
    def fn(x, f, gate):
        return (x + coef * gate * f,)
    return fn


def fn_silu_bias(v, b):
    return (_silu(v + b),)


def make_fn_gate_groupnorm(width):
    half = width // 2

    def fn(yf, yb, z, w):
        y = (yf + yb) * _silu(z)
        lane = lax.broadcasted_iota(jnp.int32, y.shape, 1)
        lo = lane < half
        sq = y * y
        s_lo = jnp.sum(jnp.where(lo, sq, 0.0), axis=-1, keepdims=True)
        s_hi = jnp.sum(jnp.where(lo, 0.0, sq), axis=-1, keepdims=True)
        r = jnp.where(lo, lax.rsqrt(s_lo / half + EPS), lax.rsqrt(s_hi / half + EPS))
        return (y * r * w,)
    return fn


def fn_glu(a, b):
    return (a * jax.nn.sigmoid(b),)


def fn_ln_silu(vw, vh, cb, lw, lb):
    v = jnp.concatenate([vw, vh], axis=-1) + cb
    mu = jnp.mean(v, axis=-1, keepdims=True)
    var = jnp.mean(jnp.square(v - mu), axis=-1, keepdims=True)
    return (_silu((v - mu) * lax.rsqrt(var + EPS) * lw + lb),)


def _col_tile(width):
    return width // 3 if width % (3 * LANES) == 0 else width


def mod_fwd(a_rows, w_shard, b_shard):
    n, d = a_rows.shape
    ws = w_shard.shape[1]
    tn = _col_tile(ws)

    def body(a_ref, w_ref, b_ref, o_ref):
        a = _silu(a_ref[...]).astype(BF16)
        o_ref[...] = jnp.dot(a, w_ref[...].astype(BF16), preferred_element_type=F32) + b_ref[...]

    return pl.pallas_call(
        body, name="mod_fwd", grid=(ws // tn,), out_shape=jax.ShapeDtypeStruct((n, ws), F32),
        in_specs=[pl.BlockSpec((n, d), lambda j: (0, 0)), pl.BlockSpec((d, tn), lambda j: (0, j)),
                  pl.BlockSpec((1, tn), lambda j: (0, j))],
        out_specs=pl.BlockSpec((n, tn), lambda j: (0, j)), compiler_params=_params(),
    )(a_rows, w_shard, b_shard)


def mod_bwd(a_rows, d_shard, d_full, w_shard, ctx_rows):
    n, d = a_rows.shape
    ws = w_shard.shape[1]
    tn = _col_tile(ws)
    n_ct = ws // tn

    def body(a_ref, ds_ref, df_ref, w_ref, gw_ref, gb_ref, q_ref):
        j = pl.program_id(0)
        a = _silu(a_ref[...])
        ds = ds_ref[...]
        gw_ref[...] = lax.dot_general(a, ds, _DIMS["tn"], precision=HI, preferred_element_type=F32)
        dctx = ds[ctx_rows[0]:ctx_rows[0] + 1, :]
        for r in ctx_rows[1:]:
            dctx = dctx + ds[r:r + 1, :]
        q = lax.dot_general(jnp.broadcast_to(dctx, (8, tn)), w_ref[...], _DIMS["nt"], precision=HI,
                            preferred_element_type=F32)

        @pl.when(j == 0)
        def _():
            q_ref[...] = q
            df = df_ref[...]
            acc = df[0:1, :]
            for r in range(1, n):
                acc = acc + df[r:r + 1, :]
            gb_ref[...] = acc

        @pl.when(j > 0)
        def _():
            q_ref[...] += q

    return pl.pallas_call(
        body, name="mod_bwd", grid=(n_ct,),
        out_shape=(jax.ShapeDtypeStruct((d, ws), F32), jax.ShapeDtypeStruct((1, d_full.shape[1]), F32),
                   jax.ShapeDtypeStruct((8, d), F32)),
        in_specs=[pl.BlockSpec((n, d), lambda j: (0, 0)), pl.BlockSpec((n, tn), lambda j: (0, j)),
                  pl.BlockSpec(d_full.shape, lambda j: (0, 0)), pl.BlockSpec((d, tn), lambda j: (0, j))],
        out_specs=(pl.BlockSpec((d, tn), lambda j: (0, j)), pl.BlockSpec((1, d_full.shape[1]), lambda j: (0, 0)),
                   pl.BlockSpec((8, d), lambda j: (0, 0))),
        compiler_params=_params(40 << 20),
    )(a_rows, d_shard, d_full, w_shard)


def _shifted(xs, d, tok, width):
    if d == 0:
        return xs
    n = xs.shape[0]
    sh = pltpu.roll(xs, (-d) % n, axis=0)
    return jnp.where((tok + d >= 0) & (tok + d < width), sh, 0.0)


def tapsum_roll(name, x, xcb, w, wcb, *, seq_len, n_seq, row_blk_off, width, piece, cb, ncb, pad, flip):
    n_tap = w.shape[0]
    n_piece = seq_len // piece

    def body(x_ref, w_ref, o_ref):
        wv = w_ref[...]
        tok = lax.broadcasted_iota(jnp.int32, (piece, 1), 0) % width

        def do_piece(p, carry):
            start = pl.multiple_of(p * piece, piece)
            xs = x_ref[pl.ds(start, piece), :]
            acc = jnp.zeros_like(xs)
            for k in range(n_tap):
                d = pad - k if flip else k - pad
                acc = acc + wv[k:k + 1, :] * _shifted(xs, d, tok, width)
            o_ref[pl.ds(start, piece), :] = acc
            return carry

        lax.fori_loop(0, n_piece, do_piece, 0)

    return pl.pallas_call(
        body, name=name, grid=(ncb, n_seq), out_shape=jax.ShapeDtypeStruct((n_seq * seq_len, ncb * cb), F32),
        in_specs=[pl.BlockSpec((seq_len, cb), lambda j, s: (row_blk_off + s, xcb + j)),
                  pl.BlockSpec((n_tap, cb), lambda j, s: (0, wcb + j))],
        out_specs=pl.BlockSpec((seq_len, cb), lambda j, s: (s, j)),
        compiler_params=_params(8 * seq_len * cb * 4 + (8 << 20), 2),
    )(x, w)


def tapgrad_roll(name, dy, dycb, dy_blk_off, x, xcb, x_blk_off, *, n_tap, seq_len, n_seq, width, piece, cb, ncb, pad):
    n_piece = seq_len // piece

    def body(dy_ref, x_ref, o_ref):
        @pl.when(pl.program_id(1) == 0)
        def _():
            o_ref[...] = jnp.zeros_like(o_ref)

        tok = lax.broadcasted_iota(jnp.int32, (piece, 1), 0) % width

        def do_piece(p, carry):
            start = pl.multiple_of(p * piece, piece)
            xs = x_ref[pl.ds(start, piece), :]
            dv = dy_ref[pl.ds(start, piece), :]
            for k in range(n_tap):
                o_ref[k:k + 1, :] += jnp.sum(dv * _shifted(xs, k - pad, tok, width), axis=0, keepdims=True)
            return carry

        lax.fori_loop(0, n_piece, do_piece, 0)

    return pl.pallas_call(
        body, name=name, grid=(ncb, n_seq), out_shape=jax.ShapeDtypeStruct((n_tap, ncb * cb), F32),
        in_specs=[pl.BlockSpec((seq_len, cb), lambda j, s: (dy_blk_off + s, dycb + j)),
                  pl.BlockSpec((seq_len, cb), lambda j, s: (x_blk_off + s, xcb + j))],
        out_specs=pl.BlockSpec((n_tap, cb), lambda j, s: (0, j)),
        compiler_params=_params(8 * seq_len * cb * 4 + (8 << 20), 2),
    )(dy, x)


def tapsum_rows(name, x, xcb, w, wcb, *, seq_len, n_seq, cb, ncb, pad, flip):
    n_tap = w.shape[0]
    n_row = seq_len // GRID_W
    halo = pad * GRID_W

    def body(x_ref, w_ref, o_ref, xp):
        xp[pl.ds(0, halo), :] = jnp.zeros((halo, cb), F32)
        xp[pl.ds(halo + seq_len, halo), :] = jnp.zeros((halo, cb), F32)
        xp[pl.ds(halo, seq_len), :] = x_ref[...]
        wv = w_ref[...]

        def do_row(r, carry):
            acc = jnp.zeros((GRID_W, cb), F32)
            for k in range(n_tap):
                d = pad - k if flip else k - pad
                acc = acc + wv[k:k + 1, :] * xp[pl.ds(pl.multiple_of((r + pad + d) * GRID_W, GRID_W), GRID_W), :]
            o_ref[pl.ds(pl.multiple_of(r * GRID_W, GRID_W), GRID_W), :] = acc
            return carry

        lax.fori_loop(0, n_row, do_row, 0)

    return pl.pallas_call(
        body, name=name, grid=(ncb, n_seq), out_shape=jax.ShapeDtypeStruct((n_seq * seq_len, ncb * cb), F32),
        in_specs=[pl.BlockSpec((seq_len, cb), lambda j, s: (s, xcb + j)),
                  pl.BlockSpec((n_tap, cb), lambda j, s: (0, wcb + j))],
        out_specs=pl.BlockSpec((seq_len, cb), lambda j, s: (s, j)),
        scratch_shapes=[pltpu.VMEM((seq_len + 2 * halo, cb), F32)],
        compiler_params=_params(10 * seq_len * cb * 4 + (8 << 20), 2),
    )(x, w)


def tapgrad_rows(name, dy, dycb, x, xcb, *, n_tap, seq_len, n_seq, cb, ncb, pad):
    n_row = seq_len // GRID_W
    halo = pad * GRID_W

    def body(dy_ref, x_ref, o_ref, xp):
        @pl.when(pl.program_id(1) == 0)
        def _():
            o_ref[...] = jnp.zeros_like(o_ref)

        xp[pl.ds(0, halo), :] = jnp.zeros((halo, cb), F32)
        xp[pl.ds(halo + seq_len, halo), :] = jnp.zeros((halo, cb), F32)
        xp[pl.ds(halo, seq_len), :] = x_ref[...]

        def do_row(r, carry):
            dv = dy_ref[pl.ds(pl.multiple_of(r * GRID_W, GRID_W), GRID_W), :]
            for k in range(n_tap):
                xs = xp[pl.ds(pl.multiple_of((r + k) * GRID_W, GRID_W), GRID_W), :]
                o_ref[k:k + 1, :] += jnp.sum(dv * xs, axis=0, keepdims=True)
            return carry

        lax.fori_loop(0, n_row, do_row, 0)

    return pl.pallas_call(
        body, name=name, grid=(ncb, n_seq), out_shape=jax.ShapeDtypeStruct((n_tap, ncb * cb), F32),
        in_specs=[pl.BlockSpec((seq_len, cb), lambda j, s: (s, dycb + j)),
                  pl.BlockSpec((seq_len, cb), lambda j, s: (s, xcb + j))],
        out_specs=pl.BlockSpec((n_tap, cb), lambda j, s: (0, j)),
        scratch_shapes=[pltpu.VMEM((seq_len + 2 * halo, cb), F32)],
        compiler_params=_params(10 * seq_len * cb * 4 + (8 << 20), 2),
    )(dy, x)


def _ssd_blocks(b, s, *, rev, n_ctx, n_lat, lat_blocks):
    if rev:
        return jnp.where(s < n_ctx, lat_blocks + b * n_ctx + (n_ctx - 1 - s), b * n_lat + (n_lat - 1 - (s - n_ctx)))
    return jnp.where(s < n_ctx, lat_blocks + b * n_ctx + s, b * n_lat + (s - n_ctx))


def _ssd_common(xbc, raw, dtb, alog, dsk, *, rev, ds, n_head):
    if rev:
        raw = pltpu.roll(raw, LANES - n_head, axis=1)
    pre = raw + dtb
    dt = jnp.maximum(pre, 0.0) + jnp.log1p(jnp.exp(-jnp.abs(pre)))
    sig = jax.nn.sigmoid(pre)
    a = -jnp.exp(alog)
    da = dt * a
    ri = lax.broadcasted_iota(jnp.int32, (CHUNK, CHUNK), 0)
    ci = lax.broadcasted_iota(jnp.int32, (CHUNK, CHUNK), 1)
    mask = (ci >= ri) if rev else (ci <= ri)
    tri = mask.astype(F32)
    tri_t = ((ci <= ri) if rev else (ci >= ri)).astype(F32)
    cs = jnp.dot(tri, da, precision=HI, preferred_element_type=F32)
    tot = jnp.sum(da, axis=0, keepdims=True)
    def wide(v):
        first = lax.broadcasted_iota(jnp.int32, (v.shape[0], LANES), 1) < HEAD_DIM
        return jnp.concatenate(
            [jnp.where(first, jnp.broadcast_to(v[:, 2 * p:2 * p + 1], first.shape),
                       jnp.broadcast_to(v[:, 2 * p + 1:2 * p + 2], first.shape)) for p in range(n_head // 2)], axis=1)

    cs_w, tot_w = wide(cs), wide(tot)
    xh = xbc[:, :ds]
    dt_w = wide(dt)
    return dict(
        dt=dt, sig=sig, a=a, cs=cs, cs_t=cs.T, tot=tot, mask=mask, tri_t=tri_t,
        e_w=jnp.exp(cs_w), wt_w=jnp.exp(tot_w - cs_w), dec_w=jnp.exp(tot_w), dt_w=dt_w, dsk_w=wide(dsk),
        xh=xh, xs_w=xh * dt_w, bm=xbc[:, ds:ds + 2 * N_STATE], cm=xbc[:, ds + 2 * N_STATE:ds + 4 * N_STATE])


def _decay(q, col):
    seg = q["cs"][:, col:col + 1] - q["cs_t"][col:col + 1, :]
    return jnp.exp(jnp.where(q["mask"], seg, -jnp.inf))


def _split_heads(v):
    lane = lax.broadcasted_iota(jnp.int32, v.shape, 1)
    return jnp.concatenate([jnp.where(lane < HEAD_DIM, v, 0.0), jnp.where(lane >= HEAD_DIM, v, 0.0)], axis=0)


def ssd_fwd(name, xbc, proj, dt_cb, dtb, alog, dsk, *, rev, n_ex, seq_len, ctx_len, ds, rider=None):
    n_head, half = ds // HEAD_DIM, ds // 2
    n_ctx, n_lat = ctx_len // CHUNK, seq_len // CHUNK
    n_step = n_ctx + n_lat
    blk = functools.partial(_ssd_blocks, rev=rev, n_ctx=n_ctx, n_lat=n_lat, lat_blocks=n_ex * n_lat)
    xw = xbc.shape[1]

    def y_blk(b, s):
        sl = jnp.maximum(s, n_ctx) - n_ctx
        return b * n_lat + ((n_lat - 1 - sl) if rev else sl)

    hosted = _Hosted(rider, 5, 2, 1, (n_ex, n_step))

    def body(*refs):
        (xbc_ref, dt_ref, dtb_ref, alog_ref, dsk_ref), (y_ref, hs_ref), (h_scr,) = hosted.split(refs)

        @pl.when(pl.program_id(1) == 0)
        def _():
            h_scr[...] = jnp.zeros_like(h_scr)

        q = _ssd_common(xbc_ref[...], dt_ref[...], dtb_ref[...], alog_ref[...], dsk_ref[...], rev=rev, ds=ds, n_head=n_head)
        h = h_scr[...]
        hs_ref[...] = h
        for g in range(2):
            lo = g * half
            bg = q["bm"][:, g * N_STATE:(g + 1) * N_STATE].astype(BF16)
            cg = q["cm"][:, g * N_STATE:(g + 1) * N_STATE].astype(BF16)
            scores = lax.dot_general(cg, bg, _DIMS["nt"], preferred_element_type=F32)
            hg = h[:, lo:lo + half]
            off = jnp.dot(cg, hg.astype(BF16), preferred_element_type=F32)
            for j in range(half // LANES):
                c0 = (lo + j * LANES) // HEAD_DIM
                ln = slice(lo + j * LANES, lo + (j + 1) * LANES)
                p_cat = jnp.concatenate([scores * _decay(q, c0), scores * _decay(q, c0 + 1)], axis=1).astype(BF16)
                diag = jnp.dot(p_cat, _split_heads(q["xs_w"][:, ln]).astype(BF16), preferred_element_type=F32)
                y_ref[:, ln] = (diag + q["e_w"][:, ln] * off[:, j * LANES:(j + 1) * LANES]
                                + q["dsk_w"][:, ln] * q["xh"][:, ln])
            v = (q["wt_w"][:, lo:lo + half] * q["xs_w"][:, lo:lo + half]).astype(BF16)
            h_scr[:, lo:lo + half] = (q["dec_w"][:, lo:lo + half] * hg
                                      + lax.dot_general(bg, v, _DIMS["tn"], preferred_element_type=F32))
        hosted.finish()

    vec = pl.BlockSpec((1, LANES), lambda b, s: (0, 0))
    in_specs, out_shape, out_specs, scratch, args = hosted.call_args(
        [pl.BlockSpec((CHUNK, xw), lambda b, s: (blk(b, s), 0)),
         pl.BlockSpec((CHUNK, LANES), lambda b, s: (blk(b, s), dt_cb)), vec, vec, vec],
        (jax.ShapeDtypeStruct((n_ex * seq_len, ds), F32), jax.ShapeDtypeStruct((n_ex, n_step, N_STATE, ds), F32)),
        (pl.BlockSpec((CHUNK, ds), lambda b, s: (y_blk(b, s), 0)),
         pl.BlockSpec((None, None, N_STATE, ds), lambda b, s: (b, s, 0, 0))),
        [pltpu.VMEM((N_STATE, ds), F32)], [xbc, proj, dtb, alog, dsk])
    return hosted.results(pl.pallas_call(
        body, name=name, grid=(n_ex, n_step), out_shape=out_shape, in_specs=in_specs, out_specs=out_specs,
        scratch_shapes=scratch, compiler_params=_params(40 << 20, 2),
    )(*args))


def ssd_bwd(name, xbc, proj, dt_cb, hs, dy, dtb, alog, dsk, *, rev, n_ex, seq_len, ctx_len, ds, rider=None):
    n_head, half = ds // HEAD_DIM, ds // 2
    n_ctx, n_lat = ctx_len // CHUNK, seq_len // CHUNK
    n_step = n_ctx + n_lat
    n_tok = n_ex * (seq_len + ctx_len)
    blk0 = functools.partial(_ssd_blocks, rev=rev, n_ctx=n_ctx, n_lat=n_lat, lat_blocks=n_ex * n_lat)
    step = lambda sp: n_step - 1 - sp
    blk = lambda b, sp: blk0(b, step(sp))
    xw = xbc.shape[1]

    def dy_blk(b, sp):
        sl = jnp.maximum(step(sp), n_ctx) - n_ctx
        return b * n_lat + ((n_lat - 1 - sl) if rev else sl)

    hosted = _Hosted(rider, 7, 5, 1, (n_ex, n_step))

    def body(*refs):
        ((xbc_ref, dt_ref, hs_ref, dy_ref, dtb_ref, alog_ref, dsk_ref),
         (dxbc_ref, ddt_ref, dalog_ref, ddtb_ref, ddsk_ref), (dh_scr,)) = hosted.split(refs)
        b, sp = pl.program_id(0), pl.program_id(1)

        @pl.when(sp == 0)
        def _():
            dh_scr[...] = jnp.zeros_like(dh_scr)

        @pl.when((sp == 0) & (b == 0))
        def _():
            dalog_ref[...] = jnp.zeros_like(dalog_ref)
            ddtb_ref[...] = jnp.zeros_like(ddtb_ref)
            ddsk_ref[...] = jnp.zeros_like(ddsk_ref)

        q = _ssd_common(xbc_ref[...], dt_ref[...], dtb_ref[...], alog_ref[...], dsk_ref[...], rev=rev, ds=ds, n_head=n_head)
        h = hs_ref[...]
        d_y = jnp.where(step(sp) >= n_ctx, dy_ref[...], 0.0)
        dh_next = dh_scr[...]
        lane_row = lax.broadcasted_iota(jnp.int32, (1, LANES), 1)
        d_cs = jnp.zeros((CHUNK, LANES), F32)
        dxs_parts, de_parts, dwt_parts, ddec_parts = [], [], [], []
        for g in range(2):
            lo = g * half
            gs = slice(lo, lo + half)
            bg = q["bm"][:, g * N_STATE:(g + 1) * N_STATE].astype(BF16)
            cg = q["cm"][:, g * N_STATE:(g + 1) * N_STATE].astype(BF16)
            scores = lax.dot_general(cg, bg, _DIMS["nt"], preferred_element_type=F32)
            hg, dyg, dhn = h[:, gs], d_y[:, gs], dh_next[:, gs]
            off = jnp.dot(cg, hg.astype(BF16), preferred_element_type=F32)
            d_off = (q["e_w"][:, gs] * dyg).astype(BF16)
            de_parts.append(dyg * off)
            d_c = lax.dot_general(d_off, hg.astype(BF16), _DIMS["nt"], preferred_element_type=F32)
            dh_scr[:, gs] = (lax.dot_general(cg, d_off, _DIMS["tn"], preferred_element_type=F32)
                             + q["dec_w"][:, gs] * dhn)
            b_dh = jnp.dot(bg, dhn.astype(BF16), preferred_element_type=F32)
            v = q["wt_w"][:, gs] * q["xs_w"][:, gs]
            d_b = lax.dot_general(v.astype(BF16), dhn.astype(BF16), _DIMS["nt"], preferred_element_type=F32)
            dwt_parts.append(q["xs_w"][:, gs] * b_dh)
            ddec_parts.append(jnp.sum(hg * dhn, axis=0, keepdims=True))
            d_scores = jnp.zeros((CHUNK, CHUNK), F32)
            for j in range(half // LANES):
                c0 = (lo + j * LANES) // HEAD_DIM
                ln = slice(lo + j * LANES, lo + (j + 1) * LANES)
                l0, l1 = _decay(q, c0), _decay(q, c0 + 1)
                p0, p1 = scores * l0, scores * l1
                dy_st = _split_heads(d_y[:, ln]).astype(BF16)
                d_p = lax.dot_general(dy_st, q["xs_w"][:, ln].astype(BF16), _DIMS["nt"], preferred_element_type=F32)
                d_p0, d_p1 = d_p[:CHUNK], d_p[CHUNK:]
                d_scores = d_scores + d_p0 * l0 + d_p1 * l1
                for col, t in ((c0, d_p0 * p0), (c0 + 1, d_p1 * p1)):
                    d_cs = d_cs + jnp.sum(t - t.T, axis=1, keepdims=True) * (lane_row == col).astype(F32)
                p_st = jnp.concatenate([p0, p1], axis=0).astype(BF16)
                dxs_parts.append(lax.dot_general(p_st, dy_st, _DIMS["tn"], preferred_element_type=F32)
                                 + q["wt_w"][:, ln] * b_dh[:, j * LANES:(j + 1) * LANES])
            d_sc = d_scores.astype(BF16)
            d_c = d_c + jnp.dot(d_sc, bg, preferred_element_type=F32)
            d_b = d_b + lax.dot_general(d_sc, cg, _DIMS["tn"], preferred_element_type=F32)
            dxbc_ref[:, ds + g * N_STATE:ds + (g + 1) * N_STATE] = d_b
            dxbc_ref[:, ds + (2 + g) * N_STATE:ds + (3 + g) * N_STATE] = d_c
        d_xs = jnp.concatenate(dxs_parts, axis=1)
        narrow_m = (lax.broadcasted_iota(jnp.int32, (ds, LANES), 0) // HEAD_DIM
                    == lax.broadcasted_iota(jnp.int32, (ds, LANES), 1)).astype(BF16)
        rows8 = lambda v: jnp.broadcast_to(v, (8, ds))
        stacked = jnp.concatenate(
            [jnp.concatenate(dwt_parts, axis=1), jnp.concatenate(de_parts, axis=1), d_xs * q["xh"],
             rows8(jnp.concatenate(ddec_parts, axis=1)), rows8(jnp.sum(d_y * q["xh"], axis=0, keepdims=True))], axis=0)
        hi = stacked.astype(BF16)
        lo = (stacked - hi.astype(F32)).astype(BF16)
        sums = (jnp.dot(hi, narrow_m, preferred_element_type=F32) + jnp.dot(lo, narrow_m, preferred_element_type=F32))
        n_wt, n_e, n_xs = sums[:CHUNK], sums[CHUNK:2 * CHUNK], sums[2 * CHUNK:3 * CHUNK]
        n_dec, n_dsk = sums[3 * CHUNK:3 * CHUNK + 1], sums[3 * CHUNK + 8:3 * CHUNK + 9]
        e, wt, dec = jnp.exp(q["cs"]), jnp.exp(q["tot"] - q["cs"]), jnp.exp(q["tot"])
        d_wt = n_wt * wt
        d_cs = d_cs + n_e * e - d_wt
        d_tot = jnp.sum(d_wt, axis=0, keepdims=True) + n_dec * dec
        d_da = jnp.dot(q["tri_t"], d_cs, precision=HI, preferred_element_type=F32) + d_tot
        d_dt = d_da * q["a"] + n_xs
        dxbc_ref[:, :ds] = d_xs * q["dt_w"] + q["dsk_w"] * d_y
        dalog_ref[...] += jnp.sum(d_da * q["dt"], axis=0, keepdims=True) * q["a"]
        d_raw = d_dt * q["sig"]
        ddtb_ref[...] += jnp.sum(d_raw, axis=0, keepdims=True)
        ddsk_ref[...] += n_dsk
        ddt_ref[...] = pltpu.roll(d_raw, n_head, axis=1) if rev else d_raw
        hosted.finish()

    vec = pl.BlockSpec((1, LANES), lambda b, s: (0, 0))
    vec_shape = jax.ShapeDtypeStruct((1, LANES), F32)
    in_specs, out_shape, out_specs, scratch, args = hosted.call_args(
        [pl.BlockSpec((CHUNK, xw), lambda b, s: (blk(b, s), 0)),
         pl.BlockSpec((CHUNK, LANES), lambda b, s: (blk(b, s), dt_cb)),
         pl.BlockSpec((None, None, N_STATE, ds), lambda b, s: (b, step(s), 0, 0)),
         pl.BlockSpec((CHUNK, ds), lambda b, s: (dy_blk(b, s), 0)), vec, vec, vec],
        (jax.ShapeDtypeStruct((n_tok, xw), F32), jax.ShapeDtypeStruct((n_tok, LANES), F32), vec_shape, vec_shape, vec_shape),
        (pl.BlockSpec((CHUNK, xw), lambda b, s: (blk(b, s), 0)),
         pl.BlockSpec((CHUNK, LANES), lambda b, s: (blk(b, s), 0)), vec, vec, vec),
        [pltpu.VMEM((N_STATE, ds), F32)], [xbc, proj, hs, dy, dtb, alog, dsk])
    return hosted.results(pl.pallas_call(
        body, name=name, grid=(n_ex, n_step), out_shape=out_shape, in_specs=in_specs, out_specs=out_specs,
        scratch_shapes=scratch, compiler_params=_params(48 << 20, 2),
    )(*args))


def final_loss(x3, target, w, *, tm):
    n, d = x3.shape

    def body(x_ref, t_ref, w_ref, dx_ref, dw_ref, loss_ref):
        i = pl.program_id(0)
        t = t_ref[...]

        def per_feature(xv, wv):
            err = _rms(xv, wv) - t
            return 0.5 * jnp.sum(err * err, axis=0, keepdims=True) / d

        lv, vjp = jax.vjp(per_feature, x_ref[...], w_ref[...])
        dx, dw = vjp(jnp.ones_like(lv))
        dx_ref[...] = dx

        @pl.when(i == 0)
        def _():
            dw_ref[...] = dw
            loss_ref[...] = lv

        @pl.when(i > 0)
        def _():
            dw_ref[...] += dw
            loss_ref[...] += lv

    tile = pl.BlockSpec((tm, d), lambda i: (i, 0))
    vec = pl.BlockSpec((1, d), lambda i: (0, 0))
    return pl.pallas_call(
        body, name="final_loss", grid=(n // tm,), in_specs=[tile, tile, vec],
        out_shape=(jax.ShapeDtypeStruct((n, d), F32), jax.ShapeDtypeStruct((1, d), F32), jax.ShapeDtypeStruct((1, d), F32)),
        out_specs=(tile, vec, vec), compiler_params=_params(tm * d * 4 * 16 + (8 << 20)),
    )(x3, target, w)


def sum_slots(name, arr):
    n_slot, n_row, width = arr.shape
    tm = _row_tile(n_row, width * n_slot, mult=16)

    def body(a_ref, o_ref):
        acc = a_ref[0].astype(F32)
        for j in range(1, n_slot):
            acc = acc + a_ref[j].astype(F32)
        o_ref[...] = acc

    return pl.pallas_call(
        body, name=name, grid=(n_row // tm,), out_shape=jax.ShapeDtypeStruct((n_row, width), F32),
        in_specs=[pl.BlockSpec((n_slot, tm, width), lambda i: (0, i, 0))],
        out_specs=pl.BlockSpec((tm, width), lambda i: (i, 0)), compiler_params=_params(),
    )(arr)


def adamw(name, w, g_slots, m, v):
    n_slot, n_row, width = g_slots.shape
    tm = _row_tile(n_row, width * 2)

    def body(w_ref, g_ref, m_ref, v_ref, go_ref, d_ref, mo_ref, vo_ref):
        g = g_ref[0]
        for j in range(1, n_slot):
            g = g + g_ref[j]
        m2 = ADAM_B1 * m_ref[...] + (1.0 - ADAM_B1) * g
        v2 = ADAM_B2 * v_ref[...] + (1.0 - ADAM_B2) * jnp.square(g)
        m_hat = m2 / (1.0 - ADAM_B1 ** ADAM_STEP)
        v_hat = v2 / (1.0 - ADAM_B2 ** ADAM_STEP)
        go_ref[...] = g
        d_ref[...] = -ADAM_LR * (m_hat / (jnp.sqrt(v_hat) + ADAM_EPS) + ADAM_WD * w_ref[...])
        mo_ref[...] = m2
        vo_ref[...] = v2

    tile = pl.BlockSpec((tm, width), lambda i: (i, 0))
    shape = jax.ShapeDtypeStruct((n_row, width), F32)
    return pl.pallas_call(
        body, name=name, grid=(n_row // tm,), out_shape=(shape,) * 4,
        in_specs=[tile, pl.BlockSpec((n_slot, tm, width), lambda i: (0, i, 0)), tile, tile],
        out_specs=(tile,) * 4, compiler_params=_params(),
    )(w, g_slots, m, v)


def cctx_grad(q_all, c_ctx_row):
    d = c_ctx_row.shape[1]

    def body(q_ref, c_ref, o_ref):
        acc = q_ref[0, 0:1, :]
        for j in (2, 4, 6):
            acc = acc + q_ref[j, 0:1, :]
        _, vjp = jax.vjp(_silu, c_ref[...])
        o_ref[...] = vjp(acc)[0]

    return pl.pallas_call(
        body, name="cctx_grad", out_shape=jax.ShapeDtypeStruct((1, d), F32),
    )(q_all, c_ctx_row)


def loss_total(pack_sum, d):
    def body(p_ref, o_ref):
        o_ref[...] = jnp.sum(p_ref[:, 0:d], axis=1, keepdims=True)

    return pl.pallas_call(
        body, name="loss_total", out_shape=jax.ShapeDtypeStruct((1, 1), F32),
    )(pack_sum)


class _Plan:
    def __init__(self):
        self.builders, self.got = {}, {}

    def on(self, host, key, builder):
        self.builders.setdefault(host, []).append((key, builder))

    def run(self, host, fn, *args, **kw):
        if host not in self.builders:
            return fn(host, *args, **kw)
        keys, riders = zip(*[(key, builder(self)) for key, builder in self.builders[host]])
        res, landed = fn(host, *args, rider=Riders(riders), **kw)
        for key, r in zip(keys, riders):
            self.got[key], landed = landed[:r.n], landed[r.n:]
        return res


def _val(w):
    return w() if callable(w) else w


def _matmul_tile(n_rows, tm):
    return 2 * tm if n_rows % (2 * tm) == 0 else tm


def _ffn_fwd(plan, tag, xin, n_rows, tm, seg_fn, shift, scale, gate, norm_w, wg, wu, wd):
    d = xin.shape[1]
    n_tiles = n_rows // tm
    (h,) = plan.run(f"{tag}_norm", rowwise, fn_norm_mod, [row(xin)], [shift, scale], [norm_w], [(n_rows, d, BF16)],
                    tm=tm, n_tiles=n_tiles, seg_fn=seg_fn)
    tmm = _matmul_tile(n_rows, tm)
    g = plan.run(f"{tag}_gate", matmul, [(h, _val(wg))], "nn", out_dtype=BF16, b_ch=True, out_ch=True, tm=tmm)
    u, act = plan.run(f"{tag}_up", matmul, [(h, _val(wu))], "nn", b_ch=True, out_ch=True, tm=tmm,
                      post=([g], lambda acc, gv: (acc, fn_act(gv, acc)[0]), [BF16, BF16]))
    f = plan.run(f"{tag}_down", matmul, [(act, _val(wd))], "nn", a_ch=True, b_ch=True, tm=tmm)
    (xo,) = plan.run(f"{tag}_resid", rowwise, make_fn_resid(0.5), [row(xin), row(f)], [gate], [], [(n_rows, d, F32)],
                     tm=tm, n_tiles=n_tiles, seg_fn=seg_fn)
    return xo, (h, g, u, act, f)


def _ffn_bwd(plan, tag, d_xo, saved, xin, n_rows, tm, seg_fn, first_fn, shift, scale, gate, norm_w, wg, wu, wd, dx_rows, dx_limit):
    h, g, u, act, f = saved
    d = xin.shape[1]
    n_tiles = n_rows // tm
    n_ch, _, n_hid = g.shape
    d_f, d_gate = plan.run(f"{tag}_resid_bwd", rowwise_bwd, make_fn_resid(0.5), [row(xin), row(f)], [gate], [], [[row(d_xo)]],
                           [None, (n_rows, BF16, None)], tm=tm, n_tiles=n_tiles, seg_fn=seg_fn, first_fn=first_fn)
    tmm = _matmul_tile(n_rows, tm)
    act_vjp = lambda acc, gv, uv: jax.vjp(lambda a, b: fn_act(a, b)[0], gv, uv)[1](acc)
    d_g, d_u = plan.run(f"{tag}_down_dx", matmul, [(d_f, wd)], "nt", b_ch=True, out_ch=True, tm=tmm,
                        post=([g, u], act_vjp, [BF16, BF16]))
    plan.got[f"{tag}_d_wd"] = plan.run(f"{tag}_down_dw", matmul, [(act, d_f)], "tn", out_dtype=BF16, a_ch=True, out_ch=True, tm=tmm)
    d_h = plan.run(f"{tag}_up_dx", matmul, [(d_g, wg), (d_u, wu)], "nt", a_ch=True, b_ch=True, tm=tmm)
    plan.got[f"{tag}_d_wg"] = plan.run(f"{tag}_gate_dw", matmul, [(h, d_g)], "tn", out_dtype=BF16, b_ch=True, out_ch=True, tm=tmm)
    plan.got[f"{tag}_d_wu"] = plan.run(f"{tag}_up_dw", matmul, [(h, d_u)], "tn", out_dtype=BF16, b_ch=True, out_ch=True, tm=tmm)
    d_x, d_shift, d_scale, d_nw = plan.run(
        f"{tag}_norm_bwd", rowwise_bwd, fn_norm_mod, [row(xin)], [shift, scale], [norm_w], [[row(d_h)]], [(dx_rows, F32, dx_limit)],
        tm=tm, n_tiles=n_tiles, seg_fn=seg_fn, first_fn=first_fn, adds={0: (row(d_xo), None)})
    return d_x, (d_shift, d_scale, d_gate), d_nw


def kernel(x, c, ctx, c_ctx, w_mod, b_mod, norm_ffn1, ffn1_gate, ffn1_up, ffn1_down, norm_mix, w_in, ssm_conv_w, ssm_conv_b, dt_bias_fwd, dt_bias_bwd, a_log_fwd, a_log_bwd, ssm_d, ssm_norm_w, cconv_w, cconv_b, cconv_ln_w, cconv_ln_b, w_out, norm_ffn2, ffn2_gate, ffn2_up, ffn2_down, final_norm, loss_target, m_c_ctx, m_w_mod, m_b_mod, m_norm_ffn1, m_ffn1_gate, m_ffn1_up, m_ffn1_down, m_norm_mix, m_w_in, m_ssm_conv_w, m_ssm_conv_b, m_dt_bias_fwd, m_dt_bias_bwd, m_a_log_fwd, m_a_log_bwd, m_ssm_d, m_ssm_norm_w, m_cconv_w, m_cconv_b, m_cconv_ln_w, m_cconv_ln_b, m_w_out, m_norm_ffn2, m_ffn2_gate, m_ffn2_up, m_ffn2_down, m_final_norm, v_c_ctx, v_w_mod, v_b_mod, v_norm_ffn1, v_ffn1_gate, v_ffn1_up, v_ffn1_down, v_norm_mix, v_w_in, v_ssm_conv_w, v_ssm_conv_b, v_dt_bias_fwd, v_dt_bias_bwd, v_a_log_fwd, v_a_log_bwd, v_ssm_d, v_ssm_norm_w, v_cconv_w, v_cconv_b, v_cconv_ln_w, v_cconv_ln_b, v_w_out, v_norm_ffn2, v_ffn2_gate, v_ffn2_up, v_ffn2_down, v_final_norm):
    weights = dict(c_ctx=c_ctx, w_mod=w_mod, b_mod=b_mod, norm_ffn1=norm_ffn1, ffn1_gate=ffn1_gate, ffn1_up=ffn1_up, ffn1_down=ffn1_down, norm_mix=norm_mix, w_in=w_in, ssm_conv_w=ssm_conv_w, ssm_conv_b=ssm_conv_b, dt_bias_fwd=dt_bias_fwd, dt_bias_bwd=dt_bias_bwd, a_log_fwd=a_log_fwd, a_log_bwd=a_log_bwd, ssm_d=ssm_d, ssm_norm_w=ssm_norm_w, cconv_w=cconv_w, cconv_b=cconv_b, cconv_ln_w=cconv_ln_w, cconv_ln_b=cconv_ln_b, w_out=w_out, norm_ffn2=norm_ffn2, ffn2_gate=ffn2_gate, ffn2_up=ffn2_up, ffn2_down=ffn2_down, final_norm=final_norm)
    mom1 = dict(c_ctx=m_c_ctx, w_mod=m_w_mod, b_mod=m_b_mod, norm_ffn1=m_norm_ffn1, ffn1_gate=m_ffn1_gate, ffn1_up=m_ffn1_up, ffn1_down=m_ffn1_down, norm_mix=m_norm_mix, w_in=m_w_in, ssm_conv_w=m_ssm_conv_w, ssm_conv_b=m_ssm_conv_b, dt_bias_fwd=m_dt_bias_fwd, dt_bias_bwd=m_dt_bias_bwd, a_log_fwd=m_a_log_fwd, a_log_bwd=m_a_log_bwd, ssm_d=m_ssm_d, ssm_norm_w=m_ssm_norm_w, cconv_w=m_cconv_w, cconv_b=m_cconv_b, cconv_ln_w=m_cconv_ln_w, cconv_ln_b=m_cconv_ln_b, w_out=m_w_out, norm_ffn2=m_norm_ffn2, ffn2_gate=m_ffn2_gate, ffn2_up=m_ffn2_up, ffn2_down=m_ffn2_down, final_norm=m_final_norm)
    mom2 = dict(c_ctx=v_c_ctx, w_mod=v_w_mod, b_mod=v_b_mod, norm_ffn1=v_norm_ffn1, ffn1_gate=v_ffn1_gate, ffn1_up=v_ffn1_up, ffn1_down=v_ffn1_down, norm_mix=v_norm_mix, w_in=v_w_in, ssm_conv_w=v_ssm_conv_w, ssm_conv_b=v_ssm_conv_b, dt_bias_fwd=v_dt_bias_fwd, dt_bias_bwd=v_dt_bias_bwd, a_log_fwd=v_a_log_fwd, a_log_bwd=v_a_log_bwd, ssm_d=v_ssm_d, ssm_norm_w=v_ssm_norm_w, cconv_w=v_cconv_w, cconv_b=v_cconv_b, cconv_ln_w=v_cconv_ln_w, cconv_ln_b=v_cconv_ln_b, w_out=v_w_out, norm_ffn2=v_norm_ffn2, ffn2_gate=v_ffn2_gate, ffn2_up=v_ffn2_up, ffn2_down=v_ffn2_down, final_norm=v_final_norm)
    order = list(weights)

    n_ex, seq_len, d = x.shape
    ctx_len = ctx.shape[1]
    ds = d
    n_head = ds // HEAD_DIM
    xw = ds + 4 * N_STATE
    n_lat, n_ctx_rows = n_ex * seq_len, n_ex * ctx_len
    n_tok = n_lat + n_ctx_rows
    tm = math.gcd(math.gcd(512, seq_len), n_ctx_rows)
    seg_all, first_all = _segmenter(tm, seq_len, n_lat)
    lat_tiles = n_lat // tm

    xi, yi, ci = lax.axis_index("x"), lax.axis_index("y"), lax.axis_index("c")
    me, chip = 4 * xi + 2 * yi + ci, 2 * xi + yi

    (c_all,) = exchange("gather_c", [c], "all8")
    n_all = 8 * n_ex
    n_cond = -(-(n_all + 1) // 8) * 8
    cond = jnp.concatenate([c_all.reshape(n_all, d), c_ctx[None, :], jnp.zeros((n_cond - n_all - 1, d), F32)])
    mod_w = w_mod.shape[2]
    b_shard = lax.dynamic_slice(b_mod, (0, chip * mod_w), (1, mod_w))
    (mod_g,) = exchange("gather_mod", [mod_fwd(cond, w_mod[0], b_shard)], "chips")
    mod_full = mod_g.transpose(1, 0, 2).reshape(n_cond, N_CHIPS * mod_w)
    mod_mine = lax.dynamic_slice(mod_full, (me * n_ex, 0), (n_ex, 9 * d)).reshape(n_ex, 9, d)
    mod_ctx = mod_full[n_all].reshape(9, d)
    tabs = [jnp.concatenate([mod_mine[:, j], mod_ctx[j][None]])[:, None, :] for j in range(9)]
    lat = lambda t: t[:n_ex]

    bf = lambda w: w[0].astype(BF16)
    plan = _Plan()
    gather = lambda *ws: (lambda p: Rider(list(ws), "chips"))
    wg1, w5_g, w31_g = exchange("gather_first", [bf(ffn1_gate), ssm_conv_w[0], cconv_w[0]], "chips")
    plan.on("ffn1_gate", "wu1", gather(bf(ffn1_up)))
    plan.on("ffn1_up", "wd1_wout", gather(bf(ffn1_down), bf(w_out)))
    plan.on("ffn1_down", "win", gather(bf(w_in)))
    xt = jnp.concatenate([x.reshape(n_lat, d), ctx.reshape(n_ctx_rows, d)])
    x1, saved1 = _ffn_fwd(plan, "ffn1", xt, n_tok, tm, seg_all, tabs[0], tabs[1], tabs[2], norm_ffn1, wg1,
                          lambda: plan.got["wu1"][0], lambda: plan.got["wd1_wout"][0])
    (wu1,), (wd1, wout_g), (win_g,) = (plan.got[k] for k in ("wu1", "wd1_wout", "win"))
    unshard_cols = lambda t: t.transpose(1, 0, 2).reshape(t.shape[1], N_CHIPS * t.shape[2])
    win = unshard_cols(win_g)
    o_x, o_dt, o_glu = ds, ds + xw, ds + xw + 2 * n_head
    w_z, w_xbc, w_dt = win[:, :ds], win[:, o_x:o_dt], win[:, o_dt:o_glu]
    w_ga, w_gb = win[:, o_glu:o_glu + d], win[:, o_glu + d:]
    w_dtp = jnp.concatenate([w_dt, jnp.zeros((d, LANES - 2 * n_head), BF16)], axis=1)
    w_cat = jnp.concatenate([w_z, w_ga, w_gb, w_xbc, w_dtp], axis=1)
    cbw = d // 2
    xbc_cb, dt_cb = 3 * d // cbw, (3 * d + xw) // LANES
    wout = wout_g.reshape(2 * d, d)
    wo_y, wo_u = wout[:ds], wout[ds:]
    w5, w31 = unshard_cols(w5_g), unshard_cols(w31_g)
    pad_vec = lambda v: jnp.concatenate([v.reshape(1, -1), jnp.zeros((1, LANES - v.size), F32)], axis=1)
    dtb_f, dtb_b, alog_f, alog_b = map(pad_vec, (dt_bias_fwd, dt_bias_bwd, a_log_fwd, a_log_bwd))
    dsk_f, dsk_b = pad_vec(ssm_d), jnp.zeros((1, LANES), F32)

    (h2,) = rowwise("mix_norm", fn_norm_mod, [row(x1)], [tabs[3], tabs[4]], [norm_mix], [(n_tok, d, BF16)],
                    tm=tm, n_tiles=n_tok // tm, seg_fn=seg_all)
    proj, (wg2,) = matmul("mix_proj", [(h2, w_cat)], "nn", tm=min(tm, 256), rider=Rider([bf(ffn2_gate)], "chips"))
    c5 = lambda name, src, cb0, flip, seq, off: tapsum_roll(
        name, src, cb0, w5, 0, seq_len=seq, n_seq=n_ex, row_blk_off=off, width=seq, piece=seq, cb=cbw, ncb=xw // cbw,
        pad=w5.shape[0] // 2, flip=flip)
    craw = jnp.concatenate([c5("xbc_conv_lat", proj, xbc_cb, False, seq_len, 0),
                            c5("xbc_conv_ctx", proj, xbc_cb, False, ctx_len, n_lat // ctx_len)])
    (xbc,) = rowwise("xbc_silu", fn_silu_bias, [row(craw)], [], [ssm_conv_b], [(n_tok, xw, F32)], tm=tm, n_tiles=n_tok // tm)
    ssd = dict(n_ex=n_ex, seq_len=seq_len, ctx_len=ctx_len, ds=ds)
    (y_f, hs_f), (wu2, wd2) = ssd_fwd("ssd_fwd_f", xbc, proj, dt_cb, dtb_f, alog_f, dsk_f, rev=False,
                                      rider=Rider([bf(ffn2_up), bf(ffn2_down)], "chips"), **ssd)
    y_b, hs_b = ssd_fwd("ssd_fwd_b", xbc, proj, dt_cb, dtb_b, alog_b, dsk_b, rev=True, **ssd)
    fn_gate = make_fn_gate_groupnorm(ds)
    (yn,) = rowwise("ssd_gate", fn_gate, [row(y_f), row(y_b), row(proj, d, 0)], [], [ssm_norm_w], [(n_lat, ds, BF16)],
                    tm=tm, n_tiles=lat_tiles)
    (u0,) = rowwise("glu", fn_glu, [row(proj, d, 1), row(proj, d, 2)], [], [], [(n_lat, d, F32)], tm=tm, n_tiles=lat_tiles)
    cb31 = max(LANES, d // 4)
    ncb31 = (d // 2) // cb31
    pad31 = w31.shape[0] // 2
    piece31 = min(seq_len, 4 * GRID_W)
    v_w = tapsum_roll("cconv_cols", u0, 0, w31, 0, seq_len=seq_len, n_seq=n_ex, row_blk_off=0, width=GRID_W,
                      piece=piece31, cb=cb31, ncb=ncb31, pad=pad31, flip=False)
    v_h = tapsum_rows("cconv_rows", u0, ncb31, w31, ncb31, seq_len=seq_len, n_seq=n_ex, cb=cb31, ncb=ncb31, pad=pad31, flip=False)
    (un,) = rowwise("cconv_ln", fn_ln_silu, [row(v_w), row(v_h)], [], [cconv_b, cconv_ln_w, cconv_ln_b], [(n_lat, d, BF16)],
                    tm=tm, n_tiles=lat_tiles)
    mix = matmul("mix_out", [(yn, wo_y), (un, wo_u)], "nn", tm=tm)
    seg_lat, first_lat = _segmenter(tm, seq_len, n_lat)
    (x2,) = rowwise("mix_resid", make_fn_resid(1.0), [row(x1), row(mix)], [lat(tabs[5])], [], [(n_lat, d, F32)],
                    tm=tm, n_tiles=lat_tiles, seg_fn=seg_lat)
    x3, saved2 = _ffn_fwd(plan, "ffn2", x2, n_lat, tm, seg_lat, lat(tabs[6]), lat(tabs[7]), lat(tabs[8]), norm_ffn2, wg2, wu2, wd2)
    d_x3, d_final, loss_vec = final_loss(x3, loss_target.reshape(n_lat, d), final_norm.reshape(1, d), tm=tm)

    shard_cols = lambda t: t.reshape(t.shape[0], N_CHIPS, -1).transpose(1, 0, 2)

    def pieces(t):
        t = jnp.pad(t, ((0, 0), (0, t.shape[1] % 2), (0, 0)))
        return t.reshape(2 * N_CHIPS, t.shape[1] // 2, t.shape[2]).astype(BF16)

    scatter = lambda *ts: Rider([pieces(t) for t in ts], "all8", scatter=True)
    halves = lambda names, landed: Rider([sum_slots(f"sum_{nm}", r) for nm, r in zip(names, landed)], "sibling")
    swapped = {}
    plan.on("ffn2_up_dx", "sc_ffn2_down", lambda p: scatter(p.got["ffn2_d_wd"]))
    plan.on("ffn2_up_dw", "sc_ffn2_gate", lambda p: scatter(p.got["ffn2_d_wg"]))
    d_x2, (d_s6, d_s7, d_g8), d_nffn2 = _ffn_bwd(
        plan, "ffn2", d_x3, saved2, x2, n_lat, tm, seg_lat, first_lat, lat(tabs[6]), lat(tabs[7]), lat(tabs[8]), norm_ffn2,
        wg2, wu2, wd2, n_lat, None)
    d_mix, d_g5 = rowwise_bwd("mix_resid_bwd", make_fn_resid(1.0), [row(x1), row(mix)], [lat(tabs[5])], [], [[row(d_x2)]],
                              [None, (n_lat, BF16, None)], tm=tm, n_tiles=lat_tiles, seg_fn=seg_lat, first_fn=first_lat)
    d_yn = matmul("mix_out_dy", [(d_mix, wo_y)], "nt", tm=tm)
    d_un = matmul("mix_out_du", [(d_mix, wo_u)], "nt", tm=tm)
    d_wout = jnp.concatenate([matmul("mix_out_dwy", [(yn, d_mix)], "tn", out_dtype=BF16, tm=tm), matmul("mix_out_dwu", [(un, d_mix)], "tn", out_dtype=BF16, tm=tm)])
    d_vw, d_vh, d_cb, d_lnw, d_lnb = rowwise_bwd(
        "cconv_ln_bwd", fn_ln_silu, [row(v_w), row(v_h)], [], [cconv_b, cconv_ln_w, cconv_ln_b], [[row(d_un)]],
        [(n_lat, F32, None)] * 2, tm=tm, n_tiles=lat_tiles)
    d_u0w = tapsum_roll("cconv_cols_dx", d_vw, 0, w31, 0, seq_len=seq_len, n_seq=n_ex, row_blk_off=0, width=GRID_W,
                        piece=piece31, cb=cb31, ncb=ncb31, pad=pad31, flip=True)
    d_u0h = tapsum_rows("cconv_rows_dx", d_vh, 0, w31, ncb31, seq_len=seq_len, n_seq=n_ex, cb=cb31, ncb=ncb31, pad=pad31, flip=True)
    d_w31 = jnp.concatenate([
        tapgrad_roll("cconv_cols_dw", d_vw, 0, 0, u0, 0, 0, n_tap=w31.shape[0], seq_len=seq_len, n_seq=n_ex, width=GRID_W,
                     piece=piece31, cb=cb31, ncb=ncb31, pad=pad31),
        tapgrad_rows("cconv_rows_dw", d_vh, 0, u0, ncb31, n_tap=w31.shape[0], seq_len=seq_len, n_seq=n_ex, cb=cb31,
                     ncb=ncb31, pad=pad31)], axis=1)
    d_u0 = jnp.concatenate([d_u0w, d_u0h], axis=1)
    d_ga, d_gb = rowwise_bwd("glu_bwd", fn_glu, [row(proj, d, 1), row(proj, d, 2)], [], [], [[row(d_u0)]],
                             [(n_lat, BF16, None)] * 2, tm=tm, n_tiles=lat_tiles)
    d_ysum, d_z, d_ssmnw = rowwise_bwd(
        "ssd_gate_bwd", fn_gate, [row(y_f), row(y_b), row(proj, d, 0)], [], [ssm_norm_w], [[row(d_yn)]],
        [(n_lat, F32, None), None, (n_lat, BF16, None)], tm=tm, n_tiles=lat_tiles)
    (dxbc_f, ddt_f, dalog_f, ddtb_f, ddsk), landed = ssd_bwd(
        "ssd_bwd_f", xbc, proj, dt_cb, hs_f, d_ysum, dtb_f, alog_f, dsk_f, rev=False,
        rider=scatter(plan.got["ffn2_d_wu"], d_wout.reshape(N_CHIPS, -1, d)), **ssd)
    (dxbc_b, ddt_b, dalog_b, ddtb_b, _), both = ssd_bwd(
        "ssd_bwd_b", xbc, proj, dt_cb, hs_b, d_ysum, dtb_b, alog_b, dsk_b, rev=True,
        rider=halves(["ffn2_down", "ffn2_gate"], plan.got["sc_ffn2_down"] + plan.got["sc_ffn2_gate"]), **ssd)
    swapped.update(zip(["ffn2_down", "ffn2_gate"], both))
    (d_craw, d_conv_b), both = rowwise_bwd(
        "xbc_silu_bwd", fn_silu_bias, [row(craw)], [], [ssm_conv_b], [[row(dxbc_f), row(dxbc_b)]],
        [(n_tok, F32, None)], tm=tm, n_tiles=n_tok // tm, rider=halves(["ffn2_up", "w_out"], landed))
    swapped.update(zip(["ffn2_up", "w_out"], both))
    d_pxbc = jnp.concatenate([c5("xbc_conv_lat_dx", d_craw, 0, True, seq_len, 0),
                              c5("xbc_conv_ctx_dx", d_craw, 0, True, ctx_len, n_lat // ctx_len)])
    g5 = lambda name, seq, off: tapgrad_roll(name, d_craw, 0, off, proj, xbc_cb, off, n_tap=w5.shape[0], seq_len=seq,
                                             n_seq=n_ex, width=seq, piece=seq, cb=cbw, ncb=xw // cbw, pad=w5.shape[0] // 2)
    d_w5 = g5("xbc_conv_lat_dw", seq_len, 0) + g5("xbc_conv_ctx_dw", ctx_len, n_lat // ctx_len)
    lat_pairs = [(d_z, w_z), (d_ga, w_ga), (d_gb, w_gb), (d_pxbc, w_xbc), (ddt_f, w_dtp), (ddt_b, w_dtp)]
    d_h2 = jnp.concatenate([matmul("mix_proj_dx_lat", lat_pairs, "nt", rows=n_lat, tm=min(tm, 256)),
                            matmul("mix_proj_dx_ctx", lat_pairs[3:], "nt", rows=n_ctx_rows, row_off=n_lat, tm=min(tm, 256))])
    d_wz = matmul("mix_proj_dwz", [(h2, d_z)], "tn", out_dtype=BF16, rows=n_lat, tm=tm)
    d_wga = matmul("mix_proj_dwa", [(h2, d_ga)], "tn", out_dtype=BF16, rows=n_lat, tm=tm)
    d_wgb = matmul("mix_proj_dwb", [(h2, d_gb)], "tn", out_dtype=BF16, rows=n_lat, tm=tm)
    d_wxbc = matmul("mix_proj_dwx", [(h2, d_pxbc)], "tn", out_dtype=BF16, tm=tm)
    d_wdt = matmul("mix_proj_dwt", [(h2, ddt_f), (h2, ddt_b)], "tn", out_dtype=BF16, tm=tm)
    d_win = jnp.concatenate([d_wz, d_wxbc, d_wdt[:, :2 * n_head], d_wga, d_wgb], axis=1)
    d_x1, d_s3, d_s4, d_nmix = rowwise_bwd(
        "mix_norm_bwd", fn_norm_mod, [row(x1)], [tabs[3], tabs[4]], [norm_mix], [[row(d_h2)]], [(n_tok, F32, None)],
        tm=tm, n_tiles=n_tok // tm, seg_fn=seg_all, first_fn=first_all, adds={0: (row(d_x2), lat_tiles)})
    mix_names = ["w_in", "ssm_conv_w", "cconv_w"]
    plan.on("ffn1_down_dx", "sc_mix", lambda p: scatter(shard_cols(d_win), shard_cols(d_w5), shard_cols(d_w31)))
    plan.on("ffn1_up_dx", "sc_ffn1_down", lambda p: scatter(p.got["ffn1_d_wd"]))
    plan.on("ffn1_gate_dw", "sw_mix", lambda p: halves(mix_names, p.got["sc_mix"]))
    plan.on("ffn1_up_dw", "sc_ffn1_gate", lambda p: scatter(p.got["ffn1_d_wg"]))
    plan.on("ffn1_up_dw", "sw_ffn1_down", lambda p: halves(["ffn1_down"], p.got["sc_ffn1_down"]))
    plan.on("ffn1_norm_bwd", "sc_ffn1_up", lambda p: scatter(p.got["ffn1_d_wu"]))
    d_xt, (d_s0, d_s1, d_g2), d_nffn1 = _ffn_bwd(
        plan, "ffn1", d_x1, saved1, xt, n_tok, tm, seg_all, first_all, tabs[0], tabs[1], tabs[2], norm_ffn1, wg1, wu1, wd1,
        n_lat, lat_tiles)
    swapped.update(zip(mix_names + ["ffn1_down"], plan.got["sw_mix"] + plan.got["sw_ffn1_down"]))
    last_names = ["ffn1_gate", "ffn1_up"]
    last = halves(last_names, plan.got["sc_ffn1_gate"] + plan.got["sc_ffn1_up"])
    swapped.update(zip(last_names, exchange("swap_sibling", last.arrs, "sibling")))
    grad_x = d_xt.reshape(n_ex, seq_len, d)

    with_ctx0 = lambda t: jnp.concatenate([t, jnp.zeros((1, 1, d), F32)])
    d_tabs = [d_s0, d_s1, d_g2, d_s3, d_s4, with_ctx0(d_g5), with_ctx0(d_s6), with_ctx0(d_s7), with_ctx0(d_g8)]
    d_mod_rows = jnp.concatenate([t[:, 0, :] for t in d_tabs], axis=1)
    n_pad_rows = -(-(n_ex + 1) // 8) * 8
    d_mod_rows = jnp.concatenate([d_mod_rows, jnp.zeros((n_pad_rows - n_ex - 1, 9 * d), F32)])
    small = [("loss", loss_vec), ("norm_ffn1", d_nffn1), ("norm_mix", d_nmix), ("ssm_conv_b", d_conv_b),
             ("dt_bias_fwd", ddtb_f[:, :n_head]), ("dt_bias_bwd", ddtb_b[:, :n_head]), ("a_log_fwd", dalog_f[:, :n_head]),
             ("a_log_bwd", dalog_b[:, :n_head]), ("ssm_d", ddsk[:, :n_head]), ("ssm_norm_w", d_ssmnw), ("cconv_b", d_cb),
             ("cconv_ln_w", d_lnw), ("cconv_ln_b", d_lnb), ("norm_ffn2", d_nffn2), ("final_norm", d_final)]
    n_small = sum(v.size for _, v in small)
    n_pack = -(-n_small // (8 * LANES)) * (8 * LANES)
    pack = jnp.concatenate([v.reshape(-1) for _, v in small] + [jnp.zeros((n_pack - n_small,), F32)]).reshape(-1, LANES)
    pack_all, d_mod_all = exchange("gather_small", [pack, d_mod_rows], "all8")
    pack_sum = sum_slots("small_sum", pack_all)
    loss = loss_total(pack_sum.reshape(1, n_pack), d).reshape(())
    flat_sum = pack_sum.reshape(-1)
    small_grads, pos = {}, 0
    for nm, v in small:
        small_grads[nm] = flat_sum[pos:pos + v.size]
        pos += v.size
    d_mod_all = d_mod_all.reshape(8 * n_pad_rows, 9 * d)
    cond_rows = [jnp.concatenate([cond[j * n_ex:(j + 1) * n_ex], c_ctx[None, :],
                                  jnp.zeros((n_pad_rows - n_ex - 1, d), F32)]) for j in range(8)]
    cond_bwd = jnp.concatenate(cond_rows)
    d_mod_shard = lax.dynamic_slice(d_mod_all, (0, chip * mod_w), (8 * n_pad_rows, mod_w))
    g_wmod, g_bmod, q_part = mod_bwd(cond_bwd, d_mod_shard, d_mod_all, w_mod[0],
                                     tuple(j * n_pad_rows + n_ex for j in range(8)))
    (q_all,) = exchange("gather_cctx", [q_part], "all8")
    g_cctx = cctx_grad(q_all, c_ctx.reshape(1, d))
    small_grads["c_ctx"], small_grads["b_mod"] = g_cctx.reshape(-1), g_bmod.reshape(-1)

    results = {}
    for nm, both in swapped.items():
        shape = weights[nm].shape
        two_d = lambda t: t.reshape(shape[-2], shape[-1])
        g_full = both.reshape(1, -1, shape[-1])[:, :shape[-2]]
        results[nm] = [r.reshape(shape) for r in adamw(f"adamw_{nm}", two_d(weights[nm]), g_full, two_d(mom1[nm]), two_d(mom2[nm]))]
    results["w_mod"] = [r.reshape(w_mod.shape) for r in adamw("adamw_w_mod", w_mod[0], g_wmod[None], m_w_mod[0], v_w_mod[0])]
    small_names = [nm for nm in order if nm not in results]
    n_sm = sum(weights[nm].size for nm in small_names)
    n_smp = -(-n_sm // (8 * LANES)) * (8 * LANES)
    packed = lambda src: jnp.concatenate([src[nm].reshape(-1) for nm in small_names] + [jnp.zeros((n_smp - n_sm,), F32)]).reshape(-1, LANES)
    sm_out = adamw("adamw_small", packed(weights), packed(small_grads)[None], packed(mom1), packed(mom2))
    pos = 0
    for nm in small_names:
        size = weights[nm].size
        results[nm] = [r.reshape(-1)[pos:pos + size].reshape(weights[nm].shape) for r in sm_out]
        pos += size
    return (loss, grad_x, *[results[nm][0] for nm in order], *[results[nm][1] for nm in order],
            *[results[nm][2] for nm in order], *[results[nm][3] for nm in order])
```

```python
import functools
import math

import jax
import jax.numpy as jnp
from jax import lax
from jax.experimental import pallas as pl
from jax.experimental.pallas import tpu as pltpu

F32 = jnp.float32
BF16 = jnp.bfloat16
HI = lax.Precision.HIGHEST
MESH = pl.DeviceIdType.MESH

EPS = 1e-6
GRID_W = 64
HEAD_DIM = 64
N_STATE = 128
CHUNK = 128
LANES = 128
N_CHIPS = 4
ADAM_LR, ADAM_B1, ADAM_B2, ADAM_EPS, ADAM_WD, ADAM_STEP = 0.001, 0.9, 0.999, 1e-08, 0.01, 10
VMEM_CAP = 56 * 1024 * 1024


def _params(vmem_bytes=None, n_axes=1):
    kw = dict(dimension_semantics=("arbitrary",) * n_axes)
    if vmem_bytes is not None:
        kw["vmem_limit_bytes"] = int(min(VMEM_CAP, max(32 * 1024 * 1024, vmem_bytes)))
    return pltpu.CompilerParams(**kw)


def _nbytes(shape, dtype):
    return math.prod(shape) * jnp.dtype(dtype).itemsize


def _row_tile(rows, width, cap_bytes=1 << 20, mult=8):
    best = None
    for t in range(mult, rows + 1, mult):
        if rows % t == 0 and t * width * 4 <= cap_bytes:
            best = t
    return best if best is not None else rows


_MODES = {"all8": (8, (1, 2, 3, 4, 5, 6, 7), 0), "chips": (4, (2, 4, 6), 1), "sibling": (2, (1,), 0)}


class Rider:
    def __init__(self, arrs, mode, scatter=False):
        self.arrs, self.scatter = list(arrs), scatter
        self.nslot, self.deltas, self.shift = _MODES[mode]
        self.n = len(self.arrs)
        self.out_shape = [jax.ShapeDtypeStruct((self.nslot,) + (a.shape[1:] if scatter else a.shape), a.dtype)
                          for a in self.arrs]
        any_spec = pl.BlockSpec(memory_space=pl.ANY)
        self.in_specs = [any_spec] * self.n
        self.out_specs = [any_spec] * self.n
        n_peer = len(self.deltas)
        self.scratch = [pltpu.SemaphoreType.DMA((self.n, n_peer)), pltpu.SemaphoreType.DMA((self.n, n_peer)),
                        pltpu.SemaphoreType.DMA((self.n,))]

    def _copies(self, ins, outs, sems, arrivals):
        send_sems, recv_sems, local_sems = sems
        x, y, c = lax.axis_index("x"), lax.axis_index("y"), lax.axis_index("c")
        me = 4 * x + 2 * y + c
        slot_of = lambda dev: (dev >> self.shift) & (self.nslot - 1)
        src = lambda a, slot: ins[a].at[slot] if self.scatter else ins[a]
        flip = lambda v, bit: 1 - v if bit else v

        def remote(a, k, d, from_slot, to_slot):
            return pltpu.make_async_remote_copy(
                src_ref=src(a, from_slot), dst_ref=outs[a].at[to_slot], send_sem=send_sems.at[a, k],
                recv_sem=recv_sems.at[a, k], device_id=(flip(x, (d >> 2) & 1), flip(y, (d >> 1) & 1), flip(c, d & 1)),
                device_id_type=MESH)

        mine = slot_of(me)
        local = [pltpu.make_async_copy(src(a, mine), outs[a].at[mine], local_sems.at[a]) for a in range(self.n)]
        sends = [remote(a, k, d, slot_of(me ^ d), mine) for k, d in enumerate(self.deltas) for a in range(self.n)]
        if not arrivals:
            return local, sends
        return local, sends, [remote(a, k, d, mine, slot_of(me ^ d)) for k, d in enumerate(self.deltas) for a in range(self.n)]

    def start(self, ins, outs, sems):
        local, sends = self._copies(ins, outs, sems, arrivals=False)
        for cp in local + sends:
            cp.start()

    def wait(self, ins, outs, sems):
        local, sends, recvs = self._copies(ins, outs, sems, arrivals=True)
        for cp in recvs:
            cp.wait_recv()
        for cp in sends:
            cp.wait_send()
        for cp in local:
            cp.wait()


class Riders:
    def __init__(self, riders):
        self.riders = list(riders)
        self.n = sum(r.n for r in self.riders)
        cat = lambda attr: [v for r in self.riders for v in getattr(r, attr)]
        self.arrs, self.out_shape, self.in_specs = cat("arrs"), cat("out_shape"), cat("in_specs")
        self.out_specs, self.scratch = cat("out_specs"), cat("scratch")

    def _each(self, method, ins, outs, sems):
        i = s = 0
        for r in self.riders:
            getattr(r, method)(ins[i:i + r.n], outs[i:i + r.n], sems[s:s + len(r.scratch)])
            i, s = i + r.n, s + len(r.scratch)

    def start(self, ins, outs, sems):
        self._each("start", ins, outs, sems)

    def wait(self, ins, outs, sems):
        self._each("wait", ins, outs, sems)


class _Hosted:
    def __init__(self, rider, n_in, n_out, n_scratch, grid):
        self.rider, self.n_in, self.n_out, self.n_scratch, self.grid = rider, n_in, n_out, n_scratch, grid
        self.n = rider.n if rider else 0

    def split(self, refs):
        a, b = self.n_in, self.n_in + self.n
        c, e = b + self.n_out, b + self.n_out + self.n
        self._r = (refs[a:b], refs[c:e], refs[e + self.n_scratch:])
        if self.rider:
            ids = [pl.program_id(ax) for ax in range(len(self.grid))]
            first = functools.reduce(jnp.logical_and, [i == 0 for i in ids]) if ids else True
            pl.when(first)(lambda: self.rider.start(*self._r))
        return refs[:a], refs[b:c], refs[e:e + self.n_scratch]

    def finish(self):
        if self.rider:
            ids = [pl.program_id(ax) for ax in range(len(self.grid))]
            last = functools.reduce(jnp.logical_and, [i == n - 1 for i, n in zip(ids, self.grid)]) if ids else True
            pl.when(last)(lambda: self.rider.wait(*self._r))

    def call_args(self, in_specs, out_shape, out_specs, scratch, args):
        r = self.rider
        if not r:
            return list(in_specs), tuple(out_shape), tuple(out_specs), list(scratch), list(args)
        return (list(in_specs) + r.in_specs, tuple(out_shape) + tuple(r.out_shape), tuple(out_specs) + tuple(r.out_specs),
                list(scratch) + r.scratch, list(args) + r.arrs)

    def results(self, res, unwrap=True):
        res = list(res) if isinstance(res, (tuple, list)) else [res]
        host = res[:self.n_out]
        host = host[0] if (self.n_out == 1 and unwrap) else tuple(host)
        return (host, res[self.n_out:]) if self.rider else host


def exchange(name, arrs, mode, scatter=False):
    rider = Rider(arrs, mode, scatter)

    def body(*refs):
        ins, outs, sems = refs[:rider.n], refs[rider.n:2 * rider.n], refs[2 * rider.n:]
        rider.start(ins, outs, sems)
        rider.wait(ins, outs, sems)

    return pl.pallas_call(
        body, name=name, out_shape=tuple(rider.out_shape), in_specs=rider.in_specs, out_specs=tuple(rider.out_specs),
        scratch_shapes=rider.scratch,
    )(*arrs)


_DIMS = {"nn": (((1,), (0,)), ((), ())), "nt": (((1,), (1,)), ((), ())), "tn": (((0,), (0,)), ((), ()))}


def matmul(name, pairs, kind, *, a_ch=False, b_ch=False, out_ch=False, out_dtype=F32, rows=None, row_off=0, tm=512,
           rider=None, post=None, fold=False):
    a0, b0 = pairs[0]
    n_chunk = a0.shape[0] if a_ch else (b0.shape[0] if b_ch else 1)
    total_rows = a0.shape[-2]
    rows = total_rows - row_off if rows is None else rows
    tm = min(tm, rows)
    assert rows % tm == 0 and row_off % tm == 0, (name, rows, tm, row_off)
    n_rt, off = rows // tm, row_off // tm
    dims = _DIMS[kind]
    n_pair = len(pairs)

    if kind == "tn":
        grid, red_axis, n_red = (n_chunk, n_rt), 1, n_rt
        a_idx = (lambda k, i: (k, i + off, 0)) if a_ch else (lambda k, i: (i + off, 0))
        b_idx = (lambda k, i: (k, i + off, 0)) if b_ch else (lambda k, i: (i + off, 0))
        a_blk = lambda a: ((None, tm, a.shape[-1]) if a_ch else (tm, a.shape[-1]))
        b_blk = lambda b: ((None, tm, b.shape[-1]) if b_ch else (tm, b.shape[-1]))
        o2 = (a0.shape[-1], b0.shape[-1])
        out_shape = ((n_chunk,) + o2) if out_ch else o2
        out_spec = pl.BlockSpec((None,) + o2, lambda k, i: (k, 0, 0)) if out_ch else pl.BlockSpec(o2, lambda k, i: (0, 0))
        acc_shape = o2
    else:
        n_out = b0.shape[-1] if kind == "nn" else b0.shape[-2]
        b2 = b0.shape[-2:]
        if a_ch and b_ch and not out_ch and fold:
            grid, red_axis, n_red = (n_rt,), None, 1
            a_idx, b_idx = (lambda i: (0, i + off, 0)), (lambda i: (0, 0, 0))
            a_blk = lambda a: (n_chunk, tm, a.shape[-1])
            b_blk = lambda b: tuple(b.shape)
            out_shape, out_spec = (rows, n_out), pl.BlockSpec((tm, n_out), lambda i: (i, 0))
        elif a_ch and b_ch and not out_ch:
            grid, red_axis, n_red = (n_rt, n_chunk), 1, n_chunk
            a_idx, b_idx = (lambda i, k: (k, i + off, 0)), (lambda i, k: (k, 0, 0))
            a_blk = lambda a: (None, tm, a.shape[-1])
            b_blk = lambda b: (None,) + tuple(b.shape[-2:])
            out_shape, out_spec = (rows, n_out), pl.BlockSpec((tm, n_out), lambda i, k: (i, 0))
        elif out_ch:
            assert b_ch and not a_ch
            grid, red_axis, n_red = (n_chunk, n_rt), None, 1
            a_idx, b_idx = (lambda k, i: (i + off, 0)), (lambda k, i: (k, 0, 0))
            a_blk = lambda a: (tm, a.shape[-1])
            b_blk = lambda b: (None,) + tuple(b.shape[-2:])
            out_shape, out_spec = (n_chunk, rows, n_out), pl.BlockSpec((None, tm, n_out), lambda k, i: (k, i, 0))
        else:
            assert not (a_ch or b_ch)
            grid, red_axis, n_red = (n_rt,), None, 1
            a_idx, b_idx = (lambda i: (i + off, 0)), (lambda i: (0, 0))
            a_blk = lambda a: (tm, a.shape[-1])
            b_blk = lambda b: tuple(b.shape)
            out_shape, out_spec = (rows, n_out), pl.BlockSpec((tm, n_out), lambda i: (i, 0))
        acc_shape = (tm, n_out)

    post_ins, post_fn, out_dtypes = ([], None, [out_dtype]) if post is None else post
    hosted = _Hosted(rider, 2 * n_pair + len(post_ins), len(out_dtypes), int(n_red > 1), grid)

    def body(*refs):
        ins, outs, scr = hosted.split(refs)

        def compute():
            acc = None
            for p in range(n_pair):
                for k in ([None] if not fold else range(n_chunk)):
                    pick = (lambda r: r[...]) if k is None else (lambda r: r[k])
                    d = lax.dot_general(pick(ins[2 * p]).astype(BF16), pick(ins[2 * p + 1]).astype(BF16), dims,
                                        preferred_element_type=F32)
                    acc = d if acc is None else acc + d
            return acc

        def emit(acc):
            vals = (acc,) if post_fn is None else post_fn(acc, *[r[...].astype(F32) for r in ins[2 * n_pair:]])
            for o_ref, v in zip(outs, vals):
                o_ref[...] = v.astype(o_ref.dtype)

        if n_red == 1:
            emit(compute())
        else:
            acc_ref = scr[0]
            r = pl.program_id(red_axis)

            @pl.when(r == 0)
            def _():
                acc_ref[...] = jnp.zeros_like(acc_ref)

            acc_ref[...] += compute()

            @pl.when(r == n_red - 1)
            def _():
                emit(acc_ref[...])
        hosted.finish()

    in_specs, args, vmem = [], [], 0
    for a, b in pairs:
        in_specs += [pl.BlockSpec(a_blk(a), a_idx), pl.BlockSpec(b_blk(b), b_idx)]
        args += [a, b]
        vmem += 2 * (_nbytes([s for s in a_blk(a) if s], a.dtype) + _nbytes([s for s in b_blk(b) if s], b.dtype))
    in_specs += [out_spec] * len(post_ins)
    args += list(post_ins)
    vmem += (3 + 2 * n_pair + 2 * len(post_ins) + 2 * len(out_dtypes)) * _nbytes(acc_shape, F32)
    scratch = [pltpu.VMEM(acc_shape, F32)] if n_red > 1 else []
    in_specs, out_shapes, out_specs, scratch, args = hosted.call_args(
        in_specs, [jax.ShapeDtypeStruct(out_shape, dt) for dt in out_dtypes], [out_spec] * len(out_dtypes), scratch, args)
    return hosted.results(pl.pallas_call(
        body, name=name, out_shape=out_shapes, grid=grid, in_specs=in_specs, out_specs=out_specs,
        scratch_shapes=scratch, compiler_params=_params(vmem + (8 << 20), len(grid)),
    )(*args))


def row(arr, width=None, cb=0, roff=0):
    return (arr, arr.shape[-1] if width is None else width, cb, roff)


def _row_spec(desc, tm, limit=None):
    _, width, cb, roff = desc
    if limit is None:
        return pl.BlockSpec((tm, width), lambda i: (i + roff, cb))
    return pl.BlockSpec((tm, width), lambda i: (jnp.minimum(i, limit - 1) + roff, cb))


def _segmenter(tm, seq_len, n_lat):
    seg = lambda i: jnp.where(i * tm < n_lat, (i * tm) // seq_len, n_lat // seq_len)
    first = lambda i: jnp.where(i * tm < n_lat, (i * tm) % seq_len == 0, i * tm == n_lat)
    return seg, first


def rowwise(name, fn, rows, segs, params, outs, *, tm, n_tiles, seg_fn=None, rider=None):
    n_r, n_s, n_p = len(rows), len(segs), len(params)
    hosted = _Hosted(rider, n_r + n_s + n_p, len(outs), 0, (n_tiles,))

    def body(*refs):
        ins, out_refs, _ = hosted.split(refs)
        vals = [r[...].astype(F32) for r in ins[:n_r]] + [r[...] for r in ins[n_r:]]
        res = fn(*vals)
        for o_ref, v in zip(out_refs, res):
            o_ref[...] = v.astype(o_ref.dtype)
        hosted.finish()

    in_specs = [_row_spec(d, tm) for d in rows]
    in_specs += [pl.BlockSpec((None, 1, s.shape[-1]), lambda i: (seg_fn(i), 0, 0)) for s in segs]
    in_specs += [pl.BlockSpec(p.shape, lambda i: (0, 0)) for p in params]
    vmem = sum(2 * tm * d[1] * 4 for d in rows) + sum(3 * tm * w * 4 for _, w, _ in outs) + sum(2 * p.size * 4 for p in params)
    in_specs, out_shapes, out_specs, scratch, args = hosted.call_args(
        in_specs, [jax.ShapeDtypeStruct((r, w), dt) for r, w, dt in outs],
        [pl.BlockSpec((tm, w), lambda i: (i, 0)) for _, w, _ in outs], [], [d[0] for d in rows] + list(segs) + list(params))
    return hosted.results(pl.pallas_call(
        body, name=name, grid=(n_tiles,), in_specs=in_specs, out_shape=out_shapes, out_specs=out_specs,
        scratch_shapes=scratch, compiler_params=_params(2 * vmem + (8 << 20)),
    )(*args), unwrap=False)


def rowwise_bwd(name, fn, rows, segs, params, cts, row_grads, *, tm, n_tiles, seg_fn=None, first_fn=None, adds=None,
                rider=None):
    adds = adds or {}
    need = [k for k, v in enumerate(row_grads) if v is not None]
    n_r, n_s, n_p = len(rows), len(segs), len(params)
    n_ct = sum(len(lst) for lst in cts)
    add_keys = sorted(adds)
    hosted = _Hosted(rider, n_r + n_s + n_p + n_ct + len(add_keys), len(need) + n_s + n_p, 0, (n_tiles,))

    def body(*refs):
        host_in, host_out, _ = hosted.split(refs)
        it = iter(list(host_in) + list(host_out))
        row_refs = [next(it) for _ in range(n_r)]
        seg_refs = [next(it) for _ in range(n_s)]
        par_refs = [next(it) for _ in range(n_p)]
        ct_refs = [[next(it) for _ in lst] for lst in cts]
        add_refs = {k: next(it) for k in add_keys}
        rg_refs = {k: next(it) for k in need}
        sg_refs = [next(it) for _ in range(n_s)]
        pg_refs = [next(it) for _ in range(n_p)]
        i = pl.program_id(0)
        rv = [r[...].astype(F32) for r in row_refs]
        sv = [r[...] for r in seg_refs]
        pv = [r[...] for r in par_refs]

        def f(*args):
            rr = list(rv)
            for j, k in enumerate(need):
                rr[k] = args[j]
            return fn(*rr, *args[len(need):])

        _, vjp = jax.vjp(f, *[rv[k] for k in need], *sv, *pv)
        ctv = []
        for lst in ct_refs:
            acc = lst[0][...].astype(F32)
            for r in lst[1:]:
                acc = acc + r[...].astype(F32)
            ctv.append(acc)
        g = vjp(tuple(ctv))
        for j, k in enumerate(need):
            gv = g[j]
            if k in adds:
                lim = adds[k][1]
                av = add_refs[k][...].astype(F32)
                gv = gv + (av if lim is None else jnp.where(i < lim, av, 0.0))
            lim = row_grads[k][2]
            if lim is None:
                rg_refs[k][...] = gv.astype(rg_refs[k].dtype)
            else:
                @pl.when(i < lim)
                def _(gv=gv, k=k):
                    rg_refs[k][...] = gv.astype(rg_refs[k].dtype)
        if n_s:
            opens = first_fn(i)
            for ref, gv in zip(sg_refs, g[len(need):len(need) + n_s]):
                @pl.when(opens)
                def _(ref=ref, gv=gv):
                    ref[...] = gv

                @pl.when(jnp.logical_not(opens))
                def _(ref=ref, gv=gv):
                    ref[...] += gv
        for ref, gv in zip(pg_refs, g[len(need) + n_s:]):
            @pl.when(i == 0)
            def _(ref=ref, gv=gv):
                ref[...] = gv

            @pl.when(i > 0)
            def _(ref=ref, gv=gv):
                ref[...] += gv
        hosted.finish()

    seg_spec = lambda s: pl.BlockSpec((None, 1, s.shape[-1]), lambda i: (seg_fn(i), 0, 0))
    par_spec = lambda p: pl.BlockSpec(p.shape, lambda i: (0, 0))
    in_specs = [_row_spec(d, tm) for d in rows] + [seg_spec(s) for s in segs] + [par_spec(p) for p in params]
    args = [d[0] for d in rows] + list(segs) + list(params)
    for lst in cts:
        in_specs += [_row_spec(d, tm) for d in lst]
        args += [d[0] for d in lst]
    for k in add_keys:
        in_specs.append(_row_spec(adds[k][0], tm, adds[k][1]))
        args.append(adds[k][0][0])
    out_shape, out_specs = [], []
    for k in need:
        n_rows, dt, lim = row_grads[k]
        out_shape.append(jax.ShapeDtypeStruct((n_rows, rows[k][1]), dt))
        out_specs.append(_row_spec((None, rows[k][1], 0, 0), tm, lim))
    for s in segs:
        out_shape.append(jax.ShapeDtypeStruct(s.shape, F32))
        out_specs.append(seg_spec(s))
    for p in params:
        out_shape.append(jax.ShapeDtypeStruct(p.shape, F32))
        out_specs.append(par_spec(p))
    vmem = sum(tm * d[1] * 4 for d in rows) * 6 + n_ct * tm * max(d[1] for d in rows) * 8
    in_specs, out_shape, out_specs, scratch, args = hosted.call_args(in_specs, out_shape, out_specs, [], args)
    return hosted.results(pl.pallas_call(
        body, name=name, grid=(n_tiles,), in_specs=in_specs, out_shape=out_shape, out_specs=out_specs,
        scratch_shapes=scratch, compiler_params=_params(vmem + (8 << 20)),
    )(*args), unwrap=False)


def _silu(v):
    return v * jax.nn.sigmoid(v)


def _rms(v, w):
    return v * lax.rsqrt(jnp.mean(v * v, axis=-1, keepdims=True) + EPS) * w


def fn_norm_mod(x, shift, scale, w):
    return (_rms(x, w) * (1.0 + scale) + shift,)


def fn_act(g, u):
    return (_silu(g) * u,)


def make_fn_resid(coef):
    def fn(x, f, gate):
        return (x + coef * gate * f,)
    return fn


def fn_silu_bias(v, b):
    return (_silu(v + b),)


def make_fn_gate_groupnorm(width):
    half = width // 2

    def fn(yf, yb, z, w):
        y = (yf + yb) * _silu(z)
        lane = lax.broadcasted_iota(jnp.int32, y.shape, 1)
        lo = lane < half
        sq = y * y
        s_lo = jnp.sum(jnp.where(lo, sq, 0.0), axis=-1, keepdims=True)
        s_hi = jnp.sum(jnp.where(lo, 0.0, sq), axis=-1, keepdims=True)
        r = jnp.where(lo, lax.rsqrt(s_lo / half + EPS), lax.rsqrt(s_hi / half + EPS))
        return (y * r * w,)
    return fn


def fn_glu(a, b):
    return (a * jax.nn.sigmoid(b),)


def fn_ln_silu(vw, vh, cb, lw, lb):
    v = jnp.concatenate([vw, vh], axis=-1) + cb
    mu = jnp.mean(v, axis=-1, keepdims=True)
    var = jnp.mean(jnp.square(v - mu), axis=-1, keepdims=True)
    return (_silu((v - mu) * lax.rsqrt(var + EPS) * lw + lb),)


def _col_tile(width):
    return width // 3 if width % (3 * LANES) == 0 else width


def mod_fwd(a_rows, w_shard, b_shard):
    n, d = a_rows.shape
    ws = w_shard.shape[1]
    tn = _col_tile(ws)

    def body(a_ref, w_ref, b_ref, o_ref):
        a = _silu(a_ref[...]).astype(BF16)
        o_ref[...] = jnp.dot(a, w_ref[...].astype(BF16), preferred_element_type=F32) + b_ref[...]

    return pl.pallas_call(
        body, name="mod_fwd", grid=(ws // tn,), out_shape=jax.ShapeDtypeStruct((n, ws), F32),
        in_specs=[pl.BlockSpec((n, d), lambda j: (0, 0)), pl.BlockSpec((d, tn), lambda j: (0, j)),
                  pl.BlockSpec((1, tn), lambda j: (0, j))],
        out_specs=pl.BlockSpec((n, tn), lambda j: (0, j)), compiler_params=_params(),
    )(a_rows, w_shard, b_shard)


def mod_bwd(a_rows, d_shard, d_full, w_shard, ctx_rows):
    n, d = a_rows.shape
    ws = w_shard.shape[1]
    tn = _col_tile(ws)
    n_ct = ws // tn

    def body(a_ref, ds_ref, df_ref, w_ref, gw_ref, gb_ref, q_ref):
        j = pl.program_id(0)
        a = _silu(a_ref[...])
        ds = ds_ref[...]
        gw_ref[...] = lax.dot_general(a, ds, _DIMS["tn"], precision=HI, preferred_element_type=F32)
        dctx = ds[ctx_rows[0]:ctx_rows[0] + 1, :]
        for r in ctx_rows[1:]:
            dctx = dctx + ds[r:r + 1, :]
        q = lax.dot_general(jnp.broadcast_to(dctx, (8, tn)), w_ref[...], _DIMS["nt"], precision=HI,
                            preferred_element_type=F32)

        @pl.when(j == 0)
        def _():
            q_ref[...] = q
            df = df_ref[...]
            acc = df[0:1, :]
            for r in range(1, n):
                acc = acc + df[r:r + 1, :]
            gb_ref[...] = acc

        @pl.when(j > 0)
        def _():
            q_ref[...] += q

    return pl.pallas_call(
        body, name="mod_bwd", grid=(n_ct,),
        out_shape=(jax.ShapeDtypeStruct((d, ws), F32), jax.ShapeDtypeStruct((1, d_full.shape[1]), F32),
                   jax.ShapeDtypeStruct((8, d), F32)),
        in_specs=[pl.BlockSpec((n, d), lambda j: (0, 0)), pl.BlockSpec((n, tn), lambda j: (0, j)),
                  pl.BlockSpec(d_full.shape, lambda j: (0, 0)), pl.BlockSpec((d, tn), lambda j: (0, j))],
        out_specs=(pl.BlockSpec((d, tn), lambda j: (0, j)), pl.BlockSpec((1, d_full.shape[1]), lambda j: (0, 0)),
                   pl.BlockSpec((8, d), lambda j: (0, 0))),
        compiler_params=_params(40 << 20),
    )(a_rows, d_shard, d_full, w_shard)


def _shifted(xs, d, tok, width):
    if d == 0:
        return xs
    n = xs.shape[0]
    sh = pltpu.roll(xs, (-d) % n, axis=0)
    return jnp.where((tok + d >= 0) & (tok + d < width), sh, 0.0)


def _placed(out_shape, place):
    if place is None:
        return out_shape, 0, 0, None
    return place


def tapsum_roll(name, x, xcb, w, wcb, *, seq_len, n_seq, row_blk_off, width, piece, cb, ncb, pad, flip, place=None):
    n_tap = w.shape[0]
    n_piece = seq_len // piece
    out_shape, o_rb, o_cb, into = _placed((n_seq * seq_len, ncb * cb), place)

    def body(x_ref, w_ref, *rest):
        o_ref = rest[-1]
        wv = w_ref[...]
        tok = lax.broadcasted_iota(jnp.int32, (piece, 1), 0) % width

        def do_piece(p, carry):
            start = pl.multiple_of(p * piece, piece)
            xs = x_ref[pl.ds(start, piece), :]
            acc = jnp.zeros_like(xs)
            for k in range(n_tap):
                d = pad - k if flip else k - pad
                acc = acc + wv[k:k + 1, :] * _shifted(xs, d, tok, width)
            o_ref[pl.ds(start, piece), :] = acc
            return carry

        lax.fori_loop(0, n_piece, do_piece, 0)

    extra = [] if into is None else [into]
    return pl.pallas_call(
        body, name=name, grid=(ncb, n_seq), out_shape=jax.ShapeDtypeStruct(out_shape, F32),
        in_specs=[pl.BlockSpec((seq_len, cb), lambda j, s: (row_blk_off + s, xcb + j)),
                  pl.BlockSpec((n_tap, cb), lambda j, s: (0, wcb + j))] + [pl.BlockSpec(memory_space=pl.ANY)] * len(extra),
        out_specs=pl.BlockSpec((seq_len, cb), lambda j, s: (o_rb + s, o_cb + j)),
        input_output_aliases={2: 0} if extra else {},
        compiler_params=_params(8 * seq_len * cb * 4 + (8 << 20), 2),
    )(x, w, *extra)


def tapgrad_roll(name, dy, dycb, dy_blk_off, x, xcb, x_blk_off, *, n_tap, seq_len, n_seq, width, piece, cb, ncb, pad):
    n_piece = seq_len // piece

    def body(dy_ref, x_ref, o_ref):
        @pl.when(pl.program_id(1) == 0)
        def _():
            o_ref[...] = jnp.zeros_like(o_ref)

        tok = lax.broadcasted_iota(jnp.int32, (piece, 1), 0) % width

        def do_piece(p, carry):
            start = pl.multiple_of(p * piece, piece)
            xs = x_ref[pl.ds(start, piece), :]
            dv = dy_ref[pl.ds(start, piece), :]
            for k in range(n_tap):
                o_ref[k:k + 1, :] += jnp.sum(dv * _shifted(xs, k - pad, tok, width), axis=0, keepdims=True)
            return carry

        lax.fori_loop(0, n_piece, do_piece, 0)

    return pl.pallas_call(
        body, name=name, grid=(ncb, n_seq), out_shape=jax.ShapeDtypeStruct((n_tap, ncb * cb), F32),
        in_specs=[pl.BlockSpec((seq_len, cb), lambda j, s: (dy_blk_off + s, dycb + j)),
                  pl.BlockSpec((seq_len, cb), lambda j, s: (x_blk_off + s, xcb + j))],
        out_specs=pl.BlockSpec((n_tap, cb), lambda j, s: (0, j)),
        compiler_params=_params(8 * seq_len * cb * 4 + (8 << 20), 2),
    )(dy, x)


def tapsum_rows(name, x, xcb, w, wcb, *, seq_len, n_seq, cb, ncb, pad, flip, place=None):
    n_tap = w.shape[0]
    n_row = seq_len // GRID_W
    halo = pad * GRID_W
    out_shape, o_rb, o_cb, into = _placed((n_seq * seq_len, ncb * cb), place)

    def body(x_ref, w_ref, *rest):
        o_ref, xp = rest[-2:]
        xp[pl.ds(0, halo), :] = jnp.zeros((halo, cb), F32)
        xp[pl.ds(halo + seq_len, halo), :] = jnp.zeros((halo, cb), F32)
        xp[pl.ds(halo, seq_len), :] = x_ref[...]
        wv = w_ref[...]

        def do_row(r, carry):
            acc = jnp.zeros((GRID_W, cb), F32)
            for k in range(n_tap):
                d = pad - k if flip else k - pad
                acc = acc + wv[k:k + 1, :] * xp[pl.ds(pl.multiple_of((r + pad + d) * GRID_W, GRID_W), GRID_W), :]
            o_ref[pl.ds(pl.multiple_of(r * GRID_W, GRID_W), GRID_W), :] = acc
            return carry

        lax.fori_loop(0, n_row, do_row, 0)

    extra = [] if into is None else [into]
    return pl.pallas_call(
        body, name=name, grid=(ncb, n_seq), out_shape=jax.ShapeDtypeStruct(out_shape, F32),
        in_specs=[pl.BlockSpec((seq_len, cb), lambda j, s: (s, xcb + j)),
                  pl.BlockSpec((n_tap, cb), lambda j, s: (0, wcb + j))] + [pl.BlockSpec(memory_space=pl.ANY)] * len(extra),
        out_specs=pl.BlockSpec((seq_len, cb), lambda j, s: (o_rb + s, o_cb + j)),
        input_output_aliases={2: 0} if extra else {},
        scratch_shapes=[pltpu.VMEM((seq_len + 2 * halo, cb), F32)],
        compiler_params=_params(10 * seq_len * cb * 4 + (8 << 20), 2),
    )(x, w, *extra)


def tapgrad_rows(name, dy, dycb, x, xcb, *, n_tap, seq_len, n_seq, cb, ncb, pad):
    n_row = seq_len // GRID_W
    halo = pad * GRID_W

    def body(dy_ref, x_ref, o_ref, xp):
        @pl.when(pl.program_id(1) == 0)
        def _():
            o_ref[...] = jnp.zeros_like(o_ref)

        xp[pl.ds(0, halo), :] = jnp.zeros((halo, cb), F32)
        xp[pl.ds(halo + seq_len, halo), :] = jnp.zeros((halo, cb), F32)
        xp[pl.ds(halo, seq_len), :] = x_ref[...]

        def do_row(r, carry):
            dv = dy_ref[pl.ds(pl.multiple_of(r * GRID_W, GRID_W), GRID_W), :]
            for k in range(n_tap):
                xs = xp[pl.ds(pl.multiple_of((r + k) * GRID_W, GRID_W), GRID_W), :]
                o_ref[k:k + 1, :] += jnp.sum(dv * xs, axis=0, keepdims=True)
            return carry

        lax.fori_loop(0, n_row, do_row, 0)

    return pl.pallas_call(
        body, name=name, grid=(ncb, n_seq), out_shape=jax.ShapeDtypeStruct((n_tap, ncb * cb), F32),
        in_specs=[pl.BlockSpec((seq_len, cb), lambda j, s: (s, dycb + j)),
                  pl.BlockSpec((seq_len, cb), lambda j, s: (s, xcb + j))],
        out_specs=pl.BlockSpec((n_tap, cb), lambda j, s: (0, j)),
        scratch_shapes=[pltpu.VMEM((seq_len + 2 * halo, cb), F32)],
        compiler_params=_params(10 * seq_len * cb * 4 + (8 << 20), 2),
    )(dy, x)


def _ssd_blocks(b, s, *, rev, n_ctx, n_lat, lat_blocks):
    if rev:
        return jnp.where(s < n_ctx, lat_blocks + b * n_ctx + (n_ctx - 1 - s), b * n_lat + (n_lat - 1 - (s - n_ctx)))
    return jnp.where(s < n_ctx, lat_blocks + b * n_ctx + s, b * n_lat + (s - n_ctx))


def _ssd_common(xbc, raw, dtb, alog, dsk, *, rev, ds, n_head):
    if rev:
        raw = pltpu.roll(raw, LANES - n_head, axis=1)
    pre = raw + dtb
    dt = jnp.maximum(pre, 0.0) + jnp.log1p(jnp.exp(-jnp.abs(pre)))
    sig = jax.nn.sigmoid(pre)
    a = -jnp.exp(alog)
    da = dt * a
    ri = lax.broadcasted_iota(jnp.int32, (CHUNK, CHUNK), 0)
    ci = lax.broadcasted_iota(jnp.int32, (CHUNK, CHUNK), 1)
    mask = (ci >= ri) if rev else (ci <= ri)
    tri = mask.astype(F32)
    tri_t = ((ci <= ri) if rev else (ci >= ri)).astype(F32)
    cs = jnp.dot(tri, da, precision=HI, preferred_element_type=F32)
    tot = jnp.sum(da, axis=0, keepdims=True)
    def wide(v):
        first = lax.broadcasted_iota(jnp.int32, (v.shape[0], LANES), 1) < HEAD_DIM
        return jnp.concatenate(
            [jnp.where(first, jnp.broadcast_to(v[:, 2 * p:2 * p + 1], first.shape),
                       jnp.broadcast_to(v[:, 2 * p + 1:2 * p + 2], first.shape)) for p in range(n_head // 2)], axis=1)

    cs_w, tot_w = wide(cs), wide(tot)
    xh = xbc[:, :ds]
    dt_w = wide(dt)
    return dict(
        dt=dt, sig=sig, a=a, cs=cs, cs_t=cs.T, tot=tot, mask=mask, tri_t=tri_t,
        e_w=jnp.exp(cs_w), wt_w=jnp.exp(tot_w - cs_w), dec_w=jnp.exp(tot_w), dt_w=dt_w, dsk_w=wide(dsk),
        xh=xh, xs_w=xh * dt_w, bm=xbc[:, ds:ds + 2 * N_STATE], cm=xbc[:, ds + 2 * N_STATE:ds + 4 * N_STATE])


def _decay(q, col):
    seg = q["cs"][:, col:col + 1] - q["cs_t"][col:col + 1, :]
    return jnp.exp(jnp.where(q["mask"], seg, -jnp.inf))


def _split_heads(v):
    lane = lax.broadcasted_iota(jnp.int32, v.shape, 1)
    return jnp.concatenate([jnp.where(lane < HEAD_DIM, v, 0.0), jnp.where(lane >= HEAD_DIM, v, 0.0)], axis=0)


def ssd_fwd(name, xbc, proj, dt_cb, dtb, alog, dsk, *, rev, n_ex, seq_len, ctx_len, ds, rider=None):
    n_head, half = ds // HEAD_DIM, ds // 2
    n_ctx, n_lat = ctx_len // CHUNK, seq_len // CHUNK
    n_step = n_ctx + n_lat
    blk = functools.partial(_ssd_blocks, rev=rev, n_ctx=n_ctx, n_lat=n_lat, lat_blocks=n_ex * n_lat)
    xw = xbc.shape[1]

    def y_blk(b, s):
        sl = jnp.maximum(s, n_ctx) - n_ctx
        return b * n_lat + ((n_lat - 1 - sl) if rev else sl)

    hosted = _Hosted(rider, 5, 2, 1, (n_ex, n_step))

    def body(*refs):
        (xbc_ref, dt_ref, dtb_ref, alog_ref, dsk_ref), (y_ref, hs_ref), (h_scr,) = hosted.split(refs)

        @pl.when(pl.program_id(1) == 0)
        def _():
            h_scr[...] = jnp.zeros_like(h_scr)

        q = _ssd_common(xbc_ref[...], dt_ref[...], dtb_ref[...], alog_ref[...], dsk_ref[...], rev=rev, ds=ds, n_head=n_head)
        h = h_scr[...]
        hs_ref[...] = h
        for g in range(2):
            lo = g * half
            bg = q["bm"][:, g * N_STATE:(g + 1) * N_STATE].astype(BF16)
            cg = q["cm"][:, g * N_STATE:(g + 1) * N_STATE].astype(BF16)
            scores = lax.dot_general(cg, bg, _DIMS["nt"], preferred_element_type=F32)
            hg = h[:, lo:lo + half]
            off = jnp.dot(cg, hg.astype(BF16), preferred_element_type=F32)
            for j in range(half // LANES):
                c0 = (lo + j * LANES) // HEAD_DIM
                ln = slice(lo + j * LANES, lo + (j + 1) * LANES)
                p_cat = jnp.concatenate([scores * _decay(q, c0), scores * _decay(q, c0 + 1)], axis=1).astype(BF16)
                diag = jnp.dot(p_cat, _split_heads(q["xs_w"][:, ln]).astype(BF16), preferred_element_type=F32)
                y_ref[:, ln] = (diag + q["e_w"][:, ln] * off[:, j * LANES:(j + 1) * LANES]
                                + q["dsk_w"][:, ln] * q["xh"][:, ln])
            v = (q["wt_w"][:, lo:lo + half] * q["xs_w"][:, lo:lo + half]).astype(BF16)
            h_scr[:, lo:lo + half] = (q["dec_w"][:, lo:lo + half] * hg
                                      + lax.dot_general(bg, v, _DIMS["tn"], preferred_element_type=F32))
        hosted.finish()

    vec = pl.BlockSpec((1, LANES), lambda b, s: (0, 0))
    in_specs, out_shape, out_specs, scratch, args = hosted.call_args(
        [pl.BlockSpec((CHUNK, xw), lambda b, s: (blk(b, s), 0)),
         pl.BlockSpec((CHUNK, LANES), lambda b, s: (blk(b, s), dt_cb)), vec, vec, vec],
        (jax.ShapeDtypeStruct((n_ex * seq_len, ds), F32), jax.ShapeDtypeStruct((n_ex, n_step, N_STATE, ds), F32)),
        (pl.BlockSpec((CHUNK, ds), lambda b, s: (y_blk(b, s), 0)),
         pl.BlockSpec((None, None, N_STATE, ds), lambda b, s: (b, s, 0, 0))),
        [pltpu.VMEM((N_STATE, ds), F32)], [xbc, proj, dtb, alog, dsk])
    return hosted.results(pl.pallas_call(
        body, name=name, grid=(n_ex, n_step), out_shape=out_shape, in_specs=in_specs, out_specs=out_specs,
        scratch_shapes=scratch, compiler_params=_params(40 << 20, 2),
    )(*args))


def ssd_bwd(name, xbc, proj, dt_cb, hs, dy, dtb, alog, dsk, *, rev, n_ex, seq_len, ctx_len, ds, rider=None):
    n_head, half = ds // HEAD_DIM, ds // 2
    n_ctx, n_lat = ctx_len // CHUNK, seq_len // CHUNK
    n_step = n_ctx + n_lat
    n_tok = n_ex * (seq_len + ctx_len)
    blk0 = functools.partial(_ssd_blocks, rev=rev, n_ctx=n_ctx, n_lat=n_lat, lat_blocks=n_ex * n_lat)
    step = lambda sp: n_step - 1 - sp
    blk = lambda b, sp: blk0(b, step(sp))
    xw = xbc.shape[1]

    def dy_blk(b, sp):
        sl = jnp.maximum(step(sp), n_ctx) - n_ctx
        return b * n_lat + ((n_lat - 1 - sl) if rev else sl)

    hosted = _Hosted(rider, 7, 5, 1, (n_ex, n_step))

    def body(*refs):
        ((xbc_ref, dt_ref, hs_ref, dy_ref, dtb_ref, alog_ref, dsk_ref),
         (dxbc_ref, ddt_ref, dalog_ref, ddtb_ref, ddsk_ref), (dh_scr,)) = hosted.split(refs)
        b, sp = pl.program_id(0), pl.program_id(1)

        @pl.when(sp == 0)
        def _():
            dh_scr[...] = jnp.zeros_like(dh_scr)

        @pl.when((sp == 0) & (b == 0))
        def _():
            dalog_ref[...] = jnp.zeros_like(dalog_ref)
            ddtb_ref[...] = jnp.zeros_like(ddtb_ref)
            ddsk_ref[...] = jnp.zeros_like(ddsk_ref)

        q = _ssd_common(xbc_ref[...], dt_ref[...], dtb_ref[...], alog_ref[...], dsk_ref[...], rev=rev, ds=ds, n_head=n_head)
        h = hs_ref[...]
        d_y = jnp.where(step(sp) >= n_ctx, dy_ref[...], 0.0)
        dh_next = dh_scr[...]
        lane_row = lax.broadcasted_iota(jnp.int32, (1, LANES), 1)
        d_cs = jnp.zeros((CHUNK, LANES), F32)
        dxs_parts, de_parts, dwt_parts, ddec_parts = [], [], [], []
        for g in range(2):
            lo = g * half
            gs = slice(lo, lo + half)
            bg = q["bm"][:, g * N_STATE:(g + 1) * N_STATE].astype(BF16)
            cg = q["cm"][:, g * N_STATE:(g + 1) * N_STATE].astype(BF16)
            scores = lax.dot_general(cg, bg, _DIMS["nt"], preferred_element_type=F32)
            hg, dyg, dhn = h[:, gs], d_y[:, gs], dh_next[:, gs]
            off = jnp.dot(cg, hg.astype(BF16), preferred_element_type=F32)
            d_off = (q["e_w"][:, gs] * dyg).astype(BF16)
            de_parts.append(dyg * off)
            d_c = lax.dot_general(d_off, hg.astype(BF16), _DIMS["nt"], preferred_element_type=F32)
            dh_scr[:, gs] = (lax.dot_general(cg, d_off, _DIMS["tn"], preferred_element_type=F32)
                             + q["dec_w"][:, gs] * dhn)
            b_dh = jnp.dot(bg, dhn.astype(BF16), preferred_element_type=F32)
            v = q["wt_w"][:, gs] * q["xs_w"][:, gs]
            d_b = lax.dot_general(v.astype(BF16), dhn.astype(BF16), _DIMS["nt"], preferred_element_type=F32)
            dwt_parts.append(q["xs_w"][:, gs] * b_dh)
            ddec_parts.append(jnp.sum(hg * dhn, axis=0, keepdims=True))
            d_scores = jnp.zeros((CHUNK, CHUNK), F32)
            for j in range(half // LANES):
                c0 = (lo + j * LANES) // HEAD_DIM
                ln = slice(lo + j * LANES, lo + (j + 1) * LANES)
                l0, l1 = _decay(q, c0), _decay(q, c0 + 1)
                p0, p1 = scores * l0, scores * l1
                dy_st = _split_heads(d_y[:, ln]).astype(BF16)
                d_p = lax.dot_general(dy_st, q["xs_w"][:, ln].astype(BF16), _DIMS["nt"], preferred_element_type=F32)
                d_p0, d_p1 = d_p[:CHUNK], d_p[CHUNK:]
                d_scores = d_scores + d_p0 * l0 + d_p1 * l1
                for col, t in ((c0, d_p0 * p0), (c0 + 1, d_p1 * p1)):
                    d_cs = d_cs + jnp.sum(t - t.T, axis=1, keepdims=True) * (lane_row == col).astype(F32)
                p_st = jnp.concatenate([p0, p1], axis=0).astype(BF16)
                dxs_parts.append(lax.dot_general(p_st, dy_st, _DIMS["tn"], preferred_element_type=F32)
                                 + q["wt_w"][:, ln] * b_dh[:, j * LANES:(j + 1) * LANES])
            d_sc = d_scores.astype(BF16)
            d_c = d_c + jnp.dot(d_sc, bg, preferred_element_type=F32)
            d_b = d_b + lax.dot_general(d_sc, cg, _DIMS["tn"], preferred_element_type=F32)
            dxbc_ref[:, ds + g * N_STATE:ds + (g + 1) * N_STATE] = d_b
            dxbc_ref[:, ds + (2 + g) * N_STATE:ds + (3 + g) * N_STATE] = d_c
        d_xs = jnp.concatenate(dxs_parts, axis=1)
        narrow_m = (lax.broadcasted_iota(jnp.int32, (ds, LANES), 0) // HEAD_DIM
                    == lax.broadcasted_iota(jnp.int32, (ds, LANES), 1)).astype(BF16)
        rows8 = lambda v: jnp.broadcast_to(v, (8, ds))
        stacked = jnp.concatenate(
            [jnp.concatenate(dwt_parts, axis=1), jnp.concatenate(de_parts, axis=1), d_xs * q["xh"],
             rows8(jnp.concatenate(ddec_parts, axis=1)), rows8(jnp.sum(d_y * q["xh"], axis=0, keepdims=True))], axis=0)
        hi = stacked.astype(BF16)
        lo = (stacked - hi.astype(F32)).astype(BF16)
        sums = (jnp.dot(hi, narrow_m, preferred_element_type=F32) + jnp.dot(lo, narrow_m, preferred_element_type=F32))
        n_wt, n_e, n_xs = sums[:CHUNK], sums[CHUNK:2 * CHUNK], sums[2 * CHUNK:3 * CHUNK]
        n_dec, n_dsk = sums[3 * CHUNK:3 * CHUNK + 1], sums[3 * CHUNK + 8:3 * CHUNK + 9]
        e, wt, dec = jnp.exp(q["cs"]), jnp.exp(q["tot"] - q["cs"]), jnp.exp(q["tot"])
        d_wt = n_wt * wt
        d_cs = d_cs + n_e * e - d_wt
        d_tot = jnp.sum(d_wt, axis=0, keepdims=True) + n_dec * dec
        d_da = jnp.dot(q["tri_t"], d_cs, precision=HI, preferred_element_type=F32) + d_tot
        d_dt = d_da * q["a"] + n_xs
        dxbc_ref[:, :ds] = d_xs * q["dt_w"] + q["dsk_w"] * d_y
        dalog_ref[...] += jnp.sum(d_da * q["dt"], axis=0, keepdims=True) * q["a"]
        d_raw = d_dt * q["sig"]
        ddtb_ref[...] += jnp.sum(d_raw, axis=0, keepdims=True)
        ddsk_ref[...] += n_dsk
        ddt_ref[...] = pltpu.roll(d_raw, n_head, axis=1) if rev else d_raw
        hosted.finish()

    vec = pl.BlockSpec((1, LANES), lambda b, s: (0, 0))
    vec_shape = jax.ShapeDtypeStruct((1, LANES), F32)
    in_specs, out_shape, out_specs, scratch, args = hosted.call_args(
        [pl.BlockSpec((CHUNK, xw), lambda b, s: (blk(b, s), 0)),
         pl.BlockSpec((CHUNK, LANES), lambda b, s: (blk(b, s), dt_cb)),
         pl.BlockSpec((None, None, N_STATE, ds), lambda b, s: (b, step(s), 0, 0)),
         pl.BlockSpec((CHUNK, ds), lambda b, s: (dy_blk(b, s), 0)), vec, vec, vec],
        (jax.ShapeDtypeStruct((n_tok, xw), F32), jax.ShapeDtypeStruct((n_tok, LANES), F32), vec_shape, vec_shape, vec_shape),
        (pl.BlockSpec((CHUNK, xw), lambda b, s: (blk(b, s), 0)),
         pl.BlockSpec((CHUNK, LANES), lambda b, s: (blk(b, s), 0)), vec, vec, vec),
        [pltpu.VMEM((N_STATE, ds), F32)], [xbc, proj, hs, dy, dtb, alog, dsk])
    return hosted.results(pl.pallas_call(
        body, name=name, grid=(n_ex, n_step), out_shape=out_shape, in_specs=in_specs, out_specs=out_specs,
        scratch_shapes=scratch, compiler_params=_params(48 << 20, 2),
    )(*args))


def final_loss(x3, target, w, *, tm):
    n, d = x3.shape

    def body(x_ref, t_ref, w_ref, dx_ref, dw_ref, loss_ref):
        i = pl.program_id(0)
        t = t_ref[...]

        def per_feature(xv, wv):
            err = _rms(xv, wv) - t
            return 0.5 * jnp.sum(err * err, axis=0, keepdims=True) / d

        lv, vjp = jax.vjp(per_feature, x_ref[...], w_ref[...])
        dx, dw = vjp(jnp.ones_like(lv))
        dx_ref[...] = dx

        @pl.when(i == 0)
        def _():
            dw_ref[...] = dw
            loss_ref[...] = lv

        @pl.when(i > 0)
        def _():
            dw_ref[...] += dw
            loss_ref[...] += lv

    tile = pl.BlockSpec((tm, d), lambda i: (i, 0))
    vec = pl.BlockSpec((1, d), lambda i: (0, 0))
    return pl.pallas_call(
        body, name="final_loss", grid=(n // tm,), in_specs=[tile, tile, vec],
        out_shape=(jax.ShapeDtypeStruct((n, d), F32), jax.ShapeDtypeStruct((1, d), F32), jax.ShapeDtypeStruct((1, d), F32)),
        out_specs=(tile, vec, vec), compiler_params=_params(tm * d * 4 * 16 + (8 << 20)),
    )(x3, target, w)


def sum_slots(name, arr):
    n_slot, n_row, width = arr.shape
    tm = _row_tile(n_row, width * n_slot, mult=16)

    def body(a_ref, o_ref):
        acc = a_ref[0].astype(F32)
        for j in range(1, n_slot):
            acc = acc + a_ref[j].astype(F32)
        o_ref[...] = acc

    return pl.pallas_call(
        body, name=name, grid=(n_row // tm,), out_shape=jax.ShapeDtypeStruct((n_row, width), F32),
        in_specs=[pl.BlockSpec((n_slot, tm, width), lambda i: (0, i, 0))],
        out_specs=pl.BlockSpec((tm, width), lambda i: (i, 0)), compiler_params=_params(),
    )(arr)


def adamw(name, w, g_slots, m, v):
    n_slot, n_row, width = g_slots.shape
    tm = _row_tile(n_row, width * 2)

    def body(w_ref, g_ref, m_ref, v_ref, go_ref, d_ref, mo_ref, vo_ref):
        g = g_ref[0]
        for j in range(1, n_slot):
            g = g + g_ref[j]
        m2 = ADAM_B1 * m_ref[...] + (1.0 - ADAM_B1) * g
        v2 = ADAM_B2 * v_ref[...] + (1.0 - ADAM_B2) * jnp.square(g)
        m_hat = m2 / (1.0 - ADAM_B1 ** ADAM_STEP)
        v_hat = v2 / (1.0 - ADAM_B2 ** ADAM_STEP)
        go_ref[...] = g
        d_ref[...] = -ADAM_LR * (m_hat / (jnp.sqrt(v_hat) + ADAM_EPS) + ADAM_WD * w_ref[...])
        mo_ref[...] = m2
        vo_ref[...] = v2

    tile = pl.BlockSpec((tm, width), lambda i: (i, 0))
    shape = jax.ShapeDtypeStruct((n_row, width), F32)
    return pl.pallas_call(
        body, name=name, grid=(n_row // tm,), out_shape=(shape,) * 4,
        in_specs=[tile, pl.BlockSpec((n_slot, tm, width), lambda i: (0, i, 0)), tile, tile],
        out_specs=(tile,) * 4, compiler_params=_params(),
    )(w, g_slots, m, v)


def cctx_grad(q_all, c_ctx_row):
    d = c_ctx_row.shape[1]

    def body(q_ref, c_ref, o_ref):
        acc = q_ref[0, 0:1, :]
        for j in (2, 4, 6):
            acc = acc + q_ref[j, 0:1, :]
        _, vjp = jax.vjp(_silu, c_ref[...])
        o_ref[...] = vjp(acc)[0]

    return pl.pallas_call(
        body, name="cctx_grad", out_shape=jax.ShapeDtypeStruct((1, d), F32),
    )(q_all, c_ctx_row)


def loss_total(pack_sum, d):
    def body(p_ref, o_ref):
        o_ref[...] = jnp.sum(p_ref[:, 0:d], axis=1, keepdims=True)

    return pl.pallas_call(
        body, name="loss_total", out_shape=jax.ShapeDtypeStruct((1, 1), F32),
    )(pack_sum)


class _Plan:
    def __init__(self):
        self.builders, self.got = {}, {}

    def on(self, host, key, builder):
        self.builders.setdefault(host, []).append((key, builder))

    def run(self, host, fn, *args, **kw):
        if host not in self.builders:
            return fn(host, *args, **kw)
        keys, riders = zip(*[(key, builder(self)) for key, builder in self.builders[host]])
        res, landed = fn(host, *args, rider=Riders(riders), **kw)
        for key, r in zip(keys, riders):
            self.got[key], landed = landed[:r.n], landed[r.n:]
        return res


def _val(w):
    return w() if callable(w) else w


def _matmul_tile(n_rows, tm):
    return 2 * tm if n_rows % (2 * tm) == 0 else tm


def _ffn_fwd(plan, tag, xin, n_rows, tm, seg_fn, shift, scale, gate, norm_w, wg, wu, wd):
    d = xin.shape[1]
    n_tiles = n_rows // tm
    (h,) = plan.run(f"{tag}_norm", rowwise, fn_norm_mod, [row(xin)], [shift, scale], [norm_w], [(n_rows, d, BF16)],
                    tm=tm, n_tiles=n_tiles, seg_fn=seg_fn)
    tmm = _matmul_tile(n_rows, tm)
    g = plan.run(f"{tag}_gate", matmul, [(h, _val(wg))], "nn", out_dtype=BF16, b_ch=True, out_ch=True, tm=tmm)
    u, act = plan.run(f"{tag}_up", matmul, [(h, _val(wu))], "nn", b_ch=True, out_ch=True, tm=tmm,
                      post=([g], lambda acc, gv: (acc, fn_act(gv, acc)[0]), [BF16, BF16]))
    f = plan.run(f"{tag}_down", matmul, [(act, _val(wd))], "nn", a_ch=True, b_ch=True, tm=tmm, fold=True)
    (xo,) = plan.run(f"{tag}_resid", rowwise, make_fn_resid(0.5), [row(xin), row(f)], [gate], [], [(n_rows, d, F32)],
                     tm=tm, n_tiles=n_tiles, seg_fn=seg_fn)
    return xo, (h, g, u, act, f)


def _ffn_bwd(plan, tag, d_xo, saved, xin, n_rows, tm, seg_fn, first_fn, shift, scale, gate, norm_w, wg, wu, wd, dx_rows, dx_limit):
    h, g, u, act, f = saved
    d = xin.shape[1]
    n_tiles = n_rows // tm
    n_ch, _, n_hid = g.shape
    d_f, d_gate = plan.run(f"{tag}_resid_bwd", rowwise_bwd, make_fn_resid(0.5), [row(xin), row(f)], [gate], [], [[row(d_xo)]],
                           [None, (n_rows, BF16, None)], tm=tm, n_tiles=n_tiles, seg_fn=seg_fn, first_fn=first_fn)
    tmm = _matmul_tile(n_rows, tm)
    act_vjp = lambda acc, gv, uv: jax.vjp(lambda a, b: fn_act(a, b)[0], gv, uv)[1](acc)
    d_g, d_u = plan.run(f"{tag}_down_dx", matmul, [(d_f, wd)], "nt", b_ch=True, out_ch=True, tm=tmm,
                        post=([g, u], act_vjp, [BF16, BF16]))
    plan.got[f"{tag}_d_wd"] = plan.run(f"{tag}_down_dw", matmul, [(act, d_f)], "tn", out_dtype=BF16, a_ch=True, out_ch=True, tm=tmm)
    d_h = plan.run(f"{tag}_up_dx", matmul, [(d_g, wg), (d_u, wu)], "nt", a_ch=True, b_ch=True, tm=tmm)
    plan.got[f"{tag}_d_wg"] = plan.run(f"{tag}_gate_dw", matmul, [(d_g, h)], "tn", out_dtype=BF16, a_ch=True, out_ch=True, tm=tmm)
    plan.got[f"{tag}_d_wu"] = plan.run(f"{tag}_up_dw", matmul, [(d_u, h)], "tn", out_dtype=BF16, a_ch=True, out_ch=True, tm=tmm)
    d_x, d_shift, d_scale, d_nw = plan.run(
        f"{tag}_norm_bwd", rowwise_bwd, fn_norm_mod, [row(xin)], [shift, scale], [norm_w], [[row(d_h)]], [(dx_rows, F32, dx_limit)],
        tm=tm, n_tiles=n_tiles, seg_fn=seg_fn, first_fn=first_fn, adds={0: (row(d_xo), None)})
    return d_x, (d_shift, d_scale, d_gate), d_nw


def kernel(x, c, ctx, c_ctx, w_mod, b_mod, norm_ffn1, ffn1_gate, ffn1_up, ffn1_down, norm_mix, w_in, ssm_conv_w, ssm_conv_b, dt_bias_fwd, dt_bias_bwd, a_log_fwd, a_log_bwd, ssm_d, ssm_norm_w, cconv_w, cconv_b, cconv_ln_w, cconv_ln_b, w_out, norm_ffn2, ffn2_gate, ffn2_up, ffn2_down, final_norm, loss_target, m_c_ctx, m_w_mod, m_b_mod, m_norm_ffn1, m_ffn1_gate, m_ffn1_up, m_ffn1_down, m_norm_mix, m_w_in, m_ssm_conv_w, m_ssm_conv_b, m_dt_bias_fwd, m_dt_bias_bwd, m_a_log_fwd, m_a_log_bwd, m_ssm_d, m_ssm_norm_w, m_cconv_w, m_cconv_b, m_cconv_ln_w, m_cconv_ln_b, m_w_out, m_norm_ffn2, m_ffn2_gate, m_ffn2_up, m_ffn2_down, m_final_norm, v_c_ctx, v_w_mod, v_b_mod, v_norm_ffn1, v_ffn1_gate, v_ffn1_up, v_ffn1_down, v_norm_mix, v_w_in, v_ssm_conv_w, v_ssm_conv_b, v_dt_bias_fwd, v_dt_bias_bwd, v_a_log_fwd, v_a_log_bwd, v_ssm_d, v_ssm_norm_w, v_cconv_w, v_cconv_b, v_cconv_ln_w, v_cconv_ln_b, v_w_out, v_norm_ffn2, v_ffn2_gate, v_ffn2_up, v_ffn2_down, v_final_norm):
    weights = dict(c_ctx=c_ctx, w_mod=w_mod, b_mod=b_mod, norm_ffn1=norm_ffn1, ffn1_gate=ffn1_gate, ffn1_up=ffn1_up, ffn1_down=ffn1_down, norm_mix=norm_mix, w_in=w_in, ssm_conv_w=ssm_conv_w, ssm_conv_b=ssm_conv_b, dt_bias_fwd=dt_bias_fwd, dt_bias_bwd=dt_bias_bwd, a_log_fwd=a_log_fwd, a_log_bwd=a_log_bwd, ssm_d=ssm_d, ssm_norm_w=ssm_norm_w, cconv_w=cconv_w, cconv_b=cconv_b, cconv_ln_w=cconv_ln_w, cconv_ln_b=cconv_ln_b, w_out=w_out, norm_ffn2=norm_ffn2, ffn2_gate=ffn2_gate, ffn2_up=ffn2_up, ffn2_down=ffn2_down, final_norm=final_norm)
    mom1 = dict(c_ctx=m_c_ctx, w_mod=m_w_mod, b_mod=m_b_mod, norm_ffn1=m_norm_ffn1, ffn1_gate=m_ffn1_gate, ffn1_up=m_ffn1_up, ffn1_down=m_ffn1_down, norm_mix=m_norm_mix, w_in=m_w_in, ssm_conv_w=m_ssm_conv_w, ssm_conv_b=m_ssm_conv_b, dt_bias_fwd=m_dt_bias_fwd, dt_bias_bwd=m_dt_bias_bwd, a_log_fwd=m_a_log_fwd, a_log_bwd=m_a_log_bwd, ssm_d=m_ssm_d, ssm_norm_w=m_ssm_norm_w, cconv_w=m_cconv_w, cconv_b=m_cconv_b, cconv_ln_w=m_cconv_ln_w, cconv_ln_b=m_cconv_ln_b, w_out=m_w_out, norm_ffn2=m_norm_ffn2, ffn2_gate=m_ffn2_gate, ffn2_up=m_ffn2_up, ffn2_down=m_ffn2_down, final_norm=m_final_norm)
    mom2 = dict(c_ctx=v_c_ctx, w_mod=v_w_mod, b_mod=v_b_mod, norm_ffn1=v_norm_ffn1, ffn1_gate=v_ffn1_gate, ffn1_up=v_ffn1_up, ffn1_down=v_ffn1_down, norm_mix=v_norm_mix, w_in=v_w_in, ssm_conv_w=v_ssm_conv_w, ssm_conv_b=v_ssm_conv_b, dt_bias_fwd=v_dt_bias_fwd, dt_bias_bwd=v_dt_bias_bwd, a_log_fwd=v_a_log_fwd, a_log_bwd=v_a_log_bwd, ssm_d=v_ssm_d, ssm_norm_w=v_ssm_norm_w, cconv_w=v_cconv_w, cconv_b=v_cconv_b, cconv_ln_w=v_cconv_ln_w, cconv_ln_b=v_cconv_ln_b, w_out=v_w_out, norm_ffn2=v_norm_ffn2, ffn2_gate=v_ffn2_gate, ffn2_up=v_ffn2_up, ffn2_down=v_ffn2_down, final_norm=v_final_norm)
    order = list(weights)

    n_ex, seq_len, d = x.shape
    ctx_len = ctx.shape[1]
    ds = d
    n_head = ds // HEAD_DIM
    xw = ds + 4 * N_STATE
    n_lat, n_ctx_rows = n_ex * seq_len, n_ex * ctx_len
    n_tok = n_lat + n_ctx_rows
    tm = math.gcd(math.gcd(512, seq_len), n_ctx_rows)
    seg_all, first_all = _segmenter(tm, seq_len, n_lat)
    lat_tiles = n_lat // tm

    xi, yi, ci = lax.axis_index("x"), lax.axis_index("y"), lax.axis_index("c")
    me, chip = 4 * xi + 2 * yi + ci, 2 * xi + yi

    (c_all,) = exchange("gather_c", [c], "all8")
    n_all = 8 * n_ex
    n_cond = -(-(n_all + 1) // 8) * 8
    cond = jnp.concatenate([c_all.reshape(n_all, d), c_ctx[None, :], jnp.zeros((n_cond - n_all - 1, d), F32)])
    mod_w = w_mod.shape[2]
    b_shard = lax.dynamic_slice(b_mod, (0, chip * mod_w), (1, mod_w))
    (mod_g,) = exchange("gather_mod", [mod_fwd(cond, w_mod[0], b_shard)], "chips")
    mod_full = mod_g.transpose(1, 0, 2).reshape(n_cond, N_CHIPS * mod_w)
    mod_mine = lax.dynamic_slice(mod_full, (me * n_ex, 0), (n_ex, 9 * d)).reshape(n_ex, 9, d)
    mod_ctx = mod_full[n_all].reshape(9, d)
    tabs = [jnp.concatenate([mod_mine[:, j], mod_ctx[j][None]])[:, None, :] for j in range(9)]
    lat = lambda t: t[:n_ex]

    bf = lambda w: w[0].astype(BF16)
    plan = _Plan()
    gather = lambda *ws: (lambda p: Rider(list(ws), "chips"))
    wg1, w5_g, w31_g = exchange("gather_first", [bf(ffn1_gate), ssm_conv_w[0], cconv_w[0]], "chips")
    plan.on("ffn1_gate", "wu1", gather(bf(ffn1_up)))
    plan.on("ffn1_up", "wd1_wout", gather(bf(ffn1_down), bf(w_out)))
    plan.on("ffn1_down", "win", gather(bf(w_in)))
    xt = jnp.concatenate([x.reshape(n_lat, d), ctx.reshape(n_ctx_rows, d)])
    x1, saved1 = _ffn_fwd(plan, "ffn1", xt, n_tok, tm, seg_all, tabs[0], tabs[1], tabs[2], norm_ffn1, wg1,
                          lambda: plan.got["wu1"][0], lambda: plan.got["wd1_wout"][0])
    (wu1,), (wd1, wout_g), (win_g,) = (plan.got[k] for k in ("wu1", "wd1_wout", "win"))
    unshard_cols = lambda t: t.transpose(1, 0, 2).reshape(t.shape[1], N_CHIPS * t.shape[2])
    win = unshard_cols(win_g)
    o_x, o_dt, o_glu = ds, ds + xw, ds + xw + 2 * n_head
    w_z, w_xbc, w_dt = win[:, :ds], win[:, o_x:o_dt], win[:, o_dt:o_glu]
    w_ga, w_gb = win[:, o_glu:o_glu + d], win[:, o_glu + d:]
    w_dtp = jnp.concatenate([w_dt, jnp.zeros((d, LANES - 2 * n_head), BF16)], axis=1)
    w_cat = jnp.concatenate([w_z, w_ga, w_gb, w_xbc, w_dtp], axis=1)
    cbw = d // 2
    xbc_cb, dt_cb = 3 * d // cbw, (3 * d + xw) // LANES
    wout = wout_g.reshape(2 * d, d)
    wo_y, wo_u = wout[:ds], wout[ds:]
    w5, w31 = unshard_cols(w5_g), unshard_cols(w31_g)
    pad_vec = lambda v: jnp.concatenate([v.reshape(1, -1), jnp.zeros((1, LANES - v.size), F32)], axis=1)
    dtb_f, dtb_b, alog_f, alog_b = map(pad_vec, (dt_bias_fwd, dt_bias_bwd, a_log_fwd, a_log_bwd))
    dsk_f, dsk_b = pad_vec(ssm_d), jnp.zeros((1, LANES), F32)

    (h2,) = rowwise("mix_norm", fn_norm_mod, [row(x1)], [tabs[3], tabs[4]], [norm_mix], [(n_tok, d, BF16)],
                    tm=tm, n_tiles=n_tok // tm, seg_fn=seg_all)
    proj, (wg2,) = matmul("mix_proj", [(h2, w_cat)], "nn", tm=min(tm, 256), rider=Rider([bf(ffn2_gate)], "chips"))
    def conv5(name, src, cb0, flip):
        out = None
        for part, seq, off in (("lat", seq_len, 0), ("ctx", ctx_len, n_lat // ctx_len)):
            out = tapsum_roll(f"{name}_{part}", src, cb0, w5, 0, seq_len=seq, n_seq=n_ex, row_blk_off=off, width=seq,
                              piece=seq, cb=cbw, ncb=xw // cbw, pad=w5.shape[0] // 2, flip=flip,
                              place=((n_tok, xw), off, 0, out))
        return out

    craw = conv5("xbc_conv", proj, xbc_cb, False)
    (xbc,) = rowwise("xbc_silu", fn_silu_bias, [row(craw)], [], [ssm_conv_b], [(n_tok, xw, F32)], tm=tm, n_tiles=n_tok // tm)
    ssd = dict(n_ex=n_ex, seq_len=seq_len, ctx_len=ctx_len, ds=ds)
    (y_f, hs_f), (wu2, wd2) = ssd_fwd("ssd_fwd_f", xbc, proj, dt_cb, dtb_f, alog_f, dsk_f, rev=False,
                                      rider=Rider([bf(ffn2_up), bf(ffn2_down)], "chips"), **ssd)
    y_b, hs_b = ssd_fwd("ssd_fwd_b", xbc, proj, dt_cb, dtb_b, alog_b, dsk_b, rev=True, **ssd)
    fn_gate = make_fn_gate_groupnorm(ds)
    (yn,) = rowwise("ssd_gate", fn_gate, [row(y_f), row(y_b), row(proj, d, 0)], [], [ssm_norm_w], [(n_lat, ds, BF16)],
                    tm=tm, n_tiles=lat_tiles)
    (u0,) = rowwise("glu", fn_glu, [row(proj, d, 1), row(proj, d, 2)], [], [], [(n_lat, d, F32)], tm=tm, n_tiles=lat_tiles)
    cb31 = max(LANES, d // 4)
    ncb31 = (d // 2) // cb31
    pad31 = w31.shape[0] // 2
    piece31 = min(seq_len, 4 * GRID_W)
    v_w = tapsum_roll("cconv_cols", u0, 0, w31, 0, seq_len=seq_len, n_seq=n_ex, row_blk_off=0, width=GRID_W,
                      piece=piece31, cb=cb31, ncb=ncb31, pad=pad31, flip=False)
    v_h = tapsum_rows("cconv_rows", u0, ncb31, w31, ncb31, seq_len=seq_len, n_seq=n_ex, cb=cb31, ncb=ncb31, pad=pad31, flip=False)
    (un,) = rowwise("cconv_ln", fn_ln_silu, [row(v_w), row(v_h)], [], [cconv_b, cconv_ln_w, cconv_ln_b], [(n_lat, d, BF16)],
                    tm=tm, n_tiles=lat_tiles)
    mix = matmul("mix_out", [(yn, wo_y), (un, wo_u)], "nn", tm=tm)
    seg_lat, first_lat = _segmenter(tm, seq_len, n_lat)
    (x2,) = rowwise("mix_resid", make_fn_resid(1.0), [row(x1), row(mix)], [lat(tabs[5])], [], [(n_lat, d, F32)],
                    tm=tm, n_tiles=lat_tiles, seg_fn=seg_lat)
    x3, saved2 = _ffn_fwd(plan, "ffn2", x2, n_lat, tm, seg_lat, lat(tabs[6]), lat(tabs[7]), lat(tabs[8]), norm_ffn2, wg2, wu2, wd2)
    d_x3, d_final, loss_vec = final_loss(x3, loss_target.reshape(n_lat, d), final_norm.reshape(1, d), tm=tm)

    shard_cols = lambda t: t.reshape(t.shape[0], N_CHIPS, -1).transpose(1, 0, 2)

    def pieces(t):
        t = jnp.pad(t, ((0, 0), (0, -t.shape[1] % 32), (0, 0)))
        return t.reshape(2 * N_CHIPS, t.shape[1] // 2, t.shape[2]).astype(BF16)

    scatter = lambda *ts: Rider([pieces(t) for t in ts], "all8", scatter=True)
    halves = lambda names, landed: Rider([sum_slots(f"sum_{nm}", r) for nm, r in zip(names, landed)], "sibling")
    swapped = {}
    plan.on("ffn2_up_dx", "sc_ffn2_down", lambda p: scatter(p.got["ffn2_d_wd"]))
    plan.on("ffn2_up_dw", "sc_ffn2_gate", lambda p: scatter(p.got["ffn2_d_wg"]))
    d_x2, (d_s6, d_s7, d_g8), d_nffn2 = _ffn_bwd(
        plan, "ffn2", d_x3, saved2, x2, n_lat, tm, seg_lat, first_lat, lat(tabs[6]), lat(tabs[7]), lat(tabs[8]), norm_ffn2,
        wg2, wu2, wd2, n_lat, None)
    d_mix, d_g5 = rowwise_bwd("mix_resid_bwd", make_fn_resid(1.0), [row(x1), row(mix)], [lat(tabs[5])], [], [[row(d_x2)]],
                              [None, (n_lat, BF16, None)], tm=tm, n_tiles=lat_tiles, seg_fn=seg_lat, first_fn=first_lat)
    d_yn = matmul("mix_out_dy", [(d_mix, wo_y)], "nt", tm=tm)
    d_un = matmul("mix_out_du", [(d_mix, wo_u)], "nt", tm=tm)
    d_wout = jnp.concatenate([matmul("mix_out_dwy", [(yn, d_mix)], "tn", out_dtype=BF16, tm=tm), matmul("mix_out_dwu", [(un, d_mix)], "tn", out_dtype=BF16, tm=tm)])
    d_vw, d_vh, d_cb, d_lnw, d_lnb = rowwise_bwd(
        "cconv_ln_bwd", fn_ln_silu, [row(v_w), row(v_h)], [], [cconv_b, cconv_ln_w, cconv_ln_b], [[row(d_un)]],
        [(n_lat, F32, None)] * 2, tm=tm, n_tiles=lat_tiles)
    d_u0 = tapsum_roll("cconv_cols_dx", d_vw, 0, w31, 0, seq_len=seq_len, n_seq=n_ex, row_blk_off=0, width=GRID_W,
                       piece=piece31, cb=cb31, ncb=ncb31, pad=pad31, flip=True, place=((n_lat, d), 0, 0, None))
    d_u0 = tapsum_rows("cconv_rows_dx", d_vh, 0, w31, ncb31, seq_len=seq_len, n_seq=n_ex, cb=cb31, ncb=ncb31, pad=pad31,
                       flip=True, place=((n_lat, d), 0, ncb31, d_u0))
    d_w31 = jnp.concatenate([
        tapgrad_roll("cconv_cols_dw", d_vw, 0, 0, u0, 0, 0, n_tap=w31.shape[0], seq_len=seq_len, n_seq=n_ex, width=GRID_W,
                     piece=piece31, cb=cb31, ncb=ncb31, pad=pad31),
        tapgrad_rows("cconv_rows_dw", d_vh, 0, u0, ncb31, n_tap=w31.shape[0], seq_len=seq_len, n_seq=n_ex, cb=cb31,
                     ncb=ncb31, pad=pad31)], axis=1)
    d_ga, d_gb = rowwise_bwd("glu_bwd", fn_glu, [row(proj, d, 1), row(proj, d, 2)], [], [], [[row(d_u0)]],
                             [(n_lat, BF16, None)] * 2, tm=tm, n_tiles=lat_tiles)
    d_ysum, d_z, d_ssmnw = rowwise_bwd(
        "ssd_gate_bwd", fn_gate, [row(y_f), row(y_b), row(proj, d, 0)], [], [ssm_norm_w], [[row(d_yn)]],
        [(n_lat, F32, None), None, (n_lat, BF16, None)], tm=tm, n_tiles=lat_tiles)
    (dxbc_f, ddt_f, dalog_f, ddtb_f, ddsk), landed = ssd_bwd(
        "ssd_bwd_f", xbc, proj, dt_cb, hs_f, d_ysum, dtb_f, alog_f, dsk_f, rev=False,
        rider=scatter(plan.got["ffn2_d_wu"], d_wout.reshape(N_CHIPS, -1, d)), **ssd)
    (dxbc_b, ddt_b, dalog_b, ddtb_b, _), both = ssd_bwd(
        "ssd_bwd_b", xbc, proj, dt_cb, hs_b, d_ysum, dtb_b, alog_b, dsk_b, rev=True,
        rider=halves(["ffn2_down", "ffn2_gate"], plan.got["sc_ffn2_down"] + plan.got["sc_ffn2_gate"]), **ssd)
    swapped.update(zip(["ffn2_down", "ffn2_gate"], both))
    (d_craw, d_conv_b), both = rowwise_bwd(
        "xbc_silu_bwd", fn_silu_bias, [row(craw)], [], [ssm_conv_b], [[row(dxbc_f), row(dxbc_b)]],
        [(n_tok, F32, None)], tm=tm, n_tiles=n_tok // tm, rider=halves(["ffn2_up", "w_out"], landed))
    swapped.update(zip(["ffn2_up", "w_out"], both))
    d_pxbc = conv5("xbc_conv_dx", d_craw, 0, True)
    g5 = lambda name, seq, off: tapgrad_roll(name, d_craw, 0, off, proj, xbc_cb, off, n_tap=w5.shape[0], seq_len=seq,
                                             n_seq=n_ex, width=seq, piece=seq, cb=cbw, ncb=xw // cbw, pad=w5.shape[0] // 2)
    d_w5 = g5("xbc_conv_lat_dw", seq_len, 0) + g5("xbc_conv_ctx_dw", ctx_len, n_lat // ctx_len)
    lat_pairs = [(d_z, w_z), (d_ga, w_ga), (d_gb, w_gb), (d_pxbc, w_xbc), (ddt_f, w_dtp), (ddt_b, w_dtp)]
    d_h2 = jnp.concatenate([matmul("mix_proj_dx_lat", lat_pairs, "nt", rows=n_lat, tm=min(tm, 256)),
                            matmul("mix_proj_dx_ctx", lat_pairs[3:], "nt", rows=n_ctx_rows, row_off=n_lat, tm=min(tm, 256))])
    d_wz = matmul("mix_proj_dwz", [(d_z, h2)], "tn", out_dtype=BF16, rows=n_lat, tm=tm)
    d_wga = matmul("mix_proj_dwa", [(d_ga, h2)], "tn", out_dtype=BF16, rows=n_lat, tm=tm)
    d_wgb = matmul("mix_proj_dwb", [(d_gb, h2)], "tn", out_dtype=BF16, rows=n_lat, tm=tm)
    d_wxbc = matmul("mix_proj_dwx", [(d_pxbc, h2)], "tn", out_dtype=BF16, tm=tm)
    d_wdt = matmul("mix_proj_dwt", [(ddt_f, h2), (ddt_b, h2)], "tn", out_dtype=BF16, tm=tm)
    d_win_t = jnp.concatenate([d_wz, d_wxbc, d_wdt[:2 * n_head], d_wga, d_wgb]).reshape(N_CHIPS, -1, d)
    d_x1, d_s3, d_s4, d_nmix = rowwise_bwd(
        "mix_norm_bwd", fn_norm_mod, [row(x1)], [tabs[3], tabs[4]], [norm_mix], [[row(d_h2)]], [(n_tok, F32, None)],
        tm=tm, n_tiles=n_tok // tm, seg_fn=seg_all, first_fn=first_all, adds={0: (row(d_x2), lat_tiles)})
    mix_names = ["w_in", "ssm_conv_w", "cconv_w"]
    plan.on("ffn1_down_dx", "sc_mix", lambda p: scatter(d_win_t, shard_cols(d_w5), shard_cols(d_w31)))
    plan.on("ffn1_up_dx", "sc_ffn1_down", lambda p: scatter(p.got["ffn1_d_wd"]))
    plan.on("ffn1_gate_dw", "sw_mix", lambda p: halves(mix_names, p.got["sc_mix"]))
    plan.on("ffn1_up_dw", "sc_ffn1_gate", lambda p: scatter(p.got["ffn1_d_wg"]))
    plan.on("ffn1_up_dw", "sw_ffn1_down", lambda p: halves(["ffn1_down"], p.got["sc_ffn1_down"]))
    plan.on("ffn1_norm_bwd", "sc_ffn1_up", lambda p: scatter(p.got["ffn1_d_wu"]))
    d_xt, (d_s0, d_s1, d_g2), d_nffn1 = _ffn_bwd(
        plan, "ffn1", d_x1, saved1, xt, n_tok, tm, seg_all, first_all, tabs[0], tabs[1], tabs[2], norm_ffn1, wg1, wu1, wd1,
        n_lat, lat_tiles)
    swapped.update(zip(mix_names + ["ffn1_down"], plan.got["sw_mix"] + plan.got["sw_ffn1_down"]))
    last_names = ["ffn1_gate", "ffn1_up"]
    last = halves(last_names, plan.got["sc_ffn1_gate"] + plan.got["sc_ffn1_up"])
    swapped.update(zip(last_names, exchange("swap_sibling", last.arrs, "sibling")))
    grad_x = d_xt.reshape(n_ex, seq_len, d)

    with_ctx0 = lambda t: jnp.concatenate([t, jnp.zeros((1, 1, d), F32)])
    d_tabs = [d_s0, d_s1, d_g2, d_s3, d_s4, with_ctx0(d_g5), with_ctx0(d_s6), with_ctx0(d_s7), with_ctx0(d_g8)]
    d_mod_rows = jnp.concatenate([t[:, 0, :] for t in d_tabs], axis=1)
    n_pad_rows = -(-(n_ex + 1) // 8) * 8
    d_mod_rows = jnp.concatenate([d_mod_rows, jnp.zeros((n_pad_rows - n_ex - 1, 9 * d), F32)])
    small = [("loss", loss_vec), ("norm_ffn1", d_nffn1), ("norm_mix", d_nmix), ("ssm_conv_b", d_conv_b),
             ("dt_bias_fwd", ddtb_f[:, :n_head]), ("dt_bias_bwd", ddtb_b[:, :n_head]), ("a_log_fwd", dalog_f[:, :n_head]),
             ("a_log_bwd", dalog_b[:, :n_head]), ("ssm_d", ddsk[:, :n_head]), ("ssm_norm_w", d_ssmnw), ("cconv_b", d_cb),
             ("cconv_ln_w", d_lnw), ("cconv_ln_b", d_lnb), ("norm_ffn2", d_nffn2), ("final_norm", d_final)]
    n_small = sum(v.size for _, v in small)
    n_pack = -(-n_small // (8 * LANES)) * (8 * LANES)
    pack = jnp.concatenate([v.reshape(-1) for _, v in small] + [jnp.zeros((n_pack - n_small,), F32)]).reshape(-1, LANES)
    pack_all, d_mod_all = exchange("gather_small", [pack, d_mod_rows], "all8")
    pack_sum = sum_slots("small_sum", pack_all)
    loss = loss_total(pack_sum.reshape(1, n_pack), d).reshape(())
    flat_sum = pack_sum.reshape(-1)
    small_grads, pos = {}, 0
    for nm, v in small:
        small_grads[nm] = flat_sum[pos:pos + v.size]
        pos += v.size
    d_mod_all = d_mod_all.reshape(8 * n_pad_rows, 9 * d)
    cond_rows = [jnp.concatenate([cond[j * n_ex:(j + 1) * n_ex], c_ctx[None, :],
                                  jnp.zeros((n_pad_rows - n_ex - 1, d), F32)]) for j in range(8)]
    cond_bwd = jnp.concatenate(cond_rows)
    d_mod_shard = lax.dynamic_slice(d_mod_all, (0, chip * mod_w), (8 * n_pad_rows, mod_w))
    g_wmod, g_bmod, q_part = mod_bwd(cond_bwd, d_mod_shard, d_mod_all, w_mod[0],
                                     tuple(j * n_pad_rows + n_ex for j in range(8)))
    (q_all,) = exchange("gather_cctx", [q_part], "all8")
    g_cctx = cctx_grad(q_all, c_ctx.reshape(1, d))
    small_grads["c_ctx"], small_grads["b_mod"] = g_cctx.reshape(-1), g_bmod.reshape(-1)

    transposed = {"ffn1_gate", "ffn1_up", "ffn2_gate", "ffn2_up", "w_in"}
    results = {}
    for nm, both in swapped.items():
        flip = (lambda t: jnp.swapaxes(t, 1, 2)) if nm in transposed else (lambda t: t)
        shape = flip(weights[nm]).shape
        two_d = lambda t: flip(t).reshape(shape[-2], shape[-1])
        g_full = both.reshape(1, -1, shape[-1])[:, :shape[-2]]
        results[nm] = [flip(r.reshape(shape)) for r in
                       adamw(f"adamw_{nm}", two_d(weights[nm]), g_full, two_d(mom1[nm]), two_d(mom2[nm]))]
    results["w_mod"] = [r.reshape(w_mod.shape) for r in adamw("adamw_w_mod", w_mod[0], g_wmod[None], m_w_mod[0], v_w_mod[0])]
    small_names = [nm for nm in order if nm not in results]
    n_sm = sum(weights[nm].size for nm in small_names)
    n_smp = -(-n_sm // (8 * LANES)) * (8 * LANES)
    packed = lambda src: jnp.concatenate([src[nm].reshape(-1) for nm in small_names] + [jnp.zeros((n_smp - n_sm,), F32)]).reshape(-1, LANES)
    sm_out = adamw("adamw_small", packed(weights), packed(small_grads)[None], packed(mom1), packed(mom2))
    pos = 0
    for nm in small_names:
        size = weights[nm].size
        results[nm] = [r.reshape(-1)[pos:pos + size].reshape(weights[nm].shape) for r in sm_out]
        pos += size
    return (loss, grad_x, *[results[nm][0] for nm in order], *[results[nm][1] for nm in order],
            *[results[nm][2] for nm in order], *[results[nm][3] for nm in order])
```

```python
import functools
import math

import jax
import jax.numpy as jnp
from jax import lax
from jax.experimental import pallas as pl
from jax.experimental.pallas import tpu as pltpu

F32 = jnp.float32
BF16 = jnp.bfloat16
HI = lax.Precision.HIGHEST
MESH = pl.DeviceIdType.MESH

EPS = 1e-6
GRID_W = 64
HEAD_DIM = 64
N_STATE = 128
CHUNK = 128
LANES = 128
N_CHIPS = 4
ADAM_LR, ADAM_B1, ADAM_B2, ADAM_EPS, ADAM_WD, ADAM_STEP = 0.001, 0.9, 0.999, 1e-08, 0.01, 10
VMEM_CAP = 56 * 1024 * 1024


def _params(vmem_bytes=None, n_axes=1):
    kw = dict(dimension_semantics=("arbitrary",) * n_axes)
    if vmem_bytes is not None:
        kw["vmem_limit_bytes"] = int(min(VMEM_CAP, max(32 * 1024 * 1024, vmem_bytes)))
    return pltpu.CompilerParams(**kw)


def _nbytes(shape, dtype):
    return math.prod(shape) * jnp.dtype(dtype).itemsize


def _row_tile(rows, width, cap_bytes=1 << 20, mult=8):
    best = None
    for t in range(mult, rows + 1, mult):
        if rows % t == 0 and t * width * 4 <= cap_bytes:
            best = t
    return best if best is not None else rows


_MODES = {"all8": (8, (1, 2, 3, 4, 5, 6, 7), 0), "chips": (4, (2, 4, 6), 1), "sibling": (2, (1,), 0)}


class Rider:
    def __init__(self, arrs, mode, scatter=False):
        self.arrs, self.scatter = list(arrs), scatter
        self.nslot, self.deltas, self.shift = _MODES[mode]
        self.n = len(self.arrs)
        self.out_shape = [jax.ShapeDtypeStruct((self.nslot,) + (a.shape[1:] if scatter else a.shape), a.dtype)
                          for a in self.arrs]
        any_spec = pl.BlockSpec(memory_space=pl.ANY)
        self.in_specs = [any_spec] * self.n
        self.out_specs = [any_spec] * self.n
        n_peer = len(self.deltas)
        self.scratch = [pltpu.SemaphoreType.DMA((self.n, n_peer)), pltpu.SemaphoreType.DMA((self.n, n_peer)),
                        pltpu.SemaphoreType.DMA((self.n,))]

    def _copies(self, ins, outs, sems, arrivals):
        send_sems, recv_sems, local_sems = sems
        x, y, c = lax.axis_index("x"), lax.axis_index("y"), lax.axis_index("c")
        me = 4 * x + 2 * y + c
        slot_of = lambda dev: (dev >> self.shift) & (self.nslot - 1)
        src = lambda a, slot: ins[a].at[slot] if self.scatter else ins[a]
        flip = lambda v, bit: 1 - v if bit else v

        def remote(a, k, d, from_slot, to_slot):
            return pltpu.make_async_remote_copy(
                src_ref=src(a, from_slot), dst_ref=outs[a].at[to_slot], send_sem=send_sems.at[a, k],
                recv_sem=recv_sems.at[a, k], device_id=(flip(x, (d >> 2) & 1), flip(y, (d >> 1) & 1), flip(c, d & 1)),
                device_id_type=MESH)

        mine = slot_of(me)
        local = [pltpu.make_async_copy(src(a, mine), outs[a].at[mine], local_sems.at[a]) for a in range(self.n)]
        sends = [remote(a, k, d, slot_of(me ^ d), mine) for k, d in enumerate(self.deltas) for a in range(self.n)]
        if not arrivals:
            return local, sends
        return local, sends, [remote(a, k, d, mine, slot_of(me ^ d)) for k, d in enumerate(self.deltas) for a in range(self.n)]

    def start(self, ins, outs, sems):
        local, sends = self._copies(ins, outs, sems, arrivals=False)
        for cp in local + sends:
            cp.start()

    def wait(self, ins, outs, sems):
        local, sends, recvs = self._copies(ins, outs, sems, arrivals=True)
        for cp in recvs:
            cp.wait_recv()
        for cp in sends:
            cp.wait_send()
        for cp in local:
            cp.wait()


class Riders:
    def __init__(self, riders):
        self.riders = list(riders)
        self.n = sum(r.n for r in self.riders)
        cat = lambda attr: [v for r in self.riders for v in getattr(r, attr)]
        self.arrs, self.out_shape, self.in_specs = cat("arrs"), cat("out_shape"), cat("in_specs")
        self.out_specs, self.scratch = cat("out_specs"), cat("scratch")

    def _each(self, method, ins, outs, sems):
        i = s = 0
        for r in self.riders:
            getattr(r, method)(ins[i:i + r.n], outs[i:i + r.n], sems[s:s + len(r.scratch)])
            i, s = i + r.n, s + len(r.scratch)

    def start(self, ins, outs, sems):
        self._each("start", ins, outs, sems)

    def wait(self, ins, outs, sems):
        self._each("wait", ins, outs, sems)


class _Hosted:
    def __init__(self, rider, n_in, n_out, n_scratch, grid):
        self.rider, self.n_in, self.n_out, self.n_scratch, self.grid = rider, n_in, n_out, n_scratch, grid
        self.n = rider.n if rider else 0

    def split(self, refs):
        a, b = self.n_in, self.n_in + self.n
        c, e = b + self.n_out, b + self.n_out + self.n
        self._r = (refs[a:b], refs[c:e], refs[e + self.n_scratch:])
        if self.rider:
            ids = [pl.program_id(ax) for ax in range(len(self.grid))]
            first = functools.reduce(jnp.logical_and, [i == 0 for i in ids]) if ids else True
            pl.when(first)(lambda: self.rider.start(*self._r))
        return refs[:a], refs[b:c], refs[e:e + self.n_scratch]

    def finish(self):
        if self.rider:
            ids = [pl.program_id(ax) for ax in range(len(self.grid))]
            last = functools.reduce(jnp.logical_and, [i == n - 1 for i, n in zip(ids, self.grid)]) if ids else True
            pl.when(last)(lambda: self.rider.wait(*self._r))

    def call_args(self, in_specs, out_shape, out_specs, scratch, args):
        r = self.rider
        if not r:
            return list(in_specs), tuple(out_shape), tuple(out_specs), list(scratch), list(args)
        return (list(in_specs) + r.in_specs, tuple(out_shape) + tuple(r.out_shape), tuple(out_specs) + tuple(r.out_specs),
                list(scratch) + r.scratch, list(args) + r.arrs)

    def results(self, res, unwrap=True):
        res = list(res) if isinstance(res, (tuple, list)) else [res]
        host = res[:self.n_out]
        host = host[0] if (self.n_out == 1 and unwrap) else tuple(host)
        return (host, res[self.n_out:]) if self.rider else host


def exchange(name, arrs, mode, scatter=False):
    rider = Rider(arrs, mode, scatter)

    def body(*refs):
        ins, outs, sems = refs[:rider.n], refs[rider.n:2 * rider.n], refs[2 * rider.n:]
        rider.start(ins, outs, sems)
        rider.wait(ins, outs, sems)

    return pl.pallas_call(
        body, name=name, out_shape=tuple(rider.out_shape), in_specs=rider.in_specs, out_specs=tuple(rider.out_specs),
        scratch_shapes=rider.scratch,
    )(*arrs)


_DIMS = {"nn": (((1,), (0,)), ((), ())), "nt": (((1,), (1,)), ((), ())), "tn": (((0,), (0,)), ((), ()))}


def matmul(name, pairs, kind, *, a_ch=False, b_ch=False, out_ch=False, out_dtype=F32, rows=None, row_off=0, tm=512,
           rider=None, post=None, fold=False):
    a0, b0 = pairs[0]
    n_chunk = a0.shape[0] if a_ch else (b0.shape[0] if b_ch else 1)
    total_rows = a0.shape[-2]
    rows = total_rows - row_off if rows is None else rows
    tm = min(tm, rows)
    assert rows % tm == 0 and row_off % tm == 0, (name, rows, tm, row_off)
    n_rt, off = rows // tm, row_off // tm
    dims = _DIMS[kind]
    n_pair = len(pairs)

    if kind == "tn":
        grid, red_axis, n_red = (n_chunk, n_rt), 1, n_rt
        a_idx = (lambda k, i: (k, i + off, 0)) if a_ch else (lambda k, i: (i + off, 0))
        b_idx = (lambda k, i: (k, i + off, 0)) if b_ch else (lambda k, i: (i + off, 0))
        a_blk = lambda a: ((None, tm, a.shape[-1]) if a_ch else (tm, a.shape[-1]))
        b_blk = lambda b: ((None, tm, b.shape[-1]) if b_ch else (tm, b.shape[-1]))
        o2 = (a0.shape[-1], b0.shape[-1])
        out_shape = ((n_chunk,) + o2) if out_ch else o2
        out_spec = pl.BlockSpec((None,) + o2, lambda k, i: (k, 0, 0)) if out_ch else pl.BlockSpec(o2, lambda k, i: (0, 0))
        acc_shape = o2
    else:
        n_out = b0.shape[-1] if kind == "nn" else b0.shape[-2]
        b2 = b0.shape[-2:]
        if a_ch and b_ch and not out_ch and fold:
            grid, red_axis, n_red = (n_rt,), None, 1
            a_idx, b_idx = (lambda i: (0, i + off, 0)), (lambda i: (0, 0, 0))
            a_blk = lambda a: (n_chunk, tm, a.shape[-1])
            b_blk = lambda b: tuple(b.shape)
            out_shape, out_spec = (rows, n_out), pl.BlockSpec((tm, n_out), lambda i: (i, 0))
        elif a_ch and b_ch and not out_ch:
            grid, red_axis, n_red = (n_rt, n_chunk), 1, n_chunk
            a_idx, b_idx = (lambda i, k: (k, i + off, 0)), (lambda i, k: (k, 0, 0))
            a_blk = lambda a: (None, tm, a.shape[-1])
            b_blk = lambda b: (None,) + tuple(b.shape[-2:])
            out_shape, out_spec = (rows, n_out), pl.BlockSpec((tm, n_out), lambda i, k: (i, 0))
        elif out_ch:
            assert b_ch and not a_ch
            grid, red_axis, n_red = (n_chunk, n_rt), None, 1
            a_idx, b_idx = (lambda k, i: (i + off, 0)), (lambda k, i: (k, 0, 0))
            a_blk = lambda a: (tm, a.shape[-1])
            b_blk = lambda b: (None,) + tuple(b.shape[-2:])
            out_shape, out_spec = (n_chunk, rows, n_out), pl.BlockSpec((None, tm, n_out), lambda k, i: (k, i, 0))
        else:
            assert not (a_ch or b_ch)
            grid, red_axis, n_red = (n_rt,), None, 1
            a_idx, b_idx = (lambda i: (i + off, 0)), (lambda i: (0, 0))
            a_blk = lambda a: (tm, a.shape[-1])
            b_blk = lambda b: tuple(b.shape)
            out_shape, out_spec = (rows, n_out), pl.BlockSpec((tm, n_out), lambda i: (i, 0))
        acc_shape = (tm, n_out)

    post_ins, post_fn, out_dtypes = ([], None, [out_dtype]) if post is None else post
    hosted = _Hosted(rider, 2 * n_pair + len(post_ins), len(out_dtypes), int(n_red > 1), grid)

    def body(*refs):
        ins, outs, scr = hosted.split(refs)

        def compute():
            acc = None
            for p in range(n_pair):
                for k in ([None] if not fold else range(n_chunk)):
                    pick = (lambda r: r[...]) if k is None else (lambda r: r[k])
                    d = lax.dot_general(pick(ins[2 * p]).astype(BF16), pick(ins[2 * p + 1]).astype(BF16), dims,
                                        preferred_element_type=F32)
                    acc = d if acc is None else acc + d
            return acc

        def emit(acc):
            vals = (acc,) if post_fn is None else post_fn(acc, *[r[...].astype(F32) for r in ins[2 * n_pair:]])
            for o_ref, v in zip(outs, vals):
                o_ref[...] = v.astype(o_ref.dtype)

        sub = 256
        if n_red == 1 and post_fn is not None and not fold and tm % sub == 0 and tm > sub:
            for s in range(tm // sub):
                rs = slice(s * sub, (s + 1) * sub)
                acc = None
                for p in range(n_pair):
                    d_ = lax.dot_general(ins[2 * p][rs, :].astype(BF16), ins[2 * p + 1][...].astype(BF16), dims,
                                         preferred_element_type=F32)
                    acc = d_ if acc is None else acc + d_
                vals = post_fn(acc, *[r[rs, :].astype(F32) for r in ins[2 * n_pair:]])
                for o_ref, v in zip(outs, vals):
                    o_ref[rs, :] = v.astype(o_ref.dtype)
        elif n_red == 1:
            emit(compute())
        else:
            acc_ref = scr[0]
            r = pl.program_id(red_axis)

            @pl.when(r == 0)
            def _():
                acc_ref[...] = jnp.zeros_like(acc_ref)

            acc_ref[...] += compute()

            @pl.when(r == n_red - 1)
            def _():
                emit(acc_ref[...])
        hosted.finish()

    in_specs, args, vmem = [], [], 0
    for a, b in pairs:
        in_specs += [pl.BlockSpec(a_blk(a), a_idx), pl.BlockSpec(b_blk(b), b_idx)]
        args += [a, b]
        vmem += 2 * (_nbytes([s for s in a_blk(a) if s], a.dtype) + _nbytes([s for s in b_blk(b) if s], b.dtype))
    in_specs += [out_spec] * len(post_ins)
    args += list(post_ins)
    vmem += (3 + 2 * n_pair + 2 * len(post_ins) + 2 * len(out_dtypes)) * _nbytes(acc_shape, F32)
    scratch = [pltpu.VMEM(acc_shape, F32)] if n_red > 1 else []
    in_specs, out_shapes, out_specs, scratch, args = hosted.call_args(
        in_specs, [jax.ShapeDtypeStruct(out_shape, dt) for dt in out_dtypes], [out_spec] * len(out_dtypes), scratch, args)
    return hosted.results(pl.pallas_call(
        body, name=name, out_shape=out_shapes, grid=grid, in_specs=in_specs, out_specs=out_specs,
        scratch_shapes=scratch, compiler_params=_params(vmem + (8 << 20), len(grid)),
    )(*args))


def row(arr, width=None, cb=0, roff=0):
    return (arr, arr.shape[-1] if width is None else width, cb, roff)


def _row_spec(desc, tm, limit=None):
    _, width, cb, roff = desc
    if limit is None:
        return pl.BlockSpec((tm, width), lambda i: (i + roff, cb))
    return pl.BlockSpec((tm, width), lambda i: (jnp.minimum(i, limit - 1) + roff, cb))


def _segmenter(tm, seq_len, n_lat):
    seg = lambda i: jnp.where(i * tm < n_lat, (i * tm) // seq_len, n_lat // seq_len)
    first = lambda i: jnp.where(i * tm < n_lat, (i * tm) % seq_len == 0, i * tm == n_lat)
    return seg, first


def rowwise(name, fn, rows, segs, params, outs, *, tm, n_tiles, seg_fn=None, rider=None):
    n_r, n_s, n_p = len(rows), len(segs), len(params)
    hosted = _Hosted(rider, n_r + n_s + n_p, len(outs), 0, (n_tiles,))

    def body(*refs):
        ins, out_refs, _ = hosted.split(refs)
        vals = [r[...].astype(F32) for r in ins[:n_r]] + [r[...] for r in ins[n_r:]]
        res = fn(*vals)
        for o_ref, v in zip(out_refs, res):
            o_ref[...] = v.astype(o_ref.dtype)
        hosted.finish()

    in_specs = [_row_spec(d, tm) for d in rows]
    in_specs += [pl.BlockSpec((None, 1, s.shape[-1]), lambda i: (seg_fn(i), 0, 0)) for s in segs]
    in_specs += [pl.BlockSpec(p.shape, lambda i: (0, 0)) for p in params]
    vmem = sum(2 * tm * d[1] * 4 for d in rows) + sum(3 * tm * w * 4 for _, w, _ in outs) + sum(2 * p.size * 4 for p in params)
    in_specs, out_shapes, out_specs, scratch, args = hosted.call_args(
        in_specs, [jax.ShapeDtypeStruct((r, w), dt) for r, w, dt in outs],
        [pl.BlockSpec((tm, w), lambda i: (i, 0)) for _, w, _ in outs], [], [d[0] for d in rows] + list(segs) + list(params))
    return hosted.results(pl.pallas_call(
        body, name=name, grid=(n_tiles,), in_specs=in_specs, out_shape=out_shapes, out_specs=out_specs,
        scratch_shapes=scratch, compiler_params=_params(2 * vmem + (8 << 20)),
    )(*args), unwrap=False)


def rowwise_bwd(name, fn, rows, segs, params, cts, row_grads, *, tm, n_tiles, seg_fn=None, first_fn=None, adds=None,
                rider=None):
    adds = adds or {}
    need = [k for k, v in enumerate(row_grads) if v is not None]
    n_r, n_s, n_p = len(rows), len(segs), len(params)
    n_ct = sum(len(lst) for lst in cts)
    add_keys = sorted(adds)
    hosted = _Hosted(rider, n_r + n_s + n_p + n_ct + len(add_keys), len(need) + n_s + n_p, 0, (n_tiles,))

    def body(*refs):
        host_in, host_out, _ = hosted.split(refs)
        it = iter(list(host_in) + list(host_out))
        row_refs = [next(it) for _ in range(n_r)]
        seg_refs = [next(it) for _ in range(n_s)]
        par_refs = [next(it) for _ in range(n_p)]
        ct_refs = [[next(it) for _ in lst] for lst in cts]
        add_refs = {k: next(it) for k in add_keys}
        rg_refs = {k: next(it) for k in need}
        sg_refs = [next(it) for _ in range(n_s)]
        pg_refs = [next(it) for _ in range(n_p)]
        i = pl.program_id(0)
        rv = [r[...].astype(F32) for r in row_refs]
        sv = [r[...] for r in seg_refs]
        pv = [r[...] for r in par_refs]

        def f(*args):
            rr = list(rv)
            for j, k in enumerate(need):
                rr[k] = args[j]
            return fn(*rr, *args[len(need):])

        _, vjp = jax.vjp(f, *[rv[k] for k in need], *sv, *pv)
        ctv = []
        for lst in ct_refs:
            acc = lst[0][...].astype(F32)
            for r in lst[1:]:
                acc = acc + r[...].astype(F32)
            ctv.append(acc)
        g = vjp(tuple(ctv))
        for j, k in enumerate(need):
            gv = g[j]
            if k in adds:
                lim = adds[k][1]
                av = add_refs[k][...].astype(F32)
                gv = gv + (av if lim is None else jnp.where(i < lim, av, 0.0))
            lim = row_grads[k][2]
            if lim is None:
                rg_refs[k][...] = gv.astype(rg_refs[k].dtype)
            else:
                @pl.when(i < lim)
                def _(gv=gv, k=k):
                    rg_refs[k][...] = gv.astype(rg_refs[k].dtype)
        if n_s:
            opens = first_fn(i)
            for ref, gv in zip(sg_refs, g[len(need):len(need) + n_s]):
                @pl.when(opens)
                def _(ref=ref, gv=gv):
                    ref[...] = gv

                @pl.when(jnp.logical_not(opens))
                def _(ref=ref, gv=gv):
                    ref[...] += gv
        for ref, gv in zip(pg_refs, g[len(need) + n_s:]):
            @pl.when(i == 0)
            def _(ref=ref, gv=gv):
                ref[...] = gv

            @pl.when(i > 0)
            def _(ref=ref, gv=gv):
                ref[...] += gv
        hosted.finish()

    seg_spec = lambda s: pl.BlockSpec((None, 1, s.shape[-1]), lambda i: (seg_fn(i), 0, 0))
    par_spec = lambda p: pl.BlockSpec(p.shape, lambda i: (0, 0))
    in_specs = [_row_spec(d, tm) for d in rows] + [seg_spec(s) for s in segs] + [par_spec(p) for p in params]
    args = [d[0] for d in rows] + list(segs) + list(params)
    for lst in cts:
        in_specs += [_row_spec(d, tm) for d in lst]
        args += [d[0] for d in lst]
    for k in add_keys:
        in_specs.append(_row_spec(adds[k][0], tm, adds[k][1]))
        args.append(adds[k][0][0])
    out_shape, out_specs = [], []
    for k in need:
        n_rows, dt, lim = row_grads[k]
        out_shape.append(jax.ShapeDtypeStruct((n_rows, rows[k][1]), dt))
        out_specs.append(_row_spec((None, rows[k][1], 0, 0), tm, lim))
    for s in segs:
        out_shape.append(jax.ShapeDtypeStruct(s.shape, F32))
        out_specs.append(seg_spec(s))
    for p in params:
        out_shape.append(jax.ShapeDtypeStruct(p.shape, F32))
        out_specs.append(par_spec(p))
    vmem = sum(tm * d[1] * 4 for d in rows) * 6 + n_ct * tm * max(d[1] for d in rows) * 8
    in_specs, out_shape, out_specs, scratch, args = hosted.call_args(in_specs, out_shape, out_specs, [], args)
    return hosted.results(pl.pallas_call(
        body, name=name, grid=(n_tiles,), in_specs=in_specs, out_shape=out_shape, out_specs=out_specs,
        scratch_shapes=scratch, compiler_params=_params(vmem + (8 << 20)),
    )(*args), unwrap=False)


def _silu(v):
    return v * jax.nn.sigmoid(v)


def _rms(v, w):
    return v * lax.rsqrt(jnp.mean(v * v, axis=-1, keepdims=True) + EPS) * w


def fn_norm_mod(x, shift, scale, w):
    return (_rms(x, w) * (1.0 + scale) + shift,)


def fn_act(g, u):
    return (_silu(g) * u,)


def make_fn_resid(coef):
    def fn(x, f, gate):
        return (x + coef * gate * f,)
    return fn


def fn_silu_bias(v, b):
    return (_silu(v + b),)


def make_fn_gate_groupnorm(width):
    half = width // 2

    def fn(yf, yb, z, w):
        y = (yf + yb) * _silu(z)
        lane = lax.broadcasted_iota(jnp.int32, y.shape, 1)
        lo = lane < half
        sq = y * y
        s_lo = jnp.sum(jnp.where(lo, sq, 0.0), axis=-1, keepdims=True)
        s_hi = jnp.sum(jnp.where(lo, 0.0, sq), axis=-1, keepdims=True)
        r = jnp.where(lo, lax.rsqrt(s_lo / half + EPS), lax.rsqrt(s_hi / half + EPS))
        return (y * r * w,)
    return fn


def fn_glu(a, b):
    return (a * jax.nn.sigmoid(b),)


def fn_ln_silu(vw, vh, cb, lw, lb):
    v = jnp.concatenate([vw, vh], axis=-1) + cb
    mu = jnp.mean(v, axis=-1, keepdims=True)
    var = jnp.mean(jnp.square(v - mu), axis=-1, keepdims=True)
    return (_silu((v - mu) * lax.rsqrt(var + EPS) * lw + lb),)


def _col_tile(width):
    return width // 3 if width % (3 * LANES) == 0 else width


def mod_fwd(a_rows, w_shard, b_shard):
    n, d = a_rows.shape
    ws = w_shard.shape[1]
    tn = _col_tile(ws)

    def body(a_ref, w_ref, b_ref, o_ref):
        a = _silu(a_ref[...]).astype(BF16)
        o_ref[...] = jnp.dot(a, w_ref[...].astype(BF16), preferred_element_type=F32) + b_ref[...]

    return pl.pallas_call(
        body, name="mod_fwd", grid=(ws // tn,), out_shape=jax.ShapeDtypeStruct((n, ws), F32),
        in_specs=[pl.BlockSpec((n, d), lambda j: (0, 0)), pl.BlockSpec((d, tn), lambda j: (0, j)),
                  pl.BlockSpec((1, tn), lambda j: (0, j))],
        out_specs=pl.BlockSpec((n, tn), lambda j: (0, j)), compiler_params=_params(),
    )(a_rows, w_shard, b_shard)


def mod_bwd(a_rows, d_shard, d_full, w_shard, ctx_rows):
    n, d = a_rows.shape
    ws = w_shard.shape[1]
    tn = _col_tile(ws)
    n_ct = ws // tn

    def body(a_ref, ds_ref, df_ref, w_ref, gw_ref, gb_ref, q_ref):
        j = pl.program_id(0)
        a = _silu(a_ref[...])
        ds = ds_ref[...]
        gw_ref[...] = lax.dot_general(a, ds, _DIMS["tn"], precision=HI, preferred_element_type=F32)
        dctx = ds[ctx_rows[0]:ctx_rows[0] + 1, :]
        for r in ctx_rows[1:]:
            dctx = dctx + ds[r:r + 1, :]
        q = lax.dot_general(jnp.broadcast_to(dctx, (8, tn)), w_ref[...], _DIMS["nt"], precision=HI,
                            preferred_element_type=F32)

        @pl.when(j == 0)
        def _():
            q_ref[...] = q
            df = df_ref[...]
            acc = df[0:1, :]
            for r in range(1, n):
                acc = acc + df[r:r + 1, :]
            gb_ref[...] = acc

        @pl.when(j > 0)
        def _():
            q_ref[...] += q

    return pl.pallas_call(
        body, name="mod_bwd", grid=(n_ct,),
        out_shape=(jax.ShapeDtypeStruct((d, ws), F32), jax.ShapeDtypeStruct((1, d_full.shape[1]), F32),
                   jax.ShapeDtypeStruct((8, d), F32)),
        in_specs=[pl.BlockSpec((n, d), lambda j: (0, 0)), pl.BlockSpec((n, tn), lambda j: (0, j)),
                  pl.BlockSpec(d_full.shape, lambda j: (0, 0)), pl.BlockSpec((d, tn), lambda j: (0, j))],
        out_specs=(pl.BlockSpec((d, tn), lambda j: (0, j)), pl.BlockSpec((1, d_full.shape[1]), lambda j: (0, 0)),
                   pl.BlockSpec((8, d), lambda j: (0, 0))),
        compiler_params=_params(40 << 20),
    )(a_rows, d_shard, d_full, w_shard)


def _shifted(xs, d, tok, width):
    if d == 0:
        return xs
    n = xs.shape[0]
    sh = pltpu.roll(xs, (-d) % n, axis=0)
    return jnp.where((tok + d >= 0) & (tok + d < width), sh, 0.0)


def _placed(out_shape, place):
    if place is None:
        return out_shape, 0, 0, None
    return place


def tapsum_roll(name, x, xcb, w, wcb, *, seq_len, n_seq, row_blk_off, width, piece, cb, ncb, pad, flip, place=None):
    n_tap = w.shape[0]
    n_piece = seq_len // piece
    out_shape, o_rb, o_cb, into = _placed((n_seq * seq_len, ncb * cb), place)

    def body(x_ref, w_ref, *rest):
        o_ref = rest[-1]
        wv = w_ref[...]
        tok = lax.broadcasted_iota(jnp.int32, (piece, 1), 0) % width

        def do_piece(p, carry):
            start = pl.multiple_of(p * piece, piece)
            xs = x_ref[pl.ds(start, piece), :]
            acc = jnp.zeros_like(xs)
            for k in range(n_tap):
                d = pad - k if flip else k - pad
                acc = acc + wv[k:k + 1, :] * _shifted(xs, d, tok, width)
            o_ref[pl.ds(start, piece), :] = acc
            return carry

        lax.fori_loop(0, n_piece, do_piece, 0)

    extra = [] if into is None else [into]
    return pl.pallas_call(
        body, name=name, grid=(ncb, n_seq), out_shape=jax.ShapeDtypeStruct(out_shape, F32),
        in_specs=[pl.BlockSpec((seq_len, cb), lambda j, s: (row_blk_off + s, xcb + j)),
                  pl.BlockSpec((n_tap, cb), lambda j, s: (0, wcb + j))] + [pl.BlockSpec(memory_space=pl.ANY)] * len(extra),
        out_specs=pl.BlockSpec((seq_len, cb), lambda j, s: (o_rb + s, o_cb + j)),
        input_output_aliases={2: 0} if extra else {},
        compiler_params=_params(8 * seq_len * cb * 4 + (8 << 20), 2),
    )(x, w, *extra)


def tapgrad_roll(name, dy, dycb, dy_blk_off, x, xcb, x_blk_off, *, n_tap, seq_len, n_seq, width, piece, cb, ncb, pad):
    n_piece = seq_len // piece

    def body(dy_ref, x_ref, o_ref):
        @pl.when(pl.program_id(1) == 0)
        def _():
            o_ref[...] = jnp.zeros_like(o_ref)

        tok = lax.broadcasted_iota(jnp.int32, (piece, 1), 0) % width

        def do_piece(p, carry):
            start = pl.multiple_of(p * piece, piece)
            xs = x_ref[pl.ds(start, piece), :]
            dv = dy_ref[pl.ds(start, piece), :]
            for k in range(n_tap):
                o_ref[k:k + 1, :] += jnp.sum(dv * _shifted(xs, k - pad, tok, width), axis=0, keepdims=True)
            return carry

        lax.fori_loop(0, n_piece, do_piece, 0)

    return pl.pallas_call(
        body, name=name, grid=(ncb, n_seq), out_shape=jax.ShapeDtypeStruct((n_tap, ncb * cb), F32),
        in_specs=[pl.BlockSpec((seq_len, cb), lambda j, s: (dy_blk_off + s, dycb + j)),
                  pl.BlockSpec((seq_len, cb), lambda j, s: (x_blk_off + s, xcb + j))],
        out_specs=pl.BlockSpec((n_tap, cb), lambda j, s: (0, j)),
        compiler_params=_params(8 * seq_len * cb * 4 + (8 << 20), 2),
    )(dy, x)


def tapsum_rows(name, x, xcb, w, wcb, *, seq_len, n_seq, cb, ncb, pad, flip, place=None):
    n_tap = w.shape[0]
    n_row = seq_len // GRID_W
    halo = pad * GRID_W
    out_shape, o_rb, o_cb, into = _placed((n_seq * seq_len, ncb * cb), place)

    def body(x_ref, w_ref, *rest):
        o_ref, xp = rest[-2:]
        xp[pl.ds(0, halo), :] = jnp.zeros((halo, cb), F32)
        xp[pl.ds(halo + seq_len, halo), :] = jnp.zeros((halo, cb), F32)
        xp[pl.ds(halo, seq_len), :] = x_ref[...]
        wv = w_ref[...]

        def do_row(r, carry):
            acc = jnp.zeros((GRID_W, cb), F32)
            for k in range(n_tap):
                d = pad - k if flip else k - pad
                acc = acc + wv[k:k + 1, :] * xp[pl.ds(pl.multiple_of((r + pad + d) * GRID_W, GRID_W), GRID_W), :]
            o_ref[pl.ds(pl.multiple_of(r * GRID_W, GRID_W), GRID_W), :] = acc
            return carry

        lax.fori_loop(0, n_row, do_row, 0)

    extra = [] if into is None else [into]
    return pl.pallas_call(
        body, name=name, grid=(ncb, n_seq), out_shape=jax.ShapeDtypeStruct(out_shape, F32),
        in_specs=[pl.BlockSpec((seq_len, cb), lambda j, s: (s, xcb + j)),
                  pl.BlockSpec((n_tap, cb), lambda j, s: (0, wcb + j))] + [pl.BlockSpec(memory_space=pl.ANY)] * len(extra),
        out_specs=pl.BlockSpec((seq_len, cb), lambda j, s: (o_rb + s, o_cb + j)),
        input_output_aliases={2: 0} if extra else {},
        scratch_shapes=[pltpu.VMEM((seq_len + 2 * halo, cb), F32)],
        compiler_params=_params(10 * seq_len * cb * 4 + (8 << 20), 2),
    )(x, w, *extra)


def tapgrad_rows(name, dy, dycb, x, xcb, *, n_tap, seq_len, n_seq, cb, ncb, pad):
    n_row = seq_len // GRID_W
    halo = pad * GRID_W

    def body(dy_ref, x_ref, o_ref, xp):
        @pl.when(pl.program_id(1) == 0)
        def _():
            o_ref[...] = jnp.zeros_like(o_ref)

        xp[pl.ds(0, halo), :] = jnp.zeros((halo, cb), F32)
        xp[pl.ds(halo + seq_len, halo), :] = jnp.zeros((halo, cb), F32)
        xp[pl.ds(halo, seq_len), :] = x_ref[...]

        def do_row(r, carry):
            dv = dy_ref[pl.ds(pl.multiple_of(r * GRID_W, GRID_W), GRID_W), :]
            for k in range(n_tap):
                xs = xp[pl.ds(pl.multiple_of((r + k) * GRID_W, GRID_W), GRID_W), :]
                o_ref[k:k + 1, :] += jnp.sum(dv * xs, axis=0, keepdims=True)
            return carry

        lax.fori_loop(0, n_row, do_row, 0)

    return pl.pallas_call(
        body, name=name, grid=(ncb, n_seq), out_shape=jax.ShapeDtypeStruct((n_tap, ncb * cb), F32),
        in_specs=[pl.BlockSpec((seq_len, cb), lambda j, s: (s, dycb + j)),
                  pl.BlockSpec((seq_len, cb), lambda j, s: (s, xcb + j))],
        out_specs=pl.BlockSpec((n_tap, cb), lambda j, s: (0, j)),
        scratch_shapes=[pltpu.VMEM((seq_len + 2 * halo, cb), F32)],
        compiler_params=_params(10 * seq_len * cb * 4 + (8 << 20), 2),
    )(dy, x)


def _ssd_blocks(b, s, *, rev, n_ctx, n_lat, lat_blocks):
    if rev:
        return jnp.where(s < n_ctx, lat_blocks + b * n_ctx + (n_ctx - 1 - s), b * n_lat + (n_lat - 1 - (s - n_ctx)))
    return jnp.where(s < n_ctx, lat_blocks + b * n_ctx + s, b * n_lat + (s - n_ctx))


def _ssd_common(xbc, raw, dtb, alog, dsk, *, rev, ds, n_head):
    if rev:
        raw = pltpu.roll(raw, LANES - n_head, axis=1)
    pre = raw + dtb
    dt = jnp.maximum(pre, 0.0) + jnp.log1p(jnp.exp(-jnp.abs(pre)))
    sig = jax.nn.sigmoid(pre)
    a = -jnp.exp(alog)
    da = dt * a
    ri = lax.broadcasted_iota(jnp.int32, (CHUNK, CHUNK), 0)
    ci = lax.broadcasted_iota(jnp.int32, (CHUNK, CHUNK), 1)
    mask = (ci >= ri) if rev else (ci <= ri)
    tri = mask.astype(F32)
    tri_t = ((ci <= ri) if rev else (ci >= ri)).astype(F32)
    cs = jnp.dot(tri, da, precision=HI, preferred_element_type=F32)
    tot = jnp.sum(da, axis=0, keepdims=True)
    def wide(v):
        first = lax.broadcasted_iota(jnp.int32, (v.shape[0], LANES), 1) < HEAD_DIM
        return jnp.concatenate(
            [jnp.where(first, jnp.broadcast_to(v[:, 2 * p:2 * p + 1], first.shape),
                       jnp.broadcast_to(v[:, 2 * p + 1:2 * p + 2], first.shape)) for p in range(n_head // 2)], axis=1)

    cs_w, tot_w = wide(cs), wide(tot)
    xh = xbc[:, :ds]
    dt_w = wide(dt)
    return dict(
        dt=dt, sig=sig, a=a, cs=cs, cs_t=cs.T, tot=tot, mask=mask, tri_t=tri_t,
        e_w=jnp.exp(cs_w), wt_w=jnp.exp(tot_w - cs_w), dec_w=jnp.exp(tot_w), dt_w=dt_w, dsk_w=wide(dsk),
        xh=xh, xs_w=xh * dt_w, bm=xbc[:, ds:ds + 2 * N_STATE], cm=xbc[:, ds + 2 * N_STATE:ds + 4 * N_STATE])


def _decay(q, col):
    seg = q["cs"][:, col:col + 1] - q["cs_t"][col:col + 1, :]
    return jnp.exp(jnp.where(q["mask"], seg, -jnp.inf))


def _split_heads(v):
    lane = lax.broadcasted_iota(jnp.int32, v.shape, 1)
    return jnp.concatenate([jnp.where(lane < HEAD_DIM, v, 0.0), jnp.where(lane >= HEAD_DIM, v, 0.0)], axis=0)


def ssd_fwd(name, xbc, proj, dt_cb, dtb, alog, dsk, *, rev, n_ex, seq_len, ctx_len, ds, rider=None):
    n_head, half = ds // HEAD_DIM, ds // 2
    n_ctx, n_lat = ctx_len // CHUNK, seq_len // CHUNK
    n_step = n_ctx + n_lat
    blk = functools.partial(_ssd_blocks, rev=rev, n_ctx=n_ctx, n_lat=n_lat, lat_blocks=n_ex * n_lat)
    xw = xbc.shape[1]

    def y_blk(b, s):
        sl = jnp.maximum(s, n_ctx) - n_ctx
        return b * n_lat + ((n_lat - 1 - sl) if rev else sl)

    hosted = _Hosted(rider, 5, 2, 1, (n_ex, n_step))

    def body(*refs):
        (xbc_ref, dt_ref, dtb_ref, alog_ref, dsk_ref), (y_ref, hs_ref), (h_scr,) = hosted.split(refs)

        @pl.when(pl.program_id(1) == 0)
        def _():
            h_scr[...] = jnp.zeros_like(h_scr)

        q = _ssd_common(xbc_ref[...], dt_ref[...], dtb_ref[...], alog_ref[...], dsk_ref[...], rev=rev, ds=ds, n_head=n_head)
        h = h_scr[...]
        hs_ref[...] = h
        for g in range(2):
            lo = g * half
            bg = q["bm"][:, g * N_STATE:(g + 1) * N_STATE].astype(BF16)
            cg = q["cm"][:, g * N_STATE:(g + 1) * N_STATE].astype(BF16)
            scores = lax.dot_general(cg, bg, _DIMS["nt"], preferred_element_type=F32)
            hg = h[:, lo:lo + half]
            off = jnp.dot(cg, hg.astype(BF16), preferred_element_type=F32)
            for j in range(half // LANES):
                c0 = (lo + j * LANES) // HEAD_DIM
                ln = slice(lo + j * LANES, lo + (j + 1) * LANES)
                p_cat = jnp.concatenate([scores * _decay(q, c0), scores * _decay(q, c0 + 1)], axis=1).astype(BF16)
                diag = jnp.dot(p_cat, _split_heads(q["xs_w"][:, ln]).astype(BF16), preferred_element_type=F32)
                y_ref[:, ln] = (diag + q["e_w"][:, ln] * off[:, j * LANES:(j + 1) * LANES]
                                + q["dsk_w"][:, ln] * q["xh"][:, ln])
            v = (q["wt_w"][:, lo:lo + half] * q["xs_w"][:, lo:lo + half]).astype(BF16)
            h_scr[:, lo:lo + half] = (q["dec_w"][:, lo:lo + half] * hg
                                      + lax.dot_general(bg, v, _DIMS["tn"], preferred_element_type=F32))
        hosted.finish()

    vec = pl.BlockSpec((1, LANES), lambda b, s: (0, 0))
    in_specs, out_shape, out_specs, scratch, args = hosted.call_args(
        [pl.BlockSpec((CHUNK, xw), lambda b, s: (blk(b, s), 0)),
         pl.BlockSpec((CHUNK, LANES), lambda b, s: (blk(b, s), dt_cb)), vec, vec, vec],
        (jax.ShapeDtypeStruct((n_ex * seq_len, ds), F32), jax.ShapeDtypeStruct((n_ex, n_step, N_STATE, ds), F32)),
        (pl.BlockSpec((CHUNK, ds), lambda b, s: (y_blk(b, s), 0)),
         pl.BlockSpec((None, None, N_STATE, ds), lambda b, s: (b, s, 0, 0))),
        [pltpu.VMEM((N_STATE, ds), F32)], [xbc, proj, dtb, alog, dsk])
    return hosted.results(pl.pallas_call(
        body, name=name, grid=(n_ex, n_step), out_shape=out_shape, in_specs=in_specs, out_specs=out_specs,
        scratch_shapes=scratch, compiler_params=_params(40 << 20, 2),
    )(*args))


def ssd_bwd(name, xbc, proj, dt_cb, hs, dy, dtb, alog, dsk, *, rev, n_ex, seq_len, ctx_len, ds, rider=None):
    n_head, half = ds // HEAD_DIM, ds // 2
    n_ctx, n_lat = ctx_len // CHUNK, seq_len // CHUNK
    n_step = n_ctx + n_lat
    n_tok = n_ex * (seq_len + ctx_len)
    blk0 = functools.partial(_ssd_blocks, rev=rev, n_ctx=n_ctx, n_lat=n_lat, lat_blocks=n_ex * n_lat)
    step = lambda sp: n_step - 1 - sp
    blk = lambda b, sp: blk0(b, step(sp))
    xw = xbc.shape[1]

    def dy_blk(b, sp):
        sl = jnp.maximum(step(sp), n_ctx) - n_ctx
        return b * n_lat + ((n_lat - 1 - sl) if rev else sl)

    hosted = _Hosted(rider, 7, 5, 1, (n_ex, n_step))

    def body(*refs):
        ((xbc_ref, dt_ref, hs_ref, dy_ref, dtb_ref, alog_ref, dsk_ref),
         (dxbc_ref, ddt_ref, dalog_ref, ddtb_ref, ddsk_ref), (dh_scr,)) = hosted.split(refs)
        b, sp = pl.program_id(0), pl.program_id(1)

        @pl.when(sp == 0)
        def _():
            dh_scr[...] = jnp.zeros_like(dh_scr)

        @pl.when((sp == 0) & (b == 0))
        def _():
            dalog_ref[...] = jnp.zeros_like(dalog_ref)
            ddtb_ref[...] = jnp.zeros_like(ddtb_ref)
            ddsk_ref[...] = jnp.zeros_like(ddsk_ref)

        q = _ssd_common(xbc_ref[...], dt_ref[...], dtb_ref[...], alog_ref[...], dsk_ref[...], rev=rev, ds=ds, n_head=n_head)
        h = hs_ref[...]
        d_y = jnp.where(step(sp) >= n_ctx, dy_ref[...], 0.0)
        dh_next = dh_scr[...]
        lane_row = lax.broadcasted_iota(jnp.int32, (1, LANES), 1)
        d_cs = jnp.zeros((CHUNK, LANES), F32)
        dxs_parts, de_parts, dwt_parts, ddec_parts = [], [], [], []
        for g in range(2):
            lo = g * half
            gs = slice(lo, lo + half)
            bg = q["bm"][:, g * N_STATE:(g + 1) * N_STATE].astype(BF16)
            cg = q["cm"][:, g * N_STATE:(g + 1) * N_STATE].astype(BF16)
            scores = lax.dot_general(cg, bg, _DIMS["nt"], preferred_element_type=F32)
            hg, dyg, dhn = h[:, gs], d_y[:, gs], dh_next[:, gs]
            off = jnp.dot(cg, hg.astype(BF16), preferred_element_type=F32)
            d_off = (q["e_w"][:, gs] * dyg).astype(BF16)
            de_parts.append(dyg * off)
            d_c = lax.dot_general(d_off, hg.astype(BF16), _DIMS["nt"], preferred_element_type=F32)
            dh_scr[:, gs] = (lax.dot_general(cg, d_off, _DIMS["tn"], preferred_element_type=F32)
                             + q["dec_w"][:, gs] * dhn)
            b_dh = jnp.dot(bg, dhn.astype(BF16), preferred_element_type=F32)
            v = q["wt_w"][:, gs] * q["xs_w"][:, gs]
            d_b = lax.dot_general(v.astype(BF16), dhn.astype(BF16), _DIMS["nt"], preferred_element_type=F32)
            dwt_parts.append(q["xs_w"][:, gs] * b_dh)
            ddec_parts.append(jnp.sum(hg * dhn, axis=0, keepdims=True))
            d_scores = jnp.zeros((CHUNK, CHUNK), F32)
            for j in range(half // LANES):
                c0 = (lo + j * LANES) // HEAD_DIM
                ln = slice(lo + j * LANES, lo + (j + 1) * LANES)
                l0, l1 = _decay(q, c0), _decay(q, c0 + 1)
                p0, p1 = scores * l0, scores * l1
                dy_st = _split_heads(d_y[:, ln]).astype(BF16)
                d_p = lax.dot_general(dy_st, q["xs_w"][:, ln].astype(BF16), _DIMS["nt"], preferred_element_type=F32)
                d_p0, d_p1 = d_p[:CHUNK], d_p[CHUNK:]
                d_scores = d_scores + d_p0 * l0 + d_p1 * l1
                for col, t in ((c0, d_p0 * p0), (c0 + 1, d_p1 * p1)):
                    d_cs = d_cs + jnp.sum(t - t.T, axis=1, keepdims=True) * (lane_row == col).astype(F32)
                p_st = jnp.concatenate([p0, p1], axis=0).astype(BF16)
                dxs_parts.append(lax.dot_general(p_st, dy_st, _DIMS["tn"], preferred_element_type=F32)
                                 + q["wt_w"][:, ln] * b_dh[:, j * LANES:(j + 1) * LANES])
            d_sc = d_scores.astype(BF16)
            d_c = d_c + jnp.dot(d_sc, bg, preferred_element_type=F32)
            d_b = d_b + lax.dot_general(d_sc, cg, _DIMS["tn"], preferred_element_type=F32)
            dxbc_ref[:, ds + g * N_STATE:ds + (g + 1) * N_STATE] = d_b
            dxbc_ref[:, ds + (2 + g) * N_STATE:ds + (3 + g) * N_STATE] = d_c
        d_xs = jnp.concatenate(dxs_parts, axis=1)
        narrow_m = (lax.broadcasted_iota(jnp.int32, (ds, LANES), 0) // HEAD_DIM
                    == lax.broadcasted_iota(jnp.int32, (ds, LANES), 1)).astype(BF16)
        rows8 = lambda v: jnp.broadcast_to(v, (8, ds))
        stacked = jnp.concatenate(
            [jnp.concatenate(dwt_parts, axis=1), jnp.concatenate(de_parts, axis=1), d_xs * q["xh"],
             rows8(jnp.concatenate(ddec_parts, axis=1)), rows8(jnp.sum(d_y * q["xh"], axis=0, keepdims=True))], axis=0)
        hi = stacked.astype(BF16)
        lo = (stacked - hi.astype(F32)).astype(BF16)
        sums = (jnp.dot(hi, narrow_m, preferred_element_type=F32) + jnp.dot(lo, narrow_m, preferred_element_type=F32))
        n_wt, n_e, n_xs = sums[:CHUNK], sums[CHUNK:2 * CHUNK], sums[2 * CHUNK:3 * CHUNK]
        n_dec, n_dsk = sums[3 * CHUNK:3 * CHUNK + 1], sums[3 * CHUNK + 8:3 * CHUNK + 9]
        e, wt, dec = jnp.exp(q["cs"]), jnp.exp(q["tot"] - q["cs"]), jnp.exp(q["tot"])
        d_wt = n_wt * wt
        d_cs = d_cs + n_e * e - d_wt
        d_tot = jnp.sum(d_wt, axis=0, keepdims=True) + n_dec * dec
        d_da = jnp.dot(q["tri_t"], d_cs, precision=HI, preferred_element_type=F32) + d_tot
        d_dt = d_da * q["a"] + n_xs
        dxbc_ref[:, :ds] = d_xs * q["dt_w"] + q["dsk_w"] * d_y
        dalog_ref[...] += jnp.sum(d_da * q["dt"], axis=0, keepdims=True) * q["a"]
        d_raw = d_dt * q["sig"]
        ddtb_ref[...] += jnp.sum(d_raw, axis=0, keepdims=True)
        ddsk_ref[...] += n_dsk
        ddt_ref[...] = pltpu.roll(d_raw, n_head, axis=1) if rev else d_raw
        hosted.finish()

    vec = pl.BlockSpec((1, LANES), lambda b, s: (0, 0))
    vec_shape = jax.ShapeDtypeStruct((1, LANES), F32)
    in_specs, out_shape, out_specs, scratch, args = hosted.call_args(
        [pl.BlockSpec((CHUNK, xw), lambda b, s: (blk(b, s), 0)),
         pl.BlockSpec((CHUNK, LANES), lambda b, s: (blk(b, s), dt_cb)),
         pl.BlockSpec((None, None, N_STATE, ds), lambda b, s: (b, step(s), 0, 0)),
         pl.BlockSpec((CHUNK, ds), lambda b, s: (dy_blk(b, s), 0)), vec, vec, vec],
        (jax.ShapeDtypeStruct((n_tok, xw), F32), jax.ShapeDtypeStruct((n_tok, LANES), F32), vec_shape, vec_shape, vec_shape),
        (pl.BlockSpec((CHUNK, xw), lambda b, s: (blk(b, s), 0)),
         pl.BlockSpec((CHUNK, LANES), lambda b, s: (blk(b, s), 0)), vec, vec, vec),
        [pltpu.VMEM((N_STATE, ds), F32)], [xbc, proj, hs, dy, dtb, alog, dsk])
    return hosted.results(pl.pallas_call(
        body, name=name, grid=(n_ex, n_step), out_shape=out_shape, in_specs=in_specs, out_specs=out_specs,
        scratch_shapes=scratch, compiler_params=_params(48 << 20, 2),
    )(*args))


def final_loss(x3, target, w, *, tm):
    n, d = x3.shape

    def body(x_ref, t_ref, w_ref, dx_ref, dw_ref, loss_ref):
        i = pl.program_id(0)
        t = t_ref[...]

        def per_feature(xv, wv):
            err = _rms(xv, wv) - t
            return 0.5 * jnp.sum(err * err, axis=0, keepdims=True) / d

        lv, vjp = jax.vjp(per_feature, x_ref[...], w_ref[...])
        dx, dw = vjp(jnp.ones_like(lv))
        dx_ref[...] = dx

        @pl.when(i == 0)
        def _():
            dw_ref[...] = dw
            loss_ref[...] = lv

        @pl.when(i > 0)
        def _():
            dw_ref[...] += dw
            loss_ref[...] += lv

    tile = pl.BlockSpec((tm, d), lambda i: (i, 0))
    vec = pl.BlockSpec((1, d), lambda i: (0, 0))
    return pl.pallas_call(
        body, name="final_loss", grid=(n // tm,), in_specs=[tile, tile, vec],
        out_shape=(jax.ShapeDtypeStruct((n, d), F32), jax.ShapeDtypeStruct((1, d), F32), jax.ShapeDtypeStruct((1, d), F32)),
        out_specs=(tile, vec, vec), compiler_params=_params(tm * d * 4 * 16 + (8 << 20)),
    )(x3, target, w)


def sum_slots(name, arr):
    n_slot, n_row, width = arr.shape
    tm = _row_tile(n_row, width * n_slot, mult=16)

    def body(a_ref, o_ref):
        acc = a_ref[0].astype(F32)
        for j in range(1, n_slot):
            acc = acc + a_ref[j].astype(F32)
        o_ref[...] = acc

    return pl.pallas_call(
        body, name=name, grid=(n_row // tm,), out_shape=jax.ShapeDtypeStruct((n_row, width), F32),
        in_specs=[pl.BlockSpec((n_slot, tm, width), lambda i: (0, i, 0))],
        out_specs=pl.BlockSpec((tm, width), lambda i: (i, 0)), compiler_params=_params(),
    )(arr)


def adamw(name, w, g_slots, m, v):
    n_slot, n_row, width = g_slots.shape
    tm = _row_tile(n_row, width * 2)

    def body(w_ref, g_ref, m_ref, v_ref, go_ref, d_ref, mo_ref, vo_ref):
        g = g_ref[0]
        for j in range(1, n_slot):
            g = g + g_ref[j]
        m2 = ADAM_B1 * m_ref[...] + (1.0 - ADAM_B1) * g
        v2 = ADAM_B2 * v_ref[...] + (1.0 - ADAM_B2) * jnp.square(g)
        m_hat = m2 / (1.0 - ADAM_B1 ** ADAM_STEP)
        v_hat = v2 / (1.0 - ADAM_B2 ** ADAM_STEP)
        go_ref[...] = g
        d_ref[...] = -ADAM_LR * (m_hat / (jnp.sqrt(v_hat) + ADAM_EPS) + ADAM_WD * w_ref[...])
        mo_ref[...] = m2
        vo_ref[...] = v2

    tile = pl.BlockSpec((tm, width), lambda i: (i, 0))
    shape = jax.ShapeDtypeStruct((n_row, width), F32)
    return pl.pallas_call(
        body, name=name, grid=(n_row // tm,), out_shape=(shape,) * 4,
        in_specs=[tile, pl.BlockSpec((n_slot, tm, width), lambda i: (0, i, 0)), tile, tile],
        out_specs=(tile,) * 4, compiler_params=_params(),
    )(w, g_slots, m, v)


def cctx_grad(q_all, c_ctx_row):
    d = c_ctx_row.shape[1]

    def body(q_ref, c_ref, o_ref):
        acc = q_ref[0, 0:1, :]
        for j in (2, 4, 6):
            acc = acc + q_ref[j, 0:1, :]
        _, vjp = jax.vjp(_silu, c_ref[...])
        o_ref[...] = vjp(acc)[0]

    return pl.pallas_call(
        body, name="cctx_grad", out_shape=jax.ShapeDtypeStruct((1, d), F32),
    )(q_all, c_ctx_row)


def loss_total(pack_sum, d):
    def body(p_ref, o_ref):
        o_ref[...] = jnp.sum(p_ref[:, 0:d], axis=1, keepdims=True)

    return pl.pallas_call(
        body, name="loss_total", out_shape=jax.ShapeDtypeStruct((1, 1), F32),
    )(pack_sum)


class _Plan:
    def __init__(self):
        self.builders, self.got = {}, {}

    def on(self, host, key, builder):
        self.builders.setdefault(host, []).append((key, builder))

    def run(self, host, fn, *args, **kw):
        if host not in self.builders:
            return fn(host, *args, **kw)
        keys, riders = zip(*[(key, builder(self)) for key, builder in self.builders[host]])
        res, landed = fn(host, *args, rider=Riders(riders), **kw)
        for key, r in zip(keys, riders):
            self.got[key], landed = landed[:r.n], landed[r.n:]
        return res


def _val(w):
    return w() if callable(w) else w


def _matmul_tile(n_rows, tm):
    return 2 * tm if n_rows % (2 * tm) == 0 else tm


def _ffn_fwd(plan, tag, xin, n_rows, tm, seg_fn, shift, scale, gate, norm_w, wg, wu, wd):
    d = xin.shape[1]
    n_tiles = n_rows // tm
    (h,) = plan.run(f"{tag}_norm", rowwise, fn_norm_mod, [row(xin)], [shift, scale], [norm_w], [(n_rows, d, BF16)],
                    tm=tm, n_tiles=n_tiles, seg_fn=seg_fn)
    tmm = _matmul_tile(n_rows, tm)
    g = plan.run(f"{tag}_gate", matmul, [(h, _val(wg))], "nn", out_dtype=BF16, b_ch=True, out_ch=True, tm=tmm)
    u, act = plan.run(f"{tag}_up", matmul, [(h, _val(wu))], "nn", b_ch=True, out_ch=True, tm=tmm,
                      post=([g], lambda acc, gv: (acc, fn_act(gv, acc)[0]), [BF16, BF16]))
    f = plan.run(f"{tag}_down", matmul, [(act, _val(wd))], "nn", a_ch=True, b_ch=True, tm=tmm, fold=True)
    (xo,) = plan.run(f"{tag}_resid", rowwise, make_fn_resid(0.5), [row(xin), row(f)], [gate], [], [(n_rows, d, F32)],
                     tm=tm, n_tiles=n_tiles, seg_fn=seg_fn)
    return xo, (h, g, u, act, f)


def _ffn_bwd(plan, tag, d_xo, saved, xin, n_rows, tm, seg_fn, first_fn, shift, scale, gate, norm_w, wg, wu, wd, dx_rows, dx_limit):
    h, g, u, act, f = saved
    d = xin.shape[1]
    n_tiles = n_rows // tm
    n_ch, _, n_hid = g.shape
    d_f, d_gate = plan.run(f"{tag}_resid_bwd", rowwise_bwd, make_fn_resid(0.5), [row(xin), row(f)], [gate], [], [[row(d_xo)]],
                           [None, (n_rows, BF16, None)], tm=tm, n_tiles=n_tiles, seg_fn=seg_fn, first_fn=first_fn)
    tmm = _matmul_tile(n_rows, tm)
    act_vjp = lambda acc, gv, uv: jax.vjp(lambda a, b: fn_act(a, b)[0], gv, uv)[1](acc)
    d_g, d_u = plan.run(f"{tag}_down_dx", matmul, [(d_f, wd)], "nt", b_ch=True, out_ch=True, tm=tmm,
                        post=([g, u], act_vjp, [BF16, BF16]))
    plan.got[f"{tag}_d_wd"] = plan.run(f"{tag}_down_dw", matmul, [(act, d_f)], "tn", out_dtype=BF16, a_ch=True, out_ch=True, tm=tmm)
    d_h = plan.run(f"{tag}_up_dx", matmul, [(d_g, wg), (d_u, wu)], "nt", a_ch=True, b_ch=True, tm=tmm)
    plan.got[f"{tag}_d_wg"] = plan.run(f"{tag}_gate_dw", matmul, [(d_g, h)], "tn", out_dtype=BF16, a_ch=True, out_ch=True, tm=tmm)
    plan.got[f"{tag}_d_wu"] = plan.run(f"{tag}_up_dw", matmul, [(d_u, h)], "tn", out_dtype=BF16, a_ch=True, out_ch=True, tm=tmm)
    d_x, d_shift, d_scale, d_nw = plan.run(
        f"{tag}_norm_bwd", rowwise_bwd, fn_norm_mod, [row(xin)], [shift, scale], [norm_w], [[row(d_h)]], [(dx_rows, F32, dx_limit)],
        tm=tm, n_tiles=n_tiles, seg_fn=seg_fn, first_fn=first_fn, adds={0: (row(d_xo), None)})
    return d_x, (d_shift, d_scale, d_gate), d_nw


def kernel(x, c, ctx, c_ctx, w_mod, b_mod, norm_ffn1, ffn1_gate, ffn1_up, ffn1_down, norm_mix, w_in, ssm_conv_w, ssm_conv_b, dt_bias_fwd, dt_bias_bwd, a_log_fwd, a_log_bwd, ssm_d, ssm_norm_w, cconv_w, cconv_b, cconv_ln_w, cconv_ln_b, w_out, norm_ffn2, ffn2_gate, ffn2_up, ffn2_down, final_norm, loss_target, m_c_ctx, m_w_mod, m_b_mod, m_norm_ffn1, m_ffn1_gate, m_ffn1_up, m_ffn1_down, m_norm_mix, m_w_in, m_ssm_conv_w, m_ssm_conv_b, m_dt_bias_fwd, m_dt_bias_bwd, m_a_log_fwd, m_a_log_bwd, m_ssm_d, m_ssm_norm_w, m_cconv_w, m_cconv_b, m_cconv_ln_w, m_cconv_ln_b, m_w_out, m_norm_ffn2, m_ffn2_gate, m_ffn2_up, m_ffn2_down, m_final_norm, v_c_ctx, v_w_mod, v_b_mod, v_norm_ffn1, v_ffn1_gate, v_ffn1_up, v_ffn1_down, v_norm_mix, v_w_in, v_ssm_conv_w, v_ssm_conv_b, v_dt_bias_fwd, v_dt_bias_bwd, v_a_log_fwd, v_a_log_bwd, v_ssm_d, v_ssm_norm_w, v_cconv_w, v_cconv_b, v_cconv_ln_w, v_cconv_ln_b, v_w_out, v_norm_ffn2, v_ffn2_gate, v_ffn2_up, v_ffn2_down, v_final_norm):
    weights = dict(c_ctx=c_ctx, w_mod=w_mod, b_mod=b_mod, norm_ffn1=norm_ffn1, ffn1_gate=ffn1_gate, ffn1_up=ffn1_up, ffn1_down=ffn1_down, norm_mix=norm_mix, w_in=w_in, ssm_conv_w=ssm_conv_w, ssm_conv_b=ssm_conv_b, dt_bias_fwd=dt_bias_fwd, dt_bias_bwd=dt_bias_bwd, a_log_fwd=a_log_fwd, a_log_bwd=a_log_bwd, ssm_d=ssm_d, ssm_norm_w=ssm_norm_w, cconv_w=cconv_w, cconv_b=cconv_b, cconv_ln_w=cconv_ln_w, cconv_ln_b=cconv_ln_b, w_out=w_out, norm_ffn2=norm_ffn2, ffn2_gate=ffn2_gate, ffn2_up=ffn2_up, ffn2_down=ffn2_down, final_norm=final_norm)
    mom1 = dict(c_ctx=m_c_ctx, w_mod=m_w_mod, b_mod=m_b_mod, norm_ffn1=m_norm_ffn1, ffn1_gate=m_ffn1_gate, ffn1_up=m_ffn1_up, ffn1_down=m_ffn1_down, norm_mix=m_norm_mix, w_in=m_w_in, ssm_conv_w=m_ssm_conv_w, ssm_conv_b=m_ssm_conv_b, dt_bias_fwd=m_dt_bias_fwd, dt_bias_bwd=m_dt_bias_bwd, a_log_fwd=m_a_log_fwd, a_log_bwd=m_a_log_bwd, ssm_d=m_ssm_d, ssm_norm_w=m_ssm_norm_w, cconv_w=m_cconv_w, cconv_b=m_cconv_b, cconv_ln_w=m_cconv_ln_w, cconv_ln_b=m_cconv_ln_b, w_out=m_w_out, norm_ffn2=m_norm_ffn2, ffn2_gate=m_ffn2_gate, ffn2_up=m_ffn2_up, ffn2_down=m_ffn2_down, final_norm=m_final_norm)
    mom2 = dict(c_ctx=v_c_ctx, w_mod=v_w_mod, b_mod=v_b_mod, norm_ffn1=v_norm_ffn1, ffn1_gate=v_ffn1_gate, ffn1_up=v_ffn1_up, ffn1_down=v_ffn1_down, norm_mix=v_norm_mix, w_in=v_w_in, ssm_conv_w=v_ssm_conv_w, ssm_conv_b=v_ssm_conv_b, dt_bias_fwd=v_dt_bias_fwd, dt_bias_bwd=v_dt_bias_bwd, a_log_fwd=v_a_log_fwd, a_log_bwd=v_a_log_bwd, ssm_d=v_ssm_d, ssm_norm_w=v_ssm_norm_w, cconv_w=v_cconv_w, cconv_b=v_cconv_b, cconv_ln_w=v_cconv_ln_w, cconv_ln_b=v_cconv_ln_b, w_out=v_w_out, norm_ffn2=v_norm_ffn2, ffn2_gate=v_ffn2_gate, ffn2_up=v_ffn2_up, ffn2_down=v_ffn2_down, final_norm=v_final_norm)
    order = list(weights)

    n_ex, seq_len, d = x.shape
    ctx_len = ctx.shape[1]
    ds = d
    n_head = ds // HEAD_DIM
    xw = ds + 4 * N_STATE
    n_lat, n_ctx_rows = n_ex * seq_len, n_ex * ctx_len
    n_tok = n_lat + n_ctx_rows
    tm = math.gcd(math.gcd(512, seq_len), n_ctx_rows)
    seg_all, first_all = _segmenter(tm, seq_len, n_lat)
    lat_tiles = n_lat // tm

    xi, yi, ci = lax.axis_index("x"), lax.axis_index("y"), lax.axis_index("c")
    me, chip = 4 * xi + 2 * yi + ci, 2 * xi + yi

    (c_all,) = exchange("gather_c", [c], "all8")
    n_all = 8 * n_ex
    n_cond = -(-(n_all + 1) // 8) * 8
    cond = jnp.concatenate([c_all.reshape(n_all, d), c_ctx[None, :], jnp.zeros((n_cond - n_all - 1, d), F32)])
    mod_w = w_mod.shape[2]
    b_shard = lax.dynamic_slice(b_mod, (0, chip * mod_w), (1, mod_w))
    (mod_g,) = exchange("gather_mod", [mod_fwd(cond, w_mod[0], b_shard)], "chips")
    mod_full = mod_g.transpose(1, 0, 2).reshape(n_cond, N_CHIPS * mod_w)
    mod_mine = lax.dynamic_slice(mod_full, (me * n_ex, 0), (n_ex, 9 * d)).reshape(n_ex, 9, d)
    mod_ctx = mod_full[n_all].reshape(9, d)
    tabs = [jnp.concatenate([mod_mine[:, j], mod_ctx[j][None]])[:, None, :] for j in range(9)]
    lat = lambda t: t[:n_ex]

    bf = lambda w: w[0].astype(BF16)
    plan = _Plan()
    gather = lambda *ws: (lambda p: Rider(list(ws), "chips"))
    plan.on("ffn1_norm", "wg1", gather(bf(ffn1_gate)))
    plan.on("ffn1_gate", "wu1", gather(bf(ffn1_up)))
    plan.on("ffn1_up", "wd1_wout", gather(bf(ffn1_down), bf(w_out)))
    plan.on("ffn1_down", "win", gather(bf(w_in), ssm_conv_w[0], cconv_w[0]))
    xt = jnp.concatenate([x.reshape(n_lat, d), ctx.reshape(n_ctx_rows, d)])
    x1, saved1 = _ffn_fwd(plan, "ffn1", xt, n_tok, tm, seg_all, tabs[0], tabs[1], tabs[2], norm_ffn1,
                          lambda: plan.got["wg1"][0], lambda: plan.got["wu1"][0], lambda: plan.got["wd1_wout"][0])
    (wg1,), (wu1,), (wd1, wout_g), (win_g, w5_g, w31_g) = (plan.got[k] for k in ("wg1", "wu1", "wd1_wout", "win"))
    unshard_cols = lambda t: t.transpose(1, 0, 2).reshape(t.shape[1], N_CHIPS * t.shape[2])
    win = unshard_cols(win_g)
    o_x, o_dt, o_glu = ds, ds + xw, ds + xw + 2 * n_head
    w_z, w_xbc, w_dt = win[:, :ds], win[:, o_x:o_dt], win[:, o_dt:o_glu]
    w_ga, w_gb = win[:, o_glu:o_glu + d], win[:, o_glu + d:]
    w_dtp = jnp.concatenate([w_dt, jnp.zeros((d, LANES - 2 * n_head), BF16)], axis=1)
    w_cat = jnp.concatenate([w_z, w_ga, w_gb, w_xbc, w_dtp], axis=1)
    cbw = d // 2
    xbc_cb, dt_cb = 3 * d // cbw, (3 * d + xw) // LANES
    wout = wout_g.reshape(2 * d, d)
    wo_y, wo_u = wout[:ds], wout[ds:]
    w5, w31 = unshard_cols(w5_g), unshard_cols(w31_g)
    pad_vec = lambda v: jnp.concatenate([v.reshape(1, -1), jnp.zeros((1, LANES - v.size), F32)], axis=1)
    dtb_f, dtb_b, alog_f, alog_b = map(pad_vec, (dt_bias_fwd, dt_bias_bwd, a_log_fwd, a_log_bwd))
    dsk_f, dsk_b = pad_vec(ssm_d), jnp.zeros((1, LANES), F32)

    (h2,) = rowwise("mix_norm", fn_norm_mod, [row(x1)], [tabs[3], tabs[4]], [norm_mix], [(n_tok, d, BF16)],
                    tm=tm, n_tiles=n_tok // tm, seg_fn=seg_all)
    proj, (wg2,) = matmul("mix_proj", [(h2, w_cat)], "nn", tm=min(tm, 256), rider=Rider([bf(ffn2_gate)], "chips"))
    def conv5(name, src, cb0, flip):
        out = None
        for part, seq, off in (("lat", seq_len, 0), ("ctx", ctx_len, n_lat // ctx_len)):
            out = tapsum_roll(f"{name}_{part}", src, cb0, w5, 0, seq_len=seq, n_seq=n_ex, row_blk_off=off, width=seq,
                              piece=seq, cb=cbw, ncb=xw // cbw, pad=w5.shape[0] // 2, flip=flip,
                              place=((n_tok, xw), off, 0, out))
        return out

    craw = conv5("xbc_conv", proj, xbc_cb, False)
    (xbc,) = rowwise("xbc_silu", fn_silu_bias, [row(craw)], [], [ssm_conv_b], [(n_tok, xw, F32)], tm=tm, n_tiles=n_tok // tm)
    ssd = dict(n_ex=n_ex, seq_len=seq_len, ctx_len=ctx_len, ds=ds)
    (y_f, hs_f), (wu2, wd2) = ssd_fwd("ssd_fwd_f", xbc, proj, dt_cb, dtb_f, alog_f, dsk_f, rev=False,
                                      rider=Rider([bf(ffn2_up), bf(ffn2_down)], "chips"), **ssd)
    y_b, hs_b = ssd_fwd("ssd_fwd_b", xbc, proj, dt_cb, dtb_b, alog_b, dsk_b, rev=True, **ssd)
    fn_gate = make_fn_gate_groupnorm(ds)
    (yn,) = rowwise("ssd_gate", fn_gate, [row(y_f), row(y_b), row(proj, d, 0)], [], [ssm_norm_w], [(n_lat, ds, BF16)],
                    tm=tm, n_tiles=lat_tiles)
    (u0,) = rowwise("glu", fn_glu, [row(proj, d, 1), row(proj, d, 2)], [], [], [(n_lat, d, F32)], tm=tm, n_tiles=lat_tiles)
    cb31 = max(LANES, d // 4)
    ncb31 = (d // 2) // cb31
    pad31 = w31.shape[0] // 2
    piece31 = min(seq_len, 4 * GRID_W)
    v_w = tapsum_roll("cconv_cols", u0, 0, w31, 0, seq_len=seq_len, n_seq=n_ex, row_blk_off=0, width=GRID_W,
                      piece=piece31, cb=cb31, ncb=ncb31, pad=pad31, flip=False)
    v_h = tapsum_rows("cconv_rows", u0, ncb31, w31, ncb31, seq_len=seq_len, n_seq=n_ex, cb=cb31, ncb=ncb31, pad=pad31, flip=False)
    (un,) = rowwise("cconv_ln", fn_ln_silu, [row(v_w), row(v_h)], [], [cconv_b, cconv_ln_w, cconv_ln_b], [(n_lat, d, BF16)],
                    tm=tm, n_tiles=lat_tiles)
    mix = matmul("mix_out", [(yn, wo_y), (un, wo_u)], "nn", tm=tm)
    seg_lat, first_lat = _segmenter(tm, seq_len, n_lat)
    (x2,) = rowwise("mix_resid", make_fn_resid(1.0), [row(x1), row(mix)], [lat(tabs[5])], [], [(n_lat, d, F32)],
                    tm=tm, n_tiles=lat_tiles, seg_fn=seg_lat)
    x3, saved2 = _ffn_fwd(plan, "ffn2", x2, n_lat, tm, seg_lat, lat(tabs[6]), lat(tabs[7]), lat(tabs[8]), norm_ffn2, wg2, wu2, wd2)
    d_x3, d_final, loss_vec = final_loss(x3, loss_target.reshape(n_lat, d), final_norm.reshape(1, d), tm=tm)

    shard_cols = lambda t: t.reshape(t.shape[0], N_CHIPS, -1).transpose(1, 0, 2)

    def pieces(t):
        t = jnp.pad(t, ((0, 0), (0, -t.shape[1] % 32), (0, 0)))
        return t.reshape(2 * N_CHIPS, t.shape[1] // 2, t.shape[2]).astype(BF16)

    scatter = lambda *ts: Rider([pieces(t) for t in ts], "all8", scatter=True)
    halves = lambda names, landed: Rider([sum_slots(f"sum_{nm}", r) for nm, r in zip(names, landed)], "sibling")
    swapped = {}
    plan.on("ffn2_up_dx", "sc_ffn2_down", lambda p: scatter(p.got["ffn2_d_wd"]))
    plan.on("ffn2_up_dw", "sc_ffn2_gate", lambda p: scatter(p.got["ffn2_d_wg"]))
    d_x2, (d_s6, d_s7, d_g8), d_nffn2 = _ffn_bwd(
        plan, "ffn2", d_x3, saved2, x2, n_lat, tm, seg_lat, first_lat, lat(tabs[6]), lat(tabs[7]), lat(tabs[8]), norm_ffn2,
        wg2, wu2, wd2, n_lat, None)
    d_mix, d_g5 = rowwise_bwd("mix_resid_bwd", make_fn_resid(1.0), [row(x1), row(mix)], [lat(tabs[5])], [], [[row(d_x2)]],
                              [None, (n_lat, BF16, None)], tm=tm, n_tiles=lat_tiles, seg_fn=seg_lat, first_fn=first_lat)
    d_yn = matmul("mix_out_dy", [(d_mix, wo_y)], "nt", tm=tm)
    d_un = matmul("mix_out_du", [(d_mix, wo_u)], "nt", tm=tm)
    d_wout = jnp.concatenate([matmul("mix_out_dwy", [(yn, d_mix)], "tn", out_dtype=BF16, tm=tm), matmul("mix_out_dwu", [(un, d_mix)], "tn", out_dtype=BF16, tm=tm)])
    d_vw, d_vh, d_cb, d_lnw, d_lnb = rowwise_bwd(
        "cconv_ln_bwd", fn_ln_silu, [row(v_w), row(v_h)], [], [cconv_b, cconv_ln_w, cconv_ln_b], [[row(d_un)]],
        [(n_lat, F32, None)] * 2, tm=tm, n_tiles=lat_tiles)
    d_u0 = tapsum_roll("cconv_cols_dx", d_vw, 0, w31, 0, seq_len=seq_len, n_seq=n_ex, row_blk_off=0, width=GRID_W,
                       piece=piece31, cb=cb31, ncb=ncb31, pad=pad31, flip=True, place=((n_lat, d), 0, 0, None))
    d_u0 = tapsum_rows("cconv_rows_dx", d_vh, 0, w31, ncb31, seq_len=seq_len, n_seq=n_ex, cb=cb31, ncb=ncb31, pad=pad31,
                       flip=True, place=((n_lat, d), 0, ncb31, d_u0))
    d_w31 = jnp.concatenate([
        tapgrad_roll("cconv_cols_dw", d_vw, 0, 0, u0, 0, 0, n_tap=w31.shape[0], seq_len=seq_len, n_seq=n_ex, width=GRID_W,
                     piece=piece31, cb=cb31, ncb=ncb31, pad=pad31),
        tapgrad_rows("cconv_rows_dw", d_vh, 0, u0, ncb31, n_tap=w31.shape[0], seq_len=seq_len, n_seq=n_ex, cb=cb31,
                     ncb=ncb31, pad=pad31)], axis=1)
    d_ga, d_gb = rowwise_bwd("glu_bwd", fn_glu, [row(proj, d, 1), row(proj, d, 2)], [], [], [[row(d_u0)]],
                             [(n_lat, BF16, None)] * 2, tm=tm, n_tiles=lat_tiles)
    d_ysum, d_z, d_ssmnw = rowwise_bwd(
        "ssd_gate_bwd", fn_gate, [row(y_f), row(y_b), row(proj, d, 0)], [], [ssm_norm_w], [[row(d_yn)]],
        [(n_lat, F32, None), None, (n_lat, BF16, None)], tm=tm, n_tiles=lat_tiles)
    (dxbc_f, ddt_f, dalog_f, ddtb_f, ddsk), landed = ssd_bwd(
        "ssd_bwd_f", xbc, proj, dt_cb, hs_f, d_ysum, dtb_f, alog_f, dsk_f, rev=False,
        rider=scatter(plan.got["ffn2_d_wu"], d_wout.reshape(N_CHIPS, -1, d)), **ssd)
    (dxbc_b, ddt_b, dalog_b, ddtb_b, _), both = ssd_bwd(
        "ssd_bwd_b", xbc, proj, dt_cb, hs_b, d_ysum, dtb_b, alog_b, dsk_b, rev=True,
        rider=halves(["ffn2_down", "ffn2_gate"], plan.got["sc_ffn2_down"] + plan.got["sc_ffn2_gate"]), **ssd)
    swapped.update(zip(["ffn2_down", "ffn2_gate"], both))
    (d_craw, d_conv_b), both = rowwise_bwd(
        "xbc_silu_bwd", fn_silu_bias, [row(craw)], [], [ssm_conv_b], [[row(dxbc_f), row(dxbc_b)]],
        [(n_tok, F32, None)], tm=tm, n_tiles=n_tok // tm, rider=halves(["ffn2_up", "w_out"], landed))
    swapped.update(zip(["ffn2_up", "w_out"], both))
    d_pxbc = conv5("xbc_conv_dx", d_craw, 0, True)
    g5 = lambda name, seq, off: tapgrad_roll(name, d_craw, 0, off, proj, xbc_cb, off, n_tap=w5.shape[0], seq_len=seq,
                                             n_seq=n_ex, width=seq, piece=seq, cb=cbw, ncb=xw // cbw, pad=w5.shape[0] // 2)
    d_w5 = g5("xbc_conv_lat_dw", seq_len, 0) + g5("xbc_conv_ctx_dw", ctx_len, n_lat // ctx_len)
    lat_pairs = [(d_z, w_z), (d_ga, w_ga), (d_gb, w_gb), (d_pxbc, w_xbc), (ddt_f, w_dtp), (ddt_b, w_dtp)]
    d_h2 = jnp.concatenate([matmul("mix_proj_dx_lat", lat_pairs, "nt", rows=n_lat, tm=min(tm, 256)),
                            matmul("mix_proj_dx_ctx", lat_pairs[3:], "nt", rows=n_ctx_rows, row_off=n_lat, tm=min(tm, 256))])
    d_wz = matmul("mix_proj_dwz", [(d_z, h2)], "tn", out_dtype=BF16, rows=n_lat, tm=tm)
    d_wga = matmul("mix_proj_dwa", [(d_ga, h2)], "tn", out_dtype=BF16, rows=n_lat, tm=tm)
    d_wgb = matmul("mix_proj_dwb", [(d_gb, h2)], "tn", out_dtype=BF16, rows=n_lat, tm=tm)
    d_wxbc = matmul("mix_proj_dwx", [(d_pxbc, h2)], "tn", out_dtype=BF16, tm=tm)
    d_wdt = matmul("mix_proj_dwt", [(ddt_f, h2), (ddt_b, h2)], "tn", out_dtype=BF16, tm=tm)
    d_win_t = jnp.concatenate([d_wz, d_wxbc, d_wdt[:2 * n_head], d_wga, d_wgb]).reshape(N_CHIPS, -1, d)
    d_x1, d_s3, d_s4, d_nmix = rowwise_bwd(
        "mix_norm_bwd", fn_norm_mod, [row(x1)], [tabs[3], tabs[4]], [norm_mix], [[row(d_h2)]], [(n_tok, F32, None)],
        tm=tm, n_tiles=n_tok // tm, seg_fn=seg_all, first_fn=first_all, adds={0: (row(d_x2), lat_tiles)})
    mix_names = ["w_in", "ssm_conv_w", "cconv_w"]
    plan.on("ffn1_down_dx", "sc_mix", lambda p: scatter(d_win_t, shard_cols(d_w5), shard_cols(d_w31)))
    plan.on("ffn1_up_dx", "sc_ffn1_down", lambda p: scatter(p.got["ffn1_d_wd"]))
    plan.on("ffn1_gate_dw", "sw_mix", lambda p: halves(mix_names, p.got["sc_mix"]))
    plan.on("ffn1_up_dw", "sc_ffn1_gate", lambda p: scatter(p.got["ffn1_d_wg"]))
    plan.on("ffn1_up_dw", "sw_ffn1_down", lambda p: halves(["ffn1_down"], p.got["sc_ffn1_down"]))
    plan.on("ffn1_norm_bwd", "sc_ffn1_up", lambda p: scatter(p.got["ffn1_d_wu"]))
    d_xt, (d_s0, d_s1, d_g2), d_nffn1 = _ffn_bwd(
        plan, "ffn1", d_x1, saved1, xt, n_tok, tm, seg_all, first_all, tabs[0], tabs[1], tabs[2], norm_ffn1, wg1, wu1, wd1,
        n_lat, lat_tiles)
    swapped.update(zip(mix_names + ["ffn1_down"], plan.got["sw_mix"] + plan.got["sw_ffn1_down"]))
    last_names = ["ffn1_gate", "ffn1_up"]
    last = halves(last_names, plan.got["sc_ffn1_gate"] + plan.got["sc_ffn1_up"])
    swapped.update(zip(last_names, exchange("swap_sibling", last.arrs, "sibling")))
    grad_x = d_xt.reshape(n_ex, seq_len, d)

    with_ctx0 = lambda t: jnp.concatenate([t, jnp.zeros((1, 1, d), F32)])
    d_tabs = [d_s0, d_s1, d_g2, d_s3, d_s4, with_ctx0(d_g5), with_ctx0(d_s6), with_ctx0(d_s7), with_ctx0(d_g8)]
    d_mod_rows = jnp.concatenate([t[:, 0, :] for t in d_tabs], axis=1)
    n_pad_rows = -(-(n_ex + 1) // 8) * 8
    d_mod_rows = jnp.concatenate([d_mod_rows, jnp.zeros((n_pad_rows - n_ex - 1, 9 * d), F32)])
    small = [("loss", loss_vec), ("norm_ffn1", d_nffn1), ("norm_mix", d_nmix), ("ssm_conv_b", d_conv_b),
             ("dt_bias_fwd", ddtb_f[:, :n_head]), ("dt_bias_bwd", ddtb_b[:, :n_head]), ("a_log_fwd", dalog_f[:, :n_head]),
             ("a_log_bwd", dalog_b[:, :n_head]), ("ssm_d", ddsk[:, :n_head]), ("ssm_norm_w", d_ssmnw), ("cconv_b", d_cb),
             ("cconv_ln_w", d_lnw), ("cconv_ln_b", d_lnb), ("norm_ffn2", d_nffn2), ("final_norm", d_final)]
    n_small = sum(v.size for _, v in small)
    n_pack = -(-n_small // (8 * LANES)) * (8 * LANES)
    pack = jnp.concatenate([v.reshape(-1) for _, v in small] + [jnp.zeros((n_pack - n_small,), F32)]).reshape(-1, LANES)
    pack_all, d_mod_all = exchange("gather_small", [pack, d_mod_rows], "all8")
    pack_sum = sum_slots("small_sum", pack_all)
    loss = loss_total(pack_sum.reshape(1, n_pack), d).reshape(())
    flat_sum = pack_sum.reshape(-1)
    small_grads, pos = {}, 0
    for nm, v in small:
        small_grads[nm] = flat_sum[pos:pos + v.size]
        pos += v.size
    d_mod_all = d_mod_all.reshape(8 * n_pad_rows, 9 * d)
    cond_rows = [jnp.concatenate([cond[j * n_ex:(j + 1) * n_ex], c_ctx[None, :],
                                  jnp.zeros((n_pad_rows - n_ex - 1, d), F32)]) for j in range(8)]
    cond_bwd = jnp.concatenate(cond_rows)
    d_mod_shard = lax.dynamic_slice(d_mod_all, (0, chip * mod_w), (8 * n_pad_rows, mod_w))
    g_wmod, g_bmod, q_part = mod_bwd(cond_bwd, d_mod_shard, d_mod_all, w_mod[0],
                                     tuple(j * n_pad_rows + n_ex for j in range(8)))
    (q_all,) = exchange("gather_cctx", [q_part], "all8")
    g_cctx = cctx_grad(q_all, c_ctx.reshape(1, d))
    small_grads["c_ctx"], small_grads["b_mod"] = g_cctx.reshape(-1), g_bmod.reshape(-1)

    transposed = {"ffn1_gate", "ffn1_up", "ffn2_gate", "ffn2_up", "w_in"}
    results = {}
    for nm, both in swapped.items():
        flip = (lambda t: jnp.swapaxes(t, 1, 2)) if nm in transposed else (lambda t: t)
        shape = flip(weights[nm]).shape
        two_d = lambda t: flip(t).reshape(shape[-2], shape[-1])
        g_full = both.reshape(1, -1, shape[-1])[:, :shape[-2]]
        results[nm] = [flip(r.reshape(shape)) for r in
                       adamw(f"adamw_{nm}", two_d(weights[nm]), g_full, two_d(mom1[nm]), two_d(mom2[nm]))]
    results["w_mod"] = [r.reshape(w_mod.shape) for r in adamw("adamw_w_mod", w_mod[0], g_wmod[None], m_w_mod[0], v_w_mod[0])]
    small_names = [nm for nm in order if nm not in results]
    n_sm = sum(weights[nm].size for nm in small_names)
    n_smp = -(-n_sm // (8 * LANES)) * (8 * LANES)
    packed = lambda src: jnp.concatenate([src[nm].reshape(-1) for nm in small_names] + [jnp.zeros((n_smp - n_sm,), F32)]).reshape(-1, LANES)
    sm_out = adamw("adamw_small", packed(weights), packed(small_grads)[None], packed(mom1), packed(mom2))
    pos = 0
    for nm in small_names:
        size = weights[nm].size
        results[nm] = [r.reshape(-1)[pos:pos + size].reshape(weights[nm].shape) for r in sm_out]
        pos += size
    return (loss, grad_x, *[results[nm][0] for nm in order], *[results[nm][1] for nm in order],
            *[results[nm][2] for nm in order], *[results[nm][3] for nm in order])
```

```python
import functools
import math

import jax
import jax.numpy as jnp
from jax import lax
from jax.experimental import pallas as pl
from jax.experimental.pallas import tpu as pltpu

F32 = jnp.float32
BF16 = jnp.bfloat16
HI = lax.Precision.HIGHEST
MESH = pl.DeviceIdType.MESH

EPS = 1e-6
GRID_W = 64
HEAD_DIM = 64
N_STATE = 128
CHUNK = 128
LANES = 128
N_CHIPS = 4
ADAM_LR, ADAM_B1, ADAM_B2, ADAM_EPS, ADAM_WD, ADAM_STEP = 0.001, 0.9, 0.999, 1e-08, 0.01, 10
VMEM_CAP = 56 * 1024 * 1024


def _params(vmem_bytes=None, n_axes=1):
    kw = dict(dimension_semantics=("arbitrary",) * n_axes)
    if vmem_bytes is not None:
        kw["vmem_limit_bytes"] = int(min(VMEM_CAP, max(32 * 1024 * 1024, vmem_bytes)))
    return pltpu.CompilerParams(**kw)


def _big(shape, dtype):
    return pltpu.HBM(tuple(shape), dtype)


def _nbytes(shape, dtype):
    return math.prod(shape) * jnp.dtype(dtype).itemsize


def _row_tile(rows, width, cap_bytes=1 << 20, mult=8):
    best = None
    for t in range(mult, rows + 1, mult):
        if rows % t == 0 and t * width * 4 <= cap_bytes:
            best = t
    return best if best is not None else rows


_MODES = {"all8": (8, (1, 2, 3, 4, 5, 6, 7), 0), "chips": (4, (2, 4, 6), 1), "sibling": (2, (1,), 0)}


class Rider:
    def __init__(self, arrs, mode, scatter=False):
        self.arrs, self.scatter = list(arrs), scatter
        self.nslot, self.deltas, self.shift = _MODES[mode]
        self.n = len(self.arrs)
        self.out_shape = [jax.ShapeDtypeStruct((self.nslot,) + (a.shape[1:] if scatter else a.shape), a.dtype)
                          for a in self.arrs]
        any_spec = pl.BlockSpec(memory_space=pl.ANY)
        self.in_specs = [any_spec] * self.n
        self.out_specs = [any_spec] * self.n
        n_peer = len(self.deltas)
        self.scratch = [pltpu.SemaphoreType.DMA((self.n, n_peer)), pltpu.SemaphoreType.DMA((self.n, n_peer)),
                        pltpu.SemaphoreType.DMA((self.n,))]

    def _copies(self, ins, outs, sems, arrivals):
        send_sems, recv_sems, local_sems = sems
        x, y, c = lax.axis_index("x"), lax.axis_index("y"), lax.axis_index("c")
        me = 4 * x + 2 * y + c
        slot_of = lambda dev: (dev >> self.shift) & (self.nslot - 1)
        src = lambda a, slot: ins[a].at[slot] if self.scatter else ins[a]
        flip = lambda v, bit: 1 - v if bit else v

        def remote(a, k, d, from_slot, to_slot):
            return pltpu.make_async_remote_copy(
                src_ref=src(a, from_slot), dst_ref=outs[a].at[to_slot], send_sem=send_sems.at[a, k],
                recv_sem=recv_sems.at[a, k], device_id=(flip(x, (d >> 2) & 1), flip(y, (d >> 1) & 1), flip(c, d & 1)),
                device_id_type=MESH)

        mine = slot_of(me)
        local = [pltpu.make_async_copy(src(a, mine), outs[a].at[mine], local_sems.at[a]) for a in range(self.n)]
        sends = [remote(a, k, d, slot_of(me ^ d), mine) for k, d in enumerate(self.deltas) for a in range(self.n)]
        if not arrivals:
            return local, sends
        return local, sends, [remote(a, k, d, mine, slot_of(me ^ d)) for k, d in enumerate(self.deltas) for a in range(self.n)]

    def start(self, ins, outs, sems):
        local, sends = self._copies(ins, outs, sems, arrivals=False)
        for cp in local + sends:
            cp.start()

    def wait(self, ins, outs, sems):
        local, sends, recvs = self._copies(ins, outs, sems, arrivals=True)
        for cp in recvs:
            cp.wait_recv()
        for cp in sends:
            cp.wait_send()
        for cp in local:
            cp.wait()


class Riders:
    def __init__(self, riders):
        self.riders = list(riders)
        self.n = sum(r.n for r in self.riders)
        cat = lambda attr: [v for r in self.riders for v in getattr(r, attr)]
        self.arrs, self.out_shape, self.in_specs = cat("arrs"), cat("out_shape"), cat("in_specs")
        self.out_specs, self.scratch = cat("out_specs"), cat("scratch")

    def _each(self, method, ins, outs, sems):
        i = s = 0
        for r in self.riders:
            getattr(r, method)(ins[i:i + r.n], outs[i:i + r.n], sems[s:s + len(r.scratch)])
            i, s = i + r.n, s + len(r.scratch)

    def start(self, ins, outs, sems):
        self._each("start", ins, outs, sems)

    def wait(self, ins, outs, sems):
        self._each("wait", ins, outs, sems)


class _Hosted:
    def __init__(self, rider, n_in, n_out, n_scratch, grid):
        self.rider, self.n_in, self.n_out, self.n_scratch, self.grid = rider, n_in, n_out, n_scratch, grid
        self.n = rider.n if rider else 0

    def split(self, refs):
        a, b = self.n_in, self.n_in + self.n
        c, e = b + self.n_out, b + self.n_out + self.n
        self._r = (refs[a:b], refs[c:e], refs[e + self.n_scratch:])
        if self.rider:
            ids = [pl.program_id(ax) for ax in range(len(self.grid))]
            first = functools.reduce(jnp.logical_and, [i == 0 for i in ids]) if ids else True
            pl.when(first)(lambda: self.rider.start(*self._r))
        return refs[:a], refs[b:c], refs[e:e + self.n_scratch]

    def finish(self):
        if self.rider:
            ids = [pl.program_id(ax) for ax in range(len(self.grid))]
            last = functools.reduce(jnp.logical_and, [i == n - 1 for i, n in zip(ids, self.grid)]) if ids else True
            pl.when(last)(lambda: self.rider.wait(*self._r))

    def call_args(self, in_specs, out_shape, out_specs, scratch, args):
        r = self.rider
        if not r:
            return list(in_specs), tuple(out_shape), tuple(out_specs), list(scratch), list(args)
        return (list(in_specs) + r.in_specs, tuple(out_shape) + tuple(r.out_shape), tuple(out_specs) + tuple(r.out_specs),
                list(scratch) + r.scratch, list(args) + r.arrs)

    def results(self, res, unwrap=True):
        res = list(res) if isinstance(res, (tuple, list)) else [res]
        host = res[:self.n_out]
        host = host[0] if (self.n_out == 1 and unwrap) else tuple(host)
        return (host, res[self.n_out:]) if self.rider else host


def exchange(name, arrs, mode, scatter=False):
    rider = Rider(arrs, mode, scatter)

    def body(*refs):
        ins, outs, sems = refs[:rider.n], refs[rider.n:2 * rider.n], refs[2 * rider.n:]
        rider.start(ins, outs, sems)
        rider.wait(ins, outs, sems)

    return pl.pallas_call(
        body, name=name, out_shape=tuple(rider.out_shape), in_specs=rider.in_specs, out_specs=tuple(rider.out_specs),
        scratch_shapes=rider.scratch,
    )(*arrs)


_DIMS = {"nn": (((1,), (0,)), ((), ())), "nt": (((1,), (1,)), ((), ())), "tn": (((0,), (0,)), ((), ()))}


def matmul(name, pairs, kind, *, a_ch=False, b_ch=False, out_ch=False, out_dtype=F32, rows=None, row_off=0, tm=512,
           rider=None, post=None, fold=False):
    a0, b0 = pairs[0]
    n_chunk = a0.shape[0] if a_ch else (b0.shape[0] if b_ch else 1)
    total_rows = a0.shape[-2]
    rows = total_rows - row_off if rows is None else rows
    tm = min(tm, rows)
    assert rows % tm == 0 and row_off % tm == 0, (name, rows, tm, row_off)
    n_rt, off = rows // tm, row_off // tm
    dims = _DIMS[kind]
    n_pair = len(pairs)

    if kind == "tn":
        grid, red_axis, n_red = (n_chunk, n_rt), 1, n_rt
        a_idx = (lambda k, i: (k, i + off, 0)) if a_ch else (lambda k, i: (i + off, 0))
        b_idx = (lambda k, i: (k, i + off, 0)) if b_ch else (lambda k, i: (i + off, 0))
        a_blk = lambda a: ((None, tm, a.shape[-1]) if a_ch else (tm, a.shape[-1]))
        b_blk = lambda b: ((None, tm, b.shape[-1]) if b_ch else (tm, b.shape[-1]))
        o2 = (a0.shape[-1], b0.shape[-1])
        out_shape = ((n_chunk,) + o2) if out_ch else o2
        out_spec = pl.BlockSpec((None,) + o2, lambda k, i: (k, 0, 0)) if out_ch else pl.BlockSpec(o2, lambda k, i: (0, 0))
        acc_shape = o2
    else:
        n_out = b0.shape[-1] if kind == "nn" else b0.shape[-2]
        b2 = b0.shape[-2:]
        if a_ch and b_ch and not out_ch and fold:
            grid, red_axis, n_red = (n_rt,), None, 1
            a_idx, b_idx = (lambda i: (0, i + off, 0)), (lambda i: (0, 0, 0))
            a_blk = lambda a: (n_chunk, tm, a.shape[-1])
            b_blk = lambda b: tuple(b.shape)
            out_shape, out_spec = (rows, n_out), pl.BlockSpec((tm, n_out), lambda i: (i, 0))
        elif a_ch and b_ch and not out_ch:
            grid, red_axis, n_red = (n_rt, n_chunk), 1, n_chunk
            a_idx, b_idx = (lambda i, k: (k, i + off, 0)), (lambda i, k: (k, 0, 0))
            a_blk = lambda a: (None, tm, a.shape[-1])
            b_blk = lambda b: (None,) + tuple(b.shape[-2:])
            out_shape, out_spec = (rows, n_out), pl.BlockSpec((tm, n_out), lambda i, k: (i, 0))
        elif out_ch:
            assert b_ch and not a_ch
            grid, red_axis, n_red = (n_chunk, n_rt), None, 1
            a_idx, b_idx = (lambda k, i: (i + off, 0)), (lambda k, i: (k, 0, 0))
            a_blk = lambda a: (tm, a.shape[-1])
            b_blk = lambda b: (None,) + tuple(b.shape[-2:])
            out_shape, out_spec = (n_chunk, rows, n_out), pl.BlockSpec((None, tm, n_out), lambda k, i: (k, i, 0))
        else:
            assert not (a_ch or b_ch)
            grid, red_axis, n_red = (n_rt,), None, 1
            a_idx, b_idx = (lambda i: (i + off, 0)), (lambda i: (0, 0))
            a_blk = lambda a: (tm, a.shape[-1])
            b_blk = lambda b: tuple(b.shape)
            out_shape, out_spec = (rows, n_out), pl.BlockSpec((tm, n_out), lambda i: (i, 0))
        acc_shape = (tm, n_out)

    post_ins, post_fn, out_dtypes = ([], None, [out_dtype]) if post is None else post
    hosted = _Hosted(rider, 2 * n_pair + len(post_ins), len(out_dtypes), int(n_red > 1), grid)

    def body(*refs):
        ins, outs, scr = hosted.split(refs)

        def compute():
            acc = None
            for p in range(n_pair):
                for k in ([None] if not fold else range(n_chunk)):
                    pick = (lambda r: r[...]) if k is None else (lambda r: r[k])
                    d = lax.dot_general(pick(ins[2 * p]).astype(BF16), pick(ins[2 * p + 1]).astype(BF16), dims,
                                        preferred_element_type=F32)
                    acc = d if acc is None else acc + d
            return acc

        def emit(acc):
            vals = (acc,) if post_fn is None else post_fn(acc, *[r[...].astype(F32) for r in ins[2 * n_pair:]])
            for o_ref, v in zip(outs, vals):
                o_ref[...] = v.astype(o_ref.dtype)

        if n_red == 1:
            emit(compute())
        else:
            acc_ref = scr[0]
            r = pl.program_id(red_axis)

            @pl.when(r == 0)
            def _():
                acc_ref[...] = jnp.zeros_like(acc_ref)

            acc_ref[...] += compute()

            @pl.when(r == n_red - 1)
            def _():
                emit(acc_ref[...])
        hosted.finish()

    in_specs, args, vmem = [], [], 0
    for a, b in pairs:
        in_specs += [pl.BlockSpec(a_blk(a), a_idx), pl.BlockSpec(b_blk(b), b_idx)]
        args += [a, b]
        vmem += 2 * (_nbytes([s for s in a_blk(a) if s], a.dtype) + _nbytes([s for s in b_blk(b) if s], b.dtype))
    in_specs += [out_spec] * len(post_ins)
    args += list(post_ins)
    vmem += (3 + 2 * n_pair + 2 * len(post_ins) + 2 * len(out_dtypes)) * _nbytes(acc_shape, F32)
    scratch = [pltpu.VMEM(acc_shape, F32)] if n_red > 1 else []
    in_specs, out_shapes, out_specs, scratch, args = hosted.call_args(
        in_specs, [_big(out_shape, dt) for dt in out_dtypes], [out_spec] * len(out_dtypes), scratch, args)
    return hosted.results(pl.pallas_call(
        body, name=name, out_shape=out_shapes, grid=grid, in_specs=in_specs, out_specs=out_specs,
        scratch_shapes=scratch, compiler_params=_params(vmem + (8 << 20), len(grid)),
    )(*args))


def row(arr, width=None, cb=0, roff=0):
    return (arr, arr.shape[-1] if width is None else width, cb, roff)


def _row_spec(desc, tm, limit=None):
    _, width, cb, roff = desc
    if limit is None:
        return pl.BlockSpec((tm, width), lambda i: (i + roff, cb))
    return pl.BlockSpec((tm, width), lambda i: (jnp.minimum(i, limit - 1) + roff, cb))


def _segmenter(tm, seq_len, n_lat):
    seg = lambda i: jnp.where(i * tm < n_lat, (i * tm) // seq_len, n_lat // seq_len)
    first = lambda i: jnp.where(i * tm < n_lat, (i * tm) % seq_len == 0, i * tm == n_lat)
    return seg, first


def rowwise(name, fn, rows, segs, params, outs, *, tm, n_tiles, seg_fn=None, rider=None):
    n_r, n_s, n_p = len(rows), len(segs), len(params)
    hosted = _Hosted(rider, n_r + n_s + n_p, len(outs), 0, (n_tiles,))

    def body(*refs):
        ins, out_refs, _ = hosted.split(refs)
        vals = [r[...].astype(F32) for r in ins[:n_r]] + [r[...] for r in ins[n_r:]]
        res = fn(*vals)
        for o_ref, v in zip(out_refs, res):
            o_ref[...] = v.astype(o_ref.dtype)
        hosted.finish()

    in_specs = [_row_spec(d, tm) for d in rows]
    in_specs += [pl.BlockSpec((None, 1, s.shape[-1]), lambda i: (seg_fn(i), 0, 0)) for s in segs]
    in_specs += [pl.BlockSpec(p.shape, lambda i: (0, 0)) for p in params]
    vmem = sum(2 * tm * d[1] * 4 for d in rows) + sum(3 * tm * w * 4 for _, w, _ in outs) + sum(2 * p.size * 4 for p in params)
    in_specs, out_shapes, out_specs, scratch, args = hosted.call_args(
        in_specs, [_big((r, w), dt) for r, w, dt in outs],
        [pl.BlockSpec((tm, w), lambda i: (i, 0)) for _, w, _ in outs], [], [d[0] for d in rows] + list(segs) + list(params))
    return hosted.results(pl.pallas_call(
        body, name=name, grid=(n_tiles,), in_specs=in_specs, out_shape=out_shapes, out_specs=out_specs,
        scratch_shapes=scratch, compiler_params=_params(2 * vmem + (8 << 20)),
    )(*args), unwrap=False)


def rowwise_bwd(name, fn, rows, segs, params, cts, row_grads, *, tm, n_tiles, seg_fn=None, first_fn=None, adds=None,
                rider=None):
    adds = adds or {}
    need = [k for k, v in enumerate(row_grads) if v is not None]
    n_r, n_s, n_p = len(rows), len(segs), len(params)
    n_ct = sum(len(lst) for lst in cts)
    add_keys = sorted(adds)
    hosted = _Hosted(rider, n_r + n_s + n_p + n_ct + len(add_keys), len(need) + n_s + n_p, 0, (n_tiles,))

    def body(*refs):
        host_in, host_out, _ = hosted.split(refs)
        it = iter(list(host_in) + list(host_out))
        row_refs = [next(it) for _ in range(n_r)]
        seg_refs = [next(it) for _ in range(n_s)]
        par_refs = [next(it) for _ in range(n_p)]
        ct_refs = [[next(it) for _ in lst] for lst in cts]
        add_refs = {k: next(it) for k in add_keys}
        rg_refs = {k: next(it) for k in need}
        sg_refs = [next(it) for _ in range(n_s)]
        pg_refs = [next(it) for _ in range(n_p)]
        i = pl.program_id(0)
        rv = [r[...].astype(F32) for r in row_refs]
        sv = [r[...] for r in seg_refs]
        pv = [r[...] for r in par_refs]

        def f(*args):
            rr = list(rv)
            for j, k in enumerate(need):
                rr[k] = args[j]
            return fn(*rr, *args[len(need):])

        _, vjp = jax.vjp(f, *[rv[k] for k in need], *sv, *pv)
        ctv = []
        for lst in ct_refs:
            acc = lst[0][...].astype(F32)
            for r in lst[1:]:
                acc = acc + r[...].astype(F32)
            ctv.append(acc)
        g = vjp(tuple(ctv))
        for j, k in enumerate(need):
            gv = g[j]
            if k in adds:
                lim = adds[k][1]
                av = add_refs[k][...].astype(F32)
                gv = gv + (av if lim is None else jnp.where(i < lim, av, 0.0))
            lim = row_grads[k][2]
            if lim is None:
                rg_refs[k][...] = gv.astype(rg_refs[k].dtype)
            else:
                @pl.when(i < lim)
                def _(gv=gv, k=k):
                    rg_refs[k][...] = gv.astype(rg_refs[k].dtype)
        if n_s:
            opens = first_fn(i)
            for ref, gv in zip(sg_refs, g[len(need):len(need) + n_s]):
                @pl.when(opens)
                def _(ref=ref, gv=gv):
                    ref[...] = gv

                @pl.when(jnp.logical_not(opens))
                def _(ref=ref, gv=gv):
                    ref[...] += gv
        for ref, gv in zip(pg_refs, g[len(need) + n_s:]):
            @pl.when(i == 0)
            def _(ref=ref, gv=gv):
                ref[...] = gv

            @pl.when(i > 0)
            def _(ref=ref, gv=gv):
                ref[...] += gv
        hosted.finish()

    seg_spec = lambda s: pl.BlockSpec((None, 1, s.shape[-1]), lambda i: (seg_fn(i), 0, 0))
    par_spec = lambda p: pl.BlockSpec(p.shape, lambda i: (0, 0))
    in_specs = [_row_spec(d, tm) for d in rows] + [seg_spec(s) for s in segs] + [par_spec(p) for p in params]
    args = [d[0] for d in rows] + list(segs) + list(params)
    for lst in cts:
        in_specs += [_row_spec(d, tm) for d in lst]
        args += [d[0] for d in lst]
    for k in add_keys:
        in_specs.append(_row_spec(adds[k][0], tm, adds[k][1]))
        args.append(adds[k][0][0])
    out_shape, out_specs = [], []
    for k in need:
        n_rows, dt, lim = row_grads[k]
        out_shape.append(_big((n_rows, rows[k][1]), dt))
        out_specs.append(_row_spec((None, rows[k][1], 0, 0), tm, lim))
    for s in segs:
        out_shape.append(jax.ShapeDtypeStruct(s.shape, F32))
        out_specs.append(seg_spec(s))
    for p in params:
        out_shape.append(jax.ShapeDtypeStruct(p.shape, F32))
        out_specs.append(par_spec(p))
    vmem = sum(tm * d[1] * 4 for d in rows) * 6 + n_ct * tm * max(d[1] for d in rows) * 8
    in_specs, out_shape, out_specs, scratch, args = hosted.call_args(in_specs, out_shape, out_specs, [], args)
    return hosted.results(pl.pallas_call(
        body, name=name, grid=(n_tiles,), in_specs=in_specs, out_shape=out_shape, out_specs=out_specs,
        scratch_shapes=scratch, compiler_params=_params(vmem + (8 << 20)),
    )(*args), unwrap=False)


def _silu(v):
    return v * jax.nn.sigmoid(v)


def _rms(v, w):
    return v * lax.rsqrt(jnp.mean(v * v, axis=-1, keepdims=True) + EPS) * w


def fn_norm_mod(x, shift, scale, w):
    return (_rms(x, w) * (1.0 + scale) + shift,)


def fn_act(g, u):
    return (_silu(g) * u,)


def make_fn_resid(coef):
    def fn(x, f, gate):
        return (x + coef * gate * f,)
    return fn


def fn_silu_bias(v, b):
    return (_silu(v + b),)


def make_fn_gate_groupnorm(width):
    half = width // 2

    def fn(yf, yb, z, w):
        y = (yf + yb) * _silu(z)
        lane = lax.broadcasted_iota(jnp.int32, y.shape, 1)
        lo = lane < half
        sq = y * y
        s_lo = jnp.sum(jnp.where(lo, sq, 0.0), axis=-1, keepdims=True)
        s_hi = jnp.sum(jnp.where(lo, 0.0, sq), axis=-1, keepdims=True)
        r = jnp.where(lo, lax.rsqrt(s_lo / half + EPS), lax.rsqrt(s_hi / half + EPS))
        return (y * r * w,)
    return fn


def fn_glu(a, b):
    return (a * jax.nn.sigmoid(b),)


def fn_ln_silu(vw, vh, cb, lw, lb):
    v = jnp.concatenate([vw, vh], axis=-1) + cb
    mu = jnp.mean(v, axis=-1, keepdims=True)
    var = jnp.mean(jnp.square(v - mu), axis=-1, keepdims=True)
    return (_silu((v - mu) * lax.rsqrt(var + EPS) * lw + lb),)


def _col_tile(width):
    return width // 3 if width % (3 * LANES) == 0 else width


def mod_fwd(a_rows, w_shard, b_shard):
    n, d = a_rows.shape
    ws = w_shard.shape[1]
    tn = _col_tile(ws)

    def body(a_ref, w_ref, b_ref, o_ref):
        a = _silu(a_ref[...]).astype(BF16)
        o_ref[...] = jnp.dot(a, w_ref[...].astype(BF16), preferred_element_type=F32) + b_ref[...]

    return pl.pallas_call(
        body, name="mod_fwd", grid=(ws // tn,), out_shape=jax.ShapeDtypeStruct((n, ws), F32),
        in_specs=[pl.BlockSpec((n, d), lambda j: (0, 0)), pl.BlockSpec((d, tn), lambda j: (0, j)),
                  pl.BlockSpec((1, tn), lambda j: (0, j))],
        out_specs=pl.BlockSpec((n, tn), lambda j: (0, j)), compiler_params=_params(),
    )(a_rows, w_shard, b_shard)


def mod_bwd(a_rows, d_shard, d_full, w_shard, ctx_rows):
    n, d = a_rows.shape
    ws = w_shard.shape[1]
    tn = _col_tile(ws)
    n_ct = ws // tn

    def body(a_ref, ds_ref, df_ref, w_ref, gw_ref, gb_ref, q_ref):
        j = pl.program_id(0)
        a = _silu(a_ref[...])
        ds = ds_ref[...]
        gw_ref[...] = lax.dot_general(a, ds, _DIMS["tn"], precision=HI, preferred_element_type=F32)
        dctx = ds[ctx_rows[0]:ctx_rows[0] + 1, :]
        for r in ctx_rows[1:]:
            dctx = dctx + ds[r:r + 1, :]
        q = lax.dot_general(jnp.broadcast_to(dctx, (8, tn)), w_ref[...], _DIMS["nt"], precision=HI,
                            preferred_element_type=F32)

        @pl.when(j == 0)
        def _():
            q_ref[...] = q
            df = df_ref[...]
            acc = df[0:1, :]
            for r in range(1, n):
                acc = acc + df[r:r + 1, :]
            gb_ref[...] = acc

        @pl.when(j > 0)
        def _():
            q_ref[...] += q

    return pl.pallas_call(
        body, name="mod_bwd", grid=(n_ct,),
        out_shape=(jax.ShapeDtypeStruct((d, ws), F32), jax.ShapeDtypeStruct((1, d_full.shape[1]), F32),
                   jax.ShapeDtypeStruct((8, d), F32)),
        in_specs=[pl.BlockSpec((n, d), lambda j: (0, 0)), pl.BlockSpec((n, tn), lambda j: (0, j)),
                  pl.BlockSpec(d_full.shape, lambda j: (0, 0)), pl.BlockSpec((d, tn), lambda j: (0, j))],
        out_specs=(pl.BlockSpec((d, tn), lambda j: (0, j)), pl.BlockSpec((1, d_full.shape[1]), lambda j: (0, 0)),
                   pl.BlockSpec((8, d), lambda j: (0, 0))),
        compiler_params=_params(40 << 20),
    )(a_rows, d_shard, d_full, w_shard)


def _shifted(xs, d, tok, width):
    if d == 0:
        return xs
    n = xs.shape[0]
    sh = pltpu.roll(xs, (-d) % n, axis=0)
    return jnp.where((tok + d >= 0) & (tok + d < width), sh, 0.0)


def _placed(out_shape, place):
    if place is None:
        return out_shape, 0, 0, None
    return place


def tapsum_roll(name, x, xcb, w, wcb, *, seq_len, n_seq, row_blk_off, width, piece, cb, ncb, pad, flip, place=None):
    n_tap = w.shape[0]
    n_piece = seq_len // piece
    out_shape, o_rb, o_cb, into = _placed((n_seq * seq_len, ncb * cb), place)

    def body(x_ref, w_ref, *rest):
        o_ref = rest[-1]
        wv = w_ref[...]
        tok = lax.broadcasted_iota(jnp.int32, (piece, 1), 0) % width

        def do_piece(p, carry):
            start = pl.multiple_of(p * piece, piece)
            xs = x_ref[pl.ds(start, piece), :]
            acc = jnp.zeros_like(xs)
            for k in range(n_tap):
                d = pad - k if flip else k - pad
                acc = acc + wv[k:k + 1, :] * _shifted(xs, d, tok, width)
            o_ref[pl.ds(start, piece), :] = acc
            return carry

        lax.fori_loop(0, n_piece, do_piece, 0)

    extra = [] if into is None else [into]
    return pl.pallas_call(
        body, name=name, grid=(ncb, n_seq), out_shape=_big(out_shape, F32),
        in_specs=[pl.BlockSpec((seq_len, cb), lambda j, s: (row_blk_off + s, xcb + j)),
                  pl.BlockSpec((n_tap, cb), lambda j, s: (0, wcb + j))] + [pl.BlockSpec(memory_space=pl.ANY)] * len(extra),
        out_specs=pl.BlockSpec((seq_len, cb), lambda j, s: (o_rb + s, o_cb + j)),
        input_output_aliases={2: 0} if extra else {},
        compiler_params=_params(8 * seq_len * cb * 4 + (8 << 20), 2),
    )(x, w, *extra)


def tapgrad_roll(name, dy, dycb, dy_blk_off, x, xcb, x_blk_off, *, n_tap, seq_len, n_seq, width, piece, cb, ncb, pad):
    n_piece = seq_len // piece

    def body(dy_ref, x_ref, o_ref):
        @pl.when(pl.program_id(1) == 0)
        def _():
            o_ref[...] = jnp.zeros_like(o_ref)

        tok = lax.broadcasted_iota(jnp.int32, (piece, 1), 0) % width

        def do_piece(p, carry):
            start = pl.multiple_of(p * piece, piece)
            xs = x_ref[pl.ds(start, piece), :]
            dv = dy_ref[pl.ds(start, piece), :]
            for k in range(n_tap):
                o_ref[k:k + 1, :] += jnp.sum(dv * _shifted(xs, k - pad, tok, width), axis=0, keepdims=True)
            return carry

        lax.fori_loop(0, n_piece, do_piece, 0)

    return pl.pallas_call(
        body, name=name, grid=(ncb, n_seq), out_shape=jax.ShapeDtypeStruct((n_tap, ncb * cb), F32),
        in_specs=[pl.BlockSpec((seq_len, cb), lambda j, s: (dy_blk_off + s, dycb + j)),
                  pl.BlockSpec((seq_len, cb), lambda j, s: (x_blk_off + s, xcb + j))],
        out_specs=pl.BlockSpec((n_tap, cb), lambda j, s: (0, j)),
        compiler_params=_params(8 * seq_len * cb * 4 + (8 << 20), 2),
    )(dy, x)


def tapsum_rows(name, x, xcb, w, wcb, *, seq_len, n_seq, cb, ncb, pad, flip, place=None):
    n_tap = w.shape[0]
    n_row = seq_len // GRID_W
    halo = pad * GRID_W
    out_shape, o_rb, o_cb, into = _placed((n_seq * seq_len, ncb * cb), place)

    def body(x_ref, w_ref, *rest):
        o_ref, xp = rest[-2:]
        xp[pl.ds(0, halo), :] = jnp.zeros((halo, cb), F32)
        xp[pl.ds(halo + seq_len, halo), :] = jnp.zeros((halo, cb), F32)
        xp[pl.ds(halo, seq_len), :] = x_ref[...]
        wv = w_ref[...]

        def do_row(r, carry):
            acc = jnp.zeros((GRID_W, cb), F32)
            for k in range(n_tap):
                d = pad - k if flip else k - pad
                acc = acc + wv[k:k + 1, :] * xp[pl.ds(pl.multiple_of((r + pad + d) * GRID_W, GRID_W), GRID_W), :]
            o_ref[pl.ds(pl.multiple_of(r * GRID_W, GRID_W), GRID_W), :] = acc
            return carry

        lax.fori_loop(0, n_row, do_row, 0)

    extra = [] if into is None else [into]
    return pl.pallas_call(
        body, name=name, grid=(ncb, n_seq), out_shape=_big(out_shape, F32),
        in_specs=[pl.BlockSpec((seq_len, cb), lambda j, s: (s, xcb + j)),
                  pl.BlockSpec((n_tap, cb), lambda j, s: (0, wcb + j))] + [pl.BlockSpec(memory_space=pl.ANY)] * len(extra),
        out_specs=pl.BlockSpec((seq_len, cb), lambda j, s: (o_rb + s, o_cb + j)),
        input_output_aliases={2: 0} if extra else {},
        scratch_shapes=[pltpu.VMEM((seq_len + 2 * halo, cb), F32)],
        compiler_params=_params(10 * seq_len * cb * 4 + (8 << 20), 2),
    )(x, w, *extra)


def tapgrad_rows(name, dy, dycb, x, xcb, *, n_tap, seq_len, n_seq, cb, ncb, pad):
    n_row = seq_len // GRID_W
    halo = pad * GRID_W

    def body(dy_ref, x_ref, o_ref, xp):
        @pl.when(pl.program_id(1) == 0)
        def _():
            o_ref[...] = jnp.zeros_like(o_ref)

        xp[pl.ds(0, halo), :] = jnp.zeros((halo, cb), F32)
        xp[pl.ds(halo + seq_len, halo), :] = jnp.zeros((halo, cb), F32)
        xp[pl.ds(halo, seq_len), :] = x_ref[...]

        def do_row(r, carry):
            dv = dy_ref[pl.ds(pl.multiple_of(r * GRID_W, GRID_W), GRID_W), :]
            for k in range(n_tap):
                xs = xp[pl.ds(pl.multiple_of((r + k) * GRID_W, GRID_W), GRID_W), :]
                o_ref[k:k + 1, :] += jnp.sum(dv * xs, axis=0, keepdims=True)
            return carry

        lax.fori_loop(0, n_row, do_row, 0)

    return pl.pallas_call(
        body, name=name, grid=(ncb, n_seq), out_shape=jax.ShapeDtypeStruct((n_tap, ncb * cb), F32),
        in_specs=[pl.BlockSpec((seq_len, cb), lambda j, s: (s, dycb + j)),
                  pl.BlockSpec((seq_len, cb), lambda j, s: (s, xcb + j))],
        out_specs=pl.BlockSpec((n_tap, cb), lambda j, s: (0, j)),
        scratch_shapes=[pltpu.VMEM((seq_len + 2 * halo, cb), F32)],
        compiler_params=_params(10 * seq_len * cb * 4 + (8 << 20), 2),
    )(dy, x)


def _ssd_blocks(b, s, *, rev, n_ctx, n_lat, lat_blocks):
    if rev:
        return jnp.where(s < n_ctx, lat_blocks + b * n_ctx + (n_ctx - 1 - s), b * n_lat + (n_lat - 1 - (s - n_ctx)))
    return jnp.where(s < n_ctx, lat_blocks + b * n_ctx + s, b * n_lat + (s - n_ctx))


def _ssd_common(xbc, raw, dtb, alog, dsk, *, rev, ds, n_head):
    if rev:
        raw = pltpu.roll(raw, LANES - n_head, axis=1)
    pre = raw + dtb
    dt = jnp.maximum(pre, 0.0) + jnp.log1p(jnp.exp(-jnp.abs(pre)))
    sig = jax.nn.sigmoid(pre)
    a = -jnp.exp(alog)
    da = dt * a
    ri = lax.broadcasted_iota(jnp.int32, (CHUNK, CHUNK), 0)
    ci = lax.broadcasted_iota(jnp.int32, (CHUNK, CHUNK), 1)
    mask = (ci >= ri) if rev else (ci <= ri)
    tri = mask.astype(F32)
    tri_t = ((ci <= ri) if rev else (ci >= ri)).astype(F32)
    cs = jnp.dot(tri, da, precision=HI, preferred_element_type=F32)
    tot = jnp.sum(da, axis=0, keepdims=True)
    def wide(v):
        first = lax.broadcasted_iota(jnp.int32, (v.shape[0], LANES), 1) < HEAD_DIM
        return jnp.concatenate(
            [jnp.where(first, jnp.broadcast_to(v[:, 2 * p:2 * p + 1], first.shape),
                       jnp.broadcast_to(v[:, 2 * p + 1:2 * p + 2], first.shape)) for p in range(n_head // 2)], axis=1)

    cs_w, tot_w = wide(cs), wide(tot)
    xh = xbc[:, :ds]
    dt_w = wide(dt)
    return dict(
        dt=dt, sig=sig, a=a, cs=cs, cs_t=cs.T, tot=tot, mask=mask, tri_t=tri_t,
        e_w=jnp.exp(cs_w), wt_w=jnp.exp(tot_w - cs_w), dec_w=jnp.exp(tot_w), dt_w=dt_w, dsk_w=wide(dsk),
        xh=xh, xs_w=xh * dt_w, bm=xbc[:, ds:ds + 2 * N_STATE], cm=xbc[:, ds + 2 * N_STATE:ds + 4 * N_STATE])


def _decay(q, col):
    seg = q["cs"][:, col:col + 1] - q["cs_t"][col:col + 1, :]
    return jnp.exp(jnp.where(q["mask"], seg, -jnp.inf))


def _split_heads(v):
    lane = lax.broadcasted_iota(jnp.int32, v.shape, 1)
    return jnp.concatenate([jnp.where(lane < HEAD_DIM, v, 0.0), jnp.where(lane >= HEAD_DIM, v, 0.0)], axis=0)


def ssd_fwd(name, xbc, proj, dt_cb, dtb, alog, dsk, *, rev, n_ex, seq_len, ctx_len, ds, rider=None):
    n_head, half = ds // HEAD_DIM, ds // 2
    n_ctx, n_lat = ctx_len // CHUNK, seq_len // CHUNK
    n_step = n_ctx + n_lat
    blk = functools.partial(_ssd_blocks, rev=rev, n_ctx=n_ctx, n_lat=n_lat, lat_blocks=n_ex * n_lat)
    xw = xbc.shape[1]

    def y_blk(b, s):
        sl = jnp.maximum(s, n_ctx) - n_ctx
        return b * n_lat + ((n_lat - 1 - sl) if rev else sl)

    hosted = _Hosted(rider, 5, 2, 1, (n_ex, n_step))

    def body(*refs):
        (xbc_ref, dt_ref, dtb_ref, alog_ref, dsk_ref), (y_ref, hs_ref), (h_scr,) = hosted.split(refs)

        @pl.when(pl.program_id(1) == 0)
        def _():
            h_scr[...] = jnp.zeros_like(h_scr)

        q = _ssd_common(xbc_ref[...], dt_ref[...], dtb_ref[...], alog_ref[...], dsk_ref[...], rev=rev, ds=ds, n_head=n_head)
        h = h_scr[...]
        hs_ref[...] = h
        for g in range(2):
            lo = g * half
            bg = q["bm"][:, g * N_STATE:(g + 1) * N_STATE].astype(BF16)
            cg = q["cm"][:, g * N_STATE:(g + 1) * N_STATE].astype(BF16)
            scores = lax.dot_general(cg, bg, _DIMS["nt"], preferred_element_type=F32)
            hg = h[:, lo:lo + half]
            off = jnp.dot(cg, hg.astype(BF16), preferred_element_type=F32)
            for j in range(half // LANES):
                c0 = (lo + j * LANES) // HEAD_DIM
                ln = slice(lo + j * LANES, lo + (j + 1) * LANES)
                p_cat = jnp.concatenate([scores * _decay(q, c0), scores * _decay(q, c0 + 1)], axis=1).astype(BF16)
                diag = jnp.dot(p_cat, _split_heads(q["xs_w"][:, ln]).astype(BF16), preferred_element_type=F32)
                y_ref[:, ln] = (diag + q["e_w"][:, ln] * off[:, j * LANES:(j + 1) * LANES]
                                + q["dsk_w"][:, ln] * q["xh"][:, ln])
            v = (q["wt_w"][:, lo:lo + half] * q["xs_w"][:, lo:lo + half]).astype(BF16)
            h_scr[:, lo:lo + half] = (q["dec_w"][:, lo:lo + half] * hg
                                      + lax.dot_general(bg, v, _DIMS["tn"], preferred_element_type=F32))
        hosted.finish()

    vec = pl.BlockSpec((1, LANES), lambda b, s: (0, 0))
    in_specs, out_shape, out_specs, scratch, args = hosted.call_args(
        [pl.BlockSpec((CHUNK, xw), lambda b, s: (blk(b, s), 0)),
         pl.BlockSpec((CHUNK, LANES), lambda b, s: (blk(b, s), dt_cb)), vec, vec, vec],
        (_big((n_ex * seq_len, ds), F32), _big((n_ex, n_step, N_STATE, ds), F32)),
        (pl.BlockSpec((CHUNK, ds), lambda b, s: (y_blk(b, s), 0)),
         pl.BlockSpec((None, None, N_STATE, ds), lambda b, s: (b, s, 0, 0))),
        [pltpu.VMEM((N_STATE, ds), F32)], [xbc, proj, dtb, alog, dsk])
    return hosted.results(pl.pallas_call(
        body, name=name, grid=(n_ex, n_step), out_shape=out_shape, in_specs=in_specs, out_specs=out_specs,
        scratch_shapes=scratch, compiler_params=_params(40 << 20, 2),
    )(*args))


def ssd_bwd(name, xbc, proj, dt_cb, hs, dy, dtb, alog, dsk, *, rev, n_ex, seq_len, ctx_len, ds, rider=None):
    n_head, half = ds // HEAD_DIM, ds // 2
    n_ctx, n_lat = ctx_len // CHUNK, seq_len // CHUNK
    n_step = n_ctx + n_lat
    n_tok = n_ex * (seq_len + ctx_len)
    blk0 = functools.partial(_ssd_blocks, rev=rev, n_ctx=n_ctx, n_lat=n_lat, lat_blocks=n_ex * n_lat)
    step = lambda sp: n_step - 1 - sp
    blk = lambda b, sp: blk0(b, step(sp))
    xw = xbc.shape[1]

    def dy_blk(b, sp):
        sl = jnp.maximum(step(sp), n_ctx) - n_ctx
        return b * n_lat + ((n_lat - 1 - sl) if rev else sl)

    hosted = _Hosted(rider, 7, 5, 1, (n_ex, n_step))

    def body(*refs):
        ((xbc_ref, dt_ref, hs_ref, dy_ref, dtb_ref, alog_ref, dsk_ref),
         (dxbc_ref, ddt_ref, dalog_ref, ddtb_ref, ddsk_ref), (dh_scr,)) = hosted.split(refs)
        b, sp = pl.program_id(0), pl.program_id(1)

        @pl.when(sp == 0)
        def _():
            dh_scr[...] = jnp.zeros_like(dh_scr)

        @pl.when((sp == 0) & (b == 0))
        def _():
            dalog_ref[...] = jnp.zeros_like(dalog_ref)
            ddtb_ref[...] = jnp.zeros_like(ddtb_ref)
            ddsk_ref[...] = jnp.zeros_like(ddsk_ref)

        q = _ssd_common(xbc_ref[...], dt_ref[...], dtb_ref[...], alog_ref[...], dsk_ref[...], rev=rev, ds=ds, n_head=n_head)
        h = hs_ref[...]
        d_y = jnp.where(step(sp) >= n_ctx, dy_ref[...], 0.0)
        dh_next = dh_scr[...]
        lane_row = lax.broadcasted_iota(jnp.int32, (1, LANES), 1)
        d_cs = jnp.zeros((CHUNK, LANES), F32)
        dxs_parts, de_parts, dwt_parts, ddec_parts = [], [], [], []
        for g in range(2):
            lo = g * half
            gs = slice(lo, lo + half)
            bg = q["bm"][:, g * N_STATE:(g + 1) * N_STATE].astype(BF16)
            cg = q["cm"][:, g * N_STATE:(g + 1) * N_STATE].astype(BF16)
            scores = lax.dot_general(cg, bg, _DIMS["nt"], preferred_element_type=F32)
            hg, dyg, dhn = h[:, gs], d_y[:, gs], dh_next[:, gs]
            off = jnp.dot(cg, hg.astype(BF16), preferred_element_type=F32)
            d_off = (q["e_w"][:, gs] * dyg).astype(BF16)
            de_parts.append(dyg * off)
            d_c = lax.dot_general(d_off, hg.astype(BF16), _DIMS["nt"], preferred_element_type=F32)
            dh_scr[:, gs] = (lax.dot_general(cg, d_off, _DIMS["tn"], preferred_element_type=F32)
                             + q["dec_w"][:, gs] * dhn)
            b_dh = jnp.dot(bg, dhn.astype(BF16), preferred_element_type=F32)
            v = q["wt_w"][:, gs] * q["xs_w"][:, gs]
            d_b = lax.dot_general(v.astype(BF16), dhn.astype(BF16), _DIMS["nt"], preferred_element_type=F32)
            dwt_parts.append(q["xs_w"][:, gs] * b_dh)
            ddec_parts.append(jnp.sum(hg * dhn, axis=0, keepdims=True))
            d_scores = jnp.zeros((CHUNK, CHUNK), F32)
            for j in range(half // LANES):
                c0 = (lo + j * LANES) // HEAD_DIM
                ln = slice(lo + j * LANES, lo + (j + 1) * LANES)
                l0, l1 = _decay(q, c0), _decay(q, c0 + 1)
                p0, p1 = scores * l0, scores * l1
                dy_st = _split_heads(d_y[:, ln]).astype(BF16)
                d_p = lax.dot_general(dy_st, q["xs_w"][:, ln].astype(BF16), _DIMS["nt"], preferred_element_type=F32)
                d_p0, d_p1 = d_p[:CHUNK], d_p[CHUNK:]
                d_scores = d_scores + d_p0 * l0 + d_p1 * l1
                for col, t in ((c0, d_p0 * p0), (c0 + 1, d_p1 * p1)):
                    d_cs = d_cs + jnp.sum(t - t.T, axis=1, keepdims=True) * (lane_row == col).astype(F32)
                p_st = jnp.concatenate([p0, p1], axis=0).astype(BF16)
                dxs_parts.append(lax.dot_general(p_st, dy_st, _DIMS["tn"], preferred_element_type=F32)
                                 + q["wt_w"][:, ln] * b_dh[:, j * LANES:(j + 1) * LANES])
            d_sc = d_scores.astype(BF16)
            d_c = d_c + jnp.dot(d_sc, bg, preferred_element_type=F32)
            d_b = d_b + lax.dot_general(d_sc, cg, _DIMS["tn"], preferred_element_type=F32)
            dxbc_ref[:, ds + g * N_STATE:ds + (g + 1) * N_STATE] = d_b
            dxbc_ref[:, ds + (2 + g) * N_STATE:ds + (3 + g) * N_STATE] = d_c
        d_xs = jnp.concatenate(dxs_parts, axis=1)
        narrow_m = (lax.broadcasted_iota(jnp.int32, (ds, LANES), 0) // HEAD_DIM
                    == lax.broadcasted_iota(jnp.int32, (ds, LANES), 1)).astype(BF16)
        rows8 = lambda v: jnp.broadcast_to(v, (8, ds))
        stacked = jnp.concatenate(
            [jnp.concatenate(dwt_parts, axis=1), jnp.concatenate(de_parts, axis=1), d_xs * q["xh"],
             rows8(jnp.concatenate(ddec_parts, axis=1)), rows8(jnp.sum(d_y * q["xh"], axis=0, keepdims=True))], axis=0)
        hi = stacked.astype(BF16)
        lo = (stacked - hi.astype(F32)).astype(BF16)
        sums = (jnp.dot(hi, narrow_m, preferred_element_type=F32) + jnp.dot(lo, narrow_m, preferred_element_type=F32))
        n_wt, n_e, n_xs = sums[:CHUNK], sums[CHUNK:2 * CHUNK], sums[2 * CHUNK:3 * CHUNK]
        n_dec, n_dsk = sums[3 * CHUNK:3 * CHUNK + 1], sums[3 * CHUNK + 8:3 * CHUNK + 9]
        e, wt, dec = jnp.exp(q["cs"]), jnp.exp(q["tot"] - q["cs"]), jnp.exp(q["tot"])
        d_wt = n_wt * wt
        d_cs = d_cs + n_e * e - d_wt
        d_tot = jnp.sum(d_wt, axis=0, keepdims=True) + n_dec * dec
        d_da = jnp.dot(q["tri_t"], d_cs, precision=HI, preferred_element_type=F32) + d_tot
        d_dt = d_da * q["a"] + n_xs
        dxbc_ref[:, :ds] = d_xs * q["dt_w"] + q["dsk_w"] * d_y
        dalog_ref[...] += jnp.sum(d_da * q["dt"], axis=0, keepdims=True) * q["a"]
        d_raw = d_dt * q["sig"]
        ddtb_ref[...] += jnp.sum(d_raw, axis=0, keepdims=True)
        ddsk_ref[...] += n_dsk
        ddt_ref[...] = pltpu.roll(d_raw, n_head, axis=1) if rev else d_raw
        hosted.finish()

    vec = pl.BlockSpec((1, LANES), lambda b, s: (0, 0))
    vec_shape = jax.ShapeDtypeStruct((1, LANES), F32)
    in_specs, out_shape, out_specs, scratch, args = hosted.call_args(
        [pl.BlockSpec((CHUNK, xw), lambda b, s: (blk(b, s), 0)),
         pl.BlockSpec((CHUNK, LANES), lambda b, s: (blk(b, s), dt_cb)),
         pl.BlockSpec((None, None, N_STATE, ds), lambda b, s: (b, step(s), 0, 0)),
         pl.BlockSpec((CHUNK, ds), lambda b, s: (dy_blk(b, s), 0)), vec, vec, vec],
        (_big((n_tok, xw), F32), _big((n_tok, LANES), F32), vec_shape, vec_shape, vec_shape),
        (pl.BlockSpec((CHUNK, xw), lambda b, s: (blk(b, s), 0)),
         pl.BlockSpec((CHUNK, LANES), lambda b, s: (blk(b, s), 0)), vec, vec, vec),
        [pltpu.VMEM((N_STATE, ds), F32)], [xbc, proj, hs, dy, dtb, alog, dsk])
    return hosted.results(pl.pallas_call(
        body, name=name, grid=(n_ex, n_step), out_shape=out_shape, in_specs=in_specs, out_specs=out_specs,
        scratch_shapes=scratch, compiler_params=_params(48 << 20, 2),
    )(*args))


def final_loss(x3, target, w, *, tm):
    n, d = x3.shape

    def body(x_ref, t_ref, w_ref, dx_ref, dw_ref, loss_ref):
        i = pl.program_id(0)
        t = t_ref[...]

        def per_feature(xv, wv):
            err = _rms(xv, wv) - t
            return 0.5 * jnp.sum(err * err, axis=0, keepdims=True) / d

        lv, vjp = jax.vjp(per_feature, x_ref[...], w_ref[...])
        dx, dw = vjp(jnp.ones_like(lv))
        dx_ref[...] = dx

        @pl.when(i == 0)
        def _():
            dw_ref[...] = dw
            loss_ref[...] = lv

        @pl.when(i > 0)
        def _():
            dw_ref[...] += dw
            loss_ref[...] += lv

    tile = pl.BlockSpec((tm, d), lambda i: (i, 0))
    vec = pl.BlockSpec((1, d), lambda i: (0, 0))
    return pl.pallas_call(
        body, name="final_loss", grid=(n // tm,), in_specs=[tile, tile, vec],
        out_shape=(jax.ShapeDtypeStruct((n, d), F32), jax.ShapeDtypeStruct((1, d), F32), jax.ShapeDtypeStruct((1, d), F32)),
        out_specs=(tile, vec, vec), compiler_params=_params(tm * d * 4 * 16 + (8 << 20)),
    )(x3, target, w)


def sum_slots(name, arr):
    n_slot, n_row, width = arr.shape
    tm = _row_tile(n_row, width * n_slot, mult=16)

    def body(a_ref, o_ref):
        acc = a_ref[0].astype(F32)
        for j in range(1, n_slot):
            acc = acc + a_ref[j].astype(F32)
        o_ref[...] = acc

    return pl.pallas_call(
        body, name=name, grid=(n_row // tm,), out_shape=jax.ShapeDtypeStruct((n_row, width), F32),
        in_specs=[pl.BlockSpec((n_slot, tm, width), lambda i: (0, i, 0))],
        out_specs=pl.BlockSpec((tm, width), lambda i: (i, 0)), compiler_params=_params(),
    )(arr)


def adamw(name, w, g_slots, m, v):
    n_slot, n_row, width = g_slots.shape
    tm = _row_tile(n_row, width * 2)

    def body(w_ref, g_ref, m_ref, v_ref, go_ref, d_ref, mo_ref, vo_ref):
        g = g_ref[0]
        for j in range(1, n_slot):
            g = g + g_ref[j]
        m2 = ADAM_B1 * m_ref[...] + (1.0 - ADAM_B1) * g
        v2 = ADAM_B2 * v_ref[...] + (1.0 - ADAM_B2) * jnp.square(g)
        m_hat = m2 / (1.0 - ADAM_B1 ** ADAM_STEP)
        v_hat = v2 / (1.0 - ADAM_B2 ** ADAM_STEP)
        go_ref[...] = g
        d_ref[...] = -ADAM_LR * (m_hat / (jnp.sqrt(v_hat) + ADAM_EPS) + ADAM_WD * w_ref[...])
        mo_ref[...] = m2
        vo_ref[...] = v2

    tile = pl.BlockSpec((tm, width), lambda i: (i, 0))
    shape = jax.ShapeDtypeStruct((n_row, width), F32)
    return pl.pallas_call(
        body, name=name, grid=(n_row // tm,), out_shape=(shape,) * 4,
        in_specs=[tile, pl.BlockSpec((n_slot, tm, width), lambda i: (0, i, 0)), tile, tile],
        out_specs=(tile,) * 4, compiler_params=_params(),
    )(w, g_slots, m, v)


def cctx_grad(q_all, c_ctx_row):
    d = c_ctx_row.shape[1]

    def body(q_ref, c_ref, o_ref):
        acc = q_ref[0, 0:1, :]
        for j in (2, 4, 6):
            acc = acc + q_ref[j, 0:1, :]
        _, vjp = jax.vjp(_silu, c_ref[...])
        o_ref[...] = vjp(acc)[0]

    return pl.pallas_call(
        body, name="cctx_grad", out_shape=jax.ShapeDtypeStruct((1, d), F32),
    )(q_all, c_ctx_row)


def loss_total(pack_sum, d):
    def body(p_ref, o_ref):
        o_ref[...] = jnp.sum(p_ref[:, 0:d], axis=1, keepdims=True)

    return pl.pallas_call(
        body, name="loss_total", out_shape=jax.ShapeDtypeStruct((1, 1), F32),
    )(pack_sum)


class _Plan:
    def __init__(self):
        self.builders, self.got = {}, {}

    def on(self, host, key, builder):
        self.builders.setdefault(host, []).append((key, builder))

    def run(self, host, fn, *args, **kw):
        if host not in self.builders:
            return fn(host, *args, **kw)
        keys, riders = zip(*[(key, builder(self)) for key, builder in self.builders[host]])
        res, landed = fn(host, *args, rider=Riders(riders), **kw)
        for key, r in zip(keys, riders):
            self.got[key], landed = landed[:r.n], landed[r.n:]
        return res


def _val(w):
    return w() if callable(w) else w


def _matmul_tile(n_rows, tm):
    return 2 * tm if n_rows % (2 * tm) == 0 else tm


def _ffn_fwd(plan, tag, xin, n_rows, tm, seg_fn, shift, scale, gate, norm_w, wg, wu, wd):
    d = xin.shape[1]
    n_tiles = n_rows // tm
    (h,) = plan.run(f"{tag}_norm", rowwise, fn_norm_mod, [row(xin)], [shift, scale], [norm_w], [(n_rows, d, BF16)],
                    tm=tm, n_tiles=n_tiles, seg_fn=seg_fn)
    tmm = _matmul_tile(n_rows, tm)
    g = plan.run(f"{tag}_gate", matmul, [(h, _val(wg))], "nn", out_dtype=BF16, b_ch=True, out_ch=True, tm=tmm)
    u, act = plan.run(f"{tag}_up", matmul, [(h, _val(wu))], "nn", b_ch=True, out_ch=True, tm=tmm,
                      post=([g], lambda acc, gv: (acc, fn_act(gv, acc)[0]), [BF16, BF16]))
    f = plan.run(f"{tag}_down", matmul, [(act, _val(wd))], "nn", a_ch=True, b_ch=True, tm=tmm, fold=True)
    (xo,) = plan.run(f"{tag}_resid", rowwise, make_fn_resid(0.5), [row(xin), row(f)], [gate], [], [(n_rows, d, F32)],
                     tm=tm, n_tiles=n_tiles, seg_fn=seg_fn)
    return xo, (h, g, u, act, f)


def _ffn_bwd(plan, tag, d_xo, saved, xin, n_rows, tm, seg_fn, first_fn, shift, scale, gate, norm_w, wg, wu, wd, dx_rows, dx_limit):
    h, g, u, act, f = saved
    d = xin.shape[1]
    n_tiles = n_rows // tm
    n_ch, _, n_hid = g.shape
    d_f, d_gate = plan.run(f"{tag}_resid_bwd", rowwise_bwd, make_fn_resid(0.5), [row(xin), row(f)], [gate], [], [[row(d_xo)]],
                           [None, (n_rows, BF16, None)], tm=tm, n_tiles=n_tiles, seg_fn=seg_fn, first_fn=first_fn)
    tmm = _matmul_tile(n_rows, tm)
    act_vjp = lambda acc, gv, uv: jax.vjp(lambda a, b: fn_act(a, b)[0], gv, uv)[1](acc)
    d_g, d_u = plan.run(f"{tag}_down_dx", matmul, [(d_f, wd)], "nt", b_ch=True, out_ch=True, tm=tmm,
                        post=([g, u], act_vjp, [BF16, BF16]))
    plan.got[f"{tag}_d_wd"] = plan.run(f"{tag}_down_dw", matmul, [(act, d_f)], "tn", out_dtype=BF16, a_ch=True, out_ch=True, tm=tmm)
    d_h = plan.run(f"{tag}_up_dx", matmul, [(d_g, wg), (d_u, wu)], "nt", a_ch=True, b_ch=True, tm=tmm)
    plan.got[f"{tag}_d_wg"] = plan.run(f"{tag}_gate_dw", matmul, [(d_g, h)], "tn", out_dtype=BF16, a_ch=True, out_ch=True, tm=tmm)
    plan.got[f"{tag}_d_wu"] = plan.run(f"{tag}_up_dw", matmul, [(d_u, h)], "tn", out_dtype=BF16, a_ch=True, out_ch=True, tm=tmm)
    d_x, d_shift, d_scale, d_nw = plan.run(
        f"{tag}_norm_bwd", rowwise_bwd, fn_norm_mod, [row(xin)], [shift, scale], [norm_w], [[row(d_h)]], [(dx_rows, F32, dx_limit)],
        tm=tm, n_tiles=n_tiles, seg_fn=seg_fn, first_fn=first_fn, adds={0: (row(d_xo), None)})
    return d_x, (d_shift, d_scale, d_gate), d_nw


def kernel(x, c, ctx, c_ctx, w_mod, b_mod, norm_ffn1, ffn1_gate, ffn1_up, ffn1_down, norm_mix, w_in, ssm_conv_w, ssm_conv_b, dt_bias_fwd, dt_bias_bwd, a_log_fwd, a_log_bwd, ssm_d, ssm_norm_w, cconv_w, cconv_b, cconv_ln_w, cconv_ln_b, w_out, norm_ffn2, ffn2_gate, ffn2_up, ffn2_down, final_norm, loss_target, m_c_ctx, m_w_mod, m_b_mod, m_norm_ffn1, m_ffn1_gate, m_ffn1_up, m_ffn1_down, m_norm_mix, m_w_in, m_ssm_conv_w, m_ssm_conv_b, m_dt_bias_fwd, m_dt_bias_bwd, m_a_log_fwd, m_a_log_bwd, m_ssm_d, m_ssm_norm_w, m_cconv_w, m_cconv_b, m_cconv_ln_w, m_cconv_ln_b, m_w_out, m_norm_ffn2, m_ffn2_gate, m_ffn2_up, m_ffn2_down, m_final_norm, v_c_ctx, v_w_mod, v_b_mod, v_norm_ffn1, v_ffn1_gate, v_ffn1_up, v_ffn1_down, v_norm_mix, v_w_in, v_ssm_conv_w, v_ssm_conv_b, v_dt_bias_fwd, v_dt_bias_bwd, v_a_log_fwd, v_a_log_bwd, v_ssm_d, v_ssm_norm_w, v_cconv_w, v_cconv_b, v_cconv_ln_w, v_cconv_ln_b, v_w_out, v_norm_ffn2, v_ffn2_gate, v_ffn2_up, v_ffn2_down, v_final_norm):
    weights = dict(c_ctx=c_ctx, w_mod=w_mod, b_mod=b_mod, norm_ffn1=norm_ffn1, ffn1_gate=ffn1_gate, ffn1_up=ffn1_up, ffn1_down=ffn1_down, norm_mix=norm_mix, w_in=w_in, ssm_conv_w=ssm_conv_w, ssm_conv_b=ssm_conv_b, dt_bias_fwd=dt_bias_fwd, dt_bias_bwd=dt_bias_bwd, a_log_fwd=a_log_fwd, a_log_bwd=a_log_bwd, ssm_d=ssm_d, ssm_norm_w=ssm_norm_w, cconv_w=cconv_w, cconv_b=cconv_b, cconv_ln_w=cconv_ln_w, cconv_ln_b=cconv_ln_b, w_out=w_out, norm_ffn2=norm_ffn2, ffn2_gate=ffn2_gate, ffn2_up=ffn2_up, ffn2_down=ffn2_down, final_norm=final_norm)
    mom1 = dict(c_ctx=m_c_ctx, w_mod=m_w_mod, b_mod=m_b_mod, norm_ffn1=m_norm_ffn1, ffn1_gate=m_ffn1_gate, ffn1_up=m_ffn1_up, ffn1_down=m_ffn1_down, norm_mix=m_norm_mix, w_in=m_w_in, ssm_conv_w=m_ssm_conv_w, ssm_conv_b=m_ssm_conv_b, dt_bias_fwd=m_dt_bias_fwd, dt_bias_bwd=m_dt_bias_bwd, a_log_fwd=m_a_log_fwd, a_log_bwd=m_a_log_bwd, ssm_d=m_ssm_d, ssm_norm_w=m_ssm_norm_w, cconv_w=m_cconv_w, cconv_b=m_cconv_b, cconv_ln_w=m_cconv_ln_w, cconv_ln_b=m_cconv_ln_b, w_out=m_w_out, norm_ffn2=m_norm_ffn2, ffn2_gate=m_ffn2_gate, ffn2_up=m_ffn2_up, ffn2_down=m_ffn2_down, final_norm=m_final_norm)
    mom2 = dict(c_ctx=v_c_ctx, w_mod=v_w_mod, b_mod=v_b_mod, norm_ffn1=v_norm_ffn1, ffn1_gate=v_ffn1_gate, ffn1_up=v_ffn1_up, ffn1_down=v_ffn1_down, norm_mix=v_norm_mix, w_in=v_w_in, ssm_conv_w=v_ssm_conv_w, ssm_conv_b=v_ssm_conv_b, dt_bias_fwd=v_dt_bias_fwd, dt_bias_bwd=v_dt_bias_bwd, a_log_fwd=v_a_log_fwd, a_log_bwd=v_a_log_bwd, ssm_d=v_ssm_d, ssm_norm_w=v_ssm_norm_w, cconv_w=v_cconv_w, cconv_b=v_cconv_b, cconv_ln_w=v_cconv_ln_w, cconv_ln_b=v_cconv_ln_b, w_out=v_w_out, norm_ffn2=v_norm_ffn2, ffn2_gate=v_ffn2_gate, ffn2_up=v_ffn2_up, ffn2_down=v_ffn2_down, final_norm=v_final_norm)
    order = list(weights)

    n_ex, seq_len, d = x.shape
    ctx_len = ctx.shape[1]
    ds = d
    n_head = ds // HEAD_DIM
    xw = ds + 4 * N_STATE
    n_lat, n_ctx_rows = n_ex * seq_len, n_ex * ctx_len
    n_tok = n_lat + n_ctx_rows
    tm = math.gcd(math.gcd(512, seq_len), n_ctx_rows)
    seg_all, first_all = _segmenter(tm, seq_len, n_lat)
    lat_tiles = n_lat // tm

    xi, yi, ci = lax.axis_index("x"), lax.axis_index("y"), lax.axis_index("c")
    me, chip = 4 * xi + 2 * yi + ci, 2 * xi + yi

    (c_all,) = exchange("gather_c", [c], "all8")
    n_all = 8 * n_ex
    n_cond = -(-(n_all + 1) // 8) * 8
    cond = jnp.concatenate([c_all.reshape(n_all, d), c_ctx[None, :], jnp.zeros((n_cond - n_all - 1, d), F32)])
    mod_w = w_mod.shape[2]
    b_shard = lax.dynamic_slice(b_mod, (0, chip * mod_w), (1, mod_w))
    (mod_g,) = exchange("gather_mod", [mod_fwd(cond, w_mod[0], b_shard)], "chips")
    mod_full = mod_g.transpose(1, 0, 2).reshape(n_cond, N_CHIPS * mod_w)
    mod_mine = lax.dynamic_slice(mod_full, (me * n_ex, 0), (n_ex, 9 * d)).reshape(n_ex, 9, d)
    mod_ctx = mod_full[n_all].reshape(9, d)
    tabs = [jnp.concatenate([mod_mine[:, j], mod_ctx[j][None]])[:, None, :] for j in range(9)]
    lat = lambda t: t[:n_ex]

    bf = lambda w: w[0].astype(BF16)
    plan = _Plan()
    gather = lambda *ws: (lambda p: Rider(list(ws), "chips"))
    plan.on("ffn1_norm", "wg1", gather(bf(ffn1_gate)))
    plan.on("ffn1_gate", "wu1", gather(bf(ffn1_up)))
    plan.on("ffn1_up", "wd1", gather(bf(ffn1_down)))
    win_cut = d * 5 // 8
    plan.on("ffn1_down", "win_a", gather(bf(w_in)[:win_cut]))
    plan.on("ffn1_resid", "win_b", gather(bf(w_in)[win_cut:], ssm_conv_w[0], cconv_w[0]))
    xt = jnp.concatenate([x.reshape(n_lat, d), ctx.reshape(n_ctx_rows, d)])
    x1, saved1 = _ffn_fwd(plan, "ffn1", xt, n_tok, tm, seg_all, tabs[0], tabs[1], tabs[2], norm_ffn1,
                          lambda: plan.got["wg1"][0], lambda: plan.got["wu1"][0], lambda: plan.got["wd1"][0])
    (wg1,), (wu1,), (wd1,), (win_a,), (win_b, w5_g, w31_g) = (plan.got[k] for k in ("wg1", "wu1", "wd1", "win_a", "win_b"))
    win_g = jnp.concatenate([win_a, win_b], axis=1)
    unshard_cols = lambda t: t.transpose(1, 0, 2).reshape(t.shape[1], N_CHIPS * t.shape[2])
    win = unshard_cols(win_g)
    o_x, o_dt, o_glu = ds, ds + xw, ds + xw + 2 * n_head
    w_z, w_xbc, w_dt = win[:, :ds], win[:, o_x:o_dt], win[:, o_dt:o_glu]
    w_ga, w_gb = win[:, o_glu:o_glu + d], win[:, o_glu + d:]
    w_dtp = jnp.concatenate([w_dt, jnp.zeros((d, LANES - 2 * n_head), BF16)], axis=1)
    w_cat = jnp.concatenate([w_z, w_ga, w_gb, w_xbc, w_dtp], axis=1)
    cbw = d // 2
    xbc_cb, dt_cb = 3 * d // cbw, (3 * d + xw) // LANES
    w5, w31 = unshard_cols(w5_g), unshard_cols(w31_g)
    pad_vec = lambda v: jnp.concatenate([v.reshape(1, -1), jnp.zeros((1, LANES - v.size), F32)], axis=1)
    dtb_f, dtb_b, alog_f, alog_b = map(pad_vec, (dt_bias_fwd, dt_bias_bwd, a_log_fwd, a_log_bwd))
    dsk_f, dsk_b = pad_vec(ssm_d), jnp.zeros((1, LANES), F32)

    (h2,) = rowwise("mix_norm", fn_norm_mod, [row(x1)], [tabs[3], tabs[4]], [norm_mix], [(n_tok, d, BF16)],
                    tm=tm, n_tiles=n_tok // tm, seg_fn=seg_all)
    proj, (wg2,) = matmul("mix_proj", [(h2, w_cat)], "nn", tm=min(tm, 256), rider=Rider([bf(ffn2_gate)], "chips"))
    def conv5(name, src, cb0, flip):
        out = None
        for part, seq, off in (("lat", seq_len, 0), ("ctx", ctx_len, n_lat // ctx_len)):
            out = tapsum_roll(f"{name}_{part}", src, cb0, w5, 0, seq_len=seq, n_seq=n_ex, row_blk_off=off, width=seq,
                              piece=seq, cb=cbw, ncb=xw // cbw, pad=w5.shape[0] // 2, flip=flip,
                              place=((n_tok, xw), off, 0, out))
        return out

    craw = conv5("xbc_conv", proj, xbc_cb, False)
    (xbc,) = rowwise("xbc_silu", fn_silu_bias, [row(craw)], [], [ssm_conv_b], [(n_tok, xw, F32)], tm=tm, n_tiles=n_tok // tm)
    ssd = dict(n_ex=n_ex, seq_len=seq_len, ctx_len=ctx_len, ds=ds)
    (y_f, hs_f), (wu2, wd2) = ssd_fwd("ssd_fwd_f", xbc, proj, dt_cb, dtb_f, alog_f, dsk_f, rev=False,
                                      rider=Rider([bf(ffn2_up), bf(ffn2_down)], "chips"), **ssd)
    (y_b, hs_b), (wout_g,) = ssd_fwd("ssd_fwd_b", xbc, proj, dt_cb, dtb_b, alog_b, dsk_b, rev=True,
                                     rider=Rider([bf(w_out)], "chips"), **ssd)
    wout = wout_g.reshape(2 * d, d)
    wo_y, wo_u = wout[:ds], wout[ds:]
    fn_gate = make_fn_gate_groupnorm(ds)
    (yn,) = rowwise("ssd_gate", fn_gate, [row(y_f), row(y_b), row(proj, d, 0)], [], [ssm_norm_w], [(n_lat, ds, BF16)],
                    tm=tm, n_tiles=lat_tiles)
    (u0,) = rowwise("glu", fn_glu, [row(proj, d, 1), row(proj, d, 2)], [], [], [(n_lat, d, F32)], tm=tm, n_tiles=lat_tiles)
    cb31 = max(LANES, d // 4)
    ncb31 = (d // 2) // cb31
    pad31 = w31.shape[0] // 2
    piece31 = min(seq_len, 4 * GRID_W)
    v_w = tapsum_roll("cconv_cols", u0, 0, w31, 0, seq_len=seq_len, n_seq=n_ex, row_blk_off=0, width=GRID_W,
                      piece=piece31, cb=cb31, ncb=ncb31, pad=pad31, flip=False)
    v_h = tapsum_rows("cconv_rows", u0, ncb31, w31, ncb31, seq_len=seq_len, n_seq=n_ex, cb=cb31, ncb=ncb31, pad=pad31, flip=False)
    (un,) = rowwise("cconv_ln", fn_ln_silu, [row(v_w), row(v_h)], [], [cconv_b, cconv_ln_w, cconv_ln_b], [(n_lat, d, BF16)],
                    tm=tm, n_tiles=lat_tiles)
    mix = matmul("mix_out", [(yn, wo_y), (un, wo_u)], "nn", tm=tm)
    seg_lat, first_lat = _segmenter(tm, seq_len, n_lat)
    (x2,) = rowwise("mix_resid", make_fn_resid(1.0), [row(x1), row(mix)], [lat(tabs[5])], [], [(n_lat, d, F32)],
                    tm=tm, n_tiles=lat_tiles, seg_fn=seg_lat)
    x3, saved2 = _ffn_fwd(plan, "ffn2", x2, n_lat, tm, seg_lat, lat(tabs[6]), lat(tabs[7]), lat(tabs[8]), norm_ffn2, wg2, wu2, wd2)
    d_x3, d_final, loss_vec = final_loss(x3, loss_target.reshape(n_lat, d), final_norm.reshape(1, d), tm=tm)

    shard_cols = lambda t: t.reshape(t.shape[0], N_CHIPS, -1).transpose(1, 0, 2)

    def pieces(t):
        t = jnp.pad(t, ((0, 0), (0, -t.shape[1] % 32), (0, 0)))
        return t.reshape(2 * N_CHIPS, t.shape[1] // 2, t.shape[2]).astype(BF16)

    scatter = lambda *ts: Rider([pieces(t) for t in ts], "all8", scatter=True)
    halves = lambda names, landed: Rider([sum_slots(f"sum_{nm}", r) for nm, r in zip(names, landed)], "sibling")
    swapped = {}
    plan.on("ffn2_up_dx", "sc_ffn2_down", lambda p: scatter(p.got["ffn2_d_wd"]))
    plan.on("ffn2_up_dw", "sc_ffn2_gate", lambda p: scatter(p.got["ffn2_d_wg"]))
    d_x2, (d_s6, d_s7, d_g8), d_nffn2 = _ffn_bwd(
        plan, "ffn2", d_x3, saved2, x2, n_lat, tm, seg_lat, first_lat, lat(tabs[6]), lat(tabs[7]), lat(tabs[8]), norm_ffn2,
        wg2, wu2, wd2, n_lat, None)
    d_mix, d_g5 = rowwise_bwd("mix_resid_bwd", make_fn_resid(1.0), [row(x1), row(mix)], [lat(tabs[5])], [], [[row(d_x2)]],
                              [None, (n_lat, BF16, None)], tm=tm, n_tiles=lat_tiles, seg_fn=seg_lat, first_fn=first_lat)
    d_yn = matmul("mix_out_dy", [(d_mix, wo_y)], "nt", tm=tm)
    d_un = matmul("mix_out_du", [(d_mix, wo_u)], "nt", tm=tm)
    d_wout = jnp.concatenate([matmul("mix_out_dwy", [(yn, d_mix)], "tn", out_dtype=BF16, tm=tm), matmul("mix_out_dwu", [(un, d_mix)], "tn", out_dtype=BF16, tm=tm)])
    d_vw, d_vh, d_cb, d_lnw, d_lnb = rowwise_bwd(
        "cconv_ln_bwd", fn_ln_silu, [row(v_w), row(v_h)], [], [cconv_b, cconv_ln_w, cconv_ln_b], [[row(d_un)]],
        [(n_lat, F32, None)] * 2, tm=tm, n_tiles=lat_tiles)
    d_u0 = tapsum_roll("cconv_cols_dx", d_vw, 0, w31, 0, seq_len=seq_len, n_seq=n_ex, row_blk_off=0, width=GRID_W,
                       piece=piece31, cb=cb31, ncb=ncb31, pad=pad31, flip=True, place=((n_lat, d), 0, 0, None))
    d_u0 = tapsum_rows("cconv_rows_dx", d_vh, 0, w31, ncb31, seq_len=seq_len, n_seq=n_ex, cb=cb31, ncb=ncb31, pad=pad31,
                       flip=True, place=((n_lat, d), 0, ncb31, d_u0))
    d_w31 = jnp.concatenate([
        tapgrad_roll("cconv_cols_dw", d_vw, 0, 0, u0, 0, 0, n_tap=w31.shape[0], seq_len=seq_len, n_seq=n_ex, width=GRID_W,
                     piece=piece31, cb=cb31, ncb=ncb31, pad=pad31),
        tapgrad_rows("cconv_rows_dw", d_vh, 0, u0, ncb31, n_tap=w31.shape[0], seq_len=seq_len, n_seq=n_ex, cb=cb31,
                     ncb=ncb31, pad=pad31)], axis=1)
    d_ga, d_gb = rowwise_bwd("glu_bwd", fn_glu, [row(proj, d, 1), row(proj, d, 2)], [], [], [[row(d_u0)]],
                             [(n_lat, BF16, None)] * 2, tm=tm, n_tiles=lat_tiles)
    d_ysum, d_z, d_ssmnw = rowwise_bwd(
        "ssd_gate_bwd", fn_gate, [row(y_f), row(y_b), row(proj, d, 0)], [], [ssm_norm_w], [[row(d_yn)]],
        [(n_lat, F32, None), None, (n_lat, BF16, None)], tm=tm, n_tiles=lat_tiles)
    (dxbc_f, ddt_f, dalog_f, ddtb_f, ddsk), landed = ssd_bwd(
        "ssd_bwd_f", xbc, proj, dt_cb, hs_f, d_ysum, dtb_f, alog_f, dsk_f, rev=False,
        rider=scatter(plan.got["ffn2_d_wu"], d_wout.reshape(N_CHIPS, -1, d)), **ssd)
    (dxbc_b, ddt_b, dalog_b, ddtb_b, _), both = ssd_bwd(
        "ssd_bwd_b", xbc, proj, dt_cb, hs_b, d_ysum, dtb_b, alog_b, dsk_b, rev=True,
        rider=halves(["ffn2_down", "ffn2_gate"], plan.got["sc_ffn2_down"] + plan.got["sc_ffn2_gate"]), **ssd)
    swapped.update(zip(["ffn2_down", "ffn2_gate"], both))
    (d_craw, d_conv_b), both = rowwise_bwd(
        "xbc_silu_bwd", fn_silu_bias, [row(craw)], [], [ssm_conv_b], [[row(dxbc_f), row(dxbc_b)]],
        [(n_tok, F32, None)], tm=tm, n_tiles=n_tok // tm, rider=halves(["ffn2_up", "w_out"], landed))
    swapped.update(zip(["ffn2_up", "w_out"], both))
    d_pxbc = conv5("xbc_conv_dx", d_craw, 0, True)
    g5 = lambda name, seq, off: tapgrad_roll(name, d_craw, 0, off, proj, xbc_cb, off, n_tap=w5.shape[0], seq_len=seq,
                                             n_seq=n_ex, width=seq, piece=seq, cb=cbw, ncb=xw // cbw, pad=w5.shape[0] // 2)
    d_w5 = g5("xbc_conv_lat_dw", seq_len, 0) + g5("xbc_conv_ctx_dw", ctx_len, n_lat // ctx_len)
    lat_pairs = [(d_z, w_z), (d_ga, w_ga), (d_gb, w_gb), (d_pxbc, w_xbc), (ddt_f, w_dtp), (ddt_b, w_dtp)]
    d_h2 = jnp.concatenate([matmul("mix_proj_dx_lat", lat_pairs, "nt", rows=n_lat, tm=min(tm, 256)),
                            matmul("mix_proj_dx_ctx", lat_pairs[3:], "nt", rows=n_ctx_rows, row_off=n_lat, tm=min(tm, 256))])
    d_wz = matmul("mix_proj_dwz", [(d_z, h2)], "tn", out_dtype=BF16, rows=n_lat, tm=tm)
    d_wga = matmul("mix_proj_dwa", [(d_ga, h2)], "tn", out_dtype=BF16, rows=n_lat, tm=tm)
    d_wgb = matmul("mix_proj_dwb", [(d_gb, h2)], "tn", out_dtype=BF16, rows=n_lat, tm=tm)
    d_wxbc = matmul("mix_proj_dwx", [(d_pxbc, h2)], "tn", out_dtype=BF16, tm=tm)
    d_wdt = matmul("mix_proj_dwt", [(ddt_f, h2), (ddt_b, h2)], "tn", out_dtype=BF16, tm=tm)
    d_win_t = jnp.concatenate([d_wz, d_wxbc, d_wdt[:2 * n_head], d_wga, d_wgb]).reshape(N_CHIPS, -1, d)
    d_x1, d_s3, d_s4, d_nmix = rowwise_bwd(
        "mix_norm_bwd", fn_norm_mod, [row(x1)], [tabs[3], tabs[4]], [norm_mix], [[row(d_h2)]], [(n_tok, F32, None)],
        tm=tm, n_tiles=n_tok // tm, seg_fn=seg_all, first_fn=first_all, adds={0: (row(d_x2), lat_tiles)})
    mix_names = ["w_in", "ssm_conv_w", "cconv_w"]
    plan.on("ffn1_down_dx", "sc_conv", lambda p: scatter(shard_cols(d_w5), shard_cols(d_w31)))
    plan.on("ffn1_up_dx", "sc_win", lambda p: scatter(d_win_t))
    plan.on("ffn1_gate_dw", "sc_ffn1_down", lambda p: scatter(p.got["ffn1_d_wd"]))
    plan.on("ffn1_up_dw", "sc_ffn1_gate", lambda p: scatter(p.got["ffn1_d_wg"]))
    plan.on("ffn1_up_dw", "sw_mix", lambda p: halves(mix_names, p.got["sc_win"] + p.got["sc_conv"]))
    plan.on("ffn1_norm_bwd", "sc_ffn1_up", lambda p: scatter(p.got["ffn1_d_wu"]))
    plan.on("ffn1_norm_bwd", "sw_ffn1_down", lambda p: halves(["ffn1_down"], p.got["sc_ffn1_down"]))
    d_xt, (d_s0, d_s1, d_g2), d_nffn1 = _ffn_bwd(
        plan, "ffn1", d_x1, saved1, xt, n_tok, tm, seg_all, first_all, tabs[0], tabs[1], tabs[2], norm_ffn1, wg1, wu1, wd1,
        n_lat, lat_tiles)
    swapped.update(zip(mix_names + ["ffn1_down"], plan.got["sw_mix"] + plan.got["sw_ffn1_down"]))
    last_names = ["ffn1_gate", "ffn1_up"]
    last = halves(last_names, plan.got["sc_ffn1_gate"] + plan.got["sc_ffn1_up"])
    swapped.update(zip(last_names, exchange("swap_sibling", last.arrs, "sibling")))
    grad_x = d_xt.reshape(n_ex, seq_len, d)

    with_ctx0 = lambda t: jnp.concatenate([t, jnp.zeros((1, 1, d), F32)])
    d_tabs = [d_s0, d_s1, d_g2, d_s3, d_s4, with_ctx0(d_g5), with_ctx0(d_s6), with_ctx0(d_s7), with_ctx0(d_g8)]
    d_mod_rows = jnp.concatenate([t[:, 0, :] for t in d_tabs], axis=1)
    n_pad_rows = -(-(n_ex + 1) // 8) * 8
    d_mod_rows = jnp.concatenate([d_mod_rows, jnp.zeros((n_pad_rows - n_ex - 1, 9 * d), F32)])
    small = [("loss", loss_vec), ("norm_ffn1", d_nffn1), ("norm_mix", d_nmix), ("ssm_conv_b", d_conv_b),
             ("dt_bias_fwd", ddtb_f[:, :n_head]), ("dt_bias_bwd", ddtb_b[:, :n_head]), ("a_log_fwd", dalog_f[:, :n_head]),
             ("a_log_bwd", dalog_b[:, :n_head]), ("ssm_d", ddsk[:, :n_head]), ("ssm_norm_w", d_ssmnw), ("cconv_b", d_cb),
             ("cconv_ln_w", d_lnw), ("cconv_ln_b", d_lnb), ("norm_ffn2", d_nffn2), ("final_norm", d_final)]
    n_small = sum(v.size for _, v in small)
    n_pack = -(-n_small // (8 * LANES)) * (8 * LANES)
    pack = jnp.concatenate([v.reshape(-1) for _, v in small] + [jnp.zeros((n_pack - n_small,), F32)]).reshape(-1, LANES)
    pack_all, d_mod_all = exchange("gather_small", [pack, d_mod_rows], "all8")
    pack_sum = sum_slots("small_sum", pack_all)
    loss = loss_total(pack_sum.reshape(1, n_pack), d).reshape(())
    flat_sum = pack_sum.reshape(-1)
    small_grads, pos = {}, 0
    for nm, v in small:
        small_grads[nm] = flat_sum[pos:pos + v.size]
        pos += v.size
    d_mod_all = d_mod_all.reshape(8 * n_pad_rows, 9 * d)
    cond_rows = [jnp.concatenate([cond[j * n_ex:(j + 1) * n_ex], c_ctx[None, :],
                                  jnp.zeros((n_pad_rows - n_ex - 1, d), F32)]) for j in range(8)]
    cond_bwd = jnp.concatenate(cond_rows)
    d_mod_shard = lax.dynamic_slice(d_mod_all, (0, chip * mod_w), (8 * n_pad_rows, mod_w))
    g_wmod, g_bmod, q_part = mod_bwd(cond_bwd, d_mod_shard, d_mod_all, w_mod[0],
                                     tuple(j * n_pad_rows + n_ex for j in range(8)))
    (q_all,) = exchange("gather_cctx", [q_part], "all8")
    g_cctx = cctx_grad(q_all, c_ctx.reshape(1, d))
    small_grads["c_ctx"], small_grads["b_mod"] = g_cctx.reshape(-1), g_bmod.reshape(-1)

    transposed = {"ffn1_gate", "ffn1_up", "ffn2_gate", "ffn2_up", "w_in"}
    results = {}
    for nm, both in swapped.items():
        flip = (lambda t: jnp.swapaxes(t, 1, 2)) if nm in transposed else (lambda t: t)
        shape = flip(weights[nm]).shape
        two_d = lambda t: flip(t).reshape(shape[-2], shape[-1])
        g_full = both.reshape(1, -1, shape[-1])[:, :shape[-2]]
        results[nm] = [flip(r.reshape(shape)) for r in
                       adamw(f"adamw_{nm}", two_d(weights[nm]), g_full, two_d(mom1[nm]), two_d(mom2[nm]))]
    results["w_mod"] = [r.reshape(w_mod.shape) for r in adamw("adamw_w_mod", w_mod[0], g_wmod[None], m_w_mod[0], v_w_mod[0])]
    small_names = [nm for nm in order if nm not in results]
    n_sm = sum(weights[nm].size for nm in small_names)
    n_smp = -(-n_sm // (8 * LANES)) * (8 * LANES)
    packed = lambda src: jnp.concatenate([src[nm].reshape(-1) for nm in small_names] + [jnp.zeros((n_smp - n_sm,), F32)]).reshape(-1, LANES)
    sm_out = adamw("adamw_small", packed(weights), packed(small_grads)[None], packed(mom1), packed(mom2))
    pos = 0
    for nm in small_names:
        size = weights[nm].size
        results[nm] = [r.reshape(-1)[pos:pos + size].reshape(weights[nm].shape) for r in sm_out]
        pos += size
    return (loss, grad_x, *[results[nm][0] for nm in order], *[results[nm][1] for nm in order],
            *[results[nm][2] for nm in order], *[results[nm][3] for nm in order])
```

```python
import functools
import math

import jax
import jax.numpy as jnp
from jax import lax
from jax.experimental import pallas as pl
from jax.experimental.pallas import tpu as pltpu

F32 = jnp.float32
BF16 = jnp.bfloat16
HI = lax.Precision.HIGHEST
MESH = pl.DeviceIdType.MESH

EPS = 1e-6
GRID_W = 64
HEAD_DIM = 64
N_STATE = 128
CHUNK = 128
LANES = 128
N_CHIPS = 4
ADAM_LR, ADAM_B1, ADAM_B2, ADAM_EPS, ADAM_WD, ADAM_STEP = 0.001, 0.9, 0.999, 1e-08, 0.01, 10
VMEM_CAP = 56 * 1024 * 1024


def _params(vmem_bytes=None, n_axes=1):
    kw = dict(dimension_semantics=("arbitrary",) * n_axes)
    if vmem_bytes is not None:
        kw["vmem_limit_bytes"] = int(min(VMEM_CAP, max(32 * 1024 * 1024, vmem_bytes)))
    return pltpu.CompilerParams(**kw)


def _big(shape, dtype):
    return pltpu.HBM(tuple(shape), dtype)


def _nbytes(shape, dtype):
    return math.prod(shape) * jnp.dtype(dtype).itemsize


def _row_tile(rows, width, cap_bytes=1 << 20, mult=8):
    best = None
    for t in range(mult, rows + 1, mult):
        if rows % t == 0 and t * width * 4 <= cap_bytes:
            best = t
    return best if best is not None else rows


_MODES = {"all8": (8, (1, 2, 3, 4, 5, 6, 7), 0), "chips": (4, (2, 4, 6), 1), "sibling": (2, (1,), 0)}


class Rider:
    def __init__(self, arrs, mode, scatter=False):
        self.arrs, self.scatter = list(arrs), scatter
        self.nslot, self.deltas, self.shift = _MODES[mode]
        self.n = len(self.arrs)
        self.out_shape = [jax.ShapeDtypeStruct((self.nslot,) + (a.shape[1:] if scatter else a.shape), a.dtype)
                          for a in self.arrs]
        any_spec = pl.BlockSpec(memory_space=pl.ANY)
        self.in_specs = [any_spec] * self.n
        self.out_specs = [any_spec] * self.n
        n_peer = len(self.deltas)
        self.scratch = [pltpu.SemaphoreType.DMA((self.n, n_peer)), pltpu.SemaphoreType.DMA((self.n, n_peer)),
                        pltpu.SemaphoreType.DMA((self.n,))]

    def _copies(self, ins, outs, sems, arrivals):
        send_sems, recv_sems, local_sems = sems
        x, y, c = lax.axis_index("x"), lax.axis_index("y"), lax.axis_index("c")
        me = 4 * x + 2 * y + c
        slot_of = lambda dev: (dev >> self.shift) & (self.nslot - 1)
        src = lambda a, slot: ins[a].at[slot] if self.scatter else ins[a]
        flip = lambda v, bit: 1 - v if bit else v

        def remote(a, k, d, from_slot, to_slot):
            return pltpu.make_async_remote_copy(
                src_ref=src(a, from_slot), dst_ref=outs[a].at[to_slot], send_sem=send_sems.at[a, k],
                recv_sem=recv_sems.at[a, k], device_id=(flip(x, (d >> 2) & 1), flip(y, (d >> 1) & 1), flip(c, d & 1)),
                device_id_type=MESH)

        mine = slot_of(me)
        local = [pltpu.make_async_copy(src(a, mine), outs[a].at[mine], local_sems.at[a]) for a in range(self.n)]
        sends = [remote(a, k, d, slot_of(me ^ d), mine) for k, d in enumerate(self.deltas) for a in range(self.n)]
        if not arrivals:
            return local, sends
        return local, sends, [remote(a, k, d, mine, slot_of(me ^ d)) for k, d in enumerate(self.deltas) for a in range(self.n)]

    def start(self, ins, outs, sems):
        local, sends = self._copies(ins, outs, sems, arrivals=False)
        for cp in local + sends:
            cp.start()

    def wait(self, ins, outs, sems):
        local, sends, recvs = self._copies(ins, outs, sems, arrivals=True)
        for cp in recvs:
            cp.wait_recv()
        for cp in sends:
            cp.wait_send()
        for cp in local:
            cp.wait()


class Riders:
    def __init__(self, riders):
        self.riders = list(riders)
        self.n = sum(r.n for r in self.riders)
        cat = lambda attr: [v for r in self.riders for v in getattr(r, attr)]
        self.arrs, self.out_shape, self.in_specs = cat("arrs"), cat("out_shape"), cat("in_specs")
        self.out_specs, self.scratch = cat("out_specs"), cat("scratch")

    def _each(self, method, ins, outs, sems):
        i = s = 0
        for r in self.riders:
            getattr(r, method)(ins[i:i + r.n], outs[i:i + r.n], sems[s:s + len(r.scratch)])
            i, s = i + r.n, s + len(r.scratch)

    def start(self, ins, outs, sems):
        self._each("start", ins, outs, sems)

    def wait(self, ins, outs, sems):
        self._each("wait", ins, outs, sems)


class _Hosted:
    def __init__(self, rider, n_in, n_out, n_scratch, grid):
        self.rider, self.n_in, self.n_out, self.n_scratch, self.grid = rider, n_in, n_out, n_scratch, grid
        self.n = rider.n if rider else 0

    def split(self, refs):
        a, b = self.n_in, self.n_in + self.n
        c, e = b + self.n_out, b + self.n_out + self.n
        self._r = (refs[a:b], refs[c:e], refs[e + self.n_scratch:])
        if self.rider:
            ids = [pl.program_id(ax) for ax in range(len(self.grid))]
            first = functools.reduce(jnp.logical_and, [i == 0 for i in ids]) if ids else True
            pl.when(first)(lambda: self.rider.start(*self._r))
        return refs[:a], refs[b:c], refs[e:e + self.n_scratch]

    def finish(self):
        if self.rider:
            ids = [pl.program_id(ax) for ax in range(len(self.grid))]
            last = functools.reduce(jnp.logical_and, [i == n - 1 for i, n in zip(ids, self.grid)]) if ids else True
            pl.when(last)(lambda: self.rider.wait(*self._r))

    def call_args(self, in_specs, out_shape, out_specs, scratch, args):
        r = self.rider
        if not r:
            return list(in_specs), tuple(out_shape), tuple(out_specs), list(scratch), list(args)
        return (list(in_specs) + r.in_specs, tuple(out_shape) + tuple(r.out_shape), tuple(out_specs) + tuple(r.out_specs),
                list(scratch) + r.scratch, list(args) + r.arrs)

    def results(self, res, unwrap=True):
        res = list(res) if isinstance(res, (tuple, list)) else [res]
        host = res[:self.n_out]
        host = host[0] if (self.n_out == 1 and unwrap) else tuple(host)
        return (host, res[self.n_out:]) if self.rider else host


def exchange(name, arrs, mode, scatter=False):
    rider = Rider(arrs, mode, scatter)

    def body(*refs):
        ins, outs, sems = refs[:rider.n], refs[rider.n:2 * rider.n], refs[2 * rider.n:]
        rider.start(ins, outs, sems)
        rider.wait(ins, outs, sems)

    return pl.pallas_call(
        body, name=name, out_shape=tuple(rider.out_shape), in_specs=rider.in_specs, out_specs=tuple(rider.out_specs),
        scratch_shapes=rider.scratch,
    )(*arrs)


_DIMS = {"nn": (((1,), (0,)), ((), ())), "nt": (((1,), (1,)), ((), ())), "tn": (((0,), (0,)), ((), ()))}


def matmul(name, pairs, kind, *, a_ch=False, b_ch=False, out_ch=False, out_dtype=F32, rows=None, row_off=0, tm=512,
           rider=None, post=None, fold=False):
    a0, b0 = pairs[0]
    n_chunk = a0.shape[0] if a_ch else (b0.shape[0] if b_ch else 1)
    total_rows = a0.shape[-2]
    rows = total_rows - row_off if rows is None else rows
    tm = min(tm, rows)
    assert rows % tm == 0 and row_off % tm == 0, (name, rows, tm, row_off)
    n_rt, off = rows // tm, row_off // tm
    dims = _DIMS[kind]
    n_pair = len(pairs)

    if kind == "tn":
        grid, red_axis, n_red = (n_chunk, n_rt), 1, n_rt
        a_idx = (lambda k, i: (k, i + off, 0)) if a_ch else (lambda k, i: (i + off, 0))
        b_idx = (lambda k, i: (k, i + off, 0)) if b_ch else (lambda k, i: (i + off, 0))
        a_blk = lambda a: ((None, tm, a.shape[-1]) if a_ch else (tm, a.shape[-1]))
        b_blk = lambda b: ((None, tm, b.shape[-1]) if b_ch else (tm, b.shape[-1]))
        o2 = (a0.shape[-1], b0.shape[-1])
        out_shape = ((n_chunk,) + o2) if out_ch else o2
        out_spec = pl.BlockSpec((None,) + o2, lambda k, i: (k, 0, 0)) if out_ch else pl.BlockSpec(o2, lambda k, i: (0, 0))
        acc_shape = o2
    else:
        n_out = b0.shape[-1] if kind == "nn" else b0.shape[-2]
        b2 = b0.shape[-2:]
        if a_ch and b_ch and not out_ch and fold:
            grid, red_axis, n_red = (n_rt,), None, 1
            a_idx, b_idx = (lambda i: (0, i + off, 0)), (lambda i: (0, 0, 0))
            a_blk = lambda a: (n_chunk, tm, a.shape[-1])
            b_blk = lambda b: tuple(b.shape)
            out_shape, out_spec = (rows, n_out), pl.BlockSpec((tm, n_out), lambda i: (i, 0))
        elif a_ch and b_ch and not out_ch:
            grid, red_axis, n_red = (n_rt, n_chunk), 1, n_chunk
            a_idx, b_idx = (lambda i, k: (k, i + off, 0)), (lambda i, k: (k, 0, 0))
            a_blk = lambda a: (None, tm, a.shape[-1])
            b_blk = lambda b: (None,) + tuple(b.shape[-2:])
            out_shape, out_spec = (rows, n_out), pl.BlockSpec((tm, n_out), lambda i, k: (i, 0))
        elif out_ch and fold:
            assert b_ch and not a_ch and n_pair == 1
            grid, red_axis, n_red = (n_rt,), None, 1
            a_idx, b_idx = (lambda i: (i + off, 0)), (lambda i: (0, 0, 0))
            a_blk = lambda a: (tm, a.shape[-1])
            b_blk = lambda b: tuple(b.shape)
            out_shape, out_spec = (n_chunk, rows, n_out), pl.BlockSpec((n_chunk, tm, n_out), lambda i: (0, i, 0))
        elif out_ch:
            assert b_ch and not a_ch
            grid, red_axis, n_red = (n_chunk, n_rt), None, 1
            a_idx, b_idx = (lambda k, i: (i + off, 0)), (lambda k, i: (k, 0, 0))
            a_blk = lambda a: (tm, a.shape[-1])
            b_blk = lambda b: (None,) + tuple(b.shape[-2:])
            out_shape, out_spec = (n_chunk, rows, n_out), pl.BlockSpec((None, tm, n_out), lambda k, i: (k, i, 0))
        else:
            assert not (a_ch or b_ch)
            grid, red_axis, n_red = (n_rt,), None, 1
            a_idx, b_idx = (lambda i: (i + off, 0)), (lambda i: (0, 0))
            a_blk = lambda a: (tm, a.shape[-1])
            b_blk = lambda b: tuple(b.shape)
            out_shape, out_spec = (rows, n_out), pl.BlockSpec((tm, n_out), lambda i: (i, 0))
        acc_shape = (tm, n_out)

    post_ins, post_fn, out_dtypes = ([], None, [out_dtype]) if post is None else post
    hosted = _Hosted(rider, 2 * n_pair + len(post_ins), len(out_dtypes), int(n_red > 1), grid)

    def body(*refs):
        ins, outs, scr = hosted.split(refs)

        def compute():
            acc = None
            for p in range(n_pair):
                for k in ([None] if not fold else range(n_chunk)):
                    pick = (lambda r: r[...]) if k is None else (lambda r: r[k])
                    d = lax.dot_general(pick(ins[2 * p]).astype(BF16), pick(ins[2 * p + 1]).astype(BF16), dims,
                                        preferred_element_type=F32)
                    acc = d if acc is None else acc + d
            return acc

        def emit(acc):
            vals = (acc,) if post_fn is None else post_fn(acc, *[r[...].astype(F32) for r in ins[2 * n_pair:]])
            for o_ref, v in zip(outs, vals):
                o_ref[...] = v.astype(o_ref.dtype)

        if out_ch and fold:
            a_tile = ins[0][...].astype(BF16)
            for k in range(n_chunk):
                acc = lax.dot_general(a_tile, ins[1][k].astype(BF16), dims, preferred_element_type=F32)
                vals = (acc,) if post_fn is None else post_fn(acc, *[r[k].astype(F32) for r in ins[2:]])
                for o_ref, v in zip(outs, vals):
                    o_ref[k] = v.astype(o_ref.dtype)
        elif n_red == 1:
            emit(compute())
        else:
            acc_ref = scr[0]
            r = pl.program_id(red_axis)

            @pl.when(r == 0)
            def _():
                acc_ref[...] = jnp.zeros_like(acc_ref)

            acc_ref[...] += compute()

            @pl.when(r == n_red - 1)
            def _():
                emit(acc_ref[...])
        hosted.finish()

    in_specs, args, vmem = [], [], 0
    for a, b in pairs:
        in_specs += [pl.BlockSpec(a_blk(a), a_idx), pl.BlockSpec(b_blk(b), b_idx)]
        args += [a, b]
        vmem += 2 * (_nbytes([s for s in a_blk(a) if s], a.dtype) + _nbytes([s for s in b_blk(b) if s], b.dtype))
    in_specs += [out_spec] * len(post_ins)
    args += list(post_ins)
    tiles_per_step = n_chunk if (out_ch and fold) else 1
    vmem += (3 + 2 * n_pair + tiles_per_step * (len(post_ins) + len(out_dtypes))) * _nbytes(acc_shape, F32)
    scratch = [pltpu.VMEM(acc_shape, F32)] if n_red > 1 else []
    in_specs, out_shapes, out_specs, scratch, args = hosted.call_args(
        in_specs, [_big(out_shape, dt) for dt in out_dtypes], [out_spec] * len(out_dtypes), scratch, args)
    return hosted.results(pl.pallas_call(
        body, name=name, out_shape=out_shapes, grid=grid, in_specs=in_specs, out_specs=out_specs,
        scratch_shapes=scratch, compiler_params=_params(vmem + (8 << 20), len(grid)),
    )(*args))


def row(arr, width=None, cb=0, roff=0):
    return (arr, arr.shape[-1] if width is None else width, cb, roff)


def _row_spec(desc, tm, limit=None):
    _, width, cb, roff = desc
    if limit is None:
        return pl.BlockSpec((tm, width), lambda i: (i + roff, cb))
    return pl.BlockSpec((tm, width), lambda i: (jnp.minimum(i, limit - 1) + roff, cb))


def _segmenter(tm, seq_len, n_lat):
    seg = lambda i: jnp.where(i * tm < n_lat, (i * tm) // seq_len, n_lat // seq_len)
    first = lambda i: jnp.where(i * tm < n_lat, (i * tm) % seq_len == 0, i * tm == n_lat)
    return seg, first


def rowwise(name, fn, rows, segs, params, outs, *, tm, n_tiles, seg_fn=None, rider=None):
    n_r, n_s, n_p = len(rows), len(segs), len(params)
    hosted = _Hosted(rider, n_r + n_s + n_p, len(outs), 0, (n_tiles,))

    def body(*refs):
        ins, out_refs, _ = hosted.split(refs)
        vals = [r[...].astype(F32) for r in ins[:n_r]] + [r[...] for r in ins[n_r:]]
        res = fn(*vals)
        for o_ref, v in zip(out_refs, res):
            o_ref[...] = v.astype(o_ref.dtype)
        hosted.finish()

    in_specs = [_row_spec(d, tm) for d in rows]
    in_specs += [pl.BlockSpec((None, 1, s.shape[-1]), lambda i: (seg_fn(i), 0, 0)) for s in segs]
    in_specs += [pl.BlockSpec(p.shape, lambda i: (0, 0)) for p in params]
    vmem = sum(2 * tm * d[1] * 4 for d in rows) + sum(3 * tm * w * 4 for _, w, _ in outs) + sum(2 * p.size * 4 for p in params)
    in_specs, out_shapes, out_specs, scratch, args = hosted.call_args(
        in_specs, [_big((r, w), dt) for r, w, dt in outs],
        [pl.BlockSpec((tm, w), lambda i: (i, 0)) for _, w, _ in outs], [], [d[0] for d in rows] + list(segs) + list(params))
    return hosted.results(pl.pallas_call(
        body, name=name, grid=(n_tiles,), in_specs=in_specs, out_shape=out_shapes, out_specs=out_specs,
        scratch_shapes=scratch, compiler_params=_params(2 * vmem + (8 << 20)),
    )(*args), unwrap=False)


def rowwise_bwd(name, fn, rows, segs, params, cts, row_grads, *, tm, n_tiles, seg_fn=None, first_fn=None, adds=None,
                rider=None):
    adds = adds or {}
    need = [k for k, v in enumerate(row_grads) if v is not None]
    n_r, n_s, n_p = len(rows), len(segs), len(params)
    n_ct = sum(len(lst) for lst in cts)
    add_keys = sorted(adds)
    hosted = _Hosted(rider, n_r + n_s + n_p + n_ct + len(add_keys), len(need) + n_s + n_p, 0, (n_tiles,))

    def body(*refs):
        host_in, host_out, _ = hosted.split(refs)
        it = iter(list(host_in) + list(host_out))
        row_refs = [next(it) for _ in range(n_r)]
        seg_refs = [next(it) for _ in range(n_s)]
        par_refs = [next(it) for _ in range(n_p)]
        ct_refs = [[next(it) for _ in lst] for lst in cts]
        add_refs = {k: next(it) for k in add_keys}
        rg_refs = {k: next(it) for k in need}
        sg_refs = [next(it) for _ in range(n_s)]
        pg_refs = [next(it) for _ in range(n_p)]
        i = pl.program_id(0)
        rv = [r[...].astype(F32) for r in row_refs]
        sv = [r[...] for r in seg_refs]
        pv = [r[...] for r in par_refs]

        def f(*args):
            rr = list(rv)
            for j, k in enumerate(need):
                rr[k] = args[j]
            return fn(*rr, *args[len(need):])

        _, vjp = jax.vjp(f, *[rv[k] for k in need], *sv, *pv)
        ctv = []
        for lst in ct_refs:
            acc = lst[0][...].astype(F32)
            for r in lst[1:]:
                acc = acc + r[...].astype(F32)
            ctv.append(acc)
        g = vjp(tuple(ctv))
        for j, k in enumerate(need):
            gv = g[j]
            if k in adds:
                lim = adds[k][1]
                av = add_refs[k][...].astype(F32)
                gv = gv + (av if lim is None else jnp.where(i < lim, av, 0.0))
            lim = row_grads[k][2]
            if lim is None:
                rg_refs[k][...] = gv.astype(rg_refs[k].dtype)
            else:
                @pl.when(i < lim)
                def _(gv=gv, k=k):
                    rg_refs[k][...] = gv.astype(rg_refs[k].dtype)
        if n_s:
            opens = first_fn(i)
            for ref, gv in zip(sg_refs, g[len(need):len(need) + n_s]):
                @pl.when(opens)
                def _(ref=ref, gv=gv):
                    ref[...] = gv

                @pl.when(jnp.logical_not(opens))
                def _(ref=ref, gv=gv):
                    ref[...] += gv
        for ref, gv in zip(pg_refs, g[len(need) + n_s:]):
            @pl.when(i == 0)
            def _(ref=ref, gv=gv):
                ref[...] = gv

            @pl.when(i > 0)
            def _(ref=ref, gv=gv):
                ref[...] += gv
        hosted.finish()

    seg_spec = lambda s: pl.BlockSpec((None, 1, s.shape[-1]), lambda i: (seg_fn(i), 0, 0))
    par_spec = lambda p: pl.BlockSpec(p.shape, lambda i: (0, 0))
    in_specs = [_row_spec(d, tm) for d in rows] + [seg_spec(s) for s in segs] + [par_spec(p) for p in params]
    args = [d[0] for d in rows] + list(segs) + list(params)
    for lst in cts:
        in_specs += [_row_spec(d, tm) for d in lst]
        args += [d[0] for d in lst]
    for k in add_keys:
        in_specs.append(_row_spec(adds[k][0], tm, adds[k][1]))
        args.append(adds[k][0][0])
    out_shape, out_specs = [], []
    for k in need:
        n_rows, dt, lim = row_grads[k]
        out_shape.append(_big((n_rows, rows[k][1]), dt))
        out_specs.append(_row_spec((None, rows[k][1], 0, 0), tm, lim))
    for s in segs:
        out_shape.append(jax.ShapeDtypeStruct(s.shape, F32))
        out_specs.append(seg_spec(s))
    for p in params:
        out_shape.append(jax.ShapeDtypeStruct(p.shape, F32))
        out_specs.append(par_spec(p))
    vmem = sum(tm * d[1] * 4 for d in rows) * 6 + n_ct * tm * max(d[1] for d in rows) * 8
    in_specs, out_shape, out_specs, scratch, args = hosted.call_args(in_specs, out_shape, out_specs, [], args)
    return hosted.results(pl.pallas_call(
        body, name=name, grid=(n_tiles,), in_specs=in_specs, out_shape=out_shape, out_specs=out_specs,
        scratch_shapes=scratch, compiler_params=_params(vmem + (8 << 20)),
    )(*args), unwrap=False)


def _silu(v):
    return v * jax.nn.sigmoid(v)


def _rms(v, w):
    return v * lax.rsqrt(jnp.mean(v * v, axis=-1, keepdims=True) + EPS) * w


def fn_norm_mod(x, shift, scale, w):
    return (_rms(x, w) * (1.0 + scale) + shift,)


def fn_act(g, u):
    return (_silu(g) * u,)


def make_fn_resid(coef):
    def fn(x, f, gate):
        return (x + coef * gate * f,)
    return fn


def fn_silu_bias(v, b):
    return (_silu(v + b),)


def make_fn_gate_groupnorm(width):
    half = width // 2

    def fn(yf, yb, z, w):
        y = (yf + yb) * _silu(z)
        lane = lax.broadcasted_iota(jnp.int32, y.shape, 1)
        lo = lane < half
        sq = y * y
        s_lo = jnp.sum(jnp.where(lo, sq, 0.0), axis=-1, keepdims=True)
        s_hi = jnp.sum(jnp.where(lo, 0.0, sq), axis=-1, keepdims=True)
        r = jnp.where(lo, lax.rsqrt(s_lo / half + EPS), lax.rsqrt(s_hi / half + EPS))
        return (y * r * w,)
    return fn


def fn_glu(a, b):
    return (a * jax.nn.sigmoid(b),)


def fn_ln_silu(vw, vh, cb, lw, lb):
    v = jnp.concatenate([vw, vh], axis=-1) + cb
    mu = jnp.mean(v, axis=-1, keepdims=True)
    var = jnp.mean(jnp.square(v - mu), axis=-1, keepdims=True)
    return (_silu((v - mu) * lax.rsqrt(var + EPS) * lw + lb),)


def _col_tile(width):
    return width // 3 if width % (3 * LANES) == 0 else width


def mod_fwd(a_rows, w_shard, b_shard):
    n, d = a_rows.shape
    ws = w_shard.shape[1]
    tn = _col_tile(ws)

    def body(a_ref, w_ref, b_ref, o_ref):
        a = _silu(a_ref[...]).astype(BF16)
        o_ref[...] = jnp.dot(a, w_ref[...].astype(BF16), preferred_element_type=F32) + b_ref[...]

    return pl.pallas_call(
        body, name="mod_fwd", grid=(ws // tn,), out_shape=jax.ShapeDtypeStruct((n, ws), F32),
        in_specs=[pl.BlockSpec((n, d), lambda j: (0, 0)), pl.BlockSpec((d, tn), lambda j: (0, j)),
                  pl.BlockSpec((1, tn), lambda j: (0, j))],
        out_specs=pl.BlockSpec((n, tn), lambda j: (0, j)), compiler_params=_params(),
    )(a_rows, w_shard, b_shard)


def mod_bwd(a_rows, d_shard, d_full, w_shard, ctx_rows):
    n, d = a_rows.shape
    ws = w_shard.shape[1]
    tn = _col_tile(ws)
    n_ct = ws // tn

    def body(a_ref, ds_ref, df_ref, w_ref, gw_ref, gb_ref, q_ref):
        j = pl.program_id(0)
        a = _silu(a_ref[...])
        ds = ds_ref[...]
        gw_ref[...] = lax.dot_general(a, ds, _DIMS["tn"], precision=HI, preferred_element_type=F32)
        dctx = ds[ctx_rows[0]:ctx_rows[0] + 1, :]
        for r in ctx_rows[1:]:
            dctx = dctx + ds[r:r + 1, :]
        q = lax.dot_general(jnp.broadcast_to(dctx, (8, tn)), w_ref[...], _DIMS["nt"], precision=HI,
                            preferred_element_type=F32)

        @pl.when(j == 0)
        def _():
            q_ref[...] = q
            df = df_ref[...]
            acc = df[0:1, :]
            for r in range(1, n):
                acc = acc + df[r:r + 1, :]
            gb_ref[...] = acc

        @pl.when(j > 0)
        def _():
            q_ref[...] += q

    return pl.pallas_call(
        body, name="mod_bwd", grid=(n_ct,),
        out_shape=(jax.ShapeDtypeStruct((d, ws), F32), jax.ShapeDtypeStruct((1, d_full.shape[1]), F32),
                   jax.ShapeDtypeStruct((8, d), F32)),
        in_specs=[pl.BlockSpec((n, d), lambda j: (0, 0)), pl.BlockSpec((n, tn), lambda j: (0, j)),
                  pl.BlockSpec(d_full.shape, lambda j: (0, 0)), pl.BlockSpec((d, tn), lambda j: (0, j))],
        out_specs=(pl.BlockSpec((d, tn), lambda j: (0, j)), pl.BlockSpec((1, d_full.shape[1]), lambda j: (0, 0)),
                   pl.BlockSpec((8, d), lambda j: (0, 0))),
        compiler_params=_params(40 << 20),
    )(a_rows, d_shard, d_full, w_shard)


def _shifted(xs, d, tok, width):
    if d == 0:
        return xs
    n = xs.shape[0]
    sh = pltpu.roll(xs, (-d) % n, axis=0)
    return jnp.where((tok + d >= 0) & (tok + d < width), sh, 0.0)


def _placed(out_shape, place):
    if place is None:
        return out_shape, 0, 0, None
    return place


def tapsum_roll(name, x, xcb, w, wcb, *, seq_len, n_seq, row_blk_off, width, piece, cb, ncb, pad, flip, place=None):
    n_tap = w.shape[0]
    n_piece = seq_len // piece
    out_shape, o_rb, o_cb, into = _placed((n_seq * seq_len, ncb * cb), place)

    masked_weights = n_piece > 1

    def body(x_ref, w_ref, *rest):
        o_ref = rest[-2] if masked_weights else rest[-1]
        wv = w_ref[...]
        tok = lax.broadcasted_iota(jnp.int32, (piece, 1), 0) % width
        shift = lambda k: pad - k if flip else k - pad
        if masked_weights:
            wm = rest[-1]
            for k in range(n_tap):
                wm[k] = jnp.where((tok + shift(k) >= 0) & (tok + shift(k) < width), wv[k:k + 1, :], 0.0)

        def do_piece(p, carry):
            start = pl.multiple_of(p * piece, piece)
            xs = x_ref[pl.ds(start, piece), :]
            acc = jnp.zeros_like(xs)
            for k in range(n_tap):
                d = shift(k)
                if masked_weights:
                    acc = acc + wm[k] * (xs if d == 0 else pltpu.roll(xs, (-d) % piece, axis=0))
                else:
                    acc = acc + wv[k:k + 1, :] * _shifted(xs, d, tok, width)
            o_ref[pl.ds(start, piece), :] = acc
            return carry

        lax.fori_loop(0, n_piece, do_piece, 0)

    extra = [] if into is None else [into]
    return pl.pallas_call(
        body, name=name, grid=(ncb, n_seq), out_shape=_big(out_shape, F32),
        in_specs=[pl.BlockSpec((seq_len, cb), lambda j, s: (row_blk_off + s, xcb + j)),
                  pl.BlockSpec((n_tap, cb), lambda j, s: (0, wcb + j))] + [pl.BlockSpec(memory_space=pl.ANY)] * len(extra),
        out_specs=pl.BlockSpec((seq_len, cb), lambda j, s: (o_rb + s, o_cb + j)),
        input_output_aliases={2: 0} if extra else {},
        scratch_shapes=[pltpu.VMEM((n_tap, piece, cb), F32)] if masked_weights else [],
        compiler_params=_params(8 * seq_len * cb * 4 + n_tap * piece * cb * 4 + (8 << 20), 2),
    )(x, w, *extra)


def tapgrad_roll(name, dy, dycb, dy_blk_off, x, xcb, x_blk_off, *, n_tap, seq_len, n_seq, width, piece, cb, ncb, pad):
    n_piece = seq_len // piece

    def body(dy_ref, x_ref, o_ref):
        @pl.when(pl.program_id(1) == 0)
        def _():
            o_ref[...] = jnp.zeros_like(o_ref)

        tok = lax.broadcasted_iota(jnp.int32, (piece, 1), 0) % width

        def do_piece(p, carry):
            start = pl.multiple_of(p * piece, piece)
            xs = x_ref[pl.ds(start, piece), :]
            dv = dy_ref[pl.ds(start, piece), :]
            for k in range(n_tap):
                o_ref[k:k + 1, :] += jnp.sum(dv * _shifted(xs, k - pad, tok, width), axis=0, keepdims=True)
            return carry

        lax.fori_loop(0, n_piece, do_piece, 0)

    return pl.pallas_call(
        body, name=name, grid=(ncb, n_seq), out_shape=jax.ShapeDtypeStruct((n_tap, ncb * cb), F32),
        in_specs=[pl.BlockSpec((seq_len, cb), lambda j, s: (dy_blk_off + s, dycb + j)),
                  pl.BlockSpec((seq_len, cb), lambda j, s: (x_blk_off + s, xcb + j))],
        out_specs=pl.BlockSpec((n_tap, cb), lambda j, s: (0, j)),
        compiler_params=_params(8 * seq_len * cb * 4 + (8 << 20), 2),
    )(dy, x)


def tapsum_rows(name, x, xcb, w, wcb, *, seq_len, n_seq, cb, ncb, pad, flip, place=None):
    n_tap = w.shape[0]
    n_row = seq_len // GRID_W
    halo = pad * GRID_W
    out_shape, o_rb, o_cb, into = _placed((n_seq * seq_len, ncb * cb), place)

    def body(x_ref, w_ref, *rest):
        o_ref, xp = rest[-2:]
        xp[pl.ds(0, halo), :] = jnp.zeros((halo, cb), F32)
        xp[pl.ds(halo + seq_len, halo), :] = jnp.zeros((halo, cb), F32)
        xp[pl.ds(halo, seq_len), :] = x_ref[...]
        wv = w_ref[...]

        def do_row(r, carry):
            acc = jnp.zeros((GRID_W, cb), F32)
            for k in range(n_tap):
                d = pad - k if flip else k - pad
                acc = acc + wv[k:k + 1, :] * xp[pl.ds(pl.multiple_of((r + pad + d) * GRID_W, GRID_W), GRID_W), :]
            o_ref[pl.ds(pl.multiple_of(r * GRID_W, GRID_W), GRID_W), :] = acc
            return carry

        lax.fori_loop(0, n_row, do_row, 0)

    extra = [] if into is None else [into]
    return pl.pallas_call(
        body, name=name, grid=(ncb, n_seq), out_shape=_big(out_shape, F32),
        in_specs=[pl.BlockSpec((seq_len, cb), lambda j, s: (s, xcb + j)),
                  pl.BlockSpec((n_tap, cb), lambda j, s: (0, wcb + j))] + [pl.BlockSpec(memory_space=pl.ANY)] * len(extra),
        out_specs=pl.BlockSpec((seq_len, cb), lambda j, s: (o_rb + s, o_cb + j)),
        input_output_aliases={2: 0} if extra else {},
        scratch_shapes=[pltpu.VMEM((seq_len + 2 * halo, cb), F32)],
        compiler_params=_params(10 * seq_len * cb * 4 + (8 << 20), 2),
    )(x, w, *extra)


def tapgrad_rows(name, dy, dycb, x, xcb, *, n_tap, seq_len, n_seq, cb, ncb, pad):
    n_row = seq_len // GRID_W
    halo = pad * GRID_W

    def body(dy_ref, x_ref, o_ref, xp):
        @pl.when(pl.program_id(1) == 0)
        def _():
            o_ref[...] = jnp.zeros_like(o_ref)

        xp[pl.ds(0, halo), :] = jnp.zeros((halo, cb), F32)
        xp[pl.ds(halo + seq_len, halo), :] = jnp.zeros((halo, cb), F32)
        xp[pl.ds(halo, seq_len), :] = x_ref[...]

        def do_row(r, carry):
            dv = dy_ref[pl.ds(pl.multiple_of(r * GRID_W, GRID_W), GRID_W), :]
            for k in range(n_tap):
                xs = xp[pl.ds(pl.multiple_of((r + k) * GRID_W, GRID_W), GRID_W), :]
                o_ref[k:k + 1, :] += jnp.sum(dv * xs, axis=0, keepdims=True)
            return carry

        lax.fori_loop(0, n_row, do_row, 0)

    return pl.pallas_call(
        body, name=name, grid=(ncb, n_seq), out_shape=jax.ShapeDtypeStruct((n_tap, ncb * cb), F32),
        in_specs=[pl.BlockSpec((seq_len, cb), lambda j, s: (s, dycb + j)),
                  pl.BlockSpec((seq_len, cb), lambda j, s: (s, xcb + j))],
        out_specs=pl.BlockSpec((n_tap, cb), lambda j, s: (0, j)),
        scratch_shapes=[pltpu.VMEM((seq_len + 2 * halo, cb), F32)],
        compiler_params=_params(10 * seq_len * cb * 4 + (8 << 20), 2),
    )(dy, x)


def _ssd_blocks(b, s, *, rev, n_ctx, n_lat, lat_blocks):
    if rev:
        return jnp.where(s < n_ctx, lat_blocks + b * n_ctx + (n_ctx - 1 - s), b * n_lat + (n_lat - 1 - (s - n_ctx)))
    return jnp.where(s < n_ctx, lat_blocks + b * n_ctx + s, b * n_lat + (s - n_ctx))


def _ssd_common(xbc, raw, dtb, alog, dsk, *, rev, ds, n_head):
    if rev:
        raw = pltpu.roll(raw, LANES - n_head, axis=1)
    pre = raw + dtb
    dt = jnp.maximum(pre, 0.0) + jnp.log1p(jnp.exp(-jnp.abs(pre)))
    sig = jax.nn.sigmoid(pre)
    a = -jnp.exp(alog)
    da = dt * a
    ri = lax.broadcasted_iota(jnp.int32, (CHUNK, CHUNK), 0)
    ci = lax.broadcasted_iota(jnp.int32, (CHUNK, CHUNK), 1)
    mask = (ci >= ri) if rev else (ci <= ri)
    tri = mask.astype(F32)
    tri_t = ((ci <= ri) if rev else (ci >= ri)).astype(F32)
    cs = jnp.dot(tri, da, precision=HI, preferred_element_type=F32)
    tot = jnp.sum(da, axis=0, keepdims=True)
    def wide(v):
        first = lax.broadcasted_iota(jnp.int32, (v.shape[0], LANES), 1) < HEAD_DIM
        return jnp.concatenate(
            [jnp.where(first, jnp.broadcast_to(v[:, 2 * p:2 * p + 1], first.shape),
                       jnp.broadcast_to(v[:, 2 * p + 1:2 * p + 2], first.shape)) for p in range(n_head // 2)], axis=1)

    cs_w, tot_w = wide(cs), wide(tot)
    xh = xbc[:, :ds]
    dt_w = wide(dt)
    return dict(
        dt=dt, sig=sig, a=a, cs=cs, cs_t=cs.T, tot=tot, mask=mask, tri_t=tri_t,
        e_w=jnp.exp(cs_w), wt_w=jnp.exp(tot_w - cs_w), dec_w=jnp.exp(tot_w), dt_w=dt_w, dsk_w=wide(dsk),
        xh=xh, xs_w=xh * dt_w, bm=xbc[:, ds:ds + 2 * N_STATE], cm=xbc[:, ds + 2 * N_STATE:ds + 4 * N_STATE])


def _decay(q, col):
    seg = q["cs"][:, col:col + 1] - q["cs_t"][col:col + 1, :]
    return jnp.exp(jnp.where(q["mask"], seg, -jnp.inf))


def _split_heads(v):
    lane = lax.broadcasted_iota(jnp.int32, v.shape, 1)
    return jnp.concatenate([jnp.where(lane < HEAD_DIM, v, 0.0), jnp.where(lane >= HEAD_DIM, v, 0.0)], axis=0)


def ssd_fwd(name, xbc, proj, dt_cb, dtb, alog, dsk, *, rev, n_ex, seq_len, ctx_len, ds, rider=None):
    n_head, half = ds // HEAD_DIM, ds // 2
    n_ctx, n_lat = ctx_len // CHUNK, seq_len // CHUNK
    n_step = n_ctx + n_lat
    blk = functools.partial(_ssd_blocks, rev=rev, n_ctx=n_ctx, n_lat=n_lat, lat_blocks=n_ex * n_lat)
    xw = xbc.shape[1]

    def y_blk(b, s):
        sl = jnp.maximum(s, n_ctx) - n_ctx
        return b * n_lat + ((n_lat - 1 - sl) if rev else sl)

    hosted = _Hosted(rider, 5, 2, 1, (n_ex, n_step))

    def body(*refs):
        (xbc_ref, dt_ref, dtb_ref, alog_ref, dsk_ref), (y_ref, hs_ref), (h_scr,) = hosted.split(refs)

        @pl.when(pl.program_id(1) == 0)
        def _():
            h_scr[...] = jnp.zeros_like(h_scr)

        q = _ssd_common(xbc_ref[...], dt_ref[...], dtb_ref[...], alog_ref[...], dsk_ref[...], rev=rev, ds=ds, n_head=n_head)
        h = h_scr[...]
        hs_ref[...] = h
        for g in range(2):
            lo = g * half
            bg = q["bm"][:, g * N_STATE:(g + 1) * N_STATE].astype(BF16)
            cg = q["cm"][:, g * N_STATE:(g + 1) * N_STATE].astype(BF16)
            scores = lax.dot_general(cg, bg, _DIMS["nt"], preferred_element_type=F32)
            hg = h[:, lo:lo + half]
            off = jnp.dot(cg, hg.astype(BF16), preferred_element_type=F32)
            for j in range(half // LANES):
                c0 = (lo + j * LANES) // HEAD_DIM
                ln = slice(lo + j * LANES, lo + (j + 1) * LANES)
                p_cat = jnp.concatenate([scores * _decay(q, c0), scores * _decay(q, c0 + 1)], axis=1).astype(BF16)
                diag = jnp.dot(p_cat, _split_heads(q["xs_w"][:, ln]).astype(BF16), preferred_element_type=F32)
                y_ref[:, ln] = (diag + q["e_w"][:, ln] * off[:, j * LANES:(j + 1) * LANES]
                                + q["dsk_w"][:, ln] * q["xh"][:, ln])
            v = (q["wt_w"][:, lo:lo + half] * q["xs_w"][:, lo:lo + half]).astype(BF16)
            h_scr[:, lo:lo + half] = (q["dec_w"][:, lo:lo + half] * hg
                                      + lax.dot_general(bg, v, _DIMS["tn"], preferred_element_type=F32))
        hosted.finish()

    vec = pl.BlockSpec((1, LANES), lambda b, s: (0, 0))
    in_specs, out_shape, out_specs, scratch, args = hosted.call_args(
        [pl.BlockSpec((CHUNK, xw), lambda b, s: (blk(b, s), 0)),
         pl.BlockSpec((CHUNK, LANES), lambda b, s: (blk(b, s), dt_cb)), vec, vec, vec],
        (_big((n_ex * seq_len, ds), F32), _big((n_ex, n_step, N_STATE, ds), F32)),
        (pl.BlockSpec((CHUNK, ds), lambda b, s: (y_blk(b, s), 0)),
         pl.BlockSpec((None, None, N_STATE, ds), lambda b, s: (b, s, 0, 0))),
        [pltpu.VMEM((N_STATE, ds), F32)], [xbc, proj, dtb, alog, dsk])
    return hosted.results(pl.pallas_call(
        body, name=name, grid=(n_ex, n_step), out_shape=out_shape, in_specs=in_specs, out_specs=out_specs,
        scratch_shapes=scratch, compiler_params=_params(40 << 20, 2),
    )(*args))


def ssd_bwd(name, xbc, proj, dt_cb, hs, dy, dtb, alog, dsk, *, rev, n_ex, seq_len, ctx_len, ds, rider=None):
    n_head, half = ds // HEAD_DIM, ds // 2
    n_ctx, n_lat = ctx_len // CHUNK, seq_len // CHUNK
    n_step = n_ctx + n_lat
    n_tok = n_ex * (seq_len + ctx_len)
    blk0 = functools.partial(_ssd_blocks, rev=rev, n_ctx=n_ctx, n_lat=n_lat, lat_blocks=n_ex * n_lat)
    step = lambda sp: n_step - 1 - sp
    blk = lambda b, sp: blk0(b, step(sp))
    xw = xbc.shape[1]

    def dy_blk(b, sp):
        sl = jnp.maximum(step(sp), n_ctx) - n_ctx
        return b * n_lat + ((n_lat - 1 - sl) if rev else sl)

    hosted = _Hosted(rider, 7, 5, 1, (n_ex, n_step))

    def body(*refs):
        ((xbc_ref, dt_ref, hs_ref, dy_ref, dtb_ref, alog_ref, dsk_ref),
         (dxbc_ref, ddt_ref, dalog_ref, ddtb_ref, ddsk_ref), (dh_scr,)) = hosted.split(refs)
        b, sp = pl.program_id(0), pl.program_id(1)

        @pl.when(sp == 0)
        def _():
            dh_scr[...] = jnp.zeros_like(dh_scr)

        @pl.when((sp == 0) & (b == 0))
        def _():
            dalog_ref[...] = jnp.zeros_like(dalog_ref)
            ddtb_ref[...] = jnp.zeros_like(ddtb_ref)
            ddsk_ref[...] = jnp.zeros_like(ddsk_ref)

        q = _ssd_common(xbc_ref[...], dt_ref[...], dtb_ref[...], alog_ref[...], dsk_ref[...], rev=rev, ds=ds, n_head=n_head)
        h = hs_ref[...]
        d_y = jnp.where(step(sp) >= n_ctx, dy_ref[...], 0.0)
        dh_next = dh_scr[...]
        lane_row = lax.broadcasted_iota(jnp.int32, (1, LANES), 1)
        d_cs = jnp.zeros((CHUNK, LANES), F32)
        dxs_parts, de_parts, dwt_parts, ddec_parts = [], [], [], []
        for g in range(2):
            lo = g * half
            gs = slice(lo, lo + half)
            bg = q["bm"][:, g * N_STATE:(g + 1) * N_STATE].astype(BF16)
            cg = q["cm"][:, g * N_STATE:(g + 1) * N_STATE].astype(BF16)
            scores = lax.dot_general(cg, bg, _DIMS["nt"], preferred_element_type=F32)
            hg, dyg, dhn = h[:, gs], d_y[:, gs], dh_next[:, gs]
            off = jnp.dot(cg, hg.astype(BF16), preferred_element_type=F32)
            d_off = (q["e_w"][:, gs] * dyg).astype(BF16)
            de_parts.append(dyg * off)
            d_c = lax.dot_general(d_off, hg.astype(BF16), _DIMS["nt"], preferred_element_type=F32)
            dh_scr[:, gs] = (lax.dot_general(cg, d_off, _DIMS["tn"], preferred_element_type=F32)
                             + q["dec_w"][:, gs] * dhn)
            b_dh = jnp.dot(bg, dhn.astype(BF16), preferred_element_type=F32)
            v = q["wt_w"][:, gs] * q["xs_w"][:, gs]
            d_b = lax.dot_general(v.astype(BF16), dhn.astype(BF16), _DIMS["nt"], preferred_element_type=F32)
            dwt_parts.append(q["xs_w"][:, gs] * b_dh)
            ddec_parts.append(jnp.sum(hg * dhn, axis=0, keepdims=True))
            d_scores = jnp.zeros((CHUNK, CHUNK), F32)
            for j in range(half // LANES):
                c0 = (lo + j * LANES) // HEAD_DIM
                ln = slice(lo + j * LANES, lo + (j + 1) * LANES)
                l0, l1 = _decay(q, c0), _decay(q, c0 + 1)
                p0, p1 = scores * l0, scores * l1
                dy_st = _split_heads(d_y[:, ln]).astype(BF16)
                d_p = lax.dot_general(dy_st, q["xs_w"][:, ln].astype(BF16), _DIMS["nt"], preferred_element_type=F32)
                d_p0, d_p1 = d_p[:CHUNK], d_p[CHUNK:]
                d_scores = d_scores + d_p0 * l0 + d_p1 * l1
                for col, t in ((c0, d_p0 * p0), (c0 + 1, d_p1 * p1)):
                    d_cs = d_cs + jnp.sum(t - t.T, axis=1, keepdims=True) * (lane_row == col).astype(F32)
                p_st = jnp.concatenate([p0, p1], axis=0).astype(BF16)
                dxs_parts.append(lax.dot_general(p_st, dy_st, _DIMS["tn"], preferred_element_type=F32)
                                 + q["wt_w"][:, ln] * b_dh[:, j * LANES:(j + 1) * LANES])
            d_sc = d_scores.astype(BF16)
            d_c = d_c + jnp.dot(d_sc, bg, preferred_element_type=F32)
            d_b = d_b + lax.dot_general(d_sc, cg, _DIMS["tn"], preferred_element_type=F32)
            dxbc_ref[:, ds + g * N_STATE:ds + (g + 1) * N_STATE] = d_b
            dxbc_ref[:, ds + (2 + g) * N_STATE:ds + (3 + g) * N_STATE] = d_c
        d_xs = jnp.concatenate(dxs_parts, axis=1)
        narrow_m = (lax.broadcasted_iota(jnp.int32, (ds, LANES), 0) // HEAD_DIM
                    == lax.broadcasted_iota(jnp.int32, (ds, LANES), 1)).astype(BF16)
        rows8 = lambda v: jnp.broadcast_to(v, (8, ds))
        stacked = jnp.concatenate(
            [jnp.concatenate(dwt_parts, axis=1), jnp.concatenate(de_parts, axis=1), d_xs * q["xh"],
             rows8(jnp.concatenate(ddec_parts, axis=1)), rows8(jnp.sum(d_y * q["xh"], axis=0, keepdims=True))], axis=0)
        hi = stacked.astype(BF16)
        lo = (stacked - hi.astype(F32)).astype(BF16)
        sums = (jnp.dot(hi, narrow_m, preferred_element_type=F32) + jnp.dot(lo, narrow_m, preferred_element_type=F32))
        n_wt, n_e, n_xs = sums[:CHUNK], sums[CHUNK:2 * CHUNK], sums[2 * CHUNK:3 * CHUNK]
        n_dec, n_dsk = sums[3 * CHUNK:3 * CHUNK + 1], sums[3 * CHUNK + 8:3 * CHUNK + 9]
        e, wt, dec = jnp.exp(q["cs"]), jnp.exp(q["tot"] - q["cs"]), jnp.exp(q["tot"])
        d_wt = n_wt * wt
        d_cs = d_cs + n_e * e - d_wt
        d_tot = jnp.sum(d_wt, axis=0, keepdims=True) + n_dec * dec
        d_da = jnp.dot(q["tri_t"], d_cs, precision=HI, preferred_element_type=F32) + d_tot
        d_dt = d_da * q["a"] + n_xs
        dxbc_ref[:, :ds] = d_xs * q["dt_w"] + q["dsk_w"] * d_y
        dalog_ref[...] += jnp.sum(d_da * q["dt"], axis=0, keepdims=True) * q["a"]
        d_raw = d_dt * q["sig"]
        ddtb_ref[...] += jnp.sum(d_raw, axis=0, keepdims=True)
        ddsk_ref[...] += n_dsk
        ddt_ref[...] = pltpu.roll(d_raw, n_head, axis=1) if rev else d_raw
        hosted.finish()

    vec = pl.BlockSpec((1, LANES), lambda b, s: (0, 0))
    vec_shape = jax.ShapeDtypeStruct((1, LANES), F32)
    in_specs, out_shape, out_specs, scratch, args = hosted.call_args(
        [pl.BlockSpec((CHUNK, xw), lambda b, s: (blk(b, s), 0)),
         pl.BlockSpec((CHUNK, LANES), lambda b, s: (blk(b, s), dt_cb)),
         pl.BlockSpec((None, None, N_STATE, ds), lambda b, s: (b, step(s), 0, 0)),
         pl.BlockSpec((CHUNK, ds), lambda b, s: (dy_blk(b, s), 0)), vec, vec, vec],
        (_big((n_tok, xw), F32), _big((n_tok, LANES), F32), vec_shape, vec_shape, vec_shape),
        (pl.BlockSpec((CHUNK, xw), lambda b, s: (blk(b, s), 0)),
         pl.BlockSpec((CHUNK, LANES), lambda b, s: (blk(b, s), 0)), vec, vec, vec),
        [pltpu.VMEM((N_STATE, ds), F32)], [xbc, proj, hs, dy, dtb, alog, dsk])
    return hosted.results(pl.pallas_call(
        body, name=name, grid=(n_ex, n_step), out_shape=out_shape, in_specs=in_specs, out_specs=out_specs,
        scratch_shapes=scratch, compiler_params=_params(48 << 20, 2),
    )(*args))


def final_loss(x3, target, w, *, tm):
    n, d = x3.shape

    def body(x_ref, t_ref, w_ref, dx_ref, dw_ref, loss_ref):
        i = pl.program_id(0)
        t = t_ref[...]

        def per_feature(xv, wv):
            err = _rms(xv, wv) - t
            return 0.5 * jnp.sum(err * err, axis=0, keepdims=True) / d

        lv, vjp = jax.vjp(per_feature, x_ref[...], w_ref[...])
        dx, dw = vjp(jnp.ones_like(lv))
        dx_ref[...] = dx

        @pl.when(i == 0)
        def _():
            dw_ref[...] = dw
            loss_ref[...] = lv

        @pl.when(i > 0)
        def _():
            dw_ref[...] += dw
            loss_ref[...] += lv

    tile = pl.BlockSpec((tm, d), lambda i: (i, 0))
    vec = pl.BlockSpec((1, d), lambda i: (0, 0))
    return pl.pallas_call(
        body, name="final_loss", grid=(n // tm,), in_specs=[tile, tile, vec],
        out_shape=(jax.ShapeDtypeStruct((n, d), F32), jax.ShapeDtypeStruct((1, d), F32), jax.ShapeDtypeStruct((1, d), F32)),
        out_specs=(tile, vec, vec), compiler_params=_params(tm * d * 4 * 16 + (8 << 20)),
    )(x3, target, w)


def sum_slots(name, arr):
    n_slot, n_row, width = arr.shape
    tm = _row_tile(n_row, width * n_slot, mult=16)

    def body(a_ref, o_ref):
        acc = a_ref[0].astype(F32)
        for j in range(1, n_slot):
            acc = acc + a_ref[j].astype(F32)
        o_ref[...] = acc

    return pl.pallas_call(
        body, name=name, grid=(n_row // tm,), out_shape=jax.ShapeDtypeStruct((n_row, width), F32),
        in_specs=[pl.BlockSpec((n_slot, tm, width), lambda i: (0, i, 0))],
        out_specs=pl.BlockSpec((tm, width), lambda i: (i, 0)), compiler_params=_params(),
    )(arr)


def adamw(name, w, g_slots, m, v):
    n_slot, n_row, width = g_slots.shape
    tm = _row_tile(n_row, width * 2)

    def body(w_ref, g_ref, m_ref, v_ref, go_ref, d_ref, mo_ref, vo_ref):
        g = g_ref[0]
        for j in range(1, n_slot):
            g = g + g_ref[j]
        m2 = ADAM_B1 * m_ref[...] + (1.0 - ADAM_B1) * g
        v2 = ADAM_B2 * v_ref[...] + (1.0 - ADAM_B2) * jnp.square(g)
        m_hat = m2 / (1.0 - ADAM_B1 ** ADAM_STEP)
        v_hat = v2 / (1.0 - ADAM_B2 ** ADAM_STEP)
        go_ref[...] = g
        d_ref[...] = -ADAM_LR * (m_hat / (jnp.sqrt(v_hat) + ADAM_EPS) + ADAM_WD * w_ref[...])
        mo_ref[...] = m2
        vo_ref[...] = v2

    tile = pl.BlockSpec((tm, width), lambda i: (i, 0))
    shape = jax.ShapeDtypeStruct((n_row, width), F32)
    return pl.pallas_call(
        body, name=name, grid=(n_row // tm,), out_shape=(shape,) * 4,
        in_specs=[tile, pl.BlockSpec((n_slot, tm, width), lambda i: (0, i, 0)), tile, tile],
        out_specs=(tile,) * 4, compiler_params=_params(),
    )(w, g_slots, m, v)


def cctx_grad(q_all, c_ctx_row):
    d = c_ctx_row.shape[1]

    def body(q_ref, c_ref, o_ref):
        acc = q_ref[0, 0:1, :]
        for j in (2, 4, 6):
            acc = acc + q_ref[j, 0:1, :]
        _, vjp = jax.vjp(_silu, c_ref[...])
        o_ref[...] = vjp(acc)[0]

    return pl.pallas_call(
        body, name="cctx_grad", out_shape=jax.ShapeDtypeStruct((1, d), F32),
    )(q_all, c_ctx_row)


def loss_total(pack_sum, d):
    def body(p_ref, o_ref):
        o_ref[...] = jnp.sum(p_ref[:, 0:d], axis=1, keepdims=True)

    return pl.pallas_call(
        body, name="loss_total", out_shape=jax.ShapeDtypeStruct((1, 1), F32),
    )(pack_sum)


class _Plan:
    def __init__(self):
        self.builders, self.got = {}, {}

    def on(self, host, key, builder):
        self.builders.setdefault(host, []).append((key, builder))

    def run(self, host, fn, *args, **kw):
        if host not in self.builders:
            return fn(host, *args, **kw)
        keys, riders = zip(*[(key, builder(self)) for key, builder in self.builders[host]])
        res, landed = fn(host, *args, rider=Riders(riders), **kw)
        for key, r in zip(keys, riders):
            self.got[key], landed = landed[:r.n], landed[r.n:]
        return res


def _val(w):
    return w() if callable(w) else w


def _matmul_tile(n_rows, tm):
    return 2 * tm if n_rows % (2 * tm) == 0 else tm


def _ffn_fwd(plan, tag, xin, n_rows, tm, seg_fn, shift, scale, gate, norm_w, wg, wu, wd):
    d = xin.shape[1]
    n_tiles = n_rows // tm
    (h,) = plan.run(f"{tag}_norm", rowwise, fn_norm_mod, [row(xin)], [shift, scale], [norm_w], [(n_rows, d, BF16)],
                    tm=tm, n_tiles=n_tiles, seg_fn=seg_fn)
    tmm = _matmul_tile(n_rows, tm)
    g = plan.run(f"{tag}_gate", matmul, [(h, _val(wg))], "nn", out_dtype=BF16, b_ch=True, out_ch=True, tm=tmm)
    u, act = plan.run(f"{tag}_up", matmul, [(h, _val(wu))], "nn", b_ch=True, out_ch=True, tm=tm, fold=True,
                      post=([g], lambda acc, gv: (acc, fn_act(gv, acc)[0]), [BF16, BF16]))
    f = plan.run(f"{tag}_down", matmul, [(act, _val(wd))], "nn", a_ch=True, b_ch=True, tm=tmm, fold=True)
    (xo,) = plan.run(f"{tag}_resid", rowwise, make_fn_resid(0.5), [row(xin), row(f)], [gate], [], [(n_rows, d, F32)],
                     tm=tm, n_tiles=n_tiles, seg_fn=seg_fn)
    return xo, (h, g, u, act, f)


def _ffn_bwd(plan, tag, d_xo, saved, xin, n_rows, tm, seg_fn, first_fn, shift, scale, gate, norm_w, wg, wu, wd, dx_rows, dx_limit):
    h, g, u, act, f = saved
    d = xin.shape[1]
    n_tiles = n_rows // tm
    n_ch, _, n_hid = g.shape
    d_f, d_gate = plan.run(f"{tag}_resid_bwd", rowwise_bwd, make_fn_resid(0.5), [row(xin), row(f)], [gate], [], [[row(d_xo)]],
                           [None, (n_rows, BF16, None)], tm=tm, n_tiles=n_tiles, seg_fn=seg_fn, first_fn=first_fn)
    tmm = _matmul_tile(n_rows, tm)
    def act_vjp(d_act, gv, uv):
        s = jax.nn.sigmoid(gv)
        gs = gv * s
        return d_act * uv * (s + gs * (1.0 - s)), d_act * gs
    d_g, d_u = plan.run(f"{tag}_down_dx", matmul, [(d_f, wd)], "nt", b_ch=True, out_ch=True, tm=tmm,
                        post=([g, u], act_vjp, [BF16, BF16]))
    plan.got[f"{tag}_d_wd"] = plan.run(f"{tag}_down_dw", matmul, [(act, d_f)], "tn", out_dtype=BF16, a_ch=True, out_ch=True, tm=tmm)
    d_h = plan.run(f"{tag}_up_dx", matmul, [(d_g, wg), (d_u, wu)], "nt", a_ch=True, b_ch=True, tm=tmm)
    plan.got[f"{tag}_d_wg"] = plan.run(f"{tag}_gate_dw", matmul, [(d_g, h)], "tn", out_dtype=BF16, a_ch=True, out_ch=True, tm=tmm)
    plan.got[f"{tag}_d_wu"] = plan.run(f"{tag}_up_dw", matmul, [(d_u, h)], "tn", out_dtype=BF16, a_ch=True, out_ch=True, tm=tmm)
    d_x, d_shift, d_scale, d_nw = plan.run(
        f"{tag}_norm_bwd", rowwise_bwd, fn_norm_mod, [row(xin)], [shift, scale], [norm_w], [[row(d_h)]], [(dx_rows, F32, dx_limit)],
        tm=tm, n_tiles=n_tiles, seg_fn=seg_fn, first_fn=first_fn, adds={0: (row(d_xo), None)})
    return d_x, (d_shift, d_scale, d_gate), d_nw


def kernel(x, c, ctx, c_ctx, w_mod, b_mod, norm_ffn1, ffn1_gate, ffn1_up, ffn1_down, norm_mix, w_in, ssm_conv_w, ssm_conv_b, dt_bias_fwd, dt_bias_bwd, a_log_fwd, a_log_bwd, ssm_d, ssm_norm_w, cconv_w, cconv_b, cconv_ln_w, cconv_ln_b, w_out, norm_ffn2, ffn2_gate, ffn2_up, ffn2_down, final_norm, loss_target, m_c_ctx, m_w_mod, m_b_mod, m_norm_ffn1, m_ffn1_gate, m_ffn1_up, m_ffn1_down, m_norm_mix, m_w_in, m_ssm_conv_w, m_ssm_conv_b, m_dt_bias_fwd, m_dt_bias_bwd, m_a_log_fwd, m_a_log_bwd, m_ssm_d, m_ssm_norm_w, m_cconv_w, m_cconv_b, m_cconv_ln_w, m_cconv_ln_b, m_w_out, m_norm_ffn2, m_ffn2_gate, m_ffn2_up, m_ffn2_down, m_final_norm, v_c_ctx, v_w_mod, v_b_mod, v_norm_ffn1, v_ffn1_gate, v_ffn1_up, v_ffn1_down, v_norm_mix, v_w_in, v_ssm_conv_w, v_ssm_conv_b, v_dt_bias_fwd, v_dt_bias_bwd, v_a_log_fwd, v_a_log_bwd, v_ssm_d, v_ssm_norm_w, v_cconv_w, v_cconv_b, v_cconv_ln_w, v_cconv_ln_b, v_w_out, v_norm_ffn2, v_ffn2_gate, v_ffn2_up, v_ffn2_down, v_final_norm):
    weights = dict(c_ctx=c_ctx, w_mod=w_mod, b_mod=b_mod, norm_ffn1=norm_ffn1, ffn1_gate=ffn1_gate, ffn1_up=ffn1_up, ffn1_down=ffn1_down, norm_mix=norm_mix, w_in=w_in, ssm_conv_w=ssm_conv_w, ssm_conv_b=ssm_conv_b, dt_bias_fwd=dt_bias_fwd, dt_bias_bwd=dt_bias_bwd, a_log_fwd=a_log_fwd, a_log_bwd=a_log_bwd, ssm_d=ssm_d, ssm_norm_w=ssm_norm_w, cconv_w=cconv_w, cconv_b=cconv_b, cconv_ln_w=cconv_ln_w, cconv_ln_b=cconv_ln_b, w_out=w_out, norm_ffn2=norm_ffn2, ffn2_gate=ffn2_gate, ffn2_up=ffn2_up, ffn2_down=ffn2_down, final_norm=final_norm)
    mom1 = dict(c_ctx=m_c_ctx, w_mod=m_w_mod, b_mod=m_b_mod, norm_ffn1=m_norm_ffn1, ffn1_gate=m_ffn1_gate, ffn1_up=m_ffn1_up, ffn1_down=m_ffn1_down, norm_mix=m_norm_mix, w_in=m_w_in, ssm_conv_w=m_ssm_conv_w, ssm_conv_b=m_ssm_conv_b, dt_bias_fwd=m_dt_bias_fwd, dt_bias_bwd=m_dt_bias_bwd, a_log_fwd=m_a_log_fwd, a_log_bwd=m_a_log_bwd, ssm_d=m_ssm_d, ssm_norm_w=m_ssm_norm_w, cconv_w=m_cconv_w, cconv_b=m_cconv_b, cconv_ln_w=m_cconv_ln_w, cconv_ln_b=m_cconv_ln_b, w_out=m_w_out, norm_ffn2=m_norm_ffn2, ffn2_gate=m_ffn2_gate, ffn2_up=m_ffn2_up, ffn2_down=m_ffn2_down, final_norm=m_final_norm)
    mom2 = dict(c_ctx=v_c_ctx, w_mod=v_w_mod, b_mod=v_b_mod, norm_ffn1=v_norm_ffn1, ffn1_gate=v_ffn1_gate, ffn1_up=v_ffn1_up, ffn1_down=v_ffn1_down, norm_mix=v_norm_mix, w_in=v_w_in, ssm_conv_w=v_ssm_conv_w, ssm_conv_b=v_ssm_conv_b, dt_bias_fwd=v_dt_bias_fwd, dt_bias_bwd=v_dt_bias_bwd, a_log_fwd=v_a_log_fwd, a_log_bwd=v_a_log_bwd, ssm_d=v_ssm_d, ssm_norm_w=v_ssm_norm_w, cconv_w=v_cconv_w, cconv_b=v_cconv_b, cconv_ln_w=v_cconv_ln_w, cconv_ln_b=v_cconv_ln_b, w_out=v_w_out, norm_ffn2=v_norm_ffn2, ffn2_gate=v_ffn2_gate, ffn2_up=v_ffn2_up, ffn2_down=v_ffn2_down, final_norm=v_final_norm)
    order = list(weights)

    n_ex, seq_len, d = x.shape
    ctx_len = ctx.shape[1]
    ds = d
    n_head = ds // HEAD_DIM
    xw = ds + 4 * N_STATE
    n_lat, n_ctx_rows = n_ex * seq_len, n_ex * ctx_len
    n_tok = n_lat + n_ctx_rows
    tm = math.gcd(math.gcd(512, seq_len), n_ctx_rows)
    seg_all, first_all = _segmenter(tm, seq_len, n_lat)
    lat_tiles = n_lat // tm

    xi, yi, ci = lax.axis_index("x"), lax.axis_index("y"), lax.axis_index("c")
    me, chip = 4 * xi + 2 * yi + ci, 2 * xi + yi

    (c_all,) = exchange("gather_c", [c], "all8")
    n_all = 8 * n_ex
    n_cond = -(-(n_all + 1) // 8) * 8
    cond = jnp.concatenate([c_all.reshape(n_all, d), c_ctx[None, :], jnp.zeros((n_cond - n_all - 1, d), F32)])
    mod_w = w_mod.shape[2]
    b_shard = lax.dynamic_slice(b_mod, (0, chip * mod_w), (1, mod_w))
    (mod_g,) = exchange("gather_mod", [mod_fwd(cond, w_mod[0], b_shard)], "chips")
    mod_full = mod_g.transpose(1, 0, 2).reshape(n_cond, N_CHIPS * mod_w)
    mod_mine = lax.dynamic_slice(mod_full, (me * n_ex, 0), (n_ex, 9 * d)).reshape(n_ex, 9, d)
    mod_ctx = mod_full[n_all].reshape(9, d)
    tabs = [jnp.concatenate([mod_mine[:, j], mod_ctx[j][None]])[:, None, :] for j in range(9)]
    lat = lambda t: t[:n_ex]

    bf = lambda w: w[0].astype(BF16)
    plan = _Plan()
    gather = lambda *ws: (lambda p: Rider(list(ws), "chips"))
    plan.on("ffn1_norm", "wg1", gather(bf(ffn1_gate)))
    plan.on("ffn1_gate", "wu1", gather(bf(ffn1_up)))
    plan.on("ffn1_up", "wd1", gather(bf(ffn1_down)))
    win_cut = d * 5 // 8
    plan.on("ffn1_down", "win_a", gather(bf(w_in)[:win_cut]))
    plan.on("ffn1_resid", "win_b", gather(bf(w_in)[win_cut:], ssm_conv_w[0], cconv_w[0]))
    xt = jnp.concatenate([x.reshape(n_lat, d), ctx.reshape(n_ctx_rows, d)])
    x1, saved1 = _ffn_fwd(plan, "ffn1", xt, n_tok, tm, seg_all, tabs[0], tabs[1], tabs[2], norm_ffn1,
                          lambda: plan.got["wg1"][0], lambda: plan.got["wu1"][0], lambda: plan.got["wd1"][0])
    (wg1,), (wu1,), (wd1,), (win_a,), (win_b, w5_g, w31_g) = (plan.got[k] for k in ("wg1", "wu1", "wd1", "win_a", "win_b"))
    win_g = jnp.concatenate([win_a, win_b], axis=1)
    unshard_cols = lambda t: t.transpose(1, 0, 2).reshape(t.shape[1], N_CHIPS * t.shape[2])
    win = unshard_cols(win_g)
    o_x, o_dt, o_glu = ds, ds + xw, ds + xw + 2 * n_head
    w_z, w_xbc, w_dt = win[:, :ds], win[:, o_x:o_dt], win[:, o_dt:o_glu]
    w_ga, w_gb = win[:, o_glu:o_glu + d], win[:, o_glu + d:]
    w_dtp = jnp.concatenate([w_dt, jnp.zeros((d, LANES - 2 * n_head), BF16)], axis=1)
    w_cat = jnp.concatenate([w_z, w_ga, w_gb, w_xbc, w_dtp], axis=1)
    cbw = d // 2
    xbc_cb, dt_cb = 3 * d // cbw, (3 * d + xw) // LANES
    w5, w31 = unshard_cols(w5_g), unshard_cols(w31_g)
    pad_vec = lambda v: jnp.concatenate([v.reshape(1, -1), jnp.zeros((1, LANES - v.size), F32)], axis=1)
    dtb_f, dtb_b, alog_f, alog_b = map(pad_vec, (dt_bias_fwd, dt_bias_bwd, a_log_fwd, a_log_bwd))
    dsk_f, dsk_b = pad_vec(ssm_d), jnp.zeros((1, LANES), F32)

    (h2,) = rowwise("mix_norm", fn_norm_mod, [row(x1)], [tabs[3], tabs[4]], [norm_mix], [(n_tok, d, BF16)],
                    tm=tm, n_tiles=n_tok // tm, seg_fn=seg_all)
    proj, (wg2,) = matmul("mix_proj", [(h2, w_cat)], "nn", tm=min(tm, 256), rider=Rider([bf(ffn2_gate)], "chips"))
    def conv5(name, src, cb0, flip):
        out = None
        for part, seq, off in (("lat", seq_len, 0), ("ctx", ctx_len, n_lat // ctx_len)):
            out = tapsum_roll(f"{name}_{part}", src, cb0, w5, 0, seq_len=seq, n_seq=n_ex, row_blk_off=off, width=seq,
                              piece=seq, cb=cbw, ncb=xw // cbw, pad=w5.shape[0] // 2, flip=flip,
                              place=((n_tok, xw), off, 0, out))
        return out

    craw = conv5("xbc_conv", proj, xbc_cb, False)
    (xbc,) = rowwise("xbc_silu", fn_silu_bias, [row(craw)], [], [ssm_conv_b], [(n_tok, xw, F32)], tm=tm, n_tiles=n_tok // tm)
    ssd = dict(n_ex=n_ex, seq_len=seq_len, ctx_len=ctx_len, ds=ds)
    (y_f, hs_f), (wu2, wd2) = ssd_fwd("ssd_fwd_f", xbc, proj, dt_cb, dtb_f, alog_f, dsk_f, rev=False,
                                      rider=Rider([bf(ffn2_up), bf(ffn2_down)], "chips"), **ssd)
    (y_b, hs_b), (wout_g,) = ssd_fwd("ssd_fwd_b", xbc, proj, dt_cb, dtb_b, alog_b, dsk_b, rev=True,
                                     rider=Rider([bf(w_out)], "chips"), **ssd)
    wout = wout_g.reshape(2 * d, d)
    wo_y, wo_u = wout[:ds], wout[ds:]
    fn_gate = make_fn_gate_groupnorm(ds)
    (yn,) = rowwise("ssd_gate", fn_gate, [row(y_f), row(y_b), row(proj, d, 0)], [], [ssm_norm_w], [(n_lat, ds, BF16)],
                    tm=tm, n_tiles=lat_tiles)
    (u0,) = rowwise("glu", fn_glu, [row(proj, d, 1), row(proj, d, 2)], [], [], [(n_lat, d, F32)], tm=tm, n_tiles=lat_tiles)
    cb31 = max(LANES, d // 4)
    ncb31 = (d // 2) // cb31
    pad31 = w31.shape[0] // 2
    piece31 = min(seq_len, 4 * GRID_W)
    v_w = tapsum_roll("cconv_cols", u0, 0, w31, 0, seq_len=seq_len, n_seq=n_ex, row_blk_off=0, width=GRID_W,
                      piece=piece31, cb=cb31, ncb=ncb31, pad=pad31, flip=False)
    v_h = tapsum_rows("cconv_rows", u0, ncb31, w31, ncb31, seq_len=seq_len, n_seq=n_ex, cb=cb31, ncb=ncb31, pad=pad31, flip=False)
    (un,) = rowwise("cconv_ln", fn_ln_silu, [row(v_w), row(v_h)], [], [cconv_b, cconv_ln_w, cconv_ln_b], [(n_lat, d, BF16)],
                    tm=tm, n_tiles=lat_tiles)
    mix = matmul("mix_out", [(yn, wo_y), (un, wo_u)], "nn", tm=tm)
    seg_lat, first_lat = _segmenter(tm, seq_len, n_lat)
    (x2,) = rowwise("mix_resid", make_fn_resid(1.0), [row(x1), row(mix)], [lat(tabs[5])], [], [(n_lat, d, F32)],
                    tm=tm, n_tiles=lat_tiles, seg_fn=seg_lat)
    x3, saved2 = _ffn_fwd(plan, "ffn2", x2, n_lat, tm, seg_lat, lat(tabs[6]), lat(tabs[7]), lat(tabs[8]), norm_ffn2, wg2, wu2, wd2)
    d_x3, d_final, loss_vec = final_loss(x3, loss_target.reshape(n_lat, d), final_norm.reshape(1, d), tm=tm)

    shard_cols = lambda t: t.reshape(t.shape[0], N_CHIPS, -1).transpose(1, 0, 2)

    def pieces(t):
        t = jnp.pad(t, ((0, 0), (0, -t.shape[1] % 32), (0, 0)))
        return t.reshape(2 * N_CHIPS, t.shape[1] // 2, t.shape[2]).astype(BF16)

    scatter = lambda *ts: Rider([pieces(t) for t in ts], "all8", scatter=True)
    halves = lambda names, landed: Rider([sum_slots(f"sum_{nm}", r) for nm, r in zip(names, landed)], "sibling")
    swapped = {}
    plan.on("ffn2_up_dx", "sc_ffn2_down", lambda p: scatter(p.got["ffn2_d_wd"]))
    plan.on("ffn2_up_dw", "sc_ffn2_gate", lambda p: scatter(p.got["ffn2_d_wg"]))
    d_x2, (d_s6, d_s7, d_g8), d_nffn2 = _ffn_bwd(
        plan, "ffn2", d_x3, saved2, x2, n_lat, tm, seg_lat, first_lat, lat(tabs[6]), lat(tabs[7]), lat(tabs[8]), norm_ffn2,
        wg2, wu2, wd2, n_lat, None)
    d_mix, d_g5 = rowwise_bwd("mix_resid_bwd", make_fn_resid(1.0), [row(x1), row(mix)], [lat(tabs[5])], [], [[row(d_x2)]],
                              [None, (n_lat, BF16, None)], tm=tm, n_tiles=lat_tiles, seg_fn=seg_lat, first_fn=first_lat)
    d_yn = matmul("mix_out_dy", [(d_mix, wo_y)], "nt", tm=tm)
    d_un = matmul("mix_out_du", [(d_mix, wo_u)], "nt", tm=tm)
    d_wout = jnp.concatenate([matmul("mix_out_dwy", [(yn, d_mix)], "tn", out_dtype=BF16, tm=tm), matmul("mix_out_dwu", [(un, d_mix)], "tn", out_dtype=BF16, tm=tm)])
    d_vw, d_vh, d_cb, d_lnw, d_lnb = rowwise_bwd(
        "cconv_ln_bwd", fn_ln_silu, [row(v_w), row(v_h)], [], [cconv_b, cconv_ln_w, cconv_ln_b], [[row(d_un)]],
        [(n_lat, F32, None)] * 2, tm=tm, n_tiles=lat_tiles)
    d_u0 = tapsum_roll("cconv_cols_dx", d_vw, 0, w31, 0, seq_len=seq_len, n_seq=n_ex, row_blk_off=0, width=GRID_W,
                       piece=piece31, cb=cb31, ncb=ncb31, pad=pad31, flip=True, place=((n_lat, d), 0, 0, None))
    d_u0 = tapsum_rows("cconv_rows_dx", d_vh, 0, w31, ncb31, seq_len=seq_len, n_seq=n_ex, cb=cb31, ncb=ncb31, pad=pad31,
                       flip=True, place=((n_lat, d), 0, ncb31, d_u0))
    d_w31 = jnp.concatenate([
        tapgrad_roll("cconv_cols_dw", d_vw, 0, 0, u0, 0, 0, n_tap=w31.shape[0], seq_len=seq_len, n_seq=n_ex, width=GRID_W,
                     piece=piece31, cb=cb31, ncb=ncb31, pad=pad31),
        tapgrad_rows("cconv_rows_dw", d_vh, 0, u0, ncb31, n_tap=w31.shape[0], seq_len=seq_len, n_seq=n_ex, cb=cb31,
                     ncb=ncb31, pad=pad31)], axis=1)
    d_ga, d_gb = rowwise_bwd("glu_bwd", fn_glu, [row(proj, d, 1), row(proj, d, 2)], [], [], [[row(d_u0)]],
                             [(n_lat, BF16, None)] * 2, tm=tm, n_tiles=lat_tiles)
    d_ysum, d_z, d_ssmnw = rowwise_bwd(
        "ssd_gate_bwd", fn_gate, [row(y_f), row(y_b), row(proj, d, 0)], [], [ssm_norm_w], [[row(d_yn)]],
        [(n_lat, F32, None), None, (n_lat, BF16, None)], tm=tm, n_tiles=lat_tiles)
    (dxbc_f, ddt_f, dalog_f, ddtb_f, ddsk), landed = ssd_bwd(
        "ssd_bwd_f", xbc, proj, dt_cb, hs_f, d_ysum, dtb_f, alog_f, dsk_f, rev=False,
        rider=scatter(plan.got["ffn2_d_wu"], d_wout.reshape(N_CHIPS, -1, d)), **ssd)
    (dxbc_b, ddt_b, dalog_b, ddtb_b, _), both = ssd_bwd(
        "ssd_bwd_b", xbc, proj, dt_cb, hs_b, d_ysum, dtb_b, alog_b, dsk_b, rev=True,
        rider=halves(["ffn2_down", "ffn2_gate"], plan.got["sc_ffn2_down"] + plan.got["sc_ffn2_gate"]), **ssd)
    swapped.update(zip(["ffn2_down", "ffn2_gate"], both))
    (d_craw, d_conv_b), both = rowwise_bwd(
        "xbc_silu_bwd", fn_silu_bias, [row(craw)], [], [ssm_conv_b], [[row(dxbc_f), row(dxbc_b)]],
        [(n_tok, F32, None)], tm=tm, n_tiles=n_tok // tm, rider=halves(["ffn2_up", "w_out"], landed))
    swapped.update(zip(["ffn2_up", "w_out"], both))
    d_pxbc = conv5("xbc_conv_dx", d_craw, 0, True)
    g5 = lambda name, seq, off: tapgrad_roll(name, d_craw, 0, off, proj, xbc_cb, off, n_tap=w5.shape[0], seq_len=seq,
                                             n_seq=n_ex, width=seq, piece=seq, cb=cbw, ncb=xw // cbw, pad=w5.shape[0] // 2)
    d_w5 = g5("xbc_conv_lat_dw", seq_len, 0) + g5("xbc_conv_ctx_dw", ctx_len, n_lat // ctx_len)
    lat_pairs = [(d_z, w_z), (d_ga, w_ga), (d_gb, w_gb), (d_pxbc, w_xbc), (ddt_f, w_dtp), (ddt_b, w_dtp)]
    d_h2 = jnp.concatenate([matmul("mix_proj_dx_lat", lat_pairs, "nt", rows=n_lat, tm=min(tm, 256)),
                            matmul("mix_proj_dx_ctx", lat_pairs[3:], "nt", rows=n_ctx_rows, row_off=n_lat, tm=min(tm, 256))])
    d_wz = matmul("mix_proj_dwz", [(d_z, h2)], "tn", out_dtype=BF16, rows=n_lat, tm=tm)
    d_wga = matmul("mix_proj_dwa", [(d_ga, h2)], "tn", out_dtype=BF16, rows=n_lat, tm=tm)
    d_wgb = matmul("mix_proj_dwb", [(d_gb, h2)], "tn", out_dtype=BF16, rows=n_lat, tm=tm)
    d_wxbc = matmul("mix_proj_dwx", [(d_pxbc, h2)], "tn", out_dtype=BF16, tm=tm)
    d_wdt = matmul("mix_proj_dwt", [(ddt_f, h2), (ddt_b, h2)], "tn", out_dtype=BF16, tm=tm)
    d_win_t = jnp.concatenate([d_wz, d_wxbc, d_wdt[:2 * n_head], d_wga, d_wgb]).reshape(N_CHIPS, -1, d)
    d_x1, d_s3, d_s4, d_nmix = rowwise_bwd(
        "mix_norm_bwd", fn_norm_mod, [row(x1)], [tabs[3], tabs[4]], [norm_mix], [[row(d_h2)]], [(n_tok, F32, None)],
        tm=tm, n_tiles=n_tok // tm, seg_fn=seg_all, first_fn=first_all, adds={0: (row(d_x2), lat_tiles)})
    mix_names = ["w_in", "ssm_conv_w", "cconv_w"]
    plan.on("ffn1_down_dx", "sc_conv", lambda p: scatter(shard_cols(d_w5), shard_cols(d_w31)))
    plan.on("ffn1_up_dx", "sc_win", lambda p: scatter(d_win_t))
    plan.on("ffn1_gate_dw", "sc_ffn1_down", lambda p: scatter(p.got["ffn1_d_wd"]))
    plan.on("ffn1_up_dw", "sc_ffn1_gate", lambda p: scatter(p.got["ffn1_d_wg"]))
    plan.on("ffn1_up_dw", "sw_mix", lambda p: halves(mix_names, p.got["sc_win"] + p.got["sc_conv"]))
    plan.on("ffn1_norm_bwd", "sc_ffn1_up", lambda p: scatter(p.got["ffn1_d_wu"]))
    plan.on("ffn1_norm_bwd", "sw_ffn1_down", lambda p: halves(["ffn1_down"], p.got["sc_ffn1_down"]))
    d_xt, (d_s0, d_s1, d_g2), d_nffn1 = _ffn_bwd(
        plan, "ffn1", d_x1, saved1, xt, n_tok, tm, seg_all, first_all, tabs[0], tabs[1], tabs[2], norm_ffn1, wg1, wu1, wd1,
        n_lat, lat_tiles)
    swapped.update(zip(mix_names + ["ffn1_down"], plan.got["sw_mix"] + plan.got["sw_ffn1_down"]))
    last_names = ["ffn1_gate", "ffn1_up"]
    last = halves(last_names, plan.got["sc_ffn1_gate"] + plan.got["sc_ffn1_up"])
    swapped.update(zip(last_names, exchange("swap_sibling", last.arrs, "sibling")))
    grad_x = d_xt.reshape(n_ex, seq_len, d)

    with_ctx0 = lambda t: jnp.concatenate([t, jnp.zeros((1, 1, d), F32)])
    d_tabs = [d_s0, d_s1, d_g2, d_s3, d_s4, with_ctx0(d_g5), with_ctx0(d_s6), with_ctx0(d_s7), with_ctx0(d_g8)]
    d_mod_rows = jnp.concatenate([t[:, 0, :] for t in d_tabs], axis=1)
    n_pad_rows = -(-(n_ex + 1) // 8) * 8
    d_mod_rows = jnp.concatenate([d_mod_rows, jnp.zeros((n_pad_rows - n_ex - 1, 9 * d), F32)])
    small = [("loss", loss_vec), ("norm_ffn1", d_nffn1), ("norm_mix", d_nmix), ("ssm_conv_b", d_conv_b),
             ("dt_bias_fwd", ddtb_f[:, :n_head]), ("dt_bias_bwd", ddtb_b[:, :n_head]), ("a_log_fwd", dalog_f[:, :n_head]),
             ("a_log_bwd", dalog_b[:, :n_head]), ("ssm_d", ddsk[:, :n_head]), ("ssm_norm_w", d_ssmnw), ("cconv_b", d_cb),
             ("cconv_ln_w", d_lnw), ("cconv_ln_b", d_lnb), ("norm_ffn2", d_nffn2), ("final_norm", d_final)]
    n_small = sum(v.size for _, v in small)
    n_pack = -(-n_small // (8 * LANES)) * (8 * LANES)
    pack = jnp.concatenate([v.reshape(-1) for _, v in small] + [jnp.zeros((n_pack - n_small,), F32)]).reshape(-1, LANES)
    pack_all, d_mod_all = exchange("gather_small", [pack, d_mod_rows], "all8")
    pack_sum = sum_slots("small_sum", pack_all)
    loss = loss_total(pack_sum.reshape(1, n_pack), d).reshape(())
    flat_sum = pack_sum.reshape(-1)
    small_grads, pos = {}, 0
    for nm, v in small:
        small_grads[nm] = flat_sum[pos:pos + v.size]
        pos += v.size
    d_mod_all = d_mod_all.reshape(8 * n_pad_rows, 9 * d)
    cond_rows = [jnp.concatenate([cond[j * n_ex:(j + 1) * n_ex], c_ctx[None, :],
                                  jnp.zeros((n_pad_rows - n_ex - 1, d), F32)]) for j in range(8)]
    cond_bwd = jnp.concatenate(cond_rows)
    d_mod_shard = lax.dynamic_slice(d_mod_all, (0, chip * mod_w), (8 * n_pad_rows, mod_w))
    g_wmod, g_bmod, q_part = mod_bwd(cond_bwd, d_mod_shard, d_mod_all, w_mod[0],
                                     tuple(j * n_pad_rows + n_ex for j in range(8)))
    (q_all,) = exchange("gather_cctx", [q_part], "all8")
    g_cctx = cctx_grad(q_all, c_ctx.reshape(1, d))
    small_grads["c_ctx"], small_grads["b_mod"] = g_cctx.reshape(-1), g_bmod.reshape(-1)

    transposed = {"ffn1_gate", "ffn1_up", "ffn2_gate", "ffn2_up", "w_in"}
    results = {}
    for nm, both in swapped.items():
        flip = (lambda t: jnp.swapaxes(t, 1, 2)) if nm in transposed else (lambda t: t)
        shape = flip(weights[nm]).shape
        two_d = lambda t: flip(t).reshape(shape[-2], shape[-1])
        g_full = both.reshape(1, -1, shape[-1])[:, :shape[-2]]
        results[nm] = [flip(r.reshape(shape)) for r in
                       adamw(f"adamw_{nm}", two_d(weights[nm]), g_full, two_d(mom1[nm]), two_d(mom2[nm]))]
    results["w_mod"] = [r.reshape(w_mod.shape) for r in adamw("adamw_w_mod", w_mod[0], g_wmod[None], m_w_mod[0], v_w_mod[0])]
    small_names = [nm for nm in order if nm not in results]
    n_sm = sum(weights[nm].size for nm in small_names)
    n_smp = -(-n_sm // (8 * LANES)) * (8 * LANES)
    packed = lambda src: jnp.concatenate([src[nm].reshape(-1) for nm in small_names] + [jnp.zeros((n_smp - n_sm,), F32)]).reshape(-1, LANES)
    sm_out = adamw("adamw_small", packed(weights), packed(small_grads)[None], packed(mom1), packed(mom2))
    pos = 0
    for nm in small_names:
        size = weights[nm].size
        results[nm] = [r.reshape(-1)[pos:pos + size].reshape(weights[nm].shape) for r in sm_out]
        pos += size
    return (loss, grad_x, *[results[nm][0] for nm in order], *[results[nm][1] for nm in order],
            *[results[nm][2] for nm in order], *[results[nm][3] for nm in order])
```

```python
import functools
import math

import jax
import jax.numpy as jnp
from jax import lax
from jax.experimental import pallas as pl
from jax.experimental.pallas import tpu as pltpu

F32 = jnp.float32
BF16 = jnp.bfloat16
HI = lax.Precision.HIGHEST
MESH = pl.DeviceIdType.MESH

EPS = 1e-6
GRID_W = 64
HEAD_DIM = 64
N_STATE = 128
CHUNK = 128
LANES = 128
N_CHIPS = 4
ADAM_LR, ADAM_B1, ADAM_B2, ADAM_EPS, ADAM_WD, ADAM_STEP = 0.001, 0.9, 0.999, 1e-08, 0.01, 10
VMEM_CAP = 56 * 1024 * 1024


def _params(vmem_bytes=None, n_axes=1):
    kw = dict(dimension_semantics=("arbitrary",) * n_axes)
    if vmem_bytes is not None:
        kw["vmem_limit_bytes"] = int(min(VMEM_CAP, max(32 * 1024 * 1024, vmem_bytes)))
    return pltpu.CompilerParams(**kw)


def _big(shape, dtype):
    return pltpu.HBM(tuple(shape), dtype)


def _in_hbm(args):
    return [pltpu.with_memory_space_constraint(a, pltpu.HBM) if a.size * a.dtype.itemsize >= (1 << 20) else a for a in args]


def _nbytes(shape, dtype):
    return math.prod(shape) * jnp.dtype(dtype).itemsize


def _row_tile(rows, width, cap_bytes=1 << 20, mult=8):
    best = None
    for t in range(mult, rows + 1, mult):
        if rows % t == 0 and t * width * 4 <= cap_bytes:
            best = t
    return best if best is not None else rows


_MODES = {"all8": (8, (1, 2, 3, 4, 5, 6, 7), 0), "chips": (4, (2, 4, 6), 1), "sibling": (2, (1,), 0)}


class Rider:
    def __init__(self, arrs, mode, scatter=False):
        self.arrs, self.scatter = list(arrs), scatter
        self.nslot, self.deltas, self.shift = _MODES[mode]
        self.n = len(self.arrs)
        self.out_shape = [jax.ShapeDtypeStruct((self.nslot,) + (a.shape[1:] if scatter else a.shape), a.dtype)
                          for a in self.arrs]
        any_spec = pl.BlockSpec(memory_space=pl.ANY)
        self.in_specs = [any_spec] * self.n
        self.out_specs = [any_spec] * self.n
        n_peer = len(self.deltas)
        self.scratch = [pltpu.SemaphoreType.DMA((self.n, n_peer)), pltpu.SemaphoreType.DMA((self.n, n_peer)),
                        pltpu.SemaphoreType.DMA((self.n,))]

    def _copies(self, ins, outs, sems, arrivals):
        send_sems, recv_sems, local_sems = sems
        x, y, c = lax.axis_index("x"), lax.axis_index("y"), lax.axis_index("c")
        me = 4 * x + 2 * y + c
        slot_of = lambda dev: (dev >> self.shift) & (self.nslot - 1)
        src = lambda a, slot: ins[a].at[slot] if self.scatter else ins[a]
        flip = lambda v, bit: 1 - v if bit else v

        def remote(a, k, d, from_slot, to_slot):
            return pltpu.make_async_remote_copy(
                src_ref=src(a, from_slot), dst_ref=outs[a].at[to_slot], send_sem=send_sems.at[a, k],
                recv_sem=recv_sems.at[a, k], device_id=(flip(x, (d >> 2) & 1), flip(y, (d >> 1) & 1), flip(c, d & 1)),
                device_id_type=MESH)

        mine = slot_of(me)
        local = [pltpu.make_async_copy(src(a, mine), outs[a].at[mine], local_sems.at[a]) for a in range(self.n)]
        sends = [remote(a, k, d, slot_of(me ^ d), mine) for k, d in enumerate(self.deltas) for a in range(self.n)]
        if not arrivals:
            return local, sends
        return local, sends, [remote(a, k, d, mine, slot_of(me ^ d)) for k, d in enumerate(self.deltas) for a in range(self.n)]

    def start(self, ins, outs, sems):
        local, sends = self._copies(ins, outs, sems, arrivals=False)
        for cp in local + sends:
            cp.start()

    def wait(self, ins, outs, sems):
        local, sends, recvs = self._copies(ins, outs, sems, arrivals=True)
        for cp in recvs:
            cp.wait_recv()
        for cp in sends:
            cp.wait_send()
        for cp in local:
            cp.wait()


class Riders:
    def __init__(self, riders):
        self.riders = list(riders)
        self.n = sum(r.n for r in self.riders)
        cat = lambda attr: [v for r in self.riders for v in getattr(r, attr)]
        self.arrs, self.out_shape, self.in_specs = cat("arrs"), cat("out_shape"), cat("in_specs")
        self.out_specs, self.scratch = cat("out_specs"), cat("scratch")

    def _each(self, method, ins, outs, sems):
        i = s = 0
        for r in self.riders:
            getattr(r, method)(ins[i:i + r.n], outs[i:i + r.n], sems[s:s + len(r.scratch)])
            i, s = i + r.n, s + len(r.scratch)

    def start(self, ins, outs, sems):
        self._each("start", ins, outs, sems)

    def wait(self, ins, outs, sems):
        self._each("wait", ins, outs, sems)


class _Hosted:
    def __init__(self, rider, n_in, n_out, n_scratch, grid):
        self.rider, self.n_in, self.n_out, self.n_scratch, self.grid = rider, n_in, n_out, n_scratch, grid
        self.n = rider.n if rider else 0

    def split(self, refs):
        a, b = self.n_in, self.n_in + self.n
        c, e = b + self.n_out, b + self.n_out + self.n
        self._r = (refs[a:b], refs[c:e], refs[e + self.n_scratch:])
        if self.rider:
            ids = [pl.program_id(ax) for ax in range(len(self.grid))]
            first = functools.reduce(jnp.logical_and, [i == 0 for i in ids]) if ids else True
            pl.when(first)(lambda: self.rider.start(*self._r))
        return refs[:a], refs[b:c], refs[e:e + self.n_scratch]

    def finish(self):
        if self.rider:
            ids = [pl.program_id(ax) for ax in range(len(self.grid))]
            last = functools.reduce(jnp.logical_and, [i == n - 1 for i, n in zip(ids, self.grid)]) if ids else True
            pl.when(last)(lambda: self.rider.wait(*self._r))

    def call_args(self, in_specs, out_shape, out_specs, scratch, args):
        r = self.rider
        if not r:
            return list(in_specs), tuple(out_shape), tuple(out_specs), list(scratch), list(args)
        return (list(in_specs) + r.in_specs, tuple(out_shape) + tuple(r.out_shape), tuple(out_specs) + tuple(r.out_specs),
                list(scratch) + r.scratch, list(args) + r.arrs)

    def results(self, res, unwrap=True):
        res = list(res) if isinstance(res, (tuple, list)) else [res]
        host = res[:self.n_out]
        host = host[0] if (self.n_out == 1 and unwrap) else tuple(host)
        return (host, res[self.n_out:]) if self.rider else host


def exchange(name, arrs, mode, scatter=False):
    rider = Rider(arrs, mode, scatter)

    def body(*refs):
        ins, outs, sems = refs[:rider.n], refs[rider.n:2 * rider.n], refs[2 * rider.n:]
        rider.start(ins, outs, sems)
        rider.wait(ins, outs, sems)

    return pl.pallas_call(
        body, name=name, out_shape=tuple(rider.out_shape), in_specs=rider.in_specs, out_specs=tuple(rider.out_specs),
        scratch_shapes=rider.scratch,
    )(*arrs)


_DIMS = {"nn": (((1,), (0,)), ((), ())), "nt": (((1,), (1,)), ((), ())), "tn": (((0,), (0,)), ((), ()))}


def matmul(name, pairs, kind, *, a_ch=False, b_ch=False, out_ch=False, out_dtype=F32, rows=None, row_off=0, tm=512,
           rider=None, post=None, fold=False):
    a0, b0 = pairs[0]
    n_chunk = a0.shape[0] if a_ch else (b0.shape[0] if b_ch else 1)
    total_rows = a0.shape[-2]
    rows = total_rows - row_off if rows is None else rows
    tm = min(tm, rows)
    assert rows % tm == 0 and row_off % tm == 0, (name, rows, tm, row_off)
    n_rt, off = rows // tm, row_off // tm
    dims = _DIMS[kind]
    n_pair = len(pairs)

    if kind == "tn":
        grid, red_axis, n_red = (n_chunk, n_rt), 1, n_rt
        a_idx = (lambda k, i: (k, i + off, 0)) if a_ch else (lambda k, i: (i + off, 0))
        b_idx = (lambda k, i: (k, i + off, 0)) if b_ch else (lambda k, i: (i + off, 0))
        a_blk = lambda a: ((None, tm, a.shape[-1]) if a_ch else (tm, a.shape[-1]))
        b_blk = lambda b: ((None, tm, b.shape[-1]) if b_ch else (tm, b.shape[-1]))
        o2 = (a0.shape[-1], b0.shape[-1])
        out_shape = ((n_chunk,) + o2) if out_ch else o2
        out_spec = pl.BlockSpec((None,) + o2, lambda k, i: (k, 0, 0)) if out_ch else pl.BlockSpec(o2, lambda k, i: (0, 0))
        acc_shape = o2
    else:
        n_out = b0.shape[-1] if kind == "nn" else b0.shape[-2]
        b2 = b0.shape[-2:]
        if a_ch and b_ch and not out_ch and fold:
            grid, red_axis, n_red = (n_rt,), None, 1
            a_idx, b_idx = (lambda i: (0, i + off, 0)), (lambda i: (0, 0, 0))
            a_blk = lambda a: (n_chunk, tm, a.shape[-1])
            b_blk = lambda b: tuple(b.shape)
            out_shape, out_spec = (rows, n_out), pl.BlockSpec((tm, n_out), lambda i: (i, 0))
        elif a_ch and b_ch and not out_ch:
            grid, red_axis, n_red = (n_rt, n_chunk), 1, n_chunk
            a_idx, b_idx = (lambda i, k: (k, i + off, 0)), (lambda i, k: (k, 0, 0))
            a_blk = lambda a: (None, tm, a.shape[-1])
            b_blk = lambda b: (None,) + tuple(b.shape[-2:])
            out_shape, out_spec = (rows, n_out), pl.BlockSpec((tm, n_out), lambda i, k: (i, 0))
        elif out_ch and fold:
            assert b_ch and not a_ch and n_pair == 1
            grid, red_axis, n_red = (n_rt,), None, 1
            a_idx, b_idx = (lambda i: (i + off, 0)), (lambda i: (0, 0, 0))
            a_blk = lambda a: (tm, a.shape[-1])
            b_blk = lambda b: tuple(b.shape)
            out_shape, out_spec = (n_chunk, rows, n_out), pl.BlockSpec((n_chunk, tm, n_out), lambda i: (0, i, 0))
        elif out_ch:
            assert b_ch and not a_ch
            grid, red_axis, n_red = (n_chunk, n_rt), None, 1
            a_idx, b_idx = (lambda k, i: (i + off, 0)), (lambda k, i: (k, 0, 0))
            a_blk = lambda a: (tm, a.shape[-1])
            b_blk = lambda b: (None,) + tuple(b.shape[-2:])
            out_shape, out_spec = (n_chunk, rows, n_out), pl.BlockSpec((None, tm, n_out), lambda k, i: (k, i, 0))
        else:
            assert not (a_ch or b_ch)
            grid, red_axis, n_red = (n_rt,), None, 1
            a_idx, b_idx = (lambda i: (i + off, 0)), (lambda i: (0, 0))
            a_blk = lambda a: (tm, a.shape[-1])
            b_blk = lambda b: tuple(b.shape)
            out_shape, out_spec = (rows, n_out), pl.BlockSpec((tm, n_out), lambda i: (i, 0))
        acc_shape = (tm, n_out)

    post_ins, post_fn, out_dtypes = ([], None, [out_dtype]) if post is None else post
    hosted = _Hosted(rider, 2 * n_pair + len(post_ins), len(out_dtypes), int(n_red > 1), grid)

    def body(*refs):
        ins, outs, scr = hosted.split(refs)

        def compute():
            acc = None
            for p in range(n_pair):
                for k in ([None] if not fold else range(n_chunk)):
                    pick = (lambda r: r[...]) if k is None else (lambda r: r[k])
                    d = lax.dot_general(pick(ins[2 * p]).astype(BF16), pick(ins[2 * p + 1]).astype(BF16), dims,
                                        preferred_element_type=F32)
                    acc = d if acc is None else acc + d
            return acc

        def emit(acc):
            vals = (acc,) if post_fn is None else post_fn(acc, *[r[...].astype(F32) for r in ins[2 * n_pair:]])
            for o_ref, v in zip(outs, vals):
                o_ref[...] = v.astype(o_ref.dtype)

        if out_ch and fold:
            a_tile = ins[0][...].astype(BF16)
            for k in range(n_chunk):
                acc = lax.dot_general(a_tile, ins[1][k].astype(BF16), dims, preferred_element_type=F32)
                vals = (acc,) if post_fn is None else post_fn(acc, *[r[k].astype(F32) for r in ins[2:]])
                for o_ref, v in zip(outs, vals):
                    o_ref[k] = v.astype(o_ref.dtype)
        elif n_red == 1:
            emit(compute())
        else:
            acc_ref = scr[0]
            r = pl.program_id(red_axis)

            @pl.when(r == 0)
            def _():
                acc_ref[...] = jnp.zeros_like(acc_ref)

            acc_ref[...] += compute()

            @pl.when(r == n_red - 1)
            def _():
                emit(acc_ref[...])
        hosted.finish()

    in_specs, args, vmem = [], [], 0
    for a, b in pairs:
        in_specs += [pl.BlockSpec(a_blk(a), a_idx), pl.BlockSpec(b_blk(b), b_idx)]
        args += [a, b]
        vmem += 2 * (_nbytes([s for s in a_blk(a) if s], a.dtype) + _nbytes([s for s in b_blk(b) if s], b.dtype))
    in_specs += [out_spec] * len(post_ins)
    args += list(post_ins)
    tiles_per_step = n_chunk if (out_ch and fold) else 1
    vmem += (3 + 2 * n_pair + tiles_per_step * (len(post_ins) + len(out_dtypes))) * _nbytes(acc_shape, F32)
    scratch = [pltpu.VMEM(acc_shape, F32)] if n_red > 1 else []
    in_specs, out_shapes, out_specs, scratch, args = hosted.call_args(
        in_specs, [_big(out_shape, dt) for dt in out_dtypes], [out_spec] * len(out_dtypes), scratch, args)
    return hosted.results(pl.pallas_call(
        body, name=name, out_shape=out_shapes, grid=grid, in_specs=in_specs, out_specs=out_specs,
        scratch_shapes=scratch, compiler_params=_params(vmem + (8 << 20), len(grid)),
    )(*_in_hbm(args)))


def row(arr, width=None, cb=0, roff=0):
    return (arr, arr.shape[-1] if width is None else width, cb, roff)


def _row_spec(desc, tm, limit=None):
    _, width, cb, roff = desc
    if limit is None:
        return pl.BlockSpec((tm, width), lambda i: (i + roff, cb))
    return pl.BlockSpec((tm, width), lambda i: (jnp.minimum(i, limit - 1) + roff, cb))


def _segmenter(tm, seq_len, n_lat):
    seg = lambda i: jnp.where(i * tm < n_lat, (i * tm) // seq_len, n_lat // seq_len)
    first = lambda i: jnp.where(i * tm < n_lat, (i * tm) % seq_len == 0, i * tm == n_lat)
    return seg, first


def rowwise(name, fn, rows, segs, params, outs, *, tm, n_tiles, seg_fn=None, rider=None):
    n_r, n_s, n_p = len(rows), len(segs), len(params)
    hosted = _Hosted(rider, n_r + n_s + n_p, len(outs), 0, (n_tiles,))

    def body(*refs):
        ins, out_refs, _ = hosted.split(refs)
        vals = [r[...].astype(F32) for r in ins[:n_r]] + [r[...] for r in ins[n_r:]]
        res = fn(*vals)
        for o_ref, v in zip(out_refs, res):
            o_ref[...] = v.astype(o_ref.dtype)
        hosted.finish()

    in_specs = [_row_spec(d, tm) for d in rows]
    in_specs += [pl.BlockSpec((None, 1, s.shape[-1]), lambda i: (seg_fn(i), 0, 0)) for s in segs]
    in_specs += [pl.BlockSpec(p.shape, lambda i: (0, 0)) for p in params]
    vmem = sum(2 * tm * d[1] * 4 for d in rows) + sum(3 * tm * w * 4 for _, w, _ in outs) + sum(2 * p.size * 4 for p in params)
    in_specs, out_shapes, out_specs, scratch, args = hosted.call_args(
        in_specs, [_big((r, w), dt) for r, w, dt in outs],
        [pl.BlockSpec((tm, w), lambda i: (i, 0)) for _, w, _ in outs], [], [d[0] for d in rows] + list(segs) + list(params))
    return hosted.results(pl.pallas_call(
        body, name=name, grid=(n_tiles,), in_specs=in_specs, out_shape=out_shapes, out_specs=out_specs,
        scratch_shapes=scratch, compiler_params=_params(2 * vmem + (8 << 20)),
    )(*_in_hbm(args)), unwrap=False)


def rowwise_bwd(name, fn, rows, segs, params, cts, row_grads, *, tm, n_tiles, seg_fn=None, first_fn=None, adds=None,
                rider=None):
    adds = adds or {}
    need = [k for k, v in enumerate(row_grads) if v is not None]
    n_r, n_s, n_p = len(rows), len(segs), len(params)
    n_ct = sum(len(lst) for lst in cts)
    add_keys = sorted(adds)
    hosted = _Hosted(rider, n_r + n_s + n_p + n_ct + len(add_keys), len(need) + n_s + n_p, 0, (n_tiles,))

    def body(*refs):
        host_in, host_out, _ = hosted.split(refs)
        it = iter(list(host_in) + list(host_out))
        row_refs = [next(it) for _ in range(n_r)]
        seg_refs = [next(it) for _ in range(n_s)]
        par_refs = [next(it) for _ in range(n_p)]
        ct_refs = [[next(it) for _ in lst] for lst in cts]
        add_refs = {k: next(it) for k in add_keys}
        rg_refs = {k: next(it) for k in need}
        sg_refs = [next(it) for _ in range(n_s)]
        pg_refs = [next(it) for _ in range(n_p)]
        i = pl.program_id(0)
        rv = [r[...].astype(F32) for r in row_refs]
        sv = [r[...] for r in seg_refs]
        pv = [r[...] for r in par_refs]

        def f(*args):
            rr = list(rv)
            for j, k in enumerate(need):
                rr[k] = args[j]
            return fn(*rr, *args[len(need):])

        _, vjp = jax.vjp(f, *[rv[k] for k in need], *sv, *pv)
        ctv = []
        for lst in ct_refs:
            acc = lst[0][...].astype(F32)
            for r in lst[1:]:
                acc = acc + r[...].astype(F32)
            ctv.append(acc)
        g = vjp(tuple(ctv))
        for j, k in enumerate(need):
            gv = g[j]
            if k in adds:
                lim = adds[k][1]
                av = add_refs[k][...].astype(F32)
                gv = gv + (av if lim is None else jnp.where(i < lim, av, 0.0))
            lim = row_grads[k][2]
            if lim is None:
                rg_refs[k][...] = gv.astype(rg_refs[k].dtype)
            else:
                @pl.when(i < lim)
                def _(gv=gv, k=k):
                    rg_refs[k][...] = gv.astype(rg_refs[k].dtype)
        if n_s:
            opens = first_fn(i)
            for ref, gv in zip(sg_refs, g[len(need):len(need) + n_s]):
                @pl.when(opens)
                def _(ref=ref, gv=gv):
                    ref[...] = gv

                @pl.when(jnp.logical_not(opens))
                def _(ref=ref, gv=gv):
                    ref[...] += gv
        for ref, gv in zip(pg_refs, g[len(need) + n_s:]):
            @pl.when(i == 0)
            def _(ref=ref, gv=gv):
                ref[...] = gv

            @pl.when(i > 0)
            def _(ref=ref, gv=gv):
                ref[...] += gv
        hosted.finish()

    seg_spec = lambda s: pl.BlockSpec((None, 1, s.shape[-1]), lambda i: (seg_fn(i), 0, 0))
    par_spec = lambda p: pl.BlockSpec(p.shape, lambda i: (0, 0))
    in_specs = [_row_spec(d, tm) for d in rows] + [seg_spec(s) for s in segs] + [par_spec(p) for p in params]
    args = [d[0] for d in rows] + list(segs) + list(params)
    for lst in cts:
        in_specs += [_row_spec(d, tm) for d in lst]
        args += [d[0] for d in lst]
    for k in add_keys:
        in_specs.append(_row_spec(adds[k][0], tm, adds[k][1]))
        args.append(adds[k][0][0])
    out_shape, out_specs = [], []
    for k in need:
        n_rows, dt, lim = row_grads[k]
        out_shape.append(_big((n_rows, rows[k][1]), dt))
        out_specs.append(_row_spec((None, rows[k][1], 0, 0), tm, lim))
    for s in segs:
        out_shape.append(jax.ShapeDtypeStruct(s.shape, F32))
        out_specs.append(seg_spec(s))
    for p in params:
        out_shape.append(jax.ShapeDtypeStruct(p.shape, F32))
        out_specs.append(par_spec(p))
    vmem = sum(tm * d[1] * 4 for d in rows) * 6 + n_ct * tm * max(d[1] for d in rows) * 8
    in_specs, out_shape, out_specs, scratch, args = hosted.call_args(in_specs, out_shape, out_specs, [], args)
    return hosted.results(pl.pallas_call(
        body, name=name, grid=(n_tiles,), in_specs=in_specs, out_shape=out_shape, out_specs=out_specs,
        scratch_shapes=scratch, compiler_params=_params(vmem + (8 << 20)),
    )(*_in_hbm(args)), unwrap=False)


def _silu(v):
    return v * jax.nn.sigmoid(v)


def _rms(v, w):
    return v * lax.rsqrt(jnp.mean(v * v, axis=-1, keepdims=True) + EPS) * w


def fn_norm_mod(x, shift, scale, w):
    return (_rms(x, w) * (1.0 + scale) + shift,)


def fn_act(g, u):
    return (_silu(g) * u,)


def make_fn_resid(coef):
    def fn(x, f, gate):
        return (x + coef * gate * f,)
    return fn


def fn_silu_bias(v, b):
    return (_silu(v + b),)


def make_fn_gate_groupnorm(width):
    half = width // 2

    def fn(yf, yb, z, w):
        y = (yf + yb) * _silu(z)
        lane = lax.broadcasted_iota(jnp.int32, y.shape, 1)
        lo = lane < half
        sq = y * y
        s_lo = jnp.sum(jnp.where(lo, sq, 0.0), axis=-1, keepdims=True)
        s_hi = jnp.sum(jnp.where(lo, 0.0, sq), axis=-1, keepdims=True)
        r = jnp.where(lo, lax.rsqrt(s_lo / half + EPS), lax.rsqrt(s_hi / half + EPS))
        return (y * r * w,)
    return fn


def fn_glu(a, b):
    return (a * jax.nn.sigmoid(b),)


def fn_ln_silu(vw, vh, cb, lw, lb):
    v = jnp.concatenate([vw, vh], axis=-1) + cb
    mu = jnp.mean(v, axis=-1, keepdims=True)
    var = jnp.mean(jnp.square(v - mu), axis=-1, keepdims=True)
    return (_silu((v - mu) * lax.rsqrt(var + EPS) * lw + lb),)


def _col_tile(width):
    return width // 3 if width % (3 * LANES) == 0 else width


def mod_fwd(a_rows, w_shard, b_shard):
    n, d = a_rows.shape
    ws = w_shard.shape[1]
    tn = _col_tile(ws)

    def body(a_ref, w_ref, b_ref, o_ref):
        a = _silu(a_ref[...]).astype(BF16)
        o_ref[...] = jnp.dot(a, w_ref[...].astype(BF16), preferred_element_type=F32) + b_ref[...]

    return pl.pallas_call(
        body, name="mod_fwd", grid=(ws // tn,), out_shape=jax.ShapeDtypeStruct((n, ws), F32),
        in_specs=[pl.BlockSpec((n, d), lambda j: (0, 0)), pl.BlockSpec((d, tn), lambda j: (0, j)),
                  pl.BlockSpec((1, tn), lambda j: (0, j))],
        out_specs=pl.BlockSpec((n, tn), lambda j: (0, j)), compiler_params=_params(),
    )(a_rows, w_shard, b_shard)


def mod_bwd(a_rows, d_shard, d_full, w_shard, ctx_rows):
    n, d = a_rows.shape
    ws = w_shard.shape[1]
    tn = _col_tile(ws)
    n_ct = ws // tn

    def body(a_ref, ds_ref, df_ref, w_ref, gw_ref, gb_ref, q_ref):
        j = pl.program_id(0)
        a = _silu(a_ref[...])
        ds = ds_ref[...]
        gw_ref[...] = lax.dot_general(a, ds, _DIMS["tn"], precision=HI, preferred_element_type=F32)
        dctx = ds[ctx_rows[0]:ctx_rows[0] + 1, :]
        for r in ctx_rows[1:]:
            dctx = dctx + ds[r:r + 1, :]
        q = lax.dot_general(jnp.broadcast_to(dctx, (8, tn)), w_ref[...], _DIMS["nt"], precision=HI,
                            preferred_element_type=F32)

        @pl.when(j == 0)
        def _():
            q_ref[...] = q
            df = df_ref[...]
            acc = df[0:1, :]
            for r in range(1, n):
                acc = acc + df[r:r + 1, :]
            gb_ref[...] = acc

        @pl.when(j > 0)
        def _():
            q_ref[...] += q

    return pl.pallas_call(
        body, name="mod_bwd", grid=(n_ct,),
        out_shape=(jax.ShapeDtypeStruct((d, ws), F32), jax.ShapeDtypeStruct((1, d_full.shape[1]), F32),
                   jax.ShapeDtypeStruct((8, d), F32)),
        in_specs=[pl.BlockSpec((n, d), lambda j: (0, 0)), pl.BlockSpec((n, tn), lambda j: (0, j)),
                  pl.BlockSpec(d_full.shape, lambda j: (0, 0)), pl.BlockSpec((d, tn), lambda j: (0, j))],
        out_specs=(pl.BlockSpec((d, tn), lambda j: (0, j)), pl.BlockSpec((1, d_full.shape[1]), lambda j: (0, 0)),
                   pl.BlockSpec((8, d), lambda j: (0, 0))),
        compiler_params=_params(40 << 20),
    )(a_rows, d_shard, d_full, w_shard)


def _shifted(xs, d, tok, width):
    if d == 0:
        return xs
    n = xs.shape[0]
    sh = pltpu.roll(xs, (-d) % n, axis=0)
    return jnp.where((tok + d >= 0) & (tok + d < width), sh, 0.0)


def _placed(out_shape, place):
    if place is None:
        return out_shape, 0, 0, None
    return place


def tapsum_roll(name, x, xcb, w, wcb, *, seq_len, n_seq, row_blk_off, width, piece, cb, ncb, pad, flip, place=None):
    n_tap = w.shape[0]
    n_piece = seq_len // piece
    out_shape, o_rb, o_cb, into = _placed((n_seq * seq_len, ncb * cb), place)

    def body(x_ref, w_ref, *rest):
        o_ref = rest[-1]
        wv = w_ref[...]
        tok = lax.broadcasted_iota(jnp.int32, (piece, 1), 0) % width

        def do_piece(p, carry):
            start = pl.multiple_of(p * piece, piece)
            xs = x_ref[pl.ds(start, piece), :]
            acc = jnp.zeros_like(xs)
            for k in range(n_tap):
                d = pad - k if flip else k - pad
                acc = acc + wv[k:k + 1, :] * _shifted(xs, d, tok, width)
            o_ref[pl.ds(start, piece), :] = acc
            return carry

        lax.fori_loop(0, n_piece, do_piece, 0)

    extra = [] if into is None else [into]
    return pl.pallas_call(
        body, name=name, grid=(ncb, n_seq), out_shape=_big(out_shape, F32),
        in_specs=[pl.BlockSpec((seq_len, cb), lambda j, s: (row_blk_off + s, xcb + j)),
                  pl.BlockSpec((n_tap, cb), lambda j, s: (0, wcb + j))] + [pl.BlockSpec(memory_space=pl.ANY)] * len(extra),
        out_specs=pl.BlockSpec((seq_len, cb), lambda j, s: (o_rb + s, o_cb + j)),
        input_output_aliases={2: 0} if extra else {},
        compiler_params=_params(8 * seq_len * cb * 4 + (8 << 20), 2),
    )(*_in_hbm([x, w] + extra))


def tapgrad_roll(name, dy, dycb, dy_blk_off, x, xcb, x_blk_off, *, n_tap, seq_len, n_seq, width, piece, cb, ncb, pad):
    n_piece = seq_len // piece

    def body(dy_ref, x_ref, o_ref):
        @pl.when(pl.program_id(1) == 0)
        def _():
            o_ref[...] = jnp.zeros_like(o_ref)

        tok = lax.broadcasted_iota(jnp.int32, (piece, 1), 0) % width

        def do_piece(p, carry):
            start = pl.multiple_of(p * piece, piece)
            xs = x_ref[pl.ds(start, piece), :]
            dv = dy_ref[pl.ds(start, piece), :]
            for k in range(n_tap):
                o_ref[k:k + 1, :] += jnp.sum(dv * _shifted(xs, k - pad, tok, width), axis=0, keepdims=True)
            return carry

        lax.fori_loop(0, n_piece, do_piece, 0)

    return pl.pallas_call(
        body, name=name, grid=(ncb, n_seq), out_shape=jax.ShapeDtypeStruct((n_tap, ncb * cb), F32),
        in_specs=[pl.BlockSpec((seq_len, cb), lambda j, s: (dy_blk_off + s, dycb + j)),
                  pl.BlockSpec((seq_len, cb), lambda j, s: (x_blk_off + s, xcb + j))],
        out_specs=pl.BlockSpec((n_tap, cb), lambda j, s: (0, j)),
        compiler_params=_params(8 * seq_len * cb * 4 + (8 << 20), 2),
    )(*_in_hbm([dy, x]))


def tapsum_rows(name, x, xcb, w, wcb, *, seq_len, n_seq, cb, ncb, pad, flip, place=None):
    n_tap = w.shape[0]
    n_row = seq_len // GRID_W
    halo = pad * GRID_W
    out_shape, o_rb, o_cb, into = _placed((n_seq * seq_len, ncb * cb), place)

    def body(x_ref, w_ref, *rest):
        o_ref, xp = rest[-2:]
        xp[pl.ds(0, halo), :] = jnp.zeros((halo, cb), F32)
        xp[pl.ds(halo + seq_len, halo), :] = jnp.zeros((halo, cb), F32)
        xp[pl.ds(halo, seq_len), :] = x_ref[...]
        wv = w_ref[...]

        def do_row(r, carry):
            acc = jnp.zeros((GRID_W, cb), F32)
            for k in range(n_tap):
                d = pad - k if flip else k - pad
                acc = acc + wv[k:k + 1, :] * xp[pl.ds(pl.multiple_of((r + pad + d) * GRID_W, GRID_W), GRID_W), :]
            o_ref[pl.ds(pl.multiple_of(r * GRID_W, GRID_W), GRID_W), :] = acc
            return carry

        lax.fori_loop(0, n_row, do_row, 0)

    extra = [] if into is None else [into]
    return pl.pallas_call(
        body, name=name, grid=(ncb, n_seq), out_shape=_big(out_shape, F32),
        in_specs=[pl.BlockSpec((seq_len, cb), lambda j, s: (s, xcb + j)),
                  pl.BlockSpec((n_tap, cb), lambda j, s: (0, wcb + j))] + [pl.BlockSpec(memory_space=pl.ANY)] * len(extra),
        out_specs=pl.BlockSpec((seq_len, cb), lambda j, s: (o_rb + s, o_cb + j)),
        input_output_aliases={2: 0} if extra else {},
        scratch_shapes=[pltpu.VMEM((seq_len + 2 * halo, cb), F32)],
        compiler_params=_params(10 * seq_len * cb * 4 + (8 << 20), 2),
    )(*_in_hbm([x, w] + extra))


def tapgrad_rows(name, dy, dycb, x, xcb, *, n_tap, seq_len, n_seq, cb, ncb, pad):
    n_row = seq_len // GRID_W
    halo = pad * GRID_W

    def body(dy_ref, x_ref, o_ref, xp):
        @pl.when(pl.program_id(1) == 0)
        def _():
            o_ref[...] = jnp.zeros_like(o_ref)

        xp[pl.ds(0, halo), :] = jnp.zeros((halo, cb), F32)
        xp[pl.ds(halo + seq_len, halo), :] = jnp.zeros((halo, cb), F32)
        xp[pl.ds(halo, seq_len), :] = x_ref[...]

        def do_row(r, carry):
            dv = dy_ref[pl.ds(pl.multiple_of(r * GRID_W, GRID_W), GRID_W), :]
            for k in range(n_tap):
                xs = xp[pl.ds(pl.multiple_of((r + k) * GRID_W, GRID_W), GRID_W), :]
                o_ref[k:k + 1, :] += jnp.sum(dv * xs, axis=0, keepdims=True)
            return carry

        lax.fori_loop(0, n_row, do_row, 0)

    return pl.pallas_call(
        body, name=name, grid=(ncb, n_seq), out_shape=jax.ShapeDtypeStruct((n_tap, ncb * cb), F32),
        in_specs=[pl.BlockSpec((seq_len, cb), lambda j, s: (s, dycb + j)),
                  pl.BlockSpec((seq_len, cb), lambda j, s: (s, xcb + j))],
        out_specs=pl.BlockSpec((n_tap, cb), lambda j, s: (0, j)),
        scratch_shapes=[pltpu.VMEM((seq_len + 2 * halo, cb), F32)],
        compiler_params=_params(10 * seq_len * cb * 4 + (8 << 20), 2),
    )(*_in_hbm([dy, x]))


def _ssd_blocks(b, s, *, rev, n_ctx, n_lat, lat_blocks):
    if rev:
        return jnp.where(s < n_ctx, lat_blocks + b * n_ctx + (n_ctx - 1 - s), b * n_lat + (n_lat - 1 - (s - n_ctx)))
    return jnp.where(s < n_ctx, lat_blocks + b * n_ctx + s, b * n_lat + (s - n_ctx))


def _ssd_common(xbc, raw, dtb, alog, dsk, *, rev, ds, n_head):
    if rev:
        raw = pltpu.roll(raw, LANES - n_head, axis=1)
    pre = raw + dtb
    dt = jnp.maximum(pre, 0.0) + jnp.log1p(jnp.exp(-jnp.abs(pre)))
    sig = jax.nn.sigmoid(pre)
    a = -jnp.exp(alog)
    da = dt * a
    ri = lax.broadcasted_iota(jnp.int32, (CHUNK, CHUNK), 0)
    ci = lax.broadcasted_iota(jnp.int32, (CHUNK, CHUNK), 1)
    mask = (ci >= ri) if rev else (ci <= ri)
    tri = mask.astype(F32)
    tri_t = ((ci <= ri) if rev else (ci >= ri)).astype(F32)
    cs = jnp.dot(tri, da, precision=HI, preferred_element_type=F32)
    tot = jnp.sum(da, axis=0, keepdims=True)
    def wide(v):
        first = lax.broadcasted_iota(jnp.int32, (v.shape[0], LANES), 1) < HEAD_DIM
        return jnp.concatenate(
            [jnp.where(first, jnp.broadcast_to(v[:, 2 * p:2 * p + 1], first.shape),
                       jnp.broadcast_to(v[:, 2 * p + 1:2 * p + 2], first.shape)) for p in range(n_head // 2)], axis=1)

    cs_w, tot_w = wide(cs), wide(tot)
    xh = xbc[:, :ds]
    dt_w = wide(dt)
    return dict(
        dt=dt, sig=sig, a=a, cs=cs, cs_t=cs.T, tot=tot, mask=mask, tri_t=tri_t,
        e_w=jnp.exp(cs_w), wt_w=jnp.exp(tot_w - cs_w), dec_w=jnp.exp(tot_w), dt_w=dt_w, dsk_w=wide(dsk),
        xh=xh, xs_w=xh * dt_w, bm=xbc[:, ds:ds + 2 * N_STATE], cm=xbc[:, ds + 2 * N_STATE:ds + 4 * N_STATE])


def _decay(q, col):
    seg = q["cs"][:, col:col + 1] - q["cs_t"][col:col + 1, :]
    return jnp.exp(jnp.where(q["mask"], seg, -jnp.inf))


def _split_heads(v):
    lane = lax.broadcasted_iota(jnp.int32, v.shape, 1)
    return jnp.concatenate([jnp.where(lane < HEAD_DIM, v, 0.0), jnp.where(lane >= HEAD_DIM, v, 0.0)], axis=0)


def ssd_fwd(name, xbc, proj, dt_cb, dtb, alog, dsk, *, rev, n_ex, seq_len, ctx_len, ds, rider=None):
    n_head, half = ds // HEAD_DIM, ds // 2
    n_ctx, n_lat = ctx_len // CHUNK, seq_len // CHUNK
    n_step = n_ctx + n_lat
    blk = functools.partial(_ssd_blocks, rev=rev, n_ctx=n_ctx, n_lat=n_lat, lat_blocks=n_ex * n_lat)
    xw = xbc.shape[1]

    def y_blk(b, s):
        sl = jnp.maximum(s, n_ctx) - n_ctx
        return b * n_lat + ((n_lat - 1 - sl) if rev else sl)

    hosted = _Hosted(rider, 5, 2, 1, (n_ex, n_step))

    def body(*refs):
        (xbc_ref, dt_ref, dtb_ref, alog_ref, dsk_ref), (y_ref, hs_ref), (h_scr,) = hosted.split(refs)

        @pl.when(pl.program_id(1) == 0)
        def _():
            h_scr[...] = jnp.zeros_like(h_scr)

        q = _ssd_common(xbc_ref[...], dt_ref[...], dtb_ref[...], alog_ref[...], dsk_ref[...], rev=rev, ds=ds, n_head=n_head)
        h = h_scr[...]
        hs_ref[...] = h
        for g in range(2):
            lo = g * half
            bg = q["bm"][:, g * N_STATE:(g + 1) * N_STATE].astype(BF16)
            cg = q["cm"][:, g * N_STATE:(g + 1) * N_STATE].astype(BF16)
            scores = lax.dot_general(cg, bg, _DIMS["nt"], preferred_element_type=F32)
            hg = h[:, lo:lo + half]
            off = jnp.dot(cg, hg.astype(BF16), preferred_element_type=F32)
            for j in range(half // LANES):
                c0 = (lo + j * LANES) // HEAD_DIM
                ln = slice(lo + j * LANES, lo + (j + 1) * LANES)
                p_cat = jnp.concatenate([scores * _decay(q, c0), scores * _decay(q, c0 + 1)], axis=1).astype(BF16)
                diag = jnp.dot(p_cat, _split_heads(q["xs_w"][:, ln]).astype(BF16), preferred_element_type=F32)
                y_ref[:, ln] = (diag + q["e_w"][:, ln] * off[:, j * LANES:(j + 1) * LANES]
                                + q["dsk_w"][:, ln] * q["xh"][:, ln])
            v = (q["wt_w"][:, lo:lo + half] * q["xs_w"][:, lo:lo + half]).astype(BF16)
            h_scr[:, lo:lo + half] = (q["dec_w"][:, lo:lo + half] * hg
                                      + lax.dot_general(bg, v, _DIMS["tn"], preferred_element_type=F32))
        hosted.finish()

    vec = pl.BlockSpec((1, LANES), lambda b, s: (0, 0))
    in_specs, out_shape, out_specs, scratch, args = hosted.call_args(
        [pl.BlockSpec((CHUNK, xw), lambda b, s: (blk(b, s), 0)),
         pl.BlockSpec((CHUNK, LANES), lambda b, s: (blk(b, s), dt_cb)), vec, vec, vec],
        (_big((n_ex * seq_len, ds), F32), _big((n_ex, n_step, N_STATE, ds), F32)),
        (pl.BlockSpec((CHUNK, ds), lambda b, s: (y_blk(b, s), 0)),
         pl.BlockSpec((None, None, N_STATE, ds), lambda b, s: (b, s, 0, 0))),
        [pltpu.VMEM((N_STATE, ds), F32)], [xbc, proj, dtb, alog, dsk])
    return hosted.results(pl.pallas_call(
        body, name=name, grid=(n_ex, n_step), out_shape=out_shape, in_specs=in_specs, out_specs=out_specs,
        scratch_shapes=scratch, compiler_params=_params(40 << 20, 2),
    )(*_in_hbm(args)))


def ssd_bwd(name, xbc, proj, dt_cb, hs, dy, dtb, alog, dsk, *, rev, n_ex, seq_len, ctx_len, ds, rider=None):
    n_head, half = ds // HEAD_DIM, ds // 2
    n_ctx, n_lat = ctx_len // CHUNK, seq_len // CHUNK
    n_step = n_ctx + n_lat
    n_tok = n_ex * (seq_len + ctx_len)
    blk0 = functools.partial(_ssd_blocks, rev=rev, n_ctx=n_ctx, n_lat=n_lat, lat_blocks=n_ex * n_lat)
    step = lambda sp: n_step - 1 - sp
    blk = lambda b, sp: blk0(b, step(sp))
    xw = xbc.shape[1]

    def dy_blk(b, sp):
        sl = jnp.maximum(step(sp), n_ctx) - n_ctx
        return b * n_lat + ((n_lat - 1 - sl) if rev else sl)

    hosted = _Hosted(rider, 7, 5, 1, (n_ex, n_step))

    def body(*refs):
        ((xbc_ref, dt_ref, hs_ref, dy_ref, dtb_ref, alog_ref, dsk_ref),
         (dxbc_ref, ddt_ref, dalog_ref, ddtb_ref, ddsk_ref), (dh_scr,)) = hosted.split(refs)
        b, sp = pl.program_id(0), pl.program_id(1)

        @pl.when(sp == 0)
        def _():
            dh_scr[...] = jnp.zeros_like(dh_scr)

        @pl.when((sp == 0) & (b == 0))
        def _():
            dalog_ref[...] = jnp.zeros_like(dalog_ref)
            ddtb_ref[...] = jnp.zeros_like(ddtb_ref)
            ddsk_ref[...] = jnp.zeros_like(ddsk_ref)

        q = _ssd_common(xbc_ref[...], dt_ref[...], dtb_ref[...], alog_ref[...], dsk_ref[...], rev=rev, ds=ds, n_head=n_head)
        h = hs_ref[...]
        d_y = jnp.where(step(sp) >= n_ctx, dy_ref[...], 0.0)
        dh_next = dh_scr[...]
        lane_row = lax.broadcasted_iota(jnp.int32, (1, LANES), 1)
        d_cs = jnp.zeros((CHUNK, LANES), F32)
        dxs_parts, de_parts, dwt_parts, ddec_parts = [], [], [], []
        for g in range(2):
            lo = g * half
            gs = slice(lo, lo + half)
            bg = q["bm"][:, g * N_STATE:(g + 1) * N_STATE].astype(BF16)
            cg = q["cm"][:, g * N_STATE:(g + 1) * N_STATE].astype(BF16)
            scores = lax.dot_general(cg, bg, _DIMS["nt"], preferred_element_type=F32)
            hg, dyg, dhn = h[:, gs], d_y[:, gs], dh_next[:, gs]
            off = jnp.dot(cg, hg.astype(BF16), preferred_element_type=F32)
            d_off = (q["e_w"][:, gs] * dyg).astype(BF16)
            de_parts.append(dyg * off)
            d_c = lax.dot_general(d_off, hg.astype(BF16), _DIMS["nt"], preferred_element_type=F32)
            dh_scr[:, gs] = (lax.dot_general(cg, d_off, _DIMS["tn"], preferred_element_type=F32)
                             + q["dec_w"][:, gs] * dhn)
            b_dh = jnp.dot(bg, dhn.astype(BF16), preferred_element_type=F32)
            v = q["wt_w"][:, gs] * q["xs_w"][:, gs]
            d_b = lax.dot_general(v.astype(BF16), dhn.astype(BF16), _DIMS["nt"], preferred_element_type=F32)
            dwt_parts.append(q["xs_w"][:, gs] * b_dh)
            ddec_parts.append(jnp.sum(hg * dhn, axis=0, keepdims=True))
            d_scores = jnp.zeros((CHUNK, CHUNK), F32)
            for j in range(half // LANES):
                c0 = (lo + j * LANES) // HEAD_DIM
                ln = slice(lo + j * LANES, lo + (j + 1) * LANES)
                l0, l1 = _decay(q, c0), _decay(q, c0 + 1)
                p0, p1 = scores * l0, scores * l1
                dy_st = _split_heads(d_y[:, ln]).astype(BF16)
                d_p = lax.dot_general(dy_st, q["xs_w"][:, ln].astype(BF16), _DIMS["nt"], preferred_element_type=F32)
                d_p0, d_p1 = d_p[:CHUNK], d_p[CHUNK:]
                d_scores = d_scores + d_p0 * l0 + d_p1 * l1
                for col, t in ((c0, d_p0 * p0), (c0 + 1, d_p1 * p1)):
                    d_cs = d_cs + jnp.sum(t - t.T, axis=1, keepdims=True) * (lane_row == col).astype(F32)
                p_st = jnp.concatenate([p0, p1], axis=0).astype(BF16)
                dxs_parts.append(lax.dot_general(p_st, dy_st, _DIMS["tn"], preferred_element_type=F32)
                                 + q["wt_w"][:, ln] * b_dh[:, j * LANES:(j + 1) * LANES])
            d_sc = d_scores.astype(BF16)
            d_c = d_c + jnp.dot(d_sc, bg, preferred_element_type=F32)
            d_b = d_b + lax.dot_general(d_sc, cg, _DIMS["tn"], preferred_element_type=F32)
            dxbc_ref[:, ds + g * N_STATE:ds + (g + 1) * N_STATE] = d_b
            dxbc_ref[:, ds + (2 + g) * N_STATE:ds + (3 + g) * N_STATE] = d_c
        d_xs = jnp.concatenate(dxs_parts, axis=1)
        narrow_m = (lax.broadcasted_iota(jnp.int32, (ds, LANES), 0) // HEAD_DIM
                    == lax.broadcasted_iota(jnp.int32, (ds, LANES), 1)).astype(BF16)
        rows8 = lambda v: jnp.broadcast_to(v, (8, ds))
        stacked = jnp.concatenate(
            [jnp.concatenate(dwt_parts, axis=1), jnp.concatenate(de_parts, axis=1), d_xs * q["xh"],
             rows8(jnp.concatenate(ddec_parts, axis=1)), rows8(jnp.sum(d_y * q["xh"], axis=0, keepdims=True))], axis=0)
        hi = stacked.astype(BF16)
        lo = (stacked - hi.astype(F32)).astype(BF16)
        sums = (jnp.dot(hi, narrow_m, preferred_element_type=F32) + jnp.dot(lo, narrow_m, preferred_element_type=F32))
        n_wt, n_e, n_xs = sums[:CHUNK], sums[CHUNK:2 * CHUNK], sums[2 * CHUNK:3 * CHUNK]
        n_dec, n_dsk = sums[3 * CHUNK:3 * CHUNK + 1], sums[3 * CHUNK + 8:3 * CHUNK + 9]
        e, wt, dec = jnp.exp(q["cs"]), jnp.exp(q["tot"] - q["cs"]), jnp.exp(q["tot"])
        d_wt = n_wt * wt
        d_cs = d_cs + n_e * e - d_wt
        d_tot = jnp.sum(d_wt, axis=0, keepdims=True) + n_dec * dec
        d_da = jnp.dot(q["tri_t"], d_cs, precision=HI, preferred_element_type=F32) + d_tot
        d_dt = d_da * q["a"] + n_xs
        dxbc_ref[:, :ds] = d_xs * q["dt_w"] + q["dsk_w"] * d_y
        dalog_ref[...] += jnp.sum(d_da * q["dt"], axis=0, keepdims=True) * q["a"]
        d_raw = d_dt * q["sig"]
        ddtb_ref[...] += jnp.sum(d_raw, axis=0, keepdims=True)
        ddsk_ref[...] += n_dsk
        ddt_ref[...] = pltpu.roll(d_raw, n_head, axis=1) if rev else d_raw
        hosted.finish()

    vec = pl.BlockSpec((1, LANES), lambda b, s: (0, 0))
    vec_shape = jax.ShapeDtypeStruct((1, LANES), F32)
    in_specs, out_shape, out_specs, scratch, args = hosted.call_args(
        [pl.BlockSpec((CHUNK, xw), lambda b, s: (blk(b, s), 0)),
         pl.BlockSpec((CHUNK, LANES), lambda b, s: (blk(b, s), dt_cb)),
         pl.BlockSpec((None, None, N_STATE, ds), lambda b, s: (b, step(s), 0, 0)),
         pl.BlockSpec((CHUNK, ds), lambda b, s: (dy_blk(b, s), 0)), vec, vec, vec],
        (_big((n_tok, xw), F32), _big((n_tok, LANES), F32), vec_shape, vec_shape, vec_shape),
        (pl.BlockSpec((CHUNK, xw), lambda b, s: (blk(b, s), 0)),
         pl.BlockSpec((CHUNK, LANES), lambda b, s: (blk(b, s), 0)), vec, vec, vec),
        [pltpu.VMEM((N_STATE, ds), F32)], [xbc, proj, hs, dy, dtb, alog, dsk])
    return hosted.results(pl.pallas_call(
        body, name=name, grid=(n_ex, n_step), out_shape=out_shape, in_specs=in_specs, out_specs=out_specs,
        scratch_shapes=scratch, compiler_params=_params(48 << 20, 2),
    )(*_in_hbm(args)))


def final_loss(x3, target, w, *, tm):
    n, d = x3.shape

    def body(x_ref, t_ref, w_ref, dx_ref, dw_ref, loss_ref):
        i = pl.program_id(0)
        t = t_ref[...]

        def per_feature(xv, wv):
            err = _rms(xv, wv) - t
            return 0.5 * jnp.sum(err * err, axis=0, keepdims=True) / d

        lv, vjp = jax.vjp(per_feature, x_ref[...], w_ref[...])
        dx, dw = vjp(jnp.ones_like(lv))
        dx_ref[...] = dx

        @pl.when(i == 0)
        def _():
            dw_ref[...] = dw
            loss_ref[...] = lv

        @pl.when(i > 0)
        def _():
            dw_ref[...] += dw
            loss_ref[...] += lv

    tile = pl.BlockSpec((tm, d), lambda i: (i, 0))
    vec = pl.BlockSpec((1, d), lambda i: (0, 0))
    return pl.pallas_call(
        body, name="final_loss", grid=(n // tm,), in_specs=[tile, tile, vec],
        out_shape=(jax.ShapeDtypeStruct((n, d), F32), jax.ShapeDtypeStruct((1, d), F32), jax.ShapeDtypeStruct((1, d), F32)),
        out_specs=(tile, vec, vec), compiler_params=_params(tm * d * 4 * 16 + (8 << 20)),
    )(x3, target, w)


def sum_slots(name, arr):
    n_slot, n_row, width = arr.shape
    tm = _row_tile(n_row, width * n_slot, mult=16)

    def body(a_ref, o_ref):
        acc = a_ref[0].astype(F32)
        for j in range(1, n_slot):
            acc = acc + a_ref[j].astype(F32)
        o_ref[...] = acc

    return pl.pallas_call(
        body, name=name, grid=(n_row // tm,), out_shape=jax.ShapeDtypeStruct((n_row, width), F32),
        in_specs=[pl.BlockSpec((n_slot, tm, width), lambda i: (0, i, 0))],
        out_specs=pl.BlockSpec((tm, width), lambda i: (i, 0)), compiler_params=_params(),
    )(arr)


def adamw(name, w, g_slots, m, v):
    n_slot, n_row, width = g_slots.shape
    tm = _row_tile(n_row, width * 2)

    def body(w_ref, g_ref, m_ref, v_ref, go_ref, d_ref, mo_ref, vo_ref):
        g = g_ref[0]
        for j in range(1, n_slot):
            g = g + g_ref[j]
        m2 = ADAM_B1 * m_ref[...] + (1.0 - ADAM_B1) * g
        v2 = ADAM_B2 * v_ref[...] + (1.0 - ADAM_B2) * jnp.square(g)
        m_hat = m2 / (1.0 - ADAM_B1 ** ADAM_STEP)
        v_hat = v2 / (1.0 - ADAM_B2 ** ADAM_STEP)
        go_ref[...] = g
        d_ref[...] = -ADAM_LR * (m_hat / (jnp.sqrt(v_hat) + ADAM_EPS) + ADAM_WD * w_ref[...])
        mo_ref[...] = m2
        vo_ref[...] = v2

    tile = pl.BlockSpec((tm, width), lambda i: (i, 0))
    shape = jax.ShapeDtypeStruct((n_row, width), F32)
    return pl.pallas_call(
        body, name=name, grid=(n_row // tm,), out_shape=(shape,) * 4,
        in_specs=[tile, pl.BlockSpec((n_slot, tm, width), lambda i: (0, i, 0)), tile, tile],
        out_specs=(tile,) * 4, compiler_params=_params(),
    )(w, g_slots, m, v)


def cctx_grad(q_all, c_ctx_row):
    d = c_ctx_row.shape[1]

    def body(q_ref, c_ref, o_ref):
        acc = q_ref[0, 0:1, :]
        for j in (2, 4, 6):
            acc = acc + q_ref[j, 0:1, :]
        _, vjp = jax.vjp(_silu, c_ref[...])
        o_ref[...] = vjp(acc)[0]

    return pl.pallas_call(
        body, name="cctx_grad", out_shape=jax.ShapeDtypeStruct((1, d), F32),
    )(q_all, c_ctx_row)


def loss_total(pack_sum, d):
    def body(p_ref, o_ref):
        o_ref[...] = jnp.sum(p_ref[:, 0:d], axis=1, keepdims=True)

    return pl.pallas_call(
        body, name="loss_total", out_shape=jax.ShapeDtypeStruct((1, 1), F32),
    )(pack_sum)


class _Plan:
    def __init__(self):
        self.builders, self.got = {}, {}

    def on(self, host, key, builder):
        self.builders.setdefault(host, []).append((key, builder))

    def run(self, host, fn, *args, **kw):
        if host not in self.builders:
            return fn(host, *args, **kw)
        keys, riders = zip(*[(key, builder(self)) for key, builder in self.builders[host]])
        res, landed = fn(host, *args, rider=Riders(riders), **kw)
        for key, r in zip(keys, riders):
            self.got[key], landed = landed[:r.n], landed[r.n:]
        return res


def _val(w):
    return w() if callable(w) else w


def _matmul_tile(n_rows, tm):
    return 2 * tm if n_rows % (2 * tm) == 0 else tm


def _ffn_fwd(plan, tag, xin, n_rows, tm, seg_fn, shift, scale, gate, norm_w, wg, wu, wd):
    d = xin.shape[1]
    n_tiles = n_rows // tm
    (h,) = plan.run(f"{tag}_norm", rowwise, fn_norm_mod, [row(xin)], [shift, scale], [norm_w], [(n_rows, d, BF16)],
                    tm=tm, n_tiles=n_tiles, seg_fn=seg_fn)
    tmm = _matmul_tile(n_rows, tm)
    g = plan.run(f"{tag}_gate", matmul, [(h, _val(wg))], "nn", out_dtype=BF16, b_ch=True, out_ch=True, tm=tmm)
    u, act = plan.run(f"{tag}_up", matmul, [(h, _val(wu))], "nn", b_ch=True, out_ch=True, tm=tm, fold=True,
                      post=([g], lambda acc, gv: (acc, fn_act(gv, acc)[0]), [BF16, BF16]))
    f = plan.run(f"{tag}_down", matmul, [(act, _val(wd))], "nn", a_ch=True, b_ch=True, tm=tmm, fold=True)
    (xo,) = plan.run(f"{tag}_resid", rowwise, make_fn_resid(0.5), [row(xin), row(f)], [gate], [], [(n_rows, d, F32)],
                     tm=tm, n_tiles=n_tiles, seg_fn=seg_fn)
    return xo, (h, g, u, act, f)


def _ffn_bwd(plan, tag, d_xo, saved, xin, n_rows, tm, seg_fn, first_fn, shift, scale, gate, norm_w, wg, wu, wd, dx_rows, dx_limit):
    h, g, u, act, f = saved
    d = xin.shape[1]
    n_tiles = n_rows // tm
    n_ch, _, n_hid = g.shape
    d_f, d_gate = plan.run(f"{tag}_resid_bwd", rowwise_bwd, make_fn_resid(0.5), [row(xin), row(f)], [gate], [], [[row(d_xo)]],
                           [None, (n_rows, BF16, None)], tm=tm, n_tiles=n_tiles, seg_fn=seg_fn, first_fn=first_fn)
    tmm = _matmul_tile(n_rows, tm)
    def act_vjp(d_act, gv, uv):
        s = jax.nn.sigmoid(gv)
        gs = gv * s
        return d_act * uv * (s + gs * (1.0 - s)), d_act * gs
    d_g, d_u = plan.run(f"{tag}_down_dx", matmul, [(d_f, wd)], "nt", b_ch=True, out_ch=True, tm=tmm,
                        post=([g, u], act_vjp, [BF16, BF16]))
    plan.got[f"{tag}_d_wd"] = plan.run(f"{tag}_down_dw", matmul, [(act, d_f)], "tn", out_dtype=BF16, a_ch=True, out_ch=True, tm=tmm)
    d_h = plan.run(f"{tag}_up_dx", matmul, [(d_g, wg), (d_u, wu)], "nt", a_ch=True, b_ch=True, tm=tmm)
    plan.got[f"{tag}_d_wg"] = plan.run(f"{tag}_gate_dw", matmul, [(d_g, h)], "tn", out_dtype=BF16, a_ch=True, out_ch=True, tm=tmm)
    plan.got[f"{tag}_d_wu"] = plan.run(f"{tag}_up_dw", matmul, [(d_u, h)], "tn", out_dtype=BF16, a_ch=True, out_ch=True, tm=tmm)
    d_x, d_shift, d_scale, d_nw = plan.run(
        f"{tag}_norm_bwd", rowwise_bwd, fn_norm_mod, [row(xin)], [shift, scale], [norm_w], [[row(d_h)]], [(dx_rows, F32, dx_limit)],
        tm=tm, n_tiles=n_tiles, seg_fn=seg_fn, first_fn=first_fn, adds={0: (row(d_xo), None)})
    return d_x, (d_shift, d_scale, d_gate), d_nw


def kernel(x, c, ctx, c_ctx, w_mod, b_mod, norm_ffn1, ffn1_gate, ffn1_up, ffn1_down, norm_mix, w_in, ssm_conv_w, ssm_conv_b, dt_bias_fwd, dt_bias_bwd, a_log_fwd, a_log_bwd, ssm_d, ssm_norm_w, cconv_w, cconv_b, cconv_ln_w, cconv_ln_b, w_out, norm_ffn2, ffn2_gate, ffn2_up, ffn2_down, final_norm, loss_target, m_c_ctx, m_w_mod, m_b_mod, m_norm_ffn1, m_ffn1_gate, m_ffn1_up, m_ffn1_down, m_norm_mix, m_w_in, m_ssm_conv_w, m_ssm_conv_b, m_dt_bias_fwd, m_dt_bias_bwd, m_a_log_fwd, m_a_log_bwd, m_ssm_d, m_ssm_norm_w, m_cconv_w, m_cconv_b, m_cconv_ln_w, m_cconv_ln_b, m_w_out, m_norm_ffn2, m_ffn2_gate, m_ffn2_up, m_ffn2_down, m_final_norm, v_c_ctx, v_w_mod, v_b_mod, v_norm_ffn1, v_ffn1_gate, v_ffn1_up, v_ffn1_down, v_norm_mix, v_w_in, v_ssm_conv_w, v_ssm_conv_b, v_dt_bias_fwd, v_dt_bias_bwd, v_a_log_fwd, v_a_log_bwd, v_ssm_d, v_ssm_norm_w, v_cconv_w, v_cconv_b, v_cconv_ln_w, v_cconv_ln_b, v_w_out, v_norm_ffn2, v_ffn2_gate, v_ffn2_up, v_ffn2_down, v_final_norm):
    weights = dict(c_ctx=c_ctx, w_mod=w_mod, b_mod=b_mod, norm_ffn1=norm_ffn1, ffn1_gate=ffn1_gate, ffn1_up=ffn1_up, ffn1_down=ffn1_down, norm_mix=norm_mix, w_in=w_in, ssm_conv_w=ssm_conv_w, ssm_conv_b=ssm_conv_b, dt_bias_fwd=dt_bias_fwd, dt_bias_bwd=dt_bias_bwd, a_log_fwd=a_log_fwd, a_log_bwd=a_log_bwd, ssm_d=ssm_d, ssm_norm_w=ssm_norm_w, cconv_w=cconv_w, cconv_b=cconv_b, cconv_ln_w=cconv_ln_w, cconv_ln_b=cconv_ln_b, w_out=w_out, norm_ffn2=norm_ffn2, ffn2_gate=ffn2_gate, ffn2_up=ffn2_up, ffn2_down=ffn2_down, final_norm=final_norm)
    mom1 = dict(c_ctx=m_c_ctx, w_mod=m_w_mod, b_mod=m_b_mod, norm_ffn1=m_norm_ffn1, ffn1_gate=m_ffn1_gate, ffn1_up=m_ffn1_up, ffn1_down=m_ffn1_down, norm_mix=m_norm_mix, w_in=m_w_in, ssm_conv_w=m_ssm_conv_w, ssm_conv_b=m_ssm_conv_b, dt_bias_fwd=m_dt_bias_fwd, dt_bias_bwd=m_dt_bias_bwd, a_log_fwd=m_a_log_fwd, a_log_bwd=m_a_log_bwd, ssm_d=m_ssm_d, ssm_norm_w=m_ssm_norm_w, cconv_w=m_cconv_w, cconv_b=m_cconv_b, cconv_ln_w=m_cconv_ln_w, cconv_ln_b=m_cconv_ln_b, w_out=m_w_out, norm_ffn2=m_norm_ffn2, ffn2_gate=m_ffn2_gate, ffn2_up=m_ffn2_up, ffn2_down=m_ffn2_down, final_norm=m_final_norm)
    mom2 = dict(c_ctx=v_c_ctx, w_mod=v_w_mod, b_mod=v_b_mod, norm_ffn1=v_norm_ffn1, ffn1_gate=v_ffn1_gate, ffn1_up=v_ffn1_up, ffn1_down=v_ffn1_down, norm_mix=v_norm_mix, w_in=v_w_in, ssm_conv_w=v_ssm_conv_w, ssm_conv_b=v_ssm_conv_b, dt_bias_fwd=v_dt_bias_fwd, dt_bias_bwd=v_dt_bias_bwd, a_log_fwd=v_a_log_fwd, a_log_bwd=v_a_log_bwd, ssm_d=v_ssm_d, ssm_norm_w=v_ssm_norm_w, cconv_w=v_cconv_w, cconv_b=v_cconv_b, cconv_ln_w=v_cconv_ln_w, cconv_ln_b=v_cconv_ln_b, w_out=v_w_out, norm_ffn2=v_norm_ffn2, ffn2_gate=v_ffn2_gate, ffn2_up=v_ffn2_up, ffn2_down=v_ffn2_down, final_norm=v_final_norm)
    order = list(weights)

    n_ex, seq_len, d = x.shape
    ctx_len = ctx.shape[1]
    ds = d
    n_head = ds // HEAD_DIM
    xw = ds + 4 * N_STATE
    n_lat, n_ctx_rows = n_ex * seq_len, n_ex * ctx_len
    n_tok = n_lat + n_ctx_rows
    tm = math.gcd(math.gcd(512, seq_len), n_ctx_rows)
    seg_all, first_all = _segmenter(tm, seq_len, n_lat)
    lat_tiles = n_lat // tm

    xi, yi, ci = lax.axis_index("x"), lax.axis_index("y"), lax.axis_index("c")
    me, chip = 4 * xi + 2 * yi + ci, 2 * xi + yi

    (c_all,) = exchange("gather_c", [c], "all8")
    n_all = 8 * n_ex
    n_cond = -(-(n_all + 1) // 8) * 8
    cond = jnp.concatenate([c_all.reshape(n_all, d), c_ctx[None, :], jnp.zeros((n_cond - n_all - 1, d), F32)])
    mod_w = w_mod.shape[2]
    b_shard = lax.dynamic_slice(b_mod, (0, chip * mod_w), (1, mod_w))
    (mod_g,) = exchange("gather_mod", [mod_fwd(cond, w_mod[0], b_shard)], "chips")
    mod_full = mod_g.transpose(1, 0, 2).reshape(n_cond, N_CHIPS * mod_w)
    mod_mine = lax.dynamic_slice(mod_full, (me * n_ex, 0), (n_ex, 9 * d)).reshape(n_ex, 9, d)
    mod_ctx = mod_full[n_all].reshape(9, d)
    tabs = [jnp.concatenate([mod_mine[:, j], mod_ctx[j][None]])[:, None, :] for j in range(9)]
    lat = lambda t: t[:n_ex]

    bf = lambda w: w[0].astype(BF16)
    plan = _Plan()
    gather = lambda *ws: (lambda p: Rider(list(ws), "chips"))
    plan.on("ffn1_norm", "wg1", gather(bf(ffn1_gate)))
    plan.on("ffn1_gate", "wu1", gather(bf(ffn1_up)))
    plan.on("ffn1_up", "wd1", gather(bf(ffn1_down)))
    win_cut = d * 5 // 8
    plan.on("ffn1_down", "win_a", gather(bf(w_in)[:win_cut]))
    plan.on("ffn1_resid", "win_b", gather(bf(w_in)[win_cut:], ssm_conv_w[0], cconv_w[0]))
    xt = jnp.concatenate([x.reshape(n_lat, d), ctx.reshape(n_ctx_rows, d)])
    x1, saved1 = _ffn_fwd(plan, "ffn1", xt, n_tok, tm, seg_all, tabs[0], tabs[1], tabs[2], norm_ffn1,
                          lambda: plan.got["wg1"][0], lambda: plan.got["wu1"][0], lambda: plan.got["wd1"][0])
    (wg1,), (wu1,), (wd1,), (win_a,), (win_b, w5_g, w31_g) = (plan.got[k] for k in ("wg1", "wu1", "wd1", "win_a", "win_b"))
    win_g = jnp.concatenate([win_a, win_b], axis=1)
    unshard_cols = lambda t: t.transpose(1, 0, 2).reshape(t.shape[1], N_CHIPS * t.shape[2])
    win = unshard_cols(win_g)
    o_x, o_dt, o_glu = ds, ds + xw, ds + xw + 2 * n_head
    w_z, w_xbc, w_dt = win[:, :ds], win[:, o_x:o_dt], win[:, o_dt:o_glu]
    w_ga, w_gb = win[:, o_glu:o_glu + d], win[:, o_glu + d:]
    w_dtp = jnp.concatenate([w_dt, jnp.zeros((d, LANES - 2 * n_head), BF16)], axis=1)
    w_cat = jnp.concatenate([w_z, w_ga, w_gb, w_xbc, w_dtp], axis=1)
    cbw = d // 2
    xbc_cb, dt_cb = 3 * d // cbw, (3 * d + xw) // LANES
    w5, w31 = unshard_cols(w5_g), unshard_cols(w31_g)
    pad_vec = lambda v: jnp.concatenate([v.reshape(1, -1), jnp.zeros((1, LANES - v.size), F32)], axis=1)
    dtb_f, dtb_b, alog_f, alog_b = map(pad_vec, (dt_bias_fwd, dt_bias_bwd, a_log_fwd, a_log_bwd))
    dsk_f, dsk_b = pad_vec(ssm_d), jnp.zeros((1, LANES), F32)

    (h2,) = rowwise("mix_norm", fn_norm_mod, [row(x1)], [tabs[3], tabs[4]], [norm_mix], [(n_tok, d, BF16)],
                    tm=tm, n_tiles=n_tok // tm, seg_fn=seg_all)
    proj, (wg2,) = matmul("mix_proj", [(h2, w_cat)], "nn", tm=min(tm, 256), rider=Rider([bf(ffn2_gate)], "chips"))
    def conv5(name, src, cb0, flip):
        out = None
        for part, seq, off in (("lat", seq_len, 0), ("ctx", ctx_len, n_lat // ctx_len)):
            out = tapsum_roll(f"{name}_{part}", src, cb0, w5, 0, seq_len=seq, n_seq=n_ex, row_blk_off=off, width=seq,
                              piece=seq, cb=cbw, ncb=xw // cbw, pad=w5.shape[0] // 2, flip=flip,
                              place=((n_tok, xw), off, 0, out))
        return out

    craw = conv5("xbc_conv", proj, xbc_cb, False)
    (xbc,) = rowwise("xbc_silu", fn_silu_bias, [row(craw)], [], [ssm_conv_b], [(n_tok, xw, F32)], tm=tm, n_tiles=n_tok // tm)
    ssd = dict(n_ex=n_ex, seq_len=seq_len, ctx_len=ctx_len, ds=ds)
    (y_f, hs_f), (wu2, wd2) = ssd_fwd("ssd_fwd_f", xbc, proj, dt_cb, dtb_f, alog_f, dsk_f, rev=False,
                                      rider=Rider([bf(ffn2_up), bf(ffn2_down)], "chips"), **ssd)
    (y_b, hs_b), (wout_g,) = ssd_fwd("ssd_fwd_b", xbc, proj, dt_cb, dtb_b, alog_b, dsk_b, rev=True,
                                     rider=Rider([bf(w_out)], "chips"), **ssd)
    wout = wout_g.reshape(2 * d, d)
    wo_y, wo_u = wout[:ds], wout[ds:]
    fn_gate = make_fn_gate_groupnorm(ds)
    (yn,) = rowwise("ssd_gate", fn_gate, [row(y_f), row(y_b), row(proj, d, 0)], [], [ssm_norm_w], [(n_lat, ds, BF16)],
                    tm=tm, n_tiles=lat_tiles)
    (u0,) = rowwise("glu", fn_glu, [row(proj, d, 1), row(proj, d, 2)], [], [], [(n_lat, d, F32)], tm=tm, n_tiles=lat_tiles)
    cb31 = max(LANES, d // 4)
    ncb31 = (d // 2) // cb31
    pad31 = w31.shape[0] // 2
    piece31 = min(seq_len, 4 * GRID_W)
    v_w = tapsum_roll("cconv_cols", u0, 0, w31, 0, seq_len=seq_len, n_seq=n_ex, row_blk_off=0, width=GRID_W,
                      piece=piece31, cb=cb31, ncb=ncb31, pad=pad31, flip=False)
    v_h = tapsum_rows("cconv_rows", u0, ncb31, w31, ncb31, seq_len=seq_len, n_seq=n_ex, cb=cb31, ncb=ncb31, pad=pad31, flip=False)
    (un,) = rowwise("cconv_ln", fn_ln_silu, [row(v_w), row(v_h)], [], [cconv_b, cconv_ln_w, cconv_ln_b], [(n_lat, d, BF16)],
                    tm=tm, n_tiles=lat_tiles)
    mix = matmul("mix_out", [(yn, wo_y), (un, wo_u)], "nn", tm=tm)
    seg_lat, first_lat = _segmenter(tm, seq_len, n_lat)
    (x2,) = rowwise("mix_resid", make_fn_resid(1.0), [row(x1), row(mix)], [lat(tabs[5])], [], [(n_lat, d, F32)],
                    tm=tm, n_tiles=lat_tiles, seg_fn=seg_lat)
    x3, saved2 = _ffn_fwd(plan, "ffn2", x2, n_lat, tm, seg_lat, lat(tabs[6]), lat(tabs[7]), lat(tabs[8]), norm_ffn2, wg2, wu2, wd2)
    d_x3, d_final, loss_vec = final_loss(x3, loss_target.reshape(n_lat, d), final_norm.reshape(1, d), tm=tm)

    shard_cols = lambda t: t.reshape(t.shape[0], N_CHIPS, -1).transpose(1, 0, 2)

    def pieces(t):
        t = jnp.pad(t, ((0, 0), (0, -t.shape[1] % 32), (0, 0)))
        return t.reshape(2 * N_CHIPS, t.shape[1] // 2, t.shape[2]).astype(BF16)

    scatter = lambda *ts: Rider([pieces(t) for t in ts], "all8", scatter=True)
    halves = lambda names, landed: Rider([sum_slots(f"sum_{nm}", r) for nm, r in zip(names, landed)], "sibling")
    swapped = {}
    plan.on("ffn2_up_dx", "sc_ffn2_down", lambda p: scatter(p.got["ffn2_d_wd"]))
    plan.on("ffn2_up_dw", "sc_ffn2_gate", lambda p: scatter(p.got["ffn2_d_wg"]))
    d_x2, (d_s6, d_s7, d_g8), d_nffn2 = _ffn_bwd(
        plan, "ffn2", d_x3, saved2, x2, n_lat, tm, seg_lat, first_lat, lat(tabs[6]), lat(tabs[7]), lat(tabs[8]), norm_ffn2,
        wg2, wu2, wd2, n_lat, None)
    d_mix, d_g5 = rowwise_bwd("mix_resid_bwd", make_fn_resid(1.0), [row(x1), row(mix)], [lat(tabs[5])], [], [[row(d_x2)]],
                              [None, (n_lat, BF16, None)], tm=tm, n_tiles=lat_tiles, seg_fn=seg_lat, first_fn=first_lat)
    d_yn = matmul("mix_out_dy", [(d_mix, wo_y)], "nt", tm=tm)
    d_un = matmul("mix_out_du", [(d_mix, wo_u)], "nt", tm=tm)
    d_wout = jnp.concatenate([matmul("mix_out_dwy", [(yn, d_mix)], "tn", out_dtype=BF16, tm=tm), matmul("mix_out_dwu", [(un, d_mix)], "tn", out_dtype=BF16, tm=tm)])
    d_vw, d_vh, d_cb, d_lnw, d_lnb = rowwise_bwd(
        "cconv_ln_bwd", fn_ln_silu, [row(v_w), row(v_h)], [], [cconv_b, cconv_ln_w, cconv_ln_b], [[row(d_un)]],
        [(n_lat, F32, None)] * 2, tm=tm, n_tiles=lat_tiles)
    d_u0 = tapsum_roll("cconv_cols_dx", d_vw, 0, w31, 0, seq_len=seq_len, n_seq=n_ex, row_blk_off=0, width=GRID_W,
                       piece=piece31, cb=cb31, ncb=ncb31, pad=pad31, flip=True, place=((n_lat, d), 0, 0, None))
    d_u0 = tapsum_rows("cconv_rows_dx", d_vh, 0, w31, ncb31, seq_len=seq_len, n_seq=n_ex, cb=cb31, ncb=ncb31, pad=pad31,
                       flip=True, place=((n_lat, d), 0, ncb31, d_u0))
    d_w31 = jnp.concatenate([
        tapgrad_roll("cconv_cols_dw", d_vw, 0, 0, u0, 0, 0, n_tap=w31.shape[0], seq_len=seq_len, n_seq=n_ex, width=GRID_W,
                     piece=piece31, cb=cb31, ncb=ncb31, pad=pad31),
        tapgrad_rows("cconv_rows_dw", d_vh, 0, u0, ncb31, n_tap=w31.shape[0], seq_len=seq_len, n_seq=n_ex, cb=cb31,
                     ncb=ncb31, pad=pad31)], axis=1)
    d_ga, d_gb = rowwise_bwd("glu_bwd", fn_glu, [row(proj, d, 1), row(proj, d, 2)], [], [], [[row(d_u0)]],
                             [(n_lat, BF16, None)] * 2, tm=tm, n_tiles=lat_tiles)
    d_ysum, d_z, d_ssmnw = rowwise_bwd(
        "ssd_gate_bwd", fn_gate, [row(y_f), row(y_b), row(proj, d, 0)], [], [ssm_norm_w], [[row(d_yn)]],
        [(n_lat, F32, None), None, (n_lat, BF16, None)], tm=tm, n_tiles=lat_tiles)
    (dxbc_f, ddt_f, dalog_f, ddtb_f, ddsk), landed = ssd_bwd(
        "ssd_bwd_f", xbc, proj, dt_cb, hs_f, d_ysum, dtb_f, alog_f, dsk_f, rev=False,
        rider=scatter(plan.got["ffn2_d_wu"], d_wout.reshape(N_CHIPS, -1, d)), **ssd)
    (dxbc_b, ddt_b, dalog_b, ddtb_b, _), both = ssd_bwd(
        "ssd_bwd_b", xbc, proj, dt_cb, hs_b, d_ysum, dtb_b, alog_b, dsk_b, rev=True,
        rider=halves(["ffn2_down", "ffn2_gate"], plan.got["sc_ffn2_down"] + plan.got["sc_ffn2_gate"]), **ssd)
    swapped.update(zip(["ffn2_down", "ffn2_gate"], both))
    (d_craw, d_conv_b), both = rowwise_bwd(
        "xbc_silu_bwd", fn_silu_bias, [row(craw)], [], [ssm_conv_b], [[row(dxbc_f), row(dxbc_b)]],
        [(n_tok, F32, None)], tm=tm, n_tiles=n_tok // tm, rider=halves(["ffn2_up", "w_out"], landed))
    swapped.update(zip(["ffn2_up", "w_out"], both))
    d_pxbc = conv5("xbc_conv_dx", d_craw, 0, True)
    g5 = lambda name, seq, off: tapgrad_roll(name, d_craw, 0, off, proj, xbc_cb, off, n_tap=w5.shape[0], seq_len=seq,
                                             n_seq=n_ex, width=seq, piece=seq, cb=cbw, ncb=xw // cbw, pad=w5.shape[0] // 2)
    d_w5 = g5("xbc_conv_lat_dw", seq_len, 0) + g5("xbc_conv_ctx_dw", ctx_len, n_lat // ctx_len)
    lat_pairs = [(d_z, w_z), (d_ga, w_ga), (d_gb, w_gb), (d_pxbc, w_xbc), (ddt_f, w_dtp), (ddt_b, w_dtp)]
    d_h2 = jnp.concatenate([matmul("mix_proj_dx_lat", lat_pairs, "nt", rows=n_lat, tm=min(tm, 256)),
                            matmul("mix_proj_dx_ctx", lat_pairs[3:], "nt", rows=n_ctx_rows, row_off=n_lat, tm=min(tm, 256))])
    d_wz = matmul("mix_proj_dwz", [(d_z, h2)], "tn", out_dtype=BF16, rows=n_lat, tm=tm)
    d_wga = matmul("mix_proj_dwa", [(d_ga, h2)], "tn", out_dtype=BF16, rows=n_lat, tm=tm)
    d_wgb = matmul("mix_proj_dwb", [(d_gb, h2)], "tn", out_dtype=BF16, rows=n_lat, tm=tm)
    d_wxbc = matmul("mix_proj_dwx", [(d_pxbc, h2)], "tn", out_dtype=BF16, tm=tm)
    d_wdt = matmul("mix_proj_dwt", [(ddt_f, h2), (ddt_b, h2)], "tn", out_dtype=BF16, tm=tm)
    d_win_t = jnp.concatenate([d_wz, d_wxbc, d_wdt[:2 * n_head], d_wga, d_wgb]).reshape(N_CHIPS, -1, d)
    d_x1, d_s3, d_s4, d_nmix = rowwise_bwd(
        "mix_norm_bwd", fn_norm_mod, [row(x1)], [tabs[3], tabs[4]], [norm_mix], [[row(d_h2)]], [(n_tok, F32, None)],
        tm=tm, n_tiles=n_tok // tm, seg_fn=seg_all, first_fn=first_all, adds={0: (row(d_x2), lat_tiles)})
    mix_names = ["w_in", "ssm_conv_w", "cconv_w"]
    plan.on("ffn1_down_dx", "sc_conv", lambda p: scatter(shard_cols(d_w5), shard_cols(d_w31)))
    plan.on("ffn1_up_dx", "sc_win", lambda p: scatter(d_win_t))
    plan.on("ffn1_gate_dw", "sc_ffn1_down", lambda p: scatter(p.got["ffn1_d_wd"]))
    plan.on("ffn1_up_dw", "sc_ffn1_gate", lambda p: scatter(p.got["ffn1_d_wg"]))
    plan.on("ffn1_up_dw", "sw_mix", lambda p: halves(mix_names, p.got["sc_win"] + p.got["sc_conv"]))
    plan.on("ffn1_norm_bwd", "sc_ffn1_up", lambda p: scatter(p.got["ffn1_d_wu"]))
    plan.on("ffn1_norm_bwd", "sw_ffn1_down", lambda p: halves(["ffn1_down"], p.got["sc_ffn1_down"]))
    d_xt, (d_s0, d_s1, d_g2), d_nffn1 = _ffn_bwd(
        plan, "ffn1", d_x1, saved1, xt, n_tok, tm, seg_all, first_all, tabs[0], tabs[1], tabs[2], norm_ffn1, wg1, wu1, wd1,
        n_lat, lat_tiles)
    swapped.update(zip(mix_names + ["ffn1_down"], plan.got["sw_mix"] + plan.got["sw_ffn1_down"]))
    last_names = ["ffn1_gate", "ffn1_up"]
    last = halves(last_names, plan.got["sc_ffn1_gate"] + plan.got["sc_ffn1_up"])
    swapped.update(zip(last_names, exchange("swap_sibling", last.arrs, "sibling")))
    grad_x = d_xt.reshape(n_ex, seq_len, d)

    with_ctx0 = lambda t: jnp.concatenate([t, jnp.zeros((1, 1, d), F32)])
    d_tabs = [d_s0, d_s1, d_g2, d_s3, d_s4, with_ctx0(d_g5), with_ctx0(d_s6), with_ctx0(d_s7), with_ctx0(d_g8)]
    d_mod_rows = jnp.concatenate([t[:, 0, :] for t in d_tabs], axis=1)
    n_pad_rows = -(-(n_ex + 1) // 8) * 8
    d_mod_rows = jnp.concatenate([d_mod_rows, jnp.zeros((n_pad_rows - n_ex - 1, 9 * d), F32)])
    small = [("loss", loss_vec), ("norm_ffn1", d_nffn1), ("norm_mix", d_nmix), ("ssm_conv_b", d_conv_b),
             ("dt_bias_fwd", ddtb_f[:, :n_head]), ("dt_bias_bwd", ddtb_b[:, :n_head]), ("a_log_fwd", dalog_f[:, :n_head]),
             ("a_log_bwd", dalog_b[:, :n_head]), ("ssm_d", ddsk[:, :n_head]), ("ssm_norm_w", d_ssmnw), ("cconv_b", d_cb),
             ("cconv_ln_w", d_lnw), ("cconv_ln_b", d_lnb), ("norm_ffn2", d_nffn2), ("final_norm", d_final)]
    n_small = sum(v.size for _, v in small)
    n_pack = -(-n_small // (8 * LANES)) * (8 * LANES)
    pack = jnp.concatenate([v.reshape(-1) for _, v in small] + [jnp.zeros((n_pack - n_small,), F32)]).reshape(-1, LANES)
    pack_all, d_mod_all = exchange("gather_small", [pack, d_mod_rows], "all8")
    pack_sum = sum_slots("small_sum", pack_all)
    loss = loss_total(pack_sum.reshape(1, n_pack), d).reshape(())
    flat_sum = pack_sum.reshape(-1)
    small_grads, pos = {}, 0
    for nm, v in small:
        small_grads[nm] = flat_sum[pos:pos + v.size]
        pos += v.size
    d_mod_all = d_mod_all.reshape(8 * n_pad_rows, 9 * d)
    cond_rows = [jnp.concatenate([cond[j * n_ex:(j + 1) * n_ex], c_ctx[None, :],
                                  jnp.zeros((n_pad_rows - n_ex - 1, d), F32)]) for j in range(8)]
    cond_bwd = jnp.concatenate(cond_rows)
    d_mod_shard = lax.dynamic_slice(d_mod_all, (0, chip * mod_w), (8 * n_pad_rows, mod_w))
    g_wmod, g_bmod, q_part = mod_bwd(cond_bwd, d_mod_shard, d_mod_all, w_mod[0],
                                     tuple(j * n_pad_rows + n_ex for j in range(8)))
    (q_all,) = exchange("gather_cctx", [q_part], "all8")
    g_cctx = cctx_grad(q_all, c_ctx.reshape(1, d))
    small_grads["c_ctx"], small_grads["b_mod"] = g_cctx.reshape(-1), g_bmod.reshape(-1)

    transposed = {"ffn1_gate", "ffn1_up", "ffn2_gate", "ffn2_up", "w_in"}
    results = {}
    for nm, both in swapped.items():
        flip = (lambda t: jnp.swapaxes(t, 1, 2)) if nm in transposed else (lambda t: t)
        shape = flip(weights[nm]).shape
        two_d = lambda t: flip(t).reshape(shape[-2], shape[-1])
        g_full = both.reshape(1, -1, shape[-1])[:, :shape[-2]]
        results[nm] = [flip(r.reshape(shape)) for r in
                       adamw(f"adamw_{nm}", two_d(weights[nm]), g_full, two_d(mom1[nm]), two_d(mom2[nm]))]
    results["w_mod"] = [r.reshape(w_mod.shape) for r in adamw("adamw_w_mod", w_mod[0], g_wmod[None], m_w_mod[0], v_w_mod[0])]
    small_names = [nm for nm in order if nm not in results]
    n_sm = sum(weights[nm].size for nm in small_names)
    n_smp = -(-n_sm // (8 * LANES)) * (8 * LANES)
    packed = lambda src: jnp.concatenate([src[nm].reshape(-1) for nm in small_names] + [jnp.zeros((n_smp - n_sm,), F32)]).reshape(-1, LANES)
    sm_out = adamw("adamw_small", packed(weights), packed(small_grads)[None], packed(mom1), packed(mom2))
    pos = 0
    for nm in small_names:
        size = weights[nm].size
        results[nm] = [r.reshape(-1)[pos:pos + size].reshape(weights[nm].shape) for r in sm_out]
        pos += size
    return (loss, grad_x, *[results[nm][0] for nm in order], *[results[nm][1] for nm in order],
            *[results[nm][2] for nm in order], *[results[nm][3] for nm in order])
```

```python
import functools
import math

import jax
import jax.numpy as jnp
from jax import lax
from jax.experimental import pallas as pl
from jax.experimental.pallas import tpu as pltpu

F32 = jnp.float32
BF16 = jnp.bfloat16
HI = lax.Precision.HIGHEST
MESH = pl.DeviceIdType.MESH

EPS = 1e-6
GRID_W = 64
HEAD_DIM = 64
N_STATE = 128
CHUNK = 128
LANES = 128
N_CHIPS = 4
ADAM_LR, ADAM_B1, ADAM_B2, ADAM_EPS, ADAM_WD, ADAM_STEP = 0.001, 0.9, 0.999, 1e-08, 0.01, 10
VMEM_CAP = 56 * 1024 * 1024


def _params(vmem_bytes=None, n_axes=1):
    kw = dict(dimension_semantics=("arbitrary",) * n_axes)
    if vmem_bytes is not None:
        kw["vmem_limit_bytes"] = int(min(VMEM_CAP, max(32 * 1024 * 1024, vmem_bytes)))
    return pltpu.CompilerParams(**kw)


def _big(shape, dtype):
    return pltpu.HBM(tuple(shape), dtype)


def _in_hbm(args):
    return [pltpu.with_memory_space_constraint(a, pltpu.HBM) if a.size * a.dtype.itemsize >= (1 << 20) else a for a in args]


def _nbytes(shape, dtype):
    return math.prod(shape) * jnp.dtype(dtype).itemsize


def _row_tile(rows, width, cap_bytes=1 << 20, mult=8):
    best = None
    for t in range(mult, rows + 1, mult):
        if rows % t == 0 and t * width * 4 <= cap_bytes:
            best = t
    return best if best is not None else rows


_MODES = {"all8": (8, (1, 2, 3, 4, 5, 6, 7), 0), "chips": (4, (2, 4, 6), 1), "sibling": (2, (1,), 0)}


class Rider:
    def __init__(self, arrs, mode, scatter=False):
        self.arrs, self.scatter = list(arrs), scatter
        self.nslot, self.deltas, self.shift = _MODES[mode]
        self.n = len(self.arrs)
        self.out_shape = [jax.ShapeDtypeStruct((self.nslot,) + (a.shape[1:] if scatter else a.shape), a.dtype)
                          for a in self.arrs]
        any_spec = pl.BlockSpec(memory_space=pl.ANY)
        self.in_specs = [any_spec] * self.n
        self.out_specs = [any_spec] * self.n
        n_peer = len(self.deltas)
        self.scratch = [pltpu.SemaphoreType.DMA((self.n, n_peer)), pltpu.SemaphoreType.DMA((self.n, n_peer)),
                        pltpu.SemaphoreType.DMA((self.n,))]

    def _copies(self, ins, outs, sems, arrivals):
        send_sems, recv_sems, local_sems = sems
        x, y, c = lax.axis_index("x"), lax.axis_index("y"), lax.axis_index("c")
        me = 4 * x + 2 * y + c
        slot_of = lambda dev: (dev >> self.shift) & (self.nslot - 1)
        src = lambda a, slot: ins[a].at[slot] if self.scatter else ins[a]
        flip = lambda v, bit: 1 - v if bit else v

        def remote(a, k, d, from_slot, to_slot):
            return pltpu.make_async_remote_copy(
                src_ref=src(a, from_slot), dst_ref=outs[a].at[to_slot], send_sem=send_sems.at[a, k],
                recv_sem=recv_sems.at[a, k], device_id=(flip(x, (d >> 2) & 1), flip(y, (d >> 1) & 1), flip(c, d & 1)),
                device_id_type=MESH)

        mine = slot_of(me)
        local = [pltpu.make_async_copy(src(a, mine), outs[a].at[mine], local_sems.at[a]) for a in range(self.n)]
        sends = [remote(a, k, d, slot_of(me ^ d), mine) for k, d in enumerate(self.deltas) for a in range(self.n)]
        if not arrivals:
            return local, sends
        return local, sends, [remote(a, k, d, mine, slot_of(me ^ d)) for k, d in enumerate(self.deltas) for a in range(self.n)]

    def start(self, ins, outs, sems):
        local, sends = self._copies(ins, outs, sems, arrivals=False)
        for cp in local + sends:
            cp.start()

    def wait(self, ins, outs, sems):
        local, sends, recvs = self._copies(ins, outs, sems, arrivals=True)
        for cp in recvs:
            cp.wait_recv()
        for cp in sends:
            cp.wait_send()
        for cp in local:
            cp.wait()


class Riders:
    def __init__(self, riders):
        self.riders = list(riders)
        self.n = sum(r.n for r in self.riders)
        cat = lambda attr: [v for r in self.riders for v in getattr(r, attr)]
        self.arrs, self.out_shape, self.in_specs = cat("arrs"), cat("out_shape"), cat("in_specs")
        self.out_specs, self.scratch = cat("out_specs"), cat("scratch")

    def _each(self, method, ins, outs, sems):
        i = s = 0
        for r in self.riders:
            getattr(r, method)(ins[i:i + r.n], outs[i:i + r.n], sems[s:s + len(r.scratch)])
            i, s = i + r.n, s + len(r.scratch)

    def start(self, ins, outs, sems):
        self._each("start", ins, outs, sems)

    def wait(self, ins, outs, sems):
        self._each("wait", ins, outs, sems)


class _Hosted:
    def __init__(self, rider, n_in, n_out, n_scratch, grid):
        self.rider, self.n_in, self.n_out, self.n_scratch, self.grid = rider, n_in, n_out, n_scratch, grid
        self.n = rider.n if rider else 0

    def split(self, refs):
        a, b = self.n_in, self.n_in + self.n
        c, e = b + self.n_out, b + self.n_out + self.n
        self._r = (refs[a:b], refs[c:e], refs[e + self.n_scratch:])
        if self.rider:
            ids = [pl.program_id(ax) for ax in range(len(self.grid))]
            first = functools.reduce(jnp.logical_and, [i == 0 for i in ids]) if ids else True
            pl.when(first)(lambda: self.rider.start(*self._r))
        return refs[:a], refs[b:c], refs[e:e + self.n_scratch]

    def finish(self):
        if self.rider:
            ids = [pl.program_id(ax) for ax in range(len(self.grid))]
            last = functools.reduce(jnp.logical_and, [i == n - 1 for i, n in zip(ids, self.grid)]) if ids else True
            pl.when(last)(lambda: self.rider.wait(*self._r))

    def call_args(self, in_specs, out_shape, out_specs, scratch, args):
        r = self.rider
        if not r:
            return list(in_specs), tuple(out_shape), tuple(out_specs), list(scratch), list(args)
        return (list(in_specs) + r.in_specs, tuple(out_shape) + tuple(r.out_shape), tuple(out_specs) + tuple(r.out_specs),
                list(scratch) + r.scratch, list(args) + r.arrs)

    def results(self, res, unwrap=True):
        res = list(res) if isinstance(res, (tuple, list)) else [res]
        host = res[:self.n_out]
        host = host[0] if (self.n_out == 1 and unwrap) else tuple(host)
        return (host, res[self.n_out:]) if self.rider else host


def exchange_many(name, riders):
    both = Riders(riders)

    def body(*refs):
        ins, outs, sems = refs[:both.n], refs[both.n:2 * both.n], refs[2 * both.n:]
        both.start(ins, outs, sems)
        both.wait(ins, outs, sems)

    res = list(pl.pallas_call(
        body, name=name, out_shape=tuple(both.out_shape), in_specs=both.in_specs, out_specs=tuple(both.out_specs),
        scratch_shapes=both.scratch,
    )(*both.arrs))
    split = []
    for r in riders:
        split.append(res[:r.n])
        res = res[r.n:]
    return split


def exchange(name, arrs, mode, scatter=False):
    rider = Rider(arrs, mode, scatter)

    def body(*refs):
        ins, outs, sems = refs[:rider.n], refs[rider.n:2 * rider.n], refs[2 * rider.n:]
        rider.start(ins, outs, sems)
        rider.wait(ins, outs, sems)

    return pl.pallas_call(
        body, name=name, out_shape=tuple(rider.out_shape), in_specs=rider.in_specs, out_specs=tuple(rider.out_specs),
        scratch_shapes=rider.scratch,
    )(*arrs)


_DIMS = {"nn": (((1,), (0,)), ((), ())), "nt": (((1,), (1,)), ((), ())), "tn": (((0,), (0,)), ((), ()))}


def matmul(name, pairs, kind, *, a_ch=False, b_ch=False, out_ch=False, out_dtype=F32, rows=None, row_off=0, tm=512,
           rider=None, post=None, fold=False, place=None):
    a0, b0 = pairs[0]
    n_chunk = a0.shape[0] if a_ch else (b0.shape[0] if b_ch else 1)
    total_rows = a0.shape[-2]
    rows = total_rows - row_off if rows is None else rows
    tm = min(tm, rows)
    assert rows % tm == 0 and row_off % tm == 0, (name, rows, tm, row_off)
    n_rt, off = rows // tm, row_off // tm
    dims = _DIMS[kind]
    n_pair = len(pairs)

    if kind == "tn":
        grid, red_axis, n_red = (n_chunk, n_rt), 1, n_rt
        a_idx = (lambda k, i: (k, i + off, 0)) if a_ch else (lambda k, i: (i + off, 0))
        b_idx = (lambda k, i: (k, i + off, 0)) if b_ch else (lambda k, i: (i + off, 0))
        a_blk = lambda a: ((None, tm, a.shape[-1]) if a_ch else (tm, a.shape[-1]))
        b_blk = lambda b: ((None, tm, b.shape[-1]) if b_ch else (tm, b.shape[-1]))
        o2 = (a0.shape[-1], b0.shape[-1])
        out_shape = ((n_chunk,) + o2) if out_ch else o2
        out_spec = pl.BlockSpec((None,) + o2, lambda k, i: (k, 0, 0)) if out_ch else pl.BlockSpec(o2, lambda k, i: (0, 0))
        acc_shape = o2
    else:
        n_out = b0.shape[-1] if kind == "nn" else b0.shape[-2]
        b2 = b0.shape[-2:]
        if a_ch and b_ch and not out_ch and fold:
            grid, red_axis, n_red = (n_rt,), None, 1
            a_idx, b_idx = (lambda i: (0, i + off, 0)), (lambda i: (0, 0, 0))
            a_blk = lambda a: (n_chunk, tm, a.shape[-1])
            b_blk = lambda b: tuple(b.shape)
            out_shape, out_spec = (rows, n_out), pl.BlockSpec((tm, n_out), lambda i: (i, 0))
        elif a_ch and b_ch and not out_ch:
            grid, red_axis, n_red = (n_rt, n_chunk), 1, n_chunk
            a_idx, b_idx = (lambda i, k: (k, i + off, 0)), (lambda i, k: (k, 0, 0))
            a_blk = lambda a: (None, tm, a.shape[-1])
            b_blk = lambda b: (None,) + tuple(b.shape[-2:])
            out_shape, out_spec = (rows, n_out), pl.BlockSpec((tm, n_out), lambda i, k: (i, 0))
        elif out_ch and fold:
            assert b_ch and not a_ch and n_pair == 1
            grid, red_axis, n_red = (n_rt,), None, 1
            a_idx, b_idx = (lambda i: (i + off, 0)), (lambda i: (0, 0, 0))
            a_blk = lambda a: (tm, a.shape[-1])
            b_blk = lambda b: tuple(b.shape)
            out_shape, out_spec = (n_chunk, rows, n_out), pl.BlockSpec((n_chunk, tm, n_out), lambda i: (0, i, 0))
        elif out_ch:
            assert b_ch and not a_ch
            grid, red_axis, n_red = (n_chunk, n_rt), None, 1
            a_idx, b_idx = (lambda k, i: (i + off, 0)), (lambda k, i: (k, 0, 0))
            a_blk = lambda a: (tm, a.shape[-1])
            b_blk = lambda b: (None,) + tuple(b.shape[-2:])
            out_shape, out_spec = (n_chunk, rows, n_out), pl.BlockSpec((None, tm, n_out), lambda k, i: (k, i, 0))
        else:
            assert not (a_ch or b_ch)
            grid, red_axis, n_red = (n_rt,), None, 1
            a_idx, b_idx = (lambda i: (i + off, 0)), (lambda i: (0, 0))
            a_blk = lambda a: (tm, a.shape[-1])
            b_blk = lambda b: tuple(b.shape)
            out_shape, out_spec = (rows, n_out), pl.BlockSpec((tm, n_out), lambda i: (i, 0))
            if place is not None:
                out_shape, o_off = (place[0], n_out), place[1] // tm
                out_spec = pl.BlockSpec((tm, n_out), lambda i: (i + o_off, 0))
        acc_shape = (tm, n_out)

    into = [] if place is None or place[2] is None else [place[2]]
    post_ins, post_fn, out_dtypes = ([], None, [out_dtype]) if post is None else post
    hosted = _Hosted(rider, 2 * n_pair + len(post_ins) + len(into), len(out_dtypes), int(n_red > 1), grid)

    def body(*refs):
        ins, outs, scr = hosted.split(refs)

        def compute():
            acc = None
            for p in range(n_pair):
                for k in ([None] if not fold else range(n_chunk)):
                    pick = (lambda r: r[...]) if k is None else (lambda r: r[k])
                    d = lax.dot_general(pick(ins[2 * p]).astype(BF16), pick(ins[2 * p + 1]).astype(BF16), dims,
                                        preferred_element_type=F32)
                    acc = d if acc is None else acc + d
            return acc

        def emit(acc):
            vals = (acc,) if post_fn is None else post_fn(
                acc, *[r[...].astype(F32) for r in ins[2 * n_pair:2 * n_pair + len(post_ins)]])
            for o_ref, v in zip(outs, vals):
                o_ref[...] = v.astype(o_ref.dtype)

        if out_ch and fold:
            a_tile = ins[0][...].astype(BF16)
            for k in range(n_chunk):
                acc = lax.dot_general(a_tile, ins[1][k].astype(BF16), dims, preferred_element_type=F32)
                vals = (acc,) if post_fn is None else post_fn(acc, *[r[k].astype(F32) for r in ins[2:2 + len(post_ins)]])
                for o_ref, v in zip(outs, vals):
                    o_ref[k] = v.astype(o_ref.dtype)
        elif n_red == 1:
            emit(compute())
        else:
            acc_ref = scr[0]
            r = pl.program_id(red_axis)

            @pl.when(r == 0)
            def _():
                acc_ref[...] = jnp.zeros_like(acc_ref)

            acc_ref[...] += compute()

            @pl.when(r == n_red - 1)
            def _():
                emit(acc_ref[...])
        hosted.finish()

    in_specs, args, vmem = [], [], 0
    for a, b in pairs:
        in_specs += [pl.BlockSpec(a_blk(a), a_idx), pl.BlockSpec(b_blk(b), b_idx)]
        args += [a, b]
        vmem += 2 * (_nbytes([s for s in a_blk(a) if s], a.dtype) + _nbytes([s for s in b_blk(b) if s], b.dtype))
    in_specs += [out_spec] * len(post_ins)
    args += list(post_ins)
    aliases = {len(args): 0} if into else {}
    in_specs += [pl.BlockSpec(memory_space=pl.ANY)] * len(into)
    args += into
    tiles_per_step = n_chunk if (out_ch and fold) else 1
    vmem += (3 + 2 * n_pair + tiles_per_step * (len(post_ins) + len(out_dtypes))) * _nbytes(acc_shape, F32)
    scratch = [pltpu.VMEM(acc_shape, F32)] if n_red > 1 else []
    in_specs, out_shapes, out_specs, scratch, args = hosted.call_args(
        in_specs, [_big(out_shape, dt) for dt in out_dtypes], [out_spec] * len(out_dtypes), scratch, args)
    return hosted.results(pl.pallas_call(
        body, name=name, out_shape=out_shapes, grid=grid, in_specs=in_specs, out_specs=out_specs,
        input_output_aliases=aliases, scratch_shapes=scratch, compiler_params=_params(vmem + (8 << 20), len(grid)),
    )(*_in_hbm(args)))


def row(arr, width=None, cb=0, roff=0):
    return (arr, arr.shape[-1] if width is None else width, cb, roff)


def two_rows(first, second, limit):
    return (first, first.shape[-1], 0, 0, (second, limit))


def _row_inputs(rows, tm):
    specs, arrs, slots = [], [], []
    for d in rows:
        second, limit = d[4] if len(d) > 4 else (None, None)
        slots.append((len(arrs), limit))
        specs.append(_row_spec(d[:4], tm, limit))
        arrs.append(d[0])
        if second is not None:
            specs.append(pl.BlockSpec((tm, d[1]), lambda i, limit=limit: (jnp.maximum(i - limit, 0), 0)))
            arrs.append(second)

    def read(refs, i):
        vals = []
        for at, limit in slots:
            v = refs[at][...].astype(F32)
            vals.append(v if limit is None else jnp.where(i < limit, v, refs[at + 1][...].astype(F32)))
        return vals

    return specs, arrs, read


def _row_spec(desc, tm, limit=None):
    _, width, cb, roff = desc[:4]
    if limit is None:
        return pl.BlockSpec((tm, width), lambda i: (i + roff, cb))
    return pl.BlockSpec((tm, width), lambda i: (jnp.minimum(i, limit - 1) + roff, cb))


def _segmenter(tm, seq_len, n_lat):
    seg = lambda i: jnp.where(i * tm < n_lat, (i * tm) // seq_len, n_lat // seq_len)
    first = lambda i: jnp.where(i * tm < n_lat, (i * tm) % seq_len == 0, i * tm == n_lat)
    return seg, first


def rowwise(name, fn, rows, segs, params, outs, *, tm, n_tiles, seg_fn=None, rider=None):
    row_specs, row_arrs, read_rows = _row_inputs(rows, tm)
    n_r, n_s, n_p = len(row_arrs), len(segs), len(params)
    hosted = _Hosted(rider, n_r + n_s + n_p, len(outs), 0, (n_tiles,))

    def body(*refs):
        ins, out_refs, _ = hosted.split(refs)
        vals = read_rows(ins[:n_r], pl.program_id(0)) + [r[...] for r in ins[n_r:]]
        res = fn(*vals)
        for o_ref, v in zip(out_refs, res):
            o_ref[...] = v.astype(o_ref.dtype)
        hosted.finish()

    in_specs = list(row_specs)
    in_specs += [pl.BlockSpec((None, 1, s.shape[-1]), lambda i: (seg_fn(i), 0, 0)) for s in segs]
    in_specs += [pl.BlockSpec(p.shape, lambda i: (0, 0)) for p in params]
    vmem = sum(2 * tm * d[1] * 4 for d in rows) + sum(3 * tm * w * 4 for _, w, _ in outs) + sum(2 * p.size * 4 for p in params)
    in_specs, out_shapes, out_specs, scratch, args = hosted.call_args(
        in_specs, [_big((r, w), dt) for r, w, dt in outs],
        [pl.BlockSpec((tm, w), lambda i: (i, 0)) for _, w, _ in outs], [], row_arrs + list(segs) + list(params))
    return hosted.results(pl.pallas_call(
        body, name=name, grid=(n_tiles,), in_specs=in_specs, out_shape=out_shapes, out_specs=out_specs,
        scratch_shapes=scratch, compiler_params=_params(2 * vmem + (8 << 20)),
    )(*_in_hbm(args)), unwrap=False)


def rowwise_bwd(name, fn, rows, segs, params, cts, row_grads, *, tm, n_tiles, seg_fn=None, first_fn=None, adds=None,
                rider=None):
    adds = adds or {}
    need = [k for k, v in enumerate(row_grads) if v is not None]
    row_specs, row_arrs, read_rows = _row_inputs(rows, tm)
    n_r, n_s, n_p = len(row_arrs), len(segs), len(params)
    n_ct = sum(len(lst) for lst in cts)
    add_keys = sorted(adds)
    hosted = _Hosted(rider, n_r + n_s + n_p + n_ct + len(add_keys), len(need) + n_s + n_p, 0, (n_tiles,))

    def body(*refs):
        host_in, host_out, _ = hosted.split(refs)
        it = iter(list(host_in) + list(host_out))
        row_refs = [next(it) for _ in range(n_r)]
        seg_refs = [next(it) for _ in range(n_s)]
        par_refs = [next(it) for _ in range(n_p)]
        ct_refs = [[next(it) for _ in lst] for lst in cts]
        add_refs = {k: next(it) for k in add_keys}
        rg_refs = {k: next(it) for k in need}
        sg_refs = [next(it) for _ in range(n_s)]
        pg_refs = [next(it) for _ in range(n_p)]
        i = pl.program_id(0)
        rv = read_rows(row_refs, i)
        sv = [r[...] for r in seg_refs]
        pv = [r[...] for r in par_refs]

        def f(*args):
            rr = list(rv)
            for j, k in enumerate(need):
                rr[k] = args[j]
            return fn(*rr, *args[len(need):])

        _, vjp = jax.vjp(f, *[rv[k] for k in need], *sv, *pv)
        ctv = []
        for lst in ct_refs:
            acc = lst[0][...].astype(F32)
            for r in lst[1:]:
                acc = acc + r[...].astype(F32)
            ctv.append(acc)
        g = vjp(tuple(ctv))
        for j, k in enumerate(need):
            gv = g[j]
            if k in adds:
                lim = adds[k][1]
                av = add_refs[k][...].astype(F32)
                gv = gv + (av if lim is None else jnp.where(i < lim, av, 0.0))
            lim = row_grads[k][2]
            if lim is None:
                rg_refs[k][...] = gv.astype(rg_refs[k].dtype)
            else:
                @pl.when(i < lim)
                def _(gv=gv, k=k):
                    rg_refs[k][...] = gv.astype(rg_refs[k].dtype)
        if n_s:
            opens = first_fn(i)
            for ref, gv in zip(sg_refs, g[len(need):len(need) + n_s]):
                @pl.when(opens)
                def _(ref=ref, gv=gv):
                    ref[...] = gv

                @pl.when(jnp.logical_not(opens))
                def _(ref=ref, gv=gv):
                    ref[...] += gv
        for ref, gv in zip(pg_refs, g[len(need) + n_s:]):
            @pl.when(i == 0)
            def _(ref=ref, gv=gv):
                ref[...] = gv

            @pl.when(i > 0)
            def _(ref=ref, gv=gv):
                ref[...] += gv
        hosted.finish()

    seg_spec = lambda s: pl.BlockSpec((None, 1, s.shape[-1]), lambda i: (seg_fn(i), 0, 0))
    par_spec = lambda p: pl.BlockSpec(p.shape, lambda i: (0, 0))
    in_specs = list(row_specs) + [seg_spec(s) for s in segs] + [par_spec(p) for p in params]
    args = row_arrs + list(segs) + list(params)
    for lst in cts:
        in_specs += [_row_spec(d, tm) for d in lst]
        args += [d[0] for d in lst]
    for k in add_keys:
        in_specs.append(_row_spec(adds[k][0], tm, adds[k][1]))
        args.append(adds[k][0][0])
    out_shape, out_specs = [], []
    for k in need:
        n_rows, dt, lim = row_grads[k]
        out_shape.append(_big((n_rows, rows[k][1]), dt))
        out_specs.append(_row_spec((None, rows[k][1], 0, 0), tm, lim))
    for s in segs:
        out_shape.append(jax.ShapeDtypeStruct(s.shape, F32))
        out_specs.append(seg_spec(s))
    for p in params:
        out_shape.append(jax.ShapeDtypeStruct(p.shape, F32))
        out_specs.append(par_spec(p))
    vmem = sum(tm * d[1] * 4 for d in rows) * 6 + n_ct * tm * max(d[1] for d in rows) * 8
    in_specs, out_shape, out_specs, scratch, args = hosted.call_args(in_specs, out_shape, out_specs, [], args)
    return hosted.results(pl.pallas_call(
        body, name=name, grid=(n_tiles,), in_specs=in_specs, out_shape=out_shape, out_specs=out_specs,
        scratch_shapes=scratch, compiler_params=_params(vmem + (8 << 20)),
    )(*_in_hbm(args)), unwrap=False)


def _silu(v):
    return v * jax.nn.sigmoid(v)


def _rms(v, w):
    return v * lax.rsqrt(jnp.mean(v * v, axis=-1, keepdims=True) + EPS) * w


def fn_norm_mod(x, shift, scale, w):
    return (_rms(x, w) * (1.0 + scale) + shift,)


def fn_act(g, u):
    return (_silu(g) * u,)


def make_fn_resid(coef):
    def fn(x, f, gate):
        return (x + coef * gate * f,)
    return fn


def fn_silu_bias(v, b):
    return (_silu(v + b),)


def make_fn_gate_groupnorm(width):
    half = width // 2

    def fn(yf, yb, z, w):
        y = (yf + yb) * _silu(z)
        lane = lax.broadcasted_iota(jnp.int32, y.shape, 1)
        lo = lane < half
        sq = y * y
        s_lo = jnp.sum(jnp.where(lo, sq, 0.0), axis=-1, keepdims=True)
        s_hi = jnp.sum(jnp.where(lo, 0.0, sq), axis=-1, keepdims=True)
        r = jnp.where(lo, lax.rsqrt(s_lo / half + EPS), lax.rsqrt(s_hi / half + EPS))
        return (y * r * w,)
    return fn


def fn_glu(a, b):
    return (a * jax.nn.sigmoid(b),)


def fn_ln_silu(vw, vh, cb, lw, lb):
    v = jnp.concatenate([vw, vh], axis=-1) + cb
    mu = jnp.mean(v, axis=-1, keepdims=True)
    var = jnp.mean(jnp.square(v - mu), axis=-1, keepdims=True)
    return (_silu((v - mu) * lax.rsqrt(var + EPS) * lw + lb),)


def _col_tile(width):
    return width // 3 if width % (3 * LANES) == 0 else width


def mod_fwd(a_rows, w_shard, b_shard):
    n, d = a_rows.shape
    ws = w_shard.shape[1]
    tn = _col_tile(ws)

    def body(a_ref, w_ref, b_ref, o_ref):
        a = _silu(a_ref[...]).astype(BF16)
        o_ref[...] = jnp.dot(a, w_ref[...].astype(BF16), preferred_element_type=F32) + b_ref[...]

    return pl.pallas_call(
        body, name="mod_fwd", grid=(ws // tn,), out_shape=jax.ShapeDtypeStruct((n, ws), F32),
        in_specs=[pl.BlockSpec((n, d), lambda j: (0, 0)), pl.BlockSpec((d, tn), lambda j: (0, j)),
                  pl.BlockSpec((1, tn), lambda j: (0, j))],
        out_specs=pl.BlockSpec((n, tn), lambda j: (0, j)), compiler_params=_params(),
    )(a_rows, w_shard, b_shard)


def mod_bwd(a_rows, d_shard, d_full, w_shard, ctx_rows):
    n, d = a_rows.shape
    ws = w_shard.shape[1]
    tn = _col_tile(ws)
    n_ct = ws // tn

    def body(a_ref, ds_ref, df_ref, w_ref, gw_ref, gb_ref, q_ref):
        j = pl.program_id(0)
        a = _silu(a_ref[...])
        ds = ds_ref[...]
        gw_ref[...] = lax.dot_general(a, ds, _DIMS["tn"], precision=HI, preferred_element_type=F32)
        dctx = ds[ctx_rows[0]:ctx_rows[0] + 1, :]
        for r in ctx_rows[1:]:
            dctx = dctx + ds[r:r + 1, :]
        q = lax.dot_general(jnp.broadcast_to(dctx, (8, tn)), w_ref[...], _DIMS["nt"], precision=HI,
                            preferred_element_type=F32)

        @pl.when(j == 0)
        def _():
            q_ref[...] = q
            df = df_ref[...]
            acc = df[0:1, :]
            for r in range(1, n):
                acc = acc + df[r:r + 1, :]
            gb_ref[...] = acc

        @pl.when(j > 0)
        def _():
            q_ref[...] += q

    return pl.pallas_call(
        body, name="mod_bwd", grid=(n_ct,),
        out_shape=(jax.ShapeDtypeStruct((d, ws), F32), jax.ShapeDtypeStruct((1, d_full.shape[1]), F32),
                   jax.ShapeDtypeStruct((8, d), F32)),
        in_specs=[pl.BlockSpec((n, d), lambda j: (0, 0)), pl.BlockSpec((n, tn), lambda j: (0, j)),
                  pl.BlockSpec(d_full.shape, lambda j: (0, 0)), pl.BlockSpec((d, tn), lambda j: (0, j))],
        out_specs=(pl.BlockSpec((d, tn), lambda j: (0, j)), pl.BlockSpec((1, d_full.shape[1]), lambda j: (0, 0)),
                   pl.BlockSpec((8, d), lambda j: (0, 0))),
        compiler_params=_params(40 << 20),
    )(a_rows, d_shard, d_full, w_shard)


def _shifted(xs, d, tok, width):
    if d == 0:
        return xs
    n = xs.shape[0]
    sh = pltpu.roll(xs, (-d) % n, axis=0)
    return jnp.where((tok + d >= 0) & (tok + d < width), sh, 0.0)


def _placed(out_shape, place):
    if place is None:
        return out_shape, 0, 0, None
    return place


def tapsum_roll(name, x, xcb, w, wcb, *, seq_len, n_seq, row_blk_off, width, piece, cb, ncb, pad, flip, place=None):
    n_tap = w.shape[0]
    n_piece = seq_len // piece
    out_shape, o_rb, o_cb, into = _placed((n_seq * seq_len, ncb * cb), place)

    def body(x_ref, w_ref, *rest):
        o_ref = rest[-1]
        wv = w_ref[...]
        tok = lax.broadcasted_iota(jnp.int32, (piece, 1), 0) % width

        def do_piece(p, carry):
            start = pl.multiple_of(p * piece, piece)
            xs = x_ref[pl.ds(start, piece), :]
            acc = jnp.zeros_like(xs)
            for k in range(n_tap):
                d = pad - k if flip else k - pad
                acc = acc + wv[k:k + 1, :] * _shifted(xs, d, tok, width)
            o_ref[pl.ds(start, piece), :] = acc
            return carry

        lax.fori_loop(0, n_piece, do_piece, 0)

    extra = [] if into is None else [into]
    return pl.pallas_call(
        body, name=name, grid=(ncb, n_seq), out_shape=_big(out_shape, F32),
        in_specs=[pl.BlockSpec((seq_len, cb), lambda j, s: (row_blk_off + s, xcb + j)),
                  pl.BlockSpec((n_tap, cb), lambda j, s: (0, wcb + j))] + [pl.BlockSpec(memory_space=pl.ANY)] * len(extra),
        out_specs=pl.BlockSpec((seq_len, cb), lambda j, s: (o_rb + s, o_cb + j)),
        input_output_aliases={2: 0} if extra else {},
        compiler_params=_params(8 * seq_len * cb * 4 + (8 << 20), 2),
    )(*_in_hbm([x, w] + extra))


def tapgrad_roll(name, dy, dycb, dy_blk_off, x, xcb, x_blk_off, *, n_tap, seq_len, n_seq, width, piece, cb, ncb, pad):
    n_piece = seq_len // piece

    def body(dy_ref, x_ref, o_ref):
        @pl.when(pl.program_id(1) == 0)
        def _():
            o_ref[...] = jnp.zeros_like(o_ref)

        tok = lax.broadcasted_iota(jnp.int32, (piece, 1), 0) % width

        def do_piece(p, carry):
            start = pl.multiple_of(p * piece, piece)
            xs = x_ref[pl.ds(start, piece), :]
            dv = dy_ref[pl.ds(start, piece), :]
            for k in range(n_tap):
                o_ref[k:k + 1, :] += jnp.sum(dv * _shifted(xs, k - pad, tok, width), axis=0, keepdims=True)
            return carry

        lax.fori_loop(0, n_piece, do_piece, 0)

    return pl.pallas_call(
        body, name=name, grid=(ncb, n_seq), out_shape=jax.ShapeDtypeStruct((n_tap, ncb * cb), F32),
        in_specs=[pl.BlockSpec((seq_len, cb), lambda j, s: (dy_blk_off + s, dycb + j)),
                  pl.BlockSpec((seq_len, cb), lambda j, s: (x_blk_off + s, xcb + j))],
        out_specs=pl.BlockSpec((n_tap, cb), lambda j, s: (0, j)),
        compiler_params=_params(8 * seq_len * cb * 4 + (8 << 20), 2),
    )(*_in_hbm([dy, x]))


def tapsum_rows(name, x, xcb, w, wcb, *, seq_len, n_seq, cb, ncb, pad, flip, place=None):
    n_tap = w.shape[0]
    n_row = seq_len // GRID_W
    halo = pad * GRID_W
    out_shape, o_rb, o_cb, into = _placed((n_seq * seq_len, ncb * cb), place)

    def body(x_ref, w_ref, *rest):
        o_ref, xp = rest[-2:]
        xp[pl.ds(0, halo), :] = jnp.zeros((halo, cb), F32)
        xp[pl.ds(halo + seq_len, halo), :] = jnp.zeros((halo, cb), F32)
        xp[pl.ds(halo, seq_len), :] = x_ref[...]
        wv = w_ref[...]

        def do_row(r, carry):
            acc = jnp.zeros((GRID_W, cb), F32)
            for k in range(n_tap):
                d = pad - k if flip else k - pad
                acc = acc + wv[k:k + 1, :] * xp[pl.ds(pl.multiple_of((r + pad + d) * GRID_W, GRID_W), GRID_W), :]
            o_ref[pl.ds(pl.multiple_of(r * GRID_W, GRID_W), GRID_W), :] = acc
            return carry

        lax.fori_loop(0, n_row, do_row, 0)

    extra = [] if into is None else [into]
    return pl.pallas_call(
        body, name=name, grid=(ncb, n_seq), out_shape=_big(out_shape, F32),
        in_specs=[pl.BlockSpec((seq_len, cb), lambda j, s: (s, xcb + j)),
                  pl.BlockSpec((n_tap, cb), lambda j, s: (0, wcb + j))] + [pl.BlockSpec(memory_space=pl.ANY)] * len(extra),
        out_specs=pl.BlockSpec((seq_len, cb), lambda j, s: (o_rb + s, o_cb + j)),
        input_output_aliases={2: 0} if extra else {},
        scratch_shapes=[pltpu.VMEM((seq_len + 2 * halo, cb), F32)],
        compiler_params=_params(10 * seq_len * cb * 4 + (8 << 20), 2),
    )(*_in_hbm([x, w] + extra))


def tapgrad_rows(name, dy, dycb, x, xcb, *, n_tap, seq_len, n_seq, cb, ncb, pad):
    n_row = seq_len // GRID_W
    halo = pad * GRID_W

    def body(dy_ref, x_ref, o_ref, xp):
        @pl.when(pl.program_id(1) == 0)
        def _():
            o_ref[...] = jnp.zeros_like(o_ref)

        xp[pl.ds(0, halo), :] = jnp.zeros((halo, cb), F32)
        xp[pl.ds(halo + seq_len, halo), :] = jnp.zeros((halo, cb), F32)
        xp[pl.ds(halo, seq_len), :] = x_ref[...]

        def do_row(r, carry):
            dv = dy_ref[pl.ds(pl.multiple_of(r * GRID_W, GRID_W), GRID_W), :]
            for k in range(n_tap):
                xs = xp[pl.ds(pl.multiple_of((r + k) * GRID_W, GRID_W), GRID_W), :]
                o_ref[k:k + 1, :] += jnp.sum(dv * xs, axis=0, keepdims=True)
            return carry

        lax.fori_loop(0, n_row, do_row, 0)

    return pl.pallas_call(
        body, name=name, grid=(ncb, n_seq), out_shape=jax.ShapeDtypeStruct((n_tap, ncb * cb), F32),
        in_specs=[pl.BlockSpec((seq_len, cb), lambda j, s: (s, dycb + j)),
                  pl.BlockSpec((seq_len, cb), lambda j, s: (s, xcb + j))],
        out_specs=pl.BlockSpec((n_tap, cb), lambda j, s: (0, j)),
        scratch_shapes=[pltpu.VMEM((seq_len + 2 * halo, cb), F32)],
        compiler_params=_params(10 * seq_len * cb * 4 + (8 << 20), 2),
    )(*_in_hbm([dy, x]))


def _ssd_blocks(b, s, *, rev, n_ctx, n_lat, lat_blocks):
    if rev:
        return jnp.where(s < n_ctx, lat_blocks + b * n_ctx + (n_ctx - 1 - s), b * n_lat + (n_lat - 1 - (s - n_ctx)))
    return jnp.where(s < n_ctx, lat_blocks + b * n_ctx + s, b * n_lat + (s - n_ctx))


def _ssd_common(xbc, raw, dtb, alog, dsk, *, rev, ds, n_head):
    if rev:
        raw = pltpu.roll(raw, LANES - n_head, axis=1)
    pre = raw + dtb
    dt = jnp.maximum(pre, 0.0) + jnp.log1p(jnp.exp(-jnp.abs(pre)))
    sig = jax.nn.sigmoid(pre)
    a = -jnp.exp(alog)
    da = dt * a
    ri = lax.broadcasted_iota(jnp.int32, (CHUNK, CHUNK), 0)
    ci = lax.broadcasted_iota(jnp.int32, (CHUNK, CHUNK), 1)
    mask = (ci >= ri) if rev else (ci <= ri)
    tri = mask.astype(F32)
    tri_t = ((ci <= ri) if rev else (ci >= ri)).astype(F32)
    cs = jnp.dot(tri, da, precision=HI, preferred_element_type=F32)
    tot = jnp.sum(da, axis=0, keepdims=True)
    def wide(v):
        first = lax.broadcasted_iota(jnp.int32, (v.shape[0], LANES), 1) < HEAD_DIM
        return jnp.concatenate(
            [jnp.where(first, jnp.broadcast_to(v[:, 2 * p:2 * p + 1], first.shape),
                       jnp.broadcast_to(v[:, 2 * p + 1:2 * p + 2], first.shape)) for p in range(n_head // 2)], axis=1)

    cs_w, tot_w = wide(cs), wide(tot)
    xh = xbc[:, :ds]
    dt_w = wide(dt)
    return dict(
        dt=dt, sig=sig, a=a, cs=cs, cs_t=cs.T, tot=tot, mask=mask, tri_t=tri_t,
        e_w=jnp.exp(cs_w), wt_w=jnp.exp(tot_w - cs_w), dec_w=jnp.exp(tot_w), dt_w=dt_w, dsk_w=wide(dsk),
        xh=xh, xs_w=xh * dt_w, bm=xbc[:, ds:ds + 2 * N_STATE], cm=xbc[:, ds + 2 * N_STATE:ds + 4 * N_STATE])


def _decay(q, col):
    seg = q["cs"][:, col:col + 1] - q["cs_t"][col:col + 1, :]
    return jnp.exp(jnp.where(q["mask"], seg, -jnp.inf))


def _split_heads(v):
    lane = lax.broadcasted_iota(jnp.int32, v.shape, 1)
    return jnp.concatenate([jnp.where(lane < HEAD_DIM, v, 0.0), jnp.where(lane >= HEAD_DIM, v, 0.0)], axis=0)


def ssd_fwd(name, xbc, proj, dt_cb, dtb, alog, dsk, *, rev, n_ex, seq_len, ctx_len, ds, rider=None):
    n_head, half = ds // HEAD_DIM, ds // 2
    n_ctx, n_lat = ctx_len // CHUNK, seq_len // CHUNK
    n_step = n_ctx + n_lat
    blk = functools.partial(_ssd_blocks, rev=rev, n_ctx=n_ctx, n_lat=n_lat, lat_blocks=n_ex * n_lat)
    xw = xbc.shape[1]

    def y_blk(b, s):
        sl = jnp.maximum(s, n_ctx) - n_ctx
        return b * n_lat + ((n_lat - 1 - sl) if rev else sl)

    hosted = _Hosted(rider, 5, 2, 1, (n_ex, n_step))

    def body(*refs):
        (xbc_ref, dt_ref, dtb_ref, alog_ref, dsk_ref), (y_ref, hs_ref), (h_scr,) = hosted.split(refs)

        @pl.when(pl.program_id(1) == 0)
        def _():
            h_scr[...] = jnp.zeros_like(h_scr)

        q = _ssd_common(xbc_ref[...], dt_ref[...], dtb_ref[...], alog_ref[...], dsk_ref[...], rev=rev, ds=ds, n_head=n_head)
        h = h_scr[...]
        hs_ref[...] = h
        for g in range(2):
            lo = g * half
            bg = q["bm"][:, g * N_STATE:(g + 1) * N_STATE].astype(BF16)
            cg = q["cm"][:, g * N_STATE:(g + 1) * N_STATE].astype(BF16)
            scores = lax.dot_general(cg, bg, _DIMS["nt"], preferred_element_type=F32)
            hg = h[:, lo:lo + half]
            off = jnp.dot(cg, hg.astype(BF16), preferred_element_type=F32)
            for j in range(half // LANES):
                c0 = (lo + j * LANES) // HEAD_DIM
                ln = slice(lo + j * LANES, lo + (j + 1) * LANES)
                p_cat = jnp.concatenate([scores * _decay(q, c0), scores * _decay(q, c0 + 1)], axis=1).astype(BF16)
                diag = jnp.dot(p_cat, _split_heads(q["xs_w"][:, ln]).astype(BF16), preferred_element_type=F32)
                y_ref[:, ln] = (diag + q["e_w"][:, ln] * off[:, j * LANES:(j + 1) * LANES]
                                + q["dsk_w"][:, ln] * q["xh"][:, ln])
            v = (q["wt_w"][:, lo:lo + half] * q["xs_w"][:, lo:lo + half]).astype(BF16)
            h_scr[:, lo:lo + half] = (q["dec_w"][:, lo:lo + half] * hg
                                      + lax.dot_general(bg, v, _DIMS["tn"], preferred_element_type=F32))
        hosted.finish()

    vec = pl.BlockSpec((1, LANES), lambda b, s: (0, 0))
    in_specs, out_shape, out_specs, scratch, args = hosted.call_args(
        [pl.BlockSpec((CHUNK, xw), lambda b, s: (blk(b, s), 0)),
         pl.BlockSpec((CHUNK, LANES), lambda b, s: (blk(b, s), dt_cb)), vec, vec, vec],
        (_big((n_ex * seq_len, ds), F32), _big((n_ex, n_step, N_STATE, ds), F32)),
        (pl.BlockSpec((CHUNK, ds), lambda b, s: (y_blk(b, s), 0)),
         pl.BlockSpec((None, None, N_STATE, ds), lambda b, s: (b, s, 0, 0))),
        [pltpu.VMEM((N_STATE, ds), F32)], [xbc, proj, dtb, alog, dsk])
    return hosted.results(pl.pallas_call(
        body, name=name, grid=(n_ex, n_step), out_shape=out_shape, in_specs=in_specs, out_specs=out_specs,
        scratch_shapes=scratch, compiler_params=_params(40 << 20, 2),
    )(*_in_hbm(args)))


def ssd_bwd(name, xbc, proj, dt_cb, hs, dy, dtb, alog, dsk, *, rev, n_ex, seq_len, ctx_len, ds, rider=None):
    n_head, half = ds // HEAD_DIM, ds // 2
    n_ctx, n_lat = ctx_len // CHUNK, seq_len // CHUNK
    n_step = n_ctx + n_lat
    n_tok = n_ex * (seq_len + ctx_len)
    blk0 = functools.partial(_ssd_blocks, rev=rev, n_ctx=n_ctx, n_lat=n_lat, lat_blocks=n_ex * n_lat)
    step = lambda sp: n_step - 1 - sp
    blk = lambda b, sp: blk0(b, step(sp))
    xw = xbc.shape[1]

    def dy_blk(b, sp):
        sl = jnp.maximum(step(sp), n_ctx) - n_ctx
        return b * n_lat + ((n_lat - 1 - sl) if rev else sl)

    hosted = _Hosted(rider, 7, 5, 1, (n_ex, n_step))

    def body(*refs):
        ((xbc_ref, dt_ref, hs_ref, dy_ref, dtb_ref, alog_ref, dsk_ref),
         (dxbc_ref, ddt_ref, dalog_ref, ddtb_ref, ddsk_ref), (dh_scr,)) = hosted.split(refs)
        b, sp = pl.program_id(0), pl.program_id(1)

        @pl.when(sp == 0)
        def _():
            dh_scr[...] = jnp.zeros_like(dh_scr)

        @pl.when((sp == 0) & (b == 0))
        def _():
            dalog_ref[...] = jnp.zeros_like(dalog_ref)
            ddtb_ref[...] = jnp.zeros_like(ddtb_ref)
            ddsk_ref[...] = jnp.zeros_like(ddsk_ref)

        q = _ssd_common(xbc_ref[...], dt_ref[...], dtb_ref[...], alog_ref[...], dsk_ref[...], rev=rev, ds=ds, n_head=n_head)
        h = hs_ref[...]
        d_y = jnp.where(step(sp) >= n_ctx, dy_ref[...], 0.0)
        dh_next = dh_scr[...]
        lane_row = lax.broadcasted_iota(jnp.int32, (1, LANES), 1)
        d_cs = jnp.zeros((CHUNK, LANES), F32)
        dxs_parts, de_parts, dwt_parts, ddec_parts = [], [], [], []
        for g in range(2):
            lo = g * half
            gs = slice(lo, lo + half)
            bg = q["bm"][:, g * N_STATE:(g + 1) * N_STATE].astype(BF16)
            cg = q["cm"][:, g * N_STATE:(g + 1) * N_STATE].astype(BF16)
            scores = lax.dot_general(cg, bg, _DIMS["nt"], preferred_element_type=F32)
            hg, dyg, dhn = h[:, gs], d_y[:, gs], dh_next[:, gs]
            off = jnp.dot(cg, hg.astype(BF16), preferred_element_type=F32)
            d_off = (q["e_w"][:, gs] * dyg).astype(BF16)
            de_parts.append(dyg * off)
            d_c = lax.dot_general(d_off, hg.astype(BF16), _DIMS["nt"], preferred_element_type=F32)
            dh_scr[:, gs] = (lax.dot_general(cg, d_off, _DIMS["tn"], preferred_element_type=F32)
                             + q["dec_w"][:, gs] * dhn)
            b_dh = jnp.dot(bg, dhn.astype(BF16), preferred_element_type=F32)
            v = q["wt_w"][:, gs] * q["xs_w"][:, gs]
            d_b = lax.dot_general(v.astype(BF16), dhn.astype(BF16), _DIMS["nt"], preferred_element_type=F32)
            dwt_parts.append(q["xs_w"][:, gs] * b_dh)
            ddec_parts.append(jnp.sum(hg * dhn, axis=0, keepdims=True))
            d_scores = jnp.zeros((CHUNK, CHUNK), F32)
            for j in range(half // LANES):
                c0 = (lo + j * LANES) // HEAD_DIM
                ln = slice(lo + j * LANES, lo + (j + 1) * LANES)
                l0, l1 = _decay(q, c0), _decay(q, c0 + 1)
                p0, p1 = scores * l0, scores * l1
                dy_st = _split_heads(d_y[:, ln]).astype(BF16)
                d_p = lax.dot_general(dy_st, q["xs_w"][:, ln].astype(BF16), _DIMS["nt"], preferred_element_type=F32)
                d_p0, d_p1 = d_p[:CHUNK], d_p[CHUNK:]
                d_scores = d_scores + d_p0 * l0 + d_p1 * l1
                for col, t in ((c0, d_p0 * p0), (c0 + 1, d_p1 * p1)):
                    d_cs = d_cs + jnp.sum(t - t.T, axis=1, keepdims=True) * (lane_row == col).astype(F32)
                p_st = jnp.concatenate([p0, p1], axis=0).astype(BF16)
                dxs_parts.append(lax.dot_general(p_st, dy_st, _DIMS["tn"], preferred_element_type=F32)
                                 + q["wt_w"][:, ln] * b_dh[:, j * LANES:(j + 1) * LANES])
            d_sc = d_scores.astype(BF16)
            d_c = d_c + jnp.dot(d_sc, bg, preferred_element_type=F32)
            d_b = d_b + lax.dot_general(d_sc, cg, _DIMS["tn"], preferred_element_type=F32)
            dxbc_ref[:, ds + g * N_STATE:ds + (g + 1) * N_STATE] = d_b
            dxbc_ref[:, ds + (2 + g) * N_STATE:ds + (3 + g) * N_STATE] = d_c
        d_xs = jnp.concatenate(dxs_parts, axis=1)
        narrow_m = (lax.broadcasted_iota(jnp.int32, (ds, LANES), 0) // HEAD_DIM
                    == lax.broadcasted_iota(jnp.int32, (ds, LANES), 1)).astype(BF16)
        rows8 = lambda v: jnp.broadcast_to(v, (8, ds))
        stacked = jnp.concatenate(
            [jnp.concatenate(dwt_parts, axis=1), jnp.concatenate(de_parts, axis=1), d_xs * q["xh"],
             rows8(jnp.concatenate(ddec_parts, axis=1)), rows8(jnp.sum(d_y * q["xh"], axis=0, keepdims=True))], axis=0)
        hi = stacked.astype(BF16)
        lo = (stacked - hi.astype(F32)).astype(BF16)
        sums = (jnp.dot(hi, narrow_m, preferred_element_type=F32) + jnp.dot(lo, narrow_m, preferred_element_type=F32))
        n_wt, n_e, n_xs = sums[:CHUNK], sums[CHUNK:2 * CHUNK], sums[2 * CHUNK:3 * CHUNK]
        n_dec, n_dsk = sums[3 * CHUNK:3 * CHUNK + 1], sums[3 * CHUNK + 8:3 * CHUNK + 9]
        e, wt, dec = jnp.exp(q["cs"]), jnp.exp(q["tot"] - q["cs"]), jnp.exp(q["tot"])
        d_wt = n_wt * wt
        d_cs = d_cs + n_e * e - d_wt
        d_tot = jnp.sum(d_wt, axis=0, keepdims=True) + n_dec * dec
        d_da = jnp.dot(q["tri_t"], d_cs, precision=HI, preferred_element_type=F32) + d_tot
        d_dt = d_da * q["a"] + n_xs
        dxbc_ref[:, :ds] = d_xs * q["dt_w"] + q["dsk_w"] * d_y
        dalog_ref[...] += jnp.sum(d_da * q["dt"], axis=0, keepdims=True) * q["a"]
        d_raw = d_dt * q["sig"]
        ddtb_ref[...] += jnp.sum(d_raw, axis=0, keepdims=True)
        ddsk_ref[...] += n_dsk
        ddt_ref[...] = pltpu.roll(d_raw, n_head, axis=1) if rev else d_raw
        hosted.finish()

    vec = pl.BlockSpec((1, LANES), lambda b, s: (0, 0))
    vec_shape = jax.ShapeDtypeStruct((1, LANES), F32)
    in_specs, out_shape, out_specs, scratch, args = hosted.call_args(
        [pl.BlockSpec((CHUNK, xw), lambda b, s: (blk(b, s), 0)),
         pl.BlockSpec((CHUNK, LANES), lambda b, s: (blk(b, s), dt_cb)),
         pl.BlockSpec((None, None, N_STATE, ds), lambda b, s: (b, step(s), 0, 0)),
         pl.BlockSpec((CHUNK, ds), lambda b, s: (dy_blk(b, s), 0)), vec, vec, vec],
        (_big((n_tok, xw), F32), _big((n_tok, LANES), F32), vec_shape, vec_shape, vec_shape),
        (pl.BlockSpec((CHUNK, xw), lambda b, s: (blk(b, s), 0)),
         pl.BlockSpec((CHUNK, LANES), lambda b, s: (blk(b, s), 0)), vec, vec, vec),
        [pltpu.VMEM((N_STATE, ds), F32)], [xbc, proj, hs, dy, dtb, alog, dsk])
    return hosted.results(pl.pallas_call(
        body, name=name, grid=(n_ex, n_step), out_shape=out_shape, in_specs=in_specs, out_specs=out_specs,
        scratch_shapes=scratch, compiler_params=_params(48 << 20, 2),
    )(*_in_hbm(args)))


def final_loss(x3, target, w, *, tm):
    n, d = x3.shape

    def body(x_ref, t_ref, w_ref, dx_ref, dw_ref, loss_ref):
        i = pl.program_id(0)
        t = t_ref[...]

        def per_feature(xv, wv):
            err = _rms(xv, wv) - t
            return 0.5 * jnp.sum(err * err, axis=0, keepdims=True) / d

        lv, vjp = jax.vjp(per_feature, x_ref[...], w_ref[...])
        dx, dw = vjp(jnp.ones_like(lv))
        dx_ref[...] = dx

        @pl.when(i == 0)
        def _():
            dw_ref[...] = dw
            loss_ref[...] = lv

        @pl.when(i > 0)
        def _():
            dw_ref[...] += dw
            loss_ref[...] += lv

    tile = pl.BlockSpec((tm, d), lambda i: (i, 0))
    vec = pl.BlockSpec((1, d), lambda i: (0, 0))
    return pl.pallas_call(
        body, name="final_loss", grid=(n // tm,), in_specs=[tile, tile, vec],
        out_shape=(jax.ShapeDtypeStruct((n, d), F32), jax.ShapeDtypeStruct((1, d), F32), jax.ShapeDtypeStruct((1, d), F32)),
        out_specs=(tile, vec, vec), compiler_params=_params(tm * d * 4 * 16 + (8 << 20)),
    )(x3, target, w)


def sum_slots(name, arr):
    n_slot, n_row, width = arr.shape
    tm = _row_tile(n_row, width * n_slot, mult=16)

    def body(a_ref, o_ref):
        acc = a_ref[0].astype(F32)
        for j in range(1, n_slot):
            acc = acc + a_ref[j].astype(F32)
        o_ref[...] = acc

    return pl.pallas_call(
        body, name=name, grid=(n_row // tm,), out_shape=jax.ShapeDtypeStruct((n_row, width), F32),
        in_specs=[pl.BlockSpec((n_slot, tm, width), lambda i: (0, i, 0))],
        out_specs=pl.BlockSpec((tm, width), lambda i: (i, 0)), compiler_params=_params(),
    )(arr)


def adamw(name, w, g_slots, m, v):
    n_slot, n_row, width = g_slots.shape
    tm = _row_tile(n_row, width * 2)

    def body(w_ref, g_ref, m_ref, v_ref, go_ref, d_ref, mo_ref, vo_ref):
        g = g_ref[0]
        for j in range(1, n_slot):
            g = g + g_ref[j]
        m2 = ADAM_B1 * m_ref[...] + (1.0 - ADAM_B1) * g
        v2 = ADAM_B2 * v_ref[...] + (1.0 - ADAM_B2) * jnp.square(g)
        m_hat = m2 / (1.0 - ADAM_B1 ** ADAM_STEP)
        v_hat = v2 / (1.0 - ADAM_B2 ** ADAM_STEP)
        go_ref[...] = g
        d_ref[...] = -ADAM_LR * (m_hat / (jnp.sqrt(v_hat) + ADAM_EPS) + ADAM_WD * w_ref[...])
        mo_ref[...] = m2
        vo_ref[...] = v2

    tile = pl.BlockSpec((tm, width), lambda i: (i, 0))
    shape = jax.ShapeDtypeStruct((n_row, width), F32)
    return pl.pallas_call(
        body, name=name, grid=(n_row // tm,), out_shape=(shape,) * 4,
        in_specs=[tile, pl.BlockSpec((n_slot, tm, width), lambda i: (0, i, 0)), tile, tile],
        out_specs=(tile,) * 4, compiler_params=_params(),
    )(w, g_slots, m, v)


def cctx_grad(q_all, c_ctx_row):
    d = c_ctx_row.shape[1]

    def body(q_ref, c_ref, o_ref):
        acc = q_ref[0, 0:1, :]
        for j in (2, 4, 6):
            acc = acc + q_ref[j, 0:1, :]
        _, vjp = jax.vjp(_silu, c_ref[...])
        o_ref[...] = vjp(acc)[0]

    return pl.pallas_call(
        body, name="cctx_grad", out_shape=jax.ShapeDtypeStruct((1, d), F32),
    )(q_all, c_ctx_row)


def loss_total(pack_sum, d):
    def body(p_ref, o_ref):
        o_ref[...] = jnp.sum(p_ref[:, 0:d], axis=1, keepdims=True)

    return pl.pallas_call(
        body, name="loss_total", out_shape=jax.ShapeDtypeStruct((1, 1), F32),
    )(pack_sum)


class _Plan:
    def __init__(self):
        self.builders, self.got = {}, {}

    def on(self, host, key, builder):
        self.builders.setdefault(host, []).append((key, builder))

    def run(self, host, fn, *args, **kw):
        if host not in self.builders:
            return fn(host, *args, **kw)
        keys, riders = zip(*[(key, builder(self)) for key, builder in self.builders[host]])
        res, landed = fn(host, *args, rider=Riders(riders), **kw)
        for key, r in zip(keys, riders):
            self.got[key], landed = landed[:r.n], landed[r.n:]
        return res


def _val(w):
    return w() if callable(w) else w


def _matmul_tile(n_rows, tm):
    return 2 * tm if n_rows % (2 * tm) == 0 else tm


def _ffn_fwd(plan, tag, xin, n_rows, tm, seg_fn, shift, scale, gate, norm_w, wg, wu, wd):
    d = xin[1]
    n_tiles = n_rows // tm
    (h,) = plan.run(f"{tag}_norm", rowwise, fn_norm_mod, [xin], [shift, scale], [norm_w], [(n_rows, d, BF16)],
                    tm=tm, n_tiles=n_tiles, seg_fn=seg_fn)
    tmm = _matmul_tile(n_rows, tm)
    g = plan.run(f"{tag}_gate", matmul, [(h, _val(wg))], "nn", out_dtype=BF16, b_ch=True, out_ch=True, tm=tmm)
    u, act = plan.run(f"{tag}_up", matmul, [(h, _val(wu))], "nn", b_ch=True, out_ch=True, tm=tm, fold=True,
                      post=([g], lambda acc, gv: (acc, fn_act(gv, acc)[0]), [BF16, BF16]))
    f = plan.run(f"{tag}_down", matmul, [(act, _val(wd))], "nn", a_ch=True, b_ch=True, tm=tmm, fold=True)
    (xo,) = plan.run(f"{tag}_resid", rowwise, make_fn_resid(0.5), [xin, row(f)], [gate], [], [(n_rows, d, F32)],
                     tm=tm, n_tiles=n_tiles, seg_fn=seg_fn)
    return xo, (h, g, u, act, f)


def _ffn_bwd(plan, tag, d_xo, saved, xin, n_rows, tm, seg_fn, first_fn, shift, scale, gate, norm_w, wg, wu, wd, dx_rows, dx_limit):
    h, g, u, act, f = saved
    d = xin[1]
    n_tiles = n_rows // tm
    n_ch, _, n_hid = g.shape
    d_f, d_gate = plan.run(f"{tag}_resid_bwd", rowwise_bwd, make_fn_resid(0.5), [xin, row(f)], [gate], [], [[row(d_xo)]],
                           [None, (n_rows, BF16, None)], tm=tm, n_tiles=n_tiles, seg_fn=seg_fn, first_fn=first_fn)
    tmm = _matmul_tile(n_rows, tm)
    def act_vjp(d_act, gv, uv):
        s = jax.nn.sigmoid(gv)
        gs = gv * s
        return d_act * uv * (s + gs * (1.0 - s)), d_act * gs
    d_g, d_u = plan.run(f"{tag}_down_dx", matmul, [(d_f, wd)], "nt", b_ch=True, out_ch=True, tm=tmm,
                        post=([g, u], act_vjp, [BF16, BF16]))
    plan.got[f"{tag}_d_wd"] = plan.run(f"{tag}_down_dw", matmul, [(act, d_f)], "tn", out_dtype=BF16, a_ch=True, out_ch=True, tm=tmm)
    d_h = plan.run(f"{tag}_up_dx", matmul, [(d_g, wg), (d_u, wu)], "nt", a_ch=True, b_ch=True, tm=tmm)
    plan.got[f"{tag}_d_wg"] = plan.run(f"{tag}_gate_dw", matmul, [(d_g, h)], "tn", out_dtype=BF16, a_ch=True, out_ch=True, tm=tmm)
    plan.got[f"{tag}_d_wu"] = plan.run(f"{tag}_up_dw", matmul, [(d_u, h)], "tn", out_dtype=BF16, a_ch=True, out_ch=True, tm=tmm)
    d_x, d_shift, d_scale, d_nw = plan.run(
        f"{tag}_norm_bwd", rowwise_bwd, fn_norm_mod, [xin], [shift, scale], [norm_w], [[row(d_h)]], [(dx_rows, F32, dx_limit)],
        tm=tm, n_tiles=n_tiles, seg_fn=seg_fn, first_fn=first_fn, adds={0: (row(d_xo), None)})
    return d_x, (d_shift, d_scale, d_gate), d_nw


def kernel(x, c, ctx, c_ctx, w_mod, b_mod, norm_ffn1, ffn1_gate, ffn1_up, ffn1_down, norm_mix, w_in, ssm_conv_w, ssm_conv_b, dt_bias_fwd, dt_bias_bwd, a_log_fwd, a_log_bwd, ssm_d, ssm_norm_w, cconv_w, cconv_b, cconv_ln_w, cconv_ln_b, w_out, norm_ffn2, ffn2_gate, ffn2_up, ffn2_down, final_norm, loss_target, m_c_ctx, m_w_mod, m_b_mod, m_norm_ffn1, m_ffn1_gate, m_ffn1_up, m_ffn1_down, m_norm_mix, m_w_in, m_ssm_conv_w, m_ssm_conv_b, m_dt_bias_fwd, m_dt_bias_bwd, m_a_log_fwd, m_a_log_bwd, m_ssm_d, m_ssm_norm_w, m_cconv_w, m_cconv_b, m_cconv_ln_w, m_cconv_ln_b, m_w_out, m_norm_ffn2, m_ffn2_gate, m_ffn2_up, m_ffn2_down, m_final_norm, v_c_ctx, v_w_mod, v_b_mod, v_norm_ffn1, v_ffn1_gate, v_ffn1_up, v_ffn1_down, v_norm_mix, v_w_in, v_ssm_conv_w, v_ssm_conv_b, v_dt_bias_fwd, v_dt_bias_bwd, v_a_log_fwd, v_a_log_bwd, v_ssm_d, v_ssm_norm_w, v_cconv_w, v_cconv_b, v_cconv_ln_w, v_cconv_ln_b, v_w_out, v_norm_ffn2, v_ffn2_gate, v_ffn2_up, v_ffn2_down, v_final_norm):
    weights = dict(c_ctx=c_ctx, w_mod=w_mod, b_mod=b_mod, norm_ffn1=norm_ffn1, ffn1_gate=ffn1_gate, ffn1_up=ffn1_up, ffn1_down=ffn1_down, norm_mix=norm_mix, w_in=w_in, ssm_conv_w=ssm_conv_w, ssm_conv_b=ssm_conv_b, dt_bias_fwd=dt_bias_fwd, dt_bias_bwd=dt_bias_bwd, a_log_fwd=a_log_fwd, a_log_bwd=a_log_bwd, ssm_d=ssm_d, ssm_norm_w=ssm_norm_w, cconv_w=cconv_w, cconv_b=cconv_b, cconv_ln_w=cconv_ln_w, cconv_ln_b=cconv_ln_b, w_out=w_out, norm_ffn2=norm_ffn2, ffn2_gate=ffn2_gate, ffn2_up=ffn2_up, ffn2_down=ffn2_down, final_norm=final_norm)
    mom1 = dict(c_ctx=m_c_ctx, w_mod=m_w_mod, b_mod=m_b_mod, norm_ffn1=m_norm_ffn1, ffn1_gate=m_ffn1_gate, ffn1_up=m_ffn1_up, ffn1_down=m_ffn1_down, norm_mix=m_norm_mix, w_in=m_w_in, ssm_conv_w=m_ssm_conv_w, ssm_conv_b=m_ssm_conv_b, dt_bias_fwd=m_dt_bias_fwd, dt_bias_bwd=m_dt_bias_bwd, a_log_fwd=m_a_log_fwd, a_log_bwd=m_a_log_bwd, ssm_d=m_ssm_d, ssm_norm_w=m_ssm_norm_w, cconv_w=m_cconv_w, cconv_b=m_cconv_b, cconv_ln_w=m_cconv_ln_w, cconv_ln_b=m_cconv_ln_b, w_out=m_w_out, norm_ffn2=m_norm_ffn2, ffn2_gate=m_ffn2_gate, ffn2_up=m_ffn2_up, ffn2_down=m_ffn2_down, final_norm=m_final_norm)
    mom2 = dict(c_ctx=v_c_ctx, w_mod=v_w_mod, b_mod=v_b_mod, norm_ffn1=v_norm_ffn1, ffn1_gate=v_ffn1_gate, ffn1_up=v_ffn1_up, ffn1_down=v_ffn1_down, norm_mix=v_norm_mix, w_in=v_w_in, ssm_conv_w=v_ssm_conv_w, ssm_conv_b=v_ssm_conv_b, dt_bias_fwd=v_dt_bias_fwd, dt_bias_bwd=v_dt_bias_bwd, a_log_fwd=v_a_log_fwd, a_log_bwd=v_a_log_bwd, ssm_d=v_ssm_d, ssm_norm_w=v_ssm_norm_w, cconv_w=v_cconv_w, cconv_b=v_cconv_b, cconv_ln_w=v_cconv_ln_w, cconv_ln_b=v_cconv_ln_b, w_out=v_w_out, norm_ffn2=v_norm_ffn2, ffn2_gate=v_ffn2_gate, ffn2_up=v_ffn2_up, ffn2_down=v_ffn2_down, final_norm=v_final_norm)
    order = list(weights)

    n_ex, seq_len, d = x.shape
    ctx_len = ctx.shape[1]
    ds = d
    n_head = ds // HEAD_DIM
    xw = ds + 4 * N_STATE
    n_lat, n_ctx_rows = n_ex * seq_len, n_ex * ctx_len
    n_tok = n_lat + n_ctx_rows
    tm = math.gcd(math.gcd(512, seq_len), n_ctx_rows)
    seg_all, first_all = _segmenter(tm, seq_len, n_lat)
    lat_tiles = n_lat // tm

    xi, yi, ci = lax.axis_index("x"), lax.axis_index("y"), lax.axis_index("c")
    me, chip = 4 * xi + 2 * yi + ci, 2 * xi + yi

    (c_all,) = exchange("gather_c", [c], "all8")
    n_all = 8 * n_ex
    n_cond = -(-(n_all + 1) // 8) * 8
    cond = jnp.concatenate([c_all.reshape(n_all, d), c_ctx[None, :], jnp.zeros((n_cond - n_all - 1, d), F32)])
    mod_w = w_mod.shape[2]
    b_shard = lax.dynamic_slice(b_mod, (0, chip * mod_w), (1, mod_w))
    (mod_g,) = exchange("gather_mod", [mod_fwd(cond, w_mod[0], b_shard)], "chips")
    mod_full = mod_g.transpose(1, 0, 2).reshape(n_cond, N_CHIPS * mod_w)
    mod_mine = lax.dynamic_slice(mod_full, (me * n_ex, 0), (n_ex, 9 * d)).reshape(n_ex, 9, d)
    mod_ctx = mod_full[n_all].reshape(9, d)
    tabs = [jnp.concatenate([mod_mine[:, j], mod_ctx[j][None]])[:, None, :] for j in range(9)]
    lat = lambda t: t[:n_ex]

    bf = lambda w: w[0].astype(BF16)
    plan = _Plan()
    gather = lambda *ws: (lambda p: Rider(list(ws), "chips"))
    plan.on("ffn1_norm", "wg1", gather(bf(ffn1_gate)))
    plan.on("ffn1_gate", "wu1", gather(bf(ffn1_up)))
    plan.on("ffn1_up", "wd1", gather(bf(ffn1_down)))
    win_cut = d * 5 // 8
    plan.on("ffn1_down", "win_a", gather(bf(w_in)[:win_cut]))
    plan.on("ffn1_resid", "win_b", gather(bf(w_in)[win_cut:], ssm_conv_w[0], cconv_w[0]))
    xt = two_rows(x.reshape(n_lat, d), ctx.reshape(n_ctx_rows, d), lat_tiles)
    x1, saved1 = _ffn_fwd(plan, "ffn1", xt, n_tok, tm, seg_all, tabs[0], tabs[1], tabs[2], norm_ffn1,
                          lambda: plan.got["wg1"][0], lambda: plan.got["wu1"][0], lambda: plan.got["wd1"][0])
    (wg1,), (wu1,), (wd1,), (win_a,), (win_b, w5_g, w31_g) = (plan.got[k] for k in ("wg1", "wu1", "wd1", "win_a", "win_b"))
    win_g = jnp.concatenate([win_a, win_b], axis=1)
    unshard_cols = lambda t: t.transpose(1, 0, 2).reshape(t.shape[1], N_CHIPS * t.shape[2])
    win = unshard_cols(win_g)
    o_x, o_dt, o_glu = ds, ds + xw, ds + xw + 2 * n_head
    w_z, w_xbc, w_dt = win[:, :ds], win[:, o_x:o_dt], win[:, o_dt:o_glu]
    w_ga, w_gb = win[:, o_glu:o_glu + d], win[:, o_glu + d:]
    w_dtp = jnp.concatenate([w_dt, jnp.zeros((d, LANES - 2 * n_head), BF16)], axis=1)
    w_cat = jnp.concatenate([w_z, w_ga, w_gb, w_xbc, w_dtp], axis=1)
    cbw = d // 2
    xbc_cb, dt_cb = 3 * d // cbw, (3 * d + xw) // LANES
    w5, w31 = unshard_cols(w5_g), unshard_cols(w31_g)
    pad_vec = lambda v: jnp.concatenate([v.reshape(1, -1), jnp.zeros((1, LANES - v.size), F32)], axis=1)
    dtb_f, dtb_b, alog_f, alog_b = map(pad_vec, (dt_bias_fwd, dt_bias_bwd, a_log_fwd, a_log_bwd))
    dsk_f, dsk_b = pad_vec(ssm_d), jnp.zeros((1, LANES), F32)

    (h2,) = rowwise("mix_norm", fn_norm_mod, [row(x1)], [tabs[3], tabs[4]], [norm_mix], [(n_tok, d, BF16)],
                    tm=tm, n_tiles=n_tok // tm, seg_fn=seg_all)
    proj, (wg2,) = matmul("mix_proj", [(h2, w_cat)], "nn", tm=min(tm, 256), rider=Rider([bf(ffn2_gate)], "chips"))
    def conv5(name, src, cb0, flip):
        out = None
        for part, seq, off in (("lat", seq_len, 0), ("ctx", ctx_len, n_lat // ctx_len)):
            out = tapsum_roll(f"{name}_{part}", src, cb0, w5, 0, seq_len=seq, n_seq=n_ex, row_blk_off=off, width=seq,
                              piece=seq, cb=cbw, ncb=xw // cbw, pad=w5.shape[0] // 2, flip=flip,
                              place=((n_tok, xw), off, 0, out))
        return out

    craw = conv5("xbc_conv", proj, xbc_cb, False)
    (xbc,) = rowwise("xbc_silu", fn_silu_bias, [row(craw)], [], [ssm_conv_b], [(n_tok, xw, F32)], tm=tm, n_tiles=n_tok // tm)
    ssd = dict(n_ex=n_ex, seq_len=seq_len, ctx_len=ctx_len, ds=ds)
    (y_f, hs_f), (wu2, wd2) = ssd_fwd("ssd_fwd_f", xbc, proj, dt_cb, dtb_f, alog_f, dsk_f, rev=False,
                                      rider=Rider([bf(ffn2_up), bf(ffn2_down)], "chips"), **ssd)
    (y_b, hs_b), (wout_g,) = ssd_fwd("ssd_fwd_b", xbc, proj, dt_cb, dtb_b, alog_b, dsk_b, rev=True,
                                     rider=Rider([bf(w_out)], "chips"), **ssd)
    wout = wout_g.reshape(2 * d, d)
    wo_y, wo_u = wout[:ds], wout[ds:]
    fn_gate = make_fn_gate_groupnorm(ds)
    (yn,) = rowwise("ssd_gate", fn_gate, [row(y_f), row(y_b), row(proj, d, 0)], [], [ssm_norm_w], [(n_lat, ds, BF16)],
                    tm=tm, n_tiles=lat_tiles)
    (u0,) = rowwise("glu", fn_glu, [row(proj, d, 1), row(proj, d, 2)], [], [], [(n_lat, d, F32)], tm=tm, n_tiles=lat_tiles)
    cb31 = max(LANES, d // 4)
    ncb31 = (d // 2) // cb31
    pad31 = w31.shape[0] // 2
    piece31 = min(seq_len, 4 * GRID_W)
    v_w = tapsum_roll("cconv_cols", u0, 0, w31, 0, seq_len=seq_len, n_seq=n_ex, row_blk_off=0, width=GRID_W,
                      piece=piece31, cb=cb31, ncb=ncb31, pad=pad31, flip=False)
    v_h = tapsum_rows("cconv_rows", u0, ncb31, w31, ncb31, seq_len=seq_len, n_seq=n_ex, cb=cb31, ncb=ncb31, pad=pad31, flip=False)
    (un,) = rowwise("cconv_ln", fn_ln_silu, [row(v_w), row(v_h)], [], [cconv_b, cconv_ln_w, cconv_ln_b], [(n_lat, d, BF16)],
                    tm=tm, n_tiles=lat_tiles)
    mix = matmul("mix_out", [(yn, wo_y), (un, wo_u)], "nn", tm=tm)
    seg_lat, first_lat = _segmenter(tm, seq_len, n_lat)
    (x2,) = rowwise("mix_resid", make_fn_resid(1.0), [row(x1), row(mix)], [lat(tabs[5])], [], [(n_lat, d, F32)],
                    tm=tm, n_tiles=lat_tiles, seg_fn=seg_lat)
    x3, saved2 = _ffn_fwd(plan, "ffn2", row(x2), n_lat, tm, seg_lat, lat(tabs[6]), lat(tabs[7]), lat(tabs[8]), norm_ffn2, wg2, wu2, wd2)
    d_x3, d_final, loss_vec = final_loss(x3, loss_target.reshape(n_lat, d), final_norm.reshape(1, d), tm=tm)

    shard_cols = lambda t: t.reshape(t.shape[0], N_CHIPS, -1).transpose(1, 0, 2)

    def pieces(t):
        t = jnp.pad(t, ((0, 0), (0, -t.shape[1] % 32), (0, 0)))
        return t.reshape(2 * N_CHIPS, t.shape[1] // 2, t.shape[2]).astype(BF16)

    scatter = lambda *ts: Rider([pieces(t) for t in ts], "all8", scatter=True)
    halves = lambda names, landed: Rider([sum_slots(f"sum_{nm}", r) for nm, r in zip(names, landed)], "sibling")
    swapped = {}
    plan.on("ffn2_up_dx", "sc_ffn2_down", lambda p: scatter(p.got["ffn2_d_wd"]))
    plan.on("ffn2_up_dw", "sc_ffn2_gate", lambda p: scatter(p.got["ffn2_d_wg"]))
    d_x2, (d_s6, d_s7, d_g8), d_nffn2 = _ffn_bwd(
        plan, "ffn2", d_x3, saved2, row(x2), n_lat, tm, seg_lat, first_lat, lat(tabs[6]), lat(tabs[7]), lat(tabs[8]), norm_ffn2,
        wg2, wu2, wd2, n_lat, None)
    d_mix, d_g5 = rowwise_bwd("mix_resid_bwd", make_fn_resid(1.0), [row(x1), row(mix)], [lat(tabs[5])], [], [[row(d_x2)]],
                              [None, (n_lat, BF16, None)], tm=tm, n_tiles=lat_tiles, seg_fn=seg_lat, first_fn=first_lat)
    d_yn = matmul("mix_out_dy", [(d_mix, wo_y)], "nt", tm=tm)
    d_un = matmul("mix_out_du", [(d_mix, wo_u)], "nt", tm=tm)
    d_wout = jnp.concatenate([matmul("mix_out_dwy", [(yn, d_mix)], "tn", out_dtype=BF16, tm=tm), matmul("mix_out_dwu", [(un, d_mix)], "tn", out_dtype=BF16, tm=tm)])
    d_vw, d_vh, d_cb, d_lnw, d_lnb = rowwise_bwd(
        "cconv_ln_bwd", fn_ln_silu, [row(v_w), row(v_h)], [], [cconv_b, cconv_ln_w, cconv_ln_b], [[row(d_un)]],
        [(n_lat, F32, None)] * 2, tm=tm, n_tiles=lat_tiles)
    d_u0 = tapsum_roll("cconv_cols_dx", d_vw, 0, w31, 0, seq_len=seq_len, n_seq=n_ex, row_blk_off=0, width=GRID_W,
                       piece=piece31, cb=cb31, ncb=ncb31, pad=pad31, flip=True, place=((n_lat, d), 0, 0, None))
    d_u0 = tapsum_rows("cconv_rows_dx", d_vh, 0, w31, ncb31, seq_len=seq_len, n_seq=n_ex, cb=cb31, ncb=ncb31, pad=pad31,
                       flip=True, place=((n_lat, d), 0, ncb31, d_u0))
    d_w31 = jnp.concatenate([
        tapgrad_roll("cconv_cols_dw", d_vw, 0, 0, u0, 0, 0, n_tap=w31.shape[0], seq_len=seq_len, n_seq=n_ex, width=GRID_W,
                     piece=piece31, cb=cb31, ncb=ncb31, pad=pad31),
        tapgrad_rows("cconv_rows_dw", d_vh, 0, u0, ncb31, n_tap=w31.shape[0], seq_len=seq_len, n_seq=n_ex, cb=cb31,
                     ncb=ncb31, pad=pad31)], axis=1)
    d_ga, d_gb = rowwise_bwd("glu_bwd", fn_glu, [row(proj, d, 1), row(proj, d, 2)], [], [], [[row(d_u0)]],
                             [(n_lat, BF16, None)] * 2, tm=tm, n_tiles=lat_tiles)
    d_ysum, d_z, d_ssmnw = rowwise_bwd(
        "ssd_gate_bwd", fn_gate, [row(y_f), row(y_b), row(proj, d, 0)], [], [ssm_norm_w], [[row(d_yn)]],
        [(n_lat, F32, None), None, (n_lat, BF16, None)], tm=tm, n_tiles=lat_tiles)
    (dxbc_f, ddt_f, dalog_f, ddtb_f, ddsk), landed = ssd_bwd(
        "ssd_bwd_f", xbc, proj, dt_cb, hs_f, d_ysum, dtb_f, alog_f, dsk_f, rev=False,
        rider=scatter(plan.got["ffn2_d_wu"], d_wout.reshape(N_CHIPS, -1, d)), **ssd)
    (dxbc_b, ddt_b, dalog_b, ddtb_b, _), both = ssd_bwd(
        "ssd_bwd_b", xbc, proj, dt_cb, hs_b, d_ysum, dtb_b, alog_b, dsk_b, rev=True,
        rider=halves(["ffn2_down", "ffn2_gate"], plan.got["sc_ffn2_down"] + plan.got["sc_ffn2_gate"]), **ssd)
    swapped.update(zip(["ffn2_down", "ffn2_gate"], both))
    (d_craw, d_conv_b), both = rowwise_bwd(
        "xbc_silu_bwd", fn_silu_bias, [row(craw)], [], [ssm_conv_b], [[row(dxbc_f), row(dxbc_b)]],
        [(n_tok, F32, None)], tm=tm, n_tiles=n_tok // tm, rider=halves(["ffn2_up", "w_out"], landed))
    swapped.update(zip(["ffn2_up", "w_out"], both))
    d_pxbc = conv5("xbc_conv_dx", d_craw, 0, True)
    g5 = lambda name, seq, off: tapgrad_roll(name, d_craw, 0, off, proj, xbc_cb, off, n_tap=w5.shape[0], seq_len=seq,
                                             n_seq=n_ex, width=seq, piece=seq, cb=cbw, ncb=xw // cbw, pad=w5.shape[0] // 2)
    d_w5 = g5("xbc_conv_lat_dw", seq_len, 0) + g5("xbc_conv_ctx_dw", ctx_len, n_lat // ctx_len)
    lat_pairs = [(d_z, w_z), (d_ga, w_ga), (d_gb, w_gb), (d_pxbc, w_xbc), (ddt_f, w_dtp), (ddt_b, w_dtp)]
    d_h2 = matmul("mix_proj_dx_lat", lat_pairs, "nt", rows=n_lat, tm=min(tm, 256), place=(n_tok, 0, None))
    d_h2 = matmul("mix_proj_dx_ctx", lat_pairs[3:], "nt", rows=n_ctx_rows, row_off=n_lat, tm=min(tm, 256),
                  place=(n_tok, n_lat, d_h2))
    d_wz = matmul("mix_proj_dwz", [(d_z, h2)], "tn", out_dtype=BF16, rows=n_lat, tm=tm)
    d_wga = matmul("mix_proj_dwa", [(d_ga, h2)], "tn", out_dtype=BF16, rows=n_lat, tm=tm)
    d_wgb = matmul("mix_proj_dwb", [(d_gb, h2)], "tn", out_dtype=BF16, rows=n_lat, tm=tm)
    d_wxbc = matmul("mix_proj_dwx", [(d_pxbc, h2)], "tn", out_dtype=BF16, tm=tm)
    d_wdt = matmul("mix_proj_dwt", [(ddt_f, h2), (ddt_b, h2)], "tn", out_dtype=BF16, tm=tm)
    d_win_t = jnp.concatenate([d_wz, d_wxbc, d_wdt[:2 * n_head], d_wga, d_wgb]).reshape(N_CHIPS, -1, d)
    d_x1, d_s3, d_s4, d_nmix = rowwise_bwd(
        "mix_norm_bwd", fn_norm_mod, [row(x1)], [tabs[3], tabs[4]], [norm_mix], [[row(d_h2)]], [(n_tok, F32, None)],
        tm=tm, n_tiles=n_tok // tm, seg_fn=seg_all, first_fn=first_all, adds={0: (row(d_x2), lat_tiles)})
    mix_names = ["w_in", "ssm_conv_w", "cconv_w"]
    plan.on("ffn1_down_dx", "sc_conv", lambda p: scatter(shard_cols(d_w5), shard_cols(d_w31)))
    plan.on("ffn1_up_dx", "sc_win", lambda p: scatter(d_win_t))
    plan.on("ffn1_gate_dw", "sc_ffn1_down", lambda p: scatter(p.got["ffn1_d_wd"]))
    plan.on("ffn1_up_dw", "sc_ffn1_gate", lambda p: scatter(p.got["ffn1_d_wg"]))
    plan.on("ffn1_up_dw", "sw_mix", lambda p: halves(mix_names, p.got["sc_win"] + p.got["sc_conv"]))
    plan.on("ffn1_norm_bwd", "sc_ffn1_up", lambda p: scatter(p.got["ffn1_d_wu"]))
    plan.on("ffn1_norm_bwd", "sw_ffn1_down", lambda p: halves(["ffn1_down"], p.got["sc_ffn1_down"]))
    d_xt, (d_s0, d_s1, d_g2), d_nffn1 = _ffn_bwd(
        plan, "ffn1", d_x1, saved1, xt, n_tok, tm, seg_all, first_all, tabs[0], tabs[1], tabs[2], norm_ffn1, wg1, wu1, wd1,
        n_lat, lat_tiles)
    swapped.update(zip(mix_names + ["ffn1_down"], plan.got["sw_mix"] + plan.got["sw_ffn1_down"]))
    last_names = ["ffn1_gate", "ffn1_up"]
    last = halves(last_names, plan.got["sc_ffn1_gate"] + plan.got["sc_ffn1_up"])
    grad_x = d_xt.reshape(n_ex, seq_len, d)

    with_ctx0 = lambda t: jnp.concatenate([t, jnp.zeros((1, 1, d), F32)])
    d_tabs = [d_s0, d_s1, d_g2, d_s3, d_s4, with_ctx0(d_g5), with_ctx0(d_s6), with_ctx0(d_s7), with_ctx0(d_g8)]
    d_mod_rows = jnp.concatenate([t[:, 0, :] for t in d_tabs], axis=1)
    n_pad_rows = -(-(n_ex + 1) // 8) * 8
    d_mod_rows = jnp.concatenate([d_mod_rows, jnp.zeros((n_pad_rows - n_ex - 1, 9 * d), F32)])
    small = [("loss", loss_vec), ("norm_ffn1", d_nffn1), ("norm_mix", d_nmix), ("ssm_conv_b", d_conv_b),
             ("dt_bias_fwd", ddtb_f[:, :n_head]), ("dt_bias_bwd", ddtb_b[:, :n_head]), ("a_log_fwd", dalog_f[:, :n_head]),
             ("a_log_bwd", dalog_b[:, :n_head]), ("ssm_d", ddsk[:, :n_head]), ("ssm_norm_w", d_ssmnw), ("cconv_b", d_cb),
             ("cconv_ln_w", d_lnw), ("cconv_ln_b", d_lnb), ("norm_ffn2", d_nffn2), ("final_norm", d_final)]
    n_small = sum(v.size for _, v in small)
    n_pack = -(-n_small // (8 * LANES)) * (8 * LANES)
    pack = jnp.concatenate([v.reshape(-1) for _, v in small] + [jnp.zeros((n_pack - n_small,), F32)]).reshape(-1, LANES)
    (pack_all, d_mod_all), both = exchange_many("gather_small_swap_last", [Rider([pack, d_mod_rows], "all8"), last])
    swapped.update(zip(last_names, both))
    pack_sum = sum_slots("small_sum", pack_all)
    loss = loss_total(pack_sum.reshape(1, n_pack), d).reshape(())
    flat_sum = pack_sum.reshape(-1)
    small_grads, pos = {}, 0
    for nm, v in small:
        small_grads[nm] = flat_sum[pos:pos + v.size]
        pos += v.size
    d_mod_all = d_mod_all.reshape(8 * n_pad_rows, 9 * d)
    cond_rows = [jnp.concatenate([cond[j * n_ex:(j + 1) * n_ex], c_ctx[None, :],
                                  jnp.zeros((n_pad_rows - n_ex - 1, d), F32)]) for j in range(8)]
    cond_bwd = jnp.concatenate(cond_rows)
    d_mod_shard = lax.dynamic_slice(d_mod_all, (0, chip * mod_w), (8 * n_pad_rows, mod_w))
    g_wmod, g_bmod, q_part = mod_bwd(cond_bwd, d_mod_shard, d_mod_all, w_mod[0],
                                     tuple(j * n_pad_rows + n_ex for j in range(8)))
    (q_all,) = exchange("gather_cctx", [q_part], "all8")
    g_cctx = cctx_grad(q_all, c_ctx.reshape(1, d))
    small_grads["c_ctx"], small_grads["b_mod"] = g_cctx.reshape(-1), g_bmod.reshape(-1)

    transposed = {"ffn1_gate", "ffn1_up", "ffn2_gate", "ffn2_up", "w_in"}
    results = {}
    for nm, both in swapped.items():
        flip = (lambda t: jnp.swapaxes(t, 1, 2)) if nm in transposed else (lambda t: t)
        shape = flip(weights[nm]).shape
        two_d = lambda t: flip(t).reshape(shape[-2], shape[-1])
        g_full = both.reshape(1, -1, shape[-1])[:, :shape[-2]]
        results[nm] = [flip(r.reshape(shape)) for r in
                       adamw(f"adamw_{nm}", two_d(weights[nm]), g_full, two_d(mom1[nm]), two_d(mom2[nm]))]
    results["w_mod"] = [r.reshape(w_mod.shape) for r in adamw("adamw_w_mod", w_mod[0], g_wmod[None], m_w_mod[0], v_w_mod[0])]
    small_names = [nm for nm in order if nm not in results]
    n_sm = sum(weights[nm].size for nm in small_names)
    n_smp = -(-n_sm // (8 * LANES)) * (8 * LANES)
    packed = lambda src: jnp.concatenate([src[nm].reshape(-1) for nm in small_names] + [jnp.zeros((n_smp - n_sm,), F32)]).reshape(-1, LANES)
    sm_out = adamw("adamw_small", packed(weights), packed(small_grads)[None], packed(mom1), packed(mom2))
    pos = 0
    for nm in small_names:
        size = weights[nm].size
        results[nm] = [r.reshape(-1)[pos:pos + size].reshape(weights[nm].shape) for r in sm_out]
        pos += size
    return (loss, grad_x, *[results[nm][0] for nm in order], *[results[nm][1] for nm in order],
            *[results[nm][2] for nm in order], *[results[nm][3] for nm in order])
```

```python
import functools
import math

import jax
import jax.numpy as jnp
from jax import lax
from jax.experimental import pallas as pl
from jax.experimental.pallas import tpu as pltpu

F32 = jnp.float32
BF16 = jnp.bfloat16
HI = lax.Precision.HIGHEST
MESH = pl.DeviceIdType.MESH

EPS = 1e-6
GRID_W = 64
HEAD_DIM = 64
N_STATE = 128
CHUNK = 128
LANES = 128
N_CHIPS = 4
ADAM_LR, ADAM_B1, ADAM_B2, ADAM_EPS, ADAM_WD, ADAM_STEP = 0.001, 0.9, 0.999, 1e-08, 0.01, 10
VMEM_CAP = 56 * 1024 * 1024


def _params(vmem_bytes=None, n_axes=1):
    kw = dict(dimension_semantics=("arbitrary",) * n_axes)
    if vmem_bytes is not None:
        kw["vmem_limit_bytes"] = int(min(VMEM_CAP, max(32 * 1024 * 1024, vmem_bytes)))
    return pltpu.CompilerParams(**kw)


def _big(shape, dtype):
    return pltpu.HBM(tuple(shape), dtype)


def _in_hbm(args):
    return [pltpu.with_memory_space_constraint(a, pltpu.HBM) if a.size * a.dtype.itemsize >= (1 << 20) else a for a in args]


def _nbytes(shape, dtype):
    return math.prod(shape) * jnp.dtype(dtype).itemsize


def _row_tile(rows, width, cap_bytes=1 << 20, mult=8):
    best = None
    for t in range(mult, rows + 1, mult):
        if rows % t == 0 and t * width * 4 <= cap_bytes:
            best = t
    return best if best is not None else rows


_MODES = {"all8": (8, (1, 2, 3, 4, 5, 6, 7), 0), "chips": (4, (2, 4, 6), 1), "sibling": (2, (1,), 0)}


class Rider:
    def __init__(self, arrs, mode, scatter=False):
        self.arrs, self.scatter = list(arrs), scatter
        self.nslot, self.deltas, self.shift = _MODES[mode]
        self.n = len(self.arrs)
        self.out_shape = [jax.ShapeDtypeStruct((self.nslot,) + (a.shape[1:] if scatter else a.shape), a.dtype)
                          for a in self.arrs]
        any_spec = pl.BlockSpec(memory_space=pl.ANY)
        self.in_specs = [any_spec] * self.n
        self.out_specs = [any_spec] * self.n
        n_peer = len(self.deltas)
        self.scratch = [pltpu.SemaphoreType.DMA((self.n, n_peer)), pltpu.SemaphoreType.DMA((self.n, n_peer)),
                        pltpu.SemaphoreType.DMA((self.n,))]

    def _copies(self, ins, outs, sems, arrivals):
        send_sems, recv_sems, local_sems = sems
        x, y, c = lax.axis_index("x"), lax.axis_index("y"), lax.axis_index("c")
        me = 4 * x + 2 * y + c
        slot_of = lambda dev: (dev >> self.shift) & (self.nslot - 1)
        src = lambda a, slot: ins[a].at[slot] if self.scatter else ins[a]
        flip = lambda v, bit: 1 - v if bit else v

        def remote(a, k, d, from_slot, to_slot):
            return pltpu.make_async_remote_copy(
                src_ref=src(a, from_slot), dst_ref=outs[a].at[to_slot], send_sem=send_sems.at[a, k],
                recv_sem=recv_sems.at[a, k], device_id=(flip(x, (d >> 2) & 1), flip(y, (d >> 1) & 1), flip(c, d & 1)),
                device_id_type=MESH)

        mine = slot_of(me)
        local = [pltpu.make_async_copy(src(a, mine), outs[a].at[mine], local_sems.at[a]) for a in range(self.n)]
        sends = [remote(a, k, d, slot_of(me ^ d), mine) for k, d in enumerate(self.deltas) for a in range(self.n)]
        if not arrivals:
            return local, sends
        return local, sends, [remote(a, k, d, mine, slot_of(me ^ d)) for k, d in enumerate(self.deltas) for a in range(self.n)]

    def start(self, ins, outs, sems):
        local, sends = self._copies(ins, outs, sems, arrivals=False)
        for cp in local + sends:
            cp.start()

    def wait(self, ins, outs, sems):
        local, sends, recvs = self._copies(ins, outs, sems, arrivals=True)
        for cp in recvs:
            cp.wait_recv()
        for cp in sends:
            cp.wait_send()
        for cp in local:
            cp.wait()


class Riders:
    def __init__(self, riders):
        self.riders = list(riders)
        self.n = sum(r.n for r in self.riders)
        cat = lambda attr: [v for r in self.riders for v in getattr(r, attr)]
        self.arrs, self.out_shape, self.in_specs = cat("arrs"), cat("out_shape"), cat("in_specs")
        self.out_specs, self.scratch = cat("out_specs"), cat("scratch")

    def _each(self, method, ins, outs, sems):
        i = s = 0
        for r in self.riders:
            getattr(r, method)(ins[i:i + r.n], outs[i:i + r.n], sems[s:s + len(r.scratch)])
            i, s = i + r.n, s + len(r.scratch)

    def start(self, ins, outs, sems):
        self._each("start", ins, outs, sems)

    def wait(self, ins, outs, sems):
        self._each("wait", ins, outs, sems)


class _Hosted:
    def __init__(self, rider, n_in, n_out, n_scratch, grid):
        self.rider, self.n_in, self.n_out, self.n_scratch, self.grid = rider, n_in, n_out, n_scratch, grid
        self.n = rider.n if rider else 0

    def split(self, refs):
        a, b = self.n_in, self.n_in + self.n
        c, e = b + self.n_out, b + self.n_out + self.n
        self._r = (refs[a:b], refs[c:e], refs[e + self.n_scratch:])
        if self.rider:
            ids = [pl.program_id(ax) for ax in range(len(self.grid))]
            first = functools.reduce(jnp.logical_and, [i == 0 for i in ids]) if ids else True
            pl.when(first)(lambda: self.rider.start(*self._r))
        return refs[:a], refs[b:c], refs[e:e + self.n_scratch]

    def finish(self):
        if self.rider:
            ids = [pl.program_id(ax) for ax in range(len(self.grid))]
            last = functools.reduce(jnp.logical_and, [i == n - 1 for i, n in zip(ids, self.grid)]) if ids else True
            pl.when(last)(lambda: self.rider.wait(*self._r))

    def call_args(self, in_specs, out_shape, out_specs, scratch, args):
        r = self.rider
        if not r:
            return list(in_specs), tuple(out_shape), tuple(out_specs), list(scratch), list(args)
        return (list(in_specs) + r.in_specs, tuple(out_shape) + tuple(r.out_shape), tuple(out_specs) + tuple(r.out_specs),
                list(scratch) + r.scratch, list(args) + r.arrs)

    def results(self, res, unwrap=True):
        res = list(res) if isinstance(res, (tuple, list)) else [res]
        host = res[:self.n_out]
        host = host[0] if (self.n_out == 1 and unwrap) else tuple(host)
        return (host, res[self.n_out:]) if self.rider else host


def exchange_many(name, riders):
    both = Riders(riders)

    def body(*refs):
        ins, outs, sems = refs[:both.n], refs[both.n:2 * both.n], refs[2 * both.n:]
        both.start(ins, outs, sems)
        both.wait(ins, outs, sems)

    res = list(pl.pallas_call(
        body, name=name, out_shape=tuple(both.out_shape), in_specs=both.in_specs, out_specs=tuple(both.out_specs),
        scratch_shapes=both.scratch,
    )(*both.arrs))
    split = []
    for r in riders:
        split.append(res[:r.n])
        res = res[r.n:]
    return split


def exchange(name, arrs, mode, scatter=False):
    rider = Rider(arrs, mode, scatter)

    def body(*refs):
        ins, outs, sems = refs[:rider.n], refs[rider.n:2 * rider.n], refs[2 * rider.n:]
        rider.start(ins, outs, sems)
        rider.wait(ins, outs, sems)

    return pl.pallas_call(
        body, name=name, out_shape=tuple(rider.out_shape), in_specs=rider.in_specs, out_specs=tuple(rider.out_specs),
        scratch_shapes=rider.scratch,
    )(*arrs)


_DIMS = {"nn": (((1,), (0,)), ((), ())), "nt": (((1,), (1,)), ((), ())), "tn": (((0,), (0,)), ((), ()))}


def matmul(name, pairs, kind, *, a_ch=False, b_ch=False, out_ch=False, out_dtype=F32, rows=None, row_off=0, tm=512,
           rider=None, post=None, fold=False, place=None):
    a0, b0 = pairs[0]
    n_chunk = a0.shape[0] if a_ch else (b0.shape[0] if b_ch else 1)
    total_rows = a0.shape[-2]
    rows = total_rows - row_off if rows is None else rows
    tm = min(tm, rows)
    assert rows % tm == 0 and row_off % tm == 0, (name, rows, tm, row_off)
    n_rt, off = rows // tm, row_off // tm
    dims = _DIMS[kind]
    n_pair = len(pairs)

    if kind == "tn":
        grid, red_axis, n_red = (n_chunk, n_rt), 1, n_rt
        a_idx = (lambda k, i: (k, i + off, 0)) if a_ch else (lambda k, i: (i + off, 0))
        b_idx = (lambda k, i: (k, i + off, 0)) if b_ch else (lambda k, i: (i + off, 0))
        a_blk = lambda a: ((None, tm, a.shape[-1]) if a_ch else (tm, a.shape[-1]))
        b_blk = lambda b: ((None, tm, b.shape[-1]) if b_ch else (tm, b.shape[-1]))
        o2 = (a0.shape[-1], b0.shape[-1])
        out_shape = ((n_chunk,) + o2) if out_ch else o2
        out_spec = pl.BlockSpec((None,) + o2, lambda k, i: (k, 0, 0)) if out_ch else pl.BlockSpec(o2, lambda k, i: (0, 0))
        acc_shape = o2
    else:
        n_out = b0.shape[-1] if kind == "nn" else b0.shape[-2]
        b2 = b0.shape[-2:]
        if a_ch and b_ch and not out_ch and fold:
            grid, red_axis, n_red = (n_rt,), None, 1
            a_idx, b_idx = (lambda i: (0, i + off, 0)), (lambda i: (0, 0, 0))
            a_blk = lambda a: (n_chunk, tm, a.shape[-1])
            b_blk = lambda b: tuple(b.shape)
            out_shape, out_spec = (rows, n_out), pl.BlockSpec((tm, n_out), lambda i: (i, 0))
        elif a_ch and b_ch and not out_ch:
            grid, red_axis, n_red = (n_rt, n_chunk), 1, n_chunk
            a_idx, b_idx = (lambda i, k: (k, i + off, 0)), (lambda i, k: (k, 0, 0))
            a_blk = lambda a: (None, tm, a.shape[-1])
            b_blk = lambda b: (None,) + tuple(b.shape[-2:])
            out_shape, out_spec = (rows, n_out), pl.BlockSpec((tm, n_out), lambda i, k: (i, 0))
        elif out_ch and fold:
            assert b_ch and not a_ch and n_pair == 1
            grid, red_axis, n_red = (n_rt,), None, 1
            a_idx, b_idx = (lambda i: (i + off, 0)), (lambda i: (0, 0, 0))
            a_blk = lambda a: (tm, a.shape[-1])
            b_blk = lambda b: tuple(b.shape)
            out_shape, out_spec = (n_chunk, rows, n_out), pl.BlockSpec((n_chunk, tm, n_out), lambda i: (0, i, 0))
        elif out_ch:
            assert b_ch and not a_ch
            grid, red_axis, n_red = (n_chunk, n_rt), None, 1
            a_idx, b_idx = (lambda k, i: (i + off, 0)), (lambda k, i: (k, 0, 0))
            a_blk = lambda a: (tm, a.shape[-1])
            b_blk = lambda b: (None,) + tuple(b.shape[-2:])
            out_shape, out_spec = (n_chunk, rows, n_out), pl.BlockSpec((None, tm, n_out), lambda k, i: (k, i, 0))
        else:
            assert not (a_ch or b_ch)
            grid, red_axis, n_red = (n_rt,), None, 1
            a_idx, b_idx = (lambda i: (i + off, 0)), (lambda i: (0, 0))
            a_blk = lambda a: (tm, a.shape[-1])
            b_blk = lambda b: tuple(b.shape)
            out_shape, out_spec = (rows, n_out), pl.BlockSpec((tm, n_out), lambda i: (i, 0))
            if place is not None:
                out_shape, o_off = (place[0], n_out), place[1] // tm
                out_spec = pl.BlockSpec((tm, n_out), lambda i: (i + o_off, 0))
        acc_shape = (tm, n_out)

    into = [] if place is None or place[2] is None else [place[2]]
    post_ins, post_fn, out_dtypes = ([], None, [out_dtype]) if post is None else post
    hosted = _Hosted(rider, 2 * n_pair + len(post_ins) + len(into), len(out_dtypes), int(n_red > 1), grid)

    def body(*refs):
        ins, outs, scr = hosted.split(refs)

        def compute():
            acc = None
            for p in range(n_pair):
                for k in ([None] if not fold else range(n_chunk)):
                    pick = (lambda r: r[...]) if k is None else (lambda r: r[k])
                    d = lax.dot_general(pick(ins[2 * p]).astype(BF16), pick(ins[2 * p + 1]).astype(BF16), dims,
                                        preferred_element_type=F32)
                    acc = d if acc is None else acc + d
            return acc

        def emit(acc):
            vals = (acc,) if post_fn is None else post_fn(
                acc, *[r[...].astype(F32) for r in ins[2 * n_pair:2 * n_pair + len(post_ins)]])
            for o_ref, v in zip(outs, vals):
                o_ref[...] = v.astype(o_ref.dtype)

        if out_ch and fold:
            a_tile = ins[0][...].astype(BF16)
            for k in range(n_chunk):
                acc = lax.dot_general(a_tile, ins[1][k].astype(BF16), dims, preferred_element_type=F32)
                vals = (acc,) if post_fn is None else post_fn(acc, *[r[k].astype(F32) for r in ins[2:2 + len(post_ins)]])
                for o_ref, v in zip(outs, vals):
                    o_ref[k] = v.astype(o_ref.dtype)
        elif n_red == 1:
            emit(compute())
        else:
            acc_ref = scr[0]
            r = pl.program_id(red_axis)

            @pl.when(r == 0)
            def _():
                acc_ref[...] = jnp.zeros_like(acc_ref)

            acc_ref[...] += compute()

            @pl.when(r == n_red - 1)
            def _():
                emit(acc_ref[...])
        hosted.finish()

    in_specs, args, vmem = [], [], 0
    for a, b in pairs:
        in_specs += [pl.BlockSpec(a_blk(a), a_idx), pl.BlockSpec(b_blk(b), b_idx)]
        args += [a, b]
        vmem += 2 * (_nbytes([s for s in a_blk(a) if s], a.dtype) + _nbytes([s for s in b_blk(b) if s], b.dtype))
    in_specs += [out_spec] * len(post_ins)
    args += list(post_ins)
    aliases = {len(args): 0} if into else {}
    in_specs += [pl.BlockSpec(memory_space=pl.ANY)] * len(into)
    args += into
    tiles_per_step = n_chunk if (out_ch and fold) else 1
    vmem += (3 + 2 * n_pair + tiles_per_step * (len(post_ins) + len(out_dtypes))) * _nbytes(acc_shape, F32)
    scratch = [pltpu.VMEM(acc_shape, F32)] if n_red > 1 else []
    in_specs, out_shapes, out_specs, scratch, args = hosted.call_args(
        in_specs, [_big(out_shape, dt) for dt in out_dtypes], [out_spec] * len(out_dtypes), scratch, args)
    return hosted.results(pl.pallas_call(
        body, name=name, out_shape=out_shapes, grid=grid, in_specs=in_specs, out_specs=out_specs,
        input_output_aliases=aliases, scratch_shapes=scratch, compiler_params=_params(vmem + (8 << 20), len(grid)),
    )(*_in_hbm(args)))


def row(arr, width=None, cb=0, roff=0):
    return (arr, arr.shape[-1] if width is None else width, cb, roff)


def two_rows(first, second, limit):
    return (first, first.shape[-1], 0, 0, (second, limit))


def _row_inputs(rows, tm):
    specs, arrs, slots = [], [], []
    for d in rows:
        second, limit = d[4] if len(d) > 4 else (None, None)
        slots.append((len(arrs), limit))
        specs.append(_row_spec(d[:4], tm, limit))
        arrs.append(d[0])
        if second is not None:
            specs.append(pl.BlockSpec((tm, d[1]), lambda i, limit=limit: (jnp.maximum(i - limit, 0), 0)))
            arrs.append(second)

    def read(refs, i):
        vals = []
        for at, limit in slots:
            v = refs[at][...].astype(F32)
            vals.append(v if limit is None else jnp.where(i < limit, v, refs[at + 1][...].astype(F32)))
        return vals

    return specs, arrs, read


def _row_spec(desc, tm, limit=None):
    _, width, cb, roff = desc[:4]
    if limit is None:
        return pl.BlockSpec((tm, width), lambda i: (i + roff, cb))
    return pl.BlockSpec((tm, width), lambda i: (jnp.minimum(i, limit - 1) + roff, cb))


def _segmenter(tm, seq_len, n_lat):
    seg = lambda i: jnp.where(i * tm < n_lat, (i * tm) // seq_len, n_lat // seq_len)
    first = lambda i: jnp.where(i * tm < n_lat, (i * tm) % seq_len == 0, i * tm == n_lat)
    return seg, first


def rowwise(name, fn, rows, segs, params, outs, *, tm, n_tiles, seg_fn=None, rider=None):
    row_specs, row_arrs, read_rows = _row_inputs(rows, tm)
    n_r, n_s, n_p = len(row_arrs), len(segs), len(params)
    hosted = _Hosted(rider, n_r + n_s + n_p, len(outs), 0, (n_tiles,))

    def body(*refs):
        ins, out_refs, _ = hosted.split(refs)
        vals = read_rows(ins[:n_r], pl.program_id(0)) + [r[...] for r in ins[n_r:]]
        res = fn(*vals)
        for o_ref, v in zip(out_refs, res):
            o_ref[...] = v.astype(o_ref.dtype)
        hosted.finish()

    in_specs = list(row_specs)
    in_specs += [pl.BlockSpec((None, 1, s.shape[-1]), lambda i: (seg_fn(i), 0, 0)) for s in segs]
    in_specs += [pl.BlockSpec(p.shape, lambda i: (0, 0)) for p in params]
    vmem = sum(2 * tm * d[1] * 4 for d in rows) + sum(3 * tm * w * 4 for _, w, _ in outs) + sum(2 * p.size * 4 for p in params)
    in_specs, out_shapes, out_specs, scratch, args = hosted.call_args(
        in_specs, [_big((r, w), dt) for r, w, dt in outs],
        [pl.BlockSpec((tm, w), lambda i: (i, 0)) for _, w, _ in outs], [], row_arrs + list(segs) + list(params))
    return hosted.results(pl.pallas_call(
        body, name=name, grid=(n_tiles,), in_specs=in_specs, out_shape=out_shapes, out_specs=out_specs,
        scratch_shapes=scratch, compiler_params=_params(2 * vmem + (8 << 20)),
    )(*_in_hbm(args)), unwrap=False)


def rowwise_bwd(name, fn, rows, segs, params, cts, row_grads, *, tm, n_tiles, seg_fn=None, first_fn=None, adds=None,
                rider=None):
    adds = adds or {}
    need = [k for k, v in enumerate(row_grads) if v is not None]
    row_specs, row_arrs, read_rows = _row_inputs(rows, tm)
    n_r, n_s, n_p = len(row_arrs), len(segs), len(params)
    n_ct = sum(len(lst) for lst in cts)
    add_keys = sorted(adds)
    hosted = _Hosted(rider, n_r + n_s + n_p + n_ct + len(add_keys), len(need) + n_s + n_p, 0, (n_tiles,))

    def body(*refs):
        host_in, host_out, _ = hosted.split(refs)
        it = iter(list(host_in) + list(host_out))
        row_refs = [next(it) for _ in range(n_r)]
        seg_refs = [next(it) for _ in range(n_s)]
        par_refs = [next(it) for _ in range(n_p)]
        ct_refs = [[next(it) for _ in lst] for lst in cts]
        add_refs = {k: next(it) for k in add_keys}
        rg_refs = {k: next(it) for k in need}
        sg_refs = [next(it) for _ in range(n_s)]
        pg_refs = [next(it) for _ in range(n_p)]
        i = pl.program_id(0)
        rv = read_rows(row_refs, i)
        sv = [r[...] for r in seg_refs]
        pv = [r[...] for r in par_refs]

        def f(*args):
            rr = list(rv)
            for j, k in enumerate(need):
                rr[k] = args[j]
            return fn(*rr, *args[len(need):])

        _, vjp = jax.vjp(f, *[rv[k] for k in need], *sv, *pv)
        ctv = []
        for lst in ct_refs:
            acc = lst[0][...].astype(F32)
            for r in lst[1:]:
                acc = acc + r[...].astype(F32)
            ctv.append(acc)
        g = vjp(tuple(ctv))
        for j, k in enumerate(need):
            gv = g[j]
            if k in adds:
                lim = adds[k][1]
                av = add_refs[k][...].astype(F32)
                gv = gv + (av if lim is None else jnp.where(i < lim, av, 0.0))
            lim = row_grads[k][2]
            if lim is None:
                rg_refs[k][...] = gv.astype(rg_refs[k].dtype)
            else:
                @pl.when(i < lim)
                def _(gv=gv, k=k):
                    rg_refs[k][...] = gv.astype(rg_refs[k].dtype)
        if n_s:
            opens = first_fn(i)
            for ref, gv in zip(sg_refs, g[len(need):len(need) + n_s]):
                @pl.when(opens)
                def _(ref=ref, gv=gv):
                    ref[...] = gv

                @pl.when(jnp.logical_not(opens))
                def _(ref=ref, gv=gv):
                    ref[...] += gv
        for ref, gv in zip(pg_refs, g[len(need) + n_s:]):
            @pl.when(i == 0)
            def _(ref=ref, gv=gv):
                ref[...] = gv

            @pl.when(i > 0)
            def _(ref=ref, gv=gv):
                ref[...] += gv
        hosted.finish()

    seg_spec = lambda s: pl.BlockSpec((None, 1, s.shape[-1]), lambda i: (seg_fn(i), 0, 0))
    par_spec = lambda p: pl.BlockSpec(p.shape, lambda i: (0, 0))
    in_specs = list(row_specs) + [seg_spec(s) for s in segs] + [par_spec(p) for p in params]
    args = row_arrs + list(segs) + list(params)
    for lst in cts:
        in_specs += [_row_spec(d, tm) for d in lst]
        args += [d[0] for d in lst]
    for k in add_keys:
        in_specs.append(_row_spec(adds[k][0], tm, adds[k][1]))
        args.append(adds[k][0][0])
    out_shape, out_specs = [], []
    for k in need:
        n_rows, dt, lim = row_grads[k]
        out_shape.append(_big((n_rows, rows[k][1]), dt))
        out_specs.append(_row_spec((None, rows[k][1], 0, 0), tm, lim))
    for s in segs:
        out_shape.append(jax.ShapeDtypeStruct(s.shape, F32))
        out_specs.append(seg_spec(s))
    for p in params:
        out_shape.append(jax.ShapeDtypeStruct(p.shape, F32))
        out_specs.append(par_spec(p))
    vmem = sum(tm * d[1] * 4 for d in rows) * 6 + n_ct * tm * max(d[1] for d in rows) * 8
    in_specs, out_shape, out_specs, scratch, args = hosted.call_args(in_specs, out_shape, out_specs, [], args)
    return hosted.results(pl.pallas_call(
        body, name=name, grid=(n_tiles,), in_specs=in_specs, out_shape=out_shape, out_specs=out_specs,
        scratch_shapes=scratch, compiler_params=_params(vmem + (8 << 20)),
    )(*_in_hbm(args)), unwrap=False)


def _silu(v):
    return v * jax.nn.sigmoid(v)


def _rms(v, w):
    return v * lax.rsqrt(jnp.mean(v * v, axis=-1, keepdims=True) + EPS) * w


def fn_norm_mod(x, shift, scale, w):
    return (_rms(x, w) * (1.0 + scale) + shift,)


def fn_act(g, u):
    return (_silu(g) * u,)


def make_fn_resid(coef):
    def fn(x, f, gate):
        return (x + coef * gate * f,)
    return fn


def fn_silu_bias(v, b):
    return (_silu(v + b),)


def make_fn_gate_groupnorm(width):
    half = width // 2

    def fn(y_both, z, w):
        y = y_both * _silu(z)
        lane = lax.broadcasted_iota(jnp.int32, y.shape, 1)
        lo = lane < half
        sq = y * y
        s_lo = jnp.sum(jnp.where(lo, sq, 0.0), axis=-1, keepdims=True)
        s_hi = jnp.sum(jnp.where(lo, 0.0, sq), axis=-1, keepdims=True)
        r = jnp.where(lo, lax.rsqrt(s_lo / half + EPS), lax.rsqrt(s_hi / half + EPS))
        return (y * r * w,)
    return fn


def fn_glu(a, b):
    return (a * jax.nn.sigmoid(b),)


def fn_ln_silu(vw, vh, cb, lw, lb):
    v = jnp.concatenate([vw, vh], axis=-1) + cb
    mu = jnp.mean(v, axis=-1, keepdims=True)
    var = jnp.mean(jnp.square(v - mu), axis=-1, keepdims=True)
    return (_silu((v - mu) * lax.rsqrt(var + EPS) * lw + lb),)


def _col_tile(width):
    return width // 3 if width % (3 * LANES) == 0 else width


def mod_fwd(a_rows, w_shard, b_shard):
    n, d = a_rows.shape
    ws = w_shard.shape[1]
    tn = _col_tile(ws)

    def body(a_ref, w_ref, b_ref, o_ref):
        a = _silu(a_ref[...]).astype(BF16)
        o_ref[...] = jnp.dot(a, w_ref[...].astype(BF16), preferred_element_type=F32) + b_ref[...]

    return pl.pallas_call(
        body, name="mod_fwd", grid=(ws // tn,), out_shape=jax.ShapeDtypeStruct((n, ws), F32),
        in_specs=[pl.BlockSpec((n, d), lambda j: (0, 0)), pl.BlockSpec((d, tn), lambda j: (0, j)),
                  pl.BlockSpec((1, tn), lambda j: (0, j))],
        out_specs=pl.BlockSpec((n, tn), lambda j: (0, j)), compiler_params=_params(),
    )(a_rows, w_shard, b_shard)


def mod_bwd(a_rows, d_shard, d_full, w_shard, ctx_rows):
    n, d = a_rows.shape
    ws = w_shard.shape[1]
    tn = _col_tile(ws)
    n_ct = ws // tn

    def body(a_ref, ds_ref, df_ref, w_ref, gw_ref, gb_ref, q_ref):
        j = pl.program_id(0)
        a = _silu(a_ref[...])
        ds = ds_ref[...]
        gw_ref[...] = lax.dot_general(a, ds, _DIMS["tn"], precision=HI, preferred_element_type=F32)
        dctx = ds[ctx_rows[0]:ctx_rows[0] + 1, :]
        for r in ctx_rows[1:]:
            dctx = dctx + ds[r:r + 1, :]
        q = lax.dot_general(jnp.broadcast_to(dctx, (8, tn)), w_ref[...], _DIMS["nt"], precision=HI,
                            preferred_element_type=F32)

        @pl.when(j == 0)
        def _():
            q_ref[...] = q
            df = df_ref[...]
            acc = df[0:1, :]
            for r in range(1, n):
                acc = acc + df[r:r + 1, :]
            gb_ref[...] = acc

        @pl.when(j > 0)
        def _():
            q_ref[...] += q

    return pl.pallas_call(
        body, name="mod_bwd", grid=(n_ct,),
        out_shape=(jax.ShapeDtypeStruct((d, ws), F32), jax.ShapeDtypeStruct((1, d_full.shape[1]), F32),
                   jax.ShapeDtypeStruct((8, d), F32)),
        in_specs=[pl.BlockSpec((n, d), lambda j: (0, 0)), pl.BlockSpec((n, tn), lambda j: (0, j)),
                  pl.BlockSpec(d_full.shape, lambda j: (0, 0)), pl.BlockSpec((d, tn), lambda j: (0, j))],
        out_specs=(pl.BlockSpec((d, tn), lambda j: (0, j)), pl.BlockSpec((1, d_full.shape[1]), lambda j: (0, 0)),
                   pl.BlockSpec((8, d), lambda j: (0, 0))),
        compiler_params=_params(40 << 20),
    )(a_rows, d_shard, d_full, w_shard)


def _shifted(xs, d, tok, width):
    if d == 0:
        return xs
    n = xs.shape[0]
    sh = pltpu.roll(xs, (-d) % n, axis=0)
    return jnp.where((tok + d >= 0) & (tok + d < width), sh, 0.0)


def _placed(out_shape, place):
    if place is None:
        return out_shape, 0, 0, None
    return place


def tapsum_roll(name, x, xcb, w, wcb, *, seq_len, n_seq, row_blk_off, width, piece, cb, ncb, pad, flip, place=None):
    n_tap = w.shape[0]
    n_piece = seq_len // piece
    out_shape, o_rb, o_cb, into = _placed((n_seq * seq_len, ncb * cb), place)

    def body(x_ref, w_ref, *rest):
        o_ref = rest[-1]
        wv = w_ref[...]
        tok = lax.broadcasted_iota(jnp.int32, (piece, 1), 0) % width

        def do_piece(p, carry):
            start = pl.multiple_of(p * piece, piece)
            xs = x_ref[pl.ds(start, piece), :]
            acc = jnp.zeros_like(xs)
            for k in range(n_tap):
                d = pad - k if flip else k - pad
                acc = acc + wv[k:k + 1, :] * _shifted(xs, d, tok, width)
            o_ref[pl.ds(start, piece), :] = acc
            return carry

        lax.fori_loop(0, n_piece, do_piece, 0)

    extra = [] if into is None else [into]
    return pl.pallas_call(
        body, name=name, grid=(ncb, n_seq), out_shape=_big(out_shape, F32),
        in_specs=[pl.BlockSpec((seq_len, cb), lambda j, s: (row_blk_off + s, xcb + j)),
                  pl.BlockSpec((n_tap, cb), lambda j, s: (0, wcb + j))] + [pl.BlockSpec(memory_space=pl.ANY)] * len(extra),
        out_specs=pl.BlockSpec((seq_len, cb), lambda j, s: (o_rb + s, o_cb + j)),
        input_output_aliases={2: 0} if extra else {},
        compiler_params=_params(8 * seq_len * cb * 4 + (8 << 20), 2),
    )(*_in_hbm([x, w] + extra))


def tapgrad_roll(name, dy, dycb, dy_blk_off, x, xcb, x_blk_off, *, n_tap, seq_len, n_seq, width, piece, cb, ncb, pad):
    n_piece = seq_len // piece

    def body(dy_ref, x_ref, o_ref):
        @pl.when(pl.program_id(1) == 0)
        def _():
            o_ref[...] = jnp.zeros_like(o_ref)

        tok = lax.broadcasted_iota(jnp.int32, (piece, 1), 0) % width

        def do_piece(p, carry):
            start = pl.multiple_of(p * piece, piece)
            xs = x_ref[pl.ds(start, piece), :]
            dv = dy_ref[pl.ds(start, piece), :]
            for k in range(n_tap):
                o_ref[k:k + 1, :] += jnp.sum(dv * _shifted(xs, k - pad, tok, width), axis=0, keepdims=True)
            return carry

        lax.fori_loop(0, n_piece, do_piece, 0)

    return pl.pallas_call(
        body, name=name, grid=(ncb, n_seq), out_shape=jax.ShapeDtypeStruct((n_tap, ncb * cb), F32),
        in_specs=[pl.BlockSpec((seq_len, cb), lambda j, s: (dy_blk_off + s, dycb + j)),
                  pl.BlockSpec((seq_len, cb), lambda j, s: (x_blk_off + s, xcb + j))],
        out_specs=pl.BlockSpec((n_tap, cb), lambda j, s: (0, j)),
        compiler_params=_params(8 * seq_len * cb * 4 + (8 << 20), 2),
    )(*_in_hbm([dy, x]))


def tapsum_rows(name, x, xcb, w, wcb, *, seq_len, n_seq, cb, ncb, pad, flip, place=None):
    n_tap = w.shape[0]
    n_row = seq_len // GRID_W
    halo = pad * GRID_W
    out_shape, o_rb, o_cb, into = _placed((n_seq * seq_len, ncb * cb), place)

    def body(x_ref, w_ref, *rest):
        o_ref, xp = rest[-2:]
        xp[pl.ds(0, halo), :] = jnp.zeros((halo, cb), F32)
        xp[pl.ds(halo + seq_len, halo), :] = jnp.zeros((halo, cb), F32)
        xp[pl.ds(halo, seq_len), :] = x_ref[...]
        wv = w_ref[...]

        def do_row(r, carry):
            acc = jnp.zeros((GRID_W, cb), F32)
            for k in range(n_tap):
                d = pad - k if flip else k - pad
                acc = acc + wv[k:k + 1, :] * xp[pl.ds(pl.multiple_of((r + pad + d) * GRID_W, GRID_W), GRID_W), :]
            o_ref[pl.ds(pl.multiple_of(r * GRID_W, GRID_W), GRID_W), :] = acc
            return carry

        lax.fori_loop(0, n_row, do_row, 0)

    extra = [] if into is None else [into]
    return pl.pallas_call(
        body, name=name, grid=(ncb, n_seq), out_shape=_big(out_shape, F32),
        in_specs=[pl.BlockSpec((seq_len, cb), lambda j, s: (s, xcb + j)),
                  pl.BlockSpec((n_tap, cb), lambda j, s: (0, wcb + j))] + [pl.BlockSpec(memory_space=pl.ANY)] * len(extra),
        out_specs=pl.BlockSpec((seq_len, cb), lambda j, s: (o_rb + s, o_cb + j)),
        input_output_aliases={2: 0} if extra else {},
        scratch_shapes=[pltpu.VMEM((seq_len + 2 * halo, cb), F32)],
        compiler_params=_params(10 * seq_len * cb * 4 + (8 << 20), 2),
    )(*_in_hbm([x, w] + extra))


def tapgrad_rows(name, dy, dycb, x, xcb, *, n_tap, seq_len, n_seq, cb, ncb, pad):
    n_row = seq_len // GRID_W
    halo = pad * GRID_W

    def body(dy_ref, x_ref, o_ref, xp):
        @pl.when(pl.program_id(1) == 0)
        def _():
            o_ref[...] = jnp.zeros_like(o_ref)

        xp[pl.ds(0, halo), :] = jnp.zeros((halo, cb), F32)
        xp[pl.ds(halo + seq_len, halo), :] = jnp.zeros((halo, cb), F32)
        xp[pl.ds(halo, seq_len), :] = x_ref[...]

        def do_row(r, carry):
            dv = dy_ref[pl.ds(pl.multiple_of(r * GRID_W, GRID_W), GRID_W), :]
            for k in range(n_tap):
                xs = xp[pl.ds(pl.multiple_of((r + k) * GRID_W, GRID_W), GRID_W), :]
                o_ref[k:k + 1, :] += jnp.sum(dv * xs, axis=0, keepdims=True)
            return carry

        lax.fori_loop(0, n_row, do_row, 0)

    return pl.pallas_call(
        body, name=name, grid=(ncb, n_seq), out_shape=jax.ShapeDtypeStruct((n_tap, ncb * cb), F32),
        in_specs=[pl.BlockSpec((seq_len, cb), lambda j, s: (s, dycb + j)),
                  pl.BlockSpec((seq_len, cb), lambda j, s: (s, xcb + j))],
        out_specs=pl.BlockSpec((n_tap, cb), lambda j, s: (0, j)),
        scratch_shapes=[pltpu.VMEM((seq_len + 2 * halo, cb), F32)],
        compiler_params=_params(10 * seq_len * cb * 4 + (8 << 20), 2),
    )(*_in_hbm([dy, x]))


def _ssd_blocks(b, s, *, rev, n_ctx, n_lat, lat_blocks):
    if rev:
        return jnp.where(s < n_ctx, lat_blocks + b * n_ctx + (n_ctx - 1 - s), b * n_lat + (n_lat - 1 - (s - n_ctx)))
    return jnp.where(s < n_ctx, lat_blocks + b * n_ctx + s, b * n_lat + (s - n_ctx))


def _ssd_common(xbc, raw, dtb, alog, dsk, *, rev, ds, n_head):
    if rev:
        raw = pltpu.roll(raw, LANES - n_head, axis=1)
    pre = raw + dtb
    dt = jnp.maximum(pre, 0.0) + jnp.log1p(jnp.exp(-jnp.abs(pre)))
    sig = jax.nn.sigmoid(pre)
    a = -jnp.exp(alog)
    da = dt * a
    ri = lax.broadcasted_iota(jnp.int32, (CHUNK, CHUNK), 0)
    ci = lax.broadcasted_iota(jnp.int32, (CHUNK, CHUNK), 1)
    mask = (ci >= ri) if rev else (ci <= ri)
    tri = mask.astype(F32)
    tri_t = ((ci <= ri) if rev else (ci >= ri)).astype(F32)
    cs = jnp.dot(tri, da, precision=HI, preferred_element_type=F32)
    tot = jnp.sum(da, axis=0, keepdims=True)
    def wide(v):
        first = lax.broadcasted_iota(jnp.int32, (v.shape[0], LANES), 1) < HEAD_DIM
        return jnp.concatenate(
            [jnp.where(first, jnp.broadcast_to(v[:, 2 * p:2 * p + 1], first.shape),
                       jnp.broadcast_to(v[:, 2 * p + 1:2 * p + 2], first.shape)) for p in range(n_head // 2)], axis=1)

    cs_w, tot_w = wide(cs), wide(tot)
    xh = xbc[:, :ds]
    dt_w = wide(dt)
    return dict(
        dt=dt, sig=sig, a=a, cs=cs, cs_t=cs.T, tot=tot, mask=mask, tri_t=tri_t,
        e_w=jnp.exp(cs_w), wt_w=jnp.exp(tot_w - cs_w), dec_w=jnp.exp(tot_w), dt_w=dt_w, dsk_w=wide(dsk),
        xh=xh, xs_w=xh * dt_w, bm=xbc[:, ds:ds + 2 * N_STATE], cm=xbc[:, ds + 2 * N_STATE:ds + 4 * N_STATE])


def _decay(q, col):
    seg = q["cs"][:, col:col + 1] - q["cs_t"][col:col + 1, :]
    return jnp.exp(jnp.where(q["mask"], seg, -jnp.inf))


def _split_heads(v):
    lane = lax.broadcasted_iota(jnp.int32, v.shape, 1)
    return jnp.concatenate([jnp.where(lane < HEAD_DIM, v, 0.0), jnp.where(lane >= HEAD_DIM, v, 0.0)], axis=0)


def ssd_fwd(name, xbc, proj, dt_cb, dtb, alog, dsk, *, rev, n_ex, seq_len, ctx_len, ds, rider=None, add=None):
    n_head, half = ds // HEAD_DIM, ds // 2
    n_ctx, n_lat = ctx_len // CHUNK, seq_len // CHUNK
    n_step = n_ctx + n_lat
    blk = functools.partial(_ssd_blocks, rev=rev, n_ctx=n_ctx, n_lat=n_lat, lat_blocks=n_ex * n_lat)
    xw = xbc.shape[1]

    def y_blk(b, s):
        sl = jnp.maximum(s, n_ctx) - n_ctx
        return b * n_lat + ((n_lat - 1 - sl) if rev else sl)

    hosted = _Hosted(rider, 5 + (add is not None), 2, 1, (n_ex, n_step))

    def body(*refs):
        (xbc_ref, dt_ref, dtb_ref, alog_ref, dsk_ref, *add_ref), (y_ref, hs_ref), (h_scr,) = hosted.split(refs)

        @pl.when(pl.program_id(1) == 0)
        def _():
            h_scr[...] = jnp.zeros_like(h_scr)

        q = _ssd_common(xbc_ref[...], dt_ref[...], dtb_ref[...], alog_ref[...], dsk_ref[...], rev=rev, ds=ds, n_head=n_head)
        h = h_scr[...]
        hs_ref[...] = h
        for g in range(2):
            lo = g * half
            bg = q["bm"][:, g * N_STATE:(g + 1) * N_STATE].astype(BF16)
            cg = q["cm"][:, g * N_STATE:(g + 1) * N_STATE].astype(BF16)
            scores = lax.dot_general(cg, bg, _DIMS["nt"], preferred_element_type=F32)
            hg = h[:, lo:lo + half]
            off = jnp.dot(cg, hg.astype(BF16), preferred_element_type=F32)
            for j in range(half // LANES):
                c0 = (lo + j * LANES) // HEAD_DIM
                ln = slice(lo + j * LANES, lo + (j + 1) * LANES)
                p_cat = jnp.concatenate([scores * _decay(q, c0), scores * _decay(q, c0 + 1)], axis=1).astype(BF16)
                diag = jnp.dot(p_cat, _split_heads(q["xs_w"][:, ln]).astype(BF16), preferred_element_type=F32)
                y_ref[:, ln] = (diag + q["e_w"][:, ln] * off[:, j * LANES:(j + 1) * LANES]
                                + q["dsk_w"][:, ln] * q["xh"][:, ln] + (add_ref[0][:, ln] if add_ref else 0.0))
            v = (q["wt_w"][:, lo:lo + half] * q["xs_w"][:, lo:lo + half]).astype(BF16)
            h_scr[:, lo:lo + half] = (q["dec_w"][:, lo:lo + half] * hg
                                      + lax.dot_general(bg, v, _DIMS["tn"], preferred_element_type=F32))
        hosted.finish()

    vec = pl.BlockSpec((1, LANES), lambda b, s: (0, 0))
    in_specs, out_shape, out_specs, scratch, args = hosted.call_args(
        [pl.BlockSpec((CHUNK, xw), lambda b, s: (blk(b, s), 0)),
         pl.BlockSpec((CHUNK, LANES), lambda b, s: (blk(b, s), dt_cb)), vec, vec, vec]
        + [pl.BlockSpec((CHUNK, ds), lambda b, s: (y_blk(b, s), 0))] * (add is not None),
        (_big((n_ex * seq_len, ds), F32), _big((n_ex, n_step, N_STATE, ds), F32)),
        (pl.BlockSpec((CHUNK, ds), lambda b, s: (y_blk(b, s), 0)),
         pl.BlockSpec((None, None, N_STATE, ds), lambda b, s: (b, s, 0, 0))),
        [pltpu.VMEM((N_STATE, ds), F32)], [xbc, proj, dtb, alog, dsk] + ([] if add is None else [add]))
    return hosted.results(pl.pallas_call(
        body, name=name, grid=(n_ex, n_step), out_shape=out_shape, in_specs=in_specs, out_specs=out_specs,
        scratch_shapes=scratch, compiler_params=_params(40 << 20, 2),
    )(*_in_hbm(args)))


def ssd_bwd(name, xbc, proj, dt_cb, hs, dy, dtb, alog, dsk, *, rev, n_ex, seq_len, ctx_len, ds, rider=None, add=None):
    n_head, half = ds // HEAD_DIM, ds // 2
    n_ctx, n_lat = ctx_len // CHUNK, seq_len // CHUNK
    n_step = n_ctx + n_lat
    n_tok = n_ex * (seq_len + ctx_len)
    blk0 = functools.partial(_ssd_blocks, rev=rev, n_ctx=n_ctx, n_lat=n_lat, lat_blocks=n_ex * n_lat)
    step = lambda sp: n_step - 1 - sp
    blk = lambda b, sp: blk0(b, step(sp))
    xw = xbc.shape[1]

    def dy_blk(b, sp):
        sl = jnp.maximum(step(sp), n_ctx) - n_ctx
        return b * n_lat + ((n_lat - 1 - sl) if rev else sl)

    hosted = _Hosted(rider, 7 + (add is not None), 5, 1, (n_ex, n_step))

    def body(*refs):
        ((xbc_ref, dt_ref, hs_ref, dy_ref, dtb_ref, alog_ref, dsk_ref, *add_ref),
         (dxbc_ref, ddt_ref, dalog_ref, ddtb_ref, ddsk_ref), (dh_scr,)) = hosted.split(refs)
        b, sp = pl.program_id(0), pl.program_id(1)
        more = (lambda cols: add_ref[0][:, cols]) if add_ref else (lambda cols: 0.0)

        @pl.when(sp == 0)
        def _():
            dh_scr[...] = jnp.zeros_like(dh_scr)

        @pl.when((sp == 0) & (b == 0))
        def _():
            dalog_ref[...] = jnp.zeros_like(dalog_ref)
            ddtb_ref[...] = jnp.zeros_like(ddtb_ref)
            ddsk_ref[...] = jnp.zeros_like(ddsk_ref)

        q = _ssd_common(xbc_ref[...], dt_ref[...], dtb_ref[...], alog_ref[...], dsk_ref[...], rev=rev, ds=ds, n_head=n_head)
        h = hs_ref[...]
        d_y = jnp.where(step(sp) >= n_ctx, dy_ref[...], 0.0)
        dh_next = dh_scr[...]
        lane_row = lax.broadcasted_iota(jnp.int32, (1, LANES), 1)
        d_cs = jnp.zeros((CHUNK, LANES), F32)
        dxs_parts, de_parts, dwt_parts, ddec_parts = [], [], [], []
        for g in range(2):
            lo = g * half
            gs = slice(lo, lo + half)
            bg = q["bm"][:, g * N_STATE:(g + 1) * N_STATE].astype(BF16)
            cg = q["cm"][:, g * N_STATE:(g + 1) * N_STATE].astype(BF16)
            scores = lax.dot_general(cg, bg, _DIMS["nt"], preferred_element_type=F32)
            hg, dyg, dhn = h[:, gs], d_y[:, gs], dh_next[:, gs]
            off = jnp.dot(cg, hg.astype(BF16), preferred_element_type=F32)
            d_off = (q["e_w"][:, gs] * dyg).astype(BF16)
            de_parts.append(dyg * off)
            d_c = lax.dot_general(d_off, hg.astype(BF16), _DIMS["nt"], preferred_element_type=F32)
            dh_scr[:, gs] = (lax.dot_general(cg, d_off, _DIMS["tn"], preferred_element_type=F32)
                             + q["dec_w"][:, gs] * dhn)
            b_dh = jnp.dot(bg, dhn.astype(BF16), preferred_element_type=F32)
            v = q["wt_w"][:, gs] * q["xs_w"][:, gs]
            d_b = lax.dot_general(v.astype(BF16), dhn.astype(BF16), _DIMS["nt"], preferred_element_type=F32)
            dwt_parts.append(q["xs_w"][:, gs] * b_dh)
            ddec_parts.append(jnp.sum(hg * dhn, axis=0, keepdims=True))
            d_scores = jnp.zeros((CHUNK, CHUNK), F32)
            for j in range(half // LANES):
                c0 = (lo + j * LANES) // HEAD_DIM
                ln = slice(lo + j * LANES, lo + (j + 1) * LANES)
                l0, l1 = _decay(q, c0), _decay(q, c0 + 1)
                p0, p1 = scores * l0, scores * l1
                dy_st = _split_heads(d_y[:, ln]).astype(BF16)
                d_p = lax.dot_general(dy_st, q["xs_w"][:, ln].astype(BF16), _DIMS["nt"], preferred_element_type=F32)
                d_p0, d_p1 = d_p[:CHUNK], d_p[CHUNK:]
                d_scores = d_scores + d_p0 * l0 + d_p1 * l1
                for col, t in ((c0, d_p0 * p0), (c0 + 1, d_p1 * p1)):
                    d_cs = d_cs + jnp.sum(t - t.T, axis=1, keepdims=True) * (lane_row == col).astype(F32)
                p_st = jnp.concatenate([p0, p1], axis=0).astype(BF16)
                dxs_parts.append(lax.dot_general(p_st, dy_st, _DIMS["tn"], preferred_element_type=F32)
                                 + q["wt_w"][:, ln] * b_dh[:, j * LANES:(j + 1) * LANES])
            d_sc = d_scores.astype(BF16)
            d_c = d_c + jnp.dot(d_sc, bg, preferred_element_type=F32)
            d_b = d_b + lax.dot_general(d_sc, cg, _DIMS["tn"], preferred_element_type=F32)
            b_cols, c_cols = slice(ds + g * N_STATE, ds + (g + 1) * N_STATE), slice(ds + (2 + g) * N_STATE, ds + (3 + g) * N_STATE)
            dxbc_ref[:, b_cols] = d_b + more(b_cols)
            dxbc_ref[:, c_cols] = d_c + more(c_cols)
        d_xs = jnp.concatenate(dxs_parts, axis=1)
        narrow_m = (lax.broadcasted_iota(jnp.int32, (ds, LANES), 0) // HEAD_DIM
                    == lax.broadcasted_iota(jnp.int32, (ds, LANES), 1)).astype(BF16)
        rows8 = lambda v: jnp.broadcast_to(v, (8, ds))
        stacked = jnp.concatenate(
            [jnp.concatenate(dwt_parts, axis=1), jnp.concatenate(de_parts, axis=1), d_xs * q["xh"],
             rows8(jnp.concatenate(ddec_parts, axis=1)), rows8(jnp.sum(d_y * q["xh"], axis=0, keepdims=True))], axis=0)
        hi = stacked.astype(BF16)
        lo = (stacked - hi.astype(F32)).astype(BF16)
        sums = (jnp.dot(hi, narrow_m, preferred_element_type=F32) + jnp.dot(lo, narrow_m, preferred_element_type=F32))
        n_wt, n_e, n_xs = sums[:CHUNK], sums[CHUNK:2 * CHUNK], sums[2 * CHUNK:3 * CHUNK]
        n_dec, n_dsk = sums[3 * CHUNK:3 * CHUNK + 1], sums[3 * CHUNK + 8:3 * CHUNK + 9]
        e, wt, dec = jnp.exp(q["cs"]), jnp.exp(q["tot"] - q["cs"]), jnp.exp(q["tot"])
        d_wt = n_wt * wt
        d_cs = d_cs + n_e * e - d_wt
        d_tot = jnp.sum(d_wt, axis=0, keepdims=True) + n_dec * dec
        d_da = jnp.dot(q["tri_t"], d_cs, precision=HI, preferred_element_type=F32) + d_tot
        d_dt = d_da * q["a"] + n_xs
        dxbc_ref[:, :ds] = d_xs * q["dt_w"] + q["dsk_w"] * d_y + more(slice(0, ds))
        dalog_ref[...] += jnp.sum(d_da * q["dt"], axis=0, keepdims=True) * q["a"]
        d_raw = d_dt * q["sig"]
        ddtb_ref[...] += jnp.sum(d_raw, axis=0, keepdims=True)
        ddsk_ref[...] += n_dsk
        ddt_ref[...] = pltpu.roll(d_raw, n_head, axis=1) if rev else d_raw
        hosted.finish()

    vec = pl.BlockSpec((1, LANES), lambda b, s: (0, 0))
    vec_shape = jax.ShapeDtypeStruct((1, LANES), F32)
    in_specs, out_shape, out_specs, scratch, args = hosted.call_args(
        [pl.BlockSpec((CHUNK, xw), lambda b, s: (blk(b, s), 0)),
         pl.BlockSpec((CHUNK, LANES), lambda b, s: (blk(b, s), dt_cb)),
         pl.BlockSpec((None, None, N_STATE, ds), lambda b, s: (b, step(s), 0, 0)),
         pl.BlockSpec((CHUNK, ds), lambda b, s: (dy_blk(b, s), 0)), vec, vec, vec]
        + [pl.BlockSpec((CHUNK, xw), lambda b, s: (blk(b, s), 0))] * (add is not None),
        (_big((n_tok, xw), F32), _big((n_tok, LANES), F32), vec_shape, vec_shape, vec_shape),
        (pl.BlockSpec((CHUNK, xw), lambda b, s: (blk(b, s), 0)),
         pl.BlockSpec((CHUNK, LANES), lambda b, s: (blk(b, s), 0)), vec, vec, vec),
        [pltpu.VMEM((N_STATE, ds), F32)], [xbc, proj, hs, dy, dtb, alog, dsk] + ([] if add is None else [add]))
    return hosted.results(pl.pallas_call(
        body, name=name, grid=(n_ex, n_step), out_shape=out_shape, in_specs=in_specs, out_specs=out_specs,
        scratch_shapes=scratch, compiler_params=_params(48 << 20, 2),
    )(*_in_hbm(args)))


def final_loss(x3, target, w, *, tm):
    n, d = x3.shape

    def body(x_ref, t_ref, w_ref, dx_ref, dw_ref, loss_ref):
        i = pl.program_id(0)
        t = t_ref[...]

        def per_feature(xv, wv):
            err = _rms(xv, wv) - t
            return 0.5 * jnp.sum(err * err, axis=0, keepdims=True) / d

        lv, vjp = jax.vjp(per_feature, x_ref[...], w_ref[...])
        dx, dw = vjp(jnp.ones_like(lv))
        dx_ref[...] = dx

        @pl.when(i == 0)
        def _():
            dw_ref[...] = dw
            loss_ref[...] = lv

        @pl.when(i > 0)
        def _():
            dw_ref[...] += dw
            loss_ref[...] += lv

    tile = pl.BlockSpec((tm, d), lambda i: (i, 0))
    vec = pl.BlockSpec((1, d), lambda i: (0, 0))
    return pl.pallas_call(
        body, name="final_loss", grid=(n // tm,), in_specs=[tile, tile, vec],
        out_shape=(jax.ShapeDtypeStruct((n, d), F32), jax.ShapeDtypeStruct((1, d), F32), jax.ShapeDtypeStruct((1, d), F32)),
        out_specs=(tile, vec, vec), compiler_params=_params(tm * d * 4 * 16 + (8 << 20)),
    )(x3, target, w)


def sum_slots(name, arr):
    n_slot, n_row, width = arr.shape
    tm = _row_tile(n_row, width * n_slot, mult=16)

    def body(a_ref, o_ref):
        acc = a_ref[0].astype(F32)
        for j in range(1, n_slot):
            acc = acc + a_ref[j].astype(F32)
        o_ref[...] = acc

    return pl.pallas_call(
        body, name=name, grid=(n_row // tm,), out_shape=jax.ShapeDtypeStruct((n_row, width), F32),
        in_specs=[pl.BlockSpec((n_slot, tm, width), lambda i: (0, i, 0))],
        out_specs=pl.BlockSpec((tm, width), lambda i: (i, 0)), compiler_params=_params(),
    )(arr)


def adamw(name, w, g_slots, m, v):
    n_slot, n_row, width = g_slots.shape
    tm = _row_tile(n_row, width * 2)

    def body(w_ref, g_ref, m_ref, v_ref, go_ref, d_ref, mo_ref, vo_ref):
        g = g_ref[0]
        for j in range(1, n_slot):
            g = g + g_ref[j]
        m2 = ADAM_B1 * m_ref[...] + (1.0 - ADAM_B1) * g
        v2 = ADAM_B2 * v_ref[...] + (1.0 - ADAM_B2) * jnp.square(g)
        m_hat = m2 / (1.0 - ADAM_B1 ** ADAM_STEP)
        v_hat = v2 / (1.0 - ADAM_B2 ** ADAM_STEP)
        go_ref[...] = g
        d_ref[...] = -ADAM_LR * (m_hat / (jnp.sqrt(v_hat) + ADAM_EPS) + ADAM_WD * w_ref[...])
        mo_ref[...] = m2
        vo_ref[...] = v2

    tile = pl.BlockSpec((tm, width), lambda i: (i, 0))
    shape = jax.ShapeDtypeStruct((n_row, width), F32)
    return pl.pallas_call(
        body, name=name, grid=(n_row // tm,), out_shape=(shape,) * 4,
        in_specs=[tile, pl.BlockSpec((n_slot, tm, width), lambda i: (0, i, 0)), tile, tile],
        out_specs=(tile,) * 4, compiler_params=_params(),
    )(w, g_slots, m, v)


def cctx_grad(q_all, c_ctx_row):
    d = c_ctx_row.shape[1]

    def body(q_ref, c_ref, o_ref):
        acc = q_ref[0, 0:1, :]
        for j in (2, 4, 6):
            acc = acc + q_ref[j, 0:1, :]
        _, vjp = jax.vjp(_silu, c_ref[...])
        o_ref[...] = vjp(acc)[0]

    return pl.pallas_call(
        body, name="cctx_grad", out_shape=jax.ShapeDtypeStruct((1, d), F32),
    )(q_all, c_ctx_row)


def loss_total(pack_sum, d):
    def body(p_ref, o_ref):
        o_ref[...] = jnp.sum(p_ref[:, 0:d], axis=1, keepdims=True)

    return pl.pallas_call(
        body, name="loss_total", out_shape=jax.ShapeDtypeStruct((1, 1), F32),
    )(pack_sum)


class _Plan:
    def __init__(self):
        self.builders, self.got = {}, {}

    def on(self, host, key, builder):
        self.builders.setdefault(host, []).append((key, builder))

    def run(self, host, fn, *args, **kw):
        if host not in self.builders:
            return fn(host, *args, **kw)
        keys, riders = zip(*[(key, builder(self)) for key, builder in self.builders[host]])
        res, landed = fn(host, *args, rider=Riders(riders), **kw)
        for key, r in zip(keys, riders):
            self.got[key], landed = landed[:r.n], landed[r.n:]
        return res


def _val(w):
    return w() if callable(w) else w


def _matmul_tile(n_rows, tm):
    return 2 * tm if n_rows % (2 * tm) == 0 else tm


def _ffn_fwd(plan, tag, xin, n_rows, tm, seg_fn, shift, scale, gate, norm_w, wg, wu, wd):
    d = xin[1]
    n_tiles = n_rows // tm
    (h,) = plan.run(f"{tag}_norm", rowwise, fn_norm_mod, [xin], [shift, scale], [norm_w], [(n_rows, d, BF16)],
                    tm=tm, n_tiles=n_tiles, seg_fn=seg_fn)
    tmm = _matmul_tile(n_rows, tm)
    g = plan.run(f"{tag}_gate", matmul, [(h, _val(wg))], "nn", out_dtype=BF16, b_ch=True, out_ch=True, tm=tmm)
    u, act = plan.run(f"{tag}_up", matmul, [(h, _val(wu))], "nn", b_ch=True, out_ch=True, tm=tm, fold=True,
                      post=([g], lambda acc, gv: (acc, fn_act(gv, acc)[0]), [BF16, BF16]))
    f = plan.run(f"{tag}_down", matmul, [(act, _val(wd))], "nn", a_ch=True, b_ch=True, tm=tmm, fold=True)
    (xo,) = plan.run(f"{tag}_resid", rowwise, make_fn_resid(0.5), [xin, row(f)], [gate], [], [(n_rows, d, F32)],
                     tm=tm, n_tiles=n_tiles, seg_fn=seg_fn)
    return xo, (h, g, u, act, f)


def _ffn_bwd(plan, tag, d_xo, saved, xin, n_rows, tm, seg_fn, first_fn, shift, scale, gate, norm_w, wg, wu, wd, dx_rows, dx_limit):
    h, g, u, act, f = saved
    d = xin[1]
    n_tiles = n_rows // tm
    n_ch, _, n_hid = g.shape
    d_f, d_gate = plan.run(f"{tag}_resid_bwd", rowwise_bwd, make_fn_resid(0.5), [xin, row(f)], [gate], [], [[row(d_xo)]],
                           [None, (n_rows, BF16, None)], tm=tm, n_tiles=n_tiles, seg_fn=seg_fn, first_fn=first_fn)
    tmm = _matmul_tile(n_rows, tm)
    def act_vjp(d_act, gv, uv):
        s = jax.nn.sigmoid(gv)
        gs = gv * s
        return d_act * uv * (s + gs * (1.0 - s)), d_act * gs
    d_g, d_u = plan.run(f"{tag}_down_dx", matmul, [(d_f, wd)], "nt", b_ch=True, out_ch=True, tm=tmm,
                        post=([g, u], act_vjp, [BF16, BF16]))
    plan.got[f"{tag}_d_wd"] = plan.run(f"{tag}_down_dw", matmul, [(act, d_f)], "tn", out_dtype=BF16, a_ch=True, out_ch=True, tm=tmm)
    d_h = plan.run(f"{tag}_up_dx", matmul, [(d_g, wg), (d_u, wu)], "nt", a_ch=True, b_ch=True, tm=tmm)
    plan.got[f"{tag}_d_wg"] = plan.run(f"{tag}_gate_dw", matmul, [(d_g, h)], "tn", out_dtype=BF16, a_ch=True, out_ch=True, tm=tmm)
    plan.got[f"{tag}_d_wu"] = plan.run(f"{tag}_up_dw", matmul, [(d_u, h)], "tn", out_dtype=BF16, a_ch=True, out_ch=True, tm=tmm)
    d_x, d_shift, d_scale, d_nw = plan.run(
        f"{tag}_norm_bwd", rowwise_bwd, fn_norm_mod, [xin], [shift, scale], [norm_w], [[row(d_h)]], [(dx_rows, F32, dx_limit)],
        tm=tm, n_tiles=n_tiles, seg_fn=seg_fn, first_fn=first_fn, adds={0: (row(d_xo), None)})
    return d_x, (d_shift, d_scale, d_gate), d_nw


def kernel(x, c, ctx, c_ctx, w_mod, b_mod, norm_ffn1, ffn1_gate, ffn1_up, ffn1_down, norm_mix, w_in, ssm_conv_w, ssm_conv_b, dt_bias_fwd, dt_bias_bwd, a_log_fwd, a_log_bwd, ssm_d, ssm_norm_w, cconv_w, cconv_b, cconv_ln_w, cconv_ln_b, w_out, norm_ffn2, ffn2_gate, ffn2_up, ffn2_down, final_norm, loss_target, m_c_ctx, m_w_mod, m_b_mod, m_norm_ffn1, m_ffn1_gate, m_ffn1_up, m_ffn1_down, m_norm_mix, m_w_in, m_ssm_conv_w, m_ssm_conv_b, m_dt_bias_fwd, m_dt_bias_bwd, m_a_log_fwd, m_a_log_bwd, m_ssm_d, m_ssm_norm_w, m_cconv_w, m_cconv_b, m_cconv_ln_w, m_cconv_ln_b, m_w_out, m_norm_ffn2, m_ffn2_gate, m_ffn2_up, m_ffn2_down, m_final_norm, v_c_ctx, v_w_mod, v_b_mod, v_norm_ffn1, v_ffn1_gate, v_ffn1_up, v_ffn1_down, v_norm_mix, v_w_in, v_ssm_conv_w, v_ssm_conv_b, v_dt_bias_fwd, v_dt_bias_bwd, v_a_log_fwd, v_a_log_bwd, v_ssm_d, v_ssm_norm_w, v_cconv_w, v_cconv_b, v_cconv_ln_w, v_cconv_ln_b, v_w_out, v_norm_ffn2, v_ffn2_gate, v_ffn2_up, v_ffn2_down, v_final_norm):
    weights = dict(c_ctx=c_ctx, w_mod=w_mod, b_mod=b_mod, norm_ffn1=norm_ffn1, ffn1_gate=ffn1_gate, ffn1_up=ffn1_up, ffn1_down=ffn1_down, norm_mix=norm_mix, w_in=w_in, ssm_conv_w=ssm_conv_w, ssm_conv_b=ssm_conv_b, dt_bias_fwd=dt_bias_fwd, dt_bias_bwd=dt_bias_bwd, a_log_fwd=a_log_fwd, a_log_bwd=a_log_bwd, ssm_d=ssm_d, ssm_norm_w=ssm_norm_w, cconv_w=cconv_w, cconv_b=cconv_b, cconv_ln_w=cconv_ln_w, cconv_ln_b=cconv_ln_b, w_out=w_out, norm_ffn2=norm_ffn2, ffn2_gate=ffn2_gate, ffn2_up=ffn2_up, ffn2_down=ffn2_down, final_norm=final_norm)
    mom1 = dict(c_ctx=m_c_ctx, w_mod=m_w_mod, b_mod=m_b_mod, norm_ffn1=m_norm_ffn1, ffn1_gate=m_ffn1_gate, ffn1_up=m_ffn1_up, ffn1_down=m_ffn1_down, norm_mix=m_norm_mix, w_in=m_w_in, ssm_conv_w=m_ssm_conv_w, ssm_conv_b=m_ssm_conv_b, dt_bias_fwd=m_dt_bias_fwd, dt_bias_bwd=m_dt_bias_bwd, a_log_fwd=m_a_log_fwd, a_log_bwd=m_a_log_bwd, ssm_d=m_ssm_d, ssm_norm_w=m_ssm_norm_w, cconv_w=m_cconv_w, cconv_b=m_cconv_b, cconv_ln_w=m_cconv_ln_w, cconv_ln_b=m_cconv_ln_b, w_out=m_w_out, norm_ffn2=m_norm_ffn2, ffn2_gate=m_ffn2_gate, ffn2_up=m_ffn2_up, ffn2_down=m_ffn2_down, final_norm=m_final_norm)
    mom2 = dict(c_ctx=v_c_ctx, w_mod=v_w_mod, b_mod=v_b_mod, norm_ffn1=v_norm_ffn1, ffn1_gate=v_ffn1_gate, ffn1_up=v_ffn1_up, ffn1_down=v_ffn1_down, norm_mix=v_norm_mix, w_in=v_w_in, ssm_conv_w=v_ssm_conv_w, ssm_conv_b=v_ssm_conv_b, dt_bias_fwd=v_dt_bias_fwd, dt_bias_bwd=v_dt_bias_bwd, a_log_fwd=v_a_log_fwd, a_log_bwd=v_a_log_bwd, ssm_d=v_ssm_d, ssm_norm_w=v_ssm_norm_w, cconv_w=v_cconv_w, cconv_b=v_cconv_b, cconv_ln_w=v_cconv_ln_w, cconv_ln_b=v_cconv_ln_b, w_out=v_w_out, norm_ffn2=v_norm_ffn2, ffn2_gate=v_ffn2_gate, ffn2_up=v_ffn2_up, ffn2_down=v_ffn2_down, final_norm=v_final_norm)
    order = list(weights)

    n_ex, seq_len, d = x.shape
    ctx_len = ctx.shape[1]
    ds = d
    n_head = ds // HEAD_DIM
    xw = ds + 4 * N_STATE
    n_lat, n_ctx_rows = n_ex * seq_len, n_ex * ctx_len
    n_tok = n_lat + n_ctx_rows
    tm = math.gcd(math.gcd(512, seq_len), n_ctx_rows)
    seg_all, first_all = _segmenter(tm, seq_len, n_lat)
    lat_tiles = n_lat // tm

    xi, yi, ci = lax.axis_index("x"), lax.axis_index("y"), lax.axis_index("c")
    me, chip = 4 * xi + 2 * yi + ci, 2 * xi + yi

    (c_all,) = exchange("gather_c", [c], "all8")
    n_all = 8 * n_ex
    n_cond = -(-(n_all + 1) // 8) * 8
    cond = jnp.concatenate([c_all.reshape(n_all, d), c_ctx[None, :], jnp.zeros((n_cond - n_all - 1, d), F32)])
    mod_w = w_mod.shape[2]
    b_shard = lax.dynamic_slice(b_mod, (0, chip * mod_w), (1, mod_w))
    (mod_g,) = exchange("gather_mod", [mod_fwd(cond, w_mod[0], b_shard)], "chips")
    mod_full = mod_g.transpose(1, 0, 2).reshape(n_cond, N_CHIPS * mod_w)
    mod_mine = lax.dynamic_slice(mod_full, (me * n_ex, 0), (n_ex, 9 * d)).reshape(n_ex, 9, d)
    mod_ctx = mod_full[n_all].reshape(9, d)
    tabs = [jnp.concatenate([mod_mine[:, j], mod_ctx[j][None]])[:, None, :] for j in range(9)]
    lat = lambda t: t[:n_ex]

    bf = lambda w: w[0].astype(BF16)
    plan = _Plan()
    gather = lambda *ws: (lambda p: Rider(list(ws), "chips"))
    plan.on("ffn1_norm", "wg1", gather(bf(ffn1_gate)))
    plan.on("ffn1_gate", "wu1", gather(bf(ffn1_up)))
    plan.on("ffn1_up", "wd1", gather(bf(ffn1_down)))
    win_cut = d * 5 // 8
    plan.on("ffn1_down", "win_a", gather(bf(w_in)[:win_cut]))
    plan.on("ffn1_resid", "win_b", gather(bf(w_in)[win_cut:], ssm_conv_w[0], cconv_w[0]))
    xt = two_rows(x.reshape(n_lat, d), ctx.reshape(n_ctx_rows, d), lat_tiles)
    x1, saved1 = _ffn_fwd(plan, "ffn1", xt, n_tok, tm, seg_all, tabs[0], tabs[1], tabs[2], norm_ffn1,
                          lambda: plan.got["wg1"][0], lambda: plan.got["wu1"][0], lambda: plan.got["wd1"][0])
    (wg1,), (wu1,), (wd1,), (win_a,), (win_b, w5_g, w31_g) = (plan.got[k] for k in ("wg1", "wu1", "wd1", "win_a", "win_b"))
    win_g = jnp.concatenate([win_a, win_b], axis=1)
    unshard_cols = lambda t: t.transpose(1, 0, 2).reshape(t.shape[1], N_CHIPS * t.shape[2])
    win = unshard_cols(win_g)
    o_x, o_dt, o_glu = ds, ds + xw, ds + xw + 2 * n_head
    w_z, w_xbc, w_dt = win[:, :ds], win[:, o_x:o_dt], win[:, o_dt:o_glu]
    w_ga, w_gb = win[:, o_glu:o_glu + d], win[:, o_glu + d:]
    w_dtp = jnp.concatenate([w_dt, jnp.zeros((d, LANES - 2 * n_head), BF16)], axis=1)
    w_cat = jnp.concatenate([w_z, w_ga, w_gb, w_xbc, w_dtp], axis=1)
    cbw = d // 2
    xbc_cb, dt_cb = 3 * d // cbw, (3 * d + xw) // LANES
    w5, w31 = unshard_cols(w5_g), unshard_cols(w31_g)
    pad_vec = lambda v: jnp.concatenate([v.reshape(1, -1), jnp.zeros((1, LANES - v.size), F32)], axis=1)
    dtb_f, dtb_b, alog_f, alog_b = map(pad_vec, (dt_bias_fwd, dt_bias_bwd, a_log_fwd, a_log_bwd))
    dsk_f, dsk_b = pad_vec(ssm_d), jnp.zeros((1, LANES), F32)

    (h2,) = rowwise("mix_norm", fn_norm_mod, [row(x1)], [tabs[3], tabs[4]], [norm_mix], [(n_tok, d, BF16)],
                    tm=tm, n_tiles=n_tok // tm, seg_fn=seg_all)
    proj, (wg2,) = matmul("mix_proj", [(h2, w_cat)], "nn", tm=min(tm, 256), rider=Rider([bf(ffn2_gate)], "chips"))
    def conv5(name, src, cb0, flip):
        out = None
        for part, seq, off in (("lat", seq_len, 0), ("ctx", ctx_len, n_lat // ctx_len)):
            out = tapsum_roll(f"{name}_{part}", src, cb0, w5, 0, seq_len=seq, n_seq=n_ex, row_blk_off=off, width=seq,
                              piece=seq, cb=cbw, ncb=xw // cbw, pad=w5.shape[0] // 2, flip=flip,
                              place=((n_tok, xw), off, 0, out))
        return out

    craw = conv5("xbc_conv", proj, xbc_cb, False)
    (xbc,) = rowwise("xbc_silu", fn_silu_bias, [row(craw)], [], [ssm_conv_b], [(n_tok, xw, F32)], tm=tm, n_tiles=n_tok // tm)
    ssd = dict(n_ex=n_ex, seq_len=seq_len, ctx_len=ctx_len, ds=ds)
    (y_f, hs_f), (wu2, wd2) = ssd_fwd("ssd_fwd_f", xbc, proj, dt_cb, dtb_f, alog_f, dsk_f, rev=False,
                                      rider=Rider([bf(ffn2_up), bf(ffn2_down)], "chips"), **ssd)
    (y_b, hs_b), (wout_g,) = ssd_fwd("ssd_fwd_b", xbc, proj, dt_cb, dtb_b, alog_b, dsk_b, rev=True,
                                     rider=Rider([bf(w_out)], "chips"), add=y_f, **ssd)
    wout = wout_g.reshape(2 * d, d)
    wo_y, wo_u = wout[:ds], wout[ds:]
    fn_gate = make_fn_gate_groupnorm(ds)
    (yn,) = rowwise("ssd_gate", fn_gate, [row(y_b), row(proj, d, 0)], [], [ssm_norm_w], [(n_lat, ds, BF16)],
                    tm=tm, n_tiles=lat_tiles)
    (u0,) = rowwise("glu", fn_glu, [row(proj, d, 1), row(proj, d, 2)], [], [], [(n_lat, d, F32)], tm=tm, n_tiles=lat_tiles)
    cb31 = max(LANES, d // 4)
    ncb31 = (d // 2) // cb31
    pad31 = w31.shape[0] // 2
    piece31 = min(seq_len, 4 * GRID_W)
    v_w = tapsum_roll("cconv_cols", u0, 0, w31, 0, seq_len=seq_len, n_seq=n_ex, row_blk_off=0, width=GRID_W,
                      piece=piece31, cb=cb31, ncb=ncb31, pad=pad31, flip=False)
    v_h = tapsum_rows("cconv_rows", u0, ncb31, w31, ncb31, seq_len=seq_len, n_seq=n_ex, cb=cb31, ncb=ncb31, pad=pad31, flip=False)
    (un,) = rowwise("cconv_ln", fn_ln_silu, [row(v_w), row(v_h)], [], [cconv_b, cconv_ln_w, cconv_ln_b], [(n_lat, d, BF16)],
                    tm=tm, n_tiles=lat_tiles)
    mix = matmul("mix_out", [(yn, wo_y), (un, wo_u)], "nn", tm=tm)
    seg_lat, first_lat = _segmenter(tm, seq_len, n_lat)
    (x2,) = rowwise("mix_resid", make_fn_resid(1.0), [row(x1), row(mix)], [lat(tabs[5])], [], [(n_lat, d, F32)],
                    tm=tm, n_tiles=lat_tiles, seg_fn=seg_lat)
    x3, saved2 = _ffn_fwd(plan, "ffn2", row(x2), n_lat, tm, seg_lat, lat(tabs[6]), lat(tabs[7]), lat(tabs[8]), norm_ffn2, wg2, wu2, wd2)
    d_x3, d_final, loss_vec = final_loss(x3, loss_target.reshape(n_lat, d), final_norm.reshape(1, d), tm=tm)

    shard_cols = lambda t: t.reshape(t.shape[0], N_CHIPS, -1).transpose(1, 0, 2)

    def pieces(t):
        t = jnp.pad(t, ((0, 0), (0, -t.shape[1] % 32), (0, 0)))
        return t.reshape(2 * N_CHIPS, t.shape[1] // 2, t.shape[2]).astype(BF16)

    scatter = lambda *ts: Rider([pieces(t) for t in ts], "all8", scatter=True)
    halves = lambda names, landed: Rider([sum_slots(f"sum_{nm}", r) for nm, r in zip(names, landed)], "sibling")
    swapped = {}
    plan.on("ffn2_up_dx", "sc_ffn2_down", lambda p: scatter(p.got["ffn2_d_wd"]))
    plan.on("ffn2_up_dw", "sc_ffn2_gate", lambda p: scatter(p.got["ffn2_d_wg"]))
    d_x2, (d_s6, d_s7, d_g8), d_nffn2 = _ffn_bwd(
        plan, "ffn2", d_x3, saved2, row(x2), n_lat, tm, seg_lat, first_lat, lat(tabs[6]), lat(tabs[7]), lat(tabs[8]), norm_ffn2,
        wg2, wu2, wd2, n_lat, None)
    d_mix, d_g5 = rowwise_bwd("mix_resid_bwd", make_fn_resid(1.0), [row(x1), row(mix)], [lat(tabs[5])], [], [[row(d_x2)]],
                              [None, (n_lat, BF16, None)], tm=tm, n_tiles=lat_tiles, seg_fn=seg_lat, first_fn=first_lat)
    d_yn = matmul("mix_out_dy", [(d_mix, wo_y)], "nt", tm=tm)
    d_un = matmul("mix_out_du", [(d_mix, wo_u)], "nt", tm=tm)
    d_wout = jnp.concatenate([matmul("mix_out_dwy", [(yn, d_mix)], "tn", out_dtype=BF16, tm=tm),
                              matmul("mix_out_dwu", [(un, d_mix)], "tn", out_dtype=BF16, tm=tm)])
    d_vw, d_vh, d_cb, d_lnw, d_lnb = rowwise_bwd(
        "cconv_ln_bwd", fn_ln_silu, [row(v_w), row(v_h)], [], [cconv_b, cconv_ln_w, cconv_ln_b], [[row(d_un)]],
        [(n_lat, F32, None)] * 2, tm=tm, n_tiles=lat_tiles)
    d_u0 = tapsum_roll("cconv_cols_dx", d_vw, 0, w31, 0, seq_len=seq_len, n_seq=n_ex, row_blk_off=0, width=GRID_W,
                       piece=piece31, cb=cb31, ncb=ncb31, pad=pad31, flip=True, place=((n_lat, d), 0, 0, None))
    d_u0 = tapsum_rows("cconv_rows_dx", d_vh, 0, w31, ncb31, seq_len=seq_len, n_seq=n_ex, cb=cb31, ncb=ncb31, pad=pad31,
                       flip=True, place=((n_lat, d), 0, ncb31, d_u0))
    d_w31 = jnp.concatenate([
        tapgrad_roll("cconv_cols_dw", d_vw, 0, 0, u0, 0, 0, n_tap=w31.shape[0], seq_len=seq_len, n_seq=n_ex, width=GRID_W,
                     piece=piece31, cb=cb31, ncb=ncb31, pad=pad31),
        tapgrad_rows("cconv_rows_dw", d_vh, 0, u0, ncb31, n_tap=w31.shape[0], seq_len=seq_len, n_seq=n_ex, cb=cb31,
                     ncb=ncb31, pad=pad31)], axis=1)
    d_ga, d_gb = rowwise_bwd("glu_bwd", fn_glu, [row(proj, d, 1), row(proj, d, 2)], [], [], [[row(d_u0)]],
                             [(n_lat, BF16, None)] * 2, tm=tm, n_tiles=lat_tiles)
    d_ysum, d_z, d_ssmnw = rowwise_bwd(
        "ssd_gate_bwd", fn_gate, [row(y_b), row(proj, d, 0)], [], [ssm_norm_w], [[row(d_yn)]],
        [(n_lat, F32, None), (n_lat, BF16, None)], tm=tm, n_tiles=lat_tiles)
    (dxbc_f, ddt_f, dalog_f, ddtb_f, ddsk), landed = ssd_bwd(
        "ssd_bwd_f", xbc, proj, dt_cb, hs_f, d_ysum, dtb_f, alog_f, dsk_f, rev=False,
        rider=scatter(plan.got["ffn2_d_wu"], d_wout.reshape(N_CHIPS, -1, d)), **ssd)
    (dxbc_b, ddt_b, dalog_b, ddtb_b, _), both = ssd_bwd(
        "ssd_bwd_b", xbc, proj, dt_cb, hs_b, d_ysum, dtb_b, alog_b, dsk_b, rev=True,
        rider=halves(["ffn2_down", "ffn2_gate"], plan.got["sc_ffn2_down"] + plan.got["sc_ffn2_gate"]), add=dxbc_f, **ssd)
    swapped.update(zip(["ffn2_down", "ffn2_gate"], both))
    (d_craw, d_conv_b), both = rowwise_bwd(
        "xbc_silu_bwd", fn_silu_bias, [row(craw)], [], [ssm_conv_b], [[row(dxbc_b)]],
        [(n_tok, F32, None)], tm=tm, n_tiles=n_tok // tm, rider=halves(["ffn2_up", "w_out"], landed))
    swapped.update(zip(["ffn2_up", "w_out"], both))
    d_pxbc = conv5("xbc_conv_dx", d_craw, 0, True)
    g5 = lambda name, seq, off: tapgrad_roll(name, d_craw, 0, off, proj, xbc_cb, off, n_tap=w5.shape[0], seq_len=seq,
                                             n_seq=n_ex, width=seq, piece=seq, cb=cbw, ncb=xw // cbw, pad=w5.shape[0] // 2)
    d_w5 = g5("xbc_conv_lat_dw", seq_len, 0) + g5("xbc_conv_ctx_dw", ctx_len, n_lat // ctx_len)
    lat_pairs = [(d_z, w_z), (d_ga, w_ga), (d_gb, w_gb), (d_pxbc, w_xbc), (ddt_f, w_dtp), (ddt_b, w_dtp)]
    d_h2 = matmul("mix_proj_dx_lat", lat_pairs, "nt", rows=n_lat, tm=min(tm, 256), place=(n_tok, 0, None))
    d_h2 = matmul("mix_proj_dx_ctx", lat_pairs[3:], "nt", rows=n_ctx_rows, row_off=n_lat, tm=min(tm, 256),
                  place=(n_tok, n_lat, d_h2))
    d_wz = matmul("mix_proj_dwz", [(d_z, h2)], "tn", out_dtype=BF16, rows=n_lat, tm=tm)
    d_wga = matmul("mix_proj_dwa", [(d_ga, h2)], "tn", out_dtype=BF16, rows=n_lat, tm=tm)
    d_wgb = matmul("mix_proj_dwb", [(d_gb, h2)], "tn", out_dtype=BF16, rows=n_lat, tm=tm)
    d_wxbc = matmul("mix_proj_dwx", [(d_pxbc, h2)], "tn", out_dtype=BF16, tm=tm)
    d_wdt = matmul("mix_proj_dwt", [(ddt_f, h2), (ddt_b, h2)], "tn", out_dtype=BF16, tm=tm)
    d_win_t = jnp.concatenate([d_wz, d_wxbc, d_wdt[:2 * n_head], d_wga, d_wgb]).reshape(N_CHIPS, -1, d)
    d_x1, d_s3, d_s4, d_nmix = rowwise_bwd(
        "mix_norm_bwd", fn_norm_mod, [row(x1)], [tabs[3], tabs[4]], [norm_mix], [[row(d_h2)]], [(n_tok, F32, None)],
        tm=tm, n_tiles=n_tok // tm, seg_fn=seg_all, first_fn=first_all, adds={0: (row(d_x2), lat_tiles)})
    mix_names = ["w_in", "ssm_conv_w", "cconv_w"]
    plan.on("ffn1_down_dx", "sc_conv", lambda p: scatter(shard_cols(d_w5), shard_cols(d_w31)))
    plan.on("ffn1_up_dx", "sc_win", lambda p: scatter(d_win_t))
    plan.on("ffn1_gate_dw", "sc_ffn1_down", lambda p: scatter(p.got["ffn1_d_wd"]))
    plan.on("ffn1_up_dw", "sc_ffn1_gate", lambda p: scatter(p.got["ffn1_d_wg"]))
    plan.on("ffn1_up_dw", "sw_mix", lambda p: halves(mix_names, p.got["sc_win"] + p.got["sc_conv"]))
    plan.on("ffn1_norm_bwd", "sc_ffn1_up", lambda p: scatter(p.got["ffn1_d_wu"]))
    plan.on("ffn1_norm_bwd", "sw_ffn1_down", lambda p: halves(["ffn1_down"], p.got["sc_ffn1_down"]))
    d_xt, (d_s0, d_s1, d_g2), d_nffn1 = _ffn_bwd(
        plan, "ffn1", d_x1, saved1, xt, n_tok, tm, seg_all, first_all, tabs[0], tabs[1], tabs[2], norm_ffn1, wg1, wu1, wd1,
        n_lat, lat_tiles)
    swapped.update(zip(mix_names + ["ffn1_down"], plan.got["sw_mix"] + plan.got["sw_ffn1_down"]))
    last_names = ["ffn1_gate", "ffn1_up"]
    last = halves(last_names, plan.got["sc_ffn1_gate"] + plan.got["sc_ffn1_up"])
    grad_x = d_xt.reshape(n_ex, seq_len, d)

    with_ctx0 = lambda t: jnp.concatenate([t, jnp.zeros((1, 1, d), F32)])
    d_tabs = [d_s0, d_s1, d_g2, d_s3, d_s4, with_ctx0(d_g5), with_ctx0(d_s6), with_ctx0(d_s7), with_ctx0(d_g8)]
    d_mod_rows = jnp.concatenate([t[:, 0, :] for t in d_tabs], axis=1)
    n_pad_rows = -(-(n_ex + 1) // 8) * 8
    d_mod_rows = jnp.concatenate([d_mod_rows, jnp.zeros((n_pad_rows - n_ex - 1, 9 * d), F32)])
    small = [("loss", loss_vec), ("norm_ffn1", d_nffn1), ("norm_mix", d_nmix), ("ssm_conv_b", d_conv_b),
             ("dt_bias_fwd", ddtb_f[:, :n_head]), ("dt_bias_bwd", ddtb_b[:, :n_head]), ("a_log_fwd", dalog_f[:, :n_head]),
             ("a_log_bwd", dalog_b[:, :n_head]), ("ssm_d", ddsk[:, :n_head]), ("ssm_norm_w", d_ssmnw), ("cconv_b", d_cb),
             ("cconv_ln_w", d_lnw), ("cconv_ln_b", d_lnb), ("norm_ffn2", d_nffn2), ("final_norm", d_final)]
    n_small = sum(v.size for _, v in small)
    n_pack = -(-n_small // (8 * LANES)) * (8 * LANES)
    pack = jnp.concatenate([v.reshape(-1) for _, v in small] + [jnp.zeros((n_pack - n_small,), F32)]).reshape(-1, LANES)
    (pack_all, d_mod_all), both = exchange_many("gather_small_swap_last", [Rider([pack, d_mod_rows], "all8"), last])
    swapped.update(zip(last_names, both))
    pack_sum = sum_slots("small_sum", pack_all)
    loss = loss_total(pack_sum.reshape(1, n_pack), d).reshape(())
    flat_sum = pack_sum.reshape(-1)
    small_grads, pos = {}, 0
    for nm, v in small:
        small_grads[nm] = flat_sum[pos:pos + v.size]
        pos += v.size
    d_mod_all = d_mod_all.reshape(8 * n_pad_rows, 9 * d)
    cond_rows = [jnp.concatenate([cond[j * n_ex:(j + 1) * n_ex], c_ctx[None, :],
                                  jnp.zeros((n_pad_rows - n_ex - 1, d), F32)]) for j in range(8)]
    cond_bwd = jnp.concatenate(cond_rows)
    d_mod_shard = lax.dynamic_slice(d_mod_all, (0, chip * mod_w), (8 * n_pad_rows, mod_w))
    g_wmod, g_bmod, q_part = mod_bwd(cond_bwd, d_mod_shard, d_mod_all, w_mod[0],
                                     tuple(j * n_pad_rows + n_ex for j in range(8)))
    (q_all,) = exchange("gather_cctx", [q_part], "all8")
    g_cctx = cctx_grad(q_all, c_ctx.reshape(1, d))
    small_grads["c_ctx"], small_grads["b_mod"] = g_cctx.reshape(-1), g_bmod.reshape(-1)

    transposed = {"ffn1_gate", "ffn1_up", "ffn2_gate", "ffn2_up", "w_in"}
    results = {}
    for nm, both in swapped.items():
        flip = (lambda t: jnp.swapaxes(t, 1, 2)) if nm in transposed else (lambda t: t)
        shape = flip(weights[nm]).shape
        two_d = lambda t: flip(t).reshape(shape[-2], shape[-1])
        g_full = both.reshape(1, -1, shape[-1])[:, :shape[-2]]
        results[nm] = [flip(r.reshape(shape)) for r in
                       adamw(f"adamw_{nm}", two_d(weights[nm]), g_full, two_d(mom1[nm]), two_d(mom2[nm]))]
    results["w_mod"] = [r.reshape(w_mod.shape) for r in adamw("adamw_w_mod", w_mod[0], g_wmod[None], m_w_mod[0], v_w_mod[0])]
    small_names = [nm for nm in order if nm not in results]
    n_sm = sum(weights[nm].size for nm in small_names)
    n_smp = -(-n_sm // (8 * LANES)) * (8 * LANES)
    packed = lambda src: jnp.concatenate([src[nm].reshape(-1) for nm in small_names] + [jnp.zeros((n_smp - n_sm,), F32)]).reshape(-1, LANES)
    sm_out = adamw("adamw_small", packed(weights), packed(small_grads)[None], packed(mom1), packed(mom2))
    pos = 0
    for nm in small_names:
        size = weights[nm].size
        results[nm] = [r.reshape(-1)[pos:pos + size].reshape(weights[nm].shape) for r in sm_out]
        pos += size
    return (loss, grad_x, *[results[nm][0] for nm in order], *[results[nm][1] for nm in order],
            *[results[nm][2] for nm in order], *[results[nm][3] for nm in order])
```

```python
import functools
import math

import jax
import jax.numpy as jnp
from jax import lax
from jax.experimental import pallas as pl
from jax.experimental.pallas import tpu as pltpu

F32 = jnp.float32
BF16 = jnp.bfloat16
HI = lax.Precision.HIGHEST
MESH = pl.DeviceIdType.MESH

EPS = 1e-6
GRID_W = 64
HEAD_DIM = 64
N_STATE = 128
CHUNK = 128
LANES = 128
N_CHIPS = 4
ADAM_LR, ADAM_B1, ADAM_B2, ADAM_EPS, ADAM_WD, ADAM_STEP = 0.001, 0.9, 0.999, 1e-08, 0.01, 10
VMEM_CAP = 56 * 1024 * 1024


def _params(vmem_bytes=None, n_axes=1):
    kw = dict(dimension_semantics=("arbitrary",) * n_axes)
    if vmem_bytes is not None:
        kw["vmem_limit_bytes"] = int(min(VMEM_CAP, max(32 * 1024 * 1024, vmem_bytes)))
    return pltpu.CompilerParams(**kw)


def _big(shape, dtype):
    return pltpu.HBM(tuple(shape), dtype)


def _in_hbm(args):
    return [pltpu.with_memory_space_constraint(a, pltpu.HBM) if a.size * a.dtype.itemsize >= (1 << 20) else a for a in args]


def _nbytes(shape, dtype):
    return math.prod(shape) * jnp.dtype(dtype).itemsize


def _row_tile(rows, width, cap_bytes=1 << 20, mult=8):
    best = None
    for t in range(mult, rows + 1, mult):
        if rows % t == 0 and t * width * 4 <= cap_bytes:
            best = t
    return best if best is not None else rows


_MODES = {"all8": (8, (1, 2, 3, 4, 5, 6, 7), 0), "chips": (4, (2, 4, 6), 1), "sibling": (2, (1,), 0)}


class Rider:
    def __init__(self, arrs, mode, scatter=False):
        self.arrs, self.scatter = list(arrs), scatter
        self.nslot, self.deltas, self.shift = _MODES[mode]
        self.n = len(self.arrs)
        self.out_shape = [jax.ShapeDtypeStruct((self.nslot,) + (a.shape[1:] if scatter else a.shape), a.dtype)
                          for a in self.arrs]
        any_spec = pl.BlockSpec(memory_space=pl.ANY)
        self.in_specs = [any_spec] * self.n
        self.out_specs = [any_spec] * self.n
        n_peer = len(self.deltas)
        self.scratch = [pltpu.SemaphoreType.DMA((self.n, n_peer)), pltpu.SemaphoreType.DMA((self.n, n_peer)),
                        pltpu.SemaphoreType.DMA((self.n,))]

    def _copies(self, ins, outs, sems, arrivals):
        send_sems, recv_sems, local_sems = sems
        x, y, c = lax.axis_index("x"), lax.axis_index("y"), lax.axis_index("c")
        me = 4 * x + 2 * y + c
        slot_of = lambda dev: (dev >> self.shift) & (self.nslot - 1)
        src = lambda a, slot: ins[a].at[slot] if self.scatter else ins[a]
        flip = lambda v, bit: 1 - v if bit else v

        def remote(a, k, d, from_slot, to_slot):
            return pltpu.make_async_remote_copy(
                src_ref=src(a, from_slot), dst_ref=outs[a].at[to_slot], send_sem=send_sems.at[a, k],
                recv_sem=recv_sems.at[a, k], device_id=(flip(x, (d >> 2) & 1), flip(y, (d >> 1) & 1), flip(c, d & 1)),
                device_id_type=MESH)

        mine = slot_of(me)
        local = [pltpu.make_async_copy(src(a, mine), outs[a].at[mine], local_sems.at[a]) for a in range(self.n)]
        sends = [remote(a, k, d, slot_of(me ^ d), mine) for k, d in enumerate(self.deltas) for a in range(self.n)]
        if not arrivals:
            return local, sends
        return local, sends, [remote(a, k, d, mine, slot_of(me ^ d)) for k, d in enumerate(self.deltas) for a in range(self.n)]

    def start(self, ins, outs, sems):
        local, sends = self._copies(ins, outs, sems, arrivals=False)
        for cp in local + sends:
            cp.start()

    def wait(self, ins, outs, sems):
        local, sends, recvs = self._copies(ins, outs, sems, arrivals=True)
        for cp in recvs:
            cp.wait_recv()
        for cp in sends:
            cp.wait_send()
        for cp in local:
            cp.wait()


class Riders:
    def __init__(self, riders):
        self.riders = list(riders)
        self.n = sum(r.n for r in self.riders)
        cat = lambda attr: [v for r in self.riders for v in getattr(r, attr)]
        self.arrs, self.out_shape, self.in_specs = cat("arrs"), cat("out_shape"), cat("in_specs")
        self.out_specs, self.scratch = cat("out_specs"), cat("scratch")

    def _each(self, method, ins, outs, sems):
        i = s = 0
        for r in self.riders:
            getattr(r, method)(ins[i:i + r.n], outs[i:i + r.n], sems[s:s + len(r.scratch)])
            i, s = i + r.n, s + len(r.scratch)

    def start(self, ins, outs, sems):
        self._each("start", ins, outs, sems)

    def wait(self, ins, outs, sems):
        self._each("wait", ins, outs, sems)


class _Hosted:
    def __init__(self, rider, n_in, n_out, n_scratch, grid):
        self.rider, self.n_in, self.n_out, self.n_scratch, self.grid = rider, n_in, n_out, n_scratch, grid
        self.n = rider.n if rider else 0

    def split(self, refs):
        a, b = self.n_in, self.n_in + self.n
        c, e = b + self.n_out, b + self.n_out + self.n
        self._r = (refs[a:b], refs[c:e], refs[e + self.n_scratch:])
        if self.rider:
            ids = [pl.program_id(ax) for ax in range(len(self.grid))]
            first = functools.reduce(jnp.logical_and, [i == 0 for i in ids]) if ids else True
            pl.when(first)(lambda: self.rider.start(*self._r))
        return refs[:a], refs[b:c], refs[e:e + self.n_scratch]

    def finish(self):
        if self.rider:
            ids = [pl.program_id(ax) for ax in range(len(self.grid))]
            last = functools.reduce(jnp.logical_and, [i == n - 1 for i, n in zip(ids, self.grid)]) if ids else True
            pl.when(last)(lambda: self.rider.wait(*self._r))

    def call_args(self, in_specs, out_shape, out_specs, scratch, args):
        r = self.rider
        if not r:
            return list(in_specs), tuple(out_shape), tuple(out_specs), list(scratch), list(args)
        return (list(in_specs) + r.in_specs, tuple(out_shape) + tuple(r.out_shape), tuple(out_specs) + tuple(r.out_specs),
                list(scratch) + r.scratch, list(args) + r.arrs)

    def results(self, res, unwrap=True):
        res = list(res) if isinstance(res, (tuple, list)) else [res]
        host = res[:self.n_out]
        host = host[0] if (self.n_out == 1 and unwrap) else tuple(host)
        return (host, res[self.n_out:]) if self.rider else host


def exchange_many(name, riders):
    both = Riders(riders)

    def body(*refs):
        ins, outs, sems = refs[:both.n], refs[both.n:2 * both.n], refs[2 * both.n:]
        both.start(ins, outs, sems)
        both.wait(ins, outs, sems)

    res = list(pl.pallas_call(
        body, name=name, out_shape=tuple(both.out_shape), in_specs=both.in_specs, out_specs=tuple(both.out_specs),
        scratch_shapes=both.scratch,
    )(*both.arrs))
    split = []
    for r in riders:
        split.append(res[:r.n])
        res = res[r.n:]
    return split


def exchange(name, arrs, mode, scatter=False):
    rider = Rider(arrs, mode, scatter)

    def body(*refs):
        ins, outs, sems = refs[:rider.n], refs[rider.n:2 * rider.n], refs[2 * rider.n:]
        rider.start(ins, outs, sems)
        rider.wait(ins, outs, sems)

    return pl.pallas_call(
        body, name=name, out_shape=tuple(rider.out_shape), in_specs=rider.in_specs, out_specs=tuple(rider.out_specs),
        scratch_shapes=rider.scratch,
    )(*arrs)


_DIMS = {"nn": (((1,), (0,)), ((), ())), "nt": (((1,), (1,)), ((), ())), "tn": (((0,), (0,)), ((), ()))}


def matmul(name, pairs, kind, *, a_ch=False, b_ch=False, out_ch=False, out_dtype=F32, rows=None, row_off=0, tm=512,
           rider=None, post=None, fold=False, place=None):
    a0, b0 = pairs[0]
    n_chunk = a0.shape[0] if a_ch else (b0.shape[0] if b_ch else 1)
    total_rows = a0.shape[-2]
    rows = total_rows - row_off if rows is None else rows
    tm = min(tm, rows)
    assert rows % tm == 0 and row_off % tm == 0, (name, rows, tm, row_off)
    n_rt, off = rows // tm, row_off // tm
    dims = _DIMS[kind]
    n_pair = len(pairs)

    if kind == "tn":
        grid, red_axis, n_red = (n_chunk, n_rt), 1, n_rt
        a_idx = (lambda k, i: (k, i + off, 0)) if a_ch else (lambda k, i: (i + off, 0))
        b_idx = (lambda k, i: (k, i + off, 0)) if b_ch else (lambda k, i: (i + off, 0))
        a_blk = lambda a: ((None, tm, a.shape[-1]) if a_ch else (tm, a.shape[-1]))
        b_blk = lambda b: ((None, tm, b.shape[-1]) if b_ch else (tm, b.shape[-1]))
        o2 = (a0.shape[-1], b0.shape[-1])
        out_shape = ((n_chunk,) + o2) if out_ch else o2
        out_spec = pl.BlockSpec((None,) + o2, lambda k, i: (k, 0, 0)) if out_ch else pl.BlockSpec(o2, lambda k, i: (0, 0))
        acc_shape = o2
    else:
        n_out = b0.shape[-1] if kind == "nn" else b0.shape[-2]
        b2 = b0.shape[-2:]
        if a_ch and b_ch and not out_ch and fold:
            grid, red_axis, n_red = (n_rt,), None, 1
            a_idx, b_idx = (lambda i: (0, i + off, 0)), (lambda i: (0, 0, 0))
            a_blk = lambda a: (n_chunk, tm, a.shape[-1])
            b_blk = lambda b: tuple(b.shape)
            out_shape, out_spec = (rows, n_out), pl.BlockSpec((tm, n_out), lambda i: (i, 0))
        elif a_ch and b_ch and not out_ch:
            grid, red_axis, n_red = (n_rt, n_chunk), 1, n_chunk
            a_idx, b_idx = (lambda i, k: (k, i + off, 0)), (lambda i, k: (k, 0, 0))
            a_blk = lambda a: (None, tm, a.shape[-1])
            b_blk = lambda b: (None,) + tuple(b.shape[-2:])
            out_shape, out_spec = (rows, n_out), pl.BlockSpec((tm, n_out), lambda i, k: (i, 0))
        elif out_ch and fold:
            assert b_ch and not a_ch and n_pair == 1
            grid, red_axis, n_red = (n_rt,), None, 1
            a_idx, b_idx = (lambda i: (i + off, 0)), (lambda i: (0, 0, 0))
            a_blk = lambda a: (tm, a.shape[-1])
            b_blk = lambda b: tuple(b.shape)
            out_shape, out_spec = (n_chunk, rows, n_out), pl.BlockSpec((n_chunk, tm, n_out), lambda i: (0, i, 0))
        elif out_ch:
            assert b_ch and not a_ch
            grid, red_axis, n_red = (n_chunk, n_rt), None, 1
            a_idx, b_idx = (lambda k, i: (i + off, 0)), (lambda k, i: (k, 0, 0))
            a_blk = lambda a: (tm, a.shape[-1])
            b_blk = lambda b: (None,) + tuple(b.shape[-2:])
            out_shape, out_spec = (n_chunk, rows, n_out), pl.BlockSpec((None, tm, n_out), lambda k, i: (k, i, 0))
        else:
            assert not (a_ch or b_ch)
            grid, red_axis, n_red = (n_rt,), None, 1
            a_idx, b_idx = (lambda i: (i + off, 0)), (lambda i: (0, 0))
            a_blk = lambda a: (tm, a.shape[-1])
            b_blk = lambda b: tuple(b.shape)
            out_shape, out_spec = (rows, n_out), pl.BlockSpec((tm, n_out), lambda i: (i, 0))
            if place is not None:
                out_shape, o_off = (place[0], n_out), place[1] // tm
                out_spec = pl.BlockSpec((tm, n_out), lambda i: (i + o_off, 0))
        acc_shape = (tm, n_out)

    into = [] if place is None or place[2] is None else [place[2]]
    post_ins, post_fn, out_dtypes = ([], None, [out_dtype]) if post is None else post
    hosted = _Hosted(rider, 2 * n_pair + len(post_ins) + len(into), len(out_dtypes), int(n_red > 1), grid)

    def body(*refs):
        ins, outs, scr = hosted.split(refs)

        def compute():
            acc = None
            for p in range(n_pair):
                for k in ([None] if not fold else range(n_chunk)):
                    pick = (lambda r: r[...]) if k is None else (lambda r: r[k])
                    d = lax.dot_general(pick(ins[2 * p]).astype(BF16), pick(ins[2 * p + 1]).astype(BF16), dims,
                                        preferred_element_type=F32)
                    acc = d if acc is None else acc + d
            return acc

        def emit(acc):
            vals = (acc,) if post_fn is None else post_fn(
                acc, *[r[...].astype(F32) for r in ins[2 * n_pair:2 * n_pair + len(post_ins)]])
            for o_ref, v in zip(outs, vals):
                o_ref[...] = v.astype(o_ref.dtype)

        if out_ch and fold:
            a_tile = ins[0][...].astype(BF16)
            for k in range(n_chunk):
                acc = lax.dot_general(a_tile, ins[1][k].astype(BF16), dims, preferred_element_type=F32)
                vals = (acc,) if post_fn is None else post_fn(acc, *[r[k].astype(F32) for r in ins[2:2 + len(post_ins)]])
                for o_ref, v in zip(outs, vals):
                    o_ref[k] = v.astype(o_ref.dtype)
        elif n_red == 1:
            emit(compute())
        else:
            acc_ref = scr[0]
            r = pl.program_id(red_axis)

            @pl.when(r == 0)
            def _():
                acc_ref[...] = jnp.zeros_like(acc_ref)

            acc_ref[...] += compute()

            @pl.when(r == n_red - 1)
            def _():
                emit(acc_ref[...])
        hosted.finish()

    in_specs, args, vmem = [], [], 0
    for a, b in pairs:
        in_specs += [pl.BlockSpec(a_blk(a), a_idx), pl.BlockSpec(b_blk(b), b_idx)]
        args += [a, b]
        vmem += 2 * (_nbytes([s for s in a_blk(a) if s], a.dtype) + _nbytes([s for s in b_blk(b) if s], b.dtype))
    in_specs += [out_spec] * len(post_ins)
    args += list(post_ins)
    aliases = {len(args): 0} if into else {}
    in_specs += [pl.BlockSpec(memory_space=pl.ANY)] * len(into)
    args += into
    tiles_per_step = n_chunk if (out_ch and fold) else 1
    vmem += (3 + 2 * n_pair + tiles_per_step * (len(post_ins) + len(out_dtypes))) * _nbytes(acc_shape, F32)
    scratch = [pltpu.VMEM(acc_shape, F32)] if n_red > 1 else []
    in_specs, out_shapes, out_specs, scratch, args = hosted.call_args(
        in_specs, [_big(out_shape, dt) for dt in out_dtypes], [out_spec] * len(out_dtypes), scratch, args)
    return hosted.results(pl.pallas_call(
        body, name=name, out_shape=out_shapes, grid=grid, in_specs=in_specs, out_specs=out_specs,
        input_output_aliases=aliases, scratch_shapes=scratch, compiler_params=_params(vmem + (8 << 20), len(grid)),
    )(*_in_hbm(args)))


def row(arr, width=None, cb=0, roff=0):
    return (arr, arr.shape[-1] if width is None else width, cb, roff)


def two_rows(first, second, limit):
    return (first, first.shape[-1], 0, 0, (second, limit))


def _row_inputs(rows, tm):
    specs, arrs, slots = [], [], []
    for d in rows:
        second, limit = d[4] if len(d) > 4 else (None, None)
        slots.append((len(arrs), limit))
        specs.append(_row_spec(d[:4], tm, limit))
        arrs.append(d[0])
        if second is not None:
            specs.append(pl.BlockSpec((tm, d[1]), lambda i, limit=limit: (jnp.maximum(i - limit, 0), 0)))
            arrs.append(second)

    def read(refs, i):
        vals = []
        for at, limit in slots:
            v = refs[at][...].astype(F32)
            vals.append(v if limit is None else jnp.where(i < limit, v, refs[at + 1][...].astype(F32)))
        return vals

    return specs, arrs, read


def _row_spec(desc, tm, limit=None):
    _, width, cb, roff = desc[:4]
    if limit is None:
        return pl.BlockSpec((tm, width), lambda i: (i + roff, cb))
    return pl.BlockSpec((tm, width), lambda i: (jnp.minimum(i, limit - 1) + roff, cb))


def _segmenter(tm, seq_len, n_lat):
    seg = lambda i: jnp.where(i * tm < n_lat, (i * tm) // seq_len, n_lat // seq_len)
    first = lambda i: jnp.where(i * tm < n_lat, (i * tm) % seq_len == 0, i * tm == n_lat)
    return seg, first


def rowwise(name, fn, rows, segs, params, outs, *, tm, n_tiles, seg_fn=None, rider=None):
    row_specs, row_arrs, read_rows = _row_inputs(rows, tm)
    n_r, n_s, n_p = len(row_arrs), len(segs), len(params)
    hosted = _Hosted(rider, n_r + n_s + n_p, len(outs), 0, (n_tiles,))

    def body(*refs):
        ins, out_refs, _ = hosted.split(refs)
        vals = read_rows(ins[:n_r], pl.program_id(0)) + [r[...] for r in ins[n_r:]]
        res = fn(*vals)
        for o_ref, v in zip(out_refs, res):
            o_ref[...] = v.astype(o_ref.dtype)
        hosted.finish()

    in_specs = list(row_specs)
    in_specs += [pl.BlockSpec((None, 1, s.shape[-1]), lambda i: (seg_fn(i), 0, 0)) for s in segs]
    in_specs += [pl.BlockSpec(p.shape, lambda i: (0, 0)) for p in params]
    vmem = sum(2 * tm * d[1] * 4 for d in rows) + sum(3 * tm * w * 4 for _, w, _ in outs) + sum(2 * p.size * 4 for p in params)
    in_specs, out_shapes, out_specs, scratch, args = hosted.call_args(
        in_specs, [_big((r, w), dt) for r, w, dt in outs],
        [pl.BlockSpec((tm, w), lambda i: (i, 0)) for _, w, _ in outs], [], row_arrs + list(segs) + list(params))
    return hosted.results(pl.pallas_call(
        body, name=name, grid=(n_tiles,), in_specs=in_specs, out_shape=out_shapes, out_specs=out_specs,
        scratch_shapes=scratch, compiler_params=_params(2 * vmem + (8 << 20)),
    )(*_in_hbm(args)), unwrap=False)


def rowwise_bwd(name, fn, rows, segs, params, cts, row_grads, *, tm, n_tiles, seg_fn=None, first_fn=None, adds=None,
                rider=None):
    adds = adds or {}
    need = [k for k, v in enumerate(row_grads) if v is not None]
    row_specs, row_arrs, read_rows = _row_inputs(rows, tm)
    n_r, n_s, n_p = len(row_arrs), len(segs), len(params)
    n_ct = sum(len(lst) for lst in cts)
    add_keys = sorted(adds)
    hosted = _Hosted(rider, n_r + n_s + n_p + n_ct + len(add_keys), len(need) + n_s + n_p, 0, (n_tiles,))

    def body(*refs):
        host_in, host_out, _ = hosted.split(refs)
        it = iter(list(host_in) + list(host_out))
        row_refs = [next(it) for _ in range(n_r)]
        seg_refs = [next(it) for _ in range(n_s)]
        par_refs = [next(it) for _ in range(n_p)]
        ct_refs = [[next(it) for _ in lst] for lst in cts]
        add_refs = {k: next(it) for k in add_keys}
        rg_refs = {k: next(it) for k in need}
        sg_refs = [next(it) for _ in range(n_s)]
        pg_refs = [next(it) for _ in range(n_p)]
        i = pl.program_id(0)
        rv = read_rows(row_refs, i)
        sv = [r[...] for r in seg_refs]
        pv = [r[...] for r in par_refs]

        def f(*args):
            rr = list(rv)
            for j, k in enumerate(need):
                rr[k] = args[j]
            return fn(*rr, *args[len(need):])

        _, vjp = jax.vjp(f, *[rv[k] for k in need], *sv, *pv)
        ctv = []
        for lst in ct_refs:
            acc = lst[0][...].astype(F32)
            for r in lst[1:]:
                acc = acc + r[...].astype(F32)
            ctv.append(acc)
        g = vjp(tuple(ctv))
        for j, k in enumerate(need):
            gv = g[j]
            if k in adds:
                lim = adds[k][1]
                av = add_refs[k][...].astype(F32)
                gv = gv + (av if lim is None else jnp.where(i < lim, av, 0.0))
            lim = row_grads[k][2]
            if lim is None:
                rg_refs[k][...] = gv.astype(rg_refs[k].dtype)
            else:
                @pl.when(i < lim)
                def _(gv=gv, k=k):
                    rg_refs[k][...] = gv.astype(rg_refs[k].dtype)
        if n_s:
            opens = first_fn(i)
            for ref, gv in zip(sg_refs, g[len(need):len(need) + n_s]):
                @pl.when(opens)
                def _(ref=ref, gv=gv):
                    ref[...] = gv

                @pl.when(jnp.logical_not(opens))
                def _(ref=ref, gv=gv):
                    ref[...] += gv
        for ref, gv in zip(pg_refs, g[len(need) + n_s:]):
            @pl.when(i == 0)
            def _(ref=ref, gv=gv):
                ref[...] = gv

            @pl.when(i > 0)
            def _(ref=ref, gv=gv):
                ref[...] += gv
        hosted.finish()

    seg_spec = lambda s: pl.BlockSpec((None, 1, s.shape[-1]), lambda i: (seg_fn(i), 0, 0))
    par_spec = lambda p: pl.BlockSpec(p.shape, lambda i: (0, 0))
    in_specs = list(row_specs) + [seg_spec(s) for s in segs] + [par_spec(p) for p in params]
    args = row_arrs + list(segs) + list(params)
    for lst in cts:
        in_specs += [_row_spec(d, tm) for d in lst]
        args += [d[0] for d in lst]
    for k in add_keys:
        in_specs.append(_row_spec(adds[k][0], tm, adds[k][1]))
        args.append(adds[k][0][0])
    out_shape, out_specs = [], []
    for k in need:
        n_rows, dt, lim = row_grads[k]
        out_shape.append(_big((n_rows, rows[k][1]), dt))
        out_specs.append(_row_spec((None, rows[k][1], 0, 0), tm, lim))
    for s in segs:
        out_shape.append(jax.ShapeDtypeStruct(s.shape, F32))
        out_specs.append(seg_spec(s))
    for p in params:
        out_shape.append(jax.ShapeDtypeStruct(p.shape, F32))
        out_specs.append(par_spec(p))
    vmem = sum(tm * d[1] * 4 for d in rows) * 6 + n_ct * tm * max(d[1] for d in rows) * 8
    in_specs, out_shape, out_specs, scratch, args = hosted.call_args(in_specs, out_shape, out_specs, [], args)
    return hosted.results(pl.pallas_call(
        body, name=name, grid=(n_tiles,), in_specs=in_specs, out_shape=out_shape, out_specs=out_specs,
        scratch_shapes=scratch, compiler_params=_params(vmem + (8 << 20)),
    )(*_in_hbm(args)), unwrap=False)


def _silu(v):
    return v * jax.nn.sigmoid(v)


def _rms(v, w):
    return v * lax.rsqrt(jnp.mean(v * v, axis=-1, keepdims=True) + EPS) * w


def fn_norm_mod(x, shift, scale, w):
    return (_rms(x, w) * (1.0 + scale) + shift,)


def fn_act(g, u):
    return (_silu(g) * u,)


def make_fn_resid(coef):
    def fn(x, f, gate):
        return (x + coef * gate * f,)
    return fn


def fn_silu_bias(v, b):
    return (_silu(v + b),)


def make_fn_gate_groupnorm(width):
    half = width // 2

    def fn(y_both, z, w):
        y = y_both * _silu(z)
        lane = lax.broadcasted_iota(jnp.int32, y.shape, 1)
        lo = lane < half
        sq = y * y
        s_lo = jnp.sum(jnp.where(lo, sq, 0.0), axis=-1, keepdims=True)
        s_hi = jnp.sum(jnp.where(lo, 0.0, sq), axis=-1, keepdims=True)
        r = jnp.where(lo, lax.rsqrt(s_lo / half + EPS), lax.rsqrt(s_hi / half + EPS))
        return (y * r * w,)
    return fn


def fn_glu(a, b):
    return (a * jax.nn.sigmoid(b),)


def fn_ln_silu(vw, vh, cb, lw, lb):
    v = jnp.concatenate([vw, vh], axis=-1) + cb
    mu = jnp.mean(v, axis=-1, keepdims=True)
    var = jnp.mean(jnp.square(v - mu), axis=-1, keepdims=True)
    return (_silu((v - mu) * lax.rsqrt(var + EPS) * lw + lb),)


def _col_tile(width):
    return width // 3 if width % (3 * LANES) == 0 else width


def mod_fwd(a_rows, w_shard, b_shard):
    n, d = a_rows.shape
    ws = w_shard.shape[1]
    tn = _col_tile(ws)

    def body(a_ref, w_ref, b_ref, o_ref):
        a = _silu(a_ref[...]).astype(BF16)
        o_ref[...] = jnp.dot(a, w_ref[...].astype(BF16), preferred_element_type=F32) + b_ref[...]

    return pl.pallas_call(
        body, name="mod_fwd", grid=(ws // tn,), out_shape=jax.ShapeDtypeStruct((n, ws), F32),
        in_specs=[pl.BlockSpec((n, d), lambda j: (0, 0)), pl.BlockSpec((d, tn), lambda j: (0, j)),
                  pl.BlockSpec((1, tn), lambda j: (0, j))],
        out_specs=pl.BlockSpec((n, tn), lambda j: (0, j)), compiler_params=_params(),
    )(a_rows, w_shard, b_shard)


def mod_bwd(a_rows, d_shard, d_full, w_shard, ctx_rows):
    n, d = a_rows.shape
    ws = w_shard.shape[1]
    tn = _col_tile(ws)
    n_ct = ws // tn

    def body(a_ref, ds_ref, df_ref, w_ref, gw_ref, gb_ref, q_ref):
        j = pl.program_id(0)
        a = _silu(a_ref[...])
        ds = ds_ref[...]
        gw_ref[...] = lax.dot_general(a, ds, _DIMS["tn"], precision=HI, preferred_element_type=F32)
        dctx = ds[ctx_rows[0]:ctx_rows[0] + 1, :]
        for r in ctx_rows[1:]:
            dctx = dctx + ds[r:r + 1, :]
        q = lax.dot_general(jnp.broadcast_to(dctx, (8, tn)), w_ref[...], _DIMS["nt"], precision=HI,
                            preferred_element_type=F32)

        @pl.when(j == 0)
        def _():
            q_ref[...] = q
            df = df_ref[...]
            acc = df[0:1, :]
            for r in range(1, n):
                acc = acc + df[r:r + 1, :]
            gb_ref[...] = acc

        @pl.when(j > 0)
        def _():
            q_ref[...] += q

    return pl.pallas_call(
        body, name="mod_bwd", grid=(n_ct,),
        out_shape=(jax.ShapeDtypeStruct((d, ws), F32), jax.ShapeDtypeStruct((1, d_full.shape[1]), F32),
                   jax.ShapeDtypeStruct((8, d), F32)),
        in_specs=[pl.BlockSpec((n, d), lambda j: (0, 0)), pl.BlockSpec((n, tn), lambda j: (0, j)),
                  pl.BlockSpec(d_full.shape, lambda j: (0, 0)), pl.BlockSpec((d, tn), lambda j: (0, j))],
        out_specs=(pl.BlockSpec((d, tn), lambda j: (0, j)), pl.BlockSpec((1, d_full.shape[1]), lambda j: (0, 0)),
                   pl.BlockSpec((8, d), lambda j: (0, 0))),
        compiler_params=_params(40 << 20),
    )(a_rows, d_shard, d_full, w_shard)


def _shifted(xs, d, tok, width):
    if d == 0:
        return xs
    n = xs.shape[0]
    sh = pltpu.roll(xs, (-d) % n, axis=0)
    return jnp.where((tok + d >= 0) & (tok + d < width), sh, 0.0)


def _placed(out_shape, place):
    if place is None:
        return out_shape, 0, 0, None
    return place


def tapsum_roll(name, x, xcb, w, wcb, *, seq_len, n_seq, row_blk_off, width, piece, cb, ncb, pad, flip, place=None):
    n_tap = w.shape[0]
    n_piece = seq_len // piece
    out_shape, o_rb, o_cb, into = _placed((n_seq * seq_len, ncb * cb), place)

    def body(x_ref, w_ref, *rest):
        o_ref = rest[-1]
        wv = w_ref[...]
        tok = lax.broadcasted_iota(jnp.int32, (piece, 1), 0) % width

        def do_piece(p, carry):
            start = pl.multiple_of(p * piece, piece)
            xs = x_ref[pl.ds(start, piece), :]
            acc = jnp.zeros_like(xs)
            for k in range(n_tap):
                d = pad - k if flip else k - pad
                acc = acc + wv[k:k + 1, :] * _shifted(xs, d, tok, width)
            o_ref[pl.ds(start, piece), :] = acc
            return carry

        lax.fori_loop(0, n_piece, do_piece, 0)

    extra = [] if into is None else [into]
    return pl.pallas_call(
        body, name=name, grid=(ncb, n_seq), out_shape=_big(out_shape, F32),
        in_specs=[pl.BlockSpec((seq_len, cb), lambda j, s: (row_blk_off + s, xcb + j)),
                  pl.BlockSpec((n_tap, cb), lambda j, s: (0, wcb + j))] + [pl.BlockSpec(memory_space=pl.ANY)] * len(extra),
        out_specs=pl.BlockSpec((seq_len, cb), lambda j, s: (o_rb + s, o_cb + j)),
        input_output_aliases={2: 0} if extra else {},
        compiler_params=_params(8 * seq_len * cb * 4 + (8 << 20), 2),
    )(*_in_hbm([x, w] + extra))


def tapgrad_roll(name, dy, dycb, dy_blk_off, x, xcb, x_blk_off, *, n_tap, seq_len, n_seq, width, piece, cb, ncb, pad):
    n_piece = seq_len // piece

    def body(dy_ref, x_ref, o_ref):
        @pl.when(pl.program_id(1) == 0)
        def _():
            o_ref[...] = jnp.zeros_like(o_ref)

        tok = lax.broadcasted_iota(jnp.int32, (piece, 1), 0) % width

        def do_piece(p, carry):
            start = pl.multiple_of(p * piece, piece)
            xs = x_ref[pl.ds(start, piece), :]
            dv = dy_ref[pl.ds(start, piece), :]
            for k in range(n_tap):
                o_ref[k:k + 1, :] += jnp.sum(dv * _shifted(xs, k - pad, tok, width), axis=0, keepdims=True)
            return carry

        lax.fori_loop(0, n_piece, do_piece, 0)

    return pl.pallas_call(
        body, name=name, grid=(ncb, n_seq), out_shape=jax.ShapeDtypeStruct((n_tap, ncb * cb), F32),
        in_specs=[pl.BlockSpec((seq_len, cb), lambda j, s: (dy_blk_off + s, dycb + j)),
                  pl.BlockSpec((seq_len, cb), lambda j, s: (x_blk_off + s, xcb + j))],
        out_specs=pl.BlockSpec((n_tap, cb), lambda j, s: (0, j)),
        compiler_params=_params(8 * seq_len * cb * 4 + (8 << 20), 2),
    )(*_in_hbm([dy, x]))


def tapsum_rows(name, x, xcb, w, wcb, *, seq_len, n_seq, cb, ncb, pad, flip, place=None):
    n_tap = w.shape[0]
    n_row = seq_len // GRID_W
    halo = pad * GRID_W
    out_shape, o_rb, o_cb, into = _placed((n_seq * seq_len, ncb * cb), place)

    def body(x_ref, w_ref, *rest):
        o_ref, xp = rest[-2:]
        xp[pl.ds(0, halo), :] = jnp.zeros((halo, cb), F32)
        xp[pl.ds(halo + seq_len, halo), :] = jnp.zeros((halo, cb), F32)
        xp[pl.ds(halo, seq_len), :] = x_ref[...]
        wv = w_ref[...]

        def do_row(r, carry):
            acc = jnp.zeros((GRID_W, cb), F32)
            for k in range(n_tap):
                d = pad - k if flip else k - pad
                acc = acc + wv[k:k + 1, :] * xp[pl.ds(pl.multiple_of((r + pad + d) * GRID_W, GRID_W), GRID_W), :]
            o_ref[pl.ds(pl.multiple_of(r * GRID_W, GRID_W), GRID_W), :] = acc
            return carry

        lax.fori_loop(0, n_row, do_row, 0)

    extra = [] if into is None else [into]
    return pl.pallas_call(
        body, name=name, grid=(ncb, n_seq), out_shape=_big(out_shape, F32),
        in_specs=[pl.BlockSpec((seq_len, cb), lambda j, s: (s, xcb + j)),
                  pl.BlockSpec((n_tap, cb), lambda j, s: (0, wcb + j))] + [pl.BlockSpec(memory_space=pl.ANY)] * len(extra),
        out_specs=pl.BlockSpec((seq_len, cb), lambda j, s: (o_rb + s, o_cb + j)),
        input_output_aliases={2: 0} if extra else {},
        scratch_shapes=[pltpu.VMEM((seq_len + 2 * halo, cb), F32)],
        compiler_params=_params(10 * seq_len * cb * 4 + (8 << 20), 2),
    )(*_in_hbm([x, w] + extra))


def tapgrad_rows(name, dy, dycb, x, xcb, *, n_tap, seq_len, n_seq, cb, ncb, pad):
    n_row = seq_len // GRID_W
    halo = pad * GRID_W

    def body(dy_ref, x_ref, o_ref, xp):
        @pl.when(pl.program_id(1) == 0)
        def _():
            o_ref[...] = jnp.zeros_like(o_ref)

        xp[pl.ds(0, halo), :] = jnp.zeros((halo, cb), F32)
        xp[pl.ds(halo + seq_len, halo), :] = jnp.zeros((halo, cb), F32)
        xp[pl.ds(halo, seq_len), :] = x_ref[...]

        def do_row(r, carry):
            dv = dy_ref[pl.ds(pl.multiple_of(r * GRID_W, GRID_W), GRID_W), :]
            for k in range(n_tap):
                xs = xp[pl.ds(pl.multiple_of((r + k) * GRID_W, GRID_W), GRID_W), :]
                o_ref[k:k + 1, :] += jnp.sum(dv * xs, axis=0, keepdims=True)
            return carry

        lax.fori_loop(0, n_row, do_row, 0)

    return pl.pallas_call(
        body, name=name, grid=(ncb, n_seq), out_shape=jax.ShapeDtypeStruct((n_tap, ncb * cb), F32),
        in_specs=[pl.BlockSpec((seq_len, cb), lambda j, s: (s, dycb + j)),
                  pl.BlockSpec((seq_len, cb), lambda j, s: (s, xcb + j))],
        out_specs=pl.BlockSpec((n_tap, cb), lambda j, s: (0, j)),
        scratch_shapes=[pltpu.VMEM((seq_len + 2 * halo, cb), F32)],
        compiler_params=_params(10 * seq_len * cb * 4 + (8 << 20), 2),
    )(*_in_hbm([dy, x]))


def _ssd_blocks(b, s, *, rev, n_ctx, n_lat, lat_blocks):
    if rev:
        return jnp.where(s < n_ctx, lat_blocks + b * n_ctx + (n_ctx - 1 - s), b * n_lat + (n_lat - 1 - (s - n_ctx)))
    return jnp.where(s < n_ctx, lat_blocks + b * n_ctx + s, b * n_lat + (s - n_ctx))


def _ssd_common(xbc, raw, dtb, alog, dsk, *, rev, ds, n_head):
    if rev:
        raw = pltpu.roll(raw, LANES - n_head, axis=1)
    pre = raw + dtb
    dt = jnp.maximum(pre, 0.0) + jnp.log1p(jnp.exp(-jnp.abs(pre)))
    sig = jax.nn.sigmoid(pre)
    a = -jnp.exp(alog)
    da = dt * a
    ri = lax.broadcasted_iota(jnp.int32, (CHUNK, CHUNK), 0)
    ci = lax.broadcasted_iota(jnp.int32, (CHUNK, CHUNK), 1)
    mask = (ci >= ri) if rev else (ci <= ri)
    tri = mask.astype(F32)
    tri_t = ((ci <= ri) if rev else (ci >= ri)).astype(F32)
    cs = jnp.dot(tri, da, precision=HI, preferred_element_type=F32)
    tot = jnp.sum(da, axis=0, keepdims=True)
    def wide(v):
        first = lax.broadcasted_iota(jnp.int32, (v.shape[0], LANES), 1) < HEAD_DIM
        return jnp.concatenate(
            [jnp.where(first, jnp.broadcast_to(v[:, 2 * p:2 * p + 1], first.shape),
                       jnp.broadcast_to(v[:, 2 * p + 1:2 * p + 2], first.shape)) for p in range(n_head // 2)], axis=1)

    cs_w, tot_w = wide(cs), wide(tot)
    xh = xbc[:, :ds]
    dt_w = wide(dt)
    return dict(
        dt=dt, sig=sig, a=a, cs=cs, cs_t=cs.T, tot=tot, mask=mask, tri_t=tri_t,
        e_w=jnp.exp(cs_w), wt_w=jnp.exp(tot_w - cs_w), dec_w=jnp.exp(tot_w), dt_w=dt_w, dsk_w=wide(dsk),
        xh=xh, xs_w=xh * dt_w, bm=xbc[:, ds:ds + 2 * N_STATE], cm=xbc[:, ds + 2 * N_STATE:ds + 4 * N_STATE])


def _decay(q, col):
    seg = q["cs"][:, col:col + 1] - q["cs_t"][col:col + 1, :]
    return jnp.exp(jnp.where(q["mask"], seg, -jnp.inf))


def _split_heads(v):
    lane = lax.broadcasted_iota(jnp.int32, v.shape, 1)
    return jnp.concatenate([jnp.where(lane < HEAD_DIM, v, 0.0), jnp.where(lane >= HEAD_DIM, v, 0.0)], axis=0)


def ssd_fwd(name, xbc, proj, dt_cb, dtb, alog, dsk, *, rev, n_ex, seq_len, ctx_len, ds, rider=None, add=None):
    n_head, half = ds // HEAD_DIM, ds // 2
    n_ctx, n_lat = ctx_len // CHUNK, seq_len // CHUNK
    n_step = n_ctx + n_lat
    blk = functools.partial(_ssd_blocks, rev=rev, n_ctx=n_ctx, n_lat=n_lat, lat_blocks=n_ex * n_lat)
    xw = xbc.shape[1]

    def y_blk(b, s):
        sl = jnp.maximum(s, n_ctx) - n_ctx
        return b * n_lat + ((n_lat - 1 - sl) if rev else sl)

    hosted = _Hosted(rider, 5 + (add is not None), 2, 1, (n_ex, n_step))

    def body(*refs):
        (xbc_ref, dt_ref, dtb_ref, alog_ref, dsk_ref, *add_ref), (y_ref, hs_ref), (h_scr,) = hosted.split(refs)

        @pl.when(pl.program_id(1) == 0)
        def _():
            h_scr[...] = jnp.zeros_like(h_scr)

        q = _ssd_common(xbc_ref[...], dt_ref[...], dtb_ref[...], alog_ref[...], dsk_ref[...], rev=rev, ds=ds, n_head=n_head)
        h = h_scr[...]
        hs_ref[...] = h
        for g in range(2):
            lo = g * half
            bg = q["bm"][:, g * N_STATE:(g + 1) * N_STATE].astype(BF16)
            cg = q["cm"][:, g * N_STATE:(g + 1) * N_STATE].astype(BF16)
            scores = lax.dot_general(cg, bg, _DIMS["nt"], preferred_element_type=F32)
            hg = h[:, lo:lo + half]
            off = jnp.dot(cg, hg.astype(BF16), preferred_element_type=F32)
            for j in range(half // LANES):
                c0 = (lo + j * LANES) // HEAD_DIM
                ln = slice(lo + j * LANES, lo + (j + 1) * LANES)
                p_cat = jnp.concatenate([scores * _decay(q, c0), scores * _decay(q, c0 + 1)], axis=1).astype(BF16)
                diag = jnp.dot(p_cat, _split_heads(q["xs_w"][:, ln]).astype(BF16), preferred_element_type=F32)
                y_ref[:, ln] = (diag + q["e_w"][:, ln] * off[:, j * LANES:(j + 1) * LANES]
                                + q["dsk_w"][:, ln] * q["xh"][:, ln] + (add_ref[0][:, ln] if add_ref else 0.0))
            v = (q["wt_w"][:, lo:lo + half] * q["xs_w"][:, lo:lo + half]).astype(BF16)
            h_scr[:, lo:lo + half] = (q["dec_w"][:, lo:lo + half] * hg
                                      + lax.dot_general(bg, v, _DIMS["tn"], preferred_element_type=F32))
        hosted.finish()

    vec = pl.BlockSpec((1, LANES), lambda b, s: (0, 0))
    in_specs, out_shape, out_specs, scratch, args = hosted.call_args(
        [pl.BlockSpec((CHUNK, xw), lambda b, s: (blk(b, s), 0)),
         pl.BlockSpec((CHUNK, LANES), lambda b, s: (blk(b, s), dt_cb)), vec, vec, vec]
        + [pl.BlockSpec((CHUNK, ds), lambda b, s: (y_blk(b, s), 0))] * (add is not None),
        (_big((n_ex * seq_len, ds), F32), _big((n_ex, n_step, N_STATE, ds), F32)),
        (pl.BlockSpec((CHUNK, ds), lambda b, s: (y_blk(b, s), 0)),
         pl.BlockSpec((None, None, N_STATE, ds), lambda b, s: (b, s, 0, 0))),
        [pltpu.VMEM((N_STATE, ds), F32)], [xbc, proj, dtb, alog, dsk] + ([] if add is None else [add]))
    return hosted.results(pl.pallas_call(
        body, name=name, grid=(n_ex, n_step), out_shape=out_shape, in_specs=in_specs, out_specs=out_specs,
        scratch_shapes=scratch, compiler_params=_params(40 << 20, 2),
    )(*_in_hbm(args)))


def ssd_bwd(name, xbc, proj, dt_cb, hs, dy, dtb, alog, dsk, *, rev, n_ex, seq_len, ctx_len, ds, rider=None, add=None):
    n_head, half = ds // HEAD_DIM, ds // 2
    n_ctx, n_lat = ctx_len // CHUNK, seq_len // CHUNK
    n_step = n_ctx + n_lat
    n_tok = n_ex * (seq_len + ctx_len)
    blk0 = functools.partial(_ssd_blocks, rev=rev, n_ctx=n_ctx, n_lat=n_lat, lat_blocks=n_ex * n_lat)
    step = lambda sp: n_step - 1 - sp
    blk = lambda b, sp: blk0(b, step(sp))
    xw = xbc.shape[1]

    def dy_blk(b, sp):
        sl = jnp.maximum(step(sp), n_ctx) - n_ctx
        return b * n_lat + ((n_lat - 1 - sl) if rev else sl)

    hosted = _Hosted(rider, 7 + (add is not None), 5, 1, (n_ex, n_step))

    def body(*refs):
        ((xbc_ref, dt_ref, hs_ref, dy_ref, dtb_ref, alog_ref, dsk_ref, *add_ref),
         (dxbc_ref, ddt_ref, dalog_ref, ddtb_ref, ddsk_ref), (dh_scr,)) = hosted.split(refs)
        b, sp = pl.program_id(0), pl.program_id(1)
        more = (lambda cols: add_ref[0][:, cols]) if add_ref else (lambda cols: 0.0)

        @pl.when(sp == 0)
        def _():
            dh_scr[...] = jnp.zeros_like(dh_scr)

        @pl.when((sp == 0) & (b == 0))
        def _():
            dalog_ref[...] = jnp.zeros_like(dalog_ref)
            ddtb_ref[...] = jnp.zeros_like(ddtb_ref)
            ddsk_ref[...] = jnp.zeros_like(ddsk_ref)

        q = _ssd_common(xbc_ref[...], dt_ref[...], dtb_ref[...], alog_ref[...], dsk_ref[...], rev=rev, ds=ds, n_head=n_head)
        h = hs_ref[...]
        d_y = jnp.where(step(sp) >= n_ctx, dy_ref[...], 0.0)
        dh_next = dh_scr[...]
        lane_row = lax.broadcasted_iota(jnp.int32, (1, LANES), 1)
        d_cs = jnp.zeros((CHUNK, LANES), F32)
        dxs_parts, de_parts, dwt_parts, ddec_parts = [], [], [], []
        for g in range(2):
            lo = g * half
            gs = slice(lo, lo + half)
            bg = q["bm"][:, g * N_STATE:(g + 1) * N_STATE].astype(BF16)
            cg = q["cm"][:, g * N_STATE:(g + 1) * N_STATE].astype(BF16)
            scores = lax.dot_general(cg, bg, _DIMS["nt"], preferred_element_type=F32)
            hg, dyg, dhn = h[:, gs], d_y[:, gs], dh_next[:, gs]
            off = jnp.dot(cg, hg.astype(BF16), preferred_element_type=F32)
            d_off = (q["e_w"][:, gs] * dyg).astype(BF16)
            de_parts.append(dyg * off)
            d_c = lax.dot_general(d_off, hg.astype(BF16), _DIMS["nt"], preferred_element_type=F32)
            dh_scr[:, gs] = (lax.dot_general(cg, d_off, _DIMS["tn"], preferred_element_type=F32)
                             + q["dec_w"][:, gs] * dhn)
            b_dh = jnp.dot(bg, dhn.astype(BF16), preferred_element_type=F32)
            v = q["wt_w"][:, gs] * q["xs_w"][:, gs]
            d_b = lax.dot_general(v.astype(BF16), dhn.astype(BF16), _DIMS["nt"], preferred_element_type=F32)
            dwt_parts.append(q["xs_w"][:, gs] * b_dh)
            ddec_parts.append(jnp.sum(hg * dhn, axis=0, keepdims=True))
            d_scores = jnp.zeros((CHUNK, CHUNK), F32)
            for j in range(half // LANES):
                c0 = (lo + j * LANES) // HEAD_DIM
                ln = slice(lo + j * LANES, lo + (j + 1) * LANES)
                l0, l1 = _decay(q, c0), _decay(q, c0 + 1)
                p0, p1 = scores * l0, scores * l1
                dy_st = _split_heads(d_y[:, ln]).astype(BF16)
                d_p = lax.dot_general(dy_st, q["xs_w"][:, ln].astype(BF16), _DIMS["nt"], preferred_element_type=F32)
                d_p0, d_p1 = d_p[:CHUNK], d_p[CHUNK:]
                d_scores = d_scores + d_p0 * l0 + d_p1 * l1
                for col, t in ((c0, d_p0 * p0), (c0 + 1, d_p1 * p1)):
                    d_cs = d_cs + jnp.sum(t - t.T, axis=1, keepdims=True) * (lane_row == col).astype(F32)
                p_st = jnp.concatenate([p0, p1], axis=0).astype(BF16)
                dxs_parts.append(lax.dot_general(p_st, dy_st, _DIMS["tn"], preferred_element_type=F32)
                                 + q["wt_w"][:, ln] * b_dh[:, j * LANES:(j + 1) * LANES])
            d_sc = d_scores.astype(BF16)
            d_c = d_c + jnp.dot(d_sc, bg, preferred_element_type=F32)
            d_b = d_b + lax.dot_general(d_sc, cg, _DIMS["tn"], preferred_element_type=F32)
            b_cols, c_cols = slice(ds + g * N_STATE, ds + (g + 1) * N_STATE), slice(ds + (2 + g) * N_STATE, ds + (3 + g) * N_STATE)
            dxbc_ref[:, b_cols] = d_b + more(b_cols)
            dxbc_ref[:, c_cols] = d_c + more(c_cols)
        d_xs = jnp.concatenate(dxs_parts, axis=1)
        narrow_m = (lax.broadcasted_iota(jnp.int32, (ds, LANES), 0) // HEAD_DIM
                    == lax.broadcasted_iota(jnp.int32, (ds, LANES), 1)).astype(BF16)
        rows8 = lambda v: jnp.broadcast_to(v, (8, ds))
        stacked = jnp.concatenate(
            [jnp.concatenate(dwt_parts, axis=1), jnp.concatenate(de_parts, axis=1), d_xs * q["xh"],
             rows8(jnp.concatenate(ddec_parts, axis=1)), rows8(jnp.sum(d_y * q["xh"], axis=0, keepdims=True))], axis=0)
        hi = stacked.astype(BF16)
        lo = (stacked - hi.astype(F32)).astype(BF16)
        sums = (jnp.dot(hi, narrow_m, preferred_element_type=F32) + jnp.dot(lo, narrow_m, preferred_element_type=F32))
        n_wt, n_e, n_xs = sums[:CHUNK], sums[CHUNK:2 * CHUNK], sums[2 * CHUNK:3 * CHUNK]
        n_dec, n_dsk = sums[3 * CHUNK:3 * CHUNK + 1], sums[3 * CHUNK + 8:3 * CHUNK + 9]
        e, wt, dec = jnp.exp(q["cs"]), jnp.exp(q["tot"] - q["cs"]), jnp.exp(q["tot"])
        d_wt = n_wt * wt
        d_cs = d_cs + n_e * e - d_wt
        d_tot = jnp.sum(d_wt, axis=0, keepdims=True) + n_dec * dec
        d_da = jnp.dot(q["tri_t"], d_cs, precision=HI, preferred_element_type=F32) + d_tot
        d_dt = d_da * q["a"] + n_xs
        dxbc_ref[:, :ds] = d_xs * q["dt_w"] + q["dsk_w"] * d_y + more(slice(0, ds))
        dalog_ref[...] += jnp.sum(d_da * q["dt"], axis=0, keepdims=True) * q["a"]
        d_raw = d_dt * q["sig"]
        ddtb_ref[...] += jnp.sum(d_raw, axis=0, keepdims=True)
        ddsk_ref[...] += n_dsk
        ddt_ref[...] = pltpu.roll(d_raw, n_head, axis=1) if rev else d_raw
        hosted.finish()

    vec = pl.BlockSpec((1, LANES), lambda b, s: (0, 0))
    vec_shape = jax.ShapeDtypeStruct((1, LANES), F32)
    in_specs, out_shape, out_specs, scratch, args = hosted.call_args(
        [pl.BlockSpec((CHUNK, xw), lambda b, s: (blk(b, s), 0)),
         pl.BlockSpec((CHUNK, LANES), lambda b, s: (blk(b, s), dt_cb)),
         pl.BlockSpec((None, None, N_STATE, ds), lambda b, s: (b, step(s), 0, 0)),
         pl.BlockSpec((CHUNK, ds), lambda b, s: (dy_blk(b, s), 0)), vec, vec, vec]
        + [pl.BlockSpec((CHUNK, xw), lambda b, s: (blk(b, s), 0))] * (add is not None),
        (_big((n_tok, xw), F32), _big((n_tok, LANES), F32), vec_shape, vec_shape, vec_shape),
        (pl.BlockSpec((CHUNK, xw), lambda b, s: (blk(b, s), 0)),
         pl.BlockSpec((CHUNK, LANES), lambda b, s: (blk(b, s), 0)), vec, vec, vec),
        [pltpu.VMEM((N_STATE, ds), F32)], [xbc, proj, hs, dy, dtb, alog, dsk] + ([] if add is None else [add]))
    return hosted.results(pl.pallas_call(
        body, name=name, grid=(n_ex, n_step), out_shape=out_shape, in_specs=in_specs, out_specs=out_specs,
        scratch_shapes=scratch, compiler_params=_params(48 << 20, 2),
    )(*_in_hbm(args)))


def final_loss(x3, target, w, *, tm):
    n, d = x3.shape

    def body(x_ref, t_ref, w_ref, dx_ref, dw_ref, loss_ref):
        i = pl.program_id(0)
        t = t_ref[...]

        def per_feature(xv, wv):
            err = _rms(xv, wv) - t
            return 0.5 * jnp.sum(err * err, axis=0, keepdims=True) / d

        lv, vjp = jax.vjp(per_feature, x_ref[...], w_ref[...])
        dx, dw = vjp(jnp.ones_like(lv))
        dx_ref[...] = dx

        @pl.when(i == 0)
        def _():
            dw_ref[...] = dw
            loss_ref[...] = lv

        @pl.when(i > 0)
        def _():
            dw_ref[...] += dw
            loss_ref[...] += lv

    tile = pl.BlockSpec((tm, d), lambda i: (i, 0))
    vec = pl.BlockSpec((1, d), lambda i: (0, 0))
    return pl.pallas_call(
        body, name="final_loss", grid=(n // tm,), in_specs=[tile, tile, vec],
        out_shape=(jax.ShapeDtypeStruct((n, d), F32), jax.ShapeDtypeStruct((1, d), F32), jax.ShapeDtypeStruct((1, d), F32)),
        out_specs=(tile, vec, vec), compiler_params=_params(tm * d * 4 * 16 + (8 << 20)),
    )(x3, target, w)


def sum_slots(name, arr, out_dtype=F32):
    n_slot, n_row, width = arr.shape
    tm = _row_tile(n_row, width * n_slot, mult=16)

    def body(a_ref, o_ref):
        acc = a_ref[0].astype(F32)
        for j in range(1, n_slot):
            acc = acc + a_ref[j].astype(F32)
        o_ref[...] = acc.astype(o_ref.dtype)

    return pl.pallas_call(
        body, name=name, grid=(n_row // tm,), out_shape=jax.ShapeDtypeStruct((n_row, width), out_dtype),
        in_specs=[pl.BlockSpec((n_slot, tm, width), lambda i: (0, i, 0))],
        out_specs=pl.BlockSpec((tm, width), lambda i: (i, 0)), compiler_params=_params(),
    )(arr)


def adamw(name, w, g_slots, m, v):
    n_slot, n_row, width = g_slots.shape
    tm = _row_tile(n_row, width * 2)
    if g_slots.dtype == BF16 and tm % 16:
        tm16 = _row_tile(n_row, width * 2, mult=16)
        if tm16 % 16 == 0:
            tm = tm16
        else:
            g_slots = g_slots.astype(F32)

    def body(w_ref, g_ref, m_ref, v_ref, go_ref, d_ref, mo_ref, vo_ref):
        g = g_ref[0].astype(F32)
        for j in range(1, n_slot):
            g = g + g_ref[j].astype(F32)
        m2 = ADAM_B1 * m_ref[...] + (1.0 - ADAM_B1) * g
        v2 = ADAM_B2 * v_ref[...] + (1.0 - ADAM_B2) * jnp.square(g)
        m_hat = m2 / (1.0 - ADAM_B1 ** ADAM_STEP)
        v_hat = v2 / (1.0 - ADAM_B2 ** ADAM_STEP)
        go_ref[...] = g
        d_ref[...] = -ADAM_LR * (m_hat / (jnp.sqrt(v_hat) + ADAM_EPS) + ADAM_WD * w_ref[...])
        mo_ref[...] = m2
        vo_ref[...] = v2

    tile = pl.BlockSpec((tm, width), lambda i: (i, 0))
    shape = jax.ShapeDtypeStruct((n_row, width), F32)
    return pl.pallas_call(
        body, name=name, grid=(n_row // tm,), out_shape=(shape,) * 4,
        in_specs=[tile, pl.BlockSpec((n_slot, tm, width), lambda i: (0, i, 0)), tile, tile],
        out_specs=(tile,) * 4, compiler_params=_params(),
    )(w, g_slots, m, v)


def cctx_grad(q_all, c_ctx_row):
    d = c_ctx_row.shape[1]

    def body(q_ref, c_ref, o_ref):
        acc = q_ref[0, 0:1, :]
        for j in (2, 4, 6):
            acc = acc + q_ref[j, 0:1, :]
        _, vjp = jax.vjp(_silu, c_ref[...])
        o_ref[...] = vjp(acc)[0]

    return pl.pallas_call(
        body, name="cctx_grad", out_shape=jax.ShapeDtypeStruct((1, d), F32),
    )(q_all, c_ctx_row)


def loss_total(pack_sum, d):
    def body(p_ref, o_ref):
        o_ref[...] = jnp.sum(p_ref[:, 0:d], axis=1, keepdims=True)

    return pl.pallas_call(
        body, name="loss_total", out_shape=jax.ShapeDtypeStruct((1, 1), F32),
    )(pack_sum)


class _Plan:
    def __init__(self):
        self.builders, self.got = {}, {}

    def on(self, host, key, builder):
        self.builders.setdefault(host, []).append((key, builder))

    def run(self, host, fn, *args, **kw):
        if host not in self.builders:
            return fn(host, *args, **kw)
        keys, riders = zip(*[(key, builder(self)) for key, builder in self.builders[host]])
        res, landed = fn(host, *args, rider=Riders(riders), **kw)
        for key, r in zip(keys, riders):
            self.got[key], landed = landed[:r.n], landed[r.n:]
        return res


def _val(w):
    return w() if callable(w) else w


def _matmul_tile(n_rows, tm):
    return 2 * tm if n_rows % (2 * tm) == 0 else tm


def _ffn_fwd(plan, tag, xin, n_rows, tm, seg_fn, shift, scale, gate, norm_w, wg, wu, wd):
    d = xin[1]
    n_tiles = n_rows // tm
    (h,) = plan.run(f"{tag}_norm", rowwise, fn_norm_mod, [xin], [shift, scale], [norm_w], [(n_rows, d, BF16)],
                    tm=tm, n_tiles=n_tiles, seg_fn=seg_fn)
    tmm = _matmul_tile(n_rows, tm)
    g = plan.run(f"{tag}_gate", matmul, [(h, _val(wg))], "nn", out_dtype=BF16, b_ch=True, out_ch=True, tm=tmm)
    u, act = plan.run(f"{tag}_up", matmul, [(h, _val(wu))], "nn", b_ch=True, out_ch=True, tm=tm, fold=True,
                      post=([g], lambda acc, gv: (acc, fn_act(gv, acc)[0]), [BF16, BF16]))
    f = plan.run(f"{tag}_down", matmul, [(act, _val(wd))], "nn", a_ch=True, b_ch=True, tm=tmm, fold=True)
    (xo,) = plan.run(f"{tag}_resid", rowwise, make_fn_resid(0.5), [xin, row(f)], [gate], [], [(n_rows, d, F32)],
                     tm=tm, n_tiles=n_tiles, seg_fn=seg_fn)
    return xo, (h, g, u, act, f)


def _ffn_bwd(plan, tag, d_xo, saved, xin, n_rows, tm, seg_fn, first_fn, shift, scale, gate, norm_w, wg, wu, wd, dx_rows, dx_limit):
    h, g, u, act, f = saved
    d = xin[1]
    n_tiles = n_rows // tm
    n_ch, _, n_hid = g.shape
    d_f, d_gate = plan.run(f"{tag}_resid_bwd", rowwise_bwd, make_fn_resid(0.5), [xin, row(f)], [gate], [], [[row(d_xo)]],
                           [None, (n_rows, BF16, None)], tm=tm, n_tiles=n_tiles, seg_fn=seg_fn, first_fn=first_fn)
    tmm = _matmul_tile(n_rows, tm)
    def act_vjp(d_act, gv, uv):
        s = jax.nn.sigmoid(gv)
        gs = gv * s
        return d_act * uv * (s + gs * (1.0 - s)), d_act * gs
    d_g, d_u = plan.run(f"{tag}_down_dx", matmul, [(d_f, wd)], "nt", b_ch=True, out_ch=True, tm=tmm,
                        post=([g, u], act_vjp, [BF16, BF16]))
    plan.got[f"{tag}_d_wd"] = plan.run(f"{tag}_down_dw", matmul, [(act, d_f)], "tn", out_dtype=BF16, a_ch=True, out_ch=True, tm=tmm)
    d_h = plan.run(f"{tag}_up_dx", matmul, [(d_g, wg), (d_u, wu)], "nt", a_ch=True, b_ch=True, tm=tmm)
    plan.got[f"{tag}_d_wg"] = plan.run(f"{tag}_gate_dw", matmul, [(d_g, h)], "tn", out_dtype=BF16, a_ch=True, out_ch=True, tm=tmm)
    plan.got[f"{tag}_d_wu"] = plan.run(f"{tag}_up_dw", matmul, [(d_u, h)], "tn", out_dtype=BF16, a_ch=True, out_ch=True, tm=tmm)
    d_x, d_shift, d_scale, d_nw = plan.run(
        f"{tag}_norm_bwd", rowwise_bwd, fn_norm_mod, [xin], [shift, scale], [norm_w], [[row(d_h)]], [(dx_rows, F32, dx_limit)],
        tm=tm, n_tiles=n_tiles, seg_fn=seg_fn, first_fn=first_fn, adds={0: (row(d_xo), None)})
    return d_x, (d_shift, d_scale, d_gate), d_nw


def kernel(x, c, ctx, c_ctx, w_mod, b_mod, norm_ffn1, ffn1_gate, ffn1_up, ffn1_down, norm_mix, w_in, ssm_conv_w, ssm_conv_b, dt_bias_fwd, dt_bias_bwd, a_log_fwd, a_log_bwd, ssm_d, ssm_norm_w, cconv_w, cconv_b, cconv_ln_w, cconv_ln_b, w_out, norm_ffn2, ffn2_gate, ffn2_up, ffn2_down, final_norm, loss_target, m_c_ctx, m_w_mod, m_b_mod, m_norm_ffn1, m_ffn1_gate, m_ffn1_up, m_ffn1_down, m_norm_mix, m_w_in, m_ssm_conv_w, m_ssm_conv_b, m_dt_bias_fwd, m_dt_bias_bwd, m_a_log_fwd, m_a_log_bwd, m_ssm_d, m_ssm_norm_w, m_cconv_w, m_cconv_b, m_cconv_ln_w, m_cconv_ln_b, m_w_out, m_norm_ffn2, m_ffn2_gate, m_ffn2_up, m_ffn2_down, m_final_norm, v_c_ctx, v_w_mod, v_b_mod, v_norm_ffn1, v_ffn1_gate, v_ffn1_up, v_ffn1_down, v_norm_mix, v_w_in, v_ssm_conv_w, v_ssm_conv_b, v_dt_bias_fwd, v_dt_bias_bwd, v_a_log_fwd, v_a_log_bwd, v_ssm_d, v_ssm_norm_w, v_cconv_w, v_cconv_b, v_cconv_ln_w, v_cconv_ln_b, v_w_out, v_norm_ffn2, v_ffn2_gate, v_ffn2_up, v_ffn2_down, v_final_norm):
    weights = dict(c_ctx=c_ctx, w_mod=w_mod, b_mod=b_mod, norm_ffn1=norm_ffn1, ffn1_gate=ffn1_gate, ffn1_up=ffn1_up, ffn1_down=ffn1_down, norm_mix=norm_mix, w_in=w_in, ssm_conv_w=ssm_conv_w, ssm_conv_b=ssm_conv_b, dt_bias_fwd=dt_bias_fwd, dt_bias_bwd=dt_bias_bwd, a_log_fwd=a_log_fwd, a_log_bwd=a_log_bwd, ssm_d=ssm_d, ssm_norm_w=ssm_norm_w, cconv_w=cconv_w, cconv_b=cconv_b, cconv_ln_w=cconv_ln_w, cconv_ln_b=cconv_ln_b, w_out=w_out, norm_ffn2=norm_ffn2, ffn2_gate=ffn2_gate, ffn2_up=ffn2_up, ffn2_down=ffn2_down, final_norm=final_norm)
    mom1 = dict(c_ctx=m_c_ctx, w_mod=m_w_mod, b_mod=m_b_mod, norm_ffn1=m_norm_ffn1, ffn1_gate=m_ffn1_gate, ffn1_up=m_ffn1_up, ffn1_down=m_ffn1_down, norm_mix=m_norm_mix, w_in=m_w_in, ssm_conv_w=m_ssm_conv_w, ssm_conv_b=m_ssm_conv_b, dt_bias_fwd=m_dt_bias_fwd, dt_bias_bwd=m_dt_bias_bwd, a_log_fwd=m_a_log_fwd, a_log_bwd=m_a_log_bwd, ssm_d=m_ssm_d, ssm_norm_w=m_ssm_norm_w, cconv_w=m_cconv_w, cconv_b=m_cconv_b, cconv_ln_w=m_cconv_ln_w, cconv_ln_b=m_cconv_ln_b, w_out=m_w_out, norm_ffn2=m_norm_ffn2, ffn2_gate=m_ffn2_gate, ffn2_up=m_ffn2_up, ffn2_down=m_ffn2_down, final_norm=m_final_norm)
    mom2 = dict(c_ctx=v_c_ctx, w_mod=v_w_mod, b_mod=v_b_mod, norm_ffn1=v_norm_ffn1, ffn1_gate=v_ffn1_gate, ffn1_up=v_ffn1_up, ffn1_down=v_ffn1_down, norm_mix=v_norm_mix, w_in=v_w_in, ssm_conv_w=v_ssm_conv_w, ssm_conv_b=v_ssm_conv_b, dt_bias_fwd=v_dt_bias_fwd, dt_bias_bwd=v_dt_bias_bwd, a_log_fwd=v_a_log_fwd, a_log_bwd=v_a_log_bwd, ssm_d=v_ssm_d, ssm_norm_w=v_ssm_norm_w, cconv_w=v_cconv_w, cconv_b=v_cconv_b, cconv_ln_w=v_cconv_ln_w, cconv_ln_b=v_cconv_ln_b, w_out=v_w_out, norm_ffn2=v_norm_ffn2, ffn2_gate=v_ffn2_gate, ffn2_up=v_ffn2_up, ffn2_down=v_ffn2_down, final_norm=v_final_norm)
    order = list(weights)

    n_ex, seq_len, d = x.shape
    ctx_len = ctx.shape[1]
    ds = d
    n_head = ds // HEAD_DIM
    xw = ds + 4 * N_STATE
    n_lat, n_ctx_rows = n_ex * seq_len, n_ex * ctx_len
    n_tok = n_lat + n_ctx_rows
    tm = math.gcd(math.gcd(512, seq_len), n_ctx_rows)
    seg_all, first_all = _segmenter(tm, seq_len, n_lat)
    lat_tiles = n_lat // tm

    xi, yi, ci = lax.axis_index("x"), lax.axis_index("y"), lax.axis_index("c")
    me, chip = 4 * xi + 2 * yi + ci, 2 * xi + yi

    (c_all,) = exchange("gather_c", [c], "all8")
    n_all = 8 * n_ex
    n_cond = -(-(n_all + 1) // 8) * 8
    cond = jnp.concatenate([c_all.reshape(n_all, d), c_ctx[None, :], jnp.zeros((n_cond - n_all - 1, d), F32)])
    mod_w = w_mod.shape[2]
    b_shard = lax.dynamic_slice(b_mod, (0, chip * mod_w), (1, mod_w))
    (mod_g,) = exchange("gather_mod", [mod_fwd(cond, w_mod[0], b_shard)], "chips")
    mod_full = mod_g.transpose(1, 0, 2).reshape(n_cond, N_CHIPS * mod_w)
    mod_mine = lax.dynamic_slice(mod_full, (me * n_ex, 0), (n_ex, 9 * d)).reshape(n_ex, 9, d)
    mod_ctx = mod_full[n_all].reshape(9, d)
    tabs = [jnp.concatenate([mod_mine[:, j], mod_ctx[j][None]])[:, None, :] for j in range(9)]
    lat = lambda t: t[:n_ex]

    bf = lambda w: w[0].astype(BF16)
    plan = _Plan()
    gather = lambda *ws: (lambda p: Rider(list(ws), "chips"))
    plan.on("ffn1_norm", "wg1", gather(bf(ffn1_gate)))
    plan.on("ffn1_gate", "wu1", gather(bf(ffn1_up)))
    plan.on("ffn1_up", "wd1", gather(bf(ffn1_down)))
    win_cut = d * 5 // 8
    plan.on("ffn1_down", "win_a", gather(bf(w_in)[:win_cut]))
    plan.on("ffn1_resid", "win_b", gather(bf(w_in)[win_cut:], ssm_conv_w[0], cconv_w[0]))
    xt = two_rows(x.reshape(n_lat, d), ctx.reshape(n_ctx_rows, d), lat_tiles)
    x1, saved1 = _ffn_fwd(plan, "ffn1", xt, n_tok, tm, seg_all, tabs[0], tabs[1], tabs[2], norm_ffn1,
                          lambda: plan.got["wg1"][0], lambda: plan.got["wu1"][0], lambda: plan.got["wd1"][0])
    (wg1,), (wu1,), (wd1,), (win_a,), (win_b, w5_g, w31_g) = (plan.got[k] for k in ("wg1", "wu1", "wd1", "win_a", "win_b"))
    win_g = jnp.concatenate([win_a, win_b], axis=1)
    unshard_cols = lambda t: t.transpose(1, 0, 2).reshape(t.shape[1], N_CHIPS * t.shape[2])
    win = unshard_cols(win_g)
    o_x, o_dt, o_glu = ds, ds + xw, ds + xw + 2 * n_head
    w_z, w_xbc, w_dt = win[:, :ds], win[:, o_x:o_dt], win[:, o_dt:o_glu]
    w_ga, w_gb = win[:, o_glu:o_glu + d], win[:, o_glu + d:]
    w_dtp = jnp.concatenate([w_dt, jnp.zeros((d, LANES - 2 * n_head), BF16)], axis=1)
    w_cat = jnp.concatenate([w_z, w_ga, w_gb, w_xbc, w_dtp], axis=1)
    cbw = d // 2
    xbc_cb, dt_cb = 3 * d // cbw, (3 * d + xw) // LANES
    w5, w31 = unshard_cols(w5_g), unshard_cols(w31_g)
    pad_vec = lambda v: jnp.concatenate([v.reshape(1, -1), jnp.zeros((1, LANES - v.size), F32)], axis=1)
    dtb_f, dtb_b, alog_f, alog_b = map(pad_vec, (dt_bias_fwd, dt_bias_bwd, a_log_fwd, a_log_bwd))
    dsk_f, dsk_b = pad_vec(ssm_d), jnp.zeros((1, LANES), F32)

    (h2,) = rowwise("mix_norm", fn_norm_mod, [row(x1)], [tabs[3], tabs[4]], [norm_mix], [(n_tok, d, BF16)],
                    tm=tm, n_tiles=n_tok // tm, seg_fn=seg_all)
    proj, (wg2,) = matmul("mix_proj", [(h2, w_cat)], "nn", tm=min(tm, 256), rider=Rider([bf(ffn2_gate)], "chips"))
    def conv5(name, src, cb0, flip):
        out = None
        for part, seq, off in (("lat", seq_len, 0), ("ctx", ctx_len, n_lat // ctx_len)):
            out = tapsum_roll(f"{name}_{part}", src, cb0, w5, 0, seq_len=seq, n_seq=n_ex, row_blk_off=off, width=seq,
                              piece=seq, cb=cbw, ncb=xw // cbw, pad=w5.shape[0] // 2, flip=flip,
                              place=((n_tok, xw), off, 0, out))
        return out

    craw = conv5("xbc_conv", proj, xbc_cb, False)
    (xbc,) = rowwise("xbc_silu", fn_silu_bias, [row(craw)], [], [ssm_conv_b], [(n_tok, xw, F32)], tm=tm, n_tiles=n_tok // tm)
    ssd = dict(n_ex=n_ex, seq_len=seq_len, ctx_len=ctx_len, ds=ds)
    (y_f, hs_f), (wu2, wd2) = ssd_fwd("ssd_fwd_f", xbc, proj, dt_cb, dtb_f, alog_f, dsk_f, rev=False,
                                      rider=Rider([bf(ffn2_up), bf(ffn2_down)], "chips"), **ssd)
    (y_b, hs_b), (wout_g,) = ssd_fwd("ssd_fwd_b", xbc, proj, dt_cb, dtb_b, alog_b, dsk_b, rev=True,
                                     rider=Rider([bf(w_out)], "chips"), add=y_f, **ssd)
    wout = wout_g.reshape(2 * d, d)
    wo_y, wo_u = wout[:ds], wout[ds:]
    fn_gate = make_fn_gate_groupnorm(ds)
    (yn,) = rowwise("ssd_gate", fn_gate, [row(y_b), row(proj, d, 0)], [], [ssm_norm_w], [(n_lat, ds, BF16)],
                    tm=tm, n_tiles=lat_tiles)
    (u0,) = rowwise("glu", fn_glu, [row(proj, d, 1), row(proj, d, 2)], [], [], [(n_lat, d, F32)], tm=tm, n_tiles=lat_tiles)
    cb31 = max(LANES, d // 4)
    ncb31 = (d // 2) // cb31
    pad31 = w31.shape[0] // 2
    piece31 = min(seq_len, 4 * GRID_W)
    v_w = tapsum_roll("cconv_cols", u0, 0, w31, 0, seq_len=seq_len, n_seq=n_ex, row_blk_off=0, width=GRID_W,
                      piece=piece31, cb=cb31, ncb=ncb31, pad=pad31, flip=False)
    v_h = tapsum_rows("cconv_rows", u0, ncb31, w31, ncb31, seq_len=seq_len, n_seq=n_ex, cb=cb31, ncb=ncb31, pad=pad31, flip=False)
    (un,) = rowwise("cconv_ln", fn_ln_silu, [row(v_w), row(v_h)], [], [cconv_b, cconv_ln_w, cconv_ln_b], [(n_lat, d, BF16)],
                    tm=tm, n_tiles=lat_tiles)
    mix = matmul("mix_out", [(yn, wo_y), (un, wo_u)], "nn", tm=tm)
    seg_lat, first_lat = _segmenter(tm, seq_len, n_lat)
    (x2,) = rowwise("mix_resid", make_fn_resid(1.0), [row(x1), row(mix)], [lat(tabs[5])], [], [(n_lat, d, F32)],
                    tm=tm, n_tiles=lat_tiles, seg_fn=seg_lat)
    x3, saved2 = _ffn_fwd(plan, "ffn2", row(x2), n_lat, tm, seg_lat, lat(tabs[6]), lat(tabs[7]), lat(tabs[8]), norm_ffn2, wg2, wu2, wd2)
    d_x3, d_final, loss_vec = final_loss(x3, loss_target.reshape(n_lat, d), final_norm.reshape(1, d), tm=tm)

    shard_cols = lambda t: t.reshape(t.shape[0], N_CHIPS, -1).transpose(1, 0, 2)

    def pieces(t):
        t = jnp.pad(t, ((0, 0), (0, -t.shape[1] % 32), (0, 0)))
        return t.reshape(2 * N_CHIPS, t.shape[1] // 2, t.shape[2]).astype(BF16)

    scatter = lambda *ts: Rider([pieces(t) for t in ts], "all8", scatter=True)
    halves = lambda names, landed: Rider([sum_slots(f"sum_{nm}", r, BF16) for nm, r in zip(names, landed)], "sibling")
    swapped = {}
    plan.on("ffn2_up_dx", "sc_ffn2_down", lambda p: scatter(p.got["ffn2_d_wd"]))
    plan.on("ffn2_up_dw", "sc_ffn2_gate", lambda p: scatter(p.got["ffn2_d_wg"]))
    d_x2, (d_s6, d_s7, d_g8), d_nffn2 = _ffn_bwd(
        plan, "ffn2", d_x3, saved2, row(x2), n_lat, tm, seg_lat, first_lat, lat(tabs[6]), lat(tabs[7]), lat(tabs[8]), norm_ffn2,
        wg2, wu2, wd2, n_lat, None)
    d_mix, d_g5 = rowwise_bwd("mix_resid_bwd", make_fn_resid(1.0), [row(x1), row(mix)], [lat(tabs[5])], [], [[row(d_x2)]],
                              [None, (n_lat, BF16, None)], tm=tm, n_tiles=lat_tiles, seg_fn=seg_lat, first_fn=first_lat)
    d_yn = matmul("mix_out_dy", [(d_mix, wo_y)], "nt", tm=tm)
    d_un = matmul("mix_out_du", [(d_mix, wo_u)], "nt", tm=tm)
    d_wout = jnp.concatenate([matmul("mix_out_dwy", [(yn, d_mix)], "tn", out_dtype=BF16, tm=tm),
                              matmul("mix_out_dwu", [(un, d_mix)], "tn", out_dtype=BF16, tm=tm)])
    d_vw, d_vh, d_cb, d_lnw, d_lnb = rowwise_bwd(
        "cconv_ln_bwd", fn_ln_silu, [row(v_w), row(v_h)], [], [cconv_b, cconv_ln_w, cconv_ln_b], [[row(d_un)]],
        [(n_lat, F32, None)] * 2, tm=tm, n_tiles=lat_tiles)
    d_u0 = tapsum_roll("cconv_cols_dx", d_vw, 0, w31, 0, seq_len=seq_len, n_seq=n_ex, row_blk_off=0, width=GRID_W,
                       piece=piece31, cb=cb31, ncb=ncb31, pad=pad31, flip=True, place=((n_lat, d), 0, 0, None))
    d_u0 = tapsum_rows("cconv_rows_dx", d_vh, 0, w31, ncb31, seq_len=seq_len, n_seq=n_ex, cb=cb31, ncb=ncb31, pad=pad31,
                       flip=True, place=((n_lat, d), 0, ncb31, d_u0))
    d_w31 = jnp.concatenate([
        tapgrad_roll("cconv_cols_dw", d_vw, 0, 0, u0, 0, 0, n_tap=w31.shape[0], seq_len=seq_len, n_seq=n_ex, width=GRID_W,
                     piece=piece31, cb=cb31, ncb=ncb31, pad=pad31),
        tapgrad_rows("cconv_rows_dw", d_vh, 0, u0, ncb31, n_tap=w31.shape[0], seq_len=seq_len, n_seq=n_ex, cb=cb31,
                     ncb=ncb31, pad=pad31)], axis=1)
    d_ga, d_gb = rowwise_bwd("glu_bwd", fn_glu, [row(proj, d, 1), row(proj, d, 2)], [], [], [[row(d_u0)]],
                             [(n_lat, BF16, None)] * 2, tm=tm, n_tiles=lat_tiles)
    d_ysum, d_z, d_ssmnw = rowwise_bwd(
        "ssd_gate_bwd", fn_gate, [row(y_b), row(proj, d, 0)], [], [ssm_norm_w], [[row(d_yn)]],
        [(n_lat, F32, None), (n_lat, BF16, None)], tm=tm, n_tiles=lat_tiles)
    (dxbc_f, ddt_f, dalog_f, ddtb_f, ddsk), landed = ssd_bwd(
        "ssd_bwd_f", xbc, proj, dt_cb, hs_f, d_ysum, dtb_f, alog_f, dsk_f, rev=False,
        rider=scatter(plan.got["ffn2_d_wu"], d_wout.reshape(N_CHIPS, -1, d)), **ssd)
    (dxbc_b, ddt_b, dalog_b, ddtb_b, _), both = ssd_bwd(
        "ssd_bwd_b", xbc, proj, dt_cb, hs_b, d_ysum, dtb_b, alog_b, dsk_b, rev=True,
        rider=halves(["ffn2_down", "ffn2_gate"], plan.got["sc_ffn2_down"] + plan.got["sc_ffn2_gate"]), add=dxbc_f, **ssd)
    swapped.update(zip(["ffn2_down", "ffn2_gate"], both))
    (d_craw, d_conv_b), both = rowwise_bwd(
        "xbc_silu_bwd", fn_silu_bias, [row(craw)], [], [ssm_conv_b], [[row(dxbc_b)]],
        [(n_tok, F32, None)], tm=tm, n_tiles=n_tok // tm, rider=halves(["ffn2_up", "w_out"], landed))
    swapped.update(zip(["ffn2_up", "w_out"], both))
    d_pxbc = conv5("xbc_conv_dx", d_craw, 0, True)
    g5 = lambda name, seq, off: tapgrad_roll(name, d_craw, 0, off, proj, xbc_cb, off, n_tap=w5.shape[0], seq_len=seq,
                                             n_seq=n_ex, width=seq, piece=seq, cb=cbw, ncb=xw // cbw, pad=w5.shape[0] // 2)
    d_w5 = g5("xbc_conv_lat_dw", seq_len, 0) + g5("xbc_conv_ctx_dw", ctx_len, n_lat // ctx_len)
    lat_pairs = [(d_z, w_z), (d_ga, w_ga), (d_gb, w_gb), (d_pxbc, w_xbc), (ddt_f, w_dtp), (ddt_b, w_dtp)]
    d_h2 = matmul("mix_proj_dx_lat", lat_pairs, "nt", rows=n_lat, tm=min(tm, 256), place=(n_tok, 0, None))
    d_h2 = matmul("mix_proj_dx_ctx", lat_pairs[3:], "nt", rows=n_ctx_rows, row_off=n_lat, tm=min(tm, 256),
                  place=(n_tok, n_lat, d_h2))
    d_wz = matmul("mix_proj_dwz", [(d_z, h2)], "tn", out_dtype=BF16, rows=n_lat, tm=tm)
    d_wga = matmul("mix_proj_dwa", [(d_ga, h2)], "tn", out_dtype=BF16, rows=n_lat, tm=tm)
    d_wgb = matmul("mix_proj_dwb", [(d_gb, h2)], "tn", out_dtype=BF16, rows=n_lat, tm=tm)
    d_wxbc = matmul("mix_proj_dwx", [(d_pxbc, h2)], "tn", out_dtype=BF16, tm=tm)
    d_wdt = matmul("mix_proj_dwt", [(ddt_f, h2), (ddt_b, h2)], "tn", out_dtype=BF16, tm=tm)
    d_win_t = jnp.concatenate([d_wz, d_wxbc, d_wdt[:2 * n_head], d_wga, d_wgb]).reshape(N_CHIPS, -1, d)
    d_x1, d_s3, d_s4, d_nmix = rowwise_bwd(
        "mix_norm_bwd", fn_norm_mod, [row(x1)], [tabs[3], tabs[4]], [norm_mix], [[row(d_h2)]], [(n_tok, F32, None)],
        tm=tm, n_tiles=n_tok // tm, seg_fn=seg_all, first_fn=first_all, adds={0: (row(d_x2), lat_tiles)})
    mix_names = ["w_in", "ssm_conv_w", "cconv_w"]
    plan.on("ffn1_down_dx", "sc_conv", lambda p: scatter(shard_cols(d_w5), shard_cols(d_w31)))
    plan.on("ffn1_up_dx", "sc_win", lambda p: scatter(d_win_t))
    plan.on("ffn1_gate_dw", "sc_ffn1_down", lambda p: scatter(p.got["ffn1_d_wd"]))
    plan.on("ffn1_up_dw", "sc_ffn1_gate", lambda p: scatter(p.got["ffn1_d_wg"]))
    plan.on("ffn1_up_dw", "sw_mix", lambda p: halves(mix_names, p.got["sc_win"] + p.got["sc_conv"]))
    plan.on("ffn1_norm_bwd", "sc_ffn1_up", lambda p: scatter(p.got["ffn1_d_wu"]))
    plan.on("ffn1_norm_bwd", "sw_ffn1_down", lambda p: halves(["ffn1_down"], p.got["sc_ffn1_down"]))
    d_xt, (d_s0, d_s1, d_g2), d_nffn1 = _ffn_bwd(
        plan, "ffn1", d_x1, saved1, xt, n_tok, tm, seg_all, first_all, tabs[0], tabs[1], tabs[2], norm_ffn1, wg1, wu1, wd1,
        n_lat, lat_tiles)
    swapped.update(zip(mix_names + ["ffn1_down"], plan.got["sw_mix"] + plan.got["sw_ffn1_down"]))
    last_names = ["ffn1_gate", "ffn1_up"]
    last = halves(last_names, plan.got["sc_ffn1_gate"] + plan.got["sc_ffn1_up"])
    grad_x = d_xt.reshape(n_ex, seq_len, d)

    with_ctx0 = lambda t: jnp.concatenate([t, jnp.zeros((1, 1, d), F32)])
    d_tabs = [d_s0, d_s1, d_g2, d_s3, d_s4, with_ctx0(d_g5), with_ctx0(d_s6), with_ctx0(d_s7), with_ctx0(d_g8)]
    d_mod_rows = jnp.concatenate([t[:, 0, :] for t in d_tabs], axis=1)
    n_pad_rows = -(-(n_ex + 1) // 8) * 8
    d_mod_rows = jnp.concatenate([d_mod_rows, jnp.zeros((n_pad_rows - n_ex - 1, 9 * d), F32)])
    small = [("loss", loss_vec), ("norm_ffn1", d_nffn1), ("norm_mix", d_nmix), ("ssm_conv_b", d_conv_b),
             ("dt_bias_fwd", ddtb_f[:, :n_head]), ("dt_bias_bwd", ddtb_b[:, :n_head]), ("a_log_fwd", dalog_f[:, :n_head]),
             ("a_log_bwd", dalog_b[:, :n_head]), ("ssm_d", ddsk[:, :n_head]), ("ssm_norm_w", d_ssmnw), ("cconv_b", d_cb),
             ("cconv_ln_w", d_lnw), ("cconv_ln_b", d_lnb), ("norm_ffn2", d_nffn2), ("final_norm", d_final)]
    n_small = sum(v.size for _, v in small)
    n_pack = -(-n_small // (8 * LANES)) * (8 * LANES)
    pack = jnp.concatenate([v.reshape(-1) for _, v in small] + [jnp.zeros((n_pack - n_small,), F32)]).reshape(-1, LANES)
    (pack_all, d_mod_all), both = exchange_many("gather_small_swap_last", [Rider([pack, d_mod_rows], "all8"), last])
    swapped.update(zip(last_names, both))
    pack_sum = sum_slots("small_sum", pack_all)
    loss = loss_total(pack_sum.reshape(1, n_pack), d).reshape(())
    flat_sum = pack_sum.reshape(-1)
    small_grads, pos = {}, 0
    for nm, v in small:
        small_grads[nm] = flat_sum[pos:pos + v.size]
        pos += v.size
    d_mod_all = d_mod_all.reshape(8 * n_pad_rows, 9 * d)
    cond_rows = [jnp.concatenate([cond[j * n_ex:(j + 1) * n_ex], c_ctx[None, :],
                                  jnp.zeros((n_pad_rows - n_ex - 1, d), F32)]) for j in range(8)]
    cond_bwd = jnp.concatenate(cond_rows)
    d_mod_shard = lax.dynamic_slice(d_mod_all, (0, chip * mod_w), (8 * n_pad_rows, mod_w))
    g_wmod, g_bmod, q_part = mod_bwd(cond_bwd, d_mod_shard, d_mod_all, w_mod[0],
                                     tuple(j * n_pad_rows + n_ex for j in range(8)))
    (q_all,) = exchange("gather_cctx", [q_part], "all8")
    g_cctx = cctx_grad(q_all, c_ctx.reshape(1, d))
    small_grads["c_ctx"], small_grads["b_mod"] = g_cctx.reshape(-1), g_bmod.reshape(-1)

    transposed = {"ffn1_gate", "ffn1_up", "ffn2_gate", "ffn2_up", "w_in"}
    results = {}
    for nm, both in swapped.items():
        flip = (lambda t: jnp.swapaxes(t, 1, 2)) if nm in transposed else (lambda t: t)
        shape = flip(weights[nm]).shape
        two_d = lambda t: flip(t).reshape(shape[-2], shape[-1])
        g_full = both.reshape(1, -1, shape[-1])[:, :shape[-2]]
        results[nm] = [flip(r.reshape(shape)) for r in
                       adamw(f"adamw_{nm}", two_d(weights[nm]), g_full, two_d(mom1[nm]), two_d(mom2[nm]))]
    results["w_mod"] = [r.reshape(w_mod.shape) for r in adamw("adamw_w_mod", w_mod[0], g_wmod[None], m_w_mod[0], v_w_mod[0])]
    small_names = [nm for nm in order if nm not in results]
    n_sm = sum(weights[nm].size for nm in small_names)
    n_smp = -(-n_sm // (8 * LANES)) * (8 * LANES)
    packed = lambda src: jnp.concatenate([src[nm].reshape(-1) for nm in small_names] + [jnp.zeros((n_smp - n_sm,), F32)]).reshape(-1, LANES)
    sm_out = adamw("adamw_small", packed(weights), packed(small_grads)[None], packed(mom1), packed(mom2))
    pos = 0
    for nm in small_names:
        size = weights[nm].size
        results[nm] = [r.reshape(-1)[pos:pos + size].reshape(weights[nm].shape) for r in sm_out]
        pos += size
    return (loss, grad_x, *[results[nm][0] for nm in order], *[results[nm][1] for nm in order],
            *[results[nm][2] for nm in order], *[results[nm][3] for nm in order])
```

```python
import functools
import math

import jax
import jax.numpy as jnp
from jax import lax
from jax.experimental import pallas as pl
from jax.experimental.pallas import tpu as pltpu

F32 = jnp.float32
BF16 = jnp.bfloat16
HI = lax.Precision.HIGHEST
MESH = pl.DeviceIdType.MESH

EPS = 1e-6
GRID_W = 64
HEAD_DIM = 64
N_STATE = 128
CHUNK = 128
LANES = 128
N_CHIPS = 4
ADAM_LR, ADAM_B1, ADAM_B2, ADAM_EPS, ADAM_WD, ADAM_STEP = 0.001, 0.9, 0.999, 1e-08, 0.01, 10
VMEM_CAP = 56 * 1024 * 1024


def _params(vmem_bytes=None, n_axes=1):
    kw = dict(dimension_semantics=("arbitrary",) * n_axes)
    if vmem_bytes is not None:
        kw["vmem_limit_bytes"] = int(min(VMEM_CAP, max(32 * 1024 * 1024, vmem_bytes)))
    return pltpu.CompilerParams(**kw)


def _big(shape, dtype):
    return pltpu.HBM(tuple(shape), dtype)


def _in_hbm(args):
    return [pltpu.with_memory_space_constraint(a, pltpu.HBM) if a.size * a.dtype.itemsize >= (1 << 20) else a for a in args]


def _nbytes(shape, dtype):
    return math.prod(shape) * jnp.dtype(dtype).itemsize


def _row_tile(rows, width, cap_bytes=1 << 20, mult=8):
    best = None
    for t in range(mult, rows + 1, mult):
        if rows % t == 0 and t * width * 4 <= cap_bytes:
            best = t
    return best if best is not None else rows


_MODES = {"all8": (8, (1, 2, 3, 4, 5, 6, 7), 0), "chips": (4, (2, 4, 6), 1), "sibling": (2, (1,), 0)}


class Rider:
    def __init__(self, arrs, mode, scatter=False):
        self.arrs, self.scatter = list(arrs), scatter
        self.nslot, self.deltas, self.shift = _MODES[mode]
        self.n = len(self.arrs)
        self.out_shape = [jax.ShapeDtypeStruct((self.nslot,) + (a.shape[1:] if scatter else a.shape), a.dtype)
                          for a in self.arrs]
        any_spec = pl.BlockSpec(memory_space=pl.ANY)
        self.in_specs = [any_spec] * self.n
        self.out_specs = [any_spec] * self.n
        n_peer = len(self.deltas)
        self.scratch = [pltpu.SemaphoreType.DMA((self.n, n_peer)), pltpu.SemaphoreType.DMA((self.n, n_peer)),
                        pltpu.SemaphoreType.DMA((self.n,))]

    def _copies(self, ins, outs, sems, arrivals):
        send_sems, recv_sems, local_sems = sems
        x, y, c = lax.axis_index("x"), lax.axis_index("y"), lax.axis_index("c")
        me = 4 * x + 2 * y + c
        slot_of = lambda dev: (dev >> self.shift) & (self.nslot - 1)
        src = lambda a, slot: ins[a].at[slot] if self.scatter else ins[a]
        flip = lambda v, bit: 1 - v if bit else v

        def remote(a, k, d, from_slot, to_slot):
            return pltpu.make_async_remote_copy(
                src_ref=src(a, from_slot), dst_ref=outs[a].at[to_slot], send_sem=send_sems.at[a, k],
                recv_sem=recv_sems.at[a, k], device_id=(flip(x, (d >> 2) & 1), flip(y, (d >> 1) & 1), flip(c, d & 1)),
                device_id_type=MESH)

        mine = slot_of(me)
        local = [pltpu.make_async_copy(src(a, mine), outs[a].at[mine], local_sems.at[a]) for a in range(self.n)]
        sends = [remote(a, k, d, slot_of(me ^ d), mine) for k, d in enumerate(self.deltas) for a in range(self.n)]
        if not arrivals:
            return local, sends
        return local, sends, [remote(a, k, d, mine, slot_of(me ^ d)) for k, d in enumerate(self.deltas) for a in range(self.n)]

    def start(self, ins, outs, sems):
        local, sends = self._copies(ins, outs, sems, arrivals=False)
        for cp in local + sends:
            cp.start()

    def wait(self, ins, outs, sems):
        local, sends, recvs = self._copies(ins, outs, sems, arrivals=True)
        for cp in recvs:
            cp.wait_recv()
        for cp in sends:
            cp.wait_send()
        for cp in local:
            cp.wait()


class Riders:
    def __init__(self, riders):
        self.riders = list(riders)
        self.n = sum(r.n for r in self.riders)
        cat = lambda attr: [v for r in self.riders for v in getattr(r, attr)]
        self.arrs, self.out_shape, self.in_specs = cat("arrs"), cat("out_shape"), cat("in_specs")
        self.out_specs, self.scratch = cat("out_specs"), cat("scratch")

    def _each(self, method, ins, outs, sems):
        i = s = 0
        for r in self.riders:
            getattr(r, method)(ins[i:i + r.n], outs[i:i + r.n], sems[s:s + len(r.scratch)])
            i, s = i + r.n, s + len(r.scratch)

    def start(self, ins, outs, sems):
        self._each("start", ins, outs, sems)

    def wait(self, ins, outs, sems):
        self._each("wait", ins, outs, sems)


class _Hosted:
    def __init__(self, rider, n_in, n_out, n_scratch, grid):
        self.rider, self.n_in, self.n_out, self.n_scratch, self.grid = rider, n_in, n_out, n_scratch, grid
        self.n = rider.n if rider else 0

    def split(self, refs):
        a, b = self.n_in, self.n_in + self.n
        c, e = b + self.n_out, b + self.n_out + self.n
        self._r = (refs[a:b], refs[c:e], refs[e + self.n_scratch:])
        if self.rider:
            ids = [pl.program_id(ax) for ax in range(len(self.grid))]
            first = functools.reduce(jnp.logical_and, [i == 0 for i in ids]) if ids else True
            pl.when(first)(lambda: self.rider.start(*self._r))
        return refs[:a], refs[b:c], refs[e:e + self.n_scratch]

    def finish(self):
        if self.rider:
            ids = [pl.program_id(ax) for ax in range(len(self.grid))]
            last = functools.reduce(jnp.logical_and, [i == n - 1 for i, n in zip(ids, self.grid)]) if ids else True
            pl.when(last)(lambda: self.rider.wait(*self._r))

    def call_args(self, in_specs, out_shape, out_specs, scratch, args):
        r = self.rider
        if not r:
            return list(in_specs), tuple(out_shape), tuple(out_specs), list(scratch), list(args)
        return (list(in_specs) + r.in_specs, tuple(out_shape) + tuple(r.out_shape), tuple(out_specs) + tuple(r.out_specs),
                list(scratch) + r.scratch, list(args) + r.arrs)

    def results(self, res, unwrap=True):
        res = list(res) if isinstance(res, (tuple, list)) else [res]
        host = res[:self.n_out]
        host = host[0] if (self.n_out == 1 and unwrap) else tuple(host)
        return (host, res[self.n_out:]) if self.rider else host


def exchange_many(name, riders):
    both = Riders(riders)

    def body(*refs):
        ins, outs, sems = refs[:both.n], refs[both.n:2 * both.n], refs[2 * both.n:]
        both.start(ins, outs, sems)
        both.wait(ins, outs, sems)

    res = list(pl.pallas_call(
        body, name=name, out_shape=tuple(both.out_shape), in_specs=both.in_specs, out_specs=tuple(both.out_specs),
        scratch_shapes=both.scratch,
    )(*both.arrs))
    split = []
    for r in riders:
        split.append(res[:r.n])
        res = res[r.n:]
    return split


def exchange(name, arrs, mode, scatter=False):
    rider = Rider(arrs, mode, scatter)

    def body(*refs):
        ins, outs, sems = refs[:rider.n], refs[rider.n:2 * rider.n], refs[2 * rider.n:]
        rider.start(ins, outs, sems)
        rider.wait(ins, outs, sems)

    return pl.pallas_call(
        body, name=name, out_shape=tuple(rider.out_shape), in_specs=rider.in_specs, out_specs=tuple(rider.out_specs),
        scratch_shapes=rider.scratch,
    )(*arrs)


_DIMS = {"nn": (((1,), (0,)), ((), ())), "nt": (((1,), (1,)), ((), ())), "tn": (((0,), (0,)), ((), ()))}


def matmul(name, pairs, kind, *, a_ch=False, b_ch=False, out_ch=False, out_dtype=F32, rows=None, row_off=0, tm=512,
           rider=None, post=None, fold=False, place=None):
    a0, b0 = pairs[0]
    n_chunk = a0.shape[0] if a_ch else (b0.shape[0] if b_ch else 1)
    total_rows = a0.shape[-2]
    rows = total_rows - row_off if rows is None else rows
    tm = min(tm, rows)
    assert rows % tm == 0 and row_off % tm == 0, (name, rows, tm, row_off)
    n_rt, off = rows // tm, row_off // tm
    dims = _DIMS[kind]
    n_pair = len(pairs)

    if kind == "tn":
        grid, red_axis, n_red = (n_chunk, n_rt), 1, n_rt
        a_idx = (lambda k, i: (k, i + off, 0)) if a_ch else (lambda k, i: (i + off, 0))
        b_idx = (lambda k, i: (k, i + off, 0)) if b_ch else (lambda k, i: (i + off, 0))
        a_blk = lambda a: ((None, tm, a.shape[-1]) if a_ch else (tm, a.shape[-1]))
        b_blk = lambda b: ((None, tm, b.shape[-1]) if b_ch else (tm, b.shape[-1]))
        o2 = (a0.shape[-1], b0.shape[-1])
        out_shape = ((n_chunk,) + o2) if out_ch else o2
        out_spec = pl.BlockSpec((None,) + o2, lambda k, i: (k, 0, 0)) if out_ch else pl.BlockSpec(o2, lambda k, i: (0, 0))
        acc_shape = o2
    else:
        n_out = b0.shape[-1] if kind == "nn" else b0.shape[-2]
        b2 = b0.shape[-2:]
        if a_ch and b_ch and not out_ch and fold:
            grid, red_axis, n_red = (n_rt,), None, 1
            a_idx, b_idx = (lambda i: (0, i + off, 0)), (lambda i: (0, 0, 0))
            a_blk = lambda a: (n_chunk, tm, a.shape[-1])
            b_blk = lambda b: tuple(b.shape)
            out_shape, out_spec = (rows, n_out), pl.BlockSpec((tm, n_out), lambda i: (i, 0))
        elif a_ch and b_ch and not out_ch:
            grid, red_axis, n_red = (n_rt, n_chunk), 1, n_chunk
            a_idx, b_idx = (lambda i, k: (k, i + off, 0)), (lambda i, k: (k, 0, 0))
            a_blk = lambda a: (None, tm, a.shape[-1])
            b_blk = lambda b: (None,) + tuple(b.shape[-2:])
            out_shape, out_spec = (rows, n_out), pl.BlockSpec((tm, n_out), lambda i, k: (i, 0))
        elif out_ch and fold:
            assert b_ch and not a_ch and all(a is a0 for a, _ in pairs)
            grid, red_axis, n_red = (n_rt,), None, 1
            a_idx, b_idx = (lambda i: (i + off, 0)), (lambda i: (0, 0, 0))
            a_blk = lambda a: (tm, a.shape[-1])
            b_blk = lambda b: tuple(b.shape)
            out_shape, out_spec = (n_chunk, rows, n_out), pl.BlockSpec((n_chunk, tm, n_out), lambda i: (0, i, 0))
        elif out_ch:
            assert b_ch and not a_ch
            grid, red_axis, n_red = (n_chunk, n_rt), None, 1
            a_idx, b_idx = (lambda k, i: (i + off, 0)), (lambda k, i: (k, 0, 0))
            a_blk = lambda a: (tm, a.shape[-1])
            b_blk = lambda b: (None,) + tuple(b.shape[-2:])
            out_shape, out_spec = (n_chunk, rows, n_out), pl.BlockSpec((None, tm, n_out), lambda k, i: (k, i, 0))
        else:
            assert not (a_ch or b_ch)
            grid, red_axis, n_red = (n_rt,), None, 1
            a_idx, b_idx = (lambda i: (i + off, 0)), (lambda i: (0, 0))
            a_blk = lambda a: (tm, a.shape[-1])
            b_blk = lambda b: tuple(b.shape)
            out_shape, out_spec = (rows, n_out), pl.BlockSpec((tm, n_out), lambda i: (i, 0))
            if place is not None:
                out_shape, o_off = (place[0], n_out), place[1] // tm
                out_spec = pl.BlockSpec((tm, n_out), lambda i: (i + o_off, 0))
        acc_shape = (tm, n_out)

    into = [] if place is None or place[2] is None else [place[2]]
    post_ins, post_fn, out_dtypes = ([], None, [out_dtype]) if post is None else post
    hosted = _Hosted(rider, 2 * n_pair + len(post_ins) + len(into), len(out_dtypes), int(n_red > 1), grid)

    def body(*refs):
        ins, outs, scr = hosted.split(refs)

        def compute():
            acc = None
            for p in range(n_pair):
                for k in ([None] if not fold else range(n_chunk)):
                    pick = (lambda r: r[...]) if k is None else (lambda r: r[k])
                    d = lax.dot_general(pick(ins[2 * p]).astype(BF16), pick(ins[2 * p + 1]).astype(BF16), dims,
                                        preferred_element_type=F32)
                    acc = d if acc is None else acc + d
            return acc

        def emit(acc):
            vals = (acc,) if post_fn is None else post_fn(
                acc, *[r[...].astype(F32) for r in ins[2 * n_pair:2 * n_pair + len(post_ins)]])
            for o_ref, v in zip(outs, vals):
                o_ref[...] = v.astype(o_ref.dtype)

        if out_ch and fold:
            a_tile = ins[0][...].astype(BF16)
            for k in range(n_chunk):
                accs = [lax.dot_general(a_tile, ins[2 * p + 1][k].astype(BF16), dims, preferred_element_type=F32)
                        for p in range(n_pair)]
                tiles = [r[k].astype(F32) for r in ins[2 * n_pair:2 * n_pair + len(post_ins)]]
                vals = tuple(accs) if post_fn is None else post_fn(*accs, *tiles)
                for o_ref, v in zip(outs, vals):
                    o_ref[k] = v.astype(o_ref.dtype)
        elif n_red == 1:
            emit(compute())
        else:
            acc_ref = scr[0]
            r = pl.program_id(red_axis)

            @pl.when(r == 0)
            def _():
                acc_ref[...] = jnp.zeros_like(acc_ref)

            acc_ref[...] += compute()

            @pl.when(r == n_red - 1)
            def _():
                emit(acc_ref[...])
        hosted.finish()

    in_specs, args, vmem = [], [], 0
    for a, b in pairs:
        in_specs += [pl.BlockSpec(a_blk(a), a_idx), pl.BlockSpec(b_blk(b), b_idx)]
        args += [a, b]
        vmem += 2 * (_nbytes([s for s in a_blk(a) if s], a.dtype) + _nbytes([s for s in b_blk(b) if s], b.dtype))
    in_specs += [out_spec] * len(post_ins)
    args += list(post_ins)
    aliases = {len(args): 0} if into else {}
    in_specs += [pl.BlockSpec(memory_space=pl.ANY)] * len(into)
    args += into
    tiles_per_step = n_chunk if (out_ch and fold) else 1
    vmem += (3 + 2 * n_pair + tiles_per_step * (len(post_ins) + len(out_dtypes))) * _nbytes(acc_shape, F32)
    scratch = [pltpu.VMEM(acc_shape, F32)] if n_red > 1 else []
    in_specs, out_shapes, out_specs, scratch, args = hosted.call_args(
        in_specs, [_big(out_shape, dt) for dt in out_dtypes], [out_spec] * len(out_dtypes), scratch, args)
    return hosted.results(pl.pallas_call(
        body, name=name, out_shape=out_shapes, grid=grid, in_specs=in_specs, out_specs=out_specs,
        input_output_aliases=aliases, scratch_shapes=scratch, compiler_params=_params(vmem + (8 << 20), len(grid)),
    )(*_in_hbm(args)))


def row(arr, width=None, cb=0, roff=0):
    return (arr, arr.shape[-1] if width is None else width, cb, roff)


def two_rows(first, second, limit):
    return (first, first.shape[-1], 0, 0, (second, limit))


def _row_inputs(rows, tm):
    specs, arrs, slots = [], [], []
    for d in rows:
        second, limit = d[4] if len(d) > 4 else (None, None)
        slots.append((len(arrs), limit))
        specs.append(_row_spec(d[:4], tm, limit))
        arrs.append(d[0])
        if second is not None:
            specs.append(pl.BlockSpec((tm, d[1]), lambda i, limit=limit: (jnp.maximum(i - limit, 0), 0)))
            arrs.append(second)

    def read(refs, i):
        vals = []
        for at, limit in slots:
            v = refs[at][...].astype(F32)
            vals.append(v if limit is None else jnp.where(i < limit, v, refs[at + 1][...].astype(F32)))
        return vals

    return specs, arrs, read


def _row_spec(desc, tm, limit=None):
    _, width, cb, roff = desc[:4]
    if limit is None:
        return pl.BlockSpec((tm, width), lambda i: (i + roff, cb))
    return pl.BlockSpec((tm, width), lambda i: (jnp.minimum(i, limit - 1) + roff, cb))


def _segmenter(tm, seq_len, n_lat):
    seg = lambda i: jnp.where(i * tm < n_lat, (i * tm) // seq_len, n_lat // seq_len)
    first = lambda i: jnp.where(i * tm < n_lat, (i * tm) % seq_len == 0, i * tm == n_lat)
    return seg, first


def rowwise(name, fn, rows, segs, params, outs, *, tm, n_tiles, seg_fn=None, rider=None):
    row_specs, row_arrs, read_rows = _row_inputs(rows, tm)
    n_r, n_s, n_p = len(row_arrs), len(segs), len(params)
    hosted = _Hosted(rider, n_r + n_s + n_p, len(outs), 0, (n_tiles,))

    def body(*refs):
        ins, out_refs, _ = hosted.split(refs)
        vals = read_rows(ins[:n_r], pl.program_id(0)) + [r[...] for r in ins[n_r:]]
        res = fn(*vals)
        for o_ref, v in zip(out_refs, res):
            o_ref[...] = v.astype(o_ref.dtype)
        hosted.finish()

    in_specs = list(row_specs)
    in_specs += [pl.BlockSpec((None, 1, s.shape[-1]), lambda i: (seg_fn(i), 0, 0)) for s in segs]
    in_specs += [pl.BlockSpec(p.shape, lambda i: (0, 0)) for p in params]
    vmem = sum(2 * tm * d[1] * 4 for d in rows) + sum(3 * tm * w * 4 for _, w, _ in outs) + sum(2 * p.size * 4 for p in params)
    in_specs, out_shapes, out_specs, scratch, args = hosted.call_args(
        in_specs, [_big((r, w), dt) for r, w, dt in outs],
        [pl.BlockSpec((tm, w), lambda i: (i, 0)) for _, w, _ in outs], [], row_arrs + list(segs) + list(params))
    return hosted.results(pl.pallas_call(
        body, name=name, grid=(n_tiles,), in_specs=in_specs, out_shape=out_shapes, out_specs=out_specs,
        scratch_shapes=scratch, compiler_params=_params(2 * vmem + (8 << 20)),
    )(*_in_hbm(args)), unwrap=False)


def rowwise_bwd(name, fn, rows, segs, params, cts, row_grads, *, tm, n_tiles, seg_fn=None, first_fn=None, adds=None,
                rider=None):
    adds = adds or {}
    need = [k for k, v in enumerate(row_grads) if v is not None]
    row_specs, row_arrs, read_rows = _row_inputs(rows, tm)
    n_r, n_s, n_p = len(row_arrs), len(segs), len(params)
    n_ct = sum(len(lst) for lst in cts)
    add_keys = sorted(adds)
    hosted = _Hosted(rider, n_r + n_s + n_p + n_ct + len(add_keys), len(need) + n_s + n_p, 0, (n_tiles,))

    def body(*refs):
        host_in, host_out, _ = hosted.split(refs)
        it = iter(list(host_in) + list(host_out))
        row_refs = [next(it) for _ in range(n_r)]
        seg_refs = [next(it) for _ in range(n_s)]
        par_refs = [next(it) for _ in range(n_p)]
        ct_refs = [[next(it) for _ in lst] for lst in cts]
        add_refs = {k: next(it) for k in add_keys}
        rg_refs = {k: next(it) for k in need}
        sg_refs = [next(it) for _ in range(n_s)]
        pg_refs = [next(it) for _ in range(n_p)]
        i = pl.program_id(0)
        rv = read_rows(row_refs, i)
        sv = [r[...] for r in seg_refs]
        pv = [r[...] for r in par_refs]

        def f(*args):
            rr = list(rv)
            for j, k in enumerate(need):
                rr[k] = args[j]
            return fn(*rr, *args[len(need):])

        _, vjp = jax.vjp(f, *[rv[k] for k in need], *sv, *pv)
        ctv = []
        for lst in ct_refs:
            acc = lst[0][...].astype(F32)
            for r in lst[1:]:
                acc = acc + r[...].astype(F32)
            ctv.append(acc)
        g = vjp(tuple(ctv))
        for j, k in enumerate(need):
            gv = g[j]
            if k in adds:
                lim = adds[k][1]
                av = add_refs[k][...].astype(F32)
                gv = gv + (av if lim is None else jnp.where(i < lim, av, 0.0))
            lim = row_grads[k][2]
            if lim is None:
                rg_refs[k][...] = gv.astype(rg_refs[k].dtype)
            else:
                @pl.when(i < lim)
                def _(gv=gv, k=k):
                    rg_refs[k][...] = gv.astype(rg_refs[k].dtype)
        if n_s:
            opens = first_fn(i)
            for ref, gv in zip(sg_refs, g[len(need):len(need) + n_s]):
                @pl.when(opens)
                def _(ref=ref, gv=gv):
                    ref[...] = gv

                @pl.when(jnp.logical_not(opens))
                def _(ref=ref, gv=gv):
                    ref[...] += gv
        for ref, gv in zip(pg_refs, g[len(need) + n_s:]):
            @pl.when(i == 0)
            def _(ref=ref, gv=gv):
                ref[...] = gv

            @pl.when(i > 0)
            def _(ref=ref, gv=gv):
                ref[...] += gv
        hosted.finish()

    seg_spec = lambda s: pl.BlockSpec((None, 1, s.shape[-1]), lambda i: (seg_fn(i), 0, 0))
    par_spec = lambda p: pl.BlockSpec(p.shape, lambda i: (0, 0))
    in_specs = list(row_specs) + [seg_spec(s) for s in segs] + [par_spec(p) for p in params]
    args = row_arrs + list(segs) + list(params)
    for lst in cts:
        in_specs += [_row_spec(d, tm) for d in lst]
        args += [d[0] for d in lst]
    for k in add_keys:
        in_specs.append(_row_spec(adds[k][0], tm, adds[k][1]))
        args.append(adds[k][0][0])
    out_shape, out_specs = [], []
    for k in need:
        n_rows, dt, lim = row_grads[k]
        out_shape.append(_big((n_rows, rows[k][1]), dt))
        out_specs.append(_row_spec((None, rows[k][1], 0, 0), tm, lim))
    for s in segs:
        out_shape.append(jax.ShapeDtypeStruct(s.shape, F32))
        out_specs.append(seg_spec(s))
    for p in params:
        out_shape.append(jax.ShapeDtypeStruct(p.shape, F32))
        out_specs.append(par_spec(p))
    vmem = sum(tm * d[1] * 4 for d in rows) * 6 + n_ct * tm * max(d[1] for d in rows) * 8
    in_specs, out_shape, out_specs, scratch, args = hosted.call_args(in_specs, out_shape, out_specs, [], args)
    return hosted.results(pl.pallas_call(
        body, name=name, grid=(n_tiles,), in_specs=in_specs, out_shape=out_shape, out_specs=out_specs,
        scratch_shapes=scratch, compiler_params=_params(vmem + (8 << 20)),
    )(*_in_hbm(args)), unwrap=False)


def _silu(v):
    return v * jax.nn.sigmoid(v)


def _rms(v, w):
    return v * lax.rsqrt(jnp.mean(v * v, axis=-1, keepdims=True) + EPS) * w


def fn_norm_mod(x, shift, scale, w):
    return (_rms(x, w) * (1.0 + scale) + shift,)


def fn_act(g, u):
    return (_silu(g) * u,)


def make_fn_resid(coef):
    def fn(x, f, gate):
        return (x + coef * gate * f,)
    return fn


def fn_silu_bias(v, b):
    return (_silu(v + b),)


def make_fn_gate_groupnorm(width):
    half = width // 2

    def fn(y_both, z, w):
        y = y_both * _silu(z)
        lane = lax.broadcasted_iota(jnp.int32, y.shape, 1)
        lo = lane < half
        sq = y * y
        s_lo = jnp.sum(jnp.where(lo, sq, 0.0), axis=-1, keepdims=True)
        s_hi = jnp.sum(jnp.where(lo, 0.0, sq), axis=-1, keepdims=True)
        r = jnp.where(lo, lax.rsqrt(s_lo / half + EPS), lax.rsqrt(s_hi / half + EPS))
        return (y * r * w,)
    return fn


def fn_glu(a, b):
    return (a * jax.nn.sigmoid(b),)


def fn_ln_silu(vw, vh, cb, lw, lb):
    v = jnp.concatenate([vw, vh], axis=-1) + cb
    mu = jnp.mean(v, axis=-1, keepdims=True)
    var = jnp.mean(jnp.square(v - mu), axis=-1, keepdims=True)
    return (_silu((v - mu) * lax.rsqrt(var + EPS) * lw + lb),)


def _col_tile(width):
    return width // 3 if width % (3 * LANES) == 0 else width


def mod_fwd(a_rows, w_shard, b_shard):
    n, d = a_rows.shape
    ws = w_shard.shape[1]
    tn = _col_tile(ws)

    def body(a_ref, w_ref, b_ref, o_ref):
        a = _silu(a_ref[...]).astype(BF16)
        o_ref[...] = jnp.dot(a, w_ref[...].astype(BF16), preferred_element_type=F32) + b_ref[...]

    return pl.pallas_call(
        body, name="mod_fwd", grid=(ws // tn,), out_shape=jax.ShapeDtypeStruct((n, ws), F32),
        in_specs=[pl.BlockSpec((n, d), lambda j: (0, 0)), pl.BlockSpec((d, tn), lambda j: (0, j)),
                  pl.BlockSpec((1, tn), lambda j: (0, j))],
        out_specs=pl.BlockSpec((n, tn), lambda j: (0, j)), compiler_params=_params(),
    )(a_rows, w_shard, b_shard)


def mod_bwd(a_rows, d_shard, d_full, w_shard, ctx_rows):
    n, d = a_rows.shape
    ws = w_shard.shape[1]
    tn = _col_tile(ws)
    n_ct = ws // tn

    def body(a_ref, ds_ref, df_ref, w_ref, gw_ref, gb_ref, q_ref):
        j = pl.program_id(0)
        a = _silu(a_ref[...])
        ds = ds_ref[...]
        gw_ref[...] = lax.dot_general(a, ds, _DIMS["tn"], precision=HI, preferred_element_type=F32)
        dctx = ds[ctx_rows[0]:ctx_rows[0] + 1, :]
        for r in ctx_rows[1:]:
            dctx = dctx + ds[r:r + 1, :]
        q = lax.dot_general(jnp.broadcast_to(dctx, (8, tn)), w_ref[...], _DIMS["nt"], precision=HI,
                            preferred_element_type=F32)

        @pl.when(j == 0)
        def _():
            q_ref[...] = q
            df = df_ref[...]
            acc = df[0:1, :]
            for r in range(1, n):
                acc = acc + df[r:r + 1, :]
            gb_ref[...] = acc

        @pl.when(j > 0)
        def _():
            q_ref[...] += q

    return pl.pallas_call(
        body, name="mod_bwd", grid=(n_ct,),
        out_shape=(jax.ShapeDtypeStruct((d, ws), F32), jax.ShapeDtypeStruct((1, d_full.shape[1]), F32),
                   jax.ShapeDtypeStruct((8, d), F32)),
        in_specs=[pl.BlockSpec((n, d), lambda j: (0, 0)), pl.BlockSpec((n, tn), lambda j: (0, j)),
                  pl.BlockSpec(d_full.shape, lambda j: (0, 0)), pl.BlockSpec((d, tn), lambda j: (0, j))],
        out_specs=(pl.BlockSpec((d, tn), lambda j: (0, j)), pl.BlockSpec((1, d_full.shape[1]), lambda j: (0, 0)),
                   pl.BlockSpec((8, d), lambda j: (0, 0))),
        compiler_params=_params(40 << 20),
    )(a_rows, d_shard, d_full, w_shard)


def _shifted(xs, d, tok, width):
    if d == 0:
        return xs
    n = xs.shape[0]
    sh = pltpu.roll(xs, (-d) % n, axis=0)
    return jnp.where((tok + d >= 0) & (tok + d < width), sh, 0.0)


def _placed(out_shape, place):
    if place is None:
        return out_shape, 0, 0, None
    return place


def tapsum_roll(name, x, xcb, w, wcb, *, seq_len, n_seq, row_blk_off, width, piece, cb, ncb, pad, flip, place=None):
    n_tap = w.shape[0]
    n_piece = seq_len // piece
    out_shape, o_rb, o_cb, into = _placed((n_seq * seq_len, ncb * cb), place)

    def body(x_ref, w_ref, *rest):
        o_ref = rest[-1]
        wv = w_ref[...]
        tok = lax.broadcasted_iota(jnp.int32, (piece, 1), 0) % width

        def do_piece(p, carry):
            start = pl.multiple_of(p * piece, piece)
            xs = x_ref[pl.ds(start, piece), :]
            acc = jnp.zeros_like(xs)
            for k in range(n_tap):
                d = pad - k if flip else k - pad
                acc = acc + wv[k:k + 1, :] * _shifted(xs, d, tok, width)
            o_ref[pl.ds(start, piece), :] = acc
            return carry

        lax.fori_loop(0, n_piece, do_piece, 0)

    extra = [] if into is None else [into]
    return pl.pallas_call(
        body, name=name, grid=(ncb, n_seq), out_shape=_big(out_shape, F32),
        in_specs=[pl.BlockSpec((seq_len, cb), lambda j, s: (row_blk_off + s, xcb + j)),
                  pl.BlockSpec((n_tap, cb), lambda j, s: (0, wcb + j))] + [pl.BlockSpec(memory_space=pl.ANY)] * len(extra),
        out_specs=pl.BlockSpec((seq_len, cb), lambda j, s: (o_rb + s, o_cb + j)),
        input_output_aliases={2: 0} if extra else {},
        compiler_params=_params(8 * seq_len * cb * 4 + (8 << 20), 2),
    )(*_in_hbm([x, w] + extra))


def tapgrad_roll(name, dy, dycb, dy_blk_off, x, xcb, x_blk_off, *, n_tap, seq_len, n_seq, width, piece, cb, ncb, pad):
    n_piece = seq_len // piece

    def body(dy_ref, x_ref, o_ref):
        @pl.when(pl.program_id(1) == 0)
        def _():
            o_ref[...] = jnp.zeros_like(o_ref)

        tok = lax.broadcasted_iota(jnp.int32, (piece, 1), 0) % width

        def do_piece(p, carry):
            start = pl.multiple_of(p * piece, piece)
            xs = x_ref[pl.ds(start, piece), :]
            dv = dy_ref[pl.ds(start, piece), :]
            for k in range(n_tap):
                o_ref[k:k + 1, :] += jnp.sum(dv * _shifted(xs, k - pad, tok, width), axis=0, keepdims=True)
            return carry

        lax.fori_loop(0, n_piece, do_piece, 0)

    return pl.pallas_call(
        body, name=name, grid=(ncb, n_seq), out_shape=jax.ShapeDtypeStruct((n_tap, ncb * cb), F32),
        in_specs=[pl.BlockSpec((seq_len, cb), lambda j, s: (dy_blk_off + s, dycb + j)),
                  pl.BlockSpec((seq_len, cb), lambda j, s: (x_blk_off + s, xcb + j))],
        out_specs=pl.BlockSpec((n_tap, cb), lambda j, s: (0, j)),
        compiler_params=_params(8 * seq_len * cb * 4 + (8 << 20), 2),
    )(*_in_hbm([dy, x]))


def tapsum_rows(name, x, xcb, w, wcb, *, seq_len, n_seq, cb, ncb, pad, flip, place=None):
    n_tap = w.shape[0]
    n_row = seq_len // GRID_W
    halo = pad * GRID_W
    out_shape, o_rb, o_cb, into = _placed((n_seq * seq_len, ncb * cb), place)

    def body(x_ref, w_ref, *rest):
        o_ref, xp = rest[-2:]
        xp[pl.ds(0, halo), :] = jnp.zeros((halo, cb), F32)
        xp[pl.ds(halo + seq_len, halo), :] = jnp.zeros((halo, cb), F32)
        xp[pl.ds(halo, seq_len), :] = x_ref[...]
        wv = w_ref[...]

        def do_row(r, carry):
            acc = jnp.zeros((GRID_W, cb), F32)
            for k in range(n_tap):
                d = pad - k if flip else k - pad
                acc = acc + wv[k:k + 1, :] * xp[pl.ds(pl.multiple_of((r + pad + d) * GRID_W, GRID_W), GRID_W), :]
            o_ref[pl.ds(pl.multiple_of(r * GRID_W, GRID_W), GRID_W), :] = acc
            return carry

        lax.fori_loop(0, n_row, do_row, 0)

    extra = [] if into is None else [into]
    return pl.pallas_call(
        body, name=name, grid=(ncb, n_seq), out_shape=_big(out_shape, F32),
        in_specs=[pl.BlockSpec((seq_len, cb), lambda j, s: (s, xcb + j)),
                  pl.BlockSpec((n_tap, cb), lambda j, s: (0, wcb + j))] + [pl.BlockSpec(memory_space=pl.ANY)] * len(extra),
        out_specs=pl.BlockSpec((seq_len, cb), lambda j, s: (o_rb + s, o_cb + j)),
        input_output_aliases={2: 0} if extra else {},
        scratch_shapes=[pltpu.VMEM((seq_len + 2 * halo, cb), F32)],
        compiler_params=_params(10 * seq_len * cb * 4 + (8 << 20), 2),
    )(*_in_hbm([x, w] + extra))


def tapgrad_rows(name, dy, dycb, x, xcb, *, n_tap, seq_len, n_seq, cb, ncb, pad):
    n_row = seq_len // GRID_W
    halo = pad * GRID_W

    def body(dy_ref, x_ref, o_ref, xp):
        @pl.when(pl.program_id(1) == 0)
        def _():
            o_ref[...] = jnp.zeros_like(o_ref)

        xp[pl.ds(0, halo), :] = jnp.zeros((halo, cb), F32)
        xp[pl.ds(halo + seq_len, halo), :] = jnp.zeros((halo, cb), F32)
        xp[pl.ds(halo, seq_len), :] = x_ref[...]

        def do_row(r, carry):
            dv = dy_ref[pl.ds(pl.multiple_of(r * GRID_W, GRID_W), GRID_W), :]
            for k in range(n_tap):
                xs = xp[pl.ds(pl.multiple_of((r + k) * GRID_W, GRID_W), GRID_W), :]
                o_ref[k:k + 1, :] += jnp.sum(dv * xs, axis=0, keepdims=True)
            return carry

        lax.fori_loop(0, n_row, do_row, 0)

    return pl.pallas_call(
        body, name=name, grid=(ncb, n_seq), out_shape=jax.ShapeDtypeStruct((n_tap, ncb * cb), F32),
        in_specs=[pl.BlockSpec((seq_len, cb), lambda j, s: (s, dycb + j)),
                  pl.BlockSpec((seq_len, cb), lambda j, s: (s, xcb + j))],
        out_specs=pl.BlockSpec((n_tap, cb), lambda j, s: (0, j)),
        scratch_shapes=[pltpu.VMEM((seq_len + 2 * halo, cb), F32)],
        compiler_params=_params(10 * seq_len * cb * 4 + (8 << 20), 2),
    )(*_in_hbm([dy, x]))


def _ssd_blocks(b, s, *, rev, n_ctx, n_lat, lat_blocks):
    if rev:
        return jnp.where(s < n_ctx, lat_blocks + b * n_ctx + (n_ctx - 1 - s), b * n_lat + (n_lat - 1 - (s - n_ctx)))
    return jnp.where(s < n_ctx, lat_blocks + b * n_ctx + s, b * n_lat + (s - n_ctx))


def _ssd_common(xbc, raw, dtb, alog, dsk, *, rev, ds, n_head):
    if rev:
        raw = pltpu.roll(raw, LANES - n_head, axis=1)
    pre = raw + dtb
    dt = jnp.maximum(pre, 0.0) + jnp.log1p(jnp.exp(-jnp.abs(pre)))
    sig = jax.nn.sigmoid(pre)
    a = -jnp.exp(alog)
    da = dt * a
    ri = lax.broadcasted_iota(jnp.int32, (CHUNK, CHUNK), 0)
    ci = lax.broadcasted_iota(jnp.int32, (CHUNK, CHUNK), 1)
    mask = (ci >= ri) if rev else (ci <= ri)
    tri = mask.astype(F32)
    tri_t = ((ci <= ri) if rev else (ci >= ri)).astype(F32)
    cs = jnp.dot(tri, da, precision=HI, preferred_element_type=F32)
    tot = jnp.sum(da, axis=0, keepdims=True)
    def wide(v):
        first = lax.broadcasted_iota(jnp.int32, (v.shape[0], LANES), 1) < HEAD_DIM
        return jnp.concatenate(
            [jnp.where(first, jnp.broadcast_to(v[:, 2 * p:2 * p + 1], first.shape),
                       jnp.broadcast_to(v[:, 2 * p + 1:2 * p + 2], first.shape)) for p in range(n_head // 2)], axis=1)

    cs_w, tot_w = wide(cs), wide(tot)
    xh = xbc[:, :ds]
    dt_w = wide(dt)
    return dict(
        dt=dt, sig=sig, a=a, cs=cs, cs_t=cs.T, tot=tot, mask=mask, tri_t=tri_t,
        e_w=jnp.exp(cs_w), wt_w=jnp.exp(tot_w - cs_w), dec_w=jnp.exp(tot_w), dt_w=dt_w, dsk_w=wide(dsk),
        xh=xh, xs_w=xh * dt_w, bm=xbc[:, ds:ds + 2 * N_STATE], cm=xbc[:, ds + 2 * N_STATE:ds + 4 * N_STATE])


def _decay(q, col):
    seg = q["cs"][:, col:col + 1] - q["cs_t"][col:col + 1, :]
    return jnp.exp(jnp.where(q["mask"], seg, -jnp.inf))


def _split_heads(v):
    lane = lax.broadcasted_iota(jnp.int32, v.shape, 1)
    return jnp.concatenate([jnp.where(lane < HEAD_DIM, v, 0.0), jnp.where(lane >= HEAD_DIM, v, 0.0)], axis=0)


def ssd_fwd(name, xbc, proj, dt_cb, dtb, alog, dsk, *, rev, n_ex, seq_len, ctx_len, ds, rider=None, add=None):
    n_head, half = ds // HEAD_DIM, ds // 2
    n_ctx, n_lat = ctx_len // CHUNK, seq_len // CHUNK
    n_step = n_ctx + n_lat
    blk = functools.partial(_ssd_blocks, rev=rev, n_ctx=n_ctx, n_lat=n_lat, lat_blocks=n_ex * n_lat)
    xw = xbc.shape[1]

    def y_blk(b, s):
        sl = jnp.maximum(s, n_ctx) - n_ctx
        return b * n_lat + ((n_lat - 1 - sl) if rev else sl)

    hosted = _Hosted(rider, 5 + (add is not None), 2, 1, (n_ex, n_step))

    def body(*refs):
        (xbc_ref, dt_ref, dtb_ref, alog_ref, dsk_ref, *add_ref), (y_ref, hs_ref), (h_scr,) = hosted.split(refs)

        @pl.when(pl.program_id(1) == 0)
        def _():
            h_scr[...] = jnp.zeros_like(h_scr)

        q = _ssd_common(xbc_ref[...], dt_ref[...], dtb_ref[...], alog_ref[...], dsk_ref[...], rev=rev, ds=ds, n_head=n_head)
        h = h_scr[...]
        hs_ref[...] = h
        for g in range(2):
            lo = g * half
            bg = q["bm"][:, g * N_STATE:(g + 1) * N_STATE].astype(BF16)
            cg = q["cm"][:, g * N_STATE:(g + 1) * N_STATE].astype(BF16)
            scores = lax.dot_general(cg, bg, _DIMS["nt"], preferred_element_type=F32)
            hg = h[:, lo:lo + half]
            off = jnp.dot(cg, hg.astype(BF16), preferred_element_type=F32)
            for j in range(half // LANES):
                c0 = (lo + j * LANES) // HEAD_DIM
                ln = slice(lo + j * LANES, lo + (j + 1) * LANES)
                p_cat = jnp.concatenate([scores * _decay(q, c0), scores * _decay(q, c0 + 1)], axis=1).astype(BF16)
                diag = jnp.dot(p_cat, _split_heads(q["xs_w"][:, ln]).astype(BF16), preferred_element_type=F32)
                y_ref[:, ln] = (diag + q["e_w"][:, ln] * off[:, j * LANES:(j + 1) * LANES]
                                + q["dsk_w"][:, ln] * q["xh"][:, ln] + (add_ref[0][:, ln] if add_ref else 0.0))
            v = (q["wt_w"][:, lo:lo + half] * q["xs_w"][:, lo:lo + half]).astype(BF16)
            h_scr[:, lo:lo + half] = (q["dec_w"][:, lo:lo + half] * hg
                                      + lax.dot_general(bg, v, _DIMS["tn"], preferred_element_type=F32))
        hosted.finish()

    vec = pl.BlockSpec((1, LANES), lambda b, s: (0, 0))
    in_specs, out_shape, out_specs, scratch, args = hosted.call_args(
        [pl.BlockSpec((CHUNK, xw), lambda b, s: (blk(b, s), 0)),
         pl.BlockSpec((CHUNK, LANES), lambda b, s: (blk(b, s), dt_cb)), vec, vec, vec]
        + [pl.BlockSpec((CHUNK, ds), lambda b, s: (y_blk(b, s), 0))] * (add is not None),
        (_big((n_ex * seq_len, ds), F32), _big((n_ex, n_step, N_STATE, ds), F32)),
        (pl.BlockSpec((CHUNK, ds), lambda b, s: (y_blk(b, s), 0)),
         pl.BlockSpec((None, None, N_STATE, ds), lambda b, s: (b, s, 0, 0))),
        [pltpu.VMEM((N_STATE, ds), F32)], [xbc, proj, dtb, alog, dsk] + ([] if add is None else [add]))
    return hosted.results(pl.pallas_call(
        body, name=name, grid=(n_ex, n_step), out_shape=out_shape, in_specs=in_specs, out_specs=out_specs,
        scratch_shapes=scratch, compiler_params=_params(40 << 20, 2),
    )(*_in_hbm(args)))


def ssd_bwd(name, xbc, proj, dt_cb, hs, dy, dtb, alog, dsk, *, rev, n_ex, seq_len, ctx_len, ds, rider=None, add=None):
    n_head, half = ds // HEAD_DIM, ds // 2
    n_ctx, n_lat = ctx_len // CHUNK, seq_len // CHUNK
    n_step = n_ctx + n_lat
    n_tok = n_ex * (seq_len + ctx_len)
    blk0 = functools.partial(_ssd_blocks, rev=rev, n_ctx=n_ctx, n_lat=n_lat, lat_blocks=n_ex * n_lat)
    step = lambda sp: n_step - 1 - sp
    blk = lambda b, sp: blk0(b, step(sp))
    xw = xbc.shape[1]

    def dy_blk(b, sp):
        sl = jnp.maximum(step(sp), n_ctx) - n_ctx
        return b * n_lat + ((n_lat - 1 - sl) if rev else sl)

    hosted = _Hosted(rider, 7 + (add is not None), 5, 1, (n_ex, n_step))

    def body(*refs):
        ((xbc_ref, dt_ref, hs_ref, dy_ref, dtb_ref, alog_ref, dsk_ref, *add_ref),
         (dxbc_ref, ddt_ref, dalog_ref, ddtb_ref, ddsk_ref), (dh_scr,)) = hosted.split(refs)
        b, sp = pl.program_id(0), pl.program_id(1)
        more = (lambda cols: add_ref[0][:, cols]) if add_ref else (lambda cols: 0.0)

        @pl.when(sp == 0)
        def _():
            dh_scr[...] = jnp.zeros_like(dh_scr)

        @pl.when((sp == 0) & (b == 0))
        def _():
            dalog_ref[...] = jnp.zeros_like(dalog_ref)
            ddtb_ref[...] = jnp.zeros_like(ddtb_ref)
            ddsk_ref[...] = jnp.zeros_like(ddsk_ref)

        q = _ssd_common(xbc_ref[...], dt_ref[...], dtb_ref[...], alog_ref[...], dsk_ref[...], rev=rev, ds=ds, n_head=n_head)
        h = hs_ref[...]
        d_y = jnp.where(step(sp) >= n_ctx, dy_ref[...], 0.0)
        dh_next = dh_scr[...]
        lane_row = lax.broadcasted_iota(jnp.int32, (1, LANES), 1)
        d_cs = jnp.zeros((CHUNK, LANES), F32)
        dxs_parts, de_parts, dwt_parts, ddec_parts = [], [], [], []
        for g in range(2):
            lo = g * half
            gs = slice(lo, lo + half)
            bg = q["bm"][:, g * N_STATE:(g + 1) * N_STATE].astype(BF16)
            cg = q["cm"][:, g * N_STATE:(g + 1) * N_STATE].astype(BF16)
            scores = lax.dot_general(cg, bg, _DIMS["nt"], preferred_element_type=F32)
            hg, dyg, dhn = h[:, gs], d_y[:, gs], dh_next[:, gs]
            off = jnp.dot(cg, hg.astype(BF16), preferred_element_type=F32)
            d_off = (q["e_w"][:, gs] * dyg).astype(BF16)
            de_parts.append(dyg * off)
            d_c = lax.dot_general(d_off, hg.astype(BF16), _DIMS["nt"], preferred_element_type=F32)
            dh_scr[:, gs] = (lax.dot_general(cg, d_off, _DIMS["tn"], preferred_element_type=F32)
                             + q["dec_w"][:, gs] * dhn)
            b_dh = jnp.dot(bg, dhn.astype(BF16), preferred_element_type=F32)
            v = q["wt_w"][:, gs] * q["xs_w"][:, gs]
            d_b = lax.dot_general(v.astype(BF16), dhn.astype(BF16), _DIMS["nt"], preferred_element_type=F32)
            dwt_parts.append(q["xs_w"][:, gs] * b_dh)
            ddec_parts.append(jnp.sum(hg * dhn, axis=0, keepdims=True))
            d_scores = jnp.zeros((CHUNK, CHUNK), F32)
            for j in range(half // LANES):
                c0 = (lo + j * LANES) // HEAD_DIM
                ln = slice(lo + j * LANES, lo + (j + 1) * LANES)
                l0, l1 = _decay(q, c0), _decay(q, c0 + 1)
                p0, p1 = scores * l0, scores * l1
                dy_st = _split_heads(d_y[:, ln]).astype(BF16)
                d_p = lax.dot_general(dy_st, q["xs_w"][:, ln].astype(BF16), _DIMS["nt"], preferred_element_type=F32)
                d_p0, d_p1 = d_p[:CHUNK], d_p[CHUNK:]
                d_scores = d_scores + d_p0 * l0 + d_p1 * l1
                for col, t in ((c0, d_p0 * p0), (c0 + 1, d_p1 * p1)):
                    d_cs = d_cs + jnp.sum(t - t.T, axis=1, keepdims=True) * (lane_row == col).astype(F32)
                p_st = jnp.concatenate([p0, p1], axis=0).astype(BF16)
                dxs_parts.append(lax.dot_general(p_st, dy_st, _DIMS["tn"], preferred_element_type=F32)
                                 + q["wt_w"][:, ln] * b_dh[:, j * LANES:(j + 1) * LANES])
            d_sc = d_scores.astype(BF16)
            d_c = d_c + jnp.dot(d_sc, bg, preferred_element_type=F32)
            d_b = d_b + lax.dot_general(d_sc, cg, _DIMS["tn"], preferred_element_type=F32)
            b_cols, c_cols = slice(ds + g * N_STATE, ds + (g + 1) * N_STATE), slice(ds + (2 + g) * N_STATE, ds + (3 + g) * N_STATE)
            dxbc_ref[:, b_cols] = d_b + more(b_cols)
            dxbc_ref[:, c_cols] = d_c + more(c_cols)
        d_xs = jnp.concatenate(dxs_parts, axis=1)
        narrow_m = (lax.broadcasted_iota(jnp.int32, (ds, LANES), 0) // HEAD_DIM
                    == lax.broadcasted_iota(jnp.int32, (ds, LANES), 1)).astype(BF16)
        rows8 = lambda v: jnp.broadcast_to(v, (8, ds))
        stacked = jnp.concatenate(
            [jnp.concatenate(dwt_parts, axis=1), jnp.concatenate(de_parts, axis=1), d_xs * q["xh"],
             rows8(jnp.concatenate(ddec_parts, axis=1)), rows8(jnp.sum(d_y * q["xh"], axis=0, keepdims=True))], axis=0)
        hi = stacked.astype(BF16)
        lo = (stacked - hi.astype(F32)).astype(BF16)
        sums = (jnp.dot(hi, narrow_m, preferred_element_type=F32) + jnp.dot(lo, narrow_m, preferred_element_type=F32))
        n_wt, n_e, n_xs = sums[:CHUNK], sums[CHUNK:2 * CHUNK], sums[2 * CHUNK:3 * CHUNK]
        n_dec, n_dsk = sums[3 * CHUNK:3 * CHUNK + 1], sums[3 * CHUNK + 8:3 * CHUNK + 9]
        e, wt, dec = jnp.exp(q["cs"]), jnp.exp(q["tot"] - q["cs"]), jnp.exp(q["tot"])
        d_wt = n_wt * wt
        d_cs = d_cs + n_e * e - d_wt
        d_tot = jnp.sum(d_wt, axis=0, keepdims=True) + n_dec * dec
        d_da = jnp.dot(q["tri_t"], d_cs, precision=HI, preferred_element_type=F32) + d_tot
        d_dt = d_da * q["a"] + n_xs
        dxbc_ref[:, :ds] = d_xs * q["dt_w"] + q["dsk_w"] * d_y + more(slice(0, ds))
        dalog_ref[...] += jnp.sum(d_da * q["dt"], axis=0, keepdims=True) * q["a"]
        d_raw = d_dt * q["sig"]
        ddtb_ref[...] += jnp.sum(d_raw, axis=0, keepdims=True)
        ddsk_ref[...] += n_dsk
        ddt_ref[...] = pltpu.roll(d_raw, n_head, axis=1) if rev else d_raw
        hosted.finish()

    vec = pl.BlockSpec((1, LANES), lambda b, s: (0, 0))
    vec_shape = jax.ShapeDtypeStruct((1, LANES), F32)
    in_specs, out_shape, out_specs, scratch, args = hosted.call_args(
        [pl.BlockSpec((CHUNK, xw), lambda b, s: (blk(b, s), 0)),
         pl.BlockSpec((CHUNK, LANES), lambda b, s: (blk(b, s), dt_cb)),
         pl.BlockSpec((None, None, N_STATE, ds), lambda b, s: (b, step(s), 0, 0)),
         pl.BlockSpec((CHUNK, ds), lambda b, s: (dy_blk(b, s), 0)), vec, vec, vec]
        + [pl.BlockSpec((CHUNK, xw), lambda b, s: (blk(b, s), 0))] * (add is not None),
        (_big((n_tok, xw), F32), _big((n_tok, LANES), F32), vec_shape, vec_shape, vec_shape),
        (pl.BlockSpec((CHUNK, xw), lambda b, s: (blk(b, s), 0)),
         pl.BlockSpec((CHUNK, LANES), lambda b, s: (blk(b, s), 0)), vec, vec, vec),
        [pltpu.VMEM((N_STATE, ds), F32)], [xbc, proj, hs, dy, dtb, alog, dsk] + ([] if add is None else [add]))
    return hosted.results(pl.pallas_call(
        body, name=name, grid=(n_ex, n_step), out_shape=out_shape, in_specs=in_specs, out_specs=out_specs,
        scratch_shapes=scratch, compiler_params=_params(48 << 20, 2),
    )(*_in_hbm(args)))


def final_loss(x3, target, w, *, tm):
    n, d = x3.shape

    def body(x_ref, t_ref, w_ref, dx_ref, dw_ref, loss_ref):
        i = pl.program_id(0)
        t = t_ref[...]

        def per_feature(xv, wv):
            err = _rms(xv, wv) - t
            return 0.5 * jnp.sum(err * err, axis=0, keepdims=True) / d

        lv, vjp = jax.vjp(per_feature, x_ref[...], w_ref[...])
        dx, dw = vjp(jnp.ones_like(lv))
        dx_ref[...] = dx

        @pl.when(i == 0)
        def _():
            dw_ref[...] = dw
            loss_ref[...] = lv

        @pl.when(i > 0)
        def _():
            dw_ref[...] += dw
            loss_ref[...] += lv

    tile = pl.BlockSpec((tm, d), lambda i: (i, 0))
    vec = pl.BlockSpec((1, d), lambda i: (0, 0))
    return pl.pallas_call(
        body, name="final_loss", grid=(n // tm,), in_specs=[tile, tile, vec],
        out_shape=(jax.ShapeDtypeStruct((n, d), F32), jax.ShapeDtypeStruct((1, d), F32), jax.ShapeDtypeStruct((1, d), F32)),
        out_specs=(tile, vec, vec), compiler_params=_params(tm * d * 4 * 16 + (8 << 20)),
    )(x3, target, w)


def sum_slots(name, arr, out_dtype=F32):
    n_slot, n_row, width = arr.shape
    tm = _row_tile(n_row, width * n_slot, mult=16)

    def body(a_ref, o_ref):
        acc = a_ref[0].astype(F32)
        for j in range(1, n_slot):
            acc = acc + a_ref[j].astype(F32)
        o_ref[...] = acc.astype(o_ref.dtype)

    return pl.pallas_call(
        body, name=name, grid=(n_row // tm,), out_shape=jax.ShapeDtypeStruct((n_row, width), out_dtype),
        in_specs=[pl.BlockSpec((n_slot, tm, width), lambda i: (0, i, 0))],
        out_specs=pl.BlockSpec((tm, width), lambda i: (i, 0)), compiler_params=_params(),
    )(arr)


def adamw(name, w, g_slots, m, v):
    n_slot, n_row, width = g_slots.shape
    tm = _row_tile(n_row, width * 2)
    if g_slots.dtype == BF16 and tm % 16:
        tm16 = _row_tile(n_row, width * 2, mult=16)
        if tm16 % 16 == 0:
            tm = tm16
        else:
            g_slots = g_slots.astype(F32)

    def body(w_ref, g_ref, m_ref, v_ref, go_ref, d_ref, mo_ref, vo_ref):
        g = g_ref[0].astype(F32)
        for j in range(1, n_slot):
            g = g + g_ref[j].astype(F32)
        m2 = ADAM_B1 * m_ref[...] + (1.0 - ADAM_B1) * g
        v2 = ADAM_B2 * v_ref[...] + (1.0 - ADAM_B2) * jnp.square(g)
        m_hat = m2 / (1.0 - ADAM_B1 ** ADAM_STEP)
        v_hat = v2 / (1.0 - ADAM_B2 ** ADAM_STEP)
        go_ref[...] = g
        d_ref[...] = -ADAM_LR * (m_hat / (jnp.sqrt(v_hat) + ADAM_EPS) + ADAM_WD * w_ref[...])
        mo_ref[...] = m2
        vo_ref[...] = v2

    tile = pl.BlockSpec((tm, width), lambda i: (i, 0))
    shape = jax.ShapeDtypeStruct((n_row, width), F32)
    return pl.pallas_call(
        body, name=name, grid=(n_row // tm,), out_shape=(shape,) * 4,
        in_specs=[tile, pl.BlockSpec((n_slot, tm, width), lambda i: (0, i, 0)), tile, tile],
        out_specs=(tile,) * 4, compiler_params=_params(),
    )(w, g_slots, m, v)


def cctx_grad(q_all, c_ctx_row):
    d = c_ctx_row.shape[1]

    def body(q_ref, c_ref, o_ref):
        acc = q_ref[0, 0:1, :]
        for j in (2, 4, 6):
            acc = acc + q_ref[j, 0:1, :]
        _, vjp = jax.vjp(_silu, c_ref[...])
        o_ref[...] = vjp(acc)[0]

    return pl.pallas_call(
        body, name="cctx_grad", out_shape=jax.ShapeDtypeStruct((1, d), F32),
    )(q_all, c_ctx_row)


def loss_total(pack_sum, d):
    def body(p_ref, o_ref):
        o_ref[...] = jnp.sum(p_ref[:, 0:d], axis=1, keepdims=True)

    return pl.pallas_call(
        body, name="loss_total", out_shape=jax.ShapeDtypeStruct((1, 1), F32),
    )(pack_sum)


class _Plan:
    def __init__(self):
        self.builders, self.got = {}, {}

    def on(self, host, key, builder):
        self.builders.setdefault(host, []).append((key, builder))

    def run(self, host, fn, *args, **kw):
        if host not in self.builders:
            return fn(host, *args, **kw)
        keys, riders = zip(*[(key, builder(self)) for key, builder in self.builders[host]])
        res, landed = fn(host, *args, rider=Riders(riders), **kw)
        for key, r in zip(keys, riders):
            self.got[key], landed = landed[:r.n], landed[r.n:]
        return res


def _val(w):
    return w() if callable(w) else w


def _matmul_tile(n_rows, tm):
    return 2 * tm if n_rows % (2 * tm) == 0 else tm


def _ffn_fwd(plan, tag, xin, n_rows, tm, seg_fn, shift, scale, gate, norm_w, wg, wu, wd, fuse_gate_up=False):
    d = xin[1]
    n_tiles = n_rows // tm
    (h,) = plan.run(f"{tag}_norm", rowwise, fn_norm_mod, [xin], [shift, scale], [norm_w], [(n_rows, d, BF16)],
                    tm=tm, n_tiles=n_tiles, seg_fn=seg_fn)
    tmm = _matmul_tile(n_rows, tm)
    if fuse_gate_up:
        g, u, act = plan.run(f"{tag}_gate_up", matmul, [(h, _val(wg)), (h, _val(wu))], "nn", b_ch=True, out_ch=True,
                             tm=min(tm, 256), fold=True,
                             post=([], lambda ag, au: (ag, au, fn_act(ag, au)[0]), [BF16, BF16, BF16]))
    else:
        g = plan.run(f"{tag}_gate", matmul, [(h, _val(wg))], "nn", out_dtype=BF16, b_ch=True, out_ch=True, tm=tmm)
        u, act = plan.run(f"{tag}_up", matmul, [(h, _val(wu))], "nn", b_ch=True, out_ch=True, tm=tm, fold=True,
                          post=([g], lambda acc, gv: (acc, fn_act(gv, acc)[0]), [BF16, BF16]))
    f = plan.run(f"{tag}_down", matmul, [(act, _val(wd))], "nn", a_ch=True, b_ch=True, tm=tmm, fold=True)
    (xo,) = plan.run(f"{tag}_resid", rowwise, make_fn_resid(0.5), [xin, row(f)], [gate], [], [(n_rows, d, F32)],
                     tm=tm, n_tiles=n_tiles, seg_fn=seg_fn)
    return xo, (h, g, u, act, f)


def _ffn_bwd(plan, tag, d_xo, saved, xin, n_rows, tm, seg_fn, first_fn, shift, scale, gate, norm_w, wg, wu, wd, dx_rows, dx_limit):
    h, g, u, act, f = saved
    d = xin[1]
    n_tiles = n_rows // tm
    n_ch, _, n_hid = g.shape
    d_f, d_gate = plan.run(f"{tag}_resid_bwd", rowwise_bwd, make_fn_resid(0.5), [xin, row(f)], [gate], [], [[row(d_xo)]],
                           [None, (n_rows, BF16, None)], tm=tm, n_tiles=n_tiles, seg_fn=seg_fn, first_fn=first_fn)
    tmm = _matmul_tile(n_rows, tm)
    def act_vjp(d_act, gv, uv):
        s = jax.nn.sigmoid(gv)
        gs = gv * s
        return d_act * uv * (s + gs * (1.0 - s)), d_act * gs
    d_g, d_u = plan.run(f"{tag}_down_dx", matmul, [(d_f, wd)], "nt", b_ch=True, out_ch=True, tm=tmm,
                        post=([g, u], act_vjp, [BF16, BF16]))
    plan.got[f"{tag}_d_wd"] = plan.run(f"{tag}_down_dw", matmul, [(act, d_f)], "tn", out_dtype=BF16, a_ch=True, out_ch=True, tm=tmm)
    d_h = plan.run(f"{tag}_up_dx", matmul, [(d_g, wg), (d_u, wu)], "nt", a_ch=True, b_ch=True, tm=tmm)
    plan.got[f"{tag}_d_wg"] = plan.run(f"{tag}_gate_dw", matmul, [(d_g, h)], "tn", out_dtype=BF16, a_ch=True, out_ch=True, tm=tmm)
    plan.got[f"{tag}_d_wu"] = plan.run(f"{tag}_up_dw", matmul, [(d_u, h)], "tn", out_dtype=BF16, a_ch=True, out_ch=True, tm=tmm)
    d_x, d_shift, d_scale, d_nw = plan.run(
        f"{tag}_norm_bwd", rowwise_bwd, fn_norm_mod, [xin], [shift, scale], [norm_w], [[row(d_h)]], [(dx_rows, F32, dx_limit)],
        tm=tm, n_tiles=n_tiles, seg_fn=seg_fn, first_fn=first_fn, adds={0: (row(d_xo), None)})
    return d_x, (d_shift, d_scale, d_gate), d_nw


def kernel(x, c, ctx, c_ctx, w_mod, b_mod, norm_ffn1, ffn1_gate, ffn1_up, ffn1_down, norm_mix, w_in, ssm_conv_w, ssm_conv_b, dt_bias_fwd, dt_bias_bwd, a_log_fwd, a_log_bwd, ssm_d, ssm_norm_w, cconv_w, cconv_b, cconv_ln_w, cconv_ln_b, w_out, norm_ffn2, ffn2_gate, ffn2_up, ffn2_down, final_norm, loss_target, m_c_ctx, m_w_mod, m_b_mod, m_norm_ffn1, m_ffn1_gate, m_ffn1_up, m_ffn1_down, m_norm_mix, m_w_in, m_ssm_conv_w, m_ssm_conv_b, m_dt_bias_fwd, m_dt_bias_bwd, m_a_log_fwd, m_a_log_bwd, m_ssm_d, m_ssm_norm_w, m_cconv_w, m_cconv_b, m_cconv_ln_w, m_cconv_ln_b, m_w_out, m_norm_ffn2, m_ffn2_gate, m_ffn2_up, m_ffn2_down, m_final_norm, v_c_ctx, v_w_mod, v_b_mod, v_norm_ffn1, v_ffn1_gate, v_ffn1_up, v_ffn1_down, v_norm_mix, v_w_in, v_ssm_conv_w, v_ssm_conv_b, v_dt_bias_fwd, v_dt_bias_bwd, v_a_log_fwd, v_a_log_bwd, v_ssm_d, v_ssm_norm_w, v_cconv_w, v_cconv_b, v_cconv_ln_w, v_cconv_ln_b, v_w_out, v_norm_ffn2, v_ffn2_gate, v_ffn2_up, v_ffn2_down, v_final_norm):
    weights = dict(c_ctx=c_ctx, w_mod=w_mod, b_mod=b_mod, norm_ffn1=norm_ffn1, ffn1_gate=ffn1_gate, ffn1_up=ffn1_up, ffn1_down=ffn1_down, norm_mix=norm_mix, w_in=w_in, ssm_conv_w=ssm_conv_w, ssm_conv_b=ssm_conv_b, dt_bias_fwd=dt_bias_fwd, dt_bias_bwd=dt_bias_bwd, a_log_fwd=a_log_fwd, a_log_bwd=a_log_bwd, ssm_d=ssm_d, ssm_norm_w=ssm_norm_w, cconv_w=cconv_w, cconv_b=cconv_b, cconv_ln_w=cconv_ln_w, cconv_ln_b=cconv_ln_b, w_out=w_out, norm_ffn2=norm_ffn2, ffn2_gate=ffn2_gate, ffn2_up=ffn2_up, ffn2_down=ffn2_down, final_norm=final_norm)
    mom1 = dict(c_ctx=m_c_ctx, w_mod=m_w_mod, b_mod=m_b_mod, norm_ffn1=m_norm_ffn1, ffn1_gate=m_ffn1_gate, ffn1_up=m_ffn1_up, ffn1_down=m_ffn1_down, norm_mix=m_norm_mix, w_in=m_w_in, ssm_conv_w=m_ssm_conv_w, ssm_conv_b=m_ssm_conv_b, dt_bias_fwd=m_dt_bias_fwd, dt_bias_bwd=m_dt_bias_bwd, a_log_fwd=m_a_log_fwd, a_log_bwd=m_a_log_bwd, ssm_d=m_ssm_d, ssm_norm_w=m_ssm_norm_w, cconv_w=m_cconv_w, cconv_b=m_cconv_b, cconv_ln_w=m_cconv_ln_w, cconv_ln_b=m_cconv_ln_b, w_out=m_w_out, norm_ffn2=m_norm_ffn2, ffn2_gate=m_ffn2_gate, ffn2_up=m_ffn2_up, ffn2_down=m_ffn2_down, final_norm=m_final_norm)
    mom2 = dict(c_ctx=v_c_ctx, w_mod=v_w_mod, b_mod=v_b_mod, norm_ffn1=v_norm_ffn1, ffn1_gate=v_ffn1_gate, ffn1_up=v_ffn1_up, ffn1_down=v_ffn1_down, norm_mix=v_norm_mix, w_in=v_w_in, ssm_conv_w=v_ssm_conv_w, ssm_conv_b=v_ssm_conv_b, dt_bias_fwd=v_dt_bias_fwd, dt_bias_bwd=v_dt_bias_bwd, a_log_fwd=v_a_log_fwd, a_log_bwd=v_a_log_bwd, ssm_d=v_ssm_d, ssm_norm_w=v_ssm_norm_w, cconv_w=v_cconv_w, cconv_b=v_cconv_b, cconv_ln_w=v_cconv_ln_w, cconv_ln_b=v_cconv_ln_b, w_out=v_w_out, norm_ffn2=v_norm_ffn2, ffn2_gate=v_ffn2_gate, ffn2_up=v_ffn2_up, ffn2_down=v_ffn2_down, final_norm=v_final_norm)
    order = list(weights)

    n_ex, seq_len, d = x.shape
    ctx_len = ctx.shape[1]
    ds = d
    n_head = ds // HEAD_DIM
    xw = ds + 4 * N_STATE
    n_lat, n_ctx_rows = n_ex * seq_len, n_ex * ctx_len
    n_tok = n_lat + n_ctx_rows
    tm = math.gcd(math.gcd(512, seq_len), n_ctx_rows)
    seg_all, first_all = _segmenter(tm, seq_len, n_lat)
    lat_tiles = n_lat // tm

    xi, yi, ci = lax.axis_index("x"), lax.axis_index("y"), lax.axis_index("c")
    me, chip = 4 * xi + 2 * yi + ci, 2 * xi + yi

    (c_all,) = exchange("gather_c", [c], "all8")
    n_all = 8 * n_ex
    n_cond = -(-(n_all + 1) // 8) * 8
    cond = jnp.concatenate([c_all.reshape(n_all, d), c_ctx[None, :], jnp.zeros((n_cond - n_all - 1, d), F32)])
    mod_w = w_mod.shape[2]
    b_shard = lax.dynamic_slice(b_mod, (0, chip * mod_w), (1, mod_w))
    (mod_g,) = exchange("gather_mod", [mod_fwd(cond, w_mod[0], b_shard)], "chips")
    mod_full = mod_g.transpose(1, 0, 2).reshape(n_cond, N_CHIPS * mod_w)
    mod_mine = lax.dynamic_slice(mod_full, (me * n_ex, 0), (n_ex, 9 * d)).reshape(n_ex, 9, d)
    mod_ctx = mod_full[n_all].reshape(9, d)
    tabs = [jnp.concatenate([mod_mine[:, j], mod_ctx[j][None]])[:, None, :] for j in range(9)]
    lat = lambda t: t[:n_ex]

    bf = lambda w: w[0].astype(BF16)
    plan = _Plan()
    gather = lambda *ws: (lambda p: Rider(list(ws), "chips"))
    plan.on("ffn1_norm", "wg1", gather(bf(ffn1_gate)))
    plan.on("ffn1_gate", "wu1", gather(bf(ffn1_up)))
    plan.on("ffn1_up", "wd1", gather(bf(ffn1_down)))
    win_cut = d * 5 // 8
    plan.on("ffn1_down", "win_a", gather(bf(w_in)[:win_cut]))
    plan.on("ffn1_resid", "win_b", gather(bf(w_in)[win_cut:], ssm_conv_w[0], cconv_w[0]))
    xt = two_rows(x.reshape(n_lat, d), ctx.reshape(n_ctx_rows, d), lat_tiles)
    x1, saved1 = _ffn_fwd(plan, "ffn1", xt, n_tok, tm, seg_all, tabs[0], tabs[1], tabs[2], norm_ffn1,
                          lambda: plan.got["wg1"][0], lambda: plan.got["wu1"][0], lambda: plan.got["wd1"][0])
    (wg1,), (wu1,), (wd1,), (win_a,), (win_b, w5_g, w31_g) = (plan.got[k] for k in ("wg1", "wu1", "wd1", "win_a", "win_b"))
    win_g = jnp.concatenate([win_a, win_b], axis=1)
    unshard_cols = lambda t: t.transpose(1, 0, 2).reshape(t.shape[1], N_CHIPS * t.shape[2])
    win = unshard_cols(win_g)
    o_x, o_dt, o_glu = ds, ds + xw, ds + xw + 2 * n_head
    w_z, w_xbc, w_dt = win[:, :ds], win[:, o_x:o_dt], win[:, o_dt:o_glu]
    w_ga, w_gb = win[:, o_glu:o_glu + d], win[:, o_glu + d:]
    w_dtp = jnp.concatenate([w_dt, jnp.zeros((d, LANES - 2 * n_head), BF16)], axis=1)
    w_cat = jnp.concatenate([w_z, w_ga, w_gb, w_xbc, w_dtp], axis=1)
    cbw = d // 2
    xbc_cb, dt_cb = 3 * d // cbw, (3 * d + xw) // LANES
    w5, w31 = unshard_cols(w5_g), unshard_cols(w31_g)
    pad_vec = lambda v: jnp.concatenate([v.reshape(1, -1), jnp.zeros((1, LANES - v.size), F32)], axis=1)
    dtb_f, dtb_b, alog_f, alog_b = map(pad_vec, (dt_bias_fwd, dt_bias_bwd, a_log_fwd, a_log_bwd))
    dsk_f, dsk_b = pad_vec(ssm_d), jnp.zeros((1, LANES), F32)

    (h2,) = rowwise("mix_norm", fn_norm_mod, [row(x1)], [tabs[3], tabs[4]], [norm_mix], [(n_tok, d, BF16)],
                    tm=tm, n_tiles=n_tok // tm, seg_fn=seg_all)
    proj, (wg2,) = matmul("mix_proj", [(h2, w_cat)], "nn", tm=min(tm, 256), rider=Rider([bf(ffn2_gate)], "chips"))
    def conv5(name, src, cb0, flip):
        out = None
        for part, seq, off in (("lat", seq_len, 0), ("ctx", ctx_len, n_lat // ctx_len)):
            out = tapsum_roll(f"{name}_{part}", src, cb0, w5, 0, seq_len=seq, n_seq=n_ex, row_blk_off=off, width=seq,
                              piece=seq, cb=cbw, ncb=xw // cbw, pad=w5.shape[0] // 2, flip=flip,
                              place=((n_tok, xw), off, 0, out))
        return out

    craw = conv5("xbc_conv", proj, xbc_cb, False)
    (xbc,) = rowwise("xbc_silu", fn_silu_bias, [row(craw)], [], [ssm_conv_b], [(n_tok, xw, F32)], tm=tm, n_tiles=n_tok // tm)
    ssd = dict(n_ex=n_ex, seq_len=seq_len, ctx_len=ctx_len, ds=ds)
    (y_f, hs_f), (wu2, wd2) = ssd_fwd("ssd_fwd_f", xbc, proj, dt_cb, dtb_f, alog_f, dsk_f, rev=False,
                                      rider=Rider([bf(ffn2_up), bf(ffn2_down)], "chips"), **ssd)
    (y_b, hs_b), (wout_g,) = ssd_fwd("ssd_fwd_b", xbc, proj, dt_cb, dtb_b, alog_b, dsk_b, rev=True,
                                     rider=Rider([bf(w_out)], "chips"), add=y_f, **ssd)
    wout = wout_g.reshape(2 * d, d)
    wo_y, wo_u = wout[:ds], wout[ds:]
    fn_gate = make_fn_gate_groupnorm(ds)
    (yn,) = rowwise("ssd_gate", fn_gate, [row(y_b), row(proj, d, 0)], [], [ssm_norm_w], [(n_lat, ds, BF16)],
                    tm=tm, n_tiles=lat_tiles)
    (u0,) = rowwise("glu", fn_glu, [row(proj, d, 1), row(proj, d, 2)], [], [], [(n_lat, d, F32)], tm=tm, n_tiles=lat_tiles)
    cb31 = max(LANES, d // 4)
    ncb31 = (d // 2) // cb31
    pad31 = w31.shape[0] // 2
    piece31 = min(seq_len, 4 * GRID_W)
    v_w = tapsum_roll("cconv_cols", u0, 0, w31, 0, seq_len=seq_len, n_seq=n_ex, row_blk_off=0, width=GRID_W,
                      piece=piece31, cb=cb31, ncb=ncb31, pad=pad31, flip=False)
    v_h = tapsum_rows("cconv_rows", u0, ncb31, w31, ncb31, seq_len=seq_len, n_seq=n_ex, cb=cb31, ncb=ncb31, pad=pad31, flip=False)
    (un,) = rowwise("cconv_ln", fn_ln_silu, [row(v_w), row(v_h)], [], [cconv_b, cconv_ln_w, cconv_ln_b], [(n_lat, d, BF16)],
                    tm=tm, n_tiles=lat_tiles)
    mix = matmul("mix_out", [(yn, wo_y), (un, wo_u)], "nn", tm=tm)
    seg_lat, first_lat = _segmenter(tm, seq_len, n_lat)
    (x2,) = rowwise("mix_resid", make_fn_resid(1.0), [row(x1), row(mix)], [lat(tabs[5])], [], [(n_lat, d, F32)],
                    tm=tm, n_tiles=lat_tiles, seg_fn=seg_lat)
    x3, saved2 = _ffn_fwd(plan, "ffn2", row(x2), n_lat, tm, seg_lat, lat(tabs[6]), lat(tabs[7]), lat(tabs[8]), norm_ffn2, wg2, wu2, wd2,
                          fuse_gate_up=True)
    d_x3, d_final, loss_vec = final_loss(x3, loss_target.reshape(n_lat, d), final_norm.reshape(1, d), tm=tm)

    shard_cols = lambda t: t.reshape(t.shape[0], N_CHIPS, -1).transpose(1, 0, 2)

    def pieces(t):
        t = jnp.pad(t, ((0, 0), (0, -t.shape[1] % 32), (0, 0)))
        return t.reshape(2 * N_CHIPS, t.shape[1] // 2, t.shape[2]).astype(BF16)

    scatter = lambda *ts: Rider([pieces(t) for t in ts], "all8", scatter=True)
    halves = lambda names, landed: Rider([sum_slots(f"sum_{nm}", r, BF16) for nm, r in zip(names, landed)], "sibling")
    swapped = {}
    plan.on("ffn2_up_dx", "sc_ffn2_down", lambda p: scatter(p.got["ffn2_d_wd"]))
    plan.on("ffn2_up_dw", "sc_ffn2_gate", lambda p: scatter(p.got["ffn2_d_wg"]))
    d_x2, (d_s6, d_s7, d_g8), d_nffn2 = _ffn_bwd(
        plan, "ffn2", d_x3, saved2, row(x2), n_lat, tm, seg_lat, first_lat, lat(tabs[6]), lat(tabs[7]), lat(tabs[8]), norm_ffn2,
        wg2, wu2, wd2, n_lat, None)
    d_mix, d_g5 = rowwise_bwd("mix_resid_bwd", make_fn_resid(1.0), [row(x1), row(mix)], [lat(tabs[5])], [], [[row(d_x2)]],
                              [None, (n_lat, BF16, None)], tm=tm, n_tiles=lat_tiles, seg_fn=seg_lat, first_fn=first_lat)
    d_yn = matmul("mix_out_dy", [(d_mix, wo_y)], "nt", tm=tm)
    d_un = matmul("mix_out_du", [(d_mix, wo_u)], "nt", tm=tm)
    d_wout = jnp.concatenate([matmul("mix_out_dwy", [(yn, d_mix)], "tn", out_dtype=BF16, tm=tm),
                              matmul("mix_out_dwu", [(un, d_mix)], "tn", out_dtype=BF16, tm=tm)])
    d_vw, d_vh, d_cb, d_lnw, d_lnb = rowwise_bwd(
        "cconv_ln_bwd", fn_ln_silu, [row(v_w), row(v_h)], [], [cconv_b, cconv_ln_w, cconv_ln_b], [[row(d_un)]],
        [(n_lat, F32, None)] * 2, tm=tm, n_tiles=lat_tiles)
    d_u0 = tapsum_roll("cconv_cols_dx", d_vw, 0, w31, 0, seq_len=seq_len, n_seq=n_ex, row_blk_off=0, width=GRID_W,
                       piece=piece31, cb=cb31, ncb=ncb31, pad=pad31, flip=True, place=((n_lat, d), 0, 0, None))
    d_u0 = tapsum_rows("cconv_rows_dx", d_vh, 0, w31, ncb31, seq_len=seq_len, n_seq=n_ex, cb=cb31, ncb=ncb31, pad=pad31,
                       flip=True, place=((n_lat, d), 0, ncb31, d_u0))
    d_w31 = jnp.concatenate([
        tapgrad_roll("cconv_cols_dw", d_vw, 0, 0, u0, 0, 0, n_tap=w31.shape[0], seq_len=seq_len, n_seq=n_ex, width=GRID_W,
                     piece=piece31, cb=cb31, ncb=ncb31, pad=pad31),
        tapgrad_rows("cconv_rows_dw", d_vh, 0, u0, ncb31, n_tap=w31.shape[0], seq_len=seq_len, n_seq=n_ex, cb=cb31,
                     ncb=ncb31, pad=pad31)], axis=1)
    d_ga, d_gb = rowwise_bwd("glu_bwd", fn_glu, [row(proj, d, 1), row(proj, d, 2)], [], [], [[row(d_u0)]],
                             [(n_lat, BF16, None)] * 2, tm=tm, n_tiles=lat_tiles)
    d_ysum, d_z, d_ssmnw = rowwise_bwd(
        "ssd_gate_bwd", fn_gate, [row(y_b), row(proj, d, 0)], [], [ssm_norm_w], [[row(d_yn)]],
        [(n_lat, F32, None), (n_lat, BF16, None)], tm=tm, n_tiles=lat_tiles)
    (dxbc_f, ddt_f, dalog_f, ddtb_f, ddsk), landed = ssd_bwd(
        "ssd_bwd_f", xbc, proj, dt_cb, hs_f, d_ysum, dtb_f, alog_f, dsk_f, rev=False,
        rider=scatter(plan.got["ffn2_d_wu"], d_wout.reshape(N_CHIPS, -1, d)), **ssd)
    (dxbc_b, ddt_b, dalog_b, ddtb_b, _), both = ssd_bwd(
        "ssd_bwd_b", xbc, proj, dt_cb, hs_b, d_ysum, dtb_b, alog_b, dsk_b, rev=True,
        rider=halves(["ffn2_down", "ffn2_gate"], plan.got["sc_ffn2_down"] + plan.got["sc_ffn2_gate"]), add=dxbc_f, **ssd)
    swapped.update(zip(["ffn2_down", "ffn2_gate"], both))
    (d_craw, d_conv_b), both = rowwise_bwd(
        "xbc_silu_bwd", fn_silu_bias, [row(craw)], [], [ssm_conv_b], [[row(dxbc_b)]],
        [(n_tok, F32, None)], tm=tm, n_tiles=n_tok // tm, rider=halves(["ffn2_up", "w_out"], landed))
    swapped.update(zip(["ffn2_up", "w_out"], both))
    d_pxbc = conv5("xbc_conv_dx", d_craw, 0, True)
    g5 = lambda name, seq, off: tapgrad_roll(name, d_craw, 0, off, proj, xbc_cb, off, n_tap=w5.shape[0], seq_len=seq,
                                             n_seq=n_ex, width=seq, piece=seq, cb=cbw, ncb=xw // cbw, pad=w5.shape[0] // 2)
    d_w5 = g5("xbc_conv_lat_dw", seq_len, 0) + g5("xbc_conv_ctx_dw", ctx_len, n_lat // ctx_len)
    lat_pairs = [(d_z, w_z), (d_ga, w_ga), (d_gb, w_gb), (d_pxbc, w_xbc), (ddt_f, w_dtp), (ddt_b, w_dtp)]
    d_h2 = matmul("mix_proj_dx_lat", lat_pairs, "nt", rows=n_lat, tm=min(tm, 256), place=(n_tok, 0, None))
    d_h2 = matmul("mix_proj_dx_ctx", lat_pairs[3:], "nt", rows=n_ctx_rows, row_off=n_lat, tm=min(tm, 256),
                  place=(n_tok, n_lat, d_h2))
    d_wz = matmul("mix_proj_dwz", [(d_z, h2)], "tn", out_dtype=BF16, rows=n_lat, tm=tm)
    d_wga = matmul("mix_proj_dwa", [(d_ga, h2)], "tn", out_dtype=BF16, rows=n_lat, tm=tm)
    d_wgb = matmul("mix_proj_dwb", [(d_gb, h2)], "tn", out_dtype=BF16, rows=n_lat, tm=tm)
    d_wxbc = matmul("mix_proj_dwx", [(d_pxbc, h2)], "tn", out_dtype=BF16, tm=tm)
    d_wdt = matmul("mix_proj_dwt", [(ddt_f, h2), (ddt_b, h2)], "tn", out_dtype=BF16, tm=tm)
    d_win_t = jnp.concatenate([d_wz, d_wxbc, d_wdt[:2 * n_head], d_wga, d_wgb]).reshape(N_CHIPS, -1, d)
    d_x1, d_s3, d_s4, d_nmix = rowwise_bwd(
        "mix_norm_bwd", fn_norm_mod, [row(x1)], [tabs[3], tabs[4]], [norm_mix], [[row(d_h2)]], [(n_tok, F32, None)],
        tm=tm, n_tiles=n_tok // tm, seg_fn=seg_all, first_fn=first_all, adds={0: (row(d_x2), lat_tiles)})
    mix_names = ["w_in", "ssm_conv_w", "cconv_w"]
    plan.on("ffn1_down_dx", "sc_conv", lambda p: scatter(shard_cols(d_w5), shard_cols(d_w31)))
    plan.on("ffn1_up_dx", "sc_win", lambda p: scatter(d_win_t))
    plan.on("ffn1_gate_dw", "sc_ffn1_down", lambda p: scatter(p.got["ffn1_d_wd"]))
    plan.on("ffn1_up_dw", "sc_ffn1_gate", lambda p: scatter(p.got["ffn1_d_wg"]))
    plan.on("ffn1_up_dw", "sw_mix", lambda p: halves(mix_names, p.got["sc_win"] + p.got["sc_conv"]))
    plan.on("ffn1_norm_bwd", "sc_ffn1_up", lambda p: scatter(p.got["ffn1_d_wu"]))
    plan.on("ffn1_norm_bwd", "sw_ffn1_down", lambda p: halves(["ffn1_down"], p.got["sc_ffn1_down"]))
    d_xt, (d_s0, d_s1, d_g2), d_nffn1 = _ffn_bwd(
        plan, "ffn1", d_x1, saved1, xt, n_tok, tm, seg_all, first_all, tabs[0], tabs[1], tabs[2], norm_ffn1, wg1, wu1, wd1,
        n_lat, lat_tiles)
    swapped.update(zip(mix_names + ["ffn1_down"], plan.got["sw_mix"] + plan.got["sw_ffn1_down"]))
    last_names = ["ffn1_gate", "ffn1_up"]
    last = halves(last_names, plan.got["sc_ffn1_gate"] + plan.got["sc_ffn1_up"])
    grad_x = d_xt.reshape(n_ex, seq_len, d)

    with_ctx0 = lambda t: jnp.concatenate([t, jnp.zeros((1, 1, d), F32)])
    d_tabs = [d_s0, d_s1, d_g2, d_s3, d_s4, with_ctx0(d_g5), with_ctx0(d_s6), with_ctx0(d_s7), with_ctx0(d_g8)]
    d_mod_rows = jnp.concatenate([t[:, 0, :] for t in d_tabs], axis=1)
    n_pad_rows = -(-(n_ex + 1) // 8) * 8
    d_mod_rows = jnp.concatenate([d_mod_rows, jnp.zeros((n_pad_rows - n_ex - 1, 9 * d), F32)])
    small = [("loss", loss_vec), ("norm_ffn1", d_nffn1), ("norm_mix", d_nmix), ("ssm_conv_b", d_conv_b),
             ("dt_bias_fwd", ddtb_f[:, :n_head]), ("dt_bias_bwd", ddtb_b[:, :n_head]), ("a_log_fwd", dalog_f[:, :n_head]),
             ("a_log_bwd", dalog_b[:, :n_head]), ("ssm_d", ddsk[:, :n_head]), ("ssm_norm_w", d_ssmnw), ("cconv_b", d_cb),
             ("cconv_ln_w", d_lnw), ("cconv_ln_b", d_lnb), ("norm_ffn2", d_nffn2), ("final_norm", d_final)]
    n_small = sum(v.size for _, v in small)
    n_pack = -(-n_small // (8 * LANES)) * (8 * LANES)
    pack = jnp.concatenate([v.reshape(-1) for _, v in small] + [jnp.zeros((n_pack - n_small,), F32)]).reshape(-1, LANES)
    (pack_all, d_mod_all), both = exchange_many("gather_small_swap_last", [Rider([pack, d_mod_rows], "all8"), last])
    swapped.update(zip(last_names, both))
    pack_sum = sum_slots("small_sum", pack_all)
    loss = loss_total(pack_sum.reshape(1, n_pack), d).reshape(())
    flat_sum = pack_sum.reshape(-1)
    small_grads, pos = {}, 0
    for nm, v in small:
        small_grads[nm] = flat_sum[pos:pos + v.size]
        pos += v.size
    d_mod_all = d_mod_all.reshape(8 * n_pad_rows, 9 * d)
    cond_rows = [jnp.concatenate([cond[j * n_ex:(j + 1) * n_ex], c_ctx[None, :],
                                  jnp.zeros((n_pad_rows - n_ex - 1, d), F32)]) for j in range(8)]
    cond_bwd = jnp.concatenate(cond_rows)
    d_mod_shard = lax.dynamic_slice(d_mod_all, (0, chip * mod_w), (8 * n_pad_rows, mod_w))
    g_wmod, g_bmod, q_part = mod_bwd(cond_bwd, d_mod_shard, d_mod_all, w_mod[0],
                                     tuple(j * n_pad_rows + n_ex for j in range(8)))
    (q_all,) = exchange("gather_cctx", [q_part], "all8")
    g_cctx = cctx_grad(q_all, c_ctx.reshape(1, d))
    small_grads["c_ctx"], small_grads["b_mod"] = g_cctx.reshape(-1), g_bmod.reshape(-1)

    transposed = {"ffn1_gate", "ffn1_up", "ffn2_gate", "ffn2_up", "w_in"}
    results = {}
    for nm, both in swapped.items():
        flip = (lambda t: jnp.swapaxes(t, 1, 2)) if nm in transposed else (lambda t: t)
        shape = flip(weights[nm]).shape
        two_d = lambda t: flip(t).reshape(shape[-2], shape[-1])
        g_full = both.reshape(1, -1, shape[-1])[:, :shape[-2]]
        results[nm] = [flip(r.reshape(shape)) for r in
                       adamw(f"adamw_{nm}", two_d(weights[nm]), g_full, two_d(mom1[nm]), two_d(mom2[nm]))]
    results["w_mod"] = [r.reshape(w_mod.shape) for r in adamw("adamw_w_mod", w_mod[0], g_wmod[None], m_w_mod[0], v_w_mod[0])]
    small_names = [nm for nm in order if nm not in results]
    n_sm = sum(weights[nm].size for nm in small_names)
    n_smp = -(-n_sm // (8 * LANES)) * (8 * LANES)
    packed = lambda src: jnp.concatenate([src[nm].reshape(-1) for nm in small_names] + [jnp.zeros((n_smp - n_sm,), F32)]).reshape(-1, LANES)
    sm_out = adamw("adamw_small", packed(weights), packed(small_grads)[None], packed(mom1), packed(mom2))
    pos = 0
    for nm in small_names:
        size = weights[nm].size
        results[nm] = [r.reshape(-1)[pos:pos + size].reshape(weights[nm].shape) for r in sm_out]
        pos += size
    return (loss, grad_x, *[results[nm][0] for nm in order], *[results[nm][1] for nm in order],
            *[results[nm][2] for nm in order], *[results[nm][3] for nm in order])
```

```python
import functools
import math

import jax
import jax.numpy as jnp
from jax import lax
from jax.experimental import pallas as pl
from jax.experimental.pallas import tpu as pltpu

F32 = jnp.float32
BF16 = jnp.bfloat16
HI = lax.Precision.HIGHEST
MESH = pl.DeviceIdType.MESH

EPS = 1e-6
GRID_W = 64
HEAD_DIM = 64
N_STATE = 128
CHUNK = 128
LANES = 128
N_CHIPS = 4
ADAM_LR, ADAM_B1, ADAM_B2, ADAM_EPS, ADAM_WD, ADAM_STEP = 0.001, 0.9, 0.999, 1e-08, 0.01, 10
VMEM_CAP = 56 * 1024 * 1024


def _params(vmem_bytes=None, n_axes=1):
    kw = dict(dimension_semantics=("arbitrary",) * n_axes)
    if vmem_bytes is not None:
        kw["vmem_limit_bytes"] = int(min(VMEM_CAP, max(32 * 1024 * 1024, vmem_bytes)))
    return pltpu.CompilerParams(**kw)


def _big(shape, dtype):
    return pltpu.HBM(tuple(shape), dtype)


def _in_hbm(args):
    return [pltpu.with_memory_space_constraint(a, pltpu.HBM) if a.size * a.dtype.itemsize >= (1 << 20) else a for a in args]


def _nbytes(shape, dtype):
    return math.prod(shape) * jnp.dtype(dtype).itemsize


def _row_tile(rows, width, cap_bytes=1 << 20, mult=8):
    best = None
    for t in range(mult, rows + 1, mult):
        if rows % t == 0 and t * width * 4 <= cap_bytes:
            best = t
    return best if best is not None else rows


_MODES = {"all8": (8, (1, 2, 3, 4, 5, 6, 7), 0), "chips": (4, (2, 4, 6), 1), "sibling": (2, (1,), 0)}


class Rider:
    def __init__(self, arrs, mode, scatter=False):
        self.arrs, self.scatter = list(arrs), scatter
        self.nslot, self.deltas, self.shift = _MODES[mode]
        self.n = len(self.arrs)
        self.out_shape = [jax.ShapeDtypeStruct((self.nslot,) + (a.shape[1:] if scatter else a.shape), a.dtype)
                          for a in self.arrs]
        any_spec = pl.BlockSpec(memory_space=pl.ANY)
        self.in_specs = [any_spec] * self.n
        self.out_specs = [any_spec] * self.n
        n_peer = len(self.deltas)
        self.scratch = [pltpu.SemaphoreType.DMA((self.n, n_peer)), pltpu.SemaphoreType.DMA((self.n, n_peer)),
                        pltpu.SemaphoreType.DMA((self.n,))]

    def _copies(self, ins, outs, sems, arrivals):
        send_sems, recv_sems, local_sems = sems
        x, y, c = lax.axis_index("x"), lax.axis_index("y"), lax.axis_index("c")
        me = 4 * x + 2 * y + c
        slot_of = lambda dev: (dev >> self.shift) & (self.nslot - 1)
        src = lambda a, slot: ins[a].at[slot] if self.scatter else ins[a]
        flip = lambda v, bit: 1 - v if bit else v

        def remote(a, k, d, from_slot, to_slot):
            return pltpu.make_async_remote_copy(
                src_ref=src(a, from_slot), dst_ref=outs[a].at[to_slot], send_sem=send_sems.at[a, k],
                recv_sem=recv_sems.at[a, k], device_id=(flip(x, (d >> 2) & 1), flip(y, (d >> 1) & 1), flip(c, d & 1)),
                device_id_type=MESH)

        mine = slot_of(me)
        local = [pltpu.make_async_copy(src(a, mine), outs[a].at[mine], local_sems.at[a]) for a in range(self.n)]
        sends = [remote(a, k, d, slot_of(me ^ d), mine) for k, d in enumerate(self.deltas) for a in range(self.n)]
        if not arrivals:
            return local, sends
        return local, sends, [remote(a, k, d, mine, slot_of(me ^ d)) for k, d in enumerate(self.deltas) for a in range(self.n)]

    def start(self, ins, outs, sems):
        local, sends = self._copies(ins, outs, sems, arrivals=False)
        for cp in local + sends:
            cp.start()

    def wait(self, ins, outs, sems):
        local, sends, recvs = self._copies(ins, outs, sems, arrivals=True)
        for cp in recvs:
            cp.wait_recv()
        for cp in sends:
            cp.wait_send()
        for cp in local:
            cp.wait()


class Riders:
    def __init__(self, riders):
        self.riders = list(riders)
        self.n = sum(r.n for r in self.riders)
        cat = lambda attr: [v for r in self.riders for v in getattr(r, attr)]
        self.arrs, self.out_shape, self.in_specs = cat("arrs"), cat("out_shape"), cat("in_specs")
        self.out_specs, self.scratch = cat("out_specs"), cat("scratch")

    def _each(self, method, ins, outs, sems):
        i = s = 0
        for r in self.riders:
            getattr(r, method)(ins[i:i + r.n], outs[i:i + r.n], sems[s:s + len(r.scratch)])
            i, s = i + r.n, s + len(r.scratch)

    def start(self, ins, outs, sems):
        self._each("start", ins, outs, sems)

    def wait(self, ins, outs, sems):
        self._each("wait", ins, outs, sems)


class _Hosted:
    def __init__(self, rider, n_in, n_out, n_scratch, grid):
        self.rider, self.n_in, self.n_out, self.n_scratch, self.grid = rider, n_in, n_out, n_scratch, grid
        self.n = rider.n if rider else 0

    def split(self, refs):
        a, b = self.n_in, self.n_in + self.n
        c, e = b + self.n_out, b + self.n_out + self.n
        self._r = (refs[a:b], refs[c:e], refs[e + self.n_scratch:])
        if self.rider:
            ids = [pl.program_id(ax) for ax in range(len(self.grid))]
            first = functools.reduce(jnp.logical_and, [i == 0 for i in ids]) if ids else True
            pl.when(first)(lambda: self.rider.start(*self._r))
        return refs[:a], refs[b:c], refs[e:e + self.n_scratch]

    def finish(self):
        if self.rider:
            ids = [pl.program_id(ax) for ax in range(len(self.grid))]
            last = functools.reduce(jnp.logical_and, [i == n - 1 for i, n in zip(ids, self.grid)]) if ids else True
            pl.when(last)(lambda: self.rider.wait(*self._r))

    def call_args(self, in_specs, out_shape, out_specs, scratch, args):
        r = self.rider
        if not r:
            return list(in_specs), tuple(out_shape), tuple(out_specs), list(scratch), list(args)
        return (list(in_specs) + r.in_specs, tuple(out_shape) + tuple(r.out_shape), tuple(out_specs) + tuple(r.out_specs),
                list(scratch) + r.scratch, list(args) + r.arrs)

    def results(self, res, unwrap=True):
        res = list(res) if isinstance(res, (tuple, list)) else [res]
        host = res[:self.n_out]
        host = host[0] if (self.n_out == 1 and unwrap) else tuple(host)
        return (host, res[self.n_out:]) if self.rider else host


def exchange_many(name, riders):
    both = Riders(riders)

    def body(*refs):
        ins, outs, sems = refs[:both.n], refs[both.n:2 * both.n], refs[2 * both.n:]
        both.start(ins, outs, sems)
        both.wait(ins, outs, sems)

    res = list(pl.pallas_call(
        body, name=name, out_shape=tuple(both.out_shape), in_specs=both.in_specs, out_specs=tuple(both.out_specs),
        scratch_shapes=both.scratch,
    )(*both.arrs))
    split = []
    for r in riders:
        split.append(res[:r.n])
        res = res[r.n:]
    return split


def exchange(name, arrs, mode, scatter=False):
    rider = Rider(arrs, mode, scatter)

    def body(*refs):
        ins, outs, sems = refs[:rider.n], refs[rider.n:2 * rider.n], refs[2 * rider.n:]
        rider.start(ins, outs, sems)
        rider.wait(ins, outs, sems)

    return pl.pallas_call(
        body, name=name, out_shape=tuple(rider.out_shape), in_specs=rider.in_specs, out_specs=tuple(rider.out_specs),
        scratch_shapes=rider.scratch,
    )(*arrs)


_DIMS = {"nn": (((1,), (0,)), ((), ())), "nt": (((1,), (1,)), ((), ())), "tn": (((0,), (0,)), ((), ()))}


def matmul(name, pairs, kind, *, a_ch=False, b_ch=False, out_ch=False, out_dtype=F32, rows=None, row_off=0, tm=512,
           rider=None, post=None, fold=False, place=None):
    a0, b0 = pairs[0]
    n_chunk = a0.shape[0] if a_ch else (b0.shape[0] if b_ch else 1)
    total_rows = a0.shape[-2]
    rows = total_rows - row_off if rows is None else rows
    tm = min(tm, rows)
    assert rows % tm == 0 and row_off % tm == 0, (name, rows, tm, row_off)
    n_rt, off = rows // tm, row_off // tm
    dims = _DIMS[kind]
    n_pair = len(pairs)

    if kind == "tn":
        grid, red_axis, n_red = (n_chunk, n_rt), 1, n_rt
        a_idx = (lambda k, i: (k, i + off, 0)) if a_ch else (lambda k, i: (i + off, 0))
        b_idx = (lambda k, i: (k, i + off, 0)) if b_ch else (lambda k, i: (i + off, 0))
        a_blk = lambda a: ((None, tm, a.shape[-1]) if a_ch else (tm, a.shape[-1]))
        b_blk = lambda b: ((None, tm, b.shape[-1]) if b_ch else (tm, b.shape[-1]))
        o2 = (a0.shape[-1], b0.shape[-1])
        out_shape = ((n_chunk,) + o2) if out_ch else o2
        out_spec = pl.BlockSpec((None,) + o2, lambda k, i: (k, 0, 0)) if out_ch else pl.BlockSpec(o2, lambda k, i: (0, 0))
        acc_shape = o2
    else:
        n_out = b0.shape[-1] if kind == "nn" else b0.shape[-2]
        b2 = b0.shape[-2:]
        if a_ch and b_ch and not out_ch and fold:
            grid, red_axis, n_red = (n_rt,), None, 1
            a_idx, b_idx = (lambda i: (0, i + off, 0)), (lambda i: (0, 0, 0))
            a_blk = lambda a: (n_chunk, tm, a.shape[-1])
            b_blk = lambda b: tuple(b.shape)
            out_shape, out_spec = (rows, n_out), pl.BlockSpec((tm, n_out), lambda i: (i, 0))
        elif a_ch and b_ch and not out_ch:
            grid, red_axis, n_red = (n_rt, n_chunk), 1, n_chunk
            a_idx, b_idx = (lambda i, k: (k, i + off, 0)), (lambda i, k: (k, 0, 0))
            a_blk = lambda a: (None, tm, a.shape[-1])
            b_blk = lambda b: (None,) + tuple(b.shape[-2:])
            out_shape, out_spec = (rows, n_out), pl.BlockSpec((tm, n_out), lambda i, k: (i, 0))
        elif out_ch and fold:
            assert b_ch and not a_ch and all(a is a0 for a, _ in pairs)
            grid, red_axis, n_red = (n_rt,), None, 1
            a_idx, b_idx = (lambda i: (i + off, 0)), (lambda i: (0, 0, 0))
            a_blk = lambda a: (tm, a.shape[-1])
            b_blk = lambda b: tuple(b.shape)
            out_shape, out_spec = (n_chunk, rows, n_out), pl.BlockSpec((n_chunk, tm, n_out), lambda i: (0, i, 0))
        elif out_ch:
            assert b_ch and not a_ch
            grid, red_axis, n_red = (n_chunk, n_rt), None, 1
            a_idx, b_idx = (lambda k, i: (i + off, 0)), (lambda k, i: (k, 0, 0))
            a_blk = lambda a: (tm, a.shape[-1])
            b_blk = lambda b: (None,) + tuple(b.shape[-2:])
            out_shape, out_spec = (n_chunk, rows, n_out), pl.BlockSpec((None, tm, n_out), lambda k, i: (k, i, 0))
        else:
            assert not (a_ch or b_ch)
            grid, red_axis, n_red = (n_rt,), None, 1
            a_idx, b_idx = (lambda i: (i + off, 0)), (lambda i: (0, 0))
            a_blk = lambda a: (tm, a.shape[-1])
            b_blk = lambda b: tuple(b.shape)
            out_shape, out_spec = (rows, n_out), pl.BlockSpec((tm, n_out), lambda i: (i, 0))
            if place is not None:
                out_shape, o_off = (place[0], n_out), place[1] // tm
                out_spec = pl.BlockSpec((tm, n_out), lambda i: (i + o_off, 0))
        acc_shape = (tm, n_out)

    into = [] if place is None or place[2] is None else [place[2]]
    post_ins, post_fn, out_dtypes = ([], None, [out_dtype]) if post is None else post
    hosted = _Hosted(rider, 2 * n_pair + len(post_ins) + len(into), len(out_dtypes), int(n_red > 1), grid)

    def body(*refs):
        ins, outs, scr = hosted.split(refs)

        def compute():
            acc = None
            for p in range(n_pair):
                for k in ([None] if not fold else range(n_chunk)):
                    pick = (lambda r: r[...]) if k is None else (lambda r: r[k])
                    d = lax.dot_general(pick(ins[2 * p]).astype(BF16), pick(ins[2 * p + 1]).astype(BF16), dims,
                                        preferred_element_type=F32)
                    acc = d if acc is None else acc + d
            return acc

        def emit(acc):
            vals = (acc,) if post_fn is None else post_fn(
                acc, *[r[...].astype(F32) for r in ins[2 * n_pair:2 * n_pair + len(post_ins)]])
            for o_ref, v in zip(outs, vals):
                o_ref[...] = v.astype(o_ref.dtype)

        if out_ch and fold:
            a_tile = ins[0][...].astype(BF16)
            for k in range(n_chunk):
                accs = [lax.dot_general(a_tile, ins[2 * p + 1][k].astype(BF16), dims, preferred_element_type=F32)
                        for p in range(n_pair)]
                tiles = [r[k].astype(F32) for r in ins[2 * n_pair:2 * n_pair + len(post_ins)]]
                vals = tuple(accs) if post_fn is None else post_fn(*accs, *tiles)
                for o_ref, v in zip(outs, vals):
                    o_ref[k] = v.astype(o_ref.dtype)
        elif n_red == 1:
            emit(compute())
        else:
            acc_ref = scr[0]
            r = pl.program_id(red_axis)

            @pl.when(r == 0)
            def _():
                acc_ref[...] = jnp.zeros_like(acc_ref)

            acc_ref[...] += compute()

            @pl.when(r == n_red - 1)
            def _():
                emit(acc_ref[...])
        hosted.finish()

    in_specs, args, vmem = [], [], 0
    for a, b in pairs:
        in_specs += [pl.BlockSpec(a_blk(a), a_idx), pl.BlockSpec(b_blk(b), b_idx)]
        args += [a, b]
        vmem += 2 * (_nbytes([s for s in a_blk(a) if s], a.dtype) + _nbytes([s for s in b_blk(b) if s], b.dtype))
    in_specs += [out_spec] * len(post_ins)
    args += list(post_ins)
    aliases = {len(args): 0} if into else {}
    in_specs += [pl.BlockSpec(memory_space=pl.ANY)] * len(into)
    args += into
    tiles_per_step = n_chunk if (out_ch and fold) else 1
    vmem += (3 + 2 * n_pair + tiles_per_step * (len(post_ins) + len(out_dtypes))) * _nbytes(acc_shape, F32)
    scratch = [pltpu.VMEM(acc_shape, F32)] if n_red > 1 else []
    in_specs, out_shapes, out_specs, scratch, args = hosted.call_args(
        in_specs, [_big(out_shape, dt) for dt in out_dtypes], [out_spec] * len(out_dtypes), scratch, args)
    return hosted.results(pl.pallas_call(
        body, name=name, out_shape=out_shapes, grid=grid, in_specs=in_specs, out_specs=out_specs,
        input_output_aliases=aliases, scratch_shapes=scratch, compiler_params=_params(vmem + (8 << 20), len(grid)),
    )(*_in_hbm(args)))


def row(arr, width=None, cb=0, roff=0):
    return (arr, arr.shape[-1] if width is None else width, cb, roff)


def two_rows(first, second, limit):
    return (first, first.shape[-1], 0, 0, (second, limit))


def _row_inputs(rows, tm):
    specs, arrs, slots = [], [], []
    for d in rows:
        second, limit = d[4] if len(d) > 4 else (None, None)
        slots.append((len(arrs), limit))
        specs.append(_row_spec(d[:4], tm, limit))
        arrs.append(d[0])
        if second is not None:
            specs.append(pl.BlockSpec((tm, d[1]), lambda i, limit=limit: (jnp.maximum(i - limit, 0), 0)))
            arrs.append(second)

    def read(refs, i):
        vals = []
        for at, limit in slots:
            v = refs[at][...].astype(F32)
            vals.append(v if limit is None else jnp.where(i < limit, v, refs[at + 1][...].astype(F32)))
        return vals

    return specs, arrs, read


def _row_spec(desc, tm, limit=None):
    _, width, cb, roff = desc[:4]
    if limit is None:
        return pl.BlockSpec((tm, width), lambda i: (i + roff, cb))
    return pl.BlockSpec((tm, width), lambda i: (jnp.minimum(i, limit - 1) + roff, cb))


def _segmenter(tm, seq_len, n_lat):
    seg = lambda i: jnp.where(i * tm < n_lat, (i * tm) // seq_len, n_lat // seq_len)
    first = lambda i: jnp.where(i * tm < n_lat, (i * tm) % seq_len == 0, i * tm == n_lat)
    return seg, first


def rowwise(name, fn, rows, segs, params, outs, *, tm, n_tiles, seg_fn=None, rider=None):
    row_specs, row_arrs, read_rows = _row_inputs(rows, tm)
    n_r, n_s, n_p = len(row_arrs), len(segs), len(params)
    hosted = _Hosted(rider, n_r + n_s + n_p, len(outs), 0, (n_tiles,))

    def body(*refs):
        ins, out_refs, _ = hosted.split(refs)
        vals = read_rows(ins[:n_r], pl.program_id(0)) + [r[...] for r in ins[n_r:]]
        res = fn(*vals)
        for o_ref, v in zip(out_refs, res):
            o_ref[...] = v.astype(o_ref.dtype)
        hosted.finish()

    in_specs = list(row_specs)
    in_specs += [pl.BlockSpec((None, 1, s.shape[-1]), lambda i: (seg_fn(i), 0, 0)) for s in segs]
    in_specs += [pl.BlockSpec(p.shape, lambda i: (0, 0)) for p in params]
    vmem = sum(2 * tm * d[1] * 4 for d in rows) + sum(3 * tm * w * 4 for _, w, _ in outs) + sum(2 * p.size * 4 for p in params)
    in_specs, out_shapes, out_specs, scratch, args = hosted.call_args(
        in_specs, [_big((r, w), dt) for r, w, dt in outs],
        [pl.BlockSpec((tm, w), lambda i: (i, 0)) for _, w, _ in outs], [], row_arrs + list(segs) + list(params))
    return hosted.results(pl.pallas_call(
        body, name=name, grid=(n_tiles,), in_specs=in_specs, out_shape=out_shapes, out_specs=out_specs,
        scratch_shapes=scratch, compiler_params=_params(2 * vmem + (8 << 20)),
    )(*_in_hbm(args)), unwrap=False)


def rowwise_bwd(name, fn, rows, segs, params, cts, row_grads, *, tm, n_tiles, seg_fn=None, first_fn=None, adds=None,
                rider=None):
    adds = adds or {}
    need = [k for k, v in enumerate(row_grads) if v is not None]
    row_specs, row_arrs, read_rows = _row_inputs(rows, tm)
    n_r, n_s, n_p = len(row_arrs), len(segs), len(params)
    n_ct = sum(len(lst) for lst in cts)
    add_keys = sorted(adds)
    hosted = _Hosted(rider, n_r + n_s + n_p + n_ct + len(add_keys), len(need) + n_s + n_p, 0, (n_tiles,))

    def body(*refs):
        host_in, host_out, _ = hosted.split(refs)
        it = iter(list(host_in) + list(host_out))
        row_refs = [next(it) for _ in range(n_r)]
        seg_refs = [next(it) for _ in range(n_s)]
        par_refs = [next(it) for _ in range(n_p)]
        ct_refs = [[next(it) for _ in lst] for lst in cts]
        add_refs = {k: next(it) for k in add_keys}
        rg_refs = {k: next(it) for k in need}
        sg_refs = [next(it) for _ in range(n_s)]
        pg_refs = [next(it) for _ in range(n_p)]
        i = pl.program_id(0)
        rv = read_rows(row_refs, i)
        sv = [r[...] for r in seg_refs]
        pv = [r[...] for r in par_refs]

        def f(*args):
            rr = list(rv)
            for j, k in enumerate(need):
                rr[k] = args[j]
            return fn(*rr, *args[len(need):])

        _, vjp = jax.vjp(f, *[rv[k] for k in need], *sv, *pv)
        ctv = []
        for lst in ct_refs:
            acc = lst[0][...].astype(F32)
            for r in lst[1:]:
                acc = acc + r[...].astype(F32)
            ctv.append(acc)
        g = vjp(tuple(ctv))
        for j, k in enumerate(need):
            gv = g[j]
            if k in adds:
                lim = adds[k][1]
                av = add_refs[k][...].astype(F32)
                gv = gv + (av if lim is None else jnp.where(i < lim, av, 0.0))
            lim = row_grads[k][2]
            if lim is None:
                rg_refs[k][...] = gv.astype(rg_refs[k].dtype)
            else:
                @pl.when(i < lim)
                def _(gv=gv, k=k):
                    rg_refs[k][...] = gv.astype(rg_refs[k].dtype)
        if n_s:
            opens = first_fn(i)
            for ref, gv in zip(sg_refs, g[len(need):len(need) + n_s]):
                @pl.when(opens)
                def _(ref=ref, gv=gv):
                    ref[...] = gv

                @pl.when(jnp.logical_not(opens))
                def _(ref=ref, gv=gv):
                    ref[...] += gv
        for ref, gv in zip(pg_refs, g[len(need) + n_s:]):
            @pl.when(i == 0)
            def _(ref=ref, gv=gv):
                ref[...] = gv

            @pl.when(i > 0)
            def _(ref=ref, gv=gv):
                ref[...] += gv
        hosted.finish()

    seg_spec = lambda s: pl.BlockSpec((None, 1, s.shape[-1]), lambda i: (seg_fn(i), 0, 0))
    par_spec = lambda p: pl.BlockSpec(p.shape, lambda i: (0, 0))
    in_specs = list(row_specs) + [seg_spec(s) for s in segs] + [par_spec(p) for p in params]
    args = row_arrs + list(segs) + list(params)
    for lst in cts:
        in_specs += [_row_spec(d, tm) for d in lst]
        args += [d[0] for d in lst]
    for k in add_keys:
        in_specs.append(_row_spec(adds[k][0], tm, adds[k][1]))
        args.append(adds[k][0][0])
    out_shape, out_specs = [], []
    for k in need:
        n_rows, dt, lim = row_grads[k]
        out_shape.append(_big((n_rows, rows[k][1]), dt))
        out_specs.append(_row_spec((None, rows[k][1], 0, 0), tm, lim))
    for s in segs:
        out_shape.append(jax.ShapeDtypeStruct(s.shape, F32))
        out_specs.append(seg_spec(s))
    for p in params:
        out_shape.append(jax.ShapeDtypeStruct(p.shape, F32))
        out_specs.append(par_spec(p))
    vmem = sum(tm * d[1] * 4 for d in rows) * 6 + n_ct * tm * max(d[1] for d in rows) * 8
    in_specs, out_shape, out_specs, scratch, args = hosted.call_args(in_specs, out_shape, out_specs, [], args)
    return hosted.results(pl.pallas_call(
        body, name=name, grid=(n_tiles,), in_specs=in_specs, out_shape=out_shape, out_specs=out_specs,
        scratch_shapes=scratch, compiler_params=_params(vmem + (8 << 20)),
    )(*_in_hbm(args)), unwrap=False)


def _silu(v):
    return v * jax.nn.sigmoid(v)


def _rms(v, w):
    return v * lax.rsqrt(jnp.mean(v * v, axis=-1, keepdims=True) + EPS) * w


def fn_norm_mod(x, shift, scale, w):
    return (_rms(x, w) * (1.0 + scale) + shift,)


def fn_act(g, u):
    return (_silu(g) * u,)


def make_fn_resid(coef):
    def fn(x, f, gate):
        return (x + coef * gate * f,)
    return fn


def fn_silu_bias(v, b):
    return (_silu(v + b),)


def make_fn_gate_groupnorm(width):
    half = width // 2

    def fn(y_both, z, w):
        y = y_both * _silu(z)
        lane = lax.broadcasted_iota(jnp.int32, y.shape, 1)
        lo = lane < half
        sq = y * y
        s_lo = jnp.sum(jnp.where(lo, sq, 0.0), axis=-1, keepdims=True)
        s_hi = jnp.sum(jnp.where(lo, 0.0, sq), axis=-1, keepdims=True)
        r = jnp.where(lo, lax.rsqrt(s_lo / half + EPS), lax.rsqrt(s_hi / half + EPS))
        return (y * r * w,)
    return fn


def fn_glu(a, b):
    return (a * jax.nn.sigmoid(b),)


def fn_ln_silu(vw, vh, cb, lw, lb):
    v = jnp.concatenate([vw, vh], axis=-1) + cb
    mu = jnp.mean(v, axis=-1, keepdims=True)
    var = jnp.mean(jnp.square(v - mu), axis=-1, keepdims=True)
    return (_silu((v - mu) * lax.rsqrt(var + EPS) * lw + lb),)


def _col_tile(width):
    return width // 3 if width % (3 * LANES) == 0 else width


def mod_fwd(a_rows, w_shard, b_shard):
    n, d = a_rows.shape
    ws = w_shard.shape[1]
    tn = _col_tile(ws)

    def body(a_ref, w_ref, b_ref, o_ref):
        a = _silu(a_ref[...]).astype(BF16)
        o_ref[...] = jnp.dot(a, w_ref[...].astype(BF16), preferred_element_type=F32) + b_ref[...]

    return pl.pallas_call(
        body, name="mod_fwd", grid=(ws // tn,), out_shape=jax.ShapeDtypeStruct((n, ws), F32),
        in_specs=[pl.BlockSpec((n, d), lambda j: (0, 0)), pl.BlockSpec((d, tn), lambda j: (0, j)),
                  pl.BlockSpec((1, tn), lambda j: (0, j))],
        out_specs=pl.BlockSpec((n, tn), lambda j: (0, j)), compiler_params=_params(),
    )(a_rows, w_shard, b_shard)


def mod_bwd(a_rows, d_shard, d_full, w_shard, ctx_rows):
    n, d = a_rows.shape
    ws = w_shard.shape[1]
    tn = _col_tile(ws)
    n_ct = ws // tn

    def body(a_ref, ds_ref, df_ref, w_ref, gw_ref, gb_ref, q_ref):
        j = pl.program_id(0)
        a = _silu(a_ref[...])
        ds = ds_ref[...]
        gw_ref[...] = lax.dot_general(a, ds, _DIMS["tn"], precision=HI, preferred_element_type=F32)
        dctx = ds[ctx_rows[0]:ctx_rows[0] + 1, :]
        for r in ctx_rows[1:]:
            dctx = dctx + ds[r:r + 1, :]
        q = lax.dot_general(jnp.broadcast_to(dctx, (8, tn)), w_ref[...], _DIMS["nt"], precision=HI,
                            preferred_element_type=F32)

        @pl.when(j == 0)
        def _():
            q_ref[...] = q
            df = df_ref[...]
            acc = df[0:1, :]
            for r in range(1, n):
                acc = acc + df[r:r + 1, :]
            gb_ref[...] = acc

        @pl.when(j > 0)
        def _():
            q_ref[...] += q

    return pl.pallas_call(
        body, name="mod_bwd", grid=(n_ct,),
        out_shape=(jax.ShapeDtypeStruct((d, ws), F32), jax.ShapeDtypeStruct((1, d_full.shape[1]), F32),
                   jax.ShapeDtypeStruct((8, d), F32)),
        in_specs=[pl.BlockSpec((n, d), lambda j: (0, 0)), pl.BlockSpec((n, tn), lambda j: (0, j)),
                  pl.BlockSpec(d_full.shape, lambda j: (0, 0)), pl.BlockSpec((d, tn), lambda j: (0, j))],
        out_specs=(pl.BlockSpec((d, tn), lambda j: (0, j)), pl.BlockSpec((1, d_full.shape[1]), lambda j: (0, 0)),
                   pl.BlockSpec((8, d), lambda j: (0, 0))),
        compiler_params=_params(40 << 20),
    )(a_rows, d_shard, d_full, w_shard)


def _shifted(xs, d, tok, width):
    if d == 0:
        return xs
    n = xs.shape[0]
    sh = pltpu.roll(xs, (-d) % n, axis=0)
    return jnp.where((tok + d >= 0) & (tok + d < width), sh, 0.0)


def _placed(out_shape, place):
    if place is None:
        return out_shape, 0, 0, None
    return place


def tapsum_roll(name, x, xcb, w, wcb, *, seq_len, n_seq, row_blk_off, width, piece, cb, ncb, pad, flip, place=None):
    n_tap = w.shape[0]
    n_piece = seq_len // piece
    out_shape, o_rb, o_cb, into = _placed((n_seq * seq_len, ncb * cb), place)

    def body(x_ref, w_ref, *rest):
        o_ref = rest[-1]
        wv = w_ref[...]
        tok = lax.broadcasted_iota(jnp.int32, (piece, 1), 0) % width

        def do_piece(p, carry):
            start = pl.multiple_of(p * piece, piece)
            xs = x_ref[pl.ds(start, piece), :]
            acc = jnp.zeros_like(xs)
            for k in range(n_tap):
                d = pad - k if flip else k - pad
                acc = acc + wv[k:k + 1, :] * _shifted(xs, d, tok, width)
            o_ref[pl.ds(start, piece), :] = acc
            return carry

        lax.fori_loop(0, n_piece, do_piece, 0)

    extra = [] if into is None else [into]
    return pl.pallas_call(
        body, name=name, grid=(ncb, n_seq), out_shape=_big(out_shape, F32),
        in_specs=[pl.BlockSpec((seq_len, cb), lambda j, s: (row_blk_off + s, xcb + j)),
                  pl.BlockSpec((n_tap, cb), lambda j, s: (0, wcb + j))] + [pl.BlockSpec(memory_space=pl.ANY)] * len(extra),
        out_specs=pl.BlockSpec((seq_len, cb), lambda j, s: (o_rb + s, o_cb + j)),
        input_output_aliases={2: 0} if extra else {},
        compiler_params=_params(8 * seq_len * cb * 4 + (8 << 20), 2),
    )(*_in_hbm([x, w] + extra))


def tapgrad_roll(name, dy, dycb, dy_blk_off, x, xcb, x_blk_off, *, n_tap, seq_len, n_seq, width, piece, cb, ncb, pad):
    n_piece = seq_len // piece

    def body(dy_ref, x_ref, o_ref):
        @pl.when(pl.program_id(1) == 0)
        def _():
            o_ref[...] = jnp.zeros_like(o_ref)

        tok = lax.broadcasted_iota(jnp.int32, (piece, 1), 0) % width

        def do_piece(p, carry):
            start = pl.multiple_of(p * piece, piece)
            xs = x_ref[pl.ds(start, piece), :]
            dv = dy_ref[pl.ds(start, piece), :]
            for k in range(n_tap):
                o_ref[k:k + 1, :] += jnp.sum(dv * _shifted(xs, k - pad, tok, width), axis=0, keepdims=True)
            return carry

        lax.fori_loop(0, n_piece, do_piece, 0)

    return pl.pallas_call(
        body, name=name, grid=(ncb, n_seq), out_shape=jax.ShapeDtypeStruct((n_tap, ncb * cb), F32),
        in_specs=[pl.BlockSpec((seq_len, cb), lambda j, s: (dy_blk_off + s, dycb + j)),
                  pl.BlockSpec((seq_len, cb), lambda j, s: (x_blk_off + s, xcb + j))],
        out_specs=pl.BlockSpec((n_tap, cb), lambda j, s: (0, j)),
        compiler_params=_params(8 * seq_len * cb * 4 + (8 << 20), 2),
    )(*_in_hbm([dy, x]))


def tapsum_rows(name, x, xcb, w, wcb, *, seq_len, n_seq, cb, ncb, pad, flip, place=None):
    n_tap = w.shape[0]
    n_row = seq_len // GRID_W
    halo = pad * GRID_W
    out_shape, o_rb, o_cb, into = _placed((n_seq * seq_len, ncb * cb), place)

    def body(x_ref, w_ref, *rest):
        o_ref, xp = rest[-2:]
        xp[pl.ds(0, halo), :] = jnp.zeros((halo, cb), F32)
        xp[pl.ds(halo + seq_len, halo), :] = jnp.zeros((halo, cb), F32)
        xp[pl.ds(halo, seq_len), :] = x_ref[...]
        wv = w_ref[...]

        def do_row(r, carry):
            acc = jnp.zeros((GRID_W, cb), F32)
            for k in range(n_tap):
                d = pad - k if flip else k - pad
                acc = acc + wv[k:k + 1, :] * xp[pl.ds(pl.multiple_of((r + pad + d) * GRID_W, GRID_W), GRID_W), :]
            o_ref[pl.ds(pl.multiple_of(r * GRID_W, GRID_W), GRID_W), :] = acc
            return carry

        lax.fori_loop(0, n_row, do_row, 0)

    extra = [] if into is None else [into]
    return pl.pallas_call(
        body, name=name, grid=(ncb, n_seq), out_shape=_big(out_shape, F32),
        in_specs=[pl.BlockSpec((seq_len, cb), lambda j, s: (s, xcb + j)),
                  pl.BlockSpec((n_tap, cb), lambda j, s: (0, wcb + j))] + [pl.BlockSpec(memory_space=pl.ANY)] * len(extra),
        out_specs=pl.BlockSpec((seq_len, cb), lambda j, s: (o_rb + s, o_cb + j)),
        input_output_aliases={2: 0} if extra else {},
        scratch_shapes=[pltpu.VMEM((seq_len + 2 * halo, cb), F32)],
        compiler_params=_params(10 * seq_len * cb * 4 + (8 << 20), 2),
    )(*_in_hbm([x, w] + extra))


def tapgrad_rows(name, dy, dycb, x, xcb, *, n_tap, seq_len, n_seq, cb, ncb, pad):
    n_row = seq_len // GRID_W
    halo = pad * GRID_W

    def body(dy_ref, x_ref, o_ref, xp):
        @pl.when(pl.program_id(1) == 0)
        def _():
            o_ref[...] = jnp.zeros_like(o_ref)

        xp[pl.ds(0, halo), :] = jnp.zeros((halo, cb), F32)
        xp[pl.ds(halo + seq_len, halo), :] = jnp.zeros((halo, cb), F32)
        xp[pl.ds(halo, seq_len), :] = x_ref[...]

        def do_row(r, carry):
            dv = dy_ref[pl.ds(pl.multiple_of(r * GRID_W, GRID_W), GRID_W), :]
            for k in range(n_tap):
                xs = xp[pl.ds(pl.multiple_of((r + k) * GRID_W, GRID_W), GRID_W), :]
                o_ref[k:k + 1, :] += jnp.sum(dv * xs, axis=0, keepdims=True)
            return carry

        lax.fori_loop(0, n_row, do_row, 0)

    return pl.pallas_call(
        body, name=name, grid=(ncb, n_seq), out_shape=jax.ShapeDtypeStruct((n_tap, ncb * cb), F32),
        in_specs=[pl.BlockSpec((seq_len, cb), lambda j, s: (s, dycb + j)),
                  pl.BlockSpec((seq_len, cb), lambda j, s: (s, xcb + j))],
        out_specs=pl.BlockSpec((n_tap, cb), lambda j, s: (0, j)),
        scratch_shapes=[pltpu.VMEM((seq_len + 2 * halo, cb), F32)],
        compiler_params=_params(10 * seq_len * cb * 4 + (8 << 20), 2),
    )(*_in_hbm([dy, x]))


def _ssd_blocks(b, s, *, rev, n_ctx, n_lat, lat_blocks):
    if rev:
        return jnp.where(s < n_ctx, lat_blocks + b * n_ctx + (n_ctx - 1 - s), b * n_lat + (n_lat - 1 - (s - n_ctx)))
    return jnp.where(s < n_ctx, lat_blocks + b * n_ctx + s, b * n_lat + (s - n_ctx))


def _ssd_common(xbc, raw, dtb, alog, dsk, *, rev, ds, n_head):
    if rev:
        raw = pltpu.roll(raw, LANES - n_head, axis=1)
    pre = raw + dtb
    dt = jnp.maximum(pre, 0.0) + jnp.log1p(jnp.exp(-jnp.abs(pre)))
    sig = jax.nn.sigmoid(pre)
    a = -jnp.exp(alog)
    da = dt * a
    ri = lax.broadcasted_iota(jnp.int32, (CHUNK, CHUNK), 0)
    ci = lax.broadcasted_iota(jnp.int32, (CHUNK, CHUNK), 1)
    mask = (ci >= ri) if rev else (ci <= ri)
    tri = mask.astype(F32)
    tri_t = ((ci <= ri) if rev else (ci >= ri)).astype(F32)
    cs = jnp.dot(tri, da, precision=HI, preferred_element_type=F32)
    tot = jnp.sum(da, axis=0, keepdims=True)
    def wide(v):
        first = lax.broadcasted_iota(jnp.int32, (v.shape[0], LANES), 1) < HEAD_DIM
        return jnp.concatenate(
            [jnp.where(first, jnp.broadcast_to(v[:, 2 * p:2 * p + 1], first.shape),
                       jnp.broadcast_to(v[:, 2 * p + 1:2 * p + 2], first.shape)) for p in range(n_head // 2)], axis=1)

    cs_w, tot_w = wide(cs), wide(tot)
    xh = xbc[:, :ds]
    dt_w = wide(dt)
    return dict(
        dt=dt, sig=sig, a=a, cs=cs, cs_t=cs.T, tot=tot, mask=mask, tri_t=tri_t,
        e_w=jnp.exp(cs_w), wt_w=jnp.exp(tot_w - cs_w), dec_w=jnp.exp(tot_w), dt_w=dt_w, dsk_w=wide(dsk),
        xh=xh, xs_w=xh * dt_w, bm=xbc[:, ds:ds + 2 * N_STATE], cm=xbc[:, ds + 2 * N_STATE:ds + 4 * N_STATE])


def _decay(q, col):
    seg = q["cs"][:, col:col + 1] - q["cs_t"][col:col + 1, :]
    return jnp.exp(jnp.where(q["mask"], seg, -jnp.inf))


def _split_heads(v):
    lane = lax.broadcasted_iota(jnp.int32, v.shape, 1)
    return jnp.concatenate([jnp.where(lane < HEAD_DIM, v, 0.0), jnp.where(lane >= HEAD_DIM, v, 0.0)], axis=0)


def ssd_fwd(name, xbc, proj, dt_cb, dtb, alog, dsk, *, rev, n_ex, seq_len, ctx_len, ds, rider=None, add=None):
    n_head, half = ds // HEAD_DIM, ds // 2
    n_ctx, n_lat = ctx_len // CHUNK, seq_len // CHUNK
    n_step = n_ctx + n_lat
    blk = functools.partial(_ssd_blocks, rev=rev, n_ctx=n_ctx, n_lat=n_lat, lat_blocks=n_ex * n_lat)
    xw = xbc.shape[1]

    def y_blk(b, s):
        sl = jnp.maximum(s, n_ctx) - n_ctx
        return b * n_lat + ((n_lat - 1 - sl) if rev else sl)

    hosted = _Hosted(rider, 5 + (add is not None), 2, 1, (n_ex, n_step))

    def body(*refs):
        (xbc_ref, dt_ref, dtb_ref, alog_ref, dsk_ref, *add_ref), (y_ref, hs_ref), (h_scr,) = hosted.split(refs)

        @pl.when(pl.program_id(1) == 0)
        def _():
            h_scr[...] = jnp.zeros_like(h_scr)

        q = _ssd_common(xbc_ref[...], dt_ref[...], dtb_ref[...], alog_ref[...], dsk_ref[...], rev=rev, ds=ds, n_head=n_head)
        h = h_scr[...]
        hs_ref[...] = h
        for g in range(2):
            lo = g * half
            bg = q["bm"][:, g * N_STATE:(g + 1) * N_STATE].astype(BF16)
            cg = q["cm"][:, g * N_STATE:(g + 1) * N_STATE].astype(BF16)
            scores = lax.dot_general(cg, bg, _DIMS["nt"], preferred_element_type=F32)
            hg = h[:, lo:lo + half]
            off = jnp.dot(cg, hg.astype(BF16), preferred_element_type=F32)
            for j in range(half // LANES):
                c0 = (lo + j * LANES) // HEAD_DIM
                ln = slice(lo + j * LANES, lo + (j + 1) * LANES)
                p_cat = jnp.concatenate([scores * _decay(q, c0), scores * _decay(q, c0 + 1)], axis=1).astype(BF16)
                diag = jnp.dot(p_cat, _split_heads(q["xs_w"][:, ln]).astype(BF16), preferred_element_type=F32)
                y_ref[:, ln] = (diag + q["e_w"][:, ln] * off[:, j * LANES:(j + 1) * LANES]
                                + q["dsk_w"][:, ln] * q["xh"][:, ln] + (add_ref[0][:, ln] if add_ref else 0.0))
            v = (q["wt_w"][:, lo:lo + half] * q["xs_w"][:, lo:lo + half]).astype(BF16)
            h_scr[:, lo:lo + half] = (q["dec_w"][:, lo:lo + half] * hg
                                      + lax.dot_general(bg, v, _DIMS["tn"], preferred_element_type=F32))
        hosted.finish()

    vec = pl.BlockSpec((1, LANES), lambda b, s: (0, 0))
    in_specs, out_shape, out_specs, scratch, args = hosted.call_args(
        [pl.BlockSpec((CHUNK, xw), lambda b, s: (blk(b, s), 0)),
         pl.BlockSpec((CHUNK, LANES), lambda b, s: (blk(b, s), dt_cb)), vec, vec, vec]
        + [pl.BlockSpec((CHUNK, ds), lambda b, s: (y_blk(b, s), 0))] * (add is not None),
        (_big((n_ex * seq_len, ds), F32), _big((n_ex, n_step, N_STATE, ds), F32)),
        (pl.BlockSpec((CHUNK, ds), lambda b, s: (y_blk(b, s), 0)),
         pl.BlockSpec((None, None, N_STATE, ds), lambda b, s: (b, s, 0, 0))),
        [pltpu.VMEM((N_STATE, ds), F32)], [xbc, proj, dtb, alog, dsk] + ([] if add is None else [add]))
    return hosted.results(pl.pallas_call(
        body, name=name, grid=(n_ex, n_step), out_shape=out_shape, in_specs=in_specs, out_specs=out_specs,
        scratch_shapes=scratch, compiler_params=_params(40 << 20, 2),
    )(*_in_hbm(args)))


def ssd_bwd(name, xbc, proj, dt_cb, hs, dy, dtb, alog, dsk, *, rev, n_ex, seq_len, ctx_len, ds, rider=None, add=None):
    n_head, half = ds // HEAD_DIM, ds // 2
    n_ctx, n_lat = ctx_len // CHUNK, seq_len // CHUNK
    n_step = n_ctx + n_lat
    n_tok = n_ex * (seq_len + ctx_len)
    blk0 = functools.partial(_ssd_blocks, rev=rev, n_ctx=n_ctx, n_lat=n_lat, lat_blocks=n_ex * n_lat)
    step = lambda sp: n_step - 1 - sp
    blk = lambda b, sp: blk0(b, step(sp))
    xw = xbc.shape[1]

    def dy_blk(b, sp):
        sl = jnp.maximum(step(sp), n_ctx) - n_ctx
        return b * n_lat + ((n_lat - 1 - sl) if rev else sl)

    hosted = _Hosted(rider, 7 + (add is not None), 5, 1, (n_ex, n_step))

    def body(*refs):
        ((xbc_ref, dt_ref, hs_ref, dy_ref, dtb_ref, alog_ref, dsk_ref, *add_ref),
         (dxbc_ref, ddt_ref, dalog_ref, ddtb_ref, ddsk_ref), (dh_scr,)) = hosted.split(refs)
        b, sp = pl.program_id(0), pl.program_id(1)
        more = (lambda cols: add_ref[0][:, cols]) if add_ref else (lambda cols: 0.0)

        @pl.when(sp == 0)
        def _():
            dh_scr[...] = jnp.zeros_like(dh_scr)

        @pl.when((sp == 0) & (b == 0))
        def _():
            dalog_ref[...] = jnp.zeros_like(dalog_ref)
            ddtb_ref[...] = jnp.zeros_like(ddtb_ref)
            ddsk_ref[...] = jnp.zeros_like(ddsk_ref)

        q = _ssd_common(xbc_ref[...], dt_ref[...], dtb_ref[...], alog_ref[...], dsk_ref[...], rev=rev, ds=ds, n_head=n_head)
        h = hs_ref[...]
        d_y = jnp.where(step(sp) >= n_ctx, dy_ref[...], 0.0)
        dh_next = dh_scr[...]
        lane_row = lax.broadcasted_iota(jnp.int32, (1, LANES), 1)
        d_cs = jnp.zeros((CHUNK, LANES), F32)
        dxs_parts, de_parts, dwt_parts, ddec_parts = [], [], [], []
        for g in range(2):
            lo = g * half
            gs = slice(lo, lo + half)
            bg = q["bm"][:, g * N_STATE:(g + 1) * N_STATE].astype(BF16)
            cg = q["cm"][:, g * N_STATE:(g + 1) * N_STATE].astype(BF16)
            scores = lax.dot_general(cg, bg, _DIMS["nt"], preferred_element_type=F32)
            hg, dyg, dhn = h[:, gs], d_y[:, gs], dh_next[:, gs]
            off = jnp.dot(cg, hg.astype(BF16), preferred_element_type=F32)
            d_off = (q["e_w"][:, gs] * dyg).astype(BF16)
            de_parts.append(dyg * off)
            d_c = lax.dot_general(d_off, hg.astype(BF16), _DIMS["nt"], preferred_element_type=F32)
            dh_scr[:, gs] = (lax.dot_general(cg, d_off, _DIMS["tn"], preferred_element_type=F32)
                             + q["dec_w"][:, gs] * dhn)
            b_dh = jnp.dot(bg, dhn.astype(BF16), preferred_element_type=F32)
            v = q["wt_w"][:, gs] * q["xs_w"][:, gs]
            d_b = lax.dot_general(v.astype(BF16), dhn.astype(BF16), _DIMS["nt"], preferred_element_type=F32)
            dwt_parts.append(q["xs_w"][:, gs] * b_dh)
            ddec_parts.append(jnp.sum(hg * dhn, axis=0, keepdims=True))
            d_scores = jnp.zeros((CHUNK, CHUNK), F32)
            for j in range(half // LANES):
                c0 = (lo + j * LANES) // HEAD_DIM
                ln = slice(lo + j * LANES, lo + (j + 1) * LANES)
                l0, l1 = _decay(q, c0), _decay(q, c0 + 1)
                p0, p1 = scores * l0, scores * l1
                dy_st = _split_heads(d_y[:, ln]).astype(BF16)
                d_p = lax.dot_general(dy_st, q["xs_w"][:, ln].astype(BF16), _DIMS["nt"], preferred_element_type=F32)
                d_p0, d_p1 = d_p[:CHUNK], d_p[CHUNK:]
                d_scores = d_scores + d_p0 * l0 + d_p1 * l1
                for col, t in ((c0, d_p0 * p0), (c0 + 1, d_p1 * p1)):
                    d_cs = d_cs + jnp.sum(t - t.T, axis=1, keepdims=True) * (lane_row == col).astype(F32)
                p_st = jnp.concatenate([p0, p1], axis=0).astype(BF16)
                dxs_parts.append(lax.dot_general(p_st, dy_st, _DIMS["tn"], preferred_element_type=F32)
                                 + q["wt_w"][:, ln] * b_dh[:, j * LANES:(j + 1) * LANES])
            d_sc = d_scores.astype(BF16)
            d_c = d_c + jnp.dot(d_sc, bg, preferred_element_type=F32)
            d_b = d_b + lax.dot_general(d_sc, cg, _DIMS["tn"], preferred_element_type=F32)
            b_cols, c_cols = slice(ds + g * N_STATE, ds + (g + 1) * N_STATE), slice(ds + (2 + g) * N_STATE, ds + (3 + g) * N_STATE)
            dxbc_ref[:, b_cols] = d_b + more(b_cols)
            dxbc_ref[:, c_cols] = d_c + more(c_cols)
        d_xs = jnp.concatenate(dxs_parts, axis=1)
        narrow_m = (lax.broadcasted_iota(jnp.int32, (ds, LANES), 0) // HEAD_DIM
                    == lax.broadcasted_iota(jnp.int32, (ds, LANES), 1)).astype(BF16)
        rows8 = lambda v: jnp.broadcast_to(v, (8, ds))
        stacked = jnp.concatenate(
            [jnp.concatenate(dwt_parts, axis=1), jnp.concatenate(de_parts, axis=1), d_xs * q["xh"],
             rows8(jnp.concatenate(ddec_parts, axis=1)), rows8(jnp.sum(d_y * q["xh"], axis=0, keepdims=True))], axis=0)
        hi = stacked.astype(BF16)
        lo = (stacked - hi.astype(F32)).astype(BF16)
        sums = (jnp.dot(hi, narrow_m, preferred_element_type=F32) + jnp.dot(lo, narrow_m, preferred_element_type=F32))
        n_wt, n_e, n_xs = sums[:CHUNK], sums[CHUNK:2 * CHUNK], sums[2 * CHUNK:3 * CHUNK]
        n_dec, n_dsk = sums[3 * CHUNK:3 * CHUNK + 1], sums[3 * CHUNK + 8:3 * CHUNK + 9]
        e, wt, dec = jnp.exp(q["cs"]), jnp.exp(q["tot"] - q["cs"]), jnp.exp(q["tot"])
        d_wt = n_wt * wt
        d_cs = d_cs + n_e * e - d_wt
        d_tot = jnp.sum(d_wt, axis=0, keepdims=True) + n_dec * dec
        d_da = jnp.dot(q["tri_t"], d_cs, precision=HI, preferred_element_type=F32) + d_tot
        d_dt = d_da * q["a"] + n_xs
        dxbc_ref[:, :ds] = d_xs * q["dt_w"] + q["dsk_w"] * d_y + more(slice(0, ds))
        dalog_ref[...] += jnp.sum(d_da * q["dt"], axis=0, keepdims=True) * q["a"]
        d_raw = d_dt * q["sig"]
        ddtb_ref[...] += jnp.sum(d_raw, axis=0, keepdims=True)
        ddsk_ref[...] += n_dsk
        ddt_ref[...] = pltpu.roll(d_raw, n_head, axis=1) if rev else d_raw
        hosted.finish()

    vec = pl.BlockSpec((1, LANES), lambda b, s: (0, 0))
    vec_shape = jax.ShapeDtypeStruct((1, LANES), F32)
    in_specs, out_shape, out_specs, scratch, args = hosted.call_args(
        [pl.BlockSpec((CHUNK, xw), lambda b, s: (blk(b, s), 0)),
         pl.BlockSpec((CHUNK, LANES), lambda b, s: (blk(b, s), dt_cb)),
         pl.BlockSpec((None, None, N_STATE, ds), lambda b, s: (b, step(s), 0, 0)),
         pl.BlockSpec((CHUNK, ds), lambda b, s: (dy_blk(b, s), 0)), vec, vec, vec]
        + [pl.BlockSpec((CHUNK, xw), lambda b, s: (blk(b, s), 0))] * (add is not None),
        (_big((n_tok, xw), F32), _big((n_tok, LANES), F32), vec_shape, vec_shape, vec_shape),
        (pl.BlockSpec((CHUNK, xw), lambda b, s: (blk(b, s), 0)),
         pl.BlockSpec((CHUNK, LANES), lambda b, s: (blk(b, s), 0)), vec, vec, vec),
        [pltpu.VMEM((N_STATE, ds), F32)], [xbc, proj, hs, dy, dtb, alog, dsk] + ([] if add is None else [add]))
    return hosted.results(pl.pallas_call(
        body, name=name, grid=(n_ex, n_step), out_shape=out_shape, in_specs=in_specs, out_specs=out_specs,
        scratch_shapes=scratch, compiler_params=_params(48 << 20, 2),
    )(*_in_hbm(args)))


def final_loss(x3, target, w, *, tm):
    n, d = x3.shape

    def body(x_ref, t_ref, w_ref, dx_ref, dw_ref, loss_ref):
        i = pl.program_id(0)
        t = t_ref[...]

        def per_feature(xv, wv):
            err = _rms(xv, wv) - t
            return 0.5 * jnp.sum(err * err, axis=0, keepdims=True) / d

        lv, vjp = jax.vjp(per_feature, x_ref[...], w_ref[...])
        dx, dw = vjp(jnp.ones_like(lv))
        dx_ref[...] = dx

        @pl.when(i == 0)
        def _():
            dw_ref[...] = dw
            loss_ref[...] = lv

        @pl.when(i > 0)
        def _():
            dw_ref[...] += dw
            loss_ref[...] += lv

    tile = pl.BlockSpec((tm, d), lambda i: (i, 0))
    vec = pl.BlockSpec((1, d), lambda i: (0, 0))
    return pl.pallas_call(
        body, name="final_loss", grid=(n // tm,), in_specs=[tile, tile, vec],
        out_shape=(jax.ShapeDtypeStruct((n, d), F32), jax.ShapeDtypeStruct((1, d), F32), jax.ShapeDtypeStruct((1, d), F32)),
        out_specs=(tile, vec, vec), compiler_params=_params(tm * d * 4 * 16 + (8 << 20)),
    )(x3, target, w)


def sum_slots(name, arr, out_dtype=F32):
    n_slot, n_row, width = arr.shape
    tm = _row_tile(n_row, width * n_slot, mult=16)

    def body(a_ref, o_ref):
        acc = a_ref[0].astype(F32)
        for j in range(1, n_slot):
            acc = acc + a_ref[j].astype(F32)
        o_ref[...] = acc.astype(o_ref.dtype)

    return pl.pallas_call(
        body, name=name, grid=(n_row // tm,), out_shape=jax.ShapeDtypeStruct((n_row, width), out_dtype),
        in_specs=[pl.BlockSpec((n_slot, tm, width), lambda i: (0, i, 0))],
        out_specs=pl.BlockSpec((tm, width), lambda i: (i, 0)), compiler_params=_params(),
    )(arr)


def adamw(name, w, g_slots, m, v):
    n_slot, n_row, width = g_slots.shape
    tm = _row_tile(n_row, width * 2)
    if g_slots.dtype == BF16 and tm % 16:
        tm16 = _row_tile(n_row, width * 2, mult=16)
        if tm16 % 16 == 0:
            tm = tm16
        else:
            g_slots = g_slots.astype(F32)

    def body(w_ref, g_ref, m_ref, v_ref, go_ref, d_ref, mo_ref, vo_ref):
        g = g_ref[0].astype(F32)
        for j in range(1, n_slot):
            g = g + g_ref[j].astype(F32)
        m2 = ADAM_B1 * m_ref[...] + (1.0 - ADAM_B1) * g
        v2 = ADAM_B2 * v_ref[...] + (1.0 - ADAM_B2) * jnp.square(g)
        m_hat = m2 / (1.0 - ADAM_B1 ** ADAM_STEP)
        v_hat = v2 / (1.0 - ADAM_B2 ** ADAM_STEP)
        go_ref[...] = g
        d_ref[...] = -ADAM_LR * (m_hat / (jnp.sqrt(v_hat) + ADAM_EPS) + ADAM_WD * w_ref[...])
        mo_ref[...] = m2
        vo_ref[...] = v2

    tile = pl.BlockSpec((tm, width), lambda i: (i, 0))
    shape = jax.ShapeDtypeStruct((n_row, width), F32)
    return pl.pallas_call(
        body, name=name, grid=(n_row // tm,), out_shape=(shape,) * 4,
        in_specs=[tile, pl.BlockSpec((n_slot, tm, width), lambda i: (0, i, 0)), tile, tile],
        out_specs=(tile,) * 4, compiler_params=_params(),
    )(w, g_slots, m, v)


def cctx_grad(q_all, c_ctx_row):
    d = c_ctx_row.shape[1]

    def body(q_ref, c_ref, o_ref):
        acc = q_ref[0, 0:1, :]
        for j in (2, 4, 6):
            acc = acc + q_ref[j, 0:1, :]
        _, vjp = jax.vjp(_silu, c_ref[...])
        o_ref[...] = vjp(acc)[0]

    return pl.pallas_call(
        body, name="cctx_grad", out_shape=jax.ShapeDtypeStruct((1, d), F32),
    )(q_all, c_ctx_row)


def loss_total(pack_sum, d):
    def body(p_ref, o_ref):
        o_ref[...] = jnp.sum(p_ref[:, 0:d], axis=1, keepdims=True)

    return pl.pallas_call(
        body, name="loss_total", out_shape=jax.ShapeDtypeStruct((1, 1), F32),
    )(pack_sum)


class _Plan:
    def __init__(self):
        self.builders, self.got = {}, {}

    def on(self, host, key, builder):
        self.builders.setdefault(host, []).append((key, builder))

    def run(self, host, fn, *args, **kw):
        if host not in self.builders:
            return fn(host, *args, **kw)
        keys, riders = zip(*[(key, builder(self)) for key, builder in self.builders[host]])
        res, landed = fn(host, *args, rider=Riders(riders), **kw)
        for key, r in zip(keys, riders):
            self.got[key], landed = landed[:r.n], landed[r.n:]
        return res


def _val(w):
    return w() if callable(w) else w


def _matmul_tile(n_rows, tm):
    return 2 * tm if n_rows % (2 * tm) == 0 else tm


def _ffn_fwd(plan, tag, xin, n_rows, tm, seg_fn, shift, scale, gate, norm_w, wg, wu, wd, fuse_gate_up=False):
    d = xin[1]
    n_tiles = n_rows // tm
    (h,) = plan.run(f"{tag}_norm", rowwise, fn_norm_mod, [xin], [shift, scale], [norm_w], [(n_rows, d, BF16)],
                    tm=tm, n_tiles=n_tiles, seg_fn=seg_fn)
    tmm = _matmul_tile(n_rows, tm)
    if fuse_gate_up:
        g, u, act = plan.run(f"{tag}_gate_up", matmul, [(h, _val(wg)), (h, _val(wu))], "nn", b_ch=True, out_ch=True,
                             tm=min(tm, 256), fold=True,
                             post=([], lambda ag, au: (ag, au, fn_act(ag, au)[0]), [BF16, BF16, BF16]))
    else:
        g = plan.run(f"{tag}_gate", matmul, [(h, _val(wg))], "nn", out_dtype=BF16, b_ch=True, out_ch=True, tm=tmm)
        u, act = plan.run(f"{tag}_up", matmul, [(h, _val(wu))], "nn", b_ch=True, out_ch=True, tm=tm, fold=True,
                          post=([g], lambda acc, gv: (acc, fn_act(gv, acc)[0]), [BF16, BF16]))
    f = plan.run(f"{tag}_down", matmul, [(act, _val(wd))], "nn", a_ch=True, b_ch=True, tm=tmm, fold=True)
    (xo,) = plan.run(f"{tag}_resid", rowwise, make_fn_resid(0.5), [xin, row(f)], [gate], [], [(n_rows, d, F32)],
                     tm=tm, n_tiles=n_tiles, seg_fn=seg_fn)
    return xo, (h, g, u, act, f)


def _ffn_bwd(plan, tag, d_xo, saved, xin, n_rows, tm, seg_fn, first_fn, shift, scale, gate, norm_w, wg, wu, wd, dx_rows, dx_limit):
    h, g, u, act, f = saved
    d = xin[1]
    n_tiles = n_rows // tm
    n_ch, _, n_hid = g.shape
    d_f, d_gate = plan.run(f"{tag}_resid_bwd", rowwise_bwd, make_fn_resid(0.5), [xin, row(f)], [gate], [], [[row(d_xo)]],
                           [None, (n_rows, BF16, None)], tm=tm, n_tiles=n_tiles, seg_fn=seg_fn, first_fn=first_fn)
    tmm = _matmul_tile(n_rows, tm)
    def act_vjp(d_act, gv, uv):
        s = jax.nn.sigmoid(gv)
        gs = gv * s
        return d_act * uv * (s + gs * (1.0 - s)), d_act * gs
    d_g, d_u = plan.run(f"{tag}_down_dx", matmul, [(d_f, wd)], "nt", b_ch=True, out_ch=True, tm=tmm,
                        post=([g, u], act_vjp, [BF16, BF16]))
    plan.got[f"{tag}_d_wd"] = plan.run(f"{tag}_down_dw", matmul, [(act, d_f)], "tn", out_dtype=BF16, a_ch=True, out_ch=True, tm=tmm)
    d_h = plan.run(f"{tag}_up_dx", matmul, [(d_g, wg), (d_u, wu)], "nt", a_ch=True, b_ch=True, tm=tmm)
    plan.got[f"{tag}_d_wg"] = plan.run(f"{tag}_gate_dw", matmul, [(d_g, h)], "tn", out_dtype=BF16, a_ch=True, out_ch=True, tm=tmm)
    plan.got[f"{tag}_d_wu"] = plan.run(f"{tag}_up_dw", matmul, [(d_u, h)], "tn", out_dtype=BF16, a_ch=True, out_ch=True, tm=tmm)
    d_x, d_shift, d_scale, d_nw = plan.run(
        f"{tag}_norm_bwd", rowwise_bwd, fn_norm_mod, [xin], [shift, scale], [norm_w], [[row(d_h)]], [(dx_rows, F32, dx_limit)],
        tm=tm, n_tiles=n_tiles, seg_fn=seg_fn, first_fn=first_fn, adds={0: (row(d_xo), None)})
    return d_x, (d_shift, d_scale, d_gate), d_nw


def kernel(x, c, ctx, c_ctx, w_mod, b_mod, norm_ffn1, ffn1_gate, ffn1_up, ffn1_down, norm_mix, w_in, ssm_conv_w, ssm_conv_b, dt_bias_fwd, dt_bias_bwd, a_log_fwd, a_log_bwd, ssm_d, ssm_norm_w, cconv_w, cconv_b, cconv_ln_w, cconv_ln_b, w_out, norm_ffn2, ffn2_gate, ffn2_up, ffn2_down, final_norm, loss_target, m_c_ctx, m_w_mod, m_b_mod, m_norm_ffn1, m_ffn1_gate, m_ffn1_up, m_ffn1_down, m_norm_mix, m_w_in, m_ssm_conv_w, m_ssm_conv_b, m_dt_bias_fwd, m_dt_bias_bwd, m_a_log_fwd, m_a_log_bwd, m_ssm_d, m_ssm_norm_w, m_cconv_w, m_cconv_b, m_cconv_ln_w, m_cconv_ln_b, m_w_out, m_norm_ffn2, m_ffn2_gate, m_ffn2_up, m_ffn2_down, m_final_norm, v_c_ctx, v_w_mod, v_b_mod, v_norm_ffn1, v_ffn1_gate, v_ffn1_up, v_ffn1_down, v_norm_mix, v_w_in, v_ssm_conv_w, v_ssm_conv_b, v_dt_bias_fwd, v_dt_bias_bwd, v_a_log_fwd, v_a_log_bwd, v_ssm_d, v_ssm_norm_w, v_cconv_w, v_cconv_b, v_cconv_ln_w, v_cconv_ln_b, v_w_out, v_norm_ffn2, v_ffn2_gate, v_ffn2_up, v_ffn2_down, v_final_norm):
    weights = dict(c_ctx=c_ctx, w_mod=w_mod, b_mod=b_mod, norm_ffn1=norm_ffn1, ffn1_gate=ffn1_gate, ffn1_up=ffn1_up, ffn1_down=ffn1_down, norm_mix=norm_mix, w_in=w_in, ssm_conv_w=ssm_conv_w, ssm_conv_b=ssm_conv_b, dt_bias_fwd=dt_bias_fwd, dt_bias_bwd=dt_bias_bwd, a_log_fwd=a_log_fwd, a_log_bwd=a_log_bwd, ssm_d=ssm_d, ssm_norm_w=ssm_norm_w, cconv_w=cconv_w, cconv_b=cconv_b, cconv_ln_w=cconv_ln_w, cconv_ln_b=cconv_ln_b, w_out=w_out, norm_ffn2=norm_ffn2, ffn2_gate=ffn2_gate, ffn2_up=ffn2_up, ffn2_down=ffn2_down, final_norm=final_norm)
    mom1 = dict(c_ctx=m_c_ctx, w_mod=m_w_mod, b_mod=m_b_mod, norm_ffn1=m_norm_ffn1, ffn1_gate=m_ffn1_gate, ffn1_up=m_ffn1_up, ffn1_down=m_ffn1_down, norm_mix=m_norm_mix, w_in=m_w_in, ssm_conv_w=m_ssm_conv_w, ssm_conv_b=m_ssm_conv_b, dt_bias_fwd=m_dt_bias_fwd, dt_bias_bwd=m_dt_bias_bwd, a_log_fwd=m_a_log_fwd, a_log_bwd=m_a_log_bwd, ssm_d=m_ssm_d, ssm_norm_w=m_ssm_norm_w, cconv_w=m_cconv_w, cconv_b=m_cconv_b, cconv_ln_w=m_cconv_ln_w, cconv_ln_b=m_cconv_ln_b, w_out=m_w_out, norm_ffn2=m_norm_ffn2, ffn2_gate=m_ffn2_gate, ffn2_up=m_ffn2_up, ffn2_down=m_ffn2_down, final_norm=m_final_norm)
    mom2 = dict(c_ctx=v_c_ctx, w_mod=v_w_mod, b_mod=v_b_mod, norm_ffn1=v_norm_ffn1, ffn1_gate=v_ffn1_gate, ffn1_up=v_ffn1_up, ffn1_down=v_ffn1_down, norm_mix=v_norm_mix, w_in=v_w_in, ssm_conv_w=v_ssm_conv_w, ssm_conv_b=v_ssm_conv_b, dt_bias_fwd=v_dt_bias_fwd, dt_bias_bwd=v_dt_bias_bwd, a_log_fwd=v_a_log_fwd, a_log_bwd=v_a_log_bwd, ssm_d=v_ssm_d, ssm_norm_w=v_ssm_norm_w, cconv_w=v_cconv_w, cconv_b=v_cconv_b, cconv_ln_w=v_cconv_ln_w, cconv_ln_b=v_cconv_ln_b, w_out=v_w_out, norm_ffn2=v_norm_ffn2, ffn2_gate=v_ffn2_gate, ffn2_up=v_ffn2_up, ffn2_down=v_ffn2_down, final_norm=v_final_norm)
    order = list(weights)

    n_ex, seq_len, d = x.shape
    ctx_len = ctx.shape[1]
    ds = d
    n_head = ds // HEAD_DIM
    xw = ds + 4 * N_STATE
    n_lat, n_ctx_rows = n_ex * seq_len, n_ex * ctx_len
    n_tok = n_lat + n_ctx_rows
    tm = math.gcd(math.gcd(512, seq_len), n_ctx_rows)
    seg_all, first_all = _segmenter(tm, seq_len, n_lat)
    lat_tiles = n_lat // tm

    xi, yi, ci = lax.axis_index("x"), lax.axis_index("y"), lax.axis_index("c")
    me, chip = 4 * xi + 2 * yi + ci, 2 * xi + yi

    (c_all,) = exchange("gather_c", [c], "all8")
    n_all = 8 * n_ex
    n_cond = -(-(n_all + 1) // 8) * 8
    cond = jnp.concatenate([c_all.reshape(n_all, d), c_ctx[None, :], jnp.zeros((n_cond - n_all - 1, d), F32)])
    mod_w = w_mod.shape[2]
    b_shard = lax.dynamic_slice(b_mod, (0, chip * mod_w), (1, mod_w))
    (mod_g,), (wg1,) = exchange_many("gather_mod_wg1", [Rider([mod_fwd(cond, w_mod[0], b_shard)], "chips"),
                                                        Rider([ffn1_gate[0].astype(BF16)], "chips")])
    mod_full = mod_g.transpose(1, 0, 2).reshape(n_cond, N_CHIPS * mod_w)
    mod_mine = lax.dynamic_slice(mod_full, (me * n_ex, 0), (n_ex, 9 * d)).reshape(n_ex, 9, d)
    mod_ctx = mod_full[n_all].reshape(9, d)
    tabs = [jnp.concatenate([mod_mine[:, j], mod_ctx[j][None]])[:, None, :] for j in range(9)]
    lat = lambda t: t[:n_ex]

    bf = lambda w: w[0].astype(BF16)
    plan = _Plan()
    gather = lambda *ws: (lambda p: Rider(list(ws), "chips"))
    plan.on("ffn1_gate", "wu1", gather(bf(ffn1_up)))
    plan.on("ffn1_up", "wd1", gather(bf(ffn1_down)))
    win_cut = d * 5 // 8
    plan.on("ffn1_down", "win_a", gather(bf(w_in)[:win_cut]))
    plan.on("ffn1_resid", "win_b", gather(bf(w_in)[win_cut:], ssm_conv_w[0], cconv_w[0]))
    xt = two_rows(x.reshape(n_lat, d), ctx.reshape(n_ctx_rows, d), lat_tiles)
    x1, saved1 = _ffn_fwd(plan, "ffn1", xt, n_tok, tm, seg_all, tabs[0], tabs[1], tabs[2], norm_ffn1,
                          wg1, lambda: plan.got["wu1"][0], lambda: plan.got["wd1"][0])
    (wu1,), (wd1,), (win_a,), (win_b, w5_g, w31_g) = (plan.got[k] for k in ("wu1", "wd1", "win_a", "win_b"))
    win_g = jnp.concatenate([win_a, win_b], axis=1)
    unshard_cols = lambda t: t.transpose(1, 0, 2).reshape(t.shape[1], N_CHIPS * t.shape[2])
    win = unshard_cols(win_g)
    o_x, o_dt, o_glu = ds, ds + xw, ds + xw + 2 * n_head
    w_z, w_xbc, w_dt = win[:, :ds], win[:, o_x:o_dt], win[:, o_dt:o_glu]
    w_ga, w_gb = win[:, o_glu:o_glu + d], win[:, o_glu + d:]
    w_dtp = jnp.concatenate([w_dt, jnp.zeros((d, LANES - 2 * n_head), BF16)], axis=1)
    w_cat = jnp.concatenate([w_z, w_ga, w_gb, w_xbc, w_dtp], axis=1)
    cbw = d // 2
    xbc_cb, dt_cb = 3 * d // cbw, (3 * d + xw) // LANES
    w5, w31 = unshard_cols(w5_g), unshard_cols(w31_g)
    pad_vec = lambda v: jnp.concatenate([v.reshape(1, -1), jnp.zeros((1, LANES - v.size), F32)], axis=1)
    dtb_f, dtb_b, alog_f, alog_b = map(pad_vec, (dt_bias_fwd, dt_bias_bwd, a_log_fwd, a_log_bwd))
    dsk_f, dsk_b = pad_vec(ssm_d), jnp.zeros((1, LANES), F32)

    (h2,) = rowwise("mix_norm", fn_norm_mod, [row(x1)], [tabs[3], tabs[4]], [norm_mix], [(n_tok, d, BF16)],
                    tm=tm, n_tiles=n_tok // tm, seg_fn=seg_all)
    proj, (wg2,) = matmul("mix_proj", [(h2, w_cat)], "nn", tm=min(tm, 256), rider=Rider([bf(ffn2_gate)], "chips"))
    def conv5(name, src, cb0, flip):
        out = None
        for part, seq, off in (("lat", seq_len, 0), ("ctx", ctx_len, n_lat // ctx_len)):
            out = tapsum_roll(f"{name}_{part}", src, cb0, w5, 0, seq_len=seq, n_seq=n_ex, row_blk_off=off, width=seq,
                              piece=seq, cb=cbw, ncb=xw // cbw, pad=w5.shape[0] // 2, flip=flip,
                              place=((n_tok, xw), off, 0, out))
        return out

    craw = conv5("xbc_conv", proj, xbc_cb, False)
    (xbc,) = rowwise("xbc_silu", fn_silu_bias, [row(craw)], [], [ssm_conv_b], [(n_tok, xw, F32)], tm=tm, n_tiles=n_tok // tm)
    ssd = dict(n_ex=n_ex, seq_len=seq_len, ctx_len=ctx_len, ds=ds)
    (y_f, hs_f), (wu2, wd2) = ssd_fwd("ssd_fwd_f", xbc, proj, dt_cb, dtb_f, alog_f, dsk_f, rev=False,
                                      rider=Rider([bf(ffn2_up), bf(ffn2_down)], "chips"), **ssd)
    (y_b, hs_b), (wout_g,) = ssd_fwd("ssd_fwd_b", xbc, proj, dt_cb, dtb_b, alog_b, dsk_b, rev=True,
                                     rider=Rider([bf(w_out)], "chips"), add=y_f, **ssd)
    wout = wout_g.reshape(2 * d, d)
    wo_y, wo_u = wout[:ds], wout[ds:]
    fn_gate = make_fn_gate_groupnorm(ds)
    (yn,) = rowwise("ssd_gate", fn_gate, [row(y_b), row(proj, d, 0)], [], [ssm_norm_w], [(n_lat, ds, BF16)],
                    tm=tm, n_tiles=lat_tiles)
    (u0,) = rowwise("glu", fn_glu, [row(proj, d, 1), row(proj, d, 2)], [], [], [(n_lat, d, F32)], tm=tm, n_tiles=lat_tiles)
    cb31 = max(LANES, d // 4)
    ncb31 = (d // 2) // cb31
    pad31 = w31.shape[0] // 2
    piece31 = min(seq_len, 4 * GRID_W)
    v_w = tapsum_roll("cconv_cols", u0, 0, w31, 0, seq_len=seq_len, n_seq=n_ex, row_blk_off=0, width=GRID_W,
                      piece=piece31, cb=cb31, ncb=ncb31, pad=pad31, flip=False)
    v_h = tapsum_rows("cconv_rows", u0, ncb31, w31, ncb31, seq_len=seq_len, n_seq=n_ex, cb=cb31, ncb=ncb31, pad=pad31, flip=False)
    (un,) = rowwise("cconv_ln", fn_ln_silu, [row(v_w), row(v_h)], [], [cconv_b, cconv_ln_w, cconv_ln_b], [(n_lat, d, BF16)],
                    tm=tm, n_tiles=lat_tiles)
    mix = matmul("mix_out", [(yn, wo_y), (un, wo_u)], "nn", tm=tm)
    seg_lat, first_lat = _segmenter(tm, seq_len, n_lat)
    (x2,) = rowwise("mix_resid", make_fn_resid(1.0), [row(x1), row(mix)], [lat(tabs[5])], [], [(n_lat, d, F32)],
                    tm=tm, n_tiles=lat_tiles, seg_fn=seg_lat)
    x3, saved2 = _ffn_fwd(plan, "ffn2", row(x2), n_lat, tm, seg_lat, lat(tabs[6]), lat(tabs[7]), lat(tabs[8]), norm_ffn2, wg2, wu2, wd2,
                          fuse_gate_up=True)
    d_x3, d_final, loss_vec = final_loss(x3, loss_target.reshape(n_lat, d), final_norm.reshape(1, d), tm=tm)

    shard_cols = lambda t: t.reshape(t.shape[0], N_CHIPS, -1).transpose(1, 0, 2)

    def pieces(t):
        t = jnp.pad(t, ((0, 0), (0, -t.shape[1] % 32), (0, 0)))
        return t.reshape(2 * N_CHIPS, t.shape[1] // 2, t.shape[2]).astype(BF16)

    scatter = lambda *ts: Rider([pieces(t) for t in ts], "all8", scatter=True)
    halves = lambda names, landed: Rider([sum_slots(f"sum_{nm}", r, BF16) for nm, r in zip(names, landed)], "sibling")
    swapped = {}
    plan.on("ffn2_up_dx", "sc_ffn2_down", lambda p: scatter(p.got["ffn2_d_wd"]))
    plan.on("ffn2_up_dw", "sc_ffn2_gate", lambda p: scatter(p.got["ffn2_d_wg"]))
    d_x2, (d_s6, d_s7, d_g8), d_nffn2 = _ffn_bwd(
        plan, "ffn2", d_x3, saved2, row(x2), n_lat, tm, seg_lat, first_lat, lat(tabs[6]), lat(tabs[7]), lat(tabs[8]), norm_ffn2,
        wg2, wu2, wd2, n_lat, None)
    d_mix, d_g5 = rowwise_bwd("mix_resid_bwd", make_fn_resid(1.0), [row(x1), row(mix)], [lat(tabs[5])], [], [[row(d_x2)]],
                              [None, (n_lat, BF16, None)], tm=tm, n_tiles=lat_tiles, seg_fn=seg_lat, first_fn=first_lat)
    d_yn = matmul("mix_out_dy", [(d_mix, wo_y)], "nt", tm=tm)
    d_un = matmul("mix_out_du", [(d_mix, wo_u)], "nt", tm=tm)
    d_wout = jnp.concatenate([matmul("mix_out_dwy", [(yn, d_mix)], "tn", out_dtype=BF16, tm=tm),
                              matmul("mix_out_dwu", [(un, d_mix)], "tn", out_dtype=BF16, tm=tm)])
    d_vw, d_vh, d_cb, d_lnw, d_lnb = rowwise_bwd(
        "cconv_ln_bwd", fn_ln_silu, [row(v_w), row(v_h)], [], [cconv_b, cconv_ln_w, cconv_ln_b], [[row(d_un)]],
        [(n_lat, F32, None)] * 2, tm=tm, n_tiles=lat_tiles)
    d_u0 = tapsum_roll("cconv_cols_dx", d_vw, 0, w31, 0, seq_len=seq_len, n_seq=n_ex, row_blk_off=0, width=GRID_W,
                       piece=piece31, cb=cb31, ncb=ncb31, pad=pad31, flip=True, place=((n_lat, d), 0, 0, None))
    d_u0 = tapsum_rows("cconv_rows_dx", d_vh, 0, w31, ncb31, seq_len=seq_len, n_seq=n_ex, cb=cb31, ncb=ncb31, pad=pad31,
                       flip=True, place=((n_lat, d), 0, ncb31, d_u0))
    d_w31 = jnp.concatenate([
        tapgrad_roll("cconv_cols_dw", d_vw, 0, 0, u0, 0, 0, n_tap=w31.shape[0], seq_len=seq_len, n_seq=n_ex, width=GRID_W,
                     piece=piece31, cb=cb31, ncb=ncb31, pad=pad31),
        tapgrad_rows("cconv_rows_dw", d_vh, 0, u0, ncb31, n_tap=w31.shape[0], seq_len=seq_len, n_seq=n_ex, cb=cb31,
                     ncb=ncb31, pad=pad31)], axis=1)
    d_ga, d_gb = rowwise_bwd("glu_bwd", fn_glu, [row(proj, d, 1), row(proj, d, 2)], [], [], [[row(d_u0)]],
                             [(n_lat, BF16, None)] * 2, tm=tm, n_tiles=lat_tiles)
    d_ysum, d_z, d_ssmnw = rowwise_bwd(
        "ssd_gate_bwd", fn_gate, [row(y_b), row(proj, d, 0)], [], [ssm_norm_w], [[row(d_yn)]],
        [(n_lat, F32, None), (n_lat, BF16, None)], tm=tm, n_tiles=lat_tiles)
    (dxbc_f, ddt_f, dalog_f, ddtb_f, ddsk), landed = ssd_bwd(
        "ssd_bwd_f", xbc, proj, dt_cb, hs_f, d_ysum, dtb_f, alog_f, dsk_f, rev=False,
        rider=scatter(plan.got["ffn2_d_wu"], d_wout.reshape(N_CHIPS, -1, d)), **ssd)
    (dxbc_b, ddt_b, dalog_b, ddtb_b, _), both = ssd_bwd(
        "ssd_bwd_b", xbc, proj, dt_cb, hs_b, d_ysum, dtb_b, alog_b, dsk_b, rev=True,
        rider=halves(["ffn2_down", "ffn2_gate"], plan.got["sc_ffn2_down"] + plan.got["sc_ffn2_gate"]), add=dxbc_f, **ssd)
    swapped.update(zip(["ffn2_down", "ffn2_gate"], both))
    (d_craw, d_conv_b), both = rowwise_bwd(
        "xbc_silu_bwd", fn_silu_bias, [row(craw)], [], [ssm_conv_b], [[row(dxbc_b)]],
        [(n_tok, F32, None)], tm=tm, n_tiles=n_tok // tm, rider=halves(["ffn2_up", "w_out"], landed))
    swapped.update(zip(["ffn2_up", "w_out"], both))
    d_pxbc = conv5("xbc_conv_dx", d_craw, 0, True)
    g5 = lambda name, seq, off: tapgrad_roll(name, d_craw, 0, off, proj, xbc_cb, off, n_tap=w5.shape[0], seq_len=seq,
                                             n_seq=n_ex, width=seq, piece=seq, cb=cbw, ncb=xw // cbw, pad=w5.shape[0] // 2)
    d_w5 = g5("xbc_conv_lat_dw", seq_len, 0) + g5("xbc_conv_ctx_dw", ctx_len, n_lat // ctx_len)
    lat_pairs = [(d_z, w_z), (d_ga, w_ga), (d_gb, w_gb), (d_pxbc, w_xbc), (ddt_f, w_dtp), (ddt_b, w_dtp)]
    d_h2 = matmul("mix_proj_dx_lat", lat_pairs, "nt", rows=n_lat, tm=min(tm, 256), place=(n_tok, 0, None))
    d_h2 = matmul("mix_proj_dx_ctx", lat_pairs[3:], "nt", rows=n_ctx_rows, row_off=n_lat, tm=min(tm, 256),
                  place=(n_tok, n_lat, d_h2))
    d_wz = matmul("mix_proj_dwz", [(d_z, h2)], "tn", out_dtype=BF16, rows=n_lat, tm=tm)
    d_wga = matmul("mix_proj_dwa", [(d_ga, h2)], "tn", out_dtype=BF16, rows=n_lat, tm=tm)
    d_wgb = matmul("mix_proj_dwb", [(d_gb, h2)], "tn", out_dtype=BF16, rows=n_lat, tm=tm)
    d_wxbc = matmul("mix_proj_dwx", [(d_pxbc, h2)], "tn", out_dtype=BF16, tm=tm)
    d_wdt = matmul("mix_proj_dwt", [(ddt_f, h2), (ddt_b, h2)], "tn", out_dtype=BF16, tm=tm)
    d_win_t = jnp.concatenate([d_wz, d_wxbc, d_wdt[:2 * n_head], d_wga, d_wgb]).reshape(N_CHIPS, -1, d)
    d_x1, d_s3, d_s4, d_nmix = rowwise_bwd(
        "mix_norm_bwd", fn_norm_mod, [row(x1)], [tabs[3], tabs[4]], [norm_mix], [[row(d_h2)]], [(n_tok, F32, None)],
        tm=tm, n_tiles=n_tok // tm, seg_fn=seg_all, first_fn=first_all, adds={0: (row(d_x2), lat_tiles)})
    mix_names = ["w_in", "ssm_conv_w", "cconv_w"]
    plan.on("ffn1_down_dx", "sc_conv", lambda p: scatter(shard_cols(d_w5), shard_cols(d_w31)))
    plan.on("ffn1_up_dx", "sc_win", lambda p: scatter(d_win_t))
    plan.on("ffn1_gate_dw", "sc_ffn1_down", lambda p: scatter(p.got["ffn1_d_wd"]))
    plan.on("ffn1_up_dw", "sc_ffn1_gate", lambda p: scatter(p.got["ffn1_d_wg"]))
    plan.on("ffn1_up_dw", "sw_mix", lambda p: halves(mix_names, p.got["sc_win"] + p.got["sc_conv"]))
    plan.on("ffn1_norm_bwd", "sc_ffn1_up", lambda p: scatter(p.got["ffn1_d_wu"]))
    plan.on("ffn1_norm_bwd", "sw_ffn1_down", lambda p: halves(["ffn1_down"], p.got["sc_ffn1_down"]))
    d_xt, (d_s0, d_s1, d_g2), d_nffn1 = _ffn_bwd(
        plan, "ffn1", d_x1, saved1, xt, n_tok, tm, seg_all, first_all, tabs[0], tabs[1], tabs[2], norm_ffn1, wg1, wu1, wd1,
        n_lat, lat_tiles)
    swapped.update(zip(mix_names + ["ffn1_down"], plan.got["sw_mix"] + plan.got["sw_ffn1_down"]))
    last_names = ["ffn1_gate", "ffn1_up"]
    last = halves(last_names, plan.got["sc_ffn1_gate"] + plan.got["sc_ffn1_up"])
    grad_x = d_xt.reshape(n_ex, seq_len, d)

    with_ctx0 = lambda t: jnp.concatenate([t, jnp.zeros((1, 1, d), F32)])
    d_tabs = [d_s0, d_s1, d_g2, d_s3, d_s4, with_ctx0(d_g5), with_ctx0(d_s6), with_ctx0(d_s7), with_ctx0(d_g8)]
    d_mod_rows = jnp.concatenate([t[:, 0, :] for t in d_tabs], axis=1)
    n_pad_rows = -(-(n_ex + 1) // 8) * 8
    d_mod_rows = jnp.concatenate([d_mod_rows, jnp.zeros((n_pad_rows - n_ex - 1, 9 * d), F32)])
    small = [("loss", loss_vec), ("norm_ffn1", d_nffn1), ("norm_mix", d_nmix), ("ssm_conv_b", d_conv_b),
             ("dt_bias_fwd", ddtb_f[:, :n_head]), ("dt_bias_bwd", ddtb_b[:, :n_head]), ("a_log_fwd", dalog_f[:, :n_head]),
             ("a_log_bwd", dalog_b[:, :n_head]), ("ssm_d", ddsk[:, :n_head]), ("ssm_norm_w", d_ssmnw), ("cconv_b", d_cb),
             ("cconv_ln_w", d_lnw), ("cconv_ln_b", d_lnb), ("norm_ffn2", d_nffn2), ("final_norm", d_final)]
    n_small = sum(v.size for _, v in small)
    n_pack = -(-n_small // (8 * LANES)) * (8 * LANES)
    pack = jnp.concatenate([v.reshape(-1) for _, v in small] + [jnp.zeros((n_pack - n_small,), F32)]).reshape(-1, LANES)
    (pack_all, d_mod_all), both = exchange_many("gather_small_swap_last", [Rider([pack, d_mod_rows], "all8"), last])
    swapped.update(zip(last_names, both))
    pack_sum = sum_slots("small_sum", pack_all)
    loss = loss_total(pack_sum.reshape(1, n_pack), d).reshape(())
    flat_sum = pack_sum.reshape(-1)
    small_grads, pos = {}, 0
    for nm, v in small:
        small_grads[nm] = flat_sum[pos:pos + v.size]
        pos += v.size
    d_mod_all = d_mod_all.reshape(8 * n_pad_rows, 9 * d)
    cond_rows = [jnp.concatenate([cond[j * n_ex:(j + 1) * n_ex], c_ctx[None, :],
                                  jnp.zeros((n_pad_rows - n_ex - 1, d), F32)]) for j in range(8)]
    cond_bwd = jnp.concatenate(cond_rows)
    d_mod_shard = lax.dynamic_slice(d_mod_all, (0, chip * mod_w), (8 * n_pad_rows, mod_w))
    g_wmod, g_bmod, q_part = mod_bwd(cond_bwd, d_mod_shard, d_mod_all, w_mod[0],
                                     tuple(j * n_pad_rows + n_ex for j in range(8)))
    (q_all,) = exchange("gather_cctx", [q_part], "all8")
    g_cctx = cctx_grad(q_all, c_ctx.reshape(1, d))
    small_grads["c_ctx"], small_grads["b_mod"] = g_cctx.reshape(-1), g_bmod.reshape(-1)

    transposed = {"ffn1_gate", "ffn1_up", "ffn2_gate", "ffn2_up", "w_in"}
    results = {}
    for nm, both in swapped.items():
        flip = (lambda t: jnp.swapaxes(t, 1, 2)) if nm in transposed else (lambda t: t)
        shape = flip(weights[nm]).shape
        two_d = lambda t: flip(t).reshape(shape[-2], shape[-1])
        g_full = both.reshape(1, -1, shape[-1])[:, :shape[-2]]
        results[nm] = [flip(r.reshape(shape)) for r in
                       adamw(f"adamw_{nm}", two_d(weights[nm]), g_full, two_d(mom1[nm]), two_d(mom2[nm]))]
    results["w_mod"] = [r.reshape(w_mod.shape) for r in adamw("adamw_w_mod", w_mod[0], g_wmod[None], m_w_mod[0], v_w_mod[0])]
    small_names = [nm for nm in order if nm not in results]
    n_sm = sum(weights[nm].size for nm in small_names)
    n_smp = -(-n_sm // (8 * LANES)) * (8 * LANES)
    packed = lambda src: jnp.concatenate([src[nm].reshape(-1) for nm in small_names] + [jnp.zeros((n_smp - n_sm,), F32)]).reshape(-1, LANES)
    sm_out = adamw("adamw_small", packed(weights), packed(small_grads)[None], packed(mom1), packed(mom2))
    pos = 0
    for nm in small_names:
        size = weights[nm].size
        results[nm] = [r.reshape(-1)[pos:pos + size].reshape(weights[nm].shape) for r in sm_out]
        pos += size
    return (loss, grad_x, *[results[nm][0] for nm in order], *[results[nm][1] for nm in order],
            *[results[nm][2] for nm in order], *[results[nm][3] for nm in order])
```

```python
import functools
import math

import jax
import jax.numpy as jnp
from jax import lax
from jax.experimental import pallas as pl
from jax.experimental.pallas import tpu as pltpu

F32 = jnp.float32
BF16 = jnp.bfloat16
HI = lax.Precision.HIGHEST
MESH = pl.DeviceIdType.MESH

EPS = 1e-6
GRID_W = 64
HEAD_DIM = 64
N_STATE = 128
CHUNK = 128
LANES = 128
N_CHIPS = 4
ADAM_LR, ADAM_B1, ADAM_B2, ADAM_EPS, ADAM_WD, ADAM_STEP = 0.001, 0.9, 0.999, 1e-08, 0.01, 10
VMEM_CAP = 56 * 1024 * 1024


def _params(vmem_bytes=None, n_axes=1):
    kw = dict(dimension_semantics=("arbitrary",) * n_axes)
    if vmem_bytes is not None:
        kw["vmem_limit_bytes"] = int(min(VMEM_CAP, max(32 * 1024 * 1024, vmem_bytes)))
    return pltpu.CompilerParams(**kw)


def _big(shape, dtype):
    return pltpu.HBM(tuple(shape), dtype)


def _in_hbm(args):
    return [pltpu.with_memory_space_constraint(a, pltpu.HBM) if a.size * a.dtype.itemsize >= (1 << 20) else a for a in args]


def _nbytes(shape, dtype):
    return math.prod(shape) * jnp.dtype(dtype).itemsize


def _row_tile(rows, width, cap_bytes=1 << 20, mult=8):
    best = None
    for t in range(mult, rows + 1, mult):
        if rows % t == 0 and t * width * 4 <= cap_bytes:
            best = t
    return best if best is not None else rows


_MODES = {"all8": (8, (1, 2, 3, 4, 5, 6, 7), 0), "chips": (4, (2, 4, 6), 1), "sibling": (2, (1,), 0)}


class Rider:
    def __init__(self, arrs, mode, scatter=False):
        self.arrs, self.scatter = list(arrs), scatter
        self.nslot, self.deltas, self.shift = _MODES[mode]
        self.n = len(self.arrs)
        self.out_shape = [jax.ShapeDtypeStruct((self.nslot,) + (a.shape[1:] if scatter else a.shape), a.dtype)
                          for a in self.arrs]
        any_spec = pl.BlockSpec(memory_space=pl.ANY)
        self.in_specs = [any_spec] * self.n
        self.out_specs = [any_spec] * self.n
        n_peer = len(self.deltas)
        self.scratch = [pltpu.SemaphoreType.DMA((self.n, n_peer)), pltpu.SemaphoreType.DMA((self.n, n_peer)),
                        pltpu.SemaphoreType.DMA((self.n,))]

    def _copies(self, ins, outs, sems, arrivals):
        send_sems, recv_sems, local_sems = sems
        x, y, c = lax.axis_index("x"), lax.axis_index("y"), lax.axis_index("c")
        me = 4 * x + 2 * y + c
        slot_of = lambda dev: (dev >> self.shift) & (self.nslot - 1)
        src = lambda a, slot: ins[a].at[slot] if self.scatter else ins[a]
        flip = lambda v, bit: 1 - v if bit else v

        def remote(a, k, d, from_slot, to_slot):
            return pltpu.make_async_remote_copy(
                src_ref=src(a, from_slot), dst_ref=outs[a].at[to_slot], send_sem=send_sems.at[a, k],
                recv_sem=recv_sems.at[a, k], device_id=(flip(x, (d >> 2) & 1), flip(y, (d >> 1) & 1), flip(c, d & 1)),
                device_id_type=MESH)

        mine = slot_of(me)
        local = [pltpu.make_async_copy(src(a, mine), outs[a].at[mine], local_sems.at[a]) for a in range(self.n)]
        sends = [remote(a, k, d, slot_of(me ^ d), mine) for k, d in enumerate(self.deltas) for a in range(self.n)]
        if not arrivals:
            return local, sends
        return local, sends, [remote(a, k, d, mine, slot_of(me ^ d)) for k, d in enumerate(self.deltas) for a in range(self.n)]

    def start(self, ins, outs, sems):
        local, sends = self._copies(ins, outs, sems, arrivals=False)
        for cp in local + sends:
            cp.start()

    def wait(self, ins, outs, sems):
        local, sends, recvs = self._copies(ins, outs, sems, arrivals=True)
        for cp in recvs:
            cp.wait_recv()
        for cp in sends:
            cp.wait_send()
        for cp in local:
            cp.wait()


class Riders:
    def __init__(self, riders):
        self.riders = list(riders)
        self.n = sum(r.n for r in self.riders)
        cat = lambda attr: [v for r in self.riders for v in getattr(r, attr)]
        self.arrs, self.out_shape, self.in_specs = cat("arrs"), cat("out_shape"), cat("in_specs")
        self.out_specs, self.scratch = cat("out_specs"), cat("scratch")

    def _each(self, method, ins, outs, sems):
        i = s = 0
        for r in self.riders:
            getattr(r, method)(ins[i:i + r.n], outs[i:i + r.n], sems[s:s + len(r.scratch)])
            i, s = i + r.n, s + len(r.scratch)

    def start(self, ins, outs, sems):
        self._each("start", ins, outs, sems)

    def wait(self, ins, outs, sems):
        self._each("wait", ins, outs, sems)


class _Hosted:
    def __init__(self, rider, n_in, n_out, n_scratch, grid):
        self.rider, self.n_in, self.n_out, self.n_scratch, self.grid = rider, n_in, n_out, n_scratch, grid
        self.n = rider.n if rider else 0

    def split(self, refs):
        a, b = self.n_in, self.n_in + self.n
        c, e = b + self.n_out, b + self.n_out + self.n
        self._r = (refs[a:b], refs[c:e], refs[e + self.n_scratch:])
        if self.rider:
            ids = [pl.program_id(ax) for ax in range(len(self.grid))]
            first = functools.reduce(jnp.logical_and, [i == 0 for i in ids]) if ids else True
            pl.when(first)(lambda: self.rider.start(*self._r))
        return refs[:a], refs[b:c], refs[e:e + self.n_scratch]

    def finish(self):
        if self.rider:
            ids = [pl.program_id(ax) for ax in range(len(self.grid))]
            last = functools.reduce(jnp.logical_and, [i == n - 1 for i, n in zip(ids, self.grid)]) if ids else True
            pl.when(last)(lambda: self.rider.wait(*self._r))

    def call_args(self, in_specs, out_shape, out_specs, scratch, args):
        r = self.rider
        if not r:
            return list(in_specs), tuple(out_shape), tuple(out_specs), list(scratch), list(args)
        return (list(in_specs) + r.in_specs, tuple(out_shape) + tuple(r.out_shape), tuple(out_specs) + tuple(r.out_specs),
                list(scratch) + r.scratch, list(args) + r.arrs)

    def results(self, res, unwrap=True):
        res = list(res) if isinstance(res, (tuple, list)) else [res]
        host = res[:self.n_out]
        host = host[0] if (self.n_out == 1 and unwrap) else tuple(host)
        return (host, res[self.n_out:]) if self.rider else host


def exchange_many(name, riders):
    both = Riders(riders)

    def body(*refs):
        ins, outs, sems = refs[:both.n], refs[both.n:2 * both.n], refs[2 * both.n:]
        both.start(ins, outs, sems)
        both.wait(ins, outs, sems)

    res = list(pl.pallas_call(
        body, name=name, out_shape=tuple(both.out_shape), in_specs=both.in_specs, out_specs=tuple(both.out_specs),
        scratch_shapes=both.scratch,
    )(*both.arrs))
    split = []
    for r in riders:
        split.append(res[:r.n])
        res = res[r.n:]
    return split


def exchange(name, arrs, mode, scatter=False):
    rider = Rider(arrs, mode, scatter)

    def body(*refs):
        ins, outs, sems = refs[:rider.n], refs[rider.n:2 * rider.n], refs[2 * rider.n:]
        rider.start(ins, outs, sems)
        rider.wait(ins, outs, sems)

    return pl.pallas_call(
        body, name=name, out_shape=tuple(rider.out_shape), in_specs=rider.in_specs, out_specs=tuple(rider.out_specs),
        scratch_shapes=rider.scratch,
    )(*arrs)


_DIMS = {"nn": (((1,), (0,)), ((), ())), "nt": (((1,), (1,)), ((), ())), "tn": (((0,), (0,)), ((), ()))}


def matmul(name, pairs, kind, *, a_ch=False, b_ch=False, out_ch=False, out_dtype=F32, rows=None, row_off=0, tm=512,
           rider=None, post=None, fold=False, place=None):
    a0, b0 = pairs[0]
    n_chunk = a0.shape[0] if a_ch else (b0.shape[0] if b_ch else 1)
    total_rows = a0.shape[-2]
    rows = total_rows - row_off if rows is None else rows
    tm = min(tm, rows)
    assert rows % tm == 0 and row_off % tm == 0, (name, rows, tm, row_off)
    n_rt, off = rows // tm, row_off // tm
    dims = _DIMS[kind]
    n_pair = len(pairs)

    if kind == "tn":
        grid, red_axis, n_red = (n_chunk, n_rt), 1, n_rt
        a_idx = (lambda k, i: (k, i + off, 0)) if a_ch else (lambda k, i: (i + off, 0))
        b_idx = (lambda k, i: (k, i + off, 0)) if b_ch else (lambda k, i: (i + off, 0))
        a_blk = lambda a: ((None, tm, a.shape[-1]) if a_ch else (tm, a.shape[-1]))
        b_blk = lambda b: ((None, tm, b.shape[-1]) if b_ch else (tm, b.shape[-1]))
        o2 = (a0.shape[-1], b0.shape[-1])
        out_shape = ((n_chunk,) + o2) if out_ch else o2
        out_spec = pl.BlockSpec((None,) + o2, lambda k, i: (k, 0, 0)) if out_ch else pl.BlockSpec(o2, lambda k, i: (0, 0))
        acc_shape = o2
    else:
        n_out = b0.shape[-1] if kind == "nn" else b0.shape[-2]
        b2 = b0.shape[-2:]
        if a_ch and b_ch and not out_ch and fold:
            grid, red_axis, n_red = (n_rt,), None, 1
            a_idx, b_idx = (lambda i: (0, i + off, 0)), (lambda i: (0, 0, 0))
            a_blk = lambda a: (n_chunk, tm, a.shape[-1])
            b_blk = lambda b: tuple(b.shape)
            out_shape, out_spec = (rows, n_out), pl.BlockSpec((tm, n_out), lambda i: (i, 0))
        elif a_ch and b_ch and not out_ch:
            grid, red_axis, n_red = (n_rt, n_chunk), 1, n_chunk
            a_idx, b_idx = (lambda i, k: (k, i + off, 0)), (lambda i, k: (k, 0, 0))
            a_blk = lambda a: (None, tm, a.shape[-1])
            b_blk = lambda b: (None,) + tuple(b.shape[-2:])
            out_shape, out_spec = (rows, n_out), pl.BlockSpec((tm, n_out), lambda i, k: (i, 0))
        elif out_ch and fold:
            assert b_ch and not a_ch and all(a is a0 for a, _ in pairs)
            grid, red_axis, n_red = (n_rt,), None, 1
            a_idx, b_idx = (lambda i: (i + off, 0)), (lambda i: (0, 0, 0))
            a_blk = lambda a: (tm, a.shape[-1])
            b_blk = lambda b: tuple(b.shape)
            out_shape, out_spec = (n_chunk, rows, n_out), pl.BlockSpec((n_chunk, tm, n_out), lambda i: (0, i, 0))
        elif out_ch:
            assert b_ch and not a_ch
            grid, red_axis, n_red = (n_chunk, n_rt), None, 1
            a_idx, b_idx = (lambda k, i: (i + off, 0)), (lambda k, i: (k, 0, 0))
            a_blk = lambda a: (tm, a.shape[-1])
            b_blk = lambda b: (None,) + tuple(b.shape[-2:])
            out_shape, out_spec = (n_chunk, rows, n_out), pl.BlockSpec((None, tm, n_out), lambda k, i: (k, i, 0))
        else:
            assert not (a_ch or b_ch)
            grid, red_axis, n_red = (n_rt,), None, 1
            a_idx, b_idx = (lambda i: (i + off, 0)), (lambda i: (0, 0))
            a_blk = lambda a: (tm, a.shape[-1])
            b_blk = lambda b: tuple(b.shape)
            out_shape, out_spec = (rows, n_out), pl.BlockSpec((tm, n_out), lambda i: (i, 0))
            if place is not None:
                out_shape, o_off = (place[0], n_out), place[1] // tm
                out_spec = pl.BlockSpec((tm, n_out), lambda i: (i + o_off, 0))
        acc_shape = (tm, n_out)

    into = [] if place is None or place[2] is None else [place[2]]
    post_ins, post_fn, out_dtypes = ([], None, [out_dtype]) if post is None else post
    hosted = _Hosted(rider, 2 * n_pair + len(post_ins) + len(into), len(out_dtypes), int(n_red > 1), grid)

    def body(*refs):
        ins, outs, scr = hosted.split(refs)

        def compute():
            acc = None
            for p in range(n_pair):
                for k in ([None] if not fold else range(n_chunk)):
                    pick = (lambda r: r[...]) if k is None else (lambda r: r[k])
                    d = lax.dot_general(pick(ins[2 * p]).astype(BF16), pick(ins[2 * p + 1]).astype(BF16), dims,
                                        preferred_element_type=F32)
                    acc = d if acc is None else acc + d
            return acc

        def emit(acc):
            vals = (acc,) if post_fn is None else post_fn(
                acc, *[r[...].astype(F32) for r in ins[2 * n_pair:2 * n_pair + len(post_ins)]])
            for o_ref, v in zip(outs, vals):
                o_ref[...] = v.astype(o_ref.dtype)

        if out_ch and fold:
            a_tile = ins[0][...].astype(BF16)
            for k in range(n_chunk):
                accs = [lax.dot_general(a_tile, ins[2 * p + 1][k].astype(BF16), dims, preferred_element_type=F32)
                        for p in range(n_pair)]
                tiles = [r[k].astype(F32) for r in ins[2 * n_pair:2 * n_pair + len(post_ins)]]
                vals = tuple(accs) if post_fn is None else post_fn(*accs, *tiles)
                for o_ref, v in zip(outs, vals):
                    o_ref[k] = v.astype(o_ref.dtype)
        elif n_red == 1:
            emit(compute())
        else:
            acc_ref = scr[0]
            r = pl.program_id(red_axis)

            @pl.when(r == 0)
            def _():
                acc_ref[...] = jnp.zeros_like(acc_ref)

            acc_ref[...] += compute()

            @pl.when(r == n_red - 1)
            def _():
                emit(acc_ref[...])
        hosted.finish()

    in_specs, args, vmem = [], [], 0
    for a, b in pairs:
        in_specs += [pl.BlockSpec(a_blk(a), a_idx), pl.BlockSpec(b_blk(b), b_idx)]
        args += [a, b]
        vmem += 2 * (_nbytes([s for s in a_blk(a) if s], a.dtype) + _nbytes([s for s in b_blk(b) if s], b.dtype))
    in_specs += [out_spec] * len(post_ins)
    args += list(post_ins)
    aliases = {len(args): 0} if into else {}
    in_specs += [pl.BlockSpec(memory_space=pl.ANY)] * len(into)
    args += into
    tiles_per_step = n_chunk if (out_ch and fold) else 1
    vmem += (3 + 2 * n_pair + tiles_per_step * (len(post_ins) + len(out_dtypes))) * _nbytes(acc_shape, F32)
    scratch = [pltpu.VMEM(acc_shape, F32)] if n_red > 1 else []
    in_specs, out_shapes, out_specs, scratch, args = hosted.call_args(
        in_specs, [_big(out_shape, dt) for dt in out_dtypes], [out_spec] * len(out_dtypes), scratch, args)
    return hosted.results(pl.pallas_call(
        body, name=name, out_shape=out_shapes, grid=grid, in_specs=in_specs, out_specs=out_specs,
        input_output_aliases=aliases, scratch_shapes=scratch, compiler_params=_params(vmem + (8 << 20), len(grid)),
    )(*_in_hbm(args)))


def row(arr, width=None, cb=0, roff=0):
    return (arr, arr.shape[-1] if width is None else width, cb, roff)


def two_rows(first, second, limit):
    return (first, first.shape[-1], 0, 0, (second, limit))


def _row_inputs(rows, tm):
    specs, arrs, slots = [], [], []
    for d in rows:
        second, limit = d[4] if len(d) > 4 else (None, None)
        slots.append((len(arrs), limit))
        specs.append(_row_spec(d[:4], tm, limit))
        arrs.append(d[0])
        if second is not None:
            specs.append(pl.BlockSpec((tm, d[1]), lambda i, limit=limit: (jnp.maximum(i - limit, 0), 0)))
            arrs.append(second)

    def read(refs, i):
        vals = []
        for at, limit in slots:
            v = refs[at][...].astype(F32)
            vals.append(v if limit is None else jnp.where(i < limit, v, refs[at + 1][...].astype(F32)))
        return vals

    return specs, arrs, read


def _row_spec(desc, tm, limit=None):
    _, width, cb, roff = desc[:4]
    if limit is None:
        return pl.BlockSpec((tm, width), lambda i: (i + roff, cb))
    return pl.BlockSpec((tm, width), lambda i: (jnp.minimum(i, limit - 1) + roff, cb))


def _segmenter(tm, seq_len, n_lat):
    seg = lambda i: jnp.where(i * tm < n_lat, (i * tm) // seq_len, n_lat // seq_len)
    first = lambda i: jnp.where(i * tm < n_lat, (i * tm) % seq_len == 0, i * tm == n_lat)
    return seg, first


def rowwise(name, fn, rows, segs, params, outs, *, tm, n_tiles, seg_fn=None, rider=None):
    row_specs, row_arrs, read_rows = _row_inputs(rows, tm)
    n_r, n_s, n_p = len(row_arrs), len(segs), len(params)
    hosted = _Hosted(rider, n_r + n_s + n_p, len(outs), 0, (n_tiles,))

    def body(*refs):
        ins, out_refs, _ = hosted.split(refs)
        vals = read_rows(ins[:n_r], pl.program_id(0)) + [r[...] for r in ins[n_r:]]
        res = fn(*vals)
        for o_ref, v in zip(out_refs, res):
            o_ref[...] = v.astype(o_ref.dtype)
        hosted.finish()

    in_specs = list(row_specs)
    in_specs += [pl.BlockSpec((None, 1, s.shape[-1]), lambda i: (seg_fn(i), 0, 0)) for s in segs]
    in_specs += [pl.BlockSpec(p.shape, lambda i: (0, 0)) for p in params]
    vmem = sum(2 * tm * d[1] * 4 for d in rows) + sum(3 * tm * w * 4 for _, w, _ in outs) + sum(2 * p.size * 4 for p in params)
    in_specs, out_shapes, out_specs, scratch, args = hosted.call_args(
        in_specs, [_big((r, w), dt) for r, w, dt in outs],
        [pl.BlockSpec((tm, w), lambda i: (i, 0)) for _, w, _ in outs], [], row_arrs + list(segs) + list(params))
    return hosted.results(pl.pallas_call(
        body, name=name, grid=(n_tiles,), in_specs=in_specs, out_shape=out_shapes, out_specs=out_specs,
        scratch_shapes=scratch, compiler_params=_params(2 * vmem + (8 << 20)),
    )(*_in_hbm(args)), unwrap=False)


def rowwise_bwd(name, fn, rows, segs, params, cts, row_grads, *, tm, n_tiles, seg_fn=None, first_fn=None, adds=None,
                rider=None):
    adds = adds or {}
    need = [k for k, v in enumerate(row_grads) if v is not None]
    row_specs, row_arrs, read_rows = _row_inputs(rows, tm)
    n_r, n_s, n_p = len(row_arrs), len(segs), len(params)
    n_ct = sum(len(lst) for lst in cts)
    add_keys = sorted(adds)
    hosted = _Hosted(rider, n_r + n_s + n_p + n_ct + len(add_keys), len(need) + n_s + n_p, 0, (n_tiles,))

    def body(*refs):
        host_in, host_out, _ = hosted.split(refs)
        it = iter(list(host_in) + list(host_out))
        row_refs = [next(it) for _ in range(n_r)]
        seg_refs = [next(it) for _ in range(n_s)]
        par_refs = [next(it) for _ in range(n_p)]
        ct_refs = [[next(it) for _ in lst] for lst in cts]
        add_refs = {k: next(it) for k in add_keys}
        rg_refs = {k: next(it) for k in need}
        sg_refs = [next(it) for _ in range(n_s)]
        pg_refs = [next(it) for _ in range(n_p)]
        i = pl.program_id(0)
        rv = read_rows(row_refs, i)
        sv = [r[...] for r in seg_refs]
        pv = [r[...] for r in par_refs]

        def f(*args):
            rr = list(rv)
            for j, k in enumerate(need):
                rr[k] = args[j]
            return fn(*rr, *args[len(need):])

        _, vjp = jax.vjp(f, *[rv[k] for k in need], *sv, *pv)
        ctv = []
        for lst in ct_refs:
            acc = lst[0][...].astype(F32)
            for r in lst[1:]:
                acc = acc + r[...].astype(F32)
            ctv.append(acc)
        g = vjp(tuple(ctv))
        for j, k in enumerate(need):
            gv = g[j]
            if k in adds:
                lim = adds[k][1]
                av = add_refs[k][...].astype(F32)
                gv = gv + (av if lim is None else jnp.where(i < lim, av, 0.0))
            lim = row_grads[k][2]
            if lim is None:
                rg_refs[k][...] = gv.astype(rg_refs[k].dtype)
            else:
                @pl.when(i < lim)
                def _(gv=gv, k=k):
                    rg_refs[k][...] = gv.astype(rg_refs[k].dtype)
        if n_s:
            opens = first_fn(i)
            for ref, gv in zip(sg_refs, g[len(need):len(need) + n_s]):
                @pl.when(opens)
                def _(ref=ref, gv=gv):
                    ref[...] = gv

                @pl.when(jnp.logical_not(opens))
                def _(ref=ref, gv=gv):
                    ref[...] += gv
        for ref, gv in zip(pg_refs, g[len(need) + n_s:]):
            @pl.when(i == 0)
            def _(ref=ref, gv=gv):
                ref[...] = gv

            @pl.when(i > 0)
            def _(ref=ref, gv=gv):
                ref[...] += gv
        hosted.finish()

    seg_spec = lambda s: pl.BlockSpec((None, 1, s.shape[-1]), lambda i: (seg_fn(i), 0, 0))
    par_spec = lambda p: pl.BlockSpec(p.shape, lambda i: (0, 0))
    in_specs = list(row_specs) + [seg_spec(s) for s in segs] + [par_spec(p) for p in params]
    args = row_arrs + list(segs) + list(params)
    for lst in cts:
        in_specs += [_row_spec(d, tm) for d in lst]
        args += [d[0] for d in lst]
    for k in add_keys:
        in_specs.append(_row_spec(adds[k][0], tm, adds[k][1]))
        args.append(adds[k][0][0])
    out_shape, out_specs = [], []
    for k in need:
        n_rows, dt, lim = row_grads[k]
        out_shape.append(_big((n_rows, rows[k][1]), dt))
        out_specs.append(_row_spec((None, rows[k][1], 0, 0), tm, lim))
    for s in segs:
        out_shape.append(jax.ShapeDtypeStruct(s.shape, F32))
        out_specs.append(seg_spec(s))
    for p in params:
        out_shape.append(jax.ShapeDtypeStruct(p.shape, F32))
        out_specs.append(par_spec(p))
    vmem = sum(tm * d[1] * 4 for d in rows) * 6 + n_ct * tm * max(d[1] for d in rows) * 8
    in_specs, out_shape, out_specs, scratch, args = hosted.call_args(in_specs, out_shape, out_specs, [], args)
    return hosted.results(pl.pallas_call(
        body, name=name, grid=(n_tiles,), in_specs=in_specs, out_shape=out_shape, out_specs=out_specs,
        scratch_shapes=scratch, compiler_params=_params(vmem + (8 << 20)),
    )(*_in_hbm(args)), unwrap=False)


def _silu(v):
    return v * jax.nn.sigmoid(v)


def _rms(v, w):
    return v * lax.rsqrt(jnp.mean(v * v, axis=-1, keepdims=True) + EPS) * w


def fn_norm_mod(x, shift, scale, w):
    return (_rms(x, w) * (1.0 + scale) + shift,)


def fn_act(g, u):
    return (_silu(g) * u,)


def make_fn_resid(coef):
    def fn(x, f, gate):
        return (x + coef * gate * f,)
    return fn


def fn_silu_bias(v, b):
    return (_silu(v + b),)


def make_fn_gate_groupnorm(width):
    half = width // 2

    def fn(y_both, z, w):
        y = y_both * _silu(z)
        lane = lax.broadcasted_iota(jnp.int32, y.shape, 1)
        lo = lane < half
        sq = y * y
        s_lo = jnp.sum(jnp.where(lo, sq, 0.0), axis=-1, keepdims=True)
        s_hi = jnp.sum(jnp.where(lo, 0.0, sq), axis=-1, keepdims=True)
        r = jnp.where(lo, lax.rsqrt(s_lo / half + EPS), lax.rsqrt(s_hi / half + EPS))
        return (y * r * w,)
    return fn


def fn_glu(a, b):
    return (a * jax.nn.sigmoid(b),)


def fn_ln_silu(vw, vh, cb, lw, lb):
    v = jnp.concatenate([vw, vh], axis=-1) + cb
    mu = jnp.mean(v, axis=-1, keepdims=True)
    var = jnp.mean(jnp.square(v - mu), axis=-1, keepdims=True)
    return (_silu((v - mu) * lax.rsqrt(var + EPS) * lw + lb),)


def _col_tile(width):
    return width // 3 if width % (3 * LANES) == 0 else width


def mod_fwd(a_rows, w_shard, b_shard):
    n, d = a_rows.shape
    ws = w_shard.shape[1]
    tn = _col_tile(ws)

    def body(a_ref, w_ref, b_ref, o_ref):
        a = _silu(a_ref[...]).astype(BF16)
        o_ref[...] = jnp.dot(a, w_ref[...].astype(BF16), preferred_element_type=F32) + b_ref[...]

    return pl.pallas_call(
        body, name="mod_fwd", grid=(ws // tn,), out_shape=jax.ShapeDtypeStruct((n, ws), F32),
        in_specs=[pl.BlockSpec((n, d), lambda j: (0, 0)), pl.BlockSpec((d, tn), lambda j: (0, j)),
                  pl.BlockSpec((1, tn), lambda j: (0, j))],
        out_specs=pl.BlockSpec((n, tn), lambda j: (0, j)), compiler_params=_params(),
    )(a_rows, w_shard, b_shard)


def mod_bwd(a_rows, d_shard, d_full, w_shard, ctx_rows):
    n, d = a_rows.shape
    ws = w_shard.shape[1]
    tn = _col_tile(ws)
    n_ct = ws // tn

    def body(a_ref, ds_ref, df_ref, w_ref, gw_ref, gb_ref, q_ref):
        j = pl.program_id(0)
        a = _silu(a_ref[...])
        ds = ds_ref[...]
        gw_ref[...] = lax.dot_general(a, ds, _DIMS["tn"], precision=HI, preferred_element_type=F32)
        dctx = ds[ctx_rows[0]:ctx_rows[0] + 1, :]
        for r in ctx_rows[1:]:
            dctx = dctx + ds[r:r + 1, :]
        q = lax.dot_general(jnp.broadcast_to(dctx, (8, tn)), w_ref[...], _DIMS["nt"], precision=HI,
                            preferred_element_type=F32)

        @pl.when(j == 0)
        def _():
            q_ref[...] = q
            df = df_ref[...]
            acc = df[0:1, :]
            for r in range(1, n):
                acc = acc + df[r:r + 1, :]
            gb_ref[...] = acc

        @pl.when(j > 0)
        def _():
            q_ref[...] += q

    return pl.pallas_call(
        body, name="mod_bwd", grid=(n_ct,),
        out_shape=(jax.ShapeDtypeStruct((d, ws), F32), jax.ShapeDtypeStruct((1, d_full.shape[1]), F32),
                   jax.ShapeDtypeStruct((8, d), F32)),
        in_specs=[pl.BlockSpec((n, d), lambda j: (0, 0)), pl.BlockSpec((n, tn), lambda j: (0, j)),
                  pl.BlockSpec(d_full.shape, lambda j: (0, 0)), pl.BlockSpec((d, tn), lambda j: (0, j))],
        out_specs=(pl.BlockSpec((d, tn), lambda j: (0, j)), pl.BlockSpec((1, d_full.shape[1]), lambda j: (0, 0)),
                   pl.BlockSpec((8, d), lambda j: (0, 0))),
        compiler_params=_params(40 << 20),
    )(a_rows, d_shard, d_full, w_shard)


def _shifted(xs, d, tok, width):
    if d == 0:
        return xs
    n = xs.shape[0]
    sh = pltpu.roll(xs, (-d) % n, axis=0)
    return jnp.where((tok + d >= 0) & (tok + d < width), sh, 0.0)


def _placed(out_shape, place):
    if place is None:
        return out_shape, 0, 0, None
    return place


def tapsum_roll(name, x, xcb, w, wcb, *, seq_len, n_seq, row_blk_off, width, piece, cb, ncb, pad, flip, place=None):
    n_tap = w.shape[0]
    n_piece = seq_len // piece
    out_shape, o_rb, o_cb, into = _placed((n_seq * seq_len, ncb * cb), place)

    def body(x_ref, w_ref, *rest):
        o_ref = rest[-1]
        wv = w_ref[...]
        tok = lax.broadcasted_iota(jnp.int32, (piece, 1), 0) % width

        def do_piece(p, carry):
            start = pl.multiple_of(p * piece, piece)
            xs = x_ref[pl.ds(start, piece), :]
            acc = jnp.zeros_like(xs)
            for k in range(n_tap):
                d = pad - k if flip else k - pad
                acc = acc + wv[k:k + 1, :] * _shifted(xs, d, tok, width)
            o_ref[pl.ds(start, piece), :] = acc
            return carry

        lax.fori_loop(0, n_piece, do_piece, 0)

    extra = [] if into is None else [into]
    return pl.pallas_call(
        body, name=name, grid=(ncb, n_seq), out_shape=_big(out_shape, F32),
        in_specs=[pl.BlockSpec((seq_len, cb), lambda j, s: (row_blk_off + s, xcb + j)),
                  pl.BlockSpec((n_tap, cb), lambda j, s: (0, wcb + j))] + [pl.BlockSpec(memory_space=pl.ANY)] * len(extra),
        out_specs=pl.BlockSpec((seq_len, cb), lambda j, s: (o_rb + s, o_cb + j)),
        input_output_aliases={2: 0} if extra else {},
        compiler_params=_params(8 * seq_len * cb * 4 + (8 << 20), 2),
    )(*_in_hbm([x, w] + extra))


def tapgrad_roll(name, dy, dycb, dy_blk_off, x, xcb, x_blk_off, *, n_tap, seq_len, n_seq, width, piece, cb, ncb, pad):
    n_piece = seq_len // piece

    def body(dy_ref, x_ref, o_ref):
        @pl.when(pl.program_id(1) == 0)
        def _():
            o_ref[...] = jnp.zeros_like(o_ref)

        tok = lax.broadcasted_iota(jnp.int32, (piece, 1), 0) % width

        def do_piece(p, carry):
            start = pl.multiple_of(p * piece, piece)
            xs = x_ref[pl.ds(start, piece), :]
            dv = dy_ref[pl.ds(start, piece), :]
            for k in range(n_tap):
                o_ref[k:k + 1, :] += jnp.sum(dv * _shifted(xs, k - pad, tok, width), axis=0, keepdims=True)
            return carry

        lax.fori_loop(0, n_piece, do_piece, 0)

    return pl.pallas_call(
        body, name=name, grid=(ncb, n_seq), out_shape=jax.ShapeDtypeStruct((n_tap, ncb * cb), F32),
        in_specs=[pl.BlockSpec((seq_len, cb), lambda j, s: (dy_blk_off + s, dycb + j)),
                  pl.BlockSpec((seq_len, cb), lambda j, s: (x_blk_off + s, xcb + j))],
        out_specs=pl.BlockSpec((n_tap, cb), lambda j, s: (0, j)),
        compiler_params=_params(8 * seq_len * cb * 4 + (8 << 20), 2),
    )(*_in_hbm([dy, x]))


def tapsum_rows(name, x, xcb, w, wcb, *, seq_len, n_seq, cb, ncb, pad, flip, place=None):
    n_tap = w.shape[0]
    n_row = seq_len // GRID_W
    halo = pad * GRID_W
    out_shape, o_rb, o_cb, into = _placed((n_seq * seq_len, ncb * cb), place)

    def body(x_ref, w_ref, *rest):
        o_ref, xp = rest[-2:]
        xp[pl.ds(0, halo), :] = jnp.zeros((halo, cb), F32)
        xp[pl.ds(halo + seq_len, halo), :] = jnp.zeros((halo, cb), F32)
        xp[pl.ds(halo, seq_len), :] = x_ref[...]
        wv = w_ref[...]

        def do_row(r, carry):
            acc = jnp.zeros((GRID_W, cb), F32)
            for k in range(n_tap):
                d = pad - k if flip else k - pad
                acc = acc + wv[k:k + 1, :] * xp[pl.ds(pl.multiple_of((r + pad + d) * GRID_W, GRID_W), GRID_W), :]
            o_ref[pl.ds(pl.multiple_of(r * GRID_W, GRID_W), GRID_W), :] = acc
            return carry

        lax.fori_loop(0, n_row, do_row, 0)

    extra = [] if into is None else [into]
    return pl.pallas_call(
        body, name=name, grid=(ncb, n_seq), out_shape=_big(out_shape, F32),
        in_specs=[pl.BlockSpec((seq_len, cb), lambda j, s: (s, xcb + j)),
                  pl.BlockSpec((n_tap, cb), lambda j, s: (0, wcb + j))] + [pl.BlockSpec(memory_space=pl.ANY)] * len(extra),
        out_specs=pl.BlockSpec((seq_len, cb), lambda j, s: (o_rb + s, o_cb + j)),
        input_output_aliases={2: 0} if extra else {},
        scratch_shapes=[pltpu.VMEM((seq_len + 2 * halo, cb), F32)],
        compiler_params=_params(10 * seq_len * cb * 4 + (8 << 20), 2),
    )(*_in_hbm([x, w] + extra))


def tapgrad_rows(name, dy, dycb, x, xcb, *, n_tap, seq_len, n_seq, cb, ncb, pad):
    n_row = seq_len // GRID_W
    halo = pad * GRID_W

    def body(dy_ref, x_ref, o_ref, xp):
        @pl.when(pl.program_id(1) == 0)
        def _():
            o_ref[...] = jnp.zeros_like(o_ref)

        xp[pl.ds(0, halo), :] = jnp.zeros((halo, cb), F32)
        xp[pl.ds(halo + seq_len, halo), :] = jnp.zeros((halo, cb), F32)
        xp[pl.ds(halo, seq_len), :] = x_ref[...]

        def do_row(r, carry):
            dv = dy_ref[pl.ds(pl.multiple_of(r * GRID_W, GRID_W), GRID_W), :]
            for k in range(n_tap):
                xs = xp[pl.ds(pl.multiple_of((r + k) * GRID_W, GRID_W), GRID_W), :]
                o_ref[k:k + 1, :] += jnp.sum(dv * xs, axis=0, keepdims=True)
            return carry

        lax.fori_loop(0, n_row, do_row, 0)

    return pl.pallas_call(
        body, name=name, grid=(ncb, n_seq), out_shape=jax.ShapeDtypeStruct((n_tap, ncb * cb), F32),
        in_specs=[pl.BlockSpec((seq_len, cb), lambda j, s: (s, dycb + j)),
                  pl.BlockSpec((seq_len, cb), lambda j, s: (s, xcb + j))],
        out_specs=pl.BlockSpec((n_tap, cb), lambda j, s: (0, j)),
        scratch_shapes=[pltpu.VMEM((seq_len + 2 * halo, cb), F32)],
        compiler_params=_params(10 * seq_len * cb * 4 + (8 << 20), 2),
    )(*_in_hbm([dy, x]))


def _ssd_blocks(b, s, *, rev, n_ctx, n_lat, lat_blocks):
    if rev:
        return jnp.where(s < n_ctx, lat_blocks + b * n_ctx + (n_ctx - 1 - s), b * n_lat + (n_lat - 1 - (s - n_ctx)))
    return jnp.where(s < n_ctx, lat_blocks + b * n_ctx + s, b * n_lat + (s - n_ctx))


def _ssd_common(xbc, raw, dtb, alog, dsk, *, rev, ds, n_head):
    if rev:
        raw = pltpu.roll(raw, LANES - n_head, axis=1)
    pre = raw + dtb
    dt = jnp.maximum(pre, 0.0) + jnp.log(1.0 + jnp.exp(-jnp.abs(pre)))
    sig = jax.nn.sigmoid(pre)
    a = -jnp.exp(alog)
    da = dt * a
    ri = lax.broadcasted_iota(jnp.int32, (CHUNK, CHUNK), 0)
    ci = lax.broadcasted_iota(jnp.int32, (CHUNK, CHUNK), 1)
    mask = (ci >= ri) if rev else (ci <= ri)
    tri = mask.astype(F32)
    tri_t = ((ci <= ri) if rev else (ci >= ri)).astype(F32)
    cs = jnp.dot(tri, da, precision=HI, preferred_element_type=F32)
    tot = jnp.sum(da, axis=0, keepdims=True)
    def wide(v):
        first = lax.broadcasted_iota(jnp.int32, (v.shape[0], LANES), 1) < HEAD_DIM
        return jnp.concatenate(
            [jnp.where(first, jnp.broadcast_to(v[:, 2 * p:2 * p + 1], first.shape),
                       jnp.broadcast_to(v[:, 2 * p + 1:2 * p + 2], first.shape)) for p in range(n_head // 2)], axis=1)

    cs_w, tot_w = wide(cs), wide(tot)
    xh = xbc[:, :ds]
    dt_w = wide(dt)
    return dict(
        dt=dt, sig=sig, a=a, cs=cs, cs_t=cs.T, tot=tot, mask=mask, tri_t=tri_t,
        e_w=jnp.exp(cs_w), wt_w=jnp.exp(tot_w - cs_w), dec_w=jnp.exp(tot_w), dt_w=dt_w, dsk_w=wide(dsk),
        xh=xh, xs_w=xh * dt_w, bm=xbc[:, ds:ds + 2 * N_STATE], cm=xbc[:, ds + 2 * N_STATE:ds + 4 * N_STATE])


def _decay(q, col):
    seg = q["cs"][:, col:col + 1] - q["cs_t"][col:col + 1, :]
    return jnp.exp(jnp.where(q["mask"], seg, -jnp.inf))


def _split_heads(v):
    lane = lax.broadcasted_iota(jnp.int32, v.shape, 1)
    return jnp.concatenate([jnp.where(lane < HEAD_DIM, v, 0.0), jnp.where(lane >= HEAD_DIM, v, 0.0)], axis=0)


def ssd_fwd(name, xbc, proj, dt_cb, dtb, alog, dsk, *, rev, n_ex, seq_len, ctx_len, ds, rider=None, add=None):
    n_head, half = ds // HEAD_DIM, ds // 2
    n_ctx, n_lat = ctx_len // CHUNK, seq_len // CHUNK
    n_step = n_ctx + n_lat
    blk = functools.partial(_ssd_blocks, rev=rev, n_ctx=n_ctx, n_lat=n_lat, lat_blocks=n_ex * n_lat)
    xw = xbc.shape[1]

    def y_blk(b, s):
        sl = jnp.maximum(s, n_ctx) - n_ctx
        return b * n_lat + ((n_lat - 1 - sl) if rev else sl)

    hosted = _Hosted(rider, 5 + (add is not None), 2, 1, (n_ex, n_step))

    def body(*refs):
        (xbc_ref, dt_ref, dtb_ref, alog_ref, dsk_ref, *add_ref), (y_ref, hs_ref), (h_scr,) = hosted.split(refs)

        @pl.when(pl.program_id(1) == 0)
        def _():
            h_scr[...] = jnp.zeros_like(h_scr)

        q = _ssd_common(xbc_ref[...], dt_ref[...], dtb_ref[...], alog_ref[...], dsk_ref[...], rev=rev, ds=ds, n_head=n_head)
        h = h_scr[...]
        hs_ref[...] = h
        for g in range(2):
            lo = g * half
            bg = q["bm"][:, g * N_STATE:(g + 1) * N_STATE].astype(BF16)
            cg = q["cm"][:, g * N_STATE:(g + 1) * N_STATE].astype(BF16)
            scores = lax.dot_general(cg, bg, _DIMS["nt"], preferred_element_type=F32)
            hg = h[:, lo:lo + half]
            off = jnp.dot(cg, hg.astype(BF16), preferred_element_type=F32)
            for j in range(half // LANES):
                c0 = (lo + j * LANES) // HEAD_DIM
                ln = slice(lo + j * LANES, lo + (j + 1) * LANES)
                p_cat = jnp.concatenate([scores * _decay(q, c0), scores * _decay(q, c0 + 1)], axis=1).astype(BF16)
                diag = jnp.dot(p_cat, _split_heads(q["xs_w"][:, ln]).astype(BF16), preferred_element_type=F32)
                y_ref[:, ln] = (diag + q["e_w"][:, ln] * off[:, j * LANES:(j + 1) * LANES]
                                + q["dsk_w"][:, ln] * q["xh"][:, ln] + (add_ref[0][:, ln] if add_ref else 0.0))
            v = (q["wt_w"][:, lo:lo + half] * q["xs_w"][:, lo:lo + half]).astype(BF16)
            h_scr[:, lo:lo + half] = (q["dec_w"][:, lo:lo + half] * hg
                                      + lax.dot_general(bg, v, _DIMS["tn"], preferred_element_type=F32))
        hosted.finish()

    vec = pl.BlockSpec((1, LANES), lambda b, s: (0, 0))
    in_specs, out_shape, out_specs, scratch, args = hosted.call_args(
        [pl.BlockSpec((CHUNK, xw), lambda b, s: (blk(b, s), 0)),
         pl.BlockSpec((CHUNK, LANES), lambda b, s: (blk(b, s), dt_cb)), vec, vec, vec]
        + [pl.BlockSpec((CHUNK, ds), lambda b, s: (y_blk(b, s), 0))] * (add is not None),
        (_big((n_ex * seq_len, ds), F32), _big((n_ex, n_step, N_STATE, ds), F32)),
        (pl.BlockSpec((CHUNK, ds), lambda b, s: (y_blk(b, s), 0)),
         pl.BlockSpec((None, None, N_STATE, ds), lambda b, s: (b, s, 0, 0))),
        [pltpu.VMEM((N_STATE, ds), F32)], [xbc, proj, dtb, alog, dsk] + ([] if add is None else [add]))
    return hosted.results(pl.pallas_call(
        body, name=name, grid=(n_ex, n_step), out_shape=out_shape, in_specs=in_specs, out_specs=out_specs,
        scratch_shapes=scratch, compiler_params=_params(40 << 20, 2),
    )(*_in_hbm(args)))


def ssd_bwd(name, xbc, proj, dt_cb, hs, dy, dtb, alog, dsk, *, rev, n_ex, seq_len, ctx_len, ds, rider=None, add=None):
    n_head, half = ds // HEAD_DIM, ds // 2
    n_ctx, n_lat = ctx_len // CHUNK, seq_len // CHUNK
    n_step = n_ctx + n_lat
    n_tok = n_ex * (seq_len + ctx_len)
    blk0 = functools.partial(_ssd_blocks, rev=rev, n_ctx=n_ctx, n_lat=n_lat, lat_blocks=n_ex * n_lat)
    step = lambda sp: n_step - 1 - sp
    blk = lambda b, sp: blk0(b, step(sp))
    xw = xbc.shape[1]

    def dy_blk(b, sp):
        sl = jnp.maximum(step(sp), n_ctx) - n_ctx
        return b * n_lat + ((n_lat - 1 - sl) if rev else sl)

    hosted = _Hosted(rider, 7 + (add is not None), 5, 1, (n_ex, n_step))

    def body(*refs):
        ((xbc_ref, dt_ref, hs_ref, dy_ref, dtb_ref, alog_ref, dsk_ref, *add_ref),
         (dxbc_ref, ddt_ref, dalog_ref, ddtb_ref, ddsk_ref), (dh_scr,)) = hosted.split(refs)
        b, sp = pl.program_id(0), pl.program_id(1)
        more = (lambda cols: add_ref[0][:, cols]) if add_ref else (lambda cols: 0.0)

        @pl.when(sp == 0)
        def _():
            dh_scr[...] = jnp.zeros_like(dh_scr)

        @pl.when((sp == 0) & (b == 0))
        def _():
            dalog_ref[...] = jnp.zeros_like(dalog_ref)
            ddtb_ref[...] = jnp.zeros_like(ddtb_ref)
            ddsk_ref[...] = jnp.zeros_like(ddsk_ref)

        q = _ssd_common(xbc_ref[...], dt_ref[...], dtb_ref[...], alog_ref[...], dsk_ref[...], rev=rev, ds=ds, n_head=n_head)
        h = hs_ref[...]
        d_y = jnp.where(step(sp) >= n_ctx, dy_ref[...], 0.0)
        dh_next = dh_scr[...]
        lane_row = lax.broadcasted_iota(jnp.int32, (1, LANES), 1)
        d_cs = jnp.zeros((CHUNK, LANES), F32)
        dxs_parts, de_parts, dwt_parts, ddec_parts = [], [], [], []
        for g in range(2):
            lo = g * half
            gs = slice(lo, lo + half)
            bg = q["bm"][:, g * N_STATE:(g + 1) * N_STATE].astype(BF16)
            cg = q["cm"][:, g * N_STATE:(g + 1) * N_STATE].astype(BF16)
            scores = lax.dot_general(cg, bg, _DIMS["nt"], preferred_element_type=F32)
            hg, dyg, dhn = h[:, gs], d_y[:, gs], dh_next[:, gs]
            off = jnp.dot(cg, hg.astype(BF16), preferred_element_type=F32)
            d_off = (q["e_w"][:, gs] * dyg).astype(BF16)
            de_parts.append(dyg * off)
            d_c = lax.dot_general(d_off, hg.astype(BF16), _DIMS["nt"], preferred_element_type=F32)
            dh_scr[:, gs] = (lax.dot_general(cg, d_off, _DIMS["tn"], preferred_element_type=F32)
                             + q["dec_w"][:, gs] * dhn)
            b_dh = jnp.dot(bg, dhn.astype(BF16), preferred_element_type=F32)
            v = q["wt_w"][:, gs] * q["xs_w"][:, gs]
            d_b = lax.dot_general(v.astype(BF16), dhn.astype(BF16), _DIMS["nt"], preferred_element_type=F32)
            dwt_parts.append(q["xs_w"][:, gs] * b_dh)
            ddec_parts.append(jnp.sum(hg * dhn, axis=0, keepdims=True))
            d_scores = jnp.zeros((CHUNK, CHUNK), F32)
            for j in range(half // LANES):
                c0 = (lo + j * LANES) // HEAD_DIM
                ln = slice(lo + j * LANES, lo + (j + 1) * LANES)
                l0, l1 = _decay(q, c0), _decay(q, c0 + 1)
                p0, p1 = scores * l0, scores * l1
                dy_st = _split_heads(d_y[:, ln]).astype(BF16)
                d_p = lax.dot_general(dy_st, q["xs_w"][:, ln].astype(BF16), _DIMS["nt"], preferred_element_type=F32)
                d_p0, d_p1 = d_p[:CHUNK], d_p[CHUNK:]
                d_scores = d_scores + d_p0 * l0 + d_p1 * l1
                for col, t in ((c0, d_p0 * p0), (c0 + 1, d_p1 * p1)):
                    d_cs = d_cs + jnp.sum(t - t.T, axis=1, keepdims=True) * (lane_row == col).astype(F32)
                p_st = jnp.concatenate([p0, p1], axis=0).astype(BF16)
                dxs_parts.append(lax.dot_general(p_st, dy_st, _DIMS["tn"], preferred_element_type=F32)
                                 + q["wt_w"][:, ln] * b_dh[:, j * LANES:(j + 1) * LANES])
            d_sc = d_scores.astype(BF16)
            d_c = d_c + jnp.dot(d_sc, bg, preferred_element_type=F32)
            d_b = d_b + lax.dot_general(d_sc, cg, _DIMS["tn"], preferred_element_type=F32)
            b_cols, c_cols = slice(ds + g * N_STATE, ds + (g + 1) * N_STATE), slice(ds + (2 + g) * N_STATE, ds + (3 + g) * N_STATE)
            dxbc_ref[:, b_cols] = d_b + more(b_cols)
            dxbc_ref[:, c_cols] = d_c + more(c_cols)
        d_xs = jnp.concatenate(dxs_parts, axis=1)
        narrow_m = (lax.broadcasted_iota(jnp.int32, (ds, LANES), 0) // HEAD_DIM
                    == lax.broadcasted_iota(jnp.int32, (ds, LANES), 1)).astype(BF16)
        rows8 = lambda v: jnp.broadcast_to(v, (8, ds))
        stacked = jnp.concatenate(
            [jnp.concatenate(dwt_parts, axis=1), jnp.concatenate(de_parts, axis=1), d_xs * q["xh"],
             rows8(jnp.concatenate(ddec_parts, axis=1)), rows8(jnp.sum(d_y * q["xh"], axis=0, keepdims=True))], axis=0)
        sums = jnp.dot(stacked.astype(BF16), narrow_m, preferred_element_type=F32)
        n_wt, n_e, n_xs = sums[:CHUNK], sums[CHUNK:2 * CHUNK], sums[2 * CHUNK:3 * CHUNK]
        n_dec, n_dsk = sums[3 * CHUNK:3 * CHUNK + 1], sums[3 * CHUNK + 8:3 * CHUNK + 9]
        e, wt, dec = jnp.exp(q["cs"]), jnp.exp(q["tot"] - q["cs"]), jnp.exp(q["tot"])
        d_wt = n_wt * wt
        d_cs = d_cs + n_e * e - d_wt
        d_tot = jnp.sum(d_wt, axis=0, keepdims=True) + n_dec * dec
        d_da = jnp.dot(q["tri_t"], d_cs, precision=HI, preferred_element_type=F32) + d_tot
        d_dt = d_da * q["a"] + n_xs
        dxbc_ref[:, :ds] = d_xs * q["dt_w"] + q["dsk_w"] * d_y + more(slice(0, ds))
        dalog_ref[...] += jnp.sum(d_da * q["dt"], axis=0, keepdims=True) * q["a"]
        d_raw = d_dt * q["sig"]
        ddtb_ref[...] += jnp.sum(d_raw, axis=0, keepdims=True)
        ddsk_ref[...] += n_dsk
        ddt_ref[...] = pltpu.roll(d_raw, n_head, axis=1) if rev else d_raw
        hosted.finish()

    vec = pl.BlockSpec((1, LANES), lambda b, s: (0, 0))
    vec_shape = jax.ShapeDtypeStruct((1, LANES), F32)
    in_specs, out_shape, out_specs, scratch, args = hosted.call_args(
        [pl.BlockSpec((CHUNK, xw), lambda b, s: (blk(b, s), 0)),
         pl.BlockSpec((CHUNK, LANES), lambda b, s: (blk(b, s), dt_cb)),
         pl.BlockSpec((None, None, N_STATE, ds), lambda b, s: (b, step(s), 0, 0)),
         pl.BlockSpec((CHUNK, ds), lambda b, s: (dy_blk(b, s), 0)), vec, vec, vec]
        + [pl.BlockSpec((CHUNK, xw), lambda b, s: (blk(b, s), 0))] * (add is not None),
        (_big((n_tok, xw), F32), _big((n_tok, LANES), F32), vec_shape, vec_shape, vec_shape),
        (pl.BlockSpec((CHUNK, xw), lambda b, s: (blk(b, s), 0)),
         pl.BlockSpec((CHUNK, LANES), lambda b, s: (blk(b, s), 0)), vec, vec, vec),
        [pltpu.VMEM((N_STATE, ds), F32)], [xbc, proj, hs, dy, dtb, alog, dsk] + ([] if add is None else [add]))
    return hosted.results(pl.pallas_call(
        body, name=name, grid=(n_ex, n_step), out_shape=out_shape, in_specs=in_specs, out_specs=out_specs,
        scratch_shapes=scratch, compiler_params=_params(48 << 20, 2),
    )(*_in_hbm(args)))


def final_loss(x3, target, w, *, tm):
    n, d = x3.shape

    def body(x_ref, t_ref, w_ref, dx_ref, dw_ref, loss_ref):
        i = pl.program_id(0)
        t = t_ref[...]

        def per_feature(xv, wv):
            err = _rms(xv, wv) - t
            return 0.5 * jnp.sum(err * err, axis=0, keepdims=True) / d

        lv, vjp = jax.vjp(per_feature, x_ref[...], w_ref[...])
        dx, dw = vjp(jnp.ones_like(lv))
        dx_ref[...] = dx

        @pl.when(i == 0)
        def _():
            dw_ref[...] = dw
            loss_ref[...] = lv

        @pl.when(i > 0)
        def _():
            dw_ref[...] += dw
            loss_ref[...] += lv

    tile = pl.BlockSpec((tm, d), lambda i: (i, 0))
    vec = pl.BlockSpec((1, d), lambda i: (0, 0))
    return pl.pallas_call(
        body, name="final_loss", grid=(n // tm,), in_specs=[tile, tile, vec],
        out_shape=(jax.ShapeDtypeStruct((n, d), F32), jax.ShapeDtypeStruct((1, d), F32), jax.ShapeDtypeStruct((1, d), F32)),
        out_specs=(tile, vec, vec), compiler_params=_params(tm * d * 4 * 16 + (8 << 20)),
    )(x3, target, w)


def sum_slots(name, arr, out_dtype=F32):
    n_slot, n_row, width = arr.shape
    tm = _row_tile(n_row, width * n_slot, mult=16)

    def body(a_ref, o_ref):
        acc = a_ref[0].astype(F32)
        for j in range(1, n_slot):
            acc = acc + a_ref[j].astype(F32)
        o_ref[...] = acc.astype(o_ref.dtype)

    return pl.pallas_call(
        body, name=name, grid=(n_row // tm,), out_shape=jax.ShapeDtypeStruct((n_row, width), out_dtype),
        in_specs=[pl.BlockSpec((n_slot, tm, width), lambda i: (0, i, 0))],
        out_specs=pl.BlockSpec((tm, width), lambda i: (i, 0)), compiler_params=_params(),
    )(arr)


def adamw(name, w, g_slots, m, v):
    n_slot, n_row, width = g_slots.shape
    tm = _row_tile(n_row, width * 2)
    if g_slots.dtype == BF16 and tm % 16:
        tm16 = _row_tile(n_row, width * 2, mult=16)
        if tm16 % 16 == 0:
            tm = tm16
        else:
            g_slots = g_slots.astype(F32)

    def body(w_ref, g_ref, m_ref, v_ref, go_ref, d_ref, mo_ref, vo_ref):
        g = g_ref[0].astype(F32)
        for j in range(1, n_slot):
            g = g + g_ref[j].astype(F32)
        m2 = ADAM_B1 * m_ref[...] + (1.0 - ADAM_B1) * g
        v2 = ADAM_B2 * v_ref[...] + (1.0 - ADAM_B2) * jnp.square(g)
        m_hat = m2 / (1.0 - ADAM_B1 ** ADAM_STEP)
        v_hat = v2 / (1.0 - ADAM_B2 ** ADAM_STEP)
        go_ref[...] = g
        d_ref[...] = -ADAM_LR * (m_hat / (jnp.sqrt(v_hat) + ADAM_EPS) + ADAM_WD * w_ref[...])
        mo_ref[...] = m2
        vo_ref[...] = v2

    tile = pl.BlockSpec((tm, width), lambda i: (i, 0))
    shape = jax.ShapeDtypeStruct((n_row, width), F32)
    return pl.pallas_call(
        body, name=name, grid=(n_row // tm,), out_shape=(shape,) * 4,
        in_specs=[tile, pl.BlockSpec((n_slot, tm, width), lambda i: (0, i, 0)), tile, tile],
        out_specs=(tile,) * 4, compiler_params=_params(),
    )(w, g_slots, m, v)


def cctx_grad(q_all, c_ctx_row):
    d = c_ctx_row.shape[1]

    def body(q_ref, c_ref, o_ref):
        acc = q_ref[0, 0:1, :]
        for j in (2, 4, 6):
            acc = acc + q_ref[j, 0:1, :]
        _, vjp = jax.vjp(_silu, c_ref[...])
        o_ref[...] = vjp(acc)[0]

    return pl.pallas_call(
        body, name="cctx_grad", out_shape=jax.ShapeDtypeStruct((1, d), F32),
    )(q_all, c_ctx_row)


def loss_total(pack_sum, d):
    def body(p_ref, o_ref):
        o_ref[...] = jnp.sum(p_ref[:, 0:d], axis=1, keepdims=True)

    return pl.pallas_call(
        body, name="loss_total", out_shape=jax.ShapeDtypeStruct((1, 1), F32),
    )(pack_sum)


class _Plan:
    def __init__(self):
        self.builders, self.got = {}, {}

    def on(self, host, key, builder):
        self.builders.setdefault(host, []).append((key, builder))

    def run(self, host, fn, *args, **kw):
        if host not in self.builders:
            return fn(host, *args, **kw)
        keys, riders = zip(*[(key, builder(self)) for key, builder in self.builders[host]])
        res, landed = fn(host, *args, rider=Riders(riders), **kw)
        for key, r in zip(keys, riders):
            self.got[key], landed = landed[:r.n], landed[r.n:]
        return res


def _val(w):
    return w() if callable(w) else w


def _matmul_tile(n_rows, tm):
    return 2 * tm if n_rows % (2 * tm) == 0 else tm


def _ffn_fwd(plan, tag, xin, n_rows, tm, seg_fn, shift, scale, gate, norm_w, wg, wu, wd, fuse_gate_up=False):
    d = xin[1]
    n_tiles = n_rows // tm
    (h,) = plan.run(f"{tag}_norm", rowwise, fn_norm_mod, [xin], [shift, scale], [norm_w], [(n_rows, d, BF16)],
                    tm=tm, n_tiles=n_tiles, seg_fn=seg_fn)
    tmm = _matmul_tile(n_rows, tm)
    if fuse_gate_up:
        g, u, act = plan.run(f"{tag}_gate_up", matmul, [(h, _val(wg)), (h, _val(wu))], "nn", b_ch=True, out_ch=True,
                             tm=min(tm, 256), fold=True,
                             post=([], lambda ag, au: (ag, au, fn_act(ag, au)[0]), [BF16, BF16, BF16]))
    else:
        g = plan.run(f"{tag}_gate", matmul, [(h, _val(wg))], "nn", out_dtype=BF16, b_ch=True, out_ch=True, tm=tmm)
        u, act = plan.run(f"{tag}_up", matmul, [(h, _val(wu))], "nn", b_ch=True, out_ch=True, tm=tm, fold=True,
                          post=([g], lambda acc, gv: (acc, fn_act(gv, acc)[0]), [BF16, BF16]))
    f = plan.run(f"{tag}_down", matmul, [(act, _val(wd))], "nn", a_ch=True, b_ch=True, tm=tmm, fold=True)
    (xo,) = plan.run(f"{tag}_resid", rowwise, make_fn_resid(0.5), [xin, row(f)], [gate], [], [(n_rows, d, F32)],
                     tm=tm, n_tiles=n_tiles, seg_fn=seg_fn)
    return xo, (h, g, u, act, f)


def _ffn_bwd(plan, tag, d_xo, saved, xin, n_rows, tm, seg_fn, first_fn, shift, scale, gate, norm_w, wg, wu, wd, dx_rows, dx_limit):
    h, g, u, act, f = saved
    d = xin[1]
    n_tiles = n_rows // tm
    n_ch, _, n_hid = g.shape
    d_f, d_gate = plan.run(f"{tag}_resid_bwd", rowwise_bwd, make_fn_resid(0.5), [xin, row(f)], [gate], [], [[row(d_xo)]],
                           [None, (n_rows, BF16, None)], tm=tm, n_tiles=n_tiles, seg_fn=seg_fn, first_fn=first_fn)
    tmm = _matmul_tile(n_rows, tm)
    def act_vjp(d_act, gv, uv):
        s = jax.nn.sigmoid(gv)
        gs = gv * s
        return d_act * uv * (s + gs * (1.0 - s)), d_act * gs
    d_g, d_u = plan.run(f"{tag}_down_dx", matmul, [(d_f, wd)], "nt", b_ch=True, out_ch=True, tm=tmm,
                        post=([g, u], act_vjp, [BF16, BF16]))
    plan.got[f"{tag}_d_wd"] = plan.run(f"{tag}_down_dw", matmul, [(act, d_f)], "tn", out_dtype=BF16, a_ch=True, out_ch=True, tm=tmm)
    d_h = plan.run(f"{tag}_up_dx", matmul, [(d_g, wg), (d_u, wu)], "nt", a_ch=True, b_ch=True, tm=tmm)
    plan.got[f"{tag}_d_wg"] = plan.run(f"{tag}_gate_dw", matmul, [(d_g, h)], "tn", out_dtype=BF16, a_ch=True, out_ch=True, tm=tmm)
    plan.got[f"{tag}_d_wu"] = plan.run(f"{tag}_up_dw", matmul, [(d_u, h)], "tn", out_dtype=BF16, a_ch=True, out_ch=True, tm=tmm)
    d_x, d_shift, d_scale, d_nw = plan.run(
        f"{tag}_norm_bwd", rowwise_bwd, fn_norm_mod, [xin], [shift, scale], [norm_w], [[row(d_h)]], [(dx_rows, F32, dx_limit)],
        tm=tm, n_tiles=n_tiles, seg_fn=seg_fn, first_fn=first_fn, adds={0: (row(d_xo), None)})
    return d_x, (d_shift, d_scale, d_gate), d_nw


def kernel(x, c, ctx, c_ctx, w_mod, b_mod, norm_ffn1, ffn1_gate, ffn1_up, ffn1_down, norm_mix, w_in, ssm_conv_w, ssm_conv_b, dt_bias_fwd, dt_bias_bwd, a_log_fwd, a_log_bwd, ssm_d, ssm_norm_w, cconv_w, cconv_b, cconv_ln_w, cconv_ln_b, w_out, norm_ffn2, ffn2_gate, ffn2_up, ffn2_down, final_norm, loss_target, m_c_ctx, m_w_mod, m_b_mod, m_norm_ffn1, m_ffn1_gate, m_ffn1_up, m_ffn1_down, m_norm_mix, m_w_in, m_ssm_conv_w, m_ssm_conv_b, m_dt_bias_fwd, m_dt_bias_bwd, m_a_log_fwd, m_a_log_bwd, m_ssm_d, m_ssm_norm_w, m_cconv_w, m_cconv_b, m_cconv_ln_w, m_cconv_ln_b, m_w_out, m_norm_ffn2, m_ffn2_gate, m_ffn2_up, m_ffn2_down, m_final_norm, v_c_ctx, v_w_mod, v_b_mod, v_norm_ffn1, v_ffn1_gate, v_ffn1_up, v_ffn1_down, v_norm_mix, v_w_in, v_ssm_conv_w, v_ssm_conv_b, v_dt_bias_fwd, v_dt_bias_bwd, v_a_log_fwd, v_a_log_bwd, v_ssm_d, v_ssm_norm_w, v_cconv_w, v_cconv_b, v_cconv_ln_w, v_cconv_ln_b, v_w_out, v_norm_ffn2, v_ffn2_gate, v_ffn2_up, v_ffn2_down, v_final_norm):
    weights = dict(c_ctx=c_ctx, w_mod=w_mod, b_mod=b_mod, norm_ffn1=norm_ffn1, ffn1_gate=ffn1_gate, ffn1_up=ffn1_up, ffn1_down=ffn1_down, norm_mix=norm_mix, w_in=w_in, ssm_conv_w=ssm_conv_w, ssm_conv_b=ssm_conv_b, dt_bias_fwd=dt_bias_fwd, dt_bias_bwd=dt_bias_bwd, a_log_fwd=a_log_fwd, a_log_bwd=a_log_bwd, ssm_d=ssm_d, ssm_norm_w=ssm_norm_w, cconv_w=cconv_w, cconv_b=cconv_b, cconv_ln_w=cconv_ln_w, cconv_ln_b=cconv_ln_b, w_out=w_out, norm_ffn2=norm_ffn2, ffn2_gate=ffn2_gate, ffn2_up=ffn2_up, ffn2_down=ffn2_down, final_norm=final_norm)
    mom1 = dict(c_ctx=m_c_ctx, w_mod=m_w_mod, b_mod=m_b_mod, norm_ffn1=m_norm_ffn1, ffn1_gate=m_ffn1_gate, ffn1_up=m_ffn1_up, ffn1_down=m_ffn1_down, norm_mix=m_norm_mix, w_in=m_w_in, ssm_conv_w=m_ssm_conv_w, ssm_conv_b=m_ssm_conv_b, dt_bias_fwd=m_dt_bias_fwd, dt_bias_bwd=m_dt_bias_bwd, a_log_fwd=m_a_log_fwd, a_log_bwd=m_a_log_bwd, ssm_d=m_ssm_d, ssm_norm_w=m_ssm_norm_w, cconv_w=m_cconv_w, cconv_b=m_cconv_b, cconv_ln_w=m_cconv_ln_w, cconv_ln_b=m_cconv_ln_b, w_out=m_w_out, norm_ffn2=m_norm_ffn2, ffn2_gate=m_ffn2_gate, ffn2_up=m_ffn2_up, ffn2_down=m_ffn2_down, final_norm=m_final_norm)
    mom2 = dict(c_ctx=v_c_ctx, w_mod=v_w_mod, b_mod=v_b_mod, norm_ffn1=v_norm_ffn1, ffn1_gate=v_ffn1_gate, ffn1_up=v_ffn1_up, ffn1_down=v_ffn1_down, norm_mix=v_norm_mix, w_in=v_w_in, ssm_conv_w=v_ssm_conv_w, ssm_conv_b=v_ssm_conv_b, dt_bias_fwd=v_dt_bias_fwd, dt_bias_bwd=v_dt_bias_bwd, a_log_fwd=v_a_log_fwd, a_log_bwd=v_a_log_bwd, ssm_d=v_ssm_d, ssm_norm_w=v_ssm_norm_w, cconv_w=v_cconv_w, cconv_b=v_cconv_b, cconv_ln_w=v_cconv_ln_w, cconv_ln_b=v_cconv_ln_b, w_out=v_w_out, norm_ffn2=v_norm_ffn2, ffn2_gate=v_ffn2_gate, ffn2_up=v_ffn2_up, ffn2_down=v_ffn2_down, final_norm=v_final_norm)
    order = list(weights)

    n_ex, seq_len, d = x.shape
    ctx_len = ctx.shape[1]
    ds = d
    n_head = ds // HEAD_DIM
    xw = ds + 4 * N_STATE
    n_lat, n_ctx_rows = n_ex * seq_len, n_ex * ctx_len
    n_tok = n_lat + n_ctx_rows
    tm = math.gcd(math.gcd(512, seq_len), n_ctx_rows)
    seg_all, first_all = _segmenter(tm, seq_len, n_lat)
    lat_tiles = n_lat // tm

    xi, yi, ci = lax.axis_index("x"), lax.axis_index("y"), lax.axis_index("c")
    me, chip = 4 * xi + 2 * yi + ci, 2 * xi + yi

    (c_all,) = exchange("gather_c", [c], "all8")
    n_all = 8 * n_ex
    n_cond = -(-(n_all + 1) // 8) * 8
    cond = jnp.concatenate([c_all.reshape(n_all, d), c_ctx[None, :], jnp.zeros((n_cond - n_all - 1, d), F32)])
    mod_w = w_mod.shape[2]
    b_shard = lax.dynamic_slice(b_mod, (0, chip * mod_w), (1, mod_w))
    (mod_g,) = exchange("gather_mod", [mod_fwd(cond, w_mod[0], b_shard)], "chips")
    mod_full = mod_g.transpose(1, 0, 2).reshape(n_cond, N_CHIPS * mod_w)
    mod_mine = lax.dynamic_slice(mod_full, (me * n_ex, 0), (n_ex, 9 * d)).reshape(n_ex, 9, d)
    mod_ctx = mod_full[n_all].reshape(9, d)
    tabs = [jnp.concatenate([mod_mine[:, j], mod_ctx[j][None]])[:, None, :] for j in range(9)]
    lat = lambda t: t[:n_ex]

    bf = lambda w: w[0].astype(BF16)
    plan = _Plan()
    gather = lambda *ws: (lambda p: Rider(list(ws), "chips"))
    plan.on("ffn1_norm", "wg1", gather(bf(ffn1_gate)))
    plan.on("ffn1_gate", "wu1", gather(bf(ffn1_up)))
    plan.on("ffn1_up", "wd1", gather(bf(ffn1_down)))
    win_cut = d * 5 // 8
    plan.on("ffn1_down", "win_a", gather(bf(w_in)[:win_cut]))
    plan.on("ffn1_resid", "win_b", gather(bf(w_in)[win_cut:], ssm_conv_w[0], cconv_w[0]))
    xt = two_rows(x.reshape(n_lat, d), ctx.reshape(n_ctx_rows, d), lat_tiles)
    x1, saved1 = _ffn_fwd(plan, "ffn1", xt, n_tok, tm, seg_all, tabs[0], tabs[1], tabs[2], norm_ffn1,
                          lambda: plan.got["wg1"][0], lambda: plan.got["wu1"][0], lambda: plan.got["wd1"][0])
    (wg1,), (wu1,), (wd1,), (win_a,), (win_b, w5_g, w31_g) = (plan.got[k] for k in ("wg1", "wu1", "wd1", "win_a", "win_b"))
    win_g = jnp.concatenate([win_a, win_b], axis=1)
    unshard_cols = lambda t: t.transpose(1, 0, 2).reshape(t.shape[1], N_CHIPS * t.shape[2])
    win = unshard_cols(win_g)
    o_x, o_dt, o_glu = ds, ds + xw, ds + xw + 2 * n_head
    w_z, w_xbc, w_dt = win[:, :ds], win[:, o_x:o_dt], win[:, o_dt:o_glu]
    w_ga, w_gb = win[:, o_glu:o_glu + d], win[:, o_glu + d:]
    w_dtp = jnp.concatenate([w_dt, jnp.zeros((d, LANES - 2 * n_head), BF16)], axis=1)
    w_cat = jnp.concatenate([w_z, w_ga, w_gb, w_xbc, w_dtp], axis=1)
    cbw = d // 2
    xbc_cb, dt_cb = 3 * d // cbw, (3 * d + xw) // LANES
    w5, w31 = unshard_cols(w5_g), unshard_cols(w31_g)
    pad_vec = lambda v: jnp.concatenate([v.reshape(1, -1), jnp.zeros((1, LANES - v.size), F32)], axis=1)
    dtb_f, dtb_b, alog_f, alog_b = map(pad_vec, (dt_bias_fwd, dt_bias_bwd, a_log_fwd, a_log_bwd))
    dsk_f, dsk_b = pad_vec(ssm_d), jnp.zeros((1, LANES), F32)

    (h2,) = rowwise("mix_norm", fn_norm_mod, [row(x1)], [tabs[3], tabs[4]], [norm_mix], [(n_tok, d, BF16)],
                    tm=tm, n_tiles=n_tok // tm, seg_fn=seg_all)
    proj, (wg2,) = matmul("mix_proj", [(h2, w_cat)], "nn", tm=min(tm, 256), rider=Rider([bf(ffn2_gate)], "chips"))
    def conv5(name, src, cb0, flip):
        out = None
        for part, seq, off in (("lat", seq_len, 0), ("ctx", ctx_len, n_lat // ctx_len)):
            out = tapsum_roll(f"{name}_{part}", src, cb0, w5, 0, seq_len=seq, n_seq=n_ex, row_blk_off=off, width=seq,
                              piece=seq, cb=cbw, ncb=xw // cbw, pad=w5.shape[0] // 2, flip=flip,
                              place=((n_tok, xw), off, 0, out))
        return out

    craw = conv5("xbc_conv", proj, xbc_cb, False)
    (xbc,) = rowwise("xbc_silu", fn_silu_bias, [row(craw)], [], [ssm_conv_b], [(n_tok, xw, F32)], tm=tm, n_tiles=n_tok // tm)
    ssd = dict(n_ex=n_ex, seq_len=seq_len, ctx_len=ctx_len, ds=ds)
    (y_f, hs_f), (wu2, wd2) = ssd_fwd("ssd_fwd_f", xbc, proj, dt_cb, dtb_f, alog_f, dsk_f, rev=False,
                                      rider=Rider([bf(ffn2_up), bf(ffn2_down)], "chips"), **ssd)
    (y_b, hs_b), (wout_g,) = ssd_fwd("ssd_fwd_b", xbc, proj, dt_cb, dtb_b, alog_b, dsk_b, rev=True,
                                     rider=Rider([bf(w_out)], "chips"), add=y_f, **ssd)
    wout = wout_g.reshape(2 * d, d)
    wo_y, wo_u = wout[:ds], wout[ds:]
    fn_gate = make_fn_gate_groupnorm(ds)
    (yn,) = rowwise("ssd_gate", fn_gate, [row(y_b), row(proj, d, 0)], [], [ssm_norm_w], [(n_lat, ds, BF16)],
                    tm=tm, n_tiles=lat_tiles)
    (u0,) = rowwise("glu", fn_glu, [row(proj, d, 1), row(proj, d, 2)], [], [], [(n_lat, d, F32)], tm=tm, n_tiles=lat_tiles)
    cb31 = max(LANES, d // 4)
    ncb31 = (d // 2) // cb31
    pad31 = w31.shape[0] // 2
    piece31 = min(seq_len, 4 * GRID_W)
    v_w = tapsum_roll("cconv_cols", u0, 0, w31, 0, seq_len=seq_len, n_seq=n_ex, row_blk_off=0, width=GRID_W,
                      piece=piece31, cb=cb31, ncb=ncb31, pad=pad31, flip=False)
    v_h = tapsum_rows("cconv_rows", u0, ncb31, w31, ncb31, seq_len=seq_len, n_seq=n_ex, cb=cb31, ncb=ncb31, pad=pad31, flip=False)
    (un,) = rowwise("cconv_ln", fn_ln_silu, [row(v_w), row(v_h)], [], [cconv_b, cconv_ln_w, cconv_ln_b], [(n_lat, d, BF16)],
                    tm=tm, n_tiles=lat_tiles)
    mix = matmul("mix_out", [(yn, wo_y), (un, wo_u)], "nn", tm=tm)
    seg_lat, first_lat = _segmenter(tm, seq_len, n_lat)
    (x2,) = rowwise("mix_resid", make_fn_resid(1.0), [row(x1), row(mix)], [lat(tabs[5])], [], [(n_lat, d, F32)],
                    tm=tm, n_tiles=lat_tiles, seg_fn=seg_lat)
    x3, saved2 = _ffn_fwd(plan, "ffn2", row(x2), n_lat, tm, seg_lat, lat(tabs[6]), lat(tabs[7]), lat(tabs[8]), norm_ffn2, wg2, wu2, wd2,
                          fuse_gate_up=True)
    d_x3, d_final, loss_vec = final_loss(x3, loss_target.reshape(n_lat, d), final_norm.reshape(1, d), tm=tm)

    shard_cols = lambda t: t.reshape(t.shape[0], N_CHIPS, -1).transpose(1, 0, 2)

    def pieces(t):
        t = jnp.pad(t, ((0, 0), (0, -t.shape[1] % 32), (0, 0)))
        return t.reshape(2 * N_CHIPS, t.shape[1] // 2, t.shape[2]).astype(BF16)

    scatter = lambda *ts: Rider([pieces(t) for t in ts], "all8", scatter=True)
    halves = lambda names, landed: Rider([sum_slots(f"sum_{nm}", r, BF16) for nm, r in zip(names, landed)], "sibling")
    swapped = {}
    plan.on("ffn2_up_dx", "sc_ffn2_down", lambda p: scatter(p.got["ffn2_d_wd"]))
    plan.on("ffn2_up_dw", "sc_ffn2_gate", lambda p: scatter(p.got["ffn2_d_wg"]))
    d_x2, (d_s6, d_s7, d_g8), d_nffn2 = _ffn_bwd(
        plan, "ffn2", d_x3, saved2, row(x2), n_lat, tm, seg_lat, first_lat, lat(tabs[6]), lat(tabs[7]), lat(tabs[8]), norm_ffn2,
        wg2, wu2, wd2, n_lat, None)
    d_mix, d_g5 = rowwise_bwd("mix_resid_bwd", make_fn_resid(1.0), [row(x1), row(mix)], [lat(tabs[5])], [], [[row(d_x2)]],
                              [None, (n_lat, BF16, None)], tm=tm, n_tiles=lat_tiles, seg_fn=seg_lat, first_fn=first_lat)
    d_yn = matmul("mix_out_dy", [(d_mix, wo_y)], "nt", tm=tm)
    d_un = matmul("mix_out_du", [(d_mix, wo_u)], "nt", tm=tm)
    d_wout = jnp.concatenate([matmul("mix_out_dwy", [(yn, d_mix)], "tn", out_dtype=BF16, tm=tm),
                              matmul("mix_out_dwu", [(un, d_mix)], "tn", out_dtype=BF16, tm=tm)])
    d_vw, d_vh, d_cb, d_lnw, d_lnb = rowwise_bwd(
        "cconv_ln_bwd", fn_ln_silu, [row(v_w), row(v_h)], [], [cconv_b, cconv_ln_w, cconv_ln_b], [[row(d_un)]],
        [(n_lat, F32, None)] * 2, tm=tm, n_tiles=lat_tiles)
    d_u0 = tapsum_roll("cconv_cols_dx", d_vw, 0, w31, 0, seq_len=seq_len, n_seq=n_ex, row_blk_off=0, width=GRID_W,
                       piece=piece31, cb=cb31, ncb=ncb31, pad=pad31, flip=True, place=((n_lat, d), 0, 0, None))
    d_u0 = tapsum_rows("cconv_rows_dx", d_vh, 0, w31, ncb31, seq_len=seq_len, n_seq=n_ex, cb=cb31, ncb=ncb31, pad=pad31,
                       flip=True, place=((n_lat, d), 0, ncb31, d_u0))
    d_w31 = jnp.concatenate([
        tapgrad_roll("cconv_cols_dw", d_vw, 0, 0, u0, 0, 0, n_tap=w31.shape[0], seq_len=seq_len, n_seq=n_ex, width=GRID_W,
                     piece=piece31, cb=cb31, ncb=ncb31, pad=pad31),
        tapgrad_rows("cconv_rows_dw", d_vh, 0, u0, ncb31, n_tap=w31.shape[0], seq_len=seq_len, n_seq=n_ex, cb=cb31,
                     ncb=ncb31, pad=pad31)], axis=1)
    d_ga, d_gb = rowwise_bwd("glu_bwd", fn_glu, [row(proj, d, 1), row(proj, d, 2)], [], [], [[row(d_u0)]],
                             [(n_lat, BF16, None)] * 2, tm=tm, n_tiles=lat_tiles)
    d_ysum, d_z, d_ssmnw = rowwise_bwd(
        "ssd_gate_bwd", fn_gate, [row(y_b), row(proj, d, 0)], [], [ssm_norm_w], [[row(d_yn)]],
        [(n_lat, F32, None), (n_lat, BF16, None)], tm=tm, n_tiles=lat_tiles)
    (dxbc_f, ddt_f, dalog_f, ddtb_f, ddsk), landed = ssd_bwd(
        "ssd_bwd_f", xbc, proj, dt_cb, hs_f, d_ysum, dtb_f, alog_f, dsk_f, rev=False,
        rider=scatter(plan.got["ffn2_d_wu"], d_wout.reshape(N_CHIPS, -1, d)), **ssd)
    (dxbc_b, ddt_b, dalog_b, ddtb_b, _), both = ssd_bwd(
        "ssd_bwd_b", xbc, proj, dt_cb, hs_b, d_ysum, dtb_b, alog_b, dsk_b, rev=True,
        rider=halves(["ffn2_down", "ffn2_gate"], plan.got["sc_ffn2_down"] + plan.got["sc_ffn2_gate"]), add=dxbc_f, **ssd)
    swapped.update(zip(["ffn2_down", "ffn2_gate"], both))
    (d_craw, d_conv_b), both = rowwise_bwd(
        "xbc_silu_bwd", fn_silu_bias, [row(craw)], [], [ssm_conv_b], [[row(dxbc_b)]],
        [(n_tok, F32, None)], tm=tm, n_tiles=n_tok // tm, rider=halves(["ffn2_up", "w_out"], landed))
    swapped.update(zip(["ffn2_up", "w_out"], both))
    d_pxbc = conv5("xbc_conv_dx", d_craw, 0, True)
    g5 = lambda name, seq, off: tapgrad_roll(name, d_craw, 0, off, proj, xbc_cb, off, n_tap=w5.shape[0], seq_len=seq,
                                             n_seq=n_ex, width=seq, piece=seq, cb=cbw, ncb=xw // cbw, pad=w5.shape[0] // 2)
    d_w5 = g5("xbc_conv_lat_dw", seq_len, 0) + g5("xbc_conv_ctx_dw", ctx_len, n_lat // ctx_len)
    lat_pairs = [(d_z, w_z), (d_ga, w_ga), (d_gb, w_gb), (d_pxbc, w_xbc), (ddt_f, w_dtp), (ddt_b, w_dtp)]
    d_h2 = matmul("mix_proj_dx_lat", lat_pairs, "nt", rows=n_lat, tm=min(tm, 256), place=(n_tok, 0, None))
    d_h2 = matmul("mix_proj_dx_ctx", lat_pairs[3:], "nt", rows=n_ctx_rows, row_off=n_lat, tm=min(tm, 256),
                  place=(n_tok, n_lat, d_h2))
    d_wz = matmul("mix_proj_dwz", [(d_z, h2)], "tn", out_dtype=BF16, rows=n_lat, tm=tm)
    d_wga = matmul("mix_proj_dwa", [(d_ga, h2)], "tn", out_dtype=BF16, rows=n_lat, tm=tm)
    d_wgb = matmul("mix_proj_dwb", [(d_gb, h2)], "tn", out_dtype=BF16, rows=n_lat, tm=tm)
    d_wxbc = matmul("mix_proj_dwx", [(d_pxbc, h2)], "tn", out_dtype=BF16, tm=tm)
    d_wdt = matmul("mix_proj_dwt", [(ddt_f, h2), (ddt_b, h2)], "tn", out_dtype=BF16, tm=tm)
    d_win_t = jnp.concatenate([d_wz, d_wxbc, d_wdt[:2 * n_head], d_wga, d_wgb]).reshape(N_CHIPS, -1, d)
    d_x1, d_s3, d_s4, d_nmix = rowwise_bwd(
        "mix_norm_bwd", fn_norm_mod, [row(x1)], [tabs[3], tabs[4]], [norm_mix], [[row(d_h2)]], [(n_tok, F32, None)],
        tm=tm, n_tiles=n_tok // tm, seg_fn=seg_all, first_fn=first_all, adds={0: (row(d_x2), lat_tiles)})
    mix_names = ["w_in", "ssm_conv_w", "cconv_w"]
    plan.on("ffn1_down_dx", "sc_conv", lambda p: scatter(shard_cols(d_w5), shard_cols(d_w31)))
    plan.on("ffn1_up_dx", "sc_win", lambda p: scatter(d_win_t))
    plan.on("ffn1_gate_dw", "sc_ffn1_down", lambda p: scatter(p.got["ffn1_d_wd"]))
    plan.on("ffn1_up_dw", "sc_ffn1_gate", lambda p: scatter(p.got["ffn1_d_wg"]))
    plan.on("ffn1_up_dw", "sw_mix", lambda p: halves(mix_names, p.got["sc_win"] + p.got["sc_conv"]))
    plan.on("ffn1_norm_bwd", "sc_ffn1_up", lambda p: scatter(p.got["ffn1_d_wu"]))
    plan.on("ffn1_norm_bwd", "sw_ffn1_down", lambda p: halves(["ffn1_down"], p.got["sc_ffn1_down"]))
    d_xt, (d_s0, d_s1, d_g2), d_nffn1 = _ffn_bwd(
        plan, "ffn1", d_x1, saved1, xt, n_tok, tm, seg_all, first_all, tabs[0], tabs[1], tabs[2], norm_ffn1, wg1, wu1, wd1,
        n_lat, lat_tiles)
    swapped.update(zip(mix_names + ["ffn1_down"], plan.got["sw_mix"] + plan.got["sw_ffn1_down"]))
    last_names = ["ffn1_gate", "ffn1_up"]
    last = halves(last_names, plan.got["sc_ffn1_gate"] + plan.got["sc_ffn1_up"])
    grad_x = d_xt.reshape(n_ex, seq_len, d)

    with_ctx0 = lambda t: jnp.concatenate([t, jnp.zeros((1, 1, d), F32)])
    d_tabs = [d_s0, d_s1, d_g2, d_s3, d_s4, with_ctx0(d_g5), with_ctx0(d_s6), with_ctx0(d_s7), with_ctx0(d_g8)]
    d_mod_rows = jnp.concatenate([t[:, 0, :] for t in d_tabs], axis=1)
    n_pad_rows = -(-(n_ex + 1) // 8) * 8
    d_mod_rows = jnp.concatenate([d_mod_rows, jnp.zeros((n_pad_rows - n_ex - 1, 9 * d), F32)])
    small = [("loss", loss_vec), ("norm_ffn1", d_nffn1), ("norm_mix", d_nmix), ("ssm_conv_b", d_conv_b),
             ("dt_bias_fwd", ddtb_f[:, :n_head]), ("dt_bias_bwd", ddtb_b[:, :n_head]), ("a_log_fwd", dalog_f[:, :n_head]),
             ("a_log_bwd", dalog_b[:, :n_head]), ("ssm_d", ddsk[:, :n_head]), ("ssm_norm_w", d_ssmnw), ("cconv_b", d_cb),
             ("cconv_ln_w", d_lnw), ("cconv_ln_b", d_lnb), ("norm_ffn2", d_nffn2), ("final_norm", d_final)]
    n_small = sum(v.size for _, v in small)
    n_pack = -(-n_small // (8 * LANES)) * (8 * LANES)
    pack = jnp.concatenate([v.reshape(-1) for _, v in small] + [jnp.zeros((n_pack - n_small,), F32)]).reshape(-1, LANES)
    (pack_all, d_mod_all), both = exchange_many("gather_small_swap_last", [Rider([pack, d_mod_rows], "all8"), last])
    swapped.update(zip(last_names, both))
    pack_sum = sum_slots("small_sum", pack_all)
    loss = loss_total(pack_sum.reshape(1, n_pack), d).reshape(())
    flat_sum = pack_sum.reshape(-1)
    small_grads, pos = {}, 0
    for nm, v in small:
        small_grads[nm] = flat_sum[pos:pos + v.size]
        pos += v.size
    d_mod_all = d_mod_all.reshape(8 * n_pad_rows, 9 * d)
    cond_rows = [jnp.concatenate([cond[j * n_ex:(j + 1) * n_ex], c_ctx[None, :],
                                  jnp.zeros((n_pad_rows - n_ex - 1, d), F32)]) for j in range(8)]
    cond_bwd = jnp.concatenate(cond_rows)
    d_mod_shard = lax.dynamic_slice(d_mod_all, (0, chip * mod_w), (8 * n_pad_rows, mod_w))
    g_wmod, g_bmod, q_part = mod_bwd(cond_bwd, d_mod_shard, d_mod_all, w_mod[0],
                                     tuple(j * n_pad_rows + n_ex for j in range(8)))
    (q_all,) = exchange("gather_cctx", [q_part], "all8")
    g_cctx = cctx_grad(q_all, c_ctx.reshape(1, d))
    small_grads["c_ctx"], small_grads["b_mod"] = g_cctx.reshape(-1), g_bmod.reshape(-1)

    transposed = {"ffn1_gate", "ffn1_up", "ffn2_gate", "ffn2_up", "w_in"}
    results = {}
    for nm, both in swapped.items():
        flip = (lambda t: jnp.swapaxes(t, 1, 2)) if nm in transposed else (lambda t: t)
        shape = flip(weights[nm]).shape
        two_d = lambda t: flip(t).reshape(shape[-2], shape[-1])
        g_full = both.reshape(1, -1, shape[-1])[:, :shape[-2]]
        results[nm] = [flip(r.reshape(shape)) for r in
                       adamw(f"adamw_{nm}", two_d(weights[nm]), g_full, two_d(mom1[nm]), two_d(mom2[nm]))]
    results["w_mod"] = [r.reshape(w_mod.shape) for r in adamw("adamw_w_mod", w_mod[0], g_wmod[None], m_w_mod[0], v_w_mod[0])]
    small_names = [nm for nm in order if nm not in results]
    n_sm = sum(weights[nm].size for nm in small_names)
    n_smp = -(-n_sm // (8 * LANES)) * (8 * LANES)
    packed = lambda src: jnp.concatenate([src[nm].reshape(-1) for nm in small_names] + [jnp.zeros((n_smp - n_sm,), F32)]).reshape(-1, LANES)
    sm_out = adamw("adamw_small", packed(weights), packed(small_grads)[None], packed(mom1), packed(mom2))
    pos = 0
    for nm in small_names:
        size = weights[nm].size
        results[nm] = [r.reshape(-1)[pos:pos + size].reshape(weights[nm].shape) for r in sm_out]
        pos += size
    return (loss, grad_x, *[results[nm][0] for nm in order], *[results[nm][1] for nm in order],
            *[results[nm][2] for nm in order], *[results[nm][3] for nm in order])
```

```python
import functools
import math

import jax
import jax.numpy as jnp
from jax import lax
from jax.experimental import pallas as pl
from jax.experimental.pallas import tpu as pltpu

F32 = jnp.float32
BF16 = jnp.bfloat16
HI = lax.Precision.HIGHEST
MESH = pl.DeviceIdType.MESH

EPS = 1e-6
GRID_W = 64
HEAD_DIM = 64
N_STATE = 128
CHUNK = 128
LANES = 128
N_CHIPS = 4
ADAM_LR, ADAM_B1, ADAM_B2, ADAM_EPS, ADAM_WD, ADAM_STEP = 0.001, 0.9, 0.999, 1e-08, 0.01, 10
VMEM_CAP = 56 * 1024 * 1024


def _params(vmem_bytes=None, n_axes=1):
    kw = dict(dimension_semantics=("arbitrary",) * n_axes)
    if vmem_bytes is not None:
        kw["vmem_limit_bytes"] = int(min(VMEM_CAP, max(32 * 1024 * 1024, vmem_bytes)))
    return pltpu.CompilerParams(**kw)


def _big(shape, dtype):
    return pltpu.HBM(tuple(shape), dtype)


def _in_hbm(args):
    return [pltpu.with_memory_space_constraint(a, pltpu.HBM) if a.size * a.dtype.itemsize >= (1 << 20) else a for a in args]


def _nbytes(shape, dtype):
    return math.prod(shape) * jnp.dtype(dtype).itemsize


def _row_tile(rows, width, cap_bytes=1 << 20, mult=8):
    best = None
    for t in range(mult, rows + 1, mult):
        if rows % t == 0 and t * width * 4 <= cap_bytes:
            best = t
    return best if best is not None else rows


_MODES = {"all8": (8, (1, 2, 3, 4, 5, 6, 7), 0), "chips": (4, (2, 4, 6), 1), "sibling": (2, (1,), 0)}


class Rider:
    def __init__(self, arrs, mode, scatter=False):
        self.arrs, self.scatter = list(arrs), scatter
        self.nslot, self.deltas, self.shift = _MODES[mode]
        self.n = len(self.arrs)
        self.out_shape = [jax.ShapeDtypeStruct((self.nslot,) + (a.shape[1:] if scatter else a.shape), a.dtype)
                          for a in self.arrs]
        any_spec = pl.BlockSpec(memory_space=pl.ANY)
        self.in_specs = [any_spec] * self.n
        self.out_specs = [any_spec] * self.n
        n_peer = len(self.deltas)
        self.scratch = [pltpu.SemaphoreType.DMA((self.n, n_peer)), pltpu.SemaphoreType.DMA((self.n, n_peer)),
                        pltpu.SemaphoreType.DMA((self.n,))]

    def _copies(self, ins, outs, sems, arrivals):
        send_sems, recv_sems, local_sems = sems
        x, y, c = lax.axis_index("x"), lax.axis_index("y"), lax.axis_index("c")
        me = 4 * x + 2 * y + c
        slot_of = lambda dev: (dev >> self.shift) & (self.nslot - 1)
        src = lambda a, slot: ins[a].at[slot] if self.scatter else ins[a]
        flip = lambda v, bit: 1 - v if bit else v

        def remote(a, k, d, from_slot, to_slot):
            return pltpu.make_async_remote_copy(
                src_ref=src(a, from_slot), dst_ref=outs[a].at[to_slot], send_sem=send_sems.at[a, k],
                recv_sem=recv_sems.at[a, k], device_id=(flip(x, (d >> 2) & 1), flip(y, (d >> 1) & 1), flip(c, d & 1)),
                device_id_type=MESH)

        mine = slot_of(me)
        local = [pltpu.make_async_copy(src(a, mine), outs[a].at[mine], local_sems.at[a]) for a in range(self.n)]
        sends = [remote(a, k, d, slot_of(me ^ d), mine) for k, d in enumerate(self.deltas) for a in range(self.n)]
        if not arrivals:
            return local, sends
        return local, sends, [remote(a, k, d, mine, slot_of(me ^ d)) for k, d in enumerate(self.deltas) for a in range(self.n)]

    def start(self, ins, outs, sems):
        local, sends = self._copies(ins, outs, sems, arrivals=False)
        for cp in local + sends:
            cp.start()

    def wait(self, ins, outs, sems):
        local, sends, recvs = self._copies(ins, outs, sems, arrivals=True)
        for cp in recvs:
            cp.wait_recv()
        for cp in sends:
            cp.wait_send()
        for cp in local:
            cp.wait()


class Riders:
    def __init__(self, riders):
        self.riders = list(riders)
        self.n = sum(r.n for r in self.riders)
        cat = lambda attr: [v for r in self.riders for v in getattr(r, attr)]
        self.arrs, self.out_shape, self.in_specs = cat("arrs"), cat("out_shape"), cat("in_specs")
        self.out_specs, self.scratch = cat("out_specs"), cat("scratch")

    def _each(self, method, ins, outs, sems):
        i = s = 0
        for r in self.riders:
            getattr(r, method)(ins[i:i + r.n], outs[i:i + r.n], sems[s:s + len(r.scratch)])
            i, s = i + r.n, s + len(r.scratch)

    def start(self, ins, outs, sems):
        self._each("start", ins, outs, sems)

    def wait(self, ins, outs, sems):
        self._each("wait", ins, outs, sems)


class _Hosted:
    def __init__(self, rider, n_in, n_out, n_scratch, grid):
        self.rider, self.n_in, self.n_out, self.n_scratch, self.grid = rider, n_in, n_out, n_scratch, grid
        self.n = rider.n if rider else 0

    def split(self, refs):
        a, b = self.n_in, self.n_in + self.n
        c, e = b + self.n_out, b + self.n_out + self.n
        self._r = (refs[a:b], refs[c:e], refs[e + self.n_scratch:])
        if self.rider:
            ids = [pl.program_id(ax) for ax in range(len(self.grid))]
            first = functools.reduce(jnp.logical_and, [i == 0 for i in ids]) if ids else True
            pl.when(first)(lambda: self.rider.start(*self._r))
        return refs[:a], refs[b:c], refs[e:e + self.n_scratch]

    def finish(self):
        if self.rider:
            ids = [pl.program_id(ax) for ax in range(len(self.grid))]
            last = functools.reduce(jnp.logical_and, [i == n - 1 for i, n in zip(ids, self.grid)]) if ids else True
            pl.when(last)(lambda: self.rider.wait(*self._r))

    def call_args(self, in_specs, out_shape, out_specs, scratch, args):
        r = self.rider
        if not r:
            return list(in_specs), tuple(out_shape), tuple(out_specs), list(scratch), list(args)
        return (list(in_specs) + r.in_specs, tuple(out_shape) + tuple(r.out_shape), tuple(out_specs) + tuple(r.out_specs),
                list(scratch) + r.scratch, list(args) + r.arrs)

    def results(self, res, unwrap=True):
        res = list(res) if isinstance(res, (tuple, list)) else [res]
        host = res[:self.n_out]
        host = host[0] if (self.n_out == 1 and unwrap) else tuple(host)
        return (host, res[self.n_out:]) if self.rider else host


def exchange_many(name, riders):
    both = Riders(riders)

    def body(*refs):
        ins, outs, sems = refs[:both.n], refs[both.n:2 * both.n], refs[2 * both.n:]
        both.start(ins, outs, sems)
        both.wait(ins, outs, sems)

    res = list(pl.pallas_call(
        body, name=name, out_shape=tuple(both.out_shape), in_specs=both.in_specs, out_specs=tuple(both.out_specs),
        scratch_shapes=both.scratch,
    )(*both.arrs))
    split = []
    for r in riders:
        split.append(res[:r.n])
        res = res[r.n:]
    return split


def exchange(name, arrs, mode, scatter=False):
    rider = Rider(arrs, mode, scatter)

    def body(*refs):
        ins, outs, sems = refs[:rider.n], refs[rider.n:2 * rider.n], refs[2 * rider.n:]
        rider.start(ins, outs, sems)
        rider.wait(ins, outs, sems)

    return pl.pallas_call(
        body, name=name, out_shape=tuple(rider.out_shape), in_specs=rider.in_specs, out_specs=tuple(rider.out_specs),
        scratch_shapes=rider.scratch,
    )(*arrs)


_DIMS = {"nn": (((1,), (0,)), ((), ())), "nt": (((1,), (1,)), ((), ())), "tn": (((0,), (0,)), ((), ()))}


def matmul(name, pairs, kind, *, a_ch=False, b_ch=False, out_ch=False, out_dtype=F32, rows=None, row_off=0, tm=512,
           rider=None, post=None, fold=False, place=None):
    a0, b0 = pairs[0]
    n_chunk = a0.shape[0] if a_ch else (b0.shape[0] if b_ch else 1)
    total_rows = a0.shape[-2]
    rows = total_rows - row_off if rows is None else rows
    tm = min(tm, rows)
    assert rows % tm == 0 and row_off % tm == 0, (name, rows, tm, row_off)
    n_rt, off = rows // tm, row_off // tm
    dims = _DIMS[kind]
    n_pair = len(pairs)

    if kind == "tn":
        grid, red_axis, n_red = (n_chunk, n_rt), 1, n_rt
        a_idx = (lambda k, i: (k, i + off, 0)) if a_ch else (lambda k, i: (i + off, 0))
        b_idx = (lambda k, i: (k, i + off, 0)) if b_ch else (lambda k, i: (i + off, 0))
        a_blk = lambda a: ((None, tm, a.shape[-1]) if a_ch else (tm, a.shape[-1]))
        b_blk = lambda b: ((None, tm, b.shape[-1]) if b_ch else (tm, b.shape[-1]))
        o2 = (a0.shape[-1], b0.shape[-1])
        out_shape = ((n_chunk,) + o2) if out_ch else o2
        out_spec = pl.BlockSpec((None,) + o2, lambda k, i: (k, 0, 0)) if out_ch else pl.BlockSpec(o2, lambda k, i: (0, 0))
        acc_shape = o2
    else:
        n_out = b0.shape[-1] if kind == "nn" else b0.shape[-2]
        b2 = b0.shape[-2:]
        if a_ch and b_ch and not out_ch and fold:
            grid, red_axis, n_red = (n_rt,), None, 1
            a_idx, b_idx = (lambda i: (0, i + off, 0)), (lambda i: (0, 0, 0))
            a_blk = lambda a: (n_chunk, tm, a.shape[-1])
            b_blk = lambda b: tuple(b.shape)
            out_shape, out_spec = (rows, n_out), pl.BlockSpec((tm, n_out), lambda i: (i, 0))
        elif a_ch and b_ch and not out_ch:
            grid, red_axis, n_red = (n_rt, n_chunk), 1, n_chunk
            a_idx, b_idx = (lambda i, k: (k, i + off, 0)), (lambda i, k: (k, 0, 0))
            a_blk = lambda a: (None, tm, a.shape[-1])
            b_blk = lambda b: (None,) + tuple(b.shape[-2:])
            out_shape, out_spec = (rows, n_out), pl.BlockSpec((tm, n_out), lambda i, k: (i, 0))
        elif out_ch and fold:
            assert b_ch and not a_ch and all(a is a0 for a, _ in pairs)
            grid, red_axis, n_red = (n_rt,), None, 1
            a_idx, b_idx = (lambda i: (i + off, 0)), (lambda i: (0, 0, 0))
            a_blk = lambda a: (tm, a.shape[-1])
            b_blk = lambda b: tuple(b.shape)
            out_shape, out_spec = (n_chunk, rows, n_out), pl.BlockSpec((n_chunk, tm, n_out), lambda i: (0, i, 0))
        elif out_ch:
            assert b_ch and not a_ch
            grid, red_axis, n_red = (n_chunk, n_rt), None, 1
            a_idx, b_idx = (lambda k, i: (i + off, 0)), (lambda k, i: (k, 0, 0))
            a_blk = lambda a: (tm, a.shape[-1])
            b_blk = lambda b: (None,) + tuple(b.shape[-2:])
            out_shape, out_spec = (n_chunk, rows, n_out), pl.BlockSpec((None, tm, n_out), lambda k, i: (k, i, 0))
        else:
            assert not (a_ch or b_ch)
            grid, red_axis, n_red = (n_rt,), None, 1
            a_idx, b_idx = (lambda i: (i + off, 0)), (lambda i: (0, 0))
            a_blk = lambda a: (tm, a.shape[-1])
            b_blk = lambda b: tuple(b.shape)
            out_shape, out_spec = (rows, n_out), pl.BlockSpec((tm, n_out), lambda i: (i, 0))
            if place is not None:
                out_shape, o_off = (place[0], n_out), place[1] // tm
                out_spec = pl.BlockSpec((tm, n_out), lambda i: (i + o_off, 0))
        acc_shape = (tm, n_out)

    into = [] if place is None or place[2] is None else [place[2]]
    post_ins, post_fn, out_dtypes = ([], None, [out_dtype]) if post is None else post
    hosted = _Hosted(rider, 2 * n_pair + len(post_ins) + len(into), len(out_dtypes), int(n_red > 1), grid)

    def body(*refs):
        ins, outs, scr = hosted.split(refs)

        def compute():
            acc = None
            for p in range(n_pair):
                for k in ([None] if not fold else range(n_chunk)):
                    pick = (lambda r: r[...]) if k is None else (lambda r: r[k])
                    d = lax.dot_general(pick(ins[2 * p]).astype(BF16), pick(ins[2 * p + 1]).astype(BF16), dims,
                                        preferred_element_type=F32)
                    acc = d if acc is None else acc + d
            return acc

        def emit(acc):
            vals = (acc,) if post_fn is None else post_fn(
                acc, *[r[...].astype(F32) for r in ins[2 * n_pair:2 * n_pair + len(post_ins)]])
            for o_ref, v in zip(outs, vals):
                o_ref[...] = v.astype(o_ref.dtype)

        if out_ch and fold:
            a_tile = ins[0][...].astype(BF16)
            for k in range(n_chunk):
                accs = [lax.dot_general(a_tile, ins[2 * p + 1][k].astype(BF16), dims, preferred_element_type=F32)
                        for p in range(n_pair)]
                tiles = [r[k].astype(F32) for r in ins[2 * n_pair:2 * n_pair + len(post_ins)]]
                vals = tuple(accs) if post_fn is None else post_fn(*accs, *tiles)
                for o_ref, v in zip(outs, vals):
                    o_ref[k] = v.astype(o_ref.dtype)
        elif n_red == 1:
            emit(compute())
        else:
            acc_ref = scr[0]
            r = pl.program_id(red_axis)

            @pl.when(r == 0)
            def _():
                acc_ref[...] = jnp.zeros_like(acc_ref)

            acc_ref[...] += compute()

            @pl.when(r == n_red - 1)
            def _():
                emit(acc_ref[...])
        hosted.finish()

    in_specs, args, vmem = [], [], 0
    for a, b in pairs:
        in_specs += [pl.BlockSpec(a_blk(a), a_idx), pl.BlockSpec(b_blk(b), b_idx)]
        args += [a, b]
        vmem += 2 * (_nbytes([s for s in a_blk(a) if s], a.dtype) + _nbytes([s for s in b_blk(b) if s], b.dtype))
    in_specs += [out_spec] * len(post_ins)
    args += list(post_ins)
    aliases = {len(args): 0} if into else {}
    in_specs += [pl.BlockSpec(memory_space=pl.ANY)] * len(into)
    args += into
    tiles_per_step = n_chunk if (out_ch and fold) else 1
    vmem += (3 + 2 * n_pair + tiles_per_step * (len(post_ins) + len(out_dtypes))) * _nbytes(acc_shape, F32)
    scratch = [pltpu.VMEM(acc_shape, F32)] if n_red > 1 else []
    in_specs, out_shapes, out_specs, scratch, args = hosted.call_args(
        in_specs, [_big(out_shape, dt) for dt in out_dtypes], [out_spec] * len(out_dtypes), scratch, args)
    return hosted.results(pl.pallas_call(
        body, name=name, out_shape=out_shapes, grid=grid, in_specs=in_specs, out_specs=out_specs,
        input_output_aliases=aliases, scratch_shapes=scratch, compiler_params=_params(vmem + (8 << 20), len(grid)),
    )(*_in_hbm(args)))


def row(arr, width=None, cb=0, roff=0):
    return (arr, arr.shape[-1] if width is None else width, cb, roff)


def two_rows(first, second, limit):
    return (first, first.shape[-1], 0, 0, (second, limit))


def _row_inputs(rows, tm):
    specs, arrs, slots = [], [], []
    for d in rows:
        second, limit = d[4] if len(d) > 4 else (None, None)
        slots.append((len(arrs), limit))
        specs.append(_row_spec(d[:4], tm, limit))
        arrs.append(d[0])
        if second is not None:
            specs.append(pl.BlockSpec((tm, d[1]), lambda i, limit=limit: (jnp.maximum(i - limit, 0), 0)))
            arrs.append(second)

    def read(refs, i):
        vals = []
        for at, limit in slots:
            v = refs[at][...].astype(F32)
            vals.append(v if limit is None else jnp.where(i < limit, v, refs[at + 1][...].astype(F32)))
        return vals

    return specs, arrs, read


def _row_spec(desc, tm, limit=None):
    _, width, cb, roff = desc[:4]
    if limit is None:
        return pl.BlockSpec((tm, width), lambda i: (i + roff, cb))
    return pl.BlockSpec((tm, width), lambda i: (jnp.minimum(i, limit - 1) + roff, cb))


def _segmenter(tm, seq_len, n_lat):
    seg = lambda i: jnp.where(i * tm < n_lat, (i * tm) // seq_len, n_lat // seq_len)
    first = lambda i: jnp.where(i * tm < n_lat, (i * tm) % seq_len == 0, i * tm == n_lat)
    return seg, first


def rowwise(name, fn, rows, segs, params, outs, *, tm, n_tiles, seg_fn=None, rider=None):
    row_specs, row_arrs, read_rows = _row_inputs(rows, tm)
    n_r, n_s, n_p = len(row_arrs), len(segs), len(params)
    hosted = _Hosted(rider, n_r + n_s + n_p, len(outs), 0, (n_tiles,))

    def body(*refs):
        ins, out_refs, _ = hosted.split(refs)
        vals = read_rows(ins[:n_r], pl.program_id(0)) + [r[...] for r in ins[n_r:]]
        res = fn(*vals)
        for o_ref, v in zip(out_refs, res):
            o_ref[...] = v.astype(o_ref.dtype)
        hosted.finish()

    in_specs = list(row_specs)
    in_specs += [pl.BlockSpec((None, 1, s.shape[-1]), lambda i: (seg_fn(i), 0, 0)) for s in segs]
    in_specs += [pl.BlockSpec(p.shape, lambda i: (0, 0)) for p in params]
    vmem = sum(2 * tm * d[1] * 4 for d in rows) + sum(3 * tm * w * 4 for _, w, _ in outs) + sum(2 * p.size * 4 for p in params)
    in_specs, out_shapes, out_specs, scratch, args = hosted.call_args(
        in_specs, [_big((r, w), dt) for r, w, dt in outs],
        [pl.BlockSpec((tm, w), lambda i: (i, 0)) for _, w, _ in outs], [], row_arrs + list(segs) + list(params))
    return hosted.results(pl.pallas_call(
        body, name=name, grid=(n_tiles,), in_specs=in_specs, out_shape=out_shapes, out_specs=out_specs,
        scratch_shapes=scratch, compiler_params=_params(2 * vmem + (8 << 20)),
    )(*_in_hbm(args)), unwrap=False)


def rowwise_bwd(name, fn, rows, segs, params, cts, row_grads, *, tm, n_tiles, seg_fn=None, first_fn=None, adds=None,
                rider=None):
    adds = adds or {}
    need = [k for k, v in enumerate(row_grads) if v is not None]
    row_specs, row_arrs, read_rows = _row_inputs(rows, tm)
    n_r, n_s, n_p = len(row_arrs), len(segs), len(params)
    n_ct = sum(len(lst) for lst in cts)
    add_keys = sorted(adds)
    hosted = _Hosted(rider, n_r + n_s + n_p + n_ct + len(add_keys), len(need) + n_s + n_p, 0, (n_tiles,))

    def body(*refs):
        host_in, host_out, _ = hosted.split(refs)
        it = iter(list(host_in) + list(host_out))
        row_refs = [next(it) for _ in range(n_r)]
        seg_refs = [next(it) for _ in range(n_s)]
        par_refs = [next(it) for _ in range(n_p)]
        ct_refs = [[next(it) for _ in lst] for lst in cts]
        add_refs = {k: next(it) for k in add_keys}
        rg_refs = {k: next(it) for k in need}
        sg_refs = [next(it) for _ in range(n_s)]
        pg_refs = [next(it) for _ in range(n_p)]
        i = pl.program_id(0)
        rv = read_rows(row_refs, i)
        sv = [r[...] for r in seg_refs]
        pv = [r[...] for r in par_refs]

        def f(*args):
            rr = list(rv)
            for j, k in enumerate(need):
                rr[k] = args[j]
            return fn(*rr, *args[len(need):])

        _, vjp = jax.vjp(f, *[rv[k] for k in need], *sv, *pv)
        ctv = []
        for lst in ct_refs:
            acc = lst[0][...].astype(F32)
            for r in lst[1:]:
                acc = acc + r[...].astype(F32)
            ctv.append(acc)
        g = vjp(tuple(ctv))
        for j, k in enumerate(need):
            gv = g[j]
            if k in adds:
                lim = adds[k][1]
                av = add_refs[k][...].astype(F32)
                gv = gv + (av if lim is None else jnp.where(i < lim, av, 0.0))
            lim = row_grads[k][2]
            if lim is None:
                rg_refs[k][...] = gv.astype(rg_refs[k].dtype)
            else:
                @pl.when(i < lim)
                def _(gv=gv, k=k):
                    rg_refs[k][...] = gv.astype(rg_refs[k].dtype)
        if n_s:
            opens = first_fn(i)
            for ref, gv in zip(sg_refs, g[len(need):len(need) + n_s]):
                @pl.when(opens)
                def _(ref=ref, gv=gv):
                    ref[...] = gv

                @pl.when(jnp.logical_not(opens))
                def _(ref=ref, gv=gv):
                    ref[...] += gv
        for ref, gv in zip(pg_refs, g[len(need) + n_s:]):
            @pl.when(i == 0)
            def _(ref=ref, gv=gv):
                ref[...] = gv

            @pl.when(i > 0)
            def _(ref=ref, gv=gv):
                ref[...] += gv
        hosted.finish()

    seg_spec = lambda s: pl.BlockSpec((None, 1, s.shape[-1]), lambda i: (seg_fn(i), 0, 0))
    par_spec = lambda p: pl.BlockSpec(p.shape, lambda i: (0, 0))
    in_specs = list(row_specs) + [seg_spec(s) for s in segs] + [par_spec(p) for p in params]
    args = row_arrs + list(segs) + list(params)
    for lst in cts:
        in_specs += [_row_spec(d, tm) for d in lst]
        args += [d[0] for d in lst]
    for k in add_keys:
        in_specs.append(_row_spec(adds[k][0], tm, adds[k][1]))
        args.append(adds[k][0][0])
    out_shape, out_specs = [], []
    for k in need:
        n_rows, dt, lim = row_grads[k]
        out_shape.append(_big((n_rows, rows[k][1]), dt))
        out_specs.append(_row_spec((None, rows[k][1], 0, 0), tm, lim))
    for s in segs:
        out_shape.append(jax.ShapeDtypeStruct(s.shape, F32))
        out_specs.append(seg_spec(s))
    for p in params:
        out_shape.append(jax.ShapeDtypeStruct(p.shape, F32))
        out_specs.append(par_spec(p))
    vmem = sum(tm * d[1] * 4 for d in rows) * 6 + n_ct * tm * max(d[1] for d in rows) * 8
    in_specs, out_shape, out_specs, scratch, args = hosted.call_args(in_specs, out_shape, out_specs, [], args)
    return hosted.results(pl.pallas_call(
        body, name=name, grid=(n_tiles,), in_specs=in_specs, out_shape=out_shape, out_specs=out_specs,
        scratch_shapes=scratch, compiler_params=_params(vmem + (8 << 20)),
    )(*_in_hbm(args)), unwrap=False)


def _silu(v):
    return v * jax.nn.sigmoid(v)


def _rms(v, w):
    return v * lax.rsqrt(jnp.mean(v * v, axis=-1, keepdims=True) + EPS) * w


def fn_norm_mod(x, shift, scale, w):
    return (_rms(x, w) * (1.0 + scale) + shift,)


def fn_act(g, u):
    return (_silu(g) * u,)


def make_fn_resid(coef):
    def fn(x, f, gate):
        return (x + coef * gate * f,)
    return fn


def fn_silu_bias(v, b):
    return (_silu(v + b),)


def make_fn_gate_groupnorm(width):
    half = width // 2

    def fn(y_both, z, w):
        y = y_both * _silu(z)
        lane = lax.broadcasted_iota(jnp.int32, y.shape, 1)
        lo = lane < half
        sq = y * y
        s_lo = jnp.sum(jnp.where(lo, sq, 0.0), axis=-1, keepdims=True)
        s_hi = jnp.sum(jnp.where(lo, 0.0, sq), axis=-1, keepdims=True)
        r = jnp.where(lo, lax.rsqrt(s_lo / half + EPS), lax.rsqrt(s_hi / half + EPS))
        return (y * r * w,)
    return fn


def fn_glu(a, b):
    return (a * jax.nn.sigmoid(b),)


def fn_ln_silu(vw, vh, cb, lw, lb):
    v = jnp.concatenate([vw, vh], axis=-1) + cb
    mu = jnp.mean(v, axis=-1, keepdims=True)
    var = jnp.mean(jnp.square(v - mu), axis=-1, keepdims=True)
    return (_silu((v - mu) * lax.rsqrt(var + EPS) * lw + lb),)


def _col_tile(width):
    return width // 3 if width % (3 * LANES) == 0 else width


def mod_fwd(a_rows, w_shard, b_shard):
    n, d = a_rows.shape
    ws = w_shard.shape[1]
    tn = _col_tile(ws)

    def body(a_ref, w_ref, b_ref, o_ref):
        a = _silu(a_ref[...]).astype(BF16)
        o_ref[...] = jnp.dot(a, w_ref[...].astype(BF16), preferred_element_type=F32) + b_ref[...]

    return pl.pallas_call(
        body, name="mod_fwd", grid=(ws // tn,), out_shape=jax.ShapeDtypeStruct((n, ws), F32),
        in_specs=[pl.BlockSpec((n, d), lambda j: (0, 0)), pl.BlockSpec((d, tn), lambda j: (0, j)),
                  pl.BlockSpec((1, tn), lambda j: (0, j))],
        out_specs=pl.BlockSpec((n, tn), lambda j: (0, j)), compiler_params=_params(),
    )(a_rows, w_shard, b_shard)


def mod_bwd(a_rows, d_shard, d_full, w_shard, ctx_rows):
    n, d = a_rows.shape
    ws = w_shard.shape[1]
    tn = _col_tile(ws)
    n_ct = ws // tn

    def body(a_ref, ds_ref, df_ref, w_ref, gw_ref, gb_ref, q_ref):
        j = pl.program_id(0)
        a = _silu(a_ref[...])
        ds = ds_ref[...]
        gw_ref[...] = lax.dot_general(a, ds, _DIMS["tn"], precision=HI, preferred_element_type=F32)
        dctx = ds[ctx_rows[0]:ctx_rows[0] + 1, :]
        for r in ctx_rows[1:]:
            dctx = dctx + ds[r:r + 1, :]
        q = lax.dot_general(jnp.broadcast_to(dctx, (8, tn)), w_ref[...], _DIMS["nt"], precision=HI,
                            preferred_element_type=F32)

        @pl.when(j == 0)
        def _():
            q_ref[...] = q
            df = df_ref[...]
            acc = df[0:1, :]
            for r in range(1, n):
                acc = acc + df[r:r + 1, :]
            gb_ref[...] = acc

        @pl.when(j > 0)
        def _():
            q_ref[...] += q

    return pl.pallas_call(
        body, name="mod_bwd", grid=(n_ct,),
        out_shape=(jax.ShapeDtypeStruct((d, ws), F32), jax.ShapeDtypeStruct((1, d_full.shape[1]), F32),
                   jax.ShapeDtypeStruct((8, d), F32)),
        in_specs=[pl.BlockSpec((n, d), lambda j: (0, 0)), pl.BlockSpec((n, tn), lambda j: (0, j)),
                  pl.BlockSpec(d_full.shape, lambda j: (0, 0)), pl.BlockSpec((d, tn), lambda j: (0, j))],
        out_specs=(pl.BlockSpec((d, tn), lambda j: (0, j)), pl.BlockSpec((1, d_full.shape[1]), lambda j: (0, 0)),
                   pl.BlockSpec((8, d), lambda j: (0, 0))),
        compiler_params=_params(40 << 20),
    )(a_rows, d_shard, d_full, w_shard)


def _shifted(xs, d, tok, width):
    if d == 0:
        return xs
    n = xs.shape[0]
    sh = pltpu.roll(xs, (-d) % n, axis=0)
    return jnp.where((tok + d >= 0) & (tok + d < width), sh, 0.0)


def _placed(out_shape, place):
    if place is None:
        return out_shape, 0, 0, None
    return place


def tapsum_roll(name, x, xcb, w, wcb, *, seq_len, n_seq, row_blk_off, width, piece, cb, ncb, pad, flip, place=None):
    n_tap = w.shape[0]
    n_piece = seq_len // piece
    out_shape, o_rb, o_cb, into = _placed((n_seq * seq_len, ncb * cb), place)

    def body(x_ref, w_ref, *rest):
        o_ref = rest[-1]
        wv = w_ref[...]
        tok = lax.broadcasted_iota(jnp.int32, (piece, 1), 0) % width

        def do_piece(p, carry):
            start = pl.multiple_of(p * piece, piece)
            xs = x_ref[pl.ds(start, piece), :]
            acc = jnp.zeros_like(xs)
            for k in range(n_tap):
                d = pad - k if flip else k - pad
                acc = acc + wv[k:k + 1, :] * _shifted(xs, d, tok, width)
            o_ref[pl.ds(start, piece), :] = acc
            return carry

        lax.fori_loop(0, n_piece, do_piece, 0)

    extra = [] if into is None else [into]
    return pl.pallas_call(
        body, name=name, grid=(ncb, n_seq), out_shape=_big(out_shape, F32),
        in_specs=[pl.BlockSpec((seq_len, cb), lambda j, s: (row_blk_off + s, xcb + j)),
                  pl.BlockSpec((n_tap, cb), lambda j, s: (0, wcb + j))] + [pl.BlockSpec(memory_space=pl.ANY)] * len(extra),
        out_specs=pl.BlockSpec((seq_len, cb), lambda j, s: (o_rb + s, o_cb + j)),
        input_output_aliases={2: 0} if extra else {},
        compiler_params=_params(8 * seq_len * cb * 4 + (8 << 20), 2),
    )(*_in_hbm([x, w] + extra))


def tapgrad_roll(name, dy, dycb, dy_blk_off, x, xcb, x_blk_off, *, n_tap, seq_len, n_seq, width, piece, cb, ncb, pad):
    n_piece = seq_len // piece

    def body(dy_ref, x_ref, o_ref):
        @pl.when(pl.program_id(1) == 0)
        def _():
            o_ref[...] = jnp.zeros_like(o_ref)

        tok = lax.broadcasted_iota(jnp.int32, (piece, 1), 0) % width

        def do_piece(p, carry):
            start = pl.multiple_of(p * piece, piece)
            xs = x_ref[pl.ds(start, piece), :]
            dv = dy_ref[pl.ds(start, piece), :]
            for k in range(n_tap):
                o_ref[k:k + 1, :] += jnp.sum(dv * _shifted(xs, k - pad, tok, width), axis=0, keepdims=True)
            return carry

        lax.fori_loop(0, n_piece, do_piece, 0)

    return pl.pallas_call(
        body, name=name, grid=(ncb, n_seq), out_shape=jax.ShapeDtypeStruct((n_tap, ncb * cb), F32),
        in_specs=[pl.BlockSpec((seq_len, cb), lambda j, s: (dy_blk_off + s, dycb + j)),
                  pl.BlockSpec((seq_len, cb), lambda j, s: (x_blk_off + s, xcb + j))],
        out_specs=pl.BlockSpec((n_tap, cb), lambda j, s: (0, j)),
        compiler_params=_params(8 * seq_len * cb * 4 + (8 << 20), 2),
    )(*_in_hbm([dy, x]))


def tapsum_rows(name, x, xcb, w, wcb, *, seq_len, n_seq, cb, ncb, pad, flip, place=None):
    n_tap = w.shape[0]
    n_row = seq_len // GRID_W
    halo = pad * GRID_W
    out_shape, o_rb, o_cb, into = _placed((n_seq * seq_len, ncb * cb), place)

    def body(x_ref, w_ref, *rest):
        o_ref, xp = rest[-2:]
        xp[pl.ds(0, halo), :] = jnp.zeros((halo, cb), F32)
        xp[pl.ds(halo + seq_len, halo), :] = jnp.zeros((halo, cb), F32)
        xp[pl.ds(halo, seq_len), :] = x_ref[...]
        wv = w_ref[...]

        def do_row(r, carry):
            acc = jnp.zeros((GRID_W, cb), F32)
            for k in range(n_tap):
                d = pad - k if flip else k - pad
                acc = acc + wv[k:k + 1, :] * xp[pl.ds(pl.multiple_of((r + pad + d) * GRID_W, GRID_W), GRID_W), :]
            o_ref[pl.ds(pl.multiple_of(r * GRID_W, GRID_W), GRID_W), :] = acc
            return carry

        lax.fori_loop(0, n_row, do_row, 0)

    extra = [] if into is None else [into]
    return pl.pallas_call(
        body, name=name, grid=(ncb, n_seq), out_shape=_big(out_shape, F32),
        in_specs=[pl.BlockSpec((seq_len, cb), lambda j, s: (s, xcb + j)),
                  pl.BlockSpec((n_tap, cb), lambda j, s: (0, wcb + j))] + [pl.BlockSpec(memory_space=pl.ANY)] * len(extra),
        out_specs=pl.BlockSpec((seq_len, cb), lambda j, s: (o_rb + s, o_cb + j)),
        input_output_aliases={2: 0} if extra else {},
        scratch_shapes=[pltpu.VMEM((seq_len + 2 * halo, cb), F32)],
        compiler_params=_params(10 * seq_len * cb * 4 + (8 << 20), 2),
    )(*_in_hbm([x, w] + extra))


def tapgrad_rows(name, dy, dycb, x, xcb, *, n_tap, seq_len, n_seq, cb, ncb, pad):
    n_row = seq_len // GRID_W
    halo = pad * GRID_W

    def body(dy_ref, x_ref, o_ref, xp):
        @pl.when(pl.program_id(1) == 0)
        def _():
            o_ref[...] = jnp.zeros_like(o_ref)

        xp[pl.ds(0, halo), :] = jnp.zeros((halo, cb), F32)
        xp[pl.ds(halo + seq_len, halo), :] = jnp.zeros((halo, cb), F32)
        xp[pl.ds(halo, seq_len), :] = x_ref[...]

        def do_row(r, carry):
            dv = dy_ref[pl.ds(pl.multiple_of(r * GRID_W, GRID_W), GRID_W), :]
            for k in range(n_tap):
                xs = xp[pl.ds(pl.multiple_of((r + k) * GRID_W, GRID_W), GRID_W), :]
                o_ref[k:k + 1, :] += jnp.sum(dv * xs, axis=0, keepdims=True)
            return carry

        lax.fori_loop(0, n_row, do_row, 0)

    return pl.pallas_call(
        body, name=name, grid=(ncb, n_seq), out_shape=jax.ShapeDtypeStruct((n_tap, ncb * cb), F32),
        in_specs=[pl.BlockSpec((seq_len, cb), lambda j, s: (s, dycb + j)),
                  pl.BlockSpec((seq_len, cb), lambda j, s: (s, xcb + j))],
        out_specs=pl.BlockSpec((n_tap, cb), lambda j, s: (0, j)),
        scratch_shapes=[pltpu.VMEM((seq_len + 2 * halo, cb), F32)],
        compiler_params=_params(10 * seq_len * cb * 4 + (8 << 20), 2),
    )(*_in_hbm([dy, x]))


def _ssd_blocks(b, s, *, rev, n_ctx, n_lat, lat_blocks):
    if rev:
        return jnp.where(s < n_ctx, lat_blocks + b * n_ctx + (n_ctx - 1 - s), b * n_lat + (n_lat - 1 - (s - n_ctx)))
    return jnp.where(s < n_ctx, lat_blocks + b * n_ctx + s, b * n_lat + (s - n_ctx))


def _ssd_common(xbc, raw, dtb, alog, dsk, *, rev, ds, n_head):
    if rev:
        raw = pltpu.roll(raw, LANES - n_head, axis=1)
    pre = raw + dtb
    dt = jnp.maximum(pre, 0.0) + jnp.log(1.0 + jnp.exp(-jnp.abs(pre)))
    sig = jax.nn.sigmoid(pre)
    a = -jnp.exp(alog)
    da = dt * a
    ri = lax.broadcasted_iota(jnp.int32, (CHUNK, CHUNK), 0)
    ci = lax.broadcasted_iota(jnp.int32, (CHUNK, CHUNK), 1)
    mask = (ci >= ri) if rev else (ci <= ri)
    tri = mask.astype(F32)
    tri_t = ((ci <= ri) if rev else (ci >= ri)).astype(F32)
    cs = jnp.dot(tri, da, precision=HI, preferred_element_type=F32)
    tot = jnp.sum(da, axis=0, keepdims=True)
    def wide(v):
        first = lax.broadcasted_iota(jnp.int32, (v.shape[0], LANES), 1) < HEAD_DIM
        return jnp.concatenate(
            [jnp.where(first, jnp.broadcast_to(v[:, 2 * p:2 * p + 1], first.shape),
                       jnp.broadcast_to(v[:, 2 * p + 1:2 * p + 2], first.shape)) for p in range(n_head // 2)], axis=1)

    cs_w, tot_w = wide(cs), wide(tot)
    xh = xbc[:, :ds]
    dt_w = wide(dt)
    return dict(
        dt=dt, sig=sig, a=a, cs=cs, cs_t=cs.T, tot=tot, mask=mask, tri_t=tri_t,
        e_w=jnp.exp(cs_w), wt_w=jnp.exp(tot_w - cs_w), dec_w=jnp.exp(tot_w), dt_w=dt_w, dsk_w=wide(dsk),
        xh=xh, xs_w=xh * dt_w, bm=xbc[:, ds:ds + 2 * N_STATE], cm=xbc[:, ds + 2 * N_STATE:ds + 4 * N_STATE])


def _decay(q, col):
    seg = q["cs"][:, col:col + 1] - q["cs_t"][col:col + 1, :]
    return jnp.exp(jnp.where(q["mask"], seg, -jnp.inf))


def _split_heads(v):
    lane = lax.broadcasted_iota(jnp.int32, v.shape, 1)
    return jnp.concatenate([jnp.where(lane < HEAD_DIM, v, 0.0), jnp.where(lane >= HEAD_DIM, v, 0.0)], axis=0)


def ssd_fwd(name, xbc, proj, dt_cb, dtb, alog, dsk, *, rev, n_ex, seq_len, ctx_len, ds, rider=None, add=None):
    n_head, half = ds // HEAD_DIM, ds // 2
    n_ctx, n_lat = ctx_len // CHUNK, seq_len // CHUNK
    n_step = n_ctx + n_lat
    blk = functools.partial(_ssd_blocks, rev=rev, n_ctx=n_ctx, n_lat=n_lat, lat_blocks=n_ex * n_lat)
    xw = xbc.shape[1]

    def y_blk(b, s):
        sl = jnp.maximum(s, n_ctx) - n_ctx
        return b * n_lat + ((n_lat - 1 - sl) if rev else sl)

    hosted = _Hosted(rider, 5 + (add is not None), 2, 1, (n_ex, n_step))

    def body(*refs):
        (xbc_ref, dt_ref, dtb_ref, alog_ref, dsk_ref, *add_ref), (y_ref, hs_ref), (h_scr,) = hosted.split(refs)

        @pl.when(pl.program_id(1) == 0)
        def _():
            h_scr[...] = jnp.zeros_like(h_scr)

        q = _ssd_common(xbc_ref[...], dt_ref[...], dtb_ref[...], alog_ref[...], dsk_ref[...], rev=rev, ds=ds, n_head=n_head)
        h = h_scr[...]
        hs_ref[...] = h
        for g in range(2):
            lo = g * half
            bg = q["bm"][:, g * N_STATE:(g + 1) * N_STATE].astype(BF16)
            cg = q["cm"][:, g * N_STATE:(g + 1) * N_STATE].astype(BF16)
            scores = lax.dot_general(cg, bg, _DIMS["nt"], preferred_element_type=F32)
            hg = h[:, lo:lo + half]
            off = jnp.dot(cg, hg.astype(BF16), preferred_element_type=F32)
            for j in range(half // LANES):
                c0 = (lo + j * LANES) // HEAD_DIM
                ln = slice(lo + j * LANES, lo + (j + 1) * LANES)
                p_cat = jnp.concatenate([scores * _decay(q, c0), scores * _decay(q, c0 + 1)], axis=1).astype(BF16)
                diag = jnp.dot(p_cat, _split_heads(q["xs_w"][:, ln]).astype(BF16), preferred_element_type=F32)
                y_ref[:, ln] = (diag + q["e_w"][:, ln] * off[:, j * LANES:(j + 1) * LANES]
                                + q["dsk_w"][:, ln] * q["xh"][:, ln] + (add_ref[0][:, ln] if add_ref else 0.0))
            v = (q["wt_w"][:, lo:lo + half] * q["xs_w"][:, lo:lo + half]).astype(BF16)
            h_scr[:, lo:lo + half] = (q["dec_w"][:, lo:lo + half] * hg
                                      + lax.dot_general(bg, v, _DIMS["tn"], preferred_element_type=F32))
        hosted.finish()

    vec = pl.BlockSpec((1, LANES), lambda b, s: (0, 0))
    in_specs, out_shape, out_specs, scratch, args = hosted.call_args(
        [pl.BlockSpec((CHUNK, xw), lambda b, s: (blk(b, s), 0)),
         pl.BlockSpec((CHUNK, LANES), lambda b, s: (blk(b, s), dt_cb)), vec, vec, vec]
        + [pl.BlockSpec((CHUNK, ds), lambda b, s: (y_blk(b, s), 0))] * (add is not None),
        (_big((n_ex * seq_len, ds), F32), _big((n_ex, n_step, N_STATE, ds), F32)),
        (pl.BlockSpec((CHUNK, ds), lambda b, s: (y_blk(b, s), 0)),
         pl.BlockSpec((None, None, N_STATE, ds), lambda b, s: (b, s, 0, 0))),
        [pltpu.VMEM((N_STATE, ds), F32)], [xbc, proj, dtb, alog, dsk] + ([] if add is None else [add]))
    return hosted.results(pl.pallas_call(
        body, name=name, grid=(n_ex, n_step), out_shape=out_shape, in_specs=in_specs, out_specs=out_specs,
        scratch_shapes=scratch, compiler_params=_params(40 << 20, 2),
    )(*_in_hbm(args)))


def ssd_bwd(name, xbc, proj, dt_cb, hs, dy, dtb, alog, dsk, *, rev, n_ex, seq_len, ctx_len, ds, rider=None, add=None):
    n_head, half = ds // HEAD_DIM, ds // 2
    n_ctx, n_lat = ctx_len // CHUNK, seq_len // CHUNK
    n_step = n_ctx + n_lat
    n_tok = n_ex * (seq_len + ctx_len)
    blk0 = functools.partial(_ssd_blocks, rev=rev, n_ctx=n_ctx, n_lat=n_lat, lat_blocks=n_ex * n_lat)
    step = lambda sp: n_step - 1 - sp
    blk = lambda b, sp: blk0(b, step(sp))
    xw = xbc.shape[1]

    def dy_blk(b, sp):
        sl = jnp.maximum(step(sp), n_ctx) - n_ctx
        return b * n_lat + ((n_lat - 1 - sl) if rev else sl)

    hosted = _Hosted(rider, 7 + (add is not None), 5, 1, (n_ex, n_step))

    def body(*refs):
        ((xbc_ref, dt_ref, hs_ref, dy_ref, dtb_ref, alog_ref, dsk_ref, *add_ref),
         (dxbc_ref, ddt_ref, dalog_ref, ddtb_ref, ddsk_ref), (dh_scr,)) = hosted.split(refs)
        b, sp = pl.program_id(0), pl.program_id(1)
        more = (lambda cols: add_ref[0][:, cols]) if add_ref else (lambda cols: 0.0)

        @pl.when(sp == 0)
        def _():
            dh_scr[...] = jnp.zeros_like(dh_scr)

        @pl.when((sp == 0) & (b == 0))
        def _():
            dalog_ref[...] = jnp.zeros_like(dalog_ref)
            ddtb_ref[...] = jnp.zeros_like(ddtb_ref)
            ddsk_ref[...] = jnp.zeros_like(ddsk_ref)

        q = _ssd_common(xbc_ref[...], dt_ref[...], dtb_ref[...], alog_ref[...], dsk_ref[...], rev=rev, ds=ds, n_head=n_head)
        h = hs_ref[...]
        d_y = jnp.where(step(sp) >= n_ctx, dy_ref[...], 0.0)
        dh_next = dh_scr[...]
        lane_row = lax.broadcasted_iota(jnp.int32, (1, LANES), 1)
        d_cs = jnp.zeros((CHUNK, LANES), F32)
        dxs_parts, de_parts, dwt_parts, ddec_parts = [], [], [], []
        for g in range(2):
            lo = g * half
            gs = slice(lo, lo + half)
            bg = q["bm"][:, g * N_STATE:(g + 1) * N_STATE].astype(BF16)
            cg = q["cm"][:, g * N_STATE:(g + 1) * N_STATE].astype(BF16)
            scores = lax.dot_general(cg, bg, _DIMS["nt"], preferred_element_type=F32)
            hg, dyg, dhn = h[:, gs], d_y[:, gs], dh_next[:, gs]
            off = jnp.dot(cg, hg.astype(BF16), preferred_element_type=F32)
            d_off = (q["e_w"][:, gs] * dyg).astype(BF16)
            de_parts.append(dyg * off)
            d_c = lax.dot_general(d_off, hg.astype(BF16), _DIMS["nt"], preferred_element_type=F32)
            dh_scr[:, gs] = (lax.dot_general(cg, d_off, _DIMS["tn"], preferred_element_type=F32)
                             + q["dec_w"][:, gs] * dhn)
            b_dh = jnp.dot(bg, dhn.astype(BF16), preferred_element_type=F32)
            v = q["wt_w"][:, gs] * q["xs_w"][:, gs]
            d_b = lax.dot_general(v.astype(BF16), dhn.astype(BF16), _DIMS["nt"], preferred_element_type=F32)
            dwt_parts.append(q["xs_w"][:, gs] * b_dh)
            ddec_parts.append(jnp.sum(hg * dhn, axis=0, keepdims=True))
            d_scores = jnp.zeros((CHUNK, CHUNK), F32)
            for j in range(half // LANES):
                c0 = (lo + j * LANES) // HEAD_DIM
                ln = slice(lo + j * LANES, lo + (j + 1) * LANES)
                l0, l1 = _decay(q, c0), _decay(q, c0 + 1)
                p0, p1 = scores * l0, scores * l1
                dy_st = _split_heads(d_y[:, ln]).astype(BF16)
                d_p = lax.dot_general(dy_st, q["xs_w"][:, ln].astype(BF16), _DIMS["nt"], preferred_element_type=F32)
                d_p0, d_p1 = d_p[:CHUNK], d_p[CHUNK:]
                d_scores = d_scores + d_p0 * l0 + d_p1 * l1
                for col, t in ((c0, d_p0 * p0), (c0 + 1, d_p1 * p1)):
                    d_cs = d_cs + jnp.sum(t - t.T, axis=1, keepdims=True) * (lane_row == col).astype(F32)
                p_st = jnp.concatenate([p0, p1], axis=0).astype(BF16)
                dxs_parts.append(lax.dot_general(p_st, dy_st, _DIMS["tn"], preferred_element_type=F32)
                                 + q["wt_w"][:, ln] * b_dh[:, j * LANES:(j + 1) * LANES])
            d_sc = d_scores.astype(BF16)
            d_c = d_c + jnp.dot(d_sc, bg, preferred_element_type=F32)
            d_b = d_b + lax.dot_general(d_sc, cg, _DIMS["tn"], preferred_element_type=F32)
            b_cols, c_cols = slice(ds + g * N_STATE, ds + (g + 1) * N_STATE), slice(ds + (2 + g) * N_STATE, ds + (3 + g) * N_STATE)
            dxbc_ref[:, b_cols] = d_b + more(b_cols)
            dxbc_ref[:, c_cols] = d_c + more(c_cols)
        d_xs = jnp.concatenate(dxs_parts, axis=1)
        narrow_m = (lax.broadcasted_iota(jnp.int32, (ds, LANES), 0) // HEAD_DIM
                    == lax.broadcasted_iota(jnp.int32, (ds, LANES), 1)).astype(BF16)
        rows8 = lambda v: jnp.broadcast_to(v, (8, ds))
        stacked = jnp.concatenate(
            [jnp.concatenate(dwt_parts, axis=1), jnp.concatenate(de_parts, axis=1), d_xs * q["xh"],
             rows8(jnp.concatenate(ddec_parts, axis=1)), rows8(jnp.sum(d_y * q["xh"], axis=0, keepdims=True))], axis=0)
        sums = jnp.dot(stacked.astype(BF16), narrow_m, preferred_element_type=F32)
        n_wt, n_e, n_xs = sums[:CHUNK], sums[CHUNK:2 * CHUNK], sums[2 * CHUNK:3 * CHUNK]
        n_dec, n_dsk = sums[3 * CHUNK:3 * CHUNK + 1], sums[3 * CHUNK + 8:3 * CHUNK + 9]
        e, wt, dec = jnp.exp(q["cs"]), jnp.exp(q["tot"] - q["cs"]), jnp.exp(q["tot"])
        d_wt = n_wt * wt
        d_cs = d_cs + n_e * e - d_wt
        d_tot = jnp.sum(d_wt, axis=0, keepdims=True) + n_dec * dec
        d_da = jnp.dot(q["tri_t"], d_cs, precision=HI, preferred_element_type=F32) + d_tot
        d_dt = d_da * q["a"] + n_xs
        dxbc_ref[:, :ds] = d_xs * q["dt_w"] + q["dsk_w"] * d_y + more(slice(0, ds))
        dalog_ref[...] += jnp.sum(d_da * q["dt"], axis=0, keepdims=True) * q["a"]
        d_raw = d_dt * q["sig"]
        ddtb_ref[...] += jnp.sum(d_raw, axis=0, keepdims=True)
        ddsk_ref[...] += n_dsk
        ddt_ref[...] = pltpu.roll(d_raw, n_head, axis=1) if rev else d_raw
        hosted.finish()

    vec = pl.BlockSpec((1, LANES), lambda b, s: (0, 0))
    vec_shape = jax.ShapeDtypeStruct((1, LANES), F32)
    in_specs, out_shape, out_specs, scratch, args = hosted.call_args(
        [pl.BlockSpec((CHUNK, xw), lambda b, s: (blk(b, s), 0)),
         pl.BlockSpec((CHUNK, LANES), lambda b, s: (blk(b, s), dt_cb)),
         pl.BlockSpec((None, None, N_STATE, ds), lambda b, s: (b, step(s), 0, 0)),
         pl.BlockSpec((CHUNK, ds), lambda b, s: (dy_blk(b, s), 0)), vec, vec, vec]
        + [pl.BlockSpec((CHUNK, xw), lambda b, s: (blk(b, s), 0))] * (add is not None),
        (_big((n_tok, xw), F32), _big((n_tok, LANES), F32), vec_shape, vec_shape, vec_shape),
        (pl.BlockSpec((CHUNK, xw), lambda b, s: (blk(b, s), 0)),
         pl.BlockSpec((CHUNK, LANES), lambda b, s: (blk(b, s), 0)), vec, vec, vec),
        [pltpu.VMEM((N_STATE, ds), F32)], [xbc, proj, hs, dy, dtb, alog, dsk] + ([] if add is None else [add]))
    return hosted.results(pl.pallas_call(
        body, name=name, grid=(n_ex, n_step), out_shape=out_shape, in_specs=in_specs, out_specs=out_specs,
        scratch_shapes=scratch, compiler_params=_params(48 << 20, 2),
    )(*_in_hbm(args)))


def final_loss(x3, target, w, *, tm):
    n, d = x3.shape

    def body(x_ref, t_ref, w_ref, dx_ref, dw_ref, loss_ref):
        i = pl.program_id(0)
        t = t_ref[...]

        def per_feature(xv, wv):
            err = _rms(xv, wv) - t
            return 0.5 * jnp.sum(err * err, axis=0, keepdims=True) / d

        lv, vjp = jax.vjp(per_feature, x_ref[...], w_ref[...])
        dx, dw = vjp(jnp.ones_like(lv))
        dx_ref[...] = dx

        @pl.when(i == 0)
        def _():
            dw_ref[...] = dw
            loss_ref[...] = lv

        @pl.when(i > 0)
        def _():
            dw_ref[...] += dw
            loss_ref[...] += lv

    tile = pl.BlockSpec((tm, d), lambda i: (i, 0))
    vec = pl.BlockSpec((1, d), lambda i: (0, 0))
    return pl.pallas_call(
        body, name="final_loss", grid=(n // tm,), in_specs=[tile, tile, vec],
        out_shape=(jax.ShapeDtypeStruct((n, d), F32), jax.ShapeDtypeStruct((1, d), F32), jax.ShapeDtypeStruct((1, d), F32)),
        out_specs=(tile, vec, vec), compiler_params=_params(tm * d * 4 * 16 + (8 << 20)),
    )(x3, target, w)


def sum_slots(name, arr, out_dtype=F32):
    n_slot, n_row, width = arr.shape
    tm = _row_tile(n_row, width * n_slot, mult=16)

    def body(a_ref, o_ref):
        acc = a_ref[0].astype(F32)
        for j in range(1, n_slot):
            acc = acc + a_ref[j].astype(F32)
        o_ref[...] = acc.astype(o_ref.dtype)

    return pl.pallas_call(
        body, name=name, grid=(n_row // tm,), out_shape=jax.ShapeDtypeStruct((n_row, width), out_dtype),
        in_specs=[pl.BlockSpec((n_slot, tm, width), lambda i: (0, i, 0))],
        out_specs=pl.BlockSpec((tm, width), lambda i: (i, 0)), compiler_params=_params(),
    )(arr)


def adamw(name, w, g_slots, m, v):
    n_slot, n_row, width = g_slots.shape
    tm = _row_tile(n_row, width * 2)
    if g_slots.dtype == BF16 and tm % 16:
        tm16 = _row_tile(n_row, width * 2, mult=16)
        if tm16 % 16 == 0:
            tm = tm16
        else:
            g_slots = g_slots.astype(F32)

    def body(w_ref, g_ref, m_ref, v_ref, go_ref, d_ref, mo_ref, vo_ref):
        g = g_ref[0].astype(F32)
        for j in range(1, n_slot):
            g = g + g_ref[j].astype(F32)
        m2 = ADAM_B1 * m_ref[...] + (1.0 - ADAM_B1) * g
        v2 = ADAM_B2 * v_ref[...] + (1.0 - ADAM_B2) * jnp.square(g)
        m_hat = m2 / (1.0 - ADAM_B1 ** ADAM_STEP)
        v_hat = v2 / (1.0 - ADAM_B2 ** ADAM_STEP)
        go_ref[...] = g
        d_ref[...] = -ADAM_LR * (m_hat / (jnp.sqrt(v_hat) + ADAM_EPS) + ADAM_WD * w_ref[...])
        mo_ref[...] = m2
        vo_ref[...] = v2

    tile = pl.BlockSpec((tm, width), lambda i: (i, 0))
    shape = jax.ShapeDtypeStruct((n_row, width), F32)
    return pl.pallas_call(
        body, name=name, grid=(n_row // tm,), out_shape=(shape,) * 4,
        in_specs=[tile, pl.BlockSpec((n_slot, tm, width), lambda i: (0, i, 0)), tile, tile],
        out_specs=(tile,) * 4, compiler_params=_params(),
    )(w, g_slots, m, v)


def cctx_grad(q_all, c_ctx_row):
    d = c_ctx_row.shape[1]

    def body(q_ref, c_ref, o_ref):
        acc = q_ref[0, 0:1, :]
        for j in (2, 4, 6):
            acc = acc + q_ref[j, 0:1, :]
        _, vjp = jax.vjp(_silu, c_ref[...])
        o_ref[...] = vjp(acc)[0]

    return pl.pallas_call(
        body, name="cctx_grad", out_shape=jax.ShapeDtypeStruct((1, d), F32),
    )(q_all, c_ctx_row)


def loss_total(pack_sum, d):
    def body(p_ref, o_ref):
        o_ref[...] = jnp.sum(p_ref[:, 0:d], axis=1, keepdims=True)

    return pl.pallas_call(
        body, name="loss_total", out_shape=jax.ShapeDtypeStruct((1, 1), F32),
    )(pack_sum)


class _Plan:
    def __init__(self):
        self.builders, self.got = {}, {}

    def on(self, host, key, builder):
        self.builders.setdefault(host, []).append((key, builder))

    def run(self, host, fn, *args, **kw):
        if host not in self.builders:
            return fn(host, *args, **kw)
        keys, riders = zip(*[(key, builder(self)) for key, builder in self.builders[host]])
        res, landed = fn(host, *args, rider=Riders(riders), **kw)
        for key, r in zip(keys, riders):
            self.got[key], landed = landed[:r.n], landed[r.n:]
        return res


def _val(w):
    return w() if callable(w) else w


def _matmul_tile(n_rows, tm):
    return 2 * tm if n_rows % (2 * tm) == 0 else tm


def _ffn_fwd(plan, tag, xin, n_rows, tm, seg_fn, shift, scale, gate, norm_w, wg, wu, wd, fuse_gate_up=False):
    d = xin[1]
    n_tiles = n_rows // tm
    (h,) = plan.run(f"{tag}_norm", rowwise, fn_norm_mod, [xin], [shift, scale], [norm_w], [(n_rows, d, BF16)],
                    tm=tm, n_tiles=n_tiles, seg_fn=seg_fn)
    tmm = _matmul_tile(n_rows, tm)
    if fuse_gate_up:
        g, u, act = plan.run(f"{tag}_gate_up", matmul, [(h, _val(wg)), (h, _val(wu))], "nn", b_ch=True, out_ch=True,
                             tm=min(tm, 256), fold=True,
                             post=([], lambda ag, au: (ag, au, fn_act(ag, au)[0]), [BF16, BF16, BF16]))
    else:
        g = plan.run(f"{tag}_gate", matmul, [(h, _val(wg))], "nn", out_dtype=BF16, b_ch=True, out_ch=True, tm=tmm)
        u, act = plan.run(f"{tag}_up", matmul, [(h, _val(wu))], "nn", b_ch=True, out_ch=True, tm=tm, fold=True,
                          post=([g], lambda acc, gv: (acc, fn_act(gv, acc)[0]), [BF16, BF16]))
    f = plan.run(f"{tag}_down", matmul, [(act, _val(wd))], "nn", a_ch=True, b_ch=True, tm=tmm, fold=True)
    (xo,) = plan.run(f"{tag}_resid", rowwise, make_fn_resid(0.5), [xin, row(f)], [gate], [], [(n_rows, d, F32)],
                     tm=tm, n_tiles=n_tiles, seg_fn=seg_fn)
    return xo, (h, g, u, act, f)


def _ffn_bwd(plan, tag, d_xo, saved, xin, n_rows, tm, seg_fn, first_fn, shift, scale, gate, norm_w, wg, wu, wd, dx_rows, dx_limit):
    h, g, u, act, f = saved
    d = xin[1]
    n_tiles = n_rows // tm
    n_ch, _, n_hid = g.shape
    d_f, d_gate = plan.run(f"{tag}_resid_bwd", rowwise_bwd, make_fn_resid(0.5), [xin, row(f)], [gate], [], [[row(d_xo)]],
                           [None, (n_rows, BF16, None)], tm=tm, n_tiles=n_tiles, seg_fn=seg_fn, first_fn=first_fn)
    tmm = _matmul_tile(n_rows, tm)
    def act_vjp(d_act, gv, uv):
        s = jax.nn.sigmoid(gv)
        gs = gv * s
        return d_act * uv * (s + gs * (1.0 - s)), d_act * gs
    d_g, d_u = plan.run(f"{tag}_down_dx", matmul, [(d_f, wd)], "nt", b_ch=True, out_ch=True, tm=tmm,
                        post=([g, u], act_vjp, [BF16, BF16]))
    plan.got[f"{tag}_d_wd"] = plan.run(f"{tag}_down_dw", matmul, [(act, d_f)], "tn", out_dtype=BF16, a_ch=True, out_ch=True, tm=tmm)
    d_h = plan.run(f"{tag}_up_dx", matmul, [(d_g, wg), (d_u, wu)], "nt", a_ch=True, b_ch=True, tm=tmm)
    plan.got[f"{tag}_d_wg"] = plan.run(f"{tag}_gate_dw", matmul, [(d_g, h)], "tn", out_dtype=BF16, a_ch=True, out_ch=True, tm=tmm)
    plan.got[f"{tag}_d_wu"] = plan.run(f"{tag}_up_dw", matmul, [(d_u, h)], "tn", out_dtype=BF16, a_ch=True, out_ch=True, tm=tmm)
    d_x, d_shift, d_scale, d_nw = plan.run(
        f"{tag}_norm_bwd", rowwise_bwd, fn_norm_mod, [xin], [shift, scale], [norm_w], [[row(d_h)]], [(dx_rows, F32, dx_limit)],
        tm=tm, n_tiles=n_tiles, seg_fn=seg_fn, first_fn=first_fn, adds={0: (row(d_xo), None)})
    return d_x, (d_shift, d_scale, d_gate), d_nw


def kernel(x, c, ctx, c_ctx, w_mod, b_mod, norm_ffn1, ffn1_gate, ffn1_up, ffn1_down, norm_mix, w_in, ssm_conv_w, ssm_conv_b, dt_bias_fwd, dt_bias_bwd, a_log_fwd, a_log_bwd, ssm_d, ssm_norm_w, cconv_w, cconv_b, cconv_ln_w, cconv_ln_b, w_out, norm_ffn2, ffn2_gate, ffn2_up, ffn2_down, final_norm, loss_target, m_c_ctx, m_w_mod, m_b_mod, m_norm_ffn1, m_ffn1_gate, m_ffn1_up, m_ffn1_down, m_norm_mix, m_w_in, m_ssm_conv_w, m_ssm_conv_b, m_dt_bias_fwd, m_dt_bias_bwd, m_a_log_fwd, m_a_log_bwd, m_ssm_d, m_ssm_norm_w, m_cconv_w, m_cconv_b, m_cconv_ln_w, m_cconv_ln_b, m_w_out, m_norm_ffn2, m_ffn2_gate, m_ffn2_up, m_ffn2_down, m_final_norm, v_c_ctx, v_w_mod, v_b_mod, v_norm_ffn1, v_ffn1_gate, v_ffn1_up, v_ffn1_down, v_norm_mix, v_w_in, v_ssm_conv_w, v_ssm_conv_b, v_dt_bias_fwd, v_dt_bias_bwd, v_a_log_fwd, v_a_log_bwd, v_ssm_d, v_ssm_norm_w, v_cconv_w, v_cconv_b, v_cconv_ln_w, v_cconv_ln_b, v_w_out, v_norm_ffn2, v_ffn2_gate, v_ffn2_up, v_ffn2_down, v_final_norm):
    weights = dict(c_ctx=c_ctx, w_mod=w_mod, b_mod=b_mod, norm_ffn1=norm_ffn1, ffn1_gate=ffn1_gate, ffn1_up=ffn1_up, ffn1_down=ffn1_down, norm_mix=norm_mix, w_in=w_in, ssm_conv_w=ssm_conv_w, ssm_conv_b=ssm_conv_b, dt_bias_fwd=dt_bias_fwd, dt_bias_bwd=dt_bias_bwd, a_log_fwd=a_log_fwd, a_log_bwd=a_log_bwd, ssm_d=ssm_d, ssm_norm_w=ssm_norm_w, cconv_w=cconv_w, cconv_b=cconv_b, cconv_ln_w=cconv_ln_w, cconv_ln_b=cconv_ln_b, w_out=w_out, norm_ffn2=norm_ffn2, ffn2_gate=ffn2_gate, ffn2_up=ffn2_up, ffn2_down=ffn2_down, final_norm=final_norm)
    mom1 = dict(c_ctx=m_c_ctx, w_mod=m_w_mod, b_mod=m_b_mod, norm_ffn1=m_norm_ffn1, ffn1_gate=m_ffn1_gate, ffn1_up=m_ffn1_up, ffn1_down=m_ffn1_down, norm_mix=m_norm_mix, w_in=m_w_in, ssm_conv_w=m_ssm_conv_w, ssm_conv_b=m_ssm_conv_b, dt_bias_fwd=m_dt_bias_fwd, dt_bias_bwd=m_dt_bias_bwd, a_log_fwd=m_a_log_fwd, a_log_bwd=m_a_log_bwd, ssm_d=m_ssm_d, ssm_norm_w=m_ssm_norm_w, cconv_w=m_cconv_w, cconv_b=m_cconv_b, cconv_ln_w=m_cconv_ln_w, cconv_ln_b=m_cconv_ln_b, w_out=m_w_out, norm_ffn2=m_norm_ffn2, ffn2_gate=m_ffn2_gate, ffn2_up=m_ffn2_up, ffn2_down=m_ffn2_down, final_norm=m_final_norm)
    mom2 = dict(c_ctx=v_c_ctx, w_mod=v_w_mod, b_mod=v_b_mod, norm_ffn1=v_norm_ffn1, ffn1_gate=v_ffn1_gate, ffn1_up=v_ffn1_up, ffn1_down=v_ffn1_down, norm_mix=v_norm_mix, w_in=v_w_in, ssm_conv_w=v_ssm_conv_w, ssm_conv_b=v_ssm_conv_b, dt_bias_fwd=v_dt_bias_fwd, dt_bias_bwd=v_dt_bias_bwd, a_log_fwd=v_a_log_fwd, a_log_bwd=v_a_log_bwd, ssm_d=v_ssm_d, ssm_norm_w=v_ssm_norm_w, cconv_w=v_cconv_w, cconv_b=v_cconv_b, cconv_ln_w=v_cconv_ln_w, cconv_ln_b=v_cconv_ln_b, w_out=v_w_out, norm_ffn2=v_norm_ffn2, ffn2_gate=v_ffn2_gate, ffn2_up=v_ffn2_up, ffn2_down=v_ffn2_down, final_norm=v_final_norm)
    order = list(weights)

    n_ex, seq_len, d = x.shape
    ctx_len = ctx.shape[1]
    ds = d
    n_head = ds // HEAD_DIM
    xw = ds + 4 * N_STATE
    n_lat, n_ctx_rows = n_ex * seq_len, n_ex * ctx_len
    n_tok = n_lat + n_ctx_rows
    tm = math.gcd(math.gcd(512, seq_len), n_ctx_rows)
    seg_all, first_all = _segmenter(tm, seq_len, n_lat)
    lat_tiles = n_lat // tm

    xi, yi, ci = lax.axis_index("x"), lax.axis_index("y"), lax.axis_index("c")
    me, chip = 4 * xi + 2 * yi + ci, 2 * xi + yi

    (c_all,) = exchange("gather_c", [c], "all8")
    n_all = 8 * n_ex
    n_cond = -(-(n_all + 1) // 8) * 8
    cond = jnp.concatenate([c_all.reshape(n_all, d), c_ctx[None, :], jnp.zeros((n_cond - n_all - 1, d), F32)])
    mod_w = w_mod.shape[2]
    b_shard = lax.dynamic_slice(b_mod, (0, chip * mod_w), (1, mod_w))
    (mod_g,) = exchange("gather_mod", [mod_fwd(cond, w_mod[0], b_shard)], "chips")
    mod_full = mod_g.transpose(1, 0, 2).reshape(n_cond, N_CHIPS * mod_w)
    mod_mine = lax.dynamic_slice(mod_full, (me * n_ex, 0), (n_ex, 9 * d)).reshape(n_ex, 9, d)
    mod_ctx = mod_full[n_all].reshape(9, d)
    tabs = [jnp.concatenate([mod_mine[:, j], mod_ctx[j][None]])[:, None, :] for j in range(9)]
    lat = lambda t: t[:n_ex]

    bf = lambda w: w[0].astype(BF16)
    plan = _Plan()
    gather = lambda *ws: (lambda p: Rider(list(ws), "chips"))
    plan.on("ffn1_norm", "wg1", gather(bf(ffn1_gate)))
    plan.on("ffn1_gate", "wu1", gather(bf(ffn1_up)))
    plan.on("ffn1_up", "wd1", gather(bf(ffn1_down)))
    cut_a, cut_b = d * 5 // 8, d * 7 // 8
    plan.on("ffn1_down", "win_a", gather(bf(w_in)[:cut_a]))
    plan.on("ffn1_resid", "win_b", gather(bf(w_in)[cut_a:cut_b], ssm_conv_w[0], cconv_w[0]))
    xt = two_rows(x.reshape(n_lat, d), ctx.reshape(n_ctx_rows, d), lat_tiles)
    x1, saved1 = _ffn_fwd(plan, "ffn1", xt, n_tok, tm, seg_all, tabs[0], tabs[1], tabs[2], norm_ffn1,
                          lambda: plan.got["wg1"][0], lambda: plan.got["wu1"][0], lambda: plan.got["wd1"][0])
    (wg1,), (wu1,), (wd1,), (win_a,), (win_b, w5_g, w31_g) = (plan.got[k] for k in ("wg1", "wu1", "wd1", "win_a", "win_b"))
    (h2,), (win_c,) = rowwise("mix_norm", fn_norm_mod, [row(x1)], [tabs[3], tabs[4]], [norm_mix], [(n_tok, d, BF16)],
                              tm=tm, n_tiles=n_tok // tm, seg_fn=seg_all, rider=Rider([bf(w_in)[cut_b:]], "chips"))
    win_g = jnp.concatenate([win_a, win_b, win_c], axis=1)
    unshard_cols = lambda t: t.transpose(1, 0, 2).reshape(t.shape[1], N_CHIPS * t.shape[2])
    win = unshard_cols(win_g)
    o_x, o_dt, o_glu = ds, ds + xw, ds + xw + 2 * n_head
    w_z, w_xbc, w_dt = win[:, :ds], win[:, o_x:o_dt], win[:, o_dt:o_glu]
    w_ga, w_gb = win[:, o_glu:o_glu + d], win[:, o_glu + d:]
    w_dtp = jnp.concatenate([w_dt, jnp.zeros((d, LANES - 2 * n_head), BF16)], axis=1)
    w_cat = jnp.concatenate([w_z, w_ga, w_gb, w_xbc, w_dtp], axis=1)
    cbw = d // 2
    xbc_cb, dt_cb = 3 * d // cbw, (3 * d + xw) // LANES
    w5, w31 = unshard_cols(w5_g), unshard_cols(w31_g)
    pad_vec = lambda v: jnp.concatenate([v.reshape(1, -1), jnp.zeros((1, LANES - v.size), F32)], axis=1)
    dtb_f, dtb_b, alog_f, alog_b = map(pad_vec, (dt_bias_fwd, dt_bias_bwd, a_log_fwd, a_log_bwd))
    dsk_f, dsk_b = pad_vec(ssm_d), jnp.zeros((1, LANES), F32)

    proj, (wg2,) = matmul("mix_proj", [(h2, w_cat)], "nn", tm=min(tm, 256), rider=Rider([bf(ffn2_gate)], "chips"))
    def conv5(name, src, cb0, flip):
        out = None
        for part, seq, off in (("lat", seq_len, 0), ("ctx", ctx_len, n_lat // ctx_len)):
            out = tapsum_roll(f"{name}_{part}", src, cb0, w5, 0, seq_len=seq, n_seq=n_ex, row_blk_off=off, width=seq,
                              piece=seq, cb=cbw, ncb=xw // cbw, pad=w5.shape[0] // 2, flip=flip,
                              place=((n_tok, xw), off, 0, out))
        return out

    craw = conv5("xbc_conv", proj, xbc_cb, False)
    (xbc,) = rowwise("xbc_silu", fn_silu_bias, [row(craw)], [], [ssm_conv_b], [(n_tok, xw, F32)], tm=tm, n_tiles=n_tok // tm)
    ssd = dict(n_ex=n_ex, seq_len=seq_len, ctx_len=ctx_len, ds=ds)
    (y_f, hs_f), (wu2,) = ssd_fwd("ssd_fwd_f", xbc, proj, dt_cb, dtb_f, alog_f, dsk_f, rev=False,
                                  rider=Rider([bf(ffn2_up)], "chips"), **ssd)
    (y_b, hs_b), (wout_g, wd2) = ssd_fwd("ssd_fwd_b", xbc, proj, dt_cb, dtb_b, alog_b, dsk_b, rev=True,
                                         rider=Rider([bf(w_out), bf(ffn2_down)], "chips"), add=y_f, **ssd)
    wout = wout_g.reshape(2 * d, d)
    wo_y, wo_u = wout[:ds], wout[ds:]
    fn_gate = make_fn_gate_groupnorm(ds)
    (yn,) = rowwise("ssd_gate", fn_gate, [row(y_b), row(proj, d, 0)], [], [ssm_norm_w], [(n_lat, ds, BF16)],
                    tm=tm, n_tiles=lat_tiles)
    (u0,) = rowwise("glu", fn_glu, [row(proj, d, 1), row(proj, d, 2)], [], [], [(n_lat, d, F32)], tm=tm, n_tiles=lat_tiles)
    cb31 = max(LANES, d // 4)
    ncb31 = (d // 2) // cb31
    pad31 = w31.shape[0] // 2
    piece31 = min(seq_len, 4 * GRID_W)
    v_w = tapsum_roll("cconv_cols", u0, 0, w31, 0, seq_len=seq_len, n_seq=n_ex, row_blk_off=0, width=GRID_W,
                      piece=piece31, cb=cb31, ncb=ncb31, pad=pad31, flip=False)
    v_h = tapsum_rows("cconv_rows", u0, ncb31, w31, ncb31, seq_len=seq_len, n_seq=n_ex, cb=cb31, ncb=ncb31, pad=pad31, flip=False)
    (un,) = rowwise("cconv_ln", fn_ln_silu, [row(v_w), row(v_h)], [], [cconv_b, cconv_ln_w, cconv_ln_b], [(n_lat, d, BF16)],
                    tm=tm, n_tiles=lat_tiles)
    mix = matmul("mix_out", [(yn, wo_y), (un, wo_u)], "nn", tm=tm)
    seg_lat, first_lat = _segmenter(tm, seq_len, n_lat)
    (x2,) = rowwise("mix_resid", make_fn_resid(1.0), [row(x1), row(mix)], [lat(tabs[5])], [], [(n_lat, d, F32)],
                    tm=tm, n_tiles=lat_tiles, seg_fn=seg_lat)
    x3, saved2 = _ffn_fwd(plan, "ffn2", row(x2), n_lat, tm, seg_lat, lat(tabs[6]), lat(tabs[7]), lat(tabs[8]), norm_ffn2, wg2, wu2, wd2,
                          fuse_gate_up=True)
    d_x3, d_final, loss_vec = final_loss(x3, loss_target.reshape(n_lat, d), final_norm.reshape(1, d), tm=tm)

    shard_cols = lambda t: t.reshape(t.shape[0], N_CHIPS, -1).transpose(1, 0, 2)

    def pieces(t):
        t = jnp.pad(t, ((0, 0), (0, -t.shape[1] % 32), (0, 0)))
        return t.reshape(2 * N_CHIPS, t.shape[1] // 2, t.shape[2]).astype(BF16)

    scatter = lambda *ts: Rider([pieces(t) for t in ts], "all8", scatter=True)
    halves = lambda names, landed: Rider([sum_slots(f"sum_{nm}", r, BF16) for nm, r in zip(names, landed)], "sibling")
    swapped = {}
    plan.on("ffn2_up_dx", "sc_ffn2_down", lambda p: scatter(p.got["ffn2_d_wd"]))
    plan.on("ffn2_up_dw", "sc_ffn2_gate", lambda p: scatter(p.got["ffn2_d_wg"]))
    d_x2, (d_s6, d_s7, d_g8), d_nffn2 = _ffn_bwd(
        plan, "ffn2", d_x3, saved2, row(x2), n_lat, tm, seg_lat, first_lat, lat(tabs[6]), lat(tabs[7]), lat(tabs[8]), norm_ffn2,
        wg2, wu2, wd2, n_lat, None)
    d_mix, d_g5 = rowwise_bwd("mix_resid_bwd", make_fn_resid(1.0), [row(x1), row(mix)], [lat(tabs[5])], [], [[row(d_x2)]],
                              [None, (n_lat, BF16, None)], tm=tm, n_tiles=lat_tiles, seg_fn=seg_lat, first_fn=first_lat)
    d_yn = matmul("mix_out_dy", [(d_mix, wo_y)], "nt", tm=tm)
    d_un = matmul("mix_out_du", [(d_mix, wo_u)], "nt", tm=tm)
    d_wout = jnp.concatenate([matmul("mix_out_dwy", [(yn, d_mix)], "tn", out_dtype=BF16, tm=tm),
                              matmul("mix_out_dwu", [(un, d_mix)], "tn", out_dtype=BF16, tm=tm)])
    d_vw, d_vh, d_cb, d_lnw, d_lnb = rowwise_bwd(
        "cconv_ln_bwd", fn_ln_silu, [row(v_w), row(v_h)], [], [cconv_b, cconv_ln_w, cconv_ln_b], [[row(d_un)]],
        [(n_lat, F32, None)] * 2, tm=tm, n_tiles=lat_tiles)
    d_u0 = tapsum_roll("cconv_cols_dx", d_vw, 0, w31, 0, seq_len=seq_len, n_seq=n_ex, row_blk_off=0, width=GRID_W,
                       piece=piece31, cb=cb31, ncb=ncb31, pad=pad31, flip=True, place=((n_lat, d), 0, 0, None))
    d_u0 = tapsum_rows("cconv_rows_dx", d_vh, 0, w31, ncb31, seq_len=seq_len, n_seq=n_ex, cb=cb31, ncb=ncb31, pad=pad31,
                       flip=True, place=((n_lat, d), 0, ncb31, d_u0))
    d_w31 = jnp.concatenate([
        tapgrad_roll("cconv_cols_dw", d_vw, 0, 0, u0, 0, 0, n_tap=w31.shape[0], seq_len=seq_len, n_seq=n_ex, width=GRID_W,
                     piece=piece31, cb=cb31, ncb=ncb31, pad=pad31),
        tapgrad_rows("cconv_rows_dw", d_vh, 0, u0, ncb31, n_tap=w31.shape[0], seq_len=seq_len, n_seq=n_ex, cb=cb31,
                     ncb=ncb31, pad=pad31)], axis=1)
    d_ga, d_gb = rowwise_bwd("glu_bwd", fn_glu, [row(proj, d, 1), row(proj, d, 2)], [], [], [[row(d_u0)]],
                             [(n_lat, BF16, None)] * 2, tm=tm, n_tiles=lat_tiles)
    d_ysum, d_z, d_ssmnw = rowwise_bwd(
        "ssd_gate_bwd", fn_gate, [row(y_b), row(proj, d, 0)], [], [ssm_norm_w], [[row(d_yn)]],
        [(n_lat, F32, None), (n_lat, BF16, None)], tm=tm, n_tiles=lat_tiles)
    (dxbc_f, ddt_f, dalog_f, ddtb_f, ddsk), landed = ssd_bwd(
        "ssd_bwd_f", xbc, proj, dt_cb, hs_f, d_ysum, dtb_f, alog_f, dsk_f, rev=False,
        rider=scatter(plan.got["ffn2_d_wu"], d_wout.reshape(N_CHIPS, -1, d)), **ssd)
    (dxbc_b, ddt_b, dalog_b, ddtb_b, _), both = ssd_bwd(
        "ssd_bwd_b", xbc, proj, dt_cb, hs_b, d_ysum, dtb_b, alog_b, dsk_b, rev=True,
        rider=halves(["ffn2_down", "ffn2_gate"], plan.got["sc_ffn2_down"] + plan.got["sc_ffn2_gate"]), add=dxbc_f, **ssd)
    swapped.update(zip(["ffn2_down", "ffn2_gate"], both))
    (d_craw, d_conv_b), both = rowwise_bwd(
        "xbc_silu_bwd", fn_silu_bias, [row(craw)], [], [ssm_conv_b], [[row(dxbc_b)]],
        [(n_tok, F32, None)], tm=tm, n_tiles=n_tok // tm, rider=halves(["ffn2_up", "w_out"], landed))
    swapped.update(zip(["ffn2_up", "w_out"], both))
    d_pxbc = conv5("xbc_conv_dx", d_craw, 0, True)
    g5 = lambda name, seq, off: tapgrad_roll(name, d_craw, 0, off, proj, xbc_cb, off, n_tap=w5.shape[0], seq_len=seq,
                                             n_seq=n_ex, width=seq, piece=seq, cb=cbw, ncb=xw // cbw, pad=w5.shape[0] // 2)
    d_w5 = g5("xbc_conv_lat_dw", seq_len, 0) + g5("xbc_conv_ctx_dw", ctx_len, n_lat // ctx_len)
    lat_pairs = [(d_z, w_z), (d_ga, w_ga), (d_gb, w_gb), (d_pxbc, w_xbc), (ddt_f, w_dtp), (ddt_b, w_dtp)]
    d_h2 = matmul("mix_proj_dx_lat", lat_pairs, "nt", rows=n_lat, tm=min(tm, 256), place=(n_tok, 0, None))
    d_h2 = matmul("mix_proj_dx_ctx", lat_pairs[3:], "nt", rows=n_ctx_rows, row_off=n_lat, tm=min(tm, 256),
                  place=(n_tok, n_lat, d_h2))
    d_wz = matmul("mix_proj_dwz", [(d_z, h2)], "tn", out_dtype=BF16, rows=n_lat, tm=tm)
    d_wga = matmul("mix_proj_dwa", [(d_ga, h2)], "tn", out_dtype=BF16, rows=n_lat, tm=tm)
    d_wgb = matmul("mix_proj_dwb", [(d_gb, h2)], "tn", out_dtype=BF16, rows=n_lat, tm=tm)
    d_wxbc = matmul("mix_proj_dwx", [(d_pxbc, h2)], "tn", out_dtype=BF16, tm=tm)
    d_wdt = matmul("mix_proj_dwt", [(ddt_f, h2), (ddt_b, h2)], "tn", out_dtype=BF16, tm=tm)
    d_win_t = jnp.concatenate([d_wz, d_wxbc, d_wdt[:2 * n_head], d_wga, d_wgb]).reshape(N_CHIPS, -1, d)
    d_x1, d_s3, d_s4, d_nmix = rowwise_bwd(
        "mix_norm_bwd", fn_norm_mod, [row(x1)], [tabs[3], tabs[4]], [norm_mix], [[row(d_h2)]], [(n_tok, F32, None)],
        tm=tm, n_tiles=n_tok // tm, seg_fn=seg_all, first_fn=first_all, adds={0: (row(d_x2), lat_tiles)})
    mix_names = ["w_in", "ssm_conv_w", "cconv_w"]
    plan.on("ffn1_down_dx", "sc_conv", lambda p: scatter(shard_cols(d_w5), shard_cols(d_w31)))
    plan.on("ffn1_up_dx", "sc_win", lambda p: scatter(d_win_t))
    plan.on("ffn1_gate_dw", "sc_ffn1_down", lambda p: scatter(p.got["ffn1_d_wd"]))
    plan.on("ffn1_up_dw", "sc_ffn1_gate", lambda p: scatter(p.got["ffn1_d_wg"]))
    plan.on("ffn1_up_dw", "sw_mix", lambda p: halves(mix_names, p.got["sc_win"] + p.got["sc_conv"]))
    plan.on("ffn1_norm_bwd", "sc_ffn1_up", lambda p: scatter(p.got["ffn1_d_wu"]))
    plan.on("ffn1_up_dw", "sw_ffn1_down", lambda p: halves(["ffn1_down"], p.got["sc_ffn1_down"]))
    d_xt, (d_s0, d_s1, d_g2), d_nffn1 = _ffn_bwd(
        plan, "ffn1", d_x1, saved1, xt, n_tok, tm, seg_all, first_all, tabs[0], tabs[1], tabs[2], norm_ffn1, wg1, wu1, wd1,
        n_lat, lat_tiles)
    swapped.update(zip(mix_names + ["ffn1_down"], plan.got["sw_mix"] + plan.got["sw_ffn1_down"]))
    last_names = ["ffn1_gate", "ffn1_up"]
    last = halves(last_names, plan.got["sc_ffn1_gate"] + plan.got["sc_ffn1_up"])
    grad_x = d_xt.reshape(n_ex, seq_len, d)

    with_ctx0 = lambda t: jnp.concatenate([t, jnp.zeros((1, 1, d), F32)])
    d_tabs = [d_s0, d_s1, d_g2, d_s3, d_s4, with_ctx0(d_g5), with_ctx0(d_s6), with_ctx0(d_s7), with_ctx0(d_g8)]
    d_mod_rows = jnp.concatenate([t[:, 0, :] for t in d_tabs], axis=1)
    n_pad_rows = -(-(n_ex + 1) // 8) * 8
    d_mod_rows = jnp.concatenate([d_mod_rows, jnp.zeros((n_pad_rows - n_ex - 1, 9 * d), F32)])
    small = [("loss", loss_vec), ("norm_ffn1", d_nffn1), ("norm_mix", d_nmix), ("ssm_conv_b", d_conv_b),
             ("dt_bias_fwd", ddtb_f[:, :n_head]), ("dt_bias_bwd", ddtb_b[:, :n_head]), ("a_log_fwd", dalog_f[:, :n_head]),
             ("a_log_bwd", dalog_b[:, :n_head]), ("ssm_d", ddsk[:, :n_head]), ("ssm_norm_w", d_ssmnw), ("cconv_b", d_cb),
             ("cconv_ln_w", d_lnw), ("cconv_ln_b", d_lnb), ("norm_ffn2", d_nffn2), ("final_norm", d_final)]
    n_small = sum(v.size for _, v in small)
    n_pack = -(-n_small // (8 * LANES)) * (8 * LANES)
    pack = jnp.concatenate([v.reshape(-1) for _, v in small] + [jnp.zeros((n_pack - n_small,), F32)]).reshape(-1, LANES)
    (pack_all, d_mod_all), both = exchange_many("gather_small_swap_last", [Rider([pack, d_mod_rows], "all8"), last])
    swapped.update(zip(last_names, both))
    pack_sum = sum_slots("small_sum", pack_all)
    loss = loss_total(pack_sum.reshape(1, n_pack), d).reshape(())
    flat_sum = pack_sum.reshape(-1)
    small_grads, pos = {}, 0
    for nm, v in small:
        small_grads[nm] = flat_sum[pos:pos + v.size]
        pos += v.size
    d_mod_all = d_mod_all.reshape(8 * n_pad_rows, 9 * d)
    cond_rows = [jnp.concatenate([cond[j * n_ex:(j + 1) * n_ex], c_ctx[None, :],
                                  jnp.zeros((n_pad_rows - n_ex - 1, d), F32)]) for j in range(8)]
    cond_bwd = jnp.concatenate(cond_rows)
    d_mod_shard = lax.dynamic_slice(d_mod_all, (0, chip * mod_w), (8 * n_pad_rows, mod_w))
    g_wmod, g_bmod, q_part = mod_bwd(cond_bwd, d_mod_shard, d_mod_all, w_mod[0],
                                     tuple(j * n_pad_rows + n_ex for j in range(8)))
    (q_all,) = exchange("gather_cctx", [q_part], "all8")
    g_cctx = cctx_grad(q_all, c_ctx.reshape(1, d))
    small_grads["c_ctx"], small_grads["b_mod"] = g_cctx.reshape(-1), g_bmod.reshape(-1)

    transposed = {"ffn1_gate", "ffn1_up", "ffn2_gate", "ffn2_up", "w_in"}
    results = {}
    for nm, both in swapped.items():
        flip = (lambda t: jnp.swapaxes(t, 1, 2)) if nm in transposed else (lambda t: t)
        shape = flip(weights[nm]).shape
        two_d = lambda t: flip(t).reshape(shape[-2], shape[-1])
        g_full = both.reshape(1, -1, shape[-1])[:, :shape[-2]]
        results[nm] = [flip(r.reshape(shape)) for r in
                       adamw(f"adamw_{nm}", two_d(weights[nm]), g_full, two_d(mom1[nm]), two_d(mom2[nm]))]
    results["w_mod"] = [r.reshape(w_mod.shape) for r in adamw("adamw_w_mod", w_mod[0], g_wmod[None], m_w_mod[0], v_w_mod[0])]
    small_names = [nm for nm in order if nm not in results]
    n_sm = sum(weights[nm].size for nm in small_names)
    n_smp = -(-n_sm // (8 * LANES)) * (8 * LANES)
    packed = lambda src: jnp.concatenate([src[nm].reshape(-1) for nm in small_names] + [jnp.zeros((n_smp - n_sm,), F32)]).reshape(-1, LANES)
    sm_out = adamw("adamw_small", packed(weights), packed(small_grads)[None], packed(mom1), packed(mom2))
    pos = 0
    for nm in small_names:
        size = weights[nm].size
        results[nm] = [r.reshape(-1)[pos:pos + size].reshape(weights[nm].shape) for r in sm_out]
        pos += size
    return (loss, grad_x, *[results[nm][0] for nm in order], *[results[nm][1] for nm in order],
            *[results[nm][2] for nm in order], *[results[nm][3] for nm in order])
```

```python
import functools
import math

import jax
import jax.numpy as jnp
from jax import lax
from jax.experimental import pallas as pl
from jax.experimental.pallas import tpu as pltpu

F32 = jnp.float32
BF16 = jnp.bfloat16
HI = lax.Precision.HIGHEST
MESH = pl.DeviceIdType.MESH

EPS = 1e-6
GRID_W = 64
HEAD_DIM = 64
N_STATE = 128
CHUNK = 128
LANES = 128
N_CHIPS = 4
ADAM_LR, ADAM_B1, ADAM_B2, ADAM_EPS, ADAM_WD, ADAM_STEP = 0.001, 0.9, 0.999, 1e-08, 0.01, 10
VMEM_CAP = 56 * 1024 * 1024
STREAM_TILE_BYTES = 3 << 19


def _params(vmem_bytes=None, n_axes=1):
    kw = dict(dimension_semantics=("arbitrary",) * n_axes)
    if vmem_bytes is not None:
        kw["vmem_limit_bytes"] = int(min(VMEM_CAP, max(32 * 1024 * 1024, vmem_bytes)))
    return pltpu.CompilerParams(**kw)


def _big(shape, dtype):
    return pltpu.HBM(tuple(shape), dtype)


def _in_hbm(args):
    return [pltpu.with_memory_space_constraint(a, pltpu.HBM) if a.size * a.dtype.itemsize >= (1 << 20) else a for a in args]


def _nbytes(shape, dtype):
    return math.prod(shape) * jnp.dtype(dtype).itemsize


def _row_tile(rows, width, cap_bytes=1 << 20, mult=8):
    best = None
    for t in range(mult, rows + 1, mult):
        if rows % t == 0 and t * width * 4 <= cap_bytes:
            best = t
    return best if best is not None else rows


_MODES = {"all8": (8, (1, 2, 3, 4, 5, 6, 7), 0), "chips": (4, (2, 4, 6), 1), "sibling": (2, (1,), 0)}


class Rider:
    def __init__(self, arrs, mode, scatter=False):
        self.arrs, self.scatter = list(arrs), scatter
        self.nslot, self.deltas, self.shift = _MODES[mode]
        self.n = len(self.arrs)
        self.out_shape = [jax.ShapeDtypeStruct((self.nslot,) + (a.shape[1:] if scatter else a.shape), a.dtype)
                          for a in self.arrs]
        any_spec = pl.BlockSpec(memory_space=pl.ANY)
        self.in_specs = [any_spec] * self.n
        self.out_specs = [any_spec] * self.n
        n_peer = len(self.deltas)
        self.scratch = [pltpu.SemaphoreType.DMA((self.n, n_peer)), pltpu.SemaphoreType.DMA((self.n, n_peer)),
                        pltpu.SemaphoreType.DMA((self.n,))]

    def _copies(self, ins, outs, sems, arrivals):
        send_sems, recv_sems, local_sems = sems
        x, y, c = lax.axis_index("x"), lax.axis_index("y"), lax.axis_index("c")
        me = 4 * x + 2 * y + c
        slot_of = lambda dev: (dev >> self.shift) & (self.nslot - 1)
        src = lambda a, slot: ins[a].at[slot] if self.scatter else ins[a]
        flip = lambda v, bit: 1 - v if bit else v

        def remote(a, k, d, from_slot, to_slot):
            return pltpu.make_async_remote_copy(
                src_ref=src(a, from_slot), dst_ref=outs[a].at[to_slot], send_sem=send_sems.at[a, k],
                recv_sem=recv_sems.at[a, k], device_id=(flip(x, (d >> 2) & 1), flip(y, (d >> 1) & 1), flip(c, d & 1)),
                device_id_type=MESH)

        mine = slot_of(me)
        local = [pltpu.make_async_copy(src(a, mine), outs[a].at[mine], local_sems.at[a]) for a in range(self.n)]
        sends = [remote(a, k, d, slot_of(me ^ d), mine) for k, d in enumerate(self.deltas) for a in range(self.n)]
        if not arrivals:
            return local, sends
        return local, sends, [remote(a, k, d, mine, slot_of(me ^ d)) for k, d in enumerate(self.deltas) for a in range(self.n)]

    def start(self, ins, outs, sems):
        local, sends = self._copies(ins, outs, sems, arrivals=False)
        for cp in local + sends:
            cp.start()

    def wait(self, ins, outs, sems):
        local, sends, recvs = self._copies(ins, outs, sems, arrivals=True)
        for cp in recvs:
            cp.wait_recv()
        for cp in sends:
            cp.wait_send()
        for cp in local:
            cp.wait()


class Riders:
    def __init__(self, riders):
        self.riders = list(riders)
        self.n = sum(r.n for r in self.riders)
        cat = lambda attr: [v for r in self.riders for v in getattr(r, attr)]
        self.arrs, self.out_shape, self.in_specs = cat("arrs"), cat("out_shape"), cat("in_specs")
        self.out_specs, self.scratch = cat("out_specs"), cat("scratch")

    def _each(self, method, ins, outs, sems):
        i = s = 0
        for r in self.riders:
            getattr(r, method)(ins[i:i + r.n], outs[i:i + r.n], sems[s:s + len(r.scratch)])
            i, s = i + r.n, s + len(r.scratch)

    def start(self, ins, outs, sems):
        self._each("start", ins, outs, sems)

    def wait(self, ins, outs, sems):
        self._each("wait", ins, outs, sems)


class _Hosted:
    def __init__(self, rider, n_in, n_out, n_scratch, grid):
        self.rider, self.n_in, self.n_out, self.n_scratch, self.grid = rider, n_in, n_out, n_scratch, grid
        self.n = rider.n if rider else 0

    def split(self, refs):
        a, b = self.n_in, self.n_in + self.n
        c, e = b + self.n_out, b + self.n_out + self.n
        self._r = (refs[a:b], refs[c:e], refs[e + self.n_scratch:])
        if self.rider:
            ids = [pl.program_id(ax) for ax in range(len(self.grid))]
            first = functools.reduce(jnp.logical_and, [i == 0 for i in ids]) if ids else True
            pl.when(first)(lambda: self.rider.start(*self._r))
        return refs[:a], refs[b:c], refs[e:e + self.n_scratch]

    def finish(self):
        if self.rider:
            ids = [pl.program_id(ax) for ax in range(len(self.grid))]
            last = functools.reduce(jnp.logical_and, [i == n - 1 for i, n in zip(ids, self.grid)]) if ids else True
            pl.when(last)(lambda: self.rider.wait(*self._r))

    def call_args(self, in_specs, out_shape, out_specs, scratch, args):
        r = self.rider
        if not r:
            return list(in_specs), tuple(out_shape), tuple(out_specs), list(scratch), list(args)
        return (list(in_specs) + r.in_specs, tuple(out_shape) + tuple(r.out_shape), tuple(out_specs) + tuple(r.out_specs),
                list(scratch) + r.scratch, list(args) + r.arrs)

    def results(self, res, unwrap=True):
        res = list(res) if isinstance(res, (tuple, list)) else [res]
        host = res[:self.n_out]
        host = host[0] if (self.n_out == 1 and unwrap) else tuple(host)
        return (host, res[self.n_out:]) if self.rider else host


def exchange_many(name, riders):
    both = Riders(riders)

    def body(*refs):
        ins, outs, sems = refs[:both.n], refs[both.n:2 * both.n], refs[2 * both.n:]
        both.start(ins, outs, sems)
        both.wait(ins, outs, sems)

    res = list(pl.pallas_call(
        body, name=name, out_shape=tuple(both.out_shape), in_specs=both.in_specs, out_specs=tuple(both.out_specs),
        scratch_shapes=both.scratch,
    )(*both.arrs))
    split = []
    for r in riders:
        split.append(res[:r.n])
        res = res[r.n:]
    return split


def exchange(name, arrs, mode, scatter=False):
    rider = Rider(arrs, mode, scatter)

    def body(*refs):
        ins, outs, sems = refs[:rider.n], refs[rider.n:2 * rider.n], refs[2 * rider.n:]
        rider.start(ins, outs, sems)
        rider.wait(ins, outs, sems)

    return pl.pallas_call(
        body, name=name, out_shape=tuple(rider.out_shape), in_specs=rider.in_specs, out_specs=tuple(rider.out_specs),
        scratch_shapes=rider.scratch,
    )(*arrs)


_DIMS = {"nn": (((1,), (0,)), ((), ())), "nt": (((1,), (1,)), ((), ())), "tn": (((0,), (0,)), ((), ()))}


def matmul(name, pairs, kind, *, a_ch=False, b_ch=False, out_ch=False, out_dtype=F32, rows=None, row_off=0, tm=512,
           rider=None, post=None, fold=False, place=None):
    a0, b0 = pairs[0]
    n_chunk = a0.shape[0] if a_ch else (b0.shape[0] if b_ch else 1)
    total_rows = a0.shape[-2]
    rows = total_rows - row_off if rows is None else rows
    tm = min(tm, rows)
    assert rows % tm == 0 and row_off % tm == 0, (name, rows, tm, row_off)
    n_rt, off = rows // tm, row_off // tm
    dims = _DIMS[kind]
    n_pair = len(pairs)

    if kind == "tn":
        grid, red_axis, n_red = (n_chunk, n_rt), 1, n_rt
        a_idx = (lambda k, i: (k, i + off, 0)) if a_ch else (lambda k, i: (i + off, 0))
        b_idx = (lambda k, i: (k, i + off, 0)) if b_ch else (lambda k, i: (i + off, 0))
        a_blk = lambda a: ((None, tm, a.shape[-1]) if a_ch else (tm, a.shape[-1]))
        b_blk = lambda b: ((None, tm, b.shape[-1]) if b_ch else (tm, b.shape[-1]))
        o2 = (a0.shape[-1], b0.shape[-1])
        out_shape = ((n_chunk,) + o2) if out_ch else o2
        out_spec = pl.BlockSpec((None,) + o2, lambda k, i: (k, 0, 0)) if out_ch else pl.BlockSpec(o2, lambda k, i: (0, 0))
        acc_shape = o2
    else:
        n_out = b0.shape[-1] if kind == "nn" else b0.shape[-2]
        b2 = b0.shape[-2:]
        if a_ch and b_ch and not out_ch and fold:
            grid, red_axis, n_red = (n_rt,), None, 1
            a_idx, b_idx = (lambda i: (0, i + off, 0)), (lambda i: (0, 0, 0))
            a_blk = lambda a: (n_chunk, tm, a.shape[-1])
            b_blk = lambda b: tuple(b.shape)
            out_shape, out_spec = (rows, n_out), pl.BlockSpec((tm, n_out), lambda i: (i, 0))
        elif a_ch and b_ch and not out_ch:
            grid, red_axis, n_red = (n_rt, n_chunk), 1, n_chunk
            a_idx, b_idx = (lambda i, k: (k, i + off, 0)), (lambda i, k: (k, 0, 0))
            a_blk = lambda a: (None, tm, a.shape[-1])
            b_blk = lambda b: (None,) + tuple(b.shape[-2:])
            out_shape, out_spec = (rows, n_out), pl.BlockSpec((tm, n_out), lambda i, k: (i, 0))
        elif out_ch and fold:
            assert b_ch and not a_ch and all(a is a0 for a, _ in pairs)
            grid, red_axis, n_red = (n_rt,), None, 1
            a_idx, b_idx = (lambda i: (i + off, 0)), (lambda i: (0, 0, 0))
            a_blk = lambda a: (tm, a.shape[-1])
            b_blk = lambda b: tuple(b.shape)
            out_shape, out_spec = (n_chunk, rows, n_out), pl.BlockSpec((n_chunk, tm, n_out), lambda i: (0, i, 0))
        elif out_ch:
            assert b_ch and not a_ch
            grid, red_axis, n_red = (n_chunk, n_rt), None, 1
            a_idx, b_idx = (lambda k, i: (i + off, 0)), (lambda k, i: (k, 0, 0))
            a_blk = lambda a: (tm, a.shape[-1])
            b_blk = lambda b: (None,) + tuple(b.shape[-2:])
            out_shape, out_spec = (n_chunk, rows, n_out), pl.BlockSpec((None, tm, n_out), lambda k, i: (k, i, 0))
        else:
            assert not (a_ch or b_ch)
            grid, red_axis, n_red = (n_rt,), None, 1
            a_idx, b_idx = (lambda i: (i + off, 0)), (lambda i: (0, 0))
            a_blk = lambda a: (tm, a.shape[-1])
            b_blk = lambda b: tuple(b.shape)
            out_shape, out_spec = (rows, n_out), pl.BlockSpec((tm, n_out), lambda i: (i, 0))
            if place is not None:
                out_shape, o_off = (place[0], n_out), place[1] // tm
                out_spec = pl.BlockSpec((tm, n_out), lambda i: (i + o_off, 0))
        acc_shape = (tm, n_out)

    into = [] if place is None or place[2] is None else [place[2]]
    post_ins, post_fn, out_dtypes = ([], None, [out_dtype]) if post is None else post
    hosted = _Hosted(rider, 2 * n_pair + len(post_ins) + len(into), len(out_dtypes), int(n_red > 1), grid)

    def body(*refs):
        ins, outs, scr = hosted.split(refs)

        def compute():
            acc = None
            for p in range(n_pair):
                for k in ([None] if not fold else range(n_chunk)):
                    pick = (lambda r: r[...]) if k is None else (lambda r: r[k])
                    d = lax.dot_general(pick(ins[2 * p]).astype(BF16), pick(ins[2 * p + 1]).astype(BF16), dims,
                                        preferred_element_type=F32)
                    acc = d if acc is None else acc + d
            return acc

        def emit(acc):
            vals = (acc,) if post_fn is None else post_fn(
                acc, *[r[...].astype(F32) for r in ins[2 * n_pair:2 * n_pair + len(post_ins)]])
            for o_ref, v in zip(outs, vals):
                o_ref[...] = v.astype(o_ref.dtype)

        if out_ch and fold:
            a_tile = ins[0][...].astype(BF16)
            for k in range(n_chunk):
                accs = [lax.dot_general(a_tile, ins[2 * p + 1][k].astype(BF16), dims, preferred_element_type=F32)
                        for p in range(n_pair)]
                tiles = [r[k].astype(F32) for r in ins[2 * n_pair:2 * n_pair + len(post_ins)]]
                vals = tuple(accs) if post_fn is None else post_fn(*accs, *tiles)
                for o_ref, v in zip(outs, vals):
                    o_ref[k] = v.astype(o_ref.dtype)
        elif n_red == 1:
            emit(compute())
        else:
            acc_ref = scr[0]
            r = pl.program_id(red_axis)

            @pl.when(r == 0)
            def _():
                acc_ref[...] = jnp.zeros_like(acc_ref)

            acc_ref[...] += compute()

            @pl.when(r == n_red - 1)
            def _():
                emit(acc_ref[...])
        hosted.finish()

    in_specs, args, vmem = [], [], 0
    for a, b in pairs:
        in_specs += [pl.BlockSpec(a_blk(a), a_idx), pl.BlockSpec(b_blk(b), b_idx)]
        args += [a, b]
        vmem += 2 * (_nbytes([s for s in a_blk(a) if s], a.dtype) + _nbytes([s for s in b_blk(b) if s], b.dtype))
    in_specs += [out_spec] * len(post_ins)
    args += list(post_ins)
    aliases = {len(args): 0} if into else {}
    in_specs += [pl.BlockSpec(memory_space=pl.ANY)] * len(into)
    args += into
    tiles_per_step = n_chunk if (out_ch and fold) else 1
    vmem += (3 + 2 * n_pair + tiles_per_step * (len(post_ins) + len(out_dtypes))) * _nbytes(acc_shape, F32)
    scratch = [pltpu.VMEM(acc_shape, F32)] if n_red > 1 else []
    in_specs, out_shapes, out_specs, scratch, args = hosted.call_args(
        in_specs, [_big(out_shape, dt) for dt in out_dtypes], [out_spec] * len(out_dtypes), scratch, args)
    return hosted.results(pl.pallas_call(
        body, name=name, out_shape=out_shapes, grid=grid, in_specs=in_specs, out_specs=out_specs,
        input_output_aliases=aliases, scratch_shapes=scratch, compiler_params=_params(vmem + (8 << 20), len(grid)),
    )(*_in_hbm(args)))


def row(arr, width=None, cb=0, roff=0):
    return (arr, arr.shape[-1] if width is None else width, cb, roff)


def two_rows(first, second, limit):
    return (first, first.shape[-1], 0, 0, (second, limit))


def _row_inputs(rows, tm):
    specs, arrs, slots = [], [], []
    for d in rows:
        second, limit = d[4] if len(d) > 4 else (None, None)
        slots.append((len(arrs), limit))
        specs.append(_row_spec(d[:4], tm, limit))
        arrs.append(d[0])
        if second is not None:
            specs.append(pl.BlockSpec((tm, d[1]), lambda i, limit=limit: (jnp.maximum(i - limit, 0), 0)))
            arrs.append(second)

    def read(refs, i):
        vals = []
        for at, limit in slots:
            v = refs[at][...].astype(F32)
            vals.append(v if limit is None else jnp.where(i < limit, v, refs[at + 1][...].astype(F32)))
        return vals

    return specs, arrs, read


def _row_spec(desc, tm, limit=None):
    _, width, cb, roff = desc[:4]
    if limit is None:
        return pl.BlockSpec((tm, width), lambda i: (i + roff, cb))
    return pl.BlockSpec((tm, width), lambda i: (jnp.minimum(i, limit - 1) + roff, cb))


def _segmenter(tm, seq_len, n_lat):
    seg = lambda i: jnp.where(i * tm < n_lat, (i * tm) // seq_len, n_lat // seq_len)
    first = lambda i: jnp.where(i * tm < n_lat, (i * tm) % seq_len == 0, i * tm == n_lat)
    return seg, first


def rowwise(name, fn, rows, segs, params, outs, *, tm, n_tiles, seg_fn=None, rider=None):
    row_specs, row_arrs, read_rows = _row_inputs(rows, tm)
    n_r, n_s, n_p = len(row_arrs), len(segs), len(params)
    hosted = _Hosted(rider, n_r + n_s + n_p, len(outs), 0, (n_tiles,))

    def body(*refs):
        ins, out_refs, _ = hosted.split(refs)
        vals = read_rows(ins[:n_r], pl.program_id(0)) + [r[...] for r in ins[n_r:]]
        res = fn(*vals)
        for o_ref, v in zip(out_refs, res):
            o_ref[...] = v.astype(o_ref.dtype)
        hosted.finish()

    in_specs = list(row_specs)
    in_specs += [pl.BlockSpec((None, 1, s.shape[-1]), lambda i: (seg_fn(i), 0, 0)) for s in segs]
    in_specs += [pl.BlockSpec(p.shape, lambda i: (0, 0)) for p in params]
    vmem = sum(2 * tm * d[1] * 4 for d in rows) + sum(3 * tm * w * 4 for _, w, _ in outs) + sum(2 * p.size * 4 for p in params)
    in_specs, out_shapes, out_specs, scratch, args = hosted.call_args(
        in_specs, [_big((r, w), dt) for r, w, dt in outs],
        [pl.BlockSpec((tm, w), lambda i: (i, 0)) for _, w, _ in outs], [], row_arrs + list(segs) + list(params))
    return hosted.results(pl.pallas_call(
        body, name=name, grid=(n_tiles,), in_specs=in_specs, out_shape=out_shapes, out_specs=out_specs,
        scratch_shapes=scratch, compiler_params=_params(2 * vmem + (8 << 20)),
    )(*_in_hbm(args)), unwrap=False)


def rowwise_bwd(name, fn, rows, segs, params, cts, row_grads, *, tm, n_tiles, seg_fn=None, first_fn=None, adds=None,
                rider=None):
    adds = adds or {}
    need = [k for k, v in enumerate(row_grads) if v is not None]
    row_specs, row_arrs, read_rows = _row_inputs(rows, tm)
    n_r, n_s, n_p = len(row_arrs), len(segs), len(params)
    n_ct = sum(len(lst) for lst in cts)
    add_keys = sorted(adds)
    hosted = _Hosted(rider, n_r + n_s + n_p + n_ct + len(add_keys), len(need) + n_s + n_p, 0, (n_tiles,))

    def body(*refs):
        host_in, host_out, _ = hosted.split(refs)
        it = iter(list(host_in) + list(host_out))
        row_refs = [next(it) for _ in range(n_r)]
        seg_refs = [next(it) for _ in range(n_s)]
        par_refs = [next(it) for _ in range(n_p)]
        ct_refs = [[next(it) for _ in lst] for lst in cts]
        add_refs = {k: next(it) for k in add_keys}
        rg_refs = {k: next(it) for k in need}
        sg_refs = [next(it) for _ in range(n_s)]
        pg_refs = [next(it) for _ in range(n_p)]
        i = pl.program_id(0)
        rv = read_rows(row_refs, i)
        sv = [r[...] for r in seg_refs]
        pv = [r[...] for r in par_refs]

        def f(*args):
            rr = list(rv)
            for j, k in enumerate(need):
                rr[k] = args[j]
            return fn(*rr, *args[len(need):])

        _, vjp = jax.vjp(f, *[rv[k] for k in need], *sv, *pv)
        ctv = []
        for lst in ct_refs:
            acc = lst[0][...].astype(F32)
            for r in lst[1:]:
                acc = acc + r[...].astype(F32)
            ctv.append(acc)
        g = vjp(tuple(ctv))
        for j, k in enumerate(need):
            gv = g[j]
            if k in adds:
                lim = adds[k][1]
                av = add_refs[k][...].astype(F32)
                gv = gv + (av if lim is None else jnp.where(i < lim, av, 0.0))
            lim = row_grads[k][2]
            if lim is None:
                rg_refs[k][...] = gv.astype(rg_refs[k].dtype)
            else:
                @pl.when(i < lim)
                def _(gv=gv, k=k):
                    rg_refs[k][...] = gv.astype(rg_refs[k].dtype)
        if n_s:
            opens = first_fn(i)
            for ref, gv in zip(sg_refs, g[len(need):len(need) + n_s]):
                @pl.when(opens)
                def _(ref=ref, gv=gv):
                    ref[...] = gv

                @pl.when(jnp.logical_not(opens))
                def _(ref=ref, gv=gv):
                    ref[...] += gv
        for ref, gv in zip(pg_refs, g[len(need) + n_s:]):
            @pl.when(i == 0)
            def _(ref=ref, gv=gv):
                ref[...] = gv

            @pl.when(i > 0)
            def _(ref=ref, gv=gv):
                ref[...] += gv
        hosted.finish()

    seg_spec = lambda s: pl.BlockSpec((None, 1, s.shape[-1]), lambda i: (seg_fn(i), 0, 0))
    par_spec = lambda p: pl.BlockSpec(p.shape, lambda i: (0, 0))
    in_specs = list(row_specs) + [seg_spec(s) for s in segs] + [par_spec(p) for p in params]
    args = row_arrs + list(segs) + list(params)
    for lst in cts:
        in_specs += [_row_spec(d, tm) for d in lst]
        args += [d[0] for d in lst]
    for k in add_keys:
        in_specs.append(_row_spec(adds[k][0], tm, adds[k][1]))
        args.append(adds[k][0][0])
    out_shape, out_specs = [], []
    for k in need:
        n_rows, dt, lim = row_grads[k]
        out_shape.append(_big((n_rows, rows[k][1]), dt))
        out_specs.append(_row_spec((None, rows[k][1], 0, 0), tm, lim))
    for s in segs:
        out_shape.append(jax.ShapeDtypeStruct(s.shape, F32))
        out_specs.append(seg_spec(s))
    for p in params:
        out_shape.append(jax.ShapeDtypeStruct(p.shape, F32))
        out_specs.append(par_spec(p))
    vmem = sum(tm * d[1] * 4 for d in rows) * 6 + n_ct * tm * max(d[1] for d in rows) * 8
    in_specs, out_shape, out_specs, scratch, args = hosted.call_args(in_specs, out_shape, out_specs, [], args)
    return hosted.results(pl.pallas_call(
        body, name=name, grid=(n_tiles,), in_specs=in_specs, out_shape=out_shape, out_specs=out_specs,
        scratch_shapes=scratch, compiler_params=_params(vmem + (8 << 20)),
    )(*_in_hbm(args)), unwrap=False)


def _silu(v):
    return v * jax.nn.sigmoid(v)


def _rms(v, w):
    return v * lax.rsqrt(jnp.mean(v * v, axis=-1, keepdims=True) + EPS) * w


def fn_norm_mod(x, shift, scale, w):
    return (_rms(x, w) * (1.0 + scale) + shift,)


def fn_act(g, u):
    return (_silu(g) * u,)


def make_fn_resid(coef):
    def fn(x, f, gate):
        return (x + coef * gate * f,)
    return fn


def fn_silu_bias(v, b):
    return (_silu(v + b),)


def make_fn_gate_groupnorm(width):
    half = width // 2

    def fn(y_both, z, w):
        y = y_both * _silu(z)
        lane = lax.broadcasted_iota(jnp.int32, y.shape, 1)
        lo = lane < half
        sq = y * y
        s_lo = jnp.sum(jnp.where(lo, sq, 0.0), axis=-1, keepdims=True)
        s_hi = jnp.sum(jnp.where(lo, 0.0, sq), axis=-1, keepdims=True)
        r = jnp.where(lo, lax.rsqrt(s_lo / half + EPS), lax.rsqrt(s_hi / half + EPS))
        return (y * r * w,)
    return fn


def fn_glu(a, b):
    return (a * jax.nn.sigmoid(b),)


def fn_ln_silu(vw, vh, cb, lw, lb):
    v = jnp.concatenate([vw, vh], axis=-1) + cb
    mu = jnp.mean(v, axis=-1, keepdims=True)
    var = jnp.mean(jnp.square(v - mu), axis=-1, keepdims=True)
    return (_silu((v - mu) * lax.rsqrt(var + EPS) * lw + lb),)


def _col_tile(width):
    return width // 3 if width % (3 * LANES) == 0 else width


def mod_fwd(a_rows, w_shard, b_shard):
    n, d = a_rows.shape
    ws = w_shard.shape[1]
    tn = _col_tile(ws)

    def body(a_ref, w_ref, b_ref, o_ref):
        a = _silu(a_ref[...]).astype(BF16)
        o_ref[...] = jnp.dot(a, w_ref[...].astype(BF16), preferred_element_type=F32) + b_ref[...]

    return pl.pallas_call(
        body, name="mod_fwd", grid=(ws // tn,), out_shape=jax.ShapeDtypeStruct((n, ws), F32),
        in_specs=[pl.BlockSpec((n, d), lambda j: (0, 0)), pl.BlockSpec((d, tn), lambda j: (0, j)),
                  pl.BlockSpec((1, tn), lambda j: (0, j))],
        out_specs=pl.BlockSpec((n, tn), lambda j: (0, j)), compiler_params=_params(),
    )(a_rows, w_shard, b_shard)


def mod_bwd(a_rows, d_shard, d_full, w_shard, ctx_rows):
    n, d = a_rows.shape
    ws = w_shard.shape[1]
    tn = _col_tile(ws)
    n_ct = ws // tn

    def body(a_ref, ds_ref, df_ref, w_ref, gw_ref, gb_ref, q_ref):
        j = pl.program_id(0)
        a = _silu(a_ref[...])
        ds = ds_ref[...]
        gw_ref[...] = lax.dot_general(a, ds, _DIMS["tn"], precision=HI, preferred_element_type=F32)
        dctx = ds[ctx_rows[0]:ctx_rows[0] + 1, :]
        for r in ctx_rows[1:]:
            dctx = dctx + ds[r:r + 1, :]
        q = lax.dot_general(jnp.broadcast_to(dctx, (8, tn)), w_ref[...], _DIMS["nt"], precision=HI,
                            preferred_element_type=F32)

        @pl.when(j == 0)
        def _():
            q_ref[...] = q
            df = df_ref[...]
            acc = df[0:1, :]
            for r in range(1, n):
                acc = acc + df[r:r + 1, :]
            gb_ref[...] = acc

        @pl.when(j > 0)
        def _():
            q_ref[...] += q

    return pl.pallas_call(
        body, name="mod_bwd", grid=(n_ct,),
        out_shape=(jax.ShapeDtypeStruct((d, ws), F32), jax.ShapeDtypeStruct((1, d_full.shape[1]), F32),
                   jax.ShapeDtypeStruct((8, d), F32)),
        in_specs=[pl.BlockSpec((n, d), lambda j: (0, 0)), pl.BlockSpec((n, tn), lambda j: (0, j)),
                  pl.BlockSpec(d_full.shape, lambda j: (0, 0)), pl.BlockSpec((d, tn), lambda j: (0, j))],
        out_specs=(pl.BlockSpec((d, tn), lambda j: (0, j)), pl.BlockSpec((1, d_full.shape[1]), lambda j: (0, 0)),
                   pl.BlockSpec((8, d), lambda j: (0, 0))),
        compiler_params=_params(40 << 20),
    )(a_rows, d_shard, d_full, w_shard)


def _shifted(xs, d, tok, width):
    if d == 0:
        return xs
    n = xs.shape[0]
    sh = pltpu.roll(xs, (-d) % n, axis=0)
    return jnp.where((tok + d >= 0) & (tok + d < width), sh, 0.0)


def _placed(out_shape, place):
    if place is None:
        return out_shape, 0, 0, None
    return place


def tapsum_roll(name, x, xcb, w, wcb, *, seq_len, n_seq, row_blk_off, width, piece, cb, ncb, pad, flip, place=None):
    n_tap = w.shape[0]
    n_piece = seq_len // piece
    out_shape, o_rb, o_cb, into = _placed((n_seq * seq_len, ncb * cb), place)

    def body(x_ref, w_ref, *rest):
        o_ref = rest[-1]
        wv = w_ref[...]
        tok = lax.broadcasted_iota(jnp.int32, (piece, 1), 0) % width

        def do_piece(p, carry):
            start = pl.multiple_of(p * piece, piece)
            xs = x_ref[pl.ds(start, piece), :]
            acc = jnp.zeros_like(xs)
            for k in range(n_tap):
                d = pad - k if flip else k - pad
                acc = acc + wv[k:k + 1, :] * _shifted(xs, d, tok, width)
            o_ref[pl.ds(start, piece), :] = acc
            return carry

        lax.fori_loop(0, n_piece, do_piece, 0)

    extra = [] if into is None else [into]
    return pl.pallas_call(
        body, name=name, grid=(ncb, n_seq), out_shape=_big(out_shape, F32),
        in_specs=[pl.BlockSpec((seq_len, cb), lambda j, s: (row_blk_off + s, xcb + j)),
                  pl.BlockSpec((n_tap, cb), lambda j, s: (0, wcb + j))] + [pl.BlockSpec(memory_space=pl.ANY)] * len(extra),
        out_specs=pl.BlockSpec((seq_len, cb), lambda j, s: (o_rb + s, o_cb + j)),
        input_output_aliases={2: 0} if extra else {},
        compiler_params=_params(8 * seq_len * cb * 4 + (8 << 20), 2),
    )(*_in_hbm([x, w] + extra))


def tapgrad_roll(name, dy, dycb, dy_blk_off, x, xcb, x_blk_off, *, n_tap, seq_len, n_seq, width, piece, cb, ncb, pad):
    n_piece = seq_len // piece

    def body(dy_ref, x_ref, o_ref):
        @pl.when(pl.program_id(1) == 0)
        def _():
            o_ref[...] = jnp.zeros_like(o_ref)

        tok = lax.broadcasted_iota(jnp.int32, (piece, 1), 0) % width

        def do_piece(p, carry):
            start = pl.multiple_of(p * piece, piece)
            xs = x_ref[pl.ds(start, piece), :]
            dv = dy_ref[pl.ds(start, piece), :]
            for k in range(n_tap):
                o_ref[k:k + 1, :] += jnp.sum(dv * _shifted(xs, k - pad, tok, width), axis=0, keepdims=True)
            return carry

        lax.fori_loop(0, n_piece, do_piece, 0)

    return pl.pallas_call(
        body, name=name, grid=(ncb, n_seq), out_shape=jax.ShapeDtypeStruct((n_tap, ncb * cb), F32),
        in_specs=[pl.BlockSpec((seq_len, cb), lambda j, s: (dy_blk_off + s, dycb + j)),
                  pl.BlockSpec((seq_len, cb), lambda j, s: (x_blk_off + s, xcb + j))],
        out_specs=pl.BlockSpec((n_tap, cb), lambda j, s: (0, j)),
        compiler_params=_params(8 * seq_len * cb * 4 + (8 << 20), 2),
    )(*_in_hbm([dy, x]))


def tapsum_rows(name, x, xcb, w, wcb, *, seq_len, n_seq, cb, ncb, pad, flip, place=None):
    n_tap = w.shape[0]
    n_row = seq_len // GRID_W
    halo = pad * GRID_W
    out_shape, o_rb, o_cb, into = _placed((n_seq * seq_len, ncb * cb), place)

    def body(x_ref, w_ref, *rest):
        o_ref, xp = rest[-2:]
        xp[pl.ds(0, halo), :] = jnp.zeros((halo, cb), F32)
        xp[pl.ds(halo + seq_len, halo), :] = jnp.zeros((halo, cb), F32)
        xp[pl.ds(halo, seq_len), :] = x_ref[...]
        wv = w_ref[...]

        def do_row(r, carry):
            acc = jnp.zeros((GRID_W, cb), F32)
            for k in range(n_tap):
                d = pad - k if flip else k - pad
                acc = acc + wv[k:k + 1, :] * xp[pl.ds(pl.multiple_of((r + pad + d) * GRID_W, GRID_W), GRID_W), :]
            o_ref[pl.ds(pl.multiple_of(r * GRID_W, GRID_W), GRID_W), :] = acc
            return carry

        lax.fori_loop(0, n_row, do_row, 0)

    extra = [] if into is None else [into]
    return pl.pallas_call(
        body, name=name, grid=(ncb, n_seq), out_shape=_big(out_shape, F32),
        in_specs=[pl.BlockSpec((seq_len, cb), lambda j, s: (s, xcb + j)),
                  pl.BlockSpec((n_tap, cb), lambda j, s: (0, wcb + j))] + [pl.BlockSpec(memory_space=pl.ANY)] * len(extra),
        out_specs=pl.BlockSpec((seq_len, cb), lambda j, s: (o_rb + s, o_cb + j)),
        input_output_aliases={2: 0} if extra else {},
        scratch_shapes=[pltpu.VMEM((seq_len + 2 * halo, cb), F32)],
        compiler_params=_params(10 * seq_len * cb * 4 + (8 << 20), 2),
    )(*_in_hbm([x, w] + extra))


def tapgrad_rows(name, dy, dycb, x, xcb, *, n_tap, seq_len, n_seq, cb, ncb, pad):
    n_row = seq_len // GRID_W
    halo = pad * GRID_W

    def body(dy_ref, x_ref, o_ref, xp):
        @pl.when(pl.program_id(1) == 0)
        def _():
            o_ref[...] = jnp.zeros_like(o_ref)

        xp[pl.ds(0, halo), :] = jnp.zeros((halo, cb), F32)
        xp[pl.ds(halo + seq_len, halo), :] = jnp.zeros((halo, cb), F32)
        xp[pl.ds(halo, seq_len), :] = x_ref[...]

        def do_row(r, carry):
            dv = dy_ref[pl.ds(pl.multiple_of(r * GRID_W, GRID_W), GRID_W), :]
            for k in range(n_tap):
                xs = xp[pl.ds(pl.multiple_of((r + k) * GRID_W, GRID_W), GRID_W), :]
                o_ref[k:k + 1, :] += jnp.sum(dv * xs, axis=0, keepdims=True)
            return carry

        lax.fori_loop(0, n_row, do_row, 0)

    return pl.pallas_call(
        body, name=name, grid=(ncb, n_seq), out_shape=jax.ShapeDtypeStruct((n_tap, ncb * cb), F32),
        in_specs=[pl.BlockSpec((seq_len, cb), lambda j, s: (s, dycb + j)),
                  pl.BlockSpec((seq_len, cb), lambda j, s: (s, xcb + j))],
        out_specs=pl.BlockSpec((n_tap, cb), lambda j, s: (0, j)),
        scratch_shapes=[pltpu.VMEM((seq_len + 2 * halo, cb), F32)],
        compiler_params=_params(10 * seq_len * cb * 4 + (8 << 20), 2),
    )(*_in_hbm([dy, x]))


def _ssd_blocks(b, s, *, rev, n_ctx, n_lat, lat_blocks):
    if rev:
        return jnp.where(s < n_ctx, lat_blocks + b * n_ctx + (n_ctx - 1 - s), b * n_lat + (n_lat - 1 - (s - n_ctx)))
    return jnp.where(s < n_ctx, lat_blocks + b * n_ctx + s, b * n_lat + (s - n_ctx))


def _ssd_common(xbc, raw, dtb, alog, dsk, *, rev, ds, n_head):
    if rev:
        raw = pltpu.roll(raw, LANES - n_head, axis=1)
    pre = raw + dtb
    dt = jnp.maximum(pre, 0.0) + jnp.log(1.0 + jnp.exp(-jnp.abs(pre)))
    sig = jax.nn.sigmoid(pre)
    a = -jnp.exp(alog)
    da = dt * a
    ri = lax.broadcasted_iota(jnp.int32, (CHUNK, CHUNK), 0)
    ci = lax.broadcasted_iota(jnp.int32, (CHUNK, CHUNK), 1)
    mask = (ci >= ri) if rev else (ci <= ri)
    tri = mask.astype(F32)
    tri_t = ((ci <= ri) if rev else (ci >= ri)).astype(F32)
    cs = jnp.dot(tri, da, precision=HI, preferred_element_type=F32)
    tot = jnp.sum(da, axis=0, keepdims=True)
    def wide(v):
        first = lax.broadcasted_iota(jnp.int32, (v.shape[0], LANES), 1) < HEAD_DIM
        return jnp.concatenate(
            [jnp.where(first, jnp.broadcast_to(v[:, 2 * p:2 * p + 1], first.shape),
                       jnp.broadcast_to(v[:, 2 * p + 1:2 * p + 2], first.shape)) for p in range(n_head // 2)], axis=1)

    cs_w, tot_w = wide(cs), wide(tot)
    xh = xbc[:, :ds]
    dt_w = wide(dt)
    return dict(
        dt=dt, sig=sig, a=a, cs=cs, cs_t=cs.T, tot=tot, mask=mask, tri_t=tri_t,
        e_w=jnp.exp(cs_w), wt_w=jnp.exp(tot_w - cs_w), dec_w=jnp.exp(tot_w), dt_w=dt_w, dsk_w=wide(dsk),
        xh=xh, xs_w=xh * dt_w, bm=xbc[:, ds:ds + 2 * N_STATE], cm=xbc[:, ds + 2 * N_STATE:ds + 4 * N_STATE])


def _decay(q, col):
    seg = q["cs"][:, col:col + 1] - q["cs_t"][col:col + 1, :]
    return jnp.exp(jnp.where(q["mask"], seg, -jnp.inf))


def _split_heads(v):
    lane = lax.broadcasted_iota(jnp.int32, v.shape, 1)
    return jnp.concatenate([jnp.where(lane < HEAD_DIM, v, 0.0), jnp.where(lane >= HEAD_DIM, v, 0.0)], axis=0)


def ssd_fwd(name, xbc, proj, dt_cb, dtb, alog, dsk, *, rev, n_ex, seq_len, ctx_len, ds, rider=None, add=None):
    n_head, half = ds // HEAD_DIM, ds // 2
    n_ctx, n_lat = ctx_len // CHUNK, seq_len // CHUNK
    n_step = n_ctx + n_lat
    blk = functools.partial(_ssd_blocks, rev=rev, n_ctx=n_ctx, n_lat=n_lat, lat_blocks=n_ex * n_lat)
    xw = xbc.shape[1]

    def y_blk(b, s):
        sl = jnp.maximum(s, n_ctx) - n_ctx
        return b * n_lat + ((n_lat - 1 - sl) if rev else sl)

    hosted = _Hosted(rider, 5 + (add is not None), 2, 1, (n_ex, n_step))

    def body(*refs):
        (xbc_ref, dt_ref, dtb_ref, alog_ref, dsk_ref, *add_ref), (y_ref, hs_ref), (h_scr,) = hosted.split(refs)

        @pl.when(pl.program_id(1) == 0)
        def _():
            h_scr[...] = jnp.zeros_like(h_scr)

        q = _ssd_common(xbc_ref[...], dt_ref[...], dtb_ref[...], alog_ref[...], dsk_ref[...], rev=rev, ds=ds, n_head=n_head)
        h = h_scr[...]
        hs_ref[...] = h
        for g in range(2):
            lo = g * half
            bg = q["bm"][:, g * N_STATE:(g + 1) * N_STATE].astype(BF16)
            cg = q["cm"][:, g * N_STATE:(g + 1) * N_STATE].astype(BF16)
            scores = lax.dot_general(cg, bg, _DIMS["nt"], preferred_element_type=F32)
            hg = h[:, lo:lo + half]
            off = jnp.dot(cg, hg.astype(BF16), preferred_element_type=F32)
            for j in range(half // LANES):
                c0 = (lo + j * LANES) // HEAD_DIM
                ln = slice(lo + j * LANES, lo + (j + 1) * LANES)
                p_cat = jnp.concatenate([scores * _decay(q, c0), scores * _decay(q, c0 + 1)], axis=1).astype(BF16)
                diag = jnp.dot(p_cat, _split_heads(q["xs_w"][:, ln]).astype(BF16), preferred_element_type=F32)
                y_ref[:, ln] = (diag + q["e_w"][:, ln] * off[:, j * LANES:(j + 1) * LANES]
                                + q["dsk_w"][:, ln] * q["xh"][:, ln] + (add_ref[0][:, ln] if add_ref else 0.0))
            v = (q["wt_w"][:, lo:lo + half] * q["xs_w"][:, lo:lo + half]).astype(BF16)
            h_scr[:, lo:lo + half] = (q["dec_w"][:, lo:lo + half] * hg
                                      + lax.dot_general(bg, v, _DIMS["tn"], preferred_element_type=F32))
        hosted.finish()

    vec = pl.BlockSpec((1, LANES), lambda b, s: (0, 0))
    in_specs, out_shape, out_specs, scratch, args = hosted.call_args(
        [pl.BlockSpec((CHUNK, xw), lambda b, s: (blk(b, s), 0)),
         pl.BlockSpec((CHUNK, LANES), lambda b, s: (blk(b, s), dt_cb)), vec, vec, vec]
        + [pl.BlockSpec((CHUNK, ds), lambda b, s: (y_blk(b, s), 0))] * (add is not None),
        (_big((n_ex * seq_len, ds), F32), _big((n_ex, n_step, N_STATE, ds), F32)),
        (pl.BlockSpec((CHUNK, ds), lambda b, s: (y_blk(b, s), 0)),
         pl.BlockSpec((None, None, N_STATE, ds), lambda b, s: (b, s, 0, 0))),
        [pltpu.VMEM((N_STATE, ds), F32)], [xbc, proj, dtb, alog, dsk] + ([] if add is None else [add]))
    return hosted.results(pl.pallas_call(
        body, name=name, grid=(n_ex, n_step), out_shape=out_shape, in_specs=in_specs, out_specs=out_specs,
        scratch_shapes=scratch, compiler_params=_params(40 << 20, 2),
    )(*_in_hbm(args)))


def ssd_bwd(name, xbc, proj, dt_cb, hs, dy, dtb, alog, dsk, *, rev, n_ex, seq_len, ctx_len, ds, rider=None, add=None):
    n_head, half = ds // HEAD_DIM, ds // 2
    n_ctx, n_lat = ctx_len // CHUNK, seq_len // CHUNK
    n_step = n_ctx + n_lat
    n_tok = n_ex * (seq_len + ctx_len)
    blk0 = functools.partial(_ssd_blocks, rev=rev, n_ctx=n_ctx, n_lat=n_lat, lat_blocks=n_ex * n_lat)
    step = lambda sp: n_step - 1 - sp
    blk = lambda b, sp: blk0(b, step(sp))
    xw = xbc.shape[1]

    def dy_blk(b, sp):
        sl = jnp.maximum(step(sp), n_ctx) - n_ctx
        return b * n_lat + ((n_lat - 1 - sl) if rev else sl)

    hosted = _Hosted(rider, 7 + (add is not None), 5, 1, (n_ex, n_step))

    def body(*refs):
        ((xbc_ref, dt_ref, hs_ref, dy_ref, dtb_ref, alog_ref, dsk_ref, *add_ref),
         (dxbc_ref, ddt_ref, dalog_ref, ddtb_ref, ddsk_ref), (dh_scr,)) = hosted.split(refs)
        b, sp = pl.program_id(0), pl.program_id(1)
        more = (lambda cols: add_ref[0][:, cols]) if add_ref else (lambda cols: 0.0)

        @pl.when(sp == 0)
        def _():
            dh_scr[...] = jnp.zeros_like(dh_scr)

        @pl.when((sp == 0) & (b == 0))
        def _():
            dalog_ref[...] = jnp.zeros_like(dalog_ref)
            ddtb_ref[...] = jnp.zeros_like(ddtb_ref)
            ddsk_ref[...] = jnp.zeros_like(ddsk_ref)

        q = _ssd_common(xbc_ref[...], dt_ref[...], dtb_ref[...], alog_ref[...], dsk_ref[...], rev=rev, ds=ds, n_head=n_head)
        h = hs_ref[...]
        d_y = jnp.where(step(sp) >= n_ctx, dy_ref[...], 0.0)
        dh_next = dh_scr[...]
        lane_row = lax.broadcasted_iota(jnp.int32, (1, LANES), 1)
        d_cs = jnp.zeros((CHUNK, LANES), F32)
        dxs_parts, de_parts, dwt_parts, ddec_parts = [], [], [], []
        for g in range(2):
            lo = g * half
            gs = slice(lo, lo + half)
            bg = q["bm"][:, g * N_STATE:(g + 1) * N_STATE].astype(BF16)
            cg = q["cm"][:, g * N_STATE:(g + 1) * N_STATE].astype(BF16)
            scores = lax.dot_general(cg, bg, _DIMS["nt"], preferred_element_type=F32)
            hg, dyg, dhn = h[:, gs], d_y[:, gs], dh_next[:, gs]
            off = jnp.dot(cg, hg.astype(BF16), preferred_element_type=F32)
            d_off = (q["e_w"][:, gs] * dyg).astype(BF16)
            de_parts.append(dyg * off)
            d_c = lax.dot_general(d_off, hg.astype(BF16), _DIMS["nt"], preferred_element_type=F32)
            dh_scr[:, gs] = (lax.dot_general(cg, d_off, _DIMS["tn"], preferred_element_type=F32)
                             + q["dec_w"][:, gs] * dhn)
            b_dh = jnp.dot(bg, dhn.astype(BF16), preferred_element_type=F32)
            v = q["wt_w"][:, gs] * q["xs_w"][:, gs]
            d_b = lax.dot_general(v.astype(BF16), dhn.astype(BF16), _DIMS["nt"], preferred_element_type=F32)
            dwt_parts.append(q["xs_w"][:, gs] * b_dh)
            ddec_parts.append(jnp.sum(hg * dhn, axis=0, keepdims=True))
            d_scores = jnp.zeros((CHUNK, CHUNK), F32)
            for j in range(half // LANES):
                c0 = (lo + j * LANES) // HEAD_DIM
                ln = slice(lo + j * LANES, lo + (j + 1) * LANES)
                l0, l1 = _decay(q, c0), _decay(q, c0 + 1)
                p0, p1 = scores * l0, scores * l1
                dy_st = _split_heads(d_y[:, ln]).astype(BF16)
                d_p = lax.dot_general(dy_st, q["xs_w"][:, ln].astype(BF16), _DIMS["nt"], preferred_element_type=F32)
                d_p0, d_p1 = d_p[:CHUNK], d_p[CHUNK:]
                d_scores = d_scores + d_p0 * l0 + d_p1 * l1
                for col, t in ((c0, d_p0 * p0), (c0 + 1, d_p1 * p1)):
                    d_cs = d_cs + jnp.sum(t - t.T, axis=1, keepdims=True) * (lane_row == col).astype(F32)
                p_st = jnp.concatenate([p0, p1], axis=0).astype(BF16)
                dxs_parts.append(lax.dot_general(p_st, dy_st, _DIMS["tn"], preferred_element_type=F32)
                                 + q["wt_w"][:, ln] * b_dh[:, j * LANES:(j + 1) * LANES])
            d_sc = d_scores.astype(BF16)
            d_c = d_c + jnp.dot(d_sc, bg, preferred_element_type=F32)
            d_b = d_b + lax.dot_general(d_sc, cg, _DIMS["tn"], preferred_element_type=F32)
            b_cols, c_cols = slice(ds + g * N_STATE, ds + (g + 1) * N_STATE), slice(ds + (2 + g) * N_STATE, ds + (3 + g) * N_STATE)
            dxbc_ref[:, b_cols] = d_b + more(b_cols)
            dxbc_ref[:, c_cols] = d_c + more(c_cols)
        d_xs = jnp.concatenate(dxs_parts, axis=1)
        narrow_m = (lax.broadcasted_iota(jnp.int32, (ds, LANES), 0) // HEAD_DIM
                    == lax.broadcasted_iota(jnp.int32, (ds, LANES), 1)).astype(BF16)
        rows8 = lambda v: jnp.broadcast_to(v, (8, ds))
        stacked = jnp.concatenate(
            [jnp.concatenate(dwt_parts, axis=1), jnp.concatenate(de_parts, axis=1), d_xs * q["xh"],
             rows8(jnp.concatenate(ddec_parts, axis=1)), rows8(jnp.sum(d_y * q["xh"], axis=0, keepdims=True))], axis=0)
        sums = jnp.dot(stacked.astype(BF16), narrow_m, preferred_element_type=F32)
        n_wt, n_e, n_xs = sums[:CHUNK], sums[CHUNK:2 * CHUNK], sums[2 * CHUNK:3 * CHUNK]
        n_dec, n_dsk = sums[3 * CHUNK:3 * CHUNK + 1], sums[3 * CHUNK + 8:3 * CHUNK + 9]
        e, wt, dec = jnp.exp(q["cs"]), jnp.exp(q["tot"] - q["cs"]), jnp.exp(q["tot"])
        d_wt = n_wt * wt
        d_cs = d_cs + n_e * e - d_wt
        d_tot = jnp.sum(d_wt, axis=0, keepdims=True) + n_dec * dec
        d_da = jnp.dot(q["tri_t"], d_cs, precision=HI, preferred_element_type=F32) + d_tot
        d_dt = d_da * q["a"] + n_xs
        dxbc_ref[:, :ds] = d_xs * q["dt_w"] + q["dsk_w"] * d_y + more(slice(0, ds))
        dalog_ref[...] += jnp.sum(d_da * q["dt"], axis=0, keepdims=True) * q["a"]
        d_raw = d_dt * q["sig"]
        ddtb_ref[...] += jnp.sum(d_raw, axis=0, keepdims=True)
        ddsk_ref[...] += n_dsk
        ddt_ref[...] = pltpu.roll(d_raw, n_head, axis=1) if rev else d_raw
        hosted.finish()

    vec = pl.BlockSpec((1, LANES), lambda b, s: (0, 0))
    vec_shape = jax.ShapeDtypeStruct((1, LANES), F32)
    in_specs, out_shape, out_specs, scratch, args = hosted.call_args(
        [pl.BlockSpec((CHUNK, xw), lambda b, s: (blk(b, s), 0)),
         pl.BlockSpec((CHUNK, LANES), lambda b, s: (blk(b, s), dt_cb)),
         pl.BlockSpec((None, None, N_STATE, ds), lambda b, s: (b, step(s), 0, 0)),
         pl.BlockSpec((CHUNK, ds), lambda b, s: (dy_blk(b, s), 0)), vec, vec, vec]
        + [pl.BlockSpec((CHUNK, xw), lambda b, s: (blk(b, s), 0))] * (add is not None),
        (_big((n_tok, xw), F32), _big((n_tok, LANES), F32), vec_shape, vec_shape, vec_shape),
        (pl.BlockSpec((CHUNK, xw), lambda b, s: (blk(b, s), 0)),
         pl.BlockSpec((CHUNK, LANES), lambda b, s: (blk(b, s), 0)), vec, vec, vec),
        [pltpu.VMEM((N_STATE, ds), F32)], [xbc, proj, hs, dy, dtb, alog, dsk] + ([] if add is None else [add]))
    return hosted.results(pl.pallas_call(
        body, name=name, grid=(n_ex, n_step), out_shape=out_shape, in_specs=in_specs, out_specs=out_specs,
        scratch_shapes=scratch, compiler_params=_params(48 << 20, 2),
    )(*_in_hbm(args)))


def final_loss(x3, target, w, *, tm):
    n, d = x3.shape

    def body(x_ref, t_ref, w_ref, dx_ref, dw_ref, loss_ref):
        i = pl.program_id(0)
        t = t_ref[...]

        def per_feature(xv, wv):
            err = _rms(xv, wv) - t
            return 0.5 * jnp.sum(err * err, axis=0, keepdims=True) / d

        lv, vjp = jax.vjp(per_feature, x_ref[...], w_ref[...])
        dx, dw = vjp(jnp.ones_like(lv))
        dx_ref[...] = dx

        @pl.when(i == 0)
        def _():
            dw_ref[...] = dw
            loss_ref[...] = lv

        @pl.when(i > 0)
        def _():
            dw_ref[...] += dw
            loss_ref[...] += lv

    tile = pl.BlockSpec((tm, d), lambda i: (i, 0))
    vec = pl.BlockSpec((1, d), lambda i: (0, 0))
    return pl.pallas_call(
        body, name="final_loss", grid=(n // tm,), in_specs=[tile, tile, vec],
        out_shape=(jax.ShapeDtypeStruct((n, d), F32), jax.ShapeDtypeStruct((1, d), F32), jax.ShapeDtypeStruct((1, d), F32)),
        out_specs=(tile, vec, vec), compiler_params=_params(tm * d * 4 * 16 + (8 << 20)),
    )(x3, target, w)


def sum_slots(name, arr, out_dtype=F32):
    n_slot, n_row, width = arr.shape
    tm = _row_tile(n_row, width * n_slot, cap_bytes=STREAM_TILE_BYTES * 14, mult=16)
    vmem = 2 * n_slot * tm * width * arr.dtype.itemsize + 4 * tm * width * 4

    def body(a_ref, o_ref):
        acc = a_ref[0].astype(F32)
        for j in range(1, n_slot):
            acc = acc + a_ref[j].astype(F32)
        o_ref[...] = acc.astype(o_ref.dtype)

    return pl.pallas_call(
        body, name=name, grid=(n_row // tm,), out_shape=jax.ShapeDtypeStruct((n_row, width), out_dtype),
        in_specs=[pl.BlockSpec((n_slot, tm, width), lambda i: (0, i, 0))],
        out_specs=pl.BlockSpec((tm, width), lambda i: (i, 0)), compiler_params=_params(vmem + (4 << 20)),
    )(arr)


def adamw(name, w, g_slots, m, v):
    n_slot, n_row, width = g_slots.shape
    tm = _row_tile(n_row, width, cap_bytes=STREAM_TILE_BYTES)
    if g_slots.dtype == BF16 and tm % 16:
        tm16 = _row_tile(n_row, width, cap_bytes=STREAM_TILE_BYTES, mult=16)
        if tm16 % 16 == 0:
            tm = tm16
        else:
            g_slots = g_slots.astype(F32)

    def body(w_ref, g_ref, m_ref, v_ref, go_ref, d_ref, mo_ref, vo_ref):
        g = g_ref[0].astype(F32)
        for j in range(1, n_slot):
            g = g + g_ref[j].astype(F32)
        m2 = ADAM_B1 * m_ref[...] + (1.0 - ADAM_B1) * g
        v2 = ADAM_B2 * v_ref[...] + (1.0 - ADAM_B2) * jnp.square(g)
        m_hat = m2 / (1.0 - ADAM_B1 ** ADAM_STEP)
        v_hat = v2 / (1.0 - ADAM_B2 ** ADAM_STEP)
        go_ref[...] = g
        d_ref[...] = -ADAM_LR * (m_hat / (jnp.sqrt(v_hat) + ADAM_EPS) + ADAM_WD * w_ref[...])
        mo_ref[...] = m2
        vo_ref[...] = v2

    tile = pl.BlockSpec((tm, width), lambda i: (i, 0))
    shape = jax.ShapeDtypeStruct((n_row, width), F32)
    return pl.pallas_call(
        body, name=name, grid=(n_row // tm,), out_shape=(shape,) * 4,
        in_specs=[tile, pl.BlockSpec((n_slot, tm, width), lambda i: (0, i, 0)), tile, tile],
        out_specs=(tile,) * 4, compiler_params=_params(2 * (7 + n_slot) * tm * width * 4 + (4 << 20)),
    )(w, g_slots, m, v)


def cctx_grad(q_all, c_ctx_row):
    d = c_ctx_row.shape[1]

    def body(q_ref, c_ref, o_ref):
        acc = q_ref[0, 0:1, :]
        for j in (2, 4, 6):
            acc = acc + q_ref[j, 0:1, :]
        _, vjp = jax.vjp(_silu, c_ref[...])
        o_ref[...] = vjp(acc)[0]

    return pl.pallas_call(
        body, name="cctx_grad", out_shape=jax.ShapeDtypeStruct((1, d), F32),
    )(q_all, c_ctx_row)


def loss_total(pack_sum, d):
    def body(p_ref, o_ref):
        o_ref[...] = jnp.sum(p_ref[:, 0:d], axis=1, keepdims=True)

    return pl.pallas_call(
        body, name="loss_total", out_shape=jax.ShapeDtypeStruct((1, 1), F32),
    )(pack_sum)


class _Plan:
    def __init__(self):
        self.builders, self.got = {}, {}

    def on(self, host, key, builder):
        self.builders.setdefault(host, []).append((key, builder))

    def run(self, host, fn, *args, **kw):
        if host not in self.builders:
            return fn(host, *args, **kw)
        keys, riders = zip(*[(key, builder(self)) for key, builder in self.builders[host]])
        res, landed = fn(host, *args, rider=Riders(riders), **kw)
        for key, r in zip(keys, riders):
            self.got[key], landed = landed[:r.n], landed[r.n:]
        return res


def _val(w):
    return w() if callable(w) else w


def _matmul_tile(n_rows, tm):
    return 2 * tm if n_rows % (2 * tm) == 0 else tm


def _ffn_fwd(plan, tag, xin, n_rows, tm, seg_fn, shift, scale, gate, norm_w, wg, wu, wd, fuse_gate_up=False):
    d = xin[1]
    n_tiles = n_rows // tm
    (h,) = plan.run(f"{tag}_norm", rowwise, fn_norm_mod, [xin], [shift, scale], [norm_w], [(n_rows, d, BF16)],
                    tm=tm, n_tiles=n_tiles, seg_fn=seg_fn)
    tmm = _matmul_tile(n_rows, tm)
    if fuse_gate_up:
        g, u, act = plan.run(f"{tag}_gate_up", matmul, [(h, _val(wg)), (h, _val(wu))], "nn", b_ch=True, out_ch=True,
                             tm=min(tm, 256), fold=True,
                             post=([], lambda ag, au: (ag, au, fn_act(ag, au)[0]), [BF16, BF16, BF16]))
    else:
        g = plan.run(f"{tag}_gate", matmul, [(h, _val(wg))], "nn", out_dtype=BF16, b_ch=True, out_ch=True, tm=tmm)
        u, act = plan.run(f"{tag}_up", matmul, [(h, _val(wu))], "nn", b_ch=True, out_ch=True, tm=tm, fold=True,
                          post=([g], lambda acc, gv: (acc, fn_act(gv, acc)[0]), [BF16, BF16]))
    f = plan.run(f"{tag}_down", matmul, [(act, _val(wd))], "nn", a_ch=True, b_ch=True, tm=tmm, fold=True)
    (xo,) = plan.run(f"{tag}_resid", rowwise, make_fn_resid(0.5), [xin, row(f)], [gate], [], [(n_rows, d, F32)],
                     tm=tm, n_tiles=n_tiles, seg_fn=seg_fn)
    return xo, (h, g, u, act, f)


def _ffn_bwd(plan, tag, d_xo, saved, xin, n_rows, tm, seg_fn, first_fn, shift, scale, gate, norm_w, wg, wu, wd, dx_rows, dx_limit):
    h, g, u, act, f = saved
    d = xin[1]
    n_tiles = n_rows // tm
    n_ch, _, n_hid = g.shape
    d_f, d_gate = plan.run(f"{tag}_resid_bwd", rowwise_bwd, make_fn_resid(0.5), [xin, row(f)], [gate], [], [[row(d_xo)]],
                           [None, (n_rows, BF16, None)], tm=tm, n_tiles=n_tiles, seg_fn=seg_fn, first_fn=first_fn)
    tmm = _matmul_tile(n_rows, tm)
    def act_vjp(d_act, gv, uv):
        s = jax.nn.sigmoid(gv)
        gs = gv * s
        return d_act * uv * (s + gs * (1.0 - s)), d_act * gs
    d_g, d_u = plan.run(f"{tag}_down_dx", matmul, [(d_f, wd)], "nt", b_ch=True, out_ch=True, tm=tmm,
                        post=([g, u], act_vjp, [BF16, BF16]))
    plan.got[f"{tag}_d_wd"] = plan.run(f"{tag}_down_dw", matmul, [(act, d_f)], "tn", out_dtype=BF16, a_ch=True, out_ch=True, tm=tmm)
    d_h = plan.run(f"{tag}_up_dx", matmul, [(d_g, wg), (d_u, wu)], "nt", a_ch=True, b_ch=True, tm=tmm)
    plan.got[f"{tag}_d_wg"] = plan.run(f"{tag}_gate_dw", matmul, [(d_g, h)], "tn", out_dtype=BF16, a_ch=True, out_ch=True, tm=tmm)
    plan.got[f"{tag}_d_wu"] = plan.run(f"{tag}_up_dw", matmul, [(d_u, h)], "tn", out_dtype=BF16, a_ch=True, out_ch=True, tm=tmm)
    d_x, d_shift, d_scale, d_nw = plan.run(
        f"{tag}_norm_bwd", rowwise_bwd, fn_norm_mod, [xin], [shift, scale], [norm_w], [[row(d_h)]], [(dx_rows, F32, dx_limit)],
        tm=tm, n_tiles=n_tiles, seg_fn=seg_fn, first_fn=first_fn, adds={0: (row(d_xo), None)})
    return d_x, (d_shift, d_scale, d_gate), d_nw


def kernel(x, c, ctx, c_ctx, w_mod, b_mod, norm_ffn1, ffn1_gate, ffn1_up, ffn1_down, norm_mix, w_in, ssm_conv_w, ssm_conv_b, dt_bias_fwd, dt_bias_bwd, a_log_fwd, a_log_bwd, ssm_d, ssm_norm_w, cconv_w, cconv_b, cconv_ln_w, cconv_ln_b, w_out, norm_ffn2, ffn2_gate, ffn2_up, ffn2_down, final_norm, loss_target, m_c_ctx, m_w_mod, m_b_mod, m_norm_ffn1, m_ffn1_gate, m_ffn1_up, m_ffn1_down, m_norm_mix, m_w_in, m_ssm_conv_w, m_ssm_conv_b, m_dt_bias_fwd, m_dt_bias_bwd, m_a_log_fwd, m_a_log_bwd, m_ssm_d, m_ssm_norm_w, m_cconv_w, m_cconv_b, m_cconv_ln_w, m_cconv_ln_b, m_w_out, m_norm_ffn2, m_ffn2_gate, m_ffn2_up, m_ffn2_down, m_final_norm, v_c_ctx, v_w_mod, v_b_mod, v_norm_ffn1, v_ffn1_gate, v_ffn1_up, v_ffn1_down, v_norm_mix, v_w_in, v_ssm_conv_w, v_ssm_conv_b, v_dt_bias_fwd, v_dt_bias_bwd, v_a_log_fwd, v_a_log_bwd, v_ssm_d, v_ssm_norm_w, v_cconv_w, v_cconv_b, v_cconv_ln_w, v_cconv_ln_b, v_w_out, v_norm_ffn2, v_ffn2_gate, v_ffn2_up, v_ffn2_down, v_final_norm):
    weights = dict(c_ctx=c_ctx, w_mod=w_mod, b_mod=b_mod, norm_ffn1=norm_ffn1, ffn1_gate=ffn1_gate, ffn1_up=ffn1_up, ffn1_down=ffn1_down, norm_mix=norm_mix, w_in=w_in, ssm_conv_w=ssm_conv_w, ssm_conv_b=ssm_conv_b, dt_bias_fwd=dt_bias_fwd, dt_bias_bwd=dt_bias_bwd, a_log_fwd=a_log_fwd, a_log_bwd=a_log_bwd, ssm_d=ssm_d, ssm_norm_w=ssm_norm_w, cconv_w=cconv_w, cconv_b=cconv_b, cconv_ln_w=cconv_ln_w, cconv_ln_b=cconv_ln_b, w_out=w_out, norm_ffn2=norm_ffn2, ffn2_gate=ffn2_gate, ffn2_up=ffn2_up, ffn2_down=ffn2_down, final_norm=final_norm)
    mom1 = dict(c_ctx=m_c_ctx, w_mod=m_w_mod, b_mod=m_b_mod, norm_ffn1=m_norm_ffn1, ffn1_gate=m_ffn1_gate, ffn1_up=m_ffn1_up, ffn1_down=m_ffn1_down, norm_mix=m_norm_mix, w_in=m_w_in, ssm_conv_w=m_ssm_conv_w, ssm_conv_b=m_ssm_conv_b, dt_bias_fwd=m_dt_bias_fwd, dt_bias_bwd=m_dt_bias_bwd, a_log_fwd=m_a_log_fwd, a_log_bwd=m_a_log_bwd, ssm_d=m_ssm_d, ssm_norm_w=m_ssm_norm_w, cconv_w=m_cconv_w, cconv_b=m_cconv_b, cconv_ln_w=m_cconv_ln_w, cconv_ln_b=m_cconv_ln_b, w_out=m_w_out, norm_ffn2=m_norm_ffn2, ffn2_gate=m_ffn2_gate, ffn2_up=m_ffn2_up, ffn2_down=m_ffn2_down, final_norm=m_final_norm)
    mom2 = dict(c_ctx=v_c_ctx, w_mod=v_w_mod, b_mod=v_b_mod, norm_ffn1=v_norm_ffn1, ffn1_gate=v_ffn1_gate, ffn1_up=v_ffn1_up, ffn1_down=v_ffn1_down, norm_mix=v_norm_mix, w_in=v_w_in, ssm_conv_w=v_ssm_conv_w, ssm_conv_b=v_ssm_conv_b, dt_bias_fwd=v_dt_bias_fwd, dt_bias_bwd=v_dt_bias_bwd, a_log_fwd=v_a_log_fwd, a_log_bwd=v_a_log_bwd, ssm_d=v_ssm_d, ssm_norm_w=v_ssm_norm_w, cconv_w=v_cconv_w, cconv_b=v_cconv_b, cconv_ln_w=v_cconv_ln_w, cconv_ln_b=v_cconv_ln_b, w_out=v_w_out, norm_ffn2=v_norm_ffn2, ffn2_gate=v_ffn2_gate, ffn2_up=v_ffn2_up, ffn2_down=v_ffn2_down, final_norm=v_final_norm)
    order = list(weights)

    n_ex, seq_len, d = x.shape
    ctx_len = ctx.shape[1]
    ds = d
    n_head = ds // HEAD_DIM
    xw = ds + 4 * N_STATE
    n_lat, n_ctx_rows = n_ex * seq_len, n_ex * ctx_len
    n_tok = n_lat + n_ctx_rows
    tm = math.gcd(math.gcd(512, seq_len), n_ctx_rows)
    seg_all, first_all = _segmenter(tm, seq_len, n_lat)
    lat_tiles = n_lat // tm

    xi, yi, ci = lax.axis_index("x"), lax.axis_index("y"), lax.axis_index("c")
    me, chip = 4 * xi + 2 * yi + ci, 2 * xi + yi

    (c_all,) = exchange("gather_c", [c], "all8")
    n_all = 8 * n_ex
    n_cond = -(-(n_all + 1) // 8) * 8
    cond = jnp.concatenate([c_all.reshape(n_all, d), c_ctx[None, :], jnp.zeros((n_cond - n_all - 1, d), F32)])
    mod_w = w_mod.shape[2]
    b_shard = lax.dynamic_slice(b_mod, (0, chip * mod_w), (1, mod_w))
    (mod_g,) = exchange("gather_mod", [mod_fwd(cond, w_mod[0], b_shard)], "chips")
    mod_full = mod_g.transpose(1, 0, 2).reshape(n_cond, N_CHIPS * mod_w)
    mod_mine = lax.dynamic_slice(mod_full, (me * n_ex, 0), (n_ex, 9 * d)).reshape(n_ex, 9, d)
    mod_ctx = mod_full[n_all].reshape(9, d)
    tabs = [jnp.concatenate([mod_mine[:, j], mod_ctx[j][None]])[:, None, :] for j in range(9)]
    lat = lambda t: t[:n_ex]

    bf = lambda w: w[0].astype(BF16)
    plan = _Plan()
    gather = lambda *ws: (lambda p: Rider(list(ws), "chips"))
    plan.on("ffn1_norm", "wg1", gather(bf(ffn1_gate)))
    plan.on("ffn1_gate", "wu1", gather(bf(ffn1_up)))
    plan.on("ffn1_up", "wd1", gather(bf(ffn1_down)))
    cut_a, cut_b = d * 5 // 8, d * 7 // 8
    plan.on("ffn1_down", "win_a", gather(bf(w_in)[:cut_a]))
    plan.on("ffn1_resid", "win_b", gather(bf(w_in)[cut_a:cut_b], ssm_conv_w[0], cconv_w[0]))
    xt = two_rows(x.reshape(n_lat, d), ctx.reshape(n_ctx_rows, d), lat_tiles)
    x1, saved1 = _ffn_fwd(plan, "ffn1", xt, n_tok, tm, seg_all, tabs[0], tabs[1], tabs[2], norm_ffn1,
                          lambda: plan.got["wg1"][0], lambda: plan.got["wu1"][0], lambda: plan.got["wd1"][0])
    (wg1,), (wu1,), (wd1,), (win_a,), (win_b, w5_g, w31_g) = (plan.got[k] for k in ("wg1", "wu1", "wd1", "win_a", "win_b"))
    (h2,), (win_c,) = rowwise("mix_norm", fn_norm_mod, [row(x1)], [tabs[3], tabs[4]], [norm_mix], [(n_tok, d, BF16)],
                              tm=tm, n_tiles=n_tok // tm, seg_fn=seg_all, rider=Rider([bf(w_in)[cut_b:]], "chips"))
    win_g = jnp.concatenate([win_a, win_b, win_c], axis=1)
    unshard_cols = lambda t: t.transpose(1, 0, 2).reshape(t.shape[1], N_CHIPS * t.shape[2])
    win = unshard_cols(win_g)
    o_x, o_dt, o_glu = ds, ds + xw, ds + xw + 2 * n_head
    w_z, w_xbc, w_dt = win[:, :ds], win[:, o_x:o_dt], win[:, o_dt:o_glu]
    w_ga, w_gb = win[:, o_glu:o_glu + d], win[:, o_glu + d:]
    w_dtp = jnp.concatenate([w_dt, jnp.zeros((d, LANES - 2 * n_head), BF16)], axis=1)
    w_cat = jnp.concatenate([w_z, w_ga, w_gb, w_xbc, w_dtp], axis=1)
    cbw = d // 2
    xbc_cb, dt_cb = 3 * d // cbw, (3 * d + xw) // LANES
    w5, w31 = unshard_cols(w5_g), unshard_cols(w31_g)
    pad_vec = lambda v: jnp.concatenate([v.reshape(1, -1), jnp.zeros((1, LANES - v.size), F32)], axis=1)
    dtb_f, dtb_b, alog_f, alog_b = map(pad_vec, (dt_bias_fwd, dt_bias_bwd, a_log_fwd, a_log_bwd))
    dsk_f, dsk_b = pad_vec(ssm_d), jnp.zeros((1, LANES), F32)

    proj, (wg2,) = matmul("mix_proj", [(h2, w_cat)], "nn", tm=min(tm, 256), rider=Rider([bf(ffn2_gate)], "chips"))
    def conv5(name, src, cb0, flip):
        out = None
        for part, seq, off in (("lat", seq_len, 0), ("ctx", ctx_len, n_lat // ctx_len)):
            out = tapsum_roll(f"{name}_{part}", src, cb0, w5, 0, seq_len=seq, n_seq=n_ex, row_blk_off=off, width=seq,
                              piece=seq, cb=cbw, ncb=xw // cbw, pad=w5.shape[0] // 2, flip=flip,
                              place=((n_tok, xw), off, 0, out))
        return out

    craw = conv5("xbc_conv", proj, xbc_cb, False)
    (xbc,) = rowwise("xbc_silu", fn_silu_bias, [row(craw)], [], [ssm_conv_b], [(n_tok, xw, F32)], tm=tm, n_tiles=n_tok // tm)
    ssd = dict(n_ex=n_ex, seq_len=seq_len, ctx_len=ctx_len, ds=ds)
    (y_f, hs_f), (wu2,) = ssd_fwd("ssd_fwd_f", xbc, proj, dt_cb, dtb_f, alog_f, dsk_f, rev=False,
                                  rider=Rider([bf(ffn2_up)], "chips"), **ssd)
    (y_b, hs_b), (wout_g, wd2) = ssd_fwd("ssd_fwd_b", xbc, proj, dt_cb, dtb_b, alog_b, dsk_b, rev=True,
                                         rider=Rider([bf(w_out), bf(ffn2_down)], "chips"), add=y_f, **ssd)
    wout = wout_g.reshape(2 * d, d)
    wo_y, wo_u = wout[:ds], wout[ds:]
    fn_gate = make_fn_gate_groupnorm(ds)
    (yn,) = rowwise("ssd_gate", fn_gate, [row(y_b), row(proj, d, 0)], [], [ssm_norm_w], [(n_lat, ds, BF16)],
                    tm=tm, n_tiles=lat_tiles)
    (u0,) = rowwise("glu", fn_glu, [row(proj, d, 1), row(proj, d, 2)], [], [], [(n_lat, d, F32)], tm=tm, n_tiles=lat_tiles)
    cb31 = max(LANES, d // 4)
    ncb31 = (d // 2) // cb31
    pad31 = w31.shape[0] // 2
    piece31 = min(seq_len, 4 * GRID_W)
    v_w = tapsum_roll("cconv_cols", u0, 0, w31, 0, seq_len=seq_len, n_seq=n_ex, row_blk_off=0, width=GRID_W,
                      piece=piece31, cb=cb31, ncb=ncb31, pad=pad31, flip=False)
    v_h = tapsum_rows("cconv_rows", u0, ncb31, w31, ncb31, seq_len=seq_len, n_seq=n_ex, cb=cb31, ncb=ncb31, pad=pad31, flip=False)
    (un,) = rowwise("cconv_ln", fn_ln_silu, [row(v_w), row(v_h)], [], [cconv_b, cconv_ln_w, cconv_ln_b], [(n_lat, d, BF16)],
                    tm=tm, n_tiles=lat_tiles)
    mix = matmul("mix_out", [(yn, wo_y), (un, wo_u)], "nn", tm=tm)
    seg_lat, first_lat = _segmenter(tm, seq_len, n_lat)
    (x2,) = rowwise("mix_resid", make_fn_resid(1.0), [row(x1), row(mix)], [lat(tabs[5])], [], [(n_lat, d, F32)],
                    tm=tm, n_tiles=lat_tiles, seg_fn=seg_lat)
    x3, saved2 = _ffn_fwd(plan, "ffn2", row(x2), n_lat, tm, seg_lat, lat(tabs[6]), lat(tabs[7]), lat(tabs[8]), norm_ffn2, wg2, wu2, wd2,
                          fuse_gate_up=True)
    d_x3, d_final, loss_vec = final_loss(x3, loss_target.reshape(n_lat, d), final_norm.reshape(1, d), tm=tm)

    shard_cols = lambda t: t.reshape(t.shape[0], N_CHIPS, -1).transpose(1, 0, 2)

    def pieces(t):
        t = jnp.pad(t, ((0, 0), (0, -t.shape[1] % 32), (0, 0)))
        return t.reshape(2 * N_CHIPS, t.shape[1] // 2, t.shape[2]).astype(BF16)

    scatter = lambda *ts: Rider([pieces(t) for t in ts], "all8", scatter=True)
    halves = lambda names, landed: Rider([sum_slots(f"sum_{nm}", r, BF16) for nm, r in zip(names, landed)], "sibling")
    swapped = {}
    plan.on("ffn2_up_dx", "sc_ffn2_down", lambda p: scatter(p.got["ffn2_d_wd"]))
    plan.on("ffn2_up_dw", "sc_ffn2_gate", lambda p: scatter(p.got["ffn2_d_wg"]))
    d_x2, (d_s6, d_s7, d_g8), d_nffn2 = _ffn_bwd(
        plan, "ffn2", d_x3, saved2, row(x2), n_lat, tm, seg_lat, first_lat, lat(tabs[6]), lat(tabs[7]), lat(tabs[8]), norm_ffn2,
        wg2, wu2, wd2, n_lat, None)
    d_mix, d_g5 = rowwise_bwd("mix_resid_bwd", make_fn_resid(1.0), [row(x1), row(mix)], [lat(tabs[5])], [], [[row(d_x2)]],
                              [None, (n_lat, BF16, None)], tm=tm, n_tiles=lat_tiles, seg_fn=seg_lat, first_fn=first_lat)
    d_yn = matmul("mix_out_dy", [(d_mix, wo_y)], "nt", tm=tm)
    d_un = matmul("mix_out_du", [(d_mix, wo_u)], "nt", tm=tm)
    d_wout = jnp.concatenate([matmul("mix_out_dwy", [(yn, d_mix)], "tn", out_dtype=BF16, tm=tm),
                              matmul("mix_out_dwu", [(un, d_mix)], "tn", out_dtype=BF16, tm=tm)])
    d_vw, d_vh, d_cb, d_lnw, d_lnb = rowwise_bwd(
        "cconv_ln_bwd", fn_ln_silu, [row(v_w), row(v_h)], [], [cconv_b, cconv_ln_w, cconv_ln_b], [[row(d_un)]],
        [(n_lat, F32, None)] * 2, tm=tm, n_tiles=lat_tiles)
    d_u0 = tapsum_roll("cconv_cols_dx", d_vw, 0, w31, 0, seq_len=seq_len, n_seq=n_ex, row_blk_off=0, width=GRID_W,
                       piece=piece31, cb=cb31, ncb=ncb31, pad=pad31, flip=True, place=((n_lat, d), 0, 0, None))
    d_u0 = tapsum_rows("cconv_rows_dx", d_vh, 0, w31, ncb31, seq_len=seq_len, n_seq=n_ex, cb=cb31, ncb=ncb31, pad=pad31,
                       flip=True, place=((n_lat, d), 0, ncb31, d_u0))
    d_w31 = jnp.concatenate([
        tapgrad_roll("cconv_cols_dw", d_vw, 0, 0, u0, 0, 0, n_tap=w31.shape[0], seq_len=seq_len, n_seq=n_ex, width=GRID_W,
                     piece=piece31, cb=cb31, ncb=ncb31, pad=pad31),
        tapgrad_rows("cconv_rows_dw", d_vh, 0, u0, ncb31, n_tap=w31.shape[0], seq_len=seq_len, n_seq=n_ex, cb=cb31,
                     ncb=ncb31, pad=pad31)], axis=1)
    d_ga, d_gb = rowwise_bwd("glu_bwd", fn_glu, [row(proj, d, 1), row(proj, d, 2)], [], [], [[row(d_u0)]],
                             [(n_lat, BF16, None)] * 2, tm=tm, n_tiles=lat_tiles)
    d_ysum, d_z, d_ssmnw = rowwise_bwd(
        "ssd_gate_bwd", fn_gate, [row(y_b), row(proj, d, 0)], [], [ssm_norm_w], [[row(d_yn)]],
        [(n_lat, F32, None), (n_lat, BF16, None)], tm=tm, n_tiles=lat_tiles)
    (dxbc_f, ddt_f, dalog_f, ddtb_f, ddsk), landed = ssd_bwd(
        "ssd_bwd_f", xbc, proj, dt_cb, hs_f, d_ysum, dtb_f, alog_f, dsk_f, rev=False,
        rider=scatter(plan.got["ffn2_d_wu"], d_wout.reshape(N_CHIPS, -1, d)), **ssd)
    (dxbc_b, ddt_b, dalog_b, ddtb_b, _), both = ssd_bwd(
        "ssd_bwd_b", xbc, proj, dt_cb, hs_b, d_ysum, dtb_b, alog_b, dsk_b, rev=True,
        rider=halves(["ffn2_down", "ffn2_gate"], plan.got["sc_ffn2_down"] + plan.got["sc_ffn2_gate"]), add=dxbc_f, **ssd)
    swapped.update(zip(["ffn2_down", "ffn2_gate"], both))
    (d_craw, d_conv_b), both = rowwise_bwd(
        "xbc_silu_bwd", fn_silu_bias, [row(craw)], [], [ssm_conv_b], [[row(dxbc_b)]],
        [(n_tok, F32, None)], tm=tm, n_tiles=n_tok // tm, rider=halves(["ffn2_up", "w_out"], landed))
    swapped.update(zip(["ffn2_up", "w_out"], both))
    d_pxbc = conv5("xbc_conv_dx", d_craw, 0, True)
    g5 = lambda name, seq, off: tapgrad_roll(name, d_craw, 0, off, proj, xbc_cb, off, n_tap=w5.shape[0], seq_len=seq,
                                             n_seq=n_ex, width=seq, piece=seq, cb=cbw, ncb=xw // cbw, pad=w5.shape[0] // 2)
    d_w5 = g5("xbc_conv_lat_dw", seq_len, 0) + g5("xbc_conv_ctx_dw", ctx_len, n_lat // ctx_len)
    lat_pairs = [(d_z, w_z), (d_ga, w_ga), (d_gb, w_gb), (d_pxbc, w_xbc), (ddt_f, w_dtp), (ddt_b, w_dtp)]
    d_h2 = matmul("mix_proj_dx_lat", lat_pairs, "nt", rows=n_lat, tm=min(tm, 256), place=(n_tok, 0, None))
    d_h2 = matmul("mix_proj_dx_ctx", lat_pairs[3:], "nt", rows=n_ctx_rows, row_off=n_lat, tm=min(tm, 256),
                  place=(n_tok, n_lat, d_h2))
    d_wz = matmul("mix_proj_dwz", [(d_z, h2)], "tn", out_dtype=BF16, rows=n_lat, tm=tm)
    d_wga = matmul("mix_proj_dwa", [(d_ga, h2)], "tn", out_dtype=BF16, rows=n_lat, tm=tm)
    d_wgb = matmul("mix_proj_dwb", [(d_gb, h2)], "tn", out_dtype=BF16, rows=n_lat, tm=tm)
    d_wxbc = matmul("mix_proj_dwx", [(d_pxbc, h2)], "tn", out_dtype=BF16, tm=tm)
    d_wdt = matmul("mix_proj_dwt", [(ddt_f, h2), (ddt_b, h2)], "tn", out_dtype=BF16, tm=tm)
    d_win_t = jnp.concatenate([d_wz, d_wxbc, d_wdt[:2 * n_head], d_wga, d_wgb]).reshape(N_CHIPS, -1, d)
    d_x1, d_s3, d_s4, d_nmix = rowwise_bwd(
        "mix_norm_bwd", fn_norm_mod, [row(x1)], [tabs[3], tabs[4]], [norm_mix], [[row(d_h2)]], [(n_tok, F32, None)],
        tm=tm, n_tiles=n_tok // tm, seg_fn=seg_all, first_fn=first_all, adds={0: (row(d_x2), lat_tiles)})
    mix_names = ["w_in", "ssm_conv_w", "cconv_w"]
    plan.on("ffn1_down_dx", "sc_conv", lambda p: scatter(shard_cols(d_w5), shard_cols(d_w31)))
    plan.on("ffn1_up_dx", "sc_win", lambda p: scatter(d_win_t))
    plan.on("ffn1_gate_dw", "sc_ffn1_down", lambda p: scatter(p.got["ffn1_d_wd"]))
    plan.on("ffn1_up_dw", "sc_ffn1_gate", lambda p: scatter(p.got["ffn1_d_wg"]))
    plan.on("ffn1_up_dw", "sw_mix", lambda p: halves(mix_names, p.got["sc_win"] + p.got["sc_conv"]))
    plan.on("ffn1_norm_bwd", "sc_ffn1_up", lambda p: scatter(p.got["ffn1_d_wu"]))
    plan.on("ffn1_up_dw", "sw_ffn1_down", lambda p: halves(["ffn1_down"], p.got["sc_ffn1_down"]))
    d_xt, (d_s0, d_s1, d_g2), d_nffn1 = _ffn_bwd(
        plan, "ffn1", d_x1, saved1, xt, n_tok, tm, seg_all, first_all, tabs[0], tabs[1], tabs[2], norm_ffn1, wg1, wu1, wd1,
        n_lat, lat_tiles)
    swapped.update(zip(mix_names + ["ffn1_down"], plan.got["sw_mix"] + plan.got["sw_ffn1_down"]))
    last_names = ["ffn1_gate", "ffn1_up"]
    last = halves(last_names, plan.got["sc_ffn1_gate"] + plan.got["sc_ffn1_up"])
    grad_x = d_xt.reshape(n_ex, seq_len, d)

    with_ctx0 = lambda t: jnp.concatenate([t, jnp.zeros((1, 1, d), F32)])
    d_tabs = [d_s0, d_s1, d_g2, d_s3, d_s4, with_ctx0(d_g5), with_ctx0(d_s6), with_ctx0(d_s7), with_ctx0(d_g8)]
    d_mod_rows = jnp.concatenate([t[:, 0, :] for t in d_tabs], axis=1)
    n_pad_rows = -(-(n_ex + 1) // 8) * 8
    d_mod_rows = jnp.concatenate([d_mod_rows, jnp.zeros((n_pad_rows - n_ex - 1, 9 * d), F32)])
    small = [("loss", loss_vec), ("norm_ffn1", d_nffn1), ("norm_mix", d_nmix), ("ssm_conv_b", d_conv_b),
             ("dt_bias_fwd", ddtb_f[:, :n_head]), ("dt_bias_bwd", ddtb_b[:, :n_head]), ("a_log_fwd", dalog_f[:, :n_head]),
             ("a_log_bwd", dalog_b[:, :n_head]), ("ssm_d", ddsk[:, :n_head]), ("ssm_norm_w", d_ssmnw), ("cconv_b", d_cb),
             ("cconv_ln_w", d_lnw), ("cconv_ln_b", d_lnb), ("norm_ffn2", d_nffn2), ("final_norm", d_final)]
    n_small = sum(v.size for _, v in small)
    n_pack = -(-n_small // (8 * LANES)) * (8 * LANES)
    pack = jnp.concatenate([v.reshape(-1) for _, v in small] + [jnp.zeros((n_pack - n_small,), F32)]).reshape(-1, LANES)
    (pack_all, d_mod_all), both = exchange_many("gather_small_swap_last", [Rider([pack, d_mod_rows], "all8"), last])
    swapped.update(zip(last_names, both))
    pack_sum = sum_slots("small_sum", pack_all)
    loss = loss_total(pack_sum.reshape(1, n_pack), d).reshape(())
    flat_sum = pack_sum.reshape(-1)
    small_grads, pos = {}, 0
    for nm, v in small:
        small_grads[nm] = flat_sum[pos:pos + v.size]
        pos += v.size
    d_mod_all = d_mod_all.reshape(8 * n_pad_rows, 9 * d)
    cond_rows = [jnp.concatenate([cond[j * n_ex:(j + 1) * n_ex], c_ctx[None, :],
                                  jnp.zeros((n_pad_rows - n_ex - 1, d), F32)]) for j in range(8)]
    cond_bwd = jnp.concatenate(cond_rows)
    d_mod_shard = lax.dynamic_slice(d_mod_all, (0, chip * mod_w), (8 * n_pad_rows, mod_w))
    g_wmod, g_bmod, q_part = mod_bwd(cond_bwd, d_mod_shard, d_mod_all, w_mod[0],
                                     tuple(j * n_pad_rows + n_ex for j in range(8)))
    (q_all,) = exchange("gather_cctx", [q_part], "all8")
    g_cctx = cctx_grad(q_all, c_ctx.reshape(1, d))
    small_grads["c_ctx"], small_grads["b_mod"] = g_cctx.reshape(-1), g_bmod.reshape(-1)

    transposed = {"ffn1_gate", "ffn1_up", "ffn2_gate", "ffn2_up", "w_in"}
    results = {}
    for nm, both in swapped.items():
        flip = (lambda t: jnp.swapaxes(t, 1, 2)) if nm in transposed else (lambda t: t)
        shape = flip(weights[nm]).shape
        two_d = lambda t: flip(t).reshape(shape[-2], shape[-1])
        g_full = both.reshape(1, -1, shape[-1])[:, :shape[-2]]
        results[nm] = [flip(r.reshape(shape)) for r in
                       adamw(f"adamw_{nm}", two_d(weights[nm]), g_full, two_d(mom1[nm]), two_d(mom2[nm]))]
    results["w_mod"] = [r.reshape(w_mod.shape) for r in adamw("adamw_w_mod", w_mod[0], g_wmod[None], m_w_mod[0], v_w_mod[0])]
    small_names = [nm for nm in order if nm not in results]
    n_sm = sum(weights[nm].size for nm in small_names)
    n_smp = -(-n_sm // (8 * LANES)) * (8 * LANES)
    packed = lambda src: jnp.concatenate([src[nm].reshape(-1) for nm in small_names] + [jnp.zeros((n_smp - n_sm,), F32)]).reshape(-1, LANES)
    sm_out = adamw("adamw_small", packed(weights), packed(small_grads)[None], packed(mom1), packed(mom2))
    pos = 0
    for nm in small_names:
        size = weights[nm].size
        results[nm] = [r.reshape(-1)[pos:pos + size].reshape(weights[nm].shape) for r in sm_out]
        pos += size
    return (loss, grad_x, *[results[nm][0] for nm in order], *[results[nm][1] for nm in order],
            *[results[nm][2] for nm in order], *[results[nm][3] for nm in order])
```

```python
import functools
import math

import jax
import jax.numpy as jnp
from jax import lax
from jax.experimental import pallas as pl
from jax.experimental.pallas import tpu as pltpu

F32 = jnp.float32
BF16 = jnp.bfloat16
HI = lax.Precision.HIGHEST
MESH = pl.DeviceIdType.MESH

EPS = 1e-6
GRID_W = 64
HEAD_DIM = 64
N_STATE = 128
CHUNK = 128
LANES = 128
N_CHIPS = 4
ADAM_LR, ADAM_B1, ADAM_B2, ADAM_EPS, ADAM_WD, ADAM_STEP = 0.001, 0.9, 0.999, 1e-08, 0.01, 10
VMEM_CAP = 56 * 1024 * 1024
STREAM_TILE_BYTES = 3 << 19


def _params(vmem_bytes=None, n_axes=1):
    kw = dict(dimension_semantics=("arbitrary",) * n_axes)
    if vmem_bytes is not None:
        kw["vmem_limit_bytes"] = int(min(VMEM_CAP, max(32 * 1024 * 1024, vmem_bytes)))
    return pltpu.CompilerParams(**kw)


def _big(shape, dtype):
    return pltpu.HBM(tuple(shape), dtype)


def _in_hbm(args):
    return [pltpu.with_memory_space_constraint(a, pltpu.HBM) if a.size * a.dtype.itemsize >= (1 << 20) else a for a in args]


def _nbytes(shape, dtype):
    return math.prod(shape) * jnp.dtype(dtype).itemsize


def _row_tile(rows, width, cap_bytes=1 << 20, mult=8):
    best = None
    for t in range(mult, rows + 1, mult):
        if rows % t == 0 and t * width * 4 <= cap_bytes:
            best = t
    return best if best is not None else rows


_MODES = {"all8": (8, (1, 2, 3, 4, 5, 6, 7), 0), "chips": (4, (2, 4, 6), 1), "sibling": (2, (1,), 0)}


class Rider:
    def __init__(self, arrs, mode, scatter=False):
        self.arrs, self.scatter = list(arrs), scatter
        self.nslot, self.deltas, self.shift = _MODES[mode]
        self.n = len(self.arrs)
        self.out_shape = [jax.ShapeDtypeStruct((self.nslot,) + (a.shape[1:] if scatter else a.shape), a.dtype)
                          for a in self.arrs]
        any_spec = pl.BlockSpec(memory_space=pl.ANY)
        self.in_specs = [any_spec] * self.n
        self.out_specs = [any_spec] * self.n
        n_peer = len(self.deltas)
        self.scratch = [pltpu.SemaphoreType.DMA((self.n, n_peer)), pltpu.SemaphoreType.DMA((self.n, n_peer)),
                        pltpu.SemaphoreType.DMA((self.n,))]

    def _copies(self, ins, outs, sems, arrivals):
        send_sems, recv_sems, local_sems = sems
        x, y, c = lax.axis_index("x"), lax.axis_index("y"), lax.axis_index("c")
        me = 4 * x + 2 * y + c
        slot_of = lambda dev: (dev >> self.shift) & (self.nslot - 1)
        src = lambda a, slot: ins[a].at[slot] if self.scatter else ins[a]
        flip = lambda v, bit: 1 - v if bit else v

        def remote(a, k, d, from_slot, to_slot):
            return pltpu.make_async_remote_copy(
                src_ref=src(a, from_slot), dst_ref=outs[a].at[to_slot], send_sem=send_sems.at[a, k],
                recv_sem=recv_sems.at[a, k], device_id=(flip(x, (d >> 2) & 1), flip(y, (d >> 1) & 1), flip(c, d & 1)),
                device_id_type=MESH)

        mine = slot_of(me)
        local = [pltpu.make_async_copy(src(a, mine), outs[a].at[mine], local_sems.at[a]) for a in range(self.n)]
        sends = [remote(a, k, d, slot_of(me ^ d), mine) for k, d in enumerate(self.deltas) for a in range(self.n)]
        if not arrivals:
            return local, sends
        return local, sends, [remote(a, k, d, mine, slot_of(me ^ d)) for k, d in enumerate(self.deltas) for a in range(self.n)]

    def start(self, ins, outs, sems):
        local, sends = self._copies(ins, outs, sems, arrivals=False)
        for cp in local + sends:
            cp.start()

    def wait(self, ins, outs, sems):
        local, sends, recvs = self._copies(ins, outs, sems, arrivals=True)
        for cp in recvs:
            cp.wait_recv()
        for cp in sends:
            cp.wait_send()
        for cp in local:
            cp.wait()


class Riders:
    def __init__(self, riders):
        self.riders = list(riders)
        self.n = sum(r.n for r in self.riders)
        cat = lambda attr: [v for r in self.riders for v in getattr(r, attr)]
        self.arrs, self.out_shape, self.in_specs = cat("arrs"), cat("out_shape"), cat("in_specs")
        self.out_specs, self.scratch = cat("out_specs"), cat("scratch")

    def _each(self, method, ins, outs, sems):
        i = s = 0
        for r in self.riders:
            getattr(r, method)(ins[i:i + r.n], outs[i:i + r.n], sems[s:s + len(r.scratch)])
            i, s = i + r.n, s + len(r.scratch)

    def start(self, ins, outs, sems):
        self._each("start", ins, outs, sems)

    def wait(self, ins, outs, sems):
        self._each("wait", ins, outs, sems)


class _Hosted:
    def __init__(self, rider, n_in, n_out, n_scratch, grid):
        self.rider, self.n_in, self.n_out, self.n_scratch, self.grid = rider, n_in, n_out, n_scratch, grid
        self.n = rider.n if rider else 0

    def split(self, refs):
        a, b = self.n_in, self.n_in + self.n
        c, e = b + self.n_out, b + self.n_out + self.n
        self._r = (refs[a:b], refs[c:e], refs[e + self.n_scratch:])
        if self.rider:
            ids = [pl.program_id(ax) for ax in range(len(self.grid))]
            first = functools.reduce(jnp.logical_and, [i == 0 for i in ids]) if ids else True
            pl.when(first)(lambda: self.rider.start(*self._r))
        return refs[:a], refs[b:c], refs[e:e + self.n_scratch]

    def finish(self):
        if self.rider:
            ids = [pl.program_id(ax) for ax in range(len(self.grid))]
            last = functools.reduce(jnp.logical_and, [i == n - 1 for i, n in zip(ids, self.grid)]) if ids else True
            pl.when(last)(lambda: self.rider.wait(*self._r))

    def call_args(self, in_specs, out_shape, out_specs, scratch, args):
        r = self.rider
        if not r:
            return list(in_specs), tuple(out_shape), tuple(out_specs), list(scratch), list(args)
        return (list(in_specs) + r.in_specs, tuple(out_shape) + tuple(r.out_shape), tuple(out_specs) + tuple(r.out_specs),
                list(scratch) + r.scratch, list(args) + r.arrs)

    def results(self, res, unwrap=True):
        res = list(res) if isinstance(res, (tuple, list)) else [res]
        host = res[:self.n_out]
        host = host[0] if (self.n_out == 1 and unwrap) else tuple(host)
        return (host, res[self.n_out:]) if self.rider else host


def exchange_many(name, riders):
    both = Riders(riders)

    def body(*refs):
        ins, outs, sems = refs[:both.n], refs[both.n:2 * both.n], refs[2 * both.n:]
        both.start(ins, outs, sems)
        both.wait(ins, outs, sems)

    res = list(pl.pallas_call(
        body, name=name, out_shape=tuple(both.out_shape), in_specs=both.in_specs, out_specs=tuple(both.out_specs),
        scratch_shapes=both.scratch,
    )(*both.arrs))
    split = []
    for r in riders:
        split.append(res[:r.n])
        res = res[r.n:]
    return split


def exchange(name, arrs, mode, scatter=False):
    rider = Rider(arrs, mode, scatter)

    def body(*refs):
        ins, outs, sems = refs[:rider.n], refs[rider.n:2 * rider.n], refs[2 * rider.n:]
        rider.start(ins, outs, sems)
        rider.wait(ins, outs, sems)

    return pl.pallas_call(
        body, name=name, out_shape=tuple(rider.out_shape), in_specs=rider.in_specs, out_specs=tuple(rider.out_specs),
        scratch_shapes=rider.scratch,
    )(*arrs)


_DIMS = {"nn": (((1,), (0,)), ((), ())), "nt": (((1,), (1,)), ((), ())), "tn": (((0,), (0,)), ((), ()))}


def matmul(name, pairs, kind, *, a_ch=False, b_ch=False, out_ch=False, out_dtype=F32, rows=None, row_off=0, tm=512,
           rider=None, post=None, fold=False, place=None):
    a0, b0 = pairs[0]
    n_chunk = a0.shape[0] if a_ch else (b0.shape[0] if b_ch else 1)
    total_rows = a0.shape[-2]
    rows = total_rows - row_off if rows is None else rows
    tm = min(tm, rows)
    assert rows % tm == 0 and row_off % tm == 0, (name, rows, tm, row_off)
    n_rt, off = rows // tm, row_off // tm
    dims = _DIMS[kind]
    n_pair = len(pairs)

    if kind == "tn":
        grid, red_axis, n_red = (n_chunk, n_rt), 1, n_rt
        a_idx = (lambda k, i: (k, i + off, 0)) if a_ch else (lambda k, i: (i + off, 0))
        b_idx = (lambda k, i: (k, i + off, 0)) if b_ch else (lambda k, i: (i + off, 0))
        a_blk = lambda a: ((None, tm, a.shape[-1]) if a_ch else (tm, a.shape[-1]))
        b_blk = lambda b: ((None, tm, b.shape[-1]) if b_ch else (tm, b.shape[-1]))
        o2 = (a0.shape[-1], b0.shape[-1])
        out_shape = ((n_chunk,) + o2) if out_ch else o2
        out_spec = pl.BlockSpec((None,) + o2, lambda k, i: (k, 0, 0)) if out_ch else pl.BlockSpec(o2, lambda k, i: (0, 0))
        acc_shape = o2
    else:
        n_out = b0.shape[-1] if kind == "nn" else b0.shape[-2]
        b2 = b0.shape[-2:]
        if a_ch and b_ch and not out_ch and fold:
            grid, red_axis, n_red = (n_rt,), None, 1
            a_idx, b_idx = (lambda i: (0, i + off, 0)), (lambda i: (0, 0, 0))
            a_blk = lambda a: (n_chunk, tm, a.shape[-1])
            b_blk = lambda b: tuple(b.shape)
            out_shape, out_spec = (rows, n_out), pl.BlockSpec((tm, n_out), lambda i: (i, 0))
        elif a_ch and b_ch and not out_ch:
            grid, red_axis, n_red = (n_rt, n_chunk), 1, n_chunk
            a_idx, b_idx = (lambda i, k: (k, i + off, 0)), (lambda i, k: (k, 0, 0))
            a_blk = lambda a: (None, tm, a.shape[-1])
            b_blk = lambda b: (None,) + tuple(b.shape[-2:])
            out_shape, out_spec = (rows, n_out), pl.BlockSpec((tm, n_out), lambda i, k: (i, 0))
        elif out_ch and fold:
            assert b_ch and not a_ch and all(a is a0 for a, _ in pairs)
            grid, red_axis, n_red = (n_rt,), None, 1
            a_idx, b_idx = (lambda i: (i + off, 0)), (lambda i: (0, 0, 0))
            a_blk = lambda a: (tm, a.shape[-1])
            b_blk = lambda b: tuple(b.shape)
            out_shape, out_spec = (n_chunk, rows, n_out), pl.BlockSpec((n_chunk, tm, n_out), lambda i: (0, i, 0))
        elif out_ch:
            assert b_ch and not a_ch
            grid, red_axis, n_red = (n_chunk, n_rt), None, 1
            a_idx, b_idx = (lambda k, i: (i + off, 0)), (lambda k, i: (k, 0, 0))
            a_blk = lambda a: (tm, a.shape[-1])
            b_blk = lambda b: (None,) + tuple(b.shape[-2:])
            out_shape, out_spec = (n_chunk, rows, n_out), pl.BlockSpec((None, tm, n_out), lambda k, i: (k, i, 0))
        else:
            assert not (a_ch or b_ch)
            grid, red_axis, n_red = (n_rt,), None, 1
            a_idx, b_idx = (lambda i: (i + off, 0)), (lambda i: (0, 0))
            a_blk = lambda a: (tm, a.shape[-1])
            b_blk = lambda b: tuple(b.shape)
            out_shape, out_spec = (rows, n_out), pl.BlockSpec((tm, n_out), lambda i: (i, 0))
            if place is not None:
                out_shape, o_off = (place[0], n_out), place[1] // tm
                out_spec = pl.BlockSpec((tm, n_out), lambda i: (i + o_off, 0))
        acc_shape = (tm, n_out)

    into = [] if place is None or place[2] is None else [place[2]]
    post_ins, post_fn, out_dtypes = ([], None, [out_dtype]) if post is None else post
    hosted = _Hosted(rider, 2 * n_pair + len(post_ins) + len(into), len(out_dtypes), int(n_red > 1), grid)

    def body(*refs):
        ins, outs, scr = hosted.split(refs)

        def compute():
            acc = None
            for p in range(n_pair):
                for k in ([None] if not fold else range(n_chunk)):
                    pick = (lambda r: r[...]) if k is None else (lambda r: r[k])
                    d = lax.dot_general(pick(ins[2 * p]).astype(BF16), pick(ins[2 * p + 1]).astype(BF16), dims,
                                        preferred_element_type=F32)
                    acc = d if acc is None else acc + d
            return acc

        def emit(acc):
            vals = (acc,) if post_fn is None else post_fn(
                acc, *[r[...].astype(F32) for r in ins[2 * n_pair:2 * n_pair + len(post_ins)]])
            for o_ref, v in zip(outs, vals):
                o_ref[...] = v.astype(o_ref.dtype)

        if out_ch and fold:
            a_tile = ins[0][...].astype(BF16)
            for k in range(n_chunk):
                accs = [lax.dot_general(a_tile, ins[2 * p + 1][k].astype(BF16), dims, preferred_element_type=F32)
                        for p in range(n_pair)]
                tiles = [r[k].astype(F32) for r in ins[2 * n_pair:2 * n_pair + len(post_ins)]]
                vals = tuple(accs) if post_fn is None else post_fn(*accs, *tiles)
                for o_ref, v in zip(outs, vals):
                    o_ref[k] = v.astype(o_ref.dtype)
        elif n_red == 1:
            emit(compute())
        else:
            acc_ref = scr[0]
            r = pl.program_id(red_axis)

            @pl.when(r == 0)
            def _():
                acc_ref[...] = jnp.zeros_like(acc_ref)

            acc_ref[...] += compute()

            @pl.when(r == n_red - 1)
            def _():
                emit(acc_ref[...])
        hosted.finish()

    in_specs, args, vmem = [], [], 0
    for a, b in pairs:
        in_specs += [pl.BlockSpec(a_blk(a), a_idx), pl.BlockSpec(b_blk(b), b_idx)]
        args += [a, b]
        vmem += 2 * (_nbytes([s for s in a_blk(a) if s], a.dtype) + _nbytes([s for s in b_blk(b) if s], b.dtype))
    in_specs += [out_spec] * len(post_ins)
    args += list(post_ins)
    aliases = {len(args): 0} if into else {}
    in_specs += [pl.BlockSpec(memory_space=pl.ANY)] * len(into)
    args += into
    tiles_per_step = n_chunk if (out_ch and fold) else 1
    vmem += (3 + 2 * n_pair + tiles_per_step * (len(post_ins) + len(out_dtypes))) * _nbytes(acc_shape, F32)
    scratch = [pltpu.VMEM(acc_shape, F32)] if n_red > 1 else []
    in_specs, out_shapes, out_specs, scratch, args = hosted.call_args(
        in_specs, [_big(out_shape, dt) for dt in out_dtypes], [out_spec] * len(out_dtypes), scratch, args)
    return hosted.results(pl.pallas_call(
        body, name=name, out_shape=out_shapes, grid=grid, in_specs=in_specs, out_specs=out_specs,
        input_output_aliases=aliases, scratch_shapes=scratch, compiler_params=_params(vmem + (8 << 20), len(grid)),
    )(*_in_hbm(args)))


def row(arr, width=None, cb=0, roff=0):
    return (arr, arr.shape[-1] if width is None else width, cb, roff)


def two_rows(first, second, limit):
    return (first, first.shape[-1], 0, 0, (second, limit))


def _row_inputs(rows, tm):
    specs, arrs, slots = [], [], []
    for d in rows:
        second, limit = d[4] if len(d) > 4 else (None, None)
        slots.append((len(arrs), limit))
        specs.append(_row_spec(d[:4], tm, limit))
        arrs.append(d[0])
        if second is not None:
            specs.append(pl.BlockSpec((tm, d[1]), lambda i, limit=limit: (jnp.maximum(i - limit, 0), 0)))
            arrs.append(second)

    def read(refs, i):
        vals = []
        for at, limit in slots:
            v = refs[at][...].astype(F32)
            vals.append(v if limit is None else jnp.where(i < limit, v, refs[at + 1][...].astype(F32)))
        return vals

    return specs, arrs, read


def _row_spec(desc, tm, limit=None):
    _, width, cb, roff = desc[:4]
    if limit is None:
        return pl.BlockSpec((tm, width), lambda i: (i + roff, cb))
    return pl.BlockSpec((tm, width), lambda i: (jnp.minimum(i, limit - 1) + roff, cb))


def _segmenter(tm, seq_len, n_lat):
    seg = lambda i: jnp.where(i * tm < n_lat, (i * tm) // seq_len, n_lat // seq_len)
    first = lambda i: jnp.where(i * tm < n_lat, (i * tm) % seq_len == 0, i * tm == n_lat)
    return seg, first


def rowwise(name, fn, rows, segs, params, outs, *, tm, n_tiles, seg_fn=None, rider=None):
    row_specs, row_arrs, read_rows = _row_inputs(rows, tm)
    n_r, n_s, n_p = len(row_arrs), len(segs), len(params)
    hosted = _Hosted(rider, n_r + n_s + n_p, len(outs), 0, (n_tiles,))

    def body(*refs):
        ins, out_refs, _ = hosted.split(refs)
        vals = read_rows(ins[:n_r], pl.program_id(0)) + [r[...] for r in ins[n_r:]]
        res = fn(*vals)
        for o_ref, v in zip(out_refs, res):
            o_ref[...] = v.astype(o_ref.dtype)
        hosted.finish()

    in_specs = list(row_specs)
    in_specs += [pl.BlockSpec((None, 1, s.shape[-1]), lambda i: (seg_fn(i), 0, 0)) for s in segs]
    in_specs += [pl.BlockSpec(p.shape, lambda i: (0, 0)) for p in params]
    vmem = sum(2 * tm * d[1] * 4 for d in rows) + sum(3 * tm * w * 4 for _, w, _ in outs) + sum(2 * p.size * 4 for p in params)
    in_specs, out_shapes, out_specs, scratch, args = hosted.call_args(
        in_specs, [_big((r, w), dt) for r, w, dt in outs],
        [pl.BlockSpec((tm, w), lambda i: (i, 0)) for _, w, _ in outs], [], row_arrs + list(segs) + list(params))
    return hosted.results(pl.pallas_call(
        body, name=name, grid=(n_tiles,), in_specs=in_specs, out_shape=out_shapes, out_specs=out_specs,
        scratch_shapes=scratch, compiler_params=_params(2 * vmem + (8 << 20)),
    )(*_in_hbm(args)), unwrap=False)


def rowwise_bwd(name, fn, rows, segs, params, cts, row_grads, *, tm, n_tiles, seg_fn=None, first_fn=None, adds=None,
                rider=None):
    adds = adds or {}
    need = [k for k, v in enumerate(row_grads) if v is not None]
    row_specs, row_arrs, read_rows = _row_inputs(rows, tm)
    n_r, n_s, n_p = len(row_arrs), len(segs), len(params)
    n_ct = sum(len(lst) for lst in cts)
    add_keys = sorted(adds)
    hosted = _Hosted(rider, n_r + n_s + n_p + n_ct + len(add_keys), len(need) + n_s + n_p, 0, (n_tiles,))

    def body(*refs):
        host_in, host_out, _ = hosted.split(refs)
        it = iter(list(host_in) + list(host_out))
        row_refs = [next(it) for _ in range(n_r)]
        seg_refs = [next(it) for _ in range(n_s)]
        par_refs = [next(it) for _ in range(n_p)]
        ct_refs = [[next(it) for _ in lst] for lst in cts]
        add_refs = {k: next(it) for k in add_keys}
        rg_refs = {k: next(it) for k in need}
        sg_refs = [next(it) for _ in range(n_s)]
        pg_refs = [next(it) for _ in range(n_p)]
        i = pl.program_id(0)
        rv = read_rows(row_refs, i)
        sv = [r[...] for r in seg_refs]
        pv = [r[...] for r in par_refs]

        def f(*args):
            rr = list(rv)
            for j, k in enumerate(need):
                rr[k] = args[j]
            return fn(*rr, *args[len(need):])

        _, vjp = jax.vjp(f, *[rv[k] for k in need], *sv, *pv)
        ctv = []
        for lst in ct_refs:
            acc = lst[0][...].astype(F32)
            for r in lst[1:]:
                acc = acc + r[...].astype(F32)
            ctv.append(acc)
        g = vjp(tuple(ctv))
        for j, k in enumerate(need):
            gv = g[j]
            if k in adds:
                lim = adds[k][1]
                av = add_refs[k][...].astype(F32)
                gv = gv + (av if lim is None else jnp.where(i < lim, av, 0.0))
            lim = row_grads[k][2]
            if lim is None:
                rg_refs[k][...] = gv.astype(rg_refs[k].dtype)
            else:
                @pl.when(i < lim)
                def _(gv=gv, k=k):
                    rg_refs[k][...] = gv.astype(rg_refs[k].dtype)
        if n_s:
            opens = first_fn(i)
            for ref, gv in zip(sg_refs, g[len(need):len(need) + n_s]):
                @pl.when(opens)
                def _(ref=ref, gv=gv):
                    ref[...] = gv

                @pl.when(jnp.logical_not(opens))
                def _(ref=ref, gv=gv):
                    ref[...] += gv
        for ref, gv in zip(pg_refs, g[len(need) + n_s:]):
            @pl.when(i == 0)
            def _(ref=ref, gv=gv):
                ref[...] = gv

            @pl.when(i > 0)
            def _(ref=ref, gv=gv):
                ref[...] += gv
        hosted.finish()

    seg_spec = lambda s: pl.BlockSpec((None, 1, s.shape[-1]), lambda i: (seg_fn(i), 0, 0))
    par_spec = lambda p: pl.BlockSpec(p.shape, lambda i: (0, 0))
    in_specs = list(row_specs) + [seg_spec(s) for s in segs] + [par_spec(p) for p in params]
    args = row_arrs + list(segs) + list(params)
    for lst in cts:
        in_specs += [_row_spec(d, tm) for d in lst]
        args += [d[0] for d in lst]
    for k in add_keys:
        in_specs.append(_row_spec(adds[k][0], tm, adds[k][1]))
        args.append(adds[k][0][0])
    out_shape, out_specs = [], []
    for k in need:
        n_rows, dt, lim = row_grads[k]
        out_shape.append(_big((n_rows, rows[k][1]), dt))
        out_specs.append(_row_spec((None, rows[k][1], 0, 0), tm, lim))
    for s in segs:
        out_shape.append(jax.ShapeDtypeStruct(s.shape, F32))
        out_specs.append(seg_spec(s))
    for p in params:
        out_shape.append(jax.ShapeDtypeStruct(p.shape, F32))
        out_specs.append(par_spec(p))
    vmem = sum(tm * d[1] * 4 for d in rows) * 6 + n_ct * tm * max(d[1] for d in rows) * 8
    in_specs, out_shape, out_specs, scratch, args = hosted.call_args(in_specs, out_shape, out_specs, [], args)
    return hosted.results(pl.pallas_call(
        body, name=name, grid=(n_tiles,), in_specs=in_specs, out_shape=out_shape, out_specs=out_specs,
        scratch_shapes=scratch, compiler_params=_params(vmem + (8 << 20)),
    )(*_in_hbm(args)), unwrap=False)


def _silu(v):
    return v * jax.nn.sigmoid(v)


def _rms(v, w):
    return v * lax.rsqrt(jnp.mean(v * v, axis=-1, keepdims=True) + EPS) * w


def fn_norm_mod(x, shift, scale, w):
    return (_rms(x, w) * (1.0 + scale) + shift,)


def fn_act(g, u):
    return (_silu(g) * u,)


def make_fn_resid(coef):
    def fn(x, f, gate):
        return (x + coef * gate * f,)
    return fn


def fn_silu_bias(v, b):
    return (_silu(v + b),)


def make_fn_gate_groupnorm(width):
    half = width // 2

    def fn(y_both, z, w):
        y = y_both * _silu(z)
        lane = lax.broadcasted_iota(jnp.int32, y.shape, 1)
        lo = lane < half
        sq = y * y
        s_lo = jnp.sum(jnp.where(lo, sq, 0.0), axis=-1, keepdims=True)
        s_hi = jnp.sum(jnp.where(lo, 0.0, sq), axis=-1, keepdims=True)
        r = jnp.where(lo, lax.rsqrt(s_lo / half + EPS), lax.rsqrt(s_hi / half + EPS))
        return (y * r * w,)
    return fn


def fn_glu(a, b):
    return (a * jax.nn.sigmoid(b),)


def fn_ln_silu(vw, vh, cb, lw, lb):
    v = jnp.concatenate([vw, vh], axis=-1) + cb
    mu = jnp.mean(v, axis=-1, keepdims=True)
    var = jnp.mean(jnp.square(v - mu), axis=-1, keepdims=True)
    return (_silu((v - mu) * lax.rsqrt(var + EPS) * lw + lb),)


def _col_tile(width):
    return width // 3 if width % (3 * LANES) == 0 else width


def mod_fwd(a_rows, w_shard, b_shard):
    n, d = a_rows.shape
    ws = w_shard.shape[1]
    tn = _col_tile(ws)

    def body(a_ref, w_ref, b_ref, o_ref):
        a = _silu(a_ref[...]).astype(BF16)
        o_ref[...] = jnp.dot(a, w_ref[...].astype(BF16), preferred_element_type=F32) + b_ref[...]

    return pl.pallas_call(
        body, name="mod_fwd", grid=(ws // tn,), out_shape=jax.ShapeDtypeStruct((n, ws), F32),
        in_specs=[pl.BlockSpec((n, d), lambda j: (0, 0)), pl.BlockSpec((d, tn), lambda j: (0, j)),
                  pl.BlockSpec((1, tn), lambda j: (0, j))],
        out_specs=pl.BlockSpec((n, tn), lambda j: (0, j)), compiler_params=_params(),
    )(a_rows, w_shard, b_shard)


def mod_bwd(a_rows, d_shard, d_full, w_shard, ctx_rows):
    n, d = a_rows.shape
    ws = w_shard.shape[1]
    tn = _col_tile(ws)
    n_ct = ws // tn

    def body(a_ref, ds_ref, df_ref, w_ref, gw_ref, gb_ref, q_ref):
        j = pl.program_id(0)
        a = _silu(a_ref[...])
        ds = ds_ref[...]
        gw_ref[...] = lax.dot_general(a, ds, _DIMS["tn"], precision=HI, preferred_element_type=F32)
        dctx = ds[ctx_rows[0]:ctx_rows[0] + 1, :]
        for r in ctx_rows[1:]:
            dctx = dctx + ds[r:r + 1, :]
        q = lax.dot_general(jnp.broadcast_to(dctx, (8, tn)), w_ref[...], _DIMS["nt"], precision=HI,
                            preferred_element_type=F32)

        @pl.when(j == 0)
        def _():
            q_ref[...] = q
            df = df_ref[...]
            acc = df[0:1, :]
            for r in range(1, n):
                acc = acc + df[r:r + 1, :]
            gb_ref[...] = acc

        @pl.when(j > 0)
        def _():
            q_ref[...] += q

    return pl.pallas_call(
        body, name="mod_bwd", grid=(n_ct,),
        out_shape=(jax.ShapeDtypeStruct((d, ws), F32), jax.ShapeDtypeStruct((1, d_full.shape[1]), F32),
                   jax.ShapeDtypeStruct((8, d), F32)),
        in_specs=[pl.BlockSpec((n, d), lambda j: (0, 0)), pl.BlockSpec((n, tn), lambda j: (0, j)),
                  pl.BlockSpec(d_full.shape, lambda j: (0, 0)), pl.BlockSpec((d, tn), lambda j: (0, j))],
        out_specs=(pl.BlockSpec((d, tn), lambda j: (0, j)), pl.BlockSpec((1, d_full.shape[1]), lambda j: (0, 0)),
                   pl.BlockSpec((8, d), lambda j: (0, 0))),
        compiler_params=_params(40 << 20),
    )(a_rows, d_shard, d_full, w_shard)


def _shifted(xs, d, tok, width):
    if d == 0:
        return xs
    n = xs.shape[0]
    sh = pltpu.roll(xs, (-d) % n, axis=0)
    return jnp.where((tok + d >= 0) & (tok + d < width), sh, 0.0)


def _placed(out_shape, place):
    if place is None:
        return out_shape, 0, 0, None
    return place


def tapsum_roll(name, x, xcb, w, wcb, *, seq_len, n_seq, row_blk_off, width, piece, cb, ncb, pad, flip, place=None):
    n_tap = w.shape[0]
    n_piece = seq_len // piece
    out_shape, o_rb, o_cb, into = _placed((n_seq * seq_len, ncb * cb), place)

    def body(x_ref, w_ref, *rest):
        o_ref = rest[-1]
        wv = w_ref[...]
        tok = lax.broadcasted_iota(jnp.int32, (piece, 1), 0) % width

        def do_piece(p, carry):
            start = pl.multiple_of(p * piece, piece)
            xs = x_ref[pl.ds(start, piece), :].astype(F32)
            acc = jnp.zeros_like(xs)
            for k in range(n_tap):
                d = pad - k if flip else k - pad
                acc = acc + wv[k:k + 1, :] * _shifted(xs, d, tok, width)
            o_ref[pl.ds(start, piece), :] = acc
            return carry

        lax.fori_loop(0, n_piece, do_piece, 0)

    extra = [] if into is None else [into]
    return pl.pallas_call(
        body, name=name, grid=(ncb, n_seq), out_shape=_big(out_shape, F32),
        in_specs=[pl.BlockSpec((seq_len, cb), lambda j, s: (row_blk_off + s, xcb + j)),
                  pl.BlockSpec((n_tap, cb), lambda j, s: (0, wcb + j))] + [pl.BlockSpec(memory_space=pl.ANY)] * len(extra),
        out_specs=pl.BlockSpec((seq_len, cb), lambda j, s: (o_rb + s, o_cb + j)),
        input_output_aliases={2: 0} if extra else {},
        compiler_params=_params(8 * seq_len * cb * 4 + (8 << 20), 2),
    )(*_in_hbm([x, w] + extra))


def tapgrad_roll(name, dy, dycb, dy_blk_off, x, xcb, x_blk_off, *, n_tap, seq_len, n_seq, width, piece, cb, ncb, pad):
    n_piece = seq_len // piece

    def body(dy_ref, x_ref, o_ref):
        @pl.when(pl.program_id(1) == 0)
        def _():
            o_ref[...] = jnp.zeros_like(o_ref)

        tok = lax.broadcasted_iota(jnp.int32, (piece, 1), 0) % width

        def do_piece(p, carry):
            start = pl.multiple_of(p * piece, piece)
            xs = x_ref[pl.ds(start, piece), :].astype(F32)
            dv = dy_ref[pl.ds(start, piece), :]
            for k in range(n_tap):
                o_ref[k:k + 1, :] += jnp.sum(dv * _shifted(xs, k - pad, tok, width), axis=0, keepdims=True)
            return carry

        lax.fori_loop(0, n_piece, do_piece, 0)

    return pl.pallas_call(
        body, name=name, grid=(ncb, n_seq), out_shape=jax.ShapeDtypeStruct((n_tap, ncb * cb), F32),
        in_specs=[pl.BlockSpec((seq_len, cb), lambda j, s: (dy_blk_off + s, dycb + j)),
                  pl.BlockSpec((seq_len, cb), lambda j, s: (x_blk_off + s, xcb + j))],
        out_specs=pl.BlockSpec((n_tap, cb), lambda j, s: (0, j)),
        compiler_params=_params(8 * seq_len * cb * 4 + (8 << 20), 2),
    )(*_in_hbm([dy, x]))


def tapsum_rows(name, x, xcb, w, wcb, *, seq_len, n_seq, cb, ncb, pad, flip, place=None):
    n_tap = w.shape[0]
    n_row = seq_len // GRID_W
    halo = pad * GRID_W
    out_shape, o_rb, o_cb, into = _placed((n_seq * seq_len, ncb * cb), place)

    def body(x_ref, w_ref, *rest):
        o_ref, xp = rest[-2:]
        xp[pl.ds(0, halo), :] = jnp.zeros((halo, cb), F32)
        xp[pl.ds(halo + seq_len, halo), :] = jnp.zeros((halo, cb), F32)
        xp[pl.ds(halo, seq_len), :] = x_ref[...]
        wv = w_ref[...]

        def do_row(r, carry):
            acc = jnp.zeros((GRID_W, cb), F32)
            for k in range(n_tap):
                d = pad - k if flip else k - pad
                acc = acc + wv[k:k + 1, :] * xp[pl.ds(pl.multiple_of((r + pad + d) * GRID_W, GRID_W), GRID_W), :]
            o_ref[pl.ds(pl.multiple_of(r * GRID_W, GRID_W), GRID_W), :] = acc
            return carry

        lax.fori_loop(0, n_row, do_row, 0)

    extra = [] if into is None else [into]
    return pl.pallas_call(
        body, name=name, grid=(ncb, n_seq), out_shape=_big(out_shape, F32),
        in_specs=[pl.BlockSpec((seq_len, cb), lambda j, s: (s, xcb + j)),
                  pl.BlockSpec((n_tap, cb), lambda j, s: (0, wcb + j))] + [pl.BlockSpec(memory_space=pl.ANY)] * len(extra),
        out_specs=pl.BlockSpec((seq_len, cb), lambda j, s: (o_rb + s, o_cb + j)),
        input_output_aliases={2: 0} if extra else {},
        scratch_shapes=[pltpu.VMEM((seq_len + 2 * halo, cb), F32)],
        compiler_params=_params(10 * seq_len * cb * 4 + (8 << 20), 2),
    )(*_in_hbm([x, w] + extra))


def tapgrad_rows(name, dy, dycb, x, xcb, *, n_tap, seq_len, n_seq, cb, ncb, pad):
    n_row = seq_len // GRID_W
    halo = pad * GRID_W

    def body(dy_ref, x_ref, o_ref, xp):
        @pl.when(pl.program_id(1) == 0)
        def _():
            o_ref[...] = jnp.zeros_like(o_ref)

        xp[pl.ds(0, halo), :] = jnp.zeros((halo, cb), F32)
        xp[pl.ds(halo + seq_len, halo), :] = jnp.zeros((halo, cb), F32)
        xp[pl.ds(halo, seq_len), :] = x_ref[...]

        def do_row(r, carry):
            dv = dy_ref[pl.ds(pl.multiple_of(r * GRID_W, GRID_W), GRID_W), :]
            for k in range(n_tap):
                xs = xp[pl.ds(pl.multiple_of((r + k) * GRID_W, GRID_W), GRID_W), :]
                o_ref[k:k + 1, :] += jnp.sum(dv * xs, axis=0, keepdims=True)
            return carry

        lax.fori_loop(0, n_row, do_row, 0)

    return pl.pallas_call(
        body, name=name, grid=(ncb, n_seq), out_shape=jax.ShapeDtypeStruct((n_tap, ncb * cb), F32),
        in_specs=[pl.BlockSpec((seq_len, cb), lambda j, s: (s, dycb + j)),
                  pl.BlockSpec((seq_len, cb), lambda j, s: (s, xcb + j))],
        out_specs=pl.BlockSpec((n_tap, cb), lambda j, s: (0, j)),
        scratch_shapes=[pltpu.VMEM((seq_len + 2 * halo, cb), F32)],
        compiler_params=_params(10 * seq_len * cb * 4 + (8 << 20), 2),
    )(*_in_hbm([dy, x]))


def _ssd_blocks(b, s, *, rev, n_ctx, n_lat, lat_blocks):
    if rev:
        return jnp.where(s < n_ctx, lat_blocks + b * n_ctx + (n_ctx - 1 - s), b * n_lat + (n_lat - 1 - (s - n_ctx)))
    return jnp.where(s < n_ctx, lat_blocks + b * n_ctx + s, b * n_lat + (s - n_ctx))


def _ssd_common(xbc, raw, dtb, alog, dsk, *, rev, ds, n_head):
    if rev:
        raw = pltpu.roll(raw, LANES - n_head, axis=1)
    pre = raw + dtb
    dt = jnp.maximum(pre, 0.0) + jnp.log(1.0 + jnp.exp(-jnp.abs(pre)))
    sig = jax.nn.sigmoid(pre)
    a = -jnp.exp(alog)
    da = dt * a
    ri = lax.broadcasted_iota(jnp.int32, (CHUNK, CHUNK), 0)
    ci = lax.broadcasted_iota(jnp.int32, (CHUNK, CHUNK), 1)
    mask = (ci >= ri) if rev else (ci <= ri)
    tri = mask.astype(F32)
    tri_t = ((ci <= ri) if rev else (ci >= ri)).astype(F32)
    cs = jnp.dot(tri, da, precision=HI, preferred_element_type=F32)
    tot = jnp.sum(da, axis=0, keepdims=True)
    def wide(v):
        first = lax.broadcasted_iota(jnp.int32, (v.shape[0], LANES), 1) < HEAD_DIM
        return jnp.concatenate(
            [jnp.where(first, jnp.broadcast_to(v[:, 2 * p:2 * p + 1], first.shape),
                       jnp.broadcast_to(v[:, 2 * p + 1:2 * p + 2], first.shape)) for p in range(n_head // 2)], axis=1)

    cs_w, tot_w = wide(cs), wide(tot)
    xh = xbc[:, :ds]
    dt_w = wide(dt)
    return dict(
        dt=dt, sig=sig, a=a, cs=cs, cs_t=cs.T, tot=tot, mask=mask, tri_t=tri_t,
        e_w=jnp.exp(cs_w), wt_w=jnp.exp(tot_w - cs_w), dec_w=jnp.exp(tot_w), dt_w=dt_w, dsk_w=wide(dsk),
        xh=xh, xs_w=xh * dt_w, bm=xbc[:, ds:ds + 2 * N_STATE], cm=xbc[:, ds + 2 * N_STATE:ds + 4 * N_STATE])


def _decay(q, col):
    seg = q["cs"][:, col:col + 1] - q["cs_t"][col:col + 1, :]
    return jnp.exp(jnp.where(q["mask"], seg, -jnp.inf))


def _split_heads(v):
    lane = lax.broadcasted_iota(jnp.int32, v.shape, 1)
    return jnp.concatenate([jnp.where(lane < HEAD_DIM, v, 0.0), jnp.where(lane >= HEAD_DIM, v, 0.0)], axis=0)


def ssd_fwd(name, xbc, dt_raw, dt_cb, dtb, alog, dsk, *, rev, n_ex, seq_len, ctx_len, ds, rider=None, add=None):
    n_head, half = ds // HEAD_DIM, ds // 2
    n_ctx, n_lat = ctx_len // CHUNK, seq_len // CHUNK
    n_step = n_ctx + n_lat
    blk = functools.partial(_ssd_blocks, rev=rev, n_ctx=n_ctx, n_lat=n_lat, lat_blocks=n_ex * n_lat)
    xw = xbc.shape[1]

    def y_blk(b, s):
        sl = jnp.maximum(s, n_ctx) - n_ctx
        return b * n_lat + ((n_lat - 1 - sl) if rev else sl)

    hosted = _Hosted(rider, 5 + (add is not None), 2, 1, (n_ex, n_step))

    def body(*refs):
        (xbc_ref, dt_ref, dtb_ref, alog_ref, dsk_ref, *add_ref), (y_ref, hs_ref), (h_scr,) = hosted.split(refs)

        @pl.when(pl.program_id(1) == 0)
        def _():
            h_scr[...] = jnp.zeros_like(h_scr)

        q = _ssd_common(xbc_ref[...], dt_ref[...], dtb_ref[...], alog_ref[...], dsk_ref[...], rev=rev, ds=ds, n_head=n_head)
        h = h_scr[...]
        hs_ref[...] = h
        for g in range(2):
            lo = g * half
            bg = q["bm"][:, g * N_STATE:(g + 1) * N_STATE].astype(BF16)
            cg = q["cm"][:, g * N_STATE:(g + 1) * N_STATE].astype(BF16)
            scores = lax.dot_general(cg, bg, _DIMS["nt"], preferred_element_type=F32)
            hg = h[:, lo:lo + half]
            off = jnp.dot(cg, hg.astype(BF16), preferred_element_type=F32)
            for j in range(half // LANES):
                c0 = (lo + j * LANES) // HEAD_DIM
                ln = slice(lo + j * LANES, lo + (j + 1) * LANES)
                p_cat = jnp.concatenate([scores * _decay(q, c0), scores * _decay(q, c0 + 1)], axis=1).astype(BF16)
                diag = jnp.dot(p_cat, _split_heads(q["xs_w"][:, ln]).astype(BF16), preferred_element_type=F32)
                y_ref[:, ln] = (diag + q["e_w"][:, ln] * off[:, j * LANES:(j + 1) * LANES]
                                + q["dsk_w"][:, ln] * q["xh"][:, ln] + (add_ref[0][:, ln] if add_ref else 0.0))
            v = (q["wt_w"][:, lo:lo + half] * q["xs_w"][:, lo:lo + half]).astype(BF16)
            h_scr[:, lo:lo + half] = (q["dec_w"][:, lo:lo + half] * hg
                                      + lax.dot_general(bg, v, _DIMS["tn"], preferred_element_type=F32))
        hosted.finish()

    vec = pl.BlockSpec((1, LANES), lambda b, s: (0, 0))
    in_specs, out_shape, out_specs, scratch, args = hosted.call_args(
        [pl.BlockSpec((CHUNK, xw), lambda b, s: (blk(b, s), 0)),
         pl.BlockSpec((CHUNK, LANES), lambda b, s: (blk(b, s), dt_cb)), vec, vec, vec]
        + [pl.BlockSpec((CHUNK, ds), lambda b, s: (y_blk(b, s), 0))] * (add is not None),
        (_big((n_ex * seq_len, ds), F32), _big((n_ex, n_step, N_STATE, ds), F32)),
        (pl.BlockSpec((CHUNK, ds), lambda b, s: (y_blk(b, s), 0)),
         pl.BlockSpec((None, None, N_STATE, ds), lambda b, s: (b, s, 0, 0))),
        [pltpu.VMEM((N_STATE, ds), F32)], [xbc, dt_raw, dtb, alog, dsk] + ([] if add is None else [add]))
    return hosted.results(pl.pallas_call(
        body, name=name, grid=(n_ex, n_step), out_shape=out_shape, in_specs=in_specs, out_specs=out_specs,
        scratch_shapes=scratch, compiler_params=_params(40 << 20, 2),
    )(*_in_hbm(args)))


def ssd_bwd(name, xbc, dt_raw, dt_cb, hs, dy, dtb, alog, dsk, *, rev, n_ex, seq_len, ctx_len, ds, rider=None, add=None):
    n_head, half = ds // HEAD_DIM, ds // 2
    n_ctx, n_lat = ctx_len // CHUNK, seq_len // CHUNK
    n_step = n_ctx + n_lat
    n_tok = n_ex * (seq_len + ctx_len)
    blk0 = functools.partial(_ssd_blocks, rev=rev, n_ctx=n_ctx, n_lat=n_lat, lat_blocks=n_ex * n_lat)
    step = lambda sp: n_step - 1 - sp
    blk = lambda b, sp: blk0(b, step(sp))
    xw = xbc.shape[1]

    def dy_blk(b, sp):
        sl = jnp.maximum(step(sp), n_ctx) - n_ctx
        return b * n_lat + ((n_lat - 1 - sl) if rev else sl)

    hosted = _Hosted(rider, 7 + (add is not None), 5, 1, (n_ex, n_step))

    def body(*refs):
        ((xbc_ref, dt_ref, hs_ref, dy_ref, dtb_ref, alog_ref, dsk_ref, *add_ref),
         (dxbc_ref, ddt_ref, dalog_ref, ddtb_ref, ddsk_ref), (dh_scr,)) = hosted.split(refs)
        b, sp = pl.program_id(0), pl.program_id(1)
        more = (lambda cols: add_ref[0][:, cols]) if add_ref else (lambda cols: 0.0)

        @pl.when(sp == 0)
        def _():
            dh_scr[...] = jnp.zeros_like(dh_scr)

        @pl.when((sp == 0) & (b == 0))
        def _():
            dalog_ref[...] = jnp.zeros_like(dalog_ref)
            ddtb_ref[...] = jnp.zeros_like(ddtb_ref)
            ddsk_ref[...] = jnp.zeros_like(ddsk_ref)

        q = _ssd_common(xbc_ref[...], dt_ref[...], dtb_ref[...], alog_ref[...], dsk_ref[...], rev=rev, ds=ds, n_head=n_head)
        h = hs_ref[...]
        d_y = jnp.where(step(sp) >= n_ctx, dy_ref[...], 0.0)
        dh_next = dh_scr[...]
        lane_row = lax.broadcasted_iota(jnp.int32, (1, LANES), 1)
        d_cs = jnp.zeros((CHUNK, LANES), F32)
        dxs_parts, de_parts, dwt_parts, ddec_parts = [], [], [], []
        for g in range(2):
            lo = g * half
            gs = slice(lo, lo + half)
            bg = q["bm"][:, g * N_STATE:(g + 1) * N_STATE].astype(BF16)
            cg = q["cm"][:, g * N_STATE:(g + 1) * N_STATE].astype(BF16)
            scores = lax.dot_general(cg, bg, _DIMS["nt"], preferred_element_type=F32)
            hg, dyg, dhn = h[:, gs], d_y[:, gs], dh_next[:, gs]
            off = jnp.dot(cg, hg.astype(BF16), preferred_element_type=F32)
            d_off = (q["e_w"][:, gs] * dyg).astype(BF16)
            de_parts.append(dyg * off)
            d_c = lax.dot_general(d_off, hg.astype(BF16), _DIMS["nt"], preferred_element_type=F32)
            dh_scr[:, gs] = (lax.dot_general(cg, d_off, _DIMS["tn"], preferred_element_type=F32)
                             + q["dec_w"][:, gs] * dhn)
            b_dh = jnp.dot(bg, dhn.astype(BF16), preferred_element_type=F32)
            v = q["wt_w"][:, gs] * q["xs_w"][:, gs]
            d_b = lax.dot_general(v.astype(BF16), dhn.astype(BF16), _DIMS["nt"], preferred_element_type=F32)
            dwt_parts.append(q["xs_w"][:, gs] * b_dh)
            ddec_parts.append(jnp.sum(hg * dhn, axis=0, keepdims=True))
            d_scores = jnp.zeros((CHUNK, CHUNK), F32)
            for j in range(half // LANES):
                c0 = (lo + j * LANES) // HEAD_DIM
                ln = slice(lo + j * LANES, lo + (j + 1) * LANES)
                l0, l1 = _decay(q, c0), _decay(q, c0 + 1)
                p0, p1 = scores * l0, scores * l1
                dy_st = _split_heads(d_y[:, ln]).astype(BF16)
                d_p = lax.dot_general(dy_st, q["xs_w"][:, ln].astype(BF16), _DIMS["nt"], preferred_element_type=F32)
                d_p0, d_p1 = d_p[:CHUNK], d_p[CHUNK:]
                d_scores = d_scores + d_p0 * l0 + d_p1 * l1
                for col, t in ((c0, d_p0 * p0), (c0 + 1, d_p1 * p1)):
                    d_cs = d_cs + jnp.sum(t - t.T, axis=1, keepdims=True) * (lane_row == col).astype(F32)
                p_st = jnp.concatenate([p0, p1], axis=0).astype(BF16)
                dxs_parts.append(lax.dot_general(p_st, dy_st, _DIMS["tn"], preferred_element_type=F32)
                                 + q["wt_w"][:, ln] * b_dh[:, j * LANES:(j + 1) * LANES])
            d_sc = d_scores.astype(BF16)
            d_c = d_c + jnp.dot(d_sc, bg, preferred_element_type=F32)
            d_b = d_b + lax.dot_general(d_sc, cg, _DIMS["tn"], preferred_element_type=F32)
            b_cols, c_cols = slice(ds + g * N_STATE, ds + (g + 1) * N_STATE), slice(ds + (2 + g) * N_STATE, ds + (3 + g) * N_STATE)
            dxbc_ref[:, b_cols] = d_b + more(b_cols)
            dxbc_ref[:, c_cols] = d_c + more(c_cols)
        d_xs = jnp.concatenate(dxs_parts, axis=1)
        narrow_m = (lax.broadcasted_iota(jnp.int32, (ds, LANES), 0) // HEAD_DIM
                    == lax.broadcasted_iota(jnp.int32, (ds, LANES), 1)).astype(BF16)
        rows8 = lambda v: jnp.broadcast_to(v, (8, ds))
        stacked = jnp.concatenate(
            [jnp.concatenate(dwt_parts, axis=1), jnp.concatenate(de_parts, axis=1), d_xs * q["xh"],
             rows8(jnp.concatenate(ddec_parts, axis=1)), rows8(jnp.sum(d_y * q["xh"], axis=0, keepdims=True))], axis=0)
        sums = jnp.dot(stacked.astype(BF16), narrow_m, preferred_element_type=F32)
        n_wt, n_e, n_xs = sums[:CHUNK], sums[CHUNK:2 * CHUNK], sums[2 * CHUNK:3 * CHUNK]
        n_dec, n_dsk = sums[3 * CHUNK:3 * CHUNK + 1], sums[3 * CHUNK + 8:3 * CHUNK + 9]
        e, wt, dec = jnp.exp(q["cs"]), jnp.exp(q["tot"] - q["cs"]), jnp.exp(q["tot"])
        d_wt = n_wt * wt
        d_cs = d_cs + n_e * e - d_wt
        d_tot = jnp.sum(d_wt, axis=0, keepdims=True) + n_dec * dec
        d_da = jnp.dot(q["tri_t"], d_cs, precision=HI, preferred_element_type=F32) + d_tot
        d_dt = d_da * q["a"] + n_xs
        dxbc_ref[:, :ds] = d_xs * q["dt_w"] + q["dsk_w"] * d_y + more(slice(0, ds))
        dalog_ref[...] += jnp.sum(d_da * q["dt"], axis=0, keepdims=True) * q["a"]
        d_raw = d_dt * q["sig"]
        ddtb_ref[...] += jnp.sum(d_raw, axis=0, keepdims=True)
        ddsk_ref[...] += n_dsk
        ddt_ref[...] = pltpu.roll(d_raw, n_head, axis=1) if rev else d_raw
        hosted.finish()

    vec = pl.BlockSpec((1, LANES), lambda b, s: (0, 0))
    vec_shape = jax.ShapeDtypeStruct((1, LANES), F32)
    in_specs, out_shape, out_specs, scratch, args = hosted.call_args(
        [pl.BlockSpec((CHUNK, xw), lambda b, s: (blk(b, s), 0)),
         pl.BlockSpec((CHUNK, LANES), lambda b, s: (blk(b, s), dt_cb)),
         pl.BlockSpec((None, None, N_STATE, ds), lambda b, s: (b, step(s), 0, 0)),
         pl.BlockSpec((CHUNK, ds), lambda b, s: (dy_blk(b, s), 0)), vec, vec, vec]
        + [pl.BlockSpec((CHUNK, xw), lambda b, s: (blk(b, s), 0))] * (add is not None),
        (_big((n_tok, xw), F32), _big((n_tok, LANES), F32), vec_shape, vec_shape, vec_shape),
        (pl.BlockSpec((CHUNK, xw), lambda b, s: (blk(b, s), 0)),
         pl.BlockSpec((CHUNK, LANES), lambda b, s: (blk(b, s), 0)), vec, vec, vec),
        [pltpu.VMEM((N_STATE, ds), F32)], [xbc, dt_raw, hs, dy, dtb, alog, dsk] + ([] if add is None else [add]))
    return hosted.results(pl.pallas_call(
        body, name=name, grid=(n_ex, n_step), out_shape=out_shape, in_specs=in_specs, out_specs=out_specs,
        scratch_shapes=scratch, compiler_params=_params(48 << 20, 2),
    )(*_in_hbm(args)))


def final_loss(x3, target, w, *, tm):
    n, d = x3.shape

    def body(x_ref, t_ref, w_ref, dx_ref, dw_ref, loss_ref):
        i = pl.program_id(0)
        t = t_ref[...]

        def per_feature(xv, wv):
            err = _rms(xv, wv) - t
            return 0.5 * jnp.sum(err * err, axis=0, keepdims=True) / d

        lv, vjp = jax.vjp(per_feature, x_ref[...], w_ref[...])
        dx, dw = vjp(jnp.ones_like(lv))
        dx_ref[...] = dx

        @pl.when(i == 0)
        def _():
            dw_ref[...] = dw
            loss_ref[...] = lv

        @pl.when(i > 0)
        def _():
            dw_ref[...] += dw
            loss_ref[...] += lv

    tile = pl.BlockSpec((tm, d), lambda i: (i, 0))
    vec = pl.BlockSpec((1, d), lambda i: (0, 0))
    return pl.pallas_call(
        body, name="final_loss", grid=(n // tm,), in_specs=[tile, tile, vec],
        out_shape=(jax.ShapeDtypeStruct((n, d), F32), jax.ShapeDtypeStruct((1, d), F32), jax.ShapeDtypeStruct((1, d), F32)),
        out_specs=(tile, vec, vec), compiler_params=_params(tm * d * 4 * 16 + (8 << 20)),
    )(x3, target, w)


def sum_slots(name, arr, out_dtype=F32):
    n_slot, n_row, width = arr.shape
    tm = _row_tile(n_row, width * n_slot, cap_bytes=STREAM_TILE_BYTES * 14, mult=16)
    vmem = 2 * n_slot * tm * width * arr.dtype.itemsize + 4 * tm * width * 4

    def body(a_ref, o_ref):
        acc = a_ref[0].astype(F32)
        for j in range(1, n_slot):
            acc = acc + a_ref[j].astype(F32)
        o_ref[...] = acc.astype(o_ref.dtype)

    return pl.pallas_call(
        body, name=name, grid=(n_row // tm,), out_shape=jax.ShapeDtypeStruct((n_row, width), out_dtype),
        in_specs=[pl.BlockSpec((n_slot, tm, width), lambda i: (0, i, 0))],
        out_specs=pl.BlockSpec((tm, width), lambda i: (i, 0)), compiler_params=_params(vmem + (4 << 20)),
    )(arr)


def adamw(name, w, g_slots, m, v):
    n_slot, n_row, width = g_slots.shape
    tm = _row_tile(n_row, width, cap_bytes=STREAM_TILE_BYTES)
    if g_slots.dtype == BF16 and tm % 16:
        tm16 = _row_tile(n_row, width, cap_bytes=STREAM_TILE_BYTES, mult=16)
        if tm16 % 16 == 0:
            tm = tm16
        else:
            g_slots = g_slots.astype(F32)

    def body(w_ref, g_ref, m_ref, v_ref, go_ref, d_ref, mo_ref, vo_ref):
        g = g_ref[0].astype(F32)
        for j in range(1, n_slot):
            g = g + g_ref[j].astype(F32)
        m2 = ADAM_B1 * m_ref[...] + (1.0 - ADAM_B1) * g
        v2 = ADAM_B2 * v_ref[...] + (1.0 - ADAM_B2) * jnp.square(g)
        m_hat = m2 / (1.0 - ADAM_B1 ** ADAM_STEP)
        v_hat = v2 / (1.0 - ADAM_B2 ** ADAM_STEP)
        go_ref[...] = g
        d_ref[...] = -ADAM_LR * (m_hat / (jnp.sqrt(v_hat) + ADAM_EPS) + ADAM_WD * w_ref[...])
        mo_ref[...] = m2
        vo_ref[...] = v2

    tile = pl.BlockSpec((tm, width), lambda i: (i, 0))
    shape = jax.ShapeDtypeStruct((n_row, width), F32)
    return pl.pallas_call(
        body, name=name, grid=(n_row // tm,), out_shape=(shape,) * 4,
        in_specs=[tile, pl.BlockSpec((n_slot, tm, width), lambda i: (0, i, 0)), tile, tile],
        out_specs=(tile,) * 4, compiler_params=_params(2 * (7 + n_slot) * tm * width * 4 + (4 << 20)),
    )(w, g_slots, m, v)


def cctx_grad(q_all, c_ctx_row):
    d = c_ctx_row.shape[1]

    def body(q_ref, c_ref, o_ref):
        acc = q_ref[0, 0:1, :]
        for j in (2, 4, 6):
            acc = acc + q_ref[j, 0:1, :]
        _, vjp = jax.vjp(_silu, c_ref[...])
        o_ref[...] = vjp(acc)[0]

    return pl.pallas_call(
        body, name="cctx_grad", out_shape=jax.ShapeDtypeStruct((1, d), F32),
    )(q_all, c_ctx_row)


def loss_total(pack_sum, d):
    def body(p_ref, o_ref):
        o_ref[...] = jnp.sum(p_ref[:, 0:d], axis=1, keepdims=True)

    return pl.pallas_call(
        body, name="loss_total", out_shape=jax.ShapeDtypeStruct((1, 1), F32),
    )(pack_sum)


class _Plan:
    def __init__(self):
        self.builders, self.got = {}, {}

    def on(self, host, key, builder):
        self.builders.setdefault(host, []).append((key, builder))

    def run(self, host, fn, *args, **kw):
        if host not in self.builders:
            return fn(host, *args, **kw)
        keys, riders = zip(*[(key, builder(self)) for key, builder in self.builders[host]])
        res, landed = fn(host, *args, rider=Riders(riders), **kw)
        for key, r in zip(keys, riders):
            self.got[key], landed = landed[:r.n], landed[r.n:]
        return res


def _val(w):
    return w() if callable(w) else w


def _matmul_tile(n_rows, tm):
    return 2 * tm if n_rows % (2 * tm) == 0 else tm


def _ffn_fwd(plan, tag, xin, n_rows, tm, seg_fn, shift, scale, gate, norm_w, wg, wu, wd, fuse_gate_up=False):
    d = xin[1]
    n_tiles = n_rows // tm
    (h,) = plan.run(f"{tag}_norm", rowwise, fn_norm_mod, [xin], [shift, scale], [norm_w], [(n_rows, d, BF16)],
                    tm=tm, n_tiles=n_tiles, seg_fn=seg_fn)
    tmm = _matmul_tile(n_rows, tm)
    if fuse_gate_up:
        g, u, act = plan.run(f"{tag}_gate_up", matmul, [(h, _val(wg)), (h, _val(wu))], "nn", b_ch=True, out_ch=True,
                             tm=min(tm, 256), fold=True,
                             post=([], lambda ag, au: (ag, au, fn_act(ag, au)[0]), [BF16, BF16, BF16]))
    else:
        g = plan.run(f"{tag}_gate", matmul, [(h, _val(wg))], "nn", out_dtype=BF16, b_ch=True, out_ch=True, tm=tmm)
        u, act = plan.run(f"{tag}_up", matmul, [(h, _val(wu))], "nn", b_ch=True, out_ch=True, tm=tm, fold=True,
                          post=([g], lambda acc, gv: (acc, fn_act(gv, acc)[0]), [BF16, BF16]))
    f = plan.run(f"{tag}_down", matmul, [(act, _val(wd))], "nn", a_ch=True, b_ch=True, tm=tmm, fold=True)
    (xo,) = plan.run(f"{tag}_resid", rowwise, make_fn_resid(0.5), [xin, row(f)], [gate], [], [(n_rows, d, F32)],
                     tm=tm, n_tiles=n_tiles, seg_fn=seg_fn)
    return xo, (h, g, u, act, f)


def _ffn_bwd(plan, tag, d_xo, saved, xin, n_rows, tm, seg_fn, first_fn, shift, scale, gate, norm_w, wg, wu, wd, dx_rows, dx_limit):
    h, g, u, act, f = saved
    d = xin[1]
    n_tiles = n_rows // tm
    n_ch, _, n_hid = g.shape
    d_f, d_gate = plan.run(f"{tag}_resid_bwd", rowwise_bwd, make_fn_resid(0.5), [xin, row(f)], [gate], [], [[row(d_xo)]],
                           [None, (n_rows, BF16, None)], tm=tm, n_tiles=n_tiles, seg_fn=seg_fn, first_fn=first_fn)
    tmm = _matmul_tile(n_rows, tm)
    def act_vjp(d_act, gv, uv):
        s = jax.nn.sigmoid(gv)
        gs = gv * s
        return d_act * uv * (s + gs * (1.0 - s)), d_act * gs
    d_g, d_u = plan.run(f"{tag}_down_dx", matmul, [(d_f, wd)], "nt", b_ch=True, out_ch=True, tm=tmm,
                        post=([g, u], act_vjp, [BF16, BF16]))
    plan.got[f"{tag}_d_wd"] = plan.run(f"{tag}_down_dw", matmul, [(act, d_f)], "tn", out_dtype=BF16, a_ch=True, out_ch=True, tm=tmm)
    d_h = plan.run(f"{tag}_up_dx", matmul, [(d_g, wg), (d_u, wu)], "nt", a_ch=True, b_ch=True, tm=tmm)
    plan.got[f"{tag}_d_wg"] = plan.run(f"{tag}_gate_dw", matmul, [(d_g, h)], "tn", out_dtype=BF16, a_ch=True, out_ch=True, tm=tmm)
    plan.got[f"{tag}_d_wu"] = plan.run(f"{tag}_up_dw", matmul, [(d_u, h)], "tn", out_dtype=BF16, a_ch=True, out_ch=True, tm=tmm)
    d_x, d_shift, d_scale, d_nw = plan.run(
        f"{tag}_norm_bwd", rowwise_bwd, fn_norm_mod, [xin], [shift, scale], [norm_w], [[row(d_h)]], [(dx_rows, F32, dx_limit)],
        tm=tm, n_tiles=n_tiles, seg_fn=seg_fn, first_fn=first_fn, adds={0: (row(d_xo), None)})
    return d_x, (d_shift, d_scale, d_gate), d_nw


def kernel(x, c, ctx, c_ctx, w_mod, b_mod, norm_ffn1, ffn1_gate, ffn1_up, ffn1_down, norm_mix, w_in, ssm_conv_w, ssm_conv_b, dt_bias_fwd, dt_bias_bwd, a_log_fwd, a_log_bwd, ssm_d, ssm_norm_w, cconv_w, cconv_b, cconv_ln_w, cconv_ln_b, w_out, norm_ffn2, ffn2_gate, ffn2_up, ffn2_down, final_norm, loss_target, m_c_ctx, m_w_mod, m_b_mod, m_norm_ffn1, m_ffn1_gate, m_ffn1_up, m_ffn1_down, m_norm_mix, m_w_in, m_ssm_conv_w, m_ssm_conv_b, m_dt_bias_fwd, m_dt_bias_bwd, m_a_log_fwd, m_a_log_bwd, m_ssm_d, m_ssm_norm_w, m_cconv_w, m_cconv_b, m_cconv_ln_w, m_cconv_ln_b, m_w_out, m_norm_ffn2, m_ffn2_gate, m_ffn2_up, m_ffn2_down, m_final_norm, v_c_ctx, v_w_mod, v_b_mod, v_norm_ffn1, v_ffn1_gate, v_ffn1_up, v_ffn1_down, v_norm_mix, v_w_in, v_ssm_conv_w, v_ssm_conv_b, v_dt_bias_fwd, v_dt_bias_bwd, v_a_log_fwd, v_a_log_bwd, v_ssm_d, v_ssm_norm_w, v_cconv_w, v_cconv_b, v_cconv_ln_w, v_cconv_ln_b, v_w_out, v_norm_ffn2, v_ffn2_gate, v_ffn2_up, v_ffn2_down, v_final_norm):
    weights = dict(c_ctx=c_ctx, w_mod=w_mod, b_mod=b_mod, norm_ffn1=norm_ffn1, ffn1_gate=ffn1_gate, ffn1_up=ffn1_up, ffn1_down=ffn1_down, norm_mix=norm_mix, w_in=w_in, ssm_conv_w=ssm_conv_w, ssm_conv_b=ssm_conv_b, dt_bias_fwd=dt_bias_fwd, dt_bias_bwd=dt_bias_bwd, a_log_fwd=a_log_fwd, a_log_bwd=a_log_bwd, ssm_d=ssm_d, ssm_norm_w=ssm_norm_w, cconv_w=cconv_w, cconv_b=cconv_b, cconv_ln_w=cconv_ln_w, cconv_ln_b=cconv_ln_b, w_out=w_out, norm_ffn2=norm_ffn2, ffn2_gate=ffn2_gate, ffn2_up=ffn2_up, ffn2_down=ffn2_down, final_norm=final_norm)
    mom1 = dict(c_ctx=m_c_ctx, w_mod=m_w_mod, b_mod=m_b_mod, norm_ffn1=m_norm_ffn1, ffn1_gate=m_ffn1_gate, ffn1_up=m_ffn1_up, ffn1_down=m_ffn1_down, norm_mix=m_norm_mix, w_in=m_w_in, ssm_conv_w=m_ssm_conv_w, ssm_conv_b=m_ssm_conv_b, dt_bias_fwd=m_dt_bias_fwd, dt_bias_bwd=m_dt_bias_bwd, a_log_fwd=m_a_log_fwd, a_log_bwd=m_a_log_bwd, ssm_d=m_ssm_d, ssm_norm_w=m_ssm_norm_w, cconv_w=m_cconv_w, cconv_b=m_cconv_b, cconv_ln_w=m_cconv_ln_w, cconv_ln_b=m_cconv_ln_b, w_out=m_w_out, norm_ffn2=m_norm_ffn2, ffn2_gate=m_ffn2_gate, ffn2_up=m_ffn2_up, ffn2_down=m_ffn2_down, final_norm=m_final_norm)
    mom2 = dict(c_ctx=v_c_ctx, w_mod=v_w_mod, b_mod=v_b_mod, norm_ffn1=v_norm_ffn1, ffn1_gate=v_ffn1_gate, ffn1_up=v_ffn1_up, ffn1_down=v_ffn1_down, norm_mix=v_norm_mix, w_in=v_w_in, ssm_conv_w=v_ssm_conv_w, ssm_conv_b=v_ssm_conv_b, dt_bias_fwd=v_dt_bias_fwd, dt_bias_bwd=v_dt_bias_bwd, a_log_fwd=v_a_log_fwd, a_log_bwd=v_a_log_bwd, ssm_d=v_ssm_d, ssm_norm_w=v_ssm_norm_w, cconv_w=v_cconv_w, cconv_b=v_cconv_b, cconv_ln_w=v_cconv_ln_w, cconv_ln_b=v_cconv_ln_b, w_out=v_w_out, norm_ffn2=v_norm_ffn2, ffn2_gate=v_ffn2_gate, ffn2_up=v_ffn2_up, ffn2_down=v_ffn2_down, final_norm=v_final_norm)
    order = list(weights)

    n_ex, seq_len, d = x.shape
    ctx_len = ctx.shape[1]
    ds = d
    n_head = ds // HEAD_DIM
    xw = ds + 4 * N_STATE
    n_lat, n_ctx_rows = n_ex * seq_len, n_ex * ctx_len
    n_tok = n_lat + n_ctx_rows
    tm = math.gcd(math.gcd(512, seq_len), n_ctx_rows)
    seg_all, first_all = _segmenter(tm, seq_len, n_lat)
    lat_tiles = n_lat // tm

    xi, yi, ci = lax.axis_index("x"), lax.axis_index("y"), lax.axis_index("c")
    me, chip = 4 * xi + 2 * yi + ci, 2 * xi + yi

    (c_all,) = exchange("gather_c", [c], "all8")
    n_all = 8 * n_ex
    n_cond = -(-(n_all + 1) // 8) * 8
    cond = jnp.concatenate([c_all.reshape(n_all, d), c_ctx[None, :], jnp.zeros((n_cond - n_all - 1, d), F32)])
    mod_w = w_mod.shape[2]
    b_shard = lax.dynamic_slice(b_mod, (0, chip * mod_w), (1, mod_w))
    (mod_g,) = exchange("gather_mod", [mod_fwd(cond, w_mod[0], b_shard)], "chips")
    mod_full = mod_g.transpose(1, 0, 2).reshape(n_cond, N_CHIPS * mod_w)
    mod_mine = lax.dynamic_slice(mod_full, (me * n_ex, 0), (n_ex, 9 * d)).reshape(n_ex, 9, d)
    mod_ctx = mod_full[n_all].reshape(9, d)
    tabs = [jnp.concatenate([mod_mine[:, j], mod_ctx[j][None]])[:, None, :] for j in range(9)]
    lat = lambda t: t[:n_ex]

    bf = lambda w: w[0].astype(BF16)
    plan = _Plan()
    gather = lambda *ws: (lambda p: Rider(list(ws), "chips"))
    plan.on("ffn1_norm", "wg1", gather(bf(ffn1_gate)))
    plan.on("ffn1_gate", "wu1", gather(bf(ffn1_up)))
    plan.on("ffn1_up", "wd1", gather(bf(ffn1_down)))
    cut_a, cut_b = d * 5 // 8, d * 7 // 8
    plan.on("ffn1_down", "win_a", gather(bf(w_in)[:cut_a]))
    plan.on("ffn1_resid", "win_b", gather(bf(w_in)[cut_a:cut_b], ssm_conv_w[0], cconv_w[0]))
    xt = two_rows(x.reshape(n_lat, d), ctx.reshape(n_ctx_rows, d), lat_tiles)
    x1, saved1 = _ffn_fwd(plan, "ffn1", xt, n_tok, tm, seg_all, tabs[0], tabs[1], tabs[2], norm_ffn1,
                          lambda: plan.got["wg1"][0], lambda: plan.got["wu1"][0], lambda: plan.got["wd1"][0])
    (wg1,), (wu1,), (wd1,), (win_a,), (win_b, w5_g, w31_g) = (plan.got[k] for k in ("wg1", "wu1", "wd1", "win_a", "win_b"))
    (h2,), (win_c,) = rowwise("mix_norm", fn_norm_mod, [row(x1)], [tabs[3], tabs[4]], [norm_mix], [(n_tok, d, BF16)],
                              tm=tm, n_tiles=n_tok // tm, seg_fn=seg_all, rider=Rider([bf(w_in)[cut_b:]], "chips"))
    win_g = jnp.concatenate([win_a, win_b, win_c], axis=1)
    unshard_cols = lambda t: t.transpose(1, 0, 2).reshape(t.shape[1], N_CHIPS * t.shape[2])
    win = unshard_cols(win_g)
    o_x, o_dt, o_glu = ds, ds + xw, ds + xw + 2 * n_head
    w_z, w_xbc, w_dt = win[:, :ds], win[:, o_x:o_dt], win[:, o_dt:o_glu]
    w_ga, w_gb = win[:, o_glu:o_glu + d], win[:, o_glu + d:]
    w_dtp = jnp.concatenate([w_dt, jnp.zeros((d, LANES - 2 * n_head), BF16)], axis=1)
    w_cat = jnp.concatenate([w_z, w_ga, w_gb, w_xbc], axis=1)
    cbw = d // 2
    xbc_cb, dt_cb = 3 * d // cbw, 0
    w5, w31 = unshard_cols(w5_g), unshard_cols(w31_g)
    pad_vec = lambda v: jnp.concatenate([v.reshape(1, -1), jnp.zeros((1, LANES - v.size), F32)], axis=1)
    dtb_f, dtb_b, alog_f, alog_b = map(pad_vec, (dt_bias_fwd, dt_bias_bwd, a_log_fwd, a_log_bwd))
    dsk_f, dsk_b = pad_vec(ssm_d), jnp.zeros((1, LANES), F32)

    proj, (wg2,) = matmul("mix_proj", [(h2, w_cat)], "nn", out_dtype=BF16, tm=tm, rider=Rider([bf(ffn2_gate)], "chips"))
    dt_raw = matmul("mix_proj_dt", [(h2, w_dtp)], "nn", tm=tm)
    def conv5(name, src, cb0, flip):
        out = None
        for part, seq, off in (("lat", seq_len, 0), ("ctx", ctx_len, n_lat // ctx_len)):
            out = tapsum_roll(f"{name}_{part}", src, cb0, w5, 0, seq_len=seq, n_seq=n_ex, row_blk_off=off, width=seq,
                              piece=seq, cb=cbw, ncb=xw // cbw, pad=w5.shape[0] // 2, flip=flip,
                              place=((n_tok, xw), off, 0, out))
        return out

    craw = conv5("xbc_conv", proj, xbc_cb, False)
    (xbc,) = rowwise("xbc_silu", fn_silu_bias, [row(craw)], [], [ssm_conv_b], [(n_tok, xw, F32)], tm=tm, n_tiles=n_tok // tm)
    ssd = dict(n_ex=n_ex, seq_len=seq_len, ctx_len=ctx_len, ds=ds)
    (y_f, hs_f), (wu2,) = ssd_fwd("ssd_fwd_f", xbc, dt_raw, dt_cb, dtb_f, alog_f, dsk_f, rev=False,
                                  rider=Rider([bf(ffn2_up)], "chips"), **ssd)
    (y_b, hs_b), (wout_g, wd2) = ssd_fwd("ssd_fwd_b", xbc, dt_raw, dt_cb, dtb_b, alog_b, dsk_b, rev=True,
                                         rider=Rider([bf(w_out), bf(ffn2_down)], "chips"), add=y_f, **ssd)
    wout = wout_g.reshape(2 * d, d)
    wo_y, wo_u = wout[:ds], wout[ds:]
    fn_gate = make_fn_gate_groupnorm(ds)
    (yn,) = rowwise("ssd_gate", fn_gate, [row(y_b), row(proj, d, 0)], [], [ssm_norm_w], [(n_lat, ds, BF16)],
                    tm=tm, n_tiles=lat_tiles)
    (u0,) = rowwise("glu", fn_glu, [row(proj, d, 1), row(proj, d, 2)], [], [], [(n_lat, d, F32)], tm=tm, n_tiles=lat_tiles)
    cb31 = max(LANES, d // 4)
    ncb31 = (d // 2) // cb31
    pad31 = w31.shape[0] // 2
    piece31 = min(seq_len, 4 * GRID_W)
    v_w = tapsum_roll("cconv_cols", u0, 0, w31, 0, seq_len=seq_len, n_seq=n_ex, row_blk_off=0, width=GRID_W,
                      piece=piece31, cb=cb31, ncb=ncb31, pad=pad31, flip=False)
    v_h = tapsum_rows("cconv_rows", u0, ncb31, w31, ncb31, seq_len=seq_len, n_seq=n_ex, cb=cb31, ncb=ncb31, pad=pad31, flip=False)
    (un,) = rowwise("cconv_ln", fn_ln_silu, [row(v_w), row(v_h)], [], [cconv_b, cconv_ln_w, cconv_ln_b], [(n_lat, d, BF16)],
                    tm=tm, n_tiles=lat_tiles)
    mix = matmul("mix_out", [(yn, wo_y), (un, wo_u)], "nn", tm=tm)
    seg_lat, first_lat = _segmenter(tm, seq_len, n_lat)
    (x2,) = rowwise("mix_resid", make_fn_resid(1.0), [row(x1), row(mix)], [lat(tabs[5])], [], [(n_lat, d, F32)],
                    tm=tm, n_tiles=lat_tiles, seg_fn=seg_lat)
    x3, saved2 = _ffn_fwd(plan, "ffn2", row(x2), n_lat, tm, seg_lat, lat(tabs[6]), lat(tabs[7]), lat(tabs[8]), norm_ffn2, wg2, wu2, wd2,
                          fuse_gate_up=True)
    d_x3, d_final, loss_vec = final_loss(x3, loss_target.reshape(n_lat, d), final_norm.reshape(1, d), tm=tm)

    shard_cols = lambda t: t.reshape(t.shape[0], N_CHIPS, -1).transpose(1, 0, 2)

    def pieces(t):
        t = jnp.pad(t, ((0, 0), (0, -t.shape[1] % 32), (0, 0)))
        return t.reshape(2 * N_CHIPS, t.shape[1] // 2, t.shape[2]).astype(BF16)

    scatter = lambda *ts: Rider([pieces(t) for t in ts], "all8", scatter=True)
    halves = lambda names, landed: Rider([sum_slots(f"sum_{nm}", r, BF16) for nm, r in zip(names, landed)], "sibling")
    swapped = {}
    plan.on("ffn2_up_dx", "sc_ffn2_down", lambda p: scatter(p.got["ffn2_d_wd"]))
    plan.on("ffn2_up_dw", "sc_ffn2_gate", lambda p: scatter(p.got["ffn2_d_wg"]))
    d_x2, (d_s6, d_s7, d_g8), d_nffn2 = _ffn_bwd(
        plan, "ffn2", d_x3, saved2, row(x2), n_lat, tm, seg_lat, first_lat, lat(tabs[6]), lat(tabs[7]), lat(tabs[8]), norm_ffn2,
        wg2, wu2, wd2, n_lat, None)
    d_mix, d_g5 = rowwise_bwd("mix_resid_bwd", make_fn_resid(1.0), [row(x1), row(mix)], [lat(tabs[5])], [], [[row(d_x2)]],
                              [None, (n_lat, BF16, None)], tm=tm, n_tiles=lat_tiles, seg_fn=seg_lat, first_fn=first_lat)
    d_yn = matmul("mix_out_dy", [(d_mix, wo_y)], "nt", tm=tm)
    d_un = matmul("mix_out_du", [(d_mix, wo_u)], "nt", tm=tm)
    d_wout = jnp.concatenate([matmul("mix_out_dwy", [(yn, d_mix)], "tn", out_dtype=BF16, tm=tm),
                              matmul("mix_out_dwu", [(un, d_mix)], "tn", out_dtype=BF16, tm=tm)])
    d_vw, d_vh, d_cb, d_lnw, d_lnb = rowwise_bwd(
        "cconv_ln_bwd", fn_ln_silu, [row(v_w), row(v_h)], [], [cconv_b, cconv_ln_w, cconv_ln_b], [[row(d_un)]],
        [(n_lat, F32, None)] * 2, tm=tm, n_tiles=lat_tiles)
    d_u0 = tapsum_roll("cconv_cols_dx", d_vw, 0, w31, 0, seq_len=seq_len, n_seq=n_ex, row_blk_off=0, width=GRID_W,
                       piece=piece31, cb=cb31, ncb=ncb31, pad=pad31, flip=True, place=((n_lat, d), 0, 0, None))
    d_u0 = tapsum_rows("cconv_rows_dx", d_vh, 0, w31, ncb31, seq_len=seq_len, n_seq=n_ex, cb=cb31, ncb=ncb31, pad=pad31,
                       flip=True, place=((n_lat, d), 0, ncb31, d_u0))
    d_w31 = jnp.concatenate([
        tapgrad_roll("cconv_cols_dw", d_vw, 0, 0, u0, 0, 0, n_tap=w31.shape[0], seq_len=seq_len, n_seq=n_ex, width=GRID_W,
                     piece=piece31, cb=cb31, ncb=ncb31, pad=pad31),
        tapgrad_rows("cconv_rows_dw", d_vh, 0, u0, ncb31, n_tap=w31.shape[0], seq_len=seq_len, n_seq=n_ex, cb=cb31,
                     ncb=ncb31, pad=pad31)], axis=1)
    d_ga, d_gb = rowwise_bwd("glu_bwd", fn_glu, [row(proj, d, 1), row(proj, d, 2)], [], [], [[row(d_u0)]],
                             [(n_lat, BF16, None)] * 2, tm=tm, n_tiles=lat_tiles)
    d_ysum, d_z, d_ssmnw = rowwise_bwd(
        "ssd_gate_bwd", fn_gate, [row(y_b), row(proj, d, 0)], [], [ssm_norm_w], [[row(d_yn)]],
        [(n_lat, F32, None), (n_lat, BF16, None)], tm=tm, n_tiles=lat_tiles)
    (dxbc_f, ddt_f, dalog_f, ddtb_f, ddsk), landed = ssd_bwd(
        "ssd_bwd_f", xbc, dt_raw, dt_cb, hs_f, d_ysum, dtb_f, alog_f, dsk_f, rev=False,
        rider=scatter(plan.got["ffn2_d_wu"], d_wout.reshape(N_CHIPS, -1, d)), **ssd)
    (dxbc_b, ddt_b, dalog_b, ddtb_b, _), both = ssd_bwd(
        "ssd_bwd_b", xbc, dt_raw, dt_cb, hs_b, d_ysum, dtb_b, alog_b, dsk_b, rev=True,
        rider=halves(["ffn2_down", "ffn2_gate"], plan.got["sc_ffn2_down"] + plan.got["sc_ffn2_gate"]), add=dxbc_f, **ssd)
    swapped.update(zip(["ffn2_down", "ffn2_gate"], both))
    (d_craw, d_conv_b), both = rowwise_bwd(
        "xbc_silu_bwd", fn_silu_bias, [row(craw)], [], [ssm_conv_b], [[row(dxbc_b)]],
        [(n_tok, F32, None)], tm=tm, n_tiles=n_tok // tm, rider=halves(["ffn2_up", "w_out"], landed))
    swapped.update(zip(["ffn2_up", "w_out"], both))
    d_pxbc = conv5("xbc_conv_dx", d_craw, 0, True)
    g5 = lambda name, seq, off: tapgrad_roll(name, d_craw, 0, off, proj, xbc_cb, off, n_tap=w5.shape[0], seq_len=seq,
                                             n_seq=n_ex, width=seq, piece=seq, cb=cbw, ncb=xw // cbw, pad=w5.shape[0] // 2)
    d_w5 = g5("xbc_conv_lat_dw", seq_len, 0) + g5("xbc_conv_ctx_dw", ctx_len, n_lat // ctx_len)
    lat_pairs = [(d_z, w_z), (d_ga, w_ga), (d_gb, w_gb), (d_pxbc, w_xbc), (ddt_f, w_dtp), (ddt_b, w_dtp)]
    d_h2 = matmul("mix_proj_dx_lat", lat_pairs, "nt", rows=n_lat, tm=min(tm, 256), place=(n_tok, 0, None))
    d_h2 = matmul("mix_proj_dx_ctx", lat_pairs[3:], "nt", rows=n_ctx_rows, row_off=n_lat, tm=min(tm, 256),
                  place=(n_tok, n_lat, d_h2))
    d_wz = matmul("mix_proj_dwz", [(d_z, h2)], "tn", out_dtype=BF16, rows=n_lat, tm=tm)
    d_wga = matmul("mix_proj_dwa", [(d_ga, h2)], "tn", out_dtype=BF16, rows=n_lat, tm=tm)
    d_wgb = matmul("mix_proj_dwb", [(d_gb, h2)], "tn", out_dtype=BF16, rows=n_lat, tm=tm)
    d_wxbc = matmul("mix_proj_dwx", [(d_pxbc, h2)], "tn", out_dtype=BF16, tm=tm)
    d_wdt = matmul("mix_proj_dwt", [(ddt_f, h2), (ddt_b, h2)], "tn", out_dtype=BF16, tm=tm)
    d_win_t = jnp.concatenate([d_wz, d_wxbc, d_wdt[:2 * n_head], d_wga, d_wgb]).reshape(N_CHIPS, -1, d)
    d_x1, d_s3, d_s4, d_nmix = rowwise_bwd(
        "mix_norm_bwd", fn_norm_mod, [row(x1)], [tabs[3], tabs[4]], [norm_mix], [[row(d_h2)]], [(n_tok, F32, None)],
        tm=tm, n_tiles=n_tok // tm, seg_fn=seg_all, first_fn=first_all, adds={0: (row(d_x2), lat_tiles)})
    mix_names = ["w_in", "ssm_conv_w", "cconv_w"]
    plan.on("ffn1_down_dx", "sc_conv", lambda p: scatter(shard_cols(d_w5), shard_cols(d_w31)))
    plan.on("ffn1_up_dx", "sc_win", lambda p: scatter(d_win_t))
    plan.on("ffn1_gate_dw", "sc_ffn1_down", lambda p: scatter(p.got["ffn1_d_wd"]))
    plan.on("ffn1_up_dw", "sc_ffn1_gate", lambda p: scatter(p.got["ffn1_d_wg"]))
    plan.on("ffn1_up_dw", "sw_mix", lambda p: halves(mix_names, p.got["sc_win"] + p.got["sc_conv"]))
    plan.on("ffn1_norm_bwd", "sc_ffn1_up", lambda p: scatter(p.got["ffn1_d_wu"]))
    plan.on("ffn1_up_dw", "sw_ffn1_down", lambda p: halves(["ffn1_down"], p.got["sc_ffn1_down"]))
    d_xt, (d_s0, d_s1, d_g2), d_nffn1 = _ffn_bwd(
        plan, "ffn1", d_x1, saved1, xt, n_tok, tm, seg_all, first_all, tabs[0], tabs[1], tabs[2], norm_ffn1, wg1, wu1, wd1,
        n_lat, lat_tiles)
    swapped.update(zip(mix_names + ["ffn1_down"], plan.got["sw_mix"] + plan.got["sw_ffn1_down"]))
    last_names = ["ffn1_gate", "ffn1_up"]
    last = halves(last_names, plan.got["sc_ffn1_gate"] + plan.got["sc_ffn1_up"])
    grad_x = d_xt.reshape(n_ex, seq_len, d)

    with_ctx0 = lambda t: jnp.concatenate([t, jnp.zeros((1, 1, d), F32)])
    d_tabs = [d_s0, d_s1, d_g2, d_s3, d_s4, with_ctx0(d_g5), with_ctx0(d_s6), with_ctx0(d_s7), with_ctx0(d_g8)]
    d_mod_rows = jnp.concatenate([t[:, 0, :] for t in d_tabs], axis=1)
    n_pad_rows = -(-(n_ex + 1) // 8) * 8
    d_mod_rows = jnp.concatenate([d_mod_rows, jnp.zeros((n_pad_rows - n_ex - 1, 9 * d), F32)])
    small = [("loss", loss_vec), ("norm_ffn1", d_nffn1), ("norm_mix", d_nmix), ("ssm_conv_b", d_conv_b),
             ("dt_bias_fwd", ddtb_f[:, :n_head]), ("dt_bias_bwd", ddtb_b[:, :n_head]), ("a_log_fwd", dalog_f[:, :n_head]),
             ("a_log_bwd", dalog_b[:, :n_head]), ("ssm_d", ddsk[:, :n_head]), ("ssm_norm_w", d_ssmnw), ("cconv_b", d_cb),
             ("cconv_ln_w", d_lnw), ("cconv_ln_b", d_lnb), ("norm_ffn2", d_nffn2), ("final_norm", d_final)]
    n_small = sum(v.size for _, v in small)
    n_pack = -(-n_small // (8 * LANES)) * (8 * LANES)
    pack = jnp.concatenate([v.reshape(-1) for _, v in small] + [jnp.zeros((n_pack - n_small,), F32)]).reshape(-1, LANES)
    (pack_all, d_mod_all), both = exchange_many("gather_small_swap_last", [Rider([pack, d_mod_rows], "all8"), last])
    swapped.update(zip(last_names, both))
    pack_sum = sum_slots("small_sum", pack_all)
    loss = loss_total(pack_sum.reshape(1, n_pack), d).reshape(())
    flat_sum = pack_sum.reshape(-1)
    small_grads, pos = {}, 0
    for nm, v in small:
        small_grads[nm] = flat_sum[pos:pos + v.size]
        pos += v.size
    d_mod_all = d_mod_all.reshape(8 * n_pad_rows, 9 * d)
    cond_rows = [jnp.concatenate([cond[j * n_ex:(j + 1) * n_ex], c_ctx[None, :],
                                  jnp.zeros((n_pad_rows - n_ex - 1, d), F32)]) for j in range(8)]
    cond_bwd = jnp.concatenate(cond_rows)
    d_mod_shard = lax.dynamic_slice(d_mod_all, (0, chip * mod_w), (8 * n_pad_rows, mod_w))
    g_wmod, g_bmod, q_part = mod_bwd(cond_bwd, d_mod_shard, d_mod_all, w_mod[0],
                                     tuple(j * n_pad_rows + n_ex for j in range(8)))
    (q_all,) = exchange("gather_cctx", [q_part], "all8")
    g_cctx = cctx_grad(q_all, c_ctx.reshape(1, d))
    small_grads["c_ctx"], small_grads["b_mod"] = g_cctx.reshape(-1), g_bmod.reshape(-1)

    transposed = {"ffn1_gate", "ffn1_up", "ffn2_gate", "ffn2_up", "w_in"}
    results = {}
    for nm, both in swapped.items():
        flip = (lambda t: jnp.swapaxes(t, 1, 2)) if nm in transposed else (lambda t: t)
        shape = flip(weights[nm]).shape
        two_d = lambda t: flip(t).reshape(shape[-2], shape[-1])
        g_full = both.reshape(1, -1, shape[-1])[:, :shape[-2]]
        results[nm] = [flip(r.reshape(shape)) for r in
                       adamw(f"adamw_{nm}", two_d(weights[nm]), g_full, two_d(mom1[nm]), two_d(mom2[nm]))]
    results["w_mod"] = [r.reshape(w_mod.shape) for r in adamw("adamw_w_mod", w_mod[0], g_wmod[None], m_w_mod[0], v_w_mod[0])]
    small_names = [nm for nm in order if nm not in results]
    n_sm = sum(weights[nm].size for nm in small_names)
    n_smp = -(-n_sm // (8 * LANES)) * (8 * LANES)
    packed = lambda src: jnp.concatenate([src[nm].reshape(-1) for nm in small_names] + [jnp.zeros((n_smp - n_sm,), F32)]).reshape(-1, LANES)
    sm_out = adamw("adamw_small", packed(weights), packed(small_grads)[None], packed(mom1), packed(mom2))
    pos = 0
    for nm in small_names:
        size = weights[nm].size
        results[nm] = [r.reshape(-1)[pos:pos + size].reshape(weights[nm].shape) for r in sm_out]
        pos += size
    return (loss, grad_x, *[results[nm][0] for nm in order], *[results[nm][1] for nm in order],
            *[results[nm][2] for nm in order], *[results[nm][3] for nm in order])
```

```python
import functools
import math

import jax
import jax.numpy as jnp
from jax import lax
from jax.experimental import pallas as pl
from jax.experimental.pallas import tpu as pltpu

F32 = jnp.float32
BF16 = jnp.bfloat16
HI = lax.Precision.HIGHEST
MESH = pl.DeviceIdType.MESH

EPS = 1e-6
GRID_W = 64
HEAD_DIM = 64
N_STATE = 128
CHUNK = 128
LANES = 128
N_CHIPS = 4
ADAM_LR, ADAM_B1, ADAM_B2, ADAM_EPS, ADAM_WD, ADAM_STEP = 0.001, 0.9, 0.999, 1e-08, 0.01, 10
VMEM_CAP = 56 * 1024 * 1024
STREAM_TILE_BYTES = 3 << 19


def _params(vmem_bytes=None, n_axes=1):
    kw = dict(dimension_semantics=("arbitrary",) * n_axes)
    if vmem_bytes is not None:
        kw["vmem_limit_bytes"] = int(min(VMEM_CAP, max(32 * 1024 * 1024, vmem_bytes)))
    return pltpu.CompilerParams(**kw)


def _big(shape, dtype):
    return pltpu.HBM(tuple(shape), dtype)


def _in_hbm(args):
    return [pltpu.with_memory_space_constraint(a, pltpu.HBM) if a.size * a.dtype.itemsize >= (1 << 20) else a for a in args]


def _nbytes(shape, dtype):
    return math.prod(shape) * jnp.dtype(dtype).itemsize


def _row_tile(rows, width, cap_bytes=1 << 20, mult=8):
    best = None
    for t in range(mult, rows + 1, mult):
        if rows % t == 0 and t * width * 4 <= cap_bytes:
            best = t
    return best if best is not None else rows


_MODES = {"all8": (8, (1, 2, 3, 4, 5, 6, 7), 0), "chips": (4, (2, 4, 6), 1), "sibling": (2, (1,), 0)}


class Rider:
    def __init__(self, arrs, mode, scatter=False):
        self.arrs, self.scatter = list(arrs), scatter
        self.nslot, self.deltas, self.shift = _MODES[mode]
        self.n = len(self.arrs)
        self.out_shape = [jax.ShapeDtypeStruct((self.nslot,) + (a.shape[1:] if scatter else a.shape), a.dtype)
                          for a in self.arrs]
        any_spec = pl.BlockSpec(memory_space=pl.ANY)
        self.in_specs = [any_spec] * self.n
        self.out_specs = [any_spec] * self.n
        n_peer = len(self.deltas)
        self.scratch = [pltpu.SemaphoreType.DMA((self.n, n_peer)), pltpu.SemaphoreType.DMA((self.n, n_peer)),
                        pltpu.SemaphoreType.DMA((self.n,))]

    def _copies(self, ins, outs, sems, arrivals):
        send_sems, recv_sems, local_sems = sems
        x, y, c = lax.axis_index("x"), lax.axis_index("y"), lax.axis_index("c")
        me = 4 * x + 2 * y + c
        slot_of = lambda dev: (dev >> self.shift) & (self.nslot - 1)
        src = lambda a, slot: ins[a].at[slot] if self.scatter else ins[a]
        flip = lambda v, bit: 1 - v if bit else v

        def remote(a, k, d, from_slot, to_slot):
            return pltpu.make_async_remote_copy(
                src_ref=src(a, from_slot), dst_ref=outs[a].at[to_slot], send_sem=send_sems.at[a, k],
                recv_sem=recv_sems.at[a, k], device_id=(flip(x, (d >> 2) & 1), flip(y, (d >> 1) & 1), flip(c, d & 1)),
                device_id_type=MESH)

        mine = slot_of(me)
        local = [pltpu.make_async_copy(src(a, mine), outs[a].at[mine], local_sems.at[a]) for a in range(self.n)]
        sends = [remote(a, k, d, slot_of(me ^ d), mine) for k, d in enumerate(self.deltas) for a in range(self.n)]
        if not arrivals:
            return local, sends
        return local, sends, [remote(a, k, d, mine, slot_of(me ^ d)) for k, d in enumerate(self.deltas) for a in range(self.n)]

    def start(self, ins, outs, sems):
        local, sends = self._copies(ins, outs, sems, arrivals=False)
        for cp in local + sends:
            cp.start()

    def wait(self, ins, outs, sems):
        local, sends, recvs = self._copies(ins, outs, sems, arrivals=True)
        for cp in recvs:
            cp.wait_recv()
        for cp in sends:
            cp.wait_send()
        for cp in local:
            cp.wait()


class Riders:
    def __init__(self, riders):
        self.riders = list(riders)
        self.n = sum(r.n for r in self.riders)
        cat = lambda attr: [v for r in self.riders for v in getattr(r, attr)]
        self.arrs, self.out_shape, self.in_specs = cat("arrs"), cat("out_shape"), cat("in_specs")
        self.out_specs, self.scratch = cat("out_specs"), cat("scratch")

    def _each(self, method, ins, outs, sems):
        i = s = 0
        for r in self.riders:
            getattr(r, method)(ins[i:i + r.n], outs[i:i + r.n], sems[s:s + len(r.scratch)])
            i, s = i + r.n, s + len(r.scratch)

    def start(self, ins, outs, sems):
        self._each("start", ins, outs, sems)

    def wait(self, ins, outs, sems):
        self._each("wait", ins, outs, sems)


class _Hosted:
    def __init__(self, rider, n_in, n_out, n_scratch, grid):
        self.rider, self.n_in, self.n_out, self.n_scratch, self.grid = rider, n_in, n_out, n_scratch, grid
        self.n = rider.n if rider else 0

    def split(self, refs):
        a, b = self.n_in, self.n_in + self.n
        c, e = b + self.n_out, b + self.n_out + self.n
        self._r = (refs[a:b], refs[c:e], refs[e + self.n_scratch:])
        if self.rider:
            ids = [pl.program_id(ax) for ax in range(len(self.grid))]
            first = functools.reduce(jnp.logical_and, [i == 0 for i in ids]) if ids else True
            pl.when(first)(lambda: self.rider.start(*self._r))
        return refs[:a], refs[b:c], refs[e:e + self.n_scratch]

    def finish(self):
        if self.rider:
            ids = [pl.program_id(ax) for ax in range(len(self.grid))]
            last = functools.reduce(jnp.logical_and, [i == n - 1 for i, n in zip(ids, self.grid)]) if ids else True
            pl.when(last)(lambda: self.rider.wait(*self._r))

    def call_args(self, in_specs, out_shape, out_specs, scratch, args):
        r = self.rider
        if not r:
            return list(in_specs), tuple(out_shape), tuple(out_specs), list(scratch), list(args)
        return (list(in_specs) + r.in_specs, tuple(out_shape) + tuple(r.out_shape), tuple(out_specs) + tuple(r.out_specs),
                list(scratch) + r.scratch, list(args) + r.arrs)

    def results(self, res, unwrap=True):
        res = list(res) if isinstance(res, (tuple, list)) else [res]
        host = res[:self.n_out]
        host = host[0] if (self.n_out == 1 and unwrap) else tuple(host)
        return (host, res[self.n_out:]) if self.rider else host


def exchange_many(name, riders):
    both = Riders(riders)

    def body(*refs):
        ins, outs, sems = refs[:both.n], refs[both.n:2 * both.n], refs[2 * both.n:]
        both.start(ins, outs, sems)
        both.wait(ins, outs, sems)

    res = list(pl.pallas_call(
        body, name=name, out_shape=tuple(both.out_shape), in_specs=both.in_specs, out_specs=tuple(both.out_specs),
        scratch_shapes=both.scratch,
    )(*both.arrs))
    split = []
    for r in riders:
        split.append(res[:r.n])
        res = res[r.n:]
    return split


def exchange(name, arrs, mode, scatter=False):
    rider = Rider(arrs, mode, scatter)

    def body(*refs):
        ins, outs, sems = refs[:rider.n], refs[rider.n:2 * rider.n], refs[2 * rider.n:]
        rider.start(ins, outs, sems)
        rider.wait(ins, outs, sems)

    return pl.pallas_call(
        body, name=name, out_shape=tuple(rider.out_shape), in_specs=rider.in_specs, out_specs=tuple(rider.out_specs),
        scratch_shapes=rider.scratch,
    )(*arrs)


_DIMS = {"nn": (((1,), (0,)), ((), ())), "nt": (((1,), (1,)), ((), ())), "tn": (((0,), (0,)), ((), ()))}


def matmul(name, pairs, kind, *, a_ch=False, b_ch=False, out_ch=False, out_dtype=F32, rows=None, row_off=0, tm=512,
           rider=None, post=None, fold=False, place=None):
    a0, b0 = pairs[0]
    n_chunk = a0.shape[0] if a_ch else (b0.shape[0] if b_ch else 1)
    total_rows = a0.shape[-2]
    rows = total_rows - row_off if rows is None else rows
    tm = min(tm, rows)
    assert rows % tm == 0 and row_off % tm == 0, (name, rows, tm, row_off)
    n_rt, off = rows // tm, row_off // tm
    dims = _DIMS[kind]
    n_pair = len(pairs)

    if kind == "tn":
        grid, red_axis, n_red = (n_chunk, n_rt), 1, n_rt
        a_idx = (lambda k, i: (k, i + off, 0)) if a_ch else (lambda k, i: (i + off, 0))
        b_idx = (lambda k, i: (k, i + off, 0)) if b_ch else (lambda k, i: (i + off, 0))
        a_blk = lambda a: ((None, tm, a.shape[-1]) if a_ch else (tm, a.shape[-1]))
        b_blk = lambda b: ((None, tm, b.shape[-1]) if b_ch else (tm, b.shape[-1]))
        o2 = (a0.shape[-1], b0.shape[-1])
        out_shape = ((n_chunk,) + o2) if out_ch else o2
        out_spec = pl.BlockSpec((None,) + o2, lambda k, i: (k, 0, 0)) if out_ch else pl.BlockSpec(o2, lambda k, i: (0, 0))
        acc_shape = o2
    else:
        n_out = b0.shape[-1] if kind == "nn" else b0.shape[-2]
        b2 = b0.shape[-2:]
        if a_ch and b_ch and not out_ch and fold:
            grid, red_axis, n_red = (n_rt,), None, 1
            a_idx, b_idx = (lambda i: (0, i + off, 0)), (lambda i: (0, 0, 0))
            a_blk = lambda a: (n_chunk, tm, a.shape[-1])
            b_blk = lambda b: tuple(b.shape)
            out_shape, out_spec = (rows, n_out), pl.BlockSpec((tm, n_out), lambda i: (i, 0))
        elif a_ch and b_ch and not out_ch:
            grid, red_axis, n_red = (n_rt, n_chunk), 1, n_chunk
            a_idx, b_idx = (lambda i, k: (k, i + off, 0)), (lambda i, k: (k, 0, 0))
            a_blk = lambda a: (None, tm, a.shape[-1])
            b_blk = lambda b: (None,) + tuple(b.shape[-2:])
            out_shape, out_spec = (rows, n_out), pl.BlockSpec((tm, n_out), lambda i, k: (i, 0))
        elif out_ch and fold:
            assert b_ch and not a_ch and all(a is a0 for a, _ in pairs)
            grid, red_axis, n_red = (n_rt,), None, 1
            a_idx, b_idx = (lambda i: (i + off, 0)), (lambda i: (0, 0, 0))
            a_blk = lambda a: (tm, a.shape[-1])
            b_blk = lambda b: tuple(b.shape)
            out_shape, out_spec = (n_chunk, rows, n_out), pl.BlockSpec((n_chunk, tm, n_out), lambda i: (0, i, 0))
        elif out_ch:
            assert b_ch and not a_ch
            grid, red_axis, n_red = (n_chunk, n_rt), None, 1
            a_idx, b_idx = (lambda k, i: (i + off, 0)), (lambda k, i: (k, 0, 0))
            a_blk = lambda a: (tm, a.shape[-1])
            b_blk = lambda b: (None,) + tuple(b.shape[-2:])
            out_shape, out_spec = (n_chunk, rows, n_out), pl.BlockSpec((None, tm, n_out), lambda k, i: (k, i, 0))
        else:
            assert not (a_ch or b_ch)
            grid, red_axis, n_red = (n_rt,), None, 1
            a_idx, b_idx = (lambda i: (i + off, 0)), (lambda i: (0, 0))
            a_blk = lambda a: (tm, a.shape[-1])
            b_blk = lambda b: tuple(b.shape)
            out_shape, out_spec = (rows, n_out), pl.BlockSpec((tm, n_out), lambda i: (i, 0))
            if place is not None:
                out_shape, o_off = (place[0], n_out), place[1] // tm
                out_spec = pl.BlockSpec((tm, n_out), lambda i: (i + o_off, 0))
        acc_shape = (tm, n_out)

    into = [] if place is None or place[2] is None else [place[2]]
    post_ins, post_fn, out_dtypes = ([], None, [out_dtype]) if post is None else post
    hosted = _Hosted(rider, 2 * n_pair + len(post_ins) + len(into), len(out_dtypes), int(n_red > 1), grid)

    def body(*refs):
        ins, outs, scr = hosted.split(refs)

        def compute():
            acc = None
            for p in range(n_pair):
                for k in ([None] if not fold else range(n_chunk)):
                    pick = (lambda r: r[...]) if k is None else (lambda r: r[k])
                    d = lax.dot_general(pick(ins[2 * p]).astype(BF16), pick(ins[2 * p + 1]).astype(BF16), dims,
                                        preferred_element_type=F32)
                    acc = d if acc is None else acc + d
            return acc

        def emit(acc):
            vals = (acc,) if post_fn is None else post_fn(
                acc, *[r[...].astype(F32) for r in ins[2 * n_pair:2 * n_pair + len(post_ins)]])
            for o_ref, v in zip(outs, vals):
                o_ref[...] = v.astype(o_ref.dtype)

        if out_ch and fold:
            a_tile = ins[0][...].astype(BF16)
            for k in range(n_chunk):
                accs = [lax.dot_general(a_tile, ins[2 * p + 1][k].astype(BF16), dims, preferred_element_type=F32)
                        for p in range(n_pair)]
                tiles = [r[k].astype(F32) for r in ins[2 * n_pair:2 * n_pair + len(post_ins)]]
                vals = tuple(accs) if post_fn is None else post_fn(*accs, *tiles)
                for o_ref, v in zip(outs, vals):
                    o_ref[k] = v.astype(o_ref.dtype)
        elif n_red == 1:
            emit(compute())
        else:
            acc_ref = scr[0]
            r = pl.program_id(red_axis)

            @pl.when(r == 0)
            def _():
                acc_ref[...] = jnp.zeros_like(acc_ref)

            acc_ref[...] += compute()

            @pl.when(r == n_red - 1)
            def _():
                emit(acc_ref[...])
        hosted.finish()

    in_specs, args, vmem = [], [], 0
    for a, b in pairs:
        in_specs += [pl.BlockSpec(a_blk(a), a_idx), pl.BlockSpec(b_blk(b), b_idx)]
        args += [a, b]
        vmem += 2 * (_nbytes([s for s in a_blk(a) if s], a.dtype) + _nbytes([s for s in b_blk(b) if s], b.dtype))
    in_specs += [out_spec] * len(post_ins)
    args += list(post_ins)
    aliases = {len(args): 0} if into else {}
    in_specs += [pl.BlockSpec(memory_space=pl.ANY)] * len(into)
    args += into
    tiles_per_step = n_chunk if (out_ch and fold) else 1
    vmem += (3 + 2 * n_pair + tiles_per_step * (len(post_ins) + len(out_dtypes))) * _nbytes(acc_shape, F32)
    scratch = [pltpu.VMEM(acc_shape, F32)] if n_red > 1 else []
    in_specs, out_shapes, out_specs, scratch, args = hosted.call_args(
        in_specs, [_big(out_shape, dt) for dt in out_dtypes], [out_spec] * len(out_dtypes), scratch, args)
    return hosted.results(pl.pallas_call(
        body, name=name, out_shape=out_shapes, grid=grid, in_specs=in_specs, out_specs=out_specs,
        input_output_aliases=aliases, scratch_shapes=scratch, compiler_params=_params(vmem + (8 << 20), len(grid)),
    )(*_in_hbm(args)))


def row(arr, width=None, cb=0, roff=0):
    return (arr, arr.shape[-1] if width is None else width, cb, roff)


def two_rows(first, second, limit):
    return (first, first.shape[-1], 0, 0, (second, limit))


def _row_inputs(rows, tm):
    specs, arrs, slots = [], [], []
    for d in rows:
        second, limit = d[4] if len(d) > 4 else (None, None)
        slots.append((len(arrs), limit))
        specs.append(_row_spec(d[:4], tm, limit))
        arrs.append(d[0])
        if second is not None:
            specs.append(pl.BlockSpec((tm, d[1]), lambda i, limit=limit: (jnp.maximum(i - limit, 0), 0)))
            arrs.append(second)

    def read(refs, i):
        vals = []
        for at, limit in slots:
            v = refs[at][...].astype(F32)
            vals.append(v if limit is None else jnp.where(i < limit, v, refs[at + 1][...].astype(F32)))
        return vals

    return specs, arrs, read


def _row_spec(desc, tm, limit=None):
    _, width, cb, roff = desc[:4]
    if limit is None:
        return pl.BlockSpec((tm, width), lambda i: (i + roff, cb))
    return pl.BlockSpec((tm, width), lambda i: (jnp.minimum(i, limit - 1) + roff, cb))


def _segmenter(tm, seq_len, n_lat):
    seg = lambda i: jnp.where(i * tm < n_lat, (i * tm) // seq_len, n_lat // seq_len)
    first = lambda i: jnp.where(i * tm < n_lat, (i * tm) % seq_len == 0, i * tm == n_lat)
    return seg, first


def rowwise(name, fn, rows, segs, params, outs, *, tm, n_tiles, seg_fn=None, rider=None):
    row_specs, row_arrs, read_rows = _row_inputs(rows, tm)
    n_r, n_s, n_p = len(row_arrs), len(segs), len(params)
    hosted = _Hosted(rider, n_r + n_s + n_p, len(outs), 0, (n_tiles,))

    def body(*refs):
        ins, out_refs, _ = hosted.split(refs)
        vals = read_rows(ins[:n_r], pl.program_id(0)) + [r[...] for r in ins[n_r:]]
        res = fn(*vals)
        for o_ref, v in zip(out_refs, res):
            o_ref[...] = v.astype(o_ref.dtype)
        hosted.finish()

    in_specs = list(row_specs)
    in_specs += [pl.BlockSpec((None, 1, s.shape[-1]), lambda i: (seg_fn(i), 0, 0)) for s in segs]
    in_specs += [pl.BlockSpec(p.shape, lambda i: (0, 0)) for p in params]
    vmem = sum(2 * tm * d[1] * 4 for d in rows) + sum(3 * tm * w * 4 for _, w, _ in outs) + sum(2 * p.size * 4 for p in params)
    in_specs, out_shapes, out_specs, scratch, args = hosted.call_args(
        in_specs, [_big((r, w), dt) for r, w, dt in outs],
        [pl.BlockSpec((tm, w), lambda i: (i, 0)) for _, w, _ in outs], [], row_arrs + list(segs) + list(params))
    return hosted.results(pl.pallas_call(
        body, name=name, grid=(n_tiles,), in_specs=in_specs, out_shape=out_shapes, out_specs=out_specs,
        scratch_shapes=scratch, compiler_params=_params(2 * vmem + (8 << 20)),
    )(*_in_hbm(args)), unwrap=False)


def rowwise_bwd(name, fn, rows, segs, params, cts, row_grads, *, tm, n_tiles, seg_fn=None, first_fn=None, adds=None,
                rider=None):
    adds = adds or {}
    need = [k for k, v in enumerate(row_grads) if v is not None]
    row_specs, row_arrs, read_rows = _row_inputs(rows, tm)
    n_r, n_s, n_p = len(row_arrs), len(segs), len(params)
    n_ct = sum(len(lst) for lst in cts)
    add_keys = sorted(adds)
    hosted = _Hosted(rider, n_r + n_s + n_p + n_ct + len(add_keys), len(need) + n_s + n_p, 0, (n_tiles,))

    def body(*refs):
        host_in, host_out, _ = hosted.split(refs)
        it = iter(list(host_in) + list(host_out))
        row_refs = [next(it) for _ in range(n_r)]
        seg_refs = [next(it) for _ in range(n_s)]
        par_refs = [next(it) for _ in range(n_p)]
        ct_refs = [[next(it) for _ in lst] for lst in cts]
        add_refs = {k: next(it) for k in add_keys}
        rg_refs = {k: next(it) for k in need}
        sg_refs = [next(it) for _ in range(n_s)]
        pg_refs = [next(it) for _ in range(n_p)]
        i = pl.program_id(0)
        rv = read_rows(row_refs, i)
        sv = [r[...] for r in seg_refs]
        pv = [r[...] for r in par_refs]

        def f(*args):
            rr = list(rv)
            for j, k in enumerate(need):
                rr[k] = args[j]
            return fn(*rr, *args[len(need):])

        _, vjp = jax.vjp(f, *[rv[k] for k in need], *sv, *pv)
        ctv = []
        for lst in ct_refs:
            acc = lst[0][...].astype(F32)
            for r in lst[1:]:
                acc = acc + r[...].astype(F32)
            ctv.append(acc)
        g = vjp(tuple(ctv))
        for j, k in enumerate(need):
            gv = g[j]
            if k in adds:
                lim = adds[k][1]
                av = add_refs[k][...].astype(F32)
                gv = gv + (av if lim is None else jnp.where(i < lim, av, 0.0))
            lim = row_grads[k][2]
            if lim is None:
                rg_refs[k][...] = gv.astype(rg_refs[k].dtype)
            else:
                @pl.when(i < lim)
                def _(gv=gv, k=k):
                    rg_refs[k][...] = gv.astype(rg_refs[k].dtype)
        if n_s:
            opens = first_fn(i)
            for ref, gv in zip(sg_refs, g[len(need):len(need) + n_s]):
                @pl.when(opens)
                def _(ref=ref, gv=gv):
                    ref[...] = gv

                @pl.when(jnp.logical_not(opens))
                def _(ref=ref, gv=gv):
                    ref[...] += gv
        for ref, gv in zip(pg_refs, g[len(need) + n_s:]):
            @pl.when(i == 0)
            def _(ref=ref, gv=gv):
                ref[...] = gv

            @pl.when(i > 0)
            def _(ref=ref, gv=gv):
                ref[...] += gv
        hosted.finish()

    seg_spec = lambda s: pl.BlockSpec((None, 1, s.shape[-1]), lambda i: (seg_fn(i), 0, 0))
    par_spec = lambda p: pl.BlockSpec(p.shape, lambda i: (0, 0))
    in_specs = list(row_specs) + [seg_spec(s) for s in segs] + [par_spec(p) for p in params]
    args = row_arrs + list(segs) + list(params)
    for lst in cts:
        in_specs += [_row_spec(d, tm) for d in lst]
        args += [d[0] for d in lst]
    for k in add_keys:
        in_specs.append(_row_spec(adds[k][0], tm, adds[k][1]))
        args.append(adds[k][0][0])
    out_shape, out_specs = [], []
    for k in need:
        n_rows, dt, lim = row_grads[k]
        out_shape.append(_big((n_rows, rows[k][1]), dt))
        out_specs.append(_row_spec((None, rows[k][1], 0, 0), tm, lim))
    for s in segs:
        out_shape.append(jax.ShapeDtypeStruct(s.shape, F32))
        out_specs.append(seg_spec(s))
    for p in params:
        out_shape.append(jax.ShapeDtypeStruct(p.shape, F32))
        out_specs.append(par_spec(p))
    vmem = sum(tm * d[1] * 4 for d in rows) * 6 + n_ct * tm * max(d[1] for d in rows) * 8
    in_specs, out_shape, out_specs, scratch, args = hosted.call_args(in_specs, out_shape, out_specs, [], args)
    return hosted.results(pl.pallas_call(
        body, name=name, grid=(n_tiles,), in_specs=in_specs, out_shape=out_shape, out_specs=out_specs,
        scratch_shapes=scratch, compiler_params=_params(vmem + (8 << 20)),
    )(*_in_hbm(args)), unwrap=False)


def _silu(v):
    return v * jax.nn.sigmoid(v)


def _rms(v, w):
    return v * lax.rsqrt(jnp.mean(v * v, axis=-1, keepdims=True) + EPS) * w


def fn_norm_mod(x, shift, scale, w):
    return (_rms(x, w) * (1.0 + scale) + shift,)


def fn_act(g, u):
    return (_silu(g) * u,)


def make_fn_resid(coef):
    def fn(x, f, gate):
        return (x + coef * gate * f,)
    return fn


def fn_silu_bias(v, b):
    return (_silu(v + b),)


def make_fn_gate_groupnorm(width):
    half = width // 2

    def fn(y_both, z, w):
        y = y_both * _silu(z)
        lane = lax.broadcasted_iota(jnp.int32, y.shape, 1)
        lo = lane < half
        sq = y * y
        s_lo = jnp.sum(jnp.where(lo, sq, 0.0), axis=-1, keepdims=True)
        s_hi = jnp.sum(jnp.where(lo, 0.0, sq), axis=-1, keepdims=True)
        r = jnp.where(lo, lax.rsqrt(s_lo / half + EPS), lax.rsqrt(s_hi / half + EPS))
        return (y * r * w,)
    return fn


def fn_glu(a, b):
    return (a * jax.nn.sigmoid(b),)


def fn_ln_silu(vw, vh, cb, lw, lb):
    v = jnp.concatenate([vw, vh], axis=-1) + cb
    mu = jnp.mean(v, axis=-1, keepdims=True)
    var = jnp.mean(jnp.square(v - mu), axis=-1, keepdims=True)
    return (_silu((v - mu) * lax.rsqrt(var + EPS) * lw + lb),)


def _col_tile(width):
    return width // 3 if width % (3 * LANES) == 0 else width


def mod_fwd(a_rows, w_shard, b_shard):
    n, d = a_rows.shape
    ws = w_shard.shape[1]
    tn = _col_tile(ws)

    def body(a_ref, w_ref, b_ref, o_ref):
        a = _silu(a_ref[...]).astype(BF16)
        o_ref[...] = jnp.dot(a, w_ref[...].astype(BF16), preferred_element_type=F32) + b_ref[...]

    return pl.pallas_call(
        body, name="mod_fwd", grid=(ws // tn,), out_shape=jax.ShapeDtypeStruct((n, ws), F32),
        in_specs=[pl.BlockSpec((n, d), lambda j: (0, 0)), pl.BlockSpec((d, tn), lambda j: (0, j)),
                  pl.BlockSpec((1, tn), lambda j: (0, j))],
        out_specs=pl.BlockSpec((n, tn), lambda j: (0, j)), compiler_params=_params(),
    )(a_rows, w_shard, b_shard)


def mod_bwd(a_rows, d_shard, d_full, w_shard, ctx_rows):
    n, d = a_rows.shape
    ws = w_shard.shape[1]
    tn = _col_tile(ws)
    n_ct = ws // tn

    def body(a_ref, ds_ref, df_ref, w_ref, gw_ref, gb_ref, q_ref):
        j = pl.program_id(0)
        a = _silu(a_ref[...])
        ds = ds_ref[...]
        gw_ref[...] = lax.dot_general(a, ds, _DIMS["tn"], precision=HI, preferred_element_type=F32)
        dctx = ds[ctx_rows[0]:ctx_rows[0] + 1, :]
        for r in ctx_rows[1:]:
            dctx = dctx + ds[r:r + 1, :]
        q = lax.dot_general(jnp.broadcast_to(dctx, (8, tn)), w_ref[...], _DIMS["nt"], precision=HI,
                            preferred_element_type=F32)

        @pl.when(j == 0)
        def _():
            q_ref[...] = q
            df = df_ref[...]
            acc = df[0:1, :]
            for r in range(1, n):
                acc = acc + df[r:r + 1, :]
            gb_ref[...] = acc

        @pl.when(j > 0)
        def _():
            q_ref[...] += q

    return pl.pallas_call(
        body, name="mod_bwd", grid=(n_ct,),
        out_shape=(jax.ShapeDtypeStruct((d, ws), F32), jax.ShapeDtypeStruct((1, d_full.shape[1]), F32),
                   jax.ShapeDtypeStruct((8, d), F32)),
        in_specs=[pl.BlockSpec((n, d), lambda j: (0, 0)), pl.BlockSpec((n, tn), lambda j: (0, j)),
                  pl.BlockSpec(d_full.shape, lambda j: (0, 0)), pl.BlockSpec((d, tn), lambda j: (0, j))],
        out_specs=(pl.BlockSpec((d, tn), lambda j: (0, j)), pl.BlockSpec((1, d_full.shape[1]), lambda j: (0, 0)),
                   pl.BlockSpec((8, d), lambda j: (0, 0))),
        compiler_params=_params(40 << 20),
    )(a_rows, d_shard, d_full, w_shard)


def _shifted(xs, d, tok, width):
    if d == 0:
        return xs
    n = xs.shape[0]
    sh = pltpu.roll(xs, (-d) % n, axis=0)
    return jnp.where((tok + d >= 0) & (tok + d < width), sh, 0.0)


def _placed(out_shape, place):
    if place is None:
        return out_shape, 0, 0, None
    return place


def tapsum_roll(name, x, xcb, w, wcb, *, seq_len, n_seq, row_blk_off, width, piece, cb, ncb, pad, flip, place=None,
                out_dtype=F32):
    n_tap = w.shape[0]
    n_piece = seq_len // piece
    out_shape, o_rb, o_cb, into = _placed((n_seq * seq_len, ncb * cb), place)

    def body(x_ref, w_ref, *rest):
        o_ref = rest[-1]
        wv = w_ref[...]
        tok = lax.broadcasted_iota(jnp.int32, (piece, 1), 0) % width

        def do_piece(p, carry):
            start = pl.multiple_of(p * piece, piece)
            xs = x_ref[pl.ds(start, piece), :].astype(F32)
            acc = jnp.zeros_like(xs)
            for k in range(n_tap):
                d = pad - k if flip else k - pad
                acc = acc + wv[k:k + 1, :] * _shifted(xs, d, tok, width)
            o_ref[pl.ds(start, piece), :] = acc.astype(o_ref.dtype)
            return carry

        lax.fori_loop(0, n_piece, do_piece, 0)

    extra = [] if into is None else [into]
    return pl.pallas_call(
        body, name=name, grid=(ncb, n_seq), out_shape=_big(out_shape, out_dtype),
        in_specs=[pl.BlockSpec((seq_len, cb), lambda j, s: (row_blk_off + s, xcb + j)),
                  pl.BlockSpec((n_tap, cb), lambda j, s: (0, wcb + j))] + [pl.BlockSpec(memory_space=pl.ANY)] * len(extra),
        out_specs=pl.BlockSpec((seq_len, cb), lambda j, s: (o_rb + s, o_cb + j)),
        input_output_aliases={2: 0} if extra else {},
        compiler_params=_params(8 * seq_len * cb * 4 + (8 << 20), 2),
    )(*_in_hbm([x, w] + extra))


def tapgrad_roll(name, dy, dycb, dy_blk_off, x, xcb, x_blk_off, *, n_tap, seq_len, n_seq, width, piece, cb, ncb, pad):
    n_piece = seq_len // piece

    def body(dy_ref, x_ref, o_ref):
        @pl.when(pl.program_id(1) == 0)
        def _():
            o_ref[...] = jnp.zeros_like(o_ref)

        tok = lax.broadcasted_iota(jnp.int32, (piece, 1), 0) % width

        def do_piece(p, carry):
            start = pl.multiple_of(p * piece, piece)
            xs = x_ref[pl.ds(start, piece), :].astype(F32)
            dv = dy_ref[pl.ds(start, piece), :]
            for k in range(n_tap):
                o_ref[k:k + 1, :] += jnp.sum(dv * _shifted(xs, k - pad, tok, width), axis=0, keepdims=True)
            return carry

        lax.fori_loop(0, n_piece, do_piece, 0)

    return pl.pallas_call(
        body, name=name, grid=(ncb, n_seq), out_shape=jax.ShapeDtypeStruct((n_tap, ncb * cb), F32),
        in_specs=[pl.BlockSpec((seq_len, cb), lambda j, s: (dy_blk_off + s, dycb + j)),
                  pl.BlockSpec((seq_len, cb), lambda j, s: (x_blk_off + s, xcb + j))],
        out_specs=pl.BlockSpec((n_tap, cb), lambda j, s: (0, j)),
        compiler_params=_params(8 * seq_len * cb * 4 + (8 << 20), 2),
    )(*_in_hbm([dy, x]))


def tapsum_rows(name, x, xcb, w, wcb, *, seq_len, n_seq, cb, ncb, pad, flip, place=None):
    n_tap = w.shape[0]
    n_row = seq_len // GRID_W
    halo = pad * GRID_W
    out_shape, o_rb, o_cb, into = _placed((n_seq * seq_len, ncb * cb), place)

    def body(x_ref, w_ref, *rest):
        o_ref, xp = rest[-2:]
        xp[pl.ds(0, halo), :] = jnp.zeros((halo, cb), F32)
        xp[pl.ds(halo + seq_len, halo), :] = jnp.zeros((halo, cb), F32)
        xp[pl.ds(halo, seq_len), :] = x_ref[...].astype(F32)
        wv = w_ref[...]

        def do_row(r, carry):
            acc = jnp.zeros((GRID_W, cb), F32)
            for k in range(n_tap):
                d = pad - k if flip else k - pad
                acc = acc + wv[k:k + 1, :] * xp[pl.ds(pl.multiple_of((r + pad + d) * GRID_W, GRID_W), GRID_W), :]
            o_ref[pl.ds(pl.multiple_of(r * GRID_W, GRID_W), GRID_W), :] = acc
            return carry

        lax.fori_loop(0, n_row, do_row, 0)

    extra = [] if into is None else [into]
    return pl.pallas_call(
        body, name=name, grid=(ncb, n_seq), out_shape=_big(out_shape, F32),
        in_specs=[pl.BlockSpec((seq_len, cb), lambda j, s: (s, xcb + j)),
                  pl.BlockSpec((n_tap, cb), lambda j, s: (0, wcb + j))] + [pl.BlockSpec(memory_space=pl.ANY)] * len(extra),
        out_specs=pl.BlockSpec((seq_len, cb), lambda j, s: (o_rb + s, o_cb + j)),
        input_output_aliases={2: 0} if extra else {},
        scratch_shapes=[pltpu.VMEM((seq_len + 2 * halo, cb), F32)],
        compiler_params=_params(10 * seq_len * cb * 4 + (8 << 20), 2),
    )(*_in_hbm([x, w] + extra))


def tapgrad_rows(name, dy, dycb, x, xcb, *, n_tap, seq_len, n_seq, cb, ncb, pad):
    n_row = seq_len // GRID_W
    halo = pad * GRID_W

    def body(dy_ref, x_ref, o_ref, xp):
        @pl.when(pl.program_id(1) == 0)
        def _():
            o_ref[...] = jnp.zeros_like(o_ref)

        xp[pl.ds(0, halo), :] = jnp.zeros((halo, cb), F32)
        xp[pl.ds(halo + seq_len, halo), :] = jnp.zeros((halo, cb), F32)
        xp[pl.ds(halo, seq_len), :] = x_ref[...].astype(F32)

        def do_row(r, carry):
            dv = dy_ref[pl.ds(pl.multiple_of(r * GRID_W, GRID_W), GRID_W), :]
            for k in range(n_tap):
                xs = xp[pl.ds(pl.multiple_of((r + k) * GRID_W, GRID_W), GRID_W), :]
                o_ref[k:k + 1, :] += jnp.sum(dv * xs, axis=0, keepdims=True)
            return carry

        lax.fori_loop(0, n_row, do_row, 0)

    return pl.pallas_call(
        body, name=name, grid=(ncb, n_seq), out_shape=jax.ShapeDtypeStruct((n_tap, ncb * cb), F32),
        in_specs=[pl.BlockSpec((seq_len, cb), lambda j, s: (s, dycb + j)),
                  pl.BlockSpec((seq_len, cb), lambda j, s: (s, xcb + j))],
        out_specs=pl.BlockSpec((n_tap, cb), lambda j, s: (0, j)),
        scratch_shapes=[pltpu.VMEM((seq_len + 2 * halo, cb), F32)],
        compiler_params=_params(10 * seq_len * cb * 4 + (8 << 20), 2),
    )(*_in_hbm([dy, x]))


def _ssd_blocks(b, s, *, rev, n_ctx, n_lat, lat_blocks):
    if rev:
        return jnp.where(s < n_ctx, lat_blocks + b * n_ctx + (n_ctx - 1 - s), b * n_lat + (n_lat - 1 - (s - n_ctx)))
    return jnp.where(s < n_ctx, lat_blocks + b * n_ctx + s, b * n_lat + (s - n_ctx))


def _ssd_common(xbc, raw, dtb, alog, dsk, *, rev, ds, n_head):
    if rev:
        raw = pltpu.roll(raw, LANES - n_head, axis=1)
    pre = raw + dtb
    dt = jnp.maximum(pre, 0.0) + jnp.log(1.0 + jnp.exp(-jnp.abs(pre)))
    sig = jax.nn.sigmoid(pre)
    a = -jnp.exp(alog)
    da = dt * a
    ri = lax.broadcasted_iota(jnp.int32, (CHUNK, CHUNK), 0)
    ci = lax.broadcasted_iota(jnp.int32, (CHUNK, CHUNK), 1)
    mask = (ci >= ri) if rev else (ci <= ri)
    tri = mask.astype(F32)
    tri_t = ((ci <= ri) if rev else (ci >= ri)).astype(F32)
    cs = jnp.dot(tri, da, precision=HI, preferred_element_type=F32)
    tot = jnp.sum(da, axis=0, keepdims=True)
    def wide(v):
        first = lax.broadcasted_iota(jnp.int32, (v.shape[0], LANES), 1) < HEAD_DIM
        return jnp.concatenate(
            [jnp.where(first, jnp.broadcast_to(v[:, 2 * p:2 * p + 1], first.shape),
                       jnp.broadcast_to(v[:, 2 * p + 1:2 * p + 2], first.shape)) for p in range(n_head // 2)], axis=1)

    cs_w, tot_w = wide(cs), wide(tot)
    xh = xbc[:, :ds]
    dt_w = wide(dt)
    return dict(
        dt=dt, sig=sig, a=a, cs=cs, cs_t=cs.T, tot=tot, mask=mask, tri_t=tri_t,
        e_w=jnp.exp(cs_w), wt_w=jnp.exp(tot_w - cs_w), dec_w=jnp.exp(tot_w), dt_w=dt_w, dsk_w=wide(dsk),
        xh=xh, xs_w=xh * dt_w, bm=xbc[:, ds:ds + 2 * N_STATE], cm=xbc[:, ds + 2 * N_STATE:ds + 4 * N_STATE])


def _decay(q, col):
    seg = q["cs"][:, col:col + 1] - q["cs_t"][col:col + 1, :]
    return jnp.exp(jnp.where(q["mask"], seg, -jnp.inf))


def _split_heads(v):
    lane = lax.broadcasted_iota(jnp.int32, v.shape, 1)
    return jnp.concatenate([jnp.where(lane < HEAD_DIM, v, 0.0), jnp.where(lane >= HEAD_DIM, v, 0.0)], axis=0)


def ssd_fwd(name, xbc, dt_raw, dt_cb, dtb, alog, dsk, *, rev, n_ex, seq_len, ctx_len, ds, rider=None, add=None):
    n_head, half = ds // HEAD_DIM, ds // 2
    n_ctx, n_lat = ctx_len // CHUNK, seq_len // CHUNK
    n_step = n_ctx + n_lat
    blk = functools.partial(_ssd_blocks, rev=rev, n_ctx=n_ctx, n_lat=n_lat, lat_blocks=n_ex * n_lat)
    xw = xbc.shape[1]

    def y_blk(b, s):
        sl = jnp.maximum(s, n_ctx) - n_ctx
        return b * n_lat + ((n_lat - 1 - sl) if rev else sl)

    hosted = _Hosted(rider, 5 + (add is not None), 2, 1, (n_ex, n_step))

    def body(*refs):
        (xbc_ref, dt_ref, dtb_ref, alog_ref, dsk_ref, *add_ref), (y_ref, hs_ref), (h_scr,) = hosted.split(refs)

        @pl.when(pl.program_id(1) == 0)
        def _():
            h_scr[...] = jnp.zeros_like(h_scr)

        q = _ssd_common(xbc_ref[...], dt_ref[...], dtb_ref[...], alog_ref[...], dsk_ref[...], rev=rev, ds=ds, n_head=n_head)
        h = h_scr[...]
        hs_ref[...] = h
        for g in range(2):
            lo = g * half
            bg = q["bm"][:, g * N_STATE:(g + 1) * N_STATE].astype(BF16)
            cg = q["cm"][:, g * N_STATE:(g + 1) * N_STATE].astype(BF16)
            scores = lax.dot_general(cg, bg, _DIMS["nt"], preferred_element_type=F32)
            hg = h[:, lo:lo + half]
            off = jnp.dot(cg, hg.astype(BF16), preferred_element_type=F32)
            for j in range(half // LANES):
                c0 = (lo + j * LANES) // HEAD_DIM
                ln = slice(lo + j * LANES, lo + (j + 1) * LANES)
                p_cat = jnp.concatenate([scores * _decay(q, c0), scores * _decay(q, c0 + 1)], axis=1).astype(BF16)
                diag = jnp.dot(p_cat, _split_heads(q["xs_w"][:, ln]).astype(BF16), preferred_element_type=F32)
                y_ref[:, ln] = (diag + q["e_w"][:, ln] * off[:, j * LANES:(j + 1) * LANES]
                                + q["dsk_w"][:, ln] * q["xh"][:, ln] + (add_ref[0][:, ln] if add_ref else 0.0))
            v = (q["wt_w"][:, lo:lo + half] * q["xs_w"][:, lo:lo + half]).astype(BF16)
            h_scr[:, lo:lo + half] = (q["dec_w"][:, lo:lo + half] * hg
                                      + lax.dot_general(bg, v, _DIMS["tn"], preferred_element_type=F32))
        hosted.finish()

    vec = pl.BlockSpec((1, LANES), lambda b, s: (0, 0))
    in_specs, out_shape, out_specs, scratch, args = hosted.call_args(
        [pl.BlockSpec((CHUNK, xw), lambda b, s: (blk(b, s), 0)),
         pl.BlockSpec((CHUNK, LANES), lambda b, s: (blk(b, s), dt_cb)), vec, vec, vec]
        + [pl.BlockSpec((CHUNK, ds), lambda b, s: (y_blk(b, s), 0))] * (add is not None),
        (_big((n_ex * seq_len, ds), F32), _big((n_ex, n_step, N_STATE, ds), F32)),
        (pl.BlockSpec((CHUNK, ds), lambda b, s: (y_blk(b, s), 0)),
         pl.BlockSpec((None, None, N_STATE, ds), lambda b, s: (b, s, 0, 0))),
        [pltpu.VMEM((N_STATE, ds), F32)], [xbc, dt_raw, dtb, alog, dsk] + ([] if add is None else [add]))
    return hosted.results(pl.pallas_call(
        body, name=name, grid=(n_ex, n_step), out_shape=out_shape, in_specs=in_specs, out_specs=out_specs,
        scratch_shapes=scratch, compiler_params=_params(40 << 20, 2),
    )(*_in_hbm(args)))


def ssd_bwd(name, xbc, dt_raw, dt_cb, hs, dy, dtb, alog, dsk, *, rev, n_ex, seq_len, ctx_len, ds, rider=None, add=None):
    n_head, half = ds // HEAD_DIM, ds // 2
    n_ctx, n_lat = ctx_len // CHUNK, seq_len // CHUNK
    n_step = n_ctx + n_lat
    n_tok = n_ex * (seq_len + ctx_len)
    blk0 = functools.partial(_ssd_blocks, rev=rev, n_ctx=n_ctx, n_lat=n_lat, lat_blocks=n_ex * n_lat)
    step = lambda sp: n_step - 1 - sp
    blk = lambda b, sp: blk0(b, step(sp))
    xw = xbc.shape[1]

    def dy_blk(b, sp):
        sl = jnp.maximum(step(sp), n_ctx) - n_ctx
        return b * n_lat + ((n_lat - 1 - sl) if rev else sl)

    hosted = _Hosted(rider, 7 + (add is not None), 5, 1, (n_ex, n_step))

    def body(*refs):
        ((xbc_ref, dt_ref, hs_ref, dy_ref, dtb_ref, alog_ref, dsk_ref, *add_ref),
         (dxbc_ref, ddt_ref, dalog_ref, ddtb_ref, ddsk_ref), (dh_scr,)) = hosted.split(refs)
        b, sp = pl.program_id(0), pl.program_id(1)
        more = (lambda cols: add_ref[0][:, cols]) if add_ref else (lambda cols: 0.0)

        @pl.when(sp == 0)
        def _():
            dh_scr[...] = jnp.zeros_like(dh_scr)

        @pl.when((sp == 0) & (b == 0))
        def _():
            dalog_ref[...] = jnp.zeros_like(dalog_ref)
            ddtb_ref[...] = jnp.zeros_like(ddtb_ref)
            ddsk_ref[...] = jnp.zeros_like(ddsk_ref)

        q = _ssd_common(xbc_ref[...], dt_ref[...], dtb_ref[...], alog_ref[...], dsk_ref[...], rev=rev, ds=ds, n_head=n_head)
        h = hs_ref[...]
        d_y = jnp.where(step(sp) >= n_ctx, dy_ref[...], 0.0)
        dh_next = dh_scr[...]
        lane_row = lax.broadcasted_iota(jnp.int32, (1, LANES), 1)
        d_cs = jnp.zeros((CHUNK, LANES), F32)
        dxs_parts, de_parts, dwt_parts, ddec_parts = [], [], [], []
        for g in range(2):
            lo = g * half
            gs = slice(lo, lo + half)
            bg = q["bm"][:, g * N_STATE:(g + 1) * N_STATE].astype(BF16)
            cg = q["cm"][:, g * N_STATE:(g + 1) * N_STATE].astype(BF16)
            scores = lax.dot_general(cg, bg, _DIMS["nt"], preferred_element_type=F32)
            hg, dyg, dhn = h[:, gs], d_y[:, gs], dh_next[:, gs]
            off = jnp.dot(cg, hg.astype(BF16), preferred_element_type=F32)
            d_off = (q["e_w"][:, gs] * dyg).astype(BF16)
            de_parts.append(dyg * off)
            d_c = lax.dot_general(d_off, hg.astype(BF16), _DIMS["nt"], preferred_element_type=F32)
            dh_scr[:, gs] = (lax.dot_general(cg, d_off, _DIMS["tn"], preferred_element_type=F32)
                             + q["dec_w"][:, gs] * dhn)
            b_dh = jnp.dot(bg, dhn.astype(BF16), preferred_element_type=F32)
            v = q["wt_w"][:, gs] * q["xs_w"][:, gs]
            d_b = lax.dot_general(v.astype(BF16), dhn.astype(BF16), _DIMS["nt"], preferred_element_type=F32)
            dwt_parts.append(q["xs_w"][:, gs] * b_dh)
            ddec_parts.append(jnp.sum(hg * dhn, axis=0, keepdims=True))
            d_scores = jnp.zeros((CHUNK, CHUNK), F32)
            for j in range(half // LANES):
                c0 = (lo + j * LANES) // HEAD_DIM
                ln = slice(lo + j * LANES, lo + (j + 1) * LANES)
                l0, l1 = _decay(q, c0), _decay(q, c0 + 1)
                p0, p1 = scores * l0, scores * l1
                dy_st = _split_heads(d_y[:, ln]).astype(BF16)
                d_p = lax.dot_general(dy_st, q["xs_w"][:, ln].astype(BF16), _DIMS["nt"], preferred_element_type=F32)
                d_p0, d_p1 = d_p[:CHUNK], d_p[CHUNK:]
                d_scores = d_scores + d_p0 * l0 + d_p1 * l1
                for col, t in ((c0, d_p0 * p0), (c0 + 1, d_p1 * p1)):
                    d_cs = d_cs + jnp.sum(t - t.T, axis=1, keepdims=True) * (lane_row == col).astype(F32)
                p_st = jnp.concatenate([p0, p1], axis=0).astype(BF16)
                dxs_parts.append(lax.dot_general(p_st, dy_st, _DIMS["tn"], preferred_element_type=F32)
                                 + q["wt_w"][:, ln] * b_dh[:, j * LANES:(j + 1) * LANES])
            d_sc = d_scores.astype(BF16)
            d_c = d_c + jnp.dot(d_sc, bg, preferred_element_type=F32)
            d_b = d_b + lax.dot_general(d_sc, cg, _DIMS["tn"], preferred_element_type=F32)
            b_cols, c_cols = slice(ds + g * N_STATE, ds + (g + 1) * N_STATE), slice(ds + (2 + g) * N_STATE, ds + (3 + g) * N_STATE)
            dxbc_ref[:, b_cols] = d_b + more(b_cols)
            dxbc_ref[:, c_cols] = d_c + more(c_cols)
        d_xs = jnp.concatenate(dxs_parts, axis=1)
        narrow_m = (lax.broadcasted_iota(jnp.int32, (ds, LANES), 0) // HEAD_DIM
                    == lax.broadcasted_iota(jnp.int32, (ds, LANES), 1)).astype(BF16)
        rows8 = lambda v: jnp.broadcast_to(v, (8, ds))
        stacked = jnp.concatenate(
            [jnp.concatenate(dwt_parts, axis=1), jnp.concatenate(de_parts, axis=1), d_xs * q["xh"],
             rows8(jnp.concatenate(ddec_parts, axis=1)), rows8(jnp.sum(d_y * q["xh"], axis=0, keepdims=True))], axis=0)
        sums = jnp.dot(stacked.astype(BF16), narrow_m, preferred_element_type=F32)
        n_wt, n_e, n_xs = sums[:CHUNK], sums[CHUNK:2 * CHUNK], sums[2 * CHUNK:3 * CHUNK]
        n_dec, n_dsk = sums[3 * CHUNK:3 * CHUNK + 1], sums[3 * CHUNK + 8:3 * CHUNK + 9]
        e, wt, dec = jnp.exp(q["cs"]), jnp.exp(q["tot"] - q["cs"]), jnp.exp(q["tot"])
        d_wt = n_wt * wt
        d_cs = d_cs + n_e * e - d_wt
        d_tot = jnp.sum(d_wt, axis=0, keepdims=True) + n_dec * dec
        d_da = jnp.dot(q["tri_t"], d_cs, precision=HI, preferred_element_type=F32) + d_tot
        d_dt = d_da * q["a"] + n_xs
        dxbc_ref[:, :ds] = d_xs * q["dt_w"] + q["dsk_w"] * d_y + more(slice(0, ds))
        dalog_ref[...] += jnp.sum(d_da * q["dt"], axis=0, keepdims=True) * q["a"]
        d_raw = d_dt * q["sig"]
        ddtb_ref[...] += jnp.sum(d_raw, axis=0, keepdims=True)
        ddsk_ref[...] += n_dsk
        ddt_ref[...] = pltpu.roll(d_raw, n_head, axis=1) if rev else d_raw
        hosted.finish()

    vec = pl.BlockSpec((1, LANES), lambda b, s: (0, 0))
    vec_shape = jax.ShapeDtypeStruct((1, LANES), F32)
    in_specs, out_shape, out_specs, scratch, args = hosted.call_args(
        [pl.BlockSpec((CHUNK, xw), lambda b, s: (blk(b, s), 0)),
         pl.BlockSpec((CHUNK, LANES), lambda b, s: (blk(b, s), dt_cb)),
         pl.BlockSpec((None, None, N_STATE, ds), lambda b, s: (b, step(s), 0, 0)),
         pl.BlockSpec((CHUNK, ds), lambda b, s: (dy_blk(b, s), 0)), vec, vec, vec]
        + [pl.BlockSpec((CHUNK, xw), lambda b, s: (blk(b, s), 0))] * (add is not None),
        (_big((n_tok, xw), F32), _big((n_tok, LANES), F32), vec_shape, vec_shape, vec_shape),
        (pl.BlockSpec((CHUNK, xw), lambda b, s: (blk(b, s), 0)),
         pl.BlockSpec((CHUNK, LANES), lambda b, s: (blk(b, s), 0)), vec, vec, vec),
        [pltpu.VMEM((N_STATE, ds), F32)], [xbc, dt_raw, hs, dy, dtb, alog, dsk] + ([] if add is None else [add]))
    return hosted.results(pl.pallas_call(
        body, name=name, grid=(n_ex, n_step), out_shape=out_shape, in_specs=in_specs, out_specs=out_specs,
        scratch_shapes=scratch, compiler_params=_params(48 << 20, 2),
    )(*_in_hbm(args)))


def final_loss(x3, target, w, *, tm):
    n, d = x3.shape

    def body(x_ref, t_ref, w_ref, dx_ref, dw_ref, loss_ref):
        i = pl.program_id(0)
        t = t_ref[...]

        def per_feature(xv, wv):
            err = _rms(xv, wv) - t
            return 0.5 * jnp.sum(err * err, axis=0, keepdims=True) / d

        lv, vjp = jax.vjp(per_feature, x_ref[...], w_ref[...])
        dx, dw = vjp(jnp.ones_like(lv))
        dx_ref[...] = dx

        @pl.when(i == 0)
        def _():
            dw_ref[...] = dw
            loss_ref[...] = lv

        @pl.when(i > 0)
        def _():
            dw_ref[...] += dw
            loss_ref[...] += lv

    tile = pl.BlockSpec((tm, d), lambda i: (i, 0))
    vec = pl.BlockSpec((1, d), lambda i: (0, 0))
    return pl.pallas_call(
        body, name="final_loss", grid=(n // tm,), in_specs=[tile, tile, vec],
        out_shape=(jax.ShapeDtypeStruct((n, d), F32), jax.ShapeDtypeStruct((1, d), F32), jax.ShapeDtypeStruct((1, d), F32)),
        out_specs=(tile, vec, vec), compiler_params=_params(tm * d * 4 * 16 + (8 << 20)),
    )(x3, target, w)


def sum_slots(name, arr, out_dtype=F32):
    n_slot, n_row, width = arr.shape
    tm = _row_tile(n_row, width * n_slot, cap_bytes=STREAM_TILE_BYTES * 14, mult=16)
    vmem = 2 * n_slot * tm * width * arr.dtype.itemsize + 4 * tm * width * 4

    def body(a_ref, o_ref):
        acc = a_ref[0].astype(F32)
        for j in range(1, n_slot):
            acc = acc + a_ref[j].astype(F32)
        o_ref[...] = acc.astype(o_ref.dtype)

    return pl.pallas_call(
        body, name=name, grid=(n_row // tm,), out_shape=jax.ShapeDtypeStruct((n_row, width), out_dtype),
        in_specs=[pl.BlockSpec((n_slot, tm, width), lambda i: (0, i, 0))],
        out_specs=pl.BlockSpec((tm, width), lambda i: (i, 0)), compiler_params=_params(vmem + (4 << 20)),
    )(arr)


def adamw(name, w, g_slots, m, v):
    n_slot, n_row, width = g_slots.shape
    tm = _row_tile(n_row, width, cap_bytes=STREAM_TILE_BYTES)
    if g_slots.dtype == BF16 and tm % 16:
        tm16 = _row_tile(n_row, width, cap_bytes=STREAM_TILE_BYTES, mult=16)
        if tm16 % 16 == 0:
            tm = tm16
        else:
            g_slots = g_slots.astype(F32)

    def body(w_ref, g_ref, m_ref, v_ref, go_ref, d_ref, mo_ref, vo_ref):
        g = g_ref[0].astype(F32)
        for j in range(1, n_slot):
            g = g + g_ref[j].astype(F32)
        m2 = ADAM_B1 * m_ref[...] + (1.0 - ADAM_B1) * g
        v2 = ADAM_B2 * v_ref[...] + (1.0 - ADAM_B2) * jnp.square(g)
        m_hat = m2 / (1.0 - ADAM_B1 ** ADAM_STEP)
        v_hat = v2 / (1.0 - ADAM_B2 ** ADAM_STEP)
        go_ref[...] = g
        d_ref[...] = -ADAM_LR * (m_hat / (jnp.sqrt(v_hat) + ADAM_EPS) + ADAM_WD * w_ref[...])
        mo_ref[...] = m2
        vo_ref[...] = v2

    tile = pl.BlockSpec((tm, width), lambda i: (i, 0))
    shape = jax.ShapeDtypeStruct((n_row, width), F32)
    return pl.pallas_call(
        body, name=name, grid=(n_row // tm,), out_shape=(shape,) * 4,
        in_specs=[tile, pl.BlockSpec((n_slot, tm, width), lambda i: (0, i, 0)), tile, tile],
        out_specs=(tile,) * 4, compiler_params=_params(2 * (7 + n_slot) * tm * width * 4 + (4 << 20)),
    )(w, g_slots, m, v)


def cctx_grad(q_all, c_ctx_row):
    d = c_ctx_row.shape[1]

    def body(q_ref, c_ref, o_ref):
        acc = q_ref[0, 0:1, :]
        for j in (2, 4, 6):
            acc = acc + q_ref[j, 0:1, :]
        _, vjp = jax.vjp(_silu, c_ref[...])
        o_ref[...] = vjp(acc)[0]

    return pl.pallas_call(
        body, name="cctx_grad", out_shape=jax.ShapeDtypeStruct((1, d), F32),
    )(q_all, c_ctx_row)


def loss_total(pack_sum, d):
    def body(p_ref, o_ref):
        o_ref[...] = jnp.sum(p_ref[:, 0:d], axis=1, keepdims=True)

    return pl.pallas_call(
        body, name="loss_total", out_shape=jax.ShapeDtypeStruct((1, 1), F32),
    )(pack_sum)


class _Plan:
    def __init__(self):
        self.builders, self.got = {}, {}

    def on(self, host, key, builder):
        self.builders.setdefault(host, []).append((key, builder))

    def run(self, host, fn, *args, **kw):
        if host not in self.builders:
            return fn(host, *args, **kw)
        keys, riders = zip(*[(key, builder(self)) for key, builder in self.builders[host]])
        res, landed = fn(host, *args, rider=Riders(riders), **kw)
        for key, r in zip(keys, riders):
            self.got[key], landed = landed[:r.n], landed[r.n:]
        return res


def _val(w):
    return w() if callable(w) else w


def _matmul_tile(n_rows, tm):
    return 2 * tm if n_rows % (2 * tm) == 0 else tm


def _ffn_fwd(plan, tag, xin, n_rows, tm, seg_fn, shift, scale, gate, norm_w, wg, wu, wd, fuse_gate_up=False):
    d = xin[1]
    n_tiles = n_rows // tm
    (h,) = plan.run(f"{tag}_norm", rowwise, fn_norm_mod, [xin], [shift, scale], [norm_w], [(n_rows, d, BF16)],
                    tm=tm, n_tiles=n_tiles, seg_fn=seg_fn)
    tmm = _matmul_tile(n_rows, tm)
    if fuse_gate_up:
        g, u, act = plan.run(f"{tag}_gate_up", matmul, [(h, _val(wg)), (h, _val(wu))], "nn", b_ch=True, out_ch=True,
                             tm=min(tm, 256), fold=True,
                             post=([], lambda ag, au: (ag, au, fn_act(ag, au)[0]), [BF16, BF16, BF16]))
    else:
        g = plan.run(f"{tag}_gate", matmul, [(h, _val(wg))], "nn", out_dtype=BF16, b_ch=True, out_ch=True, tm=tmm)
        u, act = plan.run(f"{tag}_up", matmul, [(h, _val(wu))], "nn", b_ch=True, out_ch=True, tm=tm, fold=True,
                          post=([g], lambda acc, gv: (acc, fn_act(gv, acc)[0]), [BF16, BF16]))
    f = plan.run(f"{tag}_down", matmul, [(act, _val(wd))], "nn", a_ch=True, b_ch=True, tm=tmm, fold=True)
    (xo,) = plan.run(f"{tag}_resid", rowwise, make_fn_resid(0.5), [xin, row(f)], [gate], [], [(n_rows, d, F32)],
                     tm=tm, n_tiles=n_tiles, seg_fn=seg_fn)
    return xo, (h, g, u, act, f)


def _ffn_bwd(plan, tag, d_xo, saved, xin, n_rows, tm, seg_fn, first_fn, shift, scale, gate, norm_w, wg, wu, wd, dx_rows, dx_limit):
    h, g, u, act, f = saved
    d = xin[1]
    n_tiles = n_rows // tm
    n_ch, _, n_hid = g.shape
    d_f, d_gate = plan.run(f"{tag}_resid_bwd", rowwise_bwd, make_fn_resid(0.5), [xin, row(f)], [gate], [], [[row(d_xo)]],
                           [None, (n_rows, BF16, None)], tm=tm, n_tiles=n_tiles, seg_fn=seg_fn, first_fn=first_fn)
    tmm = _matmul_tile(n_rows, tm)
    def act_vjp(d_act, gv, uv):
        s = jax.nn.sigmoid(gv)
        gs = gv * s
        return d_act * uv * (s + gs * (1.0 - s)), d_act * gs
    d_g, d_u = plan.run(f"{tag}_down_dx", matmul, [(d_f, wd)], "nt", b_ch=True, out_ch=True, tm=tmm,
                        post=([g, u], act_vjp, [BF16, BF16]))
    plan.got[f"{tag}_d_wd"] = plan.run(f"{tag}_down_dw", matmul, [(act, d_f)], "tn", out_dtype=BF16, a_ch=True, out_ch=True, tm=tmm)
    d_h = plan.run(f"{tag}_up_dx", matmul, [(d_g, wg), (d_u, wu)], "nt", a_ch=True, b_ch=True, tm=tmm)
    plan.got[f"{tag}_d_wg"] = plan.run(f"{tag}_gate_dw", matmul, [(d_g, h)], "tn", out_dtype=BF16, a_ch=True, out_ch=True, tm=tmm)
    plan.got[f"{tag}_d_wu"] = plan.run(f"{tag}_up_dw", matmul, [(d_u, h)], "tn", out_dtype=BF16, a_ch=True, out_ch=True, tm=tmm)
    d_x, d_shift, d_scale, d_nw = plan.run(
        f"{tag}_norm_bwd", rowwise_bwd, fn_norm_mod, [xin], [shift, scale], [norm_w], [[row(d_h)]], [(dx_rows, F32, dx_limit)],
        tm=tm, n_tiles=n_tiles, seg_fn=seg_fn, first_fn=first_fn, adds={0: (row(d_xo), None)})
    return d_x, (d_shift, d_scale, d_gate), d_nw


def kernel(x, c, ctx, c_ctx, w_mod, b_mod, norm_ffn1, ffn1_gate, ffn1_up, ffn1_down, norm_mix, w_in, ssm_conv_w, ssm_conv_b, dt_bias_fwd, dt_bias_bwd, a_log_fwd, a_log_bwd, ssm_d, ssm_norm_w, cconv_w, cconv_b, cconv_ln_w, cconv_ln_b, w_out, norm_ffn2, ffn2_gate, ffn2_up, ffn2_down, final_norm, loss_target, m_c_ctx, m_w_mod, m_b_mod, m_norm_ffn1, m_ffn1_gate, m_ffn1_up, m_ffn1_down, m_norm_mix, m_w_in, m_ssm_conv_w, m_ssm_conv_b, m_dt_bias_fwd, m_dt_bias_bwd, m_a_log_fwd, m_a_log_bwd, m_ssm_d, m_ssm_norm_w, m_cconv_w, m_cconv_b, m_cconv_ln_w, m_cconv_ln_b, m_w_out, m_norm_ffn2, m_ffn2_gate, m_ffn2_up, m_ffn2_down, m_final_norm, v_c_ctx, v_w_mod, v_b_mod, v_norm_ffn1, v_ffn1_gate, v_ffn1_up, v_ffn1_down, v_norm_mix, v_w_in, v_ssm_conv_w, v_ssm_conv_b, v_dt_bias_fwd, v_dt_bias_bwd, v_a_log_fwd, v_a_log_bwd, v_ssm_d, v_ssm_norm_w, v_cconv_w, v_cconv_b, v_cconv_ln_w, v_cconv_ln_b, v_w_out, v_norm_ffn2, v_ffn2_gate, v_ffn2_up, v_ffn2_down, v_final_norm):
    weights = dict(c_ctx=c_ctx, w_mod=w_mod, b_mod=b_mod, norm_ffn1=norm_ffn1, ffn1_gate=ffn1_gate, ffn1_up=ffn1_up, ffn1_down=ffn1_down, norm_mix=norm_mix, w_in=w_in, ssm_conv_w=ssm_conv_w, ssm_conv_b=ssm_conv_b, dt_bias_fwd=dt_bias_fwd, dt_bias_bwd=dt_bias_bwd, a_log_fwd=a_log_fwd, a_log_bwd=a_log_bwd, ssm_d=ssm_d, ssm_norm_w=ssm_norm_w, cconv_w=cconv_w, cconv_b=cconv_b, cconv_ln_w=cconv_ln_w, cconv_ln_b=cconv_ln_b, w_out=w_out, norm_ffn2=norm_ffn2, ffn2_gate=ffn2_gate, ffn2_up=ffn2_up, ffn2_down=ffn2_down, final_norm=final_norm)
    mom1 = dict(c_ctx=m_c_ctx, w_mod=m_w_mod, b_mod=m_b_mod, norm_ffn1=m_norm_ffn1, ffn1_gate=m_ffn1_gate, ffn1_up=m_ffn1_up, ffn1_down=m_ffn1_down, norm_mix=m_norm_mix, w_in=m_w_in, ssm_conv_w=m_ssm_conv_w, ssm_conv_b=m_ssm_conv_b, dt_bias_fwd=m_dt_bias_fwd, dt_bias_bwd=m_dt_bias_bwd, a_log_fwd=m_a_log_fwd, a_log_bwd=m_a_log_bwd, ssm_d=m_ssm_d, ssm_norm_w=m_ssm_norm_w, cconv_w=m_cconv_w, cconv_b=m_cconv_b, cconv_ln_w=m_cconv_ln_w, cconv_ln_b=m_cconv_ln_b, w_out=m_w_out, norm_ffn2=m_norm_ffn2, ffn2_gate=m_ffn2_gate, ffn2_up=m_ffn2_up, ffn2_down=m_ffn2_down, final_norm=m_final_norm)
    mom2 = dict(c_ctx=v_c_ctx, w_mod=v_w_mod, b_mod=v_b_mod, norm_ffn1=v_norm_ffn1, ffn1_gate=v_ffn1_gate, ffn1_up=v_ffn1_up, ffn1_down=v_ffn1_down, norm_mix=v_norm_mix, w_in=v_w_in, ssm_conv_w=v_ssm_conv_w, ssm_conv_b=v_ssm_conv_b, dt_bias_fwd=v_dt_bias_fwd, dt_bias_bwd=v_dt_bias_bwd, a_log_fwd=v_a_log_fwd, a_log_bwd=v_a_log_bwd, ssm_d=v_ssm_d, ssm_norm_w=v_ssm_norm_w, cconv_w=v_cconv_w, cconv_b=v_cconv_b, cconv_ln_w=v_cconv_ln_w, cconv_ln_b=v_cconv_ln_b, w_out=v_w_out, norm_ffn2=v_norm_ffn2, ffn2_gate=v_ffn2_gate, ffn2_up=v_ffn2_up, ffn2_down=v_ffn2_down, final_norm=v_final_norm)
    order = list(weights)

    n_ex, seq_len, d = x.shape
    ctx_len = ctx.shape[1]
    ds = d
    n_head = ds // HEAD_DIM
    xw = ds + 4 * N_STATE
    n_lat, n_ctx_rows = n_ex * seq_len, n_ex * ctx_len
    n_tok = n_lat + n_ctx_rows
    tm = math.gcd(math.gcd(512, seq_len), n_ctx_rows)
    seg_all, first_all = _segmenter(tm, seq_len, n_lat)
    lat_tiles = n_lat // tm

    xi, yi, ci = lax.axis_index("x"), lax.axis_index("y"), lax.axis_index("c")
    me, chip = 4 * xi + 2 * yi + ci, 2 * xi + yi

    (c_all,) = exchange("gather_c", [c], "all8")
    n_all = 8 * n_ex
    n_cond = -(-(n_all + 1) // 8) * 8
    cond = jnp.concatenate([c_all.reshape(n_all, d), c_ctx[None, :], jnp.zeros((n_cond - n_all - 1, d), F32)])
    mod_w = w_mod.shape[2]
    b_shard = lax.dynamic_slice(b_mod, (0, chip * mod_w), (1, mod_w))
    (mod_g,) = exchange("gather_mod", [mod_fwd(cond, w_mod[0], b_shard)], "chips")
    mod_full = mod_g.transpose(1, 0, 2).reshape(n_cond, N_CHIPS * mod_w)
    mod_mine = lax.dynamic_slice(mod_full, (me * n_ex, 0), (n_ex, 9 * d)).reshape(n_ex, 9, d)
    mod_ctx = mod_full[n_all].reshape(9, d)
    tabs = [jnp.concatenate([mod_mine[:, j], mod_ctx[j][None]])[:, None, :] for j in range(9)]
    lat = lambda t: t[:n_ex]

    bf = lambda w: w[0].astype(BF16)
    plan = _Plan()
    gather = lambda *ws: (lambda p: Rider(list(ws), "chips"))
    plan.on("ffn1_norm", "wg1", gather(bf(ffn1_gate)))
    plan.on("ffn1_gate", "wu1", gather(bf(ffn1_up)))
    plan.on("ffn1_up", "wd1", gather(bf(ffn1_down)))
    cut_a, cut_b = d * 5 // 8, d * 7 // 8
    plan.on("ffn1_down", "win_a", gather(bf(w_in)[:cut_a]))
    plan.on("ffn1_resid", "win_b", gather(bf(w_in)[cut_a:cut_b], ssm_conv_w[0], cconv_w[0]))
    xt = two_rows(x.reshape(n_lat, d), ctx.reshape(n_ctx_rows, d), lat_tiles)
    x1, saved1 = _ffn_fwd(plan, "ffn1", xt, n_tok, tm, seg_all, tabs[0], tabs[1], tabs[2], norm_ffn1,
                          lambda: plan.got["wg1"][0], lambda: plan.got["wu1"][0], lambda: plan.got["wd1"][0])
    (wg1,), (wu1,), (wd1,), (win_a,), (win_b, w5_g, w31_g) = (plan.got[k] for k in ("wg1", "wu1", "wd1", "win_a", "win_b"))
    (h2,), (win_c,) = rowwise("mix_norm", fn_norm_mod, [row(x1)], [tabs[3], tabs[4]], [norm_mix], [(n_tok, d, BF16)],
                              tm=tm, n_tiles=n_tok // tm, seg_fn=seg_all, rider=Rider([bf(w_in)[cut_b:]], "chips"))
    win_g = jnp.concatenate([win_a, win_b, win_c], axis=1)
    unshard_cols = lambda t: t.transpose(1, 0, 2).reshape(t.shape[1], N_CHIPS * t.shape[2])
    win = unshard_cols(win_g)
    o_x, o_dt, o_glu = ds, ds + xw, ds + xw + 2 * n_head
    w_z, w_xbc, w_dt = win[:, :ds], win[:, o_x:o_dt], win[:, o_dt:o_glu]
    w_ga, w_gb = win[:, o_glu:o_glu + d], win[:, o_glu + d:]
    w_dtp = jnp.concatenate([w_dt, jnp.zeros((d, LANES - 2 * n_head), BF16)], axis=1)
    w_cat = jnp.concatenate([w_z, w_ga, w_gb, w_xbc], axis=1)
    cbw = d // 2
    xbc_cb, dt_cb = 3 * d // cbw, 0
    w5, w31 = unshard_cols(w5_g), unshard_cols(w31_g)
    pad_vec = lambda v: jnp.concatenate([v.reshape(1, -1), jnp.zeros((1, LANES - v.size), F32)], axis=1)
    dtb_f, dtb_b, alog_f, alog_b = map(pad_vec, (dt_bias_fwd, dt_bias_bwd, a_log_fwd, a_log_bwd))
    dsk_f, dsk_b = pad_vec(ssm_d), jnp.zeros((1, LANES), F32)

    proj, (wg2,) = matmul("mix_proj", [(h2, w_cat)], "nn", out_dtype=BF16, tm=tm, rider=Rider([bf(ffn2_gate)], "chips"))
    dt_raw = matmul("mix_proj_dt", [(h2, w_dtp)], "nn", tm=tm)
    def conv5(name, src, cb0, flip):
        out = None
        for part, seq, off in (("lat", seq_len, 0), ("ctx", ctx_len, n_lat // ctx_len)):
            out = tapsum_roll(f"{name}_{part}", src, cb0, w5, 0, seq_len=seq, n_seq=n_ex, row_blk_off=off, width=seq,
                              piece=seq, cb=cbw, ncb=xw // cbw, pad=w5.shape[0] // 2, flip=flip,
                              place=((n_tok, xw), off, 0, out), out_dtype=F32 if flip else BF16)
        return out

    craw = conv5("xbc_conv", proj, xbc_cb, False)
    (xbc,) = rowwise("xbc_silu", fn_silu_bias, [row(craw)], [], [ssm_conv_b], [(n_tok, xw, F32)], tm=tm, n_tiles=n_tok // tm)
    ssd = dict(n_ex=n_ex, seq_len=seq_len, ctx_len=ctx_len, ds=ds)
    (y_f, hs_f), (wu2,) = ssd_fwd("ssd_fwd_f", xbc, dt_raw, dt_cb, dtb_f, alog_f, dsk_f, rev=False,
                                  rider=Rider([bf(ffn2_up)], "chips"), **ssd)
    (y_b, hs_b), (wout_g, wd2) = ssd_fwd("ssd_fwd_b", xbc, dt_raw, dt_cb, dtb_b, alog_b, dsk_b, rev=True,
                                         rider=Rider([bf(w_out), bf(ffn2_down)], "chips"), add=y_f, **ssd)
    wout = wout_g.reshape(2 * d, d)
    wo_y, wo_u = wout[:ds], wout[ds:]
    fn_gate = make_fn_gate_groupnorm(ds)
    (yn,) = rowwise("ssd_gate", fn_gate, [row(y_b), row(proj, d, 0)], [], [ssm_norm_w], [(n_lat, ds, BF16)],
                    tm=tm, n_tiles=lat_tiles)
    (u0,) = rowwise("glu", fn_glu, [row(proj, d, 1), row(proj, d, 2)], [], [], [(n_lat, d, BF16)], tm=tm, n_tiles=lat_tiles)
    cb31 = max(LANES, d // 4)
    ncb31 = (d // 2) // cb31
    pad31 = w31.shape[0] // 2
    piece31 = min(seq_len, 4 * GRID_W)
    v_w = tapsum_roll("cconv_cols", u0, 0, w31, 0, seq_len=seq_len, n_seq=n_ex, row_blk_off=0, width=GRID_W,
                      piece=piece31, cb=cb31, ncb=ncb31, pad=pad31, flip=False)
    v_h = tapsum_rows("cconv_rows", u0, ncb31, w31, ncb31, seq_len=seq_len, n_seq=n_ex, cb=cb31, ncb=ncb31, pad=pad31, flip=False)
    (un,) = rowwise("cconv_ln", fn_ln_silu, [row(v_w), row(v_h)], [], [cconv_b, cconv_ln_w, cconv_ln_b], [(n_lat, d, BF16)],
                    tm=tm, n_tiles=lat_tiles)
    mix = matmul("mix_out", [(yn, wo_y), (un, wo_u)], "nn", tm=tm)
    seg_lat, first_lat = _segmenter(tm, seq_len, n_lat)
    (x2,) = rowwise("mix_resid", make_fn_resid(1.0), [row(x1), row(mix)], [lat(tabs[5])], [], [(n_lat, d, F32)],
                    tm=tm, n_tiles=lat_tiles, seg_fn=seg_lat)
    x3, saved2 = _ffn_fwd(plan, "ffn2", row(x2), n_lat, tm, seg_lat, lat(tabs[6]), lat(tabs[7]), lat(tabs[8]), norm_ffn2, wg2, wu2, wd2,
                          fuse_gate_up=True)
    d_x3, d_final, loss_vec = final_loss(x3, loss_target.reshape(n_lat, d), final_norm.reshape(1, d), tm=tm)

    shard_cols = lambda t: t.reshape(t.shape[0], N_CHIPS, -1).transpose(1, 0, 2)

    def pieces(t):
        t = jnp.pad(t, ((0, 0), (0, -t.shape[1] % 32), (0, 0)))
        return t.reshape(2 * N_CHIPS, t.shape[1] // 2, t.shape[2]).astype(BF16)

    scatter = lambda *ts: Rider([pieces(t) for t in ts], "all8", scatter=True)
    halves = lambda names, landed: Rider([sum_slots(f"sum_{nm}", r, BF16) for nm, r in zip(names, landed)], "sibling")
    swapped = {}
    plan.on("ffn2_up_dx", "sc_ffn2_down", lambda p: scatter(p.got["ffn2_d_wd"]))
    plan.on("ffn2_up_dw", "sc_ffn2_gate", lambda p: scatter(p.got["ffn2_d_wg"]))
    d_x2, (d_s6, d_s7, d_g8), d_nffn2 = _ffn_bwd(
        plan, "ffn2", d_x3, saved2, row(x2), n_lat, tm, seg_lat, first_lat, lat(tabs[6]), lat(tabs[7]), lat(tabs[8]), norm_ffn2,
        wg2, wu2, wd2, n_lat, None)
    d_mix, d_g5 = rowwise_bwd("mix_resid_bwd", make_fn_resid(1.0), [row(x1), row(mix)], [lat(tabs[5])], [], [[row(d_x2)]],
                              [None, (n_lat, BF16, None)], tm=tm, n_tiles=lat_tiles, seg_fn=seg_lat, first_fn=first_lat)
    d_yn = matmul("mix_out_dy", [(d_mix, wo_y)], "nt", tm=tm)
    d_un = matmul("mix_out_du", [(d_mix, wo_u)], "nt", tm=tm)
    d_wout = jnp.concatenate([matmul("mix_out_dwy", [(yn, d_mix)], "tn", out_dtype=BF16, tm=tm),
                              matmul("mix_out_dwu", [(un, d_mix)], "tn", out_dtype=BF16, tm=tm)])
    d_vw, d_vh, d_cb, d_lnw, d_lnb = rowwise_bwd(
        "cconv_ln_bwd", fn_ln_silu, [row(v_w), row(v_h)], [], [cconv_b, cconv_ln_w, cconv_ln_b], [[row(d_un)]],
        [(n_lat, F32, None)] * 2, tm=tm, n_tiles=lat_tiles)
    d_u0 = tapsum_roll("cconv_cols_dx", d_vw, 0, w31, 0, seq_len=seq_len, n_seq=n_ex, row_blk_off=0, width=GRID_W,
                       piece=piece31, cb=cb31, ncb=ncb31, pad=pad31, flip=True, place=((n_lat, d), 0, 0, None))
    d_u0 = tapsum_rows("cconv_rows_dx", d_vh, 0, w31, ncb31, seq_len=seq_len, n_seq=n_ex, cb=cb31, ncb=ncb31, pad=pad31,
                       flip=True, place=((n_lat, d), 0, ncb31, d_u0))
    d_w31 = jnp.concatenate([
        tapgrad_roll("cconv_cols_dw", d_vw, 0, 0, u0, 0, 0, n_tap=w31.shape[0], seq_len=seq_len, n_seq=n_ex, width=GRID_W,
                     piece=piece31, cb=cb31, ncb=ncb31, pad=pad31),
        tapgrad_rows("cconv_rows_dw", d_vh, 0, u0, ncb31, n_tap=w31.shape[0], seq_len=seq_len, n_seq=n_ex, cb=cb31,
                     ncb=ncb31, pad=pad31)], axis=1)
    d_ga, d_gb = rowwise_bwd("glu_bwd", fn_glu, [row(proj, d, 1), row(proj, d, 2)], [], [], [[row(d_u0)]],
                             [(n_lat, BF16, None)] * 2, tm=tm, n_tiles=lat_tiles)
    d_ysum, d_z, d_ssmnw = rowwise_bwd(
        "ssd_gate_bwd", fn_gate, [row(y_b), row(proj, d, 0)], [], [ssm_norm_w], [[row(d_yn)]],
        [(n_lat, F32, None), (n_lat, BF16, None)], tm=tm, n_tiles=lat_tiles)
    (dxbc_f, ddt_f, dalog_f, ddtb_f, ddsk), landed = ssd_bwd(
        "ssd_bwd_f", xbc, dt_raw, dt_cb, hs_f, d_ysum, dtb_f, alog_f, dsk_f, rev=False,
        rider=scatter(plan.got["ffn2_d_wu"], d_wout.reshape(N_CHIPS, -1, d)), **ssd)
    (dxbc_b, ddt_b, dalog_b, ddtb_b, _), both = ssd_bwd(
        "ssd_bwd_b", xbc, dt_raw, dt_cb, hs_b, d_ysum, dtb_b, alog_b, dsk_b, rev=True,
        rider=halves(["ffn2_down", "ffn2_gate"], plan.got["sc_ffn2_down"] + plan.got["sc_ffn2_gate"]), add=dxbc_f, **ssd)
    swapped.update(zip(["ffn2_down", "ffn2_gate"], both))
    (d_craw, d_conv_b), both = rowwise_bwd(
        "xbc_silu_bwd", fn_silu_bias, [row(craw)], [], [ssm_conv_b], [[row(dxbc_b)]],
        [(n_tok, F32, None)], tm=tm, n_tiles=n_tok // tm, rider=halves(["ffn2_up", "w_out"], landed))
    swapped.update(zip(["ffn2_up", "w_out"], both))
    d_pxbc = conv5("xbc_conv_dx", d_craw, 0, True)
    g5 = lambda name, seq, off: tapgrad_roll(name, d_craw, 0, off, proj, xbc_cb, off, n_tap=w5.shape[0], seq_len=seq,
                                             n_seq=n_ex, width=seq, piece=seq, cb=cbw, ncb=xw // cbw, pad=w5.shape[0] // 2)
    d_w5 = g5("xbc_conv_lat_dw", seq_len, 0) + g5("xbc_conv_ctx_dw", ctx_len, n_lat // ctx_len)
    lat_pairs = [(d_z, w_z), (d_ga, w_ga), (d_gb, w_gb), (d_pxbc, w_xbc), (ddt_f, w_dtp), (ddt_b, w_dtp)]
    d_h2 = matmul("mix_proj_dx_lat", lat_pairs, "nt", rows=n_lat, tm=min(tm, 256), place=(n_tok, 0, None))
    d_h2 = matmul("mix_proj_dx_ctx", lat_pairs[3:], "nt", rows=n_ctx_rows, row_off=n_lat, tm=min(tm, 256),
                  place=(n_tok, n_lat, d_h2))
    d_wz = matmul("mix_proj_dwz", [(d_z, h2)], "tn", out_dtype=BF16, rows=n_lat, tm=tm)
    d_wga = matmul("mix_proj_dwa", [(d_ga, h2)], "tn", out_dtype=BF16, rows=n_lat, tm=tm)
    d_wgb = matmul("mix_proj_dwb", [(d_gb, h2)], "tn", out_dtype=BF16, rows=n_lat, tm=tm)
    d_wxbc = matmul("mix_proj_dwx", [(d_pxbc, h2)], "tn", out_dtype=BF16, tm=tm)
    d_wdt = matmul("mix_proj_dwt", [(ddt_f, h2), (ddt_b, h2)], "tn", out_dtype=BF16, tm=tm)
    d_win_t = jnp.concatenate([d_wz, d_wxbc, d_wdt[:2 * n_head], d_wga, d_wgb]).reshape(N_CHIPS, -1, d)
    d_x1, d_s3, d_s4, d_nmix = rowwise_bwd(
        "mix_norm_bwd", fn_norm_mod, [row(x1)], [tabs[3], tabs[4]], [norm_mix], [[row(d_h2)]], [(n_tok, F32, None)],
        tm=tm, n_tiles=n_tok // tm, seg_fn=seg_all, first_fn=first_all, adds={0: (row(d_x2), lat_tiles)})
    mix_names = ["w_in", "ssm_conv_w", "cconv_w"]
    plan.on("ffn1_down_dx", "sc_conv", lambda p: scatter(shard_cols(d_w5), shard_cols(d_w31)))
    plan.on("ffn1_up_dx", "sc_win", lambda p: scatter(d_win_t))
    plan.on("ffn1_gate_dw", "sc_ffn1_down", lambda p: scatter(p.got["ffn1_d_wd"]))
    plan.on("ffn1_up_dw", "sc_ffn1_gate", lambda p: scatter(p.got["ffn1_d_wg"]))
    plan.on("ffn1_up_dw", "sw_mix", lambda p: halves(mix_names, p.got["sc_win"] + p.got["sc_conv"]))
    plan.on("ffn1_norm_bwd", "sc_ffn1_up", lambda p: scatter(p.got["ffn1_d_wu"]))
    plan.on("ffn1_up_dw", "sw_ffn1_down", lambda p: halves(["ffn1_down"], p.got["sc_ffn1_down"]))
    d_xt, (d_s0, d_s1, d_g2), d_nffn1 = _ffn_bwd(
        plan, "ffn1", d_x1, saved1, xt, n_tok, tm, seg_all, first_all, tabs[0], tabs[1], tabs[2], norm_ffn1, wg1, wu1, wd1,
        n_lat, lat_tiles)
    swapped.update(zip(mix_names + ["ffn1_down"], plan.got["sw_mix"] + plan.got["sw_ffn1_down"]))
    last_names = ["ffn1_gate", "ffn1_up"]
    last = halves(last_names, plan.got["sc_ffn1_gate"] + plan.got["sc_ffn1_up"])
    grad_x = d_xt.reshape(n_ex, seq_len, d)

    with_ctx0 = lambda t: jnp.concatenate([t, jnp.zeros((1, 1, d), F32)])
    d_tabs = [d_s0, d_s1, d_g2, d_s3, d_s4, with_ctx0(d_g5), with_ctx0(d_s6), with_ctx0(d_s7), with_ctx0(d_g8)]
    d_mod_rows = jnp.concatenate([t[:, 0, :] for t in d_tabs], axis=1)
    n_pad_rows = -(-(n_ex + 1) // 8) * 8
    d_mod_rows = jnp.concatenate([d_mod_rows, jnp.zeros((n_pad_rows - n_ex - 1, 9 * d), F32)])
    small = [("loss", loss_vec), ("norm_ffn1", d_nffn1), ("norm_mix", d_nmix), ("ssm_conv_b", d_conv_b),
             ("dt_bias_fwd", ddtb_f[:, :n_head]), ("dt_bias_bwd", ddtb_b[:, :n_head]), ("a_log_fwd", dalog_f[:, :n_head]),
             ("a_log_bwd", dalog_b[:, :n_head]), ("ssm_d", ddsk[:, :n_head]), ("ssm_norm_w", d_ssmnw), ("cconv_b", d_cb),
             ("cconv_ln_w", d_lnw), ("cconv_ln_b", d_lnb), ("norm_ffn2", d_nffn2), ("final_norm", d_final)]
    n_small = sum(v.size for _, v in small)
    n_pack = -(-n_small // (8 * LANES)) * (8 * LANES)
    pack = jnp.concatenate([v.reshape(-1) for _, v in small] + [jnp.zeros((n_pack - n_small,), F32)]).reshape(-1, LANES)
    (pack_all, d_mod_all), both = exchange_many("gather_small_swap_last", [Rider([pack, d_mod_rows], "all8"), last])
    swapped.update(zip(last_names, both))
    pack_sum = sum_slots("small_sum", pack_all)
    loss = loss_total(pack_sum.reshape(1, n_pack), d).reshape(())
    flat_sum = pack_sum.reshape(-1)
    small_grads, pos = {}, 0
    for nm, v in small:
        small_grads[nm] = flat_sum[pos:pos + v.size]
        pos += v.size
    d_mod_all = d_mod_all.reshape(8 * n_pad_rows, 9 * d)
    cond_rows = [jnp.concatenate([cond[j * n_ex:(j + 1) * n_ex], c_ctx[None, :],
                                  jnp.zeros((n_pad_rows - n_ex - 1, d), F32)]) for j in range(8)]
    cond_bwd = jnp.concatenate(cond_rows)
    d_mod_shard = lax.dynamic_slice(d_mod_all, (0, chip * mod_w), (8 * n_pad_rows, mod_w))
    g_wmod, g_bmod, q_part = mod_bwd(cond_bwd, d_mod_shard, d_mod_all, w_mod[0],
                                     tuple(j * n_pad_rows + n_ex for j in range(8)))
    (q_all,) = exchange("gather_cctx", [q_part], "all8")
    g_cctx = cctx_grad(q_all, c_ctx.reshape(1, d))
    small_grads["c_ctx"], small_grads["b_mod"] = g_cctx.reshape(-1), g_bmod.reshape(-1)

    transposed = {"ffn1_gate", "ffn1_up", "ffn2_gate", "ffn2_up", "w_in"}
    results = {}
    for nm, both in swapped.items():
        flip = (lambda t: jnp.swapaxes(t, 1, 2)) if nm in transposed else (lambda t: t)
        shape = flip(weights[nm]).shape
        two_d = lambda t: flip(t).reshape(shape[-2], shape[-1])
        g_full = both.reshape(1, -1, shape[-1])[:, :shape[-2]]
        results[nm] = [flip(r.reshape(shape)) for r in
                       adamw(f"adamw_{nm}", two_d(weights[nm]), g_full, two_d(mom1[nm]), two_d(mom2[nm]))]
    results["w_mod"] = [r.reshape(w_mod.shape) for r in adamw("adamw_w_mod", w_mod[0], g_wmod[None], m_w_mod[0], v_w_mod[0])]
    small_names = [nm for nm in order if nm not in results]
    n_sm = sum(weights[nm].size for nm in small_names)
    n_smp = -(-n_sm // (8 * LANES)) * (8 * LANES)
    packed = lambda src: jnp.concatenate([src[nm].reshape(-1) for nm in small_names] + [jnp.zeros((n_smp - n_sm,), F32)]).reshape(-1, LANES)
    sm_out = adamw("adamw_small", packed(weights), packed(small_grads)[None], packed(mom1), packed(mom2))
    pos = 0
    for nm in small_names:
        size = weights[nm].size
        results[nm] = [r.reshape(-1)[pos:pos + size].reshape(weights[nm].shape) for r in sm_out]
        pos += size
    return (loss, grad_x, *[results[nm][0] for nm in order], *[results[nm][1] for nm in order],
            *[results[nm][2] for nm in order], *[results[nm][3] for nm in order])
```

```python
import functools
import math

import jax
import jax.numpy as jnp
from jax import lax
from jax.experimental import pallas as pl
from jax.experimental.pallas import tpu as pltpu

F32 = jnp.float32
BF16 = jnp.bfloat16
HI = lax.Precision.HIGHEST
MESH = pl.DeviceIdType.MESH

EPS = 1e-6
GRID_W = 64
HEAD_DIM = 64
N_STATE = 128
CHUNK = 128
LANES = 128
N_CHIPS = 4
ADAM_LR, ADAM_B1, ADAM_B2, ADAM_EPS, ADAM_WD, ADAM_STEP = 0.001, 0.9, 0.999, 1e-08, 0.01, 10
VMEM_CAP = 56 * 1024 * 1024
STREAM_TILE_BYTES = 3 << 19


def _params(vmem_bytes=None, n_axes=1):
    kw = dict(dimension_semantics=("arbitrary",) * n_axes)
    if vmem_bytes is not None:
        kw["vmem_limit_bytes"] = int(min(VMEM_CAP, max(32 * 1024 * 1024, vmem_bytes)))
    return pltpu.CompilerParams(**kw)


def _big(shape, dtype):
    return pltpu.HBM(tuple(shape), dtype)


def _in_hbm(args):
    return [pltpu.with_memory_space_constraint(a, pltpu.HBM) if a.size * a.dtype.itemsize >= (1 << 20) else a for a in args]


def _nbytes(shape, dtype):
    return math.prod(shape) * jnp.dtype(dtype).itemsize


def _row_tile(rows, width, cap_bytes=1 << 20, mult=8):
    best = None
    for t in range(mult, rows + 1, mult):
        if rows % t == 0 and t * width * 4 <= cap_bytes:
            best = t
    return best if best is not None else rows


_MODES = {"all8": (8, (1, 2, 3, 4, 5, 6, 7), 0), "chips": (4, (2, 4, 6), 1), "sibling": (2, (1,), 0)}


class Rider:
    def __init__(self, arrs, mode, scatter=False):
        self.arrs, self.scatter = list(arrs), scatter
        self.nslot, self.deltas, self.shift = _MODES[mode]
        self.n = len(self.arrs)
        self.out_shape = [jax.ShapeDtypeStruct((self.nslot,) + (a.shape[1:] if scatter else a.shape), a.dtype)
                          for a in self.arrs]
        any_spec = pl.BlockSpec(memory_space=pl.ANY)
        self.in_specs = [any_spec] * self.n
        self.out_specs = [any_spec] * self.n
        n_peer = len(self.deltas)
        self.scratch = [pltpu.SemaphoreType.DMA((self.n, n_peer)), pltpu.SemaphoreType.DMA((self.n, n_peer)),
                        pltpu.SemaphoreType.DMA((self.n,))]

    def _copies(self, ins, outs, sems, arrivals):
        send_sems, recv_sems, local_sems = sems
        x, y, c = lax.axis_index("x"), lax.axis_index("y"), lax.axis_index("c")
        me = 4 * x + 2 * y + c
        slot_of = lambda dev: (dev >> self.shift) & (self.nslot - 1)
        src = lambda a, slot: ins[a].at[slot] if self.scatter else ins[a]
        flip = lambda v, bit: 1 - v if bit else v

        def remote(a, k, d, from_slot, to_slot):
            return pltpu.make_async_remote_copy(
                src_ref=src(a, from_slot), dst_ref=outs[a].at[to_slot], send_sem=send_sems.at[a, k],
                recv_sem=recv_sems.at[a, k], device_id=(flip(x, (d >> 2) & 1), flip(y, (d >> 1) & 1), flip(c, d & 1)),
                device_id_type=MESH)

        mine = slot_of(me)
        local = [pltpu.make_async_copy(src(a, mine), outs[a].at[mine], local_sems.at[a]) for a in range(self.n)]
        sends = [remote(a, k, d, slot_of(me ^ d), mine) for k, d in enumerate(self.deltas) for a in range(self.n)]
        if not arrivals:
            return local, sends
        return local, sends, [remote(a, k, d, mine, slot_of(me ^ d)) for k, d in enumerate(self.deltas) for a in range(self.n)]

    def start(self, ins, outs, sems):
        local, sends = self._copies(ins, outs, sems, arrivals=False)
        for cp in local + sends:
            cp.start()

    def wait(self, ins, outs, sems):
        local, sends, recvs = self._copies(ins, outs, sems, arrivals=True)
        for cp in recvs:
            cp.wait_recv()
        for cp in sends:
            cp.wait_send()
        for cp in local:
            cp.wait()


class Riders:
    def __init__(self, riders):
        self.riders = list(riders)
        self.n = sum(r.n for r in self.riders)
        cat = lambda attr: [v for r in self.riders for v in getattr(r, attr)]
        self.arrs, self.out_shape, self.in_specs = cat("arrs"), cat("out_shape"), cat("in_specs")
        self.out_specs, self.scratch = cat("out_specs"), cat("scratch")

    def _each(self, method, ins, outs, sems):
        i = s = 0
        for r in self.riders:
            getattr(r, method)(ins[i:i + r.n], outs[i:i + r.n], sems[s:s + len(r.scratch)])
            i, s = i + r.n, s + len(r.scratch)

    def start(self, ins, outs, sems):
        self._each("start", ins, outs, sems)

    def wait(self, ins, outs, sems):
        self._each("wait", ins, outs, sems)


class _Hosted:
    def __init__(self, rider, n_in, n_out, n_scratch, grid):
        self.rider, self.n_in, self.n_out, self.n_scratch, self.grid = rider, n_in, n_out, n_scratch, grid
        self.n = rider.n if rider else 0

    def split(self, refs):
        a, b = self.n_in, self.n_in + self.n
        c, e = b + self.n_out, b + self.n_out + self.n
        self._r = (refs[a:b], refs[c:e], refs[e + self.n_scratch:])
        if self.rider:
            ids = [pl.program_id(ax) for ax in range(len(self.grid))]
            first = functools.reduce(jnp.logical_and, [i == 0 for i in ids]) if ids else True
            pl.when(first)(lambda: self.rider.start(*self._r))
        return refs[:a], refs[b:c], refs[e:e + self.n_scratch]

    def finish(self):
        if self.rider:
            ids = [pl.program_id(ax) for ax in range(len(self.grid))]
            last = functools.reduce(jnp.logical_and, [i == n - 1 for i, n in zip(ids, self.grid)]) if ids else True
            pl.when(last)(lambda: self.rider.wait(*self._r))

    def call_args(self, in_specs, out_shape, out_specs, scratch, args):
        r = self.rider
        if not r:
            return list(in_specs), tuple(out_shape), tuple(out_specs), list(scratch), list(args)
        return (list(in_specs) + r.in_specs, tuple(out_shape) + tuple(r.out_shape), tuple(out_specs) + tuple(r.out_specs),
                list(scratch) + r.scratch, list(args) + r.arrs)

    def results(self, res, unwrap=True):
        res = list(res) if isinstance(res, (tuple, list)) else [res]
        host = res[:self.n_out]
        host = host[0] if (self.n_out == 1 and unwrap) else tuple(host)
        return (host, res[self.n_out:]) if self.rider else host


def exchange_many(name, riders):
    both = Riders(riders)

    def body(*refs):
        ins, outs, sems = refs[:both.n], refs[both.n:2 * both.n], refs[2 * both.n:]
        both.start(ins, outs, sems)
        both.wait(ins, outs, sems)

    res = list(pl.pallas_call(
        body, name=name, out_shape=tuple(both.out_shape), in_specs=both.in_specs, out_specs=tuple(both.out_specs),
        scratch_shapes=both.scratch,
    )(*both.arrs))
    split = []
    for r in riders:
        split.append(res[:r.n])
        res = res[r.n:]
    return split


def exchange(name, arrs, mode, scatter=False):
    rider = Rider(arrs, mode, scatter)

    def body(*refs):
        ins, outs, sems = refs[:rider.n], refs[rider.n:2 * rider.n], refs[2 * rider.n:]
        rider.start(ins, outs, sems)
        rider.wait(ins, outs, sems)

    return pl.pallas_call(
        body, name=name, out_shape=tuple(rider.out_shape), in_specs=rider.in_specs, out_specs=tuple(rider.out_specs),
        scratch_shapes=rider.scratch,
    )(*arrs)


_DIMS = {"nn": (((1,), (0,)), ((), ())), "nt": (((1,), (1,)), ((), ())), "tn": (((0,), (0,)), ((), ()))}


def matmul(name, pairs, kind, *, a_ch=False, b_ch=False, out_ch=False, out_dtype=F32, rows=None, row_off=0, tm=512,
           rider=None, post=None, fold=False, place=None):
    a0, b0 = pairs[0]
    n_chunk = a0.shape[0] if a_ch else (b0.shape[0] if b_ch else 1)
    total_rows = a0.shape[-2]
    rows = total_rows - row_off if rows is None else rows
    tm = min(tm, rows)
    assert rows % tm == 0 and row_off % tm == 0, (name, rows, tm, row_off)
    n_rt, off = rows // tm, row_off // tm
    dims = _DIMS[kind]
    n_pair = len(pairs)

    if kind == "tn":
        grid, red_axis, n_red = (n_chunk, n_rt), 1, n_rt
        a_idx = (lambda k, i: (k, i + off, 0)) if a_ch else (lambda k, i: (i + off, 0))
        b_idx = (lambda k, i: (k, i + off, 0)) if b_ch else (lambda k, i: (i + off, 0))
        a_blk = lambda a: ((None, tm, a.shape[-1]) if a_ch else (tm, a.shape[-1]))
        b_blk = lambda b: ((None, tm, b.shape[-1]) if b_ch else (tm, b.shape[-1]))
        o2 = (a0.shape[-1], b0.shape[-1])
        out_shape = ((n_chunk,) + o2) if out_ch else o2
        out_spec = pl.BlockSpec((None,) + o2, lambda k, i: (k, 0, 0)) if out_ch else pl.BlockSpec(o2, lambda k, i: (0, 0))
        acc_shape = o2
    else:
        n_out = b0.shape[-1] if kind == "nn" else b0.shape[-2]
        b2 = b0.shape[-2:]
        if a_ch and b_ch and not out_ch and fold:
            grid, red_axis, n_red = (n_rt,), None, 1
            a_idx, b_idx = (lambda i: (0, i + off, 0)), (lambda i: (0, 0, 0))
            a_blk = lambda a: (n_chunk, tm, a.shape[-1])
            b_blk = lambda b: tuple(b.shape)
            out_shape, out_spec = (rows, n_out), pl.BlockSpec((tm, n_out), lambda i: (i, 0))
        elif a_ch and b_ch and not out_ch:
            grid, red_axis, n_red = (n_rt, n_chunk), 1, n_chunk
            a_idx, b_idx = (lambda i, k: (k, i + off, 0)), (lambda i, k: (k, 0, 0))
            a_blk = lambda a: (None, tm, a.shape[-1])
            b_blk = lambda b: (None,) + tuple(b.shape[-2:])
            out_shape, out_spec = (rows, n_out), pl.BlockSpec((tm, n_out), lambda i, k: (i, 0))
        elif out_ch and fold:
            assert b_ch and not a_ch and all(a is a0 for a, _ in pairs)
            grid, red_axis, n_red = (n_rt,), None, 1
            a_idx, b_idx = (lambda i: (i + off, 0)), (lambda i: (0, 0, 0))
            a_blk = lambda a: (tm, a.shape[-1])
            b_blk = lambda b: tuple(b.shape)
            out_shape, out_spec = (n_chunk, rows, n_out), pl.BlockSpec((n_chunk, tm, n_out), lambda i: (0, i, 0))
        elif out_ch:
            assert b_ch and not a_ch
            grid, red_axis, n_red = (n_chunk, n_rt), None, 1
            a_idx, b_idx = (lambda k, i: (i + off, 0)), (lambda k, i: (k, 0, 0))
            a_blk = lambda a: (tm, a.shape[-1])
            b_blk = lambda b: (None,) + tuple(b.shape[-2:])
            out_shape, out_spec = (n_chunk, rows, n_out), pl.BlockSpec((None, tm, n_out), lambda k, i: (k, i, 0))
        else:
            assert not (a_ch or b_ch)
            grid, red_axis, n_red = (n_rt,), None, 1
            a_idx, b_idx = (lambda i: (i + off, 0)), (lambda i: (0, 0))
            a_blk = lambda a: (tm, a.shape[-1])
            b_blk = lambda b: tuple(b.shape)
            out_shape, out_spec = (rows, n_out), pl.BlockSpec((tm, n_out), lambda i: (i, 0))
            if place is not None:
                out_shape, o_off = (place[0], n_out), place[1] // tm
                out_spec = pl.BlockSpec((tm, n_out), lambda i: (i + o_off, 0))
        acc_shape = (tm, n_out)

    into = [] if place is None or place[2] is None else [place[2]]
    post_ins, post_fn, out_dtypes = ([], None, [out_dtype]) if post is None else post
    hosted = _Hosted(rider, 2 * n_pair + len(post_ins) + len(into), len(out_dtypes), int(n_red > 1), grid)

    def body(*refs):
        ins, outs, scr = hosted.split(refs)

        def compute():
            acc = None
            for p in range(n_pair):
                for k in ([None] if not fold else range(n_chunk)):
                    pick = (lambda r: r[...]) if k is None else (lambda r: r[k])
                    d = lax.dot_general(pick(ins[2 * p]).astype(BF16), pick(ins[2 * p + 1]).astype(BF16), dims,
                                        preferred_element_type=F32)
                    acc = d if acc is None else acc + d
            return acc

        def emit(acc):
            vals = (acc,) if post_fn is None else post_fn(
                acc, *[r[...].astype(F32) for r in ins[2 * n_pair:2 * n_pair + len(post_ins)]])
            for o_ref, v in zip(outs, vals):
                o_ref[...] = v.astype(o_ref.dtype)

        if out_ch and fold:
            a_tile = ins[0][...].astype(BF16)
            for k in range(n_chunk):
                accs = [lax.dot_general(a_tile, ins[2 * p + 1][k].astype(BF16), dims, preferred_element_type=F32)
                        for p in range(n_pair)]
                tiles = [r[k].astype(F32) for r in ins[2 * n_pair:2 * n_pair + len(post_ins)]]
                vals = tuple(accs) if post_fn is None else post_fn(*accs, *tiles)
                for o_ref, v in zip(outs, vals):
                    o_ref[k] = v.astype(o_ref.dtype)
        elif n_red == 1:
            emit(compute())
        else:
            acc_ref = scr[0]
            r = pl.program_id(red_axis)

            @pl.when(r == 0)
            def _():
                acc_ref[...] = jnp.zeros_like(acc_ref)

            acc_ref[...] += compute()

            @pl.when(r == n_red - 1)
            def _():
                emit(acc_ref[...])
        hosted.finish()

    in_specs, args, vmem = [], [], 0
    for a, b in pairs:
        in_specs += [pl.BlockSpec(a_blk(a), a_idx), pl.BlockSpec(b_blk(b), b_idx)]
        args += [a, b]
        vmem += 2 * (_nbytes([s for s in a_blk(a) if s], a.dtype) + _nbytes([s for s in b_blk(b) if s], b.dtype))
    in_specs += [out_spec] * len(post_ins)
    args += list(post_ins)
    aliases = {len(args): 0} if into else {}
    in_specs += [pl.BlockSpec(memory_space=pl.ANY)] * len(into)
    args += into
    tiles_per_step = n_chunk if (out_ch and fold) else 1
    vmem += (3 + 2 * n_pair + tiles_per_step * (len(post_ins) + len(out_dtypes))) * _nbytes(acc_shape, F32)
    scratch = [pltpu.VMEM(acc_shape, F32)] if n_red > 1 else []
    in_specs, out_shapes, out_specs, scratch, args = hosted.call_args(
        in_specs, [_big(out_shape, dt) for dt in out_dtypes], [out_spec] * len(out_dtypes), scratch, args)
    return hosted.results(pl.pallas_call(
        body, name=name, out_shape=out_shapes, grid=grid, in_specs=in_specs, out_specs=out_specs,
        input_output_aliases=aliases, scratch_shapes=scratch, compiler_params=_params(vmem + (8 << 20), len(grid)),
    )(*_in_hbm(args)))


def row(arr, width=None, cb=0, roff=0):
    return (arr, arr.shape[-1] if width is None else width, cb, roff)


def two_rows(first, second, limit):
    return (first, first.shape[-1], 0, 0, (second, limit))


def _row_inputs(rows, tm):
    specs, arrs, slots = [], [], []
    for d in rows:
        second, limit = d[4] if len(d) > 4 else (None, None)
        slots.append((len(arrs), limit))
        specs.append(_row_spec(d[:4], tm, limit))
        arrs.append(d[0])
        if second is not None:
            specs.append(pl.BlockSpec((tm, d[1]), lambda i, limit=limit: (jnp.maximum(i - limit, 0), 0)))
            arrs.append(second)

    def read(refs, i):
        vals = []
        for at, limit in slots:
            v = refs[at][...].astype(F32)
            vals.append(v if limit is None else jnp.where(i < limit, v, refs[at + 1][...].astype(F32)))
        return vals

    return specs, arrs, read


def _row_spec(desc, tm, limit=None):
    _, width, cb, roff = desc[:4]
    if limit is None:
        return pl.BlockSpec((tm, width), lambda i: (i + roff, cb))
    return pl.BlockSpec((tm, width), lambda i: (jnp.minimum(i, limit - 1) + roff, cb))


def _segmenter(tm, seq_len, n_lat):
    seg = lambda i: jnp.where(i * tm < n_lat, (i * tm) // seq_len, n_lat // seq_len)
    first = lambda i: jnp.where(i * tm < n_lat, (i * tm) % seq_len == 0, i * tm == n_lat)
    return seg, first


def rowwise(name, fn, rows, segs, params, outs, *, tm, n_tiles, seg_fn=None, rider=None):
    row_specs, row_arrs, read_rows = _row_inputs(rows, tm)
    n_r, n_s, n_p = len(row_arrs), len(segs), len(params)
    hosted = _Hosted(rider, n_r + n_s + n_p, len(outs), 0, (n_tiles,))

    def body(*refs):
        ins, out_refs, _ = hosted.split(refs)
        vals = read_rows(ins[:n_r], pl.program_id(0)) + [r[...] for r in ins[n_r:]]
        res = fn(*vals)
        for o_ref, v in zip(out_refs, res):
            o_ref[...] = v.astype(o_ref.dtype)
        hosted.finish()

    in_specs = list(row_specs)
    in_specs += [pl.BlockSpec((None, 1, s.shape[-1]), lambda i: (seg_fn(i), 0, 0)) for s in segs]
    in_specs += [pl.BlockSpec(p.shape, lambda i: (0, 0)) for p in params]
    vmem = sum(2 * tm * d[1] * 4 for d in rows) + sum(3 * tm * w * 4 for _, w, _ in outs) + sum(2 * p.size * 4 for p in params)
    in_specs, out_shapes, out_specs, scratch, args = hosted.call_args(
        in_specs, [_big((r, w), dt) for r, w, dt in outs],
        [pl.BlockSpec((tm, w), lambda i: (i, 0)) for _, w, _ in outs], [], row_arrs + list(segs) + list(params))
    return hosted.results(pl.pallas_call(
        body, name=name, grid=(n_tiles,), in_specs=in_specs, out_shape=out_shapes, out_specs=out_specs,
        scratch_shapes=scratch, compiler_params=_params(2 * vmem + (8 << 20)),
    )(*_in_hbm(args)), unwrap=False)


def rowwise_bwd(name, fn, rows, segs, params, cts, row_grads, *, tm, n_tiles, seg_fn=None, first_fn=None, adds=None,
                rider=None):
    adds = adds or {}
    need = [k for k, v in enumerate(row_grads) if v is not None]
    row_specs, row_arrs, read_rows = _row_inputs(rows, tm)
    n_r, n_s, n_p = len(row_arrs), len(segs), len(params)
    n_ct = sum(len(lst) for lst in cts)
    add_keys = sorted(adds)
    hosted = _Hosted(rider, n_r + n_s + n_p + n_ct + len(add_keys), len(need) + n_s + n_p, 0, (n_tiles,))

    def body(*refs):
        host_in, host_out, _ = hosted.split(refs)
        it = iter(list(host_in) + list(host_out))
        row_refs = [next(it) for _ in range(n_r)]
        seg_refs = [next(it) for _ in range(n_s)]
        par_refs = [next(it) for _ in range(n_p)]
        ct_refs = [[next(it) for _ in lst] for lst in cts]
        add_refs = {k: next(it) for k in add_keys}
        rg_refs = {k: next(it) for k in need}
        sg_refs = [next(it) for _ in range(n_s)]
        pg_refs = [next(it) for _ in range(n_p)]
        i = pl.program_id(0)
        rv = read_rows(row_refs, i)
        sv = [r[...] for r in seg_refs]
        pv = [r[...] for r in par_refs]

        def f(*args):
            rr = list(rv)
            for j, k in enumerate(need):
                rr[k] = args[j]
            return fn(*rr, *args[len(need):])

        _, vjp = jax.vjp(f, *[rv[k] for k in need], *sv, *pv)
        ctv = []
        for lst in ct_refs:
            acc = lst[0][...].astype(F32)
            for r in lst[1:]:
                acc = acc + r[...].astype(F32)
            ctv.append(acc)
        g = vjp(tuple(ctv))
        for j, k in enumerate(need):
            gv = g[j]
            if k in adds:
                lim = adds[k][1]
                av = add_refs[k][...].astype(F32)
                gv = gv + (av if lim is None else jnp.where(i < lim, av, 0.0))
            lim = row_grads[k][2]
            if lim is None:
                rg_refs[k][...] = gv.astype(rg_refs[k].dtype)
            else:
                @pl.when(i < lim)
                def _(gv=gv, k=k):
                    rg_refs[k][...] = gv.astype(rg_refs[k].dtype)
        if n_s:
            opens = first_fn(i)
            for ref, gv in zip(sg_refs, g[len(need):len(need) + n_s]):
                @pl.when(opens)
                def _(ref=ref, gv=gv):
                    ref[...] = gv

                @pl.when(jnp.logical_not(opens))
                def _(ref=ref, gv=gv):
                    ref[...] += gv
        for ref, gv in zip(pg_refs, g[len(need) + n_s:]):
            @pl.when(i == 0)
            def _(ref=ref, gv=gv):
                ref[...] = gv

            @pl.when(i > 0)
            def _(ref=ref, gv=gv):
                ref[...] += gv
        hosted.finish()

    seg_spec = lambda s: pl.BlockSpec((None, 1, s.shape[-1]), lambda i: (seg_fn(i), 0, 0))
    par_spec = lambda p: pl.BlockSpec(p.shape, lambda i: (0, 0))
    in_specs = list(row_specs) + [seg_spec(s) for s in segs] + [par_spec(p) for p in params]
    args = row_arrs + list(segs) + list(params)
    for lst in cts:
        in_specs += [_row_spec(d, tm) for d in lst]
        args += [d[0] for d in lst]
    for k in add_keys:
        in_specs.append(_row_spec(adds[k][0], tm, adds[k][1]))
        args.append(adds[k][0][0])
    out_shape, out_specs = [], []
    for k in need:
        n_rows, dt, lim = row_grads[k]
        out_shape.append(_big((n_rows, rows[k][1]), dt))
        out_specs.append(_row_spec((None, rows[k][1], 0, 0), tm, lim))
    for s in segs:
        out_shape.append(jax.ShapeDtypeStruct(s.shape, F32))
        out_specs.append(seg_spec(s))
    for p in params:
        out_shape.append(jax.ShapeDtypeStruct(p.shape, F32))
        out_specs.append(par_spec(p))
    vmem = sum(tm * d[1] * 4 for d in rows) * 6 + n_ct * tm * max(d[1] for d in rows) * 8
    in_specs, out_shape, out_specs, scratch, args = hosted.call_args(in_specs, out_shape, out_specs, [], args)
    return hosted.results(pl.pallas_call(
        body, name=name, grid=(n_tiles,), in_specs=in_specs, out_shape=out_shape, out_specs=out_specs,
        scratch_shapes=scratch, compiler_params=_params(vmem + (8 << 20)),
    )(*_in_hbm(args)), unwrap=False)


def _silu(v):
    return v * jax.nn.sigmoid(v)


def _rms(v, w):
    return v * lax.rsqrt(jnp.mean(v * v, axis=-1, keepdims=True) + EPS) * w


def fn_norm_mod(x, shift, scale, w):
    return (_rms(x, w) * (1.0 + scale) + shift,)


def fn_act(g, u):
    return (_silu(g) * u,)


def make_fn_resid(coef):
    def fn(x, f, gate):
        return (x + coef * gate * f,)
    return fn


def fn_silu_bias(v, b):
    return (_silu(v + b),)


def make_fn_gate_groupnorm(width):
    half = width // 2

    def fn(y_both, z, w):
        y = y_both * _silu(z)
        lane = lax.broadcasted_iota(jnp.int32, y.shape, 1)
        lo = lane < half
        sq = y * y
        s_lo = jnp.sum(jnp.where(lo, sq, 0.0), axis=-1, keepdims=True)
        s_hi = jnp.sum(jnp.where(lo, 0.0, sq), axis=-1, keepdims=True)
        r = jnp.where(lo, lax.rsqrt(s_lo / half + EPS), lax.rsqrt(s_hi / half + EPS))
        return (y * r * w,)
    return fn


def fn_glu(a, b):
    return (a * jax.nn.sigmoid(b),)


def fn_ln_silu(vw, vh, cb, lw, lb):
    v = jnp.concatenate([vw, vh], axis=-1) + cb
    mu = jnp.mean(v, axis=-1, keepdims=True)
    var = jnp.mean(jnp.square(v - mu), axis=-1, keepdims=True)
    return (_silu((v - mu) * lax.rsqrt(var + EPS) * lw + lb),)


def _col_tile(width):
    return width // 3 if width % (3 * LANES) == 0 else width


def mod_fwd(a_rows, w_shard, b_shard):
    n, d = a_rows.shape
    ws = w_shard.shape[1]
    tn = _col_tile(ws)

    def body(a_ref, w_ref, b_ref, o_ref):
        a = _silu(a_ref[...]).astype(BF16)
        o_ref[...] = jnp.dot(a, w_ref[...].astype(BF16), preferred_element_type=F32) + b_ref[...]

    return pl.pallas_call(
        body, name="mod_fwd", grid=(ws // tn,), out_shape=jax.ShapeDtypeStruct((n, ws), F32),
        in_specs=[pl.BlockSpec((n, d), lambda j: (0, 0)), pl.BlockSpec((d, tn), lambda j: (0, j)),
                  pl.BlockSpec((1, tn), lambda j: (0, j))],
        out_specs=pl.BlockSpec((n, tn), lambda j: (0, j)), compiler_params=_params(),
    )(a_rows, w_shard, b_shard)


def mod_bwd(a_rows, d_shard, d_full, w_shard, ctx_rows):
    n, d = a_rows.shape
    ws = w_shard.shape[1]
    tn = _col_tile(ws)
    n_ct = ws // tn

    def body(a_ref, ds_ref, df_ref, w_ref, gw_ref, gb_ref, q_ref):
        j = pl.program_id(0)
        a = _silu(a_ref[...])
        ds = ds_ref[...]
        gw_ref[...] = lax.dot_general(a, ds, _DIMS["tn"], precision=HI, preferred_element_type=F32)
        dctx = ds[ctx_rows[0]:ctx_rows[0] + 1, :]
        for r in ctx_rows[1:]:
            dctx = dctx + ds[r:r + 1, :]
        q = lax.dot_general(jnp.broadcast_to(dctx, (8, tn)), w_ref[...], _DIMS["nt"], precision=HI,
                            preferred_element_type=F32)

        @pl.when(j == 0)
        def _():
            q_ref[...] = q
            df = df_ref[...]
            acc = df[0:1, :]
            for r in range(1, n):
                acc = acc + df[r:r + 1, :]
            gb_ref[...] = acc

        @pl.when(j > 0)
        def _():
            q_ref[...] += q

    return pl.pallas_call(
        body, name="mod_bwd", grid=(n_ct,),
        out_shape=(jax.ShapeDtypeStruct((d, ws), F32), jax.ShapeDtypeStruct((1, d_full.shape[1]), F32),
                   jax.ShapeDtypeStruct((8, d), F32)),
        in_specs=[pl.BlockSpec((n, d), lambda j: (0, 0)), pl.BlockSpec((n, tn), lambda j: (0, j)),
                  pl.BlockSpec(d_full.shape, lambda j: (0, 0)), pl.BlockSpec((d, tn), lambda j: (0, j))],
        out_specs=(pl.BlockSpec((d, tn), lambda j: (0, j)), pl.BlockSpec((1, d_full.shape[1]), lambda j: (0, 0)),
                   pl.BlockSpec((8, d), lambda j: (0, 0))),
        compiler_params=_params(40 << 20),
    )(a_rows, d_shard, d_full, w_shard)


def _shifted(xs, d, tok, width):
    if d == 0:
        return xs
    n = xs.shape[0]
    sh = pltpu.roll(xs, (-d) % n, axis=0)
    return jnp.where((tok + d >= 0) & (tok + d < width), sh, 0.0)


def _placed(out_shape, place):
    if place is None:
        return out_shape, 0, 0, None
    return place


def tapsum_roll(name, x, xcb, w, wcb, *, seq_len, n_seq, row_blk_off, width, piece, cb, ncb, pad, flip, place=None,
                out_dtype=F32):
    n_tap = w.shape[0]
    n_piece = seq_len // piece
    out_shape, o_rb, o_cb, into = _placed((n_seq * seq_len, ncb * cb), place)

    def body(x_ref, w_ref, *rest):
        o_ref = rest[-1]
        wv = w_ref[...]
        tok = lax.broadcasted_iota(jnp.int32, (piece, 1), 0) % width

        def do_piece(p, carry):
            start = pl.multiple_of(p * piece, piece)
            xs = x_ref[pl.ds(start, piece), :].astype(F32)
            acc = jnp.zeros_like(xs)
            for k in range(n_tap):
                d = pad - k if flip else k - pad
                acc = acc + wv[k:k + 1, :] * _shifted(xs, d, tok, width)
            o_ref[pl.ds(start, piece), :] = acc.astype(o_ref.dtype)
            return carry

        lax.fori_loop(0, n_piece, do_piece, 0)

    extra = [] if into is None else [into]
    return pl.pallas_call(
        body, name=name, grid=(ncb, n_seq), out_shape=_big(out_shape, out_dtype),
        in_specs=[pl.BlockSpec((seq_len, cb), lambda j, s: (row_blk_off + s, xcb + j)),
                  pl.BlockSpec((n_tap, cb), lambda j, s: (0, wcb + j))] + [pl.BlockSpec(memory_space=pl.ANY)] * len(extra),
        out_specs=pl.BlockSpec((seq_len, cb), lambda j, s: (o_rb + s, o_cb + j)),
        input_output_aliases={2: 0} if extra else {},
        compiler_params=_params(8 * seq_len * cb * 4 + (8 << 20), 2),
    )(*_in_hbm([x, w] + extra))


def tapgrad_roll(name, dy, dycb, dy_blk_off, x, xcb, x_blk_off, *, n_tap, seq_len, n_seq, width, piece, cb, ncb, pad):
    n_piece = seq_len // piece

    def body(dy_ref, x_ref, o_ref):
        @pl.when(pl.program_id(1) == 0)
        def _():
            o_ref[...] = jnp.zeros_like(o_ref)

        tok = lax.broadcasted_iota(jnp.int32, (piece, 1), 0) % width

        def do_piece(p, carry):
            start = pl.multiple_of(p * piece, piece)
            xs = x_ref[pl.ds(start, piece), :].astype(F32)
            dv = dy_ref[pl.ds(start, piece), :]
            for k in range(n_tap):
                o_ref[k:k + 1, :] += jnp.sum(dv * _shifted(xs, k - pad, tok, width), axis=0, keepdims=True)
            return carry

        lax.fori_loop(0, n_piece, do_piece, 0)

    return pl.pallas_call(
        body, name=name, grid=(ncb, n_seq), out_shape=jax.ShapeDtypeStruct((n_tap, ncb * cb), F32),
        in_specs=[pl.BlockSpec((seq_len, cb), lambda j, s: (dy_blk_off + s, dycb + j)),
                  pl.BlockSpec((seq_len, cb), lambda j, s: (x_blk_off + s, xcb + j))],
        out_specs=pl.BlockSpec((n_tap, cb), lambda j, s: (0, j)),
        compiler_params=_params(8 * seq_len * cb * 4 + (8 << 20), 2),
    )(*_in_hbm([dy, x]))


def tapsum_rows(name, x, xcb, w, wcb, *, seq_len, n_seq, cb, ncb, pad, flip, place=None):
    n_tap = w.shape[0]
    n_row = seq_len // GRID_W
    halo = pad * GRID_W
    out_shape, o_rb, o_cb, into = _placed((n_seq * seq_len, ncb * cb), place)

    def body(x_ref, w_ref, *rest):
        o_ref, xp = rest[-2:]
        xp[pl.ds(0, halo), :] = jnp.zeros((halo, cb), F32)
        xp[pl.ds(halo + seq_len, halo), :] = jnp.zeros((halo, cb), F32)
        xp[pl.ds(halo, seq_len), :] = x_ref[...].astype(F32)
        wv = w_ref[...]

        def do_row(r, carry):
            acc = jnp.zeros((GRID_W, cb), F32)
            for k in range(n_tap):
                d = pad - k if flip else k - pad
                acc = acc + wv[k:k + 1, :] * xp[pl.ds(pl.multiple_of((r + pad + d) * GRID_W, GRID_W), GRID_W), :]
            o_ref[pl.ds(pl.multiple_of(r * GRID_W, GRID_W), GRID_W), :] = acc
            return carry

        lax.fori_loop(0, n_row, do_row, 0)

    extra = [] if into is None else [into]
    return pl.pallas_call(
        body, name=name, grid=(ncb, n_seq), out_shape=_big(out_shape, F32),
        in_specs=[pl.BlockSpec((seq_len, cb), lambda j, s: (s, xcb + j)),
                  pl.BlockSpec((n_tap, cb), lambda j, s: (0, wcb + j))] + [pl.BlockSpec(memory_space=pl.ANY)] * len(extra),
        out_specs=pl.BlockSpec((seq_len, cb), lambda j, s: (o_rb + s, o_cb + j)),
        input_output_aliases={2: 0} if extra else {},
        scratch_shapes=[pltpu.VMEM((seq_len + 2 * halo, cb), F32)],
        compiler_params=_params(10 * seq_len * cb * 4 + (8 << 20), 2),
    )(*_in_hbm([x, w] + extra))


def tapgrad_rows(name, dy, dycb, x, xcb, *, n_tap, seq_len, n_seq, cb, ncb, pad):
    n_row = seq_len // GRID_W
    halo = pad * GRID_W

    def body(dy_ref, x_ref, o_ref, xp):
        @pl.when(pl.program_id(1) == 0)
        def _():
            o_ref[...] = jnp.zeros_like(o_ref)

        xp[pl.ds(0, halo), :] = jnp.zeros((halo, cb), F32)
        xp[pl.ds(halo + seq_len, halo), :] = jnp.zeros((halo, cb), F32)
        xp[pl.ds(halo, seq_len), :] = x_ref[...].astype(F32)

        def do_row(r, carry):
            dv = dy_ref[pl.ds(pl.multiple_of(r * GRID_W, GRID_W), GRID_W), :]
            for k in range(n_tap):
                xs = xp[pl.ds(pl.multiple_of((r + k) * GRID_W, GRID_W), GRID_W), :]
                o_ref[k:k + 1, :] += jnp.sum(dv * xs, axis=0, keepdims=True)
            return carry

        lax.fori_loop(0, n_row, do_row, 0)

    return pl.pallas_call(
        body, name=name, grid=(ncb, n_seq), out_shape=jax.ShapeDtypeStruct((n_tap, ncb * cb), F32),
        in_specs=[pl.BlockSpec((seq_len, cb), lambda j, s: (s, dycb + j)),
                  pl.BlockSpec((seq_len, cb), lambda j, s: (s, xcb + j))],
        out_specs=pl.BlockSpec((n_tap, cb), lambda j, s: (0, j)),
        scratch_shapes=[pltpu.VMEM((seq_len + 2 * halo, cb), F32)],
        compiler_params=_params(10 * seq_len * cb * 4 + (8 << 20), 2),
    )(*_in_hbm([dy, x]))


def _ssd_blocks(b, s, *, rev, n_ctx, n_lat, lat_blocks):
    if rev:
        return jnp.where(s < n_ctx, lat_blocks + b * n_ctx + (n_ctx - 1 - s), b * n_lat + (n_lat - 1 - (s - n_ctx)))
    return jnp.where(s < n_ctx, lat_blocks + b * n_ctx + s, b * n_lat + (s - n_ctx))


def _ssd_common(xbc, raw, dtb, alog, dsk, *, rev, ds, n_head):
    if rev:
        raw = pltpu.roll(raw, LANES - n_head, axis=1)
    pre = raw + dtb
    dt = jnp.maximum(pre, 0.0) + jnp.log(1.0 + jnp.exp(-jnp.abs(pre)))
    sig = jax.nn.sigmoid(pre)
    a = -jnp.exp(alog)
    da = dt * a
    ri = lax.broadcasted_iota(jnp.int32, (CHUNK, CHUNK), 0)
    ci = lax.broadcasted_iota(jnp.int32, (CHUNK, CHUNK), 1)
    mask = (ci >= ri) if rev else (ci <= ri)
    tri = mask.astype(F32)
    tri_t = ((ci <= ri) if rev else (ci >= ri)).astype(F32)
    cs = jnp.dot(tri, da, precision=HI, preferred_element_type=F32)
    tot = jnp.sum(da, axis=0, keepdims=True)
    def wide(v):
        first = lax.broadcasted_iota(jnp.int32, (v.shape[0], LANES), 1) < HEAD_DIM
        return jnp.concatenate(
            [jnp.where(first, jnp.broadcast_to(v[:, 2 * p:2 * p + 1], first.shape),
                       jnp.broadcast_to(v[:, 2 * p + 1:2 * p + 2], first.shape)) for p in range(n_head // 2)], axis=1)

    cs_w, tot_w = wide(cs), wide(tot)
    xh = xbc[:, :ds]
    dt_w = wide(dt)
    return dict(
        dt=dt, sig=sig, a=a, cs=cs, cs_t=cs.T, tot=tot, mask=mask, tri_t=tri_t,
        e_w=jnp.exp(cs_w), wt_w=jnp.exp(tot_w - cs_w), dec_w=jnp.exp(tot_w), dt_w=dt_w, dsk_w=wide(dsk),
        xh=xh, xs_w=xh * dt_w, bm=xbc[:, ds:ds + 2 * N_STATE], cm=xbc[:, ds + 2 * N_STATE:ds + 4 * N_STATE])


def _decay(q, col):
    seg = q["cs"][:, col:col + 1] - q["cs_t"][col:col + 1, :]
    return jnp.exp(jnp.where(q["mask"], seg, -jnp.inf))


def _split_heads(v):
    lane = lax.broadcasted_iota(jnp.int32, v.shape, 1)
    return jnp.concatenate([jnp.where(lane < HEAD_DIM, v, 0.0), jnp.where(lane >= HEAD_DIM, v, 0.0)], axis=0)


def ssd_fwd(name, xbc, dt_raw, dt_cb, dtb, alog, dsk, *, rev, n_ex, seq_len, ctx_len, ds, rider=None, add=None):
    n_head, half = ds // HEAD_DIM, ds // 2
    n_ctx, n_lat = ctx_len // CHUNK, seq_len // CHUNK
    n_step = n_ctx + n_lat
    blk = functools.partial(_ssd_blocks, rev=rev, n_ctx=n_ctx, n_lat=n_lat, lat_blocks=n_ex * n_lat)
    xw = xbc.shape[1]

    def y_blk(b, s):
        sl = jnp.maximum(s, n_ctx) - n_ctx
        return b * n_lat + ((n_lat - 1 - sl) if rev else sl)

    hosted = _Hosted(rider, 5 + (add is not None), 2, 1, (n_ex, n_step))

    def body(*refs):
        (xbc_ref, dt_ref, dtb_ref, alog_ref, dsk_ref, *add_ref), (y_ref, hs_ref), (h_scr,) = hosted.split(refs)

        @pl.when(pl.program_id(1) == 0)
        def _():
            h_scr[...] = jnp.zeros_like(h_scr)

        q = _ssd_common(xbc_ref[...], dt_ref[...], dtb_ref[...], alog_ref[...], dsk_ref[...], rev=rev, ds=ds, n_head=n_head)
        h = h_scr[...]
        hs_ref[...] = h
        for g in range(2):
            lo = g * half
            bg = q["bm"][:, g * N_STATE:(g + 1) * N_STATE].astype(BF16)
            cg = q["cm"][:, g * N_STATE:(g + 1) * N_STATE].astype(BF16)
            scores = lax.dot_general(cg, bg, _DIMS["nt"], preferred_element_type=F32)
            hg = h[:, lo:lo + half]
            off = jnp.dot(cg, hg.astype(BF16), preferred_element_type=F32)
            for j in range(half // LANES):
                c0 = (lo + j * LANES) // HEAD_DIM
                ln = slice(lo + j * LANES, lo + (j + 1) * LANES)
                p_cat = jnp.concatenate([scores * _decay(q, c0), scores * _decay(q, c0 + 1)], axis=1).astype(BF16)
                diag = jnp.dot(p_cat, _split_heads(q["xs_w"][:, ln]).astype(BF16), preferred_element_type=F32)
                y_ref[:, ln] = (diag + q["e_w"][:, ln] * off[:, j * LANES:(j + 1) * LANES]
                                + q["dsk_w"][:, ln] * q["xh"][:, ln] + (add_ref[0][:, ln] if add_ref else 0.0))
            v = (q["wt_w"][:, lo:lo + half] * q["xs_w"][:, lo:lo + half]).astype(BF16)
            h_scr[:, lo:lo + half] = (q["dec_w"][:, lo:lo + half] * hg
                                      + lax.dot_general(bg, v, _DIMS["tn"], preferred_element_type=F32))
        hosted.finish()

    vec = pl.BlockSpec((1, LANES), lambda b, s: (0, 0))
    in_specs, out_shape, out_specs, scratch, args = hosted.call_args(
        [pl.BlockSpec((CHUNK, xw), lambda b, s: (blk(b, s), 0)),
         pl.BlockSpec((CHUNK, LANES), lambda b, s: (blk(b, s), dt_cb)), vec, vec, vec]
        + [pl.BlockSpec((CHUNK, ds), lambda b, s: (y_blk(b, s), 0))] * (add is not None),
        (_big((n_ex * seq_len, ds), F32), _big((n_ex, n_step, N_STATE, ds), F32)),
        (pl.BlockSpec((CHUNK, ds), lambda b, s: (y_blk(b, s), 0)),
         pl.BlockSpec((None, None, N_STATE, ds), lambda b, s: (b, s, 0, 0))),
        [pltpu.VMEM((N_STATE, ds), F32)], [xbc, dt_raw, dtb, alog, dsk] + ([] if add is None else [add]))
    return hosted.results(pl.pallas_call(
        body, name=name, grid=(n_ex, n_step), out_shape=out_shape, in_specs=in_specs, out_specs=out_specs,
        scratch_shapes=scratch, compiler_params=_params(40 << 20, 2),
    )(*_in_hbm(args)))


def ssd_bwd(name, xbc, dt_raw, dt_cb, hs, dy, dtb, alog, dsk, *, rev, n_ex, seq_len, ctx_len, ds, rider=None, add=None):
    n_head, half = ds // HEAD_DIM, ds // 2
    n_ctx, n_lat = ctx_len // CHUNK, seq_len // CHUNK
    n_step = n_ctx + n_lat
    n_tok = n_ex * (seq_len + ctx_len)
    blk0 = functools.partial(_ssd_blocks, rev=rev, n_ctx=n_ctx, n_lat=n_lat, lat_blocks=n_ex * n_lat)
    step = lambda sp: n_step - 1 - sp
    blk = lambda b, sp: blk0(b, step(sp))
    xw = xbc.shape[1]

    def dy_blk(b, sp):
        sl = jnp.maximum(step(sp), n_ctx) - n_ctx
        return b * n_lat + ((n_lat - 1 - sl) if rev else sl)

    hosted = _Hosted(rider, 7 + (add is not None), 5, 1, (n_ex, n_step))

    def body(*refs):
        ((xbc_ref, dt_ref, hs_ref, dy_ref, dtb_ref, alog_ref, dsk_ref, *add_ref),
         (dxbc_ref, ddt_ref, dalog_ref, ddtb_ref, ddsk_ref), (dh_scr,)) = hosted.split(refs)
        b, sp = pl.program_id(0), pl.program_id(1)
        more = (lambda cols: add_ref[0][:, cols]) if add_ref else (lambda cols: 0.0)

        @pl.when(sp == 0)
        def _():
            dh_scr[...] = jnp.zeros_like(dh_scr)

        @pl.when((sp == 0) & (b == 0))
        def _():
            dalog_ref[...] = jnp.zeros_like(dalog_ref)
            ddtb_ref[...] = jnp.zeros_like(ddtb_ref)
            ddsk_ref[...] = jnp.zeros_like(ddsk_ref)

        q = _ssd_common(xbc_ref[...], dt_ref[...], dtb_ref[...], alog_ref[...], dsk_ref[...], rev=rev, ds=ds, n_head=n_head)
        h = hs_ref[...]
        d_y = jnp.where(step(sp) >= n_ctx, dy_ref[...], 0.0)
        dh_next = dh_scr[...]
        lane_row = lax.broadcasted_iota(jnp.int32, (1, LANES), 1)
        d_cs = jnp.zeros((CHUNK, LANES), F32)
        dxs_parts, de_parts, dwt_parts, ddec_parts = [], [], [], []
        for g in range(2):
            lo = g * half
            gs = slice(lo, lo + half)
            bg = q["bm"][:, g * N_STATE:(g + 1) * N_STATE].astype(BF16)
            cg = q["cm"][:, g * N_STATE:(g + 1) * N_STATE].astype(BF16)
            scores = lax.dot_general(cg, bg, _DIMS["nt"], preferred_element_type=F32)
            hg, dyg, dhn = h[:, gs], d_y[:, gs], dh_next[:, gs]
            off = jnp.dot(cg, hg.astype(BF16), preferred_element_type=F32)
            d_off = (q["e_w"][:, gs] * dyg).astype(BF16)
            de_parts.append(dyg * off)
            d_c = lax.dot_general(d_off, hg.astype(BF16), _DIMS["nt"], preferred_element_type=F32)
            dh_scr[:, gs] = (lax.dot_general(cg, d_off, _DIMS["tn"], preferred_element_type=F32)
                             + q["dec_w"][:, gs] * dhn)
            b_dh = jnp.dot(bg, dhn.astype(BF16), preferred_element_type=F32)
            v = q["wt_w"][:, gs] * q["xs_w"][:, gs]
            d_b = lax.dot_general(v.astype(BF16), dhn.astype(BF16), _DIMS["nt"], preferred_element_type=F32)
            dwt_parts.append(q["xs_w"][:, gs] * b_dh)
            ddec_parts.append(jnp.sum(hg * dhn, axis=0, keepdims=True))
            d_scores = jnp.zeros((CHUNK, CHUNK), F32)
            for j in range(half // LANES):
                c0 = (lo + j * LANES) // HEAD_DIM
                ln = slice(lo + j * LANES, lo + (j + 1) * LANES)
                l0, l1 = _decay(q, c0), _decay(q, c0 + 1)
                p0, p1 = scores * l0, scores * l1
                dy_st = _split_heads(d_y[:, ln]).astype(BF16)
                d_p = lax.dot_general(dy_st, q["xs_w"][:, ln].astype(BF16), _DIMS["nt"], preferred_element_type=F32)
                d_p0, d_p1 = d_p[:CHUNK], d_p[CHUNK:]
                d_scores = d_scores + d_p0 * l0 + d_p1 * l1
                for col, t in ((c0, d_p0 * p0), (c0 + 1, d_p1 * p1)):
                    d_cs = d_cs + jnp.sum(t - t.T, axis=1, keepdims=True) * (lane_row == col).astype(F32)
                p_st = jnp.concatenate([p0, p1], axis=0).astype(BF16)
                dxs_parts.append(lax.dot_general(p_st, dy_st, _DIMS["tn"], preferred_element_type=F32)
                                 + q["wt_w"][:, ln] * b_dh[:, j * LANES:(j + 1) * LANES])
            d_sc = d_scores.astype(BF16)
            d_c = d_c + jnp.dot(d_sc, bg, preferred_element_type=F32)
            d_b = d_b + lax.dot_general(d_sc, cg, _DIMS["tn"], preferred_element_type=F32)
            b_cols, c_cols = slice(ds + g * N_STATE, ds + (g + 1) * N_STATE), slice(ds + (2 + g) * N_STATE, ds + (3 + g) * N_STATE)
            dxbc_ref[:, b_cols] = d_b + more(b_cols)
            dxbc_ref[:, c_cols] = d_c + more(c_cols)
        d_xs = jnp.concatenate(dxs_parts, axis=1)
        narrow_m = (lax.broadcasted_iota(jnp.int32, (ds, LANES), 0) // HEAD_DIM
                    == lax.broadcasted_iota(jnp.int32, (ds, LANES), 1)).astype(BF16)
        rows8 = lambda v: jnp.broadcast_to(v, (8, ds))
        stacked = jnp.concatenate(
            [jnp.concatenate(dwt_parts, axis=1), jnp.concatenate(de_parts, axis=1), d_xs * q["xh"],
             rows8(jnp.concatenate(ddec_parts, axis=1)), rows8(jnp.sum(d_y * q["xh"], axis=0, keepdims=True))], axis=0)
        sums = jnp.dot(stacked.astype(BF16), narrow_m, preferred_element_type=F32)
        n_wt, n_e, n_xs = sums[:CHUNK], sums[CHUNK:2 * CHUNK], sums[2 * CHUNK:3 * CHUNK]
        n_dec, n_dsk = sums[3 * CHUNK:3 * CHUNK + 1], sums[3 * CHUNK + 8:3 * CHUNK + 9]
        e, wt, dec = jnp.exp(q["cs"]), jnp.exp(q["tot"] - q["cs"]), jnp.exp(q["tot"])
        d_wt = n_wt * wt
        d_cs = d_cs + n_e * e - d_wt
        d_tot = jnp.sum(d_wt, axis=0, keepdims=True) + n_dec * dec
        d_da = jnp.dot(q["tri_t"], d_cs, precision=HI, preferred_element_type=F32) + d_tot
        d_dt = d_da * q["a"] + n_xs
        dxbc_ref[:, :ds] = d_xs * q["dt_w"] + q["dsk_w"] * d_y + more(slice(0, ds))
        dalog_ref[...] += jnp.sum(d_da * q["dt"], axis=0, keepdims=True) * q["a"]
        d_raw = d_dt * q["sig"]
        ddtb_ref[...] += jnp.sum(d_raw, axis=0, keepdims=True)
        ddsk_ref[...] += n_dsk
        ddt_ref[...] = pltpu.roll(d_raw, n_head, axis=1) if rev else d_raw
        hosted.finish()

    vec = pl.BlockSpec((1, LANES), lambda b, s: (0, 0))
    vec_shape = jax.ShapeDtypeStruct((1, LANES), F32)
    in_specs, out_shape, out_specs, scratch, args = hosted.call_args(
        [pl.BlockSpec((CHUNK, xw), lambda b, s: (blk(b, s), 0)),
         pl.BlockSpec((CHUNK, LANES), lambda b, s: (blk(b, s), dt_cb)),
         pl.BlockSpec((None, None, N_STATE, ds), lambda b, s: (b, step(s), 0, 0)),
         pl.BlockSpec((CHUNK, ds), lambda b, s: (dy_blk(b, s), 0)), vec, vec, vec]
        + [pl.BlockSpec((CHUNK, xw), lambda b, s: (blk(b, s), 0))] * (add is not None),
        (_big((n_tok, xw), F32), _big((n_tok, LANES), F32), vec_shape, vec_shape, vec_shape),
        (pl.BlockSpec((CHUNK, xw), lambda b, s: (blk(b, s), 0)),
         pl.BlockSpec((CHUNK, LANES), lambda b, s: (blk(b, s), 0)), vec, vec, vec),
        [pltpu.VMEM((N_STATE, ds), F32)], [xbc, dt_raw, hs, dy, dtb, alog, dsk] + ([] if add is None else [add]))
    return hosted.results(pl.pallas_call(
        body, name=name, grid=(n_ex, n_step), out_shape=out_shape, in_specs=in_specs, out_specs=out_specs,
        scratch_shapes=scratch, compiler_params=_params(48 << 20, 2),
    )(*_in_hbm(args)))


def final_loss(x3, target, w, *, tm):
    n, d = x3.shape

    def body(x_ref, t_ref, w_ref, dx_ref, dw_ref, loss_ref):
        i = pl.program_id(0)
        t = t_ref[...]

        def per_feature(xv, wv):
            err = _rms(xv, wv) - t
            return 0.5 * jnp.sum(err * err, axis=0, keepdims=True) / d

        lv, vjp = jax.vjp(per_feature, x_ref[...], w_ref[...])
        dx, dw = vjp(jnp.ones_like(lv))
        dx_ref[...] = dx

        @pl.when(i == 0)
        def _():
            dw_ref[...] = dw
            loss_ref[...] = lv

        @pl.when(i > 0)
        def _():
            dw_ref[...] += dw
            loss_ref[...] += lv

    tile = pl.BlockSpec((tm, d), lambda i: (i, 0))
    vec = pl.BlockSpec((1, d), lambda i: (0, 0))
    return pl.pallas_call(
        body, name="final_loss", grid=(n // tm,), in_specs=[tile, tile, vec],
        out_shape=(jax.ShapeDtypeStruct((n, d), F32), jax.ShapeDtypeStruct((1, d), F32), jax.ShapeDtypeStruct((1, d), F32)),
        out_specs=(tile, vec, vec), compiler_params=_params(tm * d * 4 * 16 + (8 << 20)),
    )(x3, target, w)


def sum_slots(name, arr, out_dtype=F32):
    n_slot, n_row, width = arr.shape
    tm = _row_tile(n_row, width * n_slot, cap_bytes=STREAM_TILE_BYTES * 14, mult=16)
    vmem = 2 * n_slot * tm * width * arr.dtype.itemsize + 4 * tm * width * 4

    def body(a_ref, o_ref):
        acc = a_ref[0].astype(F32)
        for j in range(1, n_slot):
            acc = acc + a_ref[j].astype(F32)
        o_ref[...] = acc.astype(o_ref.dtype)

    return pl.pallas_call(
        body, name=name, grid=(n_row // tm,), out_shape=jax.ShapeDtypeStruct((n_row, width), out_dtype),
        in_specs=[pl.BlockSpec((n_slot, tm, width), lambda i: (0, i, 0))],
        out_specs=pl.BlockSpec((tm, width), lambda i: (i, 0)), compiler_params=_params(vmem + (4 << 20)),
    )(arr)


def adamw(name, w, g_slots, m, v):
    n_slot, n_row, width = g_slots.shape
    tm = _row_tile(n_row, width, cap_bytes=STREAM_TILE_BYTES)
    if g_slots.dtype == BF16 and tm % 16:
        tm16 = _row_tile(n_row, width, cap_bytes=STREAM_TILE_BYTES, mult=16)
        if tm16 % 16 == 0:
            tm = tm16
        else:
            g_slots = g_slots.astype(F32)

    def body(w_ref, g_ref, m_ref, v_ref, go_ref, d_ref, mo_ref, vo_ref):
        g = g_ref[0].astype(F32)
        for j in range(1, n_slot):
            g = g + g_ref[j].astype(F32)
        m2 = ADAM_B1 * m_ref[...] + (1.0 - ADAM_B1) * g
        v2 = ADAM_B2 * v_ref[...] + (1.0 - ADAM_B2) * jnp.square(g)
        m_hat = m2 / (1.0 - ADAM_B1 ** ADAM_STEP)
        v_hat = v2 / (1.0 - ADAM_B2 ** ADAM_STEP)
        go_ref[...] = g
        d_ref[...] = -ADAM_LR * (m_hat / (jnp.sqrt(v_hat) + ADAM_EPS) + ADAM_WD * w_ref[...])
        mo_ref[...] = m2
        vo_ref[...] = v2

    tile = pl.BlockSpec((tm, width), lambda i: (i, 0))
    shape = jax.ShapeDtypeStruct((n_row, width), F32)
    return pl.pallas_call(
        body, name=name, grid=(n_row // tm,), out_shape=(shape,) * 4,
        in_specs=[tile, pl.BlockSpec((n_slot, tm, width), lambda i: (0, i, 0)), tile, tile],
        out_specs=(tile,) * 4, compiler_params=_params(2 * (7 + n_slot) * tm * width * 4 + (4 << 20)),
    )(w, g_slots, m, v)


def cctx_grad(q_all, c_ctx_row):
    d = c_ctx_row.shape[1]

    def body(q_ref, c_ref, o_ref):
        acc = q_ref[0, 0:1, :]
        for j in (2, 4, 6):
            acc = acc + q_ref[j, 0:1, :]
        _, vjp = jax.vjp(_silu, c_ref[...])
        o_ref[...] = vjp(acc)[0]

    return pl.pallas_call(
        body, name="cctx_grad", out_shape=jax.ShapeDtypeStruct((1, d), F32),
    )(q_all, c_ctx_row)


def loss_total(pack_sum, d):
    def body(p_ref, o_ref):
        o_ref[...] = jnp.sum(p_ref[:, 0:d], axis=1, keepdims=True)

    return pl.pallas_call(
        body, name="loss_total", out_shape=jax.ShapeDtypeStruct((1, 1), F32),
    )(pack_sum)


class _Plan:
    def __init__(self):
        self.builders, self.got = {}, {}

    def on(self, host, key, builder):
        self.builders.setdefault(host, []).append((key, builder))

    def run(self, host, fn, *args, **kw):
        if host not in self.builders:
            return fn(host, *args, **kw)
        keys, riders = zip(*[(key, builder(self)) for key, builder in self.builders[host]])
        res, landed = fn(host, *args, rider=Riders(riders), **kw)
        for key, r in zip(keys, riders):
            self.got[key], landed = landed[:r.n], landed[r.n:]
        return res


def _val(w):
    return w() if callable(w) else w


def _matmul_tile(n_rows, tm):
    return 2 * tm if n_rows % (2 * tm) == 0 else tm


def _ffn_fwd(plan, tag, xin, n_rows, tm, seg_fn, shift, scale, gate, norm_w, wg, wu, wd, fuse_gate_up=False):
    d = xin[1]
    n_tiles = n_rows // tm
    (h,) = plan.run(f"{tag}_norm", rowwise, fn_norm_mod, [xin], [shift, scale], [norm_w], [(n_rows, d, BF16)],
                    tm=tm, n_tiles=n_tiles, seg_fn=seg_fn)
    tmm = _matmul_tile(n_rows, tm)
    if fuse_gate_up:
        g, u, act = plan.run(f"{tag}_gate_up", matmul, [(h, _val(wg)), (h, _val(wu))], "nn", b_ch=True, out_ch=True,
                             tm=min(tm, 256), fold=True,
                             post=([], lambda ag, au: (ag, au, fn_act(ag, au)[0]), [BF16, BF16, BF16]))
    else:
        g = plan.run(f"{tag}_gate", matmul, [(h, _val(wg))], "nn", out_dtype=BF16, b_ch=True, out_ch=True, tm=tmm)
        u, act = plan.run(f"{tag}_up", matmul, [(h, _val(wu))], "nn", b_ch=True, out_ch=True, tm=tm, fold=True,
                          post=([g], lambda acc, gv: (acc, fn_act(gv, acc)[0]), [BF16, BF16]))
    f = plan.run(f"{tag}_down", matmul, [(act, _val(wd))], "nn", a_ch=True, b_ch=True, tm=tmm, fold=True)
    (xo,) = plan.run(f"{tag}_resid", rowwise, make_fn_resid(0.5), [xin, row(f)], [gate], [], [(n_rows, d, F32)],
                     tm=tm, n_tiles=n_tiles, seg_fn=seg_fn)
    return xo, (h, g, u, act, f)


def _ffn_bwd(plan, tag, d_xo, saved, xin, n_rows, tm, seg_fn, first_fn, shift, scale, gate, norm_w, wg, wu, wd, dx_rows, dx_limit):
    h, g, u, act, f = saved
    d = xin[1]
    n_tiles = n_rows // tm
    n_ch, _, n_hid = g.shape
    d_f, d_gate = plan.run(f"{tag}_resid_bwd", rowwise_bwd, make_fn_resid(0.5), [xin, row(f)], [gate], [], [[row(d_xo)]],
                           [None, (n_rows, BF16, None)], tm=tm, n_tiles=n_tiles, seg_fn=seg_fn, first_fn=first_fn)
    tmm = _matmul_tile(n_rows, tm)
    def act_vjp(d_act, gv, uv):
        s = jax.nn.sigmoid(gv)
        gs = gv * s
        return d_act * uv * (s + gs * (1.0 - s)), d_act * gs
    d_g, d_u = plan.run(f"{tag}_down_dx", matmul, [(d_f, wd)], "nt", b_ch=True, out_ch=True, tm=tmm,
                        post=([g, u], act_vjp, [BF16, BF16]))
    plan.got[f"{tag}_d_wd"] = plan.run(f"{tag}_down_dw", matmul, [(act, d_f)], "tn", out_dtype=BF16, a_ch=True, out_ch=True, tm=tmm)
    d_h = plan.run(f"{tag}_up_dx", matmul, [(d_g, wg), (d_u, wu)], "nt", a_ch=True, b_ch=True, tm=tmm)
    plan.got[f"{tag}_d_wg"] = plan.run(f"{tag}_gate_dw", matmul, [(d_g, h)], "tn", out_dtype=BF16, a_ch=True, out_ch=True, tm=tmm)
    plan.got[f"{tag}_d_wu"] = plan.run(f"{tag}_up_dw", matmul, [(d_u, h)], "tn", out_dtype=BF16, a_ch=True, out_ch=True, tm=tmm)
    d_x, d_shift, d_scale, d_nw = plan.run(
        f"{tag}_norm_bwd", rowwise_bwd, fn_norm_mod, [xin], [shift, scale], [norm_w], [[row(d_h)]], [(dx_rows, F32, dx_limit)],
        tm=tm, n_tiles=n_tiles, seg_fn=seg_fn, first_fn=first_fn, adds={0: (row(d_xo), None)})
    return d_x, (d_shift, d_scale, d_gate), d_nw


def kernel(x, c, ctx, c_ctx, w_mod, b_mod, norm_ffn1, ffn1_gate, ffn1_up, ffn1_down, norm_mix, w_in, ssm_conv_w, ssm_conv_b, dt_bias_fwd, dt_bias_bwd, a_log_fwd, a_log_bwd, ssm_d, ssm_norm_w, cconv_w, cconv_b, cconv_ln_w, cconv_ln_b, w_out, norm_ffn2, ffn2_gate, ffn2_up, ffn2_down, final_norm, loss_target, m_c_ctx, m_w_mod, m_b_mod, m_norm_ffn1, m_ffn1_gate, m_ffn1_up, m_ffn1_down, m_norm_mix, m_w_in, m_ssm_conv_w, m_ssm_conv_b, m_dt_bias_fwd, m_dt_bias_bwd, m_a_log_fwd, m_a_log_bwd, m_ssm_d, m_ssm_norm_w, m_cconv_w, m_cconv_b, m_cconv_ln_w, m_cconv_ln_b, m_w_out, m_norm_ffn2, m_ffn2_gate, m_ffn2_up, m_ffn2_down, m_final_norm, v_c_ctx, v_w_mod, v_b_mod, v_norm_ffn1, v_ffn1_gate, v_ffn1_up, v_ffn1_down, v_norm_mix, v_w_in, v_ssm_conv_w, v_ssm_conv_b, v_dt_bias_fwd, v_dt_bias_bwd, v_a_log_fwd, v_a_log_bwd, v_ssm_d, v_ssm_norm_w, v_cconv_w, v_cconv_b, v_cconv_ln_w, v_cconv_ln_b, v_w_out, v_norm_ffn2, v_ffn2_gate, v_ffn2_up, v_ffn2_down, v_final_norm):
    weights = dict(c_ctx=c_ctx, w_mod=w_mod, b_mod=b_mod, norm_ffn1=norm_ffn1, ffn1_gate=ffn1_gate, ffn1_up=ffn1_up, ffn1_down=ffn1_down, norm_mix=norm_mix, w_in=w_in, ssm_conv_w=ssm_conv_w, ssm_conv_b=ssm_conv_b, dt_bias_fwd=dt_bias_fwd, dt_bias_bwd=dt_bias_bwd, a_log_fwd=a_log_fwd, a_log_bwd=a_log_bwd, ssm_d=ssm_d, ssm_norm_w=ssm_norm_w, cconv_w=cconv_w, cconv_b=cconv_b, cconv_ln_w=cconv_ln_w, cconv_ln_b=cconv_ln_b, w_out=w_out, norm_ffn2=norm_ffn2, ffn2_gate=ffn2_gate, ffn2_up=ffn2_up, ffn2_down=ffn2_down, final_norm=final_norm)
    mom1 = dict(c_ctx=m_c_ctx, w_mod=m_w_mod, b_mod=m_b_mod, norm_ffn1=m_norm_ffn1, ffn1_gate=m_ffn1_gate, ffn1_up=m_ffn1_up, ffn1_down=m_ffn1_down, norm_mix=m_norm_mix, w_in=m_w_in, ssm_conv_w=m_ssm_conv_w, ssm_conv_b=m_ssm_conv_b, dt_bias_fwd=m_dt_bias_fwd, dt_bias_bwd=m_dt_bias_bwd, a_log_fwd=m_a_log_fwd, a_log_bwd=m_a_log_bwd, ssm_d=m_ssm_d, ssm_norm_w=m_ssm_norm_w, cconv_w=m_cconv_w, cconv_b=m_cconv_b, cconv_ln_w=m_cconv_ln_w, cconv_ln_b=m_cconv_ln_b, w_out=m_w_out, norm_ffn2=m_norm_ffn2, ffn2_gate=m_ffn2_gate, ffn2_up=m_ffn2_up, ffn2_down=m_ffn2_down, final_norm=m_final_norm)
    mom2 = dict(c_ctx=v_c_ctx, w_mod=v_w_mod, b_mod=v_b_mod, norm_ffn1=v_norm_ffn1, ffn1_gate=v_ffn1_gate, ffn1_up=v_ffn1_up, ffn1_down=v_ffn1_down, norm_mix=v_norm_mix, w_in=v_w_in, ssm_conv_w=v_ssm_conv_w, ssm_conv_b=v_ssm_conv_b, dt_bias_fwd=v_dt_bias_fwd, dt_bias_bwd=v_dt_bias_bwd, a_log_fwd=v_a_log_fwd, a_log_bwd=v_a_log_bwd, ssm_d=v_ssm_d, ssm_norm_w=v_ssm_norm_w, cconv_w=v_cconv_w, cconv_b=v_cconv_b, cconv_ln_w=v_cconv_ln_w, cconv_ln_b=v_cconv_ln_b, w_out=v_w_out, norm_ffn2=v_norm_ffn2, ffn2_gate=v_ffn2_gate, ffn2_up=v_ffn2_up, ffn2_down=v_ffn2_down, final_norm=v_final_norm)
    order = list(weights)

    n_ex, seq_len, d = x.shape
    ctx_len = ctx.shape[1]
    ds = d
    n_head = ds // HEAD_DIM
    xw = ds + 4 * N_STATE
    n_lat, n_ctx_rows = n_ex * seq_len, n_ex * ctx_len
    n_tok = n_lat + n_ctx_rows
    tm = math.gcd(math.gcd(512, seq_len), n_ctx_rows)
    seg_all, first_all = _segmenter(tm, seq_len, n_lat)
    lat_tiles = n_lat // tm

    xi, yi, ci = lax.axis_index("x"), lax.axis_index("y"), lax.axis_index("c")
    me, chip = 4 * xi + 2 * yi + ci, 2 * xi + yi

    (c_all,) = exchange("gather_c", [c], "all8")
    n_all = 8 * n_ex
    n_cond = -(-(n_all + 1) // 8) * 8
    cond = jnp.concatenate([c_all.reshape(n_all, d), c_ctx[None, :], jnp.zeros((n_cond - n_all - 1, d), F32)])
    mod_w = w_mod.shape[2]
    b_shard = lax.dynamic_slice(b_mod, (0, chip * mod_w), (1, mod_w))
    (mod_g,) = exchange("gather_mod", [mod_fwd(cond, w_mod[0], b_shard)], "chips")
    mod_full = mod_g.transpose(1, 0, 2).reshape(n_cond, N_CHIPS * mod_w)
    mod_mine = lax.dynamic_slice(mod_full, (me * n_ex, 0), (n_ex, 9 * d)).reshape(n_ex, 9, d)
    mod_ctx = mod_full[n_all].reshape(9, d)
    tabs = [jnp.concatenate([mod_mine[:, j], mod_ctx[j][None]])[:, None, :] for j in range(9)]
    lat = lambda t: t[:n_ex]

    bf = lambda w: w[0].astype(BF16)
    plan = _Plan()
    gather = lambda *ws: (lambda p: Rider(list(ws), "chips"))
    plan.on("ffn1_norm", "wg1", gather(bf(ffn1_gate)))
    plan.on("ffn1_gate", "wu1", gather(bf(ffn1_up)))
    plan.on("ffn1_up", "wd1", gather(bf(ffn1_down)))
    cut_a, cut_b = d * 5 // 8, d * 7 // 8
    plan.on("ffn1_down", "win_a", gather(bf(w_in)[:cut_a]))
    plan.on("ffn1_resid", "win_b", gather(bf(w_in)[cut_a:cut_b], ssm_conv_w[0], cconv_w[0]))
    xt = two_rows(x.reshape(n_lat, d), ctx.reshape(n_ctx_rows, d), lat_tiles)
    x1, saved1 = _ffn_fwd(plan, "ffn1", xt, n_tok, tm, seg_all, tabs[0], tabs[1], tabs[2], norm_ffn1,
                          lambda: plan.got["wg1"][0], lambda: plan.got["wu1"][0], lambda: plan.got["wd1"][0])
    (wg1,), (wu1,), (wd1,), (win_a,), (win_b, w5_g, w31_g) = (plan.got[k] for k in ("wg1", "wu1", "wd1", "win_a", "win_b"))
    (h2,), (win_c,) = rowwise("mix_norm", fn_norm_mod, [row(x1)], [tabs[3], tabs[4]], [norm_mix], [(n_tok, d, BF16)],
                              tm=tm, n_tiles=n_tok // tm, seg_fn=seg_all, rider=Rider([bf(w_in)[cut_b:]], "chips"))
    win_g = jnp.concatenate([win_a, win_b, win_c], axis=1)
    unshard_cols = lambda t: t.transpose(1, 0, 2).reshape(t.shape[1], N_CHIPS * t.shape[2])
    win = unshard_cols(win_g)
    o_x, o_dt, o_glu = ds, ds + xw, ds + xw + 2 * n_head
    w_z, w_xbc, w_dt = win[:, :ds], win[:, o_x:o_dt], win[:, o_dt:o_glu]
    w_ga, w_gb = win[:, o_glu:o_glu + d], win[:, o_glu + d:]
    w_dtp = jnp.concatenate([w_dt, jnp.zeros((d, LANES - 2 * n_head), BF16)], axis=1)
    w_cat = jnp.concatenate([w_z, w_ga, w_gb, w_xbc], axis=1)
    cbw = d // 2
    xbc_cb, dt_cb = 3 * d // cbw, 0
    w5, w31 = unshard_cols(w5_g), unshard_cols(w31_g)
    pad_vec = lambda v: jnp.concatenate([v.reshape(1, -1), jnp.zeros((1, LANES - v.size), F32)], axis=1)
    dtb_f, dtb_b, alog_f, alog_b = map(pad_vec, (dt_bias_fwd, dt_bias_bwd, a_log_fwd, a_log_bwd))
    dsk_f, dsk_b = pad_vec(ssm_d), jnp.zeros((1, LANES), F32)

    proj, (wg2,) = matmul("mix_proj", [(h2, w_cat)], "nn", out_dtype=BF16, tm=tm, rider=Rider([bf(ffn2_gate)], "chips"))
    dt_raw = matmul("mix_proj_dt", [(h2, w_dtp)], "nn", tm=tm)
    def conv5(name, src, cb0, flip):
        out = None
        for part, seq, off in (("lat", seq_len, 0), ("ctx", ctx_len, n_lat // ctx_len)):
            out = tapsum_roll(f"{name}_{part}", src, cb0, w5, 0, seq_len=seq, n_seq=n_ex, row_blk_off=off, width=seq,
                              piece=seq, cb=cbw, ncb=xw // cbw, pad=w5.shape[0] // 2, flip=flip,
                              place=((n_tok, xw), off, 0, out), out_dtype=F32 if flip else BF16)
        return out

    craw = conv5("xbc_conv", proj, xbc_cb, False)
    (xbc,) = rowwise("xbc_silu", fn_silu_bias, [row(craw)], [], [ssm_conv_b], [(n_tok, xw, F32)], tm=tm, n_tiles=n_tok // tm)
    ssd = dict(n_ex=n_ex, seq_len=seq_len, ctx_len=ctx_len, ds=ds)
    (y_f, hs_f), (wu2,) = ssd_fwd("ssd_fwd_f", xbc, dt_raw, dt_cb, dtb_f, alog_f, dsk_f, rev=False,
                                  rider=Rider([bf(ffn2_up)], "chips"), **ssd)
    (y_b, hs_b), (wout_g, wd2) = ssd_fwd("ssd_fwd_b", xbc, dt_raw, dt_cb, dtb_b, alog_b, dsk_b, rev=True,
                                         rider=Rider([bf(w_out), bf(ffn2_down)], "chips"), add=y_f, **ssd)
    wout = wout_g.reshape(2 * d, d)
    wo_y, wo_u = wout[:ds], wout[ds:]
    fn_gate = make_fn_gate_groupnorm(ds)
    (yn,) = rowwise("ssd_gate", fn_gate, [row(y_b), row(proj, d, 0)], [], [ssm_norm_w], [(n_lat, ds, BF16)],
                    tm=tm, n_tiles=lat_tiles)
    (u0,) = rowwise("glu", fn_glu, [row(proj, d, 1), row(proj, d, 2)], [], [], [(n_lat, d, BF16)], tm=tm, n_tiles=lat_tiles)
    cb31 = max(LANES, d // 4)
    ncb31 = (d // 2) // cb31
    pad31 = w31.shape[0] // 2
    piece31 = min(seq_len, 4 * GRID_W)
    v_w = tapsum_roll("cconv_cols", u0, 0, w31, 0, seq_len=seq_len, n_seq=n_ex, row_blk_off=0, width=GRID_W,
                      piece=piece31, cb=cb31, ncb=ncb31, pad=pad31, flip=False)
    v_h = tapsum_rows("cconv_rows", u0, ncb31, w31, ncb31, seq_len=seq_len, n_seq=n_ex, cb=cb31, ncb=ncb31, pad=pad31, flip=False)
    (un,) = rowwise("cconv_ln", fn_ln_silu, [row(v_w), row(v_h)], [], [cconv_b, cconv_ln_w, cconv_ln_b], [(n_lat, d, BF16)],
                    tm=tm, n_tiles=lat_tiles)
    mix = matmul("mix_out", [(yn, wo_y), (un, wo_u)], "nn", tm=tm)
    seg_lat, first_lat = _segmenter(tm, seq_len, n_lat)
    (x2,) = rowwise("mix_resid", make_fn_resid(1.0), [row(x1), row(mix)], [lat(tabs[5])], [], [(n_lat, d, F32)],
                    tm=tm, n_tiles=lat_tiles, seg_fn=seg_lat)
    x3, saved2 = _ffn_fwd(plan, "ffn2", row(x2), n_lat, tm, seg_lat, lat(tabs[6]), lat(tabs[7]), lat(tabs[8]), norm_ffn2, wg2, wu2, wd2,
                          fuse_gate_up=True)
    d_x3, d_final, loss_vec = final_loss(x3, loss_target.reshape(n_lat, d), final_norm.reshape(1, d), tm=tm)

    shard_cols = lambda t: t.reshape(t.shape[0], N_CHIPS, -1).transpose(1, 0, 2)

    def pieces(t):
        t = jnp.pad(t, ((0, 0), (0, -t.shape[1] % 32), (0, 0)))
        return t.reshape(2 * N_CHIPS, t.shape[1] // 2, t.shape[2]).astype(BF16)

    scatter = lambda *ts: Rider([pieces(t) for t in ts], "all8", scatter=True)
    halves = lambda names, landed: Rider([sum_slots(f"sum_{nm}", r, BF16) for nm, r in zip(names, landed)], "sibling")
    swapped = {}
    plan.on("ffn2_up_dx", "sc_ffn2_down", lambda p: scatter(p.got["ffn2_d_wd"]))
    plan.on("ffn2_up_dw", "sc_ffn2_gate", lambda p: scatter(p.got["ffn2_d_wg"]))
    d_x2, (d_s6, d_s7, d_g8), d_nffn2 = _ffn_bwd(
        plan, "ffn2", d_x3, saved2, row(x2), n_lat, tm, seg_lat, first_lat, lat(tabs[6]), lat(tabs[7]), lat(tabs[8]), norm_ffn2,
        wg2, wu2, wd2, n_lat, None)
    d_mix, d_g5 = rowwise_bwd("mix_resid_bwd", make_fn_resid(1.0), [row(x1), row(mix)], [lat(tabs[5])], [], [[row(d_x2)]],
                              [None, (n_lat, BF16, None)], tm=tm, n_tiles=lat_tiles, seg_fn=seg_lat, first_fn=first_lat)
    d_yn = matmul("mix_out_dy", [(d_mix, wo_y)], "nt", tm=tm)
    d_un = matmul("mix_out_du", [(d_mix, wo_u)], "nt", tm=tm)
    d_wout = jnp.concatenate([matmul("mix_out_dwy", [(yn, d_mix)], "tn", out_dtype=BF16, tm=tm),
                              matmul("mix_out_dwu", [(un, d_mix)], "tn", out_dtype=BF16, tm=tm)])
    d_vw, d_vh, d_cb, d_lnw, d_lnb = rowwise_bwd(
        "cconv_ln_bwd", fn_ln_silu, [row(v_w), row(v_h)], [], [cconv_b, cconv_ln_w, cconv_ln_b], [[row(d_un)]],
        [(n_lat, F32, None)] * 2, tm=tm, n_tiles=lat_tiles)
    d_u0 = tapsum_roll("cconv_cols_dx", d_vw, 0, w31, 0, seq_len=seq_len, n_seq=n_ex, row_blk_off=0, width=GRID_W,
                       piece=piece31, cb=cb31, ncb=ncb31, pad=pad31, flip=True, place=((n_lat, d), 0, 0, None))
    d_u0 = tapsum_rows("cconv_rows_dx", d_vh, 0, w31, ncb31, seq_len=seq_len, n_seq=n_ex, cb=cb31, ncb=ncb31, pad=pad31,
                       flip=True, place=((n_lat, d), 0, ncb31, d_u0))
    d_w31 = jnp.concatenate([
        tapgrad_roll("cconv_cols_dw", d_vw, 0, 0, u0, 0, 0, n_tap=w31.shape[0], seq_len=seq_len, n_seq=n_ex, width=GRID_W,
                     piece=piece31, cb=cb31, ncb=ncb31, pad=pad31),
        tapgrad_rows("cconv_rows_dw", d_vh, 0, u0, ncb31, n_tap=w31.shape[0], seq_len=seq_len, n_seq=n_ex, cb=cb31,
                     ncb=ncb31, pad=pad31)], axis=1)
    d_ga, d_gb = rowwise_bwd("glu_bwd", fn_glu, [row(proj, d, 1), row(proj, d, 2)], [], [], [[row(d_u0)]],
                             [(n_lat, BF16, None)] * 2, tm=tm, n_tiles=lat_tiles)
    d_ysum, d_z, d_ssmnw = rowwise_bwd(
        "ssd_gate_bwd", fn_gate, [row(y_b), row(proj, d, 0)], [], [ssm_norm_w], [[row(d_yn)]],
        [(n_lat, F32, None), (n_lat, BF16, None)], tm=tm, n_tiles=lat_tiles)
    (dxbc_f, ddt_f, dalog_f, ddtb_f, ddsk), landed = ssd_bwd(
        "ssd_bwd_f", xbc, dt_raw, dt_cb, hs_f, d_ysum, dtb_f, alog_f, dsk_f, rev=False,
        rider=scatter(plan.got["ffn2_d_wu"], d_wout.reshape(N_CHIPS, -1, d)), **ssd)
    (dxbc_b, ddt_b, dalog_b, ddtb_b, _), both = ssd_bwd(
        "ssd_bwd_b", xbc, dt_raw, dt_cb, hs_b, d_ysum, dtb_b, alog_b, dsk_b, rev=True,
        rider=halves(["ffn2_down", "ffn2_gate"], plan.got["sc_ffn2_down"] + plan.got["sc_ffn2_gate"]), add=dxbc_f, **ssd)
    swapped.update(zip(["ffn2_down", "ffn2_gate"], both))
    (d_craw, d_conv_b), both = rowwise_bwd(
        "xbc_silu_bwd", fn_silu_bias, [row(craw)], [], [ssm_conv_b], [[row(dxbc_b)]],
        [(n_tok, F32, None)], tm=tm, n_tiles=n_tok // tm, rider=halves(["ffn2_up", "w_out"], landed))
    swapped.update(zip(["ffn2_up", "w_out"], both))
    d_pxbc = conv5("xbc_conv_dx", d_craw, 0, True)
    g5 = lambda name, seq, off: tapgrad_roll(name, d_craw, 0, off, proj, xbc_cb, off, n_tap=w5.shape[0], seq_len=seq,
                                             n_seq=n_ex, width=seq, piece=seq, cb=cbw, ncb=xw // cbw, pad=w5.shape[0] // 2)
    d_w5 = g5("xbc_conv_lat_dw", seq_len, 0) + g5("xbc_conv_ctx_dw", ctx_len, n_lat // ctx_len)
    lat_pairs = [(d_z, w_z), (d_ga, w_ga), (d_gb, w_gb), (d_pxbc, w_xbc), (ddt_f, w_dtp), (ddt_b, w_dtp)]
    d_h2 = matmul("mix_proj_dx_lat", lat_pairs, "nt", rows=n_lat, tm=min(tm, 256), place=(n_tok, 0, None))
    d_h2 = matmul("mix_proj_dx_ctx", lat_pairs[3:], "nt", rows=n_ctx_rows, row_off=n_lat, tm=min(tm, 256),
                  place=(n_tok, n_lat, d_h2))
    d_wz = matmul("mix_proj_dwz", [(d_z, h2)], "tn", out_dtype=BF16, rows=n_lat, tm=tm)
    d_wga = matmul("mix_proj_dwa", [(d_ga, h2)], "tn", out_dtype=BF16, rows=n_lat, tm=tm)
    d_wgb = matmul("mix_proj_dwb", [(d_gb, h2)], "tn", out_dtype=BF16, rows=n_lat, tm=tm)
    d_wxbc = matmul("mix_proj_dwx", [(d_pxbc, h2)], "tn", out_dtype=BF16, tm=tm)
    d_wdt = matmul("mix_proj_dwt", [(ddt_f, h2), (ddt_b, h2)], "tn", out_dtype=BF16, tm=tm)
    d_win_t = jnp.concatenate([d_wz, d_wxbc, d_wdt[:2 * n_head], d_wga, d_wgb]).reshape(N_CHIPS, -1, d)
    d_x1, d_s3, d_s4, d_nmix = rowwise_bwd(
        "mix_norm_bwd", fn_norm_mod, [row(x1)], [tabs[3], tabs[4]], [norm_mix], [[row(d_h2)]], [(n_tok, F32, None)],
        tm=tm, n_tiles=n_tok // tm, seg_fn=seg_all, first_fn=first_all, adds={0: (row(d_x2), lat_tiles)})
    mix_names = ["w_in", "ssm_conv_w", "cconv_w"]
    plan.on("ffn1_down_dx", "sc_conv", lambda p: scatter(shard_cols(d_w5), shard_cols(d_w31)))
    plan.on("ffn1_up_dx", "sc_win", lambda p: scatter(d_win_t))
    plan.on("ffn1_gate_dw", "sc_ffn1_down", lambda p: scatter(p.got["ffn1_d_wd"]))
    plan.on("ffn1_up_dw", "sc_ffn1_gate", lambda p: scatter(p.got["ffn1_d_wg"]))
    plan.on("ffn1_up_dw", "sw_mix", lambda p: halves(mix_names, p.got["sc_win"] + p.got["sc_conv"]))
    plan.on("ffn1_norm_bwd", "sc_ffn1_up", lambda p: scatter(p.got["ffn1_d_wu"]))
    plan.on("ffn1_norm_bwd", "sw_ffn1_gate", lambda p: halves(["ffn1_gate"], p.got["sc_ffn1_gate"]))
    plan.on("ffn1_up_dw", "sw_ffn1_down", lambda p: halves(["ffn1_down"], p.got["sc_ffn1_down"]))
    d_xt, (d_s0, d_s1, d_g2), d_nffn1 = _ffn_bwd(
        plan, "ffn1", d_x1, saved1, xt, n_tok, tm, seg_all, first_all, tabs[0], tabs[1], tabs[2], norm_ffn1, wg1, wu1, wd1,
        n_lat, lat_tiles)
    swapped.update(zip(mix_names + ["ffn1_down", "ffn1_gate"],
                       plan.got["sw_mix"] + plan.got["sw_ffn1_down"] + plan.got["sw_ffn1_gate"]))
    last_names = ["ffn1_up"]
    last = halves(last_names, plan.got["sc_ffn1_up"])
    grad_x = d_xt.reshape(n_ex, seq_len, d)

    with_ctx0 = lambda t: jnp.concatenate([t, jnp.zeros((1, 1, d), F32)])
    d_tabs = [d_s0, d_s1, d_g2, d_s3, d_s4, with_ctx0(d_g5), with_ctx0(d_s6), with_ctx0(d_s7), with_ctx0(d_g8)]
    d_mod_rows = jnp.concatenate([t[:, 0, :] for t in d_tabs], axis=1)
    n_pad_rows = -(-(n_ex + 1) // 8) * 8
    d_mod_rows = jnp.concatenate([d_mod_rows, jnp.zeros((n_pad_rows - n_ex - 1, 9 * d), F32)])
    small = [("loss", loss_vec), ("norm_ffn1", d_nffn1), ("norm_mix", d_nmix), ("ssm_conv_b", d_conv_b),
             ("dt_bias_fwd", ddtb_f[:, :n_head]), ("dt_bias_bwd", ddtb_b[:, :n_head]), ("a_log_fwd", dalog_f[:, :n_head]),
             ("a_log_bwd", dalog_b[:, :n_head]), ("ssm_d", ddsk[:, :n_head]), ("ssm_norm_w", d_ssmnw), ("cconv_b", d_cb),
             ("cconv_ln_w", d_lnw), ("cconv_ln_b", d_lnb), ("norm_ffn2", d_nffn2), ("final_norm", d_final)]
    n_small = sum(v.size for _, v in small)
    n_pack = -(-n_small // (8 * LANES)) * (8 * LANES)
    pack = jnp.concatenate([v.reshape(-1) for _, v in small] + [jnp.zeros((n_pack - n_small,), F32)]).reshape(-1, LANES)
    (pack_all, d_mod_all), both = exchange_many("gather_small_swap_last", [Rider([pack, d_mod_rows], "all8"), last])
    swapped.update(zip(last_names, both))
    pack_sum = sum_slots("small_sum", pack_all)
    loss = loss_total(pack_sum.reshape(1, n_pack), d).reshape(())
    flat_sum = pack_sum.reshape(-1)
    small_grads, pos = {}, 0
    for nm, v in small:
        small_grads[nm] = flat_sum[pos:pos + v.size]
        pos += v.size
    d_mod_all = d_mod_all.reshape(8 * n_pad_rows, 9 * d)
    cond_rows = [jnp.concatenate([cond[j * n_ex:(j + 1) * n_ex], c_ctx[None, :],
                                  jnp.zeros((n_pad_rows - n_ex - 1, d), F32)]) for j in range(8)]
    cond_bwd = jnp.concatenate(cond_rows)
    d_mod_shard = lax.dynamic_slice(d_mod_all, (0, chip * mod_w), (8 * n_pad_rows, mod_w))
    g_wmod, g_bmod, q_part = mod_bwd(cond_bwd, d_mod_shard, d_mod_all, w_mod[0],
                                     tuple(j * n_pad_rows + n_ex for j in range(8)))
    (q_all,) = exchange("gather_cctx", [q_part], "all8")
    g_cctx = cctx_grad(q_all, c_ctx.reshape(1, d))
    small_grads["c_ctx"], small_grads["b_mod"] = g_cctx.reshape(-1), g_bmod.reshape(-1)

    transposed = {"ffn1_gate", "ffn1_up", "ffn2_gate", "ffn2_up", "w_in"}
    results = {}
    for nm, both in swapped.items():
        flip = (lambda t: jnp.swapaxes(t, 1, 2)) if nm in transposed else (lambda t: t)
        shape = flip(weights[nm]).shape
        two_d = lambda t: flip(t).reshape(shape[-2], shape[-1])
        g_full = both.reshape(1, -1, shape[-1])[:, :shape[-2]]
        results[nm] = [flip(r.reshape(shape)) for r in
                       adamw(f"adamw_{nm}", two_d(weights[nm]), g_full, two_d(mom1[nm]), two_d(mom2[nm]))]
    results["w_mod"] = [r.reshape(w_mod.shape) for r in adamw("adamw_w_mod", w_mod[0], g_wmod[None], m_w_mod[0], v_w_mod[0])]
    small_names = [nm for nm in order if nm not in results]
    n_sm = sum(weights[nm].size for nm in small_names)
    n_smp = -(-n_sm // (8 * LANES)) * (8 * LANES)
    packed = lambda src: jnp.concatenate([src[nm].reshape(-1) for nm in small_names] + [jnp.zeros((n_smp - n_sm,), F32)]).reshape(-1, LANES)
    sm_out = adamw("adamw_small", packed(weights), packed(small_grads)[None], packed(mom1), packed(mom2))
    pos = 0
    for nm in small_names:
        size = weights[nm].size
        results[nm] = [r.reshape(-1)[pos:pos + size].reshape(weights[nm].shape) for r in sm_out]
        pos += size
    return (loss, grad_x, *[results[nm][0] for nm in order], *[results[nm][1] for nm in order],
            *[results[nm][2] for nm in order], *[results[nm][3] for nm in order])
```

```python
import functools
import math

import jax
import jax.numpy as jnp
from jax import lax
from jax.experimental import pallas as pl
from jax.experimental.pallas import tpu as pltpu

F32 = jnp.float32
BF16 = jnp.bfloat16
HI = lax.Precision.HIGHEST
MESH = pl.DeviceIdType.MESH

EPS = 1e-6
GRID_W = 64
HEAD_DIM = 64
N_STATE = 128
CHUNK = 128
LANES = 128
N_CHIPS = 4
ADAM_LR, ADAM_B1, ADAM_B2, ADAM_EPS, ADAM_WD, ADAM_STEP = 0.001, 0.9, 0.999, 1e-08, 0.01, 10
VMEM_CAP = 56 * 1024 * 1024
STREAM_TILE_BYTES = 3 << 19


def _params(vmem_bytes=None, n_axes=1):
    kw = dict(dimension_semantics=("arbitrary",) * n_axes)
    if vmem_bytes is not None:
        kw["vmem_limit_bytes"] = int(min(VMEM_CAP, max(32 * 1024 * 1024, vmem_bytes)))
    return pltpu.CompilerParams(**kw)


def _big(shape, dtype):
    return pltpu.HBM(tuple(shape), dtype)


def _in_hbm(args):
    return [pltpu.with_memory_space_constraint(a, pltpu.HBM) if a.size * a.dtype.itemsize >= (1 << 20) else a for a in args]


def _nbytes(shape, dtype):
    return math.prod(shape) * jnp.dtype(dtype).itemsize


def _row_tile(rows, width, cap_bytes=1 << 20, mult=8):
    best = None
    for t in range(mult, rows + 1, mult):
        if rows % t == 0 and t * width * 4 <= cap_bytes:
            best = t
    return best if best is not None else rows


_MODES = {"all8": (8, (1, 2, 3, 4, 5, 6, 7), 0), "chips": (4, (2, 4, 6), 1), "sibling": (2, (1,), 0)}


class Rider:
    def __init__(self, arrs, mode, scatter=False):
        self.arrs, self.scatter = list(arrs), scatter
        self.nslot, self.deltas, self.shift = _MODES[mode]
        self.n = len(self.arrs)
        self.out_shape = [jax.ShapeDtypeStruct((self.nslot,) + (a.shape[1:] if scatter else a.shape), a.dtype)
                          for a in self.arrs]
        any_spec = pl.BlockSpec(memory_space=pl.ANY)
        self.in_specs = [any_spec] * self.n
        self.out_specs = [any_spec] * self.n
        n_peer = len(self.deltas)
        self.scratch = [pltpu.SemaphoreType.DMA((self.n, n_peer)), pltpu.SemaphoreType.DMA((self.n, n_peer)),
                        pltpu.SemaphoreType.DMA((self.n,))]

    def _copies(self, ins, outs, sems, arrivals):
        send_sems, recv_sems, local_sems = sems
        x, y, c = lax.axis_index("x"), lax.axis_index("y"), lax.axis_index("c")
        me = 4 * x + 2 * y + c
        slot_of = lambda dev: (dev >> self.shift) & (self.nslot - 1)
        src = lambda a, slot: ins[a].at[slot] if self.scatter else ins[a]
        flip = lambda v, bit: 1 - v if bit else v

        def remote(a, k, d, from_slot, to_slot):
            return pltpu.make_async_remote_copy(
                src_ref=src(a, from_slot), dst_ref=outs[a].at[to_slot], send_sem=send_sems.at[a, k],
                recv_sem=recv_sems.at[a, k], device_id=(flip(x, (d >> 2) & 1), flip(y, (d >> 1) & 1), flip(c, d & 1)),
                device_id_type=MESH)

        mine = slot_of(me)
        local = [pltpu.make_async_copy(src(a, mine), outs[a].at[mine], local_sems.at[a]) for a in range(self.n)]
        sends = [remote(a, k, d, slot_of(me ^ d), mine) for k, d in enumerate(self.deltas) for a in range(self.n)]
        if not arrivals:
            return local, sends
        return local, sends, [remote(a, k, d, mine, slot_of(me ^ d)) for k, d in enumerate(self.deltas) for a in range(self.n)]

    def start(self, ins, outs, sems):
        local, sends = self._copies(ins, outs, sems, arrivals=False)
        for cp in local + sends:
            cp.start()

    def wait(self, ins, outs, sems):
        local, sends, recvs = self._copies(ins, outs, sems, arrivals=True)
        for cp in recvs:
            cp.wait_recv()
        for cp in sends:
            cp.wait_send()
        for cp in local:
            cp.wait()


class Riders:
    def __init__(self, riders):
        self.riders = list(riders)
        self.n = sum(r.n for r in self.riders)
        cat = lambda attr: [v for r in self.riders for v in getattr(r, attr)]
        self.arrs, self.out_shape, self.in_specs = cat("arrs"), cat("out_shape"), cat("in_specs")
        self.out_specs, self.scratch = cat("out_specs"), cat("scratch")

    def _each(self, method, ins, outs, sems):
        i = s = 0
        for r in self.riders:
            getattr(r, method)(ins[i:i + r.n], outs[i:i + r.n], sems[s:s + len(r.scratch)])
            i, s = i + r.n, s + len(r.scratch)

    def start(self, ins, outs, sems):
        self._each("start", ins, outs, sems)

    def wait(self, ins, outs, sems):
        self._each("wait", ins, outs, sems)


class _Hosted:
    def __init__(self, rider, n_in, n_out, n_scratch, grid):
        self.rider, self.n_in, self.n_out, self.n_scratch, self.grid = rider, n_in, n_out, n_scratch, grid
        self.n = rider.n if rider else 0

    def split(self, refs):
        a, b = self.n_in, self.n_in + self.n
        c, e = b + self.n_out, b + self.n_out + self.n
        self._r = (refs[a:b], refs[c:e], refs[e + self.n_scratch:])
        if self.rider:
            ids = [pl.program_id(ax) for ax in range(len(self.grid))]
            first = functools.reduce(jnp.logical_and, [i == 0 for i in ids]) if ids else True
            pl.when(first)(lambda: self.rider.start(*self._r))
        return refs[:a], refs[b:c], refs[e:e + self.n_scratch]

    def finish(self):
        if self.rider:
            ids = [pl.program_id(ax) for ax in range(len(self.grid))]
            last = functools.reduce(jnp.logical_and, [i == n - 1 for i, n in zip(ids, self.grid)]) if ids else True
            pl.when(last)(lambda: self.rider.wait(*self._r))

    def call_args(self, in_specs, out_shape, out_specs, scratch, args):
        r = self.rider
        if not r:
            return list(in_specs), tuple(out_shape), tuple(out_specs), list(scratch), list(args)
        return (list(in_specs) + r.in_specs, tuple(out_shape) + tuple(r.out_shape), tuple(out_specs) + tuple(r.out_specs),
                list(scratch) + r.scratch, list(args) + r.arrs)

    def results(self, res, unwrap=True):
        res = list(res) if isinstance(res, (tuple, list)) else [res]
        host = res[:self.n_out]
        host = host[0] if (self.n_out == 1 and unwrap) else tuple(host)
        return (host, res[self.n_out:]) if self.rider else host


def exchange_many(name, riders):
    both = Riders(riders)

    def body(*refs):
        ins, outs, sems = refs[:both.n], refs[both.n:2 * both.n], refs[2 * both.n:]
        both.start(ins, outs, sems)
        both.wait(ins, outs, sems)

    res = list(pl.pallas_call(
        body, name=name, out_shape=tuple(both.out_shape), in_specs=both.in_specs, out_specs=tuple(both.out_specs),
        scratch_shapes=both.scratch,
    )(*both.arrs))
    split = []
    for r in riders:
        split.append(res[:r.n])
        res = res[r.n:]
    return split


def exchange(name, arrs, mode, scatter=False):
    rider = Rider(arrs, mode, scatter)

    def body(*refs):
        ins, outs, sems = refs[:rider.n], refs[rider.n:2 * rider.n], refs[2 * rider.n:]
        rider.start(ins, outs, sems)
        rider.wait(ins, outs, sems)

    return pl.pallas_call(
        body, name=name, out_shape=tuple(rider.out_shape), in_specs=rider.in_specs, out_specs=tuple(rider.out_specs),
        scratch_shapes=rider.scratch,
    )(*arrs)


_DIMS = {"nn": (((1,), (0,)), ((), ())), "nt": (((1,), (1,)), ((), ())), "tn": (((0,), (0,)), ((), ()))}


def matmul(name, pairs, kind, *, a_ch=False, b_ch=False, out_ch=False, out_dtype=F32, rows=None, row_off=0, tm=512,
           rider=None, post=None, fold=False, place=None):
    a0, b0 = pairs[0]
    n_chunk = a0.shape[0] if a_ch else (b0.shape[0] if b_ch else 1)
    total_rows = a0.shape[-2]
    rows = total_rows - row_off if rows is None else rows
    tm = min(tm, rows)
    assert rows % tm == 0 and row_off % tm == 0, (name, rows, tm, row_off)
    n_rt, off = rows // tm, row_off // tm
    dims = _DIMS[kind]
    n_pair = len(pairs)

    if kind == "tn":
        grid, red_axis, n_red = (n_chunk, n_rt), 1, n_rt
        a_idx = (lambda k, i: (k, i + off, 0)) if a_ch else (lambda k, i: (i + off, 0))
        b_idx = (lambda k, i: (k, i + off, 0)) if b_ch else (lambda k, i: (i + off, 0))
        a_blk = lambda a: ((None, tm, a.shape[-1]) if a_ch else (tm, a.shape[-1]))
        b_blk = lambda b: ((None, tm, b.shape[-1]) if b_ch else (tm, b.shape[-1]))
        o2 = (a0.shape[-1], b0.shape[-1])
        out_shape = ((n_chunk,) + o2) if out_ch else o2
        out_spec = pl.BlockSpec((None,) + o2, lambda k, i: (k, 0, 0)) if out_ch else pl.BlockSpec(o2, lambda k, i: (0, 0))
        acc_shape = o2
    else:
        n_out = b0.shape[-1] if kind == "nn" else b0.shape[-2]
        b2 = b0.shape[-2:]
        if a_ch and b_ch and not out_ch and fold:
            grid, red_axis, n_red = (n_rt,), None, 1
            a_idx, b_idx = (lambda i: (0, i + off, 0)), (lambda i: (0, 0, 0))
            a_blk = lambda a: (n_chunk, tm, a.shape[-1])
            b_blk = lambda b: tuple(b.shape)
            out_shape, out_spec = (rows, n_out), pl.BlockSpec((tm, n_out), lambda i: (i, 0))
        elif a_ch and b_ch and not out_ch:
            grid, red_axis, n_red = (n_rt, n_chunk), 1, n_chunk
            a_idx, b_idx = (lambda i, k: (k, i + off, 0)), (lambda i, k: (k, 0, 0))
            a_blk = lambda a: (None, tm, a.shape[-1])
            b_blk = lambda b: (None,) + tuple(b.shape[-2:])
            out_shape, out_spec = (rows, n_out), pl.BlockSpec((tm, n_out), lambda i, k: (i, 0))
        elif out_ch and fold:
            assert b_ch and not a_ch and all(a is a0 for a, _ in pairs)
            grid, red_axis, n_red = (n_rt,), None, 1
            a_idx, b_idx = (lambda i: (i + off, 0)), (lambda i: (0, 0, 0))
            a_blk = lambda a: (tm, a.shape[-1])
            b_blk = lambda b: tuple(b.shape)
            out_shape, out_spec = (n_chunk, rows, n_out), pl.BlockSpec((n_chunk, tm, n_out), lambda i: (0, i, 0))
        elif out_ch:
            assert b_ch and not a_ch
            grid, red_axis, n_red = (n_chunk, n_rt), None, 1
            a_idx, b_idx = (lambda k, i: (i + off, 0)), (lambda k, i: (k, 0, 0))
            a_blk = lambda a: (tm, a.shape[-1])
            b_blk = lambda b: (None,) + tuple(b.shape[-2:])
            out_shape, out_spec = (n_chunk, rows, n_out), pl.BlockSpec((None, tm, n_out), lambda k, i: (k, i, 0))
        else:
            assert not (a_ch or b_ch)
            grid, red_axis, n_red = (n_rt,), None, 1
            a_idx, b_idx = (lambda i: (i + off, 0)), (lambda i: (0, 0))
            a_blk = lambda a: (tm, a.shape[-1])
            b_blk = lambda b: tuple(b.shape)
            out_shape, out_spec = (rows, n_out), pl.BlockSpec((tm, n_out), lambda i: (i, 0))
            if place is not None:
                out_shape, o_off = (place[0], n_out), place[1] // tm
                out_spec = pl.BlockSpec((tm, n_out), lambda i: (i + o_off, 0))
        acc_shape = (tm, n_out)

    into = [] if place is None or place[2] is None else [place[2]]
    post_ins, post_fn, out_dtypes = ([], None, [out_dtype]) if post is None else post
    hosted = _Hosted(rider, 2 * n_pair + len(post_ins) + len(into), len(out_dtypes), int(n_red > 1), grid)

    def body(*refs):
        ins, outs, scr = hosted.split(refs)

        def compute():
            acc = None
            for p in range(n_pair):
                for k in ([None] if not fold else range(n_chunk)):
                    pick = (lambda r: r[...]) if k is None else (lambda r: r[k])
                    d = lax.dot_general(pick(ins[2 * p]).astype(BF16), pick(ins[2 * p + 1]).astype(BF16), dims,
                                        preferred_element_type=F32)
                    acc = d if acc is None else acc + d
            return acc

        def emit(acc):
            vals = (acc,) if post_fn is None else post_fn(
                acc, *[r[...].astype(F32) for r in ins[2 * n_pair:2 * n_pair + len(post_ins)]])
            for o_ref, v in zip(outs, vals):
                o_ref[...] = v.astype(o_ref.dtype)

        if out_ch and fold:
            a_tile = ins[0][...].astype(BF16)
            for k in range(n_chunk):
                accs = [lax.dot_general(a_tile, ins[2 * p + 1][k].astype(BF16), dims, preferred_element_type=F32)
                        for p in range(n_pair)]
                tiles = [r[k].astype(F32) for r in ins[2 * n_pair:2 * n_pair + len(post_ins)]]
                vals = tuple(accs) if post_fn is None else post_fn(*accs, *tiles)
                for o_ref, v in zip(outs, vals):
                    o_ref[k] = v.astype(o_ref.dtype)
        elif n_red == 1:
            emit(compute())
        else:
            acc_ref = scr[0]
            r = pl.program_id(red_axis)

            @pl.when(r == 0)
            def _():
                acc_ref[...] = jnp.zeros_like(acc_ref)

            acc_ref[...] += compute()

            @pl.when(r == n_red - 1)
            def _():
                emit(acc_ref[...])
        hosted.finish()

    in_specs, args, vmem = [], [], 0
    for a, b in pairs:
        in_specs += [pl.BlockSpec(a_blk(a), a_idx), pl.BlockSpec(b_blk(b), b_idx)]
        args += [a, b]
        vmem += 2 * (_nbytes([s for s in a_blk(a) if s], a.dtype) + _nbytes([s for s in b_blk(b) if s], b.dtype))
    in_specs += [out_spec] * len(post_ins)
    args += list(post_ins)
    aliases = {len(args): 0} if into else {}
    in_specs += [pl.BlockSpec(memory_space=pl.ANY)] * len(into)
    args += into
    tiles_per_step = n_chunk if (out_ch and fold) else 1
    vmem += (3 + 2 * n_pair + tiles_per_step * (len(post_ins) + len(out_dtypes))) * _nbytes(acc_shape, F32)
    scratch = [pltpu.VMEM(acc_shape, F32)] if n_red > 1 else []
    in_specs, out_shapes, out_specs, scratch, args = hosted.call_args(
        in_specs, [_big(out_shape, dt) for dt in out_dtypes], [out_spec] * len(out_dtypes), scratch, args)
    return hosted.results(pl.pallas_call(
        body, name=name, out_shape=out_shapes, grid=grid, in_specs=in_specs, out_specs=out_specs,
        input_output_aliases=aliases, scratch_shapes=scratch, compiler_params=_params(vmem + (8 << 20), len(grid)),
    )(*_in_hbm(args)))


def row(arr, width=None, cb=0, roff=0):
    return (arr, arr.shape[-1] if width is None else width, cb, roff)


def two_rows(first, second, limit):
    return (first, first.shape[-1], 0, 0, (second, limit))


def _row_inputs(rows, tm):
    specs, arrs, slots = [], [], []
    for d in rows:
        second, limit = d[4] if len(d) > 4 else (None, None)
        slots.append((len(arrs), limit))
        specs.append(_row_spec(d[:4], tm, limit))
        arrs.append(d[0])
        if second is not None:
            specs.append(pl.BlockSpec((tm, d[1]), lambda i, limit=limit: (jnp.maximum(i - limit, 0), 0)))
            arrs.append(second)

    def read(refs, i):
        vals = []
        for at, limit in slots:
            v = refs[at][...].astype(F32)
            vals.append(v if limit is None else jnp.where(i < limit, v, refs[at + 1][...].astype(F32)))
        return vals

    return specs, arrs, read


def _row_spec(desc, tm, limit=None):
    _, width, cb, roff = desc[:4]
    if limit is None:
        return pl.BlockSpec((tm, width), lambda i: (i + roff, cb))
    return pl.BlockSpec((tm, width), lambda i: (jnp.minimum(i, limit - 1) + roff, cb))


def _segmenter(tm, seq_len, n_lat):
    seg = lambda i: jnp.where(i * tm < n_lat, (i * tm) // seq_len, n_lat // seq_len)
    first = lambda i: jnp.where(i * tm < n_lat, (i * tm) % seq_len == 0, i * tm == n_lat)
    return seg, first


def rowwise(name, fn, rows, segs, params, outs, *, tm, n_tiles, seg_fn=None, rider=None):
    row_specs, row_arrs, read_rows = _row_inputs(rows, tm)
    n_r, n_s, n_p = len(row_arrs), len(segs), len(params)
    hosted = _Hosted(rider, n_r + n_s + n_p, len(outs), 0, (n_tiles,))

    def body(*refs):
        ins, out_refs, _ = hosted.split(refs)
        vals = read_rows(ins[:n_r], pl.program_id(0)) + [r[...] for r in ins[n_r:]]
        res = fn(*vals)
        for o_ref, v in zip(out_refs, res):
            o_ref[...] = v.astype(o_ref.dtype)
        hosted.finish()

    in_specs = list(row_specs)
    in_specs += [pl.BlockSpec((None, 1, s.shape[-1]), lambda i: (seg_fn(i), 0, 0)) for s in segs]
    in_specs += [pl.BlockSpec(p.shape, lambda i: (0, 0)) for p in params]
    vmem = sum(2 * tm * d[1] * 4 for d in rows) + sum(3 * tm * w * 4 for _, w, _ in outs) + sum(2 * p.size * 4 for p in params)
    in_specs, out_shapes, out_specs, scratch, args = hosted.call_args(
        in_specs, [_big((r, w), dt) for r, w, dt in outs],
        [pl.BlockSpec((tm, w), lambda i: (i, 0)) for _, w, _ in outs], [], row_arrs + list(segs) + list(params))
    return hosted.results(pl.pallas_call(
        body, name=name, grid=(n_tiles,), in_specs=in_specs, out_shape=out_shapes, out_specs=out_specs,
        scratch_shapes=scratch, compiler_params=_params(2 * vmem + (8 << 20)),
    )(*_in_hbm(args)), unwrap=False)


def rowwise_bwd(name, fn, rows, segs, params, cts, row_grads, *, tm, n_tiles, seg_fn=None, first_fn=None, adds=None,
                rider=None):
    adds = adds or {}
    need = [k for k, v in enumerate(row_grads) if v is not None]
    row_specs, row_arrs, read_rows = _row_inputs(rows, tm)
    n_r, n_s, n_p = len(row_arrs), len(segs), len(params)
    n_ct = sum(len(lst) for lst in cts)
    add_keys = sorted(adds)
    hosted = _Hosted(rider, n_r + n_s + n_p + n_ct + len(add_keys), len(need) + n_s + n_p, 0, (n_tiles,))

    def body(*refs):
        host_in, host_out, _ = hosted.split(refs)
        it = iter(list(host_in) + list(host_out))
        row_refs = [next(it) for _ in range(n_r)]
        seg_refs = [next(it) for _ in range(n_s)]
        par_refs = [next(it) for _ in range(n_p)]
        ct_refs = [[next(it) for _ in lst] for lst in cts]
        add_refs = {k: next(it) for k in add_keys}
        rg_refs = {k: next(it) for k in need}
        sg_refs = [next(it) for _ in range(n_s)]
        pg_refs = [next(it) for _ in range(n_p)]
        i = pl.program_id(0)
        rv = read_rows(row_refs, i)
        sv = [r[...] for r in seg_refs]
        pv = [r[...] for r in par_refs]

        def f(*args):
            rr = list(rv)
            for j, k in enumerate(need):
                rr[k] = args[j]
            return fn(*rr, *args[len(need):])

        _, vjp = jax.vjp(f, *[rv[k] for k in need], *sv, *pv)
        ctv = []
        for lst in ct_refs:
            acc = lst[0][...].astype(F32)
            for r in lst[1:]:
                acc = acc + r[...].astype(F32)
            ctv.append(acc)
        g = vjp(tuple(ctv))
        for j, k in enumerate(need):
            gv = g[j]
            if k in adds:
                lim = adds[k][1]
                av = add_refs[k][...].astype(F32)
                gv = gv + (av if lim is None else jnp.where(i < lim, av, 0.0))
            lim = row_grads[k][2]
            if lim is None:
                rg_refs[k][...] = gv.astype(rg_refs[k].dtype)
            else:
                @pl.when(i < lim)
                def _(gv=gv, k=k):
                    rg_refs[k][...] = gv.astype(rg_refs[k].dtype)
        if n_s:
            opens = first_fn(i)
            for ref, gv in zip(sg_refs, g[len(need):len(need) + n_s]):
                @pl.when(opens)
                def _(ref=ref, gv=gv):
                    ref[...] = gv

                @pl.when(jnp.logical_not(opens))
                def _(ref=ref, gv=gv):
                    ref[...] += gv
        for ref, gv in zip(pg_refs, g[len(need) + n_s:]):
            @pl.when(i == 0)
            def _(ref=ref, gv=gv):
                ref[...] = gv

            @pl.when(i > 0)
            def _(ref=ref, gv=gv):
                ref[...] += gv
        hosted.finish()

    seg_spec = lambda s: pl.BlockSpec((None, 1, s.shape[-1]), lambda i: (seg_fn(i), 0, 0))
    par_spec = lambda p: pl.BlockSpec(p.shape, lambda i: (0, 0))
    in_specs = list(row_specs) + [seg_spec(s) for s in segs] + [par_spec(p) for p in params]
    args = row_arrs + list(segs) + list(params)
    for lst in cts:
        in_specs += [_row_spec(d, tm) for d in lst]
        args += [d[0] for d in lst]
    for k in add_keys:
        in_specs.append(_row_spec(adds[k][0], tm, adds[k][1]))
        args.append(adds[k][0][0])
    out_shape, out_specs = [], []
    for k in need:
        n_rows, dt, lim = row_grads[k]
        out_shape.append(_big((n_rows, rows[k][1]), dt))
        out_specs.append(_row_spec((None, rows[k][1], 0, 0), tm, lim))
    for s in segs:
        out_shape.append(jax.ShapeDtypeStruct(s.shape, F32))
        out_specs.append(seg_spec(s))
    for p in params:
        out_shape.append(jax.ShapeDtypeStruct(p.shape, F32))
        out_specs.append(par_spec(p))
    vmem = sum(tm * d[1] * 4 for d in rows) * 6 + n_ct * tm * max(d[1] for d in rows) * 8
    in_specs, out_shape, out_specs, scratch, args = hosted.call_args(in_specs, out_shape, out_specs, [], args)
    return hosted.results(pl.pallas_call(
        body, name=name, grid=(n_tiles,), in_specs=in_specs, out_shape=out_shape, out_specs=out_specs,
        scratch_shapes=scratch, compiler_params=_params(vmem + (8 << 20)),
    )(*_in_hbm(args)), unwrap=False)


def _silu(v):
    return v * jax.nn.sigmoid(v)


def _rms(v, w):
    return v * lax.rsqrt(jnp.mean(v * v, axis=-1, keepdims=True) + EPS) * w


def fn_norm_mod(x, shift, scale, w):
    return (_rms(x, w) * (1.0 + scale) + shift,)


def fn_act(g, u):
    return (_silu(g) * u,)


def make_fn_resid(coef):
    def fn(x, f, gate):
        return (x + coef * gate * f,)
    return fn


def fn_silu_bias(v, b):
    return (_silu(v + b),)


def make_fn_gate_groupnorm(width):
    half = width // 2

    def fn(y_both, z, w):
        y = y_both * _silu(z)
        lane = lax.broadcasted_iota(jnp.int32, y.shape, 1)
        lo = lane < half
        sq = y * y
        s_lo = jnp.sum(jnp.where(lo, sq, 0.0), axis=-1, keepdims=True)
        s_hi = jnp.sum(jnp.where(lo, 0.0, sq), axis=-1, keepdims=True)
        r = jnp.where(lo, lax.rsqrt(s_lo / half + EPS), lax.rsqrt(s_hi / half + EPS))
        return (y * r * w,)
    return fn


def fn_glu(a, b):
    return (a * jax.nn.sigmoid(b),)


def fn_ln_silu(vw, vh, cb, lw, lb):
    v = jnp.concatenate([vw, vh], axis=-1) + cb
    mu = jnp.mean(v, axis=-1, keepdims=True)
    var = jnp.mean(jnp.square(v - mu), axis=-1, keepdims=True)
    return (_silu((v - mu) * lax.rsqrt(var + EPS) * lw + lb),)


def _col_tile(width):
    return width // 3 if width % (3 * LANES) == 0 else width


def mod_fwd(a_rows, w_shard, b_shard):
    n, d = a_rows.shape
    ws = w_shard.shape[1]
    tn = _col_tile(ws)

    def body(a_ref, w_ref, b_ref, o_ref):
        a = _silu(a_ref[...]).astype(BF16)
        o_ref[...] = jnp.dot(a, w_ref[...].astype(BF16), preferred_element_type=F32) + b_ref[...]

    return pl.pallas_call(
        body, name="mod_fwd", grid=(ws // tn,), out_shape=jax.ShapeDtypeStruct((n, ws), F32),
        in_specs=[pl.BlockSpec((n, d), lambda j: (0, 0)), pl.BlockSpec((d, tn), lambda j: (0, j)),
                  pl.BlockSpec((1, tn), lambda j: (0, j))],
        out_specs=pl.BlockSpec((n, tn), lambda j: (0, j)), compiler_params=_params(),
    )(a_rows, w_shard, b_shard)


def mod_bwd(a_rows, d_shard, d_full, w_shard, ctx_rows):
    n, d = a_rows.shape
    ws = w_shard.shape[1]
    tn = _col_tile(ws)
    n_ct = ws // tn

    def body(a_ref, ds_ref, df_ref, w_ref, gw_ref, gb_ref, q_ref):
        j = pl.program_id(0)
        a = _silu(a_ref[...])
        ds = ds_ref[...]
        gw_ref[...] = lax.dot_general(a, ds, _DIMS["tn"], precision=HI, preferred_element_type=F32)
        dctx = ds[ctx_rows[0]:ctx_rows[0] + 1, :]
        for r in ctx_rows[1:]:
            dctx = dctx + ds[r:r + 1, :]
        q = lax.dot_general(jnp.broadcast_to(dctx, (8, tn)), w_ref[...], _DIMS["nt"], precision=HI,
                            preferred_element_type=F32)

        @pl.when(j == 0)
        def _():
            q_ref[...] = q
            df = df_ref[...]
            acc = df[0:1, :]
            for r in range(1, n):
                acc = acc + df[r:r + 1, :]
            gb_ref[...] = acc

        @pl.when(j > 0)
        def _():
            q_ref[...] += q

    return pl.pallas_call(
        body, name="mod_bwd", grid=(n_ct,),
        out_shape=(jax.ShapeDtypeStruct((d, ws), F32), jax.ShapeDtypeStruct((1, d_full.shape[1]), F32),
                   jax.ShapeDtypeStruct((8, d), F32)),
        in_specs=[pl.BlockSpec((n, d), lambda j: (0, 0)), pl.BlockSpec((n, tn), lambda j: (0, j)),
                  pl.BlockSpec(d_full.shape, lambda j: (0, 0)), pl.BlockSpec((d, tn), lambda j: (0, j))],
        out_specs=(pl.BlockSpec((d, tn), lambda j: (0, j)), pl.BlockSpec((1, d_full.shape[1]), lambda j: (0, 0)),
                   pl.BlockSpec((8, d), lambda j: (0, 0))),
        compiler_params=_params(40 << 20),
    )(a_rows, d_shard, d_full, w_shard)


def _shifted(xs, d, tok, width):
    if d == 0:
        return xs
    n = xs.shape[0]
    sh = pltpu.roll(xs, (-d) % n, axis=0)
    return jnp.where((tok + d >= 0) & (tok + d < width), sh, 0.0)


def _placed(out_shape, place):
    if place is None:
        return out_shape, 0, 0, None
    return place


def tapsum_roll(name, x, xcb, w, wcb, *, seq_len, n_seq, row_blk_off, width, piece, cb, ncb, pad, flip, place=None,
                out_dtype=F32):
    n_tap = w.shape[0]
    n_piece = seq_len // piece
    out_shape, o_rb, o_cb, into = _placed((n_seq * seq_len, ncb * cb), place)

    def body(x_ref, w_ref, *rest):
        o_ref = rest[-1]
        wv = w_ref[...]
        tok = lax.broadcasted_iota(jnp.int32, (piece, 1), 0) % width

        def do_piece(p, carry):
            start = pl.multiple_of(p * piece, piece)
            xs = x_ref[pl.ds(start, piece), :].astype(F32)
            acc = jnp.zeros_like(xs)
            for k in range(n_tap):
                d = pad - k if flip else k - pad
                acc = acc + wv[k:k + 1, :] * _shifted(xs, d, tok, width)
            o_ref[pl.ds(start, piece), :] = acc.astype(o_ref.dtype)
            return carry

        lax.fori_loop(0, n_piece, do_piece, 0)

    extra = [] if into is None else [into]
    return pl.pallas_call(
        body, name=name, grid=(ncb, n_seq), out_shape=_big(out_shape, out_dtype),
        in_specs=[pl.BlockSpec((seq_len, cb), lambda j, s: (row_blk_off + s, xcb + j)),
                  pl.BlockSpec((n_tap, cb), lambda j, s: (0, wcb + j))] + [pl.BlockSpec(memory_space=pl.ANY)] * len(extra),
        out_specs=pl.BlockSpec((seq_len, cb), lambda j, s: (o_rb + s, o_cb + j)),
        input_output_aliases={2: 0} if extra else {},
        compiler_params=_params(8 * seq_len * cb * 4 + (8 << 20), 2),
    )(*_in_hbm([x, w] + extra))


def tapgrad_roll(name, dy, dycb, dy_blk_off, x, xcb, x_blk_off, *, n_tap, seq_len, n_seq, width, piece, cb, ncb, pad):
    n_piece = seq_len // piece

    def body(dy_ref, x_ref, o_ref):
        @pl.when(pl.program_id(1) == 0)
        def _():
            o_ref[...] = jnp.zeros_like(o_ref)

        tok = lax.broadcasted_iota(jnp.int32, (piece, 1), 0) % width

        def do_piece(p, carry):
            start = pl.multiple_of(p * piece, piece)
            xs = x_ref[pl.ds(start, piece), :].astype(F32)
            dv = dy_ref[pl.ds(start, piece), :]
            for k in range(n_tap):
                o_ref[k:k + 1, :] += jnp.sum(dv * _shifted(xs, k - pad, tok, width), axis=0, keepdims=True)
            return carry

        lax.fori_loop(0, n_piece, do_piece, 0)

    return pl.pallas_call(
        body, name=name, grid=(ncb, n_seq), out_shape=jax.ShapeDtypeStruct((n_tap, ncb * cb), F32),
        in_specs=[pl.BlockSpec((seq_len, cb), lambda j, s: (dy_blk_off + s, dycb + j)),
                  pl.BlockSpec((seq_len, cb), lambda j, s: (x_blk_off + s, xcb + j))],
        out_specs=pl.BlockSpec((n_tap, cb), lambda j, s: (0, j)),
        compiler_params=_params(8 * seq_len * cb * 4 + (8 << 20), 2),
    )(*_in_hbm([dy, x]))


def tapsum_rows(name, x, xcb, w, wcb, *, seq_len, n_seq, cb, ncb, pad, flip, place=None):
    n_tap = w.shape[0]
    n_row = seq_len // GRID_W
    halo = pad * GRID_W
    out_shape, o_rb, o_cb, into = _placed((n_seq * seq_len, ncb * cb), place)

    def body(x_ref, w_ref, *rest):
        o_ref, xp = rest[-2:]
        xp[pl.ds(0, halo), :] = jnp.zeros((halo, cb), F32)
        xp[pl.ds(halo + seq_len, halo), :] = jnp.zeros((halo, cb), F32)
        xp[pl.ds(halo, seq_len), :] = x_ref[...].astype(F32)
        wv = w_ref[...]

        def do_row(r, carry):
            acc = jnp.zeros((GRID_W, cb), F32)
            for k in range(n_tap):
                d = pad - k if flip else k - pad
                acc = acc + wv[k:k + 1, :] * xp[pl.ds(pl.multiple_of((r + pad + d) * GRID_W, GRID_W), GRID_W), :]
            o_ref[pl.ds(pl.multiple_of(r * GRID_W, GRID_W), GRID_W), :] = acc
            return carry

        lax.fori_loop(0, n_row, do_row, 0)

    extra = [] if into is None else [into]
    return pl.pallas_call(
        body, name=name, grid=(ncb, n_seq), out_shape=_big(out_shape, F32),
        in_specs=[pl.BlockSpec((seq_len, cb), lambda j, s: (s, xcb + j)),
                  pl.BlockSpec((n_tap, cb), lambda j, s: (0, wcb + j))] + [pl.BlockSpec(memory_space=pl.ANY)] * len(extra),
        out_specs=pl.BlockSpec((seq_len, cb), lambda j, s: (o_rb + s, o_cb + j)),
        input_output_aliases={2: 0} if extra else {},
        scratch_shapes=[pltpu.VMEM((seq_len + 2 * halo, cb), F32)],
        compiler_params=_params(10 * seq_len * cb * 4 + (8 << 20), 2),
    )(*_in_hbm([x, w] + extra))


def tapgrad_rows(name, dy, dycb, x, xcb, *, n_tap, seq_len, n_seq, cb, ncb, pad):
    n_row = seq_len // GRID_W
    halo = pad * GRID_W

    def body(dy_ref, x_ref, o_ref, xp):
        @pl.when(pl.program_id(1) == 0)
        def _():
            o_ref[...] = jnp.zeros_like(o_ref)

        xp[pl.ds(0, halo), :] = jnp.zeros((halo, cb), F32)
        xp[pl.ds(halo + seq_len, halo), :] = jnp.zeros((halo, cb), F32)
        xp[pl.ds(halo, seq_len), :] = x_ref[...].astype(F32)

        def do_row(r, carry):
            dv = dy_ref[pl.ds(pl.multiple_of(r * GRID_W, GRID_W), GRID_W), :]
            for k in range(n_tap):
                xs = xp[pl.ds(pl.multiple_of((r + k) * GRID_W, GRID_W), GRID_W), :]
                o_ref[k:k + 1, :] += jnp.sum(dv * xs, axis=0, keepdims=True)
            return carry

        lax.fori_loop(0, n_row, do_row, 0)

    return pl.pallas_call(
        body, name=name, grid=(ncb, n_seq), out_shape=jax.ShapeDtypeStruct((n_tap, ncb * cb), F32),
        in_specs=[pl.BlockSpec((seq_len, cb), lambda j, s: (s, dycb + j)),
                  pl.BlockSpec((seq_len, cb), lambda j, s: (s, xcb + j))],
        out_specs=pl.BlockSpec((n_tap, cb), lambda j, s: (0, j)),
        scratch_shapes=[pltpu.VMEM((seq_len + 2 * halo, cb), F32)],
        compiler_params=_params(10 * seq_len * cb * 4 + (8 << 20), 2),
    )(*_in_hbm([dy, x]))


def _ssd_blocks(b, s, *, rev, n_ctx, n_lat, lat_blocks):
    if rev:
        return jnp.where(s < n_ctx, lat_blocks + b * n_ctx + (n_ctx - 1 - s), b * n_lat + (n_lat - 1 - (s - n_ctx)))
    return jnp.where(s < n_ctx, lat_blocks + b * n_ctx + s, b * n_lat + (s - n_ctx))


def _ssd_common(xbc, raw, dtb, alog, dsk, *, rev, ds, n_head):
    if rev:
        raw = pltpu.roll(raw, LANES - n_head, axis=1)
    pre = raw + dtb
    dt = jnp.maximum(pre, 0.0) + jnp.log(1.0 + jnp.exp(-jnp.abs(pre)))
    sig = jax.nn.sigmoid(pre)
    a = -jnp.exp(alog)
    da = dt * a
    ri = lax.broadcasted_iota(jnp.int32, (CHUNK, CHUNK), 0)
    ci = lax.broadcasted_iota(jnp.int32, (CHUNK, CHUNK), 1)
    mask = (ci >= ri) if rev else (ci <= ri)
    tri = mask.astype(F32)
    tri_t = ((ci <= ri) if rev else (ci >= ri)).astype(F32)
    cs = jnp.dot(tri, da, precision=HI, preferred_element_type=F32)
    tot = jnp.sum(da, axis=0, keepdims=True)
    def wide(v):
        first = lax.broadcasted_iota(jnp.int32, (v.shape[0], LANES), 1) < HEAD_DIM
        return jnp.concatenate(
            [jnp.where(first, jnp.broadcast_to(v[:, 2 * p:2 * p + 1], first.shape),
                       jnp.broadcast_to(v[:, 2 * p + 1:2 * p + 2], first.shape)) for p in range(n_head // 2)], axis=1)

    cs_w, tot_w = wide(cs), wide(tot)
    xh = xbc[:, :ds]
    dt_w = wide(dt)
    return dict(
        dt=dt, sig=sig, a=a, cs=cs, cs_t=cs.T, tot=tot, mask=mask, tri_t=tri_t,
        e_w=jnp.exp(cs_w), wt_w=jnp.exp(tot_w - cs_w), dec_w=jnp.exp(tot_w), dt_w=dt_w, dsk_w=wide(dsk),
        xh=xh, xs_w=xh * dt_w, bm=xbc[:, ds:ds + 2 * N_STATE], cm=xbc[:, ds + 2 * N_STATE:ds + 4 * N_STATE])


def _decay(q, col):
    seg = q["cs"][:, col:col + 1] - q["cs_t"][col:col + 1, :]
    return jnp.exp(jnp.where(q["mask"], seg, -jnp.inf))


def _split_heads(v):
    lane = lax.broadcasted_iota(jnp.int32, v.shape, 1)
    return jnp.concatenate([jnp.where(lane < HEAD_DIM, v, 0.0), jnp.where(lane >= HEAD_DIM, v, 0.0)], axis=0)


def ssd_fwd(name, xbc, dt_raw, dt_cb, dtb, alog, dsk, *, rev, n_ex, seq_len, ctx_len, ds, rider=None, add=None):
    n_head, half = ds // HEAD_DIM, ds // 2
    n_ctx, n_lat = ctx_len // CHUNK, seq_len // CHUNK
    n_step = n_ctx + n_lat
    blk = functools.partial(_ssd_blocks, rev=rev, n_ctx=n_ctx, n_lat=n_lat, lat_blocks=n_ex * n_lat)
    xw = xbc.shape[1]

    def y_blk(b, s):
        sl = jnp.maximum(s, n_ctx) - n_ctx
        return b * n_lat + ((n_lat - 1 - sl) if rev else sl)

    hosted = _Hosted(rider, 5 + (add is not None), 2, 1, (n_ex, n_step))

    def body(*refs):
        (xbc_ref, dt_ref, dtb_ref, alog_ref, dsk_ref, *add_ref), (y_ref, hs_ref), (h_scr,) = hosted.split(refs)

        @pl.when(pl.program_id(1) == 0)
        def _():
            h_scr[...] = jnp.zeros_like(h_scr)

        q = _ssd_common(xbc_ref[...], dt_ref[...], dtb_ref[...], alog_ref[...], dsk_ref[...], rev=rev, ds=ds, n_head=n_head)
        h = h_scr[...]
        hs_ref[...] = h
        for g in range(2):
            lo = g * half
            bg = q["bm"][:, g * N_STATE:(g + 1) * N_STATE].astype(BF16)
            cg = q["cm"][:, g * N_STATE:(g + 1) * N_STATE].astype(BF16)
            scores = lax.dot_general(cg, bg, _DIMS["nt"], preferred_element_type=F32)
            hg = h[:, lo:lo + half]
            off = jnp.dot(cg, hg.astype(BF16), preferred_element_type=F32)
            for j in range(half // LANES):
                c0 = (lo + j * LANES) // HEAD_DIM
                ln = slice(lo + j * LANES, lo + (j + 1) * LANES)
                p_cat = jnp.concatenate([scores * _decay(q, c0), scores * _decay(q, c0 + 1)], axis=1).astype(BF16)
                diag = jnp.dot(p_cat, _split_heads(q["xs_w"][:, ln]).astype(BF16), preferred_element_type=F32)
                y_ref[:, ln] = (diag + q["e_w"][:, ln] * off[:, j * LANES:(j + 1) * LANES]
                                + q["dsk_w"][:, ln] * q["xh"][:, ln] + (add_ref[0][:, ln] if add_ref else 0.0))
            v = (q["wt_w"][:, lo:lo + half] * q["xs_w"][:, lo:lo + half]).astype(BF16)
            h_scr[:, lo:lo + half] = (q["dec_w"][:, lo:lo + half] * hg
                                      + lax.dot_general(bg, v, _DIMS["tn"], preferred_element_type=F32))
        hosted.finish()

    vec = pl.BlockSpec((1, LANES), lambda b, s: (0, 0))
    in_specs, out_shape, out_specs, scratch, args = hosted.call_args(
        [pl.BlockSpec((CHUNK, xw), lambda b, s: (blk(b, s), 0)),
         pl.BlockSpec((CHUNK, LANES), lambda b, s: (blk(b, s), dt_cb)), vec, vec, vec]
        + [pl.BlockSpec((CHUNK, ds), lambda b, s: (y_blk(b, s), 0))] * (add is not None),
        (_big((n_ex * seq_len, ds), F32), _big((n_ex, n_step, N_STATE, ds), F32)),
        (pl.BlockSpec((CHUNK, ds), lambda b, s: (y_blk(b, s), 0)),
         pl.BlockSpec((None, None, N_STATE, ds), lambda b, s: (b, s, 0, 0))),
        [pltpu.VMEM((N_STATE, ds), F32)], [xbc, dt_raw, dtb, alog, dsk] + ([] if add is None else [add]))
    return hosted.results(pl.pallas_call(
        body, name=name, grid=(n_ex, n_step), out_shape=out_shape, in_specs=in_specs, out_specs=out_specs,
        scratch_shapes=scratch, compiler_params=_params(40 << 20, 2),
    )(*_in_hbm(args)))


def ssd_bwd(name, xbc, dt_raw, dt_cb, hs, dy, dtb, alog, dsk, *, rev, n_ex, seq_len, ctx_len, ds, rider=None, add=None):
    n_head, half = ds // HEAD_DIM, ds // 2
    n_ctx, n_lat = ctx_len // CHUNK, seq_len // CHUNK
    n_step = n_ctx + n_lat
    n_tok = n_ex * (seq_len + ctx_len)
    blk0 = functools.partial(_ssd_blocks, rev=rev, n_ctx=n_ctx, n_lat=n_lat, lat_blocks=n_ex * n_lat)
    step = lambda sp: n_step - 1 - sp
    blk = lambda b, sp: blk0(b, step(sp))
    xw = xbc.shape[1]

    def dy_blk(b, sp):
        sl = jnp.maximum(step(sp), n_ctx) - n_ctx
        return b * n_lat + ((n_lat - 1 - sl) if rev else sl)

    hosted = _Hosted(rider, 7 + (add is not None), 5, 1, (n_ex, n_step))

    def body(*refs):
        ((xbc_ref, dt_ref, hs_ref, dy_ref, dtb_ref, alog_ref, dsk_ref, *add_ref),
         (dxbc_ref, ddt_ref, dalog_ref, ddtb_ref, ddsk_ref), (dh_scr,)) = hosted.split(refs)
        b, sp = pl.program_id(0), pl.program_id(1)
        more = (lambda cols: add_ref[0][:, cols]) if add_ref else (lambda cols: 0.0)

        @pl.when(sp == 0)
        def _():
            dh_scr[...] = jnp.zeros_like(dh_scr)

        @pl.when((sp == 0) & (b == 0))
        def _():
            dalog_ref[...] = jnp.zeros_like(dalog_ref)
            ddtb_ref[...] = jnp.zeros_like(ddtb_ref)
            ddsk_ref[...] = jnp.zeros_like(ddsk_ref)

        q = _ssd_common(xbc_ref[...], dt_ref[...], dtb_ref[...], alog_ref[...], dsk_ref[...], rev=rev, ds=ds, n_head=n_head)
        h = hs_ref[...]
        d_y = jnp.where(step(sp) >= n_ctx, dy_ref[...], 0.0)
        dh_next = dh_scr[...]
        lane_row = lax.broadcasted_iota(jnp.int32, (1, LANES), 1)
        d_cs = jnp.zeros((CHUNK, LANES), F32)
        dxs_parts, de_parts, dwt_parts, ddec_parts = [], [], [], []
        for g in range(2):
            lo = g * half
            gs = slice(lo, lo + half)
            bg = q["bm"][:, g * N_STATE:(g + 1) * N_STATE].astype(BF16)
            cg = q["cm"][:, g * N_STATE:(g + 1) * N_STATE].astype(BF16)
            scores = lax.dot_general(cg, bg, _DIMS["nt"], preferred_element_type=F32)
            hg, dyg, dhn = h[:, gs], d_y[:, gs], dh_next[:, gs]
            off = jnp.dot(cg, hg.astype(BF16), preferred_element_type=F32)
            d_off = (q["e_w"][:, gs] * dyg).astype(BF16)
            de_parts.append(dyg * off)
            d_c = lax.dot_general(d_off, hg.astype(BF16), _DIMS["nt"], preferred_element_type=F32)
            dh_scr[:, gs] = (lax.dot_general(cg, d_off, _DIMS["tn"], preferred_element_type=F32)
                             + q["dec_w"][:, gs] * dhn)
            b_dh = jnp.dot(bg, dhn.astype(BF16), preferred_element_type=F32)
            v = q["wt_w"][:, gs] * q["xs_w"][:, gs]
            d_b = lax.dot_general(v.astype(BF16), dhn.astype(BF16), _DIMS["nt"], preferred_element_type=F32)
            dwt_parts.append(q["xs_w"][:, gs] * b_dh)
            ddec_parts.append(jnp.sum(hg * dhn, axis=0, keepdims=True))
            d_scores = jnp.zeros((CHUNK, CHUNK), F32)
            for j in range(half // LANES):
                c0 = (lo + j * LANES) // HEAD_DIM
                ln = slice(lo + j * LANES, lo + (j + 1) * LANES)
                l0, l1 = _decay(q, c0), _decay(q, c0 + 1)
                p0, p1 = scores * l0, scores * l1
                dy_st = _split_heads(d_y[:, ln]).astype(BF16)
                d_p = lax.dot_general(dy_st, q["xs_w"][:, ln].astype(BF16), _DIMS["nt"], preferred_element_type=F32)
                d_p0, d_p1 = d_p[:CHUNK], d_p[CHUNK:]
                d_scores = d_scores + d_p0 * l0 + d_p1 * l1
                for col, t in ((c0, d_p0 * p0), (c0 + 1, d_p1 * p1)):
                    d_cs = d_cs + jnp.sum(t - t.T, axis=1, keepdims=True) * (lane_row == col).astype(F32)
                p_st = jnp.concatenate([p0, p1], axis=0).astype(BF16)
                dxs_parts.append(lax.dot_general(p_st, dy_st, _DIMS["tn"], preferred_element_type=F32)
                                 + q["wt_w"][:, ln] * b_dh[:, j * LANES:(j + 1) * LANES])
            d_sc = d_scores.astype(BF16)
            d_c = d_c + jnp.dot(d_sc, bg, preferred_element_type=F32)
            d_b = d_b + lax.dot_general(d_sc, cg, _DIMS["tn"], preferred_element_type=F32)
            b_cols, c_cols = slice(ds + g * N_STATE, ds + (g + 1) * N_STATE), slice(ds + (2 + g) * N_STATE, ds + (3 + g) * N_STATE)
            dxbc_ref[:, b_cols] = d_b + more(b_cols)
            dxbc_ref[:, c_cols] = d_c + more(c_cols)
        d_xs = jnp.concatenate(dxs_parts, axis=1)
        narrow_m = (lax.broadcasted_iota(jnp.int32, (ds, LANES), 0) // HEAD_DIM
                    == lax.broadcasted_iota(jnp.int32, (ds, LANES), 1)).astype(BF16)
        rows8 = lambda v: jnp.broadcast_to(v, (8, ds))
        stacked = jnp.concatenate(
            [jnp.concatenate(dwt_parts, axis=1), jnp.concatenate(de_parts, axis=1), d_xs * q["xh"],
             rows8(jnp.concatenate(ddec_parts, axis=1)), rows8(jnp.sum(d_y * q["xh"], axis=0, keepdims=True))], axis=0)
        sums = jnp.dot(stacked.astype(BF16), narrow_m, preferred_element_type=F32)
        n_wt, n_e, n_xs = sums[:CHUNK], sums[CHUNK:2 * CHUNK], sums[2 * CHUNK:3 * CHUNK]
        n_dec, n_dsk = sums[3 * CHUNK:3 * CHUNK + 1], sums[3 * CHUNK + 8:3 * CHUNK + 9]
        e, wt, dec = jnp.exp(q["cs"]), jnp.exp(q["tot"] - q["cs"]), jnp.exp(q["tot"])
        d_wt = n_wt * wt
        d_cs = d_cs + n_e * e - d_wt
        d_tot = jnp.sum(d_wt, axis=0, keepdims=True) + n_dec * dec
        d_da = jnp.dot(q["tri_t"], d_cs, precision=HI, preferred_element_type=F32) + d_tot
        d_dt = d_da * q["a"] + n_xs
        dxbc_ref[:, :ds] = d_xs * q["dt_w"] + q["dsk_w"] * d_y + more(slice(0, ds))
        dalog_ref[...] += jnp.sum(d_da * q["dt"], axis=0, keepdims=True) * q["a"]
        d_raw = d_dt * q["sig"]
        ddtb_ref[...] += jnp.sum(d_raw, axis=0, keepdims=True)
        ddsk_ref[...] += n_dsk
        ddt_ref[...] = pltpu.roll(d_raw, n_head, axis=1) if rev else d_raw
        hosted.finish()

    vec = pl.BlockSpec((1, LANES), lambda b, s: (0, 0))
    vec_shape = jax.ShapeDtypeStruct((1, LANES), F32)
    in_specs, out_shape, out_specs, scratch, args = hosted.call_args(
        [pl.BlockSpec((CHUNK, xw), lambda b, s: (blk(b, s), 0)),
         pl.BlockSpec((CHUNK, LANES), lambda b, s: (blk(b, s), dt_cb)),
         pl.BlockSpec((None, None, N_STATE, ds), lambda b, s: (b, step(s), 0, 0)),
         pl.BlockSpec((CHUNK, ds), lambda b, s: (dy_blk(b, s), 0)), vec, vec, vec]
        + [pl.BlockSpec((CHUNK, xw), lambda b, s: (blk(b, s), 0))] * (add is not None),
        (_big((n_tok, xw), F32), _big((n_tok, LANES), F32), vec_shape, vec_shape, vec_shape),
        (pl.BlockSpec((CHUNK, xw), lambda b, s: (blk(b, s), 0)),
         pl.BlockSpec((CHUNK, LANES), lambda b, s: (blk(b, s), 0)), vec, vec, vec),
        [pltpu.VMEM((N_STATE, ds), F32)], [xbc, dt_raw, hs, dy, dtb, alog, dsk] + ([] if add is None else [add]))
    return hosted.results(pl.pallas_call(
        body, name=name, grid=(n_ex, n_step), out_shape=out_shape, in_specs=in_specs, out_specs=out_specs,
        scratch_shapes=scratch, compiler_params=_params(48 << 20, 2),
    )(*_in_hbm(args)))


def final_loss(x2, f, gate, target, w, *, tm, seg_fn):
    n, d = x2.shape

    def body(x_ref, f_ref, g_ref, t_ref, w_ref, dx_ref, dw_ref, loss_ref):
        i = pl.program_id(0)
        t = t_ref[...]
        x3 = make_fn_resid(0.5)(x_ref[...], f_ref[...], g_ref[...])[0]

        def per_feature(xv, wv):
            err = _rms(xv, wv) - t
            return 0.5 * jnp.sum(err * err, axis=0, keepdims=True) / d

        lv, vjp = jax.vjp(per_feature, x3, w_ref[...])
        dx, dw = vjp(jnp.ones_like(lv))
        dx_ref[...] = dx

        @pl.when(i == 0)
        def _():
            dw_ref[...] = dw
            loss_ref[...] = lv

        @pl.when(i > 0)
        def _():
            dw_ref[...] += dw
            loss_ref[...] += lv

    tile = pl.BlockSpec((tm, d), lambda i: (i, 0))
    vec = pl.BlockSpec((1, d), lambda i: (0, 0))
    return pl.pallas_call(
        body, name="final_loss", grid=(n // tm,),
        in_specs=[tile, tile, pl.BlockSpec((None, 1, d), lambda i: (seg_fn(i), 0, 0)), tile, vec],
        out_shape=(jax.ShapeDtypeStruct((n, d), F32), jax.ShapeDtypeStruct((1, d), F32), jax.ShapeDtypeStruct((1, d), F32)),
        out_specs=(tile, vec, vec), compiler_params=_params(tm * d * 4 * 20 + (8 << 20)),
    )(x2, f, gate, target, w)


def sum_slots(name, arr, out_dtype=F32):
    n_slot, n_row, width = arr.shape
    tm = _row_tile(n_row, width * n_slot, cap_bytes=STREAM_TILE_BYTES * 14, mult=16)
    vmem = 2 * n_slot * tm * width * arr.dtype.itemsize + 4 * tm * width * 4

    def body(a_ref, o_ref):
        acc = a_ref[0].astype(F32)
        for j in range(1, n_slot):
            acc = acc + a_ref[j].astype(F32)
        o_ref[...] = acc.astype(o_ref.dtype)

    return pl.pallas_call(
        body, name=name, grid=(n_row // tm,), out_shape=jax.ShapeDtypeStruct((n_row, width), out_dtype),
        in_specs=[pl.BlockSpec((n_slot, tm, width), lambda i: (0, i, 0))],
        out_specs=pl.BlockSpec((tm, width), lambda i: (i, 0)), compiler_params=_params(vmem + (4 << 20)),
    )(arr)


def adamw(name, w, g_slots, m, v):
    n_slot, n_row, width = g_slots.shape
    tm = _row_tile(n_row, width, cap_bytes=STREAM_TILE_BYTES)
    if g_slots.dtype == BF16 and tm % 16:
        tm16 = _row_tile(n_row, width, cap_bytes=STREAM_TILE_BYTES, mult=16)
        if tm16 % 16 == 0:
            tm = tm16
        else:
            g_slots = g_slots.astype(F32)

    def body(w_ref, g_ref, m_ref, v_ref, go_ref, d_ref, mo_ref, vo_ref):
        g = g_ref[0].astype(F32)
        for j in range(1, n_slot):
            g = g + g_ref[j].astype(F32)
        m2 = ADAM_B1 * m_ref[...] + (1.0 - ADAM_B1) * g
        v2 = ADAM_B2 * v_ref[...] + (1.0 - ADAM_B2) * jnp.square(g)
        m_hat = m2 / (1.0 - ADAM_B1 ** ADAM_STEP)
        v_hat = v2 / (1.0 - ADAM_B2 ** ADAM_STEP)
        go_ref[...] = g
        d_ref[...] = -ADAM_LR * (m_hat / (jnp.sqrt(v_hat) + ADAM_EPS) + ADAM_WD * w_ref[...])
        mo_ref[...] = m2
        vo_ref[...] = v2

    tile = pl.BlockSpec((tm, width), lambda i: (i, 0))
    shape = jax.ShapeDtypeStruct((n_row, width), F32)
    return pl.pallas_call(
        body, name=name, grid=(n_row // tm,), out_shape=(shape,) * 4,
        in_specs=[tile, pl.BlockSpec((n_slot, tm, width), lambda i: (0, i, 0)), tile, tile],
        out_specs=(tile,) * 4, compiler_params=_params(2 * (7 + n_slot) * tm * width * 4 + (4 << 20)),
    )(w, g_slots, m, v)


def cctx_grad(q_all, c_ctx_row):
    d = c_ctx_row.shape[1]

    def body(q_ref, c_ref, o_ref):
        acc = q_ref[0, 0:1, :]
        for j in (2, 4, 6):
            acc = acc + q_ref[j, 0:1, :]
        _, vjp = jax.vjp(_silu, c_ref[...])
        o_ref[...] = vjp(acc)[0]

    return pl.pallas_call(
        body, name="cctx_grad", out_shape=jax.ShapeDtypeStruct((1, d), F32),
    )(q_all, c_ctx_row)


def loss_total(pack_sum, d):
    def body(p_ref, o_ref):
        o_ref[...] = jnp.sum(p_ref[:, 0:d], axis=1, keepdims=True)

    return pl.pallas_call(
        body, name="loss_total", out_shape=jax.ShapeDtypeStruct((1, 1), F32),
    )(pack_sum)


class _Plan:
    def __init__(self):
        self.builders, self.got = {}, {}

    def on(self, host, key, builder):
        self.builders.setdefault(host, []).append((key, builder))

    def run(self, host, fn, *args, **kw):
        if host not in self.builders:
            return fn(host, *args, **kw)
        keys, riders = zip(*[(key, builder(self)) for key, builder in self.builders[host]])
        res, landed = fn(host, *args, rider=Riders(riders), **kw)
        for key, r in zip(keys, riders):
            self.got[key], landed = landed[:r.n], landed[r.n:]
        return res


def _val(w):
    return w() if callable(w) else w


def _matmul_tile(n_rows, tm):
    return 2 * tm if n_rows % (2 * tm) == 0 else tm


def _ffn_fwd(plan, tag, xin, n_rows, tm, seg_fn, shift, scale, gate, norm_w, wg, wu, wd, fuse_gate_up=False,
             with_resid=True):
    d = xin[1]
    n_tiles = n_rows // tm
    (h,) = plan.run(f"{tag}_norm", rowwise, fn_norm_mod, [xin], [shift, scale], [norm_w], [(n_rows, d, BF16)],
                    tm=tm, n_tiles=n_tiles, seg_fn=seg_fn)
    tmm = _matmul_tile(n_rows, tm)
    if fuse_gate_up:
        g, u, act = plan.run(f"{tag}_gate_up", matmul, [(h, _val(wg)), (h, _val(wu))], "nn", b_ch=True, out_ch=True,
                             tm=min(tm, 256), fold=True,
                             post=([], lambda ag, au: (ag, au, fn_act(ag, au)[0]), [BF16, BF16, BF16]))
    else:
        g = plan.run(f"{tag}_gate", matmul, [(h, _val(wg))], "nn", out_dtype=BF16, b_ch=True, out_ch=True, tm=tmm)
        u, act = plan.run(f"{tag}_up", matmul, [(h, _val(wu))], "nn", b_ch=True, out_ch=True, tm=tm, fold=True,
                          post=([g], lambda acc, gv: (acc, fn_act(gv, acc)[0]), [BF16, BF16]))
    f = plan.run(f"{tag}_down", matmul, [(act, _val(wd))], "nn", a_ch=True, b_ch=True, tm=tmm, fold=True)
    if not with_resid:
        return None, (h, g, u, act, f)
    (xo,) = plan.run(f"{tag}_resid", rowwise, make_fn_resid(0.5), [xin, row(f)], [gate], [], [(n_rows, d, F32)],
                     tm=tm, n_tiles=n_tiles, seg_fn=seg_fn)
    return xo, (h, g, u, act, f)


def _ffn_bwd(plan, tag, d_xo, saved, xin, n_rows, tm, seg_fn, first_fn, shift, scale, gate, norm_w, wg, wu, wd, dx_rows, dx_limit):
    h, g, u, act, f = saved
    d = xin[1]
    n_tiles = n_rows // tm
    n_ch, _, n_hid = g.shape
    d_f, d_gate = plan.run(f"{tag}_resid_bwd", rowwise_bwd, make_fn_resid(0.5), [xin, row(f)], [gate], [], [[row(d_xo)]],
                           [None, (n_rows, BF16, None)], tm=tm, n_tiles=n_tiles, seg_fn=seg_fn, first_fn=first_fn)
    tmm = _matmul_tile(n_rows, tm)
    def act_vjp(d_act, gv, uv):
        s = jax.nn.sigmoid(gv)
        gs = gv * s
        return d_act * uv * (s + gs * (1.0 - s)), d_act * gs
    d_g, d_u = plan.run(f"{tag}_down_dx", matmul, [(d_f, wd)], "nt", b_ch=True, out_ch=True, tm=tmm,
                        post=([g, u], act_vjp, [BF16, BF16]))
    plan.got[f"{tag}_d_wd"] = plan.run(f"{tag}_down_dw", matmul, [(act, d_f)], "tn", out_dtype=BF16, a_ch=True, out_ch=True, tm=tmm)
    d_h = plan.run(f"{tag}_up_dx", matmul, [(d_g, wg), (d_u, wu)], "nt", a_ch=True, b_ch=True, tm=tmm)
    plan.got[f"{tag}_d_wg"] = plan.run(f"{tag}_gate_dw", matmul, [(d_g, h)], "tn", out_dtype=BF16, a_ch=True, out_ch=True, tm=tmm)
    plan.got[f"{tag}_d_wu"] = plan.run(f"{tag}_up_dw", matmul, [(d_u, h)], "tn", out_dtype=BF16, a_ch=True, out_ch=True, tm=tmm)
    d_x, d_shift, d_scale, d_nw = plan.run(
        f"{tag}_norm_bwd", rowwise_bwd, fn_norm_mod, [xin], [shift, scale], [norm_w], [[row(d_h)]], [(dx_rows, F32, dx_limit)],
        tm=tm, n_tiles=n_tiles, seg_fn=seg_fn, first_fn=first_fn, adds={0: (row(d_xo), None)})
    return d_x, (d_shift, d_scale, d_gate), d_nw


def kernel(x, c, ctx, c_ctx, w_mod, b_mod, norm_ffn1, ffn1_gate, ffn1_up, ffn1_down, norm_mix, w_in, ssm_conv_w, ssm_conv_b, dt_bias_fwd, dt_bias_bwd, a_log_fwd, a_log_bwd, ssm_d, ssm_norm_w, cconv_w, cconv_b, cconv_ln_w, cconv_ln_b, w_out, norm_ffn2, ffn2_gate, ffn2_up, ffn2_down, final_norm, loss_target, m_c_ctx, m_w_mod, m_b_mod, m_norm_ffn1, m_ffn1_gate, m_ffn1_up, m_ffn1_down, m_norm_mix, m_w_in, m_ssm_conv_w, m_ssm_conv_b, m_dt_bias_fwd, m_dt_bias_bwd, m_a_log_fwd, m_a_log_bwd, m_ssm_d, m_ssm_norm_w, m_cconv_w, m_cconv_b, m_cconv_ln_w, m_cconv_ln_b, m_w_out, m_norm_ffn2, m_ffn2_gate, m_ffn2_up, m_ffn2_down, m_final_norm, v_c_ctx, v_w_mod, v_b_mod, v_norm_ffn1, v_ffn1_gate, v_ffn1_up, v_ffn1_down, v_norm_mix, v_w_in, v_ssm_conv_w, v_ssm_conv_b, v_dt_bias_fwd, v_dt_bias_bwd, v_a_log_fwd, v_a_log_bwd, v_ssm_d, v_ssm_norm_w, v_cconv_w, v_cconv_b, v_cconv_ln_w, v_cconv_ln_b, v_w_out, v_norm_ffn2, v_ffn2_gate, v_ffn2_up, v_ffn2_down, v_final_norm):
    weights = dict(c_ctx=c_ctx, w_mod=w_mod, b_mod=b_mod, norm_ffn1=norm_ffn1, ffn1_gate=ffn1_gate, ffn1_up=ffn1_up, ffn1_down=ffn1_down, norm_mix=norm_mix, w_in=w_in, ssm_conv_w=ssm_conv_w, ssm_conv_b=ssm_conv_b, dt_bias_fwd=dt_bias_fwd, dt_bias_bwd=dt_bias_bwd, a_log_fwd=a_log_fwd, a_log_bwd=a_log_bwd, ssm_d=ssm_d, ssm_norm_w=ssm_norm_w, cconv_w=cconv_w, cconv_b=cconv_b, cconv_ln_w=cconv_ln_w, cconv_ln_b=cconv_ln_b, w_out=w_out, norm_ffn2=norm_ffn2, ffn2_gate=ffn2_gate, ffn2_up=ffn2_up, ffn2_down=ffn2_down, final_norm=final_norm)
    mom1 = dict(c_ctx=m_c_ctx, w_mod=m_w_mod, b_mod=m_b_mod, norm_ffn1=m_norm_ffn1, ffn1_gate=m_ffn1_gate, ffn1_up=m_ffn1_up, ffn1_down=m_ffn1_down, norm_mix=m_norm_mix, w_in=m_w_in, ssm_conv_w=m_ssm_conv_w, ssm_conv_b=m_ssm_conv_b, dt_bias_fwd=m_dt_bias_fwd, dt_bias_bwd=m_dt_bias_bwd, a_log_fwd=m_a_log_fwd, a_log_bwd=m_a_log_bwd, ssm_d=m_ssm_d, ssm_norm_w=m_ssm_norm_w, cconv_w=m_cconv_w, cconv_b=m_cconv_b, cconv_ln_w=m_cconv_ln_w, cconv_ln_b=m_cconv_ln_b, w_out=m_w_out, norm_ffn2=m_norm_ffn2, ffn2_gate=m_ffn2_gate, ffn2_up=m_ffn2_up, ffn2_down=m_ffn2_down, final_norm=m_final_norm)
    mom2 = dict(c_ctx=v_c_ctx, w_mod=v_w_mod, b_mod=v_b_mod, norm_ffn1=v_norm_ffn1, ffn1_gate=v_ffn1_gate, ffn1_up=v_ffn1_up, ffn1_down=v_ffn1_down, norm_mix=v_norm_mix, w_in=v_w_in, ssm_conv_w=v_ssm_conv_w, ssm_conv_b=v_ssm_conv_b, dt_bias_fwd=v_dt_bias_fwd, dt_bias_bwd=v_dt_bias_bwd, a_log_fwd=v_a_log_fwd, a_log_bwd=v_a_log_bwd, ssm_d=v_ssm_d, ssm_norm_w=v_ssm_norm_w, cconv_w=v_cconv_w, cconv_b=v_cconv_b, cconv_ln_w=v_cconv_ln_w, cconv_ln_b=v_cconv_ln_b, w_out=v_w_out, norm_ffn2=v_norm_ffn2, ffn2_gate=v_ffn2_gate, ffn2_up=v_ffn2_up, ffn2_down=v_ffn2_down, final_norm=v_final_norm)
    order = list(weights)

    n_ex, seq_len, d = x.shape
    ctx_len = ctx.shape[1]
    ds = d
    n_head = ds // HEAD_DIM
    xw = ds + 4 * N_STATE
    n_lat, n_ctx_rows = n_ex * seq_len, n_ex * ctx_len
    n_tok = n_lat + n_ctx_rows
    tm = math.gcd(math.gcd(512, seq_len), n_ctx_rows)
    seg_all, first_all = _segmenter(tm, seq_len, n_lat)
    lat_tiles = n_lat // tm

    xi, yi, ci = lax.axis_index("x"), lax.axis_index("y"), lax.axis_index("c")
    me, chip = 4 * xi + 2 * yi + ci, 2 * xi + yi

    (c_all,) = exchange("gather_c", [c], "all8")
    n_all = 8 * n_ex
    n_cond = -(-(n_all + 1) // 8) * 8
    cond = jnp.concatenate([c_all.reshape(n_all, d), c_ctx[None, :], jnp.zeros((n_cond - n_all - 1, d), F32)])
    mod_w = w_mod.shape[2]
    b_shard = lax.dynamic_slice(b_mod, (0, chip * mod_w), (1, mod_w))
    (mod_g,) = exchange("gather_mod", [mod_fwd(cond, w_mod[0], b_shard)], "chips")
    mod_full = mod_g.transpose(1, 0, 2).reshape(n_cond, N_CHIPS * mod_w)
    mod_mine = lax.dynamic_slice(mod_full, (me * n_ex, 0), (n_ex, 9 * d)).reshape(n_ex, 9, d)
    mod_ctx = mod_full[n_all].reshape(9, d)
    tabs = [jnp.concatenate([mod_mine[:, j], mod_ctx[j][None]])[:, None, :] for j in range(9)]
    lat = lambda t: t[:n_ex]

    bf = lambda w: w[0].astype(BF16)
    plan = _Plan()
    gather = lambda *ws: (lambda p: Rider(list(ws), "chips"))
    plan.on("ffn1_norm", "wg1", gather(bf(ffn1_gate)))
    plan.on("ffn1_gate", "wu1", gather(bf(ffn1_up)))
    plan.on("ffn1_up", "wd1", gather(bf(ffn1_down)))
    cut_a, cut_b = d * 5 // 8, d * 7 // 8
    plan.on("ffn1_down", "win_a", gather(bf(w_in)[:cut_a]))
    plan.on("ffn1_resid", "win_b", gather(bf(w_in)[cut_a:cut_b], ssm_conv_w[0], cconv_w[0]))
    xt = two_rows(x.reshape(n_lat, d), ctx.reshape(n_ctx_rows, d), lat_tiles)
    x1, saved1 = _ffn_fwd(plan, "ffn1", xt, n_tok, tm, seg_all, tabs[0], tabs[1], tabs[2], norm_ffn1,
                          lambda: plan.got["wg1"][0], lambda: plan.got["wu1"][0], lambda: plan.got["wd1"][0])
    (wg1,), (wu1,), (wd1,), (win_a,), (win_b, w5_g, w31_g) = (plan.got[k] for k in ("wg1", "wu1", "wd1", "win_a", "win_b"))
    (h2,), (win_c,) = rowwise("mix_norm", fn_norm_mod, [row(x1)], [tabs[3], tabs[4]], [norm_mix], [(n_tok, d, BF16)],
                              tm=tm, n_tiles=n_tok // tm, seg_fn=seg_all, rider=Rider([bf(w_in)[cut_b:]], "chips"))
    win_g = jnp.concatenate([win_a, win_b, win_c], axis=1)
    unshard_cols = lambda t: t.transpose(1, 0, 2).reshape(t.shape[1], N_CHIPS * t.shape[2])
    win = unshard_cols(win_g)
    o_x, o_dt, o_glu = ds, ds + xw, ds + xw + 2 * n_head
    w_z, w_xbc, w_dt = win[:, :ds], win[:, o_x:o_dt], win[:, o_dt:o_glu]
    w_ga, w_gb = win[:, o_glu:o_glu + d], win[:, o_glu + d:]
    w_dtp = jnp.concatenate([w_dt, jnp.zeros((d, LANES - 2 * n_head), BF16)], axis=1)
    w_cat = jnp.concatenate([w_z, w_ga, w_gb, w_xbc], axis=1)
    cbw = d // 2
    xbc_cb, dt_cb = 3 * d // cbw, 0
    w5, w31 = unshard_cols(w5_g), unshard_cols(w31_g)
    pad_vec = lambda v: jnp.concatenate([v.reshape(1, -1), jnp.zeros((1, LANES - v.size), F32)], axis=1)
    dtb_f, dtb_b, alog_f, alog_b = map(pad_vec, (dt_bias_fwd, dt_bias_bwd, a_log_fwd, a_log_bwd))
    dsk_f, dsk_b = pad_vec(ssm_d), jnp.zeros((1, LANES), F32)

    proj, (wg2,) = matmul("mix_proj", [(h2, w_cat)], "nn", out_dtype=BF16, tm=tm, rider=Rider([bf(ffn2_gate)], "chips"))
    dt_raw = matmul("mix_proj_dt", [(h2, w_dtp)], "nn", tm=tm)
    def conv5(name, src, cb0, flip):
        out = None
        for part, seq, off in (("lat", seq_len, 0), ("ctx", ctx_len, n_lat // ctx_len)):
            out = tapsum_roll(f"{name}_{part}", src, cb0, w5, 0, seq_len=seq, n_seq=n_ex, row_blk_off=off, width=seq,
                              piece=seq, cb=cbw, ncb=xw // cbw, pad=w5.shape[0] // 2, flip=flip,
                              place=((n_tok, xw), off, 0, out), out_dtype=F32 if flip else BF16)
        return out

    craw = conv5("xbc_conv", proj, xbc_cb, False)
    (xbc,) = rowwise("xbc_silu", fn_silu_bias, [row(craw)], [], [ssm_conv_b], [(n_tok, xw, F32)], tm=tm, n_tiles=n_tok // tm)
    ssd = dict(n_ex=n_ex, seq_len=seq_len, ctx_len=ctx_len, ds=ds)
    (y_f, hs_f), (wu2,) = ssd_fwd("ssd_fwd_f", xbc, dt_raw, dt_cb, dtb_f, alog_f, dsk_f, rev=False,
                                  rider=Rider([bf(ffn2_up)], "chips"), **ssd)
    (y_b, hs_b), (wout_g, wd2) = ssd_fwd("ssd_fwd_b", xbc, dt_raw, dt_cb, dtb_b, alog_b, dsk_b, rev=True,
                                         rider=Rider([bf(w_out), bf(ffn2_down)], "chips"), add=y_f, **ssd)
    wout = wout_g.reshape(2 * d, d)
    wo_y, wo_u = wout[:ds], wout[ds:]
    fn_gate = make_fn_gate_groupnorm(ds)
    (yn,) = rowwise("ssd_gate", fn_gate, [row(y_b), row(proj, d, 0)], [], [ssm_norm_w], [(n_lat, ds, BF16)],
                    tm=tm, n_tiles=lat_tiles)
    (u0,) = rowwise("glu", fn_glu, [row(proj, d, 1), row(proj, d, 2)], [], [], [(n_lat, d, BF16)], tm=tm, n_tiles=lat_tiles)
    cb31 = max(LANES, d // 4)
    ncb31 = (d // 2) // cb31
    pad31 = w31.shape[0] // 2
    piece31 = min(seq_len, 4 * GRID_W)
    v_w = tapsum_roll("cconv_cols", u0, 0, w31, 0, seq_len=seq_len, n_seq=n_ex, row_blk_off=0, width=GRID_W,
                      piece=piece31, cb=cb31, ncb=ncb31, pad=pad31, flip=False)
    v_h = tapsum_rows("cconv_rows", u0, ncb31, w31, ncb31, seq_len=seq_len, n_seq=n_ex, cb=cb31, ncb=ncb31, pad=pad31, flip=False)
    (un,) = rowwise("cconv_ln", fn_ln_silu, [row(v_w), row(v_h)], [], [cconv_b, cconv_ln_w, cconv_ln_b], [(n_lat, d, BF16)],
                    tm=tm, n_tiles=lat_tiles)
    mix = matmul("mix_out", [(yn, wo_y), (un, wo_u)], "nn", tm=tm)
    seg_lat, first_lat = _segmenter(tm, seq_len, n_lat)
    (x2,) = rowwise("mix_resid", make_fn_resid(1.0), [row(x1), row(mix)], [lat(tabs[5])], [], [(n_lat, d, F32)],
                    tm=tm, n_tiles=lat_tiles, seg_fn=seg_lat)
    _, saved2 = _ffn_fwd(plan, "ffn2", row(x2), n_lat, tm, seg_lat, lat(tabs[6]), lat(tabs[7]), lat(tabs[8]), norm_ffn2, wg2, wu2, wd2,
                         fuse_gate_up=True, with_resid=False)
    d_x3, d_final, loss_vec = final_loss(x2, saved2[4], lat(tabs[8]), loss_target.reshape(n_lat, d), final_norm.reshape(1, d),
                                         tm=tm, seg_fn=seg_lat)

    shard_cols = lambda t: t.reshape(t.shape[0], N_CHIPS, -1).transpose(1, 0, 2)

    def pieces(t):
        t = jnp.pad(t, ((0, 0), (0, -t.shape[1] % 32), (0, 0)))
        return t.reshape(2 * N_CHIPS, t.shape[1] // 2, t.shape[2]).astype(BF16)

    scatter = lambda *ts: Rider([pieces(t) for t in ts], "all8", scatter=True)
    halves = lambda names, landed: Rider([sum_slots(f"sum_{nm}", r, BF16) for nm, r in zip(names, landed)], "sibling")
    swapped = {}
    plan.on("ffn2_up_dx", "sc_ffn2_down", lambda p: scatter(p.got["ffn2_d_wd"]))
    plan.on("ffn2_up_dw", "sc_ffn2_gate", lambda p: scatter(p.got["ffn2_d_wg"]))
    d_x2, (d_s6, d_s7, d_g8), d_nffn2 = _ffn_bwd(
        plan, "ffn2", d_x3, saved2, row(x2), n_lat, tm, seg_lat, first_lat, lat(tabs[6]), lat(tabs[7]), lat(tabs[8]), norm_ffn2,
        wg2, wu2, wd2, n_lat, None)
    d_mix, d_g5 = rowwise_bwd("mix_resid_bwd", make_fn_resid(1.0), [row(x1), row(mix)], [lat(tabs[5])], [], [[row(d_x2)]],
                              [None, (n_lat, BF16, None)], tm=tm, n_tiles=lat_tiles, seg_fn=seg_lat, first_fn=first_lat)
    d_yn = matmul("mix_out_dy", [(d_mix, wo_y)], "nt", tm=tm)
    d_un = matmul("mix_out_du", [(d_mix, wo_u)], "nt", tm=tm)
    d_wout = jnp.concatenate([matmul("mix_out_dwy", [(yn, d_mix)], "tn", out_dtype=BF16, tm=tm),
                              matmul("mix_out_dwu", [(un, d_mix)], "tn", out_dtype=BF16, tm=tm)])
    d_vw, d_vh, d_cb, d_lnw, d_lnb = rowwise_bwd(
        "cconv_ln_bwd", fn_ln_silu, [row(v_w), row(v_h)], [], [cconv_b, cconv_ln_w, cconv_ln_b], [[row(d_un)]],
        [(n_lat, F32, None)] * 2, tm=tm, n_tiles=lat_tiles)
    d_u0 = tapsum_roll("cconv_cols_dx", d_vw, 0, w31, 0, seq_len=seq_len, n_seq=n_ex, row_blk_off=0, width=GRID_W,
                       piece=piece31, cb=cb31, ncb=ncb31, pad=pad31, flip=True, place=((n_lat, d), 0, 0, None))
    d_u0 = tapsum_rows("cconv_rows_dx", d_vh, 0, w31, ncb31, seq_len=seq_len, n_seq=n_ex, cb=cb31, ncb=ncb31, pad=pad31,
                       flip=True, place=((n_lat, d), 0, ncb31, d_u0))
    d_w31 = jnp.concatenate([
        tapgrad_roll("cconv_cols_dw", d_vw, 0, 0, u0, 0, 0, n_tap=w31.shape[0], seq_len=seq_len, n_seq=n_ex, width=GRID_W,
                     piece=piece31, cb=cb31, ncb=ncb31, pad=pad31),
        tapgrad_rows("cconv_rows_dw", d_vh, 0, u0, ncb31, n_tap=w31.shape[0], seq_len=seq_len, n_seq=n_ex, cb=cb31,
                     ncb=ncb31, pad=pad31)], axis=1)
    d_ga, d_gb = rowwise_bwd("glu_bwd", fn_glu, [row(proj, d, 1), row(proj, d, 2)], [], [], [[row(d_u0)]],
                             [(n_lat, BF16, None)] * 2, tm=tm, n_tiles=lat_tiles)
    d_ysum, d_z, d_ssmnw = rowwise_bwd(
        "ssd_gate_bwd", fn_gate, [row(y_b), row(proj, d, 0)], [], [ssm_norm_w], [[row(d_yn)]],
        [(n_lat, F32, None), (n_lat, BF16, None)], tm=tm, n_tiles=lat_tiles)
    (dxbc_f, ddt_f, dalog_f, ddtb_f, ddsk), landed = ssd_bwd(
        "ssd_bwd_f", xbc, dt_raw, dt_cb, hs_f, d_ysum, dtb_f, alog_f, dsk_f, rev=False,
        rider=scatter(plan.got["ffn2_d_wu"], d_wout.reshape(N_CHIPS, -1, d)), **ssd)
    (dxbc_b, ddt_b, dalog_b, ddtb_b, _), both = ssd_bwd(
        "ssd_bwd_b", xbc, dt_raw, dt_cb, hs_b, d_ysum, dtb_b, alog_b, dsk_b, rev=True,
        rider=halves(["ffn2_down", "ffn2_gate"], plan.got["sc_ffn2_down"] + plan.got["sc_ffn2_gate"]), add=dxbc_f, **ssd)
    swapped.update(zip(["ffn2_down", "ffn2_gate"], both))
    (d_craw, d_conv_b), both = rowwise_bwd(
        "xbc_silu_bwd", fn_silu_bias, [row(craw)], [], [ssm_conv_b], [[row(dxbc_b)]],
        [(n_tok, F32, None)], tm=tm, n_tiles=n_tok // tm, rider=halves(["ffn2_up", "w_out"], landed))
    swapped.update(zip(["ffn2_up", "w_out"], both))
    d_pxbc = conv5("xbc_conv_dx", d_craw, 0, True)
    g5 = lambda name, seq, off: tapgrad_roll(name, d_craw, 0, off, proj, xbc_cb, off, n_tap=w5.shape[0], seq_len=seq,
                                             n_seq=n_ex, width=seq, piece=seq, cb=cbw, ncb=xw // cbw, pad=w5.shape[0] // 2)
    d_w5 = g5("xbc_conv_lat_dw", seq_len, 0) + g5("xbc_conv_ctx_dw", ctx_len, n_lat // ctx_len)
    lat_pairs = [(d_z, w_z), (d_ga, w_ga), (d_gb, w_gb), (d_pxbc, w_xbc), (ddt_f, w_dtp), (ddt_b, w_dtp)]
    d_h2 = matmul("mix_proj_dx_lat", lat_pairs, "nt", rows=n_lat, tm=min(tm, 256), place=(n_tok, 0, None))
    d_h2 = matmul("mix_proj_dx_ctx", lat_pairs[3:], "nt", rows=n_ctx_rows, row_off=n_lat, tm=min(tm, 256),
                  place=(n_tok, n_lat, d_h2))
    d_wz = matmul("mix_proj_dwz", [(d_z, h2)], "tn", out_dtype=BF16, rows=n_lat, tm=tm)
    d_wga = matmul("mix_proj_dwa", [(d_ga, h2)], "tn", out_dtype=BF16, rows=n_lat, tm=tm)
    d_wgb = matmul("mix_proj_dwb", [(d_gb, h2)], "tn", out_dtype=BF16, rows=n_lat, tm=tm)
    d_wxbc = matmul("mix_proj_dwx", [(d_pxbc, h2)], "tn", out_dtype=BF16, tm=tm)
    d_wdt = matmul("mix_proj_dwt", [(ddt_f, h2), (ddt_b, h2)], "tn", out_dtype=BF16, tm=tm)
    d_win_t = jnp.concatenate([d_wz, d_wxbc, d_wdt[:2 * n_head], d_wga, d_wgb]).reshape(N_CHIPS, -1, d)
    d_x1, d_s3, d_s4, d_nmix = rowwise_bwd(
        "mix_norm_bwd", fn_norm_mod, [row(x1)], [tabs[3], tabs[4]], [norm_mix], [[row(d_h2)]], [(n_tok, F32, None)],
        tm=tm, n_tiles=n_tok // tm, seg_fn=seg_all, first_fn=first_all, adds={0: (row(d_x2), lat_tiles)})
    mix_names = ["w_in", "ssm_conv_w", "cconv_w"]
    plan.on("ffn1_down_dx", "sc_conv", lambda p: scatter(shard_cols(d_w5), shard_cols(d_w31)))
    plan.on("ffn1_up_dx", "sc_win", lambda p: scatter(d_win_t))
    plan.on("ffn1_gate_dw", "sc_ffn1_down", lambda p: scatter(p.got["ffn1_d_wd"]))
    plan.on("ffn1_up_dw", "sc_ffn1_gate", lambda p: scatter(p.got["ffn1_d_wg"]))
    plan.on("ffn1_up_dw", "sw_mix", lambda p: halves(mix_names, p.got["sc_win"] + p.got["sc_conv"]))
    plan.on("ffn1_norm_bwd", "sc_ffn1_up", lambda p: scatter(p.got["ffn1_d_wu"]))
    plan.on("ffn1_norm_bwd", "sw_ffn1_gate", lambda p: halves(["ffn1_gate"], p.got["sc_ffn1_gate"]))
    plan.on("ffn1_up_dw", "sw_ffn1_down", lambda p: halves(["ffn1_down"], p.got["sc_ffn1_down"]))
    d_xt, (d_s0, d_s1, d_g2), d_nffn1 = _ffn_bwd(
        plan, "ffn1", d_x1, saved1, xt, n_tok, tm, seg_all, first_all, tabs[0], tabs[1], tabs[2], norm_ffn1, wg1, wu1, wd1,
        n_lat, lat_tiles)
    swapped.update(zip(mix_names + ["ffn1_down", "ffn1_gate"],
                       plan.got["sw_mix"] + plan.got["sw_ffn1_down"] + plan.got["sw_ffn1_gate"]))
    last_names = ["ffn1_up"]
    last = halves(last_names, plan.got["sc_ffn1_up"])
    grad_x = d_xt.reshape(n_ex, seq_len, d)

    with_ctx0 = lambda t: jnp.concatenate([t, jnp.zeros((1, 1, d), F32)])
    d_tabs = [d_s0, d_s1, d_g2, d_s3, d_s4, with_ctx0(d_g5), with_ctx0(d_s6), with_ctx0(d_s7), with_ctx0(d_g8)]
    d_mod_rows = jnp.concatenate([t[:, 0, :] for t in d_tabs], axis=1)
    n_pad_rows = -(-(n_ex + 1) // 8) * 8
    d_mod_rows = jnp.concatenate([d_mod_rows, jnp.zeros((n_pad_rows - n_ex - 1, 9 * d), F32)])
    small = [("loss", loss_vec), ("norm_ffn1", d_nffn1), ("norm_mix", d_nmix), ("ssm_conv_b", d_conv_b),
             ("dt_bias_fwd", ddtb_f[:, :n_head]), ("dt_bias_bwd", ddtb_b[:, :n_head]), ("a_log_fwd", dalog_f[:, :n_head]),
             ("a_log_bwd", dalog_b[:, :n_head]), ("ssm_d", ddsk[:, :n_head]), ("ssm_norm_w", d_ssmnw), ("cconv_b", d_cb),
             ("cconv_ln_w", d_lnw), ("cconv_ln_b", d_lnb), ("norm_ffn2", d_nffn2), ("final_norm", d_final)]
    n_small = sum(v.size for _, v in small)
    n_pack = -(-n_small // (8 * LANES)) * (8 * LANES)
    pack = jnp.concatenate([v.reshape(-1) for _, v in small] + [jnp.zeros((n_pack - n_small,), F32)]).reshape(-1, LANES)
    (pack_all, d_mod_all), both = exchange_many("gather_small_swap_last", [Rider([pack, d_mod_rows], "all8"), last])
    swapped.update(zip(last_names, both))
    pack_sum = sum_slots("small_sum", pack_all)
    loss = loss_total(pack_sum.reshape(1, n_pack), d).reshape(())
    flat_sum = pack_sum.reshape(-1)
    small_grads, pos = {}, 0
    for nm, v in small:
        small_grads[nm] = flat_sum[pos:pos + v.size]
        pos += v.size
    d_mod_all = d_mod_all.reshape(8 * n_pad_rows, 9 * d)
    cond_rows = [jnp.concatenate([cond[j * n_ex:(j + 1) * n_ex], c_ctx[None, :],
                                  jnp.zeros((n_pad_rows - n_ex - 1, d), F32)]) for j in range(8)]
    cond_bwd = jnp.concatenate(cond_rows)
    d_mod_shard = lax.dynamic_slice(d_mod_all, (0, chip * mod_w), (8 * n_pad_rows, mod_w))
    g_wmod, g_bmod, q_part = mod_bwd(cond_bwd, d_mod_shard, d_mod_all, w_mod[0],
                                     tuple(j * n_pad_rows + n_ex for j in range(8)))
    (q_all,) = exchange("gather_cctx", [q_part], "all8")
    g_cctx = cctx_grad(q_all, c_ctx.reshape(1, d))
    small_grads["c_ctx"], small_grads["b_mod"] = g_cctx.reshape(-1), g_bmod.reshape(-1)

    transposed = {"ffn1_gate", "ffn1_up", "ffn2_gate", "ffn2_up", "w_in"}
    results = {}
    for nm, both in swapped.items():
        flip = (lambda t: jnp.swapaxes(t, 1, 2)) if nm in transposed else (lambda t: t)
        shape = flip(weights[nm]).shape
        two_d = lambda t: flip(t).reshape(shape[-2], shape[-1])
        g_full = both.reshape(1, -1, shape[-1])[:, :shape[-2]]
        results[nm] = [flip(r.reshape(shape)) for r in
                       adamw(f"adamw_{nm}", two_d(weights[nm]), g_full, two_d(mom1[nm]), two_d(mom2[nm]))]
    results["w_mod"] = [r.reshape(w_mod.shape) for r in adamw("adamw_w_mod", w_mod[0], g_wmod[None], m_w_mod[0], v_w_mod[0])]
    small_names = [nm for nm in order if nm not in results]
    n_sm = sum(weights[nm].size for nm in small_names)
    n_smp = -(-n_sm // (8 * LANES)) * (8 * LANES)
    packed = lambda src: jnp.concatenate([src[nm].reshape(-1) for nm in small_names] + [jnp.zeros((n_smp - n_sm,), F32)]).reshape(-1, LANES)
    sm_out = adamw("adamw_small", packed(weights), packed(small_grads)[None], packed(mom1), packed(mom2))
    pos = 0
    for nm in small_names:
        size = weights[nm].size
        results[nm] = [r.reshape(-1)[pos:pos + size].reshape(weights[nm].shape) for r in sm_out]
        pos += size
    return (loss, grad_x, *[results[nm][0] for nm in order], *[results[nm][1] for nm in order],
            *[results[nm][2] for nm in order], *[results[nm][3] for nm in order])
```

```python
import functools
import math

import jax
import jax.numpy as jnp
from jax import lax
from jax.experimental import pallas as pl
from jax.experimental.pallas import tpu as pltpu

F32 = jnp.float32
BF16 = jnp.bfloat16
HI = lax.Precision.HIGHEST
MESH = pl.DeviceIdType.MESH

EPS = 1e-6
GRID_W = 64
HEAD_DIM = 64
N_STATE = 128
CHUNK = 128
LANES = 128
N_CHIPS = 4
ADAM_LR, ADAM_B1, ADAM_B2, ADAM_EPS, ADAM_WD, ADAM_STEP = 0.001, 0.9, 0.999, 1e-08, 0.01, 10
VMEM_CAP = 56 * 1024 * 1024
STREAM_TILE_BYTES = 3 << 19


def _params(vmem_bytes=None, n_axes=1):
    kw = dict(dimension_semantics=("arbitrary",) * n_axes)
    if vmem_bytes is not None:
        kw["vmem_limit_bytes"] = int(min(VMEM_CAP, max(32 * 1024 * 1024, vmem_bytes)))
    return pltpu.CompilerParams(**kw)


def _big(shape, dtype):
    return pltpu.HBM(tuple(shape), dtype)


def _in_hbm(args):
    return [pltpu.with_memory_space_constraint(a, pltpu.HBM) if a.size * a.dtype.itemsize >= (1 << 20) else a for a in args]


def _nbytes(shape, dtype):
    return math.prod(shape) * jnp.dtype(dtype).itemsize


def _row_tile(rows, width, cap_bytes=1 << 20, mult=8):
    best = None
    for t in range(mult, rows + 1, mult):
        if rows % t == 0 and t * width * 4 <= cap_bytes:
            best = t
    return best if best is not None else rows


_MODES = {"all8": (8, (1, 2, 3, 4, 5, 6, 7), 0), "chips": (4, (2, 4, 6), 1), "sibling": (2, (1,), 0)}


class Rider:
    def __init__(self, arrs, mode, scatter=False):
        self.arrs, self.scatter = list(arrs), scatter
        self.nslot, self.deltas, self.shift = _MODES[mode]
        self.n = len(self.arrs)
        self.out_shape = [jax.ShapeDtypeStruct((self.nslot,) + (a.shape[1:] if scatter else a.shape), a.dtype)
                          for a in self.arrs]
        any_spec = pl.BlockSpec(memory_space=pl.ANY)
        self.in_specs = [any_spec] * self.n
        self.out_specs = [any_spec] * self.n
        n_peer = len(self.deltas)
        self.scratch = [pltpu.SemaphoreType.DMA((self.n, n_peer)), pltpu.SemaphoreType.DMA((self.n, n_peer)),
                        pltpu.SemaphoreType.DMA((self.n,))]

    def _copies(self, ins, outs, sems, arrivals):
        send_sems, recv_sems, local_sems = sems
        x, y, c = lax.axis_index("x"), lax.axis_index("y"), lax.axis_index("c")
        me = 4 * x + 2 * y + c
        slot_of = lambda dev: (dev >> self.shift) & (self.nslot - 1)
        src = lambda a, slot: ins[a].at[slot] if self.scatter else ins[a]
        flip = lambda v, bit: 1 - v if bit else v

        def remote(a, k, d, from_slot, to_slot):
            return pltpu.make_async_remote_copy(
                src_ref=src(a, from_slot), dst_ref=outs[a].at[to_slot], send_sem=send_sems.at[a, k],
                recv_sem=recv_sems.at[a, k], device_id=(flip(x, (d >> 2) & 1), flip(y, (d >> 1) & 1), flip(c, d & 1)),
                device_id_type=MESH)

        mine = slot_of(me)
        local = [pltpu.make_async_copy(src(a, mine), outs[a].at[mine], local_sems.at[a]) for a in range(self.n)]
        sends = [remote(a, k, d, slot_of(me ^ d), mine) for k, d in enumerate(self.deltas) for a in range(self.n)]
        if not arrivals:
            return local, sends
        return local, sends, [remote(a, k, d, mine, slot_of(me ^ d)) for k, d in enumerate(self.deltas) for a in range(self.n)]

    def start(self, ins, outs, sems):
        local, sends = self._copies(ins, outs, sems, arrivals=False)
        for cp in local + sends:
            cp.start()

    def wait(self, ins, outs, sems):
        local, sends, recvs = self._copies(ins, outs, sems, arrivals=True)
        for cp in recvs:
            cp.wait_recv()
        for cp in sends:
            cp.wait_send()
        for cp in local:
            cp.wait()


class Riders:
    def __init__(self, riders):
        self.riders = list(riders)
        self.n = sum(r.n for r in self.riders)
        cat = lambda attr: [v for r in self.riders for v in getattr(r, attr)]
        self.arrs, self.out_shape, self.in_specs = cat("arrs"), cat("out_shape"), cat("in_specs")
        self.out_specs, self.scratch = cat("out_specs"), cat("scratch")

    def _each(self, method, ins, outs, sems):
        i = s = 0
        for r in self.riders:
            getattr(r, method)(ins[i:i + r.n], outs[i:i + r.n], sems[s:s + len(r.scratch)])
            i, s = i + r.n, s + len(r.scratch)

    def start(self, ins, outs, sems):
        self._each("start", ins, outs, sems)

    def wait(self, ins, outs, sems):
        self._each("wait", ins, outs, sems)


class _Hosted:
    def __init__(self, rider, n_in, n_out, n_scratch, grid):
        self.rider, self.n_in, self.n_out, self.n_scratch, self.grid = rider, n_in, n_out, n_scratch, grid
        self.n = rider.n if rider else 0

    def split(self, refs):
        a, b = self.n_in, self.n_in + self.n
        c, e = b + self.n_out, b + self.n_out + self.n
        self._r = (refs[a:b], refs[c:e], refs[e + self.n_scratch:])
        if self.rider:
            ids = [pl.program_id(ax) for ax in range(len(self.grid))]
            first = functools.reduce(jnp.logical_and, [i == 0 for i in ids]) if ids else True
            pl.when(first)(lambda: self.rider.start(*self._r))
        return refs[:a], refs[b:c], refs[e:e + self.n_scratch]

    def finish(self):
        if self.rider:
            ids = [pl.program_id(ax) for ax in range(len(self.grid))]
            last = functools.reduce(jnp.logical_and, [i == n - 1 for i, n in zip(ids, self.grid)]) if ids else True
            pl.when(last)(lambda: self.rider.wait(*self._r))

    def call_args(self, in_specs, out_shape, out_specs, scratch, args):
        r = self.rider
        if not r:
            return list(in_specs), tuple(out_shape), tuple(out_specs), list(scratch), list(args)
        return (list(in_specs) + r.in_specs, tuple(out_shape) + tuple(r.out_shape), tuple(out_specs) + tuple(r.out_specs),
                list(scratch) + r.scratch, list(args) + r.arrs)

    def results(self, res, unwrap=True):
        res = list(res) if isinstance(res, (tuple, list)) else [res]
        host = res[:self.n_out]
        host = host[0] if (self.n_out == 1 and unwrap) else tuple(host)
        return (host, res[self.n_out:]) if self.rider else host


def exchange_many(name, riders):
    both = Riders(riders)

    def body(*refs):
        ins, outs, sems = refs[:both.n], refs[both.n:2 * both.n], refs[2 * both.n:]
        both.start(ins, outs, sems)
        both.wait(ins, outs, sems)

    res = list(pl.pallas_call(
        body, name=name, out_shape=tuple(both.out_shape), in_specs=both.in_specs, out_specs=tuple(both.out_specs),
        scratch_shapes=both.scratch,
    )(*both.arrs))
    split = []
    for r in riders:
        split.append(res[:r.n])
        res = res[r.n:]
    return split


def exchange(name, arrs, mode, scatter=False):
    rider = Rider(arrs, mode, scatter)

    def body(*refs):
        ins, outs, sems = refs[:rider.n], refs[rider.n:2 * rider.n], refs[2 * rider.n:]
        rider.start(ins, outs, sems)
        rider.wait(ins, outs, sems)

    return pl.pallas_call(
        body, name=name, out_shape=tuple(rider.out_shape), in_specs=rider.in_specs, out_specs=tuple(rider.out_specs),
        scratch_shapes=rider.scratch,
    )(*arrs)


_DIMS = {"nn": (((1,), (0,)), ((), ())), "nt": (((1,), (1,)), ((), ())), "tn": (((0,), (0,)), ((), ()))}


def matmul(name, pairs, kind, *, a_ch=False, b_ch=False, out_ch=False, out_dtype=F32, rows=None, row_off=0, tm=512,
           rider=None, post=None, fold=False, place=None):
    a0, b0 = pairs[0]
    n_chunk = a0.shape[0] if a_ch else (b0.shape[0] if b_ch else 1)
    total_rows = a0.shape[-2]
    rows = total_rows - row_off if rows is None else rows
    tm = min(tm, rows)
    assert rows % tm == 0 and row_off % tm == 0, (name, rows, tm, row_off)
    n_rt, off = rows // tm, row_off // tm
    dims = _DIMS[kind]
    n_pair = len(pairs)

    if kind == "tn":
        grid, red_axis, n_red = (n_chunk, n_rt), 1, n_rt
        a_idx = (lambda k, i: (k, i + off, 0)) if a_ch else (lambda k, i: (i + off, 0))
        b_idx = (lambda k, i: (k, i + off, 0)) if b_ch else (lambda k, i: (i + off, 0))
        a_blk = lambda a: ((None, tm, a.shape[-1]) if a_ch else (tm, a.shape[-1]))
        b_blk = lambda b: ((None, tm, b.shape[-1]) if b_ch else (tm, b.shape[-1]))
        o2 = (a0.shape[-1], b0.shape[-1])
        out_shape = ((n_chunk,) + o2) if out_ch else o2
        out_spec = pl.BlockSpec((None,) + o2, lambda k, i: (k, 0, 0)) if out_ch else pl.BlockSpec(o2, lambda k, i: (0, 0))
        acc_shape = o2
    else:
        n_out = b0.shape[-1] if kind == "nn" else b0.shape[-2]
        b2 = b0.shape[-2:]
        if a_ch and b_ch and not out_ch and fold:
            grid, red_axis, n_red = (n_rt,), None, 1
            a_idx, b_idx = (lambda i: (0, i + off, 0)), (lambda i: (0, 0, 0))
            a_blk = lambda a: (n_chunk, tm, a.shape[-1])
            b_blk = lambda b: tuple(b.shape)
            out_shape, out_spec = (rows, n_out), pl.BlockSpec((tm, n_out), lambda i: (i, 0))
        elif a_ch and b_ch and not out_ch:
            grid, red_axis, n_red = (n_rt, n_chunk), 1, n_chunk
            a_idx, b_idx = (lambda i, k: (k, i + off, 0)), (lambda i, k: (k, 0, 0))
            a_blk = lambda a: (None, tm, a.shape[-1])
            b_blk = lambda b: (None,) + tuple(b.shape[-2:])
            out_shape, out_spec = (rows, n_out), pl.BlockSpec((tm, n_out), lambda i, k: (i, 0))
        elif out_ch and fold:
            assert b_ch and not a_ch and all(a is a0 for a, _ in pairs)
            grid, red_axis, n_red = (n_rt,), None, 1
            a_idx, b_idx = (lambda i: (i + off, 0)), (lambda i: (0, 0, 0))
            a_blk = lambda a: (tm, a.shape[-1])
            b_blk = lambda b: tuple(b.shape)
            out_shape, out_spec = (n_chunk, rows, n_out), pl.BlockSpec((n_chunk, tm, n_out), lambda i: (0, i, 0))
        elif out_ch:
            assert b_ch and not a_ch
            grid, red_axis, n_red = (n_chunk, n_rt), None, 1
            a_idx, b_idx = (lambda k, i: (i + off, 0)), (lambda k, i: (k, 0, 0))
            a_blk = lambda a: (tm, a.shape[-1])
            b_blk = lambda b: (None,) + tuple(b.shape[-2:])
            out_shape, out_spec = (n_chunk, rows, n_out), pl.BlockSpec((None, tm, n_out), lambda k, i: (k, i, 0))
        else:
            assert not (a_ch or b_ch)
            grid, red_axis, n_red = (n_rt,), None, 1
            a_idx, b_idx = (lambda i: (i + off, 0)), (lambda i: (0, 0))
            a_blk = lambda a: (tm, a.shape[-1])
            b_blk = lambda b: tuple(b.shape)
            out_shape, out_spec = (rows, n_out), pl.BlockSpec((tm, n_out), lambda i: (i, 0))
            if place is not None:
                out_shape, o_off = (place[0], n_out), place[1] // tm
                out_spec = pl.BlockSpec((tm, n_out), lambda i: (i + o_off, 0))
        acc_shape = (tm, n_out)

    into = [] if place is None or place[2] is None else [place[2]]
    post_ins, post_fn, out_dtypes = ([], None, [out_dtype]) if post is None else post
    hosted = _Hosted(rider, 2 * n_pair + len(post_ins) + len(into), len(out_dtypes), int(n_red > 1), grid)

    def body(*refs):
        ins, outs, scr = hosted.split(refs)

        def compute():
            acc = None
            for p in range(n_pair):
                for k in ([None] if not fold else range(n_chunk)):
                    pick = (lambda r: r[...]) if k is None else (lambda r: r[k])
                    d = lax.dot_general(pick(ins[2 * p]).astype(BF16), pick(ins[2 * p + 1]).astype(BF16), dims,
                                        preferred_element_type=F32)
                    acc = d if acc is None else acc + d
            return acc

        def emit(acc):
            vals = (acc,) if post_fn is None else post_fn(
                acc, *[r[...].astype(F32) for r in ins[2 * n_pair:2 * n_pair + len(post_ins)]])
            for o_ref, v in zip(outs, vals):
                o_ref[...] = v.astype(o_ref.dtype)

        if out_ch and fold:
            a_tile = ins[0][...].astype(BF16)
            for k in range(n_chunk):
                accs = [lax.dot_general(a_tile, ins[2 * p + 1][k].astype(BF16), dims, preferred_element_type=F32)
                        for p in range(n_pair)]
                tiles = [r[k].astype(F32) for r in ins[2 * n_pair:2 * n_pair + len(post_ins)]]
                vals = tuple(accs) if post_fn is None else post_fn(*accs, *tiles)
                for o_ref, v in zip(outs, vals):
                    o_ref[k] = v.astype(o_ref.dtype)
        elif n_red == 1:
            emit(compute())
        else:
            acc_ref = scr[0]
            r = pl.program_id(red_axis)

            @pl.when(r == 0)
            def _():
                acc_ref[...] = jnp.zeros_like(acc_ref)

            acc_ref[...] += compute()

            @pl.when(r == n_red - 1)
            def _():
                emit(acc_ref[...])
        hosted.finish()

    in_specs, args, vmem = [], [], 0
    for a, b in pairs:
        in_specs += [pl.BlockSpec(a_blk(a), a_idx), pl.BlockSpec(b_blk(b), b_idx)]
        args += [a, b]
        vmem += 2 * (_nbytes([s for s in a_blk(a) if s], a.dtype) + _nbytes([s for s in b_blk(b) if s], b.dtype))
    in_specs += [out_spec] * len(post_ins)
    args += list(post_ins)
    aliases = {len(args): 0} if into else {}
    in_specs += [pl.BlockSpec(memory_space=pl.ANY)] * len(into)
    args += into
    tiles_per_step = n_chunk if (out_ch and fold) else 1
    vmem += (3 + 2 * n_pair + tiles_per_step * (len(post_ins) + len(out_dtypes))) * _nbytes(acc_shape, F32)
    scratch = [pltpu.VMEM(acc_shape, F32)] if n_red > 1 else []
    in_specs, out_shapes, out_specs, scratch, args = hosted.call_args(
        in_specs, [_big(out_shape, dt) for dt in out_dtypes], [out_spec] * len(out_dtypes), scratch, args)
    return hosted.results(pl.pallas_call(
        body, name=name, out_shape=out_shapes, grid=grid, in_specs=in_specs, out_specs=out_specs,
        input_output_aliases=aliases, scratch_shapes=scratch, compiler_params=_params(vmem + (8 << 20), len(grid)),
    )(*_in_hbm(args)))


def row(arr, width=None, cb=0, roff=0):
    return (arr, arr.shape[-1] if width is None else width, cb, roff)


def two_rows(first, second, limit):
    return (first, first.shape[-1], 0, 0, (second, limit))


def _row_inputs(rows, tm):
    specs, arrs, slots = [], [], []
    for d in rows:
        second, limit = d[4] if len(d) > 4 else (None, None)
        slots.append((len(arrs), limit))
        specs.append(_row_spec(d[:4], tm, limit))
        arrs.append(d[0])
        if second is not None:
            specs.append(pl.BlockSpec((tm, d[1]), lambda i, limit=limit: (jnp.maximum(i - limit, 0), 0)))
            arrs.append(second)

    def read(refs, i):
        vals = []
        for at, limit in slots:
            v = refs[at][...].astype(F32)
            vals.append(v if limit is None else jnp.where(i < limit, v, refs[at + 1][...].astype(F32)))
        return vals

    return specs, arrs, read


def _row_spec(desc, tm, limit=None):
    _, width, cb, roff = desc[:4]
    if limit is None:
        return pl.BlockSpec((tm, width), lambda i: (i + roff, cb))
    return pl.BlockSpec((tm, width), lambda i: (jnp.minimum(i, limit - 1) + roff, cb))


def _segmenter(tm, seq_len, n_lat):
    seg = lambda i: jnp.where(i * tm < n_lat, (i * tm) // seq_len, n_lat // seq_len)
    first = lambda i: jnp.where(i * tm < n_lat, (i * tm) % seq_len == 0, i * tm == n_lat)
    return seg, first


def rowwise(name, fn, rows, segs, params, outs, *, tm, n_tiles, seg_fn=None, rider=None):
    row_specs, row_arrs, read_rows = _row_inputs(rows, tm)
    n_r, n_s, n_p = len(row_arrs), len(segs), len(params)
    hosted = _Hosted(rider, n_r + n_s + n_p, len(outs), 0, (n_tiles,))

    def body(*refs):
        ins, out_refs, _ = hosted.split(refs)
        vals = read_rows(ins[:n_r], pl.program_id(0)) + [r[...] for r in ins[n_r:]]
        res = fn(*vals)
        for o_ref, v in zip(out_refs, res):
            o_ref[...] = v.astype(o_ref.dtype)
        hosted.finish()

    in_specs = list(row_specs)
    in_specs += [pl.BlockSpec((None, 1, s.shape[-1]), lambda i: (seg_fn(i), 0, 0)) for s in segs]
    in_specs += [pl.BlockSpec(p.shape, lambda i: (0, 0)) for p in params]
    vmem = sum(2 * tm * d[1] * 4 for d in rows) + sum(3 * tm * w * 4 for _, w, _ in outs) + sum(2 * p.size * 4 for p in params)
    in_specs, out_shapes, out_specs, scratch, args = hosted.call_args(
        in_specs, [_big((r, w), dt) for r, w, dt in outs],
        [pl.BlockSpec((tm, w), lambda i: (i, 0)) for _, w, _ in outs], [], row_arrs + list(segs) + list(params))
    return hosted.results(pl.pallas_call(
        body, name=name, grid=(n_tiles,), in_specs=in_specs, out_shape=out_shapes, out_specs=out_specs,
        scratch_shapes=scratch, compiler_params=_params(2 * vmem + (8 << 20)),
    )(*_in_hbm(args)), unwrap=False)


def rowwise_bwd(name, fn, rows, segs, params, cts, row_grads, *, tm, n_tiles, seg_fn=None, first_fn=None, adds=None,
                rider=None):
    adds = adds or {}
    need = [k for k, v in enumerate(row_grads) if v is not None]
    row_specs, row_arrs, read_rows = _row_inputs(rows, tm)
    n_r, n_s, n_p = len(row_arrs), len(segs), len(params)
    n_ct = sum(len(lst) for lst in cts)
    add_keys = sorted(adds)
    hosted = _Hosted(rider, n_r + n_s + n_p + n_ct + len(add_keys), len(need) + n_s + n_p, 0, (n_tiles,))

    def body(*refs):
        host_in, host_out, _ = hosted.split(refs)
        it = iter(list(host_in) + list(host_out))
        row_refs = [next(it) for _ in range(n_r)]
        seg_refs = [next(it) for _ in range(n_s)]
        par_refs = [next(it) for _ in range(n_p)]
        ct_refs = [[next(it) for _ in lst] for lst in cts]
        add_refs = {k: next(it) for k in add_keys}
        rg_refs = {k: next(it) for k in need}
        sg_refs = [next(it) for _ in range(n_s)]
        pg_refs = [next(it) for _ in range(n_p)]
        i = pl.program_id(0)
        rv = read_rows(row_refs, i)
        sv = [r[...] for r in seg_refs]
        pv = [r[...] for r in par_refs]

        def f(*args):
            rr = list(rv)
            for j, k in enumerate(need):
                rr[k] = args[j]
            return fn(*rr, *args[len(need):])

        _, vjp = jax.vjp(f, *[rv[k] for k in need], *sv, *pv)
        ctv = []
        for lst in ct_refs:
            acc = lst[0][...].astype(F32)
            for r in lst[1:]:
                acc = acc + r[...].astype(F32)
            ctv.append(acc)
        g = vjp(tuple(ctv))
        for j, k in enumerate(need):
            gv = g[j]
            if k in adds:
                lim = adds[k][1]
                av = add_refs[k][...].astype(F32)
                gv = gv + (av if lim is None else jnp.where(i < lim, av, 0.0))
            lim = row_grads[k][2]
            if lim is None:
                rg_refs[k][...] = gv.astype(rg_refs[k].dtype)
            else:
                @pl.when(i < lim)
                def _(gv=gv, k=k):
                    rg_refs[k][...] = gv.astype(rg_refs[k].dtype)
        if n_s:
            opens = first_fn(i)
            for ref, gv in zip(sg_refs, g[len(need):len(need) + n_s]):
                @pl.when(opens)
                def _(ref=ref, gv=gv):
                    ref[...] = gv

                @pl.when(jnp.logical_not(opens))
                def _(ref=ref, gv=gv):
                    ref[...] += gv
        for ref, gv in zip(pg_refs, g[len(need) + n_s:]):
            @pl.when(i == 0)
            def _(ref=ref, gv=gv):
                ref[...] = gv

            @pl.when(i > 0)
            def _(ref=ref, gv=gv):
                ref[...] += gv
        hosted.finish()

    seg_spec = lambda s: pl.BlockSpec((None, 1, s.shape[-1]), lambda i: (seg_fn(i), 0, 0))
    par_spec = lambda p: pl.BlockSpec(p.shape, lambda i: (0, 0))
    in_specs = list(row_specs) + [seg_spec(s) for s in segs] + [par_spec(p) for p in params]
    args = row_arrs + list(segs) + list(params)
    for lst in cts:
        in_specs += [_row_spec(d, tm) for d in lst]
        args += [d[0] for d in lst]
    for k in add_keys:
        in_specs.append(_row_spec(adds[k][0], tm, adds[k][1]))
        args.append(adds[k][0][0])
    out_shape, out_specs = [], []
    for k in need:
        n_rows, dt, lim = row_grads[k]
        out_shape.append(_big((n_rows, rows[k][1]), dt))
        out_specs.append(_row_spec((None, rows[k][1], 0, 0), tm, lim))
    for s in segs:
        out_shape.append(jax.ShapeDtypeStruct(s.shape, F32))
        out_specs.append(seg_spec(s))
    for p in params:
        out_shape.append(jax.ShapeDtypeStruct(p.shape, F32))
        out_specs.append(par_spec(p))
    vmem = sum(tm * d[1] * 4 for d in rows) * 6 + n_ct * tm * max(d[1] for d in rows) * 8
    in_specs, out_shape, out_specs, scratch, args = hosted.call_args(in_specs, out_shape, out_specs, [], args)
    return hosted.results(pl.pallas_call(
        body, name=name, grid=(n_tiles,), in_specs=in_specs, out_shape=out_shape, out_specs=out_specs,
        scratch_shapes=scratch, compiler_params=_params(vmem + (8 << 20)),
    )(*_in_hbm(args)), unwrap=False)


def _silu(v):
    return v * jax.nn.sigmoid(v)


def _rms(v, w):
    return v * lax.rsqrt(jnp.mean(v * v, axis=-1, keepdims=True) + EPS) * w


def fn_norm_mod(x, shift, scale, w):
    return (_rms(x, w) * (1.0 + scale) + shift,)


def fn_act(g, u):
    return (_silu(g) * u,)


def make_fn_resid(coef):
    def fn(x, f, gate):
        return (x + coef * gate * f,)
    return fn


def fn_silu_bias(v, b):
    return (_silu(v + b),)


def make_fn_gate_groupnorm(width):
    half = width // 2

    def fn(y_both, z, w):
        y = y_both * _silu(z)
        lane = lax.broadcasted_iota(jnp.int32, y.shape, 1)
        lo = lane < half
        sq = y * y
        s_lo = jnp.sum(jnp.where(lo, sq, 0.0), axis=-1, keepdims=True)
        s_hi = jnp.sum(jnp.where(lo, 0.0, sq), axis=-1, keepdims=True)
        r = jnp.where(lo, lax.rsqrt(s_lo / half + EPS), lax.rsqrt(s_hi / half + EPS))
        return (y * r * w,)
    return fn


def fn_glu(a, b):
    return (a * jax.nn.sigmoid(b),)


def fn_ln_silu(vw, vh, cb, lw, lb):
    v = jnp.concatenate([vw, vh], axis=-1) + cb
    mu = jnp.mean(v, axis=-1, keepdims=True)
    var = jnp.mean(jnp.square(v - mu), axis=-1, keepdims=True)
    return (_silu((v - mu) * lax.rsqrt(var + EPS) * lw + lb),)


def _col_tile(width):
    return width // 3 if width % (3 * LANES) == 0 else width


def mod_fwd(a_rows, w_shard, b_shard):
    n, d = a_rows.shape
    ws = w_shard.shape[1]
    tn = _col_tile(ws)

    def body(a_ref, w_ref, b_ref, o_ref):
        a = _silu(a_ref[...]).astype(BF16)
        o_ref[...] = jnp.dot(a, w_ref[...].astype(BF16), preferred_element_type=F32) + b_ref[...]

    return pl.pallas_call(
        body, name="mod_fwd", grid=(ws // tn,), out_shape=jax.ShapeDtypeStruct((n, ws), F32),
        in_specs=[pl.BlockSpec((n, d), lambda j: (0, 0)), pl.BlockSpec((d, tn), lambda j: (0, j)),
                  pl.BlockSpec((1, tn), lambda j: (0, j))],
        out_specs=pl.BlockSpec((n, tn), lambda j: (0, j)), compiler_params=_params(),
    )(a_rows, w_shard, b_shard)


def mod_bwd(a_rows, d_shard, d_full, w_shard, ctx_rows):
    n, d = a_rows.shape
    ws = w_shard.shape[1]
    tn = _col_tile(ws)
    n_ct = ws // tn

    def body(a_ref, ds_ref, df_ref, w_ref, gw_ref, gb_ref, q_ref):
        j = pl.program_id(0)
        a = _silu(a_ref[...])
        ds = ds_ref[...]
        gw_ref[...] = lax.dot_general(a, ds, _DIMS["tn"], precision=HI, preferred_element_type=F32)
        dctx = ds[ctx_rows[0]:ctx_rows[0] + 1, :]
        for r in ctx_rows[1:]:
            dctx = dctx + ds[r:r + 1, :]
        q = lax.dot_general(jnp.broadcast_to(dctx, (8, tn)), w_ref[...], _DIMS["nt"], precision=HI,
                            preferred_element_type=F32)

        @pl.when(j == 0)
        def _():
            q_ref[...] = q
            df = df_ref[...]
            acc = df[0:1, :]
            for r in range(1, n):
                acc = acc + df[r:r + 1, :]
            gb_ref[...] = acc

        @pl.when(j > 0)
        def _():
            q_ref[...] += q

    return pl.pallas_call(
        body, name="mod_bwd", grid=(n_ct,),
        out_shape=(jax.ShapeDtypeStruct((d, ws), F32), jax.ShapeDtypeStruct((1, d_full.shape[1]), F32),
                   jax.ShapeDtypeStruct((8, d), F32)),
        in_specs=[pl.BlockSpec((n, d), lambda j: (0, 0)), pl.BlockSpec((n, tn), lambda j: (0, j)),
                  pl.BlockSpec(d_full.shape, lambda j: (0, 0)), pl.BlockSpec((d, tn), lambda j: (0, j))],
        out_specs=(pl.BlockSpec((d, tn), lambda j: (0, j)), pl.BlockSpec((1, d_full.shape[1]), lambda j: (0, 0)),
                   pl.BlockSpec((8, d), lambda j: (0, 0))),
        compiler_params=_params(40 << 20),
    )(a_rows, d_shard, d_full, w_shard)


def _shifted(xs, d, tok, width):
    if d == 0:
        return xs
    n = xs.shape[0]
    sh = pltpu.roll(xs, (-d) % n, axis=0)
    return jnp.where((tok + d >= 0) & (tok + d < width), sh, 0.0)


def _placed(out_shape, place):
    if place is None:
        return out_shape, 0, 0, None
    return place


def tapsum_roll(name, x, xcb, w, wcb, *, seq_len, n_seq, row_blk_off, width, piece, cb, ncb, pad, flip, place=None,
                out_dtype=F32):
    n_tap = w.shape[0]
    n_piece = seq_len // piece
    out_shape, o_rb, o_cb, into = _placed((n_seq * seq_len, ncb * cb), place)

    def body(x_ref, w_ref, *rest):
        o_ref = rest[-1]
        wv = w_ref[...]
        tok = lax.broadcasted_iota(jnp.int32, (piece, 1), 0) % width

        def do_piece(p, carry):
            start = pl.multiple_of(p * piece, piece)
            xs = x_ref[pl.ds(start, piece), :].astype(F32)
            acc = jnp.zeros_like(xs)
            for k in range(n_tap):
                d = pad - k if flip else k - pad
                acc = acc + wv[k:k + 1, :] * _shifted(xs, d, tok, width)
            o_ref[pl.ds(start, piece), :] = acc.astype(o_ref.dtype)
            return carry

        lax.fori_loop(0, n_piece, do_piece, 0)

    extra = [] if into is None else [into]
    return pl.pallas_call(
        body, name=name, grid=(ncb, n_seq), out_shape=_big(out_shape, out_dtype),
        in_specs=[pl.BlockSpec((seq_len, cb), lambda j, s: (row_blk_off + s, xcb + j)),
                  pl.BlockSpec((n_tap, cb), lambda j, s: (0, wcb + j))] + [pl.BlockSpec(memory_space=pl.ANY)] * len(extra),
        out_specs=pl.BlockSpec((seq_len, cb), lambda j, s: (o_rb + s, o_cb + j)),
        input_output_aliases={2: 0} if extra else {},
        compiler_params=_params(8 * seq_len * cb * 4 + (8 << 20), 2),
    )(*_in_hbm([x, w] + extra))


def tapgrad_roll(name, dy, dycb, dy_blk_off, x, xcb, x_blk_off, *, n_tap, seq_len, n_seq, width, piece, cb, ncb, pad):
    n_piece = seq_len // piece

    def body(dy_ref, x_ref, o_ref):
        @pl.when(pl.program_id(1) == 0)
        def _():
            o_ref[...] = jnp.zeros_like(o_ref)

        tok = lax.broadcasted_iota(jnp.int32, (piece, 1), 0) % width

        def do_piece(p, carry):
            start = pl.multiple_of(p * piece, piece)
            xs = x_ref[pl.ds(start, piece), :].astype(F32)
            dv = dy_ref[pl.ds(start, piece), :]
            for k in range(n_tap):
                o_ref[k:k + 1, :] += jnp.sum(dv * _shifted(xs, k - pad, tok, width), axis=0, keepdims=True)
            return carry

        lax.fori_loop(0, n_piece, do_piece, 0)

    return pl.pallas_call(
        body, name=name, grid=(ncb, n_seq), out_shape=jax.ShapeDtypeStruct((n_tap, ncb * cb), F32),
        in_specs=[pl.BlockSpec((seq_len, cb), lambda j, s: (dy_blk_off + s, dycb + j)),
                  pl.BlockSpec((seq_len, cb), lambda j, s: (x_blk_off + s, xcb + j))],
        out_specs=pl.BlockSpec((n_tap, cb), lambda j, s: (0, j)),
        compiler_params=_params(8 * seq_len * cb * 4 + (8 << 20), 2),
    )(*_in_hbm([dy, x]))


def tapsum_rows(name, x, xcb, w, wcb, *, seq_len, n_seq, cb, ncb, pad, flip, place=None):
    n_tap = w.shape[0]
    n_row = seq_len // GRID_W
    halo = pad * GRID_W
    out_shape, o_rb, o_cb, into = _placed((n_seq * seq_len, ncb * cb), place)

    def body(x_ref, w_ref, *rest):
        o_ref, xp = rest[-2:]
        xp[pl.ds(0, halo), :] = jnp.zeros((halo, cb), F32)
        xp[pl.ds(halo + seq_len, halo), :] = jnp.zeros((halo, cb), F32)
        xp[pl.ds(halo, seq_len), :] = x_ref[...].astype(F32)
        wv = w_ref[...]

        def do_row(r, carry):
            acc = jnp.zeros((GRID_W, cb), F32)
            for k in range(n_tap):
                d = pad - k if flip else k - pad
                acc = acc + wv[k:k + 1, :] * xp[pl.ds(pl.multiple_of((r + pad + d) * GRID_W, GRID_W), GRID_W), :]
            o_ref[pl.ds(pl.multiple_of(r * GRID_W, GRID_W), GRID_W), :] = acc
            return carry

        lax.fori_loop(0, n_row, do_row, 0)

    extra = [] if into is None else [into]
    return pl.pallas_call(
        body, name=name, grid=(ncb, n_seq), out_shape=_big(out_shape, F32),
        in_specs=[pl.BlockSpec((seq_len, cb), lambda j, s: (s, xcb + j)),
                  pl.BlockSpec((n_tap, cb), lambda j, s: (0, wcb + j))] + [pl.BlockSpec(memory_space=pl.ANY)] * len(extra),
        out_specs=pl.BlockSpec((seq_len, cb), lambda j, s: (o_rb + s, o_cb + j)),
        input_output_aliases={2: 0} if extra else {},
        scratch_shapes=[pltpu.VMEM((seq_len + 2 * halo, cb), F32)],
        compiler_params=_params(10 * seq_len * cb * 4 + (8 << 20), 2),
    )(*_in_hbm([x, w] + extra))


def tapgrad_rows(name, dy, dycb, x, xcb, *, n_tap, seq_len, n_seq, cb, ncb, pad):
    n_row = seq_len // GRID_W
    halo = pad * GRID_W

    def body(dy_ref, x_ref, o_ref, xp):
        @pl.when(pl.program_id(1) == 0)
        def _():
            o_ref[...] = jnp.zeros_like(o_ref)

        xp[pl.ds(0, halo), :] = jnp.zeros((halo, cb), F32)
        xp[pl.ds(halo + seq_len, halo), :] = jnp.zeros((halo, cb), F32)
        xp[pl.ds(halo, seq_len), :] = x_ref[...].astype(F32)

        def do_row(r, carry):
            dv = dy_ref[pl.ds(pl.multiple_of(r * GRID_W, GRID_W), GRID_W), :]
            for k in range(n_tap):
                xs = xp[pl.ds(pl.multiple_of((r + k) * GRID_W, GRID_W), GRID_W), :]
                o_ref[k:k + 1, :] += jnp.sum(dv * xs, axis=0, keepdims=True)
            return carry

        lax.fori_loop(0, n_row, do_row, 0)

    return pl.pallas_call(
        body, name=name, grid=(ncb, n_seq), out_shape=jax.ShapeDtypeStruct((n_tap, ncb * cb), F32),
        in_specs=[pl.BlockSpec((seq_len, cb), lambda j, s: (s, dycb + j)),
                  pl.BlockSpec((seq_len, cb), lambda j, s: (s, xcb + j))],
        out_specs=pl.BlockSpec((n_tap, cb), lambda j, s: (0, j)),
        scratch_shapes=[pltpu.VMEM((seq_len + 2 * halo, cb), F32)],
        compiler_params=_params(10 * seq_len * cb * 4 + (8 << 20), 2),
    )(*_in_hbm([dy, x]))


def _ssd_blocks(b, s, *, rev, n_ctx, n_lat, lat_blocks):
    if rev:
        return jnp.where(s < n_ctx, lat_blocks + b * n_ctx + (n_ctx - 1 - s), b * n_lat + (n_lat - 1 - (s - n_ctx)))
    return jnp.where(s < n_ctx, lat_blocks + b * n_ctx + s, b * n_lat + (s - n_ctx))


def _ssd_common(xbc, raw, dtb, alog, dsk, *, rev, ds, n_head):
    if rev:
        raw = pltpu.roll(raw, LANES - n_head, axis=1)
    pre = raw + dtb
    dt = jnp.maximum(pre, 0.0) + jnp.log(1.0 + jnp.exp(-jnp.abs(pre)))
    sig = jax.nn.sigmoid(pre)
    a = -jnp.exp(alog)
    da = dt * a
    ri = lax.broadcasted_iota(jnp.int32, (CHUNK, CHUNK), 0)
    ci = lax.broadcasted_iota(jnp.int32, (CHUNK, CHUNK), 1)
    mask = (ci >= ri) if rev else (ci <= ri)
    tri = mask.astype(F32)
    tri_t = ((ci <= ri) if rev else (ci >= ri)).astype(F32)
    cs = jnp.dot(tri, da, precision=HI, preferred_element_type=F32)
    tot = jnp.sum(da, axis=0, keepdims=True)
    def wide(v):
        first = lax.broadcasted_iota(jnp.int32, (v.shape[0], LANES), 1) < HEAD_DIM
        return jnp.concatenate(
            [jnp.where(first, jnp.broadcast_to(v[:, 2 * p:2 * p + 1], first.shape),
                       jnp.broadcast_to(v[:, 2 * p + 1:2 * p + 2], first.shape)) for p in range(n_head // 2)], axis=1)

    cs_w, tot_w = wide(cs), wide(tot)
    xh = xbc[:, :ds]
    dt_w = wide(dt)
    return dict(
        dt=dt, sig=sig, a=a, cs=cs, cs_t=cs.T, tot=tot, mask=mask, tri_t=tri_t,
        e_w=jnp.exp(cs_w), wt_w=jnp.exp(tot_w - cs_w), dec_w=jnp.exp(tot_w), dt_w=dt_w, dsk_w=wide(dsk),
        xh=xh, xs_w=xh * dt_w, bm=xbc[:, ds:ds + 2 * N_STATE], cm=xbc[:, ds + 2 * N_STATE:ds + 4 * N_STATE])


def _decay(q, col):
    seg = q["cs"][:, col:col + 1] - q["cs_t"][col:col + 1, :]
    return jnp.exp(jnp.where(q["mask"], seg, -jnp.inf))


def _split_heads(v):
    lane = lax.broadcasted_iota(jnp.int32, v.shape, 1)
    return jnp.concatenate([jnp.where(lane < HEAD_DIM, v, 0.0), jnp.where(lane >= HEAD_DIM, v, 0.0)], axis=0)


def ssd_fwd(name, xbc, dt_raw, dt_cb, dtb, alog, dsk, *, rev, n_ex, seq_len, ctx_len, ds, rider=None, add=None):
    n_head, half = ds // HEAD_DIM, ds // 2
    n_ctx, n_lat = ctx_len // CHUNK, seq_len // CHUNK
    n_step = n_ctx + n_lat
    blk = functools.partial(_ssd_blocks, rev=rev, n_ctx=n_ctx, n_lat=n_lat, lat_blocks=n_ex * n_lat)
    xw = xbc.shape[1]

    def y_blk(b, s):
        sl = jnp.maximum(s, n_ctx) - n_ctx
        return b * n_lat + ((n_lat - 1 - sl) if rev else sl)

    hosted = _Hosted(rider, 5 + (add is not None), 2, 1, (n_ex, n_step))

    def body(*refs):
        (xbc_ref, dt_ref, dtb_ref, alog_ref, dsk_ref, *add_ref), (y_ref, hs_ref), (h_scr,) = hosted.split(refs)

        @pl.when(pl.program_id(1) == 0)
        def _():
            h_scr[...] = jnp.zeros_like(h_scr)

        q = _ssd_common(xbc_ref[...], dt_ref[...], dtb_ref[...], alog_ref[...], dsk_ref[...], rev=rev, ds=ds, n_head=n_head)
        h = h_scr[...]
        hs_ref[...] = h
        for g in range(2):
            lo = g * half
            bg = q["bm"][:, g * N_STATE:(g + 1) * N_STATE].astype(BF16)
            cg = q["cm"][:, g * N_STATE:(g + 1) * N_STATE].astype(BF16)
            scores = lax.dot_general(cg, bg, _DIMS["nt"], preferred_element_type=F32)
            hg = h[:, lo:lo + half]
            off = jnp.dot(cg, hg.astype(BF16), preferred_element_type=F32)
            for j in range(half // LANES):
                c0 = (lo + j * LANES) // HEAD_DIM
                ln = slice(lo + j * LANES, lo + (j + 1) * LANES)
                p_cat = jnp.concatenate([scores * _decay(q, c0), scores * _decay(q, c0 + 1)], axis=1).astype(BF16)
                diag = jnp.dot(p_cat, _split_heads(q["xs_w"][:, ln]).astype(BF16), preferred_element_type=F32)
                y_ref[:, ln] = (diag + q["e_w"][:, ln] * off[:, j * LANES:(j + 1) * LANES]
                                + q["dsk_w"][:, ln] * q["xh"][:, ln] + (add_ref[0][:, ln] if add_ref else 0.0))
            v = (q["wt_w"][:, lo:lo + half] * q["xs_w"][:, lo:lo + half]).astype(BF16)
            h_scr[:, lo:lo + half] = (q["dec_w"][:, lo:lo + half] * hg
                                      + lax.dot_general(bg, v, _DIMS["tn"], preferred_element_type=F32))
        hosted.finish()

    vec = pl.BlockSpec((1, LANES), lambda b, s: (0, 0))
    in_specs, out_shape, out_specs, scratch, args = hosted.call_args(
        [pl.BlockSpec((CHUNK, xw), lambda b, s: (blk(b, s), 0)),
         pl.BlockSpec((CHUNK, LANES), lambda b, s: (blk(b, s), dt_cb)), vec, vec, vec]
        + [pl.BlockSpec((CHUNK, ds), lambda b, s: (y_blk(b, s), 0))] * (add is not None),
        (_big((n_ex * seq_len, ds), F32), _big((n_ex, n_step, N_STATE, ds), F32)),
        (pl.BlockSpec((CHUNK, ds), lambda b, s: (y_blk(b, s), 0)),
         pl.BlockSpec((None, None, N_STATE, ds), lambda b, s: (b, s, 0, 0))),
        [pltpu.VMEM((N_STATE, ds), F32)], [xbc, dt_raw, dtb, alog, dsk] + ([] if add is None else [add]))
    return hosted.results(pl.pallas_call(
        body, name=name, grid=(n_ex, n_step), out_shape=out_shape, in_specs=in_specs, out_specs=out_specs,
        scratch_shapes=scratch, compiler_params=_params(40 << 20, 2),
    )(*_in_hbm(args)))


def ssd_bwd(name, xbc, dt_raw, dt_cb, hs, dy, dtb, alog, dsk, *, rev, n_ex, seq_len, ctx_len, ds, rider=None, add=None):
    n_head, half = ds // HEAD_DIM, ds // 2
    n_ctx, n_lat = ctx_len // CHUNK, seq_len // CHUNK
    n_step = n_ctx + n_lat
    n_tok = n_ex * (seq_len + ctx_len)
    blk0 = functools.partial(_ssd_blocks, rev=rev, n_ctx=n_ctx, n_lat=n_lat, lat_blocks=n_ex * n_lat)
    step = lambda sp: n_step - 1 - sp
    blk = lambda b, sp: blk0(b, step(sp))
    xw = xbc.shape[1]

    def dy_blk(b, sp):
        sl = jnp.maximum(step(sp), n_ctx) - n_ctx
        return b * n_lat + ((n_lat - 1 - sl) if rev else sl)

    hosted = _Hosted(rider, 7 + (add is not None), 5, 1, (n_ex, n_step))

    def body(*refs):
        ((xbc_ref, dt_ref, hs_ref, dy_ref, dtb_ref, alog_ref, dsk_ref, *add_ref),
         (dxbc_ref, ddt_ref, dalog_ref, ddtb_ref, ddsk_ref), (dh_scr,)) = hosted.split(refs)
        b, sp = pl.program_id(0), pl.program_id(1)
        more = (lambda cols: add_ref[0][:, cols]) if add_ref else (lambda cols: 0.0)

        @pl.when(sp == 0)
        def _():
            dh_scr[...] = jnp.zeros_like(dh_scr)

        @pl.when((sp == 0) & (b == 0))
        def _():
            dalog_ref[...] = jnp.zeros_like(dalog_ref)
            ddtb_ref[...] = jnp.zeros_like(ddtb_ref)
            ddsk_ref[...] = jnp.zeros_like(ddsk_ref)

        q = _ssd_common(xbc_ref[...], dt_ref[...], dtb_ref[...], alog_ref[...], dsk_ref[...], rev=rev, ds=ds, n_head=n_head)
        h = hs_ref[...]
        d_y = jnp.where(step(sp) >= n_ctx, dy_ref[...], 0.0)
        dh_next = dh_scr[...]
        lane_row = lax.broadcasted_iota(jnp.int32, (1, LANES), 1)
        d_cs = jnp.zeros((CHUNK, LANES), F32)
        dxs_parts, de_parts, dwt_parts, ddec_parts = [], [], [], []
        for g in range(2):
            lo = g * half
            gs = slice(lo, lo + half)
            bg = q["bm"][:, g * N_STATE:(g + 1) * N_STATE].astype(BF16)
            cg = q["cm"][:, g * N_STATE:(g + 1) * N_STATE].astype(BF16)
            scores = lax.dot_general(cg, bg, _DIMS["nt"], preferred_element_type=F32)
            hg, dyg, dhn = h[:, gs], d_y[:, gs], dh_next[:, gs]
            off = jnp.dot(cg, hg.astype(BF16), preferred_element_type=F32)
            d_off = (q["e_w"][:, gs] * dyg).astype(BF16)
            de_parts.append(dyg * off)
            d_c = lax.dot_general(d_off, hg.astype(BF16), _DIMS["nt"], preferred_element_type=F32)
            dh_scr[:, gs] = (lax.dot_general(cg, d_off, _DIMS["tn"], preferred_element_type=F32)
                             + q["dec_w"][:, gs] * dhn)
            b_dh = jnp.dot(bg, dhn.astype(BF16), preferred_element_type=F32)
            v = q["wt_w"][:, gs] * q["xs_w"][:, gs]
            d_b = lax.dot_general(v.astype(BF16), dhn.astype(BF16), _DIMS["nt"], preferred_element_type=F32)
            dwt_parts.append(q["xs_w"][:, gs] * b_dh)
            ddec_parts.append(jnp.sum(hg * dhn, axis=0, keepdims=True))
            d_scores = jnp.zeros((CHUNK, CHUNK), F32)
            for j in range(half // LANES):
                c0 = (lo + j * LANES) // HEAD_DIM
                ln = slice(lo + j * LANES, lo + (j + 1) * LANES)
                l0, l1 = _decay(q, c0), _decay(q, c0 + 1)
                p0, p1 = scores * l0, scores * l1
                dy_st = _split_heads(d_y[:, ln]).astype(BF16)
                d_p = lax.dot_general(dy_st, q["xs_w"][:, ln].astype(BF16), _DIMS["nt"], preferred_element_type=F32)
                d_p0, d_p1 = d_p[:CHUNK], d_p[CHUNK:]
                d_scores = d_scores + d_p0 * l0 + d_p1 * l1
                for col, t in ((c0, d_p0 * p0), (c0 + 1, d_p1 * p1)):
                    d_cs = d_cs + jnp.sum(t - t.T, axis=1, keepdims=True) * (lane_row == col).astype(F32)
                p_st = jnp.concatenate([p0, p1], axis=0).astype(BF16)
                dxs_parts.append(lax.dot_general(p_st, dy_st, _DIMS["tn"], preferred_element_type=F32)
                                 + q["wt_w"][:, ln] * b_dh[:, j * LANES:(j + 1) * LANES])
            d_sc = d_scores.astype(BF16)
            d_c = d_c + jnp.dot(d_sc, bg, preferred_element_type=F32)
            d_b = d_b + lax.dot_general(d_sc, cg, _DIMS["tn"], preferred_element_type=F32)
            b_cols, c_cols = slice(ds + g * N_STATE, ds + (g + 1) * N_STATE), slice(ds + (2 + g) * N_STATE, ds + (3 + g) * N_STATE)
            dxbc_ref[:, b_cols] = d_b + more(b_cols)
            dxbc_ref[:, c_cols] = d_c + more(c_cols)
        d_xs = jnp.concatenate(dxs_parts, axis=1)
        narrow_m = (lax.broadcasted_iota(jnp.int32, (ds, LANES), 0) // HEAD_DIM
                    == lax.broadcasted_iota(jnp.int32, (ds, LANES), 1)).astype(BF16)
        rows8 = lambda v: jnp.broadcast_to(v, (8, ds))
        stacked = jnp.concatenate(
            [jnp.concatenate(dwt_parts, axis=1), jnp.concatenate(de_parts, axis=1), d_xs * q["xh"],
             rows8(jnp.concatenate(ddec_parts, axis=1)), rows8(jnp.sum(d_y * q["xh"], axis=0, keepdims=True))], axis=0)
        sums = jnp.dot(stacked.astype(BF16), narrow_m, preferred_element_type=F32)
        n_wt, n_e, n_xs = sums[:CHUNK], sums[CHUNK:2 * CHUNK], sums[2 * CHUNK:3 * CHUNK]
        n_dec, n_dsk = sums[3 * CHUNK:3 * CHUNK + 1], sums[3 * CHUNK + 8:3 * CHUNK + 9]
        e, wt, dec = jnp.exp(q["cs"]), jnp.exp(q["tot"] - q["cs"]), jnp.exp(q["tot"])
        d_wt = n_wt * wt
        d_cs = d_cs + n_e * e - d_wt
        d_tot = jnp.sum(d_wt, axis=0, keepdims=True) + n_dec * dec
        d_da = jnp.dot(q["tri_t"], d_cs, precision=HI, preferred_element_type=F32) + d_tot
        d_dt = d_da * q["a"] + n_xs
        dxbc_ref[:, :ds] = d_xs * q["dt_w"] + q["dsk_w"] * d_y + more(slice(0, ds))
        dalog_ref[...] += jnp.sum(d_da * q["dt"], axis=0, keepdims=True) * q["a"]
        d_raw = d_dt * q["sig"]
        ddtb_ref[...] += jnp.sum(d_raw, axis=0, keepdims=True)
        ddsk_ref[...] += n_dsk
        ddt_ref[...] = pltpu.roll(d_raw, n_head, axis=1) if rev else d_raw
        hosted.finish()

    vec = pl.BlockSpec((1, LANES), lambda b, s: (0, 0))
    vec_shape = jax.ShapeDtypeStruct((1, LANES), F32)
    in_specs, out_shape, out_specs, scratch, args = hosted.call_args(
        [pl.BlockSpec((CHUNK, xw), lambda b, s: (blk(b, s), 0)),
         pl.BlockSpec((CHUNK, LANES), lambda b, s: (blk(b, s), dt_cb)),
         pl.BlockSpec((None, None, N_STATE, ds), lambda b, s: (b, step(s), 0, 0)),
         pl.BlockSpec((CHUNK, ds), lambda b, s: (dy_blk(b, s), 0)), vec, vec, vec]
        + [pl.BlockSpec((CHUNK, xw), lambda b, s: (blk(b, s), 0))] * (add is not None),
        (_big((n_tok, xw), F32), _big((n_tok, LANES), F32), vec_shape, vec_shape, vec_shape),
        (pl.BlockSpec((CHUNK, xw), lambda b, s: (blk(b, s), 0)),
         pl.BlockSpec((CHUNK, LANES), lambda b, s: (blk(b, s), 0)), vec, vec, vec),
        [pltpu.VMEM((N_STATE, ds), F32)], [xbc, dt_raw, hs, dy, dtb, alog, dsk] + ([] if add is None else [add]))
    return hosted.results(pl.pallas_call(
        body, name=name, grid=(n_ex, n_step), out_shape=out_shape, in_specs=in_specs, out_specs=out_specs,
        scratch_shapes=scratch, compiler_params=_params(48 << 20, 2),
    )(*_in_hbm(args)))


def final_loss(x2, f, gate, target, w, *, tm, seg_fn):
    n, d = x2.shape

    def body(x_ref, f_ref, g_ref, t_ref, w_ref, dx_ref, dw_ref, loss_ref):
        i = pl.program_id(0)
        t = t_ref[...]
        x3 = make_fn_resid(0.5)(x_ref[...], f_ref[...], g_ref[...])[0]

        def per_feature(xv, wv):
            err = _rms(xv, wv) - t
            return 0.5 * jnp.sum(err * err, axis=0, keepdims=True) / d

        lv, vjp = jax.vjp(per_feature, x3, w_ref[...])
        dx, dw = vjp(jnp.ones_like(lv))
        dx_ref[...] = dx

        @pl.when(i == 0)
        def _():
            dw_ref[...] = dw
            loss_ref[...] = lv

        @pl.when(i > 0)
        def _():
            dw_ref[...] += dw
            loss_ref[...] += lv

    tile = pl.BlockSpec((tm, d), lambda i: (i, 0))
    vec = pl.BlockSpec((1, d), lambda i: (0, 0))
    return pl.pallas_call(
        body, name="final_loss", grid=(n // tm,),
        in_specs=[tile, tile, pl.BlockSpec((None, 1, d), lambda i: (seg_fn(i), 0, 0)), tile, vec],
        out_shape=(jax.ShapeDtypeStruct((n, d), F32), jax.ShapeDtypeStruct((1, d), F32), jax.ShapeDtypeStruct((1, d), F32)),
        out_specs=(tile, vec, vec), compiler_params=_params(tm * d * 4 * 20 + (8 << 20)),
    )(x2, f, gate, target, w)


def sum_slots(name, arr, out_dtype=F32):
    n_slot, n_row, width = arr.shape
    tm = _row_tile(n_row, width * n_slot, cap_bytes=STREAM_TILE_BYTES * 14, mult=16)
    vmem = 2 * n_slot * tm * width * arr.dtype.itemsize + 4 * tm * width * 4

    def body(a_ref, o_ref):
        acc = a_ref[0].astype(F32)
        for j in range(1, n_slot):
            acc = acc + a_ref[j].astype(F32)
        o_ref[...] = acc.astype(o_ref.dtype)

    return pl.pallas_call(
        body, name=name, grid=(n_row // tm,), out_shape=jax.ShapeDtypeStruct((n_row, width), out_dtype),
        in_specs=[pl.BlockSpec((n_slot, tm, width), lambda i: (0, i, 0))],
        out_specs=pl.BlockSpec((tm, width), lambda i: (i, 0)), compiler_params=_params(vmem + (4 << 20)),
    )(arr)


def adamw(name, w, g_slots, m, v):
    n_slot, n_row, width = g_slots.shape
    tm = _row_tile(n_row, width, cap_bytes=STREAM_TILE_BYTES)
    if g_slots.dtype == BF16 and tm % 16:
        tm16 = _row_tile(n_row, width, cap_bytes=STREAM_TILE_BYTES, mult=16)
        if tm16 % 16 == 0:
            tm = tm16
        else:
            g_slots = g_slots.astype(F32)

    def body(w_ref, g_ref, m_ref, v_ref, go_ref, d_ref, mo_ref, vo_ref):
        g = g_ref[0].astype(F32)
        for j in range(1, n_slot):
            g = g + g_ref[j].astype(F32)
        m2 = ADAM_B1 * m_ref[...] + (1.0 - ADAM_B1) * g
        v2 = ADAM_B2 * v_ref[...] + (1.0 - ADAM_B2) * jnp.square(g)
        m_hat = m2 / (1.0 - ADAM_B1 ** ADAM_STEP)
        v_hat = v2 / (1.0 - ADAM_B2 ** ADAM_STEP)
        go_ref[...] = g
        d_ref[...] = -ADAM_LR * (m_hat / (jnp.sqrt(v_hat) + ADAM_EPS) + ADAM_WD * w_ref[...])
        mo_ref[...] = m2
        vo_ref[...] = v2

    tile = pl.BlockSpec((tm, width), lambda i: (i, 0))
    shape = jax.ShapeDtypeStruct((n_row, width), F32)
    return pl.pallas_call(
        body, name=name, grid=(n_row // tm,), out_shape=(shape,) * 4,
        in_specs=[tile, pl.BlockSpec((n_slot, tm, width), lambda i: (0, i, 0)), tile, tile],
        out_specs=(tile,) * 4, compiler_params=_params(2 * (7 + n_slot) * tm * width * 4 + (4 << 20)),
    )(w, g_slots, m, v)


def cctx_grad(q_all, c_ctx_row):
    d = c_ctx_row.shape[1]

    def body(q_ref, c_ref, o_ref):
        acc = q_ref[0, 0:1, :]
        for j in (2, 4, 6):
            acc = acc + q_ref[j, 0:1, :]
        _, vjp = jax.vjp(_silu, c_ref[...])
        o_ref[...] = vjp(acc)[0]

    return pl.pallas_call(
        body, name="cctx_grad", out_shape=jax.ShapeDtypeStruct((1, d), F32),
    )(q_all, c_ctx_row)


def loss_total(pack_sum, d):
    def body(p_ref, o_ref):
        o_ref[...] = jnp.sum(p_ref[:, 0:d], axis=1, keepdims=True)

    return pl.pallas_call(
        body, name="loss_total", out_shape=jax.ShapeDtypeStruct((1, 1), F32),
    )(pack_sum)


class _Plan:
    def __init__(self):
        self.builders, self.got = {}, {}

    def on(self, host, key, builder):
        self.builders.setdefault(host, []).append((key, builder))

    def run(self, host, fn, *args, **kw):
        if host not in self.builders:
            return fn(host, *args, **kw)
        keys, riders = zip(*[(key, builder(self)) for key, builder in self.builders[host]])
        res, landed = fn(host, *args, rider=Riders(riders), **kw)
        for key, r in zip(keys, riders):
            self.got[key], landed = landed[:r.n], landed[r.n:]
        return res


def _val(w):
    return w() if callable(w) else w


def _matmul_tile(n_rows, tm):
    return 2 * tm if n_rows % (2 * tm) == 0 else tm


def _ffn_fwd(plan, tag, xin, n_rows, tm, seg_fn, shift, scale, gate, norm_w, wg, wu, wd, fuse_gate_up=False,
             with_resid=True, h=None):
    d = xin[1]
    n_tiles = n_rows // tm
    if h is None:
        (h,) = plan.run(f"{tag}_norm", rowwise, fn_norm_mod, [xin], [shift, scale], [norm_w], [(n_rows, d, BF16)],
                        tm=tm, n_tiles=n_tiles, seg_fn=seg_fn)
    tmm = _matmul_tile(n_rows, tm)
    if fuse_gate_up:
        g, u, act = plan.run(f"{tag}_gate_up", matmul, [(h, _val(wg)), (h, _val(wu))], "nn", b_ch=True, out_ch=True,
                             tm=min(tm, 256), fold=True,
                             post=([], lambda ag, au: (ag, au, fn_act(ag, au)[0]), [BF16, BF16, BF16]))
    else:
        g = plan.run(f"{tag}_gate", matmul, [(h, _val(wg))], "nn", out_dtype=BF16, b_ch=True, out_ch=True, tm=tmm)
        u, act = plan.run(f"{tag}_up", matmul, [(h, _val(wu))], "nn", b_ch=True, out_ch=True, tm=tm, fold=True,
                          post=([g], lambda acc, gv: (acc, fn_act(gv, acc)[0]), [BF16, BF16]))
    f = plan.run(f"{tag}_down", matmul, [(act, _val(wd))], "nn", a_ch=True, b_ch=True, tm=tmm, fold=True)
    if not with_resid:
        return None, (h, g, u, act, f)
    (xo,) = plan.run(f"{tag}_resid", rowwise, make_fn_resid(0.5), [xin, row(f)], [gate], [], [(n_rows, d, F32)],
                     tm=tm, n_tiles=n_tiles, seg_fn=seg_fn)
    return xo, (h, g, u, act, f)


def _ffn_bwd(plan, tag, d_xo, saved, xin, n_rows, tm, seg_fn, first_fn, shift, scale, gate, norm_w, wg, wu, wd, dx_rows, dx_limit):
    h, g, u, act, f = saved
    d = xin[1]
    n_tiles = n_rows // tm
    n_ch, _, n_hid = g.shape
    d_f, d_gate = plan.run(f"{tag}_resid_bwd", rowwise_bwd, make_fn_resid(0.5), [xin, row(f)], [gate], [], [[row(d_xo)]],
                           [None, (n_rows, BF16, None)], tm=tm, n_tiles=n_tiles, seg_fn=seg_fn, first_fn=first_fn)
    tmm = _matmul_tile(n_rows, tm)
    def act_vjp(d_act, gv, uv):
        s = jax.nn.sigmoid(gv)
        gs = gv * s
        return d_act * uv * (s + gs * (1.0 - s)), d_act * gs
    d_g, d_u = plan.run(f"{tag}_down_dx", matmul, [(d_f, wd)], "nt", b_ch=True, out_ch=True, tm=tmm,
                        post=([g, u], act_vjp, [BF16, BF16]))
    plan.got[f"{tag}_d_wd"] = plan.run(f"{tag}_down_dw", matmul, [(act, d_f)], "tn", out_dtype=BF16, a_ch=True, out_ch=True, tm=tmm)
    d_h = plan.run(f"{tag}_up_dx", matmul, [(d_g, wg), (d_u, wu)], "nt", a_ch=True, b_ch=True, tm=tmm)
    plan.got[f"{tag}_d_wg"] = plan.run(f"{tag}_gate_dw", matmul, [(d_g, h)], "tn", out_dtype=BF16, a_ch=True, out_ch=True, tm=tmm)
    plan.got[f"{tag}_d_wu"] = plan.run(f"{tag}_up_dw", matmul, [(d_u, h)], "tn", out_dtype=BF16, a_ch=True, out_ch=True, tm=tmm)
    d_x, d_shift, d_scale, d_nw = plan.run(
        f"{tag}_norm_bwd", rowwise_bwd, fn_norm_mod, [xin], [shift, scale], [norm_w], [[row(d_h)]], [(dx_rows, F32, dx_limit)],
        tm=tm, n_tiles=n_tiles, seg_fn=seg_fn, first_fn=first_fn, adds={0: (row(d_xo), None)})
    return d_x, (d_shift, d_scale, d_gate), d_nw


def kernel(x, c, ctx, c_ctx, w_mod, b_mod, norm_ffn1, ffn1_gate, ffn1_up, ffn1_down, norm_mix, w_in, ssm_conv_w, ssm_conv_b, dt_bias_fwd, dt_bias_bwd, a_log_fwd, a_log_bwd, ssm_d, ssm_norm_w, cconv_w, cconv_b, cconv_ln_w, cconv_ln_b, w_out, norm_ffn2, ffn2_gate, ffn2_up, ffn2_down, final_norm, loss_target, m_c_ctx, m_w_mod, m_b_mod, m_norm_ffn1, m_ffn1_gate, m_ffn1_up, m_ffn1_down, m_norm_mix, m_w_in, m_ssm_conv_w, m_ssm_conv_b, m_dt_bias_fwd, m_dt_bias_bwd, m_a_log_fwd, m_a_log_bwd, m_ssm_d, m_ssm_norm_w, m_cconv_w, m_cconv_b, m_cconv_ln_w, m_cconv_ln_b, m_w_out, m_norm_ffn2, m_ffn2_gate, m_ffn2_up, m_ffn2_down, m_final_norm, v_c_ctx, v_w_mod, v_b_mod, v_norm_ffn1, v_ffn1_gate, v_ffn1_up, v_ffn1_down, v_norm_mix, v_w_in, v_ssm_conv_w, v_ssm_conv_b, v_dt_bias_fwd, v_dt_bias_bwd, v_a_log_fwd, v_a_log_bwd, v_ssm_d, v_ssm_norm_w, v_cconv_w, v_cconv_b, v_cconv_ln_w, v_cconv_ln_b, v_w_out, v_norm_ffn2, v_ffn2_gate, v_ffn2_up, v_ffn2_down, v_final_norm):
    weights = dict(c_ctx=c_ctx, w_mod=w_mod, b_mod=b_mod, norm_ffn1=norm_ffn1, ffn1_gate=ffn1_gate, ffn1_up=ffn1_up, ffn1_down=ffn1_down, norm_mix=norm_mix, w_in=w_in, ssm_conv_w=ssm_conv_w, ssm_conv_b=ssm_conv_b, dt_bias_fwd=dt_bias_fwd, dt_bias_bwd=dt_bias_bwd, a_log_fwd=a_log_fwd, a_log_bwd=a_log_bwd, ssm_d=ssm_d, ssm_norm_w=ssm_norm_w, cconv_w=cconv_w, cconv_b=cconv_b, cconv_ln_w=cconv_ln_w, cconv_ln_b=cconv_ln_b, w_out=w_out, norm_ffn2=norm_ffn2, ffn2_gate=ffn2_gate, ffn2_up=ffn2_up, ffn2_down=ffn2_down, final_norm=final_norm)
    mom1 = dict(c_ctx=m_c_ctx, w_mod=m_w_mod, b_mod=m_b_mod, norm_ffn1=m_norm_ffn1, ffn1_gate=m_ffn1_gate, ffn1_up=m_ffn1_up, ffn1_down=m_ffn1_down, norm_mix=m_norm_mix, w_in=m_w_in, ssm_conv_w=m_ssm_conv_w, ssm_conv_b=m_ssm_conv_b, dt_bias_fwd=m_dt_bias_fwd, dt_bias_bwd=m_dt_bias_bwd, a_log_fwd=m_a_log_fwd, a_log_bwd=m_a_log_bwd, ssm_d=m_ssm_d, ssm_norm_w=m_ssm_norm_w, cconv_w=m_cconv_w, cconv_b=m_cconv_b, cconv_ln_w=m_cconv_ln_w, cconv_ln_b=m_cconv_ln_b, w_out=m_w_out, norm_ffn2=m_norm_ffn2, ffn2_gate=m_ffn2_gate, ffn2_up=m_ffn2_up, ffn2_down=m_ffn2_down, final_norm=m_final_norm)
    mom2 = dict(c_ctx=v_c_ctx, w_mod=v_w_mod, b_mod=v_b_mod, norm_ffn1=v_norm_ffn1, ffn1_gate=v_ffn1_gate, ffn1_up=v_ffn1_up, ffn1_down=v_ffn1_down, norm_mix=v_norm_mix, w_in=v_w_in, ssm_conv_w=v_ssm_conv_w, ssm_conv_b=v_ssm_conv_b, dt_bias_fwd=v_dt_bias_fwd, dt_bias_bwd=v_dt_bias_bwd, a_log_fwd=v_a_log_fwd, a_log_bwd=v_a_log_bwd, ssm_d=v_ssm_d, ssm_norm_w=v_ssm_norm_w, cconv_w=v_cconv_w, cconv_b=v_cconv_b, cconv_ln_w=v_cconv_ln_w, cconv_ln_b=v_cconv_ln_b, w_out=v_w_out, norm_ffn2=v_norm_ffn2, ffn2_gate=v_ffn2_gate, ffn2_up=v_ffn2_up, ffn2_down=v_ffn2_down, final_norm=v_final_norm)
    order = list(weights)

    n_ex, seq_len, d = x.shape
    ctx_len = ctx.shape[1]
    ds = d
    n_head = ds // HEAD_DIM
    xw = ds + 4 * N_STATE
    n_lat, n_ctx_rows = n_ex * seq_len, n_ex * ctx_len
    n_tok = n_lat + n_ctx_rows
    tm = math.gcd(math.gcd(512, seq_len), n_ctx_rows)
    seg_all, first_all = _segmenter(tm, seq_len, n_lat)
    lat_tiles = n_lat // tm

    xi, yi, ci = lax.axis_index("x"), lax.axis_index("y"), lax.axis_index("c")
    me, chip = 4 * xi + 2 * yi + ci, 2 * xi + yi

    (c_all,) = exchange("gather_c", [c], "all8")
    n_all = 8 * n_ex
    n_cond = -(-(n_all + 1) // 8) * 8
    cond = jnp.concatenate([c_all.reshape(n_all, d), c_ctx[None, :], jnp.zeros((n_cond - n_all - 1, d), F32)])
    mod_w = w_mod.shape[2]
    b_shard = lax.dynamic_slice(b_mod, (0, chip * mod_w), (1, mod_w))
    (mod_g,) = exchange("gather_mod", [mod_fwd(cond, w_mod[0], b_shard)], "chips")
    mod_full = mod_g.transpose(1, 0, 2).reshape(n_cond, N_CHIPS * mod_w)
    mod_mine = lax.dynamic_slice(mod_full, (me * n_ex, 0), (n_ex, 9 * d)).reshape(n_ex, 9, d)
    mod_ctx = mod_full[n_all].reshape(9, d)
    tabs = [jnp.concatenate([mod_mine[:, j], mod_ctx[j][None]])[:, None, :] for j in range(9)]
    lat = lambda t: t[:n_ex]

    bf = lambda w: w[0].astype(BF16)
    plan = _Plan()
    gather = lambda *ws: (lambda p: Rider(list(ws), "chips"))
    plan.on("ffn1_norm", "wg1", gather(bf(ffn1_gate)))
    plan.on("ffn1_gate", "wu1", gather(bf(ffn1_up)))
    plan.on("ffn1_up", "wd1", gather(bf(ffn1_down)))
    cut_a, cut_b = d * 5 // 8, d * 7 // 8
    plan.on("ffn1_down", "win_a", gather(bf(w_in)[:cut_a]))
    plan.on("ffn1_resid", "win_b", gather(bf(w_in)[cut_a:cut_b], ssm_conv_w[0], cconv_w[0]))
    xt = two_rows(x.reshape(n_lat, d), ctx.reshape(n_ctx_rows, d), lat_tiles)
    x1, saved1 = _ffn_fwd(plan, "ffn1", xt, n_tok, tm, seg_all, tabs[0], tabs[1], tabs[2], norm_ffn1,
                          lambda: plan.got["wg1"][0], lambda: plan.got["wu1"][0], lambda: plan.got["wd1"][0])
    (wg1,), (wu1,), (wd1,), (win_a,), (win_b, w5_g, w31_g) = (plan.got[k] for k in ("wg1", "wu1", "wd1", "win_a", "win_b"))
    (h2,), (win_c,) = rowwise("mix_norm", fn_norm_mod, [row(x1)], [tabs[3], tabs[4]], [norm_mix], [(n_tok, d, BF16)],
                              tm=tm, n_tiles=n_tok // tm, seg_fn=seg_all, rider=Rider([bf(w_in)[cut_b:]], "chips"))
    win_g = jnp.concatenate([win_a, win_b, win_c], axis=1)
    unshard_cols = lambda t: t.transpose(1, 0, 2).reshape(t.shape[1], N_CHIPS * t.shape[2])
    win = unshard_cols(win_g)
    o_x, o_dt, o_glu = ds, ds + xw, ds + xw + 2 * n_head
    w_z, w_xbc, w_dt = win[:, :ds], win[:, o_x:o_dt], win[:, o_dt:o_glu]
    w_ga, w_gb = win[:, o_glu:o_glu + d], win[:, o_glu + d:]
    w_dtp = jnp.concatenate([w_dt, jnp.zeros((d, LANES - 2 * n_head), BF16)], axis=1)
    w_cat = jnp.concatenate([w_z, w_ga, w_gb, w_xbc], axis=1)
    cbw = d // 2
    xbc_cb, dt_cb = 3 * d // cbw, 0
    w5, w31 = unshard_cols(w5_g), unshard_cols(w31_g)
    pad_vec = lambda v: jnp.concatenate([v.reshape(1, -1), jnp.zeros((1, LANES - v.size), F32)], axis=1)
    dtb_f, dtb_b, alog_f, alog_b = map(pad_vec, (dt_bias_fwd, dt_bias_bwd, a_log_fwd, a_log_bwd))
    dsk_f, dsk_b = pad_vec(ssm_d), jnp.zeros((1, LANES), F32)

    proj, (wg2,) = matmul("mix_proj", [(h2, w_cat)], "nn", out_dtype=BF16, tm=tm, rider=Rider([bf(ffn2_gate)], "chips"))
    dt_raw = matmul("mix_proj_dt", [(h2, w_dtp)], "nn", tm=tm)
    def conv5(name, src, cb0, flip):
        out = None
        for part, seq, off in (("lat", seq_len, 0), ("ctx", ctx_len, n_lat // ctx_len)):
            out = tapsum_roll(f"{name}_{part}", src, cb0, w5, 0, seq_len=seq, n_seq=n_ex, row_blk_off=off, width=seq,
                              piece=seq, cb=cbw, ncb=xw // cbw, pad=w5.shape[0] // 2, flip=flip,
                              place=((n_tok, xw), off, 0, out), out_dtype=F32 if flip else BF16)
        return out

    craw = conv5("xbc_conv", proj, xbc_cb, False)
    (xbc,) = rowwise("xbc_silu", fn_silu_bias, [row(craw)], [], [ssm_conv_b], [(n_tok, xw, F32)], tm=tm, n_tiles=n_tok // tm)
    ssd = dict(n_ex=n_ex, seq_len=seq_len, ctx_len=ctx_len, ds=ds)
    (y_f, hs_f), (wu2,) = ssd_fwd("ssd_fwd_f", xbc, dt_raw, dt_cb, dtb_f, alog_f, dsk_f, rev=False,
                                  rider=Rider([bf(ffn2_up)], "chips"), **ssd)
    (y_b, hs_b), (wout_g, wd2) = ssd_fwd("ssd_fwd_b", xbc, dt_raw, dt_cb, dtb_b, alog_b, dsk_b, rev=True,
                                         rider=Rider([bf(w_out), bf(ffn2_down)], "chips"), add=y_f, **ssd)
    wout = wout_g.reshape(2 * d, d)
    wo_y, wo_u = wout[:ds], wout[ds:]
    fn_gate = make_fn_gate_groupnorm(ds)
    (yn,) = rowwise("ssd_gate", fn_gate, [row(y_b), row(proj, d, 0)], [], [ssm_norm_w], [(n_lat, ds, BF16)],
                    tm=tm, n_tiles=lat_tiles)
    (u0,) = rowwise("glu", fn_glu, [row(proj, d, 1), row(proj, d, 2)], [], [], [(n_lat, d, BF16)], tm=tm, n_tiles=lat_tiles)
    cb31 = max(LANES, d // 4)
    ncb31 = (d // 2) // cb31
    pad31 = w31.shape[0] // 2
    piece31 = min(seq_len, 4 * GRID_W)
    v_w = tapsum_roll("cconv_cols", u0, 0, w31, 0, seq_len=seq_len, n_seq=n_ex, row_blk_off=0, width=GRID_W,
                      piece=piece31, cb=cb31, ncb=ncb31, pad=pad31, flip=False)
    v_h = tapsum_rows("cconv_rows", u0, ncb31, w31, ncb31, seq_len=seq_len, n_seq=n_ex, cb=cb31, ncb=ncb31, pad=pad31, flip=False)
    (un,) = rowwise("cconv_ln", fn_ln_silu, [row(v_w), row(v_h)], [], [cconv_b, cconv_ln_w, cconv_ln_b], [(n_lat, d, BF16)],
                    tm=tm, n_tiles=lat_tiles)
    mix = matmul("mix_out", [(yn, wo_y), (un, wo_u)], "nn", tm=tm)
    seg_lat, first_lat = _segmenter(tm, seq_len, n_lat)
    def fn_resid_norm(xv, mv, gate, shift, scale, nw):
        x2v = make_fn_resid(1.0)(xv, mv, gate)[0]
        return x2v, fn_norm_mod(x2v, shift, scale, nw)[0]

    x2, h3 = rowwise("mix_resid_norm", fn_resid_norm, [row(x1), row(mix)], [lat(tabs[5]), lat(tabs[6]), lat(tabs[7])],
                     [norm_ffn2], [(n_lat, d, F32), (n_lat, d, BF16)], tm=tm, n_tiles=lat_tiles, seg_fn=seg_lat)
    _, saved2 = _ffn_fwd(plan, "ffn2", row(x2), n_lat, tm, seg_lat, lat(tabs[6]), lat(tabs[7]), lat(tabs[8]), norm_ffn2, wg2, wu2, wd2,
                         fuse_gate_up=True, with_resid=False, h=h3)
    d_x3, d_final, loss_vec = final_loss(x2, saved2[4], lat(tabs[8]), loss_target.reshape(n_lat, d), final_norm.reshape(1, d),
                                         tm=tm, seg_fn=seg_lat)

    shard_cols = lambda t: t.reshape(t.shape[0], N_CHIPS, -1).transpose(1, 0, 2)

    def pieces(t):
        t = jnp.pad(t, ((0, 0), (0, -t.shape[1] % 32), (0, 0)))
        return t.reshape(2 * N_CHIPS, t.shape[1] // 2, t.shape[2]).astype(BF16)

    scatter = lambda *ts: Rider([pieces(t) for t in ts], "all8", scatter=True)
    halves = lambda names, landed: Rider([sum_slots(f"sum_{nm}", r, BF16) for nm, r in zip(names, landed)], "sibling")
    swapped = {}
    plan.on("ffn2_up_dx", "sc_ffn2_down", lambda p: scatter(p.got["ffn2_d_wd"]))
    plan.on("ffn2_up_dw", "sc_ffn2_gate", lambda p: scatter(p.got["ffn2_d_wg"]))
    d_x2, (d_s6, d_s7, d_g8), d_nffn2 = _ffn_bwd(
        plan, "ffn2", d_x3, saved2, row(x2), n_lat, tm, seg_lat, first_lat, lat(tabs[6]), lat(tabs[7]), lat(tabs[8]), norm_ffn2,
        wg2, wu2, wd2, n_lat, None)
    d_mix, d_g5 = rowwise_bwd("mix_resid_bwd", make_fn_resid(1.0), [row(x1), row(mix)], [lat(tabs[5])], [], [[row(d_x2)]],
                              [None, (n_lat, BF16, None)], tm=tm, n_tiles=lat_tiles, seg_fn=seg_lat, first_fn=first_lat)
    d_yn = matmul("mix_out_dy", [(d_mix, wo_y)], "nt", tm=tm)
    d_un = matmul("mix_out_du", [(d_mix, wo_u)], "nt", tm=tm)
    d_wout = jnp.concatenate([matmul("mix_out_dwy", [(yn, d_mix)], "tn", out_dtype=BF16, tm=tm),
                              matmul("mix_out_dwu", [(un, d_mix)], "tn", out_dtype=BF16, tm=tm)])
    d_vw, d_vh, d_cb, d_lnw, d_lnb = rowwise_bwd(
        "cconv_ln_bwd", fn_ln_silu, [row(v_w), row(v_h)], [], [cconv_b, cconv_ln_w, cconv_ln_b], [[row(d_un)]],
        [(n_lat, F32, None)] * 2, tm=tm, n_tiles=lat_tiles)
    d_u0 = tapsum_roll("cconv_cols_dx", d_vw, 0, w31, 0, seq_len=seq_len, n_seq=n_ex, row_blk_off=0, width=GRID_W,
                       piece=piece31, cb=cb31, ncb=ncb31, pad=pad31, flip=True, place=((n_lat, d), 0, 0, None))
    d_u0 = tapsum_rows("cconv_rows_dx", d_vh, 0, w31, ncb31, seq_len=seq_len, n_seq=n_ex, cb=cb31, ncb=ncb31, pad=pad31,
                       flip=True, place=((n_lat, d), 0, ncb31, d_u0))
    d_w31 = jnp.concatenate([
        tapgrad_roll("cconv_cols_dw", d_vw, 0, 0, u0, 0, 0, n_tap=w31.shape[0], seq_len=seq_len, n_seq=n_ex, width=GRID_W,
                     piece=piece31, cb=cb31, ncb=ncb31, pad=pad31),
        tapgrad_rows("cconv_rows_dw", d_vh, 0, u0, ncb31, n_tap=w31.shape[0], seq_len=seq_len, n_seq=n_ex, cb=cb31,
                     ncb=ncb31, pad=pad31)], axis=1)
    d_ga, d_gb = rowwise_bwd("glu_bwd", fn_glu, [row(proj, d, 1), row(proj, d, 2)], [], [], [[row(d_u0)]],
                             [(n_lat, BF16, None)] * 2, tm=tm, n_tiles=lat_tiles)
    d_ysum, d_z, d_ssmnw = rowwise_bwd(
        "ssd_gate_bwd", fn_gate, [row(y_b), row(proj, d, 0)], [], [ssm_norm_w], [[row(d_yn)]],
        [(n_lat, F32, None), (n_lat, BF16, None)], tm=tm, n_tiles=lat_tiles)
    (dxbc_f, ddt_f, dalog_f, ddtb_f, ddsk), landed = ssd_bwd(
        "ssd_bwd_f", xbc, dt_raw, dt_cb, hs_f, d_ysum, dtb_f, alog_f, dsk_f, rev=False,
        rider=scatter(plan.got["ffn2_d_wu"], d_wout.reshape(N_CHIPS, -1, d)), **ssd)
    (dxbc_b, ddt_b, dalog_b, ddtb_b, _), both = ssd_bwd(
        "ssd_bwd_b", xbc, dt_raw, dt_cb, hs_b, d_ysum, dtb_b, alog_b, dsk_b, rev=True,
        rider=halves(["ffn2_down", "ffn2_gate"], plan.got["sc_ffn2_down"] + plan.got["sc_ffn2_gate"]), add=dxbc_f, **ssd)
    swapped.update(zip(["ffn2_down", "ffn2_gate"], both))
    (d_craw, d_conv_b), both = rowwise_bwd(
        "xbc_silu_bwd", fn_silu_bias, [row(craw)], [], [ssm_conv_b], [[row(dxbc_b)]],
        [(n_tok, F32, None)], tm=tm, n_tiles=n_tok // tm, rider=halves(["ffn2_up", "w_out"], landed))
    swapped.update(zip(["ffn2_up", "w_out"], both))
    d_pxbc = conv5("xbc_conv_dx", d_craw, 0, True)
    g5 = lambda name, seq, off: tapgrad_roll(name, d_craw, 0, off, proj, xbc_cb, off, n_tap=w5.shape[0], seq_len=seq,
                                             n_seq=n_ex, width=seq, piece=seq, cb=cbw, ncb=xw // cbw, pad=w5.shape[0] // 2)
    d_w5 = g5("xbc_conv_lat_dw", seq_len, 0) + g5("xbc_conv_ctx_dw", ctx_len, n_lat // ctx_len)
    lat_pairs = [(d_z, w_z), (d_ga, w_ga), (d_gb, w_gb), (d_pxbc, w_xbc), (ddt_f, w_dtp), (ddt_b, w_dtp)]
    d_h2 = matmul("mix_proj_dx_lat", lat_pairs, "nt", rows=n_lat, tm=min(tm, 256), place=(n_tok, 0, None))
    d_h2 = matmul("mix_proj_dx_ctx", lat_pairs[3:], "nt", rows=n_ctx_rows, row_off=n_lat, tm=min(tm, 256),
                  place=(n_tok, n_lat, d_h2))
    d_wz = matmul("mix_proj_dwz", [(d_z, h2)], "tn", out_dtype=BF16, rows=n_lat, tm=tm)
    d_wga = matmul("mix_proj_dwa", [(d_ga, h2)], "tn", out_dtype=BF16, rows=n_lat, tm=tm)
    d_wgb = matmul("mix_proj_dwb", [(d_gb, h2)], "tn", out_dtype=BF16, rows=n_lat, tm=tm)
    d_wxbc = matmul("mix_proj_dwx", [(d_pxbc, h2)], "tn", out_dtype=BF16, tm=tm)
    d_wdt = matmul("mix_proj_dwt", [(ddt_f, h2), (ddt_b, h2)], "tn", out_dtype=BF16, tm=tm)
    d_win_t = jnp.concatenate([d_wz, d_wxbc, d_wdt[:2 * n_head], d_wga, d_wgb]).reshape(N_CHIPS, -1, d)
    d_x1, d_s3, d_s4, d_nmix = rowwise_bwd(
        "mix_norm_bwd", fn_norm_mod, [row(x1)], [tabs[3], tabs[4]], [norm_mix], [[row(d_h2)]], [(n_tok, F32, None)],
        tm=tm, n_tiles=n_tok // tm, seg_fn=seg_all, first_fn=first_all, adds={0: (row(d_x2), lat_tiles)})
    mix_names = ["w_in", "ssm_conv_w", "cconv_w"]
    plan.on("ffn1_down_dx", "sc_conv", lambda p: scatter(shard_cols(d_w5), shard_cols(d_w31)))
    plan.on("ffn1_up_dx", "sc_win", lambda p: scatter(d_win_t))
    plan.on("ffn1_gate_dw", "sc_ffn1_down", lambda p: scatter(p.got["ffn1_d_wd"]))
    plan.on("ffn1_up_dw", "sc_ffn1_gate", lambda p: scatter(p.got["ffn1_d_wg"]))
    plan.on("ffn1_up_dw", "sw_mix", lambda p: halves(mix_names, p.got["sc_win"] + p.got["sc_conv"]))
    plan.on("ffn1_norm_bwd", "sc_ffn1_up", lambda p: scatter(p.got["ffn1_d_wu"]))
    plan.on("ffn1_norm_bwd", "sw_ffn1_gate", lambda p: halves(["ffn1_gate"], p.got["sc_ffn1_gate"]))
    plan.on("ffn1_up_dw", "sw_ffn1_down", lambda p: halves(["ffn1_down"], p.got["sc_ffn1_down"]))
    d_xt, (d_s0, d_s1, d_g2), d_nffn1 = _ffn_bwd(
        plan, "ffn1", d_x1, saved1, xt, n_tok, tm, seg_all, first_all, tabs[0], tabs[1], tabs[2], norm_ffn1, wg1, wu1, wd1,
        n_lat, lat_tiles)
    swapped.update(zip(mix_names + ["ffn1_down", "ffn1_gate"],
                       plan.got["sw_mix"] + plan.got["sw_ffn1_down"] + plan.got["sw_ffn1_gate"]))
    last_names = ["ffn1_up"]
    last = halves(last_names, plan.got["sc_ffn1_up"])
    grad_x = d_xt.reshape(n_ex, seq_len, d)

    with_ctx0 = lambda t: jnp.concatenate([t, jnp.zeros((1, 1, d), F32)])
    d_tabs = [d_s0, d_s1, d_g2, d_s3, d_s4, with_ctx0(d_g5), with_ctx0(d_s6), with_ctx0(d_s7), with_ctx0(d_g8)]
    d_mod_rows = jnp.concatenate([t[:, 0, :] for t in d_tabs], axis=1)
    n_pad_rows = -(-(n_ex + 1) // 8) * 8
    d_mod_rows = jnp.concatenate([d_mod_rows, jnp.zeros((n_pad_rows - n_ex - 1, 9 * d), F32)])
    small = [("loss", loss_vec), ("norm_ffn1", d_nffn1), ("norm_mix", d_nmix), ("ssm_conv_b", d_conv_b),
             ("dt_bias_fwd", ddtb_f[:, :n_head]), ("dt_bias_bwd", ddtb_b[:, :n_head]), ("a_log_fwd", dalog_f[:, :n_head]),
             ("a_log_bwd", dalog_b[:, :n_head]), ("ssm_d", ddsk[:, :n_head]), ("ssm_norm_w", d_ssmnw), ("cconv_b", d_cb),
             ("cconv_ln_w", d_lnw), ("cconv_ln_b", d_lnb), ("norm_ffn2", d_nffn2), ("final_norm", d_final)]
    n_small = sum(v.size for _, v in small)
    n_pack = -(-n_small // (8 * LANES)) * (8 * LANES)
    pack = jnp.concatenate([v.reshape(-1) for _, v in small] + [jnp.zeros((n_pack - n_small,), F32)]).reshape(-1, LANES)
    (pack_all, d_mod_all), both = exchange_many("gather_small_swap_last", [Rider([pack, d_mod_rows], "all8"), last])
    swapped.update(zip(last_names, both))
    pack_sum = sum_slots("small_sum", pack_all)
    loss = loss_total(pack_sum.reshape(1, n_pack), d).reshape(())
    flat_sum = pack_sum.reshape(-1)
    small_grads, pos = {}, 0
    for nm, v in small:
        small_grads[nm] = flat_sum[pos:pos + v.size]
        pos += v.size
    d_mod_all = d_mod_all.reshape(8 * n_pad_rows, 9 * d)
    cond_rows = [jnp.concatenate([cond[j * n_ex:(j + 1) * n_ex], c_ctx[None, :],
                                  jnp.zeros((n_pad_rows - n_ex - 1, d), F32)]) for j in range(8)]
    cond_bwd = jnp.concatenate(cond_rows)
    d_mod_shard = lax.dynamic_slice(d_mod_all, (0, chip * mod_w), (8 * n_pad_rows, mod_w))
    g_wmod, g_bmod, q_part = mod_bwd(cond_bwd, d_mod_shard, d_mod_all, w_mod[0],
                                     tuple(j * n_pad_rows + n_ex for j in range(8)))
    (q_all,) = exchange("gather_cctx", [q_part], "all8")
    g_cctx = cctx_grad(q_all, c_ctx.reshape(1, d))
    small_grads["c_ctx"], small_grads["b_mod"] = g_cctx.reshape(-1), g_bmod.reshape(-1)

    transposed = {"ffn1_gate", "ffn1_up", "ffn2_gate", "ffn2_up", "w_in"}
    results = {}
    for nm, both in swapped.items():
        flip = (lambda t: jnp.swapaxes(t, 1, 2)) if nm in transposed else (lambda t: t)
        shape = flip(weights[nm]).shape
        two_d = lambda t: flip(t).reshape(shape[-2], shape[-1])
        g_full = both.reshape(1, -1, shape[-1])[:, :shape[-2]]
        results[nm] = [flip(r.reshape(shape)) for r in
                       adamw(f"adamw_{nm}", two_d(weights[nm]), g_full, two_d(mom1[nm]), two_d(mom2[nm]))]
    results["w_mod"] = [r.reshape(w_mod.shape) for r in adamw("adamw_w_mod", w_mod[0], g_wmod[None], m_w_mod[0], v_w_mod[0])]
    small_names = [nm for nm in order if nm not in results]
    n_sm = sum(weights[nm].size for nm in small_names)
    n_smp = -(-n_sm // (8 * LANES)) * (8 * LANES)
    packed = lambda src: jnp.concatenate([src[nm].reshape(-1) for nm in small_names] + [jnp.zeros((n_smp - n_sm,), F32)]).reshape(-1, LANES)
    sm_out = adamw("adamw_small", packed(weights), packed(small_grads)[None], packed(mom1), packed(mom2))
    pos = 0
    for nm in small_names:
        size = weights[nm].size
        results[nm] = [r.reshape(-1)[pos:pos + size].reshape(weights[nm].shape) for r in sm_out]
        pos += size
    return (loss, grad_x, *[results[nm][0] for nm in order], *[results[nm][1] for nm in order],
            *[results[nm][2] for nm in order], *[results[nm][3] for nm in order])
```
